```python
import math
import jax, jax.numpy as jnp
from jax import lax
import numpy as np

D_MODEL = 1024
BATCH = 16
SEQ = 2048
DEPTH = 1

HEAD_DIM = 64
ATTN_WIDTH = D_MODEL // 2
N_ATTN_HEADS = ATTN_WIDTH // HEAD_DIM
DILATED_CONFIGS = ((128, 1), (512, 4), (2048, 16))
SGU_WIDTH = D_MODEL // 4
N_SGU_GROUPS = 4
SGU_GROUP = SGU_WIDTH // N_SGU_GROUPS
SGU_CHUNK = 128
MEM_WIDTH = D_MODEL // 4
N_MEM_HEADS = 4
MEM_HEAD_DIM = MEM_WIDTH // N_MEM_HEADS
N_MEM = 256
MIX_WIDTH = ATTN_WIDTH + SGU_WIDTH + MEM_WIDTH
IN_COLS = 4 * ATTN_WIDTH + 3 * SGU_WIDTH + 2 * MEM_WIDTH
EPS = 1e-6
NEG_INF = -1e30

kernel_name = "hybrid_dilated_sgu_memory_encoder"


def rmsnorm(x, g):
    xf = x.astype(jnp.float32)
    y = xf * lax.rsqrt(jnp.mean(xf * xf, axis=-1, keepdims=True) + EPS) * g.astype(jnp.float32)
    return y.astype(x.dtype)


def alibi_slopes(n):
    return jnp.power(2.0, -8.0 * (jnp.arange(n, dtype=jnp.float32) + 1.0) / n)


def dilated_window_attention(q, k, v, slopes, window, dilation):
    B, S, H, E = q.shape
    radius = window // (2 * dilation)
    blk = radius
    L = S // dilation
    N = B * dilation
    nb = -(-L // blk)
    Lp = nb * blk

    def to_sub(t):
        return t.reshape(B, L, dilation, H, E).transpose(0, 2, 1, 3, 4).reshape(N, L, H, E)

    qs = jnp.pad(to_sub(q), ((0, 0), (0, Lp - L), (0, 0), (0, 0))).reshape(N, nb, blk, H, E)

    def neighbour_blocks(t):
        tp = jnp.pad(to_sub(t), ((0, 0), (blk, Lp - L + blk), (0, 0), (0, 0))).reshape(N, nb + 2, blk, H, E)
        return jnp.concatenate([tp[:, :-2], tp[:, 1:-1], tp[:, 2:]], axis=2)

    kb = neighbour_blocks(k)
    vb = neighbour_blocks(v)
    q_idx = jnp.arange(Lp).reshape(nb, blk)
    k_idx = (jnp.arange(nb) * blk)[:, None] - blk + jnp.arange(3 * blk)[None, :]
    rel = jnp.abs(k_idx[:, None, :] - q_idx[:, :, None])
    valid = (rel <= radius) & (k_idx[:, None, :] >= 0) & (k_idx[:, None, :] < L)
    dist = (rel * dilation).astype(jnp.float32)

    s = jnp.einsum('nbqhe,nbkhe->nhbqk', qs.astype(jnp.float32), kb.astype(jnp.float32)) * (E ** -0.5)
    s = s - slopes[None, :, None, None, None] * dist[None, None]
    s = jnp.where(valid[None, None], s, NEG_INF)
    m = jnp.max(s, axis=-1, keepdims=True)
    p = jnp.exp(s - m)
    l = jnp.sum(p, axis=-1)
    o = jnp.einsum('nhbqk,nbkhe->nbqhe', p, vb.astype(jnp.float32))
    o = o / jnp.transpose(l, (0, 2, 3, 1))[..., None]
    lse = m[..., 0] + jnp.log(l)

    o = o.reshape(N, Lp, H, E)[:, :L].reshape(B, dilation, L, H, E).transpose(0, 2, 1, 3, 4).reshape(B, S, H, E)
    lse = lse.reshape(N, H, Lp)[..., :L].reshape(B, dilation, H, L).transpose(0, 3, 1, 2).reshape(B, S, H)
    return o, lse


def mixture_of_dilations(q, k, v):
    slopes = alibi_slopes(q.shape[2])
    outs, lses = [], []
    for window, dilation in DILATED_CONFIGS:
        o, lse = dilated_window_attention(q, k, v, slopes, window, dilation)
        outs.append(o)
        lses.append(lse)
    w = jax.nn.softmax(jnp.stack(lses, axis=0), axis=0)
    return jnp.sum(w[..., None] * jnp.stack(outs, axis=0), axis=0)


def chunked_spatial_gating(u, v, g_v, w_s, b_s):
    B, S, _ = v.shape
    n = S // SGU_CHUNK
    vn = rmsnorm(v, g_v).reshape(B, n, SGU_CHUNK, N_SGU_GROUPS, SGU_GROUP).astype(jnp.float32)
    mixed = jnp.einsum('gts,bnsgc->bntgc', w_s.astype(jnp.float32), vn)
    mixed = mixed + b_s.astype(jnp.float32).T[None, None, :, :, None]
    return u.astype(jnp.float32) * mixed.reshape(B, S, SGU_WIDTH)


def memory_cross_attention(q, mem, g_mem, w_mem_kv):
    B, M, _ = mem.shape
    kv = rmsnorm(mem, g_mem) @ w_mem_kv
    k, v = jnp.split(kv, 2, axis=-1)
    k = k.reshape(B, M, N_MEM_HEADS, MEM_HEAD_DIM).astype(jnp.float32)
    v = v.reshape(B, M, N_MEM_HEADS, MEM_HEAD_DIM).astype(jnp.float32)
    s = jnp.einsum('bshe,bmhe->bhsm', q.astype(jnp.float32), k) * (MEM_HEAD_DIM ** -0.5)
    p = jax.nn.softmax(s, axis=-1)
    return jnp.einsum('bhsm,bmhe->bshe', p, v)


def hybrid_layer(x, mem, g_norm, w_in, w_s, b_s, g_v, g_mem, w_mem_kv, w_out):
    B, S, _ = x.shape
    h = rmsnorm(x, g_norm)
    proj = h @ w_in
    offs = np.cumsum([ATTN_WIDTH] * 4 + [SGU_WIDTH] * 3 + [MEM_WIDTH])
    qa, ka, va, za, ub, vb, zb, qm, zm = jnp.split(proj, list(offs), axis=-1)

    shp = (B, S, N_ATTN_HEADS, HEAD_DIM)
    a = mixture_of_dilations(qa.reshape(shp), ka.reshape(shp), va.reshape(shp)).reshape(B, S, ATTN_WIDTH)
    sg = chunked_spatial_gating(jax.nn.gelu(ub), jax.nn.gelu(vb), g_v, w_s, b_s)
    mo = memory_cross_attention(qm.reshape(B, S, N_MEM_HEADS, MEM_HEAD_DIM), mem, g_mem, w_mem_kv)
    mo = mo.reshape(B, S, MEM_WIDTH)

    gated = jnp.concatenate([
        jax.nn.silu(za.astype(jnp.float32)) * a,
        jax.nn.silu(zb.astype(jnp.float32)) * sg,
        jax.nn.silu(zm.astype(jnp.float32)) * mo,
    ], axis=-1).astype(x.dtype)
    return x + gated @ w_out


def _fwd_setup_inputs(seed: int = 0) -> dict:
    key = jax.random.key(seed)
    ks = jax.random.split(key, 12)
    f32 = jnp.float32
    x = jax.random.normal(ks[0], (BATCH, SEQ, D_MODEL), f32)
    mem = jax.random.normal(ks[1], (BATCH, N_MEM, D_MODEL), f32)
    g_norm = 1.0 + 0.02 * jax.random.normal(ks[2], (DEPTH, D_MODEL), f32)
    w_in = jax.random.normal(ks[3], (DEPTH, D_MODEL, IN_COLS), f32) * D_MODEL ** -0.5
    w_sgu_spatial = jax.random.normal(ks[4], (DEPTH, N_SGU_GROUPS, SGU_CHUNK, SGU_CHUNK), f32) * (0.5 * SGU_CHUNK ** -0.5)
    b_sgu_spatial = 1.0 + 0.02 * jax.random.normal(ks[5], (DEPTH, N_SGU_GROUPS, SGU_CHUNK), f32)
    g_sgu_v = 1.0 + 0.02 * jax.random.normal(ks[6], (DEPTH, SGU_WIDTH), f32)
    g_mem = 1.0 + 0.02 * jax.random.normal(ks[7], (DEPTH, D_MODEL), f32)
    w_mem_kv = jax.random.normal(ks[8], (DEPTH, D_MODEL, 2 * MEM_WIDTH), f32) * D_MODEL ** -0.5
    w_out = jax.random.normal(ks[9], (DEPTH, MIX_WIDTH, D_MODEL), f32) * MIX_WIDTH ** -0.5
    g_final = 1.0 + 0.02 * jax.random.normal(ks[10], (D_MODEL,), f32)
    return {"x": x, "mem": mem, "g_norm": g_norm, "w_in": w_in,
            "w_sgu_spatial": w_sgu_spatial, "b_sgu_spatial": b_sgu_spatial, "g_sgu_v": g_sgu_v,
            "g_mem": g_mem, "w_mem_kv": w_mem_kv, "w_out": w_out, "g_final": g_final}


def _fwd_reference(x, mem, g_norm, w_in, w_sgu_spatial, b_sgu_spatial, g_sgu_v, g_mem, w_mem_kv, w_out, g_final):
    h = x
    for layer in range(DEPTH):
        h = hybrid_layer(h, mem, g_norm[layer], w_in[layer], w_sgu_spatial[layer], b_sgu_spatial[layer],
                         g_sgu_v[layer], g_mem[layer], w_mem_kv[layer], w_out[layer])
    return rmsnorm(h, g_final)


import jax as _jax
import jax.numpy as _jnp

TWIN_FORMAT = 'train_step'
FWD_PARAMS = ['x', 'mem', 'g_norm', 'w_in', 'w_sgu_spatial', 'b_sgu_spatial', 'g_sgu_v', 'g_mem', 'w_mem_kv', 'w_out', 'g_final']
TWIN_WEIGHTS = ['g_norm', 'w_in', 'w_sgu_spatial', 'b_sgu_spatial', 'g_sgu_v', 'g_mem', 'w_mem_kv', 'w_out', 'g_final']
TWIN_DIFF_INPUT = 'x'
TWIN_INPUTS = ['x', 'mem', 'g_norm', 'w_in', 'w_sgu_spatial', 'b_sgu_spatial', 'g_sgu_v', 'g_mem', 'w_mem_kv', 'w_out', 'g_final', 'loss_target', 'm_g_norm', 'm_w_in', 'm_w_sgu_spatial', 'm_b_sgu_spatial', 'm_g_sgu_v', 'm_g_mem', 'm_w_mem_kv', 'm_w_out', 'm_g_final', 'v_g_norm', 'v_w_in', 'v_w_sgu_spatial', 'v_b_sgu_spatial', 'v_g_sgu_v', 'v_g_mem', 'v_w_mem_kv', 'v_w_out', 'v_g_final']
TWIN_OUTPUTS = ['loss', 'grad_x', 'grad_g_norm', 'grad_w_in', 'grad_w_sgu_spatial', 'grad_b_sgu_spatial', 'grad_g_sgu_v', 'grad_g_mem', 'grad_w_mem_kv', 'grad_w_out', 'grad_g_final', 'delta_g_norm', 'delta_w_in', 'delta_w_sgu_spatial', 'delta_b_sgu_spatial', 'delta_g_sgu_v', 'delta_g_mem', 'delta_w_mem_kv', 'delta_w_out', 'delta_g_final', 'new_m_g_norm', 'new_m_w_in', 'new_m_w_sgu_spatial', 'new_m_b_sgu_spatial', 'new_m_g_sgu_v', 'new_m_g_mem', 'new_m_w_mem_kv', 'new_m_w_out', 'new_m_g_final', 'new_v_g_norm', 'new_v_w_in', 'new_v_w_sgu_spatial', 'new_v_b_sgu_spatial', 'new_v_g_sgu_v', 'new_v_g_mem', 'new_v_w_mem_kv', 'new_v_w_out', 'new_v_g_final']
TWIN_LEAF_KINDS = {'loss': 'loss', 'grad_x': 'grad_x', 'grad_g_norm': 'grad_w', 'grad_w_in': 'grad_w', 'grad_w_sgu_spatial': 'grad_w', 'grad_b_sgu_spatial': 'grad_w', 'grad_g_sgu_v': 'grad_w', 'grad_g_mem': 'grad_w', 'grad_w_mem_kv': 'grad_w', 'grad_w_out': 'grad_w', 'grad_g_final': 'grad_w', 'delta_g_norm': 'delta_w', 'delta_w_in': 'delta_w', 'delta_w_sgu_spatial': 'delta_w', 'delta_b_sgu_spatial': 'delta_w', 'delta_g_sgu_v': 'delta_w', 'delta_g_mem': 'delta_w', 'delta_w_mem_kv': 'delta_w', 'delta_w_out': 'delta_w', 'delta_g_final': 'delta_w', 'new_m_g_norm': 'new_m', 'new_m_w_in': 'new_m', 'new_m_w_sgu_spatial': 'new_m', 'new_m_b_sgu_spatial': 'new_m', 'new_m_g_sgu_v': 'new_m', 'new_m_g_mem': 'new_m', 'new_m_w_mem_kv': 'new_m', 'new_m_w_out': 'new_m', 'new_m_g_final': 'new_m', 'new_v_g_norm': 'new_v', 'new_v_w_in': 'new_v', 'new_v_w_sgu_spatial': 'new_v', 'new_v_b_sgu_spatial': 'new_v', 'new_v_g_sgu_v': 'new_v', 'new_v_g_mem': 'new_v', 'new_v_w_mem_kv': 'new_v', 'new_v_w_out': 'new_v', 'new_v_g_final': 'new_v'}


def _forward(args):
    return _fwd_reference(*[args[k] for k in FWD_PARAMS])


def _output_shape():
    out = _jax.eval_shape(lambda: _forward(_fwd_setup_inputs(0)))
    return out.shape, out.dtype

N_MICROBATCH = 1
ADAM_LR = 0.001
ADAM_B1 = 0.9
ADAM_B2 = 0.999
ADAM_EPS = 1e-08
ADAM_WD = 0.01
ADAM_STEP = 10
PER_EXAMPLE_BATCH_AXIS = {'x': 0, 'mem': 0, 'loss_target': 0}
SHARED_INPUTS = []
_WEIGHT_DTYPES = {'g_norm': _jnp.float32, 'w_in': _jnp.float32, 'w_sgu_spatial': _jnp.float32, 'b_sgu_spatial': _jnp.float32, 'g_sgu_v': _jnp.float32, 'g_mem': _jnp.float32, 'w_mem_kv': _jnp.float32, 'w_out': _jnp.float32, 'g_final': _jnp.float32}
MOMENT_SCALE = {'g_norm': 7.217426e-02, 'w_in': 3.871265e-02, 'w_sgu_spatial': 4.758003e-02, 'b_sgu_spatial': 4.809920e-02, 'g_sgu_v': 3.404053e-02, 'g_mem': 8.867302e-03, 'w_mem_kv': 1.088142e-02, 'w_out': 4.384962e-02, 'g_final': 3.197987e+01}


def _to_microbatches(a, axis):
    t = _jnp.moveaxis(a, axis, 0)
    t = t.reshape((N_MICROBATCH, t.shape[0] // N_MICROBATCH) + t.shape[1:])
    return _jnp.moveaxis(t, 1, axis + 1)


def setup_inputs(seed: int = 0) -> dict:
    inp = _fwd_setup_inputs(seed)
    key = _jax.random.fold_in(_jax.random.key(seed), 7919)
    shape, _ = _output_shape()
    out = dict(inp)
    out["loss_target"] = _jax.random.normal(_jax.random.fold_in(key, 0), shape, _jnp.float32)
    for i, name in enumerate(TWIN_WEIGHTS):
        w = inp[name].astype(_jnp.float32)
        if MOMENT_SCALE is None:
            s = _jnp.sqrt(_jnp.mean(_jnp.square(w)) + 1e-30)
        else:
            s = MOMENT_SCALE[name]
        km, kv = _jax.random.split(_jax.random.fold_in(key, i + 1))
        out[name] = w
        out["m_" + name] = s * _jax.random.normal(km, w.shape, _jnp.float32)
        out["v_" + name] = (s * s) * _jax.random.uniform(kv, w.shape, _jnp.float32, 0.5, 1.5)
    if N_MICROBATCH > 1:
        for name, axis in PER_EXAMPLE_BATCH_AXIS.items():
            out[name] = _to_microbatches(out[name], axis)
    return {'x': out['x'], 'mem': out['mem'], 'g_norm': out['g_norm'], 'w_in': out['w_in'], 'w_sgu_spatial': out['w_sgu_spatial'], 'b_sgu_spatial': out['b_sgu_spatial'], 'g_sgu_v': out['g_sgu_v'], 'g_mem': out['g_mem'], 'w_mem_kv': out['w_mem_kv'], 'w_out': out['w_out'], 'g_final': out['g_final'], 'loss_target': out['loss_target'], 'm_g_norm': out['m_g_norm'], 'm_w_in': out['m_w_in'], 'm_w_sgu_spatial': out['m_w_sgu_spatial'], 'm_b_sgu_spatial': out['m_b_sgu_spatial'], 'm_g_sgu_v': out['m_g_sgu_v'], 'm_g_mem': out['m_g_mem'], 'm_w_mem_kv': out['m_w_mem_kv'], 'm_w_out': out['m_w_out'], 'm_g_final': out['m_g_final'], 'v_g_norm': out['v_g_norm'], 'v_w_in': out['v_w_in'], 'v_w_sgu_spatial': out['v_w_sgu_spatial'], 'v_b_sgu_spatial': out['v_b_sgu_spatial'], 'v_g_sgu_v': out['v_g_sgu_v'], 'v_g_mem': out['v_g_mem'], 'v_w_mem_kv': out['v_w_mem_kv'], 'v_w_out': out['v_w_out'], 'v_g_final': out['v_g_final']}


def _loss(weights, diff, rest, loss_target):
    with _jax.named_scope("forward"):
        args = {**rest, TWIN_DIFF_INPUT: diff, **{k: w.astype(_WEIGHT_DTYPES[k]) for k, w in weights.items()}}
        y = _forward(args)
    with _jax.named_scope("loss_head"):
        err = _jnp.square(y.astype(_jnp.float32) - loss_target)
        return 0.5 * _jnp.sum(_jnp.mean(err, axis=-1)) if err.ndim else 0.5 * err


def _adamw(w, g, m, v):
    m = ADAM_B1 * m + (1.0 - ADAM_B1) * g
    v = ADAM_B2 * v + (1.0 - ADAM_B2) * _jnp.square(g)
    m_hat = m / (1.0 - ADAM_B1 ** ADAM_STEP)
    v_hat = v / (1.0 - ADAM_B2 ** ADAM_STEP)
    delta = -ADAM_LR * (m_hat / (_jnp.sqrt(v_hat) + ADAM_EPS) + ADAM_WD * w)
    return delta, m, v


def reference(x, mem, g_norm, w_in, w_sgu_spatial, b_sgu_spatial, g_sgu_v, g_mem, w_mem_kv, w_out, g_final, loss_target, m_g_norm, m_w_in, m_w_sgu_spatial, m_b_sgu_spatial, m_g_sgu_v, m_g_mem, m_w_mem_kv, m_w_out, m_g_final, v_g_norm, v_w_in, v_w_sgu_spatial, v_b_sgu_spatial, v_g_sgu_v, v_g_mem, v_w_mem_kv, v_w_out, v_g_final):
    given = dict(x=x, mem=mem, g_norm=g_norm, w_in=w_in, w_sgu_spatial=w_sgu_spatial, b_sgu_spatial=b_sgu_spatial, g_sgu_v=g_sgu_v, g_mem=g_mem, w_mem_kv=w_mem_kv, w_out=w_out, g_final=g_final, loss_target=loss_target, m_g_norm=m_g_norm, m_w_in=m_w_in, m_w_sgu_spatial=m_w_sgu_spatial, m_b_sgu_spatial=m_b_sgu_spatial, m_g_sgu_v=m_g_sgu_v, m_g_mem=m_g_mem, m_w_mem_kv=m_w_mem_kv, m_w_out=m_w_out, m_g_final=m_g_final, v_g_norm=v_g_norm, v_w_in=v_w_in, v_w_sgu_spatial=v_w_sgu_spatial, v_b_sgu_spatial=v_b_sgu_spatial, v_g_sgu_v=v_g_sgu_v, v_g_mem=v_g_mem, v_w_mem_kv=v_w_mem_kv, v_w_out=v_w_out, v_g_final=v_g_final)
    weights = {n: given[n] for n in TWIN_WEIGHTS}
    shared = {n: given[n] for n in SHARED_INPUTS}
    per_example = {n: given[n] for n in ['x', 'mem']}
    grad_fn = _jax.value_and_grad(_loss, argnums=(0, 1))

    def one_microbatch(ex, loss_target):
        ex = dict(ex)
        diff = ex.pop(TWIN_DIFF_INPUT)
        return grad_fn(weights, diff, {**shared, **ex}, loss_target)

    if N_MICROBATCH == 1:
        loss, (grad_w, grad_x) = one_microbatch(per_example, given["loss_target"])
    else:
        def body(carry, xs):
            loss_sum, grad_sum = carry
            l_k, (gw_k, gx_k) = one_microbatch(xs[0], xs[1])
            with _jax.named_scope("update"):
                return (loss_sum + l_k, _jax.tree.map(_jnp.add, grad_sum, gw_k)), gx_k

        init = (_jnp.zeros((), _jnp.float32), _jax.tree.map(_jnp.zeros_like, weights))
        (loss, grad_w), grad_x = _jax.lax.scan(body, init, (per_example, given["loss_target"]))
    with _jax.named_scope("update"):
        delta_w, new_m, new_v = {}, {}, {}
        for n in TWIN_WEIGHTS:
            delta_w[n], new_m[n], new_v[n] = _adamw(weights[n], grad_w[n], given["m_" + n], given["v_" + n])
    return (loss, grad_x, *[grad_w[n] for n in TWIN_WEIGHTS], *[delta_w[n] for n in TWIN_WEIGHTS],
            *[new_m[n] for n in TWIN_WEIGHTS], *[new_v[n] for n in TWIN_WEIGHTS])
```

```python
import functools

import jax
import jax.numpy as jnp
from jax import lax
from jax.experimental import pallas as pl
from jax.experimental.pallas import tpu as pltpu

F32 = jnp.float32
BF16 = jnp.bfloat16
MESH = pl.DeviceIdType.MESH

D_MODEL = 1024
ATTN_WIDTH = 512
SGU_WIDTH = 256
MEM_WIDTH = 256
N_MEM = 256
IN_COLS = 3328
QKV_COLS = 3 * ATTN_WIDTH
REST_COLS = IN_COLS - QKV_COLS
SGU_CHUNK = 128
N_SGU_GROUPS = 4
EPS = 1e-6
NEG_INF = -1e30
DILATIONS = (1, 4, 16)
RADIUS = 64
Q_BLOCK = 128
LANES = 128
HEAD_DIM = 64

ADAM_LR = 0.001
ADAM_B1 = 0.9
ADAM_B2 = 0.999
ADAM_EPS = 1e-08
ADAM_WD = 0.01
ADAM_STEP = 10

N_CHIPS = 4
VMEM_LIMIT = 56 * 1024 * 1024
SMALL_ROWS = 560


def _params(sem=None, vmem=VMEM_LIMIT):
    return pltpu.CompilerParams(dimension_semantics=sem, vmem_limit_bytes=vmem)


def _nn(a, b):
    return jnp.dot(a, b, preferred_element_type=F32)


def _nt(a, b):
    return lax.dot_general(a, b, (((1,), (1,)), ((), ())), preferred_element_type=F32)


def _tn(a, b):
    return lax.dot_general(a, b, (((0,), (0,)), ((), ())), preferred_element_type=F32)


def _rms(x):
    r = lax.rsqrt(jnp.mean(x * x, axis=-1, keepdims=True) + EPS)
    return r, x * r


def _head_masks():
    lane = lax.broadcasted_iota(jnp.int32, (1, LANES), 1)
    lo = lane < HEAD_DIM
    return lo, (lo.astype(F32), (~lo).astype(F32))


def _silu_parts(z):
    s = jax.nn.sigmoid(z)
    return z * s, s * (1.0 + z * (1.0 - s))


def _gelu_parts(x):
    c = 0.7978845608028654
    x2 = x * x
    t = jnp.tanh(c * (x + 0.044715 * (x * x2)))
    val = 0.5 * x * (1.0 + t)
    grad = 0.5 * (1.0 + t) + 0.5 * x * (1.0 - t * t) * (c * (1.0 + 3.0 * 0.044715 * x2))
    return val, grad


def _inproj_fwd(x2d, g_norm, w_in):
    T = x2d.shape[0]
    tm = 512

    def body(x_ref, g_ref, w_ref, o_ref):
        _, xh = _rms(x_ref[...])
        h = (xh * g_ref[...]).astype(BF16)
        o_ref[...] = _nn(h, w_ref[...])

    return pl.pallas_call(
        body, grid=(T // tm,),
        in_specs=[pl.BlockSpec((tm, D_MODEL), lambda i: (i, 0)),
                  pl.BlockSpec((1, D_MODEL), lambda i: (0, 0)),
                  pl.BlockSpec((D_MODEL, IN_COLS), lambda i: (0, 0))],
        out_specs=pl.BlockSpec((tm, IN_COLS), lambda i: (i, 0)),
        out_shape=jax.ShapeDtypeStruct((T, IN_COLS), F32),
        compiler_params=_params(("arbitrary",)), name="inproj_fwd")(x2d, g_norm, w_in)


def _kv_fwd(mem2d, g_mem, w_kv):
    Tm = mem2d.shape[0]

    def body(m_ref, g_ref, w_ref, o_ref):
        _, mh = _rms(m_ref[...])
        o_ref[...] = _nn((mh * g_ref[...]).astype(BF16), w_ref[...])

    return pl.pallas_call(
        body, out_shape=jax.ShapeDtypeStruct((Tm, 2 * MEM_WIDTH), F32),
        compiler_params=_params(), name="kv_fwd")(mem2d, g_mem, w_kv)


def _kv_bwd(mem2d, g_mem, w_kv, dkv):
    Tm = mem2d.shape[0]

    def body(m_ref, g_ref, w_ref, dkv_ref, dw_ref, dg_ref):
        _, mh = _rms(m_ref[...])
        memn = (mh * g_ref[...]).astype(BF16)
        dkvb = dkv_ref[...].astype(BF16)
        dw_ref[...] = _tn(memn, dkvb)
        dmemn = _nt(dkvb, w_ref[...])
        dg_ref[...] = jnp.sum(dmemn * mh, axis=0, keepdims=True)

    return pl.pallas_call(
        body, out_shape=(jax.ShapeDtypeStruct((D_MODEL, 2 * MEM_WIDTH), F32),
                         jax.ShapeDtypeStruct((1, D_MODEL), F32)),
        compiler_params=_params(), name="kv_bwd")(mem2d, g_mem, w_kv, dkv)


def _attn_geometry(S):
    geom = []
    for d in DILATIONS:
        L = S // d
        assert L % Q_BLOCK == 0
        geom.append((d, L, min(2 * Q_BLOCK, L), L // Q_BLOCK))
    return geom


def _init_bias(bias_scr, geom, hp):
    row = lax.broadcasted_iota(jnp.int32, (Q_BLOCK, 2 * Q_BLOCK), 0)
    col = lax.broadcasted_iota(jnp.int32, (Q_BLOCK, 2 * Q_BLOCK), 1)
    for j in (0, 1):
        bits = (126 - (2 * hp + j)) * (1 << 23)
        slope = lax.bitcast_convert_type(jnp.full((1, 1), bits, jnp.int32), F32)
        for di, (d, _, _, _) in enumerate(geom):
            for cls, off in enumerate((0, -RADIUS, -2 * RADIUS)):
                dist = jnp.abs(col - row + off)
                bias_scr[di * 6 + cls * 2 + j] = jnp.where(
                    dist <= RADIUS, -(slope * float(d)) * dist.astype(F32), NEG_INF)


def _block_slices(d, L, KW, nqb, r, qb):
    qs = qb * Q_BLOCK
    ks = jnp.clip(qs - RADIUS, 0, L - KW)
    cls = jnp.where(qb == 0, 0, jnp.where(qb == nqb - 1, 2, 1))
    if d == 1:
        qsl = pl.ds(pl.multiple_of(qs, Q_BLOCK), Q_BLOCK)
        ksl = pl.ds(pl.multiple_of(ks, RADIUS), KW)
    else:
        qsl = pl.ds(r + qs * d, Q_BLOCK, stride=d)
        ksl = pl.ds(r + ks * d, KW, stride=d)
    return qsl, ksl, cls


def _for_blocks(geom, fn):
    for di, (d, L, KW, nqb) in enumerate(geom):
        def step(i, carry, di=di, d=d, L=L, KW=KW, nqb=nqb):
            fn(di, d, L, KW, nqb, i // nqb, i % nqb)
            return carry
        lax.fori_loop(0, d * nqb, step, 0)


def _attn_fwd(proj, B, S):
    T = B * S
    geom = _attn_geometry(S)
    n_pairs = ATTN_WIDTH // LANES

    def body(q_ref, k_ref, v_ref, a_ref, lse_ref, bias_scr, *per_dilation):
        o_scr, m_scr, l_scr = per_dilation[0:3], per_dilation[3:6], per_dilation[6:9]
        lo, hm = _head_masks()
        _init_bias(bias_scr, geom, pl.program_id(1))

        def block(di, d, L, KW, nqb, r, qb):
            qsl, ksl, cls = _block_slices(d, L, KW, nqb, r, qb)
            q = q_ref[qsl, :]
            kw = k_ref[ksl, :].astype(BF16)
            vw = v_ref[ksl, :].astype(BF16)
            o, m, l = [], [], []
            for j in (0, 1):
                qj = (q * (hm[j] * 0.125)).astype(BF16)
                s = _nt(qj, kw) + bias_scr[di * 6 + cls * 2 + j, :, pl.ds(0, KW)]
                mj = jnp.max(s, axis=1, keepdims=True)
                p = jnp.exp(s - mj)
                l.append(jnp.sum(p, axis=1, keepdims=True))
                m.append(mj)
                o.append(_nn(p.astype(BF16), vw))
            o_scr[di][qsl, :] = jnp.where(lo, o[0], o[1])
            m_scr[di][qsl, :] = jnp.where(lo, m[0], m[1])
            l_scr[di][qsl, :] = jnp.where(lo, l[0], l[1])

        _for_blocks(geom, block)

        rows_per = 256

        def combine(i, carry):
            rows = pl.ds(pl.multiple_of(i * rows_per, rows_per), rows_per)
            ms = [m_scr[di][rows, :] for di in range(3)]
            mx = jnp.maximum(jnp.maximum(ms[0], ms[1]), ms[2])
            num = 0.0
            den = 0.0
            for di in range(3):
                w = jnp.exp(ms[di] - mx)
                num = num + w * o_scr[di][rows, :]
                den = den + w * l_scr[di][rows, :]
            a_ref[rows, :] = num / den
            lse_ref[rows, :] = mx + jnp.log(den)
            return carry

        lax.fori_loop(0, S // rows_per, combine, 0)

    blk = lambda off: pl.BlockSpec((S, LANES), lambda b, h, off=off: (b, off + h))
    out_blk = pl.BlockSpec((S, LANES), lambda b, h: (b, h))
    return pl.pallas_call(
        body, grid=(B, n_pairs),
        in_specs=[blk(0), blk(n_pairs), blk(2 * n_pairs)],
        out_specs=[out_blk, out_blk],
        out_shape=[jax.ShapeDtypeStruct((T, ATTN_WIDTH), F32)] * 2,
        scratch_shapes=[pltpu.VMEM((18, Q_BLOCK, 2 * Q_BLOCK), F32)] + [pltpu.VMEM((S, LANES), F32)] * 9,
        compiler_params=_params(("arbitrary", "arbitrary")), name="attn_fwd")(proj, proj, proj)


def _attn_bwd(proj, a, lse, da, B, S):
    T = B * S
    geom = _attn_geometry(S)
    n_pairs = ATTN_WIDTH // LANES

    def body(q_ref, k_ref, v_ref, a_ref, lse_ref, do_ref, dq_ref, dk_ref, dv_ref,
             bias_scr, dq_scr, dk_scr, dv_scr):
        _, hm = _head_masks()
        _init_bias(bias_scr, geom, pl.program_id(1))
        dq_scr[...] = jnp.zeros_like(dq_scr)
        dk_scr[...] = jnp.zeros_like(dk_scr)
        dv_scr[...] = jnp.zeros_like(dv_scr)

        def block(di, d, L, KW, nqb, r, qb):
            qsl, ksl, cls = _block_slices(d, L, KW, nqb, r, qb)
            q = q_ref[qsl, :]
            do = do_ref[qsl, :]
            doa = do * a_ref[qsl, :]
            lse_q = lse_ref[qsl, :]
            kw = k_ref[ksl, :].astype(BF16)
            vw = v_ref[ksl, :].astype(BF16)
            dq = 0.0
            dkw = 0.0
            dvw = 0.0
            for j in (0, 1):
                qj = (q * (hm[j] * 0.125)).astype(BF16)
                s = _nt(qj, kw) + bias_scr[di * 6 + cls * 2 + j, :, pl.ds(0, KW)]
                p = jnp.exp(s - lse_q[:, HEAD_DIM * j:HEAD_DIM * j + 1])
                doj = (do * hm[j]).astype(BF16)
                dsum = jnp.sum(doa * hm[j], axis=1, keepdims=True)
                ds = (p * (_nt(doj, vw) - dsum)).astype(BF16)
                dq = dq + _nn(ds, kw) * (hm[j] * 0.125)
                dkw = dkw + _tn(ds, qj)
                dvw = dvw + _tn(p.astype(BF16), doj)
            dq_scr[qsl, :] = dq_scr[qsl, :] + dq
            dk_scr[ksl, :] = dk_scr[ksl, :] + dkw
            dv_scr[ksl, :] = dv_scr[ksl, :] + dvw

        _for_blocks(geom, block)
        dq_ref[...] = dq_scr[...].astype(BF16)
        dk_ref[...] = dk_scr[...].astype(BF16)
        dv_ref[...] = dv_scr[...].astype(BF16)

    blk = lambda off: pl.BlockSpec((S, LANES), lambda b, h, off=off: (b, off + h))
    return pl.pallas_call(
        body, grid=(B, n_pairs),
        in_specs=[blk(0), blk(n_pairs), blk(2 * n_pairs), blk(0), blk(0), blk(0)],
        out_specs=[blk(0), blk(0), blk(0)],
        out_shape=[jax.ShapeDtypeStruct((T, ATTN_WIDTH), BF16)] * 3,
        scratch_shapes=[pltpu.VMEM((18, Q_BLOCK, 2 * Q_BLOCK), F32),
                        pltpu.VMEM((S, LANES), F32),
                        pltpu.VMEM((S, LANES), F32),
                        pltpu.VMEM((S, LANES), F32)],
        compiler_params=_params(("arbitrary", "arbitrary")), name="attn_bwd")(proj, proj, proj, a, lse, da)


def _mid(x2d, t2d, a, proj, kv, w_s, w_sT, b_tab, g_v, w_out, g_final, B, S):
    T = B * S
    tm = 512
    nt = S // tm
    n_chunks = tm // SGU_CHUNK

    def body(x_ref, t_ref, a_ref, za_ref, ub_ref, vb_ref, zb_ref, qm_ref, zm_ref, kv_ref,
             ws_ref, wsT_ref, btab_ref, gv_ref, wout_ref, gf_ref,
             dx2_ref, da_ref, drest_ref, loss_ref, dwout_ref, dws_ref, dbs_ref, dgv_ref, dgf_ref, dkv_ref,
             dbtab_scr):
        b = pl.program_id(0)
        t = pl.program_id(1)
        first = jnp.logical_and(b == 0, t == 0)
        last = jnp.logical_and(b == B - 1, t == nt - 1)
        _, hm = _head_masks()
        lane_g = lax.broadcasted_iota(jnp.int32, (1, SGU_WIDTH), 1) // HEAD_DIM
        gm = [(lane_g == g).astype(F32) for g in range(N_SGU_GROUPS)]

        @pl.when(first)
        def _():
            loss_ref[...] = jnp.zeros_like(loss_ref)
            dwout_ref[...] = jnp.zeros_like(dwout_ref)
            dws_ref[...] = jnp.zeros_like(dws_ref)
            dbs_ref[...] = jnp.zeros_like(dbs_ref)
            dgv_ref[...] = jnp.zeros_like(dgv_ref)
            dgf_ref[...] = jnp.zeros_like(dgf_ref)
            dbtab_scr[...] = jnp.zeros_like(dbtab_scr)

        @pl.when(t == 0)
        def _():
            dkv_ref[...] = jnp.zeros_like(dkv_ref)

        sil_a, dsil_a = _silu_parts(za_ref[...])
        a_val = a_ref[...]
        gated_a = sil_a * a_val

        u, du_dub = _gelu_parts(ub_ref[...])
        vv, dvv_dvb = _gelu_parts(vb_ref[...])
        rv, vhat = _rms(vv)
        gv = gv_ref[...]
        vn = (vhat * gv).astype(BF16)
        ws = [ws_ref[g].astype(BF16) for g in range(N_SGU_GROUPS)]
        wsT = [wsT_ref[g].astype(BF16) for g in range(N_SGU_GROUPS)]
        mixed = []
        for ci in range(n_chunks):
            vn_c = vn[ci * SGU_CHUNK:(ci + 1) * SGU_CHUNK, :]
            mc = btab_ref[...]
            for g in range(N_SGU_GROUPS):
                mc = mc + gm[g] * _nn(ws[g], vn_c)
            mixed.append(mc)
        mixed = jnp.concatenate(mixed, axis=0)
        sg = u * mixed
        sil_b, dsil_b = _silu_parts(zb_ref[...])
        gated_b = sil_b * sg

        kvv = kv_ref[...].astype(BF16)
        qm = qm_ref[...]
        probs, qs_m, mo_pairs = [], [], []
        for pr in range(2):
            cols = slice(pr * LANES, (pr + 1) * LANES)
            kp = kvv[:, pr * LANES:(pr + 1) * LANES]
            vp = kvv[:, MEM_WIDTH + pr * LANES:MEM_WIDTH + (pr + 1) * LANES]
            mo_p = 0.0
            for j in (0, 1):
                qj = (qm[:, cols] * (hm[j] * 0.125)).astype(BF16)
                s = _nt(qj, kp)
                e = jnp.exp(s - jnp.max(s, axis=1, keepdims=True))
                p = e / jnp.sum(e, axis=1, keepdims=True)
                mo_p = mo_p + _nn(p.astype(BF16), vp) * hm[j]
                probs.append(p)
                qs_m.append(qj)
            mo_pairs.append(mo_p)
        mo = jnp.concatenate(mo_pairs, axis=1)
        sil_m, dsil_m = _silu_parts(zm_ref[...])
        gated_m = sil_m * mo

        gated = jnp.concatenate([gated_a, gated_b, gated_m], axis=1).astype(BF16)
        wout = wout_ref[...]
        x2 = x_ref[...] + _nn(gated, wout)
        r2, xh2 = _rms(x2)
        gf = gf_ref[...]
        err = xh2 * gf - t_ref[...]
        loss_ref[...] += jnp.sum(err * err) * (0.5 / D_MODEL)

        dy = err * (1.0 / D_MODEL)
        dgf_ref[...] += jnp.sum(dy * xh2, axis=0, keepdims=True)
        gdy = dy * gf
        dx2 = r2 * (gdy - xh2 * jnp.mean(gdy * xh2, axis=1, keepdims=True))
        dx2_ref[...] = dx2
        dx2b = dx2.astype(BF16)
        dwout_ref[...] += _tn(gated, dx2b)
        dgated = _nt(dx2b, wout)
        dga = dgated[:, 0:ATTN_WIDTH]
        dgb = dgated[:, ATTN_WIDTH:ATTN_WIDTH + SGU_WIDTH]
        dgm = dgated[:, ATTN_WIDTH + SGU_WIDTH:]

        da_ref[...] = dga * sil_a
        dza = dga * a_val * dsil_a

        dsg = dgb * sil_b
        dzb = dgb * sg * dsil_b
        dub = dsg * mixed * du_dub
        dmixed = dsg * u
        dmixed_b = dmixed.astype(BF16)
        dvn = []
        dbtab = dbtab_scr[...]
        for ci in range(n_chunks):
            rows = slice(ci * SGU_CHUNK, (ci + 1) * SGU_CHUNK)
            dm_c = dmixed_b[rows, :]
            vn_c = vn[rows, :]
            dvn_c = 0.0
            for g in range(N_SGU_GROUPS):
                dvn_c = dvn_c + gm[g] * _nn(wsT[g], dm_c)
                dws_ref[g] += _nt((dmixed[rows, :] * gm[g]).astype(BF16), vn_c)
            dvn.append(dvn_c)
            dbtab = dbtab + dmixed[rows, :]
        dbtab_scr[...] = dbtab
        dvn = jnp.concatenate(dvn, axis=0)
        dgv_ref[...] += jnp.sum(dvn * vhat, axis=0, keepdims=True)
        tv = dvn * gv
        dvv = rv * (tv - vhat * jnp.mean(tv * vhat, axis=1, keepdims=True))
        dvb = dvv * dvv_dvb

        dmo = dgm * sil_m
        dzm = dgm * mo * dsil_m
        dqm_pairs = []
        dk_pairs, dv_pairs = [], []
        for pr in range(2):
            kp = kvv[:, pr * LANES:(pr + 1) * LANES]
            vp = kvv[:, MEM_WIDTH + pr * LANES:MEM_WIDTH + (pr + 1) * LANES]
            dmo_p = dmo[:, pr * LANES:(pr + 1) * LANES]
            dq_p = 0.0
            dk_p = 0.0
            dv_p = 0.0
            for j in (0, 1):
                p = probs[2 * pr + j]
                dmo_j = (dmo_p * hm[j]).astype(BF16)
                dp = _nt(dmo_j, vp)
                ds = (p * (dp - jnp.sum(dp * p, axis=1, keepdims=True))).astype(BF16)
                dq_p = dq_p + _nn(ds, kp) * (hm[j] * 0.125)
                dk_p = dk_p + _tn(ds, qs_m[2 * pr + j])
                dv_p = dv_p + _tn(p.astype(BF16), dmo_j)
            dqm_pairs.append(dq_p)
            dk_pairs.append(dk_p)
            dv_pairs.append(dv_p)
        dqm = jnp.concatenate(dqm_pairs, axis=1)
        dkv_ref[...] += jnp.concatenate(dk_pairs + dv_pairs, axis=1)

        drest_ref[...] = jnp.concatenate([dza, dub, dvb, dzb, dqm, dzm], axis=1).astype(BF16)

        @pl.when(last)
        def _():
            lane = lax.broadcasted_iota(jnp.int32, (1, LANES), 1)
            dbt = dbtab_scr[...]
            out = jnp.zeros((SGU_CHUNK, LANES), F32)
            for g in range(N_SGU_GROUPS):
                out = out + jnp.where(lane == g, jnp.sum(dbt * gm[g], axis=1, keepdims=True), 0.0)
            dbs_ref[...] = out

    tile = lambda w, cb: pl.BlockSpec((tm, w), lambda b, t, cb=cb: (b * nt + t, cb))
    const = lambda shape: pl.BlockSpec(shape, lambda b, t, n=len(shape): (0,) * n)
    return pl.pallas_call(
        body, grid=(B, nt),
        in_specs=[tile(D_MODEL, 0), tile(D_MODEL, 0), tile(ATTN_WIDTH, 0),
                  tile(ATTN_WIDTH, 3),
                  tile(SGU_WIDTH, 8), tile(SGU_WIDTH, 9), tile(SGU_WIDTH, 10),
                  tile(MEM_WIDTH, 11), tile(MEM_WIDTH, 12),
                  pl.BlockSpec((N_MEM, 2 * MEM_WIDTH), lambda b, t: (b, 0)),
                  const((N_SGU_GROUPS, SGU_CHUNK, SGU_CHUNK)), const((N_SGU_GROUPS, SGU_CHUNK, SGU_CHUNK)),
                  const((SGU_CHUNK, SGU_WIDTH)), const((1, SGU_WIDTH)),
                  const((D_MODEL, D_MODEL)), const((1, D_MODEL))],
        out_specs=[tile(D_MODEL, 0), tile(ATTN_WIDTH, 0), tile(REST_COLS, 0),
                   const((8, LANES)), const((D_MODEL, D_MODEL)),
                   const((N_SGU_GROUPS, SGU_CHUNK, SGU_CHUNK)), const((SGU_CHUNK, LANES)),
                   const((1, SGU_WIDTH)), const((1, D_MODEL)),
                   pl.BlockSpec((N_MEM, 2 * MEM_WIDTH), lambda b, t: (b, 0))],
        out_shape=[jax.ShapeDtypeStruct((T, D_MODEL), F32), jax.ShapeDtypeStruct((T, ATTN_WIDTH), F32),
                   jax.ShapeDtypeStruct((T, REST_COLS), BF16),
                   jax.ShapeDtypeStruct((8, LANES), F32), jax.ShapeDtypeStruct((D_MODEL, D_MODEL), F32),
                   jax.ShapeDtypeStruct((N_SGU_GROUPS, SGU_CHUNK, SGU_CHUNK), F32),
                   jax.ShapeDtypeStruct((SGU_CHUNK, LANES), F32),
                   jax.ShapeDtypeStruct((1, SGU_WIDTH), F32), jax.ShapeDtypeStruct((1, D_MODEL), F32),
                   jax.ShapeDtypeStruct((B * N_MEM, 2 * MEM_WIDTH), F32)],
        scratch_shapes=[pltpu.VMEM((SGU_CHUNK, SGU_WIDTH), F32)],
        compiler_params=_params(("arbitrary", "arbitrary")), name="mid")(
            x2d, t2d, a, proj, proj, proj, proj, proj, proj, kv, w_s, w_sT, b_tab, g_v, w_out, g_final)


def _inproj_bwd_dx(dq, dk, dv, drest, x2d, dx2, g_norm, w_in):
    T = x2d.shape[0]
    tm = 512
    W = ATTN_WIDTH

    def body(dq_ref, dk_ref, dv_ref, dr_ref, x_ref, dx2_ref, g_ref, w_ref, gx_ref, dg_ref):
        @pl.when(pl.program_id(0) == 0)
        def _():
            dg_ref[...] = jnp.zeros_like(dg_ref)

        dh = (_nt(dq_ref[...], w_ref[:, 0:W]) + _nt(dk_ref[...], w_ref[:, W:2 * W])
              + _nt(dv_ref[...], w_ref[:, 2 * W:3 * W]) + _nt(dr_ref[...], w_ref[:, QKV_COLS:IN_COLS]))
        r, xh = _rms(x_ref[...])
        dg_ref[...] += jnp.sum(dh * xh, axis=0, keepdims=True)
        th = dh * g_ref[...]
        gx_ref[...] = r * (th - xh * jnp.mean(th * xh, axis=1, keepdims=True)) + dx2_ref[...]

    tile = lambda w: pl.BlockSpec((tm, w), lambda i: (i, 0))
    return pl.pallas_call(
        body, grid=(T // tm,),
        in_specs=[tile(W), tile(W), tile(W), tile(REST_COLS), tile(D_MODEL), tile(D_MODEL),
                  pl.BlockSpec((1, D_MODEL), lambda i: (0, 0)),
                  pl.BlockSpec((D_MODEL, IN_COLS), lambda i: (0, 0))],
        out_specs=[tile(D_MODEL), pl.BlockSpec((1, D_MODEL), lambda i: (0, 0))],
        out_shape=[jax.ShapeDtypeStruct((T, D_MODEL), F32), jax.ShapeDtypeStruct((1, D_MODEL), F32)],
        compiler_params=_params(("arbitrary",)), name="inproj_bwd_dx")(dq, dk, dv, drest, x2d, dx2, g_norm, w_in)


def _inproj_bwd_dw(dq, dk, dv, drest, x2d, g_norm):
    T = x2d.shape[0]
    tm = 512
    W = ATTN_WIDTH

    def body(dq_ref, dk_ref, dv_ref, dr_ref, x_ref, g_ref, dw_ref):
        @pl.when(pl.program_id(0) == 0)
        def _():
            dw_ref[...] = jnp.zeros_like(dw_ref)

        _, xh = _rms(x_ref[...])
        h = (xh * g_ref[...]).astype(BF16)
        dw_ref[:, 0:W] += _tn(h, dq_ref[...])
        dw_ref[:, W:2 * W] += _tn(h, dk_ref[...])
        dw_ref[:, 2 * W:3 * W] += _tn(h, dv_ref[...])
        dw_ref[:, QKV_COLS:IN_COLS] += _tn(h, dr_ref[...])

    tile = lambda w: pl.BlockSpec((tm, w), lambda i: (i, 0))
    return pl.pallas_call(
        body, grid=(T // tm,),
        in_specs=[tile(W), tile(W), tile(W), tile(REST_COLS), tile(D_MODEL),
                  pl.BlockSpec((1, D_MODEL), lambda i: (0, 0))],
        out_specs=pl.BlockSpec((D_MODEL, IN_COLS), lambda i: (0, 0)),
        out_shape=jax.ShapeDtypeStruct((D_MODEL, IN_COLS), F32),
        compiler_params=_params(("arbitrary",)), name="inproj_bwd_dw")(dq, dk, dv, drest, x2d, g_norm)


def _adamw(w, g, m, v, name):
    R, C = w.shape
    br = 256 if R % 256 == 0 else R

    def body(w_ref, g_ref, m_ref, v_ref, d_ref, nm_ref, nv_ref):
        gg = g_ref[...]
        nm = ADAM_B1 * m_ref[...] + (1.0 - ADAM_B1) * gg
        nv = ADAM_B2 * v_ref[...] + (1.0 - ADAM_B2) * (gg * gg)
        m_hat = nm / (1.0 - ADAM_B1 ** ADAM_STEP)
        v_hat = nv / (1.0 - ADAM_B2 ** ADAM_STEP)
        d_ref[...] = -ADAM_LR * (m_hat / (jnp.sqrt(v_hat) + ADAM_EPS) + ADAM_WD * w_ref[...])
        nm_ref[...] = nm
        nv_ref[...] = nv

    spec = pl.BlockSpec((br, C), lambda i: (i, 0))
    return pl.pallas_call(
        body, grid=(R // br,), in_specs=[spec] * 4, out_specs=[spec] * 3,
        out_shape=[jax.ShapeDtypeStruct((R, C), F32)] * 3,
        compiler_params=_params(("arbitrary",)), name=name)(w, g, m, v)


def _place():
    x, y, c = lax.axis_index("x"), lax.axis_index("y"), lax.axis_index("c")
    chip = 2 * x + y
    peers = [(x, 1 - y), (1 - x, y), (1 - x, 1 - y)]
    peer_chip = [2 * px + py for px, py in peers]
    return x, y, c, chip, peers, peer_chip


def _remote(src, dst, send_sem, recv_sem, dev):
    return pltpu.make_async_remote_copy(src_ref=src, dst_ref=dst, send_sem=send_sem, recv_sem=recv_sem,
                                        device_id=dev, device_id_type=MESH)


def _ag_weights(w_in, w_kv, w_out):
    def body(win_ref, wkv_ref, wout_ref, oin_ref, okv_ref, oout_ref, s_ici, r_ici, s_d2d, r_d2d):
        x, y, c, chip, peers, peer_chip = _place()
        sib = (x, y, 1 - c)
        outs = [oin_ref, okv_ref, oout_ref]
        for src, out in zip([win_ref, wkv_ref, wout_ref], outs):
            out[chip] = src[...].astype(BF16)

        def half(out, k, cc):
            rows = out.shape[1] // 2
            return out.at[k, pl.ds(pl.multiple_of(cc * rows, 16), rows), :]

        sent = []
        for m, (px, py) in enumerate(peers):
            for w, out in enumerate(outs):
                cp = _remote(half(out, chip, c), half(out, chip, c), s_ici.at[3 * m + w], r_ici.at[3 * m + w],
                             (px, py, c))
                cp.start()
                sent.append(cp)
        for m, (px, py) in enumerate(peers):
            for w, out in enumerate(outs):
                blk = half(out, peer_chip[m], c)
                _remote(blk, blk, s_ici.at[3 * m + w], r_ici.at[3 * m + w], (px, py, c)).wait_recv()
                fw = _remote(blk, blk, s_d2d.at[3 * m + w], r_d2d.at[3 * m + w], sib)
                fw.start()
                sent.append(fw)
        for m in range(3):
            for w, out in enumerate(outs):
                blk = half(out, peer_chip[m], 1 - c)
                _remote(blk, blk, s_d2d.at[3 * m + w], r_d2d.at[3 * m + w], sib).wait_recv()
        for cp in sent:
            cp.wait_send()

    vmem = pl.BlockSpec(memory_space=pltpu.VMEM)
    return pl.pallas_call(
        body,
        out_shape=[jax.ShapeDtypeStruct((N_CHIPS,) + w.shape, BF16) for w in (w_in, w_kv, w_out)],
        in_specs=[vmem] * 3, out_specs=[vmem] * 3,
        scratch_shapes=[pltpu.SemaphoreType.DMA((9,))] * 4,
        compiler_params=pltpu.CompilerParams(vmem_limit_bytes=VMEM_LIMIT), name="ag_weights")(w_in, w_kv, w_out)


def _reduce_grads(g_in, g_kv, g_out, g_small):
    big_shapes = [g.shape for g in (g_in, g_kv, g_out)]
    half_small = SMALL_ROWS // 2
    row_block = 64

    def body(gin_ref, gkv_ref, gout_ref, gsm_ref, oin_ref, okv_ref, oout_ref, osm_ref,
             ra_in, ra_kv, ra_out, ra_sm, sb_in, sb_kv, sb_out, rb_in, rb_kv, rb_out, p_sm,
             sa_s, sa_r, sb_s, sb_r, sc_s, sc_r):
        x, y, c, chip, peers, peer_chip = _place()
        sib = (x, y, 1 - c)
        gs = [gin_ref, gkv_ref, gout_ref]
        ras = [ra_in, ra_kv, ra_out]
        sbs = [sb_in, sb_kv, sb_out]
        rbs = [rb_in, rb_kv, rb_out]
        outs = [oin_ref, okv_ref, oout_ref]
        halves = [s[1] // 2 for s in big_shapes]

        def rows_of(cc, n):
            return pl.ds(pl.multiple_of(cc * n, 8), n)

        a_copies = []
        for w in range(3):
            cp = _remote(gs[w].at[:, rows_of(1 - c, halves[w]), :], ras[w], sa_s.at[w], sa_r.at[w], sib)
            cp.start()
            a_copies.append(cp)
        cp = _remote(gsm_ref.at[rows_of(1 - c, half_small), :], ra_sm, sa_s.at[3], sa_r.at[3], sib)
        cp.start()
        a_copies.append(cp)
        for cp in a_copies:
            cp.wait_recv()

        b_copies = []
        for w in range(3):
            n = halves[w]
            for m in range(3):
                k = peer_chip[m]

                def chip_sum(i, carry, w=w, m=m, k=k, n=n):
                    r0 = pl.multiple_of(i * row_block, row_block)
                    mine = gs[w][k, pl.ds(pl.multiple_of(c * n + r0, 8), row_block), :]
                    sbs[w][m, pl.ds(r0, row_block), :] = (mine + ras[w][k, pl.ds(r0, row_block), :]).astype(BF16)
                    return carry
                lax.fori_loop(0, n // row_block, chip_sum, 0)
                cp = _remote(sbs[w].at[m], rbs[w].at[m], sb_s.at[3 * w + m], sb_r.at[3 * w + m],
                             (peers[m][0], peers[m][1], c))
                cp.start()
                b_copies.append(cp)
        p_sm[chip] = gsm_ref[rows_of(c, half_small), :] + ra_sm[...]
        for m in range(3):
            cp = _remote(p_sm.at[chip], p_sm.at[chip], sb_s.at[9 + m], sb_r.at[9 + m],
                         (peers[m][0], peers[m][1], c))
            cp.start()
            b_copies.append(cp)

        c_copies = []
        for w in range(3):
            n = halves[w]
            for m in range(3):
                _remote(sbs[w].at[m], rbs[w].at[m], sb_s.at[3 * w + m], sb_r.at[3 * w + m],
                        (peers[m][0], peers[m][1], c)).wait_recv()

            def total(i, carry, w=w, n=n):
                r0 = pl.multiple_of(i * row_block, row_block)
                rows = pl.ds(r0, row_block)
                dst = pl.ds(pl.multiple_of(c * n + r0, 8), row_block)
                acc = gs[w][chip, dst, :] + ras[w][chip, rows, :]
                for m in range(3):
                    acc = acc + rbs[w][m, rows, :].astype(F32)
                outs[w][dst, :] = acc
                return carry
            lax.fori_loop(0, n // row_block, total, 0)
            mine = outs[w].at[rows_of(c, n), :]
            cp = _remote(mine, mine, sc_s.at[w], sc_r.at[w], sib)
            cp.start()
            c_copies.append(cp)
        for m in range(3):
            _remote(p_sm.at[chip], p_sm.at[peer_chip[m]], sb_s.at[9 + m], sb_r.at[9 + m],
                    (peers[m][0], peers[m][1], c)).wait_recv()
        osm_ref[rows_of(c, half_small), :] = (p_sm[0] + p_sm[1]) + (p_sm[2] + p_sm[3])
        mine = osm_ref.at[rows_of(c, half_small), :]
        cp = _remote(mine, mine, sc_s.at[3], sc_r.at[3], sib)
        cp.start()
        c_copies.append(cp)
        for w in range(3):
            theirs = outs[w].at[rows_of(1 - c, halves[w]), :]
            _remote(theirs, theirs, sc_s.at[w], sc_r.at[w], sib).wait_recv()
        theirs = osm_ref.at[rows_of(1 - c, half_small), :]
        _remote(theirs, theirs, sc_s.at[3], sc_r.at[3], sib).wait_recv()
        for cp in a_copies + b_copies + c_copies:
            cp.wait_send()

    vmem = pl.BlockSpec(memory_space=pltpu.VMEM)
    scratch = ([pltpu.VMEM((N_CHIPS, s[1] // 2, s[2]), F32) for s in big_shapes]
               + [pltpu.VMEM((half_small, LANES), F32)]
               + [pltpu.VMEM((3, s[1] // 2, s[2]), BF16) for s in big_shapes]
               + [pltpu.VMEM((3, s[1] // 2, s[2]), BF16) for s in big_shapes]
               + [pltpu.VMEM((N_CHIPS, half_small, LANES), F32)]
               + [pltpu.SemaphoreType.DMA((4,)), pltpu.SemaphoreType.DMA((4,)),
                  pltpu.SemaphoreType.DMA((12,)), pltpu.SemaphoreType.DMA((12,)),
                  pltpu.SemaphoreType.DMA((4,)), pltpu.SemaphoreType.DMA((4,))])
    return pl.pallas_call(
        body,
        out_shape=[jax.ShapeDtypeStruct(s[1:], F32) for s in big_shapes]
        + [jax.ShapeDtypeStruct((SMALL_ROWS, LANES), F32)],
        in_specs=[vmem] * 4, out_specs=[vmem] * 4, scratch_shapes=scratch,
        compiler_params=pltpu.CompilerParams(vmem_limit_bytes=58 * 1024 * 1024),
        name="reduce_grads")(g_in, g_kv, g_out, g_small)


_SMALL_PARTS = (("g_norm", 8, 8), ("w_s", 512, 512), ("b_s", 4, 8), ("g_v", 2, 8), ("g_mem", 8, 8),
                ("g_final", 8, 8))


def _pack_small(parts):
    rows = []
    for (name, used, padded), p in zip(_SMALL_PARTS, parts):
        p = p.reshape(used, LANES)
        if padded > used:
            p = jnp.pad(p, ((0, padded - used), (0, 0)))
        rows.append(p)
    total = sum(p for _, _, p in _SMALL_PARTS)
    rows.append(jnp.zeros((SMALL_ROWS - total, LANES), F32))
    return jnp.concatenate(rows, axis=0)


def _unpack_small(packed, shapes):
    out = []
    off = 0
    for (name, used, padded), shape in zip(_SMALL_PARTS, shapes):
        out.append(packed[off:off + used].reshape(shape))
        off += padded
    return out


def _local_step(x, mem, target, g_norm, w_in, w_s, b_s, g_v, g_mem, w_kv, w_out, g_final):
    B, S, _ = x.shape
    x2d = x.reshape(B * S, D_MODEL)
    t2d = target.reshape(B * S, D_MODEL)
    mem2d = mem.reshape(B * N_MEM, D_MODEL)

    proj = _inproj_fwd(x2d, g_norm, w_in)
    kv = _kv_fwd(mem2d, g_mem, w_kv)
    a, lse = _attn_fwd(proj, B, S)
    w_sT = jnp.swapaxes(w_s, 1, 2)
    b_tab = jnp.repeat(b_s.T, HEAD_DIM, axis=1)
    (dx2, da, drest, loss, d_wout, d_ws, d_bs, d_gv, d_gf, dkv) = _mid(
        x2d, t2d, a, proj, kv, w_s, w_sT, b_tab, g_v, w_out, g_final, B, S)
    dq, dk, dv = _attn_bwd(proj, a, lse, da, B, S)
    grad_x, d_gnorm = _inproj_bwd_dx(dq, dk, dv, drest, x2d, dx2, g_norm, w_in)
    d_win = _inproj_bwd_dw(dq, dk, dv, drest, x2d, g_norm)
    d_wkv, d_gmem = _kv_bwd(mem2d, g_mem, w_kv, dkv)
    d_bs = d_bs[:, :N_SGU_GROUPS].T
    return (loss[0, 0], grad_x.reshape(B, S, D_MODEL),
            dict(g_norm=d_gnorm, w_in=d_win, w_s=d_ws, b_s=d_bs, g_v=d_gv, g_mem=d_gmem, w_kv=d_wkv,
                 w_out=d_wout, g_final=d_gf))


def kernel(x, mem, g_norm, w_in, w_sgu_spatial, b_sgu_spatial, g_sgu_v, g_mem, w_mem_kv, w_out, g_final, loss_target, m_g_norm, m_w_in, m_w_sgu_spatial, m_b_sgu_spatial, m_g_sgu_v, m_g_mem, m_w_mem_kv, m_w_out, m_g_final, v_g_norm, v_w_in, v_w_sgu_spatial, v_b_sgu_spatial, v_g_sgu_v, v_g_mem, v_w_mem_kv, v_w_out, v_g_final):
    win_all, wkv_all, wout_all = _ag_weights(w_in[0], w_mem_kv[0], w_out[0])
    rows_in, cols_in = w_in.shape[1], w_in.shape[2]
    w_in_full = jnp.transpose(win_all, (1, 0, 2)).reshape(rows_in, N_CHIPS * cols_in)
    w_kv_full = wkv_all.reshape(-1, wkv_all.shape[-1])
    w_out_full = wout_all.reshape(-1, wout_all.shape[-1])

    loss, grad_x, g = _local_step(
        x, mem, loss_target, g_norm, w_in_full, w_sgu_spatial[0], b_sgu_spatial[0], g_sgu_v, g_mem,
        w_kv_full, w_out_full, g_final.reshape(1, D_MODEL))
    loss = lax.psum(loss, ("x", "y", "c"))

    small_names = ("g_norm", "w_s", "b_s", "g_v", "g_mem", "g_final")
    g_in_stack = jnp.transpose(g["w_in"].reshape(rows_in, N_CHIPS, cols_in), (1, 0, 2))
    g_kv_stack = g["w_kv"].reshape((N_CHIPS,) + w_mem_kv.shape[1:])
    g_out_stack = g["w_out"].reshape((N_CHIPS,) + w_out.shape[1:])
    gr_in, gr_kv, gr_out, gr_small = _reduce_grads(
        g_in_stack, g_kv_stack, g_out_stack, _pack_small([g[n] for n in small_names]))

    small_w = (g_norm, w_sgu_spatial, b_sgu_spatial, g_sgu_v, g_mem, g_final)
    small_m = (m_g_norm, m_w_sgu_spatial, m_b_sgu_spatial, m_g_sgu_v, m_g_mem, m_g_final)
    small_v = (v_g_norm, v_w_sgu_spatial, v_b_sgu_spatial, v_g_sgu_v, v_g_mem, v_g_final)
    shapes = [w.shape for w in small_w]
    d_small, nm_small, nv_small = _adamw(_pack_small(small_w), gr_small, _pack_small(small_m),
                                         _pack_small(small_v), "adamw_small")
    d_in, nm_in, nv_in = _adamw(w_in[0], gr_in, m_w_in[0], v_w_in[0], "adamw_w_in")
    d_kv, nm_kv, nv_kv = _adamw(w_mem_kv[0], gr_kv, m_w_mem_kv[0], v_w_mem_kv[0], "adamw_w_kv")
    d_out, nm_out, nv_out = _adamw(w_out[0], gr_out, m_w_out[0], v_w_out[0], "adamw_w_out")

    def leaves(packed, big_in, big_kv, big_out):
        s_norm, s_ws, s_bs, s_gv, s_gmem, s_gf = _unpack_small(packed, shapes)
        return [s_norm, big_in[None], s_ws, s_bs, s_gv, s_gmem, big_kv[None], big_out[None], s_gf]

    return (loss, grad_x, *leaves(gr_small, gr_in, gr_kv, gr_out), *leaves(d_small, d_in, d_kv, d_out),
            *leaves(nm_small, nm_in, nm_kv, nm_out), *leaves(nv_small, nv_in, nv_kv, nv_out))
```

```python
import functools

import jax
import jax.numpy as jnp
from jax import lax
from jax.experimental import pallas as pl
from jax.experimental.pallas import tpu as pltpu

F32 = jnp.float32
BF16 = jnp.bfloat16
MESH = pl.DeviceIdType.MESH

D_MODEL = 1024
ATTN_WIDTH = 512
SGU_WIDTH = 256
MEM_WIDTH = 256
N_MEM = 256
IN_COLS = 3328
QKV_COLS = 3 * ATTN_WIDTH
REST_COLS = IN_COLS - QKV_COLS
SGU_CHUNK = 128
N_SGU_GROUPS = 4
EPS = 1e-6
NEG_INF = -1e30
DILATIONS = (1, 4, 16)
RADIUS = 64
Q_BLOCK = 128
LANES = 128
HEAD_DIM = 64

ADAM_LR = 0.001
ADAM_B1 = 0.9
ADAM_B2 = 0.999
ADAM_EPS = 1e-08
ADAM_WD = 0.01
ADAM_STEP = 10

N_CHIPS = 4
VMEM_LIMIT = 56 * 1024 * 1024
SMALL_ROWS = 560


def _params(sem=None, vmem=VMEM_LIMIT):
    return pltpu.CompilerParams(dimension_semantics=sem, vmem_limit_bytes=vmem)


def _nn(a, b):
    return jnp.dot(a, b, preferred_element_type=F32)


def _nt(a, b):
    return lax.dot_general(a, b, (((1,), (1,)), ((), ())), preferred_element_type=F32)


def _tn(a, b):
    return lax.dot_general(a, b, (((0,), (0,)), ((), ())), preferred_element_type=F32)


def _rms(x):
    r = lax.rsqrt(jnp.mean(x * x, axis=-1, keepdims=True) + EPS)
    return r, x * r


def _head_masks():
    lane = lax.broadcasted_iota(jnp.int32, (1, LANES), 1)
    lo = lane < HEAD_DIM
    return lo, (lo.astype(F32), (~lo).astype(F32))


def _silu_parts(z):
    s = jax.nn.sigmoid(z)
    return z * s, s * (1.0 + z * (1.0 - s))


def _gelu_parts(x):
    c = 0.7978845608028654
    x2 = x * x
    t = jnp.tanh(c * (x + 0.044715 * (x * x2)))
    val = 0.5 * x * (1.0 + t)
    grad = 0.5 * (1.0 + t) + 0.5 * x * (1.0 - t * t) * (c * (1.0 + 3.0 * 0.044715 * x2))
    return val, grad


def _inproj_fwd(x2d, g_norm, w_in):
    T = x2d.shape[0]
    tm = 512

    def body(x_ref, g_ref, w_ref, o_ref):
        _, xh = _rms(x_ref[...])
        h = (xh * g_ref[...]).astype(BF16)
        o_ref[...] = _nn(h, w_ref[...])

    return pl.pallas_call(
        body, grid=(T // tm,),
        in_specs=[pl.BlockSpec((tm, D_MODEL), lambda i: (i, 0)),
                  pl.BlockSpec((1, D_MODEL), lambda i: (0, 0)),
                  pl.BlockSpec((D_MODEL, IN_COLS), lambda i: (0, 0))],
        out_specs=pl.BlockSpec((tm, IN_COLS), lambda i: (i, 0)),
        out_shape=jax.ShapeDtypeStruct((T, IN_COLS), F32),
        compiler_params=_params(("arbitrary",)), name="inproj_fwd")(x2d, g_norm, w_in)


def _kv_fwd(mem2d, g_mem, w_kv):
    Tm = mem2d.shape[0]

    def body(m_ref, g_ref, w_ref, o_ref):
        _, mh = _rms(m_ref[...])
        o_ref[...] = _nn((mh * g_ref[...]).astype(BF16), w_ref[...])

    return pl.pallas_call(
        body, out_shape=jax.ShapeDtypeStruct((Tm, 2 * MEM_WIDTH), F32),
        compiler_params=_params(), name="kv_fwd")(mem2d, g_mem, w_kv)


def _kv_bwd(mem2d, g_mem, w_kv, dkv):
    Tm = mem2d.shape[0]

    def body(m_ref, g_ref, w_ref, dkv_ref, dw_ref, dg_ref):
        _, mh = _rms(m_ref[...])
        memn = (mh * g_ref[...]).astype(BF16)
        dkvb = dkv_ref[...].astype(BF16)
        dw_ref[...] = _tn(memn, dkvb)
        dmemn = _nt(dkvb, w_ref[...])
        dg_ref[...] = jnp.sum(dmemn * mh, axis=0, keepdims=True)

    return pl.pallas_call(
        body, out_shape=(jax.ShapeDtypeStruct((D_MODEL, 2 * MEM_WIDTH), F32),
                         jax.ShapeDtypeStruct((1, D_MODEL), F32)),
        compiler_params=_params(), name="kv_bwd")(mem2d, g_mem, w_kv, dkv)


def _attn_geometry(S):
    geom = []
    for d in DILATIONS:
        L = S // d
        assert L % Q_BLOCK == 0
        geom.append((d, L, min(2 * Q_BLOCK, L), L // Q_BLOCK))
    return geom


def _init_bias(bias_scr, geom, hp):
    row = lax.broadcasted_iota(jnp.int32, (Q_BLOCK, 2 * Q_BLOCK), 0)
    col = lax.broadcasted_iota(jnp.int32, (Q_BLOCK, 2 * Q_BLOCK), 1)
    for j in (0, 1):
        bits = (126 - (2 * hp + j)) * (1 << 23)
        slope = lax.bitcast_convert_type(jnp.full((1, 1), bits, jnp.int32), F32)
        for di, (d, _, _, _) in enumerate(geom):
            for cls, off in enumerate((0, -RADIUS, -2 * RADIUS)):
                dist = jnp.abs(col - row + off)
                bias_scr[di * 6 + cls * 2 + j] = jnp.where(
                    dist <= RADIUS, -(slope * float(d)) * dist.astype(F32), NEG_INF)


def _block_slices(d, L, KW, nqb, r, qb):
    qs = qb * Q_BLOCK
    ks = jnp.clip(qs - RADIUS, 0, L - KW)
    cls = jnp.where(qb == 0, 0, jnp.where(qb == nqb - 1, 2, 1))
    if d == 1:
        qsl = pl.ds(pl.multiple_of(qs, Q_BLOCK), Q_BLOCK)
        ksl = pl.ds(pl.multiple_of(ks, RADIUS), KW)
    else:
        qsl = pl.ds(r + qs * d, Q_BLOCK, stride=d)
        ksl = pl.ds(r + ks * d, KW, stride=d)
    return qsl, ksl, cls


def _for_groups(geom, group, fn):
    for di, (d, L, KW, nqb) in enumerate(geom):
        assert (d * nqb) % group == 0

        def step(it, carry, di=di, d=d, L=L, KW=KW, nqb=nqb):
            slices = []
            for g in range(group):
                i = it * group + g
                slices.append(_block_slices(d, L, KW, nqb, i // nqb, i % nqb))
            fn(di, KW, slices)
            return carry
        lax.fori_loop(0, d * nqb // group, step, 0)


def _attn_fwd(proj, B, S):
    T = B * S
    geom = _attn_geometry(S)
    n_pairs = ATTN_WIDTH // LANES

    def body(q_ref, k_ref, v_ref, a_ref, lse_ref, bias_scr, *per_dilation):
        o_scr, m_scr, l_scr = per_dilation[0:3], per_dilation[3:6], per_dilation[6:9]
        lo, hm = _head_masks()
        _init_bias(bias_scr, geom, pl.program_id(1))

        def group(di, KW, slices):
            chains = [(g, j) for g in range(len(slices)) for j in (0, 1)]
            q = [q_ref[qsl, :] for qsl, _, _ in slices]
            kw = [k_ref[ksl, :].astype(BF16) for _, ksl, _ in slices]
            vw = [v_ref[ksl, :].astype(BF16) for _, ksl, _ in slices]
            s = {(g, j): _nt((q[g] * (hm[j] * 0.125)).astype(BF16), kw[g])
                 + bias_scr[di * 6 + slices[g][2] * 2 + j, :, pl.ds(0, KW)] for g, j in chains}
            m = {c: jnp.max(s[c], axis=1, keepdims=True) for c in chains}
            p = {c: jnp.exp(s[c] - m[c]) for c in chains}
            l = {c: jnp.sum(p[c], axis=1, keepdims=True) for c in chains}
            o = {(g, j): _nn(p[(g, j)].astype(BF16), vw[g]) for g, j in chains}
            for g, (qsl, _, _) in enumerate(slices):
                o_scr[di][qsl, :] = jnp.where(lo, o[(g, 0)], o[(g, 1)])
                m_scr[di][qsl, :] = jnp.where(lo, m[(g, 0)], m[(g, 1)])
                l_scr[di][qsl, :] = jnp.where(lo, l[(g, 0)], l[(g, 1)])

        _for_groups(geom, 8, group)

        rows_per = 256

        def combine(i, carry):
            rows = pl.ds(pl.multiple_of(i * rows_per, rows_per), rows_per)
            ms = [m_scr[di][rows, :] for di in range(3)]
            mx = jnp.maximum(jnp.maximum(ms[0], ms[1]), ms[2])
            num = 0.0
            den = 0.0
            for di in range(3):
                w = jnp.exp(ms[di] - mx)
                num = num + w * o_scr[di][rows, :]
                den = den + w * l_scr[di][rows, :]
            a_ref[rows, :] = num / den
            lse_ref[rows, :] = mx + jnp.log(den)
            return carry

        lax.fori_loop(0, S // rows_per, combine, 0)

    blk = lambda off: pl.BlockSpec((S, LANES), lambda b, h, off=off: (b, off + h))
    out_blk = pl.BlockSpec((S, LANES), lambda b, h: (b, h))
    return pl.pallas_call(
        body, grid=(B, n_pairs),
        in_specs=[blk(0), blk(n_pairs), blk(2 * n_pairs)],
        out_specs=[out_blk, out_blk],
        out_shape=[jax.ShapeDtypeStruct((T, ATTN_WIDTH), F32)] * 2,
        scratch_shapes=[pltpu.VMEM((18, Q_BLOCK, 2 * Q_BLOCK), F32)] + [pltpu.VMEM((S, LANES), F32)] * 9,
        compiler_params=_params(("arbitrary", "arbitrary")), name="attn_fwd")(proj, proj, proj)


def _attn_bwd(proj, a, lse, da, B, S):
    T = B * S
    geom = _attn_geometry(S)
    n_pairs = ATTN_WIDTH // LANES

    def body(q_ref, k_ref, v_ref, a_ref, lse_ref, do_ref, dq_ref, dk_ref, dv_ref,
             bias_scr, dq_scr, dk_scr, dv_scr):
        _, hm = _head_masks()
        _init_bias(bias_scr, geom, pl.program_id(1))
        dq_scr[...] = jnp.zeros_like(dq_scr)
        dk_scr[...] = jnp.zeros_like(dk_scr)
        dv_scr[...] = jnp.zeros_like(dv_scr)

        def group(di, KW, slices):
            n = len(slices)
            chains = [(g, j) for g in range(n) for j in (0, 1)]
            q = [q_ref[qsl, :] for qsl, _, _ in slices]
            do = [do_ref[qsl, :] for qsl, _, _ in slices]
            doa = [do[g] * a_ref[slices[g][0], :] for g in range(n)]
            lse_q = [lse_ref[qsl, :] for qsl, _, _ in slices]
            kw = [k_ref[ksl, :].astype(BF16) for _, ksl, _ in slices]
            vw = [v_ref[ksl, :].astype(BF16) for _, ksl, _ in slices]
            qj = {(g, j): (q[g] * (hm[j] * 0.125)).astype(BF16) for g, j in chains}
            doj = {(g, j): (do[g] * hm[j]).astype(BF16) for g, j in chains}
            s = {(g, j): _nt(qj[(g, j)], kw[g])
                 + bias_scr[di * 6 + slices[g][2] * 2 + j, :, pl.ds(0, KW)] for g, j in chains}
            dp = {(g, j): _nt(doj[(g, j)], vw[g]) for g, j in chains}
            dsum = {(g, j): jnp.sum(doa[g] * hm[j], axis=1, keepdims=True) for g, j in chains}
            p = {(g, j): jnp.exp(s[(g, j)] - lse_q[g][:, HEAD_DIM * j:HEAD_DIM * j + 1]) for g, j in chains}
            ds = {c: (p[c] * (dp[c] - dsum[c])).astype(BF16) for c in chains}
            pb = {c: p[c].astype(BF16) for c in chains}
            dq = [_nn(ds[(g, 0)], kw[g]) * (hm[0] * 0.125) + _nn(ds[(g, 1)], kw[g]) * (hm[1] * 0.125)
                  for g in range(n)]
            both = lambda t, g: jnp.concatenate([t[(g, 0)], t[(g, 1)]], axis=0)
            dkw = [_tn(both(ds, g), both(qj, g)) for g in range(n)]
            dvw = [_tn(both(pb, g), both(doj, g)) for g in range(n)]
            for g, (qsl, ksl, _) in enumerate(slices):
                dq_scr[qsl, :] = dq_scr[qsl, :] + dq[g]
                dk_scr[ksl, :] = dk_scr[ksl, :] + dkw[g]
                dv_scr[ksl, :] = dv_scr[ksl, :] + dvw[g]

        _for_groups(geom, 4, group)
        dq_ref[...] = dq_scr[...].astype(BF16)
        dk_ref[...] = dk_scr[...].astype(BF16)
        dv_ref[...] = dv_scr[...].astype(BF16)

    blk = lambda off: pl.BlockSpec((S, LANES), lambda b, h, off=off: (b, off + h))
    return pl.pallas_call(
        body, grid=(B, n_pairs),
        in_specs=[blk(0), blk(n_pairs), blk(2 * n_pairs), blk(0), blk(0), blk(0)],
        out_specs=[blk(0), blk(0), blk(0)],
        out_shape=[jax.ShapeDtypeStruct((T, ATTN_WIDTH), BF16)] * 3,
        scratch_shapes=[pltpu.VMEM((18, Q_BLOCK, 2 * Q_BLOCK), F32),
                        pltpu.VMEM((S, LANES), F32),
                        pltpu.VMEM((S, LANES), F32),
                        pltpu.VMEM((S, LANES), F32)],
        compiler_params=_params(("arbitrary", "arbitrary")), name="attn_bwd")(proj, proj, proj, a, lse, da)


def _mid(x2d, t2d, a, proj, kv, w_s, w_sT, b_tab, g_v, w_out, g_final, B, S):
    T = B * S
    tm = 512
    nt = S // tm
    n_chunks = tm // SGU_CHUNK

    def body(x_ref, t_ref, a_ref, za_ref, ub_ref, vb_ref, zb_ref, qm_ref, zm_ref, kv_ref,
             ws_ref, wsT_ref, btab_ref, gv_ref, wout_ref, gf_ref,
             dx2_ref, da_ref, drest_ref, loss_ref, dwout_ref, dws_ref, dbs_ref, dgv_ref, dgf_ref, dkv_ref,
             dbtab_scr):
        b = pl.program_id(0)
        t = pl.program_id(1)
        first = jnp.logical_and(b == 0, t == 0)
        last = jnp.logical_and(b == B - 1, t == nt - 1)
        _, hm = _head_masks()
        lane_g = lax.broadcasted_iota(jnp.int32, (1, SGU_WIDTH), 1) // HEAD_DIM
        gm = [(lane_g == g).astype(F32) for g in range(N_SGU_GROUPS)]

        @pl.when(first)
        def _():
            loss_ref[...] = jnp.zeros_like(loss_ref)
            dwout_ref[...] = jnp.zeros_like(dwout_ref)
            dws_ref[...] = jnp.zeros_like(dws_ref)
            dbs_ref[...] = jnp.zeros_like(dbs_ref)
            dgv_ref[...] = jnp.zeros_like(dgv_ref)
            dgf_ref[...] = jnp.zeros_like(dgf_ref)
            dbtab_scr[...] = jnp.zeros_like(dbtab_scr)

        @pl.when(t == 0)
        def _():
            dkv_ref[...] = jnp.zeros_like(dkv_ref)

        sil_a, dsil_a = _silu_parts(za_ref[...])
        a_val = a_ref[...]
        gated_a = sil_a * a_val

        u, du_dub = _gelu_parts(ub_ref[...])
        vv, dvv_dvb = _gelu_parts(vb_ref[...])
        rv, vhat = _rms(vv)
        gv = gv_ref[...]
        vn = (vhat * gv).astype(BF16)
        ws = [ws_ref[g].astype(BF16) for g in range(N_SGU_GROUPS)]
        wsT = [wsT_ref[g].astype(BF16) for g in range(N_SGU_GROUPS)]
        mixed = []
        for ci in range(n_chunks):
            vn_c = vn[ci * SGU_CHUNK:(ci + 1) * SGU_CHUNK, :]
            mc = btab_ref[...]
            for g in range(N_SGU_GROUPS):
                mc = mc + gm[g] * _nn(ws[g], vn_c)
            mixed.append(mc)
        mixed = jnp.concatenate(mixed, axis=0)
        sg = u * mixed
        sil_b, dsil_b = _silu_parts(zb_ref[...])
        gated_b = sil_b * sg

        kvv = kv_ref[...].astype(BF16)
        qm = qm_ref[...]
        probs, qs_m, mo_pairs = [], [], []
        for pr in range(2):
            cols = slice(pr * LANES, (pr + 1) * LANES)
            kp = kvv[:, pr * LANES:(pr + 1) * LANES]
            vp = kvv[:, MEM_WIDTH + pr * LANES:MEM_WIDTH + (pr + 1) * LANES]
            mo_p = 0.0
            for j in (0, 1):
                qj = (qm[:, cols] * (hm[j] * 0.125)).astype(BF16)
                s = _nt(qj, kp)
                e = jnp.exp(s - jnp.max(s, axis=1, keepdims=True))
                p = e / jnp.sum(e, axis=1, keepdims=True)
                mo_p = mo_p + _nn(p.astype(BF16), vp) * hm[j]
                probs.append(p)
                qs_m.append(qj)
            mo_pairs.append(mo_p)
        mo = jnp.concatenate(mo_pairs, axis=1)
        sil_m, dsil_m = _silu_parts(zm_ref[...])
        gated_m = sil_m * mo

        gated = jnp.concatenate([gated_a, gated_b, gated_m], axis=1).astype(BF16)
        wout = wout_ref[...]
        x2 = x_ref[...] + _nn(gated, wout)
        r2, xh2 = _rms(x2)
        gf = gf_ref[...]
        err = xh2 * gf - t_ref[...]
        loss_ref[...] += jnp.sum(err * err) * (0.5 / D_MODEL)

        dy = err * (1.0 / D_MODEL)
        dgf_ref[...] += jnp.sum(dy * xh2, axis=0, keepdims=True)
        gdy = dy * gf
        dx2 = r2 * (gdy - xh2 * jnp.mean(gdy * xh2, axis=1, keepdims=True))
        dx2_ref[...] = dx2
        dx2b = dx2.astype(BF16)
        dwout_ref[...] += _tn(gated, dx2b)
        dgated = _nt(dx2b, wout)
        dga = dgated[:, 0:ATTN_WIDTH]
        dgb = dgated[:, ATTN_WIDTH:ATTN_WIDTH + SGU_WIDTH]
        dgm = dgated[:, ATTN_WIDTH + SGU_WIDTH:]

        da_ref[...] = dga * sil_a
        dza = dga * a_val * dsil_a

        dsg = dgb * sil_b
        dzb = dgb * sg * dsil_b
        dub = dsg * mixed * du_dub
        dmixed = dsg * u
        dmixed_b = dmixed.astype(BF16)
        dvn = []
        dbtab = dbtab_scr[...]
        for ci in range(n_chunks):
            rows = slice(ci * SGU_CHUNK, (ci + 1) * SGU_CHUNK)
            dm_c = dmixed_b[rows, :]
            vn_c = vn[rows, :]
            dvn_c = 0.0
            for g in range(N_SGU_GROUPS):
                dvn_c = dvn_c + gm[g] * _nn(wsT[g], dm_c)
                dws_ref[g] += _nt((dmixed[rows, :] * gm[g]).astype(BF16), vn_c)
            dvn.append(dvn_c)
            dbtab = dbtab + dmixed[rows, :]
        dbtab_scr[...] = dbtab
        dvn = jnp.concatenate(dvn, axis=0)
        dgv_ref[...] += jnp.sum(dvn * vhat, axis=0, keepdims=True)
        tv = dvn * gv
        dvv = rv * (tv - vhat * jnp.mean(tv * vhat, axis=1, keepdims=True))
        dvb = dvv * dvv_dvb

        dmo = dgm * sil_m
        dzm = dgm * mo * dsil_m
        dqm_pairs = []
        dk_pairs, dv_pairs = [], []
        for pr in range(2):
            kp = kvv[:, pr * LANES:(pr + 1) * LANES]
            vp = kvv[:, MEM_WIDTH + pr * LANES:MEM_WIDTH + (pr + 1) * LANES]
            dmo_p = dmo[:, pr * LANES:(pr + 1) * LANES]
            dq_p = 0.0
            dk_p = 0.0
            dv_p = 0.0
            for j in (0, 1):
                p = probs[2 * pr + j]
                dmo_j = (dmo_p * hm[j]).astype(BF16)
                dp = _nt(dmo_j, vp)
                ds = (p * (dp - jnp.sum(dp * p, axis=1, keepdims=True))).astype(BF16)
                dq_p = dq_p + _nn(ds, kp) * (hm[j] * 0.125)
                dk_p = dk_p + _tn(ds, qs_m[2 * pr + j])
                dv_p = dv_p + _tn(p.astype(BF16), dmo_j)
            dqm_pairs.append(dq_p)
            dk_pairs.append(dk_p)
            dv_pairs.append(dv_p)
        dqm = jnp.concatenate(dqm_pairs, axis=1)
        dkv_ref[...] += jnp.concatenate(dk_pairs + dv_pairs, axis=1)

        drest_ref[...] = jnp.concatenate([dza, dub, dvb, dzb, dqm, dzm], axis=1).astype(BF16)

        @pl.when(last)
        def _():
            lane = lax.broadcasted_iota(jnp.int32, (1, LANES), 1)
            dbt = dbtab_scr[...]
            out = jnp.zeros((SGU_CHUNK, LANES), F32)
            for g in range(N_SGU_GROUPS):
                out = out + jnp.where(lane == g, jnp.sum(dbt * gm[g], axis=1, keepdims=True), 0.0)
            dbs_ref[...] = out

    tile = lambda w, cb: pl.BlockSpec((tm, w), lambda b, t, cb=cb: (b * nt + t, cb))
    const = lambda shape: pl.BlockSpec(shape, lambda b, t, n=len(shape): (0,) * n)
    return pl.pallas_call(
        body, grid=(B, nt),
        in_specs=[tile(D_MODEL, 0), tile(D_MODEL, 0), tile(ATTN_WIDTH, 0),
                  tile(ATTN_WIDTH, 3),
                  tile(SGU_WIDTH, 8), tile(SGU_WIDTH, 9), tile(SGU_WIDTH, 10),
                  tile(MEM_WIDTH, 11), tile(MEM_WIDTH, 12),
                  pl.BlockSpec((N_MEM, 2 * MEM_WIDTH), lambda b, t: (b, 0)),
                  const((N_SGU_GROUPS, SGU_CHUNK, SGU_CHUNK)), const((N_SGU_GROUPS, SGU_CHUNK, SGU_CHUNK)),
                  const((SGU_CHUNK, SGU_WIDTH)), const((1, SGU_WIDTH)),
                  const((D_MODEL, D_MODEL)), const((1, D_MODEL))],
        out_specs=[tile(D_MODEL, 0), tile(ATTN_WIDTH, 0), tile(REST_COLS, 0),
                   const((8, LANES)), const((D_MODEL, D_MODEL)),
                   const((N_SGU_GROUPS, SGU_CHUNK, SGU_CHUNK)), const((SGU_CHUNK, LANES)),
                   const((1, SGU_WIDTH)), const((1, D_MODEL)),
                   pl.BlockSpec((N_MEM, 2 * MEM_WIDTH), lambda b, t: (b, 0))],
        out_shape=[jax.ShapeDtypeStruct((T, D_MODEL), F32), jax.ShapeDtypeStruct((T, ATTN_WIDTH), F32),
                   jax.ShapeDtypeStruct((T, REST_COLS), BF16),
                   jax.ShapeDtypeStruct((8, LANES), F32), jax.ShapeDtypeStruct((D_MODEL, D_MODEL), F32),
                   jax.ShapeDtypeStruct((N_SGU_GROUPS, SGU_CHUNK, SGU_CHUNK), F32),
                   jax.ShapeDtypeStruct((SGU_CHUNK, LANES), F32),
                   jax.ShapeDtypeStruct((1, SGU_WIDTH), F32), jax.ShapeDtypeStruct((1, D_MODEL), F32),
                   jax.ShapeDtypeStruct((B * N_MEM, 2 * MEM_WIDTH), F32)],
        scratch_shapes=[pltpu.VMEM((SGU_CHUNK, SGU_WIDTH), F32)],
        compiler_params=_params(("arbitrary", "arbitrary")), name="mid")(
            x2d, t2d, a, proj, proj, proj, proj, proj, proj, kv, w_s, w_sT, b_tab, g_v, w_out, g_final)


def _inproj_bwd_dx(dq, dk, dv, drest, x2d, dx2, g_norm, w_in):
    T = x2d.shape[0]
    tm = 512
    W = ATTN_WIDTH

    def body(dq_ref, dk_ref, dv_ref, dr_ref, x_ref, dx2_ref, g_ref, w_ref, gx_ref, dg_ref):
        @pl.when(pl.program_id(0) == 0)
        def _():
            dg_ref[...] = jnp.zeros_like(dg_ref)

        dh = (_nt(dq_ref[...], w_ref[:, 0:W]) + _nt(dk_ref[...], w_ref[:, W:2 * W])
              + _nt(dv_ref[...], w_ref[:, 2 * W:3 * W]) + _nt(dr_ref[...], w_ref[:, QKV_COLS:IN_COLS]))
        r, xh = _rms(x_ref[...])
        dg_ref[...] += jnp.sum(dh * xh, axis=0, keepdims=True)
        th = dh * g_ref[...]
        gx_ref[...] = r * (th - xh * jnp.mean(th * xh, axis=1, keepdims=True)) + dx2_ref[...]

    tile = lambda w: pl.BlockSpec((tm, w), lambda i: (i, 0))
    return pl.pallas_call(
        body, grid=(T // tm,),
        in_specs=[tile(W), tile(W), tile(W), tile(REST_COLS), tile(D_MODEL), tile(D_MODEL),
                  pl.BlockSpec((1, D_MODEL), lambda i: (0, 0)),
                  pl.BlockSpec((D_MODEL, IN_COLS), lambda i: (0, 0))],
        out_specs=[tile(D_MODEL), pl.BlockSpec((1, D_MODEL), lambda i: (0, 0))],
        out_shape=[jax.ShapeDtypeStruct((T, D_MODEL), F32), jax.ShapeDtypeStruct((1, D_MODEL), F32)],
        compiler_params=_params(("arbitrary",)), name="inproj_bwd_dx")(dq, dk, dv, drest, x2d, dx2, g_norm, w_in)


def _inproj_bwd_dw(dq, dk, dv, drest, x2d, g_norm):
    T = x2d.shape[0]
    tm = 512
    W = ATTN_WIDTH

    def body(dq_ref, dk_ref, dv_ref, dr_ref, x_ref, g_ref, dw_ref):
        @pl.when(pl.program_id(0) == 0)
        def _():
            dw_ref[...] = jnp.zeros_like(dw_ref)

        _, xh = _rms(x_ref[...])
        h = (xh * g_ref[...]).astype(BF16)
        dw_ref[:, 0:W] += _tn(h, dq_ref[...])
        dw_ref[:, W:2 * W] += _tn(h, dk_ref[...])
        dw_ref[:, 2 * W:3 * W] += _tn(h, dv_ref[...])
        dw_ref[:, QKV_COLS:IN_COLS] += _tn(h, dr_ref[...])

    tile = lambda w: pl.BlockSpec((tm, w), lambda i: (i, 0))
    return pl.pallas_call(
        body, grid=(T // tm,),
        in_specs=[tile(W), tile(W), tile(W), tile(REST_COLS), tile(D_MODEL),
                  pl.BlockSpec((1, D_MODEL), lambda i: (0, 0))],
        out_specs=pl.BlockSpec((D_MODEL, IN_COLS), lambda i: (0, 0)),
        out_shape=jax.ShapeDtypeStruct((D_MODEL, IN_COLS), F32),
        compiler_params=_params(("arbitrary",)), name="inproj_bwd_dw")(dq, dk, dv, drest, x2d, g_norm)


def _adamw(w, g, m, v, name):
    R, C = w.shape
    br = 256 if R % 256 == 0 else R

    def body(w_ref, g_ref, m_ref, v_ref, d_ref, nm_ref, nv_ref):
        gg = g_ref[...]
        nm = ADAM_B1 * m_ref[...] + (1.0 - ADAM_B1) * gg
        nv = ADAM_B2 * v_ref[...] + (1.0 - ADAM_B2) * (gg * gg)
        m_hat = nm / (1.0 - ADAM_B1 ** ADAM_STEP)
        v_hat = nv / (1.0 - ADAM_B2 ** ADAM_STEP)
        d_ref[...] = -ADAM_LR * (m_hat / (jnp.sqrt(v_hat) + ADAM_EPS) + ADAM_WD * w_ref[...])
        nm_ref[...] = nm
        nv_ref[...] = nv

    spec = pl.BlockSpec((br, C), lambda i: (i, 0))
    return pl.pallas_call(
        body, grid=(R // br,), in_specs=[spec] * 4, out_specs=[spec] * 3,
        out_shape=[jax.ShapeDtypeStruct((R, C), F32)] * 3,
        compiler_params=_params(("arbitrary",)), name=name)(w, g, m, v)


def _place():
    x, y, c = lax.axis_index("x"), lax.axis_index("y"), lax.axis_index("c")
    chip = 2 * x + y
    peers = [(x, 1 - y), (1 - x, y), (1 - x, 1 - y)]
    peer_chip = [2 * px + py for px, py in peers]
    return x, y, c, chip, peers, peer_chip


def _remote(src, dst, send_sem, recv_sem, dev):
    return pltpu.make_async_remote_copy(src_ref=src, dst_ref=dst, send_sem=send_sem, recv_sem=recv_sem,
                                        device_id=dev, device_id_type=MESH)


def _ag_weights(w_in, w_kv, w_out):
    def body(win_ref, wkv_ref, wout_ref, oin_ref, okv_ref, oout_ref, s_ici, r_ici, s_d2d, r_d2d):
        x, y, c, chip, peers, peer_chip = _place()
        sib = (x, y, 1 - c)
        outs = [oin_ref, okv_ref, oout_ref]
        for src, out in zip([win_ref, wkv_ref, wout_ref], outs):
            out[chip] = src[...].astype(BF16)

        def half(out, k, cc):
            rows = out.shape[1] // 2
            return out.at[k, pl.ds(pl.multiple_of(cc * rows, 16), rows), :]

        sent = []
        for m, (px, py) in enumerate(peers):
            for w, out in enumerate(outs):
                cp = _remote(half(out, chip, c), half(out, chip, c), s_ici.at[3 * m + w], r_ici.at[3 * m + w],
                             (px, py, c))
                cp.start()
                sent.append(cp)
        for m, (px, py) in enumerate(peers):
            for w, out in enumerate(outs):
                blk = half(out, peer_chip[m], c)
                _remote(blk, blk, s_ici.at[3 * m + w], r_ici.at[3 * m + w], (px, py, c)).wait_recv()
                fw = _remote(blk, blk, s_d2d.at[3 * m + w], r_d2d.at[3 * m + w], sib)
                fw.start()
                sent.append(fw)
        for m in range(3):
            for w, out in enumerate(outs):
                blk = half(out, peer_chip[m], 1 - c)
                _remote(blk, blk, s_d2d.at[3 * m + w], r_d2d.at[3 * m + w], sib).wait_recv()
        for cp in sent:
            cp.wait_send()

    vmem = pl.BlockSpec(memory_space=pltpu.VMEM)
    return pl.pallas_call(
        body,
        out_shape=[jax.ShapeDtypeStruct((N_CHIPS,) + w.shape, BF16) for w in (w_in, w_kv, w_out)],
        in_specs=[vmem] * 3, out_specs=[vmem] * 3,
        scratch_shapes=[pltpu.SemaphoreType.DMA((9,))] * 4,
        compiler_params=pltpu.CompilerParams(vmem_limit_bytes=VMEM_LIMIT), name="ag_weights")(w_in, w_kv, w_out)


def _reduce_grads(g_in, g_kv, g_out, g_small):
    big_shapes = [g.shape for g in (g_in, g_kv, g_out)]
    half_small = SMALL_ROWS // 2
    row_block = 64

    def body(gin_ref, gkv_ref, gout_ref, gsm_ref, oin_ref, okv_ref, oout_ref, osm_ref,
             ra_in, ra_kv, ra_out, ra_sm, sb_in, sb_kv, sb_out, rb_in, rb_kv, rb_out, p_sm,
             sa_s, sa_r, sb_s, sb_r, sc_s, sc_r):
        x, y, c, chip, peers, peer_chip = _place()
        sib = (x, y, 1 - c)
        gs = [gin_ref, gkv_ref, gout_ref]
        ras = [ra_in, ra_kv, ra_out]
        sbs = [sb_in, sb_kv, sb_out]
        rbs = [rb_in, rb_kv, rb_out]
        outs = [oin_ref, okv_ref, oout_ref]
        halves = [s[1] // 2 for s in big_shapes]

        def rows_of(cc, n):
            return pl.ds(pl.multiple_of(cc * n, 8), n)

        a_copies = []
        for w in range(3):
            cp = _remote(gs[w].at[:, rows_of(1 - c, halves[w]), :], ras[w], sa_s.at[w], sa_r.at[w], sib)
            cp.start()
            a_copies.append(cp)
        cp = _remote(gsm_ref.at[rows_of(1 - c, half_small), :], ra_sm, sa_s.at[3], sa_r.at[3], sib)
        cp.start()
        a_copies.append(cp)
        for cp in a_copies:
            cp.wait_recv()

        b_copies = []
        for w in range(3):
            n = halves[w]
            for m in range(3):
                k = peer_chip[m]

                def chip_sum(i, carry, w=w, m=m, k=k, n=n):
                    r0 = pl.multiple_of(i * row_block, row_block)
                    mine = gs[w][k, pl.ds(pl.multiple_of(c * n + r0, 8), row_block), :]
                    sbs[w][m, pl.ds(r0, row_block), :] = (mine + ras[w][k, pl.ds(r0, row_block), :]).astype(BF16)
                    return carry
                lax.fori_loop(0, n // row_block, chip_sum, 0)
                cp = _remote(sbs[w].at[m], rbs[w].at[m], sb_s.at[3 * w + m], sb_r.at[3 * w + m],
                             (peers[m][0], peers[m][1], c))
                cp.start()
                b_copies.append(cp)
        p_sm[chip] = gsm_ref[rows_of(c, half_small), :] + ra_sm[...]
        for m in range(3):
            cp = _remote(p_sm.at[chip], p_sm.at[chip], sb_s.at[9 + m], sb_r.at[9 + m],
                         (peers[m][0], peers[m][1], c))
            cp.start()
            b_copies.append(cp)

        c_copies = []
        for w in range(3):
            n = halves[w]
            for m in range(3):
                _remote(sbs[w].at[m], rbs[w].at[m], sb_s.at[3 * w + m], sb_r.at[3 * w + m],
                        (peers[m][0], peers[m][1], c)).wait_recv()

            def total(i, carry, w=w, n=n):
                r0 = pl.multiple_of(i * row_block, row_block)
                rows = pl.ds(r0, row_block)
                dst = pl.ds(pl.multiple_of(c * n + r0, 8), row_block)
                acc = gs[w][chip, dst, :] + ras[w][chip, rows, :]
                for m in range(3):
                    acc = acc + rbs[w][m, rows, :].astype(F32)
                outs[w][dst, :] = acc
                return carry
            lax.fori_loop(0, n // row_block, total, 0)
            mine = outs[w].at[rows_of(c, n), :]
            cp = _remote(mine, mine, sc_s.at[w], sc_r.at[w], sib)
            cp.start()
            c_copies.append(cp)
        for m in range(3):
            _remote(p_sm.at[chip], p_sm.at[peer_chip[m]], sb_s.at[9 + m], sb_r.at[9 + m],
                    (peers[m][0], peers[m][1], c)).wait_recv()
        osm_ref[rows_of(c, half_small), :] = (p_sm[0] + p_sm[1]) + (p_sm[2] + p_sm[3])
        mine = osm_ref.at[rows_of(c, half_small), :]
        cp = _remote(mine, mine, sc_s.at[3], sc_r.at[3], sib)
        cp.start()
        c_copies.append(cp)
        for w in range(3):
            theirs = outs[w].at[rows_of(1 - c, halves[w]), :]
            _remote(theirs, theirs, sc_s.at[w], sc_r.at[w], sib).wait_recv()
        theirs = osm_ref.at[rows_of(1 - c, half_small), :]
        _remote(theirs, theirs, sc_s.at[3], sc_r.at[3], sib).wait_recv()
        for cp in a_copies + b_copies + c_copies:
            cp.wait_send()

    vmem = pl.BlockSpec(memory_space=pltpu.VMEM)
    scratch = ([pltpu.VMEM((N_CHIPS, s[1] // 2, s[2]), F32) for s in big_shapes]
               + [pltpu.VMEM((half_small, LANES), F32)]
               + [pltpu.VMEM((3, s[1] // 2, s[2]), BF16) for s in big_shapes]
               + [pltpu.VMEM((3, s[1] // 2, s[2]), BF16) for s in big_shapes]
               + [pltpu.VMEM((N_CHIPS, half_small, LANES), F32)]
               + [pltpu.SemaphoreType.DMA((4,)), pltpu.SemaphoreType.DMA((4,)),
                  pltpu.SemaphoreType.DMA((12,)), pltpu.SemaphoreType.DMA((12,)),
                  pltpu.SemaphoreType.DMA((4,)), pltpu.SemaphoreType.DMA((4,))])
    return pl.pallas_call(
        body,
        out_shape=[jax.ShapeDtypeStruct(s[1:], F32) for s in big_shapes]
        + [jax.ShapeDtypeStruct((SMALL_ROWS, LANES), F32)],
        in_specs=[vmem] * 4, out_specs=[vmem] * 4, scratch_shapes=scratch,
        compiler_params=pltpu.CompilerParams(vmem_limit_bytes=58 * 1024 * 1024),
        name="reduce_grads")(g_in, g_kv, g_out, g_small)


_SMALL_PARTS = (("g_norm", 8, 8), ("w_s", 512, 512), ("b_s", 4, 8), ("g_v", 2, 8), ("g_mem", 8, 8),
                ("g_final", 8, 8))


def _pack_small(parts):
    rows = []
    for (name, used, padded), p in zip(_SMALL_PARTS, parts):
        p = p.reshape(used, LANES)
        if padded > used:
            p = jnp.pad(p, ((0, padded - used), (0, 0)))
        rows.append(p)
    total = sum(p for _, _, p in _SMALL_PARTS)
    rows.append(jnp.zeros((SMALL_ROWS - total, LANES), F32))
    return jnp.concatenate(rows, axis=0)


def _unpack_small(packed, shapes):
    out = []
    off = 0
    for (name, used, padded), shape in zip(_SMALL_PARTS, shapes):
        out.append(packed[off:off + used].reshape(shape))
        off += padded
    return out


def _local_step(x, mem, target, g_norm, w_in, w_s, b_s, g_v, g_mem, w_kv, w_out, g_final):
    B, S, _ = x.shape
    x2d = x.reshape(B * S, D_MODEL)
    t2d = target.reshape(B * S, D_MODEL)
    mem2d = mem.reshape(B * N_MEM, D_MODEL)

    proj = _inproj_fwd(x2d, g_norm, w_in)
    kv = _kv_fwd(mem2d, g_mem, w_kv)
    a, lse = _attn_fwd(proj, B, S)
    w_sT = jnp.swapaxes(w_s, 1, 2)
    b_tab = jnp.repeat(b_s.T, HEAD_DIM, axis=1)
    (dx2, da, drest, loss, d_wout, d_ws, d_bs, d_gv, d_gf, dkv) = _mid(
        x2d, t2d, a, proj, kv, w_s, w_sT, b_tab, g_v, w_out, g_final, B, S)
    dq, dk, dv = _attn_bwd(proj, a, lse, da, B, S)
    grad_x, d_gnorm = _inproj_bwd_dx(dq, dk, dv, drest, x2d, dx2, g_norm, w_in)
    d_win = _inproj_bwd_dw(dq, dk, dv, drest, x2d, g_norm)
    d_wkv, d_gmem = _kv_bwd(mem2d, g_mem, w_kv, dkv)
    d_bs = d_bs[:, :N_SGU_GROUPS].T
    return (loss[0, 0], grad_x.reshape(B, S, D_MODEL),
            dict(g_norm=d_gnorm, w_in=d_win, w_s=d_ws, b_s=d_bs, g_v=d_gv, g_mem=d_gmem, w_kv=d_wkv,
                 w_out=d_wout, g_final=d_gf))


def kernel(x, mem, g_norm, w_in, w_sgu_spatial, b_sgu_spatial, g_sgu_v, g_mem, w_mem_kv, w_out, g_final, loss_target, m_g_norm, m_w_in, m_w_sgu_spatial, m_b_sgu_spatial, m_g_sgu_v, m_g_mem, m_w_mem_kv, m_w_out, m_g_final, v_g_norm, v_w_in, v_w_sgu_spatial, v_b_sgu_spatial, v_g_sgu_v, v_g_mem, v_w_mem_kv, v_w_out, v_g_final):
    win_all, wkv_all, wout_all = _ag_weights(w_in[0], w_mem_kv[0], w_out[0])
    rows_in, cols_in = w_in.shape[1], w_in.shape[2]
    w_in_full = jnp.transpose(win_all, (1, 0, 2)).reshape(rows_in, N_CHIPS * cols_in)
    w_kv_full = wkv_all.reshape(-1, wkv_all.shape[-1])
    w_out_full = wout_all.reshape(-1, wout_all.shape[-1])

    loss, grad_x, g = _local_step(
        x, mem, loss_target, g_norm, w_in_full, w_sgu_spatial[0], b_sgu_spatial[0], g_sgu_v, g_mem,
        w_kv_full, w_out_full, g_final.reshape(1, D_MODEL))
    loss = lax.psum(loss, ("x", "y", "c"))

    small_names = ("g_norm", "w_s", "b_s", "g_v", "g_mem", "g_final")
    g_in_stack = jnp.transpose(g["w_in"].reshape(rows_in, N_CHIPS, cols_in), (1, 0, 2))
    g_kv_stack = g["w_kv"].reshape((N_CHIPS,) + w_mem_kv.shape[1:])
    g_out_stack = g["w_out"].reshape((N_CHIPS,) + w_out.shape[1:])
    gr_in, gr_kv, gr_out, gr_small = _reduce_grads(
        g_in_stack, g_kv_stack, g_out_stack, _pack_small([g[n] for n in small_names]))

    small_w = (g_norm, w_sgu_spatial, b_sgu_spatial, g_sgu_v, g_mem, g_final)
    small_m = (m_g_norm, m_w_sgu_spatial, m_b_sgu_spatial, m_g_sgu_v, m_g_mem, m_g_final)
    small_v = (v_g_norm, v_w_sgu_spatial, v_b_sgu_spatial, v_g_sgu_v, v_g_mem, v_g_final)
    shapes = [w.shape for w in small_w]
    d_small, nm_small, nv_small = _adamw(_pack_small(small_w), gr_small, _pack_small(small_m),
                                         _pack_small(small_v), "adamw_small")
    d_in, nm_in, nv_in = _adamw(w_in[0], gr_in, m_w_in[0], v_w_in[0], "adamw_w_in")
    d_kv, nm_kv, nv_kv = _adamw(w_mem_kv[0], gr_kv, m_w_mem_kv[0], v_w_mem_kv[0], "adamw_w_kv")
    d_out, nm_out, nv_out = _adamw(w_out[0], gr_out, m_w_out[0], v_w_out[0], "adamw_w_out")

    def leaves(packed, big_in, big_kv, big_out):
        s_norm, s_ws, s_bs, s_gv, s_gmem, s_gf = _unpack_small(packed, shapes)
        return [s_norm, big_in[None], s_ws, s_bs, s_gv, s_gmem, big_kv[None], big_out[None], s_gf]

    return (loss, grad_x, *leaves(gr_small, gr_in, gr_kv, gr_out), *leaves(d_small, d_in, d_kv, d_out),
            *leaves(nm_small, nm_in, nm_kv, nm_out), *leaves(nv_small, nv_in, nv_kv, nv_out))
```

```python
import functools

import jax
import jax.numpy as jnp
from jax import lax
from jax.experimental import pallas as pl
from jax.experimental.pallas import tpu as pltpu

F32 = jnp.float32
BF16 = jnp.bfloat16
MESH = pl.DeviceIdType.MESH

D_MODEL = 1024
ATTN_WIDTH = 512
SGU_WIDTH = 256
MEM_WIDTH = 256
N_MEM = 256
IN_COLS = 3328
QKV_COLS = 3 * ATTN_WIDTH
REST_COLS = IN_COLS - QKV_COLS
SGU_CHUNK = 128
N_SGU_GROUPS = 4
EPS = 1e-6
NEG_INF = -1e30
DILATIONS = (1, 4, 16)
RADIUS = 64
Q_BLOCK = 128
LANES = 128
HEAD_DIM = 64

ADAM_LR = 0.001
ADAM_B1 = 0.9
ADAM_B2 = 0.999
ADAM_EPS = 1e-08
ADAM_WD = 0.01
ADAM_STEP = 10

N_CHIPS = 4
VMEM_LIMIT = 56 * 1024 * 1024
SMALL_ROWS = 560


def _params(sem=None, vmem=VMEM_LIMIT):
    return pltpu.CompilerParams(dimension_semantics=sem, vmem_limit_bytes=vmem)


def _nn(a, b):
    return jnp.dot(a, b, preferred_element_type=F32)


def _nt(a, b):
    return lax.dot_general(a, b, (((1,), (1,)), ((), ())), preferred_element_type=F32)


def _tn(a, b):
    return lax.dot_general(a, b, (((0,), (0,)), ((), ())), preferred_element_type=F32)


def _rms(x):
    r = lax.rsqrt(jnp.mean(x * x, axis=-1, keepdims=True) + EPS)
    return r, x * r


def _head_masks():
    lane = lax.broadcasted_iota(jnp.int32, (1, LANES), 1)
    lo = lane < HEAD_DIM
    return lo, (lo.astype(F32), (~lo).astype(F32))


def _silu_parts(z):
    s = jax.nn.sigmoid(z)
    return z * s, s * (1.0 + z * (1.0 - s))


def _gelu_parts(x):
    c = 0.7978845608028654
    x2 = x * x
    t = jnp.tanh(c * (x + 0.044715 * (x * x2)))
    val = 0.5 * x * (1.0 + t)
    grad = 0.5 * (1.0 + t) + 0.5 * x * (1.0 - t * t) * (c * (1.0 + 3.0 * 0.044715 * x2))
    return val, grad


def _inproj_fwd(x2d, g_norm, w_in_t):
    T = x2d.shape[0]
    tm = 512

    def body(x_ref, g_ref, w_ref, o_ref):
        _, xh = _rms(x_ref[...])
        h = (xh * g_ref[...]).astype(BF16)
        o_ref[...] = _nt(h, w_ref[...])

    return pl.pallas_call(
        body, grid=(T // tm,),
        in_specs=[pl.BlockSpec((tm, D_MODEL), lambda i: (i, 0)),
                  pl.BlockSpec((1, D_MODEL), lambda i: (0, 0)),
                  pl.BlockSpec((IN_COLS, D_MODEL), lambda i: (0, 0))],
        out_specs=pl.BlockSpec((tm, IN_COLS), lambda i: (i, 0)),
        out_shape=jax.ShapeDtypeStruct((T, IN_COLS), F32),
        compiler_params=_params(("arbitrary",)), name="inproj_fwd")(x2d, g_norm, w_in_t)


def _kv_fwd(mem2d, g_mem, w_kv):
    Tm = mem2d.shape[0]

    def body(m_ref, g_ref, w_ref, o_ref):
        _, mh = _rms(m_ref[...])
        o_ref[...] = _nn((mh * g_ref[...]).astype(BF16), w_ref[...])

    return pl.pallas_call(
        body, out_shape=jax.ShapeDtypeStruct((Tm, 2 * MEM_WIDTH), F32),
        compiler_params=_params(), name="kv_fwd")(mem2d, g_mem, w_kv)


def _kv_bwd(mem2d, g_mem, w_kv, dkv):
    Tm = mem2d.shape[0]

    def body(m_ref, g_ref, w_ref, dkv_ref, dw_ref, dg_ref):
        _, mh = _rms(m_ref[...])
        memn = (mh * g_ref[...]).astype(BF16)
        dkvb = dkv_ref[...].astype(BF16)
        dw_ref[...] = _tn(memn, dkvb)
        dmemn = _nt(dkvb, w_ref[...])
        dg_ref[...] = jnp.sum(dmemn * mh, axis=0, keepdims=True)

    return pl.pallas_call(
        body, out_shape=(jax.ShapeDtypeStruct((D_MODEL, 2 * MEM_WIDTH), F32),
                         jax.ShapeDtypeStruct((1, D_MODEL), F32)),
        compiler_params=_params(), name="kv_bwd")(mem2d, g_mem, w_kv, dkv)


def _attn_geometry(S):
    geom = []
    for d in DILATIONS:
        L = S // d
        assert L % Q_BLOCK == 0
        geom.append((d, L, min(2 * Q_BLOCK, L), L // Q_BLOCK))
    return geom


def _init_bias(bias_scr, geom, hp):
    row = lax.broadcasted_iota(jnp.int32, (Q_BLOCK, 2 * Q_BLOCK), 0)
    col = lax.broadcasted_iota(jnp.int32, (Q_BLOCK, 2 * Q_BLOCK), 1)
    for j in (0, 1):
        bits = (126 - (2 * hp + j)) * (1 << 23)
        slope = lax.bitcast_convert_type(jnp.full((1, 1), bits, jnp.int32), F32)
        for di, (d, _, _, _) in enumerate(geom):
            for cls, off in enumerate((0, -RADIUS, -2 * RADIUS)):
                dist = jnp.abs(col - row + off)
                bias_scr[di * 6 + cls * 2 + j] = jnp.where(
                    dist <= RADIUS, -(slope * float(d)) * dist.astype(F32), NEG_INF)


def _block_slices(d, L, KW, nqb, r, qb):
    qs = qb * Q_BLOCK
    ks = jnp.clip(qs - RADIUS, 0, L - KW)
    cls = jnp.where(qb == 0, 0, jnp.where(qb == nqb - 1, 2, 1))
    if d == 1:
        qsl = pl.ds(pl.multiple_of(qs, Q_BLOCK), Q_BLOCK)
        ksl = pl.ds(pl.multiple_of(ks, RADIUS), KW)
    else:
        qsl = pl.ds(r + qs * d, Q_BLOCK, stride=d)
        ksl = pl.ds(r + ks * d, KW, stride=d)
    return qsl, ksl, cls


def _for_groups(geom, group, fn):
    for di, (d, L, KW, nqb) in enumerate(geom):
        assert (d * nqb) % group == 0

        def step(it, carry, di=di, d=d, L=L, KW=KW, nqb=nqb):
            slices = []
            for g in range(group):
                i = it * group + g
                slices.append(_block_slices(d, L, KW, nqb, i // nqb, i % nqb))
            fn(di, KW, slices)
            return carry
        lax.fori_loop(0, d * nqb // group, step, 0)


def _attn_fwd(proj, B, S):
    T = B * S
    geom = _attn_geometry(S)
    n_pairs = ATTN_WIDTH // LANES

    def body(q_ref, k_ref, v_ref, a_ref, lse_ref, bias_scr, *per_dilation):
        o_scr, m_scr, l_scr = per_dilation[0:3], per_dilation[3:6], per_dilation[6:9]
        lo, hm = _head_masks()
        _init_bias(bias_scr, geom, pl.program_id(1))

        def group(di, KW, slices):
            chains = [(g, j) for g in range(len(slices)) for j in (0, 1)]
            q = [q_ref[qsl, :] for qsl, _, _ in slices]
            kw = [k_ref[ksl, :].astype(BF16) for _, ksl, _ in slices]
            vw = [v_ref[ksl, :].astype(BF16) for _, ksl, _ in slices]
            s = {(g, j): _nt((q[g] * (hm[j] * 0.125)).astype(BF16), kw[g])
                 + bias_scr[di * 6 + slices[g][2] * 2 + j, :, pl.ds(0, KW)] for g, j in chains}
            m = {c: jnp.max(s[c], axis=1, keepdims=True) for c in chains}
            p = {c: jnp.exp(s[c] - m[c]) for c in chains}
            l = {c: jnp.sum(p[c], axis=1, keepdims=True) for c in chains}
            o = {(g, j): _nn(p[(g, j)].astype(BF16), vw[g]) for g, j in chains}
            for g, (qsl, _, _) in enumerate(slices):
                o_scr[di][qsl, :] = jnp.where(lo, o[(g, 0)], o[(g, 1)])
                m_scr[di][qsl, :] = jnp.where(lo, m[(g, 0)], m[(g, 1)])
                l_scr[di][qsl, :] = jnp.where(lo, l[(g, 0)], l[(g, 1)])

        _for_groups(geom, 8, group)

        rows_per = 256

        def combine(i, carry):
            rows = pl.ds(pl.multiple_of(i * rows_per, rows_per), rows_per)
            ms = [m_scr[di][rows, :] for di in range(3)]
            mx = jnp.maximum(jnp.maximum(ms[0], ms[1]), ms[2])
            num = 0.0
            den = 0.0
            for di in range(3):
                w = jnp.exp(ms[di] - mx)
                num = num + w * o_scr[di][rows, :]
                den = den + w * l_scr[di][rows, :]
            a_ref[rows, :] = num / den
            lse_ref[rows, :] = mx + jnp.log(den)
            return carry

        lax.fori_loop(0, S // rows_per, combine, 0)

    blk = lambda off: pl.BlockSpec((S, LANES), lambda b, h, off=off: (b, off + h))
    out_blk = pl.BlockSpec((S, LANES), lambda b, h: (b, h))
    return pl.pallas_call(
        body, grid=(B, n_pairs),
        in_specs=[blk(0), blk(n_pairs), blk(2 * n_pairs)],
        out_specs=[out_blk, out_blk],
        out_shape=[jax.ShapeDtypeStruct((T, ATTN_WIDTH), F32)] * 2,
        scratch_shapes=[pltpu.VMEM((18, Q_BLOCK, 2 * Q_BLOCK), F32)] + [pltpu.VMEM((S, LANES), F32)] * 9,
        compiler_params=_params(("arbitrary", "arbitrary")), name="attn_fwd")(proj, proj, proj)


def _attn_bwd(proj, a, lse, da, B, S):
    T = B * S
    geom = _attn_geometry(S)
    n_pairs = ATTN_WIDTH // LANES

    def body(q_ref, k_ref, v_ref, a_ref, lse_ref, do_ref, dq_ref, dk_ref, dv_ref,
             bias_scr, dq_scr, dk_scr, dv_scr):
        _, hm = _head_masks()
        _init_bias(bias_scr, geom, pl.program_id(1))
        dq_scr[...] = jnp.zeros_like(dq_scr)
        dk_scr[...] = jnp.zeros_like(dk_scr)
        dv_scr[...] = jnp.zeros_like(dv_scr)

        def group(di, KW, slices):
            n = len(slices)
            chains = [(g, j) for g in range(n) for j in (0, 1)]
            q = [q_ref[qsl, :] for qsl, _, _ in slices]
            do = [do_ref[qsl, :] for qsl, _, _ in slices]
            doa = [do[g] * a_ref[slices[g][0], :] for g in range(n)]
            lse_q = [lse_ref[qsl, :] for qsl, _, _ in slices]
            kw = [k_ref[ksl, :].astype(BF16) for _, ksl, _ in slices]
            vw = [v_ref[ksl, :].astype(BF16) for _, ksl, _ in slices]
            qj = {(g, j): (q[g] * (hm[j] * 0.125)).astype(BF16) for g, j in chains}
            doj = {(g, j): (do[g] * hm[j]).astype(BF16) for g, j in chains}
            s = {(g, j): _nt(qj[(g, j)], kw[g])
                 + bias_scr[di * 6 + slices[g][2] * 2 + j, :, pl.ds(0, KW)] for g, j in chains}
            dp = {(g, j): _nt(doj[(g, j)], vw[g]) for g, j in chains}
            dsum = {(g, j): jnp.sum(doa[g] * hm[j], axis=1, keepdims=True) for g, j in chains}
            p = {(g, j): jnp.exp(s[(g, j)] - lse_q[g][:, HEAD_DIM * j:HEAD_DIM * j + 1]) for g, j in chains}
            ds = {c: (p[c] * (dp[c] - dsum[c])).astype(BF16) for c in chains}
            pb = {c: p[c].astype(BF16) for c in chains}
            dq = [_nn(ds[(g, 0)], kw[g]) * (hm[0] * 0.125) + _nn(ds[(g, 1)], kw[g]) * (hm[1] * 0.125)
                  for g in range(n)]
            both = lambda t, g: jnp.concatenate([t[(g, 0)], t[(g, 1)]], axis=0)
            dkw = [_tn(both(ds, g), both(qj, g)) for g in range(n)]
            dvw = [_tn(both(pb, g), both(doj, g)) for g in range(n)]
            for g, (qsl, ksl, _) in enumerate(slices):
                dq_scr[qsl, :] = dq_scr[qsl, :] + dq[g]
                dk_scr[ksl, :] = dk_scr[ksl, :] + dkw[g]
                dv_scr[ksl, :] = dv_scr[ksl, :] + dvw[g]

        _for_groups(geom, 4, group)
        dq_ref[...] = dq_scr[...].astype(BF16)
        dk_ref[...] = dk_scr[...].astype(BF16)
        dv_ref[...] = dv_scr[...].astype(BF16)

    blk = lambda off: pl.BlockSpec((S, LANES), lambda b, h, off=off: (b, off + h))
    return pl.pallas_call(
        body, grid=(B, n_pairs),
        in_specs=[blk(0), blk(n_pairs), blk(2 * n_pairs), blk(0), blk(0), blk(0)],
        out_specs=[blk(0), blk(0), blk(0)],
        out_shape=[jax.ShapeDtypeStruct((T, ATTN_WIDTH), BF16)] * 3,
        scratch_shapes=[pltpu.VMEM((18, Q_BLOCK, 2 * Q_BLOCK), F32),
                        pltpu.VMEM((S, LANES), F32),
                        pltpu.VMEM((S, LANES), F32),
                        pltpu.VMEM((S, LANES), F32)],
        compiler_params=_params(("arbitrary", "arbitrary")), name="attn_bwd")(proj, proj, proj, a, lse, da)


def _mid(x2d, t2d, a, proj, kv, w_s, w_sT, b_tab, g_v, w_out, g_final, B, S):
    T = B * S
    tm = 512
    nt = S // tm
    n_chunks = tm // SGU_CHUNK

    def body(x_ref, t_ref, a_ref, za_ref, ub_ref, vb_ref, zb_ref, qm_ref, zm_ref, kv_ref,
             ws_ref, wsT_ref, btab_ref, gv_ref, wout_ref, gf_ref,
             dx2_ref, da_ref, drest_ref, loss_ref, dwout_ref, dws_ref, dbs_ref, dgv_ref, dgf_ref, dkv_ref,
             dbtab_scr):
        b = pl.program_id(0)
        t = pl.program_id(1)
        first = jnp.logical_and(b == 0, t == 0)
        last = jnp.logical_and(b == B - 1, t == nt - 1)
        _, hm = _head_masks()
        lane_g = lax.broadcasted_iota(jnp.int32, (1, SGU_WIDTH), 1) // HEAD_DIM
        gm = [(lane_g == g).astype(F32) for g in range(N_SGU_GROUPS)]

        @pl.when(first)
        def _():
            loss_ref[...] = jnp.zeros_like(loss_ref)
            dwout_ref[...] = jnp.zeros_like(dwout_ref)
            dws_ref[...] = jnp.zeros_like(dws_ref)
            dbs_ref[...] = jnp.zeros_like(dbs_ref)
            dgv_ref[...] = jnp.zeros_like(dgv_ref)
            dgf_ref[...] = jnp.zeros_like(dgf_ref)
            dbtab_scr[...] = jnp.zeros_like(dbtab_scr)

        @pl.when(t == 0)
        def _():
            dkv_ref[...] = jnp.zeros_like(dkv_ref)

        sil_a, dsil_a = _silu_parts(za_ref[...])
        a_val = a_ref[...]
        gated_a = sil_a * a_val

        u, du_dub = _gelu_parts(ub_ref[...])
        vv, dvv_dvb = _gelu_parts(vb_ref[...])
        rv, vhat = _rms(vv)
        gv = gv_ref[...]
        vn = (vhat * gv).astype(BF16)
        ws = [ws_ref[g].astype(BF16) for g in range(N_SGU_GROUPS)]
        wsT = [wsT_ref[g].astype(BF16) for g in range(N_SGU_GROUPS)]
        mixed = []
        for ci in range(n_chunks):
            vn_c = vn[ci * SGU_CHUNK:(ci + 1) * SGU_CHUNK, :]
            mc = btab_ref[...]
            for g in range(N_SGU_GROUPS):
                mc = mc + gm[g] * _nn(ws[g], vn_c)
            mixed.append(mc)
        mixed = jnp.concatenate(mixed, axis=0)
        sg = u * mixed
        sil_b, dsil_b = _silu_parts(zb_ref[...])
        gated_b = sil_b * sg

        kvv = kv_ref[...].astype(BF16)
        qm = qm_ref[...]
        probs, qs_m, mo_pairs = [], [], []
        for pr in range(2):
            cols = slice(pr * LANES, (pr + 1) * LANES)
            kp = kvv[:, pr * LANES:(pr + 1) * LANES]
            vp = kvv[:, MEM_WIDTH + pr * LANES:MEM_WIDTH + (pr + 1) * LANES]
            mo_p = 0.0
            for j in (0, 1):
                qj = (qm[:, cols] * (hm[j] * 0.125)).astype(BF16)
                s = _nt(qj, kp)
                e = jnp.exp(s - jnp.max(s, axis=1, keepdims=True))
                p = e / jnp.sum(e, axis=1, keepdims=True)
                mo_p = mo_p + _nn(p.astype(BF16), vp) * hm[j]
                probs.append(p)
                qs_m.append(qj)
            mo_pairs.append(mo_p)
        mo = jnp.concatenate(mo_pairs, axis=1)
        sil_m, dsil_m = _silu_parts(zm_ref[...])
        gated_m = sil_m * mo

        gated = jnp.concatenate([gated_a, gated_b, gated_m], axis=1).astype(BF16)
        wout = wout_ref[...]
        x2 = x_ref[...] + _nn(gated, wout)
        r2, xh2 = _rms(x2)
        gf = gf_ref[...]
        err = xh2 * gf - t_ref[...]
        loss_ref[...] += jnp.sum(err * err) * (0.5 / D_MODEL)

        dy = err * (1.0 / D_MODEL)
        dgf_ref[...] += jnp.sum(dy * xh2, axis=0, keepdims=True)
        gdy = dy * gf
        dx2 = r2 * (gdy - xh2 * jnp.mean(gdy * xh2, axis=1, keepdims=True))
        dx2_ref[...] = dx2
        dx2b = dx2.astype(BF16)
        dwout_ref[...] += _tn(gated, dx2b)
        dgated = _nt(dx2b, wout)
        dga = dgated[:, 0:ATTN_WIDTH]
        dgb = dgated[:, ATTN_WIDTH:ATTN_WIDTH + SGU_WIDTH]
        dgm = dgated[:, ATTN_WIDTH + SGU_WIDTH:]

        da_ref[...] = dga * sil_a
        dza = dga * a_val * dsil_a

        dsg = dgb * sil_b
        dzb = dgb * sg * dsil_b
        dub = dsg * mixed * du_dub
        dmixed = dsg * u
        dmixed_b = dmixed.astype(BF16)
        dvn = []
        dbtab = dbtab_scr[...]
        for ci in range(n_chunks):
            rows = slice(ci * SGU_CHUNK, (ci + 1) * SGU_CHUNK)
            dm_c = dmixed_b[rows, :]
            vn_c = vn[rows, :]
            dvn_c = 0.0
            for g in range(N_SGU_GROUPS):
                dvn_c = dvn_c + gm[g] * _nn(wsT[g], dm_c)
                dws_ref[g] += _nt((dmixed[rows, :] * gm[g]).astype(BF16), vn_c)
            dvn.append(dvn_c)
            dbtab = dbtab + dmixed[rows, :]
        dbtab_scr[...] = dbtab
        dvn = jnp.concatenate(dvn, axis=0)
        dgv_ref[...] += jnp.sum(dvn * vhat, axis=0, keepdims=True)
        tv = dvn * gv
        dvv = rv * (tv - vhat * jnp.mean(tv * vhat, axis=1, keepdims=True))
        dvb = dvv * dvv_dvb

        dmo = dgm * sil_m
        dzm = dgm * mo * dsil_m
        dqm_pairs = []
        dk_pairs, dv_pairs = [], []
        for pr in range(2):
            kp = kvv[:, pr * LANES:(pr + 1) * LANES]
            vp = kvv[:, MEM_WIDTH + pr * LANES:MEM_WIDTH + (pr + 1) * LANES]
            dmo_p = dmo[:, pr * LANES:(pr + 1) * LANES]
            dq_p = 0.0
            dk_p = 0.0
            dv_p = 0.0
            for j in (0, 1):
                p = probs[2 * pr + j]
                dmo_j = (dmo_p * hm[j]).astype(BF16)
                dp = _nt(dmo_j, vp)
                ds = (p * (dp - jnp.sum(dp * p, axis=1, keepdims=True))).astype(BF16)
                dq_p = dq_p + _nn(ds, kp) * (hm[j] * 0.125)
                dk_p = dk_p + _tn(ds, qs_m[2 * pr + j])
                dv_p = dv_p + _tn(p.astype(BF16), dmo_j)
            dqm_pairs.append(dq_p)
            dk_pairs.append(dk_p)
            dv_pairs.append(dv_p)
        dqm = jnp.concatenate(dqm_pairs, axis=1)
        dkv_ref[...] += jnp.concatenate(dk_pairs + dv_pairs, axis=1)

        drest_ref[...] = jnp.concatenate([dza, dub, dvb, dzb, dqm, dzm], axis=1).astype(BF16)

        @pl.when(last)
        def _():
            lane = lax.broadcasted_iota(jnp.int32, (1, LANES), 1)
            dbt = dbtab_scr[...]
            out = jnp.zeros((SGU_CHUNK, LANES), F32)
            for g in range(N_SGU_GROUPS):
                out = out + jnp.where(lane == g, jnp.sum(dbt * gm[g], axis=1, keepdims=True), 0.0)
            dbs_ref[...] = out

    tile = lambda w, cb: pl.BlockSpec((tm, w), lambda b, t, cb=cb: (b * nt + t, cb))
    const = lambda shape: pl.BlockSpec(shape, lambda b, t, n=len(shape): (0,) * n)
    return pl.pallas_call(
        body, grid=(B, nt),
        in_specs=[tile(D_MODEL, 0), tile(D_MODEL, 0), tile(ATTN_WIDTH, 0),
                  tile(ATTN_WIDTH, 3),
                  tile(SGU_WIDTH, 8), tile(SGU_WIDTH, 9), tile(SGU_WIDTH, 10),
                  tile(MEM_WIDTH, 11), tile(MEM_WIDTH, 12),
                  pl.BlockSpec((N_MEM, 2 * MEM_WIDTH), lambda b, t: (b, 0)),
                  const((N_SGU_GROUPS, SGU_CHUNK, SGU_CHUNK)), const((N_SGU_GROUPS, SGU_CHUNK, SGU_CHUNK)),
                  const((SGU_CHUNK, SGU_WIDTH)), const((1, SGU_WIDTH)),
                  const((D_MODEL, D_MODEL)), const((1, D_MODEL))],
        out_specs=[tile(D_MODEL, 0), tile(ATTN_WIDTH, 0), tile(REST_COLS, 0),
                   const((8, LANES)), const((D_MODEL, D_MODEL)),
                   const((N_SGU_GROUPS, SGU_CHUNK, SGU_CHUNK)), const((SGU_CHUNK, LANES)),
                   const((1, SGU_WIDTH)), const((1, D_MODEL)),
                   pl.BlockSpec((N_MEM, 2 * MEM_WIDTH), lambda b, t: (b, 0))],
        out_shape=[jax.ShapeDtypeStruct((T, D_MODEL), F32), jax.ShapeDtypeStruct((T, ATTN_WIDTH), F32),
                   jax.ShapeDtypeStruct((T, REST_COLS), BF16),
                   jax.ShapeDtypeStruct((8, LANES), F32), jax.ShapeDtypeStruct((D_MODEL, D_MODEL), F32),
                   jax.ShapeDtypeStruct((N_SGU_GROUPS, SGU_CHUNK, SGU_CHUNK), F32),
                   jax.ShapeDtypeStruct((SGU_CHUNK, LANES), F32),
                   jax.ShapeDtypeStruct((1, SGU_WIDTH), F32), jax.ShapeDtypeStruct((1, D_MODEL), F32),
                   jax.ShapeDtypeStruct((B * N_MEM, 2 * MEM_WIDTH), F32)],
        scratch_shapes=[pltpu.VMEM((SGU_CHUNK, SGU_WIDTH), F32)],
        compiler_params=_params(("arbitrary", "arbitrary")), name="mid")(
            x2d, t2d, a, proj, proj, proj, proj, proj, proj, kv, w_s, w_sT, b_tab, g_v, w_out, g_final)


def _inproj_bwd_dx(dq, dk, dv, drest, x2d, dx2, g_norm, w_in_t):
    T = x2d.shape[0]
    tm = 512
    W = ATTN_WIDTH

    def body(dq_ref, dk_ref, dv_ref, dr_ref, x_ref, dx2_ref, g_ref, w_ref, gx_ref, dg_ref):
        @pl.when(pl.program_id(0) == 0)
        def _():
            dg_ref[...] = jnp.zeros_like(dg_ref)

        dh = (_nn(dq_ref[...], w_ref[0:W, :]) + _nn(dk_ref[...], w_ref[W:2 * W, :])
              + _nn(dv_ref[...], w_ref[2 * W:3 * W, :]) + _nn(dr_ref[...], w_ref[QKV_COLS:IN_COLS, :]))
        r, xh = _rms(x_ref[...])
        dg_ref[...] += jnp.sum(dh * xh, axis=0, keepdims=True)
        th = dh * g_ref[...]
        gx_ref[...] = r * (th - xh * jnp.mean(th * xh, axis=1, keepdims=True)) + dx2_ref[...]

    tile = lambda w: pl.BlockSpec((tm, w), lambda i: (i, 0))
    return pl.pallas_call(
        body, grid=(T // tm,),
        in_specs=[tile(W), tile(W), tile(W), tile(REST_COLS), tile(D_MODEL), tile(D_MODEL),
                  pl.BlockSpec((1, D_MODEL), lambda i: (0, 0)),
                  pl.BlockSpec((IN_COLS, D_MODEL), lambda i: (0, 0))],
        out_specs=[tile(D_MODEL), pl.BlockSpec((1, D_MODEL), lambda i: (0, 0))],
        out_shape=[jax.ShapeDtypeStruct((T, D_MODEL), F32), jax.ShapeDtypeStruct((1, D_MODEL), F32)],
        compiler_params=_params(("arbitrary",)), name="inproj_bwd_dx")(dq, dk, dv, drest, x2d, dx2, g_norm, w_in_t)


def _inproj_bwd_dw(dq, dk, dv, drest, x2d, g_norm):
    T = x2d.shape[0]
    tm = 512
    W = ATTN_WIDTH

    def body(dq_ref, dk_ref, dv_ref, dr_ref, x_ref, g_ref, dw_ref):
        @pl.when(pl.program_id(0) == 0)
        def _():
            dw_ref[...] = jnp.zeros_like(dw_ref)

        _, xh = _rms(x_ref[...])
        h = (xh * g_ref[...]).astype(BF16)
        dw_ref[0:W, :] += _tn(dq_ref[...], h)
        dw_ref[W:2 * W, :] += _tn(dk_ref[...], h)
        dw_ref[2 * W:3 * W, :] += _tn(dv_ref[...], h)
        dw_ref[QKV_COLS:IN_COLS, :] += _tn(dr_ref[...], h)

    tile = lambda w: pl.BlockSpec((tm, w), lambda i: (i, 0))
    return pl.pallas_call(
        body, grid=(T // tm,),
        in_specs=[tile(W), tile(W), tile(W), tile(REST_COLS), tile(D_MODEL),
                  pl.BlockSpec((1, D_MODEL), lambda i: (0, 0))],
        out_specs=pl.BlockSpec((IN_COLS, D_MODEL), lambda i: (0, 0)),
        out_shape=jax.ShapeDtypeStruct((IN_COLS, D_MODEL), F32),
        compiler_params=_params(("arbitrary",)), name="inproj_bwd_dw")(dq, dk, dv, drest, x2d, g_norm)


def _adamw(w, g, m, v, name):
    R, C = w.shape
    br = max(r for r in range(8, 257, 8) if R % r == 0)

    def body(w_ref, g_ref, m_ref, v_ref, d_ref, nm_ref, nv_ref):
        gg = g_ref[...]
        nm = ADAM_B1 * m_ref[...] + (1.0 - ADAM_B1) * gg
        nv = ADAM_B2 * v_ref[...] + (1.0 - ADAM_B2) * (gg * gg)
        m_hat = nm / (1.0 - ADAM_B1 ** ADAM_STEP)
        v_hat = nv / (1.0 - ADAM_B2 ** ADAM_STEP)
        d_ref[...] = -ADAM_LR * (m_hat / (jnp.sqrt(v_hat) + ADAM_EPS) + ADAM_WD * w_ref[...])
        nm_ref[...] = nm
        nv_ref[...] = nv

    spec = pl.BlockSpec((br, C), lambda i: (i, 0))
    return pl.pallas_call(
        body, grid=(R // br,), in_specs=[spec] * 4, out_specs=[spec] * 3,
        out_shape=[jax.ShapeDtypeStruct((R, C), F32)] * 3,
        compiler_params=_params(("arbitrary",)), name=name)(w, g, m, v)


def _place():
    x, y, c = lax.axis_index("x"), lax.axis_index("y"), lax.axis_index("c")
    chip = 2 * x + y
    peers = [(x, 1 - y), (1 - x, y), (1 - x, 1 - y)]
    peer_chip = [2 * px + py for px, py in peers]
    return x, y, c, chip, peers, peer_chip


def _remote(src, dst, send_sem, recv_sem, dev):
    return pltpu.make_async_remote_copy(src_ref=src, dst_ref=dst, send_sem=send_sem, recv_sem=recv_sem,
                                        device_id=dev, device_id_type=MESH)


def _ag_weights(w_in, w_kv, w_out):
    def body(win_ref, wkv_ref, wout_ref, oin_ref, okv_ref, oout_ref, s_ici, r_ici, s_d2d, r_d2d):
        x, y, c, chip, peers, peer_chip = _place()
        sib = (x, y, 1 - c)
        outs = [oin_ref, okv_ref, oout_ref]
        for src, out in zip([win_ref, wkv_ref, wout_ref], outs):
            out[chip] = src[...].astype(BF16)

        def half(out, k, cc):
            rows = out.shape[1] // 2
            return out.at[k, pl.ds(pl.multiple_of(cc * rows, 16), rows), :]

        sent = []
        for m, (px, py) in enumerate(peers):
            for w, out in enumerate(outs):
                cp = _remote(half(out, chip, c), half(out, chip, c), s_ici.at[3 * m + w], r_ici.at[3 * m + w],
                             (px, py, c))
                cp.start()
                sent.append(cp)
        for m, (px, py) in enumerate(peers):
            for w, out in enumerate(outs):
                blk = half(out, peer_chip[m], c)
                _remote(blk, blk, s_ici.at[3 * m + w], r_ici.at[3 * m + w], (px, py, c)).wait_recv()
                fw = _remote(blk, blk, s_d2d.at[3 * m + w], r_d2d.at[3 * m + w], sib)
                fw.start()
                sent.append(fw)
        for m in range(3):
            for w, out in enumerate(outs):
                blk = half(out, peer_chip[m], 1 - c)
                _remote(blk, blk, s_d2d.at[3 * m + w], r_d2d.at[3 * m + w], sib).wait_recv()
        for cp in sent:
            cp.wait_send()

    vmem = pl.BlockSpec(memory_space=pltpu.VMEM)
    return pl.pallas_call(
        body,
        out_shape=[jax.ShapeDtypeStruct((N_CHIPS,) + w.shape, BF16) for w in (w_in, w_kv, w_out)],
        in_specs=[vmem] * 3, out_specs=[vmem] * 3,
        scratch_shapes=[pltpu.SemaphoreType.DMA((9,))] * 4,
        compiler_params=pltpu.CompilerParams(vmem_limit_bytes=VMEM_LIMIT), name="ag_weights")(w_in, w_kv, w_out)


def _reduce_grads(g_in, g_kv, g_out, g_small):
    big_shapes = [g.shape for g in (g_in, g_kv, g_out)]
    half_small = SMALL_ROWS // 2
    row_block = 32

    def body(gin_ref, gkv_ref, gout_ref, gsm_ref, oin_ref, okv_ref, oout_ref, osm_ref,
             ra_in, ra_kv, ra_out, ra_sm, sb_in, sb_kv, sb_out, rb_in, rb_kv, rb_out, p_sm,
             sa_s, sa_r, sb_s, sb_r, sc_s, sc_r):
        x, y, c, chip, peers, peer_chip = _place()
        sib = (x, y, 1 - c)
        gs = [gin_ref, gkv_ref, gout_ref]
        ras = [ra_in, ra_kv, ra_out]
        sbs = [sb_in, sb_kv, sb_out]
        rbs = [rb_in, rb_kv, rb_out]
        outs = [oin_ref, okv_ref, oout_ref]
        halves = [s[1] // 2 for s in big_shapes]

        def rows_of(cc, n):
            return pl.ds(pl.multiple_of(cc * n, 8), n)

        a_copies = []
        for w in range(3):
            cp = _remote(gs[w].at[:, rows_of(1 - c, halves[w]), :], ras[w], sa_s.at[w], sa_r.at[w], sib)
            cp.start()
            a_copies.append(cp)
        cp = _remote(gsm_ref.at[rows_of(1 - c, half_small), :], ra_sm, sa_s.at[3], sa_r.at[3], sib)
        cp.start()
        a_copies.append(cp)
        for cp in a_copies:
            cp.wait_recv()

        b_copies = []
        for w in range(3):
            n = halves[w]
            for m in range(3):
                k = peer_chip[m]

                def chip_sum(i, carry, w=w, m=m, k=k, n=n):
                    r0 = pl.multiple_of(i * row_block, row_block)
                    mine = gs[w][k, pl.ds(pl.multiple_of(c * n + r0, 8), row_block), :]
                    sbs[w][m, pl.ds(r0, row_block), :] = (mine + ras[w][k, pl.ds(r0, row_block), :]).astype(BF16)
                    return carry
                lax.fori_loop(0, n // row_block, chip_sum, 0)
                cp = _remote(sbs[w].at[m], rbs[w].at[m], sb_s.at[3 * w + m], sb_r.at[3 * w + m],
                             (peers[m][0], peers[m][1], c))
                cp.start()
                b_copies.append(cp)
        p_sm[chip] = gsm_ref[rows_of(c, half_small), :] + ra_sm[...]
        for m in range(3):
            cp = _remote(p_sm.at[chip], p_sm.at[chip], sb_s.at[9 + m], sb_r.at[9 + m],
                         (peers[m][0], peers[m][1], c))
            cp.start()
            b_copies.append(cp)

        c_copies = []
        for w in range(3):
            n = halves[w]
            for m in range(3):
                _remote(sbs[w].at[m], rbs[w].at[m], sb_s.at[3 * w + m], sb_r.at[3 * w + m],
                        (peers[m][0], peers[m][1], c)).wait_recv()

            def total(i, carry, w=w, n=n):
                r0 = pl.multiple_of(i * row_block, row_block)
                rows = pl.ds(r0, row_block)
                dst = pl.ds(pl.multiple_of(c * n + r0, 8), row_block)
                acc = gs[w][chip, dst, :] + ras[w][chip, rows, :]
                for m in range(3):
                    acc = acc + rbs[w][m, rows, :].astype(F32)
                outs[w][dst, :] = acc
                return carry
            lax.fori_loop(0, n // row_block, total, 0)
            mine = outs[w].at[rows_of(c, n), :]
            cp = _remote(mine, mine, sc_s.at[w], sc_r.at[w], sib)
            cp.start()
            c_copies.append(cp)
        for m in range(3):
            _remote(p_sm.at[chip], p_sm.at[peer_chip[m]], sb_s.at[9 + m], sb_r.at[9 + m],
                    (peers[m][0], peers[m][1], c)).wait_recv()
        osm_ref[rows_of(c, half_small), :] = (p_sm[0] + p_sm[1]) + (p_sm[2] + p_sm[3])
        mine = osm_ref.at[rows_of(c, half_small), :]
        cp = _remote(mine, mine, sc_s.at[3], sc_r.at[3], sib)
        cp.start()
        c_copies.append(cp)
        for w in range(3):
            theirs = outs[w].at[rows_of(1 - c, halves[w]), :]
            _remote(theirs, theirs, sc_s.at[w], sc_r.at[w], sib).wait_recv()
        theirs = osm_ref.at[rows_of(1 - c, half_small), :]
        _remote(theirs, theirs, sc_s.at[3], sc_r.at[3], sib).wait_recv()
        for cp in a_copies + b_copies + c_copies:
            cp.wait_send()

    vmem = pl.BlockSpec(memory_space=pltpu.VMEM)
    scratch = ([pltpu.VMEM((N_CHIPS, s[1] // 2, s[2]), F32) for s in big_shapes]
               + [pltpu.VMEM((half_small, LANES), F32)]
               + [pltpu.VMEM((3, s[1] // 2, s[2]), BF16) for s in big_shapes]
               + [pltpu.VMEM((3, s[1] // 2, s[2]), BF16) for s in big_shapes]
               + [pltpu.VMEM((N_CHIPS, half_small, LANES), F32)]
               + [pltpu.SemaphoreType.DMA((4,)), pltpu.SemaphoreType.DMA((4,)),
                  pltpu.SemaphoreType.DMA((12,)), pltpu.SemaphoreType.DMA((12,)),
                  pltpu.SemaphoreType.DMA((4,)), pltpu.SemaphoreType.DMA((4,))])
    return pl.pallas_call(
        body,
        out_shape=[jax.ShapeDtypeStruct(s[1:], F32) for s in big_shapes]
        + [jax.ShapeDtypeStruct((SMALL_ROWS, LANES), F32)],
        in_specs=[vmem] * 4, out_specs=[vmem] * 4, scratch_shapes=scratch,
        compiler_params=pltpu.CompilerParams(vmem_limit_bytes=58 * 1024 * 1024),
        name="reduce_grads")(g_in, g_kv, g_out, g_small)


_SMALL_PARTS = (("g_norm", 8, 8), ("w_s", 512, 512), ("b_s", 4, 8), ("g_v", 2, 8), ("g_mem", 8, 8),
                ("g_final", 8, 8))


def _pack_small(parts):
    rows = []
    for (name, used, padded), p in zip(_SMALL_PARTS, parts):
        p = p.reshape(used, LANES)
        if padded > used:
            p = jnp.pad(p, ((0, padded - used), (0, 0)))
        rows.append(p)
    total = sum(p for _, _, p in _SMALL_PARTS)
    rows.append(jnp.zeros((SMALL_ROWS - total, LANES), F32))
    return jnp.concatenate(rows, axis=0)


def _unpack_small(packed, shapes):
    out = []
    off = 0
    for (name, used, padded), shape in zip(_SMALL_PARTS, shapes):
        out.append(packed[off:off + used].reshape(shape))
        off += padded
    return out


def _local_step(x, mem, target, g_norm, w_in, w_s, b_s, g_v, g_mem, w_kv, w_out, g_final):
    B, S, _ = x.shape
    x2d = x.reshape(B * S, D_MODEL)
    t2d = target.reshape(B * S, D_MODEL)
    mem2d = mem.reshape(B * N_MEM, D_MODEL)

    proj = _inproj_fwd(x2d, g_norm, w_in)
    kv = _kv_fwd(mem2d, g_mem, w_kv)
    a, lse = _attn_fwd(proj, B, S)
    w_sT = jnp.swapaxes(w_s, 1, 2)
    b_tab = jnp.repeat(b_s.T, HEAD_DIM, axis=1)
    (dx2, da, drest, loss, d_wout, d_ws, d_bs, d_gv, d_gf, dkv) = _mid(
        x2d, t2d, a, proj, kv, w_s, w_sT, b_tab, g_v, w_out, g_final, B, S)
    dq, dk, dv = _attn_bwd(proj, a, lse, da, B, S)
    grad_x, d_gnorm = _inproj_bwd_dx(dq, dk, dv, drest, x2d, dx2, g_norm, w_in)
    d_win = _inproj_bwd_dw(dq, dk, dv, drest, x2d, g_norm)
    d_wkv, d_gmem = _kv_bwd(mem2d, g_mem, w_kv, dkv)
    d_bs = d_bs[:, :N_SGU_GROUPS].T
    return (loss[0, 0], grad_x.reshape(B, S, D_MODEL),
            dict(g_norm=d_gnorm, w_in=d_win, w_s=d_ws, b_s=d_bs, g_v=d_gv, g_mem=d_gmem, w_kv=d_wkv,
                 w_out=d_wout, g_final=d_gf))


def kernel(x, mem, g_norm, w_in, w_sgu_spatial, b_sgu_spatial, g_sgu_v, g_mem, w_mem_kv, w_out, g_final, loss_target, m_g_norm, m_w_in, m_w_sgu_spatial, m_b_sgu_spatial, m_g_sgu_v, m_g_mem, m_w_mem_kv, m_w_out, m_g_final, v_g_norm, v_w_in, v_w_sgu_spatial, v_b_sgu_spatial, v_g_sgu_v, v_g_mem, v_w_mem_kv, v_w_out, v_g_final):
    t = lambda w: jnp.swapaxes(w[0], 0, 1)
    win_all, wkv_all, wout_all = _ag_weights(t(w_in), w_mem_kv[0], w_out[0])
    w_in_full = win_all.reshape(-1, win_all.shape[-1])
    w_kv_full = wkv_all.reshape(-1, wkv_all.shape[-1])
    w_out_full = wout_all.reshape(-1, wout_all.shape[-1])

    loss, grad_x, g = _local_step(
        x, mem, loss_target, g_norm, w_in_full, w_sgu_spatial[0], b_sgu_spatial[0], g_sgu_v, g_mem,
        w_kv_full, w_out_full, g_final.reshape(1, D_MODEL))
    loss = lax.psum(loss, ("x", "y", "c"))

    small_names = ("g_norm", "w_s", "b_s", "g_v", "g_mem", "g_final")
    g_in_stack = g["w_in"].reshape((N_CHIPS, w_in.shape[2], w_in.shape[1]))
    g_kv_stack = g["w_kv"].reshape((N_CHIPS,) + w_mem_kv.shape[1:])
    g_out_stack = g["w_out"].reshape((N_CHIPS,) + w_out.shape[1:])
    gr_in, gr_kv, gr_out, gr_small = _reduce_grads(
        g_in_stack, g_kv_stack, g_out_stack, _pack_small([g[n] for n in small_names]))

    small_w = (g_norm, w_sgu_spatial, b_sgu_spatial, g_sgu_v, g_mem, g_final)
    small_m = (m_g_norm, m_w_sgu_spatial, m_b_sgu_spatial, m_g_sgu_v, m_g_mem, m_g_final)
    small_v = (v_g_norm, v_w_sgu_spatial, v_b_sgu_spatial, v_g_sgu_v, v_g_mem, v_g_final)
    shapes = [w.shape for w in small_w]
    d_small, nm_small, nv_small = _adamw(_pack_small(small_w), gr_small, _pack_small(small_m),
                                         _pack_small(small_v), "adamw_small")
    d_in, nm_in, nv_in = _adamw(t(w_in), gr_in, t(m_w_in), t(v_w_in), "adamw_w_in")
    gr_in, d_in, nm_in, nv_in = [jnp.swapaxes(z, 0, 1) for z in (gr_in, d_in, nm_in, nv_in)]
    d_kv, nm_kv, nv_kv = _adamw(w_mem_kv[0], gr_kv, m_w_mem_kv[0], v_w_mem_kv[0], "adamw_w_kv")
    d_out, nm_out, nv_out = _adamw(w_out[0], gr_out, m_w_out[0], v_w_out[0], "adamw_w_out")

    def leaves(packed, big_in, big_kv, big_out):
        s_norm, s_ws, s_bs, s_gv, s_gmem, s_gf = _unpack_small(packed, shapes)
        return [s_norm, big_in[None], s_ws, s_bs, s_gv, s_gmem, big_kv[None], big_out[None], s_gf]

    return (loss, grad_x, *leaves(gr_small, gr_in, gr_kv, gr_out), *leaves(d_small, d_in, d_kv, d_out),
            *leaves(nm_small, nm_in, nm_kv, nm_out), *leaves(nv_small, nv_in, nv_kv, nv_out))
```

```python
import functools

import jax
import jax.numpy as jnp
from jax import lax
from jax.experimental import pallas as pl
from jax.experimental.pallas import tpu as pltpu

F32 = jnp.float32
BF16 = jnp.bfloat16
MESH = pl.DeviceIdType.MESH

D_MODEL = 1024
ATTN_WIDTH = 512
SGU_WIDTH = 256
MEM_WIDTH = 256
N_MEM = 256
IN_COLS = 3328
QKV_COLS = 3 * ATTN_WIDTH
REST_COLS = IN_COLS - QKV_COLS
SGU_CHUNK = 128
N_SGU_GROUPS = 4
EPS = 1e-6
NEG_INF = -1e30
DILATIONS = (1, 4, 16)
RADIUS = 64
Q_BLOCK = 128
LANES = 128
HEAD_DIM = 64

ADAM_LR = 0.001
ADAM_B1 = 0.9
ADAM_B2 = 0.999
ADAM_EPS = 1e-08
ADAM_WD = 0.01
ADAM_STEP = 10

N_CHIPS = 4
VMEM_LIMIT = 56 * 1024 * 1024
SMALL_ROWS = 560


def _params(sem=None, vmem=VMEM_LIMIT):
    return pltpu.CompilerParams(dimension_semantics=sem, vmem_limit_bytes=vmem)


def _nn(a, b):
    return jnp.dot(a, b, preferred_element_type=F32)


def _nt(a, b):
    return lax.dot_general(a, b, (((1,), (1,)), ((), ())), preferred_element_type=F32)


def _tn(a, b):
    return lax.dot_general(a, b, (((0,), (0,)), ((), ())), preferred_element_type=F32)


def _rms(x):
    r = lax.rsqrt(jnp.mean(x * x, axis=-1, keepdims=True) + EPS)
    return r, x * r


def _head_masks():
    lane = lax.broadcasted_iota(jnp.int32, (1, LANES), 1)
    lo = lane < HEAD_DIM
    return lo, (lo.astype(F32), (~lo).astype(F32))


def _silu_parts(z):
    s = jax.nn.sigmoid(z)
    return z * s, s * (1.0 + z * (1.0 - s))


def _gelu_parts(x):
    c = 0.7978845608028654
    x2 = x * x
    t = jnp.tanh(c * (x + 0.044715 * (x * x2)))
    val = 0.5 * x * (1.0 + t)
    grad = 0.5 * (1.0 + t) + 0.5 * x * (1.0 - t * t) * (c * (1.0 + 3.0 * 0.044715 * x2))
    return val, grad


def _inproj_fwd(x2d, g_norm, w_in_t):
    T = x2d.shape[0]
    tm = 512

    def body(x_ref, g_ref, w_ref, o_ref):
        _, xh = _rms(x_ref[...])
        h = (xh * g_ref[...]).astype(BF16)
        o_ref[...] = _nt(h, w_ref[...])

    return pl.pallas_call(
        body, grid=(T // tm,),
        in_specs=[pl.BlockSpec((tm, D_MODEL), lambda i: (i, 0)),
                  pl.BlockSpec((1, D_MODEL), lambda i: (0, 0)),
                  pl.BlockSpec((IN_COLS, D_MODEL), lambda i: (0, 0))],
        out_specs=pl.BlockSpec((tm, IN_COLS), lambda i: (i, 0)),
        out_shape=jax.ShapeDtypeStruct((T, IN_COLS), F32),
        compiler_params=_params(("arbitrary",)), name="inproj_fwd")(x2d, g_norm, w_in_t)


def _kv_fwd(mem2d, g_mem, w_kv):
    Tm = mem2d.shape[0]

    def body(m_ref, g_ref, w_ref, o_ref):
        _, mh = _rms(m_ref[...])
        o_ref[...] = _nn((mh * g_ref[...]).astype(BF16), w_ref[...])

    return pl.pallas_call(
        body, out_shape=jax.ShapeDtypeStruct((Tm, 2 * MEM_WIDTH), F32),
        compiler_params=_params(), name="kv_fwd")(mem2d, g_mem, w_kv)


def _kv_bwd(mem2d, g_mem, w_kv, dkv):
    Tm = mem2d.shape[0]

    def body(m_ref, g_ref, w_ref, dkv_ref, dw_ref, dg_ref):
        _, mh = _rms(m_ref[...])
        memn = (mh * g_ref[...]).astype(BF16)
        dkvb = dkv_ref[...].astype(BF16)
        dw_ref[...] = _tn(memn, dkvb)
        dmemn = _nt(dkvb, w_ref[...])
        dg_ref[...] = jnp.sum(dmemn * mh, axis=0, keepdims=True)

    return pl.pallas_call(
        body, out_shape=(jax.ShapeDtypeStruct((D_MODEL, 2 * MEM_WIDTH), F32),
                         jax.ShapeDtypeStruct((1, D_MODEL), F32)),
        compiler_params=_params(), name="kv_bwd")(mem2d, g_mem, w_kv, dkv)


def _attn_geometry(S):
    geom = []
    for d in DILATIONS:
        L = S // d
        assert L % Q_BLOCK == 0
        geom.append((d, L, min(2 * Q_BLOCK, L), L // Q_BLOCK))
    return geom


def _init_bias(bias_scr, geom, hp):
    row = lax.broadcasted_iota(jnp.int32, (Q_BLOCK, 2 * Q_BLOCK), 0)
    col = lax.broadcasted_iota(jnp.int32, (Q_BLOCK, 2 * Q_BLOCK), 1)
    for j in (0, 1):
        bits = (126 - (2 * hp + j)) * (1 << 23)
        slope = lax.bitcast_convert_type(jnp.full((1, 1), bits, jnp.int32), F32)
        for di, (d, _, _, _) in enumerate(geom):
            for cls, off in enumerate((0, -RADIUS, -2 * RADIUS)):
                dist = jnp.abs(col - row + off)
                bias_scr[di * 6 + cls * 2 + j] = jnp.where(
                    dist <= RADIUS, -(slope * float(d)) * dist.astype(F32), NEG_INF)


def _block_slices(d, L, KW, nqb, r, qb):
    qs = qb * Q_BLOCK
    ks = jnp.clip(qs - RADIUS, 0, L - KW)
    cls = jnp.where(qb == 0, 0, jnp.where(qb == nqb - 1, 2, 1))
    if d == 1:
        qsl = pl.ds(pl.multiple_of(qs, Q_BLOCK), Q_BLOCK)
        ksl = pl.ds(pl.multiple_of(ks, RADIUS), KW)
    else:
        qsl = pl.ds(r + qs * d, Q_BLOCK, stride=d)
        ksl = pl.ds(r + ks * d, KW, stride=d)
    return qsl, ksl, cls


def _for_groups(geom, group, fn):
    for di, (d, L, KW, nqb) in enumerate(geom):
        assert (d * nqb) % group == 0

        def step(it, carry, di=di, d=d, L=L, KW=KW, nqb=nqb):
            slices = []
            for g in range(group):
                i = it * group + g
                slices.append(_block_slices(d, L, KW, nqb, i // nqb, i % nqb))
            fn(di, KW, slices)
            return carry
        lax.fori_loop(0, d * nqb // group, step, 0)


def _attn_fwd(proj, B, S):
    T = B * S
    geom = _attn_geometry(S)
    n_pairs = ATTN_WIDTH // LANES

    def body(q_ref, k_ref, v_ref, a_ref, lse_ref, bias_scr, *per_dilation):
        o_scr, m_scr, l_scr = per_dilation[0:3], per_dilation[3:6], per_dilation[6:9]
        lo, hm = _head_masks()
        _init_bias(bias_scr, geom, pl.program_id(1))

        def group(di, KW, slices):
            chains = [(g, j) for g in range(len(slices)) for j in (0, 1)]
            q = [q_ref[qsl, :] for qsl, _, _ in slices]
            kw = [k_ref[ksl, :].astype(BF16) for _, ksl, _ in slices]
            vw = [v_ref[ksl, :].astype(BF16) for _, ksl, _ in slices]
            s = {(g, j): _nt((q[g] * (hm[j] * 0.125)).astype(BF16), kw[g])
                 + bias_scr[di * 6 + slices[g][2] * 2 + j, :, pl.ds(0, KW)] for g, j in chains}
            m = {c: jnp.max(s[c], axis=1, keepdims=True) for c in chains}
            p = {c: jnp.exp(s[c] - m[c]) for c in chains}
            l = {c: jnp.sum(p[c], axis=1, keepdims=True) for c in chains}
            o = {(g, j): _nn(p[(g, j)].astype(BF16), vw[g]) for g, j in chains}
            for g, (qsl, _, _) in enumerate(slices):
                o_scr[di][qsl, :] = jnp.where(lo, o[(g, 0)], o[(g, 1)])
                m_scr[di][qsl, :] = jnp.where(lo, m[(g, 0)], m[(g, 1)])
                l_scr[di][qsl, :] = jnp.where(lo, l[(g, 0)], l[(g, 1)])

        _for_groups(geom, 8, group)

        rows_per = 256

        def combine(i, carry):
            rows = pl.ds(pl.multiple_of(i * rows_per, rows_per), rows_per)
            ms = [m_scr[di][rows, :] for di in range(3)]
            mx = jnp.maximum(jnp.maximum(ms[0], ms[1]), ms[2])
            num = 0.0
            den = 0.0
            for di in range(3):
                w = jnp.exp(ms[di] - mx)
                num = num + w * o_scr[di][rows, :]
                den = den + w * l_scr[di][rows, :]
            a_ref[rows, :] = num / den
            lse_ref[rows, :] = mx + jnp.log(den)
            return carry

        lax.fori_loop(0, S // rows_per, combine, 0)

    blk = lambda off: pl.BlockSpec((S, LANES), lambda b, h, off=off: (b, off + h))
    out_blk = pl.BlockSpec((S, LANES), lambda b, h: (b, h))
    return pl.pallas_call(
        body, grid=(B, n_pairs),
        in_specs=[blk(0), blk(n_pairs), blk(2 * n_pairs)],
        out_specs=[out_blk, out_blk],
        out_shape=[jax.ShapeDtypeStruct((T, ATTN_WIDTH), F32)] * 2,
        scratch_shapes=[pltpu.VMEM((18, Q_BLOCK, 2 * Q_BLOCK), F32)] + [pltpu.VMEM((S, LANES), F32)] * 9,
        compiler_params=_params(("arbitrary", "arbitrary")), name="attn_fwd")(proj, proj, proj)


def _attn_bwd(proj, a, lse, da, B, S):
    T = B * S
    geom = _attn_geometry(S)
    n_pairs = ATTN_WIDTH // LANES

    def body(q_ref, k_ref, v_ref, a_ref, lse_ref, do_ref, dq_ref, dk_ref, dv_ref,
             bias_scr, dq_scr, dk_scr, dv_scr):
        _, hm = _head_masks()
        _init_bias(bias_scr, geom, pl.program_id(1))
        dq_scr[...] = jnp.zeros_like(dq_scr)
        dk_scr[...] = jnp.zeros_like(dk_scr)
        dv_scr[...] = jnp.zeros_like(dv_scr)

        def group(di, KW, slices):
            n = len(slices)
            chains = [(g, j) for g in range(n) for j in (0, 1)]
            q = [q_ref[qsl, :] for qsl, _, _ in slices]
            do = [do_ref[qsl, :] for qsl, _, _ in slices]
            doa = [do[g] * a_ref[slices[g][0], :] for g in range(n)]
            lse_q = [lse_ref[qsl, :] for qsl, _, _ in slices]
            kw = [k_ref[ksl, :].astype(BF16) for _, ksl, _ in slices]
            vw = [v_ref[ksl, :].astype(BF16) for _, ksl, _ in slices]
            qj = {(g, j): (q[g] * (hm[j] * 0.125)).astype(BF16) for g, j in chains}
            doj = {(g, j): (do[g] * hm[j]).astype(BF16) for g, j in chains}
            s = {(g, j): _nt(qj[(g, j)], kw[g])
                 + bias_scr[di * 6 + slices[g][2] * 2 + j, :, pl.ds(0, KW)] for g, j in chains}
            dp = {(g, j): _nt(doj[(g, j)], vw[g]) for g, j in chains}
            dsum = {(g, j): jnp.sum(doa[g] * hm[j], axis=1, keepdims=True) for g, j in chains}
            p = {(g, j): jnp.exp(s[(g, j)] - lse_q[g][:, HEAD_DIM * j:HEAD_DIM * j + 1]) for g, j in chains}
            ds = {c: (p[c] * (dp[c] - dsum[c])).astype(BF16) for c in chains}
            pb = {c: p[c].astype(BF16) for c in chains}
            dq = [_nn(ds[(g, 0)], kw[g]) * (hm[0] * 0.125) + _nn(ds[(g, 1)], kw[g]) * (hm[1] * 0.125)
                  for g in range(n)]
            both = lambda t, g: jnp.concatenate([t[(g, 0)], t[(g, 1)]], axis=0)
            dkw = [_tn(both(ds, g), both(qj, g)) for g in range(n)]
            dvw = [_tn(both(pb, g), both(doj, g)) for g in range(n)]
            for g, (qsl, ksl, _) in enumerate(slices):
                dq_scr[qsl, :] = dq_scr[qsl, :] + dq[g]
                dk_scr[ksl, :] = dk_scr[ksl, :] + dkw[g]
                dv_scr[ksl, :] = dv_scr[ksl, :] + dvw[g]

        _for_groups(geom, 4, group)
        dq_ref[...] = dq_scr[...].astype(BF16)
        dk_ref[...] = dk_scr[...].astype(BF16)
        dv_ref[...] = dv_scr[...].astype(BF16)

    blk = lambda off: pl.BlockSpec((S, LANES), lambda b, h, off=off: (b, off + h))
    return pl.pallas_call(
        body, grid=(B, n_pairs),
        in_specs=[blk(0), blk(n_pairs), blk(2 * n_pairs), blk(0), blk(0), blk(0)],
        out_specs=[blk(0), blk(0), blk(0)],
        out_shape=[jax.ShapeDtypeStruct((T, ATTN_WIDTH), BF16)] * 3,
        scratch_shapes=[pltpu.VMEM((18, Q_BLOCK, 2 * Q_BLOCK), F32),
                        pltpu.VMEM((S, LANES), F32),
                        pltpu.VMEM((S, LANES), F32),
                        pltpu.VMEM((S, LANES), F32)],
        compiler_params=_params(("arbitrary", "arbitrary")), name="attn_bwd")(proj, proj, proj, a, lse, da)


def _mid(x2d, t2d, a, proj, kv, w_s, w_sT, b_tab, g_v, w_out, g_final, B, S):
    T = B * S
    tm = 512
    nt = S // tm
    n_chunks = tm // SGU_CHUNK

    def body(x_ref, t_ref, a_ref, za_ref, ub_ref, vb_ref, zb_ref, qm_ref, zm_ref, kv_ref,
             ws_ref, wsT_ref, btab_ref, gv_ref, wout_ref, gf_ref,
             dx2_ref, da_ref, drest_ref, loss_ref, dwout_ref, dws_ref, dbs_ref, dgv_ref, dgf_ref, dkv_ref,
             dbtab_scr):
        b = pl.program_id(0)
        t = pl.program_id(1)
        first = jnp.logical_and(b == 0, t == 0)
        last = jnp.logical_and(b == B - 1, t == nt - 1)
        _, hm = _head_masks()
        lane_g = lax.broadcasted_iota(jnp.int32, (1, SGU_WIDTH), 1) // HEAD_DIM
        gm = [(lane_g == g).astype(F32) for g in range(N_SGU_GROUPS)]

        @pl.when(first)
        def _():
            loss_ref[...] = jnp.zeros_like(loss_ref)
            dwout_ref[...] = jnp.zeros_like(dwout_ref)
            dws_ref[...] = jnp.zeros_like(dws_ref)
            dbs_ref[...] = jnp.zeros_like(dbs_ref)
            dgv_ref[...] = jnp.zeros_like(dgv_ref)
            dgf_ref[...] = jnp.zeros_like(dgf_ref)
            dbtab_scr[...] = jnp.zeros_like(dbtab_scr)

        @pl.when(t == 0)
        def _():
            dkv_ref[...] = jnp.zeros_like(dkv_ref)

        sil_a, dsil_a = _silu_parts(za_ref[...])
        a_val = a_ref[...]
        gated_a = sil_a * a_val

        u, du_dub = _gelu_parts(ub_ref[...])
        vv, dvv_dvb = _gelu_parts(vb_ref[...])
        rv, vhat = _rms(vv)
        gv = gv_ref[...]
        vn = (vhat * gv).astype(BF16)
        ws = [ws_ref[g].astype(BF16) for g in range(N_SGU_GROUPS)]
        wsT = [wsT_ref[g].astype(BF16) for g in range(N_SGU_GROUPS)]
        mixed = []
        for ci in range(n_chunks):
            vn_c = vn[ci * SGU_CHUNK:(ci + 1) * SGU_CHUNK, :]
            mc = btab_ref[...]
            for g in range(N_SGU_GROUPS):
                mc = mc + gm[g] * _nn(ws[g], vn_c)
            mixed.append(mc)
        mixed = jnp.concatenate(mixed, axis=0)
        sg = u * mixed
        sil_b, dsil_b = _silu_parts(zb_ref[...])
        gated_b = sil_b * sg

        kvv = kv_ref[...].astype(BF16)
        qm = qm_ref[...]
        probs, qs_m, mo_pairs = [], [], []
        for pr in range(2):
            cols = slice(pr * LANES, (pr + 1) * LANES)
            kp = kvv[:, pr * LANES:(pr + 1) * LANES]
            vp = kvv[:, MEM_WIDTH + pr * LANES:MEM_WIDTH + (pr + 1) * LANES]
            mo_p = 0.0
            for j in (0, 1):
                qj = (qm[:, cols] * (hm[j] * 0.125)).astype(BF16)
                s = _nt(qj, kp)
                e = jnp.exp(s - jnp.max(s, axis=1, keepdims=True))
                p = e / jnp.sum(e, axis=1, keepdims=True)
                mo_p = mo_p + _nn(p.astype(BF16), vp) * hm[j]
                probs.append(p)
                qs_m.append(qj)
            mo_pairs.append(mo_p)
        mo = jnp.concatenate(mo_pairs, axis=1)
        sil_m, dsil_m = _silu_parts(zm_ref[...])
        gated_m = sil_m * mo

        gated = jnp.concatenate([gated_a, gated_b, gated_m], axis=1).astype(BF16)
        wout = wout_ref[...]
        x2 = x_ref[...] + _nn(gated, wout)
        r2, xh2 = _rms(x2)
        gf = gf_ref[...]
        err = xh2 * gf - t_ref[...]
        loss_ref[...] += jnp.sum(err * err) * (0.5 / D_MODEL)

        dy = err * (1.0 / D_MODEL)
        dgf_ref[...] += jnp.sum(dy * xh2, axis=0, keepdims=True)
        gdy = dy * gf
        dx2 = r2 * (gdy - xh2 * jnp.mean(gdy * xh2, axis=1, keepdims=True))
        dx2_ref[...] = dx2
        dx2b = dx2.astype(BF16)
        dwout_ref[...] += _tn(gated, dx2b)
        dgated = _nt(dx2b, wout)
        dga = dgated[:, 0:ATTN_WIDTH]
        dgb = dgated[:, ATTN_WIDTH:ATTN_WIDTH + SGU_WIDTH]
        dgm = dgated[:, ATTN_WIDTH + SGU_WIDTH:]

        da_ref[...] = dga * sil_a
        dza = dga * a_val * dsil_a

        dsg = dgb * sil_b
        dzb = dgb * sg * dsil_b
        dub = dsg * mixed * du_dub
        dmixed = dsg * u
        dmixed_b = dmixed.astype(BF16)
        dvn = []
        dbtab = dbtab_scr[...]
        for ci in range(n_chunks):
            rows = slice(ci * SGU_CHUNK, (ci + 1) * SGU_CHUNK)
            dm_c = dmixed_b[rows, :]
            vn_c = vn[rows, :]
            dvn_c = 0.0
            for g in range(N_SGU_GROUPS):
                dvn_c = dvn_c + gm[g] * _nn(wsT[g], dm_c)
                dws_ref[g] += _nt((dmixed[rows, :] * gm[g]).astype(BF16), vn_c)
            dvn.append(dvn_c)
            dbtab = dbtab + dmixed[rows, :]
        dbtab_scr[...] = dbtab
        dvn = jnp.concatenate(dvn, axis=0)
        dgv_ref[...] += jnp.sum(dvn * vhat, axis=0, keepdims=True)
        tv = dvn * gv
        dvv = rv * (tv - vhat * jnp.mean(tv * vhat, axis=1, keepdims=True))
        dvb = dvv * dvv_dvb

        dmo = dgm * sil_m
        dzm = dgm * mo * dsil_m
        dqm_pairs = []
        dk_pairs, dv_pairs = [], []
        for pr in range(2):
            kp = kvv[:, pr * LANES:(pr + 1) * LANES]
            vp = kvv[:, MEM_WIDTH + pr * LANES:MEM_WIDTH + (pr + 1) * LANES]
            dmo_p = dmo[:, pr * LANES:(pr + 1) * LANES]
            dq_p = 0.0
            dk_p = 0.0
            dv_p = 0.0
            for j in (0, 1):
                p = probs[2 * pr + j]
                dmo_j = (dmo_p * hm[j]).astype(BF16)
                dp = _nt(dmo_j, vp)
                ds = (p * (dp - jnp.sum(dp * p, axis=1, keepdims=True))).astype(BF16)
                dq_p = dq_p + _nn(ds, kp) * (hm[j] * 0.125)
                dk_p = dk_p + _tn(ds, qs_m[2 * pr + j])
                dv_p = dv_p + _tn(p.astype(BF16), dmo_j)
            dqm_pairs.append(dq_p)
            dk_pairs.append(dk_p)
            dv_pairs.append(dv_p)
        dqm = jnp.concatenate(dqm_pairs, axis=1)
        dkv_ref[...] += jnp.concatenate(dk_pairs + dv_pairs, axis=1)

        drest_ref[...] = jnp.concatenate([dza, dub, dvb, dzb, dqm, dzm], axis=1).astype(BF16)

        @pl.when(last)
        def _():
            lane = lax.broadcasted_iota(jnp.int32, (1, LANES), 1)
            dbt = dbtab_scr[...]
            out = jnp.zeros((SGU_CHUNK, LANES), F32)
            for g in range(N_SGU_GROUPS):
                out = out + jnp.where(lane == g, jnp.sum(dbt * gm[g], axis=1, keepdims=True), 0.0)
            dbs_ref[...] = out

    tile = lambda w, cb: pl.BlockSpec((tm, w), lambda b, t, cb=cb: (b * nt + t, cb))
    const = lambda shape: pl.BlockSpec(shape, lambda b, t, n=len(shape): (0,) * n)
    return pl.pallas_call(
        body, grid=(B, nt),
        in_specs=[tile(D_MODEL, 0), tile(D_MODEL, 0), tile(ATTN_WIDTH, 0),
                  tile(ATTN_WIDTH, 3),
                  tile(SGU_WIDTH, 8), tile(SGU_WIDTH, 9), tile(SGU_WIDTH, 10),
                  tile(MEM_WIDTH, 11), tile(MEM_WIDTH, 12),
                  pl.BlockSpec((N_MEM, 2 * MEM_WIDTH), lambda b, t: (b, 0)),
                  const((N_SGU_GROUPS, SGU_CHUNK, SGU_CHUNK)), const((N_SGU_GROUPS, SGU_CHUNK, SGU_CHUNK)),
                  const((SGU_CHUNK, SGU_WIDTH)), const((1, SGU_WIDTH)),
                  const((D_MODEL, D_MODEL)), const((1, D_MODEL))],
        out_specs=[tile(D_MODEL, 0), tile(ATTN_WIDTH, 0), tile(REST_COLS, 0),
                   const((8, LANES)), const((D_MODEL, D_MODEL)),
                   const((N_SGU_GROUPS, SGU_CHUNK, SGU_CHUNK)), const((SGU_CHUNK, LANES)),
                   const((1, SGU_WIDTH)), const((1, D_MODEL)),
                   pl.BlockSpec((N_MEM, 2 * MEM_WIDTH), lambda b, t: (b, 0))],
        out_shape=[jax.ShapeDtypeStruct((T, D_MODEL), F32), jax.ShapeDtypeStruct((T, ATTN_WIDTH), F32),
                   jax.ShapeDtypeStruct((T, REST_COLS), BF16),
                   jax.ShapeDtypeStruct((8, LANES), F32), jax.ShapeDtypeStruct((D_MODEL, D_MODEL), F32),
                   jax.ShapeDtypeStruct((N_SGU_GROUPS, SGU_CHUNK, SGU_CHUNK), F32),
                   jax.ShapeDtypeStruct((SGU_CHUNK, LANES), F32),
                   jax.ShapeDtypeStruct((1, SGU_WIDTH), F32), jax.ShapeDtypeStruct((1, D_MODEL), F32),
                   jax.ShapeDtypeStruct((B * N_MEM, 2 * MEM_WIDTH), F32)],
        scratch_shapes=[pltpu.VMEM((SGU_CHUNK, SGU_WIDTH), F32)],
        compiler_params=_params(("arbitrary", "arbitrary")), name="mid")(
            x2d, t2d, a, proj, proj, proj, proj, proj, proj, kv, w_s, w_sT, b_tab, g_v, w_out, g_final)


def _inproj_bwd_dx(dq, dk, dv, drest, x2d, dx2, g_norm, w_in_t):
    T = x2d.shape[0]
    tm = 512
    W = ATTN_WIDTH

    def body(dq_ref, dk_ref, dv_ref, dr_ref, x_ref, dx2_ref, g_ref, w_ref, gx_ref, dg_ref):
        @pl.when(pl.program_id(0) == 0)
        def _():
            dg_ref[...] = jnp.zeros_like(dg_ref)

        dh = (_nn(dq_ref[...], w_ref[0:W, :]) + _nn(dk_ref[...], w_ref[W:2 * W, :])
              + _nn(dv_ref[...], w_ref[2 * W:3 * W, :]) + _nn(dr_ref[...], w_ref[QKV_COLS:IN_COLS, :]))
        r, xh = _rms(x_ref[...])
        dg_ref[...] += jnp.sum(dh * xh, axis=0, keepdims=True)
        th = dh * g_ref[...]
        gx_ref[...] = r * (th - xh * jnp.mean(th * xh, axis=1, keepdims=True)) + dx2_ref[...]

    tile = lambda w: pl.BlockSpec((tm, w), lambda i: (i, 0))
    return pl.pallas_call(
        body, grid=(T // tm,),
        in_specs=[tile(W), tile(W), tile(W), tile(REST_COLS), tile(D_MODEL), tile(D_MODEL),
                  pl.BlockSpec((1, D_MODEL), lambda i: (0, 0)),
                  pl.BlockSpec((IN_COLS, D_MODEL), lambda i: (0, 0))],
        out_specs=[tile(D_MODEL), pl.BlockSpec((1, D_MODEL), lambda i: (0, 0))],
        out_shape=[jax.ShapeDtypeStruct((T, D_MODEL), F32), jax.ShapeDtypeStruct((1, D_MODEL), F32)],
        compiler_params=_params(("arbitrary",)), name="inproj_bwd_dx")(dq, dk, dv, drest, x2d, dx2, g_norm, w_in_t)


def _inproj_bwd_dw(dq, dk, dv, drest, x2d, g_norm):
    T = x2d.shape[0]
    tm = 512
    W = ATTN_WIDTH

    def body(dq_ref, dk_ref, dv_ref, dr_ref, x_ref, g_ref, dw_ref):
        @pl.when(pl.program_id(0) == 0)
        def _():
            dw_ref[...] = jnp.zeros_like(dw_ref)

        _, xh = _rms(x_ref[...])
        h = (xh * g_ref[...]).astype(BF16)
        dw_ref[0:W, :] += _tn(dq_ref[...], h)
        dw_ref[W:2 * W, :] += _tn(dk_ref[...], h)
        dw_ref[2 * W:3 * W, :] += _tn(dv_ref[...], h)
        dw_ref[QKV_COLS:IN_COLS, :] += _tn(dr_ref[...], h)

    tile = lambda w: pl.BlockSpec((tm, w), lambda i: (i, 0))
    return pl.pallas_call(
        body, grid=(T // tm,),
        in_specs=[tile(W), tile(W), tile(W), tile(REST_COLS), tile(D_MODEL),
                  pl.BlockSpec((1, D_MODEL), lambda i: (0, 0))],
        out_specs=pl.BlockSpec((IN_COLS, D_MODEL), lambda i: (0, 0)),
        out_shape=jax.ShapeDtypeStruct((IN_COLS, D_MODEL), F32),
        compiler_params=_params(("arbitrary",)), name="inproj_bwd_dw")(dq, dk, dv, drest, x2d, g_norm)


def _adamw(w, g, m, v, name):
    R, C = w.shape
    br = max(r for r in range(8, 257, 8) if R % r == 0)

    def body(w_ref, g_ref, m_ref, v_ref, d_ref, nm_ref, nv_ref):
        gg = g_ref[...]
        nm = ADAM_B1 * m_ref[...] + (1.0 - ADAM_B1) * gg
        nv = ADAM_B2 * v_ref[...] + (1.0 - ADAM_B2) * (gg * gg)
        m_hat = nm / (1.0 - ADAM_B1 ** ADAM_STEP)
        v_hat = nv / (1.0 - ADAM_B2 ** ADAM_STEP)
        d_ref[...] = -ADAM_LR * (m_hat / (jnp.sqrt(v_hat) + ADAM_EPS) + ADAM_WD * w_ref[...])
        nm_ref[...] = nm
        nv_ref[...] = nv

    spec = pl.BlockSpec((br, C), lambda i: (i, 0))
    return pl.pallas_call(
        body, grid=(R // br,), in_specs=[spec] * 4, out_specs=[spec] * 3,
        out_shape=[jax.ShapeDtypeStruct((R, C), F32)] * 3,
        compiler_params=_params(("arbitrary",)), name=name)(w, g, m, v)


def _place():
    x, y, c = lax.axis_index("x"), lax.axis_index("y"), lax.axis_index("c")
    chip = 2 * x + y
    peers = [(x, 1 - y), (1 - x, y), (1 - x, 1 - y)]
    peer_chip = [2 * px + py for px, py in peers]
    return x, y, c, chip, peers, peer_chip


def _remote(src, dst, send_sem, recv_sem, dev):
    return pltpu.make_async_remote_copy(src_ref=src, dst_ref=dst, send_sem=send_sem, recv_sem=recv_sem,
                                        device_id=dev, device_id_type=MESH)


def _ag_weights(weights):
    nw = len(weights)

    def body(*refs):
        srcs, outs = refs[:nw], refs[nw:2 * nw]
        s_ici, r_ici, s_d2d, r_d2d = refs[2 * nw:]
        x, y, c, chip, peers, peer_chip = _place()
        sib = (x, y, 1 - c)
        for src, out in zip(srcs, outs):
            out[chip] = src[...].astype(BF16)

        def half(out, k, cc):
            rows = out.shape[1] // 2
            return out.at[k, pl.ds(pl.multiple_of(cc * rows, 16), rows), :]

        sent = []
        for m, (px, py) in enumerate(peers):
            for w, out in enumerate(outs):
                cp = _remote(half(out, chip, c), half(out, chip, c), s_ici.at[nw * m + w], r_ici.at[nw * m + w],
                             (px, py, c))
                cp.start()
                sent.append(cp)
        for m, (px, py) in enumerate(peers):
            for w, out in enumerate(outs):
                blk = half(out, peer_chip[m], c)
                _remote(blk, blk, s_ici.at[nw * m + w], r_ici.at[nw * m + w], (px, py, c)).wait_recv()
                fw = _remote(blk, blk, s_d2d.at[nw * m + w], r_d2d.at[nw * m + w], sib)
                fw.start()
                sent.append(fw)
        for m in range(3):
            for w, out in enumerate(outs):
                blk = half(out, peer_chip[m], 1 - c)
                _remote(blk, blk, s_d2d.at[nw * m + w], r_d2d.at[nw * m + w], sib).wait_recv()
        for cp in sent:
            cp.wait_send()

    vmem = pl.BlockSpec(memory_space=pltpu.VMEM)
    return pl.pallas_call(
        body,
        out_shape=[jax.ShapeDtypeStruct((N_CHIPS,) + w.shape, BF16) for w in weights],
        in_specs=[vmem] * nw, out_specs=[vmem] * nw,
        scratch_shapes=[pltpu.SemaphoreType.DMA((3 * nw,))] * 4,
        compiler_params=pltpu.CompilerParams(vmem_limit_bytes=VMEM_LIMIT), name="ag_weights")(*weights)


_HBM = pl.BlockSpec(memory_space=pltpu.HBM)
_SEM = pl.BlockSpec(memory_space=pltpu.SEMAPHORE)
_ANY = pl.BlockSpec(memory_space=pl.ANY)
_DATAFLOW = pltpu.SideEffectType.DATAFLOW_SIDE_EFFECTING


def _in_hbm(a):
    return pltpu.with_memory_space_constraint(a, pltpu.HBM)


def _exchange_copies(gather, srcs, lands, send_sems, recv_sems):
    nw = len(srcs)
    x, y, c, chip, peers, peer_chip = _place()
    pairs = []
    for m, (px, py) in enumerate(peers):
        for w in range(nw):
            sems = (send_sems.at[nw * m + w], recv_sems.at[nw * m + w], (px, py, c))
            if gather:
                pairs.append((_remote(srcs[w], lands[w].at[chip], *sems),
                              _remote(srcs[w], lands[w].at[peer_chip[m]], *sems)))
            else:
                pairs.append((_remote(srcs[w].at[m], lands[w].at[m], *sems),) * 2)
    return pairs


def _exchange_start(gather, srcs, after, name):
    nw = len(srcs)
    n_copies = 3 * nw

    def body(*refs):
        send_sems, recv_sems = refs[2 * nw + 1], refs[2 * nw + 2]
        for start, _ in _exchange_copies(gather, refs[:nw], refs[nw:2 * nw], send_sems, recv_sems):
            start.start()
        refs[-1][...] = jnp.zeros_like(refs[-1])

    lands = [_in_hbm(lax.empty(((N_CHIPS,) + s.shape) if gather else s.shape, s.dtype)) for s in srcs]
    return pl.pallas_call(
        body, name=name,
        out_shape=(pltpu.SemaphoreType.DMA((n_copies,)), pltpu.SemaphoreType.DMA((n_copies,)))
        + tuple(pltpu.HBM(s.shape, s.dtype) for s in srcs)
        + tuple(pltpu.HBM(l.shape, l.dtype) for l in lands)
        + (jax.ShapeDtypeStruct((8, LANES), F32),),
        in_specs=[_HBM] * (2 * nw) + [_ANY],
        out_specs=(_SEM, _SEM) + (_HBM,) * (2 * nw) + (pl.BlockSpec(memory_space=pltpu.VMEM),),
        input_output_aliases={i: 2 + i for i in range(2 * nw)},
        compiler_params=pltpu.CompilerParams(has_side_effects=_DATAFLOW),
    )(*[_in_hbm(s) for s in srcs], *lands, after)


def _exchange_wait(gather, started, after, name):
    nw = (len(started) - 3) // 2
    send_sems, recv_sems = started[0], started[1]
    thru = started[2:2 + 2 * nw]

    def body(*refs):
        for _, arrival in _exchange_copies(gather, refs[:nw], refs[nw:2 * nw], refs[2 * nw], refs[2 * nw + 1]):
            arrival.wait_send()
            arrival.wait_recv()

    outs = pl.pallas_call(
        body, name=name,
        out_shape=tuple(pltpu.HBM(t.shape, t.dtype) for t in thru),
        in_specs=[_HBM] * (2 * nw) + [_SEM, _SEM, _ANY], out_specs=(_HBM,) * (2 * nw),
        input_output_aliases={i: i for i in range(2 * nw)},
        compiler_params=pltpu.CompilerParams(has_side_effects=_DATAFLOW),
    )(*thru, send_sems, recv_sems, after)
    return outs[nw:]


def _reduce_grads(bigs, g_small):
    big_shapes = [g.shape for g in bigs]
    nb = len(bigs)
    half_small = SMALL_ROWS // 2
    row_block = 32

    def body(*refs):
        gs, gsm_ref = refs[:nb], refs[nb]
        outs, osm_ref = refs[nb + 1:2 * nb + 1], refs[2 * nb + 1]
        scr = refs[2 * nb + 2:]
        ras, ra_sm = scr[:nb], scr[nb]
        sbs, rbs, p_sm = scr[nb + 1:2 * nb + 1], scr[2 * nb + 1:3 * nb + 1], scr[3 * nb + 1]
        sa_s, sa_r, sb_s, sb_r, sc_s, sc_r = scr[3 * nb + 2:]
        x, y, c, chip, peers, peer_chip = _place()
        sib = (x, y, 1 - c)
        halves = [s[1] // 2 for s in big_shapes]

        def rows_of(cc, n):
            return pl.ds(pl.multiple_of(cc * n, 8), n)

        a_copies = []
        for w in range(nb):
            cp = _remote(gs[w].at[:, rows_of(1 - c, halves[w]), :], ras[w], sa_s.at[w], sa_r.at[w], sib)
            cp.start()
            a_copies.append(cp)
        cp = _remote(gsm_ref.at[rows_of(1 - c, half_small), :], ra_sm, sa_s.at[nb], sa_r.at[nb], sib)
        cp.start()
        a_copies.append(cp)
        for cp in a_copies:
            cp.wait_recv()

        b_copies = []
        for w in range(nb):
            n = halves[w]
            for m in range(3):
                k = peer_chip[m]

                def chip_sum(i, carry, w=w, m=m, k=k, n=n):
                    r0 = pl.multiple_of(i * row_block, row_block)
                    mine = gs[w][k, pl.ds(pl.multiple_of(c * n + r0, 8), row_block), :]
                    sbs[w][m, pl.ds(r0, row_block), :] = (mine + ras[w][k, pl.ds(r0, row_block), :]).astype(BF16)
                    return carry
                lax.fori_loop(0, n // row_block, chip_sum, 0)
                cp = _remote(sbs[w].at[m], rbs[w].at[m], sb_s.at[3 * w + m], sb_r.at[3 * w + m],
                             (peers[m][0], peers[m][1], c))
                cp.start()
                b_copies.append(cp)
        p_sm[chip] = gsm_ref[rows_of(c, half_small), :] + ra_sm[...]
        for m in range(3):
            cp = _remote(p_sm.at[chip], p_sm.at[chip], sb_s.at[3 * nb + m], sb_r.at[3 * nb + m],
                         (peers[m][0], peers[m][1], c))
            cp.start()
            b_copies.append(cp)

        c_copies = []
        for w in range(nb):
            n = halves[w]
            for m in range(3):
                _remote(sbs[w].at[m], rbs[w].at[m], sb_s.at[3 * w + m], sb_r.at[3 * w + m],
                        (peers[m][0], peers[m][1], c)).wait_recv()

            def total(i, carry, w=w, n=n):
                r0 = pl.multiple_of(i * row_block, row_block)
                rows = pl.ds(r0, row_block)
                dst = pl.ds(pl.multiple_of(c * n + r0, 8), row_block)
                acc = gs[w][chip, dst, :] + ras[w][chip, rows, :]
                for m in range(3):
                    acc = acc + rbs[w][m, rows, :].astype(F32)
                outs[w][dst, :] = acc
                return carry
            lax.fori_loop(0, n // row_block, total, 0)
            mine = outs[w].at[rows_of(c, n), :]
            cp = _remote(mine, mine, sc_s.at[w], sc_r.at[w], sib)
            cp.start()
            c_copies.append(cp)
        for m in range(3):
            _remote(p_sm.at[chip], p_sm.at[peer_chip[m]], sb_s.at[3 * nb + m], sb_r.at[3 * nb + m],
                    (peers[m][0], peers[m][1], c)).wait_recv()
        osm_ref[rows_of(c, half_small), :] = (p_sm[0] + p_sm[1]) + (p_sm[2] + p_sm[3])
        mine = osm_ref.at[rows_of(c, half_small), :]
        cp = _remote(mine, mine, sc_s.at[nb], sc_r.at[nb], sib)
        cp.start()
        c_copies.append(cp)
        for w in range(nb):
            theirs = outs[w].at[rows_of(1 - c, halves[w]), :]
            _remote(theirs, theirs, sc_s.at[w], sc_r.at[w], sib).wait_recv()
        theirs = osm_ref.at[rows_of(1 - c, half_small), :]
        _remote(theirs, theirs, sc_s.at[nb], sc_r.at[nb], sib).wait_recv()
        for cp in a_copies + b_copies + c_copies:
            cp.wait_send()

    vmem = pl.BlockSpec(memory_space=pltpu.VMEM)
    scratch = ([pltpu.VMEM((N_CHIPS, s[1] // 2, s[2]), F32) for s in big_shapes]
               + [pltpu.VMEM((half_small, LANES), F32)]
               + [pltpu.VMEM((3, s[1] // 2, s[2]), BF16) for s in big_shapes]
               + [pltpu.VMEM((3, s[1] // 2, s[2]), BF16) for s in big_shapes]
               + [pltpu.VMEM((N_CHIPS, half_small, LANES), F32)]
               + [pltpu.SemaphoreType.DMA((nb + 1,)), pltpu.SemaphoreType.DMA((nb + 1,)),
                  pltpu.SemaphoreType.DMA((3 * nb + 3,)), pltpu.SemaphoreType.DMA((3 * nb + 3,)),
                  pltpu.SemaphoreType.DMA((nb + 1,)), pltpu.SemaphoreType.DMA((nb + 1,))])
    return pl.pallas_call(
        body,
        out_shape=[jax.ShapeDtypeStruct(s[1:], F32) for s in big_shapes]
        + [jax.ShapeDtypeStruct((SMALL_ROWS, LANES), F32)],
        in_specs=[vmem] * (nb + 1), out_specs=[vmem] * (nb + 1), scratch_shapes=scratch,
        compiler_params=pltpu.CompilerParams(vmem_limit_bytes=VMEM_LIMIT),
        name="reduce_grads")(*bigs, g_small)


def _reduce_first(g_stack):
    _, rows, cols = g_stack.shape
    n = rows // 2
    row_block = 32

    def body(g_ref, send_ref, own_ref, ra, s_sem, r_sem):
        x, y, c, chip, peers, peer_chip = _place()
        cp = _remote(g_ref.at[:, pl.ds(pl.multiple_of((1 - c) * n, 8), n), :], ra, s_sem, r_sem, (x, y, 1 - c))
        cp.start()
        cp.wait_recv()

        def sums(i, carry):
            r0 = pl.multiple_of(i * row_block, row_block)
            blk = pl.ds(r0, row_block)
            mine = pl.ds(pl.multiple_of(c * n + r0, 8), row_block)
            for m in range(3):
                send_ref[m, blk, :] = (g_ref[peer_chip[m], mine, :] + ra[peer_chip[m], blk, :]).astype(BF16)
            own_ref[blk, :] = g_ref[chip, mine, :] + ra[chip, blk, :]
            return carry
        lax.fori_loop(0, n // row_block, sums, 0)
        cp.wait_send()

    vmem = pl.BlockSpec(memory_space=pltpu.VMEM)
    return pl.pallas_call(
        body, out_shape=[jax.ShapeDtypeStruct((3, n, cols), BF16), jax.ShapeDtypeStruct((n, cols), F32)],
        in_specs=[vmem], out_specs=[vmem] * 2,
        scratch_shapes=[pltpu.VMEM((N_CHIPS, n, cols), F32), pltpu.SemaphoreType.DMA, pltpu.SemaphoreType.DMA],
        compiler_params=pltpu.CompilerParams(vmem_limit_bytes=VMEM_LIMIT), name="reduce_first")(g_stack)


def _reduce_last(own, landed):
    n, cols = own.shape
    row_block = 32

    def body(own_ref, land_ref, out_ref, s_sem, r_sem):
        x, y, c, chip, peers, peer_chip = _place()

        def total(i, carry):
            r0 = pl.multiple_of(i * row_block, row_block)
            blk = pl.ds(r0, row_block)
            acc = own_ref[blk, :]
            for m in range(3):
                acc = acc + land_ref[m, blk, :].astype(F32)
            out_ref[pl.ds(pl.multiple_of(c * n + r0, 8), row_block), :] = acc
            return carry
        lax.fori_loop(0, n // row_block, total, 0)
        mine = out_ref.at[pl.ds(pl.multiple_of(c * n, 8), n), :]
        theirs = out_ref.at[pl.ds(pl.multiple_of((1 - c) * n, 8), n), :]
        cp = _remote(mine, mine, s_sem, r_sem, (x, y, 1 - c))
        cp.start()
        _remote(theirs, theirs, s_sem, r_sem, (x, y, 1 - c)).wait_recv()
        cp.wait_send()

    vmem = pl.BlockSpec(memory_space=pltpu.VMEM)
    return pl.pallas_call(
        body, out_shape=jax.ShapeDtypeStruct((2 * n, cols), F32),
        in_specs=[vmem] * 2, out_specs=vmem,
        scratch_shapes=[pltpu.SemaphoreType.DMA, pltpu.SemaphoreType.DMA],
        compiler_params=pltpu.CompilerParams(vmem_limit_bytes=VMEM_LIMIT), name="reduce_last")(own, landed)


_SMALL_PARTS = (("g_norm", 8, 8), ("w_s", 512, 512), ("b_s", 4, 8), ("g_v", 2, 8), ("g_mem", 8, 8),
                ("g_final", 8, 8), ("loss", 1, 8))
_LOSS_ROW = SMALL_ROWS - 8
assert sum(p for _, _, p in _SMALL_PARTS) == SMALL_ROWS


def _pack_small(parts, loss=None):
    loss_row = jnp.zeros((1, LANES), F32) if loss is None else jnp.broadcast_to(loss.reshape(1, 1), (1, LANES))
    rows = []
    for (name, used, padded), p in zip(_SMALL_PARTS, list(parts) + [loss_row]):
        p = p.reshape(used, LANES)
        if padded > used:
            p = jnp.pad(p, ((0, padded - used), (0, 0)))
        rows.append(p)
    return jnp.concatenate(rows, axis=0)


def _unpack_small(packed, shapes):
    out = []
    off = 0
    for (name, used, padded), shape in zip(_SMALL_PARTS, shapes):
        out.append(packed[off:off + used].reshape(shape))
        off += padded
    return out


def _local_step(x, mem, target, g_norm, w_in, w_s, b_s, g_v, g_mem, late_weights, g_final,
                fwd_token=None, on_dw=None):
    B, S, _ = x.shape
    x2d = x.reshape(B * S, D_MODEL)
    t2d = target.reshape(B * S, D_MODEL)
    mem2d = mem.reshape(B * N_MEM, D_MODEL)

    proj = _inproj_fwd(x2d, g_norm if fwd_token is None else g_norm + fwd_token[0:1, 0:1], w_in)
    w_kv, w_out = late_weights(proj)
    kv = _kv_fwd(mem2d, g_mem, w_kv)
    a, lse = _attn_fwd(proj, B, S)
    w_sT = jnp.swapaxes(w_s, 1, 2)
    b_tab = jnp.repeat(b_s.T, HEAD_DIM, axis=1)
    (dx2, da, drest, loss, d_wout, d_ws, d_bs, d_gv, d_gf, dkv) = _mid(
        x2d, t2d, a, proj, kv, w_s, w_sT, b_tab, g_v, w_out, g_final, B, S)
    dq, dk, dv = _attn_bwd(proj, a, lse, da, B, S)
    d_win = _inproj_bwd_dw(dq, dk, dv, drest, x2d, g_norm)
    g_norm_dx = g_norm if on_dw is None else g_norm + on_dw(d_win)[0:1, 0:1]
    grad_x, d_gnorm = _inproj_bwd_dx(dq, dk, dv, drest, x2d, dx2, g_norm_dx, w_in)
    d_wkv, d_gmem = _kv_bwd(mem2d, g_mem, w_kv, dkv)
    d_bs = d_bs[:, :N_SGU_GROUPS].T
    return (loss[0, 0], grad_x.reshape(B, S, D_MODEL),
            dict(g_norm=d_gnorm, w_in=d_win, w_s=d_ws, b_s=d_bs, g_v=d_gv, g_mem=d_gmem, w_kv=d_wkv,
                 w_out=d_wout, g_final=d_gf))


def kernel(x, mem, g_norm, w_in, w_sgu_spatial, b_sgu_spatial, g_sgu_v, g_mem, w_mem_kv, w_out, g_final, loss_target, m_g_norm, m_w_in, m_w_sgu_spatial, m_b_sgu_spatial, m_g_sgu_v, m_g_mem, m_w_mem_kv, m_w_out, m_g_final, v_g_norm, v_w_in, v_w_sgu_spatial, v_b_sgu_spatial, v_g_sgu_v, v_g_mem, v_w_mem_kv, v_w_out, v_g_final):
    t = lambda w: jnp.swapaxes(w[0], 0, 1)
    chip = 2 * lax.axis_index("x") + lax.axis_index("y")
    (win_all,) = _ag_weights([t(w_in)])
    w_in_full = win_all.reshape(-1, win_all.shape[-1])
    late_shards = [w_mem_kv[0].astype(BF16), w_out[0].astype(BF16)]
    late = _exchange_start(True, late_shards, win_all, "gather_late_start")

    def late_weights(proj):
        full = []
        for shard, landed in zip(late_shards, _exchange_wait(True, late, proj, "gather_late_wait")):
            landed = lax.dynamic_update_slice(landed, shard[None], (chip, 0, 0))
            full.append(landed.reshape(-1, landed.shape[-1]))
        return full

    scatter = {}

    def on_dw(d_win):
        send, scatter["own"] = _reduce_first(d_win.reshape((N_CHIPS, w_in.shape[2], w_in.shape[1])))
        scatter["started"] = _exchange_start(False, [send], scatter["own"], "scatter_in_start")
        return scatter["started"][-1]

    loss, grad_x, g = _local_step(
        x, mem, loss_target, g_norm, w_in_full, w_sgu_spatial[0], b_sgu_spatial[0], g_sgu_v, g_mem,
        late_weights, g_final.reshape(1, D_MODEL), fwd_token=late[-1], on_dw=on_dw)

    small_names = ("g_norm", "w_s", "b_s", "g_v", "g_mem", "g_final")
    g_kv_stack = g["w_kv"].reshape((N_CHIPS,) + w_mem_kv.shape[1:])
    g_out_stack = g["w_out"].reshape((N_CHIPS,) + w_out.shape[1:])
    gr_kv, gr_out, gr_small = _reduce_grads(
        [g_kv_stack, g_out_stack], _pack_small([g[n] for n in small_names], loss))
    loss = gr_small[_LOSS_ROW, 0]
    (landed,) = _exchange_wait(False, scatter["started"], gr_small, "scatter_in_wait")
    gr_in = _reduce_last(scatter["own"], landed)

    small_w = (g_norm, w_sgu_spatial, b_sgu_spatial, g_sgu_v, g_mem, g_final)
    small_m = (m_g_norm, m_w_sgu_spatial, m_b_sgu_spatial, m_g_sgu_v, m_g_mem, m_g_final)
    small_v = (v_g_norm, v_w_sgu_spatial, v_b_sgu_spatial, v_g_sgu_v, v_g_mem, v_g_final)
    shapes = [w.shape for w in small_w]
    d_small, nm_small, nv_small = _adamw(_pack_small(small_w), gr_small, _pack_small(small_m),
                                         _pack_small(small_v), "adamw_small")
    d_in, nm_in, nv_in = _adamw(t(w_in), gr_in, t(m_w_in), t(v_w_in), "adamw_w_in")
    gr_in, d_in, nm_in, nv_in = [jnp.swapaxes(z, 0, 1) for z in (gr_in, d_in, nm_in, nv_in)]
    d_kv, nm_kv, nv_kv = _adamw(w_mem_kv[0], gr_kv, m_w_mem_kv[0], v_w_mem_kv[0], "adamw_w_kv")
    d_out, nm_out, nv_out = _adamw(w_out[0], gr_out, m_w_out[0], v_w_out[0], "adamw_w_out")

    def leaves(packed, big_in, big_kv, big_out):
        s_norm, s_ws, s_bs, s_gv, s_gmem, s_gf = _unpack_small(packed, shapes)
        return [s_norm, big_in[None], s_ws, s_bs, s_gv, s_gmem, big_kv[None], big_out[None], s_gf]

    return (loss, grad_x, *leaves(gr_small, gr_in, gr_kv, gr_out), *leaves(d_small, d_in, d_kv, d_out),
            *leaves(nm_small, nm_in, nm_kv, nm_out), *leaves(nv_small, nv_in, nv_kv, nv_out))
```

```python
import functools

import jax
import jax.numpy as jnp
from jax import lax
from jax.experimental import pallas as pl
from jax.experimental.pallas import tpu as pltpu

F32 = jnp.float32
BF16 = jnp.bfloat16
MESH = pl.DeviceIdType.MESH

D_MODEL = 1024
ATTN_WIDTH = 512
SGU_WIDTH = 256
MEM_WIDTH = 256
N_MEM = 256
IN_COLS = 3328
QKV_COLS = 3 * ATTN_WIDTH
REST_COLS = IN_COLS - QKV_COLS
SGU_CHUNK = 128
N_SGU_GROUPS = 4
EPS = 1e-6
NEG_INF = -1e30
DILATIONS = (1, 4, 16)
RADIUS = 64
Q_BLOCK = 128
LANES = 128
HEAD_DIM = 64

ADAM_LR = 0.001
ADAM_B1 = 0.9
ADAM_B2 = 0.999
ADAM_EPS = 1e-08
ADAM_WD = 0.01
ADAM_STEP = 10

N_CHIPS = 4
VMEM_LIMIT = 56 * 1024 * 1024
SMALL_ROWS = 560


def _params(sem=None, vmem=VMEM_LIMIT):
    return pltpu.CompilerParams(dimension_semantics=sem, vmem_limit_bytes=vmem)


def _nn(a, b):
    return jnp.dot(a, b, preferred_element_type=F32)


def _nt(a, b):
    return lax.dot_general(a, b, (((1,), (1,)), ((), ())), preferred_element_type=F32)


def _tn(a, b):
    return lax.dot_general(a, b, (((0,), (0,)), ((), ())), preferred_element_type=F32)


def _rms(x):
    r = lax.rsqrt(jnp.mean(x * x, axis=-1, keepdims=True) + EPS)
    return r, x * r


def _head_masks():
    lane = lax.broadcasted_iota(jnp.int32, (1, LANES), 1)
    lo = lane < HEAD_DIM
    return lo, (lo.astype(F32), (~lo).astype(F32))


def _silu_parts(z):
    s = jax.nn.sigmoid(z)
    return z * s, s * (1.0 + z * (1.0 - s))


def _gelu_parts(x):
    c = 0.7978845608028654
    x2 = x * x
    t = jnp.tanh(c * (x + 0.044715 * (x * x2)))
    val = 0.5 * x * (1.0 + t)
    grad = 0.5 * (1.0 + t) + 0.5 * x * (1.0 - t * t) * (c * (1.0 + 3.0 * 0.044715 * x2))
    return val, grad


def _inproj_fwd(x2d, g_norm, w_in_t):
    T = x2d.shape[0]
    tm = 512

    def body(x_ref, g_ref, w_ref, o_ref):
        _, xh = _rms(x_ref[...])
        h = (xh * g_ref[...]).astype(BF16)
        o_ref[...] = _nt(h, w_ref[...])

    return pl.pallas_call(
        body, grid=(T // tm,),
        in_specs=[pl.BlockSpec((tm, D_MODEL), lambda i: (i, 0)),
                  pl.BlockSpec((1, D_MODEL), lambda i: (0, 0)),
                  pl.BlockSpec((IN_COLS, D_MODEL), lambda i: (0, 0))],
        out_specs=pl.BlockSpec((tm, IN_COLS), lambda i: (i, 0)),
        out_shape=jax.ShapeDtypeStruct((T, IN_COLS), F32),
        compiler_params=_params(("arbitrary",)), name="inproj_fwd")(x2d, g_norm, w_in_t)


def _kv_fwd(mem2d, g_mem, w_kv):
    Tm = mem2d.shape[0]

    def body(m_ref, g_ref, w_ref, o_ref):
        _, mh = _rms(m_ref[...])
        o_ref[...] = _nn((mh * g_ref[...]).astype(BF16), w_ref[...])

    return pl.pallas_call(
        body, out_shape=jax.ShapeDtypeStruct((Tm, 2 * MEM_WIDTH), F32),
        compiler_params=_params(), name="kv_fwd")(mem2d, g_mem, w_kv)


def _kv_bwd(mem2d, g_mem, w_kv, dkv):
    Tm = mem2d.shape[0]

    def body(m_ref, g_ref, w_ref, dkv_ref, dw_ref, dg_ref):
        _, mh = _rms(m_ref[...])
        memn = (mh * g_ref[...]).astype(BF16)
        dkvb = dkv_ref[...].astype(BF16)
        dw_ref[...] = _tn(memn, dkvb)
        dmemn = _nt(dkvb, w_ref[...])
        dg_ref[...] = jnp.sum(dmemn * mh, axis=0, keepdims=True)

    return pl.pallas_call(
        body, out_shape=(jax.ShapeDtypeStruct((D_MODEL, 2 * MEM_WIDTH), F32),
                         jax.ShapeDtypeStruct((1, D_MODEL), F32)),
        compiler_params=_params(), name="kv_bwd")(mem2d, g_mem, w_kv, dkv)


def _attn_geometry(S):
    geom = []
    for d in DILATIONS:
        L = S // d
        assert L % Q_BLOCK == 0
        geom.append((d, L, min(2 * Q_BLOCK, L), L // Q_BLOCK))
    return geom


def _init_bias(bias_scr, geom, hp):
    row = lax.broadcasted_iota(jnp.int32, (Q_BLOCK, 2 * Q_BLOCK), 0)
    col = lax.broadcasted_iota(jnp.int32, (Q_BLOCK, 2 * Q_BLOCK), 1)
    for j in (0, 1):
        bits = (126 - (2 * hp + j)) * (1 << 23)
        slope = lax.bitcast_convert_type(jnp.full((1, 1), bits, jnp.int32), F32)
        for di, (d, _, _, _) in enumerate(geom):
            for cls, off in enumerate((0, -RADIUS, -2 * RADIUS)):
                dist = jnp.abs(col - row + off)
                bias_scr[di * 6 + cls * 2 + j] = jnp.where(
                    dist <= RADIUS, -(slope * float(d)) * dist.astype(F32), NEG_INF)


def _block_slices(d, L, KW, nqb, r, qb):
    qs = qb * Q_BLOCK
    ks = jnp.clip(qs - RADIUS, 0, L - KW)
    cls = jnp.where(qb == 0, 0, jnp.where(qb == nqb - 1, 2, 1))
    if d == 1:
        qsl = pl.ds(pl.multiple_of(qs, Q_BLOCK), Q_BLOCK)
        ksl = pl.ds(pl.multiple_of(ks, RADIUS), KW)
    else:
        qsl = pl.ds(r + qs * d, Q_BLOCK, stride=d)
        ksl = pl.ds(r + ks * d, KW, stride=d)
    return qsl, ksl, cls


def _for_groups(geom, group, fn):
    for di, (d, L, KW, nqb) in enumerate(geom):
        assert (d * nqb) % group == 0

        def step(it, carry, di=di, d=d, L=L, KW=KW, nqb=nqb):
            slices = []
            for g in range(group):
                i = it * group + g
                slices.append(_block_slices(d, L, KW, nqb, i // nqb, i % nqb))
            fn(di, KW, slices)
            return carry
        lax.fori_loop(0, d * nqb // group, step, 0)


def _attn_fwd(proj, B, S):
    T = B * S
    geom = _attn_geometry(S)
    n_pairs = ATTN_WIDTH // LANES

    def body(q_ref, k_ref, v_ref, a_ref, lse_ref, bias_scr, *per_dilation):
        o_scr, m_scr, l_scr = per_dilation[0:3], per_dilation[3:6], per_dilation[6:9]
        lo, hm = _head_masks()
        _init_bias(bias_scr, geom, pl.program_id(1))

        def group(di, KW, slices):
            chains = [(g, j) for g in range(len(slices)) for j in (0, 1)]
            q = [q_ref[qsl, :] for qsl, _, _ in slices]
            kw = [k_ref[ksl, :].astype(BF16) for _, ksl, _ in slices]
            vw = [v_ref[ksl, :].astype(BF16) for _, ksl, _ in slices]
            s = {(g, j): _nt((q[g] * (hm[j] * 0.125)).astype(BF16), kw[g])
                 + bias_scr[di * 6 + slices[g][2] * 2 + j, :, pl.ds(0, KW)] for g, j in chains}
            m = {c: jnp.max(s[c], axis=1, keepdims=True) for c in chains}
            p = {c: jnp.exp(s[c] - m[c]) for c in chains}
            l = {c: jnp.sum(p[c], axis=1, keepdims=True) for c in chains}
            o = {(g, j): _nn(p[(g, j)].astype(BF16), vw[g]) for g, j in chains}
            for g, (qsl, _, _) in enumerate(slices):
                o_scr[di][qsl, :] = jnp.where(lo, o[(g, 0)], o[(g, 1)])
                m_scr[di][qsl, :] = jnp.where(lo, m[(g, 0)], m[(g, 1)])
                l_scr[di][qsl, :] = jnp.where(lo, l[(g, 0)], l[(g, 1)])

        _for_groups(geom, 8, group)

        rows_per = 256

        def combine(i, carry):
            rows = pl.ds(pl.multiple_of(i * rows_per, rows_per), rows_per)
            ms = [m_scr[di][rows, :] for di in range(3)]
            mx = jnp.maximum(jnp.maximum(ms[0], ms[1]), ms[2])
            num = 0.0
            den = 0.0
            for di in range(3):
                w = jnp.exp(ms[di] - mx)
                num = num + w * o_scr[di][rows, :]
                den = den + w * l_scr[di][rows, :]
            a_ref[rows, :] = num / den
            lse_ref[rows, :] = mx + jnp.log(den)
            return carry

        lax.fori_loop(0, S // rows_per, combine, 0)

    blk = lambda off: pl.BlockSpec((S, LANES), lambda b, h, off=off: (b, off + h))
    out_blk = pl.BlockSpec((S, LANES), lambda b, h: (b, h))
    return pl.pallas_call(
        body, grid=(B, n_pairs),
        in_specs=[blk(0), blk(n_pairs), blk(2 * n_pairs)],
        out_specs=[out_blk, out_blk],
        out_shape=[jax.ShapeDtypeStruct((T, ATTN_WIDTH), F32)] * 2,
        scratch_shapes=[pltpu.VMEM((18, Q_BLOCK, 2 * Q_BLOCK), F32)] + [pltpu.VMEM((S, LANES), F32)] * 9,
        compiler_params=_params(("arbitrary", "arbitrary")), name="attn_fwd")(proj, proj, proj)


def _attn_bwd(proj, a, lse, da, B, S):
    T = B * S
    geom = _attn_geometry(S)
    n_pairs = ATTN_WIDTH // LANES

    def body(q_ref, k_ref, v_ref, a_ref, lse_ref, do_ref, dq_ref, dk_ref, dv_ref,
             bias_scr, dq_scr, dk_scr, dv_scr):
        _, hm = _head_masks()
        _init_bias(bias_scr, geom, pl.program_id(1))
        dq_scr[...] = jnp.zeros_like(dq_scr)
        dk_scr[...] = jnp.zeros_like(dk_scr)
        dv_scr[...] = jnp.zeros_like(dv_scr)

        def group(di, KW, slices):
            n = len(slices)
            chains = [(g, j) for g in range(n) for j in (0, 1)]
            q = [q_ref[qsl, :] for qsl, _, _ in slices]
            do = [do_ref[qsl, :] for qsl, _, _ in slices]
            doa = [do[g] * a_ref[slices[g][0], :] for g in range(n)]
            lse_q = [lse_ref[qsl, :] for qsl, _, _ in slices]
            kw = [k_ref[ksl, :].astype(BF16) for _, ksl, _ in slices]
            vw = [v_ref[ksl, :].astype(BF16) for _, ksl, _ in slices]
            qj = {(g, j): (q[g] * (hm[j] * 0.125)).astype(BF16) for g, j in chains}
            doj = {(g, j): (do[g] * hm[j]).astype(BF16) for g, j in chains}
            s = {(g, j): _nt(qj[(g, j)], kw[g])
                 + bias_scr[di * 6 + slices[g][2] * 2 + j, :, pl.ds(0, KW)] for g, j in chains}
            dp = {(g, j): _nt(doj[(g, j)], vw[g]) for g, j in chains}
            dsum = {(g, j): jnp.sum(doa[g] * hm[j], axis=1, keepdims=True) for g, j in chains}
            p = {(g, j): jnp.exp(s[(g, j)] - lse_q[g][:, HEAD_DIM * j:HEAD_DIM * j + 1]) for g, j in chains}
            ds = {c: (p[c] * (dp[c] - dsum[c])).astype(BF16) for c in chains}
            pb = {c: p[c].astype(BF16) for c in chains}
            dq = [_nn(ds[(g, 0)], kw[g]) * (hm[0] * 0.125) + _nn(ds[(g, 1)], kw[g]) * (hm[1] * 0.125)
                  for g in range(n)]
            both = lambda t, g: jnp.concatenate([t[(g, 0)], t[(g, 1)]], axis=0)
            dkw = [_tn(both(ds, g), both(qj, g)) for g in range(n)]
            dvw = [_tn(both(pb, g), both(doj, g)) for g in range(n)]
            for g, (qsl, ksl, _) in enumerate(slices):
                dq_scr[qsl, :] = dq_scr[qsl, :] + dq[g]
                dk_scr[ksl, :] = dk_scr[ksl, :] + dkw[g]
                dv_scr[ksl, :] = dv_scr[ksl, :] + dvw[g]

        _for_groups(geom, 4, group)
        dq_ref[...] = dq_scr[...].astype(BF16)
        dk_ref[...] = dk_scr[...].astype(BF16)
        dv_ref[...] = dv_scr[...].astype(BF16)

    blk = lambda off: pl.BlockSpec((S, LANES), lambda b, h, off=off: (b, off + h))
    return pl.pallas_call(
        body, grid=(B, n_pairs),
        in_specs=[blk(0), blk(n_pairs), blk(2 * n_pairs), blk(0), blk(0), blk(0)],
        out_specs=[blk(0), blk(0), blk(0)],
        out_shape=[jax.ShapeDtypeStruct((T, ATTN_WIDTH), BF16)] * 3,
        scratch_shapes=[pltpu.VMEM((18, Q_BLOCK, 2 * Q_BLOCK), F32),
                        pltpu.VMEM((S, LANES), F32),
                        pltpu.VMEM((S, LANES), F32),
                        pltpu.VMEM((S, LANES), F32)],
        compiler_params=_params(("arbitrary", "arbitrary")), name="attn_bwd")(proj, proj, proj, a, lse, da)


def _mid(x2d, t2d, a, proj, kv, w_s, w_sT, b_tab, g_v, w_out, g_final, B, S):
    T = B * S
    tm = 512
    nt = S // tm
    n_sub = 1
    sub = tm // n_sub
    n_chunks = sub // SGU_CHUNK

    def body(*refs):
        tiles_in, consts, tiles_out, accs = refs[:9], refs[9:16], refs[16:19], refs[19:]
        for sb in range(n_sub):
            rows = lambda r: r.at[pl.ds(sb * sub, sub), :]
            sub_body(*[rows(r) for r in tiles_in], *consts, *[rows(r) for r in tiles_out], *accs, sb=sb)

    def sub_body(x_ref, t_ref, a_ref, za_ref, ub_ref, vb_ref, zb_ref, qm_ref, zm_ref, kv_ref,
                 ws_ref, wsT_ref, btab_ref, gv_ref, wout_ref, gf_ref,
                 dx2_ref, da_ref, drest_ref, loss_ref, dwout_ref, dws_ref, dbs_ref, dgv_ref, dgf_ref, dkv_ref,
                 dbtab_scr, sb):
        b = pl.program_id(0)
        t = pl.program_id(1)
        first = jnp.logical_and(jnp.logical_and(b == 0, t == 0), sb == 0)
        last = jnp.logical_and(jnp.logical_and(b == B - 1, t == nt - 1), sb == n_sub - 1)
        _, hm = _head_masks()
        lane_g = lax.broadcasted_iota(jnp.int32, (1, SGU_WIDTH), 1) // HEAD_DIM
        gm = [(lane_g == g).astype(F32) for g in range(N_SGU_GROUPS)]

        @pl.when(first)
        def _():
            loss_ref[...] = jnp.zeros_like(loss_ref)
            dwout_ref[...] = jnp.zeros_like(dwout_ref)
            dws_ref[...] = jnp.zeros_like(dws_ref)
            dbs_ref[...] = jnp.zeros_like(dbs_ref)
            dgv_ref[...] = jnp.zeros_like(dgv_ref)
            dgf_ref[...] = jnp.zeros_like(dgf_ref)
            dbtab_scr[...] = jnp.zeros_like(dbtab_scr)

        @pl.when(jnp.logical_and(t == 0, sb == 0))
        def _():
            dkv_ref[...] = jnp.zeros_like(dkv_ref)

        sil_a, dsil_a = _silu_parts(za_ref[...])
        a_val = a_ref[...]
        gated_a = sil_a * a_val

        u, du_dub = _gelu_parts(ub_ref[...])
        vv, dvv_dvb = _gelu_parts(vb_ref[...])
        rv, vhat = _rms(vv)
        gv = gv_ref[...]
        vn = (vhat * gv).astype(BF16)
        ws = [ws_ref[g].astype(BF16) for g in range(N_SGU_GROUPS)]
        wsT = [wsT_ref[g].astype(BF16) for g in range(N_SGU_GROUPS)]
        mixed = []
        for ci in range(n_chunks):
            vn_c = vn[ci * SGU_CHUNK:(ci + 1) * SGU_CHUNK, :]
            mc = btab_ref[...]
            for g in range(N_SGU_GROUPS):
                mc = mc + gm[g] * _nn(ws[g], vn_c)
            mixed.append(mc)
        mixed = jnp.concatenate(mixed, axis=0)
        sg = u * mixed
        sil_b, dsil_b = _silu_parts(zb_ref[...])
        gated_b = sil_b * sg

        kvv = kv_ref[...].astype(BF16)
        qm = qm_ref[...]
        probs, qs_m, mo_pairs = [], [], []
        for pr in range(2):
            cols = slice(pr * LANES, (pr + 1) * LANES)
            kp = kvv[:, pr * LANES:(pr + 1) * LANES]
            vp = kvv[:, MEM_WIDTH + pr * LANES:MEM_WIDTH + (pr + 1) * LANES]
            mo_p = 0.0
            for j in (0, 1):
                qj = (qm[:, cols] * (hm[j] * 0.125)).astype(BF16)
                s = _nt(qj, kp)
                e = jnp.exp(s - jnp.max(s, axis=1, keepdims=True))
                p = e / jnp.sum(e, axis=1, keepdims=True)
                mo_p = mo_p + _nn(p.astype(BF16), vp) * hm[j]
                probs.append(p)
                qs_m.append(qj)
            mo_pairs.append(mo_p)
        mo = jnp.concatenate(mo_pairs, axis=1)
        sil_m, dsil_m = _silu_parts(zm_ref[...])
        gated_m = sil_m * mo

        gated = jnp.concatenate([gated_a, gated_b, gated_m], axis=1).astype(BF16)
        wout = wout_ref[...]
        x2 = x_ref[...] + _nn(gated, wout)
        r2, xh2 = _rms(x2)
        gf = gf_ref[...]
        err = xh2 * gf - t_ref[...]
        loss_ref[...] += jnp.sum(err * err) * (0.5 / D_MODEL)

        dy = err * (1.0 / D_MODEL)
        dgf_ref[...] += jnp.sum(dy * xh2, axis=0, keepdims=True)
        gdy = dy * gf
        dx2 = r2 * (gdy - xh2 * jnp.mean(gdy * xh2, axis=1, keepdims=True))
        dx2_ref[...] = dx2
        dx2b = dx2.astype(BF16)
        dwout_ref[...] += _tn(gated, dx2b)
        dgated = _nt(dx2b, wout)
        dga = dgated[:, 0:ATTN_WIDTH]
        dgb = dgated[:, ATTN_WIDTH:ATTN_WIDTH + SGU_WIDTH]
        dgm = dgated[:, ATTN_WIDTH + SGU_WIDTH:]

        da_ref[...] = dga * sil_a
        dza = dga * a_val * dsil_a

        dsg = dgb * sil_b
        dzb = dgb * sg * dsil_b
        dub = dsg * mixed * du_dub
        dmixed = dsg * u
        dmixed_b = dmixed.astype(BF16)
        dvn = []
        dbtab = dbtab_scr[...]
        for ci in range(n_chunks):
            rows = slice(ci * SGU_CHUNK, (ci + 1) * SGU_CHUNK)
            dm_c = dmixed_b[rows, :]
            vn_c = vn[rows, :]
            dvn_c = 0.0
            for g in range(N_SGU_GROUPS):
                dvn_c = dvn_c + gm[g] * _nn(wsT[g], dm_c)
                dws_ref[g] += _nt((dmixed[rows, :] * gm[g]).astype(BF16), vn_c)
            dvn.append(dvn_c)
            dbtab = dbtab + dmixed[rows, :]
        dbtab_scr[...] = dbtab
        dvn = jnp.concatenate(dvn, axis=0)
        dgv_ref[...] += jnp.sum(dvn * vhat, axis=0, keepdims=True)
        tv = dvn * gv
        dvv = rv * (tv - vhat * jnp.mean(tv * vhat, axis=1, keepdims=True))
        dvb = dvv * dvv_dvb

        dmo = dgm * sil_m
        dzm = dgm * mo * dsil_m
        dqm_pairs = []
        dk_pairs, dv_pairs = [], []
        for pr in range(2):
            kp = kvv[:, pr * LANES:(pr + 1) * LANES]
            vp = kvv[:, MEM_WIDTH + pr * LANES:MEM_WIDTH + (pr + 1) * LANES]
            dmo_p = dmo[:, pr * LANES:(pr + 1) * LANES]
            dq_p = 0.0
            dk_p = 0.0
            dv_p = 0.0
            for j in (0, 1):
                p = probs[2 * pr + j]
                dmo_j = (dmo_p * hm[j]).astype(BF16)
                dp = _nt(dmo_j, vp)
                ds = (p * (dp - jnp.sum(dp * p, axis=1, keepdims=True))).astype(BF16)
                dq_p = dq_p + _nn(ds, kp) * (hm[j] * 0.125)
                dk_p = dk_p + _tn(ds, qs_m[2 * pr + j])
                dv_p = dv_p + _tn(p.astype(BF16), dmo_j)
            dqm_pairs.append(dq_p)
            dk_pairs.append(dk_p)
            dv_pairs.append(dv_p)
        dqm = jnp.concatenate(dqm_pairs, axis=1)
        dkv_ref[...] += jnp.concatenate(dk_pairs + dv_pairs, axis=1)

        drest_ref[...] = jnp.concatenate([dza, dub, dvb, dzb, dqm, dzm], axis=1).astype(BF16)

        @pl.when(last)
        def _():
            lane = lax.broadcasted_iota(jnp.int32, (1, LANES), 1)
            dbt = dbtab_scr[...]
            out = jnp.zeros((SGU_CHUNK, LANES), F32)
            for g in range(N_SGU_GROUPS):
                out = out + jnp.where(lane == g, jnp.sum(dbt * gm[g], axis=1, keepdims=True), 0.0)
            dbs_ref[...] = out

    tile = lambda w, cb: pl.BlockSpec((tm, w), lambda b, t, cb=cb: (b * nt + t, cb))
    const = lambda shape: pl.BlockSpec(shape, lambda b, t, n=len(shape): (0,) * n)
    return pl.pallas_call(
        body, grid=(B, nt),
        in_specs=[tile(D_MODEL, 0), tile(D_MODEL, 0), tile(ATTN_WIDTH, 0),
                  tile(ATTN_WIDTH, 3),
                  tile(SGU_WIDTH, 8), tile(SGU_WIDTH, 9), tile(SGU_WIDTH, 10),
                  tile(MEM_WIDTH, 11), tile(MEM_WIDTH, 12),
                  pl.BlockSpec((N_MEM, 2 * MEM_WIDTH), lambda b, t: (b, 0)),
                  const((N_SGU_GROUPS, SGU_CHUNK, SGU_CHUNK)), const((N_SGU_GROUPS, SGU_CHUNK, SGU_CHUNK)),
                  const((SGU_CHUNK, SGU_WIDTH)), const((1, SGU_WIDTH)),
                  const((D_MODEL, D_MODEL)), const((1, D_MODEL))],
        out_specs=[tile(D_MODEL, 0), tile(ATTN_WIDTH, 0), tile(REST_COLS, 0),
                   const((8, LANES)), const((D_MODEL, D_MODEL)),
                   const((N_SGU_GROUPS, SGU_CHUNK, SGU_CHUNK)), const((SGU_CHUNK, LANES)),
                   const((1, SGU_WIDTH)), const((1, D_MODEL)),
                   pl.BlockSpec((N_MEM, 2 * MEM_WIDTH), lambda b, t: (b, 0))],
        out_shape=[jax.ShapeDtypeStruct((T, D_MODEL), F32), jax.ShapeDtypeStruct((T, ATTN_WIDTH), F32),
                   jax.ShapeDtypeStruct((T, REST_COLS), BF16),
                   jax.ShapeDtypeStruct((8, LANES), F32), jax.ShapeDtypeStruct((D_MODEL, D_MODEL), F32),
                   jax.ShapeDtypeStruct((N_SGU_GROUPS, SGU_CHUNK, SGU_CHUNK), F32),
                   jax.ShapeDtypeStruct((SGU_CHUNK, LANES), F32),
                   jax.ShapeDtypeStruct((1, SGU_WIDTH), F32), jax.ShapeDtypeStruct((1, D_MODEL), F32),
                   jax.ShapeDtypeStruct((B * N_MEM, 2 * MEM_WIDTH), F32)],
        scratch_shapes=[pltpu.VMEM((SGU_CHUNK, SGU_WIDTH), F32)],
        compiler_params=_params(("arbitrary", "arbitrary")), name="mid")(
            x2d, t2d, a, proj, proj, proj, proj, proj, proj, kv, w_s, w_sT, b_tab, g_v, w_out, g_final)


def _inproj_bwd_dx(dq, dk, dv, drest, x2d, dx2, g_norm, w_in_t):
    T = x2d.shape[0]
    tm = 512
    W = ATTN_WIDTH

    def body(dq_ref, dk_ref, dv_ref, dr_ref, x_ref, dx2_ref, g_ref, w_ref, gx_ref, dg_ref):
        @pl.when(pl.program_id(0) == 0)
        def _():
            dg_ref[...] = jnp.zeros_like(dg_ref)

        dh = (_nn(dq_ref[...], w_ref[0:W, :]) + _nn(dk_ref[...], w_ref[W:2 * W, :])
              + _nn(dv_ref[...], w_ref[2 * W:3 * W, :]) + _nn(dr_ref[...], w_ref[QKV_COLS:IN_COLS, :]))
        r, xh = _rms(x_ref[...])
        dg_ref[...] += jnp.sum(dh * xh, axis=0, keepdims=True)
        th = dh * g_ref[...]
        gx_ref[...] = r * (th - xh * jnp.mean(th * xh, axis=1, keepdims=True)) + dx2_ref[...]

    tile = lambda w: pl.BlockSpec((tm, w), lambda i: (i, 0))
    return pl.pallas_call(
        body, grid=(T // tm,),
        in_specs=[tile(W), tile(W), tile(W), tile(REST_COLS), tile(D_MODEL), tile(D_MODEL),
                  pl.BlockSpec((1, D_MODEL), lambda i: (0, 0)),
                  pl.BlockSpec((IN_COLS, D_MODEL), lambda i: (0, 0))],
        out_specs=[tile(D_MODEL), pl.BlockSpec((1, D_MODEL), lambda i: (0, 0))],
        out_shape=[jax.ShapeDtypeStruct((T, D_MODEL), F32), jax.ShapeDtypeStruct((1, D_MODEL), F32)],
        compiler_params=_params(("arbitrary",)), name="inproj_bwd_dx")(dq, dk, dv, drest, x2d, dx2, g_norm, w_in_t)


def _inproj_bwd_dw(dq, dk, dv, drest, x2d, g_norm):
    T = x2d.shape[0]
    tm = 512
    W = ATTN_WIDTH

    def body(dq_ref, dk_ref, dv_ref, dr_ref, x_ref, g_ref, dw_ref):
        @pl.when(pl.program_id(0) == 0)
        def _():
            dw_ref[...] = jnp.zeros_like(dw_ref)

        _, xh = _rms(x_ref[...])
        h = (xh * g_ref[...]).astype(BF16)
        dw_ref[0:W, :] += _tn(dq_ref[...], h)
        dw_ref[W:2 * W, :] += _tn(dk_ref[...], h)
        dw_ref[2 * W:3 * W, :] += _tn(dv_ref[...], h)
        dw_ref[QKV_COLS:IN_COLS, :] += _tn(dr_ref[...], h)

    tile = lambda w: pl.BlockSpec((tm, w), lambda i: (i, 0))
    return pl.pallas_call(
        body, grid=(T // tm,),
        in_specs=[tile(W), tile(W), tile(W), tile(REST_COLS), tile(D_MODEL),
                  pl.BlockSpec((1, D_MODEL), lambda i: (0, 0))],
        out_specs=pl.BlockSpec((IN_COLS, D_MODEL), lambda i: (0, 0)),
        out_shape=jax.ShapeDtypeStruct((IN_COLS, D_MODEL), F32),
        compiler_params=_params(("arbitrary",)), name="inproj_bwd_dw")(dq, dk, dv, drest, x2d, g_norm)


def _adamw(w, g, m, v, name):
    R, C = w.shape
    br = max(r for r in range(8, 257, 8) if R % r == 0)

    def body(w_ref, g_ref, m_ref, v_ref, d_ref, nm_ref, nv_ref):
        gg = g_ref[...]
        nm = ADAM_B1 * m_ref[...] + (1.0 - ADAM_B1) * gg
        nv = ADAM_B2 * v_ref[...] + (1.0 - ADAM_B2) * (gg * gg)
        m_hat = nm / (1.0 - ADAM_B1 ** ADAM_STEP)
        v_hat = nv / (1.0 - ADAM_B2 ** ADAM_STEP)
        d_ref[...] = -ADAM_LR * (m_hat / (jnp.sqrt(v_hat) + ADAM_EPS) + ADAM_WD * w_ref[...])
        nm_ref[...] = nm
        nv_ref[...] = nv

    spec = pl.BlockSpec((br, C), lambda i: (i, 0))
    return pl.pallas_call(
        body, grid=(R // br,), in_specs=[spec] * 4, out_specs=[spec] * 3,
        out_shape=[jax.ShapeDtypeStruct((R, C), F32)] * 3,
        compiler_params=_params(("arbitrary",)), name=name)(w, g, m, v)


def _place():
    x, y, c = lax.axis_index("x"), lax.axis_index("y"), lax.axis_index("c")
    chip = 2 * x + y
    peers = [(x, 1 - y), (1 - x, y), (1 - x, 1 - y)]
    peer_chip = [2 * px + py for px, py in peers]
    return x, y, c, chip, peers, peer_chip


def _remote(src, dst, send_sem, recv_sem, dev):
    return pltpu.make_async_remote_copy(src_ref=src, dst_ref=dst, send_sem=send_sem, recv_sem=recv_sem,
                                        device_id=dev, device_id_type=MESH)


def _ag_weights(weights):
    nw = len(weights)

    def body(*refs):
        srcs, outs = refs[:nw], refs[nw:2 * nw]
        s_ici, r_ici, s_d2d, r_d2d = refs[2 * nw:]
        x, y, c, chip, peers, peer_chip = _place()
        sib = (x, y, 1 - c)
        for src, out in zip(srcs, outs):
            out[chip] = src[...].astype(BF16)

        def half(out, k, cc):
            rows = out.shape[1] // 2
            return out.at[k, pl.ds(pl.multiple_of(cc * rows, 16), rows), :]

        sent = []
        for m, (px, py) in enumerate(peers):
            for w, out in enumerate(outs):
                cp = _remote(half(out, chip, c), half(out, chip, c), s_ici.at[nw * m + w], r_ici.at[nw * m + w],
                             (px, py, c))
                cp.start()
                sent.append(cp)
        for m, (px, py) in enumerate(peers):
            for w, out in enumerate(outs):
                blk = half(out, peer_chip[m], c)
                _remote(blk, blk, s_ici.at[nw * m + w], r_ici.at[nw * m + w], (px, py, c)).wait_recv()
                fw = _remote(blk, blk, s_d2d.at[nw * m + w], r_d2d.at[nw * m + w], sib)
                fw.start()
                sent.append(fw)
        for m in range(3):
            for w, out in enumerate(outs):
                blk = half(out, peer_chip[m], 1 - c)
                _remote(blk, blk, s_d2d.at[nw * m + w], r_d2d.at[nw * m + w], sib).wait_recv()
        for cp in sent:
            cp.wait_send()

    vmem = pl.BlockSpec(memory_space=pltpu.VMEM)
    return pl.pallas_call(
        body,
        out_shape=[jax.ShapeDtypeStruct((N_CHIPS,) + w.shape, BF16) for w in weights],
        in_specs=[vmem] * nw, out_specs=[vmem] * nw,
        scratch_shapes=[pltpu.SemaphoreType.DMA((3 * nw,))] * 4,
        compiler_params=pltpu.CompilerParams(vmem_limit_bytes=VMEM_LIMIT), name="ag_weights")(*weights)


_HBM = pl.BlockSpec(memory_space=pltpu.HBM)
_SEM = pl.BlockSpec(memory_space=pltpu.SEMAPHORE)
_ANY = pl.BlockSpec(memory_space=pl.ANY)
_DATAFLOW = pltpu.SideEffectType.DATAFLOW_SIDE_EFFECTING


def _in_hbm(a):
    return pltpu.with_memory_space_constraint(a, pltpu.HBM)


def _exchange_copies(gather, srcs, lands, send_sems, recv_sems):
    nw = len(srcs)
    x, y, c, chip, peers, peer_chip = _place()
    pairs = []
    for m, (px, py) in enumerate(peers):
        for w in range(nw):
            sems = (send_sems.at[nw * m + w], recv_sems.at[nw * m + w], (px, py, c))
            if gather:
                pairs.append((_remote(srcs[w], lands[w].at[chip], *sems),
                              _remote(srcs[w], lands[w].at[peer_chip[m]], *sems)))
            else:
                pairs.append((_remote(srcs[w].at[m], lands[w].at[m], *sems),) * 2)
    return pairs


def _exchange_start(gather, srcs, after, name):
    nw = len(srcs)
    n_copies = 3 * nw

    def body(*refs):
        send_sems, recv_sems = refs[2 * nw + 1], refs[2 * nw + 2]
        for start, _ in _exchange_copies(gather, refs[:nw], refs[nw:2 * nw], send_sems, recv_sems):
            start.start()
        refs[-1][...] = jnp.zeros_like(refs[-1])

    lands = [_in_hbm(lax.empty(((N_CHIPS,) + s.shape) if gather else s.shape, s.dtype)) for s in srcs]
    return pl.pallas_call(
        body, name=name,
        out_shape=(pltpu.SemaphoreType.DMA((n_copies,)), pltpu.SemaphoreType.DMA((n_copies,)))
        + tuple(pltpu.HBM(s.shape, s.dtype) for s in srcs)
        + tuple(pltpu.HBM(l.shape, l.dtype) for l in lands)
        + (jax.ShapeDtypeStruct((8, LANES), F32),),
        in_specs=[_HBM] * (2 * nw) + [_ANY],
        out_specs=(_SEM, _SEM) + (_HBM,) * (2 * nw) + (pl.BlockSpec(memory_space=pltpu.VMEM),),
        input_output_aliases={i: 2 + i for i in range(2 * nw)},
        compiler_params=pltpu.CompilerParams(has_side_effects=_DATAFLOW),
    )(*[_in_hbm(s) for s in srcs], *lands, after)


def _exchange_wait(gather, started, after, name):
    nw = (len(started) - 3) // 2
    send_sems, recv_sems = started[0], started[1]
    thru = started[2:2 + 2 * nw]

    def body(*refs):
        for _, arrival in _exchange_copies(gather, refs[:nw], refs[nw:2 * nw], refs[2 * nw], refs[2 * nw + 1]):
            arrival.wait_send()
            arrival.wait_recv()

    outs = pl.pallas_call(
        body, name=name,
        out_shape=tuple(pltpu.HBM(t.shape, t.dtype) for t in thru),
        in_specs=[_HBM] * (2 * nw) + [_SEM, _SEM, _ANY], out_specs=(_HBM,) * (2 * nw),
        input_output_aliases={i: i for i in range(2 * nw)},
        compiler_params=pltpu.CompilerParams(has_side_effects=_DATAFLOW),
    )(*thru, send_sems, recv_sems, after)
    return outs[nw:]


def _reduce_grads(bigs, g_small):
    big_shapes = [g.shape for g in bigs]
    nb = len(bigs)
    half_small = SMALL_ROWS // 2
    row_block = 32

    def body(*refs):
        gs, gsm_ref = refs[:nb], refs[nb]
        outs, osm_ref = refs[nb + 1:2 * nb + 1], refs[2 * nb + 1]
        scr = refs[2 * nb + 2:]
        ras, ra_sm = scr[:nb], scr[nb]
        sbs, rbs, p_sm = scr[nb + 1:2 * nb + 1], scr[2 * nb + 1:3 * nb + 1], scr[3 * nb + 1]
        sa_s, sa_r, sb_s, sb_r, sc_s, sc_r = scr[3 * nb + 2:]
        x, y, c, chip, peers, peer_chip = _place()
        sib = (x, y, 1 - c)
        halves = [s[1] // 2 for s in big_shapes]

        def rows_of(cc, n):
            return pl.ds(pl.multiple_of(cc * n, 8), n)

        a_copies = []
        for w in range(nb):
            cp = _remote(gs[w].at[:, rows_of(1 - c, halves[w]), :], ras[w], sa_s.at[w], sa_r.at[w], sib)
            cp.start()
            a_copies.append(cp)
        cp = _remote(gsm_ref.at[rows_of(1 - c, half_small), :], ra_sm, sa_s.at[nb], sa_r.at[nb], sib)
        cp.start()
        a_copies.append(cp)
        for cp in a_copies:
            cp.wait_recv()

        b_copies = []
        for w in range(nb):
            n = halves[w]
            for m in range(3):
                k = peer_chip[m]

                def chip_sum(i, carry, w=w, m=m, k=k, n=n):
                    r0 = pl.multiple_of(i * row_block, row_block)
                    mine = gs[w][k, pl.ds(pl.multiple_of(c * n + r0, 8), row_block), :]
                    sbs[w][m, pl.ds(r0, row_block), :] = (mine + ras[w][k, pl.ds(r0, row_block), :]).astype(BF16)
                    return carry
                lax.fori_loop(0, n // row_block, chip_sum, 0)
                cp = _remote(sbs[w].at[m], rbs[w].at[m], sb_s.at[3 * w + m], sb_r.at[3 * w + m],
                             (peers[m][0], peers[m][1], c))
                cp.start()
                b_copies.append(cp)
        p_sm[chip] = gsm_ref[rows_of(c, half_small), :] + ra_sm[...]
        for m in range(3):
            cp = _remote(p_sm.at[chip], p_sm.at[chip], sb_s.at[3 * nb + m], sb_r.at[3 * nb + m],
                         (peers[m][0], peers[m][1], c))
            cp.start()
            b_copies.append(cp)

        c_copies = []
        for w in range(nb):
            n = halves[w]
            for m in range(3):
                _remote(sbs[w].at[m], rbs[w].at[m], sb_s.at[3 * w + m], sb_r.at[3 * w + m],
                        (peers[m][0], peers[m][1], c)).wait_recv()

            def total(i, carry, w=w, n=n):
                r0 = pl.multiple_of(i * row_block, row_block)
                rows = pl.ds(r0, row_block)
                dst = pl.ds(pl.multiple_of(c * n + r0, 8), row_block)
                acc = gs[w][chip, dst, :] + ras[w][chip, rows, :]
                for m in range(3):
                    acc = acc + rbs[w][m, rows, :].astype(F32)
                outs[w][dst, :] = acc
                return carry
            lax.fori_loop(0, n // row_block, total, 0)
            mine = outs[w].at[rows_of(c, n), :]
            cp = _remote(mine, mine, sc_s.at[w], sc_r.at[w], sib)
            cp.start()
            c_copies.append(cp)
        for m in range(3):
            _remote(p_sm.at[chip], p_sm.at[peer_chip[m]], sb_s.at[3 * nb + m], sb_r.at[3 * nb + m],
                    (peers[m][0], peers[m][1], c)).wait_recv()
        osm_ref[rows_of(c, half_small), :] = (p_sm[0] + p_sm[1]) + (p_sm[2] + p_sm[3])
        mine = osm_ref.at[rows_of(c, half_small), :]
        cp = _remote(mine, mine, sc_s.at[nb], sc_r.at[nb], sib)
        cp.start()
        c_copies.append(cp)
        for w in range(nb):
            theirs = outs[w].at[rows_of(1 - c, halves[w]), :]
            _remote(theirs, theirs, sc_s.at[w], sc_r.at[w], sib).wait_recv()
        theirs = osm_ref.at[rows_of(1 - c, half_small), :]
        _remote(theirs, theirs, sc_s.at[nb], sc_r.at[nb], sib).wait_recv()
        for cp in a_copies + b_copies + c_copies:
            cp.wait_send()

    vmem = pl.BlockSpec(memory_space=pltpu.VMEM)
    scratch = ([pltpu.VMEM((N_CHIPS, s[1] // 2, s[2]), F32) for s in big_shapes]
               + [pltpu.VMEM((half_small, LANES), F32)]
               + [pltpu.VMEM((3, s[1] // 2, s[2]), BF16) for s in big_shapes]
               + [pltpu.VMEM((3, s[1] // 2, s[2]), BF16) for s in big_shapes]
               + [pltpu.VMEM((N_CHIPS, half_small, LANES), F32)]
               + [pltpu.SemaphoreType.DMA((nb + 1,)), pltpu.SemaphoreType.DMA((nb + 1,)),
                  pltpu.SemaphoreType.DMA((3 * nb + 3,)), pltpu.SemaphoreType.DMA((3 * nb + 3,)),
                  pltpu.SemaphoreType.DMA((nb + 1,)), pltpu.SemaphoreType.DMA((nb + 1,))])
    return pl.pallas_call(
        body,
        out_shape=[jax.ShapeDtypeStruct(s[1:], F32) for s in big_shapes]
        + [jax.ShapeDtypeStruct((SMALL_ROWS, LANES), F32)],
        in_specs=[vmem] * (nb + 1), out_specs=[vmem] * (nb + 1), scratch_shapes=scratch,
        compiler_params=pltpu.CompilerParams(vmem_limit_bytes=VMEM_LIMIT),
        name="reduce_grads")(*bigs, g_small)


def _reduce_first(stacks):
    ns = len(stacks)
    halves = [s.shape[1] // 2 for s in stacks]
    row_block = 32

    def body(*refs):
        gs, sends, owns = refs[:ns], refs[ns:2 * ns], refs[2 * ns:3 * ns]
        ras, s_sem, r_sem = refs[3 * ns:4 * ns], refs[4 * ns], refs[4 * ns + 1]
        x, y, c, chip, peers, peer_chip = _place()
        swaps = []
        for w in range(ns):
            theirs = gs[w].at[:, pl.ds(pl.multiple_of((1 - c) * halves[w], 8), halves[w]), :]
            swaps.append(_remote(theirs, ras[w], s_sem.at[w], r_sem.at[w], (x, y, 1 - c)))
            swaps[-1].start()
        for w in range(ns):
            n = halves[w]
            swaps[w].wait_recv()

            def sums(i, carry, w=w, n=n):
                r0 = pl.multiple_of(i * row_block, row_block)
                blk = pl.ds(r0, row_block)
                mine = pl.ds(pl.multiple_of(c * n + r0, 8), row_block)
                for m in range(3):
                    sends[w][m, blk, :] = (gs[w][peer_chip[m], mine, :] + ras[w][peer_chip[m], blk, :]).astype(BF16)
                owns[w][blk, :] = gs[w][chip, mine, :] + ras[w][chip, blk, :]
                return carry
            lax.fori_loop(0, n // row_block, sums, 0)
        for cp in swaps:
            cp.wait_send()

    vmem = pl.BlockSpec(memory_space=pltpu.VMEM)
    outs = pl.pallas_call(
        body,
        out_shape=[jax.ShapeDtypeStruct((3, n, s.shape[2]), BF16) for n, s in zip(halves, stacks)]
        + [jax.ShapeDtypeStruct((n, s.shape[2]), F32) for n, s in zip(halves, stacks)],
        in_specs=[vmem] * ns, out_specs=[vmem] * (2 * ns),
        scratch_shapes=[pltpu.VMEM((N_CHIPS, n, s.shape[2]), F32) for n, s in zip(halves, stacks)]
        + [pltpu.SemaphoreType.DMA((ns,)), pltpu.SemaphoreType.DMA((ns,))],
        compiler_params=pltpu.CompilerParams(vmem_limit_bytes=VMEM_LIMIT), name="reduce_first")(*stacks)
    return outs[:ns], outs[ns:]


def _reduce_last(owns, landed):
    ns = len(owns)
    row_block = 32

    def body(*refs):
        own_refs, land_refs, out_refs = refs[:ns], refs[ns:2 * ns], refs[2 * ns:3 * ns]
        s_sem, r_sem = refs[3 * ns], refs[3 * ns + 1]
        x, y, c, chip, peers, peer_chip = _place()
        swaps = []
        for w in range(ns):
            n = own_refs[w].shape[0]

            def total(i, carry, w=w, n=n):
                r0 = pl.multiple_of(i * row_block, row_block)
                blk = pl.ds(r0, row_block)
                acc = own_refs[w][blk, :]
                for m in range(3):
                    acc = acc + land_refs[w][m, blk, :].astype(F32)
                out_refs[w][pl.ds(pl.multiple_of(c * n + r0, 8), row_block), :] = acc
                return carry
            lax.fori_loop(0, n // row_block, total, 0)
            mine = out_refs[w].at[pl.ds(pl.multiple_of(c * n, 8), n), :]
            swaps.append(_remote(mine, mine, s_sem.at[w], r_sem.at[w], (x, y, 1 - c)))
            swaps[-1].start()
        for w in range(ns):
            n = own_refs[w].shape[0]
            theirs = out_refs[w].at[pl.ds(pl.multiple_of((1 - c) * n, 8), n), :]
            _remote(theirs, theirs, s_sem.at[w], r_sem.at[w], (x, y, 1 - c)).wait_recv()
        for cp in swaps:
            cp.wait_send()

    vmem = pl.BlockSpec(memory_space=pltpu.VMEM)
    return pl.pallas_call(
        body, out_shape=[jax.ShapeDtypeStruct((2 * o.shape[0], o.shape[1]), F32) for o in owns],
        in_specs=[vmem] * (2 * ns), out_specs=[vmem] * ns,
        scratch_shapes=[pltpu.SemaphoreType.DMA((ns,)), pltpu.SemaphoreType.DMA((ns,))],
        compiler_params=pltpu.CompilerParams(vmem_limit_bytes=VMEM_LIMIT), name="reduce_last")(*owns, *landed)


_SMALL_PARTS = (("g_norm", 8, 8), ("w_s", 512, 512), ("b_s", 4, 8), ("g_v", 2, 8), ("g_mem", 8, 8),
                ("g_final", 8, 8), ("loss", 1, 8))
_LOSS_ROW = SMALL_ROWS - 8
assert sum(p for _, _, p in _SMALL_PARTS) == SMALL_ROWS


def _pack_small(parts, loss=None):
    loss_row = jnp.zeros((1, LANES), F32) if loss is None else jnp.broadcast_to(loss.reshape(1, 1), (1, LANES))
    rows = []
    for (name, used, padded), p in zip(_SMALL_PARTS, list(parts) + [loss_row]):
        p = p.reshape(used, LANES)
        if padded > used:
            p = jnp.pad(p, ((0, padded - used), (0, 0)))
        rows.append(p)
    return jnp.concatenate(rows, axis=0)


def _unpack_small(packed, shapes):
    out = []
    off = 0
    for (name, used, padded), shape in zip(_SMALL_PARTS, shapes):
        out.append(packed[off:off + used].reshape(shape))
        off += padded
    return out


def _local_step(x, mem, target, g_norm, w_in, w_s, b_s, g_v, g_mem, late_weights, g_final,
                fwd_token=None, on_dw=None):
    B, S, _ = x.shape
    x2d = x.reshape(B * S, D_MODEL)
    t2d = target.reshape(B * S, D_MODEL)
    mem2d = mem.reshape(B * N_MEM, D_MODEL)

    proj = _inproj_fwd(x2d, g_norm if fwd_token is None else g_norm + fwd_token[0:1, 0:1], w_in)
    w_kv, w_out = late_weights(proj)
    kv = _kv_fwd(mem2d, g_mem, w_kv)
    a, lse = _attn_fwd(proj, B, S)
    w_sT = jnp.swapaxes(w_s, 1, 2)
    b_tab = jnp.repeat(b_s.T, HEAD_DIM, axis=1)
    (dx2, da, drest, loss, d_wout, d_ws, d_bs, d_gv, d_gf, dkv) = _mid(
        x2d, t2d, a, proj, kv, w_s, w_sT, b_tab, g_v, w_out, g_final, B, S)
    d_wkv, d_gmem = _kv_bwd(mem2d, g_mem, w_kv, dkv)
    dq, dk, dv = _attn_bwd(proj, a, lse, da, B, S)
    d_win = _inproj_bwd_dw(dq, dk, dv, drest, x2d, g_norm)
    g_norm_dx = g_norm if on_dw is None else g_norm + on_dw(d_win, d_wkv, d_wout)[0:1, 0:1]
    grad_x, d_gnorm = _inproj_bwd_dx(dq, dk, dv, drest, x2d, dx2, g_norm_dx, w_in)
    d_bs = d_bs[:, :N_SGU_GROUPS].T
    return (loss[0, 0], grad_x.reshape(B, S, D_MODEL),
            dict(g_norm=d_gnorm, w_in=d_win, w_s=d_ws, b_s=d_bs, g_v=d_gv, g_mem=d_gmem, w_kv=d_wkv,
                 w_out=d_wout, g_final=d_gf))


def kernel(x, mem, g_norm, w_in, w_sgu_spatial, b_sgu_spatial, g_sgu_v, g_mem, w_mem_kv, w_out, g_final, loss_target, m_g_norm, m_w_in, m_w_sgu_spatial, m_b_sgu_spatial, m_g_sgu_v, m_g_mem, m_w_mem_kv, m_w_out, m_g_final, v_g_norm, v_w_in, v_w_sgu_spatial, v_b_sgu_spatial, v_g_sgu_v, v_g_mem, v_w_mem_kv, v_w_out, v_g_final):
    t = lambda w: jnp.swapaxes(w[0], 0, 1)
    chip = 2 * lax.axis_index("x") + lax.axis_index("y")
    (win_all,) = _ag_weights([t(w_in)])
    w_in_full = win_all.reshape(-1, win_all.shape[-1])
    late_shards = [w_mem_kv[0].astype(BF16), w_out[0].astype(BF16)]
    late = _exchange_start(True, late_shards, win_all, "gather_late_start")

    def late_weights(proj):
        full = []
        for shard, landed in zip(late_shards, _exchange_wait(True, late, proj, "gather_late_wait")):
            landed = lax.dynamic_update_slice(landed, shard[None], (chip, 0, 0))
            full.append(landed.reshape(-1, landed.shape[-1]))
        return full

    scatter = {}

    def on_dw(d_win, d_wkv, d_wout):
        stacks = [d_win.reshape((N_CHIPS, w_in.shape[2], w_in.shape[1])),
                  d_wkv.reshape((N_CHIPS,) + w_mem_kv.shape[1:]), d_wout.reshape((N_CHIPS,) + w_out.shape[1:])]
        sends, scatter["own"] = _reduce_first(stacks)
        scatter["started"] = _exchange_start(False, list(sends), scatter["own"][0], "scatter_start")
        return scatter["started"][-1]

    loss, grad_x, g = _local_step(
        x, mem, loss_target, g_norm, w_in_full, w_sgu_spatial[0], b_sgu_spatial[0], g_sgu_v, g_mem,
        late_weights, g_final.reshape(1, D_MODEL), fwd_token=late[-1], on_dw=on_dw)

    small_names = ("g_norm", "w_s", "b_s", "g_v", "g_mem", "g_final")
    (gr_small,) = _reduce_grads([], _pack_small([g[n] for n in small_names], loss))
    loss = gr_small[_LOSS_ROW, 0]
    landed = _exchange_wait(False, scatter["started"], gr_small, "scatter_wait")
    gr_in, gr_kv, gr_out = _reduce_last(scatter["own"], landed)

    small_w = (g_norm, w_sgu_spatial, b_sgu_spatial, g_sgu_v, g_mem, g_final)
    small_m = (m_g_norm, m_w_sgu_spatial, m_b_sgu_spatial, m_g_sgu_v, m_g_mem, m_g_final)
    small_v = (v_g_norm, v_w_sgu_spatial, v_b_sgu_spatial, v_g_sgu_v, v_g_mem, v_g_final)
    shapes = [w.shape for w in small_w]
    d_small, nm_small, nv_small = _adamw(_pack_small(small_w), gr_small, _pack_small(small_m),
                                         _pack_small(small_v), "adamw_small")
    d_in, nm_in, nv_in = _adamw(t(w_in), gr_in, t(m_w_in), t(v_w_in), "adamw_w_in")
    gr_in, d_in, nm_in, nv_in = [jnp.swapaxes(z, 0, 1) for z in (gr_in, d_in, nm_in, nv_in)]
    d_kv, nm_kv, nv_kv = _adamw(w_mem_kv[0], gr_kv, m_w_mem_kv[0], v_w_mem_kv[0], "adamw_w_kv")
    d_out, nm_out, nv_out = _adamw(w_out[0], gr_out, m_w_out[0], v_w_out[0], "adamw_w_out")

    def leaves(packed, big_in, big_kv, big_out):
        s_norm, s_ws, s_bs, s_gv, s_gmem, s_gf = _unpack_small(packed, shapes)
        return [s_norm, big_in[None], s_ws, s_bs, s_gv, s_gmem, big_kv[None], big_out[None], s_gf]

    return (loss, grad_x, *leaves(gr_small, gr_in, gr_kv, gr_out), *leaves(d_small, d_in, d_kv, d_out),
            *leaves(nm_small, nm_in, nm_kv, nm_out), *leaves(nv_small, nv_in, nv_kv, nv_out))
```

```python
import functools

import jax
import jax.numpy as jnp
from jax import lax
from jax.experimental import pallas as pl
from jax.experimental.pallas import tpu as pltpu

F32 = jnp.float32
BF16 = jnp.bfloat16
MESH = pl.DeviceIdType.MESH

D_MODEL = 1024
ATTN_WIDTH = 512
SGU_WIDTH = 256
MEM_WIDTH = 256
N_MEM = 256
IN_COLS = 3328
QKV_COLS = 3 * ATTN_WIDTH
REST_COLS = IN_COLS - QKV_COLS
SGU_CHUNK = 128
N_SGU_GROUPS = 4
EPS = 1e-6
NEG_INF = -1e30
DILATIONS = (1, 4, 16)
RADIUS = 64
Q_BLOCK = 128
LANES = 128
HEAD_DIM = 64

ADAM_LR = 0.001
ADAM_B1 = 0.9
ADAM_B2 = 0.999
ADAM_EPS = 1e-08
ADAM_WD = 0.01
ADAM_STEP = 10

N_CHIPS = 4
VMEM_LIMIT = 56 * 1024 * 1024
SMALL_ROWS = 560


def _params(sem=None, vmem=VMEM_LIMIT):
    return pltpu.CompilerParams(dimension_semantics=sem, vmem_limit_bytes=vmem)


def _nn(a, b):
    return jnp.dot(a, b, preferred_element_type=F32)


def _nt(a, b):
    return lax.dot_general(a, b, (((1,), (1,)), ((), ())), preferred_element_type=F32)


def _tn(a, b):
    return lax.dot_general(a, b, (((0,), (0,)), ((), ())), preferred_element_type=F32)


def _rms(x):
    r = lax.rsqrt(jnp.mean(x * x, axis=-1, keepdims=True) + EPS)
    return r, x * r


def _head_masks():
    lane = lax.broadcasted_iota(jnp.int32, (1, LANES), 1)
    lo = lane < HEAD_DIM
    return lo, (lo.astype(F32), (~lo).astype(F32))


def _silu_parts(z):
    s = jax.nn.sigmoid(z)
    return z * s, s * (1.0 + z * (1.0 - s))


def _gelu_parts(x):
    c = 0.7978845608028654
    x2 = x * x
    t = jnp.tanh(c * (x + 0.044715 * (x * x2)))
    val = 0.5 * x * (1.0 + t)
    grad = 0.5 * (1.0 + t) + 0.5 * x * (1.0 - t * t) * (c * (1.0 + 3.0 * 0.044715 * x2))
    return val, grad


def _inproj_fwd(x2d, g_norm, w_in_t):
    T = x2d.shape[0]
    tm = 512

    def body(x_ref, g_ref, w_ref, o_ref):
        _, xh = _rms(x_ref[...])
        h = (xh * g_ref[...]).astype(BF16)
        o_ref[...] = _nt(h, w_ref[...])

    return pl.pallas_call(
        body, grid=(T // tm,),
        in_specs=[pl.BlockSpec((tm, D_MODEL), lambda i: (i, 0)),
                  pl.BlockSpec((1, D_MODEL), lambda i: (0, 0)),
                  pl.BlockSpec((IN_COLS, D_MODEL), lambda i: (0, 0))],
        out_specs=pl.BlockSpec((tm, IN_COLS), lambda i: (i, 0)),
        out_shape=jax.ShapeDtypeStruct((T, IN_COLS), F32),
        compiler_params=_params(("arbitrary",)), name="inproj_fwd")(x2d, g_norm, w_in_t)


def _kv_fwd(mem2d, g_mem, w_kv):
    Tm = mem2d.shape[0]

    def body(m_ref, g_ref, w_ref, o_ref):
        _, mh = _rms(m_ref[...])
        o_ref[...] = _nn((mh * g_ref[...]).astype(BF16), w_ref[...])

    return pl.pallas_call(
        body, out_shape=jax.ShapeDtypeStruct((Tm, 2 * MEM_WIDTH), F32),
        compiler_params=_params(), name="kv_fwd")(mem2d, g_mem, w_kv)


def _kv_bwd(mem2d, g_mem, w_kv, dkv):
    Tm = mem2d.shape[0]

    def body(m_ref, g_ref, w_ref, dkv_ref, dw_ref, dg_ref):
        _, mh = _rms(m_ref[...])
        memn = (mh * g_ref[...]).astype(BF16)
        dkvb = dkv_ref[...].astype(BF16)
        dw_ref[...] = _tn(memn, dkvb)
        dmemn = _nt(dkvb, w_ref[...])
        dg_ref[...] = jnp.sum(dmemn * mh, axis=0, keepdims=True)

    return pl.pallas_call(
        body, out_shape=(jax.ShapeDtypeStruct((D_MODEL, 2 * MEM_WIDTH), F32),
                         jax.ShapeDtypeStruct((1, D_MODEL), F32)),
        compiler_params=_params(), name="kv_bwd")(mem2d, g_mem, w_kv, dkv)


def _attn_geometry(S):
    geom = []
    for d in DILATIONS:
        L = S // d
        assert L % Q_BLOCK == 0
        geom.append((d, L, min(2 * Q_BLOCK, L), L // Q_BLOCK))
    return geom


def _init_bias(bias_scr, geom, hp):
    row = lax.broadcasted_iota(jnp.int32, (Q_BLOCK, 2 * Q_BLOCK), 0)
    col = lax.broadcasted_iota(jnp.int32, (Q_BLOCK, 2 * Q_BLOCK), 1)
    for j in (0, 1):
        bits = (126 - (2 * hp + j)) * (1 << 23)
        slope = lax.bitcast_convert_type(jnp.full((1, 1), bits, jnp.int32), F32)
        for di, (d, _, _, _) in enumerate(geom):
            for cls, off in enumerate((0, -RADIUS, -2 * RADIUS)):
                dist = jnp.abs(col - row + off)
                bias_scr[di * 6 + cls * 2 + j] = jnp.where(
                    dist <= RADIUS, -(slope * float(d)) * dist.astype(F32), NEG_INF)


def _block_slices(d, L, KW, nqb, r, qb):
    qs = qb * Q_BLOCK
    ks = jnp.clip(qs - RADIUS, 0, L - KW)
    cls = jnp.where(qb == 0, 0, jnp.where(qb == nqb - 1, 2, 1))
    if d == 1:
        qsl = pl.ds(pl.multiple_of(qs, Q_BLOCK), Q_BLOCK)
        ksl = pl.ds(pl.multiple_of(ks, RADIUS), KW)
    else:
        qsl = pl.ds(r + qs * d, Q_BLOCK, stride=d)
        ksl = pl.ds(r + ks * d, KW, stride=d)
    return qsl, ksl, cls


def _for_groups(geom, group, fn):
    for di, (d, L, KW, nqb) in enumerate(geom):
        assert (d * nqb) % group == 0

        def step(it, carry, di=di, d=d, L=L, KW=KW, nqb=nqb):
            slices = []
            for g in range(group):
                i = it * group + g
                slices.append(_block_slices(d, L, KW, nqb, i // nqb, i % nqb))
            fn(di, KW, slices)
            return carry
        lax.fori_loop(0, d * nqb // group, step, 0)


def _attn_fwd(proj, B, S):
    T = B * S
    geom = _attn_geometry(S)
    n_pairs = ATTN_WIDTH // LANES

    def body(q_ref, k_ref, v_ref, a_ref, lse_ref, bias_scr, *per_dilation):
        o_scr, m_scr, l_scr = per_dilation[0:3], per_dilation[3:6], per_dilation[6:9]
        lo, hm = _head_masks()
        _init_bias(bias_scr, geom, pl.program_id(1))

        def group(di, KW, slices):
            chains = [(g, j) for g in range(len(slices)) for j in (0, 1)]
            q = [q_ref[qsl, :] for qsl, _, _ in slices]
            kw = [k_ref[ksl, :].astype(BF16) for _, ksl, _ in slices]
            vw = [v_ref[ksl, :].astype(BF16) for _, ksl, _ in slices]
            s = {(g, j): _nt((q[g] * (hm[j] * 0.125)).astype(BF16), kw[g])
                 + bias_scr[di * 6 + slices[g][2] * 2 + j, :, pl.ds(0, KW)] for g, j in chains}
            m = {c: jnp.max(s[c], axis=1, keepdims=True) for c in chains}
            p = {c: jnp.exp(s[c] - m[c]) for c in chains}
            l = {c: jnp.sum(p[c], axis=1, keepdims=True) for c in chains}
            o = {(g, j): _nn(p[(g, j)].astype(BF16), vw[g]) for g, j in chains}
            for g, (qsl, _, _) in enumerate(slices):
                o_scr[di][qsl, :] = jnp.where(lo, o[(g, 0)], o[(g, 1)])
                m_scr[di][qsl, :] = jnp.where(lo, m[(g, 0)], m[(g, 1)])
                l_scr[di][qsl, :] = jnp.where(lo, l[(g, 0)], l[(g, 1)])

        _for_groups(geom, 8, group)

        rows_per = 256

        def combine(i, carry):
            rows = pl.ds(pl.multiple_of(i * rows_per, rows_per), rows_per)
            ms = [m_scr[di][rows, :] for di in range(3)]
            mx = jnp.maximum(jnp.maximum(ms[0], ms[1]), ms[2])
            num = 0.0
            den = 0.0
            for di in range(3):
                w = jnp.exp(ms[di] - mx)
                num = num + w * o_scr[di][rows, :]
                den = den + w * l_scr[di][rows, :]
            a_ref[rows, :] = num / den
            lse_ref[rows, :] = mx + jnp.log(den)
            return carry

        lax.fori_loop(0, S // rows_per, combine, 0)

    blk = lambda off: pl.BlockSpec((S, LANES), lambda b, h, off=off: (b, off + h))
    out_blk = pl.BlockSpec((S, LANES), lambda b, h: (b, h))
    return pl.pallas_call(
        body, grid=(B, n_pairs),
        in_specs=[blk(0), blk(n_pairs), blk(2 * n_pairs)],
        out_specs=[out_blk, out_blk],
        out_shape=[jax.ShapeDtypeStruct((T, ATTN_WIDTH), F32)] * 2,
        scratch_shapes=[pltpu.VMEM((18, Q_BLOCK, 2 * Q_BLOCK), F32)] + [pltpu.VMEM((S, LANES), F32)] * 9,
        compiler_params=_params(("arbitrary", "arbitrary")), name="attn_fwd")(proj, proj, proj)


def _attn_bwd(proj, a, lse, da, B, S):
    T = B * S
    geom = _attn_geometry(S)
    n_pairs = ATTN_WIDTH // LANES

    def body(q_ref, k_ref, v_ref, a_ref, lse_ref, do_ref, dq_ref, dk_ref, dv_ref,
             bias_scr, dq_scr, dk_scr, dv_scr):
        _, hm = _head_masks()
        _init_bias(bias_scr, geom, pl.program_id(1))
        dq_scr[...] = jnp.zeros_like(dq_scr)
        dk_scr[...] = jnp.zeros_like(dk_scr)
        dv_scr[...] = jnp.zeros_like(dv_scr)

        def group(di, KW, slices):
            n = len(slices)
            chains = [(g, j) for g in range(n) for j in (0, 1)]
            q = [q_ref[qsl, :] for qsl, _, _ in slices]
            do = [do_ref[qsl, :] for qsl, _, _ in slices]
            doa = [do[g] * a_ref[slices[g][0], :] for g in range(n)]
            lse_q = [lse_ref[qsl, :] for qsl, _, _ in slices]
            kw = [k_ref[ksl, :].astype(BF16) for _, ksl, _ in slices]
            vw = [v_ref[ksl, :].astype(BF16) for _, ksl, _ in slices]
            qj = {(g, j): (q[g] * (hm[j] * 0.125)).astype(BF16) for g, j in chains}
            doj = {(g, j): (do[g] * hm[j]).astype(BF16) for g, j in chains}
            s = {(g, j): _nt(qj[(g, j)], kw[g])
                 + bias_scr[di * 6 + slices[g][2] * 2 + j, :, pl.ds(0, KW)] for g, j in chains}
            dp = {(g, j): _nt(doj[(g, j)], vw[g]) for g, j in chains}
            dsum = {(g, j): jnp.sum(doa[g] * hm[j], axis=1, keepdims=True) for g, j in chains}
            p = {(g, j): jnp.exp(s[(g, j)] - lse_q[g][:, HEAD_DIM * j:HEAD_DIM * j + 1]) for g, j in chains}
            ds = {c: (p[c] * (dp[c] - dsum[c])).astype(BF16) for c in chains}
            pb = {c: p[c].astype(BF16) for c in chains}
            dq = [_nn(ds[(g, 0)], kw[g]) * (hm[0] * 0.125) + _nn(ds[(g, 1)], kw[g]) * (hm[1] * 0.125)
                  for g in range(n)]
            both = lambda t, g: jnp.concatenate([t[(g, 0)], t[(g, 1)]], axis=0)
            dkw = [_tn(both(ds, g), both(qj, g)) for g in range(n)]
            dvw = [_tn(both(pb, g), both(doj, g)) for g in range(n)]
            for g, (qsl, ksl, _) in enumerate(slices):
                dq_scr[qsl, :] = dq_scr[qsl, :] + dq[g]
                dk_scr[ksl, :] = dk_scr[ksl, :] + dkw[g]
                dv_scr[ksl, :] = dv_scr[ksl, :] + dvw[g]

        _for_groups(geom, 4, group)
        dq_ref[...] = dq_scr[...].astype(BF16)
        dk_ref[...] = dk_scr[...].astype(BF16)
        dv_ref[...] = dv_scr[...].astype(BF16)

    blk = lambda off: pl.BlockSpec((S, LANES), lambda b, h, off=off: (b, off + h))
    return pl.pallas_call(
        body, grid=(B, n_pairs),
        in_specs=[blk(0), blk(n_pairs), blk(2 * n_pairs), blk(0), blk(0), blk(0)],
        out_specs=[blk(0), blk(0), blk(0)],
        out_shape=[jax.ShapeDtypeStruct((T, ATTN_WIDTH), BF16)] * 3,
        scratch_shapes=[pltpu.VMEM((18, Q_BLOCK, 2 * Q_BLOCK), F32),
                        pltpu.VMEM((S, LANES), F32),
                        pltpu.VMEM((S, LANES), F32),
                        pltpu.VMEM((S, LANES), F32)],
        compiler_params=_params(("arbitrary", "arbitrary")), name="attn_bwd")(proj, proj, proj, a, lse, da)


def _mid(x2d, t2d, a, proj, kv, w_s, w_sT, b_tab, g_v, w_out, g_final, B, S):
    T = B * S
    tm = 512
    nt = S // tm
    n_sub = 1
    sub = tm // n_sub
    n_chunks = sub // SGU_CHUNK

    def body(*refs):
        tiles_in, consts, tiles_out, accs = refs[:9], refs[9:16], refs[16:19], refs[19:]
        for sb in range(n_sub):
            rows = lambda r: r.at[pl.ds(sb * sub, sub), :]
            sub_body(*[rows(r) for r in tiles_in], *consts, *[rows(r) for r in tiles_out], *accs, sb=sb)

    def sub_body(x_ref, t_ref, a_ref, za_ref, ub_ref, vb_ref, zb_ref, qm_ref, zm_ref, kv_ref,
                 ws_ref, wsT_ref, btab_ref, gv_ref, wout_ref, gf_ref,
                 dx2_ref, da_ref, drest_ref, loss_ref, dwout_ref, dws_ref, dbs_ref, dgv_ref, dgf_ref, dkv_ref,
                 dbtab_scr, sb):
        b = pl.program_id(0)
        t = pl.program_id(1)
        first = jnp.logical_and(jnp.logical_and(b == 0, t == 0), sb == 0)
        last = jnp.logical_and(jnp.logical_and(b == B - 1, t == nt - 1), sb == n_sub - 1)
        _, hm = _head_masks()
        lane_g = lax.broadcasted_iota(jnp.int32, (1, SGU_WIDTH), 1) // HEAD_DIM
        gm = [(lane_g == g).astype(F32) for g in range(N_SGU_GROUPS)]

        @pl.when(first)
        def _():
            loss_ref[...] = jnp.zeros_like(loss_ref)
            dwout_ref[...] = jnp.zeros_like(dwout_ref)
            dws_ref[...] = jnp.zeros_like(dws_ref)
            dbs_ref[...] = jnp.zeros_like(dbs_ref)
            dgv_ref[...] = jnp.zeros_like(dgv_ref)
            dgf_ref[...] = jnp.zeros_like(dgf_ref)
            dbtab_scr[...] = jnp.zeros_like(dbtab_scr)

        @pl.when(jnp.logical_and(t == 0, sb == 0))
        def _():
            dkv_ref[...] = jnp.zeros_like(dkv_ref)

        sil_a, dsil_a = _silu_parts(za_ref[...])
        a_val = a_ref[...]
        gated_a = sil_a * a_val

        u, du_dub = _gelu_parts(ub_ref[...])
        vv, dvv_dvb = _gelu_parts(vb_ref[...])
        rv, vhat = _rms(vv)
        gv = gv_ref[...]
        vn = (vhat * gv).astype(BF16)
        ws = [ws_ref[g].astype(BF16) for g in range(N_SGU_GROUPS)]
        wsT = [wsT_ref[g].astype(BF16) for g in range(N_SGU_GROUPS)]
        mixed = []
        for ci in range(n_chunks):
            vn_c = vn[ci * SGU_CHUNK:(ci + 1) * SGU_CHUNK, :]
            mc = btab_ref[...]
            for g in range(N_SGU_GROUPS):
                mc = mc + gm[g] * _nn(ws[g], vn_c)
            mixed.append(mc)
        mixed = jnp.concatenate(mixed, axis=0)
        sg = u * mixed
        sil_b, dsil_b = _silu_parts(zb_ref[...])
        gated_b = sil_b * sg

        kvv = kv_ref[...].astype(BF16)
        qm = qm_ref[...]
        probs, qs_m, mo_pairs = [], [], []
        for pr in range(2):
            cols = slice(pr * LANES, (pr + 1) * LANES)
            kp = kvv[:, pr * LANES:(pr + 1) * LANES]
            vp = kvv[:, MEM_WIDTH + pr * LANES:MEM_WIDTH + (pr + 1) * LANES]
            mo_p = 0.0
            for j in (0, 1):
                qj = (qm[:, cols] * (hm[j] * 0.125)).astype(BF16)
                s = _nt(qj, kp)
                e = jnp.exp(s - jnp.max(s, axis=1, keepdims=True))
                p = e / jnp.sum(e, axis=1, keepdims=True)
                mo_p = mo_p + _nn(p.astype(BF16), vp) * hm[j]
                probs.append(p)
                qs_m.append(qj)
            mo_pairs.append(mo_p)
        mo = jnp.concatenate(mo_pairs, axis=1)
        sil_m, dsil_m = _silu_parts(zm_ref[...])
        gated_m = sil_m * mo

        gated = jnp.concatenate([gated_a, gated_b, gated_m], axis=1).astype(BF16)
        wout = wout_ref[...]
        x2 = x_ref[...] + _nn(gated, wout)
        r2, xh2 = _rms(x2)
        gf = gf_ref[...]
        err = xh2 * gf - t_ref[...]
        loss_ref[...] += jnp.sum(err * err) * (0.5 / D_MODEL)

        dy = err * (1.0 / D_MODEL)
        dgf_ref[...] += jnp.sum(dy * xh2, axis=0, keepdims=True)
        gdy = dy * gf
        dx2 = r2 * (gdy - xh2 * jnp.mean(gdy * xh2, axis=1, keepdims=True))
        dx2_ref[...] = dx2
        dx2b = dx2.astype(BF16)
        dwout_ref[...] += _tn(gated, dx2b)
        dgated = _nt(dx2b, wout)
        dga = dgated[:, 0:ATTN_WIDTH]
        dgb = dgated[:, ATTN_WIDTH:ATTN_WIDTH + SGU_WIDTH]
        dgm = dgated[:, ATTN_WIDTH + SGU_WIDTH:]

        da_ref[...] = dga * sil_a
        dza = dga * a_val * dsil_a

        dsg = dgb * sil_b
        dzb = dgb * sg * dsil_b
        dub = dsg * mixed * du_dub
        dmixed = dsg * u
        dmixed_b = dmixed.astype(BF16)
        dvn = []
        dbtab = dbtab_scr[...]
        for ci in range(n_chunks):
            rows = slice(ci * SGU_CHUNK, (ci + 1) * SGU_CHUNK)
            dm_c = dmixed_b[rows, :]
            vn_c = vn[rows, :]
            dvn_c = 0.0
            for g in range(N_SGU_GROUPS):
                dvn_c = dvn_c + gm[g] * _nn(wsT[g], dm_c)
                dws_ref[g] += _nt((dmixed[rows, :] * gm[g]).astype(BF16), vn_c)
            dvn.append(dvn_c)
            dbtab = dbtab + dmixed[rows, :]
        dbtab_scr[...] = dbtab
        dvn = jnp.concatenate(dvn, axis=0)
        dgv_ref[...] += jnp.sum(dvn * vhat, axis=0, keepdims=True)
        tv = dvn * gv
        dvv = rv * (tv - vhat * jnp.mean(tv * vhat, axis=1, keepdims=True))
        dvb = dvv * dvv_dvb

        dmo = dgm * sil_m
        dzm = dgm * mo * dsil_m
        dqm_pairs = []
        dk_pairs, dv_pairs = [], []
        for pr in range(2):
            kp = kvv[:, pr * LANES:(pr + 1) * LANES]
            vp = kvv[:, MEM_WIDTH + pr * LANES:MEM_WIDTH + (pr + 1) * LANES]
            dmo_p = dmo[:, pr * LANES:(pr + 1) * LANES]
            dq_p = 0.0
            dk_p = 0.0
            dv_p = 0.0
            for j in (0, 1):
                p = probs[2 * pr + j]
                dmo_j = (dmo_p * hm[j]).astype(BF16)
                dp = _nt(dmo_j, vp)
                ds = (p * (dp - jnp.sum(dp * p, axis=1, keepdims=True))).astype(BF16)
                dq_p = dq_p + _nn(ds, kp) * (hm[j] * 0.125)
                dk_p = dk_p + _tn(ds, qs_m[2 * pr + j])
                dv_p = dv_p + _tn(p.astype(BF16), dmo_j)
            dqm_pairs.append(dq_p)
            dk_pairs.append(dk_p)
            dv_pairs.append(dv_p)
        dqm = jnp.concatenate(dqm_pairs, axis=1)
        dkv_ref[...] += jnp.concatenate(dk_pairs + dv_pairs, axis=1)

        drest_ref[...] = jnp.concatenate([dza, dub, dvb, dzb, dqm, dzm], axis=1).astype(BF16)

        @pl.when(last)
        def _():
            lane = lax.broadcasted_iota(jnp.int32, (1, LANES), 1)
            dbt = dbtab_scr[...]
            out = jnp.zeros((SGU_CHUNK, LANES), F32)
            for g in range(N_SGU_GROUPS):
                out = out + jnp.where(lane == g, jnp.sum(dbt * gm[g], axis=1, keepdims=True), 0.0)
            dbs_ref[...] = out

    tile = lambda w, cb: pl.BlockSpec((tm, w), lambda b, t, cb=cb: (b * nt + t, cb))
    const = lambda shape: pl.BlockSpec(shape, lambda b, t, n=len(shape): (0,) * n)
    return pl.pallas_call(
        body, grid=(B, nt),
        in_specs=[tile(D_MODEL, 0), tile(D_MODEL, 0), tile(ATTN_WIDTH, 0),
                  tile(ATTN_WIDTH, 3),
                  tile(SGU_WIDTH, 8), tile(SGU_WIDTH, 9), tile(SGU_WIDTH, 10),
                  tile(MEM_WIDTH, 11), tile(MEM_WIDTH, 12),
                  pl.BlockSpec((N_MEM, 2 * MEM_WIDTH), lambda b, t: (b, 0)),
                  const((N_SGU_GROUPS, SGU_CHUNK, SGU_CHUNK)), const((N_SGU_GROUPS, SGU_CHUNK, SGU_CHUNK)),
                  const((SGU_CHUNK, SGU_WIDTH)), const((1, SGU_WIDTH)),
                  const((D_MODEL, D_MODEL)), const((1, D_MODEL))],
        out_specs=[tile(D_MODEL, 0), tile(ATTN_WIDTH, 0), tile(REST_COLS, 0),
                   const((8, LANES)), const((D_MODEL, D_MODEL)),
                   const((N_SGU_GROUPS, SGU_CHUNK, SGU_CHUNK)), const((SGU_CHUNK, LANES)),
                   const((1, SGU_WIDTH)), const((1, D_MODEL)),
                   pl.BlockSpec((N_MEM, 2 * MEM_WIDTH), lambda b, t: (b, 0))],
        out_shape=[jax.ShapeDtypeStruct((T, D_MODEL), F32), jax.ShapeDtypeStruct((T, ATTN_WIDTH), F32),
                   jax.ShapeDtypeStruct((T, REST_COLS), BF16),
                   jax.ShapeDtypeStruct((8, LANES), F32), jax.ShapeDtypeStruct((D_MODEL, D_MODEL), F32),
                   jax.ShapeDtypeStruct((N_SGU_GROUPS, SGU_CHUNK, SGU_CHUNK), F32),
                   jax.ShapeDtypeStruct((SGU_CHUNK, LANES), F32),
                   jax.ShapeDtypeStruct((1, SGU_WIDTH), F32), jax.ShapeDtypeStruct((1, D_MODEL), F32),
                   jax.ShapeDtypeStruct((B * N_MEM, 2 * MEM_WIDTH), F32)],
        scratch_shapes=[pltpu.VMEM((SGU_CHUNK, SGU_WIDTH), F32)],
        compiler_params=_params(("arbitrary", "arbitrary")), name="mid")(
            x2d, t2d, a, proj, proj, proj, proj, proj, proj, kv, w_s, w_sT, b_tab, g_v, w_out, g_final)


def _inproj_bwd_dx(dq, dk, dv, drest, x2d, dx2, g_norm, w_in_t):
    T = x2d.shape[0]
    tm = 512
    W = ATTN_WIDTH

    def body(dq_ref, dk_ref, dv_ref, dr_ref, x_ref, dx2_ref, g_ref, w_ref, gx_ref, dg_ref):
        @pl.when(pl.program_id(0) == 0)
        def _():
            dg_ref[...] = jnp.zeros_like(dg_ref)

        dh = (_nn(dq_ref[...], w_ref[0:W, :]) + _nn(dk_ref[...], w_ref[W:2 * W, :])
              + _nn(dv_ref[...], w_ref[2 * W:3 * W, :]) + _nn(dr_ref[...], w_ref[QKV_COLS:IN_COLS, :]))
        r, xh = _rms(x_ref[...])
        dg_ref[...] += jnp.sum(dh * xh, axis=0, keepdims=True)
        th = dh * g_ref[...]
        gx_ref[...] = r * (th - xh * jnp.mean(th * xh, axis=1, keepdims=True)) + dx2_ref[...]

    tile = lambda w: pl.BlockSpec((tm, w), lambda i: (i, 0))
    return pl.pallas_call(
        body, grid=(T // tm,),
        in_specs=[tile(W), tile(W), tile(W), tile(REST_COLS), tile(D_MODEL), tile(D_MODEL),
                  pl.BlockSpec((1, D_MODEL), lambda i: (0, 0)),
                  pl.BlockSpec((IN_COLS, D_MODEL), lambda i: (0, 0))],
        out_specs=[tile(D_MODEL), pl.BlockSpec((1, D_MODEL), lambda i: (0, 0))],
        out_shape=[jax.ShapeDtypeStruct((T, D_MODEL), F32), jax.ShapeDtypeStruct((1, D_MODEL), F32)],
        compiler_params=_params(("arbitrary",)), name="inproj_bwd_dx")(dq, dk, dv, drest, x2d, dx2, g_norm, w_in_t)


def _inproj_bwd_dw(dq, dk, dv, drest, x2d, g_norm):
    T = x2d.shape[0]
    tm = 512
    W = ATTN_WIDTH

    def body(dq_ref, dk_ref, dv_ref, dr_ref, x_ref, g_ref, dw_ref):
        @pl.when(pl.program_id(0) == 0)
        def _():
            dw_ref[...] = jnp.zeros_like(dw_ref)

        _, xh = _rms(x_ref[...])
        h = (xh * g_ref[...]).astype(BF16)
        dw_ref[0:W, :] += _tn(dq_ref[...], h)
        dw_ref[W:2 * W, :] += _tn(dk_ref[...], h)
        dw_ref[2 * W:3 * W, :] += _tn(dv_ref[...], h)
        dw_ref[QKV_COLS:IN_COLS, :] += _tn(dr_ref[...], h)

    tile = lambda w: pl.BlockSpec((tm, w), lambda i: (i, 0))
    return pl.pallas_call(
        body, grid=(T // tm,),
        in_specs=[tile(W), tile(W), tile(W), tile(REST_COLS), tile(D_MODEL),
                  pl.BlockSpec((1, D_MODEL), lambda i: (0, 0))],
        out_specs=pl.BlockSpec((IN_COLS, D_MODEL), lambda i: (0, 0)),
        out_shape=jax.ShapeDtypeStruct((IN_COLS, D_MODEL), F32),
        compiler_params=_params(("arbitrary",)), name="inproj_bwd_dw")(dq, dk, dv, drest, x2d, g_norm)


def _adamw(w, g, m, v, name):
    R, C = w.shape
    br = max(r for r in range(8, 257, 8) if R % r == 0)

    def body(w_ref, g_ref, m_ref, v_ref, d_ref, nm_ref, nv_ref):
        gg = g_ref[...]
        nm = ADAM_B1 * m_ref[...] + (1.0 - ADAM_B1) * gg
        nv = ADAM_B2 * v_ref[...] + (1.0 - ADAM_B2) * (gg * gg)
        m_hat = nm / (1.0 - ADAM_B1 ** ADAM_STEP)
        v_hat = nv / (1.0 - ADAM_B2 ** ADAM_STEP)
        d_ref[...] = -ADAM_LR * (m_hat / (jnp.sqrt(v_hat) + ADAM_EPS) + ADAM_WD * w_ref[...])
        nm_ref[...] = nm
        nv_ref[...] = nv

    spec = pl.BlockSpec((br, C), lambda i: (i, 0))
    return pl.pallas_call(
        body, grid=(R // br,), in_specs=[spec] * 4, out_specs=[spec] * 3,
        out_shape=[jax.ShapeDtypeStruct((R, C), F32)] * 3,
        compiler_params=_params(("arbitrary",)), name=name)(w, g, m, v)


def _place():
    x, y, c = lax.axis_index("x"), lax.axis_index("y"), lax.axis_index("c")
    chip = 2 * x + y
    peers = [(x, 1 - y), (1 - x, y), (1 - x, 1 - y)]
    peer_chip = [2 * px + py for px, py in peers]
    return x, y, c, chip, peers, peer_chip


def _remote(src, dst, send_sem, recv_sem, dev):
    return pltpu.make_async_remote_copy(src_ref=src, dst_ref=dst, send_sem=send_sem, recv_sem=recv_sem,
                                        device_id=dev, device_id_type=MESH)


def _ag_weights(weights):
    nw = len(weights)

    def body(*refs):
        srcs, outs = refs[:nw], refs[nw:2 * nw]
        s_ici, r_ici, s_d2d, r_d2d = refs[2 * nw:]
        x, y, c = lax.axis_index("x"), lax.axis_index("y"), lax.axis_index("c")
        chip = 2 * x + y
        sib = (x, y, 1 - c)
        first = ((x + 1 - c) % 2, (y + c) % 2)
        second = ((x + c) % 2, (y + 1 - c) % 2)
        first_chip, second_chip = 2 * first[0] + first[1], 2 * second[0] + second[1]
        diag_chip = 3 - chip
        for src, out in zip(srcs, outs):
            out[chip] = src[...].astype(BF16)

        def half(out, k, cc):
            rows = out.shape[1] // 2
            return out.at[k, pl.ds(pl.multiple_of(cc * rows, 16), rows), :]

        def ici(w, slot, out, k, dev):
            blk = half(out, k, c)
            return _remote(blk, blk, s_ici.at[nw * slot + w], r_ici.at[nw * slot + w], (dev[0], dev[1], c))

        def d2d(w, slot, out, k, cc):
            blk = half(out, k, cc)
            return _remote(blk, blk, s_d2d.at[nw * slot + w], r_d2d.at[nw * slot + w], sib)

        sent = []
        for slot, dev in enumerate((first, second)):
            for w, out in enumerate(outs):
                sent.append(ici(w, slot, out, chip, dev))
                sent[-1].start()
        for slot, k, dev in ((0, first_chip, first), (1, second_chip, second), (2, diag_chip, second)):
            for w, out in enumerate(outs):
                ici(w, slot, out, k, dev).wait_recv()
                if slot == 0:
                    sent.append(ici(w, 2, out, k, second))
                    sent[-1].start()
                sent.append(d2d(w, slot, out, k, c))
                sent[-1].start()
        for slot, k in ((0, second_chip), (1, first_chip), (2, diag_chip)):
            for w, out in enumerate(outs):
                d2d(w, slot, out, k, 1 - c).wait_recv()
        for cp in sent:
            cp.wait_send()

    vmem = pl.BlockSpec(memory_space=pltpu.VMEM)
    return pl.pallas_call(
        body,
        out_shape=[jax.ShapeDtypeStruct((N_CHIPS,) + w.shape, BF16) for w in weights],
        in_specs=[vmem] * nw, out_specs=[vmem] * nw,
        scratch_shapes=[pltpu.SemaphoreType.DMA((3 * nw,))] * 4,
        compiler_params=pltpu.CompilerParams(vmem_limit_bytes=VMEM_LIMIT), name="ag_weights")(*weights)


_HBM = pl.BlockSpec(memory_space=pltpu.HBM)
_SEM = pl.BlockSpec(memory_space=pltpu.SEMAPHORE)
_ANY = pl.BlockSpec(memory_space=pl.ANY)
_DATAFLOW = pltpu.SideEffectType.DATAFLOW_SIDE_EFFECTING


def _in_hbm(a):
    return pltpu.with_memory_space_constraint(a, pltpu.HBM)


def _exchange_copies(gather, srcs, lands, send_sems, recv_sems):
    nw = len(srcs)
    x, y, c, chip, peers, peer_chip = _place()
    pairs = []
    for m, (px, py) in enumerate(peers):
        for w in range(nw):
            sems = (send_sems.at[nw * m + w], recv_sems.at[nw * m + w], (px, py, c))
            if gather:
                pairs.append((_remote(srcs[w], lands[w].at[chip], *sems),
                              _remote(srcs[w], lands[w].at[peer_chip[m]], *sems)))
            else:
                pairs.append((_remote(srcs[w].at[m], lands[w].at[m], *sems),) * 2)
    return pairs


def _exchange_start(gather, srcs, after, name):
    nw = len(srcs)
    n_copies = 3 * nw

    def body(*refs):
        send_sems, recv_sems = refs[2 * nw + 1], refs[2 * nw + 2]
        for start, _ in _exchange_copies(gather, refs[:nw], refs[nw:2 * nw], send_sems, recv_sems):
            start.start()
        refs[-1][...] = jnp.zeros_like(refs[-1])

    lands = [_in_hbm(lax.empty(((N_CHIPS,) + s.shape) if gather else s.shape, s.dtype)) for s in srcs]
    return pl.pallas_call(
        body, name=name,
        out_shape=(pltpu.SemaphoreType.DMA((n_copies,)), pltpu.SemaphoreType.DMA((n_copies,)))
        + tuple(pltpu.HBM(s.shape, s.dtype) for s in srcs)
        + tuple(pltpu.HBM(l.shape, l.dtype) for l in lands)
        + (jax.ShapeDtypeStruct((8, LANES), F32),),
        in_specs=[_HBM] * (2 * nw) + [_ANY],
        out_specs=(_SEM, _SEM) + (_HBM,) * (2 * nw) + (pl.BlockSpec(memory_space=pltpu.VMEM),),
        input_output_aliases={i: 2 + i for i in range(2 * nw)},
        compiler_params=pltpu.CompilerParams(has_side_effects=_DATAFLOW),
    )(*[_in_hbm(s) for s in srcs], *lands, after)


def _exchange_wait(gather, started, after, name):
    nw = (len(started) - 3) // 2
    send_sems, recv_sems = started[0], started[1]
    thru = started[2:2 + 2 * nw]

    def body(*refs):
        for _, arrival in _exchange_copies(gather, refs[:nw], refs[nw:2 * nw], refs[2 * nw], refs[2 * nw + 1]):
            arrival.wait_send()
            arrival.wait_recv()

    outs = pl.pallas_call(
        body, name=name,
        out_shape=tuple(pltpu.HBM(t.shape, t.dtype) for t in thru),
        in_specs=[_HBM] * (2 * nw) + [_SEM, _SEM, _ANY], out_specs=(_HBM,) * (2 * nw),
        input_output_aliases={i: i for i in range(2 * nw)},
        compiler_params=pltpu.CompilerParams(has_side_effects=_DATAFLOW),
    )(*thru, send_sems, recv_sems, after)
    return outs[nw:]


def _reduce_first(stacks):
    ns = len(stacks)
    halves = [s.shape[1] // 2 for s in stacks]
    row_block = 32

    def body(*refs):
        gs, sends, owns = refs[:ns], refs[ns:2 * ns], refs[2 * ns:3 * ns]
        ras, s_sem, r_sem = refs[3 * ns:4 * ns], refs[4 * ns], refs[4 * ns + 1]
        x, y, c, chip, peers, peer_chip = _place()
        swaps = []
        for w in range(ns):
            theirs = gs[w].at[:, pl.ds(pl.multiple_of((1 - c) * halves[w], 8), halves[w]), :]
            swaps.append(_remote(theirs, ras[w], s_sem.at[w], r_sem.at[w], (x, y, 1 - c)))
            swaps[-1].start()
        for w in range(ns):
            n = halves[w]
            swaps[w].wait_recv()

            def sums(i, carry, w=w, n=n):
                r0 = pl.multiple_of(i * row_block, row_block)
                blk = pl.ds(r0, row_block)
                mine = pl.ds(pl.multiple_of(c * n + r0, 8), row_block)
                for m in range(3):
                    sends[w][m, blk, :] = (gs[w][peer_chip[m], mine, :] + ras[w][peer_chip[m], blk, :]).astype(BF16)
                owns[w][blk, :] = gs[w][chip, mine, :] + ras[w][chip, blk, :]
                return carry
            lax.fori_loop(0, n // row_block, sums, 0)
        for cp in swaps:
            cp.wait_send()

    vmem = pl.BlockSpec(memory_space=pltpu.VMEM)
    outs = pl.pallas_call(
        body,
        out_shape=[jax.ShapeDtypeStruct((3, n, s.shape[2]), BF16) for n, s in zip(halves, stacks)]
        + [jax.ShapeDtypeStruct((n, s.shape[2]), F32) for n, s in zip(halves, stacks)],
        in_specs=[vmem] * ns, out_specs=[vmem] * (2 * ns),
        scratch_shapes=[pltpu.VMEM((N_CHIPS, n, s.shape[2]), F32) for n, s in zip(halves, stacks)]
        + [pltpu.SemaphoreType.DMA((ns,)), pltpu.SemaphoreType.DMA((ns,))],
        compiler_params=pltpu.CompilerParams(vmem_limit_bytes=VMEM_LIMIT), name="reduce_first")(*stacks)
    return outs[:ns], outs[ns:]


def _reduce_last(owns, landed, g_small):
    ns = len(owns)
    row_block = 32
    hs = SMALL_ROWS // 2

    def body(*refs):
        own_refs, land_refs, gsm_ref = refs[:ns], refs[ns:2 * ns], refs[2 * ns]
        out_refs, osm_ref = refs[2 * ns + 1:3 * ns + 1], refs[3 * ns + 1]
        ra_sm, p_sm, s_sem, r_sem, sm_s, sm_r = refs[3 * ns + 2:]
        x, y, c, chip, peers, peer_chip = _place()
        sib = (x, y, 1 - c)
        half = lambda cc: pl.ds(pl.multiple_of(cc * hs, 8), hs)
        sm_a = _remote(gsm_ref.at[half(1 - c), :], ra_sm, sm_s.at[0], sm_r.at[0], sib)
        sm_a.start()
        swaps = [sm_a]
        for w in range(ns):
            n = own_refs[w].shape[0]

            def total(i, carry, w=w, n=n):
                r0 = pl.multiple_of(i * row_block, row_block)
                blk = pl.ds(r0, row_block)
                acc = own_refs[w][blk, :]
                for m in range(3):
                    acc = acc + land_refs[w][m, blk, :].astype(F32)
                out_refs[w][pl.ds(pl.multiple_of(c * n + r0, 8), row_block), :] = acc
                return carry
            lax.fori_loop(0, n // row_block, total, 0)
            mine = out_refs[w].at[pl.ds(pl.multiple_of(c * n, 8), n), :]
            swaps.append(_remote(mine, mine, s_sem.at[w], r_sem.at[w], sib))
            swaps[-1].start()
        sm_a.wait_recv()
        p_sm[chip] = gsm_ref[half(c), :] + ra_sm[...]
        for m, (px, py) in enumerate(peers):
            swaps.append(_remote(p_sm.at[chip], p_sm.at[chip], sm_s.at[1 + m], sm_r.at[1 + m], (px, py, c)))
            swaps[-1].start()
        for w in range(ns):
            n = own_refs[w].shape[0]
            theirs = out_refs[w].at[pl.ds(pl.multiple_of((1 - c) * n, 8), n), :]
            _remote(theirs, theirs, s_sem.at[w], r_sem.at[w], sib).wait_recv()
        for m, (px, py) in enumerate(peers):
            _remote(p_sm.at[chip], p_sm.at[peer_chip[m]], sm_s.at[1 + m], sm_r.at[1 + m], (px, py, c)).wait_recv()
        osm_ref[half(c), :] = (p_sm[0] + p_sm[1]) + (p_sm[2] + p_sm[3])
        swaps.append(_remote(osm_ref.at[half(c), :], osm_ref.at[half(c), :], sm_s.at[4], sm_r.at[4], sib))
        swaps[-1].start()
        _remote(osm_ref.at[half(1 - c), :], osm_ref.at[half(1 - c), :], sm_s.at[4], sm_r.at[4], sib).wait_recv()
        for cp in swaps:
            cp.wait_send()

    vmem = pl.BlockSpec(memory_space=pltpu.VMEM)
    return pl.pallas_call(
        body, out_shape=[jax.ShapeDtypeStruct((2 * o.shape[0], o.shape[1]), F32) for o in owns]
        + [jax.ShapeDtypeStruct((SMALL_ROWS, LANES), F32)],
        in_specs=[vmem] * (2 * ns + 1), out_specs=[vmem] * (ns + 1),
        scratch_shapes=[pltpu.VMEM((hs, LANES), F32), pltpu.VMEM((N_CHIPS, hs, LANES), F32),
                        pltpu.SemaphoreType.DMA((ns,)), pltpu.SemaphoreType.DMA((ns,)),
                        pltpu.SemaphoreType.DMA((5,)), pltpu.SemaphoreType.DMA((5,))],
        compiler_params=pltpu.CompilerParams(vmem_limit_bytes=VMEM_LIMIT),
        name="reduce_last")(*owns, *landed, g_small)


_SMALL_PARTS = (("g_norm", 8, 8), ("w_s", 512, 512), ("b_s", 4, 8), ("g_v", 2, 8), ("g_mem", 8, 8),
                ("g_final", 8, 8), ("loss", 1, 8))
_LOSS_ROW = SMALL_ROWS - 8
assert sum(p for _, _, p in _SMALL_PARTS) == SMALL_ROWS


def _pack_small(parts, loss=None):
    loss_row = jnp.zeros((1, LANES), F32) if loss is None else jnp.broadcast_to(loss.reshape(1, 1), (1, LANES))
    rows = []
    for (name, used, padded), p in zip(_SMALL_PARTS, list(parts) + [loss_row]):
        p = p.reshape(used, LANES)
        if padded > used:
            p = jnp.pad(p, ((0, padded - used), (0, 0)))
        rows.append(p)
    return jnp.concatenate(rows, axis=0)


def _unpack_small(packed, shapes):
    out = []
    off = 0
    for (name, used, padded), shape in zip(_SMALL_PARTS, shapes):
        out.append(packed[off:off + used].reshape(shape))
        off += padded
    return out


def _local_step(x, mem, target, g_norm, w_in, w_s, b_s, g_v, g_mem, late_weights, g_final,
                fwd_token=None, on_dw=None):
    B, S, _ = x.shape
    x2d = x.reshape(B * S, D_MODEL)
    t2d = target.reshape(B * S, D_MODEL)
    mem2d = mem.reshape(B * N_MEM, D_MODEL)

    proj = _inproj_fwd(x2d, g_norm if fwd_token is None else g_norm + fwd_token[0:1, 0:1], w_in)
    w_kv, w_out = late_weights(proj)
    kv = _kv_fwd(mem2d, g_mem, w_kv)
    a, lse = _attn_fwd(proj, B, S)
    w_sT = jnp.swapaxes(w_s, 1, 2)
    b_tab = jnp.repeat(b_s.T, HEAD_DIM, axis=1)
    (dx2, da, drest, loss, d_wout, d_ws, d_bs, d_gv, d_gf, dkv) = _mid(
        x2d, t2d, a, proj, kv, w_s, w_sT, b_tab, g_v, w_out, g_final, B, S)
    d_wkv, d_gmem = _kv_bwd(mem2d, g_mem, w_kv, dkv)
    dq, dk, dv = _attn_bwd(proj, a, lse, da, B, S)
    d_win = _inproj_bwd_dw(dq, dk, dv, drest, x2d, g_norm)
    g_norm_dx = g_norm if on_dw is None else g_norm + on_dw(d_win, d_wkv, d_wout)[0:1, 0:1]
    grad_x, d_gnorm = _inproj_bwd_dx(dq, dk, dv, drest, x2d, dx2, g_norm_dx, w_in)
    d_bs = d_bs[:, :N_SGU_GROUPS].T
    return (loss[0, 0], grad_x.reshape(B, S, D_MODEL),
            dict(g_norm=d_gnorm, w_in=d_win, w_s=d_ws, b_s=d_bs, g_v=d_gv, g_mem=d_gmem, w_kv=d_wkv,
                 w_out=d_wout, g_final=d_gf))


def kernel(x, mem, g_norm, w_in, w_sgu_spatial, b_sgu_spatial, g_sgu_v, g_mem, w_mem_kv, w_out, g_final, loss_target, m_g_norm, m_w_in, m_w_sgu_spatial, m_b_sgu_spatial, m_g_sgu_v, m_g_mem, m_w_mem_kv, m_w_out, m_g_final, v_g_norm, v_w_in, v_w_sgu_spatial, v_b_sgu_spatial, v_g_sgu_v, v_g_mem, v_w_mem_kv, v_w_out, v_g_final):
    t = lambda w: jnp.swapaxes(w[0], 0, 1)
    chip = 2 * lax.axis_index("x") + lax.axis_index("y")
    (win_all,) = _ag_weights([t(w_in)])
    w_in_full = win_all.reshape(-1, win_all.shape[-1])
    late_shards = [w_mem_kv[0].astype(BF16), w_out[0].astype(BF16)]
    late = _exchange_start(True, late_shards, win_all, "gather_late_start")

    def late_weights(proj):
        full = []
        for shard, landed in zip(late_shards, _exchange_wait(True, late, proj, "gather_late_wait")):
            landed = lax.dynamic_update_slice(landed, shard[None], (chip, 0, 0))
            full.append(landed.reshape(-1, landed.shape[-1]))
        return full

    scatter = {}

    def on_dw(d_win, d_wkv, d_wout):
        stacks = [d_win.reshape((N_CHIPS, w_in.shape[2], w_in.shape[1])),
                  d_wkv.reshape((N_CHIPS,) + w_mem_kv.shape[1:]), d_wout.reshape((N_CHIPS,) + w_out.shape[1:])]
        sends, scatter["own"] = _reduce_first(stacks)
        scatter["started"] = _exchange_start(False, list(sends), scatter["own"][0], "scatter_start")
        return scatter["started"][-1]

    loss, grad_x, g = _local_step(
        x, mem, loss_target, g_norm, w_in_full, w_sgu_spatial[0], b_sgu_spatial[0], g_sgu_v, g_mem,
        late_weights, g_final.reshape(1, D_MODEL), fwd_token=late[-1], on_dw=on_dw)

    small_names = ("g_norm", "w_s", "b_s", "g_v", "g_mem", "g_final")
    g_small = _pack_small([g[n] for n in small_names], loss)
    landed = _exchange_wait(False, scatter["started"], g_small, "scatter_wait")
    gr_in, gr_kv, gr_out, gr_small = _reduce_last(scatter["own"], landed, g_small)
    loss = gr_small[_LOSS_ROW, 0]

    small_w = (g_norm, w_sgu_spatial, b_sgu_spatial, g_sgu_v, g_mem, g_final)
    small_m = (m_g_norm, m_w_sgu_spatial, m_b_sgu_spatial, m_g_sgu_v, m_g_mem, m_g_final)
    small_v = (v_g_norm, v_w_sgu_spatial, v_b_sgu_spatial, v_g_sgu_v, v_g_mem, v_g_final)
    shapes = [w.shape for w in small_w]
    d_small, nm_small, nv_small = _adamw(_pack_small(small_w), gr_small, _pack_small(small_m),
                                         _pack_small(small_v), "adamw_small")
    d_in, nm_in, nv_in = _adamw(t(w_in), gr_in, t(m_w_in), t(v_w_in), "adamw_w_in")
    gr_in, d_in, nm_in, nv_in = [jnp.swapaxes(z, 0, 1) for z in (gr_in, d_in, nm_in, nv_in)]
    d_kv, nm_kv, nv_kv = _adamw(w_mem_kv[0], gr_kv, m_w_mem_kv[0], v_w_mem_kv[0], "adamw_w_kv")
    d_out, nm_out, nv_out = _adamw(w_out[0], gr_out, m_w_out[0], v_w_out[0], "adamw_w_out")

    def leaves(packed, big_in, big_kv, big_out):
        s_norm, s_ws, s_bs, s_gv, s_gmem, s_gf = _unpack_small(packed, shapes)
        return [s_norm, big_in[None], s_ws, s_bs, s_gv, s_gmem, big_kv[None], big_out[None], s_gf]

    return (loss, grad_x, *leaves(gr_small, gr_in, gr_kv, gr_out), *leaves(d_small, d_in, d_kv, d_out),
            *leaves(nm_small, nm_in, nm_kv, nm_out), *leaves(nv_small, nv_in, nv_kv, nv_out))
```

```python
import functools

import jax
import jax.numpy as jnp
from jax import lax
from jax.experimental import pallas as pl
from jax.experimental.pallas import tpu as pltpu

F32 = jnp.float32
BF16 = jnp.bfloat16
MESH = pl.DeviceIdType.MESH

D_MODEL = 1024
ATTN_WIDTH = 512
SGU_WIDTH = 256
MEM_WIDTH = 256
N_MEM = 256
IN_COLS = 3328
QKV_COLS = 3 * ATTN_WIDTH
REST_COLS = IN_COLS - QKV_COLS
SGU_CHUNK = 128
N_SGU_GROUPS = 4
EPS = 1e-6
NEG_INF = -1e30
DILATIONS = (1, 4, 16)
RADIUS = 64
Q_BLOCK = 128
LANES = 128
HEAD_DIM = 64

ADAM_LR = 0.001
ADAM_B1 = 0.9
ADAM_B2 = 0.999
ADAM_EPS = 1e-08
ADAM_WD = 0.01
ADAM_STEP = 10

N_CHIPS = 4
VMEM_LIMIT = 56 * 1024 * 1024
SMALL_ROWS = 560


def _params(sem=None, vmem=VMEM_LIMIT):
    return pltpu.CompilerParams(dimension_semantics=sem, vmem_limit_bytes=vmem)


def _nn(a, b):
    return jnp.dot(a, b, preferred_element_type=F32)


def _nt(a, b):
    return lax.dot_general(a, b, (((1,), (1,)), ((), ())), preferred_element_type=F32)


def _tn(a, b):
    return lax.dot_general(a, b, (((0,), (0,)), ((), ())), preferred_element_type=F32)


def _rms(x):
    r = lax.rsqrt(jnp.mean(x * x, axis=-1, keepdims=True) + EPS)
    return r, x * r


def _head_masks():
    lane = lax.broadcasted_iota(jnp.int32, (1, LANES), 1)
    lo = lane < HEAD_DIM
    return lo, (lo.astype(F32), (~lo).astype(F32))


def _silu_parts(z):
    s = jax.nn.sigmoid(z)
    return z * s, s * (1.0 + z * (1.0 - s))


def _gelu_parts(x):
    c = 0.7978845608028654
    x2 = x * x
    t = jnp.tanh(c * (x + 0.044715 * (x * x2)))
    val = 0.5 * x * (1.0 + t)
    grad = 0.5 * (1.0 + t) + 0.5 * x * (1.0 - t * t) * (c * (1.0 + 3.0 * 0.044715 * x2))
    return val, grad


def _after(tokens):
    return [pl.BlockSpec(memory_space=pl.ANY)] * len(tokens)


def _inproj_fwd(x2d, g_norm, w_in_t, after=()):
    T = x2d.shape[0]
    tm = 512

    def body(x_ref, g_ref, w_ref, *rest):
        o_ref = rest[-1]
        _, xh = _rms(x_ref[...])
        h = (xh * g_ref[...]).astype(BF16)
        o_ref[...] = _nt(h, w_ref[...])

    return pl.pallas_call(
        body, grid=(T // tm,),
        in_specs=[pl.BlockSpec((tm, D_MODEL), lambda i: (i, 0)),
                  pl.BlockSpec((1, D_MODEL), lambda i: (0, 0)),
                  pl.BlockSpec((IN_COLS, D_MODEL), lambda i: (0, 0))] + _after(after),
        out_specs=pl.BlockSpec((tm, IN_COLS), lambda i: (i, 0)),
        out_shape=jax.ShapeDtypeStruct((T, IN_COLS), F32),
        compiler_params=_params(("arbitrary",)), name="inproj_fwd")(x2d, g_norm, w_in_t, *after)


def _kv_fwd(mem2d, g_mem, w_kv):
    Tm = mem2d.shape[0]

    def body(m_ref, g_ref, w_ref, o_ref):
        _, mh = _rms(m_ref[...])
        o_ref[...] = _nn((mh * g_ref[...]).astype(BF16), w_ref[...])

    return pl.pallas_call(
        body, out_shape=jax.ShapeDtypeStruct((Tm, 2 * MEM_WIDTH), F32),
        compiler_params=_params(), name="kv_fwd")(mem2d, g_mem, w_kv)


def _kv_bwd(mem2d, g_mem, w_kv, dkv):
    Tm = mem2d.shape[0]

    def body(m_ref, g_ref, w_ref, dkv_ref, dw_ref, dg_ref):
        _, mh = _rms(m_ref[...])
        memn = (mh * g_ref[...]).astype(BF16)
        dkvb = dkv_ref[...].astype(BF16)
        dw_ref[...] = _tn(memn, dkvb)
        dmemn = _nt(dkvb, w_ref[...])
        dg_ref[...] = jnp.sum(dmemn * mh, axis=0, keepdims=True)

    return pl.pallas_call(
        body, out_shape=(jax.ShapeDtypeStruct((D_MODEL, 2 * MEM_WIDTH), F32),
                         jax.ShapeDtypeStruct((1, D_MODEL), F32)),
        compiler_params=_params(), name="kv_bwd")(mem2d, g_mem, w_kv, dkv)


def _attn_geometry(S):
    geom = []
    for d in DILATIONS:
        L = S // d
        assert L % Q_BLOCK == 0
        geom.append((d, L, min(2 * Q_BLOCK, L), L // Q_BLOCK))
    return geom


def _init_bias(bias_scr, geom, hp):
    row = lax.broadcasted_iota(jnp.int32, (Q_BLOCK, 2 * Q_BLOCK), 0)
    col = lax.broadcasted_iota(jnp.int32, (Q_BLOCK, 2 * Q_BLOCK), 1)
    for j in (0, 1):
        bits = (126 - (2 * hp + j)) * (1 << 23)
        slope = lax.bitcast_convert_type(jnp.full((1, 1), bits, jnp.int32), F32)
        for di, (d, _, _, _) in enumerate(geom):
            for cls, off in enumerate((0, -RADIUS, -2 * RADIUS)):
                dist = jnp.abs(col - row + off)
                bias_scr[di * 6 + cls * 2 + j] = jnp.where(
                    dist <= RADIUS, -(slope * float(d)) * dist.astype(F32), NEG_INF)


def _block_slices(d, L, KW, nqb, r, qb):
    qs = qb * Q_BLOCK
    ks = jnp.clip(qs - RADIUS, 0, L - KW)
    cls = jnp.where(qb == 0, 0, jnp.where(qb == nqb - 1, 2, 1))
    if d == 1:
        qsl = pl.ds(pl.multiple_of(qs, Q_BLOCK), Q_BLOCK)
        ksl = pl.ds(pl.multiple_of(ks, RADIUS), KW)
    else:
        qsl = pl.ds(r + qs * d, Q_BLOCK, stride=d)
        ksl = pl.ds(r + ks * d, KW, stride=d)
    return qsl, ksl, cls


def _for_groups(geom, group, fn):
    for di, (d, L, KW, nqb) in enumerate(geom):
        assert (d * nqb) % group == 0

        def step(it, carry, di=di, d=d, L=L, KW=KW, nqb=nqb):
            slices = []
            for g in range(group):
                i = it * group + g
                slices.append(_block_slices(d, L, KW, nqb, i // nqb, i % nqb))
            fn(di, KW, slices)
            return carry
        lax.fori_loop(0, d * nqb // group, step, 0)


def _attn_fwd(proj, B, S):
    T = B * S
    geom = _attn_geometry(S)
    n_pairs = ATTN_WIDTH // LANES

    def body(q_ref, k_ref, v_ref, a_ref, lse_ref, bias_scr, *per_dilation):
        o_scr, m_scr, l_scr = per_dilation[0:3], per_dilation[3:6], per_dilation[6:9]
        lo, hm = _head_masks()
        _init_bias(bias_scr, geom, pl.program_id(1))

        def group(di, KW, slices):
            chains = [(g, j) for g in range(len(slices)) for j in (0, 1)]
            q = [q_ref[qsl, :] for qsl, _, _ in slices]
            kw = [k_ref[ksl, :].astype(BF16) for _, ksl, _ in slices]
            vw = [v_ref[ksl, :].astype(BF16) for _, ksl, _ in slices]
            s = {(g, j): _nt((q[g] * (hm[j] * 0.125)).astype(BF16), kw[g])
                 + bias_scr[di * 6 + slices[g][2] * 2 + j, :, pl.ds(0, KW)] for g, j in chains}
            m = {c: jnp.max(s[c], axis=1, keepdims=True) for c in chains}
            p = {c: jnp.exp(s[c] - m[c]) for c in chains}
            l = {c: jnp.sum(p[c], axis=1, keepdims=True) for c in chains}
            o = {(g, j): _nn(p[(g, j)].astype(BF16), vw[g]) for g, j in chains}
            for g, (qsl, _, _) in enumerate(slices):
                o_scr[di][qsl, :] = jnp.where(lo, o[(g, 0)], o[(g, 1)])
                m_scr[di][qsl, :] = jnp.where(lo, m[(g, 0)], m[(g, 1)])
                l_scr[di][qsl, :] = jnp.where(lo, l[(g, 0)], l[(g, 1)])

        _for_groups(geom, 8, group)

        rows_per = 256

        def combine(i, carry):
            rows = pl.ds(pl.multiple_of(i * rows_per, rows_per), rows_per)
            ms = [m_scr[di][rows, :] for di in range(3)]
            mx = jnp.maximum(jnp.maximum(ms[0], ms[1]), ms[2])
            num = 0.0
            den = 0.0
            for di in range(3):
                w = jnp.exp(ms[di] - mx)
                num = num + w * o_scr[di][rows, :]
                den = den + w * l_scr[di][rows, :]
            a_ref[rows, :] = num / den
            lse_ref[rows, :] = mx + jnp.log(den)
            return carry

        lax.fori_loop(0, S // rows_per, combine, 0)

    blk = lambda off: pl.BlockSpec((S, LANES), lambda b, h, off=off: (b, off + h))
    out_blk = pl.BlockSpec((S, LANES), lambda b, h: (b, h))
    return pl.pallas_call(
        body, grid=(B, n_pairs),
        in_specs=[blk(0), blk(n_pairs), blk(2 * n_pairs)],
        out_specs=[out_blk, out_blk],
        out_shape=[jax.ShapeDtypeStruct((T, ATTN_WIDTH), F32)] * 2,
        scratch_shapes=[pltpu.VMEM((18, Q_BLOCK, 2 * Q_BLOCK), F32)] + [pltpu.VMEM((S, LANES), F32)] * 9,
        compiler_params=_params(("arbitrary", "arbitrary")), name="attn_fwd")(proj, proj, proj)


def _attn_bwd(proj, a, lse, da, B, S):
    T = B * S
    geom = _attn_geometry(S)
    n_pairs = ATTN_WIDTH // LANES

    def body(q_ref, k_ref, v_ref, a_ref, lse_ref, do_ref, dq_ref, dk_ref, dv_ref,
             bias_scr, dq_scr, dk_scr, dv_scr):
        _, hm = _head_masks()
        _init_bias(bias_scr, geom, pl.program_id(1))
        dq_scr[...] = jnp.zeros_like(dq_scr)
        dk_scr[...] = jnp.zeros_like(dk_scr)
        dv_scr[...] = jnp.zeros_like(dv_scr)

        def group(di, KW, slices):
            n = len(slices)
            chains = [(g, j) for g in range(n) for j in (0, 1)]
            q = [q_ref[qsl, :] for qsl, _, _ in slices]
            do = [do_ref[qsl, :] for qsl, _, _ in slices]
            doa = [do[g] * a_ref[slices[g][0], :] for g in range(n)]
            lse_q = [lse_ref[qsl, :] for qsl, _, _ in slices]
            kw = [k_ref[ksl, :].astype(BF16) for _, ksl, _ in slices]
            vw = [v_ref[ksl, :].astype(BF16) for _, ksl, _ in slices]
            qj = {(g, j): (q[g] * (hm[j] * 0.125)).astype(BF16) for g, j in chains}
            doj = {(g, j): (do[g] * hm[j]).astype(BF16) for g, j in chains}
            s = {(g, j): _nt(qj[(g, j)], kw[g])
                 + bias_scr[di * 6 + slices[g][2] * 2 + j, :, pl.ds(0, KW)] for g, j in chains}
            dp = {(g, j): _nt(doj[(g, j)], vw[g]) for g, j in chains}
            dsum = {(g, j): jnp.sum(doa[g] * hm[j], axis=1, keepdims=True) for g, j in chains}
            p = {(g, j): jnp.exp(s[(g, j)] - lse_q[g][:, HEAD_DIM * j:HEAD_DIM * j + 1]) for g, j in chains}
            ds = {c: (p[c] * (dp[c] - dsum[c])).astype(BF16) for c in chains}
            pb = {c: p[c].astype(BF16) for c in chains}
            dq = [_nn(ds[(g, 0)], kw[g]) * (hm[0] * 0.125) + _nn(ds[(g, 1)], kw[g]) * (hm[1] * 0.125)
                  for g in range(n)]
            both = lambda t, g: jnp.concatenate([t[(g, 0)], t[(g, 1)]], axis=0)
            dkw = [_tn(both(ds, g), both(qj, g)) for g in range(n)]
            dvw = [_tn(both(pb, g), both(doj, g)) for g in range(n)]
            for g, (qsl, ksl, _) in enumerate(slices):
                dq_scr[qsl, :] = dq_scr[qsl, :] + dq[g]
                dk_scr[ksl, :] = dk_scr[ksl, :] + dkw[g]
                dv_scr[ksl, :] = dv_scr[ksl, :] + dvw[g]

        _for_groups(geom, 4, group)
        dq_ref[...] = dq_scr[...].astype(BF16)
        dk_ref[...] = dk_scr[...].astype(BF16)
        dv_ref[...] = dv_scr[...].astype(BF16)

    blk = lambda off: pl.BlockSpec((S, LANES), lambda b, h, off=off: (b, off + h))
    return pl.pallas_call(
        body, grid=(B, n_pairs),
        in_specs=[blk(0), blk(n_pairs), blk(2 * n_pairs), blk(0), blk(0), blk(0)],
        out_specs=[blk(0), blk(0), blk(0)],
        out_shape=[jax.ShapeDtypeStruct((T, ATTN_WIDTH), BF16)] * 3,
        scratch_shapes=[pltpu.VMEM((18, Q_BLOCK, 2 * Q_BLOCK), F32),
                        pltpu.VMEM((S, LANES), F32),
                        pltpu.VMEM((S, LANES), F32),
                        pltpu.VMEM((S, LANES), F32)],
        compiler_params=_params(("arbitrary", "arbitrary")), name="attn_bwd")(proj, proj, proj, a, lse, da)


def _mid(x2d, t2d, a, proj, kv, w_s, w_sT, b_tab, g_v, w_out, g_final, B, S):
    T = B * S
    tm = 512
    nt = S // tm
    n_sub = 1
    sub = tm // n_sub
    n_chunks = sub // SGU_CHUNK

    def body(*refs):
        tiles_in, consts, tiles_out, accs = refs[:9], refs[9:16], refs[16:19], refs[19:]
        for sb in range(n_sub):
            rows = lambda r: r.at[pl.ds(sb * sub, sub), :]
            sub_body(*[rows(r) for r in tiles_in], *consts, *[rows(r) for r in tiles_out], *accs, sb=sb)

    def sub_body(x_ref, t_ref, a_ref, za_ref, ub_ref, vb_ref, zb_ref, qm_ref, zm_ref, kv_ref,
                 ws_ref, wsT_ref, btab_ref, gv_ref, wout_ref, gf_ref,
                 dx2_ref, da_ref, drest_ref, loss_ref, dwout_ref, dws_ref, dbs_ref, dgv_ref, dgf_ref, dkv_ref,
                 dbtab_scr, sb):
        b = pl.program_id(0)
        t = pl.program_id(1)
        first = jnp.logical_and(jnp.logical_and(b == 0, t == 0), sb == 0)
        last = jnp.logical_and(jnp.logical_and(b == B - 1, t == nt - 1), sb == n_sub - 1)
        _, hm = _head_masks()
        lane_g = lax.broadcasted_iota(jnp.int32, (1, SGU_WIDTH), 1) // HEAD_DIM
        gm = [(lane_g == g).astype(F32) for g in range(N_SGU_GROUPS)]

        @pl.when(first)
        def _():
            loss_ref[...] = jnp.zeros_like(loss_ref)
            dwout_ref[...] = jnp.zeros_like(dwout_ref)
            dws_ref[...] = jnp.zeros_like(dws_ref)
            dbs_ref[...] = jnp.zeros_like(dbs_ref)
            dgv_ref[...] = jnp.zeros_like(dgv_ref)
            dgf_ref[...] = jnp.zeros_like(dgf_ref)
            dbtab_scr[...] = jnp.zeros_like(dbtab_scr)

        @pl.when(jnp.logical_and(t == 0, sb == 0))
        def _():
            dkv_ref[...] = jnp.zeros_like(dkv_ref)

        sil_a, dsil_a = _silu_parts(za_ref[...])
        a_val = a_ref[...]
        gated_a = sil_a * a_val

        u, du_dub = _gelu_parts(ub_ref[...])
        vv, dvv_dvb = _gelu_parts(vb_ref[...])
        rv, vhat = _rms(vv)
        gv = gv_ref[...]
        vn = (vhat * gv).astype(BF16)
        ws = [ws_ref[g].astype(BF16) for g in range(N_SGU_GROUPS)]
        wsT = [wsT_ref[g].astype(BF16) for g in range(N_SGU_GROUPS)]
        mixed = []
        for ci in range(n_chunks):
            vn_c = vn[ci * SGU_CHUNK:(ci + 1) * SGU_CHUNK, :]
            mc = btab_ref[...]
            for g in range(N_SGU_GROUPS):
                mc = mc + gm[g] * _nn(ws[g], vn_c)
            mixed.append(mc)
        mixed = jnp.concatenate(mixed, axis=0)
        sg = u * mixed
        sil_b, dsil_b = _silu_parts(zb_ref[...])
        gated_b = sil_b * sg

        kvv = kv_ref[...].astype(BF16)
        qm = qm_ref[...]
        probs, qs_m, mo_pairs = [], [], []
        for pr in range(2):
            cols = slice(pr * LANES, (pr + 1) * LANES)
            kp = kvv[:, pr * LANES:(pr + 1) * LANES]
            vp = kvv[:, MEM_WIDTH + pr * LANES:MEM_WIDTH + (pr + 1) * LANES]
            mo_p = 0.0
            for j in (0, 1):
                qj = (qm[:, cols] * (hm[j] * 0.125)).astype(BF16)
                s = _nt(qj, kp)
                e = jnp.exp(s - jnp.max(s, axis=1, keepdims=True))
                p = e / jnp.sum(e, axis=1, keepdims=True)
                mo_p = mo_p + _nn(p.astype(BF16), vp) * hm[j]
                probs.append(p)
                qs_m.append(qj)
            mo_pairs.append(mo_p)
        mo = jnp.concatenate(mo_pairs, axis=1)
        sil_m, dsil_m = _silu_parts(zm_ref[...])
        gated_m = sil_m * mo

        gated = jnp.concatenate([gated_a, gated_b, gated_m], axis=1).astype(BF16)
        wout = wout_ref[...]
        x2 = x_ref[...] + _nn(gated, wout)
        r2, xh2 = _rms(x2)
        gf = gf_ref[...]
        err = xh2 * gf - t_ref[...]
        loss_ref[...] += jnp.sum(err * err) * (0.5 / D_MODEL)

        dy = err * (1.0 / D_MODEL)
        dgf_ref[...] += jnp.sum(dy * xh2, axis=0, keepdims=True)
        gdy = dy * gf
        dx2 = r2 * (gdy - xh2 * jnp.mean(gdy * xh2, axis=1, keepdims=True))
        dx2_ref[...] = dx2
        dx2b = dx2.astype(BF16)
        dwout_ref[...] += _tn(gated, dx2b)
        dgated = _nt(dx2b, wout)
        dga = dgated[:, 0:ATTN_WIDTH]
        dgb = dgated[:, ATTN_WIDTH:ATTN_WIDTH + SGU_WIDTH]
        dgm = dgated[:, ATTN_WIDTH + SGU_WIDTH:]

        da_ref[...] = dga * sil_a
        dza = dga * a_val * dsil_a

        dsg = dgb * sil_b
        dzb = dgb * sg * dsil_b
        dub = dsg * mixed * du_dub
        dmixed = dsg * u
        dmixed_b = dmixed.astype(BF16)
        dvn = []
        dbtab = dbtab_scr[...]
        for ci in range(n_chunks):
            rows = slice(ci * SGU_CHUNK, (ci + 1) * SGU_CHUNK)
            dm_c = dmixed_b[rows, :]
            vn_c = vn[rows, :]
            dvn_c = 0.0
            for g in range(N_SGU_GROUPS):
                dvn_c = dvn_c + gm[g] * _nn(wsT[g], dm_c)
                dws_ref[g] += _nt((dmixed[rows, :] * gm[g]).astype(BF16), vn_c)
            dvn.append(dvn_c)
            dbtab = dbtab + dmixed[rows, :]
        dbtab_scr[...] = dbtab
        dvn = jnp.concatenate(dvn, axis=0)
        dgv_ref[...] += jnp.sum(dvn * vhat, axis=0, keepdims=True)
        tv = dvn * gv
        dvv = rv * (tv - vhat * jnp.mean(tv * vhat, axis=1, keepdims=True))
        dvb = dvv * dvv_dvb

        dmo = dgm * sil_m
        dzm = dgm * mo * dsil_m
        dqm_pairs = []
        dk_pairs, dv_pairs = [], []
        for pr in range(2):
            kp = kvv[:, pr * LANES:(pr + 1) * LANES]
            vp = kvv[:, MEM_WIDTH + pr * LANES:MEM_WIDTH + (pr + 1) * LANES]
            dmo_p = dmo[:, pr * LANES:(pr + 1) * LANES]
            dq_p = 0.0
            dk_p = 0.0
            dv_p = 0.0
            for j in (0, 1):
                p = probs[2 * pr + j]
                dmo_j = (dmo_p * hm[j]).astype(BF16)
                dp = _nt(dmo_j, vp)
                ds = (p * (dp - jnp.sum(dp * p, axis=1, keepdims=True))).astype(BF16)
                dq_p = dq_p + _nn(ds, kp) * (hm[j] * 0.125)
                dk_p = dk_p + _tn(ds, qs_m[2 * pr + j])
                dv_p = dv_p + _tn(p.astype(BF16), dmo_j)
            dqm_pairs.append(dq_p)
            dk_pairs.append(dk_p)
            dv_pairs.append(dv_p)
        dqm = jnp.concatenate(dqm_pairs, axis=1)
        dkv_ref[...] += jnp.concatenate(dk_pairs + dv_pairs, axis=1)

        drest_ref[...] = jnp.concatenate([dza, dub, dvb, dzb, dqm, dzm], axis=1).astype(BF16)

        @pl.when(last)
        def _():
            lane = lax.broadcasted_iota(jnp.int32, (1, LANES), 1)
            dbt = dbtab_scr[...]
            out = jnp.zeros((SGU_CHUNK, LANES), F32)
            for g in range(N_SGU_GROUPS):
                out = out + jnp.where(lane == g, jnp.sum(dbt * gm[g], axis=1, keepdims=True), 0.0)
            dbs_ref[...] = out

    tile = lambda w, cb: pl.BlockSpec((tm, w), lambda b, t, cb=cb: (b * nt + t, cb))
    const = lambda shape: pl.BlockSpec(shape, lambda b, t, n=len(shape): (0,) * n)
    return pl.pallas_call(
        body, grid=(B, nt),
        in_specs=[tile(D_MODEL, 0), tile(D_MODEL, 0), tile(ATTN_WIDTH, 0),
                  tile(ATTN_WIDTH, 3),
                  tile(SGU_WIDTH, 8), tile(SGU_WIDTH, 9), tile(SGU_WIDTH, 10),
                  tile(MEM_WIDTH, 11), tile(MEM_WIDTH, 12),
                  pl.BlockSpec((N_MEM, 2 * MEM_WIDTH), lambda b, t: (b, 0)),
                  const((N_SGU_GROUPS, SGU_CHUNK, SGU_CHUNK)), const((N_SGU_GROUPS, SGU_CHUNK, SGU_CHUNK)),
                  const((SGU_CHUNK, SGU_WIDTH)), const((1, SGU_WIDTH)),
                  const((D_MODEL, D_MODEL)), const((1, D_MODEL))],
        out_specs=[tile(D_MODEL, 0), tile(ATTN_WIDTH, 0), tile(REST_COLS, 0),
                   const((8, LANES)), const((D_MODEL, D_MODEL)),
                   const((N_SGU_GROUPS, SGU_CHUNK, SGU_CHUNK)), const((SGU_CHUNK, LANES)),
                   const((1, SGU_WIDTH)), const((1, D_MODEL)),
                   pl.BlockSpec((N_MEM, 2 * MEM_WIDTH), lambda b, t: (b, 0))],
        out_shape=[jax.ShapeDtypeStruct((T, D_MODEL), F32), jax.ShapeDtypeStruct((T, ATTN_WIDTH), F32),
                   jax.ShapeDtypeStruct((T, REST_COLS), BF16),
                   jax.ShapeDtypeStruct((8, LANES), F32), jax.ShapeDtypeStruct((D_MODEL, D_MODEL), F32),
                   jax.ShapeDtypeStruct((N_SGU_GROUPS, SGU_CHUNK, SGU_CHUNK), F32),
                   jax.ShapeDtypeStruct((SGU_CHUNK, LANES), F32),
                   jax.ShapeDtypeStruct((1, SGU_WIDTH), F32), jax.ShapeDtypeStruct((1, D_MODEL), F32),
                   jax.ShapeDtypeStruct((B * N_MEM, 2 * MEM_WIDTH), F32)],
        scratch_shapes=[pltpu.VMEM((SGU_CHUNK, SGU_WIDTH), F32)],
        compiler_params=_params(("arbitrary", "arbitrary")), name="mid")(
            x2d, t2d, a, proj, proj, proj, proj, proj, proj, kv, w_s, w_sT, b_tab, g_v, w_out, g_final)


def _inproj_bwd_dx(dq, dk, dv, drest, x2d, dx2, g_norm, w_in_t, after=()):
    T = x2d.shape[0]
    tm = 512
    W = ATTN_WIDTH

    def body(dq_ref, dk_ref, dv_ref, dr_ref, x_ref, dx2_ref, g_ref, w_ref, *rest):
        gx_ref, dg_ref = rest[-2:]

        @pl.when(pl.program_id(0) == 0)
        def _():
            dg_ref[...] = jnp.zeros_like(dg_ref)

        dh = (_nn(dq_ref[...], w_ref[0:W, :]) + _nn(dk_ref[...], w_ref[W:2 * W, :])
              + _nn(dv_ref[...], w_ref[2 * W:3 * W, :]) + _nn(dr_ref[...], w_ref[QKV_COLS:IN_COLS, :]))
        r, xh = _rms(x_ref[...])
        dg_ref[...] += jnp.sum(dh * xh, axis=0, keepdims=True)
        th = dh * g_ref[...]
        gx_ref[...] = r * (th - xh * jnp.mean(th * xh, axis=1, keepdims=True)) + dx2_ref[...]

    tile = lambda w: pl.BlockSpec((tm, w), lambda i: (i, 0))
    return pl.pallas_call(
        body, grid=(T // tm,),
        in_specs=[tile(W), tile(W), tile(W), tile(REST_COLS), tile(D_MODEL), tile(D_MODEL),
                  pl.BlockSpec((1, D_MODEL), lambda i: (0, 0)),
                  pl.BlockSpec((IN_COLS, D_MODEL), lambda i: (0, 0))] + _after(after),
        out_specs=[tile(D_MODEL), pl.BlockSpec((1, D_MODEL), lambda i: (0, 0))],
        out_shape=[jax.ShapeDtypeStruct((T, D_MODEL), F32), jax.ShapeDtypeStruct((1, D_MODEL), F32)],
        compiler_params=_params(("arbitrary",)), name="inproj_bwd_dx")(
            dq, dk, dv, drest, x2d, dx2, g_norm, w_in_t, *after)


def _inproj_bwd_dw(dq, dk, dv, drest, x2d, g_norm):
    T = x2d.shape[0]
    tm = 512
    W = ATTN_WIDTH

    def body(dq_ref, dk_ref, dv_ref, dr_ref, x_ref, g_ref, dw_ref):
        @pl.when(pl.program_id(0) == 0)
        def _():
            dw_ref[...] = jnp.zeros_like(dw_ref)

        _, xh = _rms(x_ref[...])
        h = (xh * g_ref[...]).astype(BF16)
        dw_ref[0:W, :] += _tn(dq_ref[...], h)
        dw_ref[W:2 * W, :] += _tn(dk_ref[...], h)
        dw_ref[2 * W:3 * W, :] += _tn(dv_ref[...], h)
        dw_ref[QKV_COLS:IN_COLS, :] += _tn(dr_ref[...], h)

    tile = lambda w: pl.BlockSpec((tm, w), lambda i: (i, 0))
    return pl.pallas_call(
        body, grid=(T // tm,),
        in_specs=[tile(W), tile(W), tile(W), tile(REST_COLS), tile(D_MODEL),
                  pl.BlockSpec((1, D_MODEL), lambda i: (0, 0))],
        out_specs=pl.BlockSpec((IN_COLS, D_MODEL), lambda i: (0, 0)),
        out_shape=jax.ShapeDtypeStruct((IN_COLS, D_MODEL), F32),
        compiler_params=_params(("arbitrary",)), name="inproj_bwd_dw")(dq, dk, dv, drest, x2d, g_norm)


def _adamw_update(w, g, m, v):
    nm = ADAM_B1 * m + (1.0 - ADAM_B1) * g
    nv = ADAM_B2 * v + (1.0 - ADAM_B2) * (g * g)
    m_hat = nm / (1.0 - ADAM_B1 ** ADAM_STEP)
    v_hat = nv / (1.0 - ADAM_B2 ** ADAM_STEP)
    return -ADAM_LR * (m_hat / (jnp.sqrt(v_hat) + ADAM_EPS) + ADAM_WD * w), nm, nv


def _adamw(w, g, m, v, name):
    R, C = w.shape
    br = max(r for r in range(8, 257, 8) if R % r == 0)

    def body(w_ref, g_ref, m_ref, v_ref, d_ref, nm_ref, nv_ref):
        d_ref[...], nm_ref[...], nv_ref[...] = _adamw_update(w_ref[...], g_ref[...], m_ref[...], v_ref[...])

    spec = pl.BlockSpec((br, C), lambda i: (i, 0))
    return pl.pallas_call(
        body, grid=(R // br,), in_specs=[spec] * 4, out_specs=[spec] * 3,
        out_shape=[jax.ShapeDtypeStruct((R, C), F32)] * 3,
        compiler_params=_params(("arbitrary",)), name=name)(w, g, m, v)


def _adamw_small(g_packed, ws, ms, vs):
    n = len(ws)

    def body(*refs):
        g_ref = refs[0]
        w_refs, m_refs, v_refs = refs[1:1 + n], refs[1 + n:1 + 2 * n], refs[1 + 2 * n:1 + 3 * n]
        outs = refs[1 + 3 * n:]
        off = 0
        for i, (_, used, padded) in enumerate(_SMALL_PARTS[:n]):
            g = g_ref[off:off + used, :]
            delta, nm, nv = _adamw_update(w_refs[i][...], g, m_refs[i][...], v_refs[i][...])
            outs[4 * i][...], outs[4 * i + 1][...], outs[4 * i + 2][...], outs[4 * i + 3][...] = g, delta, nm, nv
            off += padded

    outs = pl.pallas_call(
        body, out_shape=[jax.ShapeDtypeStruct(w.shape, F32) for w in ws for _ in range(4)],
        compiler_params=_params(), name="adamw_small")(g_packed, *ws, *ms, *vs)
    return [outs[4 * i:4 * i + 4] for i in range(n)]


def _place():
    x, y, c = lax.axis_index("x"), lax.axis_index("y"), lax.axis_index("c")
    chip = 2 * x + y
    peers = [(x, 1 - y), (1 - x, y), (1 - x, 1 - y)]
    peer_chip = [2 * px + py for px, py in peers]
    return x, y, c, chip, peers, peer_chip


def _remote(src, dst, send_sem, recv_sem, dev):
    return pltpu.make_async_remote_copy(src_ref=src, dst_ref=dst, send_sem=send_sem, recv_sem=recv_sem,
                                        device_id=dev, device_id_type=MESH)


def _ag_weights(weights, late=()):
    nw, nl = len(weights), len(late)

    def body(*refs):
        srcs, late_srcs = refs[:nw], refs[nw:nw + nl]
        outs, late_bf, late_land = (refs[nw + nl:2 * nw + nl], refs[2 * nw + nl:2 * nw + 2 * nl],
                                    refs[2 * nw + 2 * nl:2 * nw + 3 * nl])
        s_ici, r_ici, s_d2d, r_d2d = refs[2 * nw + 3 * nl:]
        x, y, c = lax.axis_index("x"), lax.axis_index("y"), lax.axis_index("c")
        chip = 2 * x + y
        sib = (x, y, 1 - c)
        first = ((x + 1 - c) % 2, (y + c) % 2)
        second = ((x + c) % 2, (y + 1 - c) % 2)
        first_chip, second_chip = 2 * first[0] + first[1], 2 * second[0] + second[1]
        diag_chip = 3 - chip
        for src, out in zip(srcs, outs):
            out[chip] = src[...].astype(BF16)

        def half(out, k, cc):
            rows = out.shape[1] // 2
            return out.at[k, pl.ds(pl.multiple_of(cc * rows, 16), rows), :]

        def ici(w, slot, out, k, dev):
            blk = half(out, k, c)
            return _remote(blk, blk, s_ici.at[nw * slot + w], r_ici.at[nw * slot + w], (dev[0], dev[1], c))

        def d2d(w, slot, out, k, cc):
            blk = half(out, k, cc)
            return _remote(blk, blk, s_d2d.at[nw * slot + w], r_d2d.at[nw * slot + w], sib)

        sent = []
        for slot, dev in enumerate((first, second)):
            for w, out in enumerate(outs):
                sent.append(ici(w, slot, out, chip, dev))
                sent[-1].start()
        for slot, k, dev in ((0, first_chip, first), (1, second_chip, second), (2, diag_chip, second)):
            for w, out in enumerate(outs):
                ici(w, slot, out, k, dev).wait_recv()
                if slot == 0:
                    sent.append(ici(w, 2, out, k, second))
                    sent[-1].start()
                sent.append(d2d(w, slot, out, k, c))
                sent[-1].start()
        for src, bf, land in zip(late_srcs, late_bf, late_land):
            bf[...] = src[...].astype(BF16)
            land[...] = jnp.zeros_like(land)
            land[chip] = bf[...]
        for slot, k in ((0, second_chip), (1, first_chip), (2, diag_chip)):
            for w, out in enumerate(outs):
                d2d(w, slot, out, k, 1 - c).wait_recv()
        for cp in sent:
            cp.wait_send()

    vmem = pl.BlockSpec(memory_space=pltpu.VMEM)
    outs = pl.pallas_call(
        body,
        out_shape=[jax.ShapeDtypeStruct((N_CHIPS,) + w.shape, BF16) for w in weights]
        + [jax.ShapeDtypeStruct(w.shape, BF16) for w in late]
        + [jax.ShapeDtypeStruct((N_CHIPS,) + w.shape, BF16) for w in late],
        in_specs=[vmem] * (nw + nl), out_specs=[vmem] * (nw + 2 * nl),
        scratch_shapes=[pltpu.SemaphoreType.DMA((3 * nw,))] * 4,
        compiler_params=pltpu.CompilerParams(vmem_limit_bytes=VMEM_LIMIT), name="ag_weights")(*weights, *late)
    return outs[:nw], outs[nw:nw + nl], outs[nw + nl:]


_HBM = pl.BlockSpec(memory_space=pltpu.HBM)
_SEM = pl.BlockSpec(memory_space=pltpu.SEMAPHORE)
_ANY = pl.BlockSpec(memory_space=pl.ANY)
_DATAFLOW = pltpu.SideEffectType.DATAFLOW_SIDE_EFFECTING


def _in_hbm(a):
    return pltpu.with_memory_space_constraint(a, pltpu.HBM)


def _exchange_copies(gather, srcs, lands, send_sems, recv_sems):
    nw = len(srcs)
    x, y, c, chip, peers, peer_chip = _place()
    pairs = []
    for m, (px, py) in enumerate(peers):
        for w in range(nw):
            sems = (send_sems.at[nw * m + w], recv_sems.at[nw * m + w], (px, py, c))
            if gather:
                pairs.append((_remote(srcs[w], lands[w].at[chip], *sems),
                              _remote(srcs[w], lands[w].at[peer_chip[m]], *sems)))
            else:
                pairs.append((_remote(srcs[w].at[m], lands[w].at[m], *sems),) * 2)
    return pairs


def _exchange_start(gather, srcs, after, name, lands=None):
    nw = len(srcs)
    n_copies = 3 * nw

    def body(*refs):
        send_sems, recv_sems = refs[2 * nw + 1], refs[2 * nw + 2]
        for start, _ in _exchange_copies(gather, refs[:nw], refs[nw:2 * nw], send_sems, recv_sems):
            start.start()
        refs[-1][...] = jnp.zeros_like(refs[-1])

    if lands is None:
        lands = [lax.empty(((N_CHIPS,) + s.shape) if gather else s.shape, s.dtype) for s in srcs]
    lands = [_in_hbm(l) for l in lands]
    return pl.pallas_call(
        body, name=name,
        out_shape=(pltpu.SemaphoreType.DMA((n_copies,)), pltpu.SemaphoreType.DMA((n_copies,)))
        + tuple(pltpu.HBM(s.shape, s.dtype) for s in srcs)
        + tuple(pltpu.HBM(l.shape, l.dtype) for l in lands)
        + (jax.ShapeDtypeStruct((8, LANES), F32),),
        in_specs=[_HBM] * (2 * nw) + [_ANY],
        out_specs=(_SEM, _SEM) + (_HBM,) * (2 * nw) + (pl.BlockSpec(memory_space=pltpu.VMEM),),
        input_output_aliases={i: 2 + i for i in range(2 * nw)},
        compiler_params=pltpu.CompilerParams(has_side_effects=_DATAFLOW),
    )(*[_in_hbm(s) for s in srcs], *lands, after)


def _exchange_wait(gather, started, after, name):
    nw = (len(started) - 3) // 2
    send_sems, recv_sems = started[0], started[1]
    thru = started[2:2 + 2 * nw]

    def body(*refs):
        for _, arrival in _exchange_copies(gather, refs[:nw], refs[nw:2 * nw], refs[2 * nw], refs[2 * nw + 1]):
            arrival.wait_send()
            arrival.wait_recv()

    outs = pl.pallas_call(
        body, name=name,
        out_shape=tuple(pltpu.HBM(t.shape, t.dtype) for t in thru),
        in_specs=[_HBM] * (2 * nw) + [_SEM, _SEM, _ANY], out_specs=(_HBM,) * (2 * nw),
        input_output_aliases={i: i for i in range(2 * nw)},
        compiler_params=pltpu.CompilerParams(has_side_effects=_DATAFLOW),
    )(*thru, send_sems, recv_sems, after)
    return outs[nw:]


def _reduce_first(stacks):
    ns = len(stacks)
    halves = [s.shape[1] // 2 for s in stacks]
    row_block = 32

    def body(*refs):
        gs, sends, owns = refs[:ns], refs[ns:2 * ns], refs[2 * ns:3 * ns]
        ras, s_sem, r_sem = refs[3 * ns:4 * ns], refs[4 * ns], refs[4 * ns + 1]
        x, y, c, chip, peers, peer_chip = _place()
        swaps = []
        for w in range(ns):
            theirs = gs[w].at[:, pl.ds(pl.multiple_of((1 - c) * halves[w], 8), halves[w]), :]
            swaps.append(_remote(theirs, ras[w], s_sem.at[w], r_sem.at[w], (x, y, 1 - c)))
            swaps[-1].start()
        for w in range(ns):
            n = halves[w]
            swaps[w].wait_recv()

            def sums(i, carry, w=w, n=n):
                r0 = pl.multiple_of(i * row_block, row_block)
                blk = pl.ds(r0, row_block)
                mine = pl.ds(pl.multiple_of(c * n + r0, 8), row_block)
                for m in range(3):
                    sends[w][m, blk, :] = (gs[w][peer_chip[m], mine, :] + ras[w][peer_chip[m], blk, :]).astype(BF16)
                owns[w][blk, :] = gs[w][chip, mine, :] + ras[w][chip, blk, :]
                return carry
            lax.fori_loop(0, n // row_block, sums, 0)
        for cp in swaps:
            cp.wait_send()

    vmem = pl.BlockSpec(memory_space=pltpu.VMEM)
    outs = pl.pallas_call(
        body,
        out_shape=[jax.ShapeDtypeStruct((3, n, s.shape[2]), BF16) for n, s in zip(halves, stacks)]
        + [jax.ShapeDtypeStruct((n, s.shape[2]), F32) for n, s in zip(halves, stacks)],
        in_specs=[vmem] * ns, out_specs=[vmem] * (2 * ns),
        scratch_shapes=[pltpu.VMEM((N_CHIPS, n, s.shape[2]), F32) for n, s in zip(halves, stacks)]
        + [pltpu.SemaphoreType.DMA((ns,)), pltpu.SemaphoreType.DMA((ns,))],
        compiler_params=pltpu.CompilerParams(vmem_limit_bytes=VMEM_LIMIT), name="reduce_first")(*stacks)
    return outs[:ns], outs[ns:]


def _reduce_last(owns, landed, g_small):
    ns = len(owns)
    row_block = 32
    hs = SMALL_ROWS // 2

    def body(*refs):
        own_refs, land_refs, gsm_ref = refs[:ns], refs[ns:2 * ns], refs[2 * ns]
        out_refs, osm_ref = refs[2 * ns + 1:3 * ns + 1], refs[3 * ns + 1]
        ra_sm, p_sm, s_sem, r_sem, sm_s, sm_r = refs[3 * ns + 2:]
        x, y, c, chip, peers, peer_chip = _place()
        sib = (x, y, 1 - c)
        half = lambda cc: pl.ds(pl.multiple_of(cc * hs, 8), hs)
        sm_a = _remote(gsm_ref.at[half(1 - c), :], ra_sm, sm_s.at[0], sm_r.at[0], sib)
        sm_a.start()
        swaps = [sm_a]
        for w in range(ns):
            n = own_refs[w].shape[0]

            def total(i, carry, w=w, n=n):
                r0 = pl.multiple_of(i * row_block, row_block)
                blk = pl.ds(r0, row_block)
                acc = own_refs[w][blk, :]
                for m in range(3):
                    acc = acc + land_refs[w][m, blk, :].astype(F32)
                out_refs[w][pl.ds(pl.multiple_of(c * n + r0, 8), row_block), :] = acc
                return carry
            lax.fori_loop(0, n // row_block, total, 0)
            mine = out_refs[w].at[pl.ds(pl.multiple_of(c * n, 8), n), :]
            swaps.append(_remote(mine, mine, s_sem.at[w], r_sem.at[w], sib))
            swaps[-1].start()
        sm_a.wait_recv()
        p_sm[chip] = gsm_ref[half(c), :] + ra_sm[...]
        for m, (px, py) in enumerate(peers):
            swaps.append(_remote(p_sm.at[chip], p_sm.at[chip], sm_s.at[1 + m], sm_r.at[1 + m], (px, py, c)))
            swaps[-1].start()
        for w in range(ns):
            n = own_refs[w].shape[0]
            theirs = out_refs[w].at[pl.ds(pl.multiple_of((1 - c) * n, 8), n), :]
            _remote(theirs, theirs, s_sem.at[w], r_sem.at[w], sib).wait_recv()
        for m, (px, py) in enumerate(peers):
            _remote(p_sm.at[chip], p_sm.at[peer_chip[m]], sm_s.at[1 + m], sm_r.at[1 + m], (px, py, c)).wait_recv()
        osm_ref[half(c), :] = (p_sm[0] + p_sm[1]) + (p_sm[2] + p_sm[3])
        swaps.append(_remote(osm_ref.at[half(c), :], osm_ref.at[half(c), :], sm_s.at[4], sm_r.at[4], sib))
        swaps[-1].start()
        _remote(osm_ref.at[half(1 - c), :], osm_ref.at[half(1 - c), :], sm_s.at[4], sm_r.at[4], sib).wait_recv()
        for cp in swaps:
            cp.wait_send()

    vmem = pl.BlockSpec(memory_space=pltpu.VMEM)
    return pl.pallas_call(
        body, out_shape=[jax.ShapeDtypeStruct((2 * o.shape[0], o.shape[1]), F32) for o in owns]
        + [jax.ShapeDtypeStruct((SMALL_ROWS, LANES), F32)],
        in_specs=[vmem] * (2 * ns + 1), out_specs=[vmem] * (ns + 1),
        scratch_shapes=[pltpu.VMEM((hs, LANES), F32), pltpu.VMEM((N_CHIPS, hs, LANES), F32),
                        pltpu.SemaphoreType.DMA((ns,)), pltpu.SemaphoreType.DMA((ns,)),
                        pltpu.SemaphoreType.DMA((5,)), pltpu.SemaphoreType.DMA((5,))],
        compiler_params=pltpu.CompilerParams(vmem_limit_bytes=VMEM_LIMIT),
        name="reduce_last")(*owns, *landed, g_small)


_SMALL_PARTS = (("g_norm", 8, 8), ("w_s", 512, 512), ("b_s", 4, 8), ("g_v", 2, 8), ("g_mem", 8, 8),
                ("g_final", 8, 8), ("loss", 1, 8))
_LOSS_ROW = SMALL_ROWS - 8
assert sum(p for _, _, p in _SMALL_PARTS) == SMALL_ROWS


def _pack_small(parts, loss=None):
    loss_row = jnp.zeros((1, LANES), F32) if loss is None else jnp.broadcast_to(loss.reshape(1, 1), (1, LANES))
    rows = []
    for (name, used, padded), p in zip(_SMALL_PARTS, list(parts) + [loss_row]):
        p = p.reshape(used, LANES)
        if padded > used:
            p = jnp.pad(p, ((0, padded - used), (0, 0)))
        rows.append(p)
    return jnp.concatenate(rows, axis=0)


def _local_step(x, mem, target, g_norm, w_in, w_s, b_s, g_v, g_mem, late_weights, g_final,
                fwd_token=None, on_dw=None):
    B, S, _ = x.shape
    x2d = x.reshape(B * S, D_MODEL)
    t2d = target.reshape(B * S, D_MODEL)
    mem2d = mem.reshape(B * N_MEM, D_MODEL)

    proj = _inproj_fwd(x2d, g_norm, w_in, after=() if fwd_token is None else (fwd_token,))
    w_kv, w_out = late_weights(proj)
    kv = _kv_fwd(mem2d, g_mem, w_kv)
    a, lse = _attn_fwd(proj, B, S)
    w_sT = jnp.swapaxes(w_s, 1, 2)
    b_tab = jnp.repeat(b_s.T, HEAD_DIM, axis=1)
    (dx2, da, drest, loss, d_wout, d_ws, d_bs, d_gv, d_gf, dkv) = _mid(
        x2d, t2d, a, proj, kv, w_s, w_sT, b_tab, g_v, w_out, g_final, B, S)
    d_wkv, d_gmem = _kv_bwd(mem2d, g_mem, w_kv, dkv)
    dq, dk, dv = _attn_bwd(proj, a, lse, da, B, S)
    d_win = _inproj_bwd_dw(dq, dk, dv, drest, x2d, g_norm)
    grad_x, d_gnorm = _inproj_bwd_dx(dq, dk, dv, drest, x2d, dx2, g_norm, w_in,
                                     after=() if on_dw is None else (on_dw(d_win, d_wkv, d_wout),))
    d_bs = d_bs[:, :N_SGU_GROUPS].T
    return (loss[0, 0], grad_x.reshape(B, S, D_MODEL),
            dict(g_norm=d_gnorm, w_in=d_win, w_s=d_ws, b_s=d_bs, g_v=d_gv, g_mem=d_gmem, w_kv=d_wkv,
                 w_out=d_wout, g_final=d_gf))


def kernel(x, mem, g_norm, w_in, w_sgu_spatial, b_sgu_spatial, g_sgu_v, g_mem, w_mem_kv, w_out, g_final, loss_target, m_g_norm, m_w_in, m_w_sgu_spatial, m_b_sgu_spatial, m_g_sgu_v, m_g_mem, m_w_mem_kv, m_w_out, m_g_final, v_g_norm, v_w_in, v_w_sgu_spatial, v_b_sgu_spatial, v_g_sgu_v, v_g_mem, v_w_mem_kv, v_w_out, v_g_final):
    t = lambda w: jnp.swapaxes(w[0], 0, 1)
    (win_all,), late_shards, late_lands = _ag_weights([t(w_in)], [w_mem_kv[0], w_out[0]])
    w_in_full = win_all.reshape(-1, win_all.shape[-1])
    late = _exchange_start(True, list(late_shards), win_all, "gather_late_start", lands=late_lands)

    def late_weights(proj):
        return [z.reshape(-1, z.shape[-1]) for z in _exchange_wait(True, late, proj, "gather_late_wait")]

    scatter = {}

    def on_dw(d_win, d_wkv, d_wout):
        stacks = [d_win.reshape((N_CHIPS, w_in.shape[2], w_in.shape[1])),
                  d_wkv.reshape((N_CHIPS,) + w_mem_kv.shape[1:]), d_wout.reshape((N_CHIPS,) + w_out.shape[1:])]
        sends, scatter["own"] = _reduce_first(stacks)
        scatter["started"] = _exchange_start(False, list(sends), scatter["own"][0], "scatter_start")
        return scatter["started"][-1]

    loss, grad_x, g = _local_step(
        x, mem, loss_target, g_norm, w_in_full, w_sgu_spatial[0], b_sgu_spatial[0], g_sgu_v, g_mem,
        late_weights, g_final.reshape(1, D_MODEL), fwd_token=late[-1], on_dw=on_dw)

    small_names = ("g_norm", "w_s", "b_s", "g_v", "g_mem", "g_final")
    g_small = _pack_small([g[n] for n in small_names], loss)
    landed = _exchange_wait(False, scatter["started"], g_small, "scatter_wait")
    gr_in, gr_kv, gr_out, gr_small = _reduce_last(scatter["own"], landed, g_small)
    loss = gr_small[_LOSS_ROW, 0]

    small_w = (g_norm, w_sgu_spatial, b_sgu_spatial, g_sgu_v, g_mem, g_final)
    small_m = (m_g_norm, m_w_sgu_spatial, m_b_sgu_spatial, m_g_sgu_v, m_g_mem, m_g_final)
    small_v = (v_g_norm, v_w_sgu_spatial, v_b_sgu_spatial, v_g_sgu_v, v_g_mem, v_g_final)
    rows = lambda ws: [w.reshape(-1, LANES) for w in ws]
    small = [[z.reshape(w.shape) for z in four]
             for w, four in zip(small_w, _adamw_small(gr_small, rows(small_w), rows(small_m), rows(small_v)))]
    d_in, nm_in, nv_in = _adamw(t(w_in), gr_in, t(m_w_in), t(v_w_in), "adamw_w_in")
    gr_in, d_in, nm_in, nv_in = [jnp.swapaxes(z, 0, 1) for z in (gr_in, d_in, nm_in, nv_in)]
    d_kv, nm_kv, nv_kv = _adamw(w_mem_kv[0], gr_kv, m_w_mem_kv[0], v_w_mem_kv[0], "adamw_w_kv")
    d_out, nm_out, nv_out = _adamw(w_out[0], gr_out, m_w_out[0], v_w_out[0], "adamw_w_out")

    def leaves(kind, big_in, big_kv, big_out):
        s_norm, s_ws, s_bs, s_gv, s_gmem, s_gf = [four[kind] for four in small]
        return [s_norm, big_in[None], s_ws, s_bs, s_gv, s_gmem, big_kv[None], big_out[None], s_gf]

    return (loss, grad_x, *leaves(0, gr_in, gr_kv, gr_out), *leaves(1, d_in, d_kv, d_out),
            *leaves(2, nm_in, nm_kv, nm_out), *leaves(3, nv_in, nv_kv, nv_out))
```

```python
import functools

import jax
import jax.numpy as jnp
from jax import lax
from jax.experimental import pallas as pl
from jax.experimental.pallas import tpu as pltpu

F32 = jnp.float32
BF16 = jnp.bfloat16
MESH = pl.DeviceIdType.MESH

D_MODEL = 1024
ATTN_WIDTH = 512
SGU_WIDTH = 256
MEM_WIDTH = 256
N_MEM = 256
IN_COLS = 3328
QKV_COLS = 3 * ATTN_WIDTH
REST_COLS = IN_COLS - QKV_COLS
SGU_CHUNK = 128
N_SGU_GROUPS = 4
EPS = 1e-6
NEG_INF = -1e30
DILATIONS = (1, 4, 16)
RADIUS = 64
Q_BLOCK = 128
LANES = 128
HEAD_DIM = 64

ADAM_LR = 0.001
ADAM_B1 = 0.9
ADAM_B2 = 0.999
ADAM_EPS = 1e-08
ADAM_WD = 0.01
ADAM_STEP = 10

N_CHIPS = 4
VMEM_LIMIT = 56 * 1024 * 1024
SMALL_ROWS = 560


def _params(sem=None, vmem=VMEM_LIMIT):
    return pltpu.CompilerParams(dimension_semantics=sem, vmem_limit_bytes=vmem)


def _nn(a, b):
    return jnp.dot(a, b, preferred_element_type=F32)


def _nt(a, b):
    return lax.dot_general(a, b, (((1,), (1,)), ((), ())), preferred_element_type=F32)


def _tn(a, b):
    return lax.dot_general(a, b, (((0,), (0,)), ((), ())), preferred_element_type=F32)


def _rms(x):
    r = lax.rsqrt(jnp.mean(x * x, axis=-1, keepdims=True) + EPS)
    return r, x * r


def _head_masks():
    lane = lax.broadcasted_iota(jnp.int32, (1, LANES), 1)
    lo = lane < HEAD_DIM
    return lo, (lo.astype(F32), (~lo).astype(F32))


def _silu_parts(z):
    s = jax.nn.sigmoid(z)
    return z * s, s * (1.0 + z * (1.0 - s))


def _gelu_parts(x):
    c = 0.7978845608028654
    x2 = x * x
    t = jnp.tanh(c * (x + 0.044715 * (x * x2)))
    val = 0.5 * x * (1.0 + t)
    grad = 0.5 * (1.0 + t) + 0.5 * x * (1.0 - t * t) * (c * (1.0 + 3.0 * 0.044715 * x2))
    return val, grad


def _after(tokens):
    return [pl.BlockSpec(memory_space=pl.ANY)] * len(tokens)


def _inproj_fwd(x2d, g_norm, w_in_t, after=()):
    T = x2d.shape[0]
    tm = 512

    def body(x_ref, g_ref, w_ref, *rest):
        o_ref = rest[-1]
        _, xh = _rms(x_ref[...])
        h = (xh * g_ref[...]).astype(BF16)
        o_ref[...] = _nt(h, w_ref[...])

    return pl.pallas_call(
        body, grid=(T // tm,),
        in_specs=[pl.BlockSpec((tm, D_MODEL), lambda i: (i, 0)),
                  pl.BlockSpec((1, D_MODEL), lambda i: (0, 0)),
                  pl.BlockSpec((IN_COLS, D_MODEL), lambda i: (0, 0))] + _after(after),
        out_specs=pl.BlockSpec((tm, IN_COLS), lambda i: (i, 0)),
        out_shape=jax.ShapeDtypeStruct((T, IN_COLS), F32),
        compiler_params=_params(("arbitrary",)), name="inproj_fwd")(x2d, g_norm, w_in_t, *after)


def _kv_fwd(mem2d, g_mem, w_kv):
    Tm = mem2d.shape[0]

    def body(m_ref, g_ref, w_ref, o_ref):
        _, mh = _rms(m_ref[...])
        o_ref[...] = _nn((mh * g_ref[...]).astype(BF16), w_ref[...])

    return pl.pallas_call(
        body, out_shape=jax.ShapeDtypeStruct((Tm, 2 * MEM_WIDTH), F32),
        compiler_params=_params(), name="kv_fwd")(mem2d, g_mem, w_kv)


def _kv_bwd(mem2d, g_mem, w_kv, dkv):
    Tm = mem2d.shape[0]

    def body(m_ref, g_ref, w_ref, dkv_ref, dw_ref, dg_ref):
        _, mh = _rms(m_ref[...])
        memn = (mh * g_ref[...]).astype(BF16)
        dkvb = dkv_ref[...].astype(BF16)
        dw_ref[...] = _tn(memn, dkvb)
        dmemn = _nt(dkvb, w_ref[...])
        dg_ref[...] = jnp.sum(dmemn * mh, axis=0, keepdims=True)

    return pl.pallas_call(
        body, out_shape=(jax.ShapeDtypeStruct((D_MODEL, 2 * MEM_WIDTH), F32),
                         jax.ShapeDtypeStruct((1, D_MODEL), F32)),
        compiler_params=_params(), name="kv_bwd")(mem2d, g_mem, w_kv, dkv)


def _attn_geometry(S):
    geom = []
    for d in DILATIONS:
        L = S // d
        assert L % Q_BLOCK == 0
        geom.append((d, L, min(2 * Q_BLOCK, L), L // Q_BLOCK))
    return geom


def _init_bias(bias_scr, geom, hp):
    row = lax.broadcasted_iota(jnp.int32, (Q_BLOCK, 2 * Q_BLOCK), 0)
    col = lax.broadcasted_iota(jnp.int32, (Q_BLOCK, 2 * Q_BLOCK), 1)
    for j in (0, 1):
        bits = (126 - (2 * hp + j)) * (1 << 23)
        slope = lax.bitcast_convert_type(jnp.full((1, 1), bits, jnp.int32), F32)
        for di, (d, _, _, _) in enumerate(geom):
            for cls, off in enumerate((0, -RADIUS, -2 * RADIUS)):
                dist = jnp.abs(col - row + off)
                bias_scr[di * 6 + cls * 2 + j] = jnp.where(
                    dist <= RADIUS, -(slope * float(d)) * dist.astype(F32), NEG_INF)


def _block_slices(d, L, KW, nqb, r, qb):
    qs = qb * Q_BLOCK
    ks = jnp.clip(qs - RADIUS, 0, L - KW)
    cls = jnp.where(qb == 0, 0, jnp.where(qb == nqb - 1, 2, 1))
    if d == 1:
        qsl = pl.ds(pl.multiple_of(qs, Q_BLOCK), Q_BLOCK)
        ksl = pl.ds(pl.multiple_of(ks, RADIUS), KW)
    else:
        qsl = pl.ds(r + qs * d, Q_BLOCK, stride=d)
        ksl = pl.ds(r + ks * d, KW, stride=d)
    return qsl, ksl, cls


def _for_groups(geom, group, fn):
    for di, (d, L, KW, nqb) in enumerate(geom):
        assert (d * nqb) % group == 0

        def step(it, carry, di=di, d=d, L=L, KW=KW, nqb=nqb):
            slices = []
            for g in range(group):
                i = it * group + g
                slices.append(_block_slices(d, L, KW, nqb, i // nqb, i % nqb))
            fn(di, KW, slices)
            return carry
        lax.fori_loop(0, d * nqb // group, step, 0)


def _attn_fwd(proj, B, S):
    T = B * S
    geom = _attn_geometry(S)
    n_pairs = ATTN_WIDTH // LANES

    def body(q_ref, k_ref, v_ref, a_ref, lse_ref, bias_scr, *per_dilation):
        o_scr, m_scr, l_scr = per_dilation[0:3], per_dilation[3:6], per_dilation[6:9]
        lo, hm = _head_masks()
        _init_bias(bias_scr, geom, pl.program_id(1))

        def group(di, KW, slices):
            chains = [(g, j) for g in range(len(slices)) for j in (0, 1)]
            q = [q_ref[qsl, :] for qsl, _, _ in slices]
            kw = [k_ref[ksl, :].astype(BF16) for _, ksl, _ in slices]
            vw = [v_ref[ksl, :].astype(BF16) for _, ksl, _ in slices]
            s = {(g, j): _nt((q[g] * (hm[j] * 0.125)).astype(BF16), kw[g])
                 + bias_scr[di * 6 + slices[g][2] * 2 + j, :, pl.ds(0, KW)] for g, j in chains}
            m = {c: jnp.max(s[c], axis=1, keepdims=True) for c in chains}
            p = {c: jnp.exp(s[c] - m[c]) for c in chains}
            l = {c: jnp.sum(p[c], axis=1, keepdims=True) for c in chains}
            o = {(g, j): _nn(p[(g, j)].astype(BF16), vw[g]) for g, j in chains}
            for g, (qsl, _, _) in enumerate(slices):
                o_scr[di][qsl, :] = jnp.where(lo, o[(g, 0)], o[(g, 1)])
                m_scr[di][qsl, :] = jnp.where(lo, m[(g, 0)], m[(g, 1)])
                l_scr[di][qsl, :] = jnp.where(lo, l[(g, 0)], l[(g, 1)])

        _for_groups(geom, 8, group)

        rows_per = 256

        def combine(i, carry):
            rows = pl.ds(pl.multiple_of(i * rows_per, rows_per), rows_per)
            ms = [m_scr[di][rows, :] for di in range(3)]
            mx = jnp.maximum(jnp.maximum(ms[0], ms[1]), ms[2])
            num = 0.0
            den = 0.0
            for di in range(3):
                w = jnp.exp(ms[di] - mx)
                num = num + w * o_scr[di][rows, :]
                den = den + w * l_scr[di][rows, :]
            a_ref[rows, :] = num / den
            lse_ref[rows, :] = mx + jnp.log(den)
            return carry

        lax.fori_loop(0, S // rows_per, combine, 0)

    blk = lambda off: pl.BlockSpec((S, LANES), lambda b, h, off=off: (b, off + h))
    out_blk = pl.BlockSpec((S, LANES), lambda b, h: (b, h))
    return pl.pallas_call(
        body, grid=(B, n_pairs),
        in_specs=[blk(0), blk(n_pairs), blk(2 * n_pairs)],
        out_specs=[out_blk, out_blk],
        out_shape=[jax.ShapeDtypeStruct((T, ATTN_WIDTH), F32)] * 2,
        scratch_shapes=[pltpu.VMEM((18, Q_BLOCK, 2 * Q_BLOCK), F32)] + [pltpu.VMEM((S, LANES), F32)] * 9,
        compiler_params=_params(("arbitrary", "arbitrary")), name="attn_fwd")(proj, proj, proj)


def _attn_bwd(proj, a, lse, da, B, S):
    T = B * S
    geom = _attn_geometry(S)
    n_pairs = ATTN_WIDTH // LANES

    def body(q_ref, k_ref, v_ref, a_ref, lse_ref, do_ref, dq_ref, dk_ref, dv_ref,
             bias_scr, dq_scr, dk_scr, dv_scr):
        _, hm = _head_masks()
        _init_bias(bias_scr, geom, pl.program_id(1))
        dq_scr[...] = jnp.zeros_like(dq_scr)
        dk_scr[...] = jnp.zeros_like(dk_scr)
        dv_scr[...] = jnp.zeros_like(dv_scr)

        def group(di, KW, slices):
            n = len(slices)
            chains = [(g, j) for g in range(n) for j in (0, 1)]
            q = [q_ref[qsl, :] for qsl, _, _ in slices]
            do = [do_ref[qsl, :] for qsl, _, _ in slices]
            doa = [do[g] * a_ref[slices[g][0], :] for g in range(n)]
            lse_q = [lse_ref[qsl, :] for qsl, _, _ in slices]
            kw = [k_ref[ksl, :].astype(BF16) for _, ksl, _ in slices]
            vw = [v_ref[ksl, :].astype(BF16) for _, ksl, _ in slices]
            qj = {(g, j): (q[g] * (hm[j] * 0.125)).astype(BF16) for g, j in chains}
            doj = {(g, j): (do[g] * hm[j]).astype(BF16) for g, j in chains}
            s = {(g, j): _nt(qj[(g, j)], kw[g])
                 + bias_scr[di * 6 + slices[g][2] * 2 + j, :, pl.ds(0, KW)] for g, j in chains}
            dp = {(g, j): _nt(doj[(g, j)], vw[g]) for g, j in chains}
            dsum = {(g, j): jnp.sum(doa[g] * hm[j], axis=1, keepdims=True) for g, j in chains}
            p = {(g, j): jnp.exp(s[(g, j)] - lse_q[g][:, HEAD_DIM * j:HEAD_DIM * j + 1]) for g, j in chains}
            ds = {c: (p[c] * (dp[c] - dsum[c])).astype(BF16) for c in chains}
            pb = {c: p[c].astype(BF16) for c in chains}
            dq = [_nn(ds[(g, 0)], kw[g]) * (hm[0] * 0.125) + _nn(ds[(g, 1)], kw[g]) * (hm[1] * 0.125)
                  for g in range(n)]
            both = lambda t, g: jnp.concatenate([t[(g, 0)], t[(g, 1)]], axis=0)
            dkw = [_tn(both(ds, g), both(qj, g)) for g in range(n)]
            dvw = [_tn(both(pb, g), both(doj, g)) for g in range(n)]
            for g, (qsl, ksl, _) in enumerate(slices):
                dq_scr[qsl, :] = dq_scr[qsl, :] + dq[g]
                dk_scr[ksl, :] = dk_scr[ksl, :] + dkw[g]
                dv_scr[ksl, :] = dv_scr[ksl, :] + dvw[g]

        _for_groups(geom, 4, group)
        dq_ref[...] = dq_scr[...].astype(BF16)
        dk_ref[...] = dk_scr[...].astype(BF16)
        dv_ref[...] = dv_scr[...].astype(BF16)

    blk = lambda off: pl.BlockSpec((S, LANES), lambda b, h, off=off: (b, off + h))
    return pl.pallas_call(
        body, grid=(B, n_pairs),
        in_specs=[blk(0), blk(n_pairs), blk(2 * n_pairs), blk(0), blk(0), blk(0)],
        out_specs=[blk(0), blk(0), blk(0)],
        out_shape=[jax.ShapeDtypeStruct((T, ATTN_WIDTH), BF16)] * 3,
        scratch_shapes=[pltpu.VMEM((18, Q_BLOCK, 2 * Q_BLOCK), F32),
                        pltpu.VMEM((S, LANES), F32),
                        pltpu.VMEM((S, LANES), F32),
                        pltpu.VMEM((S, LANES), F32)],
        compiler_params=_params(("arbitrary", "arbitrary")), name="attn_bwd")(proj, proj, proj, a, lse, da)


def _mid(x2d, t2d, a, proj, kv, w_s, w_sT, b_tab, g_v, w_out, g_final, B, S):
    T = B * S
    tm = 512
    nt = S // tm
    halves = 2
    hrows = tm // halves

    def body(x_ref, t_ref, a_ref, za_ref, ub_ref, vb_ref, zb_ref, qm_ref, zm_ref, kv_ref,
              ws_ref, wsT_ref, btab_ref, gv_ref, wout_ref, gf_ref,
              dx2_ref, da_ref, drest_ref, loss_ref, dwout_ref, dws_ref, dbs_ref, dgv_ref, dgf_ref, dkv_ref,
              dbtab_scr):
        b = pl.program_id(0)
        t = pl.program_id(1)
        first = jnp.logical_and(b == 0, t == 0)
        last = jnp.logical_and(b == B - 1, t == nt - 1)
        _, hm = _head_masks()
        lane_g = lax.broadcasted_iota(jnp.int32, (1, SGU_WIDTH), 1) // HEAD_DIM
        gm = [(lane_g == g).astype(F32) for g in range(N_SGU_GROUPS)]
        H = range(halves)
        rows = [pl.ds(h * hrows, hrows) for h in H]
        ld = lambda ref: [ref[r, :] for r in rows]
        cat = lambda parts, axis: jnp.concatenate(parts, axis=axis)
        chunks = [slice(ci * SGU_CHUNK, (ci + 1) * SGU_CHUNK) for ci in range(hrows // SGU_CHUNK)]
        pairs = [slice(pr * LANES, (pr + 1) * LANES) for pr in range(2)]
        heads = [(pr, j) for pr in range(2) for j in (0, 1)]

        @pl.when(first)
        def _():
            loss_ref[...] = jnp.zeros_like(loss_ref)
            dwout_ref[...] = jnp.zeros_like(dwout_ref)
            dws_ref[...] = jnp.zeros_like(dws_ref)
            dbs_ref[...] = jnp.zeros_like(dbs_ref)
            dgv_ref[...] = jnp.zeros_like(dgv_ref)
            dgf_ref[...] = jnp.zeros_like(dgf_ref)
            dbtab_scr[...] = jnp.zeros_like(dbtab_scr)

        @pl.when(t == 0)
        def _():
            dkv_ref[...] = jnp.zeros_like(dkv_ref)

        a_val = ld(a_ref)
        sil_a = [_silu_parts(z) for z in ld(za_ref)]
        gated_a = [s[0] * a for s, a in zip(sil_a, a_val)]
        u = [_gelu_parts(z) for z in ld(ub_ref)]
        vv = [_gelu_parts(z) for z in ld(vb_ref)]
        vnorm = [_rms(v[0]) for v in vv]
        gv = gv_ref[...]
        vn = [(n[1] * gv).astype(BF16) for n in vnorm]
        w_cat = cat([ws_ref[g].astype(BF16) for g in range(N_SGU_GROUPS)], 1)
        wT_cat = cat([wsT_ref[g].astype(BF16) for g in range(N_SGU_GROUPS)], 1)
        gmb = [m.astype(BF16) for m in gm]
        by_group = lambda chunk: cat([chunk * gmb[g] for g in range(N_SGU_GROUPS)], 0)
        btab = btab_ref[...]
        mixed = [cat([btab + _nn(w_cat, by_group(vn[h][c, :])) for c in chunks], 0) for h in H]
        sg = [u[h][0] * mixed[h] for h in H]
        sil_b = [_silu_parts(z) for z in ld(zb_ref)]
        gated_b = [sil_b[h][0] * sg[h] for h in H]

        kvv = kv_ref[...].astype(BF16)
        kp = [kvv[:, p] for p in pairs]
        vp = [kvv[:, MEM_WIDTH + pr * LANES:MEM_WIDTH + (pr + 1) * LANES] for pr in range(2)]
        qm = ld(qm_ref)
        qj = {(h, pr, j): (qm[h][:, pairs[pr]] * (hm[j] * 0.125)).astype(BF16) for h in H for pr, j in heads}
        sc = {k: _nt(qj[k], kp[k[1]]) for k in qj}
        ex = {k: jnp.exp(sc[k] - jnp.max(sc[k], axis=1, keepdims=True)) for k in qj}
        prob = {k: ex[k] * (1.0 / jnp.sum(ex[k], axis=1, keepdims=True)) for k in qj}
        probb = {k: prob[k].astype(BF16) for k in qj}
        mo = [cat([sum(_nn(probb[(h, pr, j)], vp[pr]) * hm[j] for j in (0, 1)) for pr in range(2)], 1) for h in H]
        sil_m = [_silu_parts(z) for z in ld(zm_ref)]
        gated_m = [sil_m[h][0] * mo[h] for h in H]

        gated = [cat([gated_a[h], gated_b[h], gated_m[h]], 1).astype(BF16) for h in H]
        wout = wout_ref[...]
        x_in = ld(x_ref)
        x2 = [x_in[h] + _nn(gated[h], wout) for h in H]
        fin = [_rms(z) for z in x2]
        gf = gf_ref[...]
        tgt = ld(t_ref)
        err = [fin[h][1] * gf - tgt[h] for h in H]
        loss_ref[...] += sum(jnp.sum(e * e) for e in err) * (0.5 / D_MODEL)

        dy = [e * (1.0 / D_MODEL) for e in err]
        dgf_ref[...] += sum(jnp.sum(dy[h] * fin[h][1], axis=0, keepdims=True) for h in H)
        gdy = [d * gf for d in dy]
        dx2 = [fin[h][0] * (gdy[h] - fin[h][1] * jnp.mean(gdy[h] * fin[h][1], axis=1, keepdims=True)) for h in H]
        for h in H:
            dx2_ref[rows[h], :] = dx2[h]
        dx2b = [d.astype(BF16) for d in dx2]
        dgated = [_nt(d, wout) for d in dx2b]
        dwout_ref[...] += _tn(cat(gated, 0), cat(dx2b, 0))
        dga = [d[:, 0:ATTN_WIDTH] for d in dgated]
        dgb = [d[:, ATTN_WIDTH:ATTN_WIDTH + SGU_WIDTH] for d in dgated]
        dgm = [d[:, ATTN_WIDTH + SGU_WIDTH:] for d in dgated]

        for h in H:
            da_ref[rows[h], :] = dga[h] * sil_a[h][0]
        dza = [dga[h] * a_val[h] * sil_a[h][1] for h in H]

        dsg = [dgb[h] * sil_b[h][0] for h in H]
        dzb = [dgb[h] * sg[h] * sil_b[h][1] for h in H]
        dub = [dsg[h] * mixed[h] * u[h][1] for h in H]
        dmixed = [dsg[h] * u[h][0] for h in H]
        dmixed_b = [d.astype(BF16) for d in dmixed]
        dvn = [cat([_nn(wT_cat, by_group(dmixed_b[h][c, :])) for c in chunks], 0) for h in H]
        for g in range(N_SGU_GROUPS):
            dws_ref[g] += sum(_nt((dmixed[h][c, :] * gm[g]).astype(BF16), vn[h][c, :]) for h in H for c in chunks)
        dbtab_scr[...] += sum(dmixed[h][c, :] for h in H for c in chunks)
        dgv_ref[...] += sum(jnp.sum(dvn[h] * vnorm[h][1], axis=0, keepdims=True) for h in H)
        tv = [d * gv for d in dvn]
        dvv = [vnorm[h][0] * (tv[h] - vnorm[h][1] * jnp.mean(tv[h] * vnorm[h][1], axis=1, keepdims=True)) for h in H]
        dvb = [dvv[h] * vv[h][1] for h in H]

        dmo = [dgm[h] * sil_m[h][0] for h in H]
        dzm = [dgm[h] * mo[h] * sil_m[h][1] for h in H]
        dmoj = {(h, pr, j): (dmo[h][:, pairs[pr]] * hm[j]).astype(BF16) for h in H for pr, j in heads}
        dp = {k: _nt(dmoj[k], vp[k[1]]) for k in qj}
        ds = {k: (prob[k] * (dp[k] - jnp.sum(dp[k] * prob[k], axis=1, keepdims=True))).astype(BF16) for k in qj}
        dqm = [cat([sum(_nn(ds[(h, pr, j)], kp[pr]) * (hm[j] * 0.125) for j in (0, 1)) for pr in range(2)], 1)
               for h in H]
        every = lambda tbl, pr: cat([tbl[(h, pr, j)] for h in H for j in (0, 1)], 0)
        dk = [_tn(every(ds, pr), every(qj, pr)) for pr in range(2)]
        dv = [_tn(every(probb, pr), every(dmoj, pr)) for pr in range(2)]
        dkv_ref[...] += cat(dk + dv, 1)

        for h in H:
            drest_ref[rows[h], :] = cat([dza[h], dub[h], dvb[h], dzb[h], dqm[h], dzm[h]], 1).astype(BF16)

        @pl.when(last)
        def _():
            lane = lax.broadcasted_iota(jnp.int32, (1, LANES), 1)
            dbt = dbtab_scr[...]
            out = jnp.zeros((SGU_CHUNK, LANES), F32)
            for g in range(N_SGU_GROUPS):
                out = out + jnp.where(lane == g, jnp.sum(dbt * gm[g], axis=1, keepdims=True), 0.0)
            dbs_ref[...] = out

    tile = lambda w, cb: pl.BlockSpec((tm, w), lambda b, t, cb=cb: (b * nt + t, cb))
    const = lambda shape: pl.BlockSpec(shape, lambda b, t, n=len(shape): (0,) * n)
    return pl.pallas_call(
        body, grid=(B, nt),
        in_specs=[tile(D_MODEL, 0), tile(D_MODEL, 0), tile(ATTN_WIDTH, 0),
                  tile(ATTN_WIDTH, 3),
                  tile(SGU_WIDTH, 8), tile(SGU_WIDTH, 9), tile(SGU_WIDTH, 10),
                  tile(MEM_WIDTH, 11), tile(MEM_WIDTH, 12),
                  pl.BlockSpec((N_MEM, 2 * MEM_WIDTH), lambda b, t: (b, 0)),
                  const((N_SGU_GROUPS, SGU_CHUNK, SGU_CHUNK)), const((N_SGU_GROUPS, SGU_CHUNK, SGU_CHUNK)),
                  const((SGU_CHUNK, SGU_WIDTH)), const((1, SGU_WIDTH)),
                  const((D_MODEL, D_MODEL)), const((1, D_MODEL))],
        out_specs=[tile(D_MODEL, 0), tile(ATTN_WIDTH, 0), tile(REST_COLS, 0),
                   const((8, LANES)), const((D_MODEL, D_MODEL)),
                   const((N_SGU_GROUPS, SGU_CHUNK, SGU_CHUNK)), const((SGU_CHUNK, LANES)),
                   const((1, SGU_WIDTH)), const((1, D_MODEL)),
                   pl.BlockSpec((N_MEM, 2 * MEM_WIDTH), lambda b, t: (b, 0))],
        out_shape=[jax.ShapeDtypeStruct((T, D_MODEL), F32), jax.ShapeDtypeStruct((T, ATTN_WIDTH), F32),
                   jax.ShapeDtypeStruct((T, REST_COLS), BF16),
                   jax.ShapeDtypeStruct((8, LANES), F32), jax.ShapeDtypeStruct((D_MODEL, D_MODEL), F32),
                   jax.ShapeDtypeStruct((N_SGU_GROUPS, SGU_CHUNK, SGU_CHUNK), F32),
                   jax.ShapeDtypeStruct((SGU_CHUNK, LANES), F32),
                   jax.ShapeDtypeStruct((1, SGU_WIDTH), F32), jax.ShapeDtypeStruct((1, D_MODEL), F32),
                   jax.ShapeDtypeStruct((B * N_MEM, 2 * MEM_WIDTH), F32)],
        scratch_shapes=[pltpu.VMEM((SGU_CHUNK, SGU_WIDTH), F32)],
        compiler_params=_params(("arbitrary", "arbitrary")), name="mid")(
            x2d, t2d, a, proj, proj, proj, proj, proj, proj, kv, w_s, w_sT, b_tab, g_v, w_out, g_final)


def _inproj_bwd_dx(dq, dk, dv, drest, x2d, dx2, g_norm, w_in_t, after=()):
    T = x2d.shape[0]
    tm = 512
    W = ATTN_WIDTH

    def body(dq_ref, dk_ref, dv_ref, dr_ref, x_ref, dx2_ref, g_ref, w_ref, *rest):
        gx_ref, dg_ref = rest[-2:]

        @pl.when(pl.program_id(0) == 0)
        def _():
            dg_ref[...] = jnp.zeros_like(dg_ref)

        dh = (_nn(dq_ref[...], w_ref[0:W, :]) + _nn(dk_ref[...], w_ref[W:2 * W, :])
              + _nn(dv_ref[...], w_ref[2 * W:3 * W, :]) + _nn(dr_ref[...], w_ref[QKV_COLS:IN_COLS, :]))
        r, xh = _rms(x_ref[...])
        dg_ref[...] += jnp.sum(dh * xh, axis=0, keepdims=True)
        th = dh * g_ref[...]
        gx_ref[...] = r * (th - xh * jnp.mean(th * xh, axis=1, keepdims=True)) + dx2_ref[...]

    tile = lambda w: pl.BlockSpec((tm, w), lambda i: (i, 0))
    return pl.pallas_call(
        body, grid=(T // tm,),
        in_specs=[tile(W), tile(W), tile(W), tile(REST_COLS), tile(D_MODEL), tile(D_MODEL),
                  pl.BlockSpec((1, D_MODEL), lambda i: (0, 0)),
                  pl.BlockSpec((IN_COLS, D_MODEL), lambda i: (0, 0))] + _after(after),
        out_specs=[tile(D_MODEL), pl.BlockSpec((1, D_MODEL), lambda i: (0, 0))],
        out_shape=[jax.ShapeDtypeStruct((T, D_MODEL), F32), jax.ShapeDtypeStruct((1, D_MODEL), F32)],
        compiler_params=_params(("arbitrary",)), name="inproj_bwd_dx")(
            dq, dk, dv, drest, x2d, dx2, g_norm, w_in_t, *after)


def _inproj_bwd_dw(dq, dk, dv, drest, x2d, g_norm):
    T = x2d.shape[0]
    tm = 512
    W = ATTN_WIDTH

    def body(dq_ref, dk_ref, dv_ref, dr_ref, x_ref, g_ref, dw_ref):
        @pl.when(pl.program_id(0) == 0)
        def _():
            dw_ref[...] = jnp.zeros_like(dw_ref)

        _, xh = _rms(x_ref[...])
        h = (xh * g_ref[...]).astype(BF16)
        dw_ref[0:W, :] += _tn(dq_ref[...], h)
        dw_ref[W:2 * W, :] += _tn(dk_ref[...], h)
        dw_ref[2 * W:3 * W, :] += _tn(dv_ref[...], h)
        dw_ref[QKV_COLS:IN_COLS, :] += _tn(dr_ref[...], h)

    tile = lambda w: pl.BlockSpec((tm, w), lambda i: (i, 0))
    return pl.pallas_call(
        body, grid=(T // tm,),
        in_specs=[tile(W), tile(W), tile(W), tile(REST_COLS), tile(D_MODEL),
                  pl.BlockSpec((1, D_MODEL), lambda i: (0, 0))],
        out_specs=pl.BlockSpec((IN_COLS, D_MODEL), lambda i: (0, 0)),
        out_shape=jax.ShapeDtypeStruct((IN_COLS, D_MODEL), F32),
        compiler_params=_params(("arbitrary",)), name="inproj_bwd_dw")(dq, dk, dv, drest, x2d, g_norm)


def _adamw_update(w, g, m, v):
    nm = ADAM_B1 * m + (1.0 - ADAM_B1) * g
    nv = ADAM_B2 * v + (1.0 - ADAM_B2) * (g * g)
    m_hat = nm / (1.0 - ADAM_B1 ** ADAM_STEP)
    v_hat = nv / (1.0 - ADAM_B2 ** ADAM_STEP)
    return -ADAM_LR * (m_hat / (jnp.sqrt(v_hat) + ADAM_EPS) + ADAM_WD * w), nm, nv


def _adamw(w, g, m, v, name):
    R, C = w.shape
    br = max(r for r in range(8, 257, 8) if R % r == 0)

    def body(w_ref, g_ref, m_ref, v_ref, d_ref, nm_ref, nv_ref):
        d_ref[...], nm_ref[...], nv_ref[...] = _adamw_update(w_ref[...], g_ref[...], m_ref[...], v_ref[...])

    spec = pl.BlockSpec((br, C), lambda i: (i, 0))
    return pl.pallas_call(
        body, grid=(R // br,), in_specs=[spec] * 4, out_specs=[spec] * 3,
        out_shape=[jax.ShapeDtypeStruct((R, C), F32)] * 3,
        compiler_params=_params(("arbitrary",)), name=name)(w, g, m, v)


def _adamw_small(g_packed, ws, ms, vs):
    n = len(ws)

    def body(*refs):
        g_ref = refs[0]
        w_refs, m_refs, v_refs = refs[1:1 + n], refs[1 + n:1 + 2 * n], refs[1 + 2 * n:1 + 3 * n]
        outs = refs[1 + 3 * n:]
        off = 0
        for i, (_, used, padded) in enumerate(_SMALL_PARTS[:n]):
            g = g_ref[off:off + used, :]
            delta, nm, nv = _adamw_update(w_refs[i][...], g, m_refs[i][...], v_refs[i][...])
            outs[4 * i][...], outs[4 * i + 1][...], outs[4 * i + 2][...], outs[4 * i + 3][...] = g, delta, nm, nv
            off += padded

    outs = pl.pallas_call(
        body, out_shape=[jax.ShapeDtypeStruct(w.shape, F32) for w in ws for _ in range(4)],
        compiler_params=_params(), name="adamw_small")(g_packed, *ws, *ms, *vs)
    return [outs[4 * i:4 * i + 4] for i in range(n)]


def _place():
    x, y, c = lax.axis_index("x"), lax.axis_index("y"), lax.axis_index("c")
    chip = 2 * x + y
    peers = [(x, 1 - y), (1 - x, y), (1 - x, 1 - y)]
    peer_chip = [2 * px + py for px, py in peers]
    return x, y, c, chip, peers, peer_chip


def _remote(src, dst, send_sem, recv_sem, dev):
    return pltpu.make_async_remote_copy(src_ref=src, dst_ref=dst, send_sem=send_sem, recv_sem=recv_sem,
                                        device_id=dev, device_id_type=MESH)


def _ag_weights(weights, late=()):
    nw, nl = len(weights), len(late)

    def body(*refs):
        srcs, late_srcs = refs[:nw], refs[nw:nw + nl]
        outs, late_bf, late_land = (refs[nw + nl:2 * nw + nl], refs[2 * nw + nl:2 * nw + 2 * nl],
                                    refs[2 * nw + 2 * nl:2 * nw + 3 * nl])
        s_ici, r_ici, s_d2d, r_d2d = refs[2 * nw + 3 * nl:]
        x, y, c = lax.axis_index("x"), lax.axis_index("y"), lax.axis_index("c")
        chip = 2 * x + y
        sib = (x, y, 1 - c)
        first = ((x + 1 - c) % 2, (y + c) % 2)
        second = ((x + c) % 2, (y + 1 - c) % 2)
        first_chip, second_chip = 2 * first[0] + first[1], 2 * second[0] + second[1]
        diag_chip = 3 - chip
        for src, out in zip(srcs, outs):
            out[chip] = src[...].astype(BF16)

        def half(out, k, cc):
            rows = out.shape[1] // 2
            return out.at[k, pl.ds(pl.multiple_of(cc * rows, 16), rows), :]

        def ici(w, slot, out, k, dev):
            blk = half(out, k, c)
            return _remote(blk, blk, s_ici.at[nw * slot + w], r_ici.at[nw * slot + w], (dev[0], dev[1], c))

        def d2d(w, slot, out, k, cc):
            blk = half(out, k, cc)
            return _remote(blk, blk, s_d2d.at[nw * slot + w], r_d2d.at[nw * slot + w], sib)

        sent = []
        for slot, dev in enumerate((first, second)):
            for w, out in enumerate(outs):
                sent.append(ici(w, slot, out, chip, dev))
                sent[-1].start()
        for slot, k, dev in ((0, first_chip, first), (1, second_chip, second), (2, diag_chip, second)):
            for w, out in enumerate(outs):
                ici(w, slot, out, k, dev).wait_recv()
                if slot == 0:
                    sent.append(ici(w, 2, out, k, second))
                    sent[-1].start()
                sent.append(d2d(w, slot, out, k, c))
                sent[-1].start()
        for src, bf, land in zip(late_srcs, late_bf, late_land):
            bf[...] = src[...].astype(BF16)
            land[...] = jnp.zeros_like(land)
            land[chip] = bf[...]
        for slot, k in ((0, second_chip), (1, first_chip), (2, diag_chip)):
            for w, out in enumerate(outs):
                d2d(w, slot, out, k, 1 - c).wait_recv()
        for cp in sent:
            cp.wait_send()

    vmem = pl.BlockSpec(memory_space=pltpu.VMEM)
    outs = pl.pallas_call(
        body,
        out_shape=[jax.ShapeDtypeStruct((N_CHIPS,) + w.shape, BF16) for w in weights]
        + [jax.ShapeDtypeStruct(w.shape, BF16) for w in late]
        + [jax.ShapeDtypeStruct((N_CHIPS,) + w.shape, BF16) for w in late],
        in_specs=[vmem] * (nw + nl), out_specs=[vmem] * (nw + 2 * nl),
        scratch_shapes=[pltpu.SemaphoreType.DMA((3 * nw,))] * 4,
        compiler_params=pltpu.CompilerParams(vmem_limit_bytes=VMEM_LIMIT), name="ag_weights")(*weights, *late)
    return outs[:nw], outs[nw:nw + nl], outs[nw + nl:]


_HBM = pl.BlockSpec(memory_space=pltpu.HBM)
_SEM = pl.BlockSpec(memory_space=pltpu.SEMAPHORE)
_ANY = pl.BlockSpec(memory_space=pl.ANY)
_DATAFLOW = pltpu.SideEffectType.DATAFLOW_SIDE_EFFECTING


def _in_hbm(a):
    return pltpu.with_memory_space_constraint(a, pltpu.HBM)


def _exchange_copies(gather, srcs, lands, send_sems, recv_sems):
    nw = len(srcs)
    x, y, c, chip, peers, peer_chip = _place()
    pairs = []
    for m, (px, py) in enumerate(peers):
        for w in range(nw):
            sems = (send_sems.at[nw * m + w], recv_sems.at[nw * m + w], (px, py, c))
            if gather:
                pairs.append((_remote(srcs[w], lands[w].at[chip], *sems),
                              _remote(srcs[w], lands[w].at[peer_chip[m]], *sems)))
            else:
                pairs.append((_remote(srcs[w].at[m], lands[w].at[m], *sems),) * 2)
    return pairs


def _exchange_start(gather, srcs, after, name, lands=None):
    nw = len(srcs)
    n_copies = 3 * nw

    def body(*refs):
        send_sems, recv_sems = refs[2 * nw + 1], refs[2 * nw + 2]
        for start, _ in _exchange_copies(gather, refs[:nw], refs[nw:2 * nw], send_sems, recv_sems):
            start.start()
        refs[-1][...] = jnp.zeros_like(refs[-1])

    if lands is None:
        lands = [lax.empty(((N_CHIPS,) + s.shape) if gather else s.shape, s.dtype) for s in srcs]
    lands = [_in_hbm(l) for l in lands]
    return pl.pallas_call(
        body, name=name,
        out_shape=(pltpu.SemaphoreType.DMA((n_copies,)), pltpu.SemaphoreType.DMA((n_copies,)))
        + tuple(pltpu.HBM(s.shape, s.dtype) for s in srcs)
        + tuple(pltpu.HBM(l.shape, l.dtype) for l in lands)
        + (jax.ShapeDtypeStruct((8, LANES), F32),),
        in_specs=[_HBM] * (2 * nw) + [_ANY],
        out_specs=(_SEM, _SEM) + (_HBM,) * (2 * nw) + (pl.BlockSpec(memory_space=pltpu.VMEM),),
        input_output_aliases={i: 2 + i for i in range(2 * nw)},
        compiler_params=pltpu.CompilerParams(has_side_effects=_DATAFLOW),
    )(*[_in_hbm(s) for s in srcs], *lands, after)


def _exchange_wait(gather, started, after, name):
    nw = (len(started) - 3) // 2
    send_sems, recv_sems = started[0], started[1]
    thru = started[2:2 + 2 * nw]

    def body(*refs):
        for _, arrival in _exchange_copies(gather, refs[:nw], refs[nw:2 * nw], refs[2 * nw], refs[2 * nw + 1]):
            arrival.wait_send()
            arrival.wait_recv()

    outs = pl.pallas_call(
        body, name=name,
        out_shape=tuple(pltpu.HBM(t.shape, t.dtype) for t in thru),
        in_specs=[_HBM] * (2 * nw) + [_SEM, _SEM, _ANY], out_specs=(_HBM,) * (2 * nw),
        input_output_aliases={i: i for i in range(2 * nw)},
        compiler_params=pltpu.CompilerParams(has_side_effects=_DATAFLOW),
    )(*thru, send_sems, recv_sems, after)
    return outs[nw:]


def _reduce_first(stacks):
    ns = len(stacks)
    halves = [s.shape[1] // 2 for s in stacks]
    row_block = 32

    def body(*refs):
        gs, sends, owns = refs[:ns], refs[ns:2 * ns], refs[2 * ns:3 * ns]
        ras, s_sem, r_sem = refs[3 * ns:4 * ns], refs[4 * ns], refs[4 * ns + 1]
        x, y, c, chip, peers, peer_chip = _place()
        swaps = []
        for w in range(ns):
            theirs = gs[w].at[:, pl.ds(pl.multiple_of((1 - c) * halves[w], 8), halves[w]), :]
            swaps.append(_remote(theirs, ras[w], s_sem.at[w], r_sem.at[w], (x, y, 1 - c)))
            swaps[-1].start()
        for w in range(ns):
            n = halves[w]
            swaps[w].wait_recv()

            def sums(i, carry, w=w, n=n):
                r0 = pl.multiple_of(i * row_block, row_block)
                blk = pl.ds(r0, row_block)
                mine = pl.ds(pl.multiple_of(c * n + r0, 8), row_block)
                for m in range(3):
                    sends[w][m, blk, :] = (gs[w][peer_chip[m], mine, :] + ras[w][peer_chip[m], blk, :]).astype(BF16)
                owns[w][blk, :] = gs[w][chip, mine, :] + ras[w][chip, blk, :]
                return carry
            lax.fori_loop(0, n // row_block, sums, 0)
        for cp in swaps:
            cp.wait_send()

    vmem = pl.BlockSpec(memory_space=pltpu.VMEM)
    outs = pl.pallas_call(
        body,
        out_shape=[jax.ShapeDtypeStruct((3, n, s.shape[2]), BF16) for n, s in zip(halves, stacks)]
        + [jax.ShapeDtypeStruct((n, s.shape[2]), F32) for n, s in zip(halves, stacks)],
        in_specs=[vmem] * ns, out_specs=[vmem] * (2 * ns),
        scratch_shapes=[pltpu.VMEM((N_CHIPS, n, s.shape[2]), F32) for n, s in zip(halves, stacks)]
        + [pltpu.SemaphoreType.DMA((ns,)), pltpu.SemaphoreType.DMA((ns,))],
        compiler_params=pltpu.CompilerParams(vmem_limit_bytes=VMEM_LIMIT), name="reduce_first")(*stacks)
    return outs[:ns], outs[ns:]


def _reduce_last(owns, landed, g_small):
    ns = len(owns)
    row_block = 32
    hs = SMALL_ROWS // 2

    def body(*refs):
        own_refs, land_refs, gsm_ref = refs[:ns], refs[ns:2 * ns], refs[2 * ns]
        out_refs, osm_ref = refs[2 * ns + 1:3 * ns + 1], refs[3 * ns + 1]
        ra_sm, p_sm, s_sem, r_sem, sm_s, sm_r = refs[3 * ns + 2:]
        x, y, c, chip, peers, peer_chip = _place()
        sib = (x, y, 1 - c)
        half = lambda cc: pl.ds(pl.multiple_of(cc * hs, 8), hs)
        sm_a = _remote(gsm_ref.at[half(1 - c), :], ra_sm, sm_s.at[0], sm_r.at[0], sib)
        sm_a.start()
        swaps = [sm_a]
        for w in range(ns):
            n = own_refs[w].shape[0]

            def total(i, carry, w=w, n=n):
                r0 = pl.multiple_of(i * row_block, row_block)
                blk = pl.ds(r0, row_block)
                acc = own_refs[w][blk, :]
                for m in range(3):
                    acc = acc + land_refs[w][m, blk, :].astype(F32)
                out_refs[w][pl.ds(pl.multiple_of(c * n + r0, 8), row_block), :] = acc
                return carry
            lax.fori_loop(0, n // row_block, total, 0)
            mine = out_refs[w].at[pl.ds(pl.multiple_of(c * n, 8), n), :]
            swaps.append(_remote(mine, mine, s_sem.at[w], r_sem.at[w], sib))
            swaps[-1].start()
        sm_a.wait_recv()
        p_sm[chip] = gsm_ref[half(c), :] + ra_sm[...]
        for m, (px, py) in enumerate(peers):
            swaps.append(_remote(p_sm.at[chip], p_sm.at[chip], sm_s.at[1 + m], sm_r.at[1 + m], (px, py, c)))
            swaps[-1].start()
        for w in range(ns):
            n = own_refs[w].shape[0]
            theirs = out_refs[w].at[pl.ds(pl.multiple_of((1 - c) * n, 8), n), :]
            _remote(theirs, theirs, s_sem.at[w], r_sem.at[w], sib).wait_recv()
        for m, (px, py) in enumerate(peers):
            _remote(p_sm.at[chip], p_sm.at[peer_chip[m]], sm_s.at[1 + m], sm_r.at[1 + m], (px, py, c)).wait_recv()
        osm_ref[half(c), :] = (p_sm[0] + p_sm[1]) + (p_sm[2] + p_sm[3])
        swaps.append(_remote(osm_ref.at[half(c), :], osm_ref.at[half(c), :], sm_s.at[4], sm_r.at[4], sib))
        swaps[-1].start()
        _remote(osm_ref.at[half(1 - c), :], osm_ref.at[half(1 - c), :], sm_s.at[4], sm_r.at[4], sib).wait_recv()
        for cp in swaps:
            cp.wait_send()

    vmem = pl.BlockSpec(memory_space=pltpu.VMEM)
    return pl.pallas_call(
        body, out_shape=[jax.ShapeDtypeStruct((2 * o.shape[0], o.shape[1]), F32) for o in owns]
        + [jax.ShapeDtypeStruct((SMALL_ROWS, LANES), F32)],
        in_specs=[vmem] * (2 * ns + 1), out_specs=[vmem] * (ns + 1),
        scratch_shapes=[pltpu.VMEM((hs, LANES), F32), pltpu.VMEM((N_CHIPS, hs, LANES), F32),
                        pltpu.SemaphoreType.DMA((ns,)), pltpu.SemaphoreType.DMA((ns,)),
                        pltpu.SemaphoreType.DMA((5,)), pltpu.SemaphoreType.DMA((5,))],
        compiler_params=pltpu.CompilerParams(vmem_limit_bytes=VMEM_LIMIT),
        name="reduce_last")(*owns, *landed, g_small)


_SMALL_PARTS = (("g_norm", 8, 8), ("w_s", 512, 512), ("b_s", 4, 8), ("g_v", 2, 8), ("g_mem", 8, 8),
                ("g_final", 8, 8), ("loss", 1, 8))
_LOSS_ROW = SMALL_ROWS - 8
assert sum(p for _, _, p in _SMALL_PARTS) == SMALL_ROWS


def _pack_small(parts, loss=None):
    loss_row = jnp.zeros((1, LANES), F32) if loss is None else jnp.broadcast_to(loss.reshape(1, 1), (1, LANES))
    rows = []
    for (name, used, padded), p in zip(_SMALL_PARTS, list(parts) + [loss_row]):
        p = p.reshape(used, LANES)
        if padded > used:
            p = jnp.pad(p, ((0, padded - used), (0, 0)))
        rows.append(p)
    return jnp.concatenate(rows, axis=0)


def _local_step(x, mem, target, g_norm, w_in, w_s, b_s, g_v, g_mem, late_weights, g_final,
                fwd_token=None, on_dw=None):
    B, S, _ = x.shape
    x2d = x.reshape(B * S, D_MODEL)
    t2d = target.reshape(B * S, D_MODEL)
    mem2d = mem.reshape(B * N_MEM, D_MODEL)

    proj = _inproj_fwd(x2d, g_norm, w_in, after=() if fwd_token is None else (fwd_token,))
    w_kv, w_out = late_weights(proj)
    kv = _kv_fwd(mem2d, g_mem, w_kv)
    a, lse = _attn_fwd(proj, B, S)
    w_sT = jnp.swapaxes(w_s, 1, 2)
    b_tab = jnp.repeat(b_s.T, HEAD_DIM, axis=1)
    (dx2, da, drest, loss, d_wout, d_ws, d_bs, d_gv, d_gf, dkv) = _mid(
        x2d, t2d, a, proj, kv, w_s, w_sT, b_tab, g_v, w_out, g_final, B, S)
    d_wkv, d_gmem = _kv_bwd(mem2d, g_mem, w_kv, dkv)
    dq, dk, dv = _attn_bwd(proj, a, lse, da, B, S)
    d_win = _inproj_bwd_dw(dq, dk, dv, drest, x2d, g_norm)
    grad_x, d_gnorm = _inproj_bwd_dx(dq, dk, dv, drest, x2d, dx2, g_norm, w_in,
                                     after=() if on_dw is None else (on_dw(d_win, d_wkv, d_wout),))
    d_bs = d_bs[:, :N_SGU_GROUPS].T
    return (loss[0, 0], grad_x.reshape(B, S, D_MODEL),
            dict(g_norm=d_gnorm, w_in=d_win, w_s=d_ws, b_s=d_bs, g_v=d_gv, g_mem=d_gmem, w_kv=d_wkv,
                 w_out=d_wout, g_final=d_gf))


def kernel(x, mem, g_norm, w_in, w_sgu_spatial, b_sgu_spatial, g_sgu_v, g_mem, w_mem_kv, w_out, g_final, loss_target, m_g_norm, m_w_in, m_w_sgu_spatial, m_b_sgu_spatial, m_g_sgu_v, m_g_mem, m_w_mem_kv, m_w_out, m_g_final, v_g_norm, v_w_in, v_w_sgu_spatial, v_b_sgu_spatial, v_g_sgu_v, v_g_mem, v_w_mem_kv, v_w_out, v_g_final):
    t = lambda w: jnp.swapaxes(w[0], 0, 1)
    (win_all,), late_shards, late_lands = _ag_weights([t(w_in)], [w_mem_kv[0], w_out[0]])
    w_in_full = win_all.reshape(-1, win_all.shape[-1])
    late = _exchange_start(True, list(late_shards), win_all, "gather_late_start", lands=late_lands)

    def late_weights(proj):
        return [z.reshape(-1, z.shape[-1]) for z in _exchange_wait(True, late, proj, "gather_late_wait")]

    scatter = {}

    def on_dw(d_win, d_wkv, d_wout):
        stacks = [d_win.reshape((N_CHIPS, w_in.shape[2], w_in.shape[1])),
                  d_wkv.reshape((N_CHIPS,) + w_mem_kv.shape[1:]), d_wout.reshape((N_CHIPS,) + w_out.shape[1:])]
        sends, scatter["own"] = _reduce_first(stacks)
        scatter["started"] = _exchange_start(False, list(sends), scatter["own"][0], "scatter_start")
        return scatter["started"][-1]

    loss, grad_x, g = _local_step(
        x, mem, loss_target, g_norm, w_in_full, w_sgu_spatial[0], b_sgu_spatial[0], g_sgu_v, g_mem,
        late_weights, g_final.reshape(1, D_MODEL), fwd_token=late[-1], on_dw=on_dw)

    small_names = ("g_norm", "w_s", "b_s", "g_v", "g_mem", "g_final")
    g_small = _pack_small([g[n] for n in small_names], loss)
    landed = _exchange_wait(False, scatter["started"], g_small, "scatter_wait")
    gr_in, gr_kv, gr_out, gr_small = _reduce_last(scatter["own"], landed, g_small)
    loss = gr_small[_LOSS_ROW, 0]

    small_w = (g_norm, w_sgu_spatial, b_sgu_spatial, g_sgu_v, g_mem, g_final)
    small_m = (m_g_norm, m_w_sgu_spatial, m_b_sgu_spatial, m_g_sgu_v, m_g_mem, m_g_final)
    small_v = (v_g_norm, v_w_sgu_spatial, v_b_sgu_spatial, v_g_sgu_v, v_g_mem, v_g_final)
    rows = lambda ws: [w.reshape(-1, LANES) for w in ws]
    small = [[z.reshape(w.shape) for z in four]
             for w, four in zip(small_w, _adamw_small(gr_small, rows(small_w), rows(small_m), rows(small_v)))]
    d_in, nm_in, nv_in = _adamw(t(w_in), gr_in, t(m_w_in), t(v_w_in), "adamw_w_in")
    gr_in, d_in, nm_in, nv_in = [jnp.swapaxes(z, 0, 1) for z in (gr_in, d_in, nm_in, nv_in)]
    d_kv, nm_kv, nv_kv = _adamw(w_mem_kv[0], gr_kv, m_w_mem_kv[0], v_w_mem_kv[0], "adamw_w_kv")
    d_out, nm_out, nv_out = _adamw(w_out[0], gr_out, m_w_out[0], v_w_out[0], "adamw_w_out")

    def leaves(kind, big_in, big_kv, big_out):
        s_norm, s_ws, s_bs, s_gv, s_gmem, s_gf = [four[kind] for four in small]
        return [s_norm, big_in[None], s_ws, s_bs, s_gv, s_gmem, big_kv[None], big_out[None], s_gf]

    return (loss, grad_x, *leaves(0, gr_in, gr_kv, gr_out), *leaves(1, d_in, d_kv, d_out),
            *leaves(2, nm_in, nm_kv, nm_out), *leaves(3, nv_in, nv_kv, nv_out))
```

```python
import functools

import jax
import jax.numpy as jnp
from jax import lax
from jax.experimental import pallas as pl
from jax.experimental.pallas import tpu as pltpu

F32 = jnp.float32
BF16 = jnp.bfloat16
MESH = pl.DeviceIdType.MESH

D_MODEL = 1024
ATTN_WIDTH = 512
SGU_WIDTH = 256
MEM_WIDTH = 256
N_MEM = 256
IN_COLS = 3328
QKV_COLS = 3 * ATTN_WIDTH
REST_COLS = IN_COLS - QKV_COLS
SGU_CHUNK = 128
N_SGU_GROUPS = 4
EPS = 1e-6
NEG_INF = -1e30
DILATIONS = (1, 4, 16)
RADIUS = 64
Q_BLOCK = 128
LANES = 128
HEAD_DIM = 64

ADAM_LR = 0.001
ADAM_B1 = 0.9
ADAM_B2 = 0.999
ADAM_EPS = 1e-08
ADAM_WD = 0.01
ADAM_STEP = 10

N_CHIPS = 4
VMEM_LIMIT = 56 * 1024 * 1024
SMALL_ROWS = 560


def _params(sem=None, vmem=VMEM_LIMIT):
    return pltpu.CompilerParams(dimension_semantics=sem, vmem_limit_bytes=vmem)


def _nn(a, b):
    return jnp.dot(a, b, preferred_element_type=F32)


def _nt(a, b):
    return lax.dot_general(a, b, (((1,), (1,)), ((), ())), preferred_element_type=F32)


def _tn(a, b):
    return lax.dot_general(a, b, (((0,), (0,)), ((), ())), preferred_element_type=F32)


def _rms(x):
    r = lax.rsqrt(jnp.mean(x * x, axis=-1, keepdims=True) + EPS)
    return r, x * r


def _head_masks():
    lane = lax.broadcasted_iota(jnp.int32, (1, LANES), 1)
    lo = lane < HEAD_DIM
    return lo, (lo.astype(F32), (~lo).astype(F32))


def _silu_parts(z):
    s = jax.nn.sigmoid(z)
    return z * s, s * (1.0 + z * (1.0 - s))


def _gelu_parts(x):
    c = 0.7978845608028654
    x2 = x * x
    t = jnp.tanh(c * (x + 0.044715 * (x * x2)))
    val = 0.5 * x * (1.0 + t)
    grad = 0.5 * (1.0 + t) + 0.5 * x * (1.0 - t * t) * (c * (1.0 + 3.0 * 0.044715 * x2))
    return val, grad


def _after(tokens):
    return [pl.BlockSpec(memory_space=pl.ANY)] * len(tokens)


SHARD_ROWS = IN_COLS // N_CHIPS
INTERIOR = 768


def _inproj_gather(x2d, g_norm, w_shard, late):
    T = x2d.shape[0]
    tm = 512
    nt = T // tm
    nl = len(late)
    half = SHARD_ROWS // 2
    straddles = (INTERIOR, 2 * SHARD_ROWS + INTERIOR)

    def body(x_ref, g_ref, w_ref, *rest):
        late_srcs = rest[:nl]
        proj_hbm, wt_hbm = rest[nl], rest[nl + 1]
        late_bf, late_land = rest[nl + 2:2 * nl + 2], rest[2 * nl + 2:3 * nl + 2]
        wt, h_scr, stage, edge, s_ici, r_ici, s_d2d, r_d2d, out_sem, edge_sem, wt_sem = rest[3 * nl + 2:]
        p, i = pl.program_id(0), pl.program_id(1)
        x, y, c = lax.axis_index("x"), lax.axis_index("y"), lax.axis_index("c")
        chip = 2 * x + y
        sib = (x, y, 1 - c)
        first = ((x + 1 - c) % 2, (y + c) % 2)
        second = ((x + c) % 2, (y + 1 - c) % 2)
        first_chip, second_chip = 2 * first[0] + first[1], 2 * second[0] + second[1]
        diag_chip = 3 - chip

        def rows_of(k, cc):
            return wt.at[pl.ds(pl.multiple_of(k * SHARD_ROWS + cc * half, 16), half), :]

        def ici(slot, k, dev):
            return _remote(rows_of(k, c), rows_of(k, c), s_ici.at[slot], r_ici.at[slot], (dev[0], dev[1], c))

        def d2d(slot, k, cc):
            return _remote(rows_of(k, cc), rows_of(k, cc), s_d2d.at[slot], r_d2d.at[slot], sib)

        at_start = lambda phase: jnp.logical_and(p == phase, i == 0)

        @pl.when(at_start(0))
        def _():
            wt[pl.ds(pl.multiple_of(chip * SHARD_ROWS, 16), SHARD_ROWS), :] = w_ref[...].astype(BF16)
            ici(0, chip, first).start()
            ici(1, chip, second).start()
            for src, bf, land in zip(late_srcs, late_bf, late_land):
                bf[...] = src[...].astype(BF16)
                land[...] = jnp.zeros_like(land)
                land[chip] = bf[...]

        @pl.when(at_start(1))
        def _():
            ici(0, first_chip, first).wait_recv()
            ici(2, first_chip, second).start()
            d2d(0, first_chip, c).start()
            ici(1, second_chip, second).wait_recv()
            d2d(1, second_chip, c).start()
            d2d(0, second_chip, 1 - c).wait_recv()
            d2d(1, first_chip, 1 - c).wait_recv()

        @pl.when(at_start(3))
        def _():
            ici(2, diag_chip, second).wait_recv()
            d2d(2, diag_chip, c).start()
            d2d(2, diag_chip, 1 - c).wait_recv()

        tile_rows = pl.ds(pl.multiple_of(i * tm, tm), tm)
        step = p * nt + i
        slot = step % 2

        def out_copy(sl, col0):
            return pltpu.make_async_copy(stage.at[sl], proj_hbm.at[tile_rows, pl.ds(pl.multiple_of(col0, LANES), INTERIOR)],
                                         out_sem.at[sl])

        def edge_copy(e):
            return pltpu.make_async_copy(edge.at[e], proj_hbm.at[tile_rows, pl.ds(straddles[e], LANES)],
                                         edge_sem.at[e])

        @pl.when(p == 0)
        def _():
            _, xh = _rms(x_ref[...])
            h_scr[tile_rows, :] = (xh * g_ref[...]).astype(BF16)

        @pl.when(p < N_CHIPS)
        def _():
            k = jnp.where(p == 0, chip, jnp.where(p == 1, first_chip, jnp.where(p == 2, second_chip, diag_chip)))
            col0 = k * SHARD_ROWS + (k % 2) * (LANES // 2)

            @pl.when(step >= 2)
            def _():
                out_copy(slot, col0).wait()
            stage[slot] = _nt(h_scr[tile_rows, :], wt[pl.ds(pl.multiple_of(col0, LANES), INTERIOR), :])
            out_copy(slot, col0).start()

        @pl.when(p == N_CHIPS)
        def _():
            @pl.when(i == 0)
            def _():
                for sl in (0, 1):
                    out_copy(sl, 0).wait()

            @pl.when(i > 0)
            def _():
                for e in (0, 1):
                    edge_copy(e).wait()
            for e in (0, 1):
                edge[e] = _nt(h_scr[tile_rows, :], wt[straddles[e]:straddles[e] + LANES, :])
                edge_copy(e).start()

            @pl.when(i == nt - 1)
            def _():
                for e in (0, 1):
                    edge_copy(e).wait()
                whole = pltpu.make_async_copy(wt, wt_hbm, wt_sem)
                whole.start()
                for slot_ in (0, 1, 2):
                    ici(slot_, chip, first).wait_send()
                    d2d(slot_, chip, c).wait_send()
                whole.wait()

    vmem = pl.BlockSpec(memory_space=pltpu.VMEM)
    hbm = pl.BlockSpec(memory_space=pl.ANY)
    outs = pl.pallas_call(
        body, grid=(N_CHIPS + 1, nt),
        in_specs=[pl.BlockSpec((tm, D_MODEL), lambda p, i: (jnp.where(p == 0, i, nt - 1), 0)),
                  pl.BlockSpec((1, D_MODEL), lambda p, i: (0, 0)), vmem] + [vmem] * nl,
        out_specs=[hbm, hbm] + [vmem] * (2 * nl),
        out_shape=[jax.ShapeDtypeStruct((T, IN_COLS), F32), jax.ShapeDtypeStruct((IN_COLS, D_MODEL), BF16)]
        + [jax.ShapeDtypeStruct(w.shape, BF16) for w in late]
        + [jax.ShapeDtypeStruct((N_CHIPS,) + w.shape, BF16) for w in late],
        scratch_shapes=[pltpu.VMEM((IN_COLS, D_MODEL), BF16), pltpu.VMEM((T, D_MODEL), BF16),
                        pltpu.VMEM((2, tm, INTERIOR), F32), pltpu.VMEM((2, tm, LANES), F32),
                        pltpu.SemaphoreType.DMA((3,)), pltpu.SemaphoreType.DMA((3,)),
                        pltpu.SemaphoreType.DMA((3,)), pltpu.SemaphoreType.DMA((3,)),
                        pltpu.SemaphoreType.DMA((2,)), pltpu.SemaphoreType.DMA((2,)), pltpu.SemaphoreType.DMA],
        compiler_params=_params(("arbitrary", "arbitrary")), name="inproj_gather")(x2d, g_norm, w_shard, *late)
    return outs[0], outs[1], outs[2:2 + nl], outs[2 + nl:]


def _kv_fwd(mem2d, g_mem, w_kv):
    Tm = mem2d.shape[0]

    def body(m_ref, g_ref, w_ref, o_ref):
        _, mh = _rms(m_ref[...])
        o_ref[...] = _nn((mh * g_ref[...]).astype(BF16), w_ref[...])

    return pl.pallas_call(
        body, out_shape=jax.ShapeDtypeStruct((Tm, 2 * MEM_WIDTH), F32),
        compiler_params=_params(), name="kv_fwd")(mem2d, g_mem, w_kv)


def _kv_bwd(mem2d, g_mem, w_kv, dkv):
    Tm = mem2d.shape[0]

    def body(m_ref, g_ref, w_ref, dkv_ref, dw_ref, dg_ref):
        _, mh = _rms(m_ref[...])
        memn = (mh * g_ref[...]).astype(BF16)
        dkvb = dkv_ref[...].astype(BF16)
        dw_ref[...] = _tn(memn, dkvb)
        dmemn = _nt(dkvb, w_ref[...])
        dg_ref[...] = jnp.sum(dmemn * mh, axis=0, keepdims=True)

    return pl.pallas_call(
        body, out_shape=(jax.ShapeDtypeStruct((D_MODEL, 2 * MEM_WIDTH), F32),
                         jax.ShapeDtypeStruct((1, D_MODEL), F32)),
        compiler_params=_params(), name="kv_bwd")(mem2d, g_mem, w_kv, dkv)


def _attn_geometry(S):
    geom = []
    for d in DILATIONS:
        L = S // d
        assert L % Q_BLOCK == 0
        geom.append((d, L, min(2 * Q_BLOCK, L), L // Q_BLOCK))
    return geom


def _init_bias(bias_scr, geom, hp):
    row = lax.broadcasted_iota(jnp.int32, (Q_BLOCK, 2 * Q_BLOCK), 0)
    col = lax.broadcasted_iota(jnp.int32, (Q_BLOCK, 2 * Q_BLOCK), 1)
    for j in (0, 1):
        bits = (126 - (2 * hp + j)) * (1 << 23)
        slope = lax.bitcast_convert_type(jnp.full((1, 1), bits, jnp.int32), F32)
        for di, (d, _, _, _) in enumerate(geom):
            for cls, off in enumerate((0, -RADIUS, -2 * RADIUS)):
                dist = jnp.abs(col - row + off)
                bias_scr[di * 6 + cls * 2 + j] = jnp.where(
                    dist <= RADIUS, -(slope * float(d)) * dist.astype(F32), NEG_INF)


def _block_slices(d, L, KW, nqb, r, qb):
    qs = qb * Q_BLOCK
    ks = jnp.clip(qs - RADIUS, 0, L - KW)
    cls = jnp.where(qb == 0, 0, jnp.where(qb == nqb - 1, 2, 1))
    if d == 1:
        qsl = pl.ds(pl.multiple_of(qs, Q_BLOCK), Q_BLOCK)
        ksl = pl.ds(pl.multiple_of(ks, RADIUS), KW)
    else:
        qsl = pl.ds(r + qs * d, Q_BLOCK, stride=d)
        ksl = pl.ds(r + ks * d, KW, stride=d)
    return qsl, ksl, cls


def _for_groups(geom, group, fn):
    for di, (d, L, KW, nqb) in enumerate(geom):
        assert (d * nqb) % group == 0

        def step(it, carry, di=di, d=d, L=L, KW=KW, nqb=nqb):
            slices = []
            for g in range(group):
                i = it * group + g
                slices.append(_block_slices(d, L, KW, nqb, i // nqb, i % nqb))
            fn(di, KW, slices)
            return carry
        lax.fori_loop(0, d * nqb // group, step, 0)


def _attn_fwd(proj, B, S, after=()):
    T = B * S
    geom = _attn_geometry(S)
    n_pairs = ATTN_WIDTH // LANES

    def body(q_ref, k_ref, v_ref, *rest):
        a_ref, lse_ref, bias_scr = rest[len(after):len(after) + 3]
        per_dilation = rest[len(after) + 3:]
        o_scr, m_scr, l_scr = per_dilation[0:3], per_dilation[3:6], per_dilation[6:9]
        lo, hm = _head_masks()
        _init_bias(bias_scr, geom, pl.program_id(1))

        def group(di, KW, slices):
            chains = [(g, j) for g in range(len(slices)) for j in (0, 1)]
            q = [q_ref[qsl, :] for qsl, _, _ in slices]
            kw = [k_ref[ksl, :].astype(BF16) for _, ksl, _ in slices]
            vw = [v_ref[ksl, :].astype(BF16) for _, ksl, _ in slices]
            s = {(g, j): _nt((q[g] * (hm[j] * 0.125)).astype(BF16), kw[g])
                 + bias_scr[di * 6 + slices[g][2] * 2 + j, :, pl.ds(0, KW)] for g, j in chains}
            m = {c: jnp.max(s[c], axis=1, keepdims=True) for c in chains}
            p = {c: jnp.exp(s[c] - m[c]) for c in chains}
            l = {c: jnp.sum(p[c], axis=1, keepdims=True) for c in chains}
            o = {(g, j): _nn(p[(g, j)].astype(BF16), vw[g]) for g, j in chains}
            for g, (qsl, _, _) in enumerate(slices):
                o_scr[di][qsl, :] = jnp.where(lo, o[(g, 0)], o[(g, 1)])
                m_scr[di][qsl, :] = jnp.where(lo, m[(g, 0)], m[(g, 1)])
                l_scr[di][qsl, :] = jnp.where(lo, l[(g, 0)], l[(g, 1)])

        _for_groups(geom, 8, group)

        rows_per = 256

        def combine(i, carry):
            rows = pl.ds(pl.multiple_of(i * rows_per, rows_per), rows_per)
            ms = [m_scr[di][rows, :] for di in range(3)]
            mx = jnp.maximum(jnp.maximum(ms[0], ms[1]), ms[2])
            num = 0.0
            den = 0.0
            for di in range(3):
                w = jnp.exp(ms[di] - mx)
                num = num + w * o_scr[di][rows, :]
                den = den + w * l_scr[di][rows, :]
            a_ref[rows, :] = num / den
            lse_ref[rows, :] = mx + jnp.log(den)
            return carry

        lax.fori_loop(0, S // rows_per, combine, 0)

    blk = lambda off: pl.BlockSpec((S, LANES), lambda b, h, off=off: (b, off + h))
    out_blk = pl.BlockSpec((S, LANES), lambda b, h: (b, h))
    return pl.pallas_call(
        body, grid=(B, n_pairs),
        in_specs=[blk(0), blk(n_pairs), blk(2 * n_pairs)] + _after(after),
        out_specs=[out_blk, out_blk],
        out_shape=[jax.ShapeDtypeStruct((T, ATTN_WIDTH), F32)] * 2,
        scratch_shapes=[pltpu.VMEM((18, Q_BLOCK, 2 * Q_BLOCK), F32)] + [pltpu.VMEM((S, LANES), F32)] * 9,
        compiler_params=_params(("arbitrary", "arbitrary")), name="attn_fwd")(proj, proj, proj, *after)


def _attn_bwd(proj, a, lse, da, B, S):
    T = B * S
    geom = _attn_geometry(S)
    n_pairs = ATTN_WIDTH // LANES

    def body(q_ref, k_ref, v_ref, a_ref, lse_ref, do_ref, dq_ref, dk_ref, dv_ref,
             bias_scr, dq_scr, dk_scr, dv_scr):
        _, hm = _head_masks()
        _init_bias(bias_scr, geom, pl.program_id(1))
        dq_scr[...] = jnp.zeros_like(dq_scr)
        dk_scr[...] = jnp.zeros_like(dk_scr)
        dv_scr[...] = jnp.zeros_like(dv_scr)

        def group(di, KW, slices):
            n = len(slices)
            chains = [(g, j) for g in range(n) for j in (0, 1)]
            q = [q_ref[qsl, :] for qsl, _, _ in slices]
            do = [do_ref[qsl, :] for qsl, _, _ in slices]
            doa = [do[g] * a_ref[slices[g][0], :] for g in range(n)]
            lse_q = [lse_ref[qsl, :] for qsl, _, _ in slices]
            kw = [k_ref[ksl, :].astype(BF16) for _, ksl, _ in slices]
            vw = [v_ref[ksl, :].astype(BF16) for _, ksl, _ in slices]
            qj = {(g, j): (q[g] * (hm[j] * 0.125)).astype(BF16) for g, j in chains}
            doj = {(g, j): (do[g] * hm[j]).astype(BF16) for g, j in chains}
            s = {(g, j): _nt(qj[(g, j)], kw[g])
                 + bias_scr[di * 6 + slices[g][2] * 2 + j, :, pl.ds(0, KW)] for g, j in chains}
            dp = {(g, j): _nt(doj[(g, j)], vw[g]) for g, j in chains}
            dsum = {(g, j): jnp.sum(doa[g] * hm[j], axis=1, keepdims=True) for g, j in chains}
            p = {(g, j): jnp.exp(s[(g, j)] - lse_q[g][:, HEAD_DIM * j:HEAD_DIM * j + 1]) for g, j in chains}
            ds = {c: (p[c] * (dp[c] - dsum[c])).astype(BF16) for c in chains}
            pb = {c: p[c].astype(BF16) for c in chains}
            dq = [_nn(ds[(g, 0)], kw[g]) * (hm[0] * 0.125) + _nn(ds[(g, 1)], kw[g]) * (hm[1] * 0.125)
                  for g in range(n)]
            both = lambda t, g: jnp.concatenate([t[(g, 0)], t[(g, 1)]], axis=0)
            dkw = [_tn(both(ds, g), both(qj, g)) for g in range(n)]
            dvw = [_tn(both(pb, g), both(doj, g)) for g in range(n)]
            for g, (qsl, ksl, _) in enumerate(slices):
                dq_scr[qsl, :] = dq_scr[qsl, :] + dq[g]
                dk_scr[ksl, :] = dk_scr[ksl, :] + dkw[g]
                dv_scr[ksl, :] = dv_scr[ksl, :] + dvw[g]

        _for_groups(geom, 4, group)
        dq_ref[...] = dq_scr[...].astype(BF16)
        dk_ref[...] = dk_scr[...].astype(BF16)
        dv_ref[...] = dv_scr[...].astype(BF16)

    blk = lambda off: pl.BlockSpec((S, LANES), lambda b, h, off=off: (b, off + h))
    return pl.pallas_call(
        body, grid=(B, n_pairs),
        in_specs=[blk(0), blk(n_pairs), blk(2 * n_pairs), blk(0), blk(0), blk(0)],
        out_specs=[blk(0), blk(0), blk(0)],
        out_shape=[jax.ShapeDtypeStruct((T, ATTN_WIDTH), BF16)] * 3,
        scratch_shapes=[pltpu.VMEM((18, Q_BLOCK, 2 * Q_BLOCK), F32),
                        pltpu.VMEM((S, LANES), F32),
                        pltpu.VMEM((S, LANES), F32),
                        pltpu.VMEM((S, LANES), F32)],
        compiler_params=_params(("arbitrary", "arbitrary")), name="attn_bwd")(proj, proj, proj, a, lse, da)


def _mid(x2d, t2d, a, proj, kv, w_s, w_sT, b_tab, g_v, w_out, g_final, B, S):
    T = B * S
    tm = 512
    nt = S // tm
    halves = 2
    hrows = tm // halves

    def body(x_ref, t_ref, a_ref, za_ref, ub_ref, vb_ref, zb_ref, qm_ref, zm_ref, kv_ref,
              ws_ref, wsT_ref, btab_ref, gv_ref, wout_ref, gf_ref,
              dx2_ref, da_ref, drest_ref, loss_ref, dwout_ref, dws_ref, dbs_ref, dgv_ref, dgf_ref, dkv_ref,
              dbtab_scr):
        b = pl.program_id(0)
        t = pl.program_id(1)
        first = jnp.logical_and(b == 0, t == 0)
        last = jnp.logical_and(b == B - 1, t == nt - 1)
        _, hm = _head_masks()
        lane_g = lax.broadcasted_iota(jnp.int32, (1, SGU_WIDTH), 1) // HEAD_DIM
        gm = [(lane_g == g).astype(F32) for g in range(N_SGU_GROUPS)]
        H = range(halves)
        rows = [pl.ds(h * hrows, hrows) for h in H]
        ld = lambda ref: [ref[r, :] for r in rows]
        cat = lambda parts, axis: jnp.concatenate(parts, axis=axis)
        chunks = [slice(ci * SGU_CHUNK, (ci + 1) * SGU_CHUNK) for ci in range(hrows // SGU_CHUNK)]
        pairs = [slice(pr * LANES, (pr + 1) * LANES) for pr in range(2)]
        heads = [(pr, j) for pr in range(2) for j in (0, 1)]

        @pl.when(first)
        def _():
            loss_ref[...] = jnp.zeros_like(loss_ref)
            dwout_ref[...] = jnp.zeros_like(dwout_ref)
            dws_ref[...] = jnp.zeros_like(dws_ref)
            dbs_ref[...] = jnp.zeros_like(dbs_ref)
            dgv_ref[...] = jnp.zeros_like(dgv_ref)
            dgf_ref[...] = jnp.zeros_like(dgf_ref)
            dbtab_scr[...] = jnp.zeros_like(dbtab_scr)

        @pl.when(t == 0)
        def _():
            dkv_ref[...] = jnp.zeros_like(dkv_ref)

        a_val = ld(a_ref)
        sil_a = [_silu_parts(z) for z in ld(za_ref)]
        gated_a = [s[0] * a for s, a in zip(sil_a, a_val)]
        u = [_gelu_parts(z) for z in ld(ub_ref)]
        vv = [_gelu_parts(z) for z in ld(vb_ref)]
        vnorm = [_rms(v[0]) for v in vv]
        gv = gv_ref[...]
        vn = [(n[1] * gv).astype(BF16) for n in vnorm]
        w_cat = cat([ws_ref[g].astype(BF16) for g in range(N_SGU_GROUPS)], 1)
        wT_cat = cat([wsT_ref[g].astype(BF16) for g in range(N_SGU_GROUPS)], 1)
        gmb = [m.astype(BF16) for m in gm]
        by_group = lambda chunk: cat([chunk * gmb[g] for g in range(N_SGU_GROUPS)], 0)
        btab = btab_ref[...]
        mixed = [cat([btab + _nn(w_cat, by_group(vn[h][c, :])) for c in chunks], 0) for h in H]
        sg = [u[h][0] * mixed[h] for h in H]
        sil_b = [_silu_parts(z) for z in ld(zb_ref)]
        gated_b = [sil_b[h][0] * sg[h] for h in H]

        kvv = kv_ref[...].astype(BF16)
        kp = [kvv[:, p] for p in pairs]
        vp = [kvv[:, MEM_WIDTH + pr * LANES:MEM_WIDTH + (pr + 1) * LANES] for pr in range(2)]
        qm = ld(qm_ref)
        qj = {(h, pr, j): (qm[h][:, pairs[pr]] * (hm[j] * 0.125)).astype(BF16) for h in H for pr, j in heads}
        sc = {k: _nt(qj[k], kp[k[1]]) for k in qj}
        ex = {k: jnp.exp(sc[k] - jnp.max(sc[k], axis=1, keepdims=True)) for k in qj}
        prob = {k: ex[k] * (1.0 / jnp.sum(ex[k], axis=1, keepdims=True)) for k in qj}
        probb = {k: prob[k].astype(BF16) for k in qj}
        mo = [cat([sum(_nn(probb[(h, pr, j)], vp[pr]) * hm[j] for j in (0, 1)) for pr in range(2)], 1) for h in H]
        sil_m = [_silu_parts(z) for z in ld(zm_ref)]
        gated_m = [sil_m[h][0] * mo[h] for h in H]

        gated = [cat([gated_a[h], gated_b[h], gated_m[h]], 1).astype(BF16) for h in H]
        wout = wout_ref[...]
        x_in = ld(x_ref)
        x2 = [x_in[h] + _nn(gated[h], wout) for h in H]
        fin = [_rms(z) for z in x2]
        gf = gf_ref[...]
        tgt = ld(t_ref)
        err = [fin[h][1] * gf - tgt[h] for h in H]
        loss_ref[...] += sum(jnp.sum(e * e) for e in err) * (0.5 / D_MODEL)

        dy = [e * (1.0 / D_MODEL) for e in err]
        dgf_ref[...] += sum(jnp.sum(dy[h] * fin[h][1], axis=0, keepdims=True) for h in H)
        gdy = [d * gf for d in dy]
        dx2 = [fin[h][0] * (gdy[h] - fin[h][1] * jnp.mean(gdy[h] * fin[h][1], axis=1, keepdims=True)) for h in H]
        for h in H:
            dx2_ref[rows[h], :] = dx2[h]
        dx2b = [d.astype(BF16) for d in dx2]
        dgated = [_nt(d, wout) for d in dx2b]
        dwout_ref[...] += _tn(cat(gated, 0), cat(dx2b, 0))
        dga = [d[:, 0:ATTN_WIDTH] for d in dgated]
        dgb = [d[:, ATTN_WIDTH:ATTN_WIDTH + SGU_WIDTH] for d in dgated]
        dgm = [d[:, ATTN_WIDTH + SGU_WIDTH:] for d in dgated]

        for h in H:
            da_ref[rows[h], :] = dga[h] * sil_a[h][0]
        dza = [dga[h] * a_val[h] * sil_a[h][1] for h in H]

        dsg = [dgb[h] * sil_b[h][0] for h in H]
        dzb = [dgb[h] * sg[h] * sil_b[h][1] for h in H]
        dub = [dsg[h] * mixed[h] * u[h][1] for h in H]
        dmixed = [dsg[h] * u[h][0] for h in H]
        dmixed_b = [d.astype(BF16) for d in dmixed]
        dvn = [cat([_nn(wT_cat, by_group(dmixed_b[h][c, :])) for c in chunks], 0) for h in H]
        for g in range(N_SGU_GROUPS):
            dws_ref[g] += sum(_nt((dmixed[h][c, :] * gm[g]).astype(BF16), vn[h][c, :]) for h in H for c in chunks)
        dbtab_scr[...] += sum(dmixed[h][c, :] for h in H for c in chunks)
        dgv_ref[...] += sum(jnp.sum(dvn[h] * vnorm[h][1], axis=0, keepdims=True) for h in H)
        tv = [d * gv for d in dvn]
        dvv = [vnorm[h][0] * (tv[h] - vnorm[h][1] * jnp.mean(tv[h] * vnorm[h][1], axis=1, keepdims=True)) for h in H]
        dvb = [dvv[h] * vv[h][1] for h in H]

        dmo = [dgm[h] * sil_m[h][0] for h in H]
        dzm = [dgm[h] * mo[h] * sil_m[h][1] for h in H]
        dmoj = {(h, pr, j): (dmo[h][:, pairs[pr]] * hm[j]).astype(BF16) for h in H for pr, j in heads}
        dp = {k: _nt(dmoj[k], vp[k[1]]) for k in qj}
        ds = {k: (prob[k] * (dp[k] - jnp.sum(dp[k] * prob[k], axis=1, keepdims=True))).astype(BF16) for k in qj}
        dqm = [cat([sum(_nn(ds[(h, pr, j)], kp[pr]) * (hm[j] * 0.125) for j in (0, 1)) for pr in range(2)], 1)
               for h in H]
        every = lambda tbl, pr: cat([tbl[(h, pr, j)] for h in H for j in (0, 1)], 0)
        dk = [_tn(every(ds, pr), every(qj, pr)) for pr in range(2)]
        dv = [_tn(every(probb, pr), every(dmoj, pr)) for pr in range(2)]
        dkv_ref[...] += cat(dk + dv, 1)

        for h in H:
            drest_ref[rows[h], :] = cat([dza[h], dub[h], dvb[h], dzb[h], dqm[h], dzm[h]], 1).astype(BF16)

        @pl.when(last)
        def _():
            lane = lax.broadcasted_iota(jnp.int32, (1, LANES), 1)
            dbt = dbtab_scr[...]
            out = jnp.zeros((SGU_CHUNK, LANES), F32)
            for g in range(N_SGU_GROUPS):
                out = out + jnp.where(lane == g, jnp.sum(dbt * gm[g], axis=1, keepdims=True), 0.0)
            dbs_ref[...] = out

    tile = lambda w, cb: pl.BlockSpec((tm, w), lambda b, t, cb=cb: (b * nt + t, cb))
    const = lambda shape: pl.BlockSpec(shape, lambda b, t, n=len(shape): (0,) * n)
    return pl.pallas_call(
        body, grid=(B, nt),
        in_specs=[tile(D_MODEL, 0), tile(D_MODEL, 0), tile(ATTN_WIDTH, 0),
                  tile(ATTN_WIDTH, 3),
                  tile(SGU_WIDTH, 8), tile(SGU_WIDTH, 9), tile(SGU_WIDTH, 10),
                  tile(MEM_WIDTH, 11), tile(MEM_WIDTH, 12),
                  pl.BlockSpec((N_MEM, 2 * MEM_WIDTH), lambda b, t: (b, 0)),
                  const((N_SGU_GROUPS, SGU_CHUNK, SGU_CHUNK)), const((N_SGU_GROUPS, SGU_CHUNK, SGU_CHUNK)),
                  const((SGU_CHUNK, SGU_WIDTH)), const((1, SGU_WIDTH)),
                  const((D_MODEL, D_MODEL)), const((1, D_MODEL))],
        out_specs=[tile(D_MODEL, 0), tile(ATTN_WIDTH, 0), tile(REST_COLS, 0),
                   const((8, LANES)), const((D_MODEL, D_MODEL)),
                   const((N_SGU_GROUPS, SGU_CHUNK, SGU_CHUNK)), const((SGU_CHUNK, LANES)),
                   const((1, SGU_WIDTH)), const((1, D_MODEL)),
                   pl.BlockSpec((N_MEM, 2 * MEM_WIDTH), lambda b, t: (b, 0))],
        out_shape=[jax.ShapeDtypeStruct((T, D_MODEL), F32), jax.ShapeDtypeStruct((T, ATTN_WIDTH), F32),
                   jax.ShapeDtypeStruct((T, REST_COLS), BF16),
                   jax.ShapeDtypeStruct((8, LANES), F32), jax.ShapeDtypeStruct((D_MODEL, D_MODEL), F32),
                   jax.ShapeDtypeStruct((N_SGU_GROUPS, SGU_CHUNK, SGU_CHUNK), F32),
                   jax.ShapeDtypeStruct((SGU_CHUNK, LANES), F32),
                   jax.ShapeDtypeStruct((1, SGU_WIDTH), F32), jax.ShapeDtypeStruct((1, D_MODEL), F32),
                   jax.ShapeDtypeStruct((B * N_MEM, 2 * MEM_WIDTH), F32)],
        scratch_shapes=[pltpu.VMEM((SGU_CHUNK, SGU_WIDTH), F32)],
        compiler_params=_params(("arbitrary", "arbitrary")), name="mid")(
            x2d, t2d, a, proj, proj, proj, proj, proj, proj, kv, w_s, w_sT, b_tab, g_v, w_out, g_final)


def _inproj_bwd_dx(dq, dk, dv, drest, x2d, dx2, g_norm, w_in_t, after=()):
    T = x2d.shape[0]
    tm = 512
    W = ATTN_WIDTH

    def body(dq_ref, dk_ref, dv_ref, dr_ref, x_ref, dx2_ref, g_ref, w_ref, *rest):
        gx_ref, dg_ref = rest[-2:]

        @pl.when(pl.program_id(0) == 0)
        def _():
            dg_ref[...] = jnp.zeros_like(dg_ref)

        halves = [pl.ds(h * (tm // 2), tm // 2) for h in (0, 1)]
        dh = [(_nn(dq_ref[r, :], w_ref[0:W, :]) + _nn(dk_ref[r, :], w_ref[W:2 * W, :])
               + _nn(dv_ref[r, :], w_ref[2 * W:3 * W, :]) + _nn(dr_ref[r, :], w_ref[QKV_COLS:IN_COLS, :]))
              for r in halves]
        nrm = [_rms(x_ref[r, :]) for r in halves]
        dg_ref[...] += sum(jnp.sum(d * n[1], axis=0, keepdims=True) for d, n in zip(dh, nrm))
        g = g_ref[...]
        for r, d, (rstd, xh) in zip(halves, dh, nrm):
            th = d * g
            gx_ref[r, :] = rstd * (th - xh * jnp.mean(th * xh, axis=1, keepdims=True)) + dx2_ref[r, :]

    tile = lambda w: pl.BlockSpec((tm, w), lambda i: (i, 0))
    return pl.pallas_call(
        body, grid=(T // tm,),
        in_specs=[tile(W), tile(W), tile(W), tile(REST_COLS), tile(D_MODEL), tile(D_MODEL),
                  pl.BlockSpec((1, D_MODEL), lambda i: (0, 0)),
                  pl.BlockSpec((IN_COLS, D_MODEL), lambda i: (0, 0))] + _after(after),
        out_specs=[tile(D_MODEL), pl.BlockSpec((1, D_MODEL), lambda i: (0, 0))],
        out_shape=[jax.ShapeDtypeStruct((T, D_MODEL), F32), jax.ShapeDtypeStruct((1, D_MODEL), F32)],
        compiler_params=_params(("arbitrary",)), name="inproj_bwd_dx")(
            dq, dk, dv, drest, x2d, dx2, g_norm, w_in_t, *after)


def _inproj_bwd_dw(dq, dk, dv, drest, x2d, g_norm):
    T = x2d.shape[0]
    tm = 1024
    W = ATTN_WIDTH

    def body(dq_ref, dk_ref, dv_ref, dr_ref, x_ref, g_ref, dw_ref):
        @pl.when(pl.program_id(0) == 0)
        def _():
            dw_ref[...] = jnp.zeros_like(dw_ref)

        _, xh = _rms(x_ref[...])
        h = (xh * g_ref[...]).astype(BF16)
        dw_ref[0:W, :] += _tn(dq_ref[...], h)
        dw_ref[W:2 * W, :] += _tn(dk_ref[...], h)
        dw_ref[2 * W:3 * W, :] += _tn(dv_ref[...], h)
        dw_ref[QKV_COLS:IN_COLS, :] += _tn(dr_ref[...], h)

    tile = lambda w: pl.BlockSpec((tm, w), lambda i: (i, 0))
    return pl.pallas_call(
        body, grid=(T // tm,),
        in_specs=[tile(W), tile(W), tile(W), tile(REST_COLS), tile(D_MODEL),
                  pl.BlockSpec((1, D_MODEL), lambda i: (0, 0))],
        out_specs=pl.BlockSpec((IN_COLS, D_MODEL), lambda i: (0, 0)),
        out_shape=jax.ShapeDtypeStruct((IN_COLS, D_MODEL), F32),
        compiler_params=_params(("arbitrary",)), name="inproj_bwd_dw")(dq, dk, dv, drest, x2d, g_norm)


def _adamw_update(w, g, m, v):
    nm = ADAM_B1 * m + (1.0 - ADAM_B1) * g
    nv = ADAM_B2 * v + (1.0 - ADAM_B2) * (g * g)
    m_hat = nm / (1.0 - ADAM_B1 ** ADAM_STEP)
    v_hat = nv / (1.0 - ADAM_B2 ** ADAM_STEP)
    return -ADAM_LR * (m_hat / (jnp.sqrt(v_hat) + ADAM_EPS) + ADAM_WD * w), nm, nv


def _adamw(w, g, m, v, name):
    R, C = w.shape
    br = max(r for r in range(8, 257, 8) if R % r == 0)

    def body(w_ref, g_ref, m_ref, v_ref, d_ref, nm_ref, nv_ref):
        d_ref[...], nm_ref[...], nv_ref[...] = _adamw_update(w_ref[...], g_ref[...], m_ref[...], v_ref[...])

    spec = pl.BlockSpec((br, C), lambda i: (i, 0))
    return pl.pallas_call(
        body, grid=(R // br,), in_specs=[spec] * 4, out_specs=[spec] * 3,
        out_shape=[jax.ShapeDtypeStruct((R, C), F32)] * 3,
        compiler_params=_params(("arbitrary",)), name=name)(w, g, m, v)


def _adamw_small(g_packed, ws, ms, vs):
    n = len(ws)

    def body(*refs):
        g_ref = refs[0]
        w_refs, m_refs, v_refs = refs[1:1 + n], refs[1 + n:1 + 2 * n], refs[1 + 2 * n:1 + 3 * n]
        outs = refs[1 + 3 * n:]
        off = 0
        for i, (_, used, padded) in enumerate(_SMALL_PARTS[:n]):
            g = g_ref[off:off + used, :]
            delta, nm, nv = _adamw_update(w_refs[i][...], g, m_refs[i][...], v_refs[i][...])
            outs[4 * i][...], outs[4 * i + 1][...], outs[4 * i + 2][...], outs[4 * i + 3][...] = g, delta, nm, nv
            off += padded

    outs = pl.pallas_call(
        body, out_shape=[jax.ShapeDtypeStruct(w.shape, F32) for w in ws for _ in range(4)],
        compiler_params=_params(), name="adamw_small")(g_packed, *ws, *ms, *vs)
    return [outs[4 * i:4 * i + 4] for i in range(n)]


def _place():
    x, y, c = lax.axis_index("x"), lax.axis_index("y"), lax.axis_index("c")
    chip = 2 * x + y
    peers = [(x, 1 - y), (1 - x, y), (1 - x, 1 - y)]
    peer_chip = [2 * px + py for px, py in peers]
    return x, y, c, chip, peers, peer_chip


def _remote(src, dst, send_sem, recv_sem, dev):
    return pltpu.make_async_remote_copy(src_ref=src, dst_ref=dst, send_sem=send_sem, recv_sem=recv_sem,
                                        device_id=dev, device_id_type=MESH)


_HBM = pl.BlockSpec(memory_space=pltpu.HBM)
_SEM = pl.BlockSpec(memory_space=pltpu.SEMAPHORE)
_ANY = pl.BlockSpec(memory_space=pl.ANY)
_DATAFLOW = pltpu.SideEffectType.DATAFLOW_SIDE_EFFECTING


def _in_hbm(a):
    return pltpu.with_memory_space_constraint(a, pltpu.HBM)


def _exchange_copies(gather, srcs, lands, send_sems, recv_sems):
    nw = len(srcs)
    x, y, c, chip, peers, peer_chip = _place()
    pairs = []
    for m, (px, py) in enumerate(peers):
        for w in range(nw):
            sems = (send_sems.at[nw * m + w], recv_sems.at[nw * m + w], (px, py, c))
            if gather:
                pairs.append((_remote(srcs[w], lands[w].at[chip], *sems),
                              _remote(srcs[w], lands[w].at[peer_chip[m]], *sems)))
            else:
                pairs.append((_remote(srcs[w].at[m], lands[w].at[m], *sems),) * 2)
    return pairs


def _exchange_start(gather, srcs, after, name, lands=None):
    nw = len(srcs)
    n_copies = 3 * nw

    def body(*refs):
        send_sems, recv_sems = refs[2 * nw + 1], refs[2 * nw + 2]
        for start, _ in _exchange_copies(gather, refs[:nw], refs[nw:2 * nw], send_sems, recv_sems):
            start.start()
        refs[-1][...] = jnp.zeros_like(refs[-1])

    if lands is None:
        lands = [lax.empty(((N_CHIPS,) + s.shape) if gather else s.shape, s.dtype) for s in srcs]
    lands = [_in_hbm(l) for l in lands]
    return pl.pallas_call(
        body, name=name,
        out_shape=(pltpu.SemaphoreType.DMA((n_copies,)), pltpu.SemaphoreType.DMA((n_copies,)))
        + tuple(pltpu.HBM(s.shape, s.dtype) for s in srcs)
        + tuple(pltpu.HBM(l.shape, l.dtype) for l in lands)
        + (jax.ShapeDtypeStruct((8, LANES), F32),),
        in_specs=[_HBM] * (2 * nw) + [_ANY],
        out_specs=(_SEM, _SEM) + (_HBM,) * (2 * nw) + (pl.BlockSpec(memory_space=pltpu.VMEM),),
        input_output_aliases={i: 2 + i for i in range(2 * nw)},
        compiler_params=pltpu.CompilerParams(has_side_effects=_DATAFLOW),
    )(*[_in_hbm(s) for s in srcs], *lands, after)


def _exchange_wait(gather, started, after, name):
    nw = (len(started) - 3) // 2
    send_sems, recv_sems = started[0], started[1]
    thru = started[2:2 + 2 * nw]

    def body(*refs):
        for _, arrival in _exchange_copies(gather, refs[:nw], refs[nw:2 * nw], refs[2 * nw], refs[2 * nw + 1]):
            arrival.wait_send()
            arrival.wait_recv()

    outs = pl.pallas_call(
        body, name=name,
        out_shape=tuple(pltpu.HBM(t.shape, t.dtype) for t in thru),
        in_specs=[_HBM] * (2 * nw) + [_SEM, _SEM, _ANY], out_specs=(_HBM,) * (2 * nw),
        input_output_aliases={i: i for i in range(2 * nw)},
        compiler_params=pltpu.CompilerParams(has_side_effects=_DATAFLOW),
    )(*thru, send_sems, recv_sems, after)
    return outs[nw:]


def _reduce_first(stacks):
    ns = len(stacks)
    halves = [s.shape[1] // 2 for s in stacks]
    row_block = 32

    def body(*refs):
        gs, sends, owns = refs[:ns], refs[ns:2 * ns], refs[2 * ns:3 * ns]
        ras, s_sem, r_sem = refs[3 * ns:4 * ns], refs[4 * ns], refs[4 * ns + 1]
        x, y, c, chip, peers, peer_chip = _place()
        swaps = []
        for w in range(ns):
            theirs = gs[w].at[:, pl.ds(pl.multiple_of((1 - c) * halves[w], 8), halves[w]), :]
            swaps.append(_remote(theirs, ras[w], s_sem.at[w], r_sem.at[w], (x, y, 1 - c)))
            swaps[-1].start()
        for w in range(ns):
            n = halves[w]
            swaps[w].wait_recv()

            def sums(i, carry, w=w, n=n):
                r0 = pl.multiple_of(i * row_block, row_block)
                blk = pl.ds(r0, row_block)
                mine = pl.ds(pl.multiple_of(c * n + r0, 8), row_block)
                for m in range(3):
                    sends[w][m, blk, :] = (gs[w][peer_chip[m], mine, :] + ras[w][peer_chip[m], blk, :]).astype(BF16)
                owns[w][blk, :] = gs[w][chip, mine, :] + ras[w][chip, blk, :]
                return carry
            lax.fori_loop(0, n // row_block, sums, 0)
        for cp in swaps:
            cp.wait_send()

    vmem = pl.BlockSpec(memory_space=pltpu.VMEM)
    outs = pl.pallas_call(
        body,
        out_shape=[jax.ShapeDtypeStruct((3, n, s.shape[2]), BF16) for n, s in zip(halves, stacks)]
        + [jax.ShapeDtypeStruct((n, s.shape[2]), F32) for n, s in zip(halves, stacks)],
        in_specs=[vmem] * ns, out_specs=[vmem] * (2 * ns),
        scratch_shapes=[pltpu.VMEM((N_CHIPS, n, s.shape[2]), F32) for n, s in zip(halves, stacks)]
        + [pltpu.SemaphoreType.DMA((ns,)), pltpu.SemaphoreType.DMA((ns,))],
        compiler_params=pltpu.CompilerParams(vmem_limit_bytes=VMEM_LIMIT), name="reduce_first")(*stacks)
    return outs[:ns], outs[ns:]


def _reduce_last(owns, landed, g_small):
    ns = len(owns)
    row_block = 32
    hs = SMALL_ROWS // 2

    def body(*refs):
        own_refs, land_refs, gsm_ref = refs[:ns], refs[ns:2 * ns], refs[2 * ns]
        out_refs, osm_ref = refs[2 * ns + 1:3 * ns + 1], refs[3 * ns + 1]
        ra_sm, p_sm, s_sem, r_sem, sm_s, sm_r = refs[3 * ns + 2:]
        x, y, c, chip, peers, peer_chip = _place()
        sib = (x, y, 1 - c)
        half = lambda cc: pl.ds(pl.multiple_of(cc * hs, 8), hs)
        sm_a = _remote(gsm_ref.at[half(1 - c), :], ra_sm, sm_s.at[0], sm_r.at[0], sib)
        sm_a.start()
        swaps = [sm_a]
        for w in range(ns):
            n = own_refs[w].shape[0]

            def total(i, carry, w=w, n=n):
                r0 = pl.multiple_of(i * row_block, row_block)
                blk = pl.ds(r0, row_block)
                acc = own_refs[w][blk, :]
                for m in range(3):
                    acc = acc + land_refs[w][m, blk, :].astype(F32)
                out_refs[w][pl.ds(pl.multiple_of(c * n + r0, 8), row_block), :] = acc
                return carry
            lax.fori_loop(0, n // row_block, total, 0)
            mine = out_refs[w].at[pl.ds(pl.multiple_of(c * n, 8), n), :]
            swaps.append(_remote(mine, mine, s_sem.at[w], r_sem.at[w], sib))
            swaps[-1].start()
        sm_a.wait_recv()
        p_sm[chip] = gsm_ref[half(c), :] + ra_sm[...]
        for m, (px, py) in enumerate(peers):
            swaps.append(_remote(p_sm.at[chip], p_sm.at[chip], sm_s.at[1 + m], sm_r.at[1 + m], (px, py, c)))
            swaps[-1].start()
        for w in range(ns):
            n = own_refs[w].shape[0]
            theirs = out_refs[w].at[pl.ds(pl.multiple_of((1 - c) * n, 8), n), :]
            _remote(theirs, theirs, s_sem.at[w], r_sem.at[w], sib).wait_recv()
        for m, (px, py) in enumerate(peers):
            _remote(p_sm.at[chip], p_sm.at[peer_chip[m]], sm_s.at[1 + m], sm_r.at[1 + m], (px, py, c)).wait_recv()
        osm_ref[half(c), :] = (p_sm[0] + p_sm[1]) + (p_sm[2] + p_sm[3])
        swaps.append(_remote(osm_ref.at[half(c), :], osm_ref.at[half(c), :], sm_s.at[4], sm_r.at[4], sib))
        swaps[-1].start()
        _remote(osm_ref.at[half(1 - c), :], osm_ref.at[half(1 - c), :], sm_s.at[4], sm_r.at[4], sib).wait_recv()
        for cp in swaps:
            cp.wait_send()

    vmem = pl.BlockSpec(memory_space=pltpu.VMEM)
    return pl.pallas_call(
        body, out_shape=[jax.ShapeDtypeStruct((2 * o.shape[0], o.shape[1]), F32) for o in owns]
        + [jax.ShapeDtypeStruct((SMALL_ROWS, LANES), F32)],
        in_specs=[vmem] * (2 * ns + 1), out_specs=[vmem] * (ns + 1),
        scratch_shapes=[pltpu.VMEM((hs, LANES), F32), pltpu.VMEM((N_CHIPS, hs, LANES), F32),
                        pltpu.SemaphoreType.DMA((ns,)), pltpu.SemaphoreType.DMA((ns,)),
                        pltpu.SemaphoreType.DMA((5,)), pltpu.SemaphoreType.DMA((5,))],
        compiler_params=pltpu.CompilerParams(vmem_limit_bytes=VMEM_LIMIT),
        name="reduce_last")(*owns, *landed, g_small)


_SMALL_PARTS = (("g_norm", 8, 8), ("w_s", 512, 512), ("b_s", 4, 8), ("g_v", 2, 8), ("g_mem", 8, 8),
                ("g_final", 8, 8), ("loss", 1, 8))
_LOSS_ROW = SMALL_ROWS - 8
assert sum(p for _, _, p in _SMALL_PARTS) == SMALL_ROWS


def _pack_small(parts, loss=None):
    loss_row = jnp.zeros((1, LANES), F32) if loss is None else jnp.broadcast_to(loss.reshape(1, 1), (1, LANES))
    rows = []
    for (name, used, padded), p in zip(_SMALL_PARTS, list(parts) + [loss_row]):
        p = p.reshape(used, LANES)
        if padded > used:
            p = jnp.pad(p, ((0, padded - used), (0, 0)))
        rows.append(p)
    return jnp.concatenate(rows, axis=0)


def _local_step(x, mem, target, g_norm, proj, w_in, w_s, b_s, g_v, g_mem, late_weights, g_final,
                fwd_token=None, on_dw=None):
    B, S, _ = x.shape
    x2d = x.reshape(B * S, D_MODEL)
    t2d = target.reshape(B * S, D_MODEL)
    mem2d = mem.reshape(B * N_MEM, D_MODEL)

    a, lse = _attn_fwd(proj, B, S, after=() if fwd_token is None else (fwd_token,))
    w_kv, w_out = late_weights(a)
    kv = _kv_fwd(mem2d, g_mem, w_kv)
    w_sT = jnp.swapaxes(w_s, 1, 2)
    b_tab = jnp.repeat(b_s.T, HEAD_DIM, axis=1)
    (dx2, da, drest, loss, d_wout, d_ws, d_bs, d_gv, d_gf, dkv) = _mid(
        x2d, t2d, a, proj, kv, w_s, w_sT, b_tab, g_v, w_out, g_final, B, S)
    d_wkv, d_gmem = _kv_bwd(mem2d, g_mem, w_kv, dkv)
    dq, dk, dv = _attn_bwd(proj, a, lse, da, B, S)
    d_win = _inproj_bwd_dw(dq, dk, dv, drest, x2d, g_norm)
    grad_x, d_gnorm = _inproj_bwd_dx(dq, dk, dv, drest, x2d, dx2, g_norm, w_in,
                                     after=() if on_dw is None else (on_dw(d_win, d_wkv, d_wout),))
    d_bs = d_bs[:, :N_SGU_GROUPS].T
    return (loss[0, 0], grad_x.reshape(B, S, D_MODEL),
            dict(g_norm=d_gnorm, w_in=d_win, w_s=d_ws, b_s=d_bs, g_v=d_gv, g_mem=d_gmem, w_kv=d_wkv,
                 w_out=d_wout, g_final=d_gf))


def kernel(x, mem, g_norm, w_in, w_sgu_spatial, b_sgu_spatial, g_sgu_v, g_mem, w_mem_kv, w_out, g_final, loss_target, m_g_norm, m_w_in, m_w_sgu_spatial, m_b_sgu_spatial, m_g_sgu_v, m_g_mem, m_w_mem_kv, m_w_out, m_g_final, v_g_norm, v_w_in, v_w_sgu_spatial, v_b_sgu_spatial, v_g_sgu_v, v_g_mem, v_w_mem_kv, v_w_out, v_g_final):
    t = lambda w: jnp.swapaxes(w[0], 0, 1)
    batch, seq, _ = x.shape
    proj, w_in_full, late_shards, late_lands = _inproj_gather(
        x.reshape(batch * seq, D_MODEL), g_norm, t(w_in), [w_mem_kv[0], w_out[0]])
    late = _exchange_start(True, list(late_shards), proj, "gather_late_start", lands=late_lands)

    def late_weights(a):
        return [z.reshape(-1, z.shape[-1]) for z in _exchange_wait(True, late, a, "gather_late_wait")]

    scatter = {}

    def on_dw(d_win, d_wkv, d_wout):
        stacks = [d_win.reshape((N_CHIPS, w_in.shape[2], w_in.shape[1])),
                  d_wkv.reshape((N_CHIPS,) + w_mem_kv.shape[1:]), d_wout.reshape((N_CHIPS,) + w_out.shape[1:])]
        sends, scatter["own"] = _reduce_first(stacks)
        scatter["started"] = _exchange_start(False, list(sends), scatter["own"][0], "scatter_start")
        return scatter["started"][-1]

    loss, grad_x, g = _local_step(
        x, mem, loss_target, g_norm, proj, w_in_full, w_sgu_spatial[0], b_sgu_spatial[0], g_sgu_v, g_mem,
        late_weights, g_final.reshape(1, D_MODEL), fwd_token=late[-1], on_dw=on_dw)

    small_names = ("g_norm", "w_s", "b_s", "g_v", "g_mem", "g_final")
    g_small = _pack_small([g[n] for n in small_names], loss)
    landed = _exchange_wait(False, scatter["started"], g_small, "scatter_wait")
    gr_in, gr_kv, gr_out, gr_small = _reduce_last(scatter["own"], landed, g_small)
    loss = gr_small[_LOSS_ROW, 0]

    small_w = (g_norm, w_sgu_spatial, b_sgu_spatial, g_sgu_v, g_mem, g_final)
    small_m = (m_g_norm, m_w_sgu_spatial, m_b_sgu_spatial, m_g_sgu_v, m_g_mem, m_g_final)
    small_v = (v_g_norm, v_w_sgu_spatial, v_b_sgu_spatial, v_g_sgu_v, v_g_mem, v_g_final)
    rows = lambda ws: [w.reshape(-1, LANES) for w in ws]
    small = [[z.reshape(w.shape) for z in four]
             for w, four in zip(small_w, _adamw_small(gr_small, rows(small_w), rows(small_m), rows(small_v)))]
    d_in, nm_in, nv_in = _adamw(t(w_in), gr_in, t(m_w_in), t(v_w_in), "adamw_w_in")
    gr_in, d_in, nm_in, nv_in = [jnp.swapaxes(z, 0, 1) for z in (gr_in, d_in, nm_in, nv_in)]
    d_kv, nm_kv, nv_kv = _adamw(w_mem_kv[0], gr_kv, m_w_mem_kv[0], v_w_mem_kv[0], "adamw_w_kv")
    d_out, nm_out, nv_out = _adamw(w_out[0], gr_out, m_w_out[0], v_w_out[0], "adamw_w_out")

    def leaves(kind, big_in, big_kv, big_out):
        s_norm, s_ws, s_bs, s_gv, s_gmem, s_gf = [four[kind] for four in small]
        return [s_norm, big_in[None], s_ws, s_bs, s_gv, s_gmem, big_kv[None], big_out[None], s_gf]

    return (loss, grad_x, *leaves(0, gr_in, gr_kv, gr_out), *leaves(1, d_in, d_kv, d_out),
            *leaves(2, nm_in, nm_kv, nm_out), *leaves(3, nv_in, nv_kv, nv_out))
```

```python
import functools

import jax
import jax.numpy as jnp
from jax import lax
from jax.experimental import pallas as pl
from jax.experimental.pallas import tpu as pltpu

F32 = jnp.float32
BF16 = jnp.bfloat16
MESH = pl.DeviceIdType.MESH

D_MODEL = 1024
ATTN_WIDTH = 512
SGU_WIDTH = 256
MEM_WIDTH = 256
N_MEM = 256
IN_COLS = 3328
QKV_COLS = 3 * ATTN_WIDTH
REST_COLS = IN_COLS - QKV_COLS
SGU_CHUNK = 128
N_SGU_GROUPS = 4
EPS = 1e-6
NEG_INF = -1e30
DILATIONS = (1, 4, 16)
RADIUS = 64
Q_BLOCK = 128
LANES = 128
HEAD_DIM = 64

ADAM_LR = 0.001
ADAM_B1 = 0.9
ADAM_B2 = 0.999
ADAM_EPS = 1e-08
ADAM_WD = 0.01
ADAM_STEP = 10

N_CHIPS = 4
VMEM_LIMIT = 56 * 1024 * 1024
SMALL_ROWS = 560


def _params(sem=None, vmem=VMEM_LIMIT):
    return pltpu.CompilerParams(dimension_semantics=sem, vmem_limit_bytes=vmem)


def _nn(a, b):
    return jnp.dot(a, b, preferred_element_type=F32)


def _nt(a, b):
    return lax.dot_general(a, b, (((1,), (1,)), ((), ())), preferred_element_type=F32)


def _tn(a, b):
    return lax.dot_general(a, b, (((0,), (0,)), ((), ())), preferred_element_type=F32)


def _rms(x):
    r = lax.rsqrt(jnp.mean(x * x, axis=-1, keepdims=True) + EPS)
    return r, x * r


def _head_masks():
    lane = lax.broadcasted_iota(jnp.int32, (1, LANES), 1)
    lo = lane < HEAD_DIM
    return lo, (lo.astype(F32), (~lo).astype(F32))


def _silu_parts(z):
    s = jax.nn.sigmoid(z)
    return z * s, s * (1.0 + z * (1.0 - s))


def _gelu_parts(x):
    c = 0.7978845608028654
    x2 = x * x
    t = jnp.tanh(c * (x + 0.044715 * (x * x2)))
    val = 0.5 * x * (1.0 + t)
    grad = 0.5 * (1.0 + t) + 0.5 * x * (1.0 - t * t) * (c * (1.0 + 3.0 * 0.044715 * x2))
    return val, grad


def _after(tokens):
    return [pl.BlockSpec(memory_space=pl.ANY)] * len(tokens)


SHARD_ROWS = IN_COLS // N_CHIPS
INTERIOR = 768


def _inproj_gather(x2d, g_norm, w_shard, late):
    T = x2d.shape[0]
    tm = 512
    nt = T // tm
    nl = len(late)
    half = SHARD_ROWS // 2
    straddles = (INTERIOR, 2 * SHARD_ROWS + INTERIOR)

    def body(x_ref, g_ref, w_ref, *rest):
        late_srcs = rest[:nl]
        proj_hbm, wt_hbm = rest[nl], rest[nl + 1]
        late_bf, late_land = rest[nl + 2:2 * nl + 2], rest[2 * nl + 2:3 * nl + 2]
        wt, h_scr, stage, edge, s_ici, r_ici, s_d2d, r_d2d, out_sem, edge_sem, wt_sem = rest[3 * nl + 2:]
        p, i = pl.program_id(0), pl.program_id(1)
        x, y, c = lax.axis_index("x"), lax.axis_index("y"), lax.axis_index("c")
        chip = 2 * x + y
        sib = (x, y, 1 - c)
        first = ((x + 1 - c) % 2, (y + c) % 2)
        second = ((x + c) % 2, (y + 1 - c) % 2)
        first_chip, second_chip = 2 * first[0] + first[1], 2 * second[0] + second[1]
        diag_chip = 3 - chip

        def rows_of(k, cc):
            return wt_hbm.at[pl.ds(pl.multiple_of(k * SHARD_ROWS + cc * half, 16), half), :]

        def ici(slot, k, dev):
            return _remote(rows_of(k, c), rows_of(k, c), s_ici.at[slot], r_ici.at[slot], (dev[0], dev[1], c))

        def d2d(slot, k, cc):
            return _remote(rows_of(k, cc), rows_of(k, cc), s_d2d.at[slot], r_d2d.at[slot], sib)

        def local(k, to_vmem, sem):
            rows = pl.ds(pl.multiple_of(k * SHARD_ROWS, 16), SHARD_ROWS)
            src, dst = (wt_hbm, wt) if to_vmem else (wt, wt_hbm)
            return pltpu.make_async_copy(src.at[rows, :], dst.at[rows, :], sem)

        at_start = lambda phase: jnp.logical_and(p == phase, i == 0)

        @pl.when(at_start(0))
        def _():
            wt[pl.ds(pl.multiple_of(chip * SHARD_ROWS, 16), SHARD_ROWS), :] = w_ref[...].astype(BF16)
            own = local(chip, False, wt_sem.at[0])
            own.start()
            own.wait()
            ici(0, chip, first).start()
            ici(1, chip, second).start()
            for src, bf, land in zip(late_srcs, late_bf, late_land):
                bf[...] = src[...].astype(BF16)
                land[...] = jnp.zeros_like(land)
                land[chip] = bf[...]

        @pl.when(at_start(1))
        def _():
            ici(0, first_chip, first).wait_recv()
            ici(2, first_chip, second).start()
            d2d(0, first_chip, c).start()
            ici(1, second_chip, second).wait_recv()
            d2d(1, second_chip, c).start()
            d2d(0, second_chip, 1 - c).wait_recv()
            d2d(1, first_chip, 1 - c).wait_recv()
            reads = [local(first_chip, True, wt_sem.at[0]), local(second_chip, True, wt_sem.at[1])]
            for cp in reads:
                cp.start()
            for cp in reads:
                cp.wait()

        @pl.when(at_start(3))
        def _():
            ici(2, diag_chip, second).wait_recv()
            d2d(2, diag_chip, c).start()
            d2d(2, diag_chip, 1 - c).wait_recv()
            read = local(diag_chip, True, wt_sem.at[0])
            read.start()
            read.wait()

        tile_rows = pl.ds(pl.multiple_of(i * tm, tm), tm)
        step = p * nt + i
        slot = step % 2

        def out_copy(sl, col0):
            return pltpu.make_async_copy(stage.at[sl], proj_hbm.at[tile_rows, pl.ds(pl.multiple_of(col0, LANES), INTERIOR)],
                                         out_sem.at[sl])

        def edge_copy(e):
            return pltpu.make_async_copy(edge.at[e], proj_hbm.at[tile_rows, pl.ds(straddles[e], LANES)],
                                         edge_sem.at[e])

        @pl.when(p == 0)
        def _():
            _, xh = _rms(x_ref[...])
            h_scr[tile_rows, :] = (xh * g_ref[...]).astype(BF16)

        @pl.when(p < N_CHIPS)
        def _():
            k = jnp.where(p == 0, chip, jnp.where(p == 1, first_chip, jnp.where(p == 2, second_chip, diag_chip)))
            col0 = k * SHARD_ROWS + (k % 2) * (LANES // 2)

            @pl.when(step >= 2)
            def _():
                out_copy(slot, col0).wait()
            stage[slot] = _nt(h_scr[tile_rows, :], wt[pl.ds(pl.multiple_of(col0, LANES), INTERIOR), :])
            out_copy(slot, col0).start()

        @pl.when(p == N_CHIPS)
        def _():
            @pl.when(i == 0)
            def _():
                for sl in (0, 1):
                    out_copy(sl, 0).wait()

            @pl.when(i > 0)
            def _():
                for e in (0, 1):
                    edge_copy(e).wait()
            for e in (0, 1):
                edge[e] = _nt(h_scr[tile_rows, :], wt[straddles[e]:straddles[e] + LANES, :])
                edge_copy(e).start()

            @pl.when(i == nt - 1)
            def _():
                for e in (0, 1):
                    edge_copy(e).wait()
                for slot_ in (0, 1, 2):
                    ici(slot_, chip, first).wait_send()
                    d2d(slot_, chip, c).wait_send()

    vmem = pl.BlockSpec(memory_space=pltpu.VMEM)
    hbm = pl.BlockSpec(memory_space=pl.ANY)
    outs = pl.pallas_call(
        body, grid=(N_CHIPS + 1, nt),
        in_specs=[pl.BlockSpec((tm, D_MODEL), lambda p, i: (jnp.where(p == 0, i, nt - 1), 0)),
                  pl.BlockSpec((1, D_MODEL), lambda p, i: (0, 0)), vmem] + [vmem] * nl,
        out_specs=[hbm, hbm] + [vmem] * (2 * nl),
        out_shape=[jax.ShapeDtypeStruct((T, IN_COLS), F32), jax.ShapeDtypeStruct((IN_COLS, D_MODEL), BF16)]
        + [jax.ShapeDtypeStruct(w.shape, BF16) for w in late]
        + [jax.ShapeDtypeStruct((N_CHIPS,) + w.shape, BF16) for w in late],
        scratch_shapes=[pltpu.VMEM((IN_COLS, D_MODEL), BF16), pltpu.VMEM((T, D_MODEL), BF16),
                        pltpu.VMEM((2, tm, INTERIOR), F32), pltpu.VMEM((2, tm, LANES), F32),
                        pltpu.SemaphoreType.DMA((3,)), pltpu.SemaphoreType.DMA((3,)),
                        pltpu.SemaphoreType.DMA((3,)), pltpu.SemaphoreType.DMA((3,)),
                        pltpu.SemaphoreType.DMA((2,)), pltpu.SemaphoreType.DMA((2,)), pltpu.SemaphoreType.DMA((2,))],
        compiler_params=_params(("arbitrary", "arbitrary")), name="inproj_gather")(x2d, g_norm, w_shard, *late)
    return outs[0], outs[1], outs[2:2 + nl], outs[2 + nl:]


def _kv_fwd(mem2d, g_mem, w_kv):
    Tm = mem2d.shape[0]

    def body(m_ref, g_ref, w_ref, o_ref):
        _, mh = _rms(m_ref[...])
        o_ref[...] = _nn((mh * g_ref[...]).astype(BF16), w_ref[...])

    return pl.pallas_call(
        body, out_shape=jax.ShapeDtypeStruct((Tm, 2 * MEM_WIDTH), F32),
        compiler_params=_params(), name="kv_fwd")(mem2d, g_mem, w_kv)


def _kv_bwd(mem2d, g_mem, w_kv, dkv):
    Tm = mem2d.shape[0]

    def body(m_ref, g_ref, w_ref, dkv_ref, dw_ref, dg_ref):
        _, mh = _rms(m_ref[...])
        memn = (mh * g_ref[...]).astype(BF16)
        dkvb = dkv_ref[...].astype(BF16)
        dw_ref[...] = _tn(memn, dkvb)
        dmemn = _nt(dkvb, w_ref[...])
        dg_ref[...] = jnp.sum(dmemn * mh, axis=0, keepdims=True)

    return pl.pallas_call(
        body, out_shape=(jax.ShapeDtypeStruct((D_MODEL, 2 * MEM_WIDTH), F32),
                         jax.ShapeDtypeStruct((1, D_MODEL), F32)),
        compiler_params=_params(), name="kv_bwd")(mem2d, g_mem, w_kv, dkv)


def _attn_geometry(S):
    geom = []
    for d in DILATIONS:
        L = S // d
        assert L % Q_BLOCK == 0
        geom.append((d, L, min(2 * Q_BLOCK, L), L // Q_BLOCK))
    return geom


def _init_bias(bias_scr, geom, hp):
    row = lax.broadcasted_iota(jnp.int32, (Q_BLOCK, 2 * Q_BLOCK), 0)
    col = lax.broadcasted_iota(jnp.int32, (Q_BLOCK, 2 * Q_BLOCK), 1)
    for j in (0, 1):
        bits = (126 - (2 * hp + j)) * (1 << 23)
        slope = lax.bitcast_convert_type(jnp.full((1, 1), bits, jnp.int32), F32)
        for di, (d, _, _, _) in enumerate(geom):
            for cls, off in enumerate((0, -RADIUS, -2 * RADIUS)):
                dist = jnp.abs(col - row + off)
                bias_scr[di * 6 + cls * 2 + j] = jnp.where(
                    dist <= RADIUS, -(slope * float(d)) * dist.astype(F32), NEG_INF)


def _block_slices(d, L, KW, nqb, r, qb):
    qs = qb * Q_BLOCK
    ks = jnp.clip(qs - RADIUS, 0, L - KW)
    cls = jnp.where(qb == 0, 0, jnp.where(qb == nqb - 1, 2, 1))
    if d == 1:
        qsl = pl.ds(pl.multiple_of(qs, Q_BLOCK), Q_BLOCK)
        ksl = pl.ds(pl.multiple_of(ks, RADIUS), KW)
    else:
        qsl = pl.ds(r + qs * d, Q_BLOCK, stride=d)
        ksl = pl.ds(r + ks * d, KW, stride=d)
    return qsl, ksl, cls


def _for_groups(geom, group, fn):
    for di, (d, L, KW, nqb) in enumerate(geom):
        assert (d * nqb) % group == 0

        def step(it, carry, di=di, d=d, L=L, KW=KW, nqb=nqb):
            slices = []
            for g in range(group):
                i = it * group + g
                slices.append(_block_slices(d, L, KW, nqb, i // nqb, i % nqb))
            fn(di, KW, slices)
            return carry
        lax.fori_loop(0, d * nqb // group, step, 0)


def _attn_fwd(proj, B, S, after=()):
    T = B * S
    geom = _attn_geometry(S)
    n_pairs = ATTN_WIDTH // LANES

    def body(q_ref, k_ref, v_ref, *rest):
        a_ref, lse_ref, bias_scr = rest[len(after):len(after) + 3]
        per_dilation = rest[len(after) + 3:]
        o_scr, m_scr, l_scr = per_dilation[0:3], per_dilation[3:6], per_dilation[6:9]
        lo, hm = _head_masks()
        _init_bias(bias_scr, geom, pl.program_id(1))

        def group(di, KW, slices):
            chains = [(g, j) for g in range(len(slices)) for j in (0, 1)]
            q = [q_ref[qsl, :] for qsl, _, _ in slices]
            kw = [k_ref[ksl, :].astype(BF16) for _, ksl, _ in slices]
            vw = [v_ref[ksl, :].astype(BF16) for _, ksl, _ in slices]
            s = {(g, j): _nt((q[g] * (hm[j] * 0.125)).astype(BF16), kw[g])
                 + bias_scr[di * 6 + slices[g][2] * 2 + j, :, pl.ds(0, KW)] for g, j in chains}
            m = {c: jnp.max(s[c], axis=1, keepdims=True) for c in chains}
            p = {c: jnp.exp(s[c] - m[c]) for c in chains}
            l = {c: jnp.sum(p[c], axis=1, keepdims=True) for c in chains}
            o = {(g, j): _nn(p[(g, j)].astype(BF16), vw[g]) for g, j in chains}
            for g, (qsl, _, _) in enumerate(slices):
                o_scr[di][qsl, :] = jnp.where(lo, o[(g, 0)], o[(g, 1)])
                m_scr[di][qsl, :] = jnp.where(lo, m[(g, 0)], m[(g, 1)])
                l_scr[di][qsl, :] = jnp.where(lo, l[(g, 0)], l[(g, 1)])

        _for_groups(geom, 8, group)

        rows_per = 256

        def combine(i, carry):
            rows = pl.ds(pl.multiple_of(i * rows_per, rows_per), rows_per)
            ms = [m_scr[di][rows, :] for di in range(3)]
            mx = jnp.maximum(jnp.maximum(ms[0], ms[1]), ms[2])
            num = 0.0
            den = 0.0
            for di in range(3):
                w = jnp.exp(ms[di] - mx)
                num = num + w * o_scr[di][rows, :]
                den = den + w * l_scr[di][rows, :]
            a_ref[rows, :] = num / den
            lse_ref[rows, :] = mx + jnp.log(den)
            return carry

        lax.fori_loop(0, S // rows_per, combine, 0)

    blk = lambda off: pl.BlockSpec((S, LANES), lambda b, h, off=off: (b, off + h))
    out_blk = pl.BlockSpec((S, LANES), lambda b, h: (b, h))
    return pl.pallas_call(
        body, grid=(B, n_pairs),
        in_specs=[blk(0), blk(n_pairs), blk(2 * n_pairs)] + _after(after),
        out_specs=[out_blk, out_blk],
        out_shape=[jax.ShapeDtypeStruct((T, ATTN_WIDTH), F32)] * 2,
        scratch_shapes=[pltpu.VMEM((18, Q_BLOCK, 2 * Q_BLOCK), F32)] + [pltpu.VMEM((S, LANES), F32)] * 9,
        compiler_params=_params(("arbitrary", "arbitrary")), name="attn_fwd")(proj, proj, proj, *after)


def _attn_bwd(proj, a, lse, da, B, S):
    T = B * S
    geom = _attn_geometry(S)
    n_pairs = ATTN_WIDTH // LANES

    def body(q_ref, k_ref, v_ref, a_ref, lse_ref, do_ref, dq_ref, dk_ref, dv_ref,
             bias_scr, dq_scr, dk_scr, dv_scr):
        _, hm = _head_masks()
        _init_bias(bias_scr, geom, pl.program_id(1))
        dq_scr[...] = jnp.zeros_like(dq_scr)
        dk_scr[...] = jnp.zeros_like(dk_scr)
        dv_scr[...] = jnp.zeros_like(dv_scr)

        def group(di, KW, slices):
            n = len(slices)
            chains = [(g, j) for g in range(n) for j in (0, 1)]
            q = [q_ref[qsl, :] for qsl, _, _ in slices]
            do = [do_ref[qsl, :] for qsl, _, _ in slices]
            doa = [do[g] * a_ref[slices[g][0], :] for g in range(n)]
            lse_q = [lse_ref[qsl, :] for qsl, _, _ in slices]
            kw = [k_ref[ksl, :].astype(BF16) for _, ksl, _ in slices]
            vw = [v_ref[ksl, :].astype(BF16) for _, ksl, _ in slices]
            qj = {(g, j): (q[g] * (hm[j] * 0.125)).astype(BF16) for g, j in chains}
            doj = {(g, j): (do[g] * hm[j]).astype(BF16) for g, j in chains}
            s = {(g, j): _nt(qj[(g, j)], kw[g])
                 + bias_scr[di * 6 + slices[g][2] * 2 + j, :, pl.ds(0, KW)] for g, j in chains}
            dp = {(g, j): _nt(doj[(g, j)], vw[g]) for g, j in chains}
            dsum = {(g, j): jnp.sum(doa[g] * hm[j], axis=1, keepdims=True) for g, j in chains}
            p = {(g, j): jnp.exp(s[(g, j)] - lse_q[g][:, HEAD_DIM * j:HEAD_DIM * j + 1]) for g, j in chains}
            ds = {c: (p[c] * (dp[c] - dsum[c])).astype(BF16) for c in chains}
            pb = {c: p[c].astype(BF16) for c in chains}
            dq = [_nn(ds[(g, 0)], kw[g]) * (hm[0] * 0.125) + _nn(ds[(g, 1)], kw[g]) * (hm[1] * 0.125)
                  for g in range(n)]
            both = lambda t, g: jnp.concatenate([t[(g, 0)], t[(g, 1)]], axis=0)
            dkw = [_tn(both(ds, g), both(qj, g)) for g in range(n)]
            dvw = [_tn(both(pb, g), both(doj, g)) for g in range(n)]
            for g, (qsl, ksl, _) in enumerate(slices):
                dq_scr[qsl, :] = dq_scr[qsl, :] + dq[g]
                dk_scr[ksl, :] = dk_scr[ksl, :] + dkw[g]
                dv_scr[ksl, :] = dv_scr[ksl, :] + dvw[g]

        _for_groups(geom, 4, group)
        dq_ref[...] = dq_scr[...].astype(BF16)
        dk_ref[...] = dk_scr[...].astype(BF16)
        dv_ref[...] = dv_scr[...].astype(BF16)

    blk = lambda off: pl.BlockSpec((S, LANES), lambda b, h, off=off: (b, off + h))
    return pl.pallas_call(
        body, grid=(B, n_pairs),
        in_specs=[blk(0), blk(n_pairs), blk(2 * n_pairs), blk(0), blk(0), blk(0)],
        out_specs=[blk(0), blk(0), blk(0)],
        out_shape=[jax.ShapeDtypeStruct((T, ATTN_WIDTH), BF16)] * 3,
        scratch_shapes=[pltpu.VMEM((18, Q_BLOCK, 2 * Q_BLOCK), F32),
                        pltpu.VMEM((S, LANES), F32),
                        pltpu.VMEM((S, LANES), F32),
                        pltpu.VMEM((S, LANES), F32)],
        compiler_params=_params(("arbitrary", "arbitrary")), name="attn_bwd")(proj, proj, proj, a, lse, da)


def _mid(x2d, t2d, a, proj, kv, w_s, w_sT, b_tab, g_v, w_out, g_final, B, S):
    T = B * S
    tm = 512
    nt = S // tm
    halves = 2
    hrows = tm // halves

    def body(x_ref, t_ref, a_ref, za_ref, ub_ref, vb_ref, zb_ref, qm_ref, zm_ref, kv_ref,
              ws_ref, wsT_ref, btab_ref, gv_ref, wout_ref, gf_ref,
              dx2_ref, da_ref, drest_ref, loss_ref, dwout_ref, dws_ref, dbs_ref, dgv_ref, dgf_ref, dkv_ref,
              dbtab_scr):
        b = pl.program_id(0)
        t = pl.program_id(1)
        first = jnp.logical_and(b == 0, t == 0)
        last = jnp.logical_and(b == B - 1, t == nt - 1)
        _, hm = _head_masks()
        lane_g = lax.broadcasted_iota(jnp.int32, (1, SGU_WIDTH), 1) // HEAD_DIM
        gm = [(lane_g == g).astype(F32) for g in range(N_SGU_GROUPS)]
        H = range(halves)
        rows = [pl.ds(h * hrows, hrows) for h in H]
        ld = lambda ref: [ref[r, :] for r in rows]
        cat = lambda parts, axis: jnp.concatenate(parts, axis=axis)
        chunks = [slice(ci * SGU_CHUNK, (ci + 1) * SGU_CHUNK) for ci in range(hrows // SGU_CHUNK)]
        pairs = [slice(pr * LANES, (pr + 1) * LANES) for pr in range(2)]
        heads = [(pr, j) for pr in range(2) for j in (0, 1)]

        @pl.when(first)
        def _():
            loss_ref[...] = jnp.zeros_like(loss_ref)
            dwout_ref[...] = jnp.zeros_like(dwout_ref)
            dws_ref[...] = jnp.zeros_like(dws_ref)
            dbs_ref[...] = jnp.zeros_like(dbs_ref)
            dgv_ref[...] = jnp.zeros_like(dgv_ref)
            dgf_ref[...] = jnp.zeros_like(dgf_ref)
            dbtab_scr[...] = jnp.zeros_like(dbtab_scr)

        @pl.when(t == 0)
        def _():
            dkv_ref[...] = jnp.zeros_like(dkv_ref)

        a_val = ld(a_ref)
        sil_a = [_silu_parts(z) for z in ld(za_ref)]
        gated_a = [s[0] * a for s, a in zip(sil_a, a_val)]
        u = [_gelu_parts(z) for z in ld(ub_ref)]
        vv = [_gelu_parts(z) for z in ld(vb_ref)]
        vnorm = [_rms(v[0]) for v in vv]
        gv = gv_ref[...]
        vn = [(n[1] * gv).astype(BF16) for n in vnorm]
        w_cat = cat([ws_ref[g].astype(BF16) for g in range(N_SGU_GROUPS)], 1)
        wT_cat = cat([wsT_ref[g].astype(BF16) for g in range(N_SGU_GROUPS)], 1)
        gmb = [m.astype(BF16) for m in gm]
        by_group = lambda chunk: cat([chunk * gmb[g] for g in range(N_SGU_GROUPS)], 0)
        btab = btab_ref[...]
        mixed = [cat([btab + _nn(w_cat, by_group(vn[h][c, :])) for c in chunks], 0) for h in H]
        sg = [u[h][0] * mixed[h] for h in H]
        sil_b = [_silu_parts(z) for z in ld(zb_ref)]
        gated_b = [sil_b[h][0] * sg[h] for h in H]

        kvv = kv_ref[...].astype(BF16)
        kp = [kvv[:, p] for p in pairs]
        vp = [kvv[:, MEM_WIDTH + pr * LANES:MEM_WIDTH + (pr + 1) * LANES] for pr in range(2)]
        qm = ld(qm_ref)
        qj = {(h, pr, j): (qm[h][:, pairs[pr]] * (hm[j] * 0.125)).astype(BF16) for h in H for pr, j in heads}
        sc = {k: _nt(qj[k], kp[k[1]]) for k in qj}
        ex = {k: jnp.exp(sc[k] - jnp.max(sc[k], axis=1, keepdims=True)) for k in qj}
        prob = {k: ex[k] * (1.0 / jnp.sum(ex[k], axis=1, keepdims=True)) for k in qj}
        probb = {k: prob[k].astype(BF16) for k in qj}
        mo = [cat([sum(_nn(probb[(h, pr, j)], vp[pr]) * hm[j] for j in (0, 1)) for pr in range(2)], 1) for h in H]
        sil_m = [_silu_parts(z) for z in ld(zm_ref)]
        gated_m = [sil_m[h][0] * mo[h] for h in H]

        gated = [cat([gated_a[h], gated_b[h], gated_m[h]], 1).astype(BF16) for h in H]
        wout = wout_ref[...]
        x_in = ld(x_ref)
        x2 = [x_in[h] + _nn(gated[h], wout) for h in H]
        fin = [_rms(z) for z in x2]
        gf = gf_ref[...]
        tgt = ld(t_ref)
        err = [fin[h][1] * gf - tgt[h] for h in H]
        loss_ref[...] += sum(jnp.sum(e * e) for e in err) * (0.5 / D_MODEL)

        dy = [e * (1.0 / D_MODEL) for e in err]
        dgf_ref[...] += sum(jnp.sum(dy[h] * fin[h][1], axis=0, keepdims=True) for h in H)
        gdy = [d * gf for d in dy]
        dx2 = [fin[h][0] * (gdy[h] - fin[h][1] * jnp.mean(gdy[h] * fin[h][1], axis=1, keepdims=True)) for h in H]
        for h in H:
            dx2_ref[rows[h], :] = dx2[h]
        dx2b = [d.astype(BF16) for d in dx2]
        dgated = [_nt(d, wout) for d in dx2b]
        dwout_ref[...] += _tn(cat(gated, 0), cat(dx2b, 0))
        dga = [d[:, 0:ATTN_WIDTH] for d in dgated]
        dgb = [d[:, ATTN_WIDTH:ATTN_WIDTH + SGU_WIDTH] for d in dgated]
        dgm = [d[:, ATTN_WIDTH + SGU_WIDTH:] for d in dgated]

        for h in H:
            da_ref[rows[h], :] = dga[h] * sil_a[h][0]
        dza = [dga[h] * a_val[h] * sil_a[h][1] for h in H]

        dsg = [dgb[h] * sil_b[h][0] for h in H]
        dzb = [dgb[h] * sg[h] * sil_b[h][1] for h in H]
        dub = [dsg[h] * mixed[h] * u[h][1] for h in H]
        dmixed = [dsg[h] * u[h][0] for h in H]
        dmixed_b = [d.astype(BF16) for d in dmixed]
        dvn = [cat([_nn(wT_cat, by_group(dmixed_b[h][c, :])) for c in chunks], 0) for h in H]
        for g in range(N_SGU_GROUPS):
            dws_ref[g] += sum(_nt((dmixed[h][c, :] * gm[g]).astype(BF16), vn[h][c, :]) for h in H for c in chunks)
        dbtab_scr[...] += sum(dmixed[h][c, :] for h in H for c in chunks)
        dgv_ref[...] += sum(jnp.sum(dvn[h] * vnorm[h][1], axis=0, keepdims=True) for h in H)
        tv = [d * gv for d in dvn]
        dvv = [vnorm[h][0] * (tv[h] - vnorm[h][1] * jnp.mean(tv[h] * vnorm[h][1], axis=1, keepdims=True)) for h in H]
        dvb = [dvv[h] * vv[h][1] for h in H]

        dmo = [dgm[h] * sil_m[h][0] for h in H]
        dzm = [dgm[h] * mo[h] * sil_m[h][1] for h in H]
        dmoj = {(h, pr, j): (dmo[h][:, pairs[pr]] * hm[j]).astype(BF16) for h in H for pr, j in heads}
        dp = {k: _nt(dmoj[k], vp[k[1]]) for k in qj}
        ds = {k: (prob[k] * (dp[k] - jnp.sum(dp[k] * prob[k], axis=1, keepdims=True))).astype(BF16) for k in qj}
        dqm = [cat([sum(_nn(ds[(h, pr, j)], kp[pr]) * (hm[j] * 0.125) for j in (0, 1)) for pr in range(2)], 1)
               for h in H]
        every = lambda tbl, pr: cat([tbl[(h, pr, j)] for h in H for j in (0, 1)], 0)
        dk = [_tn(every(ds, pr), every(qj, pr)) for pr in range(2)]
        dv = [_tn(every(probb, pr), every(dmoj, pr)) for pr in range(2)]
        dkv_ref[...] += cat(dk + dv, 1)

        for h in H:
            drest_ref[rows[h], :] = cat([dza[h], dub[h], dvb[h], dzb[h], dqm[h], dzm[h]], 1).astype(BF16)

        @pl.when(last)
        def _():
            lane = lax.broadcasted_iota(jnp.int32, (1, LANES), 1)
            dbt = dbtab_scr[...]
            out = jnp.zeros((SGU_CHUNK, LANES), F32)
            for g in range(N_SGU_GROUPS):
                out = out + jnp.where(lane == g, jnp.sum(dbt * gm[g], axis=1, keepdims=True), 0.0)
            dbs_ref[...] = out

    tile = lambda w, cb: pl.BlockSpec((tm, w), lambda b, t, cb=cb: (b * nt + t, cb))
    const = lambda shape: pl.BlockSpec(shape, lambda b, t, n=len(shape): (0,) * n)
    return pl.pallas_call(
        body, grid=(B, nt),
        in_specs=[tile(D_MODEL, 0), tile(D_MODEL, 0), tile(ATTN_WIDTH, 0),
                  tile(ATTN_WIDTH, 3),
                  tile(SGU_WIDTH, 8), tile(SGU_WIDTH, 9), tile(SGU_WIDTH, 10),
                  tile(MEM_WIDTH, 11), tile(MEM_WIDTH, 12),
                  pl.BlockSpec((N_MEM, 2 * MEM_WIDTH), lambda b, t: (b, 0)),
                  const((N_SGU_GROUPS, SGU_CHUNK, SGU_CHUNK)), const((N_SGU_GROUPS, SGU_CHUNK, SGU_CHUNK)),
                  const((SGU_CHUNK, SGU_WIDTH)), const((1, SGU_WIDTH)),
                  const((D_MODEL, D_MODEL)), const((1, D_MODEL))],
        out_specs=[tile(D_MODEL, 0), tile(ATTN_WIDTH, 0), tile(REST_COLS, 0),
                   const((8, LANES)), const((D_MODEL, D_MODEL)),
                   const((N_SGU_GROUPS, SGU_CHUNK, SGU_CHUNK)), const((SGU_CHUNK, LANES)),
                   const((1, SGU_WIDTH)), const((1, D_MODEL)),
                   pl.BlockSpec((N_MEM, 2 * MEM_WIDTH), lambda b, t: (b, 0))],
        out_shape=[jax.ShapeDtypeStruct((T, D_MODEL), F32), jax.ShapeDtypeStruct((T, ATTN_WIDTH), F32),
                   jax.ShapeDtypeStruct((T, REST_COLS), BF16),
                   jax.ShapeDtypeStruct((8, LANES), F32), jax.ShapeDtypeStruct((D_MODEL, D_MODEL), F32),
                   jax.ShapeDtypeStruct((N_SGU_GROUPS, SGU_CHUNK, SGU_CHUNK), F32),
                   jax.ShapeDtypeStruct((SGU_CHUNK, LANES), F32),
                   jax.ShapeDtypeStruct((1, SGU_WIDTH), F32), jax.ShapeDtypeStruct((1, D_MODEL), F32),
                   jax.ShapeDtypeStruct((B * N_MEM, 2 * MEM_WIDTH), F32)],
        scratch_shapes=[pltpu.VMEM((SGU_CHUNK, SGU_WIDTH), F32)],
        compiler_params=_params(("arbitrary", "arbitrary")), name="mid")(
            x2d, t2d, a, proj, proj, proj, proj, proj, proj, kv, w_s, w_sT, b_tab, g_v, w_out, g_final)


def _inproj_bwd_dx(dq, dk, dv, drest, x2d, dx2, g_norm, w_in_t, after=()):
    T = x2d.shape[0]
    tm = 512
    W = ATTN_WIDTH

    def body(dq_ref, dk_ref, dv_ref, dr_ref, x_ref, dx2_ref, g_ref, w_ref, *rest):
        gx_ref, dg_ref = rest[-2:]

        @pl.when(pl.program_id(0) == 0)
        def _():
            dg_ref[...] = jnp.zeros_like(dg_ref)

        halves = [pl.ds(h * (tm // 2), tm // 2) for h in (0, 1)]
        dh = [(_nn(dq_ref[r, :], w_ref[0:W, :]) + _nn(dk_ref[r, :], w_ref[W:2 * W, :])
               + _nn(dv_ref[r, :], w_ref[2 * W:3 * W, :]) + _nn(dr_ref[r, :], w_ref[QKV_COLS:IN_COLS, :]))
              for r in halves]
        nrm = [_rms(x_ref[r, :]) for r in halves]
        dg_ref[...] += sum(jnp.sum(d * n[1], axis=0, keepdims=True) for d, n in zip(dh, nrm))
        g = g_ref[...]
        for r, d, (rstd, xh) in zip(halves, dh, nrm):
            th = d * g
            gx_ref[r, :] = rstd * (th - xh * jnp.mean(th * xh, axis=1, keepdims=True)) + dx2_ref[r, :]

    tile = lambda w: pl.BlockSpec((tm, w), lambda i: (i, 0))
    return pl.pallas_call(
        body, grid=(T // tm,),
        in_specs=[tile(W), tile(W), tile(W), tile(REST_COLS), tile(D_MODEL), tile(D_MODEL),
                  pl.BlockSpec((1, D_MODEL), lambda i: (0, 0)),
                  pl.BlockSpec((IN_COLS, D_MODEL), lambda i: (0, 0))] + _after(after),
        out_specs=[tile(D_MODEL), pl.BlockSpec((1, D_MODEL), lambda i: (0, 0))],
        out_shape=[jax.ShapeDtypeStruct((T, D_MODEL), F32), jax.ShapeDtypeStruct((1, D_MODEL), F32)],
        compiler_params=_params(("arbitrary",)), name="inproj_bwd_dx")(
            dq, dk, dv, drest, x2d, dx2, g_norm, w_in_t, *after)


def _inproj_bwd_dw(dq, dk, dv, drest, x2d, g_norm):
    T = x2d.shape[0]
    tm = 512
    W = ATTN_WIDTH

    def body(dq_ref, dk_ref, dv_ref, dr_ref, x_ref, g_ref, dw_ref):
        @pl.when(pl.program_id(0) == 0)
        def _():
            dw_ref[...] = jnp.zeros_like(dw_ref)

        _, xh = _rms(x_ref[...])
        h = (xh * g_ref[...]).astype(BF16)
        dw_ref[0:W, :] += _tn(dq_ref[...], h)
        dw_ref[W:2 * W, :] += _tn(dk_ref[...], h)
        dw_ref[2 * W:3 * W, :] += _tn(dv_ref[...], h)
        dw_ref[QKV_COLS:IN_COLS, :] += _tn(dr_ref[...], h)

    tile = lambda w: pl.BlockSpec((tm, w), lambda i: (i, 0))
    return pl.pallas_call(
        body, grid=(T // tm,),
        in_specs=[tile(W), tile(W), tile(W), tile(REST_COLS), tile(D_MODEL),
                  pl.BlockSpec((1, D_MODEL), lambda i: (0, 0))],
        out_specs=pl.BlockSpec((IN_COLS, D_MODEL), lambda i: (0, 0)),
        out_shape=jax.ShapeDtypeStruct((IN_COLS, D_MODEL), F32),
        compiler_params=_params(("arbitrary",)), name="inproj_bwd_dw")(dq, dk, dv, drest, x2d, g_norm)


def _adamw_update(w, g, m, v):
    nm = ADAM_B1 * m + (1.0 - ADAM_B1) * g
    nv = ADAM_B2 * v + (1.0 - ADAM_B2) * (g * g)
    m_hat = nm / (1.0 - ADAM_B1 ** ADAM_STEP)
    v_hat = nv / (1.0 - ADAM_B2 ** ADAM_STEP)
    return -ADAM_LR * (m_hat / (jnp.sqrt(v_hat) + ADAM_EPS) + ADAM_WD * w), nm, nv


def _adamw(w, g, m, v, name):
    R, C = w.shape
    br = max(r for r in range(8, 257, 8) if R % r == 0)

    def body(w_ref, g_ref, m_ref, v_ref, d_ref, nm_ref, nv_ref):
        d_ref[...], nm_ref[...], nv_ref[...] = _adamw_update(w_ref[...], g_ref[...], m_ref[...], v_ref[...])

    spec = pl.BlockSpec((br, C), lambda i: (i, 0))
    return pl.pallas_call(
        body, grid=(R // br,), in_specs=[spec] * 4, out_specs=[spec] * 3,
        out_shape=[jax.ShapeDtypeStruct((R, C), F32)] * 3,
        compiler_params=_params(("arbitrary",)), name=name)(w, g, m, v)


def _adamw_small(g_packed, ws, ms, vs):
    n = len(ws)

    def body(*refs):
        g_ref = refs[0]
        w_refs, m_refs, v_refs = refs[1:1 + n], refs[1 + n:1 + 2 * n], refs[1 + 2 * n:1 + 3 * n]
        outs = refs[1 + 3 * n:]
        off = 0
        for i, (_, used, padded) in enumerate(_SMALL_PARTS[:n]):
            g = g_ref[off:off + used, :]
            delta, nm, nv = _adamw_update(w_refs[i][...], g, m_refs[i][...], v_refs[i][...])
            outs[4 * i][...], outs[4 * i + 1][...], outs[4 * i + 2][...], outs[4 * i + 3][...] = g, delta, nm, nv
            off += padded

    outs = pl.pallas_call(
        body, out_shape=[jax.ShapeDtypeStruct(w.shape, F32) for w in ws for _ in range(4)],
        compiler_params=_params(), name="adamw_small")(g_packed, *ws, *ms, *vs)
    return [outs[4 * i:4 * i + 4] for i in range(n)]


def _place():
    x, y, c = lax.axis_index("x"), lax.axis_index("y"), lax.axis_index("c")
    chip = 2 * x + y
    peers = [(x, 1 - y), (1 - x, y), (1 - x, 1 - y)]
    peer_chip = [2 * px + py for px, py in peers]
    return x, y, c, chip, peers, peer_chip


def _remote(src, dst, send_sem, recv_sem, dev):
    return pltpu.make_async_remote_copy(src_ref=src, dst_ref=dst, send_sem=send_sem, recv_sem=recv_sem,
                                        device_id=dev, device_id_type=MESH)


_HBM = pl.BlockSpec(memory_space=pltpu.HBM)
_SEM = pl.BlockSpec(memory_space=pltpu.SEMAPHORE)
_ANY = pl.BlockSpec(memory_space=pl.ANY)
_DATAFLOW = pltpu.SideEffectType.DATAFLOW_SIDE_EFFECTING


def _in_hbm(a):
    return pltpu.with_memory_space_constraint(a, pltpu.HBM)


def _exchange_copies(gather, srcs, lands, send_sems, recv_sems):
    nw = len(srcs)
    x, y, c, chip, peers, peer_chip = _place()
    pairs = []
    for m, (px, py) in enumerate(peers):
        for w in range(nw):
            sems = (send_sems.at[nw * m + w], recv_sems.at[nw * m + w], (px, py, c))
            if gather:
                pairs.append((_remote(srcs[w], lands[w].at[chip], *sems),
                              _remote(srcs[w], lands[w].at[peer_chip[m]], *sems)))
            else:
                pairs.append((_remote(srcs[w].at[m], lands[w].at[m], *sems),) * 2)
    return pairs


def _exchange_start(gather, srcs, after, name, lands=None):
    nw = len(srcs)
    n_copies = 3 * nw

    def body(*refs):
        send_sems, recv_sems = refs[2 * nw + 1], refs[2 * nw + 2]
        for start, _ in _exchange_copies(gather, refs[:nw], refs[nw:2 * nw], send_sems, recv_sems):
            start.start()
        refs[-1][...] = jnp.zeros_like(refs[-1])

    if lands is None:
        lands = [lax.empty(((N_CHIPS,) + s.shape) if gather else s.shape, s.dtype) for s in srcs]
    lands = [_in_hbm(l) for l in lands]
    return pl.pallas_call(
        body, name=name,
        out_shape=(pltpu.SemaphoreType.DMA((n_copies,)), pltpu.SemaphoreType.DMA((n_copies,)))
        + tuple(pltpu.HBM(s.shape, s.dtype) for s in srcs)
        + tuple(pltpu.HBM(l.shape, l.dtype) for l in lands)
        + (jax.ShapeDtypeStruct((8, LANES), F32),),
        in_specs=[_HBM] * (2 * nw) + [_ANY],
        out_specs=(_SEM, _SEM) + (_HBM,) * (2 * nw) + (pl.BlockSpec(memory_space=pltpu.VMEM),),
        input_output_aliases={i: 2 + i for i in range(2 * nw)},
        compiler_params=pltpu.CompilerParams(has_side_effects=_DATAFLOW),
    )(*[_in_hbm(s) for s in srcs], *lands, after)


def _exchange_wait(gather, started, after, name):
    nw = (len(started) - 3) // 2
    send_sems, recv_sems = started[0], started[1]
    thru = started[2:2 + 2 * nw]

    def body(*refs):
        for _, arrival in _exchange_copies(gather, refs[:nw], refs[nw:2 * nw], refs[2 * nw], refs[2 * nw + 1]):
            arrival.wait_send()
            arrival.wait_recv()

    outs = pl.pallas_call(
        body, name=name,
        out_shape=tuple(pltpu.HBM(t.shape, t.dtype) for t in thru),
        in_specs=[_HBM] * (2 * nw) + [_SEM, _SEM, _ANY], out_specs=(_HBM,) * (2 * nw),
        input_output_aliases={i: i for i in range(2 * nw)},
        compiler_params=pltpu.CompilerParams(has_side_effects=_DATAFLOW),
    )(*thru, send_sems, recv_sems, after)
    return outs[nw:]


def _reduce_first(stacks):
    ns = len(stacks)
    halves = [s.shape[1] // 2 for s in stacks]
    row_block = 32

    def body(*refs):
        gs, sends, owns = refs[:ns], refs[ns:2 * ns], refs[2 * ns:3 * ns]
        ras, s_sem, r_sem = refs[3 * ns:4 * ns], refs[4 * ns], refs[4 * ns + 1]
        x, y, c, chip, peers, peer_chip = _place()
        swaps = []
        for w in range(ns):
            theirs = gs[w].at[:, pl.ds(pl.multiple_of((1 - c) * halves[w], 8), halves[w]), :]
            swaps.append(_remote(theirs, ras[w], s_sem.at[w], r_sem.at[w], (x, y, 1 - c)))
            swaps[-1].start()
        for w in range(ns):
            n = halves[w]
            swaps[w].wait_recv()

            def sums(i, carry, w=w, n=n):
                r0 = pl.multiple_of(i * row_block, row_block)
                blk = pl.ds(r0, row_block)
                mine = pl.ds(pl.multiple_of(c * n + r0, 8), row_block)
                for m in range(3):
                    sends[w][m, blk, :] = (gs[w][peer_chip[m], mine, :] + ras[w][peer_chip[m], blk, :]).astype(BF16)
                owns[w][blk, :] = gs[w][chip, mine, :] + ras[w][chip, blk, :]
                return carry
            lax.fori_loop(0, n // row_block, sums, 0)
        for cp in swaps:
            cp.wait_send()

    vmem = pl.BlockSpec(memory_space=pltpu.VMEM)
    outs = pl.pallas_call(
        body,
        out_shape=[jax.ShapeDtypeStruct((3, n, s.shape[2]), BF16) for n, s in zip(halves, stacks)]
        + [jax.ShapeDtypeStruct((n, s.shape[2]), F32) for n, s in zip(halves, stacks)],
        in_specs=[vmem] * ns, out_specs=[vmem] * (2 * ns),
        scratch_shapes=[pltpu.VMEM((N_CHIPS, n, s.shape[2]), F32) for n, s in zip(halves, stacks)]
        + [pltpu.SemaphoreType.DMA((ns,)), pltpu.SemaphoreType.DMA((ns,))],
        compiler_params=pltpu.CompilerParams(vmem_limit_bytes=VMEM_LIMIT), name="reduce_first")(*stacks)
    return outs[:ns], outs[ns:]


def _reduce_last(owns, landed, g_small):
    ns = len(owns)
    row_block = 32
    hs = SMALL_ROWS // 2

    def body(*refs):
        own_refs, land_refs, gsm_ref = refs[:ns], refs[ns:2 * ns], refs[2 * ns]
        out_refs, osm_ref = refs[2 * ns + 1:3 * ns + 1], refs[3 * ns + 1]
        ra_sm, p_sm, s_sem, r_sem, sm_s, sm_r = refs[3 * ns + 2:]
        x, y, c, chip, peers, peer_chip = _place()
        sib = (x, y, 1 - c)
        half = lambda cc: pl.ds(pl.multiple_of(cc * hs, 8), hs)
        sm_a = _remote(gsm_ref.at[half(1 - c), :], ra_sm, sm_s.at[0], sm_r.at[0], sib)
        sm_a.start()
        swaps = [sm_a]
        for w in range(ns):
            n = own_refs[w].shape[0]

            def total(i, carry, w=w, n=n):
                r0 = pl.multiple_of(i * row_block, row_block)
                blk = pl.ds(r0, row_block)
                acc = own_refs[w][blk, :]
                for m in range(3):
                    acc = acc + land_refs[w][m, blk, :].astype(F32)
                out_refs[w][pl.ds(pl.multiple_of(c * n + r0, 8), row_block), :] = acc
                return carry
            lax.fori_loop(0, n // row_block, total, 0)
            mine = out_refs[w].at[pl.ds(pl.multiple_of(c * n, 8), n), :]
            swaps.append(_remote(mine, mine, s_sem.at[w], r_sem.at[w], sib))
            swaps[-1].start()
        sm_a.wait_recv()
        p_sm[chip] = gsm_ref[half(c), :] + ra_sm[...]
        for m, (px, py) in enumerate(peers):
            swaps.append(_remote(p_sm.at[chip], p_sm.at[chip], sm_s.at[1 + m], sm_r.at[1 + m], (px, py, c)))
            swaps[-1].start()
        for w in range(ns):
            n = own_refs[w].shape[0]
            theirs = out_refs[w].at[pl.ds(pl.multiple_of((1 - c) * n, 8), n), :]
            _remote(theirs, theirs, s_sem.at[w], r_sem.at[w], sib).wait_recv()
        for m, (px, py) in enumerate(peers):
            _remote(p_sm.at[chip], p_sm.at[peer_chip[m]], sm_s.at[1 + m], sm_r.at[1 + m], (px, py, c)).wait_recv()
        osm_ref[half(c), :] = (p_sm[0] + p_sm[1]) + (p_sm[2] + p_sm[3])
        swaps.append(_remote(osm_ref.at[half(c), :], osm_ref.at[half(c), :], sm_s.at[4], sm_r.at[4], sib))
        swaps[-1].start()
        _remote(osm_ref.at[half(1 - c), :], osm_ref.at[half(1 - c), :], sm_s.at[4], sm_r.at[4], sib).wait_recv()
        for cp in swaps:
            cp.wait_send()

    vmem = pl.BlockSpec(memory_space=pltpu.VMEM)
    return pl.pallas_call(
        body, out_shape=[jax.ShapeDtypeStruct((2 * o.shape[0], o.shape[1]), F32) for o in owns]
        + [jax.ShapeDtypeStruct((SMALL_ROWS, LANES), F32)],
        in_specs=[vmem] * (2 * ns + 1), out_specs=[vmem] * (ns + 1),
        scratch_shapes=[pltpu.VMEM((hs, LANES), F32), pltpu.VMEM((N_CHIPS, hs, LANES), F32),
                        pltpu.SemaphoreType.DMA((ns,)), pltpu.SemaphoreType.DMA((ns,)),
                        pltpu.SemaphoreType.DMA((5,)), pltpu.SemaphoreType.DMA((5,))],
        compiler_params=pltpu.CompilerParams(vmem_limit_bytes=VMEM_LIMIT),
        name="reduce_last")(*owns, *landed, g_small)


_SMALL_PARTS = (("g_norm", 8, 8), ("w_s", 512, 512), ("b_s", 4, 8), ("g_v", 2, 8), ("g_mem", 8, 8),
                ("g_final", 8, 8), ("loss", 1, 8))
_LOSS_ROW = SMALL_ROWS - 8
assert sum(p for _, _, p in _SMALL_PARTS) == SMALL_ROWS


def _pack_small(parts, loss=None):
    loss_row = jnp.zeros((1, LANES), F32) if loss is None else jnp.broadcast_to(loss.reshape(1, 1), (1, LANES))
    rows = []
    for (name, used, padded), p in zip(_SMALL_PARTS, list(parts) + [loss_row]):
        p = p.reshape(used, LANES)
        if padded > used:
            p = jnp.pad(p, ((0, padded - used), (0, 0)))
        rows.append(p)
    return jnp.concatenate(rows, axis=0)


def _local_step(x, mem, target, g_norm, proj, w_in, w_s, b_s, g_v, g_mem, late_weights, g_final,
                fwd_token=None, on_dw=None):
    B, S, _ = x.shape
    x2d = x.reshape(B * S, D_MODEL)
    t2d = target.reshape(B * S, D_MODEL)
    mem2d = mem.reshape(B * N_MEM, D_MODEL)

    a, lse = _attn_fwd(proj, B, S, after=() if fwd_token is None else (fwd_token,))
    w_kv, w_out = late_weights(a)
    kv = _kv_fwd(mem2d, g_mem, w_kv)
    w_sT = jnp.swapaxes(w_s, 1, 2)
    b_tab = jnp.repeat(b_s.T, HEAD_DIM, axis=1)
    (dx2, da, drest, loss, d_wout, d_ws, d_bs, d_gv, d_gf, dkv) = _mid(
        x2d, t2d, a, proj, kv, w_s, w_sT, b_tab, g_v, w_out, g_final, B, S)
    d_wkv, d_gmem = _kv_bwd(mem2d, g_mem, w_kv, dkv)
    dq, dk, dv = _attn_bwd(proj, a, lse, da, B, S)
    d_win = _inproj_bwd_dw(dq, dk, dv, drest, x2d, g_norm)
    grad_x, d_gnorm = _inproj_bwd_dx(dq, dk, dv, drest, x2d, dx2, g_norm, w_in,
                                     after=() if on_dw is None else (on_dw(d_win, d_wkv, d_wout),))
    d_bs = d_bs[:, :N_SGU_GROUPS].T
    return (loss[0, 0], grad_x.reshape(B, S, D_MODEL),
            dict(g_norm=d_gnorm, w_in=d_win, w_s=d_ws, b_s=d_bs, g_v=d_gv, g_mem=d_gmem, w_kv=d_wkv,
                 w_out=d_wout, g_final=d_gf))


def kernel(x, mem, g_norm, w_in, w_sgu_spatial, b_sgu_spatial, g_sgu_v, g_mem, w_mem_kv, w_out, g_final, loss_target, m_g_norm, m_w_in, m_w_sgu_spatial, m_b_sgu_spatial, m_g_sgu_v, m_g_mem, m_w_mem_kv, m_w_out, m_g_final, v_g_norm, v_w_in, v_w_sgu_spatial, v_b_sgu_spatial, v_g_sgu_v, v_g_mem, v_w_mem_kv, v_w_out, v_g_final):
    t = lambda w: jnp.swapaxes(w[0], 0, 1)
    batch, seq, _ = x.shape
    proj, w_in_full, late_shards, late_lands = _inproj_gather(
        x.reshape(batch * seq, D_MODEL), g_norm, t(w_in), [w_mem_kv[0], w_out[0]])
    late = _exchange_start(True, list(late_shards), proj, "gather_late_start", lands=late_lands)

    def late_weights(a):
        return [z.reshape(-1, z.shape[-1]) for z in _exchange_wait(True, late, a, "gather_late_wait")]

    scatter = {}

    def on_dw(d_win, d_wkv, d_wout):
        stacks = [d_win.reshape((N_CHIPS, w_in.shape[2], w_in.shape[1])),
                  d_wkv.reshape((N_CHIPS,) + w_mem_kv.shape[1:]), d_wout.reshape((N_CHIPS,) + w_out.shape[1:])]
        sends, scatter["own"] = _reduce_first(stacks)
        scatter["started"] = _exchange_start(False, list(sends), scatter["own"][0], "scatter_start")
        return scatter["started"][-1]

    loss, grad_x, g = _local_step(
        x, mem, loss_target, g_norm, proj, w_in_full, w_sgu_spatial[0], b_sgu_spatial[0], g_sgu_v, g_mem,
        late_weights, g_final.reshape(1, D_MODEL), fwd_token=late[-1], on_dw=on_dw)

    small_names = ("g_norm", "w_s", "b_s", "g_v", "g_mem", "g_final")
    g_small = _pack_small([g[n] for n in small_names], loss)
    landed = _exchange_wait(False, scatter["started"], g_small, "scatter_wait")
    gr_in, gr_kv, gr_out, gr_small = _reduce_last(scatter["own"], landed, g_small)
    loss = gr_small[_LOSS_ROW, 0]

    small_w = (g_norm, w_sgu_spatial, b_sgu_spatial, g_sgu_v, g_mem, g_final)
    small_m = (m_g_norm, m_w_sgu_spatial, m_b_sgu_spatial, m_g_sgu_v, m_g_mem, m_g_final)
    small_v = (v_g_norm, v_w_sgu_spatial, v_b_sgu_spatial, v_g_sgu_v, v_g_mem, v_g_final)
    rows = lambda ws: [w.reshape(-1, LANES) for w in ws]
    small = [[z.reshape(w.shape) for z in four]
             for w, four in zip(small_w, _adamw_small(gr_small, rows(small_w), rows(small_m), rows(small_v)))]
    d_in, nm_in, nv_in = _adamw(t(w_in), gr_in, t(m_w_in), t(v_w_in), "adamw_w_in")
    gr_in, d_in, nm_in, nv_in = [jnp.swapaxes(z, 0, 1) for z in (gr_in, d_in, nm_in, nv_in)]
    d_kv, nm_kv, nv_kv = _adamw(w_mem_kv[0], gr_kv, m_w_mem_kv[0], v_w_mem_kv[0], "adamw_w_kv")
    d_out, nm_out, nv_out = _adamw(w_out[0], gr_out, m_w_out[0], v_w_out[0], "adamw_w_out")

    def leaves(kind, big_in, big_kv, big_out):
        s_norm, s_ws, s_bs, s_gv, s_gmem, s_gf = [four[kind] for four in small]
        return [s_norm, big_in[None], s_ws, s_bs, s_gv, s_gmem, big_kv[None], big_out[None], s_gf]

    return (loss, grad_x, *leaves(0, gr_in, gr_kv, gr_out), *leaves(1, d_in, d_kv, d_out),
            *leaves(2, nm_in, nm_kv, nm_out), *leaves(3, nv_in, nv_kv, nv_out))
```

```python
import functools

import jax
import jax.numpy as jnp
from jax import lax
from jax.experimental import pallas as pl
from jax.experimental.pallas import tpu as pltpu

F32 = jnp.float32
BF16 = jnp.bfloat16
MESH = pl.DeviceIdType.MESH

D_MODEL = 1024
ATTN_WIDTH = 512
SGU_WIDTH = 256
MEM_WIDTH = 256
N_MEM = 256
IN_COLS = 3328
QKV_COLS = 3 * ATTN_WIDTH
REST_COLS = IN_COLS - QKV_COLS
SGU_CHUNK = 128
N_SGU_GROUPS = 4
EPS = 1e-6
NEG_INF = -1e30
DILATIONS = (1, 4, 16)
RADIUS = 64
Q_BLOCK = 128
LANES = 128
HEAD_DIM = 64

ADAM_LR = 0.001
ADAM_B1 = 0.9
ADAM_B2 = 0.999
ADAM_EPS = 1e-08
ADAM_WD = 0.01
ADAM_STEP = 10

N_CHIPS = 4
VMEM_LIMIT = 56 * 1024 * 1024
SMALL_ROWS = 560


def _params(sem=None, vmem=VMEM_LIMIT):
    return pltpu.CompilerParams(dimension_semantics=sem, vmem_limit_bytes=vmem)


def _nn(a, b):
    return jnp.dot(a, b, preferred_element_type=F32)


def _nt(a, b):
    return lax.dot_general(a, b, (((1,), (1,)), ((), ())), preferred_element_type=F32)


def _tn(a, b):
    return lax.dot_general(a, b, (((0,), (0,)), ((), ())), preferred_element_type=F32)


def _rms(x):
    r = lax.rsqrt(jnp.mean(x * x, axis=-1, keepdims=True) + EPS)
    return r, x * r


def _head_masks():
    lane = lax.broadcasted_iota(jnp.int32, (1, LANES), 1)
    lo = lane < HEAD_DIM
    return lo, (lo.astype(F32), (~lo).astype(F32))


def _silu_parts(z):
    s = jax.nn.sigmoid(z)
    return z * s, s * (1.0 + z * (1.0 - s))


def _gelu_parts(x):
    c = 0.7978845608028654
    x2 = x * x
    t = jnp.tanh(c * (x + 0.044715 * (x * x2)))
    val = 0.5 * x * (1.0 + t)
    grad = 0.5 * (1.0 + t) + 0.5 * x * (1.0 - t * t) * (c * (1.0 + 3.0 * 0.044715 * x2))
    return val, grad


def _after(tokens):
    return [pl.BlockSpec(memory_space=pl.ANY)] * len(tokens)


def _inproj_fwd(x2d, g_norm, w_in_t, after=()):
    T = x2d.shape[0]
    tm = 512

    def body(x_ref, g_ref, w_ref, *rest):
        o_ref = rest[-1]
        _, xh = _rms(x_ref[...])
        h = (xh * g_ref[...]).astype(BF16)
        o_ref[...] = _nt(h, w_ref[...])

    return pl.pallas_call(
        body, grid=(T // tm,),
        in_specs=[pl.BlockSpec((tm, D_MODEL), lambda i: (i, 0)),
                  pl.BlockSpec((1, D_MODEL), lambda i: (0, 0)),
                  pl.BlockSpec((IN_COLS, D_MODEL), lambda i: (0, 0))] + _after(after),
        out_specs=pl.BlockSpec((tm, IN_COLS), lambda i: (i, 0)),
        out_shape=jax.ShapeDtypeStruct((T, IN_COLS), F32),
        compiler_params=_params(("arbitrary",)), name="inproj_fwd")(x2d, g_norm, w_in_t, *after)


def _kv_fwd(mem2d, g_mem, w_kv):
    Tm = mem2d.shape[0]

    def body(m_ref, g_ref, w_ref, o_ref):
        _, mh = _rms(m_ref[...])
        o_ref[...] = _nn((mh * g_ref[...]).astype(BF16), w_ref[...])

    return pl.pallas_call(
        body, out_shape=jax.ShapeDtypeStruct((Tm, 2 * MEM_WIDTH), F32),
        compiler_params=_params(), name="kv_fwd")(mem2d, g_mem, w_kv)


def _kv_bwd(mem2d, g_mem, w_kv, dkv):
    Tm = mem2d.shape[0]

    def body(m_ref, g_ref, w_ref, dkv_ref, dw_ref, dg_ref):
        _, mh = _rms(m_ref[...])
        memn = (mh * g_ref[...]).astype(BF16)
        dkvb = dkv_ref[...].astype(BF16)
        dw_ref[...] = _tn(memn, dkvb)
        dmemn = _nt(dkvb, w_ref[...])
        dg_ref[...] = jnp.sum(dmemn * mh, axis=0, keepdims=True)

    return pl.pallas_call(
        body, out_shape=(jax.ShapeDtypeStruct((D_MODEL, 2 * MEM_WIDTH), F32),
                         jax.ShapeDtypeStruct((1, D_MODEL), F32)),
        compiler_params=_params(), name="kv_bwd")(mem2d, g_mem, w_kv, dkv)


def _attn_geometry(S):
    geom = []
    for d in DILATIONS:
        L = S // d
        assert L % Q_BLOCK == 0
        geom.append((d, L, min(2 * Q_BLOCK, L), L // Q_BLOCK))
    return geom


def _init_bias(bias_scr, geom, hp):
    row = lax.broadcasted_iota(jnp.int32, (Q_BLOCK, 2 * Q_BLOCK), 0)
    col = lax.broadcasted_iota(jnp.int32, (Q_BLOCK, 2 * Q_BLOCK), 1)
    for j in (0, 1):
        bits = (126 - (2 * hp + j)) * (1 << 23)
        slope = lax.bitcast_convert_type(jnp.full((1, 1), bits, jnp.int32), F32)
        for di, (d, _, _, _) in enumerate(geom):
            for cls, off in enumerate((0, -RADIUS, -2 * RADIUS)):
                dist = jnp.abs(col - row + off)
                bias_scr[di * 6 + cls * 2 + j] = jnp.where(
                    dist <= RADIUS, -(slope * float(d)) * dist.astype(F32), NEG_INF)


SPLIT = 4
COPY_ROWS = 256


def _by4_rows(S, step):
    per_class = S // SPLIT // COPY_ROWS
    r, j = step // per_class, step % per_class
    return (pl.ds(r + SPLIT * j * COPY_ROWS, COPY_ROWS, stride=SPLIT),
            pl.ds(pl.multiple_of(r * (S // SPLIT) + j * COPY_ROWS, COPY_ROWS), COPY_ROWS))


def _to_by4(src, dst, S):
    def step(i, carry):
        natural, by4 = _by4_rows(S, i)
        dst[by4, :] = src[natural, :]
        return carry
    lax.fori_loop(0, S // COPY_ROWS, step, 0)


def _block_slices(d, L, KW, nqb, r, qb, S):
    qs = qb * Q_BLOCK
    ks = jnp.clip(qs - RADIUS, 0, L - KW)
    cls = jnp.where(qb == 0, 0, jnp.where(qb == nqb - 1, 2, 1))
    if d == 1:
        qsl = pl.ds(pl.multiple_of(qs, Q_BLOCK), Q_BLOCK)
        ksl = pl.ds(pl.multiple_of(ks, RADIUS), KW)
    elif d == SPLIT:
        qsl = pl.ds(pl.multiple_of(r * L + qs, Q_BLOCK), Q_BLOCK)
        ksl = pl.ds(pl.multiple_of(r * L + ks, RADIUS), KW)
    else:
        sub = d // SPLIT
        base = (r % SPLIT) * (S // SPLIT) + r // SPLIT
        qsl = pl.ds(base + qs * sub, Q_BLOCK, stride=sub)
        ksl = pl.ds(base + ks * sub, KW, stride=sub)
    return qsl, ksl, cls


def _for_groups(geom, S, group, fn):
    for di, (d, L, KW, nqb) in enumerate(geom):
        assert (d * nqb) % group == 0

        def step(it, carry, di=di, d=d, L=L, KW=KW, nqb=nqb):
            slices = []
            for g in range(group):
                i = it * group + g
                slices.append(_block_slices(d, L, KW, nqb, i // nqb, i % nqb, S))
            fn(di, KW, slices)
            return carry
        lax.fori_loop(0, d * nqb // group, step, 0)


def _attn_fwd(proj, B, S):
    T = B * S
    geom = _attn_geometry(S)
    n_pairs = ATTN_WIDTH // LANES

    def body(q_ref, k_ref, v_ref, a_ref, lse_ref, bias_scr, q4, k4, v4, *per_dilation):
        o_scr, m_scr, l_scr = per_dilation[0:3], per_dilation[3:6], per_dilation[6:9]
        lo, hm = _head_masks()
        _init_bias(bias_scr, geom, pl.program_id(1))
        for src, dst in ((q_ref, q4), (k_ref, k4), (v_ref, v4)):
            _to_by4(src, dst, S)

        def group(di, KW, slices):
            chains = [(g, j) for g in range(len(slices)) for j in (0, 1)]
            q_src, k_src, v_src = (q_ref, k_ref, v_ref) if di == 0 else (q4, k4, v4)
            q = [q_src[qsl, :] for qsl, _, _ in slices]
            kw = [k_src[ksl, :].astype(BF16) for _, ksl, _ in slices]
            vw = [v_src[ksl, :].astype(BF16) for _, ksl, _ in slices]
            s = {(g, j): _nt((q[g] * (hm[j] * 0.125)).astype(BF16), kw[g])
                 + bias_scr[di * 6 + slices[g][2] * 2 + j, :, pl.ds(0, KW)] for g, j in chains}
            m = {c: jnp.max(s[c], axis=1, keepdims=True) for c in chains}
            p = {c: jnp.exp(s[c] - m[c]) for c in chains}
            l = {c: jnp.sum(p[c], axis=1, keepdims=True) for c in chains}
            o = {(g, j): _nn(p[(g, j)].astype(BF16), vw[g]) for g, j in chains}
            for g, (qsl, _, _) in enumerate(slices):
                o_scr[di][qsl, :] = jnp.where(lo, o[(g, 0)], o[(g, 1)])
                m_scr[di][qsl, :] = jnp.where(lo, m[(g, 0)], m[(g, 1)])
                l_scr[di][qsl, :] = jnp.where(lo, l[(g, 0)], l[(g, 1)])

        _for_groups(geom, S, 8, group)

        def combine(i, carry):
            natural, by4 = _by4_rows(S, i)
            rows = [natural, by4, by4]
            ms = [m_scr[di][rows[di], :] for di in range(3)]
            mx = jnp.maximum(jnp.maximum(ms[0], ms[1]), ms[2])
            num = 0.0
            den = 0.0
            for di in range(3):
                w = jnp.exp(ms[di] - mx)
                num = num + w * o_scr[di][rows[di], :]
                den = den + w * l_scr[di][rows[di], :]
            a_ref[natural, :] = num / den
            lse_ref[natural, :] = mx + jnp.log(den)
            return carry

        lax.fori_loop(0, S // COPY_ROWS, combine, 0)

    blk = lambda off: pl.BlockSpec((S, LANES), lambda b, h, off=off: (b, off + h))
    out_blk = pl.BlockSpec((S, LANES), lambda b, h: (b, h))
    return pl.pallas_call(
        body, grid=(B, n_pairs),
        in_specs=[blk(0), blk(n_pairs), blk(2 * n_pairs)],
        out_specs=[out_blk, out_blk],
        out_shape=[jax.ShapeDtypeStruct((T, ATTN_WIDTH), F32)] * 2,
        scratch_shapes=[pltpu.VMEM((18, Q_BLOCK, 2 * Q_BLOCK), F32)] + [pltpu.VMEM((S, LANES), F32)] * 12,
        compiler_params=_params(("arbitrary", "arbitrary")), name="attn_fwd")(proj, proj, proj)


def _attn_bwd(proj, a, lse, da, B, S):
    T = B * S
    geom = _attn_geometry(S)
    n_pairs = ATTN_WIDTH // LANES

    def body(q_ref, k_ref, v_ref, a_ref, lse_ref, do_ref, dq_ref, dk_ref, dv_ref, bias_scr, *scr):
        acc = (scr[0:3], scr[3:6])
        natural_in = (q_ref, k_ref, v_ref, a_ref, lse_ref, do_ref)
        by4_in = scr[6:12]
        _, hm = _head_masks()
        _init_bias(bias_scr, geom, pl.program_id(1))
        for ref in scr[0:6]:
            ref[...] = jnp.zeros_like(ref)
        for src, dst in zip(natural_in, by4_in):
            _to_by4(src, dst, S)

        def group(di, KW, slices):
            n = len(slices)
            chains = [(g, j) for g in range(n) for j in (0, 1)]
            q_src, k_src, v_src, a_src, lse_src, do_src = natural_in if di == 0 else by4_in
            dq_scr, dk_scr, dv_scr = acc[0 if di == 0 else 1]
            q = [q_src[qsl, :] for qsl, _, _ in slices]
            do = [do_src[qsl, :] for qsl, _, _ in slices]
            doa = [do[g] * a_src[slices[g][0], :] for g in range(n)]
            lse_q = [lse_src[qsl, :] for qsl, _, _ in slices]
            kw = [k_src[ksl, :].astype(BF16) for _, ksl, _ in slices]
            vw = [v_src[ksl, :].astype(BF16) for _, ksl, _ in slices]
            qj = {(g, j): (q[g] * (hm[j] * 0.125)).astype(BF16) for g, j in chains}
            doj = {(g, j): (do[g] * hm[j]).astype(BF16) for g, j in chains}
            s = {(g, j): _nt(qj[(g, j)], kw[g])
                 + bias_scr[di * 6 + slices[g][2] * 2 + j, :, pl.ds(0, KW)] for g, j in chains}
            dp = {(g, j): _nt(doj[(g, j)], vw[g]) for g, j in chains}
            dsum = {(g, j): jnp.sum(doa[g] * hm[j], axis=1, keepdims=True) for g, j in chains}
            p = {(g, j): jnp.exp(s[(g, j)] - lse_q[g][:, HEAD_DIM * j:HEAD_DIM * j + 1]) for g, j in chains}
            ds = {c: (p[c] * (dp[c] - dsum[c])).astype(BF16) for c in chains}
            pb = {c: p[c].astype(BF16) for c in chains}
            dq = [_nn(ds[(g, 0)], kw[g]) * (hm[0] * 0.125) + _nn(ds[(g, 1)], kw[g]) * (hm[1] * 0.125)
                  for g in range(n)]
            both = lambda t, g: jnp.concatenate([t[(g, 0)], t[(g, 1)]], axis=0)
            dkw = [_tn(both(ds, g), both(qj, g)) for g in range(n)]
            dvw = [_tn(both(pb, g), both(doj, g)) for g in range(n)]
            for g, (qsl, ksl, _) in enumerate(slices):
                dq_scr[qsl, :] = dq_scr[qsl, :] + dq[g]
                dk_scr[ksl, :] = dk_scr[ksl, :] + dkw[g]
                dv_scr[ksl, :] = dv_scr[ksl, :] + dvw[g]

        _for_groups(geom, S, 4, group)

        def merge(i, carry):
            natural, by4 = _by4_rows(S, i)
            for nat, split in zip(*acc):
                nat[natural, :] = nat[natural, :] + split[by4, :]
            return carry
        lax.fori_loop(0, S // COPY_ROWS, merge, 0)
        for out, nat in zip((dq_ref, dk_ref, dv_ref), acc[0]):
            out[...] = nat[...].astype(BF16)

    blk = lambda off: pl.BlockSpec((S, LANES), lambda b, h, off=off: (b, off + h))
    return pl.pallas_call(
        body, grid=(B, n_pairs),
        in_specs=[blk(0), blk(n_pairs), blk(2 * n_pairs), blk(0), blk(0), blk(0)],
        out_specs=[blk(0), blk(0), blk(0)],
        out_shape=[jax.ShapeDtypeStruct((T, ATTN_WIDTH), BF16)] * 3,
        scratch_shapes=[pltpu.VMEM((18, Q_BLOCK, 2 * Q_BLOCK), F32)] + [pltpu.VMEM((S, LANES), F32)] * 12,
        compiler_params=_params(("arbitrary", "arbitrary")), name="attn_bwd")(proj, proj, proj, a, lse, da)


def _mid(x2d, t2d, a, proj, kv, w_s, w_sT, b_tab, g_v, w_out, g_final, B, S):
    T = B * S
    tm = 512
    nt = S // tm
    halves = 2
    hrows = tm // halves

    def body(x_ref, t_ref, a_ref, za_ref, ub_ref, vb_ref, zb_ref, qm_ref, zm_ref, kv_ref,
              ws_ref, wsT_ref, btab_ref, gv_ref, wout_ref, gf_ref,
              dx2_ref, da_ref, drest_ref, loss_ref, dwout_ref, dws_ref, dbs_ref, dgv_ref, dgf_ref, dkv_ref,
              dbtab_scr):
        b = pl.program_id(0)
        t = pl.program_id(1)
        first = jnp.logical_and(b == 0, t == 0)
        last = jnp.logical_and(b == B - 1, t == nt - 1)
        _, hm = _head_masks()
        lane_g = lax.broadcasted_iota(jnp.int32, (1, SGU_WIDTH), 1) // HEAD_DIM
        gm = [(lane_g == g).astype(F32) for g in range(N_SGU_GROUPS)]
        H = range(halves)
        rows = [pl.ds(h * hrows, hrows) for h in H]
        ld = lambda ref: [ref[r, :] for r in rows]
        cat = lambda parts, axis: jnp.concatenate(parts, axis=axis)
        chunks = [slice(ci * SGU_CHUNK, (ci + 1) * SGU_CHUNK) for ci in range(hrows // SGU_CHUNK)]
        pairs = [slice(pr * LANES, (pr + 1) * LANES) for pr in range(2)]
        heads = [(pr, j) for pr in range(2) for j in (0, 1)]

        @pl.when(first)
        def _():
            loss_ref[...] = jnp.zeros_like(loss_ref)
            dwout_ref[...] = jnp.zeros_like(dwout_ref)
            dws_ref[...] = jnp.zeros_like(dws_ref)
            dbs_ref[...] = jnp.zeros_like(dbs_ref)
            dgv_ref[...] = jnp.zeros_like(dgv_ref)
            dgf_ref[...] = jnp.zeros_like(dgf_ref)
            dbtab_scr[...] = jnp.zeros_like(dbtab_scr)

        @pl.when(t == 0)
        def _():
            dkv_ref[...] = jnp.zeros_like(dkv_ref)

        a_val = ld(a_ref)
        sil_a = [_silu_parts(z) for z in ld(za_ref)]
        gated_a = [s[0] * a for s, a in zip(sil_a, a_val)]
        u = [_gelu_parts(z) for z in ld(ub_ref)]
        vv = [_gelu_parts(z) for z in ld(vb_ref)]
        vnorm = [_rms(v[0]) for v in vv]
        gv = gv_ref[...]
        vn = [(n[1] * gv).astype(BF16) for n in vnorm]
        w_cat = cat([ws_ref[g].astype(BF16) for g in range(N_SGU_GROUPS)], 1)
        wT_cat = cat([wsT_ref[g].astype(BF16) for g in range(N_SGU_GROUPS)], 1)
        gmb = [m.astype(BF16) for m in gm]
        by_group = lambda chunk: cat([chunk * gmb[g] for g in range(N_SGU_GROUPS)], 0)
        btab = btab_ref[...]
        mixed = [cat([btab + _nn(w_cat, by_group(vn[h][c, :])) for c in chunks], 0) for h in H]
        sg = [u[h][0] * mixed[h] for h in H]
        sil_b = [_silu_parts(z) for z in ld(zb_ref)]
        gated_b = [sil_b[h][0] * sg[h] for h in H]

        kvv = kv_ref[...].astype(BF16)
        kp = [kvv[:, p] for p in pairs]
        vp = [kvv[:, MEM_WIDTH + pr * LANES:MEM_WIDTH + (pr + 1) * LANES] for pr in range(2)]
        qm = ld(qm_ref)
        qj = {(h, pr, j): (qm[h][:, pairs[pr]] * (hm[j] * 0.125)).astype(BF16) for h in H for pr, j in heads}
        sc = {k: _nt(qj[k], kp[k[1]]) for k in qj}
        ex = {k: jnp.exp(sc[k] - jnp.max(sc[k], axis=1, keepdims=True)) for k in qj}
        prob = {k: ex[k] * (1.0 / jnp.sum(ex[k], axis=1, keepdims=True)) for k in qj}
        probb = {k: prob[k].astype(BF16) for k in qj}
        mo = [cat([sum(_nn(probb[(h, pr, j)], vp[pr]) * hm[j] for j in (0, 1)) for pr in range(2)], 1) for h in H]
        sil_m = [_silu_parts(z) for z in ld(zm_ref)]
        gated_m = [sil_m[h][0] * mo[h] for h in H]

        gated = [cat([gated_a[h], gated_b[h], gated_m[h]], 1).astype(BF16) for h in H]
        wout = wout_ref[...]
        x_in = ld(x_ref)
        x2 = [x_in[h] + _nn(gated[h], wout) for h in H]
        fin = [_rms(z) for z in x2]
        gf = gf_ref[...]
        tgt = ld(t_ref)
        err = [fin[h][1] * gf - tgt[h] for h in H]
        loss_ref[...] += sum(jnp.sum(e * e) for e in err) * (0.5 / D_MODEL)

        dy = [e * (1.0 / D_MODEL) for e in err]
        dgf_ref[...] += sum(jnp.sum(dy[h] * fin[h][1], axis=0, keepdims=True) for h in H)
        gdy = [d * gf for d in dy]
        dx2 = [fin[h][0] * (gdy[h] - fin[h][1] * jnp.mean(gdy[h] * fin[h][1], axis=1, keepdims=True)) for h in H]
        for h in H:
            dx2_ref[rows[h], :] = dx2[h]
        dx2b = [d.astype(BF16) for d in dx2]
        dgated = [_nt(d, wout) for d in dx2b]
        dwout_ref[...] += _tn(cat(gated, 0), cat(dx2b, 0))
        dga = [d[:, 0:ATTN_WIDTH] for d in dgated]
        dgb = [d[:, ATTN_WIDTH:ATTN_WIDTH + SGU_WIDTH] for d in dgated]
        dgm = [d[:, ATTN_WIDTH + SGU_WIDTH:] for d in dgated]

        for h in H:
            da_ref[rows[h], :] = dga[h] * sil_a[h][0]
        dza = [dga[h] * a_val[h] * sil_a[h][1] for h in H]

        dsg = [dgb[h] * sil_b[h][0] for h in H]
        dzb = [dgb[h] * sg[h] * sil_b[h][1] for h in H]
        dub = [dsg[h] * mixed[h] * u[h][1] for h in H]
        dmixed = [dsg[h] * u[h][0] for h in H]
        dmixed_b = [d.astype(BF16) for d in dmixed]
        dvn = [cat([_nn(wT_cat, by_group(dmixed_b[h][c, :])) for c in chunks], 0) for h in H]
        for g in range(N_SGU_GROUPS):
            dws_ref[g] += sum(_nt((dmixed[h][c, :] * gm[g]).astype(BF16), vn[h][c, :]) for h in H for c in chunks)
        dbtab_scr[...] += sum(dmixed[h][c, :] for h in H for c in chunks)
        dgv_ref[...] += sum(jnp.sum(dvn[h] * vnorm[h][1], axis=0, keepdims=True) for h in H)
        tv = [d * gv for d in dvn]
        dvv = [vnorm[h][0] * (tv[h] - vnorm[h][1] * jnp.mean(tv[h] * vnorm[h][1], axis=1, keepdims=True)) for h in H]
        dvb = [dvv[h] * vv[h][1] for h in H]

        dmo = [dgm[h] * sil_m[h][0] for h in H]
        dzm = [dgm[h] * mo[h] * sil_m[h][1] for h in H]
        dmoj = {(h, pr, j): (dmo[h][:, pairs[pr]] * hm[j]).astype(BF16) for h in H for pr, j in heads}
        dp = {k: _nt(dmoj[k], vp[k[1]]) for k in qj}
        ds = {k: (prob[k] * (dp[k] - jnp.sum(dp[k] * prob[k], axis=1, keepdims=True))).astype(BF16) for k in qj}
        dqm = [cat([sum(_nn(ds[(h, pr, j)], kp[pr]) * (hm[j] * 0.125) for j in (0, 1)) for pr in range(2)], 1)
               for h in H]
        every = lambda tbl, pr: cat([tbl[(h, pr, j)] for h in H for j in (0, 1)], 0)
        dk = [_tn(every(ds, pr), every(qj, pr)) for pr in range(2)]
        dv = [_tn(every(probb, pr), every(dmoj, pr)) for pr in range(2)]
        dkv_ref[...] += cat(dk + dv, 1)

        for h in H:
            drest_ref[rows[h], :] = cat([dza[h], dub[h], dvb[h], dzb[h], dqm[h], dzm[h]], 1).astype(BF16)

        @pl.when(last)
        def _():
            lane = lax.broadcasted_iota(jnp.int32, (1, LANES), 1)
            dbt = dbtab_scr[...]
            out = jnp.zeros((SGU_CHUNK, LANES), F32)
            for g in range(N_SGU_GROUPS):
                out = out + jnp.where(lane == g, jnp.sum(dbt * gm[g], axis=1, keepdims=True), 0.0)
            dbs_ref[...] = out

    tile = lambda w, cb: pl.BlockSpec((tm, w), lambda b, t, cb=cb: (b * nt + t, cb))
    const = lambda shape: pl.BlockSpec(shape, lambda b, t, n=len(shape): (0,) * n)
    return pl.pallas_call(
        body, grid=(B, nt),
        in_specs=[tile(D_MODEL, 0), tile(D_MODEL, 0), tile(ATTN_WIDTH, 0),
                  tile(ATTN_WIDTH, 3),
                  tile(SGU_WIDTH, 8), tile(SGU_WIDTH, 9), tile(SGU_WIDTH, 10),
                  tile(MEM_WIDTH, 11), tile(MEM_WIDTH, 12),
                  pl.BlockSpec((N_MEM, 2 * MEM_WIDTH), lambda b, t: (b, 0)),
                  const((N_SGU_GROUPS, SGU_CHUNK, SGU_CHUNK)), const((N_SGU_GROUPS, SGU_CHUNK, SGU_CHUNK)),
                  const((SGU_CHUNK, SGU_WIDTH)), const((1, SGU_WIDTH)),
                  const((D_MODEL, D_MODEL)), const((1, D_MODEL))],
        out_specs=[tile(D_MODEL, 0), tile(ATTN_WIDTH, 0), tile(REST_COLS, 0),
                   const((8, LANES)), const((D_MODEL, D_MODEL)),
                   const((N_SGU_GROUPS, SGU_CHUNK, SGU_CHUNK)), const((SGU_CHUNK, LANES)),
                   const((1, SGU_WIDTH)), const((1, D_MODEL)),
                   pl.BlockSpec((N_MEM, 2 * MEM_WIDTH), lambda b, t: (b, 0))],
        out_shape=[jax.ShapeDtypeStruct((T, D_MODEL), F32), jax.ShapeDtypeStruct((T, ATTN_WIDTH), F32),
                   jax.ShapeDtypeStruct((T, REST_COLS), BF16),
                   jax.ShapeDtypeStruct((8, LANES), F32), jax.ShapeDtypeStruct((D_MODEL, D_MODEL), F32),
                   jax.ShapeDtypeStruct((N_SGU_GROUPS, SGU_CHUNK, SGU_CHUNK), F32),
                   jax.ShapeDtypeStruct((SGU_CHUNK, LANES), F32),
                   jax.ShapeDtypeStruct((1, SGU_WIDTH), F32), jax.ShapeDtypeStruct((1, D_MODEL), F32),
                   jax.ShapeDtypeStruct((B * N_MEM, 2 * MEM_WIDTH), F32)],
        scratch_shapes=[pltpu.VMEM((SGU_CHUNK, SGU_WIDTH), F32)],
        compiler_params=_params(("arbitrary", "arbitrary")), name="mid")(
            x2d, t2d, a, proj, proj, proj, proj, proj, proj, kv, w_s, w_sT, b_tab, g_v, w_out, g_final)


def _inproj_bwd_dx(dq, dk, dv, drest, x2d, dx2, g_norm, w_in_t, after=()):
    T = x2d.shape[0]
    tm = 512
    W = ATTN_WIDTH

    def body(dq_ref, dk_ref, dv_ref, dr_ref, x_ref, dx2_ref, g_ref, w_ref, *rest):
        gx_ref, dg_ref = rest[-2:]

        @pl.when(pl.program_id(0) == 0)
        def _():
            dg_ref[...] = jnp.zeros_like(dg_ref)

        halves = [pl.ds(h * (tm // 2), tm // 2) for h in (0, 1)]
        dh = [(_nn(dq_ref[r, :], w_ref[0:W, :]) + _nn(dk_ref[r, :], w_ref[W:2 * W, :])
               + _nn(dv_ref[r, :], w_ref[2 * W:3 * W, :]) + _nn(dr_ref[r, :], w_ref[QKV_COLS:IN_COLS, :]))
              for r in halves]
        nrm = [_rms(x_ref[r, :]) for r in halves]
        dg_ref[...] += sum(jnp.sum(d * n[1], axis=0, keepdims=True) for d, n in zip(dh, nrm))
        g = g_ref[...]
        for r, d, (rstd, xh) in zip(halves, dh, nrm):
            th = d * g
            gx_ref[r, :] = rstd * (th - xh * jnp.mean(th * xh, axis=1, keepdims=True)) + dx2_ref[r, :]

    tile = lambda w: pl.BlockSpec((tm, w), lambda i: (i, 0))
    return pl.pallas_call(
        body, grid=(T // tm,),
        in_specs=[tile(W), tile(W), tile(W), tile(REST_COLS), tile(D_MODEL), tile(D_MODEL),
                  pl.BlockSpec((1, D_MODEL), lambda i: (0, 0)),
                  pl.BlockSpec((IN_COLS, D_MODEL), lambda i: (0, 0))] + _after(after),
        out_specs=[tile(D_MODEL), pl.BlockSpec((1, D_MODEL), lambda i: (0, 0))],
        out_shape=[jax.ShapeDtypeStruct((T, D_MODEL), F32), jax.ShapeDtypeStruct((1, D_MODEL), F32)],
        compiler_params=_params(("arbitrary",)), name="inproj_bwd_dx")(
            dq, dk, dv, drest, x2d, dx2, g_norm, w_in_t, *after)


def _inproj_bwd_dw(dq, dk, dv, drest, x2d, g_norm):
    T = x2d.shape[0]
    tm = 512
    W = ATTN_WIDTH

    def body(dq_ref, dk_ref, dv_ref, dr_ref, x_ref, g_ref, dw_ref):
        @pl.when(pl.program_id(0) == 0)
        def _():
            dw_ref[...] = jnp.zeros_like(dw_ref)

        _, xh = _rms(x_ref[...])
        h = (xh * g_ref[...]).astype(BF16)
        dw_ref[0:W, :] += _tn(dq_ref[...], h)
        dw_ref[W:2 * W, :] += _tn(dk_ref[...], h)
        dw_ref[2 * W:3 * W, :] += _tn(dv_ref[...], h)
        dw_ref[QKV_COLS:IN_COLS, :] += _tn(dr_ref[...], h)

    tile = lambda w: pl.BlockSpec((tm, w), lambda i: (i, 0))
    return pl.pallas_call(
        body, grid=(T // tm,),
        in_specs=[tile(W), tile(W), tile(W), tile(REST_COLS), tile(D_MODEL),
                  pl.BlockSpec((1, D_MODEL), lambda i: (0, 0))],
        out_specs=pl.BlockSpec((IN_COLS, D_MODEL), lambda i: (0, 0)),
        out_shape=jax.ShapeDtypeStruct((IN_COLS, D_MODEL), F32),
        compiler_params=_params(("arbitrary",)), name="inproj_bwd_dw")(dq, dk, dv, drest, x2d, g_norm)


def _adamw_update(w, g, m, v):
    nm = ADAM_B1 * m + (1.0 - ADAM_B1) * g
    nv = ADAM_B2 * v + (1.0 - ADAM_B2) * (g * g)
    m_hat = nm / (1.0 - ADAM_B1 ** ADAM_STEP)
    v_hat = nv / (1.0 - ADAM_B2 ** ADAM_STEP)
    return -ADAM_LR * (m_hat / (jnp.sqrt(v_hat) + ADAM_EPS) + ADAM_WD * w), nm, nv


def _adamw(w, g, m, v, name):
    R, C = w.shape
    br = max(r for r in range(8, 257, 8) if R % r == 0)

    def body(w_ref, g_ref, m_ref, v_ref, d_ref, nm_ref, nv_ref):
        d_ref[...], nm_ref[...], nv_ref[...] = _adamw_update(w_ref[...], g_ref[...], m_ref[...], v_ref[...])

    spec = pl.BlockSpec((br, C), lambda i: (i, 0))
    return pl.pallas_call(
        body, grid=(R // br,), in_specs=[spec] * 4, out_specs=[spec] * 3,
        out_shape=[jax.ShapeDtypeStruct((R, C), F32)] * 3,
        compiler_params=_params(("arbitrary",)), name=name)(w, g, m, v)


def _adamw_small(g_packed, ws, ms, vs):
    n = len(ws)

    def body(*refs):
        g_ref = refs[0]
        w_refs, m_refs, v_refs = refs[1:1 + n], refs[1 + n:1 + 2 * n], refs[1 + 2 * n:1 + 3 * n]
        outs = refs[1 + 3 * n:]
        off = 0
        for i, (_, used, padded) in enumerate(_SMALL_PARTS[:n]):
            g = g_ref[off:off + used, :]
            delta, nm, nv = _adamw_update(w_refs[i][...], g, m_refs[i][...], v_refs[i][...])
            outs[4 * i][...], outs[4 * i + 1][...], outs[4 * i + 2][...], outs[4 * i + 3][...] = g, delta, nm, nv
            off += padded

    outs = pl.pallas_call(
        body, out_shape=[jax.ShapeDtypeStruct(w.shape, F32) for w in ws for _ in range(4)],
        compiler_params=_params(), name="adamw_small")(g_packed, *ws, *ms, *vs)
    return [outs[4 * i:4 * i + 4] for i in range(n)]


def _place():
    x, y, c = lax.axis_index("x"), lax.axis_index("y"), lax.axis_index("c")
    chip = 2 * x + y
    peers = [(x, 1 - y), (1 - x, y), (1 - x, 1 - y)]
    peer_chip = [2 * px + py for px, py in peers]
    return x, y, c, chip, peers, peer_chip


def _remote(src, dst, send_sem, recv_sem, dev):
    return pltpu.make_async_remote_copy(src_ref=src, dst_ref=dst, send_sem=send_sem, recv_sem=recv_sem,
                                        device_id=dev, device_id_type=MESH)


def _ag_weights(weights, late=()):
    nw, nl = len(weights), len(late)

    def body(*refs):
        srcs, late_srcs = refs[:nw], refs[nw:nw + nl]
        outs, late_bf, late_land = (refs[nw + nl:2 * nw + nl], refs[2 * nw + nl:2 * nw + 2 * nl],
                                    refs[2 * nw + 2 * nl:2 * nw + 3 * nl])
        s_ici, r_ici, s_d2d, r_d2d = refs[2 * nw + 3 * nl:]
        x, y, c = lax.axis_index("x"), lax.axis_index("y"), lax.axis_index("c")
        chip = 2 * x + y
        sib = (x, y, 1 - c)
        first = ((x + 1 - c) % 2, (y + c) % 2)
        second = ((x + c) % 2, (y + 1 - c) % 2)
        first_chip, second_chip = 2 * first[0] + first[1], 2 * second[0] + second[1]
        diag_chip = 3 - chip
        for src, out in zip(srcs, outs):
            out[chip] = src[...].astype(BF16)

        def half(out, k, cc):
            rows = out.shape[1] // 2
            return out.at[k, pl.ds(pl.multiple_of(cc * rows, 16), rows), :]

        def ici(w, slot, out, k, dev):
            blk = half(out, k, c)
            return _remote(blk, blk, s_ici.at[nw * slot + w], r_ici.at[nw * slot + w], (dev[0], dev[1], c))

        def d2d(w, slot, out, k, cc):
            blk = half(out, k, cc)
            return _remote(blk, blk, s_d2d.at[nw * slot + w], r_d2d.at[nw * slot + w], sib)

        sent = []
        for slot, dev in enumerate((first, second)):
            for w, out in enumerate(outs):
                sent.append(ici(w, slot, out, chip, dev))
                sent[-1].start()
        for src, bf, land in zip(late_srcs, late_bf, late_land):
            bf[...] = src[...].astype(BF16)
            land[...] = jnp.zeros_like(land)
            land[chip] = bf[...]
        for slot, k, dev in ((0, first_chip, first), (1, second_chip, second), (2, diag_chip, second)):
            for w, out in enumerate(outs):
                ici(w, slot, out, k, dev).wait_recv()
                if slot == 0:
                    sent.append(ici(w, 2, out, k, second))
                    sent[-1].start()
                sent.append(d2d(w, slot, out, k, c))
                sent[-1].start()
        for slot, k in ((0, second_chip), (1, first_chip), (2, diag_chip)):
            for w, out in enumerate(outs):
                d2d(w, slot, out, k, 1 - c).wait_recv()
        for cp in sent:
            cp.wait_send()

    vmem = pl.BlockSpec(memory_space=pltpu.VMEM)
    outs = pl.pallas_call(
        body,
        out_shape=[jax.ShapeDtypeStruct((N_CHIPS,) + w.shape, BF16) for w in weights]
        + [jax.ShapeDtypeStruct(w.shape, BF16) for w in late]
        + [jax.ShapeDtypeStruct((N_CHIPS,) + w.shape, BF16) for w in late],
        in_specs=[vmem] * (nw + nl), out_specs=[vmem] * (nw + 2 * nl),
        scratch_shapes=[pltpu.SemaphoreType.DMA((3 * nw,))] * 4,
        compiler_params=pltpu.CompilerParams(vmem_limit_bytes=VMEM_LIMIT), name="ag_weights")(*weights, *late)
    return outs[:nw], outs[nw:nw + nl], outs[nw + nl:]


_HBM = pl.BlockSpec(memory_space=pltpu.HBM)
_SEM = pl.BlockSpec(memory_space=pltpu.SEMAPHORE)
_ANY = pl.BlockSpec(memory_space=pl.ANY)
_DATAFLOW = pltpu.SideEffectType.DATAFLOW_SIDE_EFFECTING


def _in_hbm(a):
    return pltpu.with_memory_space_constraint(a, pltpu.HBM)


def _exchange_copies(gather, srcs, lands, send_sems, recv_sems):
    nw = len(srcs)
    x, y, c, chip, peers, peer_chip = _place()
    pairs = []
    for m, (px, py) in enumerate(peers):
        for w in range(nw):
            sems = (send_sems.at[nw * m + w], recv_sems.at[nw * m + w], (px, py, c))
            if gather:
                pairs.append((_remote(srcs[w], lands[w].at[chip], *sems),
                              _remote(srcs[w], lands[w].at[peer_chip[m]], *sems)))
            else:
                pairs.append((_remote(srcs[w].at[m], lands[w].at[m], *sems),) * 2)
    return pairs


def _exchange_start(gather, srcs, after, name, lands=None):
    nw = len(srcs)
    n_copies = 3 * nw

    def body(*refs):
        send_sems, recv_sems = refs[2 * nw + 1], refs[2 * nw + 2]
        for start, _ in _exchange_copies(gather, refs[:nw], refs[nw:2 * nw], send_sems, recv_sems):
            start.start()
        refs[-1][...] = jnp.zeros_like(refs[-1])

    if lands is None:
        lands = [lax.empty(((N_CHIPS,) + s.shape) if gather else s.shape, s.dtype) for s in srcs]
    lands = [_in_hbm(l) for l in lands]
    return pl.pallas_call(
        body, name=name,
        out_shape=(pltpu.SemaphoreType.DMA((n_copies,)), pltpu.SemaphoreType.DMA((n_copies,)))
        + tuple(pltpu.HBM(s.shape, s.dtype) for s in srcs)
        + tuple(pltpu.HBM(l.shape, l.dtype) for l in lands)
        + (jax.ShapeDtypeStruct((8, LANES), F32),),
        in_specs=[_HBM] * (2 * nw) + [_ANY],
        out_specs=(_SEM, _SEM) + (_HBM,) * (2 * nw) + (pl.BlockSpec(memory_space=pltpu.VMEM),),
        input_output_aliases={i: 2 + i for i in range(2 * nw)},
        compiler_params=pltpu.CompilerParams(has_side_effects=_DATAFLOW),
    )(*[_in_hbm(s) for s in srcs], *lands, after)


def _exchange_wait(gather, started, after, name):
    nw = (len(started) - 3) // 2
    send_sems, recv_sems = started[0], started[1]
    thru = started[2:2 + 2 * nw]

    def body(*refs):
        for _, arrival in _exchange_copies(gather, refs[:nw], refs[nw:2 * nw], refs[2 * nw], refs[2 * nw + 1]):
            arrival.wait_send()
            arrival.wait_recv()

    outs = pl.pallas_call(
        body, name=name,
        out_shape=tuple(pltpu.HBM(t.shape, t.dtype) for t in thru),
        in_specs=[_HBM] * (2 * nw) + [_SEM, _SEM, _ANY], out_specs=(_HBM,) * (2 * nw),
        input_output_aliases={i: i for i in range(2 * nw)},
        compiler_params=pltpu.CompilerParams(has_side_effects=_DATAFLOW),
    )(*thru, send_sems, recv_sems, after)
    return outs[nw:]


def _reduce_first(stacks):
    ns = len(stacks)
    halves = [s.shape[1] // 2 for s in stacks]
    row_block = 32

    def body(*refs):
        gs, sends, owns = refs[:ns], refs[ns:2 * ns], refs[2 * ns:3 * ns]
        ras, s_sem, r_sem = refs[3 * ns:4 * ns], refs[4 * ns], refs[4 * ns + 1]
        x, y, c, chip, peers, peer_chip = _place()
        swaps = []
        for w in range(ns):
            theirs = gs[w].at[:, pl.ds(pl.multiple_of((1 - c) * halves[w], 8), halves[w]), :]
            swaps.append(_remote(theirs, ras[w], s_sem.at[w], r_sem.at[w], (x, y, 1 - c)))
            swaps[-1].start()
        for w in range(ns):
            n = halves[w]
            swaps[w].wait_recv()

            def sums(i, carry, w=w, n=n):
                r0 = pl.multiple_of(i * row_block, row_block)
                blk = pl.ds(r0, row_block)
                mine = pl.ds(pl.multiple_of(c * n + r0, 8), row_block)
                for m in range(3):
                    sends[w][m, blk, :] = (gs[w][peer_chip[m], mine, :] + ras[w][peer_chip[m], blk, :]).astype(BF16)
                owns[w][blk, :] = gs[w][chip, mine, :] + ras[w][chip, blk, :]
                return carry
            lax.fori_loop(0, n // row_block, sums, 0)
        for cp in swaps:
            cp.wait_send()

    vmem = pl.BlockSpec(memory_space=pltpu.VMEM)
    outs = pl.pallas_call(
        body,
        out_shape=[jax.ShapeDtypeStruct((3, n, s.shape[2]), BF16) for n, s in zip(halves, stacks)]
        + [jax.ShapeDtypeStruct((n, s.shape[2]), F32) for n, s in zip(halves, stacks)],
        in_specs=[vmem] * ns, out_specs=[vmem] * (2 * ns),
        scratch_shapes=[pltpu.VMEM((N_CHIPS, n, s.shape[2]), F32) for n, s in zip(halves, stacks)]
        + [pltpu.SemaphoreType.DMA((ns,)), pltpu.SemaphoreType.DMA((ns,))],
        compiler_params=pltpu.CompilerParams(vmem_limit_bytes=VMEM_LIMIT), name="reduce_first")(*stacks)
    return outs[:ns], outs[ns:]


def _reduce_last(owns, landed, g_small):
    ns = len(owns)
    row_block = 32
    hs = SMALL_ROWS // 2

    def body(*refs):
        own_refs, land_refs, gsm_ref = refs[:ns], refs[ns:2 * ns], refs[2 * ns]
        out_refs, osm_ref = refs[2 * ns + 1:3 * ns + 1], refs[3 * ns + 1]
        ra_sm, p_sm, s_sem, r_sem, sm_s, sm_r = refs[3 * ns + 2:]
        x, y, c, chip, peers, peer_chip = _place()
        sib = (x, y, 1 - c)
        half = lambda cc: pl.ds(pl.multiple_of(cc * hs, 8), hs)
        sm_a = _remote(gsm_ref.at[half(1 - c), :], ra_sm, sm_s.at[0], sm_r.at[0], sib)
        sm_a.start()
        swaps = [sm_a]
        for w in range(ns):
            n = own_refs[w].shape[0]

            def total(i, carry, w=w, n=n):
                r0 = pl.multiple_of(i * row_block, row_block)
                blk = pl.ds(r0, row_block)
                acc = own_refs[w][blk, :]
                for m in range(3):
                    acc = acc + land_refs[w][m, blk, :].astype(F32)
                out_refs[w][pl.ds(pl.multiple_of(c * n + r0, 8), row_block), :] = acc
                return carry
            lax.fori_loop(0, n // row_block, total, 0)
            mine = out_refs[w].at[pl.ds(pl.multiple_of(c * n, 8), n), :]
            swaps.append(_remote(mine, mine, s_sem.at[w], r_sem.at[w], sib))
            swaps[-1].start()
        sm_a.wait_recv()
        p_sm[chip] = gsm_ref[half(c), :] + ra_sm[...]
        for m, (px, py) in enumerate(peers):
            swaps.append(_remote(p_sm.at[chip], p_sm.at[chip], sm_s.at[1 + m], sm_r.at[1 + m], (px, py, c)))
            swaps[-1].start()
        for w in range(ns):
            n = own_refs[w].shape[0]
            theirs = out_refs[w].at[pl.ds(pl.multiple_of((1 - c) * n, 8), n), :]
            _remote(theirs, theirs, s_sem.at[w], r_sem.at[w], sib).wait_recv()
        for m, (px, py) in enumerate(peers):
            _remote(p_sm.at[chip], p_sm.at[peer_chip[m]], sm_s.at[1 + m], sm_r.at[1 + m], (px, py, c)).wait_recv()
        osm_ref[half(c), :] = (p_sm[0] + p_sm[1]) + (p_sm[2] + p_sm[3])
        swaps.append(_remote(osm_ref.at[half(c), :], osm_ref.at[half(c), :], sm_s.at[4], sm_r.at[4], sib))
        swaps[-1].start()
        _remote(osm_ref.at[half(1 - c), :], osm_ref.at[half(1 - c), :], sm_s.at[4], sm_r.at[4], sib).wait_recv()
        for cp in swaps:
            cp.wait_send()

    vmem = pl.BlockSpec(memory_space=pltpu.VMEM)
    return pl.pallas_call(
        body, out_shape=[jax.ShapeDtypeStruct((2 * o.shape[0], o.shape[1]), F32) for o in owns]
        + [jax.ShapeDtypeStruct((SMALL_ROWS, LANES), F32)],
        in_specs=[vmem] * (2 * ns + 1), out_specs=[vmem] * (ns + 1),
        scratch_shapes=[pltpu.VMEM((hs, LANES), F32), pltpu.VMEM((N_CHIPS, hs, LANES), F32),
                        pltpu.SemaphoreType.DMA((ns,)), pltpu.SemaphoreType.DMA((ns,)),
                        pltpu.SemaphoreType.DMA((5,)), pltpu.SemaphoreType.DMA((5,))],
        compiler_params=pltpu.CompilerParams(vmem_limit_bytes=VMEM_LIMIT),
        name="reduce_last")(*owns, *landed, g_small)


_SMALL_PARTS = (("g_norm", 8, 8), ("w_s", 512, 512), ("b_s", 4, 8), ("g_v", 2, 8), ("g_mem", 8, 8),
                ("g_final", 8, 8), ("loss", 1, 8))
_LOSS_ROW = SMALL_ROWS - 8
assert sum(p for _, _, p in _SMALL_PARTS) == SMALL_ROWS


def _pack_small(parts, loss=None):
    loss_row = jnp.zeros((1, LANES), F32) if loss is None else jnp.broadcast_to(loss.reshape(1, 1), (1, LANES))
    rows = []
    for (name, used, padded), p in zip(_SMALL_PARTS, list(parts) + [loss_row]):
        p = p.reshape(used, LANES)
        if padded > used:
            p = jnp.pad(p, ((0, padded - used), (0, 0)))
        rows.append(p)
    return jnp.concatenate(rows, axis=0)


def _local_step(x, mem, target, g_norm, w_in, w_s, b_s, g_v, g_mem, late_weights, g_final,
                fwd_token=None, on_dw=None):
    B, S, _ = x.shape
    x2d = x.reshape(B * S, D_MODEL)
    t2d = target.reshape(B * S, D_MODEL)
    mem2d = mem.reshape(B * N_MEM, D_MODEL)

    proj = _inproj_fwd(x2d, g_norm, w_in, after=() if fwd_token is None else (fwd_token,))
    w_kv, w_out = late_weights(proj)
    kv = _kv_fwd(mem2d, g_mem, w_kv)
    a, lse = _attn_fwd(proj, B, S)
    w_sT = jnp.swapaxes(w_s, 1, 2)
    b_tab = jnp.repeat(b_s.T, HEAD_DIM, axis=1)
    (dx2, da, drest, loss, d_wout, d_ws, d_bs, d_gv, d_gf, dkv) = _mid(
        x2d, t2d, a, proj, kv, w_s, w_sT, b_tab, g_v, w_out, g_final, B, S)
    d_wkv, d_gmem = _kv_bwd(mem2d, g_mem, w_kv, dkv)
    dq, dk, dv = _attn_bwd(proj, a, lse, da, B, S)
    d_win = _inproj_bwd_dw(dq, dk, dv, drest, x2d, g_norm)
    grad_x, d_gnorm = _inproj_bwd_dx(dq, dk, dv, drest, x2d, dx2, g_norm, w_in,
                                     after=() if on_dw is None else (on_dw(d_win, d_wkv, d_wout),))
    d_bs = d_bs[:, :N_SGU_GROUPS].T
    return (loss[0, 0], grad_x.reshape(B, S, D_MODEL),
            dict(g_norm=d_gnorm, w_in=d_win, w_s=d_ws, b_s=d_bs, g_v=d_gv, g_mem=d_gmem, w_kv=d_wkv,
                 w_out=d_wout, g_final=d_gf))


def kernel(x, mem, g_norm, w_in, w_sgu_spatial, b_sgu_spatial, g_sgu_v, g_mem, w_mem_kv, w_out, g_final, loss_target, m_g_norm, m_w_in, m_w_sgu_spatial, m_b_sgu_spatial, m_g_sgu_v, m_g_mem, m_w_mem_kv, m_w_out, m_g_final, v_g_norm, v_w_in, v_w_sgu_spatial, v_b_sgu_spatial, v_g_sgu_v, v_g_mem, v_w_mem_kv, v_w_out, v_g_final):
    t = lambda w: jnp.swapaxes(w[0], 0, 1)
    (win_all,), late_shards, late_lands = _ag_weights([t(w_in)], [w_mem_kv[0], w_out[0]])
    w_in_full = win_all.reshape(-1, win_all.shape[-1])
    late = _exchange_start(True, list(late_shards), win_all, "gather_late_start", lands=late_lands)

    def late_weights(proj):
        return [z.reshape(-1, z.shape[-1]) for z in _exchange_wait(True, late, proj, "gather_late_wait")]

    scatter = {}

    def on_dw(d_win, d_wkv, d_wout):
        stacks = [d_win.reshape((N_CHIPS, w_in.shape[2], w_in.shape[1])),
                  d_wkv.reshape((N_CHIPS,) + w_mem_kv.shape[1:]), d_wout.reshape((N_CHIPS,) + w_out.shape[1:])]
        sends, scatter["own"] = _reduce_first(stacks)
        scatter["started"] = _exchange_start(False, list(sends), scatter["own"][0], "scatter_start")
        return scatter["started"][-1]

    loss, grad_x, g = _local_step(
        x, mem, loss_target, g_norm, w_in_full, w_sgu_spatial[0], b_sgu_spatial[0], g_sgu_v, g_mem,
        late_weights, g_final.reshape(1, D_MODEL), fwd_token=late[-1], on_dw=on_dw)

    small_names = ("g_norm", "w_s", "b_s", "g_v", "g_mem", "g_final")
    g_small = _pack_small([g[n] for n in small_names], loss)
    landed = _exchange_wait(False, scatter["started"], g_small, "scatter_wait")
    gr_in, gr_kv, gr_out, gr_small = _reduce_last(scatter["own"], landed, g_small)
    loss = gr_small[_LOSS_ROW, 0]

    small_w = (g_norm, w_sgu_spatial, b_sgu_spatial, g_sgu_v, g_mem, g_final)
    small_m = (m_g_norm, m_w_sgu_spatial, m_b_sgu_spatial, m_g_sgu_v, m_g_mem, m_g_final)
    small_v = (v_g_norm, v_w_sgu_spatial, v_b_sgu_spatial, v_g_sgu_v, v_g_mem, v_g_final)
    rows = lambda ws: [w.reshape(-1, LANES) for w in ws]
    small = [[z.reshape(w.shape) for z in four]
             for w, four in zip(small_w, _adamw_small(gr_small, rows(small_w), rows(small_m), rows(small_v)))]
    d_in, nm_in, nv_in = _adamw(t(w_in), gr_in, t(m_w_in), t(v_w_in), "adamw_w_in")
    gr_in, d_in, nm_in, nv_in = [jnp.swapaxes(z, 0, 1) for z in (gr_in, d_in, nm_in, nv_in)]
    d_kv, nm_kv, nv_kv = _adamw(w_mem_kv[0], gr_kv, m_w_mem_kv[0], v_w_mem_kv[0], "adamw_w_kv")
    d_out, nm_out, nv_out = _adamw(w_out[0], gr_out, m_w_out[0], v_w_out[0], "adamw_w_out")

    def leaves(kind, big_in, big_kv, big_out):
        s_norm, s_ws, s_bs, s_gv, s_gmem, s_gf = [four[kind] for four in small]
        return [s_norm, big_in[None], s_ws, s_bs, s_gv, s_gmem, big_kv[None], big_out[None], s_gf]

    return (loss, grad_x, *leaves(0, gr_in, gr_kv, gr_out), *leaves(1, d_in, d_kv, d_out),
            *leaves(2, nm_in, nm_kv, nm_out), *leaves(3, nv_in, nv_kv, nv_out))
```

```python
import functools

import jax
import jax.numpy as jnp
from jax import lax
from jax.experimental import pallas as pl
from jax.experimental.pallas import tpu as pltpu

F32 = jnp.float32
BF16 = jnp.bfloat16
MESH = pl.DeviceIdType.MESH

D_MODEL = 1024
ATTN_WIDTH = 512
SGU_WIDTH = 256
MEM_WIDTH = 256
N_MEM = 256
IN_COLS = 3328
QKV_COLS = 3 * ATTN_WIDTH
REST_COLS = IN_COLS - QKV_COLS
SGU_CHUNK = 128
N_SGU_GROUPS = 4
EPS = 1e-6
NEG_INF = -1e30
DILATIONS = (1, 4, 16)
RADIUS = 64
Q_BLOCK = 128
LANES = 128
HEAD_DIM = 64

ADAM_LR = 0.001
ADAM_B1 = 0.9
ADAM_B2 = 0.999
ADAM_EPS = 1e-08
ADAM_WD = 0.01
ADAM_STEP = 10

N_CHIPS = 4
VMEM_LIMIT = 56 * 1024 * 1024
SMALL_ROWS = 560


def _params(sem=None, vmem=VMEM_LIMIT):
    return pltpu.CompilerParams(dimension_semantics=sem, vmem_limit_bytes=vmem)


def _nn(a, b):
    return jnp.dot(a, b, preferred_element_type=F32)


def _nt(a, b):
    return lax.dot_general(a, b, (((1,), (1,)), ((), ())), preferred_element_type=F32)


def _tn(a, b):
    return lax.dot_general(a, b, (((0,), (0,)), ((), ())), preferred_element_type=F32)


def _rms(x):
    r = lax.rsqrt(jnp.mean(x * x, axis=-1, keepdims=True) + EPS)
    return r, x * r


def _head_masks():
    lane = lax.broadcasted_iota(jnp.int32, (1, LANES), 1)
    lo = lane < HEAD_DIM
    return lo, (lo.astype(F32), (~lo).astype(F32))


def _silu_parts(z):
    s = jax.nn.sigmoid(z)
    return z * s, s * (1.0 + z * (1.0 - s))


def _gelu_parts(x):
    c = 0.7978845608028654
    x2 = x * x
    s = jax.nn.sigmoid((2.0 * c) * (x + 0.044715 * (x * x2)))
    return x * s, s * (1.0 + x * (1.0 - s) * ((2.0 * c) * (1.0 + 3.0 * 0.044715 * x2)))


def _after(tokens):
    return [pl.BlockSpec(memory_space=pl.ANY)] * len(tokens)


def _inproj_fwd(x2d, g_norm, w_in_t, after=()):
    T = x2d.shape[0]
    tm = 512

    def body(x_ref, g_ref, w_ref, *rest):
        o_ref = rest[-1]
        _, xh = _rms(x_ref[...])
        h = (xh * g_ref[...]).astype(BF16)
        o_ref[...] = _nt(h, w_ref[...])

    return pl.pallas_call(
        body, grid=(T // tm,),
        in_specs=[pl.BlockSpec((tm, D_MODEL), lambda i: (i, 0)),
                  pl.BlockSpec((1, D_MODEL), lambda i: (0, 0)),
                  pl.BlockSpec((IN_COLS, D_MODEL), lambda i: (0, 0))] + _after(after),
        out_specs=pl.BlockSpec((tm, IN_COLS), lambda i: (i, 0)),
        out_shape=jax.ShapeDtypeStruct((T, IN_COLS), F32),
        compiler_params=_params(("arbitrary",)), name="inproj_fwd")(x2d, g_norm, w_in_t, *after)


def _kv_fwd(mem2d, g_mem, w_kv):
    Tm = mem2d.shape[0]

    def body(m_ref, g_ref, w_ref, o_ref):
        _, mh = _rms(m_ref[...])
        o_ref[...] = _nn((mh * g_ref[...]).astype(BF16), w_ref[...])

    return pl.pallas_call(
        body, out_shape=jax.ShapeDtypeStruct((Tm, 2 * MEM_WIDTH), F32),
        compiler_params=_params(), name="kv_fwd")(mem2d, g_mem, w_kv)


def _kv_bwd(mem2d, g_mem, w_kv, dkv):
    Tm = mem2d.shape[0]

    def body(m_ref, g_ref, w_ref, dkv_ref, dw_ref, dg_ref):
        _, mh = _rms(m_ref[...])
        memn = (mh * g_ref[...]).astype(BF16)
        dkvb = dkv_ref[...].astype(BF16)
        dw_ref[...] = _tn(memn, dkvb)
        dmemn = _nt(dkvb, w_ref[...])
        dg_ref[...] = jnp.sum(dmemn * mh, axis=0, keepdims=True)

    return pl.pallas_call(
        body, out_shape=(jax.ShapeDtypeStruct((D_MODEL, 2 * MEM_WIDTH), F32),
                         jax.ShapeDtypeStruct((1, D_MODEL), F32)),
        compiler_params=_params(), name="kv_bwd")(mem2d, g_mem, w_kv, dkv)


def _attn_geometry(S):
    geom = []
    for d in DILATIONS:
        L = S // d
        assert L % Q_BLOCK == 0
        geom.append((d, L, min(2 * Q_BLOCK, L), L // Q_BLOCK))
    return geom


def _init_bias(bias_scr, geom, hp):
    row = lax.broadcasted_iota(jnp.int32, (Q_BLOCK, 2 * Q_BLOCK), 0)
    col = lax.broadcasted_iota(jnp.int32, (Q_BLOCK, 2 * Q_BLOCK), 1)
    for j in (0, 1):
        bits = (126 - (2 * hp + j)) * (1 << 23)
        slope = lax.bitcast_convert_type(jnp.full((1, 1), bits, jnp.int32), F32)
        for di, (d, _, _, _) in enumerate(geom):
            for cls, off in enumerate((0, -RADIUS, -2 * RADIUS)):
                dist = jnp.abs(col - row + off)
                bias_scr[di * 6 + cls * 2 + j] = jnp.where(
                    dist <= RADIUS, -(slope * float(d)) * dist.astype(F32), NEG_INF)


SPLIT = 4
COPY_ROWS = 256


def _by4_rows(S, step):
    per_class = S // SPLIT // COPY_ROWS
    r, j = step // per_class, step % per_class
    return (pl.ds(r + SPLIT * j * COPY_ROWS, COPY_ROWS, stride=SPLIT),
            pl.ds(pl.multiple_of(r * (S // SPLIT) + j * COPY_ROWS, COPY_ROWS), COPY_ROWS))


def _to_by4(src, dst, S):
    def step(i, carry):
        natural, by4 = _by4_rows(S, i)
        dst[by4, :] = src[natural, :]
        return carry
    lax.fori_loop(0, S // COPY_ROWS, step, 0)


def _block_slices(d, L, KW, nqb, r, qb, S):
    qs = qb * Q_BLOCK
    ks = jnp.clip(qs - RADIUS, 0, L - KW)
    cls = jnp.where(qb == 0, 0, jnp.where(qb == nqb - 1, 2, 1))
    if d == 1:
        qsl = pl.ds(pl.multiple_of(qs, Q_BLOCK), Q_BLOCK)
        ksl = pl.ds(pl.multiple_of(ks, RADIUS), KW)
    elif d == SPLIT:
        qsl = pl.ds(pl.multiple_of(r * L + qs, Q_BLOCK), Q_BLOCK)
        ksl = pl.ds(pl.multiple_of(r * L + ks, RADIUS), KW)
    else:
        sub = d // SPLIT
        base = (r % SPLIT) * (S // SPLIT) + r // SPLIT
        qsl = pl.ds(base + qs * sub, Q_BLOCK, stride=sub)
        ksl = pl.ds(base + ks * sub, KW, stride=sub)
    return qsl, ksl, cls


def _for_groups(geom, S, group, fn):
    for di, (d, L, KW, nqb) in enumerate(geom):
        assert (d * nqb) % group == 0

        def step(it, carry, di=di, d=d, L=L, KW=KW, nqb=nqb):
            slices = []
            for g in range(group):
                i = it * group + g
                slices.append(_block_slices(d, L, KW, nqb, i // nqb, i % nqb, S))
            fn(di, KW, slices)
            return carry
        lax.fori_loop(0, d * nqb // group, step, 0)


def _attn_fwd(proj, B, S):
    T = B * S
    geom = _attn_geometry(S)
    n_pairs = ATTN_WIDTH // LANES

    def body(q_ref, k_ref, v_ref, a_ref, lse_ref, bias_scr, q4, k4, v4, *per_dilation):
        o_scr, m_scr, l_scr = per_dilation[0:3], per_dilation[3:6], per_dilation[6:9]
        lo, hm = _head_masks()
        _init_bias(bias_scr, geom, pl.program_id(1))
        for src, dst in ((q_ref, q4), (k_ref, k4), (v_ref, v4)):
            _to_by4(src, dst, S)

        def group(di, KW, slices):
            chains = [(g, j) for g in range(len(slices)) for j in (0, 1)]
            q_src, k_src, v_src = (q_ref, k_ref, v_ref) if di == 0 else (q4, k4, v4)
            q = [q_src[qsl, :] for qsl, _, _ in slices]
            kw = [k_src[ksl, :].astype(BF16) for _, ksl, _ in slices]
            vw = [v_src[ksl, :].astype(BF16) for _, ksl, _ in slices]
            s = {(g, j): _nt((q[g] * (hm[j] * 0.125)).astype(BF16), kw[g])
                 + bias_scr[di * 6 + slices[g][2] * 2 + j, :, pl.ds(0, KW)] for g, j in chains}
            m = {c: jnp.max(s[c], axis=1, keepdims=True) for c in chains}
            p = {c: jnp.exp(s[c] - m[c]) for c in chains}
            l = {c: jnp.sum(p[c], axis=1, keepdims=True) for c in chains}
            o = {(g, j): _nn(p[(g, j)].astype(BF16), vw[g]) for g, j in chains}
            for g, (qsl, _, _) in enumerate(slices):
                o_scr[di][qsl, :] = jnp.where(lo, o[(g, 0)], o[(g, 1)])
                m_scr[di][qsl, :] = jnp.where(lo, m[(g, 0)], m[(g, 1)])
                l_scr[di][qsl, :] = jnp.where(lo, l[(g, 0)], l[(g, 1)])

        _for_groups(geom, S, 8, group)

        def combine(i, carry):
            natural, by4 = _by4_rows(S, i)
            rows = [natural, by4, by4]
            ms = [m_scr[di][rows[di], :] for di in range(3)]
            mx = jnp.maximum(jnp.maximum(ms[0], ms[1]), ms[2])
            num = 0.0
            den = 0.0
            for di in range(3):
                w = jnp.exp(ms[di] - mx)
                num = num + w * o_scr[di][rows[di], :]
                den = den + w * l_scr[di][rows[di], :]
            a_ref[natural, :] = num / den
            lse_ref[natural, :] = mx + jnp.log(den)
            return carry

        lax.fori_loop(0, S // COPY_ROWS, combine, 0)

    blk = lambda off: pl.BlockSpec((S, LANES), lambda b, h, off=off: (b, off + h))
    out_blk = pl.BlockSpec((S, LANES), lambda b, h: (b, h))
    return pl.pallas_call(
        body, grid=(B, n_pairs),
        in_specs=[blk(0), blk(n_pairs), blk(2 * n_pairs)],
        out_specs=[out_blk, out_blk],
        out_shape=[jax.ShapeDtypeStruct((T, ATTN_WIDTH), F32)] * 2,
        scratch_shapes=[pltpu.VMEM((18, Q_BLOCK, 2 * Q_BLOCK), F32)] + [pltpu.VMEM((S, LANES), F32)] * 12,
        compiler_params=_params(("arbitrary", "arbitrary")), name="attn_fwd")(proj, proj, proj)


def _attn_bwd(proj, a, lse, da, B, S):
    T = B * S
    geom = _attn_geometry(S)
    n_pairs = ATTN_WIDTH // LANES

    def body(q_ref, k_ref, v_ref, a_ref, lse_ref, do_ref, dq_ref, dk_ref, dv_ref, bias_scr, *scr):
        acc = (scr[0:3], scr[3:6])
        natural_in = (q_ref, k_ref, v_ref, a_ref, lse_ref, do_ref)
        by4_in = scr[6:12]
        _, hm = _head_masks()
        _init_bias(bias_scr, geom, pl.program_id(1))
        for ref in scr[0:6]:
            ref[...] = jnp.zeros_like(ref)
        for src, dst in zip(natural_in, by4_in):
            _to_by4(src, dst, S)

        def group(di, KW, slices):
            n = len(slices)
            chains = [(g, j) for g in range(n) for j in (0, 1)]
            q_src, k_src, v_src, a_src, lse_src, do_src = natural_in if di == 0 else by4_in
            dq_scr, dk_scr, dv_scr = acc[0 if di == 0 else 1]
            q = [q_src[qsl, :] for qsl, _, _ in slices]
            do = [do_src[qsl, :] for qsl, _, _ in slices]
            doa = [do[g] * a_src[slices[g][0], :] for g in range(n)]
            lse_q = [lse_src[qsl, :] for qsl, _, _ in slices]
            kw = [k_src[ksl, :].astype(BF16) for _, ksl, _ in slices]
            vw = [v_src[ksl, :].astype(BF16) for _, ksl, _ in slices]
            qj = {(g, j): (q[g] * (hm[j] * 0.125)).astype(BF16) for g, j in chains}
            doj = {(g, j): (do[g] * hm[j]).astype(BF16) for g, j in chains}
            s = {(g, j): _nt(qj[(g, j)], kw[g])
                 + bias_scr[di * 6 + slices[g][2] * 2 + j, :, pl.ds(0, KW)] for g, j in chains}
            dp = {(g, j): _nt(doj[(g, j)], vw[g]) for g, j in chains}
            dsum = {(g, j): jnp.sum(doa[g] * hm[j], axis=1, keepdims=True) for g, j in chains}
            p = {(g, j): jnp.exp(s[(g, j)] - lse_q[g][:, HEAD_DIM * j:HEAD_DIM * j + 1]) for g, j in chains}
            ds = {c: (p[c] * (dp[c] - dsum[c])).astype(BF16) for c in chains}
            pb = {c: p[c].astype(BF16) for c in chains}
            dq = [_nn(ds[(g, 0)], kw[g]) * (hm[0] * 0.125) + _nn(ds[(g, 1)], kw[g]) * (hm[1] * 0.125)
                  for g in range(n)]
            both = lambda t, g: jnp.concatenate([t[(g, 0)], t[(g, 1)]], axis=0)
            dkw = [_tn(both(ds, g), both(qj, g)) for g in range(n)]
            dvw = [_tn(both(pb, g), both(doj, g)) for g in range(n)]
            for g, (qsl, ksl, _) in enumerate(slices):
                dq_scr[qsl, :] = dq_scr[qsl, :] + dq[g]
                dk_scr[ksl, :] = dk_scr[ksl, :] + dkw[g]
                dv_scr[ksl, :] = dv_scr[ksl, :] + dvw[g]

        _for_groups(geom, S, 4, group)

        def merge(i, carry):
            natural, by4 = _by4_rows(S, i)
            for nat, split in zip(*acc):
                nat[natural, :] = nat[natural, :] + split[by4, :]
            return carry
        lax.fori_loop(0, S // COPY_ROWS, merge, 0)
        for out, nat in zip((dq_ref, dk_ref, dv_ref), acc[0]):
            out[...] = nat[...].astype(BF16)

    blk = lambda off: pl.BlockSpec((S, LANES), lambda b, h, off=off: (b, off + h))
    return pl.pallas_call(
        body, grid=(B, n_pairs),
        in_specs=[blk(0), blk(n_pairs), blk(2 * n_pairs), blk(0), blk(0), blk(0)],
        out_specs=[blk(0), blk(0), blk(0)],
        out_shape=[jax.ShapeDtypeStruct((T, ATTN_WIDTH), BF16)] * 3,
        scratch_shapes=[pltpu.VMEM((18, Q_BLOCK, 2 * Q_BLOCK), F32)] + [pltpu.VMEM((S, LANES), F32)] * 12,
        compiler_params=_params(("arbitrary", "arbitrary")), name="attn_bwd")(proj, proj, proj, a, lse, da)


def _mid(x2d, t2d, a, proj, kv, w_s, w_sT, b_tab, g_v, w_out, g_final, B, S):
    T = B * S
    tm = 512
    nt = S // tm
    halves = 2
    hrows = tm // halves

    def body(x_ref, t_ref, a_ref, za_ref, ub_ref, vb_ref, zb_ref, qm_ref, zm_ref, kv_ref,
              ws_ref, wsT_ref, btab_ref, gv_ref, wout_ref, gf_ref,
              dx2_ref, da_ref, drest_ref, loss_ref, dwout_ref, dws_ref, dbs_ref, dgv_ref, dgf_ref, dkv_ref,
              dbtab_scr):
        b = pl.program_id(0)
        t = pl.program_id(1)
        first = jnp.logical_and(b == 0, t == 0)
        last = jnp.logical_and(b == B - 1, t == nt - 1)
        _, hm = _head_masks()
        lane_g = lax.broadcasted_iota(jnp.int32, (1, SGU_WIDTH), 1) // HEAD_DIM
        gm = [(lane_g == g).astype(F32) for g in range(N_SGU_GROUPS)]
        H = range(halves)
        rows = [pl.ds(h * hrows, hrows) for h in H]
        ld = lambda ref: [ref[r, :] for r in rows]
        cat = lambda parts, axis: jnp.concatenate(parts, axis=axis)
        chunks = [slice(ci * SGU_CHUNK, (ci + 1) * SGU_CHUNK) for ci in range(hrows // SGU_CHUNK)]
        pairs = [slice(pr * LANES, (pr + 1) * LANES) for pr in range(2)]
        heads = [(pr, j) for pr in range(2) for j in (0, 1)]

        @pl.when(first)
        def _():
            loss_ref[...] = jnp.zeros_like(loss_ref)
            dwout_ref[...] = jnp.zeros_like(dwout_ref)
            dws_ref[...] = jnp.zeros_like(dws_ref)
            dbs_ref[...] = jnp.zeros_like(dbs_ref)
            dgv_ref[...] = jnp.zeros_like(dgv_ref)
            dgf_ref[...] = jnp.zeros_like(dgf_ref)
            dbtab_scr[...] = jnp.zeros_like(dbtab_scr)

        @pl.when(t == 0)
        def _():
            dkv_ref[...] = jnp.zeros_like(dkv_ref)

        a_val = ld(a_ref)
        sil_a = [_silu_parts(z) for z in ld(za_ref)]
        gated_a = [s[0] * a for s, a in zip(sil_a, a_val)]
        u = [_gelu_parts(z) for z in ld(ub_ref)]
        vv = [_gelu_parts(z) for z in ld(vb_ref)]
        vnorm = [_rms(v[0]) for v in vv]
        gv = gv_ref[...]
        vn = [(n[1] * gv).astype(BF16) for n in vnorm]
        w_cat = cat([ws_ref[g].astype(BF16) for g in range(N_SGU_GROUPS)], 1)
        wT_cat = cat([wsT_ref[g].astype(BF16) for g in range(N_SGU_GROUPS)], 1)
        gmb = [m.astype(BF16) for m in gm]
        by_group = lambda chunk: cat([chunk * gmb[g] for g in range(N_SGU_GROUPS)], 0)
        btab = btab_ref[...]
        mixed = [cat([btab + _nn(w_cat, by_group(vn[h][c, :])) for c in chunks], 0) for h in H]
        sg = [u[h][0] * mixed[h] for h in H]
        sil_b = [_silu_parts(z) for z in ld(zb_ref)]
        gated_b = [sil_b[h][0] * sg[h] for h in H]

        kvv = kv_ref[...].astype(BF16)
        kp = [kvv[:, p] for p in pairs]
        vp = [kvv[:, MEM_WIDTH + pr * LANES:MEM_WIDTH + (pr + 1) * LANES] for pr in range(2)]
        qm = ld(qm_ref)
        qj = {(h, pr, j): (qm[h][:, pairs[pr]] * (hm[j] * 0.125)).astype(BF16) for h in H for pr, j in heads}
        sc = {k: _nt(qj[k], kp[k[1]]) for k in qj}
        ex = {k: jnp.exp(sc[k] - jnp.max(sc[k], axis=1, keepdims=True)) for k in qj}
        prob = {k: ex[k] * (1.0 / jnp.sum(ex[k], axis=1, keepdims=True)) for k in qj}
        probb = {k: prob[k].astype(BF16) for k in qj}
        mo = [cat([sum(_nn(probb[(h, pr, j)], vp[pr]) * hm[j] for j in (0, 1)) for pr in range(2)], 1) for h in H]
        sil_m = [_silu_parts(z) for z in ld(zm_ref)]
        gated_m = [sil_m[h][0] * mo[h] for h in H]

        gated = [cat([gated_a[h], gated_b[h], gated_m[h]], 1).astype(BF16) for h in H]
        wout = wout_ref[...]
        x_in = ld(x_ref)
        x2 = [x_in[h] + _nn(gated[h], wout) for h in H]
        fin = [_rms(z) for z in x2]
        gf = gf_ref[...]
        tgt = ld(t_ref)
        err = [fin[h][1] * gf - tgt[h] for h in H]
        loss_ref[...] += sum(jnp.sum(e * e) for e in err) * (0.5 / D_MODEL)

        dy = [e * (1.0 / D_MODEL) for e in err]
        dgf_ref[...] += sum(jnp.sum(dy[h] * fin[h][1], axis=0, keepdims=True) for h in H)
        gdy = [d * gf for d in dy]
        dx2 = [fin[h][0] * (gdy[h] - fin[h][1] * jnp.mean(gdy[h] * fin[h][1], axis=1, keepdims=True)) for h in H]
        for h in H:
            dx2_ref[rows[h], :] = dx2[h]
        dx2b = [d.astype(BF16) for d in dx2]
        dgated = [_nt(d, wout) for d in dx2b]
        dwout_ref[...] += _tn(cat(gated, 0), cat(dx2b, 0))
        dga = [d[:, 0:ATTN_WIDTH] for d in dgated]
        dgb = [d[:, ATTN_WIDTH:ATTN_WIDTH + SGU_WIDTH] for d in dgated]
        dgm = [d[:, ATTN_WIDTH + SGU_WIDTH:] for d in dgated]

        for h in H:
            da_ref[rows[h], :] = dga[h] * sil_a[h][0]
        dza = [dga[h] * a_val[h] * sil_a[h][1] for h in H]

        dsg = [dgb[h] * sil_b[h][0] for h in H]
        dzb = [dgb[h] * sg[h] * sil_b[h][1] for h in H]
        dub = [dsg[h] * mixed[h] * u[h][1] for h in H]
        dmixed = [dsg[h] * u[h][0] for h in H]
        dmixed_b = [d.astype(BF16) for d in dmixed]
        dvn = [cat([_nn(wT_cat, by_group(dmixed_b[h][c, :])) for c in chunks], 0) for h in H]
        for g in range(N_SGU_GROUPS):
            dws_ref[g] += sum(_nt((dmixed[h][c, :] * gm[g]).astype(BF16), vn[h][c, :]) for h in H for c in chunks)
        dbtab_scr[...] += sum(dmixed[h][c, :] for h in H for c in chunks)
        dgv_ref[...] += sum(jnp.sum(dvn[h] * vnorm[h][1], axis=0, keepdims=True) for h in H)
        tv = [d * gv for d in dvn]
        dvv = [vnorm[h][0] * (tv[h] - vnorm[h][1] * jnp.mean(tv[h] * vnorm[h][1], axis=1, keepdims=True)) for h in H]
        dvb = [dvv[h] * vv[h][1] for h in H]

        dmo = [dgm[h] * sil_m[h][0] for h in H]
        dzm = [dgm[h] * mo[h] * sil_m[h][1] for h in H]
        dmoj = {(h, pr, j): (dmo[h][:, pairs[pr]] * hm[j]).astype(BF16) for h in H for pr, j in heads}
        dp = {k: _nt(dmoj[k], vp[k[1]]) for k in qj}
        ds = {k: (prob[k] * (dp[k] - jnp.sum(dp[k] * prob[k], axis=1, keepdims=True))).astype(BF16) for k in qj}
        dqm = [cat([sum(_nn(ds[(h, pr, j)], kp[pr]) * (hm[j] * 0.125) for j in (0, 1)) for pr in range(2)], 1)
               for h in H]
        every = lambda tbl, pr: cat([tbl[(h, pr, j)] for h in H for j in (0, 1)], 0)
        dk = [_tn(every(ds, pr), every(qj, pr)) for pr in range(2)]
        dv = [_tn(every(probb, pr), every(dmoj, pr)) for pr in range(2)]
        dkv_ref[...] += cat(dk + dv, 1)

        for h in H:
            drest_ref[rows[h], :] = cat([dza[h], dub[h], dvb[h], dzb[h], dqm[h], dzm[h]], 1).astype(BF16)

        @pl.when(last)
        def _():
            lane = lax.broadcasted_iota(jnp.int32, (1, LANES), 1)
            dbt = dbtab_scr[...]
            out = jnp.zeros((SGU_CHUNK, LANES), F32)
            for g in range(N_SGU_GROUPS):
                out = out + jnp.where(lane == g, jnp.sum(dbt * gm[g], axis=1, keepdims=True), 0.0)
            dbs_ref[...] = out

    tile = lambda w, cb: pl.BlockSpec((tm, w), lambda b, t, cb=cb: (b * nt + t, cb))
    const = lambda shape: pl.BlockSpec(shape, lambda b, t, n=len(shape): (0,) * n)
    return pl.pallas_call(
        body, grid=(B, nt),
        in_specs=[tile(D_MODEL, 0), tile(D_MODEL, 0), tile(ATTN_WIDTH, 0),
                  tile(ATTN_WIDTH, 3),
                  tile(SGU_WIDTH, 8), tile(SGU_WIDTH, 9), tile(SGU_WIDTH, 10),
                  tile(MEM_WIDTH, 11), tile(MEM_WIDTH, 12),
                  pl.BlockSpec((N_MEM, 2 * MEM_WIDTH), lambda b, t: (b, 0)),
                  const((N_SGU_GROUPS, SGU_CHUNK, SGU_CHUNK)), const((N_SGU_GROUPS, SGU_CHUNK, SGU_CHUNK)),
                  const((SGU_CHUNK, SGU_WIDTH)), const((1, SGU_WIDTH)),
                  const((D_MODEL, D_MODEL)), const((1, D_MODEL))],
        out_specs=[tile(D_MODEL, 0), tile(ATTN_WIDTH, 0), tile(REST_COLS, 0),
                   const((8, LANES)), const((D_MODEL, D_MODEL)),
                   const((N_SGU_GROUPS, SGU_CHUNK, SGU_CHUNK)), const((SGU_CHUNK, LANES)),
                   const((1, SGU_WIDTH)), const((1, D_MODEL)),
                   pl.BlockSpec((N_MEM, 2 * MEM_WIDTH), lambda b, t: (b, 0))],
        out_shape=[jax.ShapeDtypeStruct((T, D_MODEL), F32), jax.ShapeDtypeStruct((T, ATTN_WIDTH), F32),
                   jax.ShapeDtypeStruct((T, REST_COLS), BF16),
                   jax.ShapeDtypeStruct((8, LANES), F32), jax.ShapeDtypeStruct((D_MODEL, D_MODEL), F32),
                   jax.ShapeDtypeStruct((N_SGU_GROUPS, SGU_CHUNK, SGU_CHUNK), F32),
                   jax.ShapeDtypeStruct((SGU_CHUNK, LANES), F32),
                   jax.ShapeDtypeStruct((1, SGU_WIDTH), F32), jax.ShapeDtypeStruct((1, D_MODEL), F32),
                   jax.ShapeDtypeStruct((B * N_MEM, 2 * MEM_WIDTH), F32)],
        scratch_shapes=[pltpu.VMEM((SGU_CHUNK, SGU_WIDTH), F32)],
        compiler_params=_params(("arbitrary", "arbitrary")), name="mid")(
            x2d, t2d, a, proj, proj, proj, proj, proj, proj, kv, w_s, w_sT, b_tab, g_v, w_out, g_final)


def _inproj_bwd_dx(dq, dk, dv, drest, x2d, dx2, g_norm, w_in_t, after=()):
    T = x2d.shape[0]
    tm = 512
    W = ATTN_WIDTH

    def body(dq_ref, dk_ref, dv_ref, dr_ref, x_ref, dx2_ref, g_ref, w_ref, *rest):
        gx_ref, dg_ref = rest[-2:]

        @pl.when(pl.program_id(0) == 0)
        def _():
            dg_ref[...] = jnp.zeros_like(dg_ref)

        halves = [pl.ds(h * (tm // 2), tm // 2) for h in (0, 1)]
        dh = [(_nn(dq_ref[r, :], w_ref[0:W, :]) + _nn(dk_ref[r, :], w_ref[W:2 * W, :])
               + _nn(dv_ref[r, :], w_ref[2 * W:3 * W, :]) + _nn(dr_ref[r, :], w_ref[QKV_COLS:IN_COLS, :]))
              for r in halves]
        nrm = [_rms(x_ref[r, :]) for r in halves]
        dg_ref[...] += sum(jnp.sum(d * n[1], axis=0, keepdims=True) for d, n in zip(dh, nrm))
        g = g_ref[...]
        for r, d, (rstd, xh) in zip(halves, dh, nrm):
            th = d * g
            gx_ref[r, :] = rstd * (th - xh * jnp.mean(th * xh, axis=1, keepdims=True)) + dx2_ref[r, :]

    tile = lambda w: pl.BlockSpec((tm, w), lambda i: (i, 0))
    return pl.pallas_call(
        body, grid=(T // tm,),
        in_specs=[tile(W), tile(W), tile(W), tile(REST_COLS), tile(D_MODEL), tile(D_MODEL),
                  pl.BlockSpec((1, D_MODEL), lambda i: (0, 0)),
                  pl.BlockSpec((IN_COLS, D_MODEL), lambda i: (0, 0))] + _after(after),
        out_specs=[tile(D_MODEL), pl.BlockSpec((1, D_MODEL), lambda i: (0, 0))],
        out_shape=[jax.ShapeDtypeStruct((T, D_MODEL), F32), jax.ShapeDtypeStruct((1, D_MODEL), F32)],
        compiler_params=_params(("arbitrary",)), name="inproj_bwd_dx")(
            dq, dk, dv, drest, x2d, dx2, g_norm, w_in_t, *after)


def _inproj_bwd_dw(dq, dk, dv, drest, x2d, g_norm, reduce_with=None):
    T = x2d.shape[0]
    tm = 512
    nt = T // tm
    W = ATTN_WIDTH
    fused = reduce_with is not None
    others = list(reduce_with) if fused else []
    ns = 1 + len(others)
    shard = IN_COLS // N_CHIPS
    halves = [shard // 2] + [s.shape[1] // 2 for s in others]
    cols = [D_MODEL] + [s.shape[2] for s in others]
    row_block = 32

    def body(dq_ref, dk_ref, dv_ref, dr_ref, x_ref, g_ref, *rest):
        if fused:
            stacks = rest[:ns - 1]
            sends, owns = rest[ns - 1:2 * ns - 1], rest[2 * ns - 1:3 * ns - 1]
            acc, ras, s_sem, r_sem = rest[3 * ns - 1], rest[3 * ns:4 * ns], rest[4 * ns], rest[4 * ns + 1]
            x, y, c, chip, peers, peer_chip = _place()
            sib = (x, y, 1 - c)

            def part(w, k, cc, r0=0, rows=None):
                n = halves[w]
                rows = n if rows is None else rows
                if w == 0:
                    return acc.at[pl.ds(pl.multiple_of(k * shard + cc * n + r0, 8), rows), :]
                return stacks[w - 1].at[k, pl.ds(pl.multiple_of(cc * n + r0, 8), rows), :]

            def swap_other(w):
                theirs = stacks[w - 1].at[:, pl.ds(pl.multiple_of((1 - c) * halves[w], 8), halves[w]), :]
                return _remote(theirs, ras[w], s_sem.at[N_CHIPS - 1 + w], r_sem.at[N_CHIPS - 1 + w], sib)

            def swap_win(k):
                return _remote(part(0, k, 1 - c), ras[0].at[k], s_sem.at[k], r_sem.at[k], sib)
        else:
            acc = rest[0]

        @pl.when(pl.program_id(0) == 0)
        def _():
            acc[...] = jnp.zeros_like(acc)
            for w in range(1, ns):
                swap_other(w).start()

        _, xh = _rms(x_ref[...])
        h = (xh * g_ref[...]).astype(BF16)
        acc[0:W, :] += _tn(dq_ref[...], h)
        acc[W:2 * W, :] += _tn(dk_ref[...], h)
        acc[2 * W:3 * W, :] += _tn(dv_ref[...], h)
        acc[QKV_COLS:IN_COLS, :] += _tn(dr_ref[...], h)

        if fused:
            @pl.when(pl.program_id(0) == nt - 1)
            def _():
                for k in range(N_CHIPS):
                    swap_win(k).start()
                for w in list(range(1, ns)) + [0]:
                    if w == 0:
                        for k in range(N_CHIPS):
                            swap_win(k).wait_recv()
                    else:
                        swap_other(w).wait_recv()

                    def sums(i, carry, w=w):
                        r0 = pl.multiple_of(i * row_block, row_block)
                        blk = pl.ds(r0, row_block)
                        for m in range(3):
                            k = peer_chip[m]
                            sends[w][m, blk, :] = (part(w, k, c, r0, row_block)[...] + ras[w][k, blk, :]).astype(BF16)
                        owns[w][blk, :] = part(w, chip, c, r0, row_block)[...] + ras[w][chip, blk, :]
                        return carry
                    lax.fori_loop(0, halves[w] // row_block, sums, 0)
                for k in range(N_CHIPS):
                    swap_win(k).wait_send()
                for w in range(1, ns):
                    swap_other(w).wait_send()

    tile = lambda w: pl.BlockSpec((tm, w), lambda i: (i, 0))
    vmem = pl.BlockSpec(memory_space=pltpu.VMEM)
    in_specs = [tile(W), tile(W), tile(W), tile(REST_COLS), tile(D_MODEL), pl.BlockSpec((1, D_MODEL), lambda i: (0, 0))]
    if not fused:
        return pl.pallas_call(
            body, grid=(nt,), in_specs=in_specs,
            out_specs=pl.BlockSpec((IN_COLS, D_MODEL), lambda i: (0, 0)),
            out_shape=jax.ShapeDtypeStruct((IN_COLS, D_MODEL), F32),
            compiler_params=_params(("arbitrary",)), name="inproj_bwd_dw")(dq, dk, dv, drest, x2d, g_norm)
    outs = pl.pallas_call(
        body, grid=(nt,), in_specs=in_specs + [vmem] * (ns - 1), out_specs=[vmem] * (2 * ns),
        out_shape=[jax.ShapeDtypeStruct((3, n, cl), BF16) for n, cl in zip(halves, cols)]
        + [jax.ShapeDtypeStruct((n, cl), F32) for n, cl in zip(halves, cols)],
        scratch_shapes=[pltpu.VMEM((IN_COLS, D_MODEL), F32)]
        + [pltpu.VMEM((N_CHIPS, n, cl), F32) for n, cl in zip(halves, cols)]
        + [pltpu.SemaphoreType.DMA((N_CHIPS - 1 + ns,)), pltpu.SemaphoreType.DMA((N_CHIPS - 1 + ns,))],
        compiler_params=_params(("arbitrary",)), name="inproj_bwd_dw_reduce")(
            dq, dk, dv, drest, x2d, g_norm, *others)
    return outs[:ns], outs[ns:]


def _adamw_update(w, g, m, v):
    nm = ADAM_B1 * m + (1.0 - ADAM_B1) * g
    nv = ADAM_B2 * v + (1.0 - ADAM_B2) * (g * g)
    m_hat = nm / (1.0 - ADAM_B1 ** ADAM_STEP)
    v_hat = nv / (1.0 - ADAM_B2 ** ADAM_STEP)
    return -ADAM_LR * (m_hat / (jnp.sqrt(v_hat) + ADAM_EPS) + ADAM_WD * w), nm, nv


def _adamw(w, g, m, v, name):
    R, C = w.shape
    br = max(r for r in range(8, 257, 8) if R % r == 0)

    def body(w_ref, g_ref, m_ref, v_ref, d_ref, nm_ref, nv_ref):
        d_ref[...], nm_ref[...], nv_ref[...] = _adamw_update(w_ref[...], g_ref[...], m_ref[...], v_ref[...])

    spec = pl.BlockSpec((br, C), lambda i: (i, 0))
    return pl.pallas_call(
        body, grid=(R // br,), in_specs=[spec] * 4, out_specs=[spec] * 3,
        out_shape=[jax.ShapeDtypeStruct((R, C), F32)] * 3,
        compiler_params=_params(("arbitrary",)), name=name)(w, g, m, v)


def _adamw_small(g_packed, ws, ms, vs):
    n = len(ws)

    def body(*refs):
        g_ref = refs[0]
        w_refs, m_refs, v_refs = refs[1:1 + n], refs[1 + n:1 + 2 * n], refs[1 + 2 * n:1 + 3 * n]
        outs = refs[1 + 3 * n:]
        off = 0
        for i, (_, used, padded) in enumerate(_SMALL_PARTS[:n]):
            g = g_ref[off:off + used, :]
            delta, nm, nv = _adamw_update(w_refs[i][...], g, m_refs[i][...], v_refs[i][...])
            outs[4 * i][...], outs[4 * i + 1][...], outs[4 * i + 2][...], outs[4 * i + 3][...] = g, delta, nm, nv
            off += padded

    outs = pl.pallas_call(
        body, out_shape=[jax.ShapeDtypeStruct(w.shape, F32) for w in ws for _ in range(4)],
        compiler_params=_params(), name="adamw_small")(g_packed, *ws, *ms, *vs)
    return [outs[4 * i:4 * i + 4] for i in range(n)]


def _place():
    x, y, c = lax.axis_index("x"), lax.axis_index("y"), lax.axis_index("c")
    chip = 2 * x + y
    peers = [(x, 1 - y), (1 - x, y), (1 - x, 1 - y)]
    peer_chip = [2 * px + py for px, py in peers]
    return x, y, c, chip, peers, peer_chip


def _remote(src, dst, send_sem, recv_sem, dev):
    return pltpu.make_async_remote_copy(src_ref=src, dst_ref=dst, send_sem=send_sem, recv_sem=recv_sem,
                                        device_id=dev, device_id_type=MESH)


def _ag_weights(weights, late=()):
    nw, nl = len(weights), len(late)

    def body(*refs):
        srcs, late_srcs = refs[:nw], refs[nw:nw + nl]
        outs, late_bf, late_land = (refs[nw + nl:2 * nw + nl], refs[2 * nw + nl:2 * nw + 2 * nl],
                                    refs[2 * nw + 2 * nl:2 * nw + 3 * nl])
        s_ici, r_ici, s_d2d, r_d2d = refs[2 * nw + 3 * nl:]
        x, y, c = lax.axis_index("x"), lax.axis_index("y"), lax.axis_index("c")
        chip = 2 * x + y
        sib = (x, y, 1 - c)
        first = ((x + 1 - c) % 2, (y + c) % 2)
        second = ((x + c) % 2, (y + 1 - c) % 2)
        first_chip, second_chip = 2 * first[0] + first[1], 2 * second[0] + second[1]
        diag_chip = 3 - chip
        for src, out in zip(srcs, outs):
            out[chip] = src[...].astype(BF16)

        def half(out, k, cc):
            rows = out.shape[1] // 2
            return out.at[k, pl.ds(pl.multiple_of(cc * rows, 16), rows), :]

        def ici(w, slot, out, k, dev):
            blk = half(out, k, c)
            return _remote(blk, blk, s_ici.at[nw * slot + w], r_ici.at[nw * slot + w], (dev[0], dev[1], c))

        def d2d(w, slot, out, k, cc):
            blk = half(out, k, cc)
            return _remote(blk, blk, s_d2d.at[nw * slot + w], r_d2d.at[nw * slot + w], sib)

        sent = []
        for slot, dev in enumerate((first, second)):
            for w, out in enumerate(outs):
                sent.append(ici(w, slot, out, chip, dev))
                sent[-1].start()
        for src, bf, land in zip(late_srcs, late_bf, late_land):
            bf[...] = src[...].astype(BF16)
            land[...] = jnp.zeros_like(land)
            land[chip] = bf[...]
        for slot, k, dev in ((0, first_chip, first), (1, second_chip, second), (2, diag_chip, second)):
            for w, out in enumerate(outs):
                ici(w, slot, out, k, dev).wait_recv()
                if slot == 0:
                    sent.append(ici(w, 2, out, k, second))
                    sent[-1].start()
                sent.append(d2d(w, slot, out, k, c))
                sent[-1].start()
        for slot, k in ((0, second_chip), (1, first_chip), (2, diag_chip)):
            for w, out in enumerate(outs):
                d2d(w, slot, out, k, 1 - c).wait_recv()
        for cp in sent:
            cp.wait_send()

    vmem = pl.BlockSpec(memory_space=pltpu.VMEM)
    outs = pl.pallas_call(
        body,
        out_shape=[jax.ShapeDtypeStruct((N_CHIPS,) + w.shape, BF16) for w in weights]
        + [jax.ShapeDtypeStruct(w.shape, BF16) for w in late]
        + [jax.ShapeDtypeStruct((N_CHIPS,) + w.shape, BF16) for w in late],
        in_specs=[vmem] * (nw + nl), out_specs=[vmem] * (nw + 2 * nl),
        scratch_shapes=[pltpu.SemaphoreType.DMA((3 * nw,))] * 4,
        compiler_params=pltpu.CompilerParams(vmem_limit_bytes=VMEM_LIMIT), name="ag_weights")(*weights, *late)
    return outs[:nw], outs[nw:nw + nl], outs[nw + nl:]


_HBM = pl.BlockSpec(memory_space=pltpu.HBM)
_SEM = pl.BlockSpec(memory_space=pltpu.SEMAPHORE)
_ANY = pl.BlockSpec(memory_space=pl.ANY)
_DATAFLOW = pltpu.SideEffectType.DATAFLOW_SIDE_EFFECTING


def _in_hbm(a):
    return pltpu.with_memory_space_constraint(a, pltpu.HBM)


def _exchange_copies(gather, srcs, lands, send_sems, recv_sems):
    nw = len(srcs)
    x, y, c, chip, peers, peer_chip = _place()
    pairs = []
    for m, (px, py) in enumerate(peers):
        for w in range(nw):
            sems = (send_sems.at[nw * m + w], recv_sems.at[nw * m + w], (px, py, c))
            if gather:
                pairs.append((_remote(srcs[w], lands[w].at[chip], *sems),
                              _remote(srcs[w], lands[w].at[peer_chip[m]], *sems)))
            else:
                pairs.append((_remote(srcs[w].at[m], lands[w].at[m], *sems),) * 2)
    return pairs


def _exchange_start(gather, srcs, after, name, lands=None):
    nw = len(srcs)
    n_copies = 3 * nw

    def body(*refs):
        send_sems, recv_sems = refs[2 * nw + 1], refs[2 * nw + 2]
        for start, _ in _exchange_copies(gather, refs[:nw], refs[nw:2 * nw], send_sems, recv_sems):
            start.start()
        refs[-1][...] = jnp.zeros_like(refs[-1])

    if lands is None:
        lands = [lax.empty(((N_CHIPS,) + s.shape) if gather else s.shape, s.dtype) for s in srcs]
    lands = [_in_hbm(l) for l in lands]
    return pl.pallas_call(
        body, name=name,
        out_shape=(pltpu.SemaphoreType.DMA((n_copies,)), pltpu.SemaphoreType.DMA((n_copies,)))
        + tuple(pltpu.HBM(s.shape, s.dtype) for s in srcs)
        + tuple(pltpu.HBM(l.shape, l.dtype) for l in lands)
        + (jax.ShapeDtypeStruct((8, LANES), F32),),
        in_specs=[_HBM] * (2 * nw) + [_ANY],
        out_specs=(_SEM, _SEM) + (_HBM,) * (2 * nw) + (pl.BlockSpec(memory_space=pltpu.VMEM),),
        input_output_aliases={i: 2 + i for i in range(2 * nw)},
        compiler_params=pltpu.CompilerParams(has_side_effects=_DATAFLOW),
    )(*[_in_hbm(s) for s in srcs], *lands, after)


def _exchange_wait(gather, started, after, name):
    nw = (len(started) - 3) // 2
    send_sems, recv_sems = started[0], started[1]
    thru = started[2:2 + 2 * nw]

    def body(*refs):
        for _, arrival in _exchange_copies(gather, refs[:nw], refs[nw:2 * nw], refs[2 * nw], refs[2 * nw + 1]):
            arrival.wait_send()
            arrival.wait_recv()

    outs = pl.pallas_call(
        body, name=name,
        out_shape=tuple(pltpu.HBM(t.shape, t.dtype) for t in thru),
        in_specs=[_HBM] * (2 * nw) + [_SEM, _SEM, _ANY], out_specs=(_HBM,) * (2 * nw),
        input_output_aliases={i: i for i in range(2 * nw)},
        compiler_params=pltpu.CompilerParams(has_side_effects=_DATAFLOW),
    )(*thru, send_sems, recv_sems, after)
    return outs[nw:]


def _reduce_last(owns, landed, g_small):
    ns = len(owns)
    row_block = 32
    hs = SMALL_ROWS // 2

    def body(*refs):
        own_refs, land_refs, gsm_ref = refs[:ns], refs[ns:2 * ns], refs[2 * ns]
        out_refs, osm_ref = refs[2 * ns + 1:3 * ns + 1], refs[3 * ns + 1]
        ra_sm, p_sm, s_sem, r_sem, sm_s, sm_r = refs[3 * ns + 2:]
        x, y, c, chip, peers, peer_chip = _place()
        sib = (x, y, 1 - c)
        half = lambda cc: pl.ds(pl.multiple_of(cc * hs, 8), hs)
        sm_a = _remote(gsm_ref.at[half(1 - c), :], ra_sm, sm_s.at[0], sm_r.at[0], sib)
        sm_a.start()
        swaps = [sm_a]
        for w in range(ns):
            n = own_refs[w].shape[0]

            def total(i, carry, w=w, n=n):
                r0 = pl.multiple_of(i * row_block, row_block)
                blk = pl.ds(r0, row_block)
                acc = own_refs[w][blk, :]
                for m in range(3):
                    acc = acc + land_refs[w][m, blk, :].astype(F32)
                out_refs[w][pl.ds(pl.multiple_of(c * n + r0, 8), row_block), :] = acc
                return carry
            lax.fori_loop(0, n // row_block, total, 0)
            mine = out_refs[w].at[pl.ds(pl.multiple_of(c * n, 8), n), :]
            swaps.append(_remote(mine, mine, s_sem.at[w], r_sem.at[w], sib))
            swaps[-1].start()
        sm_a.wait_recv()
        p_sm[chip] = gsm_ref[half(c), :] + ra_sm[...]
        for m, (px, py) in enumerate(peers):
            swaps.append(_remote(p_sm.at[chip], p_sm.at[chip], sm_s.at[1 + m], sm_r.at[1 + m], (px, py, c)))
            swaps[-1].start()
        for w in range(ns):
            n = own_refs[w].shape[0]
            theirs = out_refs[w].at[pl.ds(pl.multiple_of((1 - c) * n, 8), n), :]
            _remote(theirs, theirs, s_sem.at[w], r_sem.at[w], sib).wait_recv()
        for m, (px, py) in enumerate(peers):
            _remote(p_sm.at[chip], p_sm.at[peer_chip[m]], sm_s.at[1 + m], sm_r.at[1 + m], (px, py, c)).wait_recv()
        osm_ref[half(c), :] = (p_sm[0] + p_sm[1]) + (p_sm[2] + p_sm[3])
        swaps.append(_remote(osm_ref.at[half(c), :], osm_ref.at[half(c), :], sm_s.at[4], sm_r.at[4], sib))
        swaps[-1].start()
        _remote(osm_ref.at[half(1 - c), :], osm_ref.at[half(1 - c), :], sm_s.at[4], sm_r.at[4], sib).wait_recv()
        for cp in swaps:
            cp.wait_send()

    vmem = pl.BlockSpec(memory_space=pltpu.VMEM)
    return pl.pallas_call(
        body, out_shape=[jax.ShapeDtypeStruct((2 * o.shape[0], o.shape[1]), F32) for o in owns]
        + [jax.ShapeDtypeStruct((SMALL_ROWS, LANES), F32)],
        in_specs=[vmem] * (2 * ns + 1), out_specs=[vmem] * (ns + 1),
        scratch_shapes=[pltpu.VMEM((hs, LANES), F32), pltpu.VMEM((N_CHIPS, hs, LANES), F32),
                        pltpu.SemaphoreType.DMA((ns,)), pltpu.SemaphoreType.DMA((ns,)),
                        pltpu.SemaphoreType.DMA((5,)), pltpu.SemaphoreType.DMA((5,))],
        compiler_params=pltpu.CompilerParams(vmem_limit_bytes=VMEM_LIMIT),
        name="reduce_last")(*owns, *landed, g_small)


_SMALL_PARTS = (("g_norm", 8, 8), ("w_s", 512, 512), ("b_s", 4, 8), ("g_v", 2, 8), ("g_mem", 8, 8),
                ("g_final", 8, 8), ("loss", 1, 8))
_LOSS_ROW = SMALL_ROWS - 8
assert sum(p for _, _, p in _SMALL_PARTS) == SMALL_ROWS


def _pack_small(parts, loss=None):
    loss_row = jnp.zeros((1, LANES), F32) if loss is None else jnp.broadcast_to(loss.reshape(1, 1), (1, LANES))
    rows = []
    for (name, used, padded), p in zip(_SMALL_PARTS, list(parts) + [loss_row]):
        p = p.reshape(used, LANES)
        if padded > used:
            p = jnp.pad(p, ((0, padded - used), (0, 0)))
        rows.append(p)
    return jnp.concatenate(rows, axis=0)


def _local_step(x, mem, target, g_norm, w_in, w_s, b_s, g_v, g_mem, late_weights, g_final,
                fwd_token=None, on_dw=None):
    B, S, _ = x.shape
    x2d = x.reshape(B * S, D_MODEL)
    t2d = target.reshape(B * S, D_MODEL)
    mem2d = mem.reshape(B * N_MEM, D_MODEL)

    proj = _inproj_fwd(x2d, g_norm, w_in, after=() if fwd_token is None else (fwd_token,))
    w_kv, w_out = late_weights(proj)
    kv = _kv_fwd(mem2d, g_mem, w_kv)
    a, lse = _attn_fwd(proj, B, S)
    w_sT = jnp.swapaxes(w_s, 1, 2)
    b_tab = jnp.repeat(b_s.T, HEAD_DIM, axis=1)
    (dx2, da, drest, loss, d_wout, d_ws, d_bs, d_gv, d_gf, dkv) = _mid(
        x2d, t2d, a, proj, kv, w_s, w_sT, b_tab, g_v, w_out, g_final, B, S)
    d_wkv, d_gmem = _kv_bwd(mem2d, g_mem, w_kv, dkv)
    dq, dk, dv = _attn_bwd(proj, a, lse, da, B, S)
    if on_dw is None:
        d_win = _inproj_bwd_dw(dq, dk, dv, drest, x2d, g_norm)
        after = ()
    else:
        d_win = None
        by_chip = lambda g: g.reshape((N_CHIPS, g.shape[0] // N_CHIPS, g.shape[1]))
        after = (on_dw(*_inproj_bwd_dw(dq, dk, dv, drest, x2d, g_norm, reduce_with=[by_chip(d_wkv), by_chip(d_wout)])),)
    grad_x, d_gnorm = _inproj_bwd_dx(dq, dk, dv, drest, x2d, dx2, g_norm, w_in, after=after)
    d_bs = d_bs[:, :N_SGU_GROUPS].T
    return (loss[0, 0], grad_x.reshape(B, S, D_MODEL),
            dict(g_norm=d_gnorm, w_in=d_win, w_s=d_ws, b_s=d_bs, g_v=d_gv, g_mem=d_gmem, w_kv=d_wkv,
                 w_out=d_wout, g_final=d_gf))


def kernel(x, mem, g_norm, w_in, w_sgu_spatial, b_sgu_spatial, g_sgu_v, g_mem, w_mem_kv, w_out, g_final, loss_target, m_g_norm, m_w_in, m_w_sgu_spatial, m_b_sgu_spatial, m_g_sgu_v, m_g_mem, m_w_mem_kv, m_w_out, m_g_final, v_g_norm, v_w_in, v_w_sgu_spatial, v_b_sgu_spatial, v_g_sgu_v, v_g_mem, v_w_mem_kv, v_w_out, v_g_final):
    t = lambda w: jnp.swapaxes(w[0], 0, 1)
    (win_all,), late_shards, late_lands = _ag_weights([t(w_in)], [w_mem_kv[0], w_out[0]])
    w_in_full = win_all.reshape(-1, win_all.shape[-1])
    late = _exchange_start(True, list(late_shards), win_all, "gather_late_start", lands=late_lands)

    def late_weights(proj):
        return [z.reshape(-1, z.shape[-1]) for z in _exchange_wait(True, late, proj, "gather_late_wait")]

    scatter = {}

    def on_dw(sends, owns):
        scatter["own"] = owns
        scatter["started"] = _exchange_start(False, list(sends), owns[0], "scatter_start")
        return scatter["started"][-1]

    loss, grad_x, g = _local_step(
        x, mem, loss_target, g_norm, w_in_full, w_sgu_spatial[0], b_sgu_spatial[0], g_sgu_v, g_mem,
        late_weights, g_final.reshape(1, D_MODEL), fwd_token=late[-1], on_dw=on_dw)

    small_names = ("g_norm", "w_s", "b_s", "g_v", "g_mem", "g_final")
    g_small = _pack_small([g[n] for n in small_names], loss)
    landed = _exchange_wait(False, scatter["started"], g_small, "scatter_wait")
    gr_in, gr_kv, gr_out, gr_small = _reduce_last(scatter["own"], landed, g_small)
    loss = gr_small[_LOSS_ROW, 0]

    small_w = (g_norm, w_sgu_spatial, b_sgu_spatial, g_sgu_v, g_mem, g_final)
    small_m = (m_g_norm, m_w_sgu_spatial, m_b_sgu_spatial, m_g_sgu_v, m_g_mem, m_g_final)
    small_v = (v_g_norm, v_w_sgu_spatial, v_b_sgu_spatial, v_g_sgu_v, v_g_mem, v_g_final)
    rows = lambda ws: [w.reshape(-1, LANES) for w in ws]
    small = [[z.reshape(w.shape) for z in four]
             for w, four in zip(small_w, _adamw_small(gr_small, rows(small_w), rows(small_m), rows(small_v)))]
    d_in, nm_in, nv_in = _adamw(t(w_in), gr_in, t(m_w_in), t(v_w_in), "adamw_w_in")
    gr_in, d_in, nm_in, nv_in = [jnp.swapaxes(z, 0, 1) for z in (gr_in, d_in, nm_in, nv_in)]
    d_kv, nm_kv, nv_kv = _adamw(w_mem_kv[0], gr_kv, m_w_mem_kv[0], v_w_mem_kv[0], "adamw_w_kv")
    d_out, nm_out, nv_out = _adamw(w_out[0], gr_out, m_w_out[0], v_w_out[0], "adamw_w_out")

    def leaves(kind, big_in, big_kv, big_out):
        s_norm, s_ws, s_bs, s_gv, s_gmem, s_gf = [four[kind] for four in small]
        return [s_norm, big_in[None], s_ws, s_bs, s_gv, s_gmem, big_kv[None], big_out[None], s_gf]

    return (loss, grad_x, *leaves(0, gr_in, gr_kv, gr_out), *leaves(1, d_in, d_kv, d_out),
            *leaves(2, nm_in, nm_kv, nm_out), *leaves(3, nv_in, nv_kv, nv_out))
```

```python
import functools

import jax
import jax.numpy as jnp
from jax import lax
from jax.experimental import pallas as pl
from jax.experimental.pallas import tpu as pltpu

F32 = jnp.float32
BF16 = jnp.bfloat16
MESH = pl.DeviceIdType.MESH

D_MODEL = 1024
ATTN_WIDTH = 512
SGU_WIDTH = 256
MEM_WIDTH = 256
N_MEM = 256
IN_COLS = 3328
QKV_COLS = 3 * ATTN_WIDTH
REST_COLS = IN_COLS - QKV_COLS
SGU_CHUNK = 128
N_SGU_GROUPS = 4
EPS = 1e-6
NEG_INF = -1e30
DILATIONS = (1, 4, 16)
RADIUS = 64
Q_BLOCK = 128
LANES = 128
HEAD_DIM = 64

ADAM_LR = 0.001
ADAM_B1 = 0.9
ADAM_B2 = 0.999
ADAM_EPS = 1e-08
ADAM_WD = 0.01
ADAM_STEP = 10

N_CHIPS = 4
VMEM_LIMIT = 56 * 1024 * 1024
SMALL_ROWS = 560


def _params(sem=None, vmem=VMEM_LIMIT):
    return pltpu.CompilerParams(dimension_semantics=sem, vmem_limit_bytes=vmem)


def _nn(a, b):
    return jnp.dot(a, b, preferred_element_type=F32)


def _nt(a, b):
    return lax.dot_general(a, b, (((1,), (1,)), ((), ())), preferred_element_type=F32)


def _tn(a, b):
    return lax.dot_general(a, b, (((0,), (0,)), ((), ())), preferred_element_type=F32)


def _rms(x):
    r = lax.rsqrt(jnp.mean(x * x, axis=-1, keepdims=True) + EPS)
    return r, x * r


def _head_masks():
    lane = lax.broadcasted_iota(jnp.int32, (1, LANES), 1)
    lo = lane < HEAD_DIM
    return lo, (lo.astype(F32), (~lo).astype(F32))


def _silu_parts(z):
    s = jax.nn.sigmoid(z)
    return z * s, s * (1.0 + z * (1.0 - s))


def _gelu_parts(x):
    c = 0.7978845608028654
    x2 = x * x
    s = jax.nn.sigmoid((2.0 * c) * (x + 0.044715 * (x * x2)))
    return x * s, s * (1.0 + x * (1.0 - s) * ((2.0 * c) * (1.0 + 3.0 * 0.044715 * x2)))


def _after(tokens):
    return [pl.BlockSpec(memory_space=pl.ANY)] * len(tokens)


def _inproj_fwd(x2d, g_norm, w_in_t, after=()):
    T = x2d.shape[0]
    tm = 512

    def body(x_ref, g_ref, w_ref, *rest):
        o_ref = rest[-1]
        _, xh = _rms(x_ref[...])
        h = (xh * g_ref[...]).astype(BF16)
        o_ref[...] = _nt(h, w_ref[...])

    return pl.pallas_call(
        body, grid=(T // tm,),
        in_specs=[pl.BlockSpec((tm, D_MODEL), lambda i: (i, 0)),
                  pl.BlockSpec((1, D_MODEL), lambda i: (0, 0)),
                  pl.BlockSpec((IN_COLS, D_MODEL), lambda i: (0, 0))] + _after(after),
        out_specs=pl.BlockSpec((tm, IN_COLS), lambda i: (i, 0)),
        out_shape=jax.ShapeDtypeStruct((T, IN_COLS), F32),
        compiler_params=_params(("arbitrary",)), name="inproj_fwd")(x2d, g_norm, w_in_t, *after)


def _kv_fwd(mem2d, g_mem, w_kv):
    Tm = mem2d.shape[0]

    def body(m_ref, g_ref, w_ref, o_ref):
        _, mh = _rms(m_ref[...])
        o_ref[...] = _nn((mh * g_ref[...]).astype(BF16), w_ref[...])

    return pl.pallas_call(
        body, out_shape=jax.ShapeDtypeStruct((Tm, 2 * MEM_WIDTH), F32),
        compiler_params=_params(), name="kv_fwd")(mem2d, g_mem, w_kv)


def _kv_bwd(mem2d, g_mem, w_kv, dkv):
    Tm = mem2d.shape[0]

    def body(m_ref, g_ref, w_ref, dkv_ref, dw_ref, dg_ref):
        _, mh = _rms(m_ref[...])
        memn = (mh * g_ref[...]).astype(BF16)
        dkvb = dkv_ref[...].astype(BF16)
        dw_ref[...] = _tn(memn, dkvb)
        dmemn = _nt(dkvb, w_ref[...])
        dg_ref[...] = jnp.sum(dmemn * mh, axis=0, keepdims=True)

    return pl.pallas_call(
        body, out_shape=(jax.ShapeDtypeStruct((D_MODEL, 2 * MEM_WIDTH), F32),
                         jax.ShapeDtypeStruct((1, D_MODEL), F32)),
        compiler_params=_params(), name="kv_bwd")(mem2d, g_mem, w_kv, dkv)


def _attn_geometry(S):
    geom = []
    for d in DILATIONS:
        L = S // d
        assert L % Q_BLOCK == 0
        geom.append((d, L, min(2 * Q_BLOCK, L), L // Q_BLOCK))
    return geom


def _init_bias(bias_scr, geom, hp):
    row = lax.broadcasted_iota(jnp.int32, (Q_BLOCK, 2 * Q_BLOCK), 0)
    col = lax.broadcasted_iota(jnp.int32, (Q_BLOCK, 2 * Q_BLOCK), 1)
    for j in (0, 1):
        bits = (126 - (2 * hp + j)) * (1 << 23)
        slope = lax.bitcast_convert_type(jnp.full((1, 1), bits, jnp.int32), F32)
        for di, (d, _, _, _) in enumerate(geom):
            for cls, off in enumerate((0, -RADIUS, -2 * RADIUS)):
                dist = jnp.abs(col - row + off)
                bias_scr[di * 6 + cls * 2 + j] = jnp.where(
                    dist <= RADIUS, -(slope * float(d)) * dist.astype(F32), NEG_INF)


SPLIT = 4
COPY_ROWS = 256


def _by4_rows(S, step):
    per_class = S // SPLIT // COPY_ROWS
    r, j = step // per_class, step % per_class
    return (pl.ds(r + SPLIT * j * COPY_ROWS, COPY_ROWS, stride=SPLIT),
            pl.ds(pl.multiple_of(r * (S // SPLIT) + j * COPY_ROWS, COPY_ROWS), COPY_ROWS))


def _to_by4(src, dst, S):
    def step(i, carry):
        natural, by4 = _by4_rows(S, i)
        dst[by4, :] = src[natural, :]
        return carry
    lax.fori_loop(0, S // COPY_ROWS, step, 0)


def _block_slices(d, L, KW, nqb, r, qb, S):
    qs = qb * Q_BLOCK
    ks = jnp.clip(qs - RADIUS, 0, L - KW)
    cls = jnp.where(qb == 0, 0, jnp.where(qb == nqb - 1, 2, 1))
    if d == 1:
        qsl = pl.ds(pl.multiple_of(qs, Q_BLOCK), Q_BLOCK)
        ksl = pl.ds(pl.multiple_of(ks, RADIUS), KW)
    elif d == SPLIT:
        qsl = pl.ds(pl.multiple_of(r * L + qs, Q_BLOCK), Q_BLOCK)
        ksl = pl.ds(pl.multiple_of(r * L + ks, RADIUS), KW)
    else:
        sub = d // SPLIT
        base = (r % SPLIT) * (S // SPLIT) + r // SPLIT
        qsl = pl.ds(base + qs * sub, Q_BLOCK, stride=sub)
        ksl = pl.ds(base + ks * sub, KW, stride=sub)
    return qsl, ksl, cls


def _for_groups(geom, S, group, fn):
    for di, (d, L, KW, nqb) in enumerate(geom):
        assert (d * nqb) % group == 0

        def step(it, carry, di=di, d=d, L=L, KW=KW, nqb=nqb):
            slices = []
            for g in range(group):
                i = it * group + g
                slices.append(_block_slices(d, L, KW, nqb, i // nqb, i % nqb, S))
            fn(di, KW, slices)
            return carry
        lax.fori_loop(0, d * nqb // group, step, 0)


def _attn_fwd(proj, B, S):
    T = B * S
    geom = _attn_geometry(S)
    n_pairs = ATTN_WIDTH // LANES

    def body(q_ref, k_ref, v_ref, a_ref, lse_ref, bias_scr, q4, k4, v4, *per_dilation):
        o_scr, m_scr, l_scr = per_dilation[0:3], per_dilation[3:6], per_dilation[6:9]
        lo, hm = _head_masks()
        @pl.when(pl.program_id(1) == 0)
        def _():
            _init_bias(bias_scr, geom, pl.program_id(0))
        for src, dst in ((q_ref, q4), (k_ref, k4), (v_ref, v4)):
            _to_by4(src, dst, S)

        def group(di, KW, slices):
            chains = [(g, j) for g in range(len(slices)) for j in (0, 1)]
            q_src, k_src, v_src = (q_ref, k_ref, v_ref) if di == 0 else (q4, k4, v4)
            q = [q_src[qsl, :] for qsl, _, _ in slices]
            kw = [k_src[ksl, :].astype(BF16) for _, ksl, _ in slices]
            vw = [v_src[ksl, :].astype(BF16) for _, ksl, _ in slices]
            s = {(g, j): _nt((q[g] * (hm[j] * 0.125)).astype(BF16), kw[g])
                 + bias_scr[di * 6 + slices[g][2] * 2 + j, :, pl.ds(0, KW)] for g, j in chains}
            m = {c: jnp.max(s[c], axis=1, keepdims=True) for c in chains}
            p = {c: jnp.exp(s[c] - m[c]) for c in chains}
            l = {c: jnp.sum(p[c], axis=1, keepdims=True) for c in chains}
            o = {(g, j): _nn(p[(g, j)].astype(BF16), vw[g]) for g, j in chains}
            for g, (qsl, _, _) in enumerate(slices):
                o_scr[di][qsl, :] = jnp.where(lo, o[(g, 0)], o[(g, 1)])
                m_scr[di][qsl, :] = jnp.where(lo, m[(g, 0)], m[(g, 1)])
                l_scr[di][qsl, :] = jnp.where(lo, l[(g, 0)], l[(g, 1)])

        _for_groups(geom, S, 8, group)

        def combine(i, carry):
            natural, by4 = _by4_rows(S, i)
            rows = [natural, by4, by4]
            ms = [m_scr[di][rows[di], :] for di in range(3)]
            mx = jnp.maximum(jnp.maximum(ms[0], ms[1]), ms[2])
            num = 0.0
            den = 0.0
            for di in range(3):
                w = jnp.exp(ms[di] - mx)
                num = num + w * o_scr[di][rows[di], :]
                den = den + w * l_scr[di][rows[di], :]
            a_ref[natural, :] = num / den
            lse_ref[natural, :] = mx + jnp.log(den)
            return carry

        lax.fori_loop(0, S // COPY_ROWS, combine, 0)

    blk = lambda off: pl.BlockSpec((S, LANES), lambda h, b, off=off: (b, off + h))
    out_blk = pl.BlockSpec((S, LANES), lambda h, b: (b, h))
    return pl.pallas_call(
        body, grid=(n_pairs, B),
        in_specs=[blk(0), blk(n_pairs), blk(2 * n_pairs)],
        out_specs=[out_blk, out_blk],
        out_shape=[jax.ShapeDtypeStruct((T, ATTN_WIDTH), F32)] * 2,
        scratch_shapes=[pltpu.VMEM((18, Q_BLOCK, 2 * Q_BLOCK), F32)] + [pltpu.VMEM((S, LANES), F32)] * 12,
        compiler_params=_params(("arbitrary", "arbitrary")), name="attn_fwd")(proj, proj, proj)


def _attn_bwd(proj, a, lse, da, B, S):
    T = B * S
    geom = _attn_geometry(S)
    n_pairs = ATTN_WIDTH // LANES

    def body(q_ref, k_ref, v_ref, a_ref, lse_ref, do_ref, dq_ref, dk_ref, dv_ref, bias_scr, *scr):
        acc = (scr[0:3], scr[3:6])
        natural_in = (q_ref, k_ref, v_ref, a_ref, lse_ref, do_ref)
        by4_in = scr[6:12]
        _, hm = _head_masks()
        @pl.when(pl.program_id(1) == 0)
        def _():
            _init_bias(bias_scr, geom, pl.program_id(0))
        for ref in scr[0:6]:
            ref[...] = jnp.zeros_like(ref)
        for src, dst in zip(natural_in, by4_in):
            _to_by4(src, dst, S)

        def group(di, KW, slices):
            n = len(slices)
            chains = [(g, j) for g in range(n) for j in (0, 1)]
            q_src, k_src, v_src, a_src, lse_src, do_src = natural_in if di == 0 else by4_in
            dq_scr, dk_scr, dv_scr = acc[0 if di == 0 else 1]
            q = [q_src[qsl, :] for qsl, _, _ in slices]
            do = [do_src[qsl, :] for qsl, _, _ in slices]
            doa = [do[g] * a_src[slices[g][0], :] for g in range(n)]
            lse_q = [lse_src[qsl, :] for qsl, _, _ in slices]
            kw = [k_src[ksl, :].astype(BF16) for _, ksl, _ in slices]
            vw = [v_src[ksl, :].astype(BF16) for _, ksl, _ in slices]
            qj = {(g, j): (q[g] * (hm[j] * 0.125)).astype(BF16) for g, j in chains}
            doj = {(g, j): (do[g] * hm[j]).astype(BF16) for g, j in chains}
            s = {(g, j): _nt(qj[(g, j)], kw[g])
                 + bias_scr[di * 6 + slices[g][2] * 2 + j, :, pl.ds(0, KW)] for g, j in chains}
            dp = {(g, j): _nt(doj[(g, j)], vw[g]) for g, j in chains}
            dsum = {(g, j): jnp.sum(doa[g] * hm[j], axis=1, keepdims=True) for g, j in chains}
            p = {(g, j): jnp.exp(s[(g, j)] - lse_q[g][:, HEAD_DIM * j:HEAD_DIM * j + 1]) for g, j in chains}
            ds = {c: (p[c] * (dp[c] - dsum[c])).astype(BF16) for c in chains}
            pb = {c: p[c].astype(BF16) for c in chains}
            dq = [_nn(ds[(g, 0)], kw[g]) * (hm[0] * 0.125) + _nn(ds[(g, 1)], kw[g]) * (hm[1] * 0.125)
                  for g in range(n)]
            both = lambda t, g: jnp.concatenate([t[(g, 0)], t[(g, 1)]], axis=0)
            dkw = [_tn(both(ds, g), both(qj, g)) for g in range(n)]
            dvw = [_tn(both(pb, g), both(doj, g)) for g in range(n)]
            for g, (qsl, ksl, _) in enumerate(slices):
                dq_scr[qsl, :] = dq_scr[qsl, :] + dq[g]
                dk_scr[ksl, :] = dk_scr[ksl, :] + dkw[g]
                dv_scr[ksl, :] = dv_scr[ksl, :] + dvw[g]

        _for_groups(geom, S, 4, group)

        def merge(i, carry):
            natural, by4 = _by4_rows(S, i)
            for nat, split in zip(*acc):
                nat[natural, :] = nat[natural, :] + split[by4, :]
            return carry
        lax.fori_loop(0, S // COPY_ROWS, merge, 0)
        for out, nat in zip((dq_ref, dk_ref, dv_ref), acc[0]):
            out[...] = nat[...].astype(BF16)

    blk = lambda off: pl.BlockSpec((S, LANES), lambda h, b, off=off: (b, off + h))
    return pl.pallas_call(
        body, grid=(n_pairs, B),
        in_specs=[blk(0), blk(n_pairs), blk(2 * n_pairs), blk(0), blk(0), blk(0)],
        out_specs=[blk(0), blk(0), blk(0)],
        out_shape=[jax.ShapeDtypeStruct((T, ATTN_WIDTH), BF16)] * 3,
        scratch_shapes=[pltpu.VMEM((18, Q_BLOCK, 2 * Q_BLOCK), F32)] + [pltpu.VMEM((S, LANES), F32)] * 12,
        compiler_params=_params(("arbitrary", "arbitrary")), name="attn_bwd")(proj, proj, proj, a, lse, da)


def _mid(x2d, t2d, a, proj, kv, w_s, w_sT, b_tab, g_v, w_out, g_final, B, S):
    T = B * S
    tm = 512
    nt = S // tm
    halves = 2
    hrows = tm // halves

    def body(x_ref, t_ref, a_ref, za_ref, ub_ref, vb_ref, zb_ref, qm_ref, zm_ref, kv_ref,
              ws_ref, wsT_ref, btab_ref, gv_ref, wout_ref, gf_ref,
              dx2_ref, da_ref, drest_ref, loss_ref, dwout_ref, dws_ref, dbs_ref, dgv_ref, dgf_ref, dkv_ref,
              dbtab_scr):
        b = pl.program_id(0)
        t = pl.program_id(1)
        first = jnp.logical_and(b == 0, t == 0)
        last = jnp.logical_and(b == B - 1, t == nt - 1)
        _, hm = _head_masks()
        lane_g = lax.broadcasted_iota(jnp.int32, (1, SGU_WIDTH), 1) // HEAD_DIM
        gm = [(lane_g == g).astype(F32) for g in range(N_SGU_GROUPS)]
        H = range(halves)
        rows = [pl.ds(h * hrows, hrows) for h in H]
        ld = lambda ref: [ref[r, :] for r in rows]
        cat = lambda parts, axis: jnp.concatenate(parts, axis=axis)
        chunks = [slice(ci * SGU_CHUNK, (ci + 1) * SGU_CHUNK) for ci in range(hrows // SGU_CHUNK)]
        pairs = [slice(pr * LANES, (pr + 1) * LANES) for pr in range(2)]
        heads = [(pr, j) for pr in range(2) for j in (0, 1)]

        @pl.when(first)
        def _():
            loss_ref[...] = jnp.zeros_like(loss_ref)
            dwout_ref[...] = jnp.zeros_like(dwout_ref)
            dws_ref[...] = jnp.zeros_like(dws_ref)
            dbs_ref[...] = jnp.zeros_like(dbs_ref)
            dgv_ref[...] = jnp.zeros_like(dgv_ref)
            dgf_ref[...] = jnp.zeros_like(dgf_ref)
            dbtab_scr[...] = jnp.zeros_like(dbtab_scr)

        @pl.when(t == 0)
        def _():
            dkv_ref[...] = jnp.zeros_like(dkv_ref)

        a_val = ld(a_ref)
        sil_a = [_silu_parts(z) for z in ld(za_ref)]
        gated_a = [s[0] * a for s, a in zip(sil_a, a_val)]
        u = [_gelu_parts(z) for z in ld(ub_ref)]
        vv = [_gelu_parts(z) for z in ld(vb_ref)]
        vnorm = [_rms(v[0]) for v in vv]
        gv = gv_ref[...]
        vn = [(n[1] * gv).astype(BF16) for n in vnorm]
        w_cat = cat([ws_ref[g].astype(BF16) for g in range(N_SGU_GROUPS)], 1)
        wT_cat = cat([wsT_ref[g].astype(BF16) for g in range(N_SGU_GROUPS)], 1)
        gmb = [m.astype(BF16) for m in gm]
        by_group = lambda chunk: cat([chunk * gmb[g] for g in range(N_SGU_GROUPS)], 0)
        btab = btab_ref[...]
        mixed = [cat([btab + _nn(w_cat, by_group(vn[h][c, :])) for c in chunks], 0) for h in H]
        sg = [u[h][0] * mixed[h] for h in H]
        sil_b = [_silu_parts(z) for z in ld(zb_ref)]
        gated_b = [sil_b[h][0] * sg[h] for h in H]

        kvv = kv_ref[...].astype(BF16)
        kp = [kvv[:, p] for p in pairs]
        vp = [kvv[:, MEM_WIDTH + pr * LANES:MEM_WIDTH + (pr + 1) * LANES] for pr in range(2)]
        qm = ld(qm_ref)
        qj = {(h, pr, j): (qm[h][:, pairs[pr]] * (hm[j] * 0.125)).astype(BF16) for h in H for pr, j in heads}
        sc = {k: _nt(qj[k], kp[k[1]]) for k in qj}
        ex = {k: jnp.exp(sc[k] - jnp.max(sc[k], axis=1, keepdims=True)) for k in qj}
        prob = {k: ex[k] * (1.0 / jnp.sum(ex[k], axis=1, keepdims=True)) for k in qj}
        probb = {k: prob[k].astype(BF16) for k in qj}
        mo = [cat([sum(_nn(probb[(h, pr, j)], vp[pr]) * hm[j] for j in (0, 1)) for pr in range(2)], 1) for h in H]
        sil_m = [_silu_parts(z) for z in ld(zm_ref)]
        gated_m = [sil_m[h][0] * mo[h] for h in H]

        gated = [cat([gated_a[h], gated_b[h], gated_m[h]], 1).astype(BF16) for h in H]
        wout = wout_ref[...]
        x_in = ld(x_ref)
        x2 = [x_in[h] + _nn(gated[h], wout) for h in H]
        fin = [_rms(z) for z in x2]
        gf = gf_ref[...]
        tgt = ld(t_ref)
        err = [fin[h][1] * gf - tgt[h] for h in H]
        loss_ref[...] += sum(jnp.sum(e * e) for e in err) * (0.5 / D_MODEL)

        dy = [e * (1.0 / D_MODEL) for e in err]
        dgf_ref[...] += sum(jnp.sum(dy[h] * fin[h][1], axis=0, keepdims=True) for h in H)
        gdy = [d * gf for d in dy]
        dx2 = [fin[h][0] * (gdy[h] - fin[h][1] * jnp.mean(gdy[h] * fin[h][1], axis=1, keepdims=True)) for h in H]
        for h in H:
            dx2_ref[rows[h], :] = dx2[h]
        dx2b = [d.astype(BF16) for d in dx2]
        dgated = [_nt(d, wout) for d in dx2b]
        dwout_ref[...] += _tn(cat(gated, 0), cat(dx2b, 0))
        dga = [d[:, 0:ATTN_WIDTH] for d in dgated]
        dgb = [d[:, ATTN_WIDTH:ATTN_WIDTH + SGU_WIDTH] for d in dgated]
        dgm = [d[:, ATTN_WIDTH + SGU_WIDTH:] for d in dgated]

        for h in H:
            da_ref[rows[h], :] = dga[h] * sil_a[h][0]
        dza = [dga[h] * a_val[h] * sil_a[h][1] for h in H]

        dsg = [dgb[h] * sil_b[h][0] for h in H]
        dzb = [dgb[h] * sg[h] * sil_b[h][1] for h in H]
        dub = [dsg[h] * mixed[h] * u[h][1] for h in H]
        dmixed = [dsg[h] * u[h][0] for h in H]
        dmixed_b = [d.astype(BF16) for d in dmixed]
        dvn = [cat([_nn(wT_cat, by_group(dmixed_b[h][c, :])) for c in chunks], 0) for h in H]
        for g in range(N_SGU_GROUPS):
            dws_ref[g] += sum(_nt((dmixed[h][c, :] * gm[g]).astype(BF16), vn[h][c, :]) for h in H for c in chunks)
        dbtab_scr[...] += sum(dmixed[h][c, :] for h in H for c in chunks)
        dgv_ref[...] += sum(jnp.sum(dvn[h] * vnorm[h][1], axis=0, keepdims=True) for h in H)
        tv = [d * gv for d in dvn]
        dvv = [vnorm[h][0] * (tv[h] - vnorm[h][1] * jnp.mean(tv[h] * vnorm[h][1], axis=1, keepdims=True)) for h in H]
        dvb = [dvv[h] * vv[h][1] for h in H]

        dmo = [dgm[h] * sil_m[h][0] for h in H]
        dzm = [dgm[h] * mo[h] * sil_m[h][1] for h in H]
        dmoj = {(h, pr, j): (dmo[h][:, pairs[pr]] * hm[j]).astype(BF16) for h in H for pr, j in heads}
        dp = {k: _nt(dmoj[k], vp[k[1]]) for k in qj}
        ds = {k: (prob[k] * (dp[k] - jnp.sum(dp[k] * prob[k], axis=1, keepdims=True))).astype(BF16) for k in qj}
        dqm = [cat([sum(_nn(ds[(h, pr, j)], kp[pr]) * (hm[j] * 0.125) for j in (0, 1)) for pr in range(2)], 1)
               for h in H]
        every = lambda tbl, pr: cat([tbl[(h, pr, j)] for h in H for j in (0, 1)], 0)
        dk = [_tn(every(ds, pr), every(qj, pr)) for pr in range(2)]
        dv = [_tn(every(probb, pr), every(dmoj, pr)) for pr in range(2)]
        dkv_ref[...] += cat(dk + dv, 1)

        for h in H:
            drest_ref[rows[h], :] = cat([dza[h], dub[h], dvb[h], dzb[h], dqm[h], dzm[h]], 1).astype(BF16)

        @pl.when(last)
        def _():
            lane = lax.broadcasted_iota(jnp.int32, (1, LANES), 1)
            dbt = dbtab_scr[...]
            out = jnp.zeros((SGU_CHUNK, LANES), F32)
            for g in range(N_SGU_GROUPS):
                out = out + jnp.where(lane == g, jnp.sum(dbt * gm[g], axis=1, keepdims=True), 0.0)
            dbs_ref[...] = out

    tile = lambda w, cb: pl.BlockSpec((tm, w), lambda b, t, cb=cb: (b * nt + t, cb))
    const = lambda shape: pl.BlockSpec(shape, lambda b, t, n=len(shape): (0,) * n)
    return pl.pallas_call(
        body, grid=(B, nt),
        in_specs=[tile(D_MODEL, 0), tile(D_MODEL, 0), tile(ATTN_WIDTH, 0),
                  tile(ATTN_WIDTH, 3),
                  tile(SGU_WIDTH, 8), tile(SGU_WIDTH, 9), tile(SGU_WIDTH, 10),
                  tile(MEM_WIDTH, 11), tile(MEM_WIDTH, 12),
                  pl.BlockSpec((N_MEM, 2 * MEM_WIDTH), lambda b, t: (b, 0)),
                  const((N_SGU_GROUPS, SGU_CHUNK, SGU_CHUNK)), const((N_SGU_GROUPS, SGU_CHUNK, SGU_CHUNK)),
                  const((SGU_CHUNK, SGU_WIDTH)), const((1, SGU_WIDTH)),
                  const((D_MODEL, D_MODEL)), const((1, D_MODEL))],
        out_specs=[tile(D_MODEL, 0), tile(ATTN_WIDTH, 0), tile(REST_COLS, 0),
                   const((8, LANES)), const((D_MODEL, D_MODEL)),
                   const((N_SGU_GROUPS, SGU_CHUNK, SGU_CHUNK)), const((SGU_CHUNK, LANES)),
                   const((1, SGU_WIDTH)), const((1, D_MODEL)),
                   pl.BlockSpec((N_MEM, 2 * MEM_WIDTH), lambda b, t: (b, 0))],
        out_shape=[jax.ShapeDtypeStruct((T, D_MODEL), F32), jax.ShapeDtypeStruct((T, ATTN_WIDTH), F32),
                   jax.ShapeDtypeStruct((T, REST_COLS), BF16),
                   jax.ShapeDtypeStruct((8, LANES), F32), jax.ShapeDtypeStruct((D_MODEL, D_MODEL), F32),
                   jax.ShapeDtypeStruct((N_SGU_GROUPS, SGU_CHUNK, SGU_CHUNK), F32),
                   jax.ShapeDtypeStruct((SGU_CHUNK, LANES), F32),
                   jax.ShapeDtypeStruct((1, SGU_WIDTH), F32), jax.ShapeDtypeStruct((1, D_MODEL), F32),
                   jax.ShapeDtypeStruct((B * N_MEM, 2 * MEM_WIDTH), F32)],
        scratch_shapes=[pltpu.VMEM((SGU_CHUNK, SGU_WIDTH), F32)],
        compiler_params=_params(("arbitrary", "arbitrary")), name="mid")(
            x2d, t2d, a, proj, proj, proj, proj, proj, proj, kv, w_s, w_sT, b_tab, g_v, w_out, g_final)


def _inproj_bwd_dx(dq, dk, dv, drest, x2d, dx2, g_norm, w_in_t, after=()):
    T = x2d.shape[0]
    tm = 512
    W = ATTN_WIDTH

    def body(dq_ref, dk_ref, dv_ref, dr_ref, x_ref, dx2_ref, g_ref, w_ref, *rest):
        gx_ref, dg_ref = rest[-2:]

        @pl.when(pl.program_id(0) == 0)
        def _():
            dg_ref[...] = jnp.zeros_like(dg_ref)

        halves = [pl.ds(h * (tm // 2), tm // 2) for h in (0, 1)]
        dh = [(_nn(dq_ref[r, :], w_ref[0:W, :]) + _nn(dk_ref[r, :], w_ref[W:2 * W, :])
               + _nn(dv_ref[r, :], w_ref[2 * W:3 * W, :]) + _nn(dr_ref[r, :], w_ref[QKV_COLS:IN_COLS, :]))
              for r in halves]
        nrm = [_rms(x_ref[r, :]) for r in halves]
        dg_ref[...] += sum(jnp.sum(d * n[1], axis=0, keepdims=True) for d, n in zip(dh, nrm))
        g = g_ref[...]
        for r, d, (rstd, xh) in zip(halves, dh, nrm):
            th = d * g
            gx_ref[r, :] = rstd * (th - xh * jnp.mean(th * xh, axis=1, keepdims=True)) + dx2_ref[r, :]

    tile = lambda w: pl.BlockSpec((tm, w), lambda i: (i, 0))
    return pl.pallas_call(
        body, grid=(T // tm,),
        in_specs=[tile(W), tile(W), tile(W), tile(REST_COLS), tile(D_MODEL), tile(D_MODEL),
                  pl.BlockSpec((1, D_MODEL), lambda i: (0, 0)),
                  pl.BlockSpec((IN_COLS, D_MODEL), lambda i: (0, 0))] + _after(after),
        out_specs=[tile(D_MODEL), pl.BlockSpec((1, D_MODEL), lambda i: (0, 0))],
        out_shape=[jax.ShapeDtypeStruct((T, D_MODEL), F32), jax.ShapeDtypeStruct((1, D_MODEL), F32)],
        compiler_params=_params(("arbitrary",)), name="inproj_bwd_dx")(
            dq, dk, dv, drest, x2d, dx2, g_norm, w_in_t, *after)


def _inproj_bwd_dw(dq, dk, dv, drest, x2d, g_norm, reduce_with=None):
    T = x2d.shape[0]
    tm = 512
    nt = T // tm
    W = ATTN_WIDTH
    fused = reduce_with is not None
    others = list(reduce_with) if fused else []
    ns = 1 + len(others)
    shard = IN_COLS // N_CHIPS
    halves = [shard // 2] + [s.shape[1] // 2 for s in others]
    cols = [D_MODEL] + [s.shape[2] for s in others]
    row_block = 32

    def body(dq_ref, dk_ref, dv_ref, dr_ref, x_ref, g_ref, *rest):
        if fused:
            stacks = rest[:ns - 1]
            sends, owns = rest[ns - 1:2 * ns - 1], rest[2 * ns - 1:3 * ns - 1]
            acc, ras, narrow = rest[3 * ns - 1], rest[3 * ns:4 * ns], rest[4 * ns]
            s_sem, r_sem = rest[4 * ns + 1], rest[4 * ns + 2]
            x, y, c, chip, peers, peer_chip = _place()
            sib = (x, y, 1 - c)

            def part(w, k, cc, r0=0, rows=None):
                n = halves[w]
                rows = n if rows is None else rows
                if w == 0:
                    return acc.at[pl.ds(pl.multiple_of(k * shard + cc * n + r0, 8), rows), :]
                return stacks[w - 1].at[k, pl.ds(pl.multiple_of(cc * n + r0, 8), rows), :]

            def swap_other(w):
                theirs = stacks[w - 1].at[:, pl.ds(pl.multiple_of((1 - c) * halves[w], 8), halves[w]), :]
                return _remote(theirs, ras[w], s_sem.at[N_CHIPS - 1 + w], r_sem.at[N_CHIPS - 1 + w], sib)

            def swap_win(k):
                return _remote(narrow.at[k], ras[0].at[k], s_sem.at[k], r_sem.at[k], sib)
        else:
            acc = rest[0]

        @pl.when(pl.program_id(0) == 0)
        def _():
            acc[...] = jnp.zeros_like(acc)
            for w in range(1, ns):
                swap_other(w).start()

        _, xh = _rms(x_ref[...])
        h = (xh * g_ref[...]).astype(BF16)
        acc[0:W, :] += _tn(dq_ref[...], h)
        acc[W:2 * W, :] += _tn(dk_ref[...], h)
        acc[2 * W:3 * W, :] += _tn(dv_ref[...], h)
        acc[QKV_COLS:IN_COLS, :] += _tn(dr_ref[...], h)

        if fused:
            @pl.when(pl.program_id(0) == nt - 1)
            def _():
                for k in range(N_CHIPS):
                    def to_bf16(i, carry, k=k):
                        r0 = pl.multiple_of(i * row_block, row_block)
                        narrow[k, pl.ds(r0, row_block), :] = part(0, k, 1 - c, r0, row_block)[...].astype(BF16)
                        return carry
                    lax.fori_loop(0, halves[0] // row_block, to_bf16, 0)
                    swap_win(k).start()
                for w in list(range(1, ns)) + [0]:
                    if w == 0:
                        for k in range(N_CHIPS):
                            swap_win(k).wait_recv()
                    else:
                        swap_other(w).wait_recv()

                    def sums(i, carry, w=w):
                        r0 = pl.multiple_of(i * row_block, row_block)
                        blk = pl.ds(r0, row_block)
                        for m in range(3):
                            k = peer_chip[m]
                            sends[w][m, blk, :] = (part(w, k, c, r0, row_block)[...]
                                                   + ras[w][k, blk, :].astype(F32)).astype(BF16)
                        owns[w][blk, :] = part(w, chip, c, r0, row_block)[...] + ras[w][chip, blk, :].astype(F32)
                        return carry
                    lax.fori_loop(0, halves[w] // row_block, sums, 0)
                for k in range(N_CHIPS):
                    swap_win(k).wait_send()
                for w in range(1, ns):
                    swap_other(w).wait_send()

    tile = lambda w: pl.BlockSpec((tm, w), lambda i: (i, 0))
    vmem = pl.BlockSpec(memory_space=pltpu.VMEM)
    in_specs = [tile(W), tile(W), tile(W), tile(REST_COLS), tile(D_MODEL), pl.BlockSpec((1, D_MODEL), lambda i: (0, 0))]
    if not fused:
        return pl.pallas_call(
            body, grid=(nt,), in_specs=in_specs,
            out_specs=pl.BlockSpec((IN_COLS, D_MODEL), lambda i: (0, 0)),
            out_shape=jax.ShapeDtypeStruct((IN_COLS, D_MODEL), F32),
            compiler_params=_params(("arbitrary",)), name="inproj_bwd_dw")(dq, dk, dv, drest, x2d, g_norm)
    outs = pl.pallas_call(
        body, grid=(nt,), in_specs=in_specs + [vmem] * (ns - 1), out_specs=[vmem] * (2 * ns),
        out_shape=[jax.ShapeDtypeStruct((3, n, cl), BF16) for n, cl in zip(halves, cols)]
        + [jax.ShapeDtypeStruct((n, cl), F32) for n, cl in zip(halves, cols)],
        scratch_shapes=[pltpu.VMEM((IN_COLS, D_MODEL), F32)]
        + [pltpu.VMEM((N_CHIPS, n, cl), BF16 if w == 0 else F32) for w, (n, cl) in enumerate(zip(halves, cols))]
        + [pltpu.VMEM((N_CHIPS, halves[0], D_MODEL), BF16)]
        + [pltpu.SemaphoreType.DMA((N_CHIPS - 1 + ns,)), pltpu.SemaphoreType.DMA((N_CHIPS - 1 + ns,))],
        compiler_params=_params(("arbitrary",)), name="inproj_bwd_dw_reduce")(
            dq, dk, dv, drest, x2d, g_norm, *others)
    return outs[:ns], outs[ns:]


def _adamw_update(w, g, m, v):
    nm = ADAM_B1 * m + (1.0 - ADAM_B1) * g
    nv = ADAM_B2 * v + (1.0 - ADAM_B2) * (g * g)
    m_hat = nm / (1.0 - ADAM_B1 ** ADAM_STEP)
    v_hat = nv / (1.0 - ADAM_B2 ** ADAM_STEP)
    return -ADAM_LR * (m_hat / (jnp.sqrt(v_hat) + ADAM_EPS) + ADAM_WD * w), nm, nv


def _adamw(w, g, m, v, name):
    R, C = w.shape
    br = max(r for r in range(8, 257, 8) if R % r == 0)

    def body(w_ref, g_ref, m_ref, v_ref, d_ref, nm_ref, nv_ref):
        d_ref[...], nm_ref[...], nv_ref[...] = _adamw_update(w_ref[...], g_ref[...], m_ref[...], v_ref[...])

    spec = pl.BlockSpec((br, C), lambda i: (i, 0))
    return pl.pallas_call(
        body, grid=(R // br,), in_specs=[spec] * 4, out_specs=[spec] * 3,
        out_shape=[jax.ShapeDtypeStruct((R, C), F32)] * 3,
        compiler_params=_params(("arbitrary",)), name=name)(w, g, m, v)


def _adamw_small(g_packed, ws, ms, vs):
    n = len(ws)

    def body(*refs):
        g_ref = refs[0]
        w_refs, m_refs, v_refs = refs[1:1 + n], refs[1 + n:1 + 2 * n], refs[1 + 2 * n:1 + 3 * n]
        outs = refs[1 + 3 * n:]
        off = 0
        for i, (_, used, padded) in enumerate(_SMALL_PARTS[:n]):
            g = g_ref[off:off + used, :]
            delta, nm, nv = _adamw_update(w_refs[i][...], g, m_refs[i][...], v_refs[i][...])
            outs[4 * i][...], outs[4 * i + 1][...], outs[4 * i + 2][...], outs[4 * i + 3][...] = g, delta, nm, nv
            off += padded

    outs = pl.pallas_call(
        body, out_shape=[jax.ShapeDtypeStruct(w.shape, F32) for w in ws for _ in range(4)],
        compiler_params=_params(), name="adamw_small")(g_packed, *ws, *ms, *vs)
    return [outs[4 * i:4 * i + 4] for i in range(n)]


def _place():
    x, y, c = lax.axis_index("x"), lax.axis_index("y"), lax.axis_index("c")
    chip = 2 * x + y
    peers = [(x, 1 - y), (1 - x, y), (1 - x, 1 - y)]
    peer_chip = [2 * px + py for px, py in peers]
    return x, y, c, chip, peers, peer_chip


def _remote(src, dst, send_sem, recv_sem, dev):
    return pltpu.make_async_remote_copy(src_ref=src, dst_ref=dst, send_sem=send_sem, recv_sem=recv_sem,
                                        device_id=dev, device_id_type=MESH)


def _ag_weights(weights, late=()):
    nw, nl = len(weights), len(late)

    def body(*refs):
        srcs, late_srcs = refs[:nw], refs[nw:nw + nl]
        outs, late_bf, late_land = (refs[nw + nl:2 * nw + nl], refs[2 * nw + nl:2 * nw + 2 * nl],
                                    refs[2 * nw + 2 * nl:2 * nw + 3 * nl])
        s_ici, r_ici, s_d2d, r_d2d = refs[2 * nw + 3 * nl:]
        x, y, c = lax.axis_index("x"), lax.axis_index("y"), lax.axis_index("c")
        chip = 2 * x + y
        sib = (x, y, 1 - c)
        first = ((x + 1 - c) % 2, (y + c) % 2)
        second = ((x + c) % 2, (y + 1 - c) % 2)
        first_chip, second_chip = 2 * first[0] + first[1], 2 * second[0] + second[1]
        diag_chip = 3 - chip
        for src, out in zip(srcs, outs):
            out[chip] = src[...].astype(BF16)

        def half(out, k, cc):
            rows = out.shape[1] // 2
            return out.at[k, pl.ds(pl.multiple_of(cc * rows, 16), rows), :]

        def ici(w, slot, out, k, dev):
            blk = half(out, k, c)
            return _remote(blk, blk, s_ici.at[nw * slot + w], r_ici.at[nw * slot + w], (dev[0], dev[1], c))

        def d2d(w, slot, out, k, cc):
            blk = half(out, k, cc)
            return _remote(blk, blk, s_d2d.at[nw * slot + w], r_d2d.at[nw * slot + w], sib)

        sent = []
        for slot, dev in enumerate((first, second)):
            for w, out in enumerate(outs):
                sent.append(ici(w, slot, out, chip, dev))
                sent[-1].start()
        for src, bf, land in zip(late_srcs, late_bf, late_land):
            bf[...] = src[...].astype(BF16)
            land[...] = jnp.zeros_like(land)
            land[chip] = bf[...]
        for slot, k, dev in ((0, first_chip, first), (1, second_chip, second), (2, diag_chip, second)):
            for w, out in enumerate(outs):
                ici(w, slot, out, k, dev).wait_recv()
                if slot == 0:
                    sent.append(ici(w, 2, out, k, second))
                    sent[-1].start()
                sent.append(d2d(w, slot, out, k, c))
                sent[-1].start()
        for slot, k in ((0, second_chip), (1, first_chip), (2, diag_chip)):
            for w, out in enumerate(outs):
                d2d(w, slot, out, k, 1 - c).wait_recv()
        for cp in sent:
            cp.wait_send()

    vmem = pl.BlockSpec(memory_space=pltpu.VMEM)
    outs = pl.pallas_call(
        body,
        out_shape=[jax.ShapeDtypeStruct((N_CHIPS,) + w.shape, BF16) for w in weights]
        + [jax.ShapeDtypeStruct(w.shape, BF16) for w in late]
        + [jax.ShapeDtypeStruct((N_CHIPS,) + w.shape, BF16) for w in late],
        in_specs=[vmem] * (nw + nl), out_specs=[vmem] * (nw + 2 * nl),
        scratch_shapes=[pltpu.SemaphoreType.DMA((3 * nw,))] * 4,
        compiler_params=pltpu.CompilerParams(vmem_limit_bytes=VMEM_LIMIT), name="ag_weights")(*weights, *late)
    return outs[:nw], outs[nw:nw + nl], outs[nw + nl:]


_HBM = pl.BlockSpec(memory_space=pltpu.HBM)
_SEM = pl.BlockSpec(memory_space=pltpu.SEMAPHORE)
_ANY = pl.BlockSpec(memory_space=pl.ANY)
_DATAFLOW = pltpu.SideEffectType.DATAFLOW_SIDE_EFFECTING


def _in_hbm(a):
    return pltpu.with_memory_space_constraint(a, pltpu.HBM)


def _exchange_copies(gather, srcs, lands, send_sems, recv_sems):
    nw = len(srcs)
    x, y, c, chip, peers, peer_chip = _place()
    pairs = []
    for m, (px, py) in enumerate(peers):
        for w in range(nw):
            sems = (send_sems.at[nw * m + w], recv_sems.at[nw * m + w], (px, py, c))
            if gather:
                pairs.append((_remote(srcs[w], lands[w].at[chip], *sems),
                              _remote(srcs[w], lands[w].at[peer_chip[m]], *sems)))
            else:
                pairs.append((_remote(srcs[w].at[m], lands[w].at[m], *sems),) * 2)
    return pairs


def _exchange_start(gather, srcs, after, name, lands=None):
    nw = len(srcs)
    n_copies = 3 * nw

    def body(*refs):
        send_sems, recv_sems = refs[2 * nw + 1], refs[2 * nw + 2]
        for start, _ in _exchange_copies(gather, refs[:nw], refs[nw:2 * nw], send_sems, recv_sems):
            start.start()
        refs[-1][...] = jnp.zeros_like(refs[-1])

    if lands is None:
        lands = [lax.empty(((N_CHIPS,) + s.shape) if gather else s.shape, s.dtype) for s in srcs]
    lands = [_in_hbm(l) for l in lands]
    return pl.pallas_call(
        body, name=name,
        out_shape=(pltpu.SemaphoreType.DMA((n_copies,)), pltpu.SemaphoreType.DMA((n_copies,)))
        + tuple(pltpu.HBM(s.shape, s.dtype) for s in srcs)
        + tuple(pltpu.HBM(l.shape, l.dtype) for l in lands)
        + (jax.ShapeDtypeStruct((8, LANES), F32),),
        in_specs=[_HBM] * (2 * nw) + [_ANY],
        out_specs=(_SEM, _SEM) + (_HBM,) * (2 * nw) + (pl.BlockSpec(memory_space=pltpu.VMEM),),
        input_output_aliases={i: 2 + i for i in range(2 * nw)},
        compiler_params=pltpu.CompilerParams(has_side_effects=_DATAFLOW),
    )(*[_in_hbm(s) for s in srcs], *lands, after)


def _exchange_wait(gather, started, after, name):
    nw = (len(started) - 3) // 2
    send_sems, recv_sems = started[0], started[1]
    thru = started[2:2 + 2 * nw]

    def body(*refs):
        for _, arrival in _exchange_copies(gather, refs[:nw], refs[nw:2 * nw], refs[2 * nw], refs[2 * nw + 1]):
            arrival.wait_send()
            arrival.wait_recv()

    outs = pl.pallas_call(
        body, name=name,
        out_shape=tuple(pltpu.HBM(t.shape, t.dtype) for t in thru),
        in_specs=[_HBM] * (2 * nw) + [_SEM, _SEM, _ANY], out_specs=(_HBM,) * (2 * nw),
        input_output_aliases={i: i for i in range(2 * nw)},
        compiler_params=pltpu.CompilerParams(has_side_effects=_DATAFLOW),
    )(*thru, send_sems, recv_sems, after)
    return outs[nw:]


def _reduce_last(owns, landed, g_small):
    ns = len(owns)
    row_block = 32
    hs = SMALL_ROWS // 2

    def body(*refs):
        own_refs, land_refs, gsm_ref = refs[:ns], refs[ns:2 * ns], refs[2 * ns]
        out_refs, osm_ref = refs[2 * ns + 1:3 * ns + 1], refs[3 * ns + 1]
        ra_sm, p_sm, s_sem, r_sem, sm_s, sm_r = refs[3 * ns + 2:]
        x, y, c, chip, peers, peer_chip = _place()
        sib = (x, y, 1 - c)
        half = lambda cc: pl.ds(pl.multiple_of(cc * hs, 8), hs)
        sm_a = _remote(gsm_ref.at[half(1 - c), :], ra_sm, sm_s.at[0], sm_r.at[0], sib)
        sm_a.start()
        swaps = [sm_a]
        for w in range(ns):
            n = own_refs[w].shape[0]

            def total(i, carry, w=w, n=n):
                r0 = pl.multiple_of(i * row_block, row_block)
                blk = pl.ds(r0, row_block)
                acc = own_refs[w][blk, :]
                for m in range(3):
                    acc = acc + land_refs[w][m, blk, :].astype(F32)
                out_refs[w][pl.ds(pl.multiple_of(c * n + r0, 8), row_block), :] = acc
                return carry
            lax.fori_loop(0, n // row_block, total, 0)
            mine = out_refs[w].at[pl.ds(pl.multiple_of(c * n, 8), n), :]
            swaps.append(_remote(mine, mine, s_sem.at[w], r_sem.at[w], sib))
            swaps[-1].start()
        sm_a.wait_recv()
        p_sm[chip] = gsm_ref[half(c), :] + ra_sm[...]
        for m, (px, py) in enumerate(peers):
            swaps.append(_remote(p_sm.at[chip], p_sm.at[chip], sm_s.at[1 + m], sm_r.at[1 + m], (px, py, c)))
            swaps[-1].start()
        for w in range(ns):
            n = own_refs[w].shape[0]
            theirs = out_refs[w].at[pl.ds(pl.multiple_of((1 - c) * n, 8), n), :]
            _remote(theirs, theirs, s_sem.at[w], r_sem.at[w], sib).wait_recv()
        for m, (px, py) in enumerate(peers):
            _remote(p_sm.at[chip], p_sm.at[peer_chip[m]], sm_s.at[1 + m], sm_r.at[1 + m], (px, py, c)).wait_recv()
        osm_ref[half(c), :] = (p_sm[0] + p_sm[1]) + (p_sm[2] + p_sm[3])
        swaps.append(_remote(osm_ref.at[half(c), :], osm_ref.at[half(c), :], sm_s.at[4], sm_r.at[4], sib))
        swaps[-1].start()
        _remote(osm_ref.at[half(1 - c), :], osm_ref.at[half(1 - c), :], sm_s.at[4], sm_r.at[4], sib).wait_recv()
        for cp in swaps:
            cp.wait_send()

    vmem = pl.BlockSpec(memory_space=pltpu.VMEM)
    return pl.pallas_call(
        body, out_shape=[jax.ShapeDtypeStruct((2 * o.shape[0], o.shape[1]), F32) for o in owns]
        + [jax.ShapeDtypeStruct((SMALL_ROWS, LANES), F32)],
        in_specs=[vmem] * (2 * ns + 1), out_specs=[vmem] * (ns + 1),
        scratch_shapes=[pltpu.VMEM((hs, LANES), F32), pltpu.VMEM((N_CHIPS, hs, LANES), F32),
                        pltpu.SemaphoreType.DMA((ns,)), pltpu.SemaphoreType.DMA((ns,)),
                        pltpu.SemaphoreType.DMA((5,)), pltpu.SemaphoreType.DMA((5,))],
        compiler_params=pltpu.CompilerParams(vmem_limit_bytes=VMEM_LIMIT),
        name="reduce_last")(*owns, *landed, g_small)


_SMALL_PARTS = (("g_norm", 8, 8), ("w_s", 512, 512), ("b_s", 4, 8), ("g_v", 2, 8), ("g_mem", 8, 8),
                ("g_final", 8, 8), ("loss", 1, 8))
_LOSS_ROW = SMALL_ROWS - 8
assert sum(p for _, _, p in _SMALL_PARTS) == SMALL_ROWS


def _pack_small(parts, loss=None):
    loss_row = jnp.zeros((1, LANES), F32) if loss is None else jnp.broadcast_to(loss.reshape(1, 1), (1, LANES))
    rows = []
    for (name, used, padded), p in zip(_SMALL_PARTS, list(parts) + [loss_row]):
        p = p.reshape(used, LANES)
        if padded > used:
            p = jnp.pad(p, ((0, padded - used), (0, 0)))
        rows.append(p)
    return jnp.concatenate(rows, axis=0)


def _local_step(x, mem, target, g_norm, w_in, w_s, b_s, g_v, g_mem, late_weights, g_final,
                fwd_token=None, on_dw=None):
    B, S, _ = x.shape
    x2d = x.reshape(B * S, D_MODEL)
    t2d = target.reshape(B * S, D_MODEL)
    mem2d = mem.reshape(B * N_MEM, D_MODEL)

    proj = _inproj_fwd(x2d, g_norm, w_in, after=() if fwd_token is None else (fwd_token,))
    w_kv, w_out = late_weights(proj)
    kv = _kv_fwd(mem2d, g_mem, w_kv)
    a, lse = _attn_fwd(proj, B, S)
    w_sT = jnp.swapaxes(w_s, 1, 2)
    b_tab = jnp.repeat(b_s.T, HEAD_DIM, axis=1)
    (dx2, da, drest, loss, d_wout, d_ws, d_bs, d_gv, d_gf, dkv) = _mid(
        x2d, t2d, a, proj, kv, w_s, w_sT, b_tab, g_v, w_out, g_final, B, S)
    d_wkv, d_gmem = _kv_bwd(mem2d, g_mem, w_kv, dkv)
    dq, dk, dv = _attn_bwd(proj, a, lse, da, B, S)
    if on_dw is None:
        d_win = _inproj_bwd_dw(dq, dk, dv, drest, x2d, g_norm)
        after = ()
    else:
        d_win = None
        by_chip = lambda g: g.reshape((N_CHIPS, g.shape[0] // N_CHIPS, g.shape[1]))
        after = (on_dw(*_inproj_bwd_dw(dq, dk, dv, drest, x2d, g_norm, reduce_with=[by_chip(d_wkv), by_chip(d_wout)])),)
    grad_x, d_gnorm = _inproj_bwd_dx(dq, dk, dv, drest, x2d, dx2, g_norm, w_in, after=after)
    d_bs = d_bs[:, :N_SGU_GROUPS].T
    return (loss[0, 0], grad_x.reshape(B, S, D_MODEL),
            dict(g_norm=d_gnorm, w_in=d_win, w_s=d_ws, b_s=d_bs, g_v=d_gv, g_mem=d_gmem, w_kv=d_wkv,
                 w_out=d_wout, g_final=d_gf))


def kernel(x, mem, g_norm, w_in, w_sgu_spatial, b_sgu_spatial, g_sgu_v, g_mem, w_mem_kv, w_out, g_final, loss_target, m_g_norm, m_w_in, m_w_sgu_spatial, m_b_sgu_spatial, m_g_sgu_v, m_g_mem, m_w_mem_kv, m_w_out, m_g_final, v_g_norm, v_w_in, v_w_sgu_spatial, v_b_sgu_spatial, v_g_sgu_v, v_g_mem, v_w_mem_kv, v_w_out, v_g_final):
    t = lambda w: jnp.swapaxes(w[0], 0, 1)
    (win_all,), late_shards, late_lands = _ag_weights([t(w_in)], [w_mem_kv[0], w_out[0]])
    w_in_full = win_all.reshape(-1, win_all.shape[-1])
    late = _exchange_start(True, list(late_shards), win_all, "gather_late_start", lands=late_lands)

    def late_weights(proj):
        return [z.reshape(-1, z.shape[-1]) for z in _exchange_wait(True, late, proj, "gather_late_wait")]

    scatter = {}

    def on_dw(sends, owns):
        scatter["own"] = owns
        scatter["started"] = _exchange_start(False, list(sends), owns[0], "scatter_start")
        return scatter["started"][-1]

    loss, grad_x, g = _local_step(
        x, mem, loss_target, g_norm, w_in_full, w_sgu_spatial[0], b_sgu_spatial[0], g_sgu_v, g_mem,
        late_weights, g_final.reshape(1, D_MODEL), fwd_token=late[-1], on_dw=on_dw)

    small_names = ("g_norm", "w_s", "b_s", "g_v", "g_mem", "g_final")
    g_small = _pack_small([g[n] for n in small_names], loss)
    landed = _exchange_wait(False, scatter["started"], g_small, "scatter_wait")
    gr_in, gr_kv, gr_out, gr_small = _reduce_last(scatter["own"], landed, g_small)
    loss = gr_small[_LOSS_ROW, 0]

    small_w = (g_norm, w_sgu_spatial, b_sgu_spatial, g_sgu_v, g_mem, g_final)
    small_m = (m_g_norm, m_w_sgu_spatial, m_b_sgu_spatial, m_g_sgu_v, m_g_mem, m_g_final)
    small_v = (v_g_norm, v_w_sgu_spatial, v_b_sgu_spatial, v_g_sgu_v, v_g_mem, v_g_final)
    rows = lambda ws: [w.reshape(-1, LANES) for w in ws]
    small = [[z.reshape(w.shape) for z in four]
             for w, four in zip(small_w, _adamw_small(gr_small, rows(small_w), rows(small_m), rows(small_v)))]
    d_in, nm_in, nv_in = _adamw(t(w_in), gr_in, t(m_w_in), t(v_w_in), "adamw_w_in")
    gr_in, d_in, nm_in, nv_in = [jnp.swapaxes(z, 0, 1) for z in (gr_in, d_in, nm_in, nv_in)]
    d_kv, nm_kv, nv_kv = _adamw(w_mem_kv[0], gr_kv, m_w_mem_kv[0], v_w_mem_kv[0], "adamw_w_kv")
    d_out, nm_out, nv_out = _adamw(w_out[0], gr_out, m_w_out[0], v_w_out[0], "adamw_w_out")

    def leaves(kind, big_in, big_kv, big_out):
        s_norm, s_ws, s_bs, s_gv, s_gmem, s_gf = [four[kind] for four in small]
        return [s_norm, big_in[None], s_ws, s_bs, s_gv, s_gmem, big_kv[None], big_out[None], s_gf]

    return (loss, grad_x, *leaves(0, gr_in, gr_kv, gr_out), *leaves(1, d_in, d_kv, d_out),
            *leaves(2, nm_in, nm_kv, nm_out), *leaves(3, nv_in, nv_kv, nv_out))
```

```python
import functools

import jax
import jax.numpy as jnp
from jax import lax
from jax.experimental import pallas as pl
from jax.experimental.pallas import tpu as pltpu

F32 = jnp.float32
BF16 = jnp.bfloat16
MESH = pl.DeviceIdType.MESH

D_MODEL = 1024
ATTN_WIDTH = 512
SGU_WIDTH = 256
MEM_WIDTH = 256
N_MEM = 256
IN_COLS = 3328
QKV_COLS = 3 * ATTN_WIDTH
REST_COLS = IN_COLS - QKV_COLS
SGU_CHUNK = 128
N_SGU_GROUPS = 4
EPS = 1e-6
NEG_INF = -1e30
DILATIONS = (1, 4, 16)
RADIUS = 64
Q_BLOCK = 128
LANES = 128
HEAD_DIM = 64

ADAM_LR = 0.001
ADAM_B1 = 0.9
ADAM_B2 = 0.999
ADAM_EPS = 1e-08
ADAM_WD = 0.01
ADAM_STEP = 10

N_CHIPS = 4
VMEM_LIMIT = 56 * 1024 * 1024
SMALL_ROWS = 560


def _params(sem=None, vmem=VMEM_LIMIT):
    return pltpu.CompilerParams(dimension_semantics=sem, vmem_limit_bytes=vmem)


def _nn(a, b):
    return jnp.dot(a, b, preferred_element_type=F32)


def _nt(a, b):
    return lax.dot_general(a, b, (((1,), (1,)), ((), ())), preferred_element_type=F32)


def _tn(a, b):
    return lax.dot_general(a, b, (((0,), (0,)), ((), ())), preferred_element_type=F32)


def _rms(x):
    r = lax.rsqrt(jnp.mean(x * x, axis=-1, keepdims=True) + EPS)
    return r, x * r


def _head_masks():
    lane = lax.broadcasted_iota(jnp.int32, (1, LANES), 1)
    lo = lane < HEAD_DIM
    return lo, (lo.astype(F32), (~lo).astype(F32))


def _silu_parts(z):
    s = jax.nn.sigmoid(z)
    return z * s, s * (1.0 + z * (1.0 - s))


def _gelu_parts(x):
    c = 0.7978845608028654
    x2 = x * x
    s = jax.nn.sigmoid((2.0 * c) * (x + 0.044715 * (x * x2)))
    return x * s, s * (1.0 + x * (1.0 - s) * ((2.0 * c) * (1.0 + 3.0 * 0.044715 * x2)))


def _after(tokens):
    return [pl.BlockSpec(memory_space=pl.ANY)] * len(tokens)


def _inproj_fwd(x2d, g_norm, w_in_t, after=()):
    T = x2d.shape[0]
    tm = 512

    def body(x_ref, g_ref, w_ref, *rest):
        o_ref = rest[-1]
        _, xh = _rms(x_ref[...])
        h = (xh * g_ref[...]).astype(BF16)
        o_ref[...] = _nt(h, w_ref[...])

    return pl.pallas_call(
        body, grid=(T // tm,),
        in_specs=[pl.BlockSpec((tm, D_MODEL), lambda i: (i, 0)),
                  pl.BlockSpec((1, D_MODEL), lambda i: (0, 0)),
                  pl.BlockSpec((IN_COLS, D_MODEL), lambda i: (0, 0))] + _after(after),
        out_specs=pl.BlockSpec((tm, IN_COLS), lambda i: (i, 0)),
        out_shape=jax.ShapeDtypeStruct((T, IN_COLS), F32),
        compiler_params=_params(("arbitrary",)), name="inproj_fwd")(x2d, g_norm, w_in_t, *after)


def _kv_fwd(mem2d, g_mem, w_kv):
    Tm = mem2d.shape[0]

    def body(m_ref, g_ref, w_ref, o_ref):
        _, mh = _rms(m_ref[...])
        o_ref[...] = _nn((mh * g_ref[...]).astype(BF16), w_ref[...])

    return pl.pallas_call(
        body, out_shape=jax.ShapeDtypeStruct((Tm, 2 * MEM_WIDTH), F32),
        compiler_params=_params(), name="kv_fwd")(mem2d, g_mem, w_kv)


def _kv_bwd(mem2d, g_mem, w_kv, dkv):
    Tm = mem2d.shape[0]

    def body(m_ref, g_ref, w_ref, dkv_ref, dw_ref, dg_ref):
        _, mh = _rms(m_ref[...])
        memn = (mh * g_ref[...]).astype(BF16)
        dkvb = dkv_ref[...].astype(BF16)
        dw_ref[...] = _tn(memn, dkvb)
        dmemn = _nt(dkvb, w_ref[...])
        dg_ref[...] = jnp.sum(dmemn * mh, axis=0, keepdims=True)

    return pl.pallas_call(
        body, out_shape=(jax.ShapeDtypeStruct((D_MODEL, 2 * MEM_WIDTH), F32),
                         jax.ShapeDtypeStruct((1, D_MODEL), F32)),
        compiler_params=_params(), name="kv_bwd")(mem2d, g_mem, w_kv, dkv)


def _attn_geometry(S):
    geom = []
    for d in DILATIONS:
        L = S // d
        assert L % Q_BLOCK == 0
        geom.append((d, L, min(2 * Q_BLOCK, L), L // Q_BLOCK))
    return geom


def _init_bias(bias_scr, geom, hp):
    row = lax.broadcasted_iota(jnp.int32, (Q_BLOCK, 2 * Q_BLOCK), 0)
    col = lax.broadcasted_iota(jnp.int32, (Q_BLOCK, 2 * Q_BLOCK), 1)
    for j in (0, 1):
        bits = (126 - (2 * hp + j)) * (1 << 23)
        slope = lax.bitcast_convert_type(jnp.full((1, 1), bits, jnp.int32), F32)
        for di, (d, _, _, _) in enumerate(geom):
            for cls, off in enumerate((0, -RADIUS, -2 * RADIUS)):
                dist = jnp.abs(col - row + off)
                bias_scr[di * 6 + cls * 2 + j] = jnp.where(
                    dist <= RADIUS, -(slope * float(d)) * dist.astype(F32), NEG_INF)


SPLIT = 4
COPY_ROWS = 256


def _by4_rows(S, step):
    per_class = S // SPLIT // COPY_ROWS
    r, j = step // per_class, step % per_class
    return (pl.ds(r + SPLIT * j * COPY_ROWS, COPY_ROWS, stride=SPLIT),
            pl.ds(pl.multiple_of(r * (S // SPLIT) + j * COPY_ROWS, COPY_ROWS), COPY_ROWS))


def _to_by4(src, dst, S):
    def step(i, carry):
        natural, by4 = _by4_rows(S, i)
        dst[by4, :] = src[natural, :]
        return carry
    lax.fori_loop(0, S // COPY_ROWS, step, 0)


def _block_slices(d, L, KW, nqb, r, qb, S):
    qs = qb * Q_BLOCK
    ks = jnp.clip(qs - RADIUS, 0, L - KW)
    cls = jnp.where(qb == 0, 0, jnp.where(qb == nqb - 1, 2, 1))
    if d == 1:
        qsl = pl.ds(pl.multiple_of(qs, Q_BLOCK), Q_BLOCK)
        ksl = pl.ds(pl.multiple_of(ks, RADIUS), KW)
    elif d == SPLIT:
        qsl = pl.ds(pl.multiple_of(r * L + qs, Q_BLOCK), Q_BLOCK)
        ksl = pl.ds(pl.multiple_of(r * L + ks, RADIUS), KW)
    else:
        sub = d // SPLIT
        base = (r % SPLIT) * (S // SPLIT) + r // SPLIT
        qsl = pl.ds(base + qs * sub, Q_BLOCK, stride=sub)
        ksl = pl.ds(base + ks * sub, KW, stride=sub)
    return qsl, ksl, cls


def _for_groups(geom, S, group, fn):
    for di, (d, L, KW, nqb) in enumerate(geom):
        assert (d * nqb) % group == 0

        def step(it, carry, di=di, d=d, L=L, KW=KW, nqb=nqb):
            slices = []
            for g in range(group):
                i = it * group + g
                slices.append(_block_slices(d, L, KW, nqb, i // nqb, i % nqb, S))
            fn(di, KW, slices)
            return carry
        lax.fori_loop(0, d * nqb // group, step, 0)


def _attn_fwd(proj, B, S):
    T = B * S
    geom = _attn_geometry(S)
    n_pairs = ATTN_WIDTH // LANES

    def body(q_ref, k_ref, v_ref, a_ref, lse_ref, bias_scr, q4, k4, v4, *per_dilation):
        o_scr, m_scr, l_scr = per_dilation[0:3], per_dilation[3:6], per_dilation[6:9]
        lo, hm = _head_masks()
        pair = pl.program_id(0)

        @pl.when(pl.program_id(1) == 0)
        def _():
            _init_bias(bias_scr, geom, pair)
        for src, dst in ((q_ref, q4), (k_ref, k4), (v_ref, v4)):
            _to_by4(src, dst, S)

        def group(di, KW, slices):
            chains = [(g, j) for g in range(len(slices)) for j in (0, 1)]
            q_src, k_src, v_src = (q_ref, k_ref, v_ref) if di == 0 else (q4, k4, v4)
            q = [q_src[qsl, :] for qsl, _, _ in slices]
            kw = [k_src[ksl, :].astype(BF16) for _, ksl, _ in slices]
            vw = [v_src[ksl, :].astype(BF16) for _, ksl, _ in slices]
            s = {(g, j): _nt((q[g] * (hm[j] * 0.125)).astype(BF16), kw[g])
                 + bias_scr[di * 6 + slices[g][2] * 2 + j, :, pl.ds(0, KW)] for g, j in chains}
            m = {c: jnp.max(s[c], axis=1, keepdims=True) for c in chains}
            p = {c: jnp.exp(s[c] - m[c]) for c in chains}
            l = {c: jnp.sum(p[c], axis=1, keepdims=True) for c in chains}
            o = {(g, j): _nn(p[(g, j)].astype(BF16), vw[g]) for g, j in chains}
            for g, (qsl, _, _) in enumerate(slices):
                o_scr[di][qsl, :] = jnp.where(lo, o[(g, 0)], o[(g, 1)])
                m_scr[di][qsl, :] = jnp.where(lo, m[(g, 0)], m[(g, 1)])
                l_scr[di][qsl, :] = jnp.where(lo, l[(g, 0)], l[(g, 1)])

        _for_groups(geom, S, 8, group)

        def combine(i, carry):
            natural, by4 = _by4_rows(S, i)
            rows = [natural, by4, by4]
            ms = [m_scr[di][rows[di], :] for di in range(3)]
            mx = jnp.maximum(jnp.maximum(ms[0], ms[1]), ms[2])
            num = 0.0
            den = 0.0
            for di in range(3):
                w = jnp.exp(ms[di] - mx)
                num = num + w * o_scr[di][rows[di], :]
                den = den + w * l_scr[di][rows[di], :]
            a_ref[natural, :] = num / den
            lse_ref[natural, :] = mx + jnp.log(den)
            return carry

        lax.fori_loop(0, S // COPY_ROWS, combine, 0)

    blk = lambda off: pl.BlockSpec((S, LANES), lambda h, b, off=off: (b, off + h))
    out_blk = pl.BlockSpec((S, LANES), lambda h, b: (b, h))
    return pl.pallas_call(
        body, grid=(n_pairs, B),
        in_specs=[blk(0), blk(n_pairs), blk(2 * n_pairs)],
        out_specs=[out_blk, out_blk],
        out_shape=[jax.ShapeDtypeStruct((T, ATTN_WIDTH), F32)] * 2,
        scratch_shapes=[pltpu.VMEM((18, Q_BLOCK, 2 * Q_BLOCK), F32)] + [pltpu.VMEM((S, LANES), F32)] * 12,
        compiler_params=_params(("arbitrary", "arbitrary")), name="attn_fwd")(proj, proj, proj)


def _attn_bwd(proj, a, lse, da, B, S):
    T = B * S
    geom = _attn_geometry(S)
    n_pairs = ATTN_WIDTH // LANES

    def body(q_ref, k_ref, v_ref, a_ref, lse_ref, do_ref, dq_ref, dk_ref, dv_ref, bias_scr, *scr):
        acc = (scr[0:3], scr[3:6])
        natural_in = (q_ref, k_ref, v_ref, a_ref, lse_ref, do_ref)
        by4_in = scr[6:12]
        _, hm = _head_masks()
        pair = pl.program_id(0)

        @pl.when(pl.program_id(1) == 0)
        def _():
            _init_bias(bias_scr, geom, pair)
        for ref in scr[0:6]:
            ref[...] = jnp.zeros_like(ref)
        for src, dst in zip(natural_in, by4_in):
            _to_by4(src, dst, S)

        def group(di, KW, slices):
            n = len(slices)
            chains = [(g, j) for g in range(n) for j in (0, 1)]
            q_src, k_src, v_src, a_src, lse_src, do_src = natural_in if di == 0 else by4_in
            dq_scr, dk_scr, dv_scr = acc[0 if di == 0 else 1]
            q = [q_src[qsl, :] for qsl, _, _ in slices]
            do = [do_src[qsl, :] for qsl, _, _ in slices]
            doa = [do[g] * a_src[slices[g][0], :] for g in range(n)]
            lse_q = [lse_src[qsl, :] for qsl, _, _ in slices]
            kw = [k_src[ksl, :].astype(BF16) for _, ksl, _ in slices]
            vw = [v_src[ksl, :].astype(BF16) for _, ksl, _ in slices]
            qj = {(g, j): (q[g] * (hm[j] * 0.125)).astype(BF16) for g, j in chains}
            doj = {(g, j): (do[g] * hm[j]).astype(BF16) for g, j in chains}
            s = {(g, j): _nt(qj[(g, j)], kw[g])
                 + bias_scr[di * 6 + slices[g][2] * 2 + j, :, pl.ds(0, KW)] for g, j in chains}
            dp = {(g, j): _nt(doj[(g, j)], vw[g]) for g, j in chains}
            dsum = {(g, j): jnp.sum(doa[g] * hm[j], axis=1, keepdims=True) for g, j in chains}
            p = {(g, j): jnp.exp(s[(g, j)] - lse_q[g][:, HEAD_DIM * j:HEAD_DIM * j + 1]) for g, j in chains}
            ds = {c: (p[c] * (dp[c] - dsum[c])).astype(BF16) for c in chains}
            pb = {c: p[c].astype(BF16) for c in chains}
            dq = [_nn(ds[(g, 0)], kw[g]) * (hm[0] * 0.125) + _nn(ds[(g, 1)], kw[g]) * (hm[1] * 0.125)
                  for g in range(n)]
            both = lambda t, g: jnp.concatenate([t[(g, 0)], t[(g, 1)]], axis=0)
            dkw = [_tn(both(ds, g), both(qj, g)) for g in range(n)]
            dvw = [_tn(both(pb, g), both(doj, g)) for g in range(n)]
            for g, (qsl, ksl, _) in enumerate(slices):
                dq_scr[qsl, :] = dq_scr[qsl, :] + dq[g]
                dk_scr[ksl, :] = dk_scr[ksl, :] + dkw[g]
                dv_scr[ksl, :] = dv_scr[ksl, :] + dvw[g]

        _for_groups(geom, S, 4, group)

        def merge(i, carry):
            natural, by4 = _by4_rows(S, i)
            for nat, split in zip(*acc):
                nat[natural, :] = nat[natural, :] + split[by4, :]
            return carry
        lax.fori_loop(0, S // COPY_ROWS, merge, 0)
        for out, nat in zip((dq_ref, dk_ref, dv_ref), acc[0]):
            out[...] = nat[...].astype(BF16)

    blk = lambda off: pl.BlockSpec((S, LANES), lambda h, b, off=off: (b, off + h))
    return pl.pallas_call(
        body, grid=(n_pairs, B),
        in_specs=[blk(0), blk(n_pairs), blk(2 * n_pairs), blk(0), blk(0), blk(0)],
        out_specs=[blk(0), blk(0), blk(0)],
        out_shape=[jax.ShapeDtypeStruct((T, ATTN_WIDTH), BF16)] * 3,
        scratch_shapes=[pltpu.VMEM((18, Q_BLOCK, 2 * Q_BLOCK), F32)] + [pltpu.VMEM((S, LANES), F32)] * 12,
        compiler_params=_params(("arbitrary", "arbitrary")), name="attn_bwd")(proj, proj, proj, a, lse, da)


def _mid(x2d, t2d, a, proj, kv, w_s, w_sT, b_tab, g_v, w_out, g_final, B, S):
    T = B * S
    tm = 512
    nt = S // tm
    halves = 2
    hrows = tm // halves

    def body(x_ref, t_ref, a_ref, za_ref, ub_ref, vb_ref, zb_ref, qm_ref, zm_ref, kv_ref,
              ws_ref, wsT_ref, btab_ref, gv_ref, wout_ref, gf_ref,
              dx2_ref, da_ref, drest_ref, loss_ref, dwout_ref, dws_ref, dbs_ref, dgv_ref, dgf_ref, dkv_ref,
              dbtab_scr):
        b = pl.program_id(0)
        t = pl.program_id(1)
        first = jnp.logical_and(b == 0, t == 0)
        last = jnp.logical_and(b == B - 1, t == nt - 1)
        _, hm = _head_masks()
        lane_g = lax.broadcasted_iota(jnp.int32, (1, SGU_WIDTH), 1) // HEAD_DIM
        gm = [(lane_g == g).astype(F32) for g in range(N_SGU_GROUPS)]
        H = range(halves)
        rows = [pl.ds(h * hrows, hrows) for h in H]
        ld = lambda ref: [ref[r, :] for r in rows]
        cat = lambda parts, axis: jnp.concatenate(parts, axis=axis)
        chunks = [slice(ci * SGU_CHUNK, (ci + 1) * SGU_CHUNK) for ci in range(hrows // SGU_CHUNK)]
        pairs = [slice(pr * LANES, (pr + 1) * LANES) for pr in range(2)]
        heads = [(pr, j) for pr in range(2) for j in (0, 1)]

        @pl.when(first)
        def _():
            loss_ref[...] = jnp.zeros_like(loss_ref)
            dwout_ref[...] = jnp.zeros_like(dwout_ref)
            dws_ref[...] = jnp.zeros_like(dws_ref)
            dbs_ref[...] = jnp.zeros_like(dbs_ref)
            dgv_ref[...] = jnp.zeros_like(dgv_ref)
            dgf_ref[...] = jnp.zeros_like(dgf_ref)
            dbtab_scr[...] = jnp.zeros_like(dbtab_scr)

        @pl.when(t == 0)
        def _():
            dkv_ref[...] = jnp.zeros_like(dkv_ref)

        a_val = ld(a_ref)
        sil_a = [_silu_parts(z) for z in ld(za_ref)]
        gated_a = [s[0] * a for s, a in zip(sil_a, a_val)]
        u = [_gelu_parts(z) for z in ld(ub_ref)]
        vv = [_gelu_parts(z) for z in ld(vb_ref)]
        vnorm = [_rms(v[0]) for v in vv]
        gv = gv_ref[...]
        vn = [(n[1] * gv).astype(BF16) for n in vnorm]
        w_cat = cat([ws_ref[g].astype(BF16) for g in range(N_SGU_GROUPS)], 1)
        wT_cat = cat([wsT_ref[g].astype(BF16) for g in range(N_SGU_GROUPS)], 1)
        gmb = [m.astype(BF16) for m in gm]
        by_group = lambda chunk: cat([chunk * gmb[g] for g in range(N_SGU_GROUPS)], 0)
        btab = btab_ref[...]
        mixed = [cat([btab + _nn(w_cat, by_group(vn[h][c, :])) for c in chunks], 0) for h in H]
        sg = [u[h][0] * mixed[h] for h in H]
        sil_b = [_silu_parts(z) for z in ld(zb_ref)]
        gated_b = [sil_b[h][0] * sg[h] for h in H]

        kvv = kv_ref[...].astype(BF16)
        kp = [kvv[:, p] for p in pairs]
        vp = [kvv[:, MEM_WIDTH + pr * LANES:MEM_WIDTH + (pr + 1) * LANES] for pr in range(2)]
        qm = ld(qm_ref)
        qj = {(h, pr, j): (qm[h][:, pairs[pr]] * (hm[j] * 0.125)).astype(BF16) for h in H for pr, j in heads}
        sc = {k: _nt(qj[k], kp[k[1]]) for k in qj}
        ex = {k: jnp.exp(sc[k] - jnp.max(sc[k], axis=1, keepdims=True)) for k in qj}
        prob = {k: ex[k] * (1.0 / jnp.sum(ex[k], axis=1, keepdims=True)) for k in qj}
        probb = {k: prob[k].astype(BF16) for k in qj}
        mo = [cat([sum(_nn(probb[(h, pr, j)], vp[pr]) * hm[j] for j in (0, 1)) for pr in range(2)], 1) for h in H]
        sil_m = [_silu_parts(z) for z in ld(zm_ref)]
        gated_m = [sil_m[h][0] * mo[h] for h in H]

        gated = [cat([gated_a[h], gated_b[h], gated_m[h]], 1).astype(BF16) for h in H]
        wout = wout_ref[...]
        x_in = ld(x_ref)
        x2 = [x_in[h] + _nn(gated[h], wout) for h in H]
        fin = [_rms(z) for z in x2]
        gf = gf_ref[...]
        tgt = ld(t_ref)
        err = [fin[h][1] * gf - tgt[h] for h in H]
        loss_ref[...] += sum(jnp.sum(e * e) for e in err) * (0.5 / D_MODEL)

        dy = [e * (1.0 / D_MODEL) for e in err]
        dgf_ref[...] += sum(jnp.sum(dy[h] * fin[h][1], axis=0, keepdims=True) for h in H)
        gdy = [d * gf for d in dy]
        dx2 = [fin[h][0] * (gdy[h] - fin[h][1] * jnp.mean(gdy[h] * fin[h][1], axis=1, keepdims=True)) for h in H]
        for h in H:
            dx2_ref[rows[h], :] = dx2[h]
        dx2b = [d.astype(BF16) for d in dx2]
        dgated = [_nt(d, wout) for d in dx2b]
        dwout_ref[...] += _tn(cat(gated, 0), cat(dx2b, 0))
        dga = [d[:, 0:ATTN_WIDTH] for d in dgated]
        dgb = [d[:, ATTN_WIDTH:ATTN_WIDTH + SGU_WIDTH] for d in dgated]
        dgm = [d[:, ATTN_WIDTH + SGU_WIDTH:] for d in dgated]

        for h in H:
            da_ref[rows[h], :] = dga[h] * sil_a[h][0]
        dza = [dga[h] * a_val[h] * sil_a[h][1] for h in H]

        dsg = [dgb[h] * sil_b[h][0] for h in H]
        dzb = [dgb[h] * sg[h] * sil_b[h][1] for h in H]
        dub = [dsg[h] * mixed[h] * u[h][1] for h in H]
        dmixed = [dsg[h] * u[h][0] for h in H]
        dmixed_b = [d.astype(BF16) for d in dmixed]
        dvn = [cat([_nn(wT_cat, by_group(dmixed_b[h][c, :])) for c in chunks], 0) for h in H]
        for g in range(N_SGU_GROUPS):
            dws_ref[g] += sum(_nt((dmixed[h][c, :] * gm[g]).astype(BF16), vn[h][c, :]) for h in H for c in chunks)
        dbtab_scr[...] += sum(dmixed[h][c, :] for h in H for c in chunks)
        dgv_ref[...] += sum(jnp.sum(dvn[h] * vnorm[h][1], axis=0, keepdims=True) for h in H)
        tv = [d * gv for d in dvn]
        dvv = [vnorm[h][0] * (tv[h] - vnorm[h][1] * jnp.mean(tv[h] * vnorm[h][1], axis=1, keepdims=True)) for h in H]
        dvb = [dvv[h] * vv[h][1] for h in H]

        dmo = [dgm[h] * sil_m[h][0] for h in H]
        dzm = [dgm[h] * mo[h] * sil_m[h][1] for h in H]
        dmoj = {(h, pr, j): (dmo[h][:, pairs[pr]] * hm[j]).astype(BF16) for h in H for pr, j in heads}
        dp = {k: _nt(dmoj[k], vp[k[1]]) for k in qj}
        ds = {k: (prob[k] * (dp[k] - jnp.sum(dp[k] * prob[k], axis=1, keepdims=True))).astype(BF16) for k in qj}
        dqm = [cat([sum(_nn(ds[(h, pr, j)], kp[pr]) * (hm[j] * 0.125) for j in (0, 1)) for pr in range(2)], 1)
               for h in H]
        every = lambda tbl, pr: cat([tbl[(h, pr, j)] for h in H for j in (0, 1)], 0)
        dk = [_tn(every(ds, pr), every(qj, pr)) for pr in range(2)]
        dv = [_tn(every(probb, pr), every(dmoj, pr)) for pr in range(2)]
        dkv_ref[...] += cat(dk + dv, 1)

        for h in H:
            drest_ref[rows[h], :] = cat([dza[h], dub[h], dvb[h], dzb[h], dqm[h], dzm[h]], 1).astype(BF16)

        @pl.when(last)
        def _():
            lane = lax.broadcasted_iota(jnp.int32, (1, LANES), 1)
            dbt = dbtab_scr[...]
            out = jnp.zeros((SGU_CHUNK, LANES), F32)
            for g in range(N_SGU_GROUPS):
                out = out + jnp.where(lane == g, jnp.sum(dbt * gm[g], axis=1, keepdims=True), 0.0)
            dbs_ref[...] = out

    tile = lambda w, cb: pl.BlockSpec((tm, w), lambda b, t, cb=cb: (b * nt + t, cb))
    const = lambda shape: pl.BlockSpec(shape, lambda b, t, n=len(shape): (0,) * n)
    return pl.pallas_call(
        body, grid=(B, nt),
        in_specs=[tile(D_MODEL, 0), tile(D_MODEL, 0), tile(ATTN_WIDTH, 0),
                  tile(ATTN_WIDTH, 3),
                  tile(SGU_WIDTH, 8), tile(SGU_WIDTH, 9), tile(SGU_WIDTH, 10),
                  tile(MEM_WIDTH, 11), tile(MEM_WIDTH, 12),
                  pl.BlockSpec((N_MEM, 2 * MEM_WIDTH), lambda b, t: (b, 0)),
                  const((N_SGU_GROUPS, SGU_CHUNK, SGU_CHUNK)), const((N_SGU_GROUPS, SGU_CHUNK, SGU_CHUNK)),
                  const((SGU_CHUNK, SGU_WIDTH)), const((1, SGU_WIDTH)),
                  const((D_MODEL, D_MODEL)), const((1, D_MODEL))],
        out_specs=[tile(D_MODEL, 0), tile(ATTN_WIDTH, 0), tile(REST_COLS, 0),
                   const((8, LANES)), const((D_MODEL, D_MODEL)),
                   const((N_SGU_GROUPS, SGU_CHUNK, SGU_CHUNK)), const((SGU_CHUNK, LANES)),
                   const((1, SGU_WIDTH)), const((1, D_MODEL)),
                   pl.BlockSpec((N_MEM, 2 * MEM_WIDTH), lambda b, t: (b, 0))],
        out_shape=[jax.ShapeDtypeStruct((T, D_MODEL), F32), jax.ShapeDtypeStruct((T, ATTN_WIDTH), F32),
                   jax.ShapeDtypeStruct((T, REST_COLS), BF16),
                   jax.ShapeDtypeStruct((8, LANES), F32), jax.ShapeDtypeStruct((D_MODEL, D_MODEL), F32),
                   jax.ShapeDtypeStruct((N_SGU_GROUPS, SGU_CHUNK, SGU_CHUNK), F32),
                   jax.ShapeDtypeStruct((SGU_CHUNK, LANES), F32),
                   jax.ShapeDtypeStruct((1, SGU_WIDTH), F32), jax.ShapeDtypeStruct((1, D_MODEL), F32),
                   jax.ShapeDtypeStruct((B * N_MEM, 2 * MEM_WIDTH), F32)],
        scratch_shapes=[pltpu.VMEM((SGU_CHUNK, SGU_WIDTH), F32)],
        compiler_params=_params(("arbitrary", "arbitrary")), name="mid")(
            x2d, t2d, a, proj, proj, proj, proj, proj, proj, kv, w_s, w_sT, b_tab, g_v, w_out, g_final)


def _inproj_bwd_dx(dq, dk, dv, drest, x2d, dx2, g_norm, w_in_t, after=()):
    T = x2d.shape[0]
    tm = 512
    W = ATTN_WIDTH

    def body(dq_ref, dk_ref, dv_ref, dr_ref, x_ref, dx2_ref, g_ref, w_ref, *rest):
        gx_ref, dg_ref = rest[-2:]

        @pl.when(pl.program_id(0) == 0)
        def _():
            dg_ref[...] = jnp.zeros_like(dg_ref)

        halves = [pl.ds(h * (tm // 2), tm // 2) for h in (0, 1)]
        dh = [(_nn(dq_ref[r, :], w_ref[0:W, :]) + _nn(dk_ref[r, :], w_ref[W:2 * W, :])
               + _nn(dv_ref[r, :], w_ref[2 * W:3 * W, :]) + _nn(dr_ref[r, :], w_ref[QKV_COLS:IN_COLS, :]))
              for r in halves]
        nrm = [_rms(x_ref[r, :]) for r in halves]
        dg_ref[...] += sum(jnp.sum(d * n[1], axis=0, keepdims=True) for d, n in zip(dh, nrm))
        g = g_ref[...]
        for r, d, (rstd, xh) in zip(halves, dh, nrm):
            th = d * g
            gx_ref[r, :] = rstd * (th - xh * jnp.mean(th * xh, axis=1, keepdims=True)) + dx2_ref[r, :]

    tile = lambda w: pl.BlockSpec((tm, w), lambda i: (i, 0))
    return pl.pallas_call(
        body, grid=(T // tm,),
        in_specs=[tile(W), tile(W), tile(W), tile(REST_COLS), tile(D_MODEL), tile(D_MODEL),
                  pl.BlockSpec((1, D_MODEL), lambda i: (0, 0)),
                  pl.BlockSpec((IN_COLS, D_MODEL), lambda i: (0, 0))] + _after(after),
        out_specs=[tile(D_MODEL), pl.BlockSpec((1, D_MODEL), lambda i: (0, 0))],
        out_shape=[jax.ShapeDtypeStruct((T, D_MODEL), F32), jax.ShapeDtypeStruct((1, D_MODEL), F32)],
        compiler_params=_params(("arbitrary",)), name="inproj_bwd_dx")(
            dq, dk, dv, drest, x2d, dx2, g_norm, w_in_t, *after)


def _inproj_bwd_dw(dq, dk, dv, drest, x2d, g_norm, reduce_with=None):
    T = x2d.shape[0]
    tm = 512
    nt = T // tm
    W = ATTN_WIDTH
    fused = reduce_with is not None
    others = list(reduce_with) if fused else []
    ns = 1 + len(others)
    shard = IN_COLS // N_CHIPS
    halves = [shard // 2] + [s.shape[1] // 2 for s in others]
    cols = [D_MODEL] + [s.shape[2] for s in others]
    row_block = 32

    def body(dq_ref, dk_ref, dv_ref, dr_ref, x_ref, g_ref, *rest):
        if fused:
            stacks = rest[:ns - 1]
            sends, owns = rest[ns - 1:2 * ns - 1], rest[2 * ns - 1:3 * ns - 1]
            acc, ras, narrow = rest[3 * ns - 1], rest[3 * ns:4 * ns], rest[4 * ns]
            s_sem, r_sem = rest[4 * ns + 1], rest[4 * ns + 2]
            x, y, c, chip, peers, peer_chip = _place()
            sib = (x, y, 1 - c)

            def part(w, k, cc, r0=0, rows=None):
                n = halves[w]
                rows = n if rows is None else rows
                if w == 0:
                    return acc.at[pl.ds(pl.multiple_of(k * shard + cc * n + r0, 8), rows), :]
                return stacks[w - 1].at[k, pl.ds(pl.multiple_of(cc * n + r0, 8), rows), :]

            def swap_other(w):
                theirs = stacks[w - 1].at[:, pl.ds(pl.multiple_of((1 - c) * halves[w], 8), halves[w]), :]
                return _remote(theirs, ras[w], s_sem.at[N_CHIPS - 1 + w], r_sem.at[N_CHIPS - 1 + w], sib)

            def swap_win(k):
                return _remote(narrow.at[k], ras[0].at[k], s_sem.at[k], r_sem.at[k], sib)
        else:
            acc = rest[0]

        @pl.when(pl.program_id(0) == 0)
        def _():
            acc[...] = jnp.zeros_like(acc)
            for w in range(1, ns):
                swap_other(w).start()

        _, xh = _rms(x_ref[...])
        h = (xh * g_ref[...]).astype(BF16)
        acc[0:W, :] += _tn(dq_ref[...], h)
        acc[W:2 * W, :] += _tn(dk_ref[...], h)
        acc[2 * W:3 * W, :] += _tn(dv_ref[...], h)
        acc[QKV_COLS:IN_COLS, :] += _tn(dr_ref[...], h)

        if fused:
            @pl.when(pl.program_id(0) == nt - 1)
            def _():
                for k in range(N_CHIPS):
                    def to_bf16(i, carry, k=k):
                        r0 = pl.multiple_of(i * row_block, row_block)
                        narrow[k, pl.ds(r0, row_block), :] = part(0, k, 1 - c, r0, row_block)[...].astype(BF16)
                        return carry
                    lax.fori_loop(0, halves[0] // row_block, to_bf16, 0)
                    swap_win(k).start()
                for w in list(range(1, ns)) + [0]:
                    if w == 0:
                        for k in range(N_CHIPS):
                            swap_win(k).wait_recv()
                    else:
                        swap_other(w).wait_recv()

                    def sums(i, carry, w=w):
                        r0 = pl.multiple_of(i * row_block, row_block)
                        blk = pl.ds(r0, row_block)
                        for m in range(3):
                            k = peer_chip[m]
                            sends[w][m, blk, :] = (part(w, k, c, r0, row_block)[...]
                                                   + ras[w][k, blk, :].astype(F32)).astype(BF16)
                        owns[w][blk, :] = part(w, chip, c, r0, row_block)[...] + ras[w][chip, blk, :].astype(F32)
                        return carry
                    lax.fori_loop(0, halves[w] // row_block, sums, 0)
                for k in range(N_CHIPS):
                    swap_win(k).wait_send()
                for w in range(1, ns):
                    swap_other(w).wait_send()

    tile = lambda w: pl.BlockSpec((tm, w), lambda i: (i, 0))
    vmem = pl.BlockSpec(memory_space=pltpu.VMEM)
    in_specs = [tile(W), tile(W), tile(W), tile(REST_COLS), tile(D_MODEL), pl.BlockSpec((1, D_MODEL), lambda i: (0, 0))]
    if not fused:
        return pl.pallas_call(
            body, grid=(nt,), in_specs=in_specs,
            out_specs=pl.BlockSpec((IN_COLS, D_MODEL), lambda i: (0, 0)),
            out_shape=jax.ShapeDtypeStruct((IN_COLS, D_MODEL), F32),
            compiler_params=_params(("arbitrary",)), name="inproj_bwd_dw")(dq, dk, dv, drest, x2d, g_norm)
    outs = pl.pallas_call(
        body, grid=(nt,), in_specs=in_specs + [vmem] * (ns - 1), out_specs=[vmem] * (2 * ns),
        out_shape=[jax.ShapeDtypeStruct((3, n, cl), BF16) for n, cl in zip(halves, cols)]
        + [jax.ShapeDtypeStruct((n, cl), F32) for n, cl in zip(halves, cols)],
        scratch_shapes=[pltpu.VMEM((IN_COLS, D_MODEL), F32)]
        + [pltpu.VMEM((N_CHIPS, n, cl), BF16 if w == 0 else F32) for w, (n, cl) in enumerate(zip(halves, cols))]
        + [pltpu.VMEM((N_CHIPS, halves[0], D_MODEL), BF16)]
        + [pltpu.SemaphoreType.DMA((N_CHIPS - 1 + ns,)), pltpu.SemaphoreType.DMA((N_CHIPS - 1 + ns,))],
        compiler_params=_params(("arbitrary",)), name="inproj_bwd_dw_reduce")(
            dq, dk, dv, drest, x2d, g_norm, *others)
    return outs[:ns], outs[ns:]


def _adamw_update(w, g, m, v):
    nm = ADAM_B1 * m + (1.0 - ADAM_B1) * g
    nv = ADAM_B2 * v + (1.0 - ADAM_B2) * (g * g)
    m_hat = nm / (1.0 - ADAM_B1 ** ADAM_STEP)
    v_hat = nv / (1.0 - ADAM_B2 ** ADAM_STEP)
    return -ADAM_LR * (m_hat / (jnp.sqrt(v_hat) + ADAM_EPS) + ADAM_WD * w), nm, nv


def _adamw(w, g, m, v, name):
    R, C = w.shape
    br = max(r for r in range(8, 257, 8) if R % r == 0)

    def body(w_ref, g_ref, m_ref, v_ref, d_ref, nm_ref, nv_ref):
        d_ref[...], nm_ref[...], nv_ref[...] = _adamw_update(w_ref[...], g_ref[...], m_ref[...], v_ref[...])

    spec = pl.BlockSpec((br, C), lambda i: (i, 0))
    return pl.pallas_call(
        body, grid=(R // br,), in_specs=[spec] * 4, out_specs=[spec] * 3,
        out_shape=[jax.ShapeDtypeStruct((R, C), F32)] * 3,
        compiler_params=_params(("arbitrary",)), name=name)(w, g, m, v)


def _place():
    x, y, c = lax.axis_index("x"), lax.axis_index("y"), lax.axis_index("c")
    chip = 2 * x + y
    peers = [(x, 1 - y), (1 - x, y), (1 - x, 1 - y)]
    peer_chip = [2 * px + py for px, py in peers]
    return x, y, c, chip, peers, peer_chip


def _remote(src, dst, send_sem, recv_sem, dev):
    return pltpu.make_async_remote_copy(src_ref=src, dst_ref=dst, send_sem=send_sem, recv_sem=recv_sem,
                                        device_id=dev, device_id_type=MESH)


def _ag_weights(weights, late=()):
    nw, nl = len(weights), len(late)

    def body(*refs):
        srcs, late_srcs = refs[:nw], refs[nw:nw + nl]
        outs, late_bf, late_land = (refs[nw + nl:2 * nw + nl], refs[2 * nw + nl:2 * nw + 2 * nl],
                                    refs[2 * nw + 2 * nl:2 * nw + 3 * nl])
        s_ici, r_ici, s_d2d, r_d2d = refs[2 * nw + 3 * nl:]
        x, y, c = lax.axis_index("x"), lax.axis_index("y"), lax.axis_index("c")
        chip = 2 * x + y
        sib = (x, y, 1 - c)
        first = ((x + 1 - c) % 2, (y + c) % 2)
        second = ((x + c) % 2, (y + 1 - c) % 2)
        first_chip, second_chip = 2 * first[0] + first[1], 2 * second[0] + second[1]
        diag_chip = 3 - chip
        for src, out in zip(srcs, outs):
            out[chip] = src[...].astype(BF16)

        def half(out, k, cc):
            rows = out.shape[1] // 2
            return out.at[k, pl.ds(pl.multiple_of(cc * rows, 16), rows), :]

        def ici(w, slot, out, k, dev):
            blk = half(out, k, c)
            return _remote(blk, blk, s_ici.at[nw * slot + w], r_ici.at[nw * slot + w], (dev[0], dev[1], c))

        def d2d(w, slot, out, k, cc):
            blk = half(out, k, cc)
            return _remote(blk, blk, s_d2d.at[nw * slot + w], r_d2d.at[nw * slot + w], sib)

        sent = []
        for slot, dev in enumerate((first, second)):
            for w, out in enumerate(outs):
                sent.append(ici(w, slot, out, chip, dev))
                sent[-1].start()
        for src, bf, land in zip(late_srcs, late_bf, late_land):
            bf[...] = src[...].astype(BF16)
            land[...] = jnp.zeros_like(land)
            land[chip] = bf[...]
        for slot, k, dev in ((0, first_chip, first), (1, second_chip, second), (2, diag_chip, second)):
            for w, out in enumerate(outs):
                ici(w, slot, out, k, dev).wait_recv()
                if slot == 0:
                    sent.append(ici(w, 2, out, k, second))
                    sent[-1].start()
                sent.append(d2d(w, slot, out, k, c))
                sent[-1].start()
        for slot, k in ((0, second_chip), (1, first_chip), (2, diag_chip)):
            for w, out in enumerate(outs):
                d2d(w, slot, out, k, 1 - c).wait_recv()
        for cp in sent:
            cp.wait_send()

    vmem = pl.BlockSpec(memory_space=pltpu.VMEM)
    outs = pl.pallas_call(
        body,
        out_shape=[jax.ShapeDtypeStruct((N_CHIPS,) + w.shape, BF16) for w in weights]
        + [jax.ShapeDtypeStruct(w.shape, BF16) for w in late]
        + [jax.ShapeDtypeStruct((N_CHIPS,) + w.shape, BF16) for w in late],
        in_specs=[vmem] * (nw + nl), out_specs=[vmem] * (nw + 2 * nl),
        scratch_shapes=[pltpu.SemaphoreType.DMA((3 * nw,))] * 4,
        compiler_params=pltpu.CompilerParams(vmem_limit_bytes=VMEM_LIMIT), name="ag_weights")(*weights, *late)
    return outs[:nw], outs[nw:nw + nl], outs[nw + nl:]


_HBM = pl.BlockSpec(memory_space=pltpu.HBM)
_SEM = pl.BlockSpec(memory_space=pltpu.SEMAPHORE)
_ANY = pl.BlockSpec(memory_space=pl.ANY)
_DATAFLOW = pltpu.SideEffectType.DATAFLOW_SIDE_EFFECTING


def _in_hbm(a):
    return pltpu.with_memory_space_constraint(a, pltpu.HBM)


def _exchange_copies(gather, srcs, lands, send_sems, recv_sems):
    nw = len(srcs)
    x, y, c, chip, peers, peer_chip = _place()
    pairs = []
    for m, (px, py) in enumerate(peers):
        for w in range(nw):
            sems = (send_sems.at[nw * m + w], recv_sems.at[nw * m + w], (px, py, c))
            if gather:
                pairs.append((_remote(srcs[w], lands[w].at[chip], *sems),
                              _remote(srcs[w], lands[w].at[peer_chip[m]], *sems)))
            else:
                pairs.append((_remote(srcs[w].at[m], lands[w].at[m], *sems),) * 2)
    return pairs


def _exchange_start(gather, srcs, after, name, lands=None):
    nw = len(srcs)
    n_copies = 3 * nw

    def body(*refs):
        send_sems, recv_sems = refs[2 * nw + 1], refs[2 * nw + 2]
        for start, _ in _exchange_copies(gather, refs[:nw], refs[nw:2 * nw], send_sems, recv_sems):
            start.start()
        refs[-1][...] = jnp.zeros_like(refs[-1])

    if lands is None:
        lands = [lax.empty(((N_CHIPS,) + s.shape) if gather else s.shape, s.dtype) for s in srcs]
    lands = [_in_hbm(l) for l in lands]
    return pl.pallas_call(
        body, name=name,
        out_shape=(pltpu.SemaphoreType.DMA((n_copies,)), pltpu.SemaphoreType.DMA((n_copies,)))
        + tuple(pltpu.HBM(s.shape, s.dtype) for s in srcs)
        + tuple(pltpu.HBM(l.shape, l.dtype) for l in lands)
        + (jax.ShapeDtypeStruct((8, LANES), F32),),
        in_specs=[_HBM] * (2 * nw) + [_ANY],
        out_specs=(_SEM, _SEM) + (_HBM,) * (2 * nw) + (pl.BlockSpec(memory_space=pltpu.VMEM),),
        input_output_aliases={i: 2 + i for i in range(2 * nw)},
        compiler_params=pltpu.CompilerParams(has_side_effects=_DATAFLOW),
    )(*[_in_hbm(s) for s in srcs], *lands, after)


def _exchange_wait(gather, started, after, name):
    nw = (len(started) - 3) // 2
    send_sems, recv_sems = started[0], started[1]
    thru = started[2:2 + 2 * nw]

    def body(*refs):
        for _, arrival in _exchange_copies(gather, refs[:nw], refs[nw:2 * nw], refs[2 * nw], refs[2 * nw + 1]):
            arrival.wait_send()
            arrival.wait_recv()

    outs = pl.pallas_call(
        body, name=name,
        out_shape=tuple(pltpu.HBM(t.shape, t.dtype) for t in thru),
        in_specs=[_HBM] * (2 * nw) + [_SEM, _SEM, _ANY], out_specs=(_HBM,) * (2 * nw),
        input_output_aliases={i: i for i in range(2 * nw)},
        compiler_params=pltpu.CompilerParams(has_side_effects=_DATAFLOW),
    )(*thru, send_sems, recv_sems, after)
    return outs[nw:]


def _reduce_last(owns, landed, g_small, late_wmv, small_wmv):
    ns = len(owns)
    nl = len(late_wmv)
    n_small = len(small_wmv[0])
    row_block = 32
    hs = SMALL_ROWS // 2

    def body(*refs):
        own_refs, land_refs, gsm_ref = refs[:ns], refs[ns:2 * ns], refs[2 * ns]
        n_in = 2 * ns + 1
        late_in = [refs[n_in + 3 * i:n_in + 3 * i + 3] for i in range(nl)]
        n_in += 3 * nl
        small_in = [refs[n_in + n_small * i:n_in + n_small * (i + 1)] for i in range(3)]
        n_in += 3 * n_small
        out_refs, osm_ref = refs[n_in:n_in + ns], refs[n_in + ns]
        late_out = [refs[n_in + ns + 1 + 3 * i:n_in + ns + 4 + 3 * i] for i in range(nl)]
        n_out = n_in + ns + 1 + 3 * nl
        small_out = [refs[n_out + 4 * i:n_out + 4 * i + 4] for i in range(n_small)]
        ra_sm, p_sm, s_sem, r_sem, sm_s, sm_r = refs[n_out + 4 * n_small:]
        x, y, c, chip, peers, peer_chip = _place()
        sib = (x, y, 1 - c)
        half = lambda cc: pl.ds(pl.multiple_of(cc * hs, 8), hs)
        sm_a = _remote(gsm_ref.at[half(1 - c), :], ra_sm, sm_s.at[0], sm_r.at[0], sib)
        sm_a.start()
        swaps = [sm_a]
        for w in range(ns):
            n = own_refs[w].shape[0]

            def total(i, carry, w=w, n=n):
                r0 = pl.multiple_of(i * row_block, row_block)
                blk = pl.ds(r0, row_block)
                acc = own_refs[w][blk, :]
                for m in range(3):
                    acc = acc + land_refs[w][m, blk, :].astype(F32)
                out_refs[w][pl.ds(pl.multiple_of(c * n + r0, 8), row_block), :] = acc
                return carry
            lax.fori_loop(0, n // row_block, total, 0)
            mine = out_refs[w].at[pl.ds(pl.multiple_of(c * n, 8), n), :]
            swaps.append(_remote(mine, mine, s_sem.at[w], r_sem.at[w], sib))
            swaps[-1].start()
        sm_a.wait_recv()
        p_sm[chip] = gsm_ref[half(c), :] + ra_sm[...]
        for m, (px, py) in enumerate(peers):
            swaps.append(_remote(p_sm.at[chip], p_sm.at[chip], sm_s.at[1 + m], sm_r.at[1 + m], (px, py, c)))
            swaps[-1].start()
        for m, (px, py) in enumerate(peers):
            _remote(p_sm.at[chip], p_sm.at[peer_chip[m]], sm_s.at[1 + m], sm_r.at[1 + m], (px, py, c)).wait_recv()
        osm_ref[half(c), :] = (p_sm[0] + p_sm[1]) + (p_sm[2] + p_sm[3])
        swaps.append(_remote(osm_ref.at[half(c), :], osm_ref.at[half(c), :], sm_s.at[4], sm_r.at[4], sib))
        swaps[-1].start()
        for w in list(range(1, ns)) + [0]:
            n = own_refs[w].shape[0]
            theirs = out_refs[w].at[pl.ds(pl.multiple_of((1 - c) * n, 8), n), :]
            _remote(theirs, theirs, s_sem.at[w], r_sem.at[w], sib).wait_recv()
            if w > 0:
                w_ref, m_ref, v_ref = late_in[w - 1]

                def update(i, carry, w=w, w_ref=w_ref, m_ref=m_ref, v_ref=v_ref):
                    blk = pl.ds(pl.multiple_of(i * row_block, row_block), row_block)
                    new = _adamw_update(w_ref[blk, :], out_refs[w][blk, :], m_ref[blk, :], v_ref[blk, :])
                    for ref, val in zip(late_out[w - 1], new):
                        ref[blk, :] = val
                    return carry
                lax.fori_loop(0, 2 * n // row_block, update, 0)
        _remote(osm_ref.at[half(1 - c), :], osm_ref.at[half(1 - c), :], sm_s.at[4], sm_r.at[4], sib).wait_recv()
        off = 0
        for i, (_, used, padded) in enumerate(_SMALL_PARTS[:n_small]):
            g = osm_ref[off:off + used, :]
            new = _adamw_update(small_in[0][i][...], g, small_in[1][i][...], small_in[2][i][...])
            for ref, val in zip(small_out[i], (g,) + new):
                ref[...] = val
            off += padded
        for cp in swaps:
            cp.wait_send()

    vmem = pl.BlockSpec(memory_space=pltpu.VMEM)
    late_flat = [a for wmv in late_wmv for a in wmv]
    small_flat = [a for part in small_wmv for a in part]
    outs = pl.pallas_call(
        body, out_shape=[jax.ShapeDtypeStruct((2 * o.shape[0], o.shape[1]), F32) for o in owns]
        + [jax.ShapeDtypeStruct((SMALL_ROWS, LANES), F32)]
        + [jax.ShapeDtypeStruct(wmv[0].shape, F32) for wmv in late_wmv for _ in range(3)]
        + [jax.ShapeDtypeStruct(w.shape, F32) for w in small_wmv[0] for _ in range(4)],
        in_specs=[vmem] * (2 * ns + 1 + len(late_flat) + len(small_flat)),
        out_specs=[vmem] * (ns + 1 + 3 * nl + 4 * n_small),
        scratch_shapes=[pltpu.VMEM((hs, LANES), F32), pltpu.VMEM((N_CHIPS, hs, LANES), F32),
                        pltpu.SemaphoreType.DMA((ns,)), pltpu.SemaphoreType.DMA((ns,)),
                        pltpu.SemaphoreType.DMA((5,)), pltpu.SemaphoreType.DMA((5,))],
        compiler_params=pltpu.CompilerParams(vmem_limit_bytes=VMEM_LIMIT),
        name="reduce_last")(*owns, *landed, g_small, *late_flat, *small_flat)
    large, packed = outs[:ns], outs[ns]
    late = [outs[ns + 1 + 3 * i:ns + 4 + 3 * i] for i in range(nl)]
    small = [outs[ns + 1 + 3 * nl + 4 * i:ns + 5 + 3 * nl + 4 * i] for i in range(n_small)]
    return large, packed, late, small


_SMALL_PARTS = (("g_norm", 8, 8), ("w_s", 512, 512), ("b_s", 4, 8), ("g_v", 2, 8), ("g_mem", 8, 8),
                ("g_final", 8, 8), ("loss", 1, 8))
_LOSS_ROW = SMALL_ROWS - 8
assert sum(p for _, _, p in _SMALL_PARTS) == SMALL_ROWS


def _pack_small(parts, loss=None):
    loss_row = jnp.zeros((1, LANES), F32) if loss is None else jnp.broadcast_to(loss.reshape(1, 1), (1, LANES))
    rows = []
    for (name, used, padded), p in zip(_SMALL_PARTS, list(parts) + [loss_row]):
        p = p.reshape(used, LANES)
        if padded > used:
            p = jnp.pad(p, ((0, padded - used), (0, 0)))
        rows.append(p)
    return jnp.concatenate(rows, axis=0)


def _local_step(x, mem, target, g_norm, w_in, w_s, b_s, g_v, g_mem, late_weights, g_final,
                fwd_token=None, on_dw=None):
    B, S, _ = x.shape
    x2d = x.reshape(B * S, D_MODEL)
    t2d = target.reshape(B * S, D_MODEL)
    mem2d = mem.reshape(B * N_MEM, D_MODEL)

    proj = _inproj_fwd(x2d, g_norm, w_in, after=() if fwd_token is None else (fwd_token,))
    w_kv, w_out = late_weights(proj)
    kv = _kv_fwd(mem2d, g_mem, w_kv)
    a, lse = _attn_fwd(proj, B, S)
    w_sT = jnp.swapaxes(w_s, 1, 2)
    b_tab = jnp.repeat(b_s.T, HEAD_DIM, axis=1)
    (dx2, da, drest, loss, d_wout, d_ws, d_bs, d_gv, d_gf, dkv) = _mid(
        x2d, t2d, a, proj, kv, w_s, w_sT, b_tab, g_v, w_out, g_final, B, S)
    d_wkv, d_gmem = _kv_bwd(mem2d, g_mem, w_kv, dkv)
    dq, dk, dv = _attn_bwd(proj, a, lse, da, B, S)
    if on_dw is None:
        d_win = _inproj_bwd_dw(dq, dk, dv, drest, x2d, g_norm)
        after = ()
    else:
        d_win = None
        by_chip = lambda g: g.reshape((N_CHIPS, g.shape[0] // N_CHIPS, g.shape[1]))
        after = (on_dw(*_inproj_bwd_dw(dq, dk, dv, drest, x2d, g_norm, reduce_with=[by_chip(d_wkv), by_chip(d_wout)])),)
    grad_x, d_gnorm = _inproj_bwd_dx(dq, dk, dv, drest, x2d, dx2, g_norm, w_in, after=after)
    d_bs = d_bs[:, :N_SGU_GROUPS].T
    return (loss[0, 0], grad_x.reshape(B, S, D_MODEL),
            dict(g_norm=d_gnorm, w_in=d_win, w_s=d_ws, b_s=d_bs, g_v=d_gv, g_mem=d_gmem, w_kv=d_wkv,
                 w_out=d_wout, g_final=d_gf))


def kernel(x, mem, g_norm, w_in, w_sgu_spatial, b_sgu_spatial, g_sgu_v, g_mem, w_mem_kv, w_out, g_final, loss_target, m_g_norm, m_w_in, m_w_sgu_spatial, m_b_sgu_spatial, m_g_sgu_v, m_g_mem, m_w_mem_kv, m_w_out, m_g_final, v_g_norm, v_w_in, v_w_sgu_spatial, v_b_sgu_spatial, v_g_sgu_v, v_g_mem, v_w_mem_kv, v_w_out, v_g_final):
    t = lambda w: jnp.swapaxes(w[0], 0, 1)
    (win_all,), late_shards, late_lands = _ag_weights([t(w_in)], [w_mem_kv[0], w_out[0]])
    w_in_full = win_all.reshape(-1, win_all.shape[-1])
    late = _exchange_start(True, list(late_shards), win_all, "gather_late_start", lands=late_lands)

    def late_weights(proj):
        return [z.reshape(-1, z.shape[-1]) for z in _exchange_wait(True, late, proj, "gather_late_wait")]

    scatter = {}

    def on_dw(sends, owns):
        scatter["own"] = owns
        scatter["started"] = _exchange_start(False, list(sends), owns[0], "scatter_start")
        return scatter["started"][-1]

    loss, grad_x, g = _local_step(
        x, mem, loss_target, g_norm, w_in_full, w_sgu_spatial[0], b_sgu_spatial[0], g_sgu_v, g_mem,
        late_weights, g_final.reshape(1, D_MODEL), fwd_token=late[-1], on_dw=on_dw)

    small_names = ("g_norm", "w_s", "b_s", "g_v", "g_mem", "g_final")
    g_small = _pack_small([g[n] for n in small_names], loss)
    landed = _exchange_wait(False, scatter["started"], g_small, "scatter_wait")
    small_w = (g_norm, w_sgu_spatial, b_sgu_spatial, g_sgu_v, g_mem, g_final)
    small_m = (m_g_norm, m_w_sgu_spatial, m_b_sgu_spatial, m_g_sgu_v, m_g_mem, m_g_final)
    small_v = (v_g_norm, v_w_sgu_spatial, v_b_sgu_spatial, v_g_sgu_v, v_g_mem, v_g_final)
    rows = lambda ws: [w.reshape(-1, LANES) for w in ws]
    (gr_in, gr_kv, gr_out), gr_small, late_new, small_new = _reduce_last(
        scatter["own"], landed, g_small,
        [(w_mem_kv[0], m_w_mem_kv[0], v_w_mem_kv[0]), (w_out[0], m_w_out[0], v_w_out[0])],
        (rows(small_w), rows(small_m), rows(small_v)))
    loss = gr_small[_LOSS_ROW, 0]
    small = [[z.reshape(w.shape) for z in four] for w, four in zip(small_w, small_new)]
    (d_kv, nm_kv, nv_kv), (d_out, nm_out, nv_out) = late_new
    d_in, nm_in, nv_in = _adamw(t(w_in), gr_in, t(m_w_in), t(v_w_in), "adamw_w_in")
    gr_in, d_in, nm_in, nv_in = [jnp.swapaxes(z, 0, 1) for z in (gr_in, d_in, nm_in, nv_in)]

    def leaves(kind, big_in, big_kv, big_out):
        s_norm, s_ws, s_bs, s_gv, s_gmem, s_gf = [four[kind] for four in small]
        return [s_norm, big_in[None], s_ws, s_bs, s_gv, s_gmem, big_kv[None], big_out[None], s_gf]

    return (loss, grad_x, *leaves(0, gr_in, gr_kv, gr_out), *leaves(1, d_in, d_kv, d_out),
            *leaves(2, nm_in, nm_kv, nm_out), *leaves(3, nv_in, nv_kv, nv_out))
```

```python
import functools

import jax
import jax.numpy as jnp
from jax import lax
from jax.experimental import pallas as pl
from jax.experimental.pallas import tpu as pltpu

F32 = jnp.float32
BF16 = jnp.bfloat16
MESH = pl.DeviceIdType.MESH

D_MODEL = 1024
ATTN_WIDTH = 512
SGU_WIDTH = 256
MEM_WIDTH = 256
N_MEM = 256
IN_COLS = 3328
QKV_COLS = 3 * ATTN_WIDTH
REST_COLS = IN_COLS - QKV_COLS
SGU_CHUNK = 128
N_SGU_GROUPS = 4
EPS = 1e-6
NEG_INF = -1e30
DILATIONS = (1, 4, 16)
RADIUS = 64
Q_BLOCK = 128
LANES = 128
HEAD_DIM = 64

ADAM_LR = 0.001
ADAM_B1 = 0.9
ADAM_B2 = 0.999
ADAM_EPS = 1e-08
ADAM_WD = 0.01
ADAM_STEP = 10

N_CHIPS = 4
VMEM_LIMIT = 56 * 1024 * 1024
SMALL_ROWS = 560


def _params(sem=None, vmem=VMEM_LIMIT):
    return pltpu.CompilerParams(dimension_semantics=sem, vmem_limit_bytes=vmem)


def _nn(a, b):
    return jnp.dot(a, b, preferred_element_type=F32)


def _nt(a, b):
    return lax.dot_general(a, b, (((1,), (1,)), ((), ())), preferred_element_type=F32)


def _tn(a, b):
    return lax.dot_general(a, b, (((0,), (0,)), ((), ())), preferred_element_type=F32)


def _rms(x):
    r = lax.rsqrt(jnp.mean(x * x, axis=-1, keepdims=True) + EPS)
    return r, x * r


def _head_masks():
    lane = lax.broadcasted_iota(jnp.int32, (1, LANES), 1)
    lo = lane < HEAD_DIM
    return lo, (lo.astype(F32), (~lo).astype(F32))


def _silu_parts(z):
    s = jax.nn.sigmoid(z)
    return z * s, s * (1.0 + z * (1.0 - s))


def _gelu_parts(x):
    c = 0.7978845608028654
    x2 = x * x
    s = jax.nn.sigmoid((2.0 * c) * (x + 0.044715 * (x * x2)))
    return x * s, s * (1.0 + x * (1.0 - s) * ((2.0 * c) * (1.0 + 3.0 * 0.044715 * x2)))


def _after(tokens):
    return [pl.BlockSpec(memory_space=pl.ANY)] * len(tokens)


def _inproj_fwd(x2d, g_norm, w_in_t, after=()):
    T = x2d.shape[0]
    tm = 512

    def body(x_ref, g_ref, w_ref, *rest):
        o_ref = rest[-1]
        _, xh = _rms(x_ref[...])
        h = (xh * g_ref[...]).astype(BF16)
        o_ref[...] = _nt(h, w_ref[...])

    return pl.pallas_call(
        body, grid=(T // tm,),
        in_specs=[pl.BlockSpec((tm, D_MODEL), lambda i: (i, 0)),
                  pl.BlockSpec((1, D_MODEL), lambda i: (0, 0)),
                  pl.BlockSpec((IN_COLS, D_MODEL), lambda i: (0, 0))] + _after(after),
        out_specs=pl.BlockSpec((tm, IN_COLS), lambda i: (i, 0)),
        out_shape=jax.ShapeDtypeStruct((T, IN_COLS), F32),
        compiler_params=_params(("arbitrary",)), name="inproj_fwd")(x2d, g_norm, w_in_t, *after)


def _kv_fwd(mem2d, g_mem, w_kv):
    Tm = mem2d.shape[0]

    def body(m_ref, g_ref, w_ref, o_ref):
        _, mh = _rms(m_ref[...])
        o_ref[...] = _nn((mh * g_ref[...]).astype(BF16), w_ref[...])

    return pl.pallas_call(
        body, out_shape=jax.ShapeDtypeStruct((Tm, 2 * MEM_WIDTH), F32),
        compiler_params=_params(), name="kv_fwd")(mem2d, g_mem, w_kv)


def _kv_bwd(mem2d, g_mem, w_kv, dkv):
    Tm = mem2d.shape[0]

    def body(m_ref, g_ref, w_ref, dkv_ref, dw_ref, dg_ref):
        _, mh = _rms(m_ref[...])
        memn = (mh * g_ref[...]).astype(BF16)
        dkvb = dkv_ref[...].astype(BF16)
        dw_ref[...] = _tn(memn, dkvb)
        dmemn = _nt(dkvb, w_ref[...])
        dg_ref[...] = jnp.sum(dmemn * mh, axis=0, keepdims=True)

    return pl.pallas_call(
        body, out_shape=(jax.ShapeDtypeStruct((D_MODEL, 2 * MEM_WIDTH), F32),
                         jax.ShapeDtypeStruct((1, D_MODEL), F32)),
        compiler_params=_params(), name="kv_bwd")(mem2d, g_mem, w_kv, dkv)


def _attn_geometry(S):
    geom = []
    for d in DILATIONS:
        L = S // d
        assert L % Q_BLOCK == 0
        geom.append((d, L, min(2 * Q_BLOCK, L), L // Q_BLOCK))
    return geom


def _init_bias(bias_scr, geom, hp):
    row = lax.broadcasted_iota(jnp.int32, (Q_BLOCK, 2 * Q_BLOCK), 0)
    col = lax.broadcasted_iota(jnp.int32, (Q_BLOCK, 2 * Q_BLOCK), 1)
    for j in (0, 1):
        bits = (126 - (2 * hp + j)) * (1 << 23)
        slope = lax.bitcast_convert_type(jnp.full((1, 1), bits, jnp.int32), F32)
        for di, (d, _, _, _) in enumerate(geom):
            for cls, off in enumerate((0, -RADIUS, -2 * RADIUS)):
                dist = jnp.abs(col - row + off)
                bias_scr[di * 6 + cls * 2 + j] = jnp.where(
                    dist <= RADIUS, -(slope * float(d)) * dist.astype(F32), NEG_INF)


SPLIT = 4
COPY_ROWS = 256


def _by4_rows(S, step):
    per_class = S // SPLIT // COPY_ROWS
    r, j = step // per_class, step % per_class
    return (pl.ds(r + SPLIT * j * COPY_ROWS, COPY_ROWS, stride=SPLIT),
            pl.ds(pl.multiple_of(r * (S // SPLIT) + j * COPY_ROWS, COPY_ROWS), COPY_ROWS))


def _to_by4(src, dst, S):
    def step(i, carry):
        natural, by4 = _by4_rows(S, i)
        dst[by4, :] = src[natural, :]
        return carry
    lax.fori_loop(0, S // COPY_ROWS, step, 0)


def _block_slices(d, L, KW, nqb, r, qb, S):
    qs = qb * Q_BLOCK
    ks = jnp.clip(qs - RADIUS, 0, L - KW)
    cls = jnp.where(qb == 0, 0, jnp.where(qb == nqb - 1, 2, 1))
    if d == 1:
        qsl = pl.ds(pl.multiple_of(qs, Q_BLOCK), Q_BLOCK)
        ksl = pl.ds(pl.multiple_of(ks, RADIUS), KW)
    elif d == SPLIT:
        qsl = pl.ds(pl.multiple_of(r * L + qs, Q_BLOCK), Q_BLOCK)
        ksl = pl.ds(pl.multiple_of(r * L + ks, RADIUS), KW)
    else:
        sub = d // SPLIT
        base = (r % SPLIT) * (S // SPLIT) + r // SPLIT
        qsl = pl.ds(base + qs * sub, Q_BLOCK, stride=sub)
        ksl = pl.ds(base + ks * sub, KW, stride=sub)
    return qsl, ksl, cls


def _for_groups(geom, S, group, fn):
    for di, (d, L, KW, nqb) in enumerate(geom):
        assert (d * nqb) % group == 0

        def step(it, carry, di=di, d=d, L=L, KW=KW, nqb=nqb):
            slices = []
            for g in range(group):
                i = it * group + g
                slices.append(_block_slices(d, L, KW, nqb, i // nqb, i % nqb, S))
            fn(di, KW, slices)
            return carry
        lax.fori_loop(0, d * nqb // group, step, 0)


def _attn_fwd(proj, B, S):
    T = B * S
    geom = _attn_geometry(S)
    n_pairs = ATTN_WIDTH // LANES

    def body(q_ref, k_ref, v_ref, a_ref, lse_ref, bias_scr, q4, k4, v4, *per_dilation):
        o_scr, m_scr, l_scr = per_dilation[0:3], per_dilation[3:6], per_dilation[6:9]
        lo, hm = _head_masks()
        pair = pl.program_id(0)

        @pl.when(pl.program_id(1) == 0)
        def _():
            _init_bias(bias_scr, geom, pair)
        for src, dst in ((q_ref, q4), (k_ref, k4), (v_ref, v4)):
            _to_by4(src, dst, S)

        def group(di, KW, slices):
            chains = [(g, j) for g in range(len(slices)) for j in (0, 1)]
            q_src, k_src, v_src = (q_ref, k_ref, v_ref) if di == 0 else (q4, k4, v4)
            q = [q_src[qsl, :] for qsl, _, _ in slices]
            kw = [k_src[ksl, :].astype(BF16) for _, ksl, _ in slices]
            vw = [v_src[ksl, :].astype(BF16) for _, ksl, _ in slices]
            s = {(g, j): _nt((q[g] * (hm[j] * 0.125)).astype(BF16), kw[g])
                 + bias_scr[di * 6 + slices[g][2] * 2 + j, :, pl.ds(0, KW)] for g, j in chains}
            m = {c: jnp.max(s[c], axis=1, keepdims=True) for c in chains}
            p = {c: jnp.exp(s[c] - m[c]) for c in chains}
            l = {c: jnp.sum(p[c], axis=1, keepdims=True) for c in chains}
            o = {(g, j): _nn(p[(g, j)].astype(BF16), vw[g]) for g, j in chains}
            for g, (qsl, _, _) in enumerate(slices):
                o_scr[di][qsl, :] = jnp.where(lo, o[(g, 0)], o[(g, 1)])
                m_scr[di][qsl, :] = jnp.where(lo, m[(g, 0)], m[(g, 1)])
                l_scr[di][qsl, :] = jnp.where(lo, l[(g, 0)], l[(g, 1)])

        _for_groups(geom, S, 8, group)

        def combine(i, carry):
            natural, by4 = _by4_rows(S, i)
            rows = [natural, by4, by4]
            ms = [m_scr[di][rows[di], :] for di in range(3)]
            mx = jnp.maximum(jnp.maximum(ms[0], ms[1]), ms[2])
            num = 0.0
            den = 0.0
            for di in range(3):
                w = jnp.exp(ms[di] - mx)
                num = num + w * o_scr[di][rows[di], :]
                den = den + w * l_scr[di][rows[di], :]
            a_ref[natural, :] = num / den
            lse_ref[natural, :] = mx + jnp.log(den)
            return carry

        lax.fori_loop(0, S // COPY_ROWS, combine, 0)

    blk = lambda off: pl.BlockSpec((S, LANES), lambda h, b, off=off: (b, off + h))
    out_blk = pl.BlockSpec((S, LANES), lambda h, b: (b, h))
    return pl.pallas_call(
        body, grid=(n_pairs, B),
        in_specs=[blk(0), blk(n_pairs), blk(2 * n_pairs)],
        out_specs=[out_blk, out_blk],
        out_shape=[jax.ShapeDtypeStruct((T, ATTN_WIDTH), F32)] * 2,
        scratch_shapes=[pltpu.VMEM((18, Q_BLOCK, 2 * Q_BLOCK), F32)] + [pltpu.VMEM((S, LANES), F32)] * 12,
        compiler_params=_params(("arbitrary", "arbitrary")), name="attn_fwd")(proj, proj, proj)


def _attn_bwd(proj, a, lse, da, B, S):
    T = B * S
    geom = _attn_geometry(S)
    n_pairs = ATTN_WIDTH // LANES

    def body(q_ref, k_ref, v_ref, a_ref, lse_ref, do_ref, dq_ref, dk_ref, dv_ref, bias_scr, *scr):
        acc = (scr[0:3], scr[3:6])
        natural_in = (q_ref, k_ref, v_ref, a_ref, lse_ref, do_ref)
        by4_in = scr[6:12]
        _, hm = _head_masks()
        pair = pl.program_id(0)

        @pl.when(pl.program_id(1) == 0)
        def _():
            _init_bias(bias_scr, geom, pair)
        for ref in scr[0:6]:
            ref[...] = jnp.zeros_like(ref)
        for src, dst in zip(natural_in, by4_in):
            _to_by4(src, dst, S)

        def group(di, KW, slices):
            n = len(slices)
            chains = [(g, j) for g in range(n) for j in (0, 1)]
            q_src, k_src, v_src, a_src, lse_src, do_src = natural_in if di == 0 else by4_in
            dq_scr, dk_scr, dv_scr = acc[0 if di == 0 else 1]
            q = [q_src[qsl, :] for qsl, _, _ in slices]
            do = [do_src[qsl, :] for qsl, _, _ in slices]
            doa = [do[g] * a_src[slices[g][0], :] for g in range(n)]
            lse_q = [lse_src[qsl, :] for qsl, _, _ in slices]
            kw = [k_src[ksl, :].astype(BF16) for _, ksl, _ in slices]
            vw = [v_src[ksl, :].astype(BF16) for _, ksl, _ in slices]
            qj = {(g, j): (q[g] * (hm[j] * 0.125)).astype(BF16) for g, j in chains}
            doj = {(g, j): (do[g] * hm[j]).astype(BF16) for g, j in chains}
            s = {(g, j): _nt(qj[(g, j)], kw[g])
                 + bias_scr[di * 6 + slices[g][2] * 2 + j, :, pl.ds(0, KW)] for g, j in chains}
            dp = {(g, j): _nt(doj[(g, j)], vw[g]) for g, j in chains}
            dsum = {(g, j): jnp.sum(doa[g] * hm[j], axis=1, keepdims=True) for g, j in chains}
            p = {(g, j): jnp.exp(s[(g, j)] - lse_q[g][:, HEAD_DIM * j:HEAD_DIM * j + 1]) for g, j in chains}
            ds = {c: (p[c] * (dp[c] - dsum[c])).astype(BF16) for c in chains}
            pb = {c: p[c].astype(BF16) for c in chains}
            dq = [_nn(ds[(g, 0)], kw[g]) * (hm[0] * 0.125) + _nn(ds[(g, 1)], kw[g]) * (hm[1] * 0.125)
                  for g in range(n)]
            both = lambda t, g: jnp.concatenate([t[(g, 0)], t[(g, 1)]], axis=0)
            dkw = [_tn(both(ds, g), both(qj, g)) for g in range(n)]
            dvw = [_tn(both(pb, g), both(doj, g)) for g in range(n)]
            for g, (qsl, ksl, _) in enumerate(slices):
                dq_scr[qsl, :] = dq_scr[qsl, :] + dq[g]
                dk_scr[ksl, :] = dk_scr[ksl, :] + dkw[g]
                dv_scr[ksl, :] = dv_scr[ksl, :] + dvw[g]

        _for_groups(geom, S, 4, group)

        def merge(i, carry):
            natural, by4 = _by4_rows(S, i)
            for nat, split in zip(*acc):
                nat[natural, :] = nat[natural, :] + split[by4, :]
            return carry
        lax.fori_loop(0, S // COPY_ROWS, merge, 0)
        for out, nat in zip((dq_ref, dk_ref, dv_ref), acc[0]):
            out[...] = nat[...].astype(BF16)

    blk = lambda off: pl.BlockSpec((S, LANES), lambda h, b, off=off: (b, off + h))
    return pl.pallas_call(
        body, grid=(n_pairs, B),
        in_specs=[blk(0), blk(n_pairs), blk(2 * n_pairs), blk(0), blk(0), blk(0)],
        out_specs=[blk(0), blk(0), blk(0)],
        out_shape=[jax.ShapeDtypeStruct((T, ATTN_WIDTH), BF16)] * 3,
        scratch_shapes=[pltpu.VMEM((18, Q_BLOCK, 2 * Q_BLOCK), F32)] + [pltpu.VMEM((S, LANES), F32)] * 12,
        compiler_params=_params(("arbitrary", "arbitrary")), name="attn_bwd")(proj, proj, proj, a, lse, da)


def _mid(x2d, t2d, a, proj, kv, w_s, w_sT, b_tab, g_v, w_out, g_final, B, S):
    T = B * S
    tm = 512
    nt = S // tm
    halves = 2
    hrows = tm // halves

    def body(x_ref, t_ref, a_ref, za_ref, ub_ref, vb_ref, zb_ref, qm_ref, zm_ref, kv_ref,
              ws_ref, wsT_ref, btab_ref, gv_ref, wout_ref, gf_ref,
              dx2_ref, da_ref, drest_ref, loss_ref, dwout_ref, dws_ref, dbs_ref, dgv_ref, dgf_ref, dkv_ref,
              dbtab_scr):
        b = pl.program_id(0)
        t = pl.program_id(1)
        first = jnp.logical_and(b == 0, t == 0)
        last = jnp.logical_and(b == B - 1, t == nt - 1)
        _, hm = _head_masks()
        lane_g = lax.broadcasted_iota(jnp.int32, (1, SGU_WIDTH), 1) // HEAD_DIM
        gm = [(lane_g == g).astype(F32) for g in range(N_SGU_GROUPS)]
        H = range(halves)
        rows = [pl.ds(h * hrows, hrows) for h in H]
        ld = lambda ref: [ref[r, :] for r in rows]
        cat = lambda parts, axis: jnp.concatenate(parts, axis=axis)
        chunks = [slice(ci * SGU_CHUNK, (ci + 1) * SGU_CHUNK) for ci in range(hrows // SGU_CHUNK)]
        pairs = [slice(pr * LANES, (pr + 1) * LANES) for pr in range(2)]
        heads = [(pr, j) for pr in range(2) for j in (0, 1)]

        @pl.when(first)
        def _():
            loss_ref[...] = jnp.zeros_like(loss_ref)
            dwout_ref[...] = jnp.zeros_like(dwout_ref)
            dws_ref[...] = jnp.zeros_like(dws_ref)
            dbs_ref[...] = jnp.zeros_like(dbs_ref)
            dgv_ref[...] = jnp.zeros_like(dgv_ref)
            dgf_ref[...] = jnp.zeros_like(dgf_ref)
            dbtab_scr[...] = jnp.zeros_like(dbtab_scr)

        @pl.when(t == 0)
        def _():
            dkv_ref[...] = jnp.zeros_like(dkv_ref)

        a_val = ld(a_ref)
        sil_a = [_silu_parts(z) for z in ld(za_ref)]
        gated_a = [s[0] * a for s, a in zip(sil_a, a_val)]
        u = [_gelu_parts(z) for z in ld(ub_ref)]
        vv = [_gelu_parts(z) for z in ld(vb_ref)]
        vnorm = [_rms(v[0]) for v in vv]
        gv = gv_ref[...]
        vn = [(n[1] * gv).astype(BF16) for n in vnorm]
        w_cat = cat([ws_ref[g].astype(BF16) for g in range(N_SGU_GROUPS)], 1)
        wT_cat = cat([wsT_ref[g].astype(BF16) for g in range(N_SGU_GROUPS)], 1)
        gmb = [m.astype(BF16) for m in gm]
        by_group = lambda chunk: cat([chunk * gmb[g] for g in range(N_SGU_GROUPS)], 0)
        btab = btab_ref[...]
        mixed = [cat([btab + _nn(w_cat, by_group(vn[h][c, :])) for c in chunks], 0) for h in H]
        sg = [u[h][0] * mixed[h] for h in H]
        sil_b = [_silu_parts(z) for z in ld(zb_ref)]
        gated_b = [sil_b[h][0] * sg[h] for h in H]

        kvv = kv_ref[...].astype(BF16)
        kp = [kvv[:, p] for p in pairs]
        vp = [kvv[:, MEM_WIDTH + pr * LANES:MEM_WIDTH + (pr + 1) * LANES] for pr in range(2)]
        qm = ld(qm_ref)
        qj = {(h, pr, j): (qm[h][:, pairs[pr]] * (hm[j] * 0.125)).astype(BF16) for h in H for pr, j in heads}
        sc = {k: _nt(qj[k], kp[k[1]]) for k in qj}
        ex = {k: jnp.exp(sc[k] - jnp.max(sc[k], axis=1, keepdims=True)) for k in qj}
        prob = {k: ex[k] * (1.0 / jnp.sum(ex[k], axis=1, keepdims=True)) for k in qj}
        probb = {k: prob[k].astype(BF16) for k in qj}
        mo = [cat([sum(_nn(probb[(h, pr, j)], vp[pr]) * hm[j] for j in (0, 1)) for pr in range(2)], 1) for h in H]
        sil_m = [_silu_parts(z) for z in ld(zm_ref)]
        gated_m = [sil_m[h][0] * mo[h] for h in H]

        gated = [cat([gated_a[h], gated_b[h], gated_m[h]], 1).astype(BF16) for h in H]
        wout = wout_ref[...]
        x_in = ld(x_ref)
        x2 = [x_in[h] + _nn(gated[h], wout) for h in H]
        fin = [_rms(z) for z in x2]
        gf = gf_ref[...]
        tgt = ld(t_ref)
        err = [fin[h][1] * gf - tgt[h] for h in H]
        loss_ref[...] += sum(jnp.sum(e * e) for e in err) * (0.5 / D_MODEL)

        dy = [e * (1.0 / D_MODEL) for e in err]
        dgf_ref[...] += sum(jnp.sum(dy[h] * fin[h][1], axis=0, keepdims=True) for h in H)
        gdy = [d * gf for d in dy]
        dx2 = [fin[h][0] * (gdy[h] - fin[h][1] * jnp.mean(gdy[h] * fin[h][1], axis=1, keepdims=True)) for h in H]
        for h in H:
            dx2_ref[rows[h], :] = dx2[h]
        dx2b = [d.astype(BF16) for d in dx2]
        dgated = [_nt(d, wout) for d in dx2b]
        dwout_ref[...] += _tn(cat(gated, 0), cat(dx2b, 0))
        dga = [d[:, 0:ATTN_WIDTH] for d in dgated]
        dgb = [d[:, ATTN_WIDTH:ATTN_WIDTH + SGU_WIDTH] for d in dgated]
        dgm = [d[:, ATTN_WIDTH + SGU_WIDTH:] for d in dgated]

        for h in H:
            da_ref[rows[h], :] = dga[h] * sil_a[h][0]
        dza = [dga[h] * a_val[h] * sil_a[h][1] for h in H]

        dsg = [dgb[h] * sil_b[h][0] for h in H]
        dzb = [dgb[h] * sg[h] * sil_b[h][1] for h in H]
        dub = [dsg[h] * mixed[h] * u[h][1] for h in H]
        dmixed = [dsg[h] * u[h][0] for h in H]
        dmixed_b = [d.astype(BF16) for d in dmixed]
        dvn = [cat([_nn(wT_cat, by_group(dmixed_b[h][c, :])) for c in chunks], 0) for h in H]
        for g in range(N_SGU_GROUPS):
            dws_ref[g] += sum(_nt((dmixed[h][c, :] * gm[g]).astype(BF16), vn[h][c, :]) for h in H for c in chunks)
        dbtab_scr[...] += sum(dmixed[h][c, :] for h in H for c in chunks)
        dgv_ref[...] += sum(jnp.sum(dvn[h] * vnorm[h][1], axis=0, keepdims=True) for h in H)
        tv = [d * gv for d in dvn]
        dvv = [vnorm[h][0] * (tv[h] - vnorm[h][1] * jnp.mean(tv[h] * vnorm[h][1], axis=1, keepdims=True)) for h in H]
        dvb = [dvv[h] * vv[h][1] for h in H]

        dmo = [dgm[h] * sil_m[h][0] for h in H]
        dzm = [dgm[h] * mo[h] * sil_m[h][1] for h in H]
        dmoj = {(h, pr, j): (dmo[h][:, pairs[pr]] * hm[j]).astype(BF16) for h in H for pr, j in heads}
        dp = {k: _nt(dmoj[k], vp[k[1]]) for k in qj}
        ds = {k: (prob[k] * (dp[k] - jnp.sum(dp[k] * prob[k], axis=1, keepdims=True))).astype(BF16) for k in qj}
        dqm = [cat([sum(_nn(ds[(h, pr, j)], kp[pr]) * (hm[j] * 0.125) for j in (0, 1)) for pr in range(2)], 1)
               for h in H]
        every = lambda tbl, pr: cat([tbl[(h, pr, j)] for h in H for j in (0, 1)], 0)
        dk = [_tn(every(ds, pr), every(qj, pr)) for pr in range(2)]
        dv = [_tn(every(probb, pr), every(dmoj, pr)) for pr in range(2)]
        dkv_ref[...] += cat(dk + dv, 1)

        for h in H:
            drest_ref[rows[h], :] = cat([dza[h], dub[h], dvb[h], dzb[h], dqm[h], dzm[h]], 1).astype(BF16)

        @pl.when(last)
        def _():
            lane = lax.broadcasted_iota(jnp.int32, (1, LANES), 1)
            dbt = dbtab_scr[...]
            out = jnp.zeros((SGU_CHUNK, LANES), F32)
            for g in range(N_SGU_GROUPS):
                out = out + jnp.where(lane == g, jnp.sum(dbt * gm[g], axis=1, keepdims=True), 0.0)
            dbs_ref[...] = out

    tile = lambda w, cb: pl.BlockSpec((tm, w), lambda b, t, cb=cb: (b * nt + t, cb))
    const = lambda shape: pl.BlockSpec(shape, lambda b, t, n=len(shape): (0,) * n)
    return pl.pallas_call(
        body, grid=(B, nt),
        in_specs=[tile(D_MODEL, 0), tile(D_MODEL, 0), tile(ATTN_WIDTH, 0),
                  tile(ATTN_WIDTH, 3),
                  tile(SGU_WIDTH, 8), tile(SGU_WIDTH, 9), tile(SGU_WIDTH, 10),
                  tile(MEM_WIDTH, 11), tile(MEM_WIDTH, 12),
                  pl.BlockSpec((N_MEM, 2 * MEM_WIDTH), lambda b, t: (b, 0)),
                  const((N_SGU_GROUPS, SGU_CHUNK, SGU_CHUNK)), const((N_SGU_GROUPS, SGU_CHUNK, SGU_CHUNK)),
                  const((SGU_CHUNK, SGU_WIDTH)), const((1, SGU_WIDTH)),
                  const((D_MODEL, D_MODEL)), const((1, D_MODEL))],
        out_specs=[tile(D_MODEL, 0), tile(ATTN_WIDTH, 0), tile(REST_COLS, 0),
                   const((8, LANES)), const((D_MODEL, D_MODEL)),
                   const((N_SGU_GROUPS, SGU_CHUNK, SGU_CHUNK)), const((SGU_CHUNK, LANES)),
                   const((1, SGU_WIDTH)), const((1, D_MODEL)),
                   pl.BlockSpec((N_MEM, 2 * MEM_WIDTH), lambda b, t: (b, 0))],
        out_shape=[jax.ShapeDtypeStruct((T, D_MODEL), F32), jax.ShapeDtypeStruct((T, ATTN_WIDTH), F32),
                   jax.ShapeDtypeStruct((T, REST_COLS), BF16),
                   jax.ShapeDtypeStruct((8, LANES), F32), jax.ShapeDtypeStruct((D_MODEL, D_MODEL), F32),
                   jax.ShapeDtypeStruct((N_SGU_GROUPS, SGU_CHUNK, SGU_CHUNK), F32),
                   jax.ShapeDtypeStruct((SGU_CHUNK, LANES), F32),
                   jax.ShapeDtypeStruct((1, SGU_WIDTH), F32), jax.ShapeDtypeStruct((1, D_MODEL), F32),
                   jax.ShapeDtypeStruct((B * N_MEM, 2 * MEM_WIDTH), F32)],
        scratch_shapes=[pltpu.VMEM((SGU_CHUNK, SGU_WIDTH), F32)],
        compiler_params=_params(("arbitrary", "arbitrary")), name="mid")(
            x2d, t2d, a, proj, proj, proj, proj, proj, proj, kv, w_s, w_sT, b_tab, g_v, w_out, g_final)


def _inproj_bwd_dx(dq, dk, dv, drest, x2d, dx2, g_norm, w_in_t, after=()):
    T = x2d.shape[0]
    tm = 512
    W = ATTN_WIDTH

    def body(dq_ref, dk_ref, dv_ref, dr_ref, x_ref, dx2_ref, g_ref, w_ref, *rest):
        gx_ref, dg_ref = rest[-2:]

        @pl.when(pl.program_id(0) == 0)
        def _():
            dg_ref[...] = jnp.zeros_like(dg_ref)

        halves = [pl.ds(h * (tm // 2), tm // 2) for h in (0, 1)]
        dh = [(_nn(dq_ref[r, :], w_ref[0:W, :]) + _nn(dk_ref[r, :], w_ref[W:2 * W, :])
               + _nn(dv_ref[r, :], w_ref[2 * W:3 * W, :]) + _nn(dr_ref[r, :], w_ref[QKV_COLS:IN_COLS, :]))
              for r in halves]
        nrm = [_rms(x_ref[r, :]) for r in halves]
        dg_ref[...] += sum(jnp.sum(d * n[1], axis=0, keepdims=True) for d, n in zip(dh, nrm))
        g = g_ref[...]
        for r, d, (rstd, xh) in zip(halves, dh, nrm):
            th = d * g
            gx_ref[r, :] = rstd * (th - xh * jnp.mean(th * xh, axis=1, keepdims=True)) + dx2_ref[r, :]

    tile = lambda w: pl.BlockSpec((tm, w), lambda i: (i, 0))
    return pl.pallas_call(
        body, grid=(T // tm,),
        in_specs=[tile(W), tile(W), tile(W), tile(REST_COLS), tile(D_MODEL), tile(D_MODEL),
                  pl.BlockSpec((1, D_MODEL), lambda i: (0, 0)),
                  pl.BlockSpec((IN_COLS, D_MODEL), lambda i: (0, 0))] + _after(after),
        out_specs=[tile(D_MODEL), pl.BlockSpec((1, D_MODEL), lambda i: (0, 0))],
        out_shape=[jax.ShapeDtypeStruct((T, D_MODEL), F32), jax.ShapeDtypeStruct((1, D_MODEL), F32)],
        compiler_params=_params(("arbitrary",)), name="inproj_bwd_dx")(
            dq, dk, dv, drest, x2d, dx2, g_norm, w_in_t, *after)


def _inproj_bwd_dw(dq, dk, dv, drest, x2d, g_norm, reduce_with=None):
    T = x2d.shape[0]
    tm = 512
    nt = T // tm
    W = ATTN_WIDTH
    fused = reduce_with is not None
    others = list(reduce_with) if fused else []
    ns = 1 + len(others)
    shard = IN_COLS // N_CHIPS
    halves = [shard // 2] + [s.shape[1] // 2 for s in others]
    cols = [D_MODEL] + [s.shape[2] for s in others]
    row_block = 32

    def body(dq_ref, dk_ref, dv_ref, dr_ref, x_ref, g_ref, *rest):
        if fused:
            stacks = rest[:ns - 1]
            sends, owns = rest[ns - 1:2 * ns - 1], rest[2 * ns - 1:3 * ns - 1]
            acc, ras, narrow = rest[3 * ns - 1], rest[3 * ns:4 * ns], rest[4 * ns]
            s_sem, r_sem = rest[4 * ns + 1], rest[4 * ns + 2]
            x, y, c, chip, peers, peer_chip = _place()
            sib = (x, y, 1 - c)

            def part(w, k, cc, r0=0, rows=None):
                n = halves[w]
                rows = n if rows is None else rows
                if w == 0:
                    return acc.at[pl.ds(pl.multiple_of(k * shard + cc * n + r0, 8), rows), :]
                return stacks[w - 1].at[k, pl.ds(pl.multiple_of(cc * n + r0, 8), rows), :]

            def swap_other(w):
                theirs = stacks[w - 1].at[:, pl.ds(pl.multiple_of((1 - c) * halves[w], 8), halves[w]), :]
                return _remote(theirs, ras[w], s_sem.at[N_CHIPS - 1 + w], r_sem.at[N_CHIPS - 1 + w], sib)

            def swap_win(k):
                return _remote(narrow.at[k], ras[0].at[k], s_sem.at[k], r_sem.at[k], sib)
        else:
            acc = rest[0]

        @pl.when(pl.program_id(0) == 0)
        def _():
            acc[...] = jnp.zeros_like(acc)
            for w in range(1, ns):
                swap_other(w).start()

        _, xh = _rms(x_ref[...])
        h = (xh * g_ref[...]).astype(BF16)
        acc[0:W, :] += _tn(dq_ref[...], h)
        acc[W:2 * W, :] += _tn(dk_ref[...], h)
        acc[2 * W:3 * W, :] += _tn(dv_ref[...], h)
        acc[QKV_COLS:IN_COLS, :] += _tn(dr_ref[...], h)

        if fused:
            @pl.when(pl.program_id(0) == nt - 1)
            def _():
                for k in range(N_CHIPS):
                    def to_bf16(i, carry, k=k):
                        r0 = pl.multiple_of(i * row_block, row_block)
                        narrow[k, pl.ds(r0, row_block), :] = part(0, k, 1 - c, r0, row_block)[...].astype(BF16)
                        return carry
                    lax.fori_loop(0, halves[0] // row_block, to_bf16, 0)
                    swap_win(k).start()
                for w in list(range(1, ns)) + [0]:
                    if w == 0:
                        for k in range(N_CHIPS):
                            swap_win(k).wait_recv()
                    else:
                        swap_other(w).wait_recv()

                    def sums(i, carry, w=w):
                        r0 = pl.multiple_of(i * row_block, row_block)
                        blk = pl.ds(r0, row_block)
                        for m in range(3):
                            k = peer_chip[m]
                            sends[w][m, blk, :] = (part(w, k, c, r0, row_block)[...]
                                                   + ras[w][k, blk, :].astype(F32)).astype(BF16)
                        owns[w][blk, :] = part(w, chip, c, r0, row_block)[...] + ras[w][chip, blk, :].astype(F32)
                        return carry
                    lax.fori_loop(0, halves[w] // row_block, sums, 0)
                for k in range(N_CHIPS):
                    swap_win(k).wait_send()
                for w in range(1, ns):
                    swap_other(w).wait_send()

    tile = lambda w: pl.BlockSpec((tm, w), lambda i: (i, 0))
    vmem = pl.BlockSpec(memory_space=pltpu.VMEM)
    in_specs = [tile(W), tile(W), tile(W), tile(REST_COLS), tile(D_MODEL), pl.BlockSpec((1, D_MODEL), lambda i: (0, 0))]
    if not fused:
        return pl.pallas_call(
            body, grid=(nt,), in_specs=in_specs,
            out_specs=pl.BlockSpec((IN_COLS, D_MODEL), lambda i: (0, 0)),
            out_shape=jax.ShapeDtypeStruct((IN_COLS, D_MODEL), F32),
            compiler_params=_params(("arbitrary",)), name="inproj_bwd_dw")(dq, dk, dv, drest, x2d, g_norm)
    outs = pl.pallas_call(
        body, grid=(nt,), in_specs=in_specs + [vmem] * (ns - 1), out_specs=[vmem] * (2 * ns),
        out_shape=[jax.ShapeDtypeStruct((3, n, cl), BF16) for n, cl in zip(halves, cols)]
        + [jax.ShapeDtypeStruct((n, cl), F32) for n, cl in zip(halves, cols)],
        scratch_shapes=[pltpu.VMEM((IN_COLS, D_MODEL), F32)]
        + [pltpu.VMEM((N_CHIPS, n, cl), BF16 if w == 0 else F32) for w, (n, cl) in enumerate(zip(halves, cols))]
        + [pltpu.VMEM((N_CHIPS, halves[0], D_MODEL), BF16)]
        + [pltpu.SemaphoreType.DMA((N_CHIPS - 1 + ns,)), pltpu.SemaphoreType.DMA((N_CHIPS - 1 + ns,))],
        compiler_params=_params(("arbitrary",)), name="inproj_bwd_dw_reduce")(
            dq, dk, dv, drest, x2d, g_norm, *others)
    return outs[:ns], outs[ns:]


def _adamw_update(w, g, m, v):
    nm = ADAM_B1 * m + (1.0 - ADAM_B1) * g
    nv = ADAM_B2 * v + (1.0 - ADAM_B2) * (g * g)
    m_hat = nm / (1.0 - ADAM_B1 ** ADAM_STEP)
    v_hat = nv / (1.0 - ADAM_B2 ** ADAM_STEP)
    return -ADAM_LR * (m_hat / (jnp.sqrt(v_hat) + ADAM_EPS) + ADAM_WD * w), nm, nv


def _adamw(w, g, m, v, name):
    R, C = w.shape
    br = max(r for r in range(8, 257, 8) if R % r == 0)

    def body(w_ref, g_ref, m_ref, v_ref, d_ref, nm_ref, nv_ref):
        d_ref[...], nm_ref[...], nv_ref[...] = _adamw_update(w_ref[...], g_ref[...], m_ref[...], v_ref[...])

    spec = pl.BlockSpec((br, C), lambda i: (i, 0))
    return pl.pallas_call(
        body, grid=(R // br,), in_specs=[spec] * 4, out_specs=[spec] * 3,
        out_shape=[jax.ShapeDtypeStruct((R, C), F32)] * 3,
        compiler_params=_params(("arbitrary",)), name=name)(w, g, m, v)


def _adamw_small(g_packed, ws, ms, vs):
    n = len(ws)

    def body(*refs):
        g_ref = refs[0]
        w_refs, m_refs, v_refs = refs[1:1 + n], refs[1 + n:1 + 2 * n], refs[1 + 2 * n:1 + 3 * n]
        outs = refs[1 + 3 * n:]
        off = 0
        for i, (_, used, padded) in enumerate(_SMALL_PARTS[:n]):
            g = g_ref[off:off + used, :]
            delta, nm, nv = _adamw_update(w_refs[i][...], g, m_refs[i][...], v_refs[i][...])
            outs[4 * i][...], outs[4 * i + 1][...], outs[4 * i + 2][...], outs[4 * i + 3][...] = g, delta, nm, nv
            off += padded

    outs = pl.pallas_call(
        body, out_shape=[jax.ShapeDtypeStruct(w.shape, F32) for w in ws for _ in range(4)],
        compiler_params=_params(), name="adamw_small")(g_packed, *ws, *ms, *vs)
    return [outs[4 * i:4 * i + 4] for i in range(n)]


def _place():
    x, y, c = lax.axis_index("x"), lax.axis_index("y"), lax.axis_index("c")
    chip = 2 * x + y
    peers = [(x, 1 - y), (1 - x, y), (1 - x, 1 - y)]
    peer_chip = [2 * px + py for px, py in peers]
    return x, y, c, chip, peers, peer_chip


def _remote(src, dst, send_sem, recv_sem, dev):
    return pltpu.make_async_remote_copy(src_ref=src, dst_ref=dst, send_sem=send_sem, recv_sem=recv_sem,
                                        device_id=dev, device_id_type=MESH)


def _ag_weights(weights, late=()):
    nw, nl = len(weights), len(late)

    def body(*refs):
        srcs, late_srcs = refs[:nw], refs[nw:nw + nl]
        outs, late_bf, late_land = (refs[nw + nl:2 * nw + nl], refs[2 * nw + nl:2 * nw + 2 * nl],
                                    refs[2 * nw + 2 * nl:2 * nw + 3 * nl])
        s_ici, r_ici, s_d2d, r_d2d = refs[2 * nw + 3 * nl:]
        x, y, c = lax.axis_index("x"), lax.axis_index("y"), lax.axis_index("c")
        chip = 2 * x + y
        sib = (x, y, 1 - c)
        first = ((x + 1 - c) % 2, (y + c) % 2)
        second = ((x + c) % 2, (y + 1 - c) % 2)
        first_chip, second_chip = 2 * first[0] + first[1], 2 * second[0] + second[1]
        diag_chip = 3 - chip
        for src, out in zip(srcs, outs):
            out[chip] = src[...].astype(BF16)

        def half(out, k, cc):
            rows = out.shape[1] // 2
            return out.at[k, pl.ds(pl.multiple_of(cc * rows, 16), rows), :]

        def ici(w, slot, out, k, dev):
            blk = half(out, k, c)
            return _remote(blk, blk, s_ici.at[nw * slot + w], r_ici.at[nw * slot + w], (dev[0], dev[1], c))

        def d2d(w, slot, out, k, cc):
            blk = half(out, k, cc)
            return _remote(blk, blk, s_d2d.at[nw * slot + w], r_d2d.at[nw * slot + w], sib)

        sent = []
        for slot, dev in enumerate((first, second)):
            for w, out in enumerate(outs):
                sent.append(ici(w, slot, out, chip, dev))
                sent[-1].start()
        for src, bf, land in zip(late_srcs, late_bf, late_land):
            bf[...] = src[...].astype(BF16)
            land[...] = jnp.zeros_like(land)
            land[chip] = bf[...]
        for slot, k, dev in ((0, first_chip, first), (1, second_chip, second), (2, diag_chip, second)):
            for w, out in enumerate(outs):
                ici(w, slot, out, k, dev).wait_recv()
                if slot == 0:
                    sent.append(ici(w, 2, out, k, second))
                    sent[-1].start()
                sent.append(d2d(w, slot, out, k, c))
                sent[-1].start()
        for slot, k in ((0, second_chip), (1, first_chip), (2, diag_chip)):
            for w, out in enumerate(outs):
                d2d(w, slot, out, k, 1 - c).wait_recv()
        for cp in sent:
            cp.wait_send()

    vmem = pl.BlockSpec(memory_space=pltpu.VMEM)
    outs = pl.pallas_call(
        body,
        out_shape=[jax.ShapeDtypeStruct((N_CHIPS,) + w.shape, BF16) for w in weights]
        + [jax.ShapeDtypeStruct(w.shape, BF16) for w in late]
        + [jax.ShapeDtypeStruct((N_CHIPS,) + w.shape, BF16) for w in late],
        in_specs=[vmem] * (nw + nl), out_specs=[vmem] * (nw + 2 * nl),
        scratch_shapes=[pltpu.SemaphoreType.DMA((3 * nw,))] * 4,
        compiler_params=pltpu.CompilerParams(vmem_limit_bytes=VMEM_LIMIT), name="ag_weights")(*weights, *late)
    return outs[:nw], outs[nw:nw + nl], outs[nw + nl:]


_HBM = pl.BlockSpec(memory_space=pltpu.HBM)
_SEM = pl.BlockSpec(memory_space=pltpu.SEMAPHORE)
_ANY = pl.BlockSpec(memory_space=pl.ANY)
_DATAFLOW = pltpu.SideEffectType.DATAFLOW_SIDE_EFFECTING


def _in_hbm(a):
    return pltpu.with_memory_space_constraint(a, pltpu.HBM)


def _exchange_copies(gather, srcs, lands, send_sems, recv_sems):
    nw = len(srcs)
    x, y, c, chip, peers, peer_chip = _place()
    pairs = []
    for m, (px, py) in enumerate(peers):
        for w in range(nw):
            sems = (send_sems.at[nw * m + w], recv_sems.at[nw * m + w], (px, py, c))
            if gather:
                pairs.append((_remote(srcs[w], lands[w].at[chip], *sems),
                              _remote(srcs[w], lands[w].at[peer_chip[m]], *sems)))
            else:
                pairs.append((_remote(srcs[w].at[m], lands[w].at[m], *sems),) * 2)
    return pairs


def _exchange_start(gather, srcs, after, name, lands=None):
    nw = len(srcs)
    n_copies = 3 * nw

    def body(*refs):
        send_sems, recv_sems = refs[2 * nw + 1], refs[2 * nw + 2]
        for start, _ in _exchange_copies(gather, refs[:nw], refs[nw:2 * nw], send_sems, recv_sems):
            start.start()
        refs[-1][...] = jnp.zeros_like(refs[-1])

    if lands is None:
        lands = [lax.empty(((N_CHIPS,) + s.shape) if gather else s.shape, s.dtype) for s in srcs]
    lands = [_in_hbm(l) for l in lands]
    return pl.pallas_call(
        body, name=name,
        out_shape=(pltpu.SemaphoreType.DMA((n_copies,)), pltpu.SemaphoreType.DMA((n_copies,)))
        + tuple(pltpu.HBM(s.shape, s.dtype) for s in srcs)
        + tuple(pltpu.HBM(l.shape, l.dtype) for l in lands)
        + (jax.ShapeDtypeStruct((8, LANES), F32),),
        in_specs=[_HBM] * (2 * nw) + [_ANY],
        out_specs=(_SEM, _SEM) + (_HBM,) * (2 * nw) + (pl.BlockSpec(memory_space=pltpu.VMEM),),
        input_output_aliases={i: 2 + i for i in range(2 * nw)},
        compiler_params=pltpu.CompilerParams(has_side_effects=_DATAFLOW),
    )(*[_in_hbm(s) for s in srcs], *lands, after)


def _exchange_wait(gather, started, after, name):
    nw = (len(started) - 3) // 2
    send_sems, recv_sems = started[0], started[1]
    thru = started[2:2 + 2 * nw]

    def body(*refs):
        for _, arrival in _exchange_copies(gather, refs[:nw], refs[nw:2 * nw], refs[2 * nw], refs[2 * nw + 1]):
            arrival.wait_send()
            arrival.wait_recv()

    outs = pl.pallas_call(
        body, name=name,
        out_shape=tuple(pltpu.HBM(t.shape, t.dtype) for t in thru),
        in_specs=[_HBM] * (2 * nw) + [_SEM, _SEM, _ANY], out_specs=(_HBM,) * (2 * nw),
        input_output_aliases={i: i for i in range(2 * nw)},
        compiler_params=pltpu.CompilerParams(has_side_effects=_DATAFLOW),
    )(*thru, send_sems, recv_sems, after)
    return outs[nw:]


def _reduce_last(owns, landed, g_small):
    ns = len(owns)
    row_block = 32
    hs = SMALL_ROWS // 2

    def body(*refs):
        own_refs, land_refs, gsm_ref = refs[:ns], refs[ns:2 * ns], refs[2 * ns]
        out_refs, osm_ref = refs[2 * ns + 1:3 * ns + 1], refs[3 * ns + 1]
        ra_sm, p_sm, s_sem, r_sem, sm_s, sm_r = refs[3 * ns + 2:]
        x, y, c, chip, peers, peer_chip = _place()
        sib = (x, y, 1 - c)
        half = lambda cc: pl.ds(pl.multiple_of(cc * hs, 8), hs)
        sm_a = _remote(gsm_ref.at[half(1 - c), :], ra_sm, sm_s.at[0], sm_r.at[0], sib)
        sm_a.start()
        swaps = [sm_a]
        for w in range(ns):
            n = own_refs[w].shape[0]

            def total(i, carry, w=w, n=n):
                r0 = pl.multiple_of(i * row_block, row_block)
                blk = pl.ds(r0, row_block)
                acc = own_refs[w][blk, :]
                for m in range(3):
                    acc = acc + land_refs[w][m, blk, :].astype(F32)
                out_refs[w][pl.ds(pl.multiple_of(c * n + r0, 8), row_block), :] = acc
                return carry
            lax.fori_loop(0, n // row_block, total, 0)
            mine = out_refs[w].at[pl.ds(pl.multiple_of(c * n, 8), n), :]
            swaps.append(_remote(mine, mine, s_sem.at[w], r_sem.at[w], sib))
            swaps[-1].start()
        sm_a.wait_recv()
        p_sm[chip] = gsm_ref[half(c), :] + ra_sm[...]
        for m, (px, py) in enumerate(peers):
            swaps.append(_remote(p_sm.at[chip], p_sm.at[chip], sm_s.at[1 + m], sm_r.at[1 + m], (px, py, c)))
            swaps[-1].start()
        for m, (px, py) in enumerate(peers):
            _remote(p_sm.at[chip], p_sm.at[peer_chip[m]], sm_s.at[1 + m], sm_r.at[1 + m], (px, py, c)).wait_recv()
        osm_ref[half(c), :] = (p_sm[0] + p_sm[1]) + (p_sm[2] + p_sm[3])
        swaps.append(_remote(osm_ref.at[half(c), :], osm_ref.at[half(c), :], sm_s.at[4], sm_r.at[4], sib))
        swaps[-1].start()
        for w in range(ns):
            n = own_refs[w].shape[0]
            theirs = out_refs[w].at[pl.ds(pl.multiple_of((1 - c) * n, 8), n), :]
            _remote(theirs, theirs, s_sem.at[w], r_sem.at[w], sib).wait_recv()
        _remote(osm_ref.at[half(1 - c), :], osm_ref.at[half(1 - c), :], sm_s.at[4], sm_r.at[4], sib).wait_recv()
        for cp in swaps:
            cp.wait_send()

    vmem = pl.BlockSpec(memory_space=pltpu.VMEM)
    return pl.pallas_call(
        body, out_shape=[jax.ShapeDtypeStruct((2 * o.shape[0], o.shape[1]), F32) for o in owns]
        + [jax.ShapeDtypeStruct((SMALL_ROWS, LANES), F32)],
        in_specs=[vmem] * (2 * ns + 1), out_specs=[vmem] * (ns + 1),
        scratch_shapes=[pltpu.VMEM((hs, LANES), F32), pltpu.VMEM((N_CHIPS, hs, LANES), F32),
                        pltpu.SemaphoreType.DMA((ns,)), pltpu.SemaphoreType.DMA((ns,)),
                        pltpu.SemaphoreType.DMA((5,)), pltpu.SemaphoreType.DMA((5,))],
        compiler_params=pltpu.CompilerParams(vmem_limit_bytes=VMEM_LIMIT),
        name="reduce_last")(*owns, *landed, g_small)


_SMALL_PARTS = (("g_norm", 8, 8), ("w_s", 512, 512), ("b_s", 4, 8), ("g_v", 2, 8), ("g_mem", 8, 8),
                ("g_final", 8, 8), ("loss", 1, 8))
_LOSS_ROW = SMALL_ROWS - 8
assert sum(p for _, _, p in _SMALL_PARTS) == SMALL_ROWS


def _pack_small(parts, loss=None):
    loss_row = jnp.zeros((1, LANES), F32) if loss is None else jnp.broadcast_to(loss.reshape(1, 1), (1, LANES))
    rows = []
    for (name, used, padded), p in zip(_SMALL_PARTS, list(parts) + [loss_row]):
        p = p.reshape(used, LANES)
        if padded > used:
            p = jnp.pad(p, ((0, padded - used), (0, 0)))
        rows.append(p)
    return jnp.concatenate(rows, axis=0)


def _local_step(x, mem, target, g_norm, w_in, w_s, b_s, g_v, g_mem, late_weights, g_final,
                fwd_token=None, on_dw=None):
    B, S, _ = x.shape
    x2d = x.reshape(B * S, D_MODEL)
    t2d = target.reshape(B * S, D_MODEL)
    mem2d = mem.reshape(B * N_MEM, D_MODEL)

    proj = _inproj_fwd(x2d, g_norm, w_in, after=() if fwd_token is None else (fwd_token,))
    w_kv, w_out = late_weights(proj)
    kv = _kv_fwd(mem2d, g_mem, w_kv)
    a, lse = _attn_fwd(proj, B, S)
    w_sT = jnp.swapaxes(w_s, 1, 2)
    b_tab = jnp.repeat(b_s.T, HEAD_DIM, axis=1)
    (dx2, da, drest, loss, d_wout, d_ws, d_bs, d_gv, d_gf, dkv) = _mid(
        x2d, t2d, a, proj, kv, w_s, w_sT, b_tab, g_v, w_out, g_final, B, S)
    d_wkv, d_gmem = _kv_bwd(mem2d, g_mem, w_kv, dkv)
    dq, dk, dv = _attn_bwd(proj, a, lse, da, B, S)
    if on_dw is None:
        d_win = _inproj_bwd_dw(dq, dk, dv, drest, x2d, g_norm)
        after = ()
    else:
        d_win = None
        by_chip = lambda g: g.reshape((N_CHIPS, g.shape[0] // N_CHIPS, g.shape[1]))
        after = (on_dw(*_inproj_bwd_dw(dq, dk, dv, drest, x2d, g_norm, reduce_with=[by_chip(d_wkv), by_chip(d_wout)])),)
    grad_x, d_gnorm = _inproj_bwd_dx(dq, dk, dv, drest, x2d, dx2, g_norm, w_in, after=after)
    d_bs = d_bs[:, :N_SGU_GROUPS].T
    return (loss[0, 0], grad_x.reshape(B, S, D_MODEL),
            dict(g_norm=d_gnorm, w_in=d_win, w_s=d_ws, b_s=d_bs, g_v=d_gv, g_mem=d_gmem, w_kv=d_wkv,
                 w_out=d_wout, g_final=d_gf))


def kernel(x, mem, g_norm, w_in, w_sgu_spatial, b_sgu_spatial, g_sgu_v, g_mem, w_mem_kv, w_out, g_final, loss_target, m_g_norm, m_w_in, m_w_sgu_spatial, m_b_sgu_spatial, m_g_sgu_v, m_g_mem, m_w_mem_kv, m_w_out, m_g_final, v_g_norm, v_w_in, v_w_sgu_spatial, v_b_sgu_spatial, v_g_sgu_v, v_g_mem, v_w_mem_kv, v_w_out, v_g_final):
    t = lambda w: jnp.swapaxes(w[0], 0, 1)
    (win_all,), late_shards, late_lands = _ag_weights([t(w_in)], [w_mem_kv[0], w_out[0]])
    w_in_full = win_all.reshape(-1, win_all.shape[-1])
    late = _exchange_start(True, list(late_shards), win_all, "gather_late_start", lands=late_lands)

    def late_weights(proj):
        return [z.reshape(-1, z.shape[-1]) for z in _exchange_wait(True, late, proj, "gather_late_wait")]

    scatter = {}

    def on_dw(sends, owns):
        scatter["own"] = owns
        scatter["started"] = _exchange_start(False, list(sends), owns[0], "scatter_start")
        return scatter["started"][-1]

    loss, grad_x, g = _local_step(
        x, mem, loss_target, g_norm, w_in_full, w_sgu_spatial[0], b_sgu_spatial[0], g_sgu_v, g_mem,
        late_weights, g_final.reshape(1, D_MODEL), fwd_token=late[-1], on_dw=on_dw)

    small_names = ("g_norm", "w_s", "b_s", "g_v", "g_mem", "g_final")
    g_small = _pack_small([g[n] for n in small_names], loss)
    landed = _exchange_wait(False, scatter["started"], g_small, "scatter_wait")
    gr_in, gr_kv, gr_out, gr_small = _reduce_last(scatter["own"], landed, g_small)
    loss = gr_small[_LOSS_ROW, 0]

    small_w = (g_norm, w_sgu_spatial, b_sgu_spatial, g_sgu_v, g_mem, g_final)
    small_m = (m_g_norm, m_w_sgu_spatial, m_b_sgu_spatial, m_g_sgu_v, m_g_mem, m_g_final)
    small_v = (v_g_norm, v_w_sgu_spatial, v_b_sgu_spatial, v_g_sgu_v, v_g_mem, v_g_final)
    rows = lambda ws: [w.reshape(-1, LANES) for w in ws]
    small = [[z.reshape(w.shape) for z in four]
             for w, four in zip(small_w, _adamw_small(gr_small, rows(small_w), rows(small_m), rows(small_v)))]
    d_in, nm_in, nv_in = _adamw(t(w_in), gr_in, t(m_w_in), t(v_w_in), "adamw_w_in")
    gr_in, d_in, nm_in, nv_in = [jnp.swapaxes(z, 0, 1) for z in (gr_in, d_in, nm_in, nv_in)]
    d_kv, nm_kv, nv_kv = _adamw(w_mem_kv[0], gr_kv, m_w_mem_kv[0], v_w_mem_kv[0], "adamw_w_kv")
    d_out, nm_out, nv_out = _adamw(w_out[0], gr_out, m_w_out[0], v_w_out[0], "adamw_w_out")

    def leaves(kind, big_in, big_kv, big_out):
        s_norm, s_ws, s_bs, s_gv, s_gmem, s_gf = [four[kind] for four in small]
        return [s_norm, big_in[None], s_ws, s_bs, s_gv, s_gmem, big_kv[None], big_out[None], s_gf]

    return (loss, grad_x, *leaves(0, gr_in, gr_kv, gr_out), *leaves(1, d_in, d_kv, d_out),
            *leaves(2, nm_in, nm_kv, nm_out), *leaves(3, nv_in, nv_kv, nv_out))
```

```python
import functools

import jax
import jax.numpy as jnp
from jax import lax
from jax.experimental import pallas as pl
from jax.experimental.pallas import tpu as pltpu

F32 = jnp.float32
BF16 = jnp.bfloat16
MESH = pl.DeviceIdType.MESH

D_MODEL = 1024
ATTN_WIDTH = 512
SGU_WIDTH = 256
MEM_WIDTH = 256
N_MEM = 256
IN_COLS = 3328
QKV_COLS = 3 * ATTN_WIDTH
REST_COLS = IN_COLS - QKV_COLS
SGU_CHUNK = 128
N_SGU_GROUPS = 4
EPS = 1e-6
NEG_INF = -1e30
DILATIONS = (1, 4, 16)
RADIUS = 64
Q_BLOCK = 128
LANES = 128
HEAD_DIM = 64

ADAM_LR = 0.001
ADAM_B1 = 0.9
ADAM_B2 = 0.999
ADAM_EPS = 1e-08
ADAM_WD = 0.01
ADAM_STEP = 10

N_CHIPS = 4
VMEM_LIMIT = 56 * 1024 * 1024
SMALL_ROWS = 560


def _params(sem=None, vmem=VMEM_LIMIT):
    return pltpu.CompilerParams(dimension_semantics=sem, vmem_limit_bytes=vmem)


def _nn(a, b):
    return jnp.dot(a, b, preferred_element_type=F32)


def _nt(a, b):
    return lax.dot_general(a, b, (((1,), (1,)), ((), ())), preferred_element_type=F32)


def _tn(a, b):
    return lax.dot_general(a, b, (((0,), (0,)), ((), ())), preferred_element_type=F32)


def _rms(x):
    r = lax.rsqrt(jnp.mean(x * x, axis=-1, keepdims=True) + EPS)
    return r, x * r


def _head_masks():
    lane = lax.broadcasted_iota(jnp.int32, (1, LANES), 1)
    lo = lane < HEAD_DIM
    return lo, (lo.astype(F32), (~lo).astype(F32))


def _silu_parts(z):
    s = jax.nn.sigmoid(z)
    return z * s, s * (1.0 + z * (1.0 - s))


def _gelu_parts(x):
    c = 0.7978845608028654
    x2 = x * x
    s = jax.nn.sigmoid((2.0 * c) * (x + 0.044715 * (x * x2)))
    return x * s, s * (1.0 + x * (1.0 - s) * ((2.0 * c) * (1.0 + 3.0 * 0.044715 * x2)))


def _after(tokens):
    return [pl.BlockSpec(memory_space=pl.ANY)] * len(tokens)


def _inproj_fwd(x2d, g_norm, w_in_t, after=()):
    T = x2d.shape[0]
    tm = 512

    def body(x_ref, g_ref, w_ref, *rest):
        o_ref = rest[-1]
        _, xh = _rms(x_ref[...])
        h = (xh * g_ref[...]).astype(BF16)
        o_ref[...] = _nt(h, w_ref[...])

    return pl.pallas_call(
        body, grid=(T // tm,),
        in_specs=[pl.BlockSpec((tm, D_MODEL), lambda i: (i, 0)),
                  pl.BlockSpec((1, D_MODEL), lambda i: (0, 0)),
                  pl.BlockSpec((IN_COLS, D_MODEL), lambda i: (0, 0))] + _after(after),
        out_specs=pl.BlockSpec((tm, IN_COLS), lambda i: (i, 0)),
        out_shape=jax.ShapeDtypeStruct((T, IN_COLS), F32),
        compiler_params=_params(("arbitrary",)), name="inproj_fwd")(x2d, g_norm, w_in_t, *after)


def _kv_fwd(mem2d, g_mem, w_kv):
    Tm = mem2d.shape[0]

    def body(m_ref, g_ref, w_ref, o_ref):
        _, mh = _rms(m_ref[...])
        o_ref[...] = _nn((mh * g_ref[...]).astype(BF16), w_ref[...])

    return pl.pallas_call(
        body, out_shape=jax.ShapeDtypeStruct((Tm, 2 * MEM_WIDTH), F32),
        compiler_params=_params(), name="kv_fwd")(mem2d, g_mem, w_kv)


def _kv_bwd(mem2d, g_mem, w_kv, dkv):
    Tm = mem2d.shape[0]

    def body(m_ref, g_ref, w_ref, dkv_ref, dw_ref, dg_ref):
        _, mh = _rms(m_ref[...])
        memn = (mh * g_ref[...]).astype(BF16)
        dkvb = dkv_ref[...].astype(BF16)
        dw_ref[...] = _tn(memn, dkvb)
        dmemn = _nt(dkvb, w_ref[...])
        dg_ref[...] = jnp.sum(dmemn * mh, axis=0, keepdims=True)

    return pl.pallas_call(
        body, out_shape=(jax.ShapeDtypeStruct((D_MODEL, 2 * MEM_WIDTH), F32),
                         jax.ShapeDtypeStruct((1, D_MODEL), F32)),
        compiler_params=_params(), name="kv_bwd")(mem2d, g_mem, w_kv, dkv)


def _attn_geometry(S):
    geom = []
    for d in DILATIONS:
        L = S // d
        assert L % Q_BLOCK == 0
        geom.append((d, L, min(2 * Q_BLOCK, L), L // Q_BLOCK))
    return geom


def _init_bias(bias_scr, geom, hp):
    row = lax.broadcasted_iota(jnp.int32, (Q_BLOCK, 2 * Q_BLOCK), 0)
    col = lax.broadcasted_iota(jnp.int32, (Q_BLOCK, 2 * Q_BLOCK), 1)
    for j in (0, 1):
        bits = (126 - (2 * hp + j)) * (1 << 23)
        slope = lax.bitcast_convert_type(jnp.full((1, 1), bits, jnp.int32), F32)
        for di, (d, _, _, _) in enumerate(geom):
            for cls, off in enumerate((0, -RADIUS, -2 * RADIUS)):
                dist = jnp.abs(col - row + off)
                bias_scr[di * 6 + cls * 2 + j] = jnp.where(
                    dist <= RADIUS, -(slope * float(d)) * dist.astype(F32), NEG_INF)


SPLIT = 4
COPY_ROWS = 256


def _by4_rows(S, step):
    per_class = S // SPLIT // COPY_ROWS
    r, j = step // per_class, step % per_class
    return (pl.ds(r + SPLIT * j * COPY_ROWS, COPY_ROWS, stride=SPLIT),
            pl.ds(pl.multiple_of(r * (S // SPLIT) + j * COPY_ROWS, COPY_ROWS), COPY_ROWS))


def _to_by4(src, dst, S):
    def step(i, carry):
        natural, by4 = _by4_rows(S, i)
        dst[by4, :] = src[natural, :]
        return carry
    lax.fori_loop(0, S // COPY_ROWS, step, 0)


def _block_slices(d, L, KW, nqb, r, qb, S):
    qs = qb * Q_BLOCK
    ks = jnp.clip(qs - RADIUS, 0, L - KW)
    cls = jnp.where(qb == 0, 0, jnp.where(qb == nqb - 1, 2, 1))
    if d == 1:
        qsl = pl.ds(pl.multiple_of(qs, Q_BLOCK), Q_BLOCK)
        ksl = pl.ds(pl.multiple_of(ks, RADIUS), KW)
    elif d == SPLIT:
        qsl = pl.ds(pl.multiple_of(r * L + qs, Q_BLOCK), Q_BLOCK)
        ksl = pl.ds(pl.multiple_of(r * L + ks, RADIUS), KW)
    else:
        sub = d // SPLIT
        base = (r % SPLIT) * (S // SPLIT) + r // SPLIT
        qsl = pl.ds(base + qs * sub, Q_BLOCK, stride=sub)
        ksl = pl.ds(base + ks * sub, KW, stride=sub)
    return qsl, ksl, cls


def _for_groups(geom, S, group, fn):
    for di, (d, L, KW, nqb) in enumerate(geom):
        assert (d * nqb) % group == 0

        def step(it, carry, di=di, d=d, L=L, KW=KW, nqb=nqb):
            slices = []
            for g in range(group):
                i = it * group + g
                slices.append(_block_slices(d, L, KW, nqb, i // nqb, i % nqb, S))
            fn(di, KW, slices)
            return carry
        lax.fori_loop(0, d * nqb // group, step, 0)


def _attn_fwd(proj, B, S):
    T = B * S
    geom = _attn_geometry(S)
    n_pairs = ATTN_WIDTH // LANES

    def body(q_ref, k_ref, v_ref, a_ref, lse_ref, bias_scr, q4, k4, v4, *per_dilation):
        o_scr, m_scr, l_scr = per_dilation[0:3], per_dilation[3:6], per_dilation[6:9]
        lo, hm = _head_masks()
        pair = pl.program_id(0)

        @pl.when(pl.program_id(1) == 0)
        def _():
            _init_bias(bias_scr, geom, pair)
        for src, dst in ((q_ref, q4), (k_ref, k4), (v_ref, v4)):
            _to_by4(src, dst, S)

        def group(di, KW, slices):
            chains = [(g, j) for g in range(len(slices)) for j in (0, 1)]
            q_src, k_src, v_src = (q_ref, k_ref, v_ref) if di == 0 else (q4, k4, v4)
            q = [q_src[qsl, :] for qsl, _, _ in slices]
            kw = [k_src[ksl, :].astype(BF16) for _, ksl, _ in slices]
            vw = [v_src[ksl, :].astype(BF16) for _, ksl, _ in slices]
            s = {(g, j): _nt((q[g] * (hm[j] * 0.125)).astype(BF16), kw[g])
                 + bias_scr[di * 6 + slices[g][2] * 2 + j, :, pl.ds(0, KW)] for g, j in chains}
            m = {c: jnp.max(s[c], axis=1, keepdims=True) for c in chains}
            p = {c: jnp.exp(s[c] - m[c]) for c in chains}
            l = {c: jnp.sum(p[c], axis=1, keepdims=True) for c in chains}
            o = {(g, j): _nn(p[(g, j)].astype(BF16), vw[g]) for g, j in chains}
            for g, (qsl, _, _) in enumerate(slices):
                o_scr[di][qsl, :] = jnp.where(lo, o[(g, 0)], o[(g, 1)])
                m_scr[di][qsl, :] = jnp.where(lo, m[(g, 0)], m[(g, 1)])
                l_scr[di][qsl, :] = jnp.where(lo, l[(g, 0)], l[(g, 1)])

        _for_groups(geom, S, 16, group)

        def combine(i, carry):
            natural, by4 = _by4_rows(S, i)
            rows = [natural, by4, by4]
            ms = [m_scr[di][rows[di], :] for di in range(3)]
            mx = jnp.maximum(jnp.maximum(ms[0], ms[1]), ms[2])
            num = 0.0
            den = 0.0
            for di in range(3):
                w = jnp.exp(ms[di] - mx)
                num = num + w * o_scr[di][rows[di], :]
                den = den + w * l_scr[di][rows[di], :]
            a_ref[natural, :] = num / den
            lse_ref[natural, :] = mx + jnp.log(den)
            return carry

        lax.fori_loop(0, S // COPY_ROWS, combine, 0)

    blk = lambda off: pl.BlockSpec((S, LANES), lambda h, b, off=off: (b, off + h))
    out_blk = pl.BlockSpec((S, LANES), lambda h, b: (b, h))
    return pl.pallas_call(
        body, grid=(n_pairs, B),
        in_specs=[blk(0), blk(n_pairs), blk(2 * n_pairs)],
        out_specs=[out_blk, out_blk],
        out_shape=[jax.ShapeDtypeStruct((T, ATTN_WIDTH), F32)] * 2,
        scratch_shapes=[pltpu.VMEM((18, Q_BLOCK, 2 * Q_BLOCK), F32)] + [pltpu.VMEM((S, LANES), F32)] * 12,
        compiler_params=_params(("arbitrary", "arbitrary")), name="attn_fwd")(proj, proj, proj)


def _attn_bwd(proj, a, lse, da, B, S):
    T = B * S
    geom = _attn_geometry(S)
    n_pairs = ATTN_WIDTH // LANES

    def body(q_ref, k_ref, v_ref, a_ref, lse_ref, do_ref, dq_ref, dk_ref, dv_ref, bias_scr, *scr):
        acc = (scr[0:3], scr[3:6])
        natural_in = (q_ref, k_ref, v_ref, a_ref, lse_ref, do_ref)
        by4_in = scr[6:12]
        _, hm = _head_masks()
        pair = pl.program_id(0)

        @pl.when(pl.program_id(1) == 0)
        def _():
            _init_bias(bias_scr, geom, pair)
        for ref in scr[0:6]:
            ref[...] = jnp.zeros_like(ref)
        for src, dst in zip(natural_in, by4_in):
            _to_by4(src, dst, S)

        def group(di, KW, slices):
            n = len(slices)
            chains = [(g, j) for g in range(n) for j in (0, 1)]
            q_src, k_src, v_src, a_src, lse_src, do_src = natural_in if di == 0 else by4_in
            dq_scr, dk_scr, dv_scr = acc[0 if di == 0 else 1]
            q = [q_src[qsl, :] for qsl, _, _ in slices]
            do = [do_src[qsl, :] for qsl, _, _ in slices]
            doa = [do[g] * a_src[slices[g][0], :] for g in range(n)]
            lse_q = [lse_src[qsl, :] for qsl, _, _ in slices]
            kw = [k_src[ksl, :].astype(BF16) for _, ksl, _ in slices]
            vw = [v_src[ksl, :].astype(BF16) for _, ksl, _ in slices]
            qj = {(g, j): (q[g] * (hm[j] * 0.125)).astype(BF16) for g, j in chains}
            doj = {(g, j): (do[g] * hm[j]).astype(BF16) for g, j in chains}
            s = {(g, j): _nt(qj[(g, j)], kw[g])
                 + bias_scr[di * 6 + slices[g][2] * 2 + j, :, pl.ds(0, KW)] for g, j in chains}
            dp = {(g, j): _nt(doj[(g, j)], vw[g]) for g, j in chains}
            dsum = {(g, j): jnp.sum(doa[g] * hm[j], axis=1, keepdims=True) for g, j in chains}
            p = {(g, j): jnp.exp(s[(g, j)] - lse_q[g][:, HEAD_DIM * j:HEAD_DIM * j + 1]) for g, j in chains}
            ds = {c: (p[c] * (dp[c] - dsum[c])).astype(BF16) for c in chains}
            pb = {c: p[c].astype(BF16) for c in chains}
            dq = [_nn(ds[(g, 0)], kw[g]) * (hm[0] * 0.125) + _nn(ds[(g, 1)], kw[g]) * (hm[1] * 0.125)
                  for g in range(n)]
            both = lambda t, g: jnp.concatenate([t[(g, 0)], t[(g, 1)]], axis=0)
            dkw = [_tn(both(ds, g), both(qj, g)) for g in range(n)]
            dvw = [_tn(both(pb, g), both(doj, g)) for g in range(n)]
            for g, (qsl, ksl, _) in enumerate(slices):
                dq_scr[qsl, :] = dq_scr[qsl, :] + dq[g]
                dk_scr[ksl, :] = dk_scr[ksl, :] + dkw[g]
                dv_scr[ksl, :] = dv_scr[ksl, :] + dvw[g]

        _for_groups(geom, S, 4, group)

        def merge(i, carry):
            natural, by4 = _by4_rows(S, i)
            for nat, split in zip(*acc):
                nat[natural, :] = nat[natural, :] + split[by4, :]
            return carry
        lax.fori_loop(0, S // COPY_ROWS, merge, 0)
        for out, nat in zip((dq_ref, dk_ref, dv_ref), acc[0]):
            out[...] = nat[...].astype(BF16)

    blk = lambda off: pl.BlockSpec((S, LANES), lambda h, b, off=off: (b, off + h))
    return pl.pallas_call(
        body, grid=(n_pairs, B),
        in_specs=[blk(0), blk(n_pairs), blk(2 * n_pairs), blk(0), blk(0), blk(0)],
        out_specs=[blk(0), blk(0), blk(0)],
        out_shape=[jax.ShapeDtypeStruct((T, ATTN_WIDTH), BF16)] * 3,
        scratch_shapes=[pltpu.VMEM((18, Q_BLOCK, 2 * Q_BLOCK), F32)] + [pltpu.VMEM((S, LANES), F32)] * 12,
        compiler_params=_params(("arbitrary", "arbitrary")), name="attn_bwd")(proj, proj, proj, a, lse, da)


def _mid(x2d, t2d, a, proj, kv, w_s, w_sT, b_tab, g_v, w_out, g_final, B, S):
    T = B * S
    tm = 512
    nt = S // tm
    halves = 2
    hrows = tm // halves

    def body(x_ref, t_ref, a_ref, za_ref, ub_ref, vb_ref, zb_ref, qm_ref, zm_ref, kv_ref,
              ws_ref, wsT_ref, btab_ref, gv_ref, wout_ref, gf_ref,
              dx2_ref, da_ref, drest_ref, loss_ref, dwout_ref, dws_ref, dbs_ref, dgv_ref, dgf_ref, dkv_ref,
              dbtab_scr):
        b = pl.program_id(0)
        t = pl.program_id(1)
        first = jnp.logical_and(b == 0, t == 0)
        last = jnp.logical_and(b == B - 1, t == nt - 1)
        _, hm = _head_masks()
        lane_g = lax.broadcasted_iota(jnp.int32, (1, SGU_WIDTH), 1) // HEAD_DIM
        gm = [(lane_g == g).astype(F32) for g in range(N_SGU_GROUPS)]
        H = range(halves)
        rows = [pl.ds(h * hrows, hrows) for h in H]
        ld = lambda ref: [ref[r, :] for r in rows]
        cat = lambda parts, axis: jnp.concatenate(parts, axis=axis)
        chunks = [slice(ci * SGU_CHUNK, (ci + 1) * SGU_CHUNK) for ci in range(hrows // SGU_CHUNK)]
        pairs = [slice(pr * LANES, (pr + 1) * LANES) for pr in range(2)]
        heads = [(pr, j) for pr in range(2) for j in (0, 1)]

        @pl.when(first)
        def _():
            loss_ref[...] = jnp.zeros_like(loss_ref)
            dwout_ref[...] = jnp.zeros_like(dwout_ref)
            dws_ref[...] = jnp.zeros_like(dws_ref)
            dbs_ref[...] = jnp.zeros_like(dbs_ref)
            dgv_ref[...] = jnp.zeros_like(dgv_ref)
            dgf_ref[...] = jnp.zeros_like(dgf_ref)
            dbtab_scr[...] = jnp.zeros_like(dbtab_scr)

        @pl.when(t == 0)
        def _():
            dkv_ref[...] = jnp.zeros_like(dkv_ref)

        a_val = ld(a_ref)
        sil_a = [_silu_parts(z) for z in ld(za_ref)]
        gated_a = [s[0] * a for s, a in zip(sil_a, a_val)]
        u = [_gelu_parts(z) for z in ld(ub_ref)]
        vv = [_gelu_parts(z) for z in ld(vb_ref)]
        vnorm = [_rms(v[0]) for v in vv]
        gv = gv_ref[...]
        vn = [(n[1] * gv).astype(BF16) for n in vnorm]
        w_cat = cat([ws_ref[g].astype(BF16) for g in range(N_SGU_GROUPS)], 1)
        wT_cat = cat([wsT_ref[g].astype(BF16) for g in range(N_SGU_GROUPS)], 1)
        gmb = [m.astype(BF16) for m in gm]
        by_group = lambda chunk: cat([chunk * gmb[g] for g in range(N_SGU_GROUPS)], 0)
        btab = btab_ref[...]
        mixed = [cat([btab + _nn(w_cat, by_group(vn[h][c, :])) for c in chunks], 0) for h in H]
        sg = [u[h][0] * mixed[h] for h in H]
        sil_b = [_silu_parts(z) for z in ld(zb_ref)]
        gated_b = [sil_b[h][0] * sg[h] for h in H]

        kvv = kv_ref[...].astype(BF16)
        kp = [kvv[:, p] for p in pairs]
        vp = [kvv[:, MEM_WIDTH + pr * LANES:MEM_WIDTH + (pr + 1) * LANES] for pr in range(2)]
        qm = ld(qm_ref)
        qj = {(h, pr, j): (qm[h][:, pairs[pr]] * (hm[j] * 0.125)).astype(BF16) for h in H for pr, j in heads}
        sc = {k: _nt(qj[k], kp[k[1]]) for k in qj}
        ex = {k: jnp.exp(sc[k] - jnp.max(sc[k], axis=1, keepdims=True)) for k in qj}
        prob = {k: ex[k] * (1.0 / jnp.sum(ex[k], axis=1, keepdims=True)) for k in qj}
        probb = {k: prob[k].astype(BF16) for k in qj}
        mo = [cat([sum(_nn(probb[(h, pr, j)], vp[pr]) * hm[j] for j in (0, 1)) for pr in range(2)], 1) for h in H]
        sil_m = [_silu_parts(z) for z in ld(zm_ref)]
        gated_m = [sil_m[h][0] * mo[h] for h in H]

        gated = [cat([gated_a[h], gated_b[h], gated_m[h]], 1).astype(BF16) for h in H]
        wout = wout_ref[...]
        x_in = ld(x_ref)
        x2 = [x_in[h] + _nn(gated[h], wout) for h in H]
        fin = [_rms(z) for z in x2]
        gf = gf_ref[...]
        tgt = ld(t_ref)
        err = [fin[h][1] * gf - tgt[h] for h in H]
        loss_ref[...] += sum(jnp.sum(e * e) for e in err) * (0.5 / D_MODEL)

        dy = [e * (1.0 / D_MODEL) for e in err]
        dgf_ref[...] += sum(jnp.sum(dy[h] * fin[h][1], axis=0, keepdims=True) for h in H)
        gdy = [d * gf for d in dy]
        dx2 = [fin[h][0] * (gdy[h] - fin[h][1] * jnp.mean(gdy[h] * fin[h][1], axis=1, keepdims=True)) for h in H]
        for h in H:
            dx2_ref[rows[h], :] = dx2[h]
        dx2b = [d.astype(BF16) for d in dx2]
        dgated = [_nt(d, wout) for d in dx2b]
        dwout_ref[...] += _tn(cat(gated, 0), cat(dx2b, 0))
        dga = [d[:, 0:ATTN_WIDTH] for d in dgated]
        dgb = [d[:, ATTN_WIDTH:ATTN_WIDTH + SGU_WIDTH] for d in dgated]
        dgm = [d[:, ATTN_WIDTH + SGU_WIDTH:] for d in dgated]

        for h in H:
            da_ref[rows[h], :] = dga[h] * sil_a[h][0]
        dza = [dga[h] * a_val[h] * sil_a[h][1] for h in H]

        dsg = [dgb[h] * sil_b[h][0] for h in H]
        dzb = [dgb[h] * sg[h] * sil_b[h][1] for h in H]
        dub = [dsg[h] * mixed[h] * u[h][1] for h in H]
        dmixed = [dsg[h] * u[h][0] for h in H]
        dmixed_b = [d.astype(BF16) for d in dmixed]
        dvn = [cat([_nn(wT_cat, by_group(dmixed_b[h][c, :])) for c in chunks], 0) for h in H]
        for g in range(N_SGU_GROUPS):
            dws_ref[g] += sum(_nt((dmixed[h][c, :] * gm[g]).astype(BF16), vn[h][c, :]) for h in H for c in chunks)
        dbtab_scr[...] += sum(dmixed[h][c, :] for h in H for c in chunks)
        dgv_ref[...] += sum(jnp.sum(dvn[h] * vnorm[h][1], axis=0, keepdims=True) for h in H)
        tv = [d * gv for d in dvn]
        dvv = [vnorm[h][0] * (tv[h] - vnorm[h][1] * jnp.mean(tv[h] * vnorm[h][1], axis=1, keepdims=True)) for h in H]
        dvb = [dvv[h] * vv[h][1] for h in H]

        dmo = [dgm[h] * sil_m[h][0] for h in H]
        dzm = [dgm[h] * mo[h] * sil_m[h][1] for h in H]
        dmoj = {(h, pr, j): (dmo[h][:, pairs[pr]] * hm[j]).astype(BF16) for h in H for pr, j in heads}
        dp = {k: _nt(dmoj[k], vp[k[1]]) for k in qj}
        ds = {k: (prob[k] * (dp[k] - jnp.sum(dp[k] * prob[k], axis=1, keepdims=True))).astype(BF16) for k in qj}
        dqm = [cat([sum(_nn(ds[(h, pr, j)], kp[pr]) * (hm[j] * 0.125) for j in (0, 1)) for pr in range(2)], 1)
               for h in H]
        every = lambda tbl, pr: cat([tbl[(h, pr, j)] for h in H for j in (0, 1)], 0)
        dk = [_tn(every(ds, pr), every(qj, pr)) for pr in range(2)]
        dv = [_tn(every(probb, pr), every(dmoj, pr)) for pr in range(2)]
        dkv_ref[...] += cat(dk + dv, 1)

        for h in H:
            drest_ref[rows[h], :] = cat([dza[h], dub[h], dvb[h], dzb[h], dqm[h], dzm[h]], 1).astype(BF16)

        @pl.when(last)
        def _():
            lane = lax.broadcasted_iota(jnp.int32, (1, LANES), 1)
            dbt = dbtab_scr[...]
            out = jnp.zeros((SGU_CHUNK, LANES), F32)
            for g in range(N_SGU_GROUPS):
                out = out + jnp.where(lane == g, jnp.sum(dbt * gm[g], axis=1, keepdims=True), 0.0)
            dbs_ref[...] = out

    tile = lambda w, cb: pl.BlockSpec((tm, w), lambda b, t, cb=cb: (b * nt + t, cb))
    const = lambda shape: pl.BlockSpec(shape, lambda b, t, n=len(shape): (0,) * n)
    return pl.pallas_call(
        body, grid=(B, nt),
        in_specs=[tile(D_MODEL, 0), tile(D_MODEL, 0), tile(ATTN_WIDTH, 0),
                  tile(ATTN_WIDTH, 3),
                  tile(SGU_WIDTH, 8), tile(SGU_WIDTH, 9), tile(SGU_WIDTH, 10),
                  tile(MEM_WIDTH, 11), tile(MEM_WIDTH, 12),
                  pl.BlockSpec((N_MEM, 2 * MEM_WIDTH), lambda b, t: (b, 0)),
                  const((N_SGU_GROUPS, SGU_CHUNK, SGU_CHUNK)), const((N_SGU_GROUPS, SGU_CHUNK, SGU_CHUNK)),
                  const((SGU_CHUNK, SGU_WIDTH)), const((1, SGU_WIDTH)),
                  const((D_MODEL, D_MODEL)), const((1, D_MODEL))],
        out_specs=[tile(D_MODEL, 0), tile(ATTN_WIDTH, 0), tile(REST_COLS, 0),
                   const((8, LANES)), const((D_MODEL, D_MODEL)),
                   const((N_SGU_GROUPS, SGU_CHUNK, SGU_CHUNK)), const((SGU_CHUNK, LANES)),
                   const((1, SGU_WIDTH)), const((1, D_MODEL)),
                   pl.BlockSpec((N_MEM, 2 * MEM_WIDTH), lambda b, t: (b, 0))],
        out_shape=[jax.ShapeDtypeStruct((T, D_MODEL), F32), jax.ShapeDtypeStruct((T, ATTN_WIDTH), F32),
                   jax.ShapeDtypeStruct((T, REST_COLS), BF16),
                   jax.ShapeDtypeStruct((8, LANES), F32), jax.ShapeDtypeStruct((D_MODEL, D_MODEL), F32),
                   jax.ShapeDtypeStruct((N_SGU_GROUPS, SGU_CHUNK, SGU_CHUNK), F32),
                   jax.ShapeDtypeStruct((SGU_CHUNK, LANES), F32),
                   jax.ShapeDtypeStruct((1, SGU_WIDTH), F32), jax.ShapeDtypeStruct((1, D_MODEL), F32),
                   jax.ShapeDtypeStruct((B * N_MEM, 2 * MEM_WIDTH), F32)],
        scratch_shapes=[pltpu.VMEM((SGU_CHUNK, SGU_WIDTH), F32)],
        compiler_params=_params(("arbitrary", "arbitrary")), name="mid")(
            x2d, t2d, a, proj, proj, proj, proj, proj, proj, kv, w_s, w_sT, b_tab, g_v, w_out, g_final)


def _inproj_bwd_dx(dq, dk, dv, drest, x2d, dx2, g_norm, w_in_t, after=()):
    T = x2d.shape[0]
    tm = 512
    W = ATTN_WIDTH

    def body(dq_ref, dk_ref, dv_ref, dr_ref, x_ref, dx2_ref, g_ref, w_ref, *rest):
        gx_ref, dg_ref = rest[-2:]

        @pl.when(pl.program_id(0) == 0)
        def _():
            dg_ref[...] = jnp.zeros_like(dg_ref)

        halves = [pl.ds(h * (tm // 2), tm // 2) for h in (0, 1)]
        dh = [(_nn(dq_ref[r, :], w_ref[0:W, :]) + _nn(dk_ref[r, :], w_ref[W:2 * W, :])
               + _nn(dv_ref[r, :], w_ref[2 * W:3 * W, :]) + _nn(dr_ref[r, :], w_ref[QKV_COLS:IN_COLS, :]))
              for r in halves]
        nrm = [_rms(x_ref[r, :]) for r in halves]
        dg_ref[...] += sum(jnp.sum(d * n[1], axis=0, keepdims=True) for d, n in zip(dh, nrm))
        g = g_ref[...]
        for r, d, (rstd, xh) in zip(halves, dh, nrm):
            th = d * g
            gx_ref[r, :] = rstd * (th - xh * jnp.mean(th * xh, axis=1, keepdims=True)) + dx2_ref[r, :]

    tile = lambda w: pl.BlockSpec((tm, w), lambda i: (i, 0))
    return pl.pallas_call(
        body, grid=(T // tm,),
        in_specs=[tile(W), tile(W), tile(W), tile(REST_COLS), tile(D_MODEL), tile(D_MODEL),
                  pl.BlockSpec((1, D_MODEL), lambda i: (0, 0)),
                  pl.BlockSpec((IN_COLS, D_MODEL), lambda i: (0, 0))] + _after(after),
        out_specs=[tile(D_MODEL), pl.BlockSpec((1, D_MODEL), lambda i: (0, 0))],
        out_shape=[jax.ShapeDtypeStruct((T, D_MODEL), F32), jax.ShapeDtypeStruct((1, D_MODEL), F32)],
        compiler_params=_params(("arbitrary",)), name="inproj_bwd_dx")(
            dq, dk, dv, drest, x2d, dx2, g_norm, w_in_t, *after)


def _inproj_bwd_dw(dq, dk, dv, drest, x2d, g_norm, reduce_with=None):
    T = x2d.shape[0]
    tm = 512
    nt = T // tm
    W = ATTN_WIDTH
    fused = reduce_with is not None
    others = list(reduce_with) if fused else []
    ns = 1 + len(others)
    shard = IN_COLS // N_CHIPS
    halves = [shard // 2] + [s.shape[1] // 2 for s in others]
    cols = [D_MODEL] + [s.shape[2] for s in others]
    row_block = 32

    def body(dq_ref, dk_ref, dv_ref, dr_ref, x_ref, g_ref, *rest):
        if fused:
            stacks = rest[:ns - 1]
            sends, owns = rest[ns - 1:2 * ns - 1], rest[2 * ns - 1:3 * ns - 1]
            acc, ras, narrow = rest[3 * ns - 1], rest[3 * ns:4 * ns], rest[4 * ns]
            s_sem, r_sem = rest[4 * ns + 1], rest[4 * ns + 2]
            x, y, c, chip, peers, peer_chip = _place()
            sib = (x, y, 1 - c)

            def part(w, k, cc, r0=0, rows=None):
                n = halves[w]
                rows = n if rows is None else rows
                if w == 0:
                    return acc.at[pl.ds(pl.multiple_of(k * shard + cc * n + r0, 8), rows), :]
                return stacks[w - 1].at[k, pl.ds(pl.multiple_of(cc * n + r0, 8), rows), :]

            def swap_other(w):
                theirs = stacks[w - 1].at[:, pl.ds(pl.multiple_of((1 - c) * halves[w], 8), halves[w]), :]
                return _remote(theirs, ras[w], s_sem.at[N_CHIPS - 1 + w], r_sem.at[N_CHIPS - 1 + w], sib)

            def swap_win(k):
                return _remote(narrow.at[k], ras[0].at[k], s_sem.at[k], r_sem.at[k], sib)
        else:
            acc = rest[0]

        @pl.when(pl.program_id(0) == 0)
        def _():
            acc[...] = jnp.zeros_like(acc)
            for w in range(1, ns):
                swap_other(w).start()

        _, xh = _rms(x_ref[...])
        h = (xh * g_ref[...]).astype(BF16)
        acc[0:W, :] += _tn(dq_ref[...], h)
        acc[W:2 * W, :] += _tn(dk_ref[...], h)
        acc[2 * W:3 * W, :] += _tn(dv_ref[...], h)
        acc[QKV_COLS:IN_COLS, :] += _tn(dr_ref[...], h)

        if fused:
            @pl.when(pl.program_id(0) == nt - 1)
            def _():
                for k in range(N_CHIPS):
                    def to_bf16(i, carry, k=k):
                        r0 = pl.multiple_of(i * row_block, row_block)
                        narrow[k, pl.ds(r0, row_block), :] = part(0, k, 1 - c, r0, row_block)[...].astype(BF16)
                        return carry
                    lax.fori_loop(0, halves[0] // row_block, to_bf16, 0)
                    swap_win(k).start()
                for w in list(range(1, ns)) + [0]:
                    if w == 0:
                        for k in range(N_CHIPS):
                            swap_win(k).wait_recv()
                    else:
                        swap_other(w).wait_recv()

                    def sums(i, carry, w=w):
                        r0 = pl.multiple_of(i * row_block, row_block)
                        blk = pl.ds(r0, row_block)
                        for m in range(3):
                            k = peer_chip[m]
                            sends[w][m, blk, :] = (part(w, k, c, r0, row_block)[...]
                                                   + ras[w][k, blk, :].astype(F32)).astype(BF16)
                        owns[w][blk, :] = part(w, chip, c, r0, row_block)[...] + ras[w][chip, blk, :].astype(F32)
                        return carry
                    lax.fori_loop(0, halves[w] // row_block, sums, 0)
                for k in range(N_CHIPS):
                    swap_win(k).wait_send()
                for w in range(1, ns):
                    swap_other(w).wait_send()

    tile = lambda w: pl.BlockSpec((tm, w), lambda i: (i, 0))
    vmem = pl.BlockSpec(memory_space=pltpu.VMEM)
    in_specs = [tile(W), tile(W), tile(W), tile(REST_COLS), tile(D_MODEL), pl.BlockSpec((1, D_MODEL), lambda i: (0, 0))]
    if not fused:
        return pl.pallas_call(
            body, grid=(nt,), in_specs=in_specs,
            out_specs=pl.BlockSpec((IN_COLS, D_MODEL), lambda i: (0, 0)),
            out_shape=jax.ShapeDtypeStruct((IN_COLS, D_MODEL), F32),
            compiler_params=_params(("arbitrary",)), name="inproj_bwd_dw")(dq, dk, dv, drest, x2d, g_norm)
    outs = pl.pallas_call(
        body, grid=(nt,), in_specs=in_specs + [vmem] * (ns - 1), out_specs=[vmem] * (2 * ns),
        out_shape=[jax.ShapeDtypeStruct((3, n, cl), BF16) for n, cl in zip(halves, cols)]
        + [jax.ShapeDtypeStruct((n, cl), F32) for n, cl in zip(halves, cols)],
        scratch_shapes=[pltpu.VMEM((IN_COLS, D_MODEL), F32)]
        + [pltpu.VMEM((N_CHIPS, n, cl), BF16 if w == 0 else F32) for w, (n, cl) in enumerate(zip(halves, cols))]
        + [pltpu.VMEM((N_CHIPS, halves[0], D_MODEL), BF16)]
        + [pltpu.SemaphoreType.DMA((N_CHIPS - 1 + ns,)), pltpu.SemaphoreType.DMA((N_CHIPS - 1 + ns,))],
        compiler_params=_params(("arbitrary",)), name="inproj_bwd_dw_reduce")(
            dq, dk, dv, drest, x2d, g_norm, *others)
    return outs[:ns], outs[ns:]


def _adamw_update(w, g, m, v):
    nm = ADAM_B1 * m + (1.0 - ADAM_B1) * g
    nv = ADAM_B2 * v + (1.0 - ADAM_B2) * (g * g)
    m_hat = nm / (1.0 - ADAM_B1 ** ADAM_STEP)
    v_hat = nv / (1.0 - ADAM_B2 ** ADAM_STEP)
    return -ADAM_LR * (m_hat / (jnp.sqrt(v_hat) + ADAM_EPS) + ADAM_WD * w), nm, nv


def _adamw(w, g, m, v, name):
    R, C = w.shape
    br = max(r for r in range(8, 257, 8) if R % r == 0)

    def body(w_ref, g_ref, m_ref, v_ref, d_ref, nm_ref, nv_ref):
        d_ref[...], nm_ref[...], nv_ref[...] = _adamw_update(w_ref[...], g_ref[...], m_ref[...], v_ref[...])

    spec = pl.BlockSpec((br, C), lambda i: (i, 0))
    return pl.pallas_call(
        body, grid=(R // br,), in_specs=[spec] * 4, out_specs=[spec] * 3,
        out_shape=[jax.ShapeDtypeStruct((R, C), F32)] * 3,
        compiler_params=_params(("arbitrary",)), name=name)(w, g, m, v)


def _adamw_small(g_packed, ws, ms, vs):
    n = len(ws)

    def body(*refs):
        g_ref = refs[0]
        w_refs, m_refs, v_refs = refs[1:1 + n], refs[1 + n:1 + 2 * n], refs[1 + 2 * n:1 + 3 * n]
        outs = refs[1 + 3 * n:]
        off = 0
        for i, (_, used, padded) in enumerate(_SMALL_PARTS[:n]):
            g = g_ref[off:off + used, :]
            delta, nm, nv = _adamw_update(w_refs[i][...], g, m_refs[i][...], v_refs[i][...])
            outs[4 * i][...], outs[4 * i + 1][...], outs[4 * i + 2][...], outs[4 * i + 3][...] = g, delta, nm, nv
            off += padded

    outs = pl.pallas_call(
        body, out_shape=[jax.ShapeDtypeStruct(w.shape, F32) for w in ws for _ in range(4)],
        compiler_params=_params(), name="adamw_small")(g_packed, *ws, *ms, *vs)
    return [outs[4 * i:4 * i + 4] for i in range(n)]


def _place():
    x, y, c = lax.axis_index("x"), lax.axis_index("y"), lax.axis_index("c")
    chip = 2 * x + y
    peers = [(x, 1 - y), (1 - x, y), (1 - x, 1 - y)]
    peer_chip = [2 * px + py for px, py in peers]
    return x, y, c, chip, peers, peer_chip


def _remote(src, dst, send_sem, recv_sem, dev):
    return pltpu.make_async_remote_copy(src_ref=src, dst_ref=dst, send_sem=send_sem, recv_sem=recv_sem,
                                        device_id=dev, device_id_type=MESH)


def _ag_weights(weights, late=()):
    nw, nl = len(weights), len(late)

    def body(*refs):
        srcs, late_srcs = refs[:nw], refs[nw:nw + nl]
        outs, late_bf, late_land = (refs[nw + nl:2 * nw + nl], refs[2 * nw + nl:2 * nw + 2 * nl],
                                    refs[2 * nw + 2 * nl:2 * nw + 3 * nl])
        s_ici, r_ici, s_d2d, r_d2d = refs[2 * nw + 3 * nl:]
        x, y, c = lax.axis_index("x"), lax.axis_index("y"), lax.axis_index("c")
        chip = 2 * x + y
        sib = (x, y, 1 - c)
        first = ((x + 1 - c) % 2, (y + c) % 2)
        second = ((x + c) % 2, (y + 1 - c) % 2)
        first_chip, second_chip = 2 * first[0] + first[1], 2 * second[0] + second[1]
        diag_chip = 3 - chip
        for src, out in zip(srcs, outs):
            out[chip] = src[...].astype(BF16)

        def half(out, k, cc):
            rows = out.shape[1] // 2
            return out.at[k, pl.ds(pl.multiple_of(cc * rows, 16), rows), :]

        def ici(w, slot, out, k, dev):
            blk = half(out, k, c)
            return _remote(blk, blk, s_ici.at[nw * slot + w], r_ici.at[nw * slot + w], (dev[0], dev[1], c))

        def d2d(w, slot, out, k, cc):
            blk = half(out, k, cc)
            return _remote(blk, blk, s_d2d.at[nw * slot + w], r_d2d.at[nw * slot + w], sib)

        sent = []
        for slot, dev in enumerate((first, second)):
            for w, out in enumerate(outs):
                sent.append(ici(w, slot, out, chip, dev))
                sent[-1].start()
        for src, bf, land in zip(late_srcs, late_bf, late_land):
            bf[...] = src[...].astype(BF16)
            land[...] = jnp.zeros_like(land)
            land[chip] = bf[...]
        for slot, k, dev in ((0, first_chip, first), (1, second_chip, second), (2, diag_chip, second)):
            for w, out in enumerate(outs):
                ici(w, slot, out, k, dev).wait_recv()
                if slot == 0:
                    sent.append(ici(w, 2, out, k, second))
                    sent[-1].start()
                sent.append(d2d(w, slot, out, k, c))
                sent[-1].start()
        for slot, k in ((0, second_chip), (1, first_chip), (2, diag_chip)):
            for w, out in enumerate(outs):
                d2d(w, slot, out, k, 1 - c).wait_recv()
        for cp in sent:
            cp.wait_send()

    vmem = pl.BlockSpec(memory_space=pltpu.VMEM)
    outs = pl.pallas_call(
        body,
        out_shape=[jax.ShapeDtypeStruct((N_CHIPS,) + w.shape, BF16) for w in weights]
        + [jax.ShapeDtypeStruct(w.shape, BF16) for w in late]
        + [jax.ShapeDtypeStruct((N_CHIPS,) + w.shape, BF16) for w in late],
        in_specs=[vmem] * (nw + nl), out_specs=[vmem] * (nw + 2 * nl),
        scratch_shapes=[pltpu.SemaphoreType.DMA((3 * nw,))] * 4,
        compiler_params=pltpu.CompilerParams(vmem_limit_bytes=VMEM_LIMIT), name="ag_weights")(*weights, *late)
    return outs[:nw], outs[nw:nw + nl], outs[nw + nl:]


_HBM = pl.BlockSpec(memory_space=pltpu.HBM)
_SEM = pl.BlockSpec(memory_space=pltpu.SEMAPHORE)
_ANY = pl.BlockSpec(memory_space=pl.ANY)
_DATAFLOW = pltpu.SideEffectType.DATAFLOW_SIDE_EFFECTING


def _in_hbm(a):
    return pltpu.with_memory_space_constraint(a, pltpu.HBM)


def _exchange_copies(gather, srcs, lands, send_sems, recv_sems):
    nw = len(srcs)
    x, y, c, chip, peers, peer_chip = _place()
    pairs = []
    for m, (px, py) in enumerate(peers):
        for w in range(nw):
            sems = (send_sems.at[nw * m + w], recv_sems.at[nw * m + w], (px, py, c))
            if gather:
                pairs.append((_remote(srcs[w], lands[w].at[chip], *sems),
                              _remote(srcs[w], lands[w].at[peer_chip[m]], *sems)))
            else:
                pairs.append((_remote(srcs[w].at[m], lands[w].at[m], *sems),) * 2)
    return pairs


def _exchange_start(gather, srcs, after, name, lands=None):
    nw = len(srcs)
    n_copies = 3 * nw

    def body(*refs):
        send_sems, recv_sems = refs[2 * nw + 1], refs[2 * nw + 2]
        for start, _ in _exchange_copies(gather, refs[:nw], refs[nw:2 * nw], send_sems, recv_sems):
            start.start()
        refs[-1][...] = jnp.zeros_like(refs[-1])

    if lands is None:
        lands = [lax.empty(((N_CHIPS,) + s.shape) if gather else s.shape, s.dtype) for s in srcs]
    lands = [_in_hbm(l) for l in lands]
    return pl.pallas_call(
        body, name=name,
        out_shape=(pltpu.SemaphoreType.DMA((n_copies,)), pltpu.SemaphoreType.DMA((n_copies,)))
        + tuple(pltpu.HBM(s.shape, s.dtype) for s in srcs)
        + tuple(pltpu.HBM(l.shape, l.dtype) for l in lands)
        + (jax.ShapeDtypeStruct((8, LANES), F32),),
        in_specs=[_HBM] * (2 * nw) + [_ANY],
        out_specs=(_SEM, _SEM) + (_HBM,) * (2 * nw) + (pl.BlockSpec(memory_space=pltpu.VMEM),),
        input_output_aliases={i: 2 + i for i in range(2 * nw)},
        compiler_params=pltpu.CompilerParams(has_side_effects=_DATAFLOW),
    )(*[_in_hbm(s) for s in srcs], *lands, after)


def _exchange_wait(gather, started, after, name):
    nw = (len(started) - 3) // 2
    send_sems, recv_sems = started[0], started[1]
    thru = started[2:2 + 2 * nw]

    def body(*refs):
        for _, arrival in _exchange_copies(gather, refs[:nw], refs[nw:2 * nw], refs[2 * nw], refs[2 * nw + 1]):
            arrival.wait_send()
            arrival.wait_recv()

    outs = pl.pallas_call(
        body, name=name,
        out_shape=tuple(pltpu.HBM(t.shape, t.dtype) for t in thru),
        in_specs=[_HBM] * (2 * nw) + [_SEM, _SEM, _ANY], out_specs=(_HBM,) * (2 * nw),
        input_output_aliases={i: i for i in range(2 * nw)},
        compiler_params=pltpu.CompilerParams(has_side_effects=_DATAFLOW),
    )(*thru, send_sems, recv_sems, after)
    return outs[nw:]


def _reduce_last(owns, landed, g_small):
    ns = len(owns)
    row_block = 32
    hs = SMALL_ROWS // 2

    def body(*refs):
        own_refs, land_refs, gsm_ref = refs[:ns], refs[ns:2 * ns], refs[2 * ns]
        out_refs, osm_ref = refs[2 * ns + 1:3 * ns + 1], refs[3 * ns + 1]
        ra_sm, p_sm, s_sem, r_sem, sm_s, sm_r = refs[3 * ns + 2:]
        x, y, c, chip, peers, peer_chip = _place()
        sib = (x, y, 1 - c)
        half = lambda cc: pl.ds(pl.multiple_of(cc * hs, 8), hs)
        sm_a = _remote(gsm_ref.at[half(1 - c), :], ra_sm, sm_s.at[0], sm_r.at[0], sib)
        sm_a.start()
        swaps = [sm_a]
        for w in range(ns):
            n = own_refs[w].shape[0]

            def total(i, carry, w=w, n=n):
                r0 = pl.multiple_of(i * row_block, row_block)
                blk = pl.ds(r0, row_block)
                acc = own_refs[w][blk, :]
                for m in range(3):
                    acc = acc + land_refs[w][m, blk, :].astype(F32)
                out_refs[w][pl.ds(pl.multiple_of(c * n + r0, 8), row_block), :] = acc
                return carry
            lax.fori_loop(0, n // row_block, total, 0)
            mine = out_refs[w].at[pl.ds(pl.multiple_of(c * n, 8), n), :]
            swaps.append(_remote(mine, mine, s_sem.at[w], r_sem.at[w], sib))
            swaps[-1].start()
        sm_a.wait_recv()
        p_sm[chip] = gsm_ref[half(c), :] + ra_sm[...]
        for m, (px, py) in enumerate(peers):
            swaps.append(_remote(p_sm.at[chip], p_sm.at[chip], sm_s.at[1 + m], sm_r.at[1 + m], (px, py, c)))
            swaps[-1].start()
        for m, (px, py) in enumerate(peers):
            _remote(p_sm.at[chip], p_sm.at[peer_chip[m]], sm_s.at[1 + m], sm_r.at[1 + m], (px, py, c)).wait_recv()
        osm_ref[half(c), :] = (p_sm[0] + p_sm[1]) + (p_sm[2] + p_sm[3])
        swaps.append(_remote(osm_ref.at[half(c), :], osm_ref.at[half(c), :], sm_s.at[4], sm_r.at[4], sib))
        swaps[-1].start()
        for w in range(ns):
            n = own_refs[w].shape[0]
            theirs = out_refs[w].at[pl.ds(pl.multiple_of((1 - c) * n, 8), n), :]
            _remote(theirs, theirs, s_sem.at[w], r_sem.at[w], sib).wait_recv()
        _remote(osm_ref.at[half(1 - c), :], osm_ref.at[half(1 - c), :], sm_s.at[4], sm_r.at[4], sib).wait_recv()
        for cp in swaps:
            cp.wait_send()

    vmem = pl.BlockSpec(memory_space=pltpu.VMEM)
    return pl.pallas_call(
        body, out_shape=[jax.ShapeDtypeStruct((2 * o.shape[0], o.shape[1]), F32) for o in owns]
        + [jax.ShapeDtypeStruct((SMALL_ROWS, LANES), F32)],
        in_specs=[vmem] * (2 * ns + 1), out_specs=[vmem] * (ns + 1),
        scratch_shapes=[pltpu.VMEM((hs, LANES), F32), pltpu.VMEM((N_CHIPS, hs, LANES), F32),
                        pltpu.SemaphoreType.DMA((ns,)), pltpu.SemaphoreType.DMA((ns,)),
                        pltpu.SemaphoreType.DMA((5,)), pltpu.SemaphoreType.DMA((5,))],
        compiler_params=pltpu.CompilerParams(vmem_limit_bytes=VMEM_LIMIT),
        name="reduce_last")(*owns, *landed, g_small)


_SMALL_PARTS = (("g_norm", 8, 8), ("w_s", 512, 512), ("b_s", 4, 8), ("g_v", 2, 8), ("g_mem", 8, 8),
                ("g_final", 8, 8), ("loss", 1, 8))
_LOSS_ROW = SMALL_ROWS - 8
assert sum(p for _, _, p in _SMALL_PARTS) == SMALL_ROWS


def _pack_small(parts, loss=None):
    loss_row = jnp.zeros((1, LANES), F32) if loss is None else jnp.broadcast_to(loss.reshape(1, 1), (1, LANES))
    rows = []
    for (name, used, padded), p in zip(_SMALL_PARTS, list(parts) + [loss_row]):
        p = p.reshape(used, LANES)
        if padded > used:
            p = jnp.pad(p, ((0, padded - used), (0, 0)))
        rows.append(p)
    return jnp.concatenate(rows, axis=0)


def _local_step(x, mem, target, g_norm, w_in, w_s, b_s, g_v, g_mem, late_weights, g_final,
                fwd_token=None, on_dw=None):
    B, S, _ = x.shape
    x2d = x.reshape(B * S, D_MODEL)
    t2d = target.reshape(B * S, D_MODEL)
    mem2d = mem.reshape(B * N_MEM, D_MODEL)

    proj = _inproj_fwd(x2d, g_norm, w_in, after=() if fwd_token is None else (fwd_token,))
    w_kv, w_out = late_weights(proj)
    kv = _kv_fwd(mem2d, g_mem, w_kv)
    a, lse = _attn_fwd(proj, B, S)
    w_sT = jnp.swapaxes(w_s, 1, 2)
    b_tab = jnp.repeat(b_s.T, HEAD_DIM, axis=1)
    (dx2, da, drest, loss, d_wout, d_ws, d_bs, d_gv, d_gf, dkv) = _mid(
        x2d, t2d, a, proj, kv, w_s, w_sT, b_tab, g_v, w_out, g_final, B, S)
    d_wkv, d_gmem = _kv_bwd(mem2d, g_mem, w_kv, dkv)
    dq, dk, dv = _attn_bwd(proj, a, lse, da, B, S)
    if on_dw is None:
        d_win = _inproj_bwd_dw(dq, dk, dv, drest, x2d, g_norm)
        after = ()
    else:
        d_win = None
        by_chip = lambda g: g.reshape((N_CHIPS, g.shape[0] // N_CHIPS, g.shape[1]))
        after = (on_dw(*_inproj_bwd_dw(dq, dk, dv, drest, x2d, g_norm, reduce_with=[by_chip(d_wkv), by_chip(d_wout)])),)
    grad_x, d_gnorm = _inproj_bwd_dx(dq, dk, dv, drest, x2d, dx2, g_norm, w_in, after=after)
    d_bs = d_bs[:, :N_SGU_GROUPS].T
    return (loss[0, 0], grad_x.reshape(B, S, D_MODEL),
            dict(g_norm=d_gnorm, w_in=d_win, w_s=d_ws, b_s=d_bs, g_v=d_gv, g_mem=d_gmem, w_kv=d_wkv,
                 w_out=d_wout, g_final=d_gf))


def kernel(x, mem, g_norm, w_in, w_sgu_spatial, b_sgu_spatial, g_sgu_v, g_mem, w_mem_kv, w_out, g_final, loss_target, m_g_norm, m_w_in, m_w_sgu_spatial, m_b_sgu_spatial, m_g_sgu_v, m_g_mem, m_w_mem_kv, m_w_out, m_g_final, v_g_norm, v_w_in, v_w_sgu_spatial, v_b_sgu_spatial, v_g_sgu_v, v_g_mem, v_w_mem_kv, v_w_out, v_g_final):
    t = lambda w: jnp.swapaxes(w[0], 0, 1)
    (win_all,), late_shards, late_lands = _ag_weights([t(w_in)], [w_mem_kv[0], w_out[0]])
    w_in_full = win_all.reshape(-1, win_all.shape[-1])
    late = _exchange_start(True, list(late_shards), win_all, "gather_late_start", lands=late_lands)

    def late_weights(proj):
        return [z.reshape(-1, z.shape[-1]) for z in _exchange_wait(True, late, proj, "gather_late_wait")]

    scatter = {}

    def on_dw(sends, owns):
        scatter["own"] = owns
        scatter["started"] = _exchange_start(False, list(sends), owns[0], "scatter_start")
        return scatter["started"][-1]

    loss, grad_x, g = _local_step(
        x, mem, loss_target, g_norm, w_in_full, w_sgu_spatial[0], b_sgu_spatial[0], g_sgu_v, g_mem,
        late_weights, g_final.reshape(1, D_MODEL), fwd_token=late[-1], on_dw=on_dw)

    small_names = ("g_norm", "w_s", "b_s", "g_v", "g_mem", "g_final")
    g_small = _pack_small([g[n] for n in small_names], loss)
    landed = _exchange_wait(False, scatter["started"], g_small, "scatter_wait")
    gr_in, gr_kv, gr_out, gr_small = _reduce_last(scatter["own"], landed, g_small)
    loss = gr_small[_LOSS_ROW, 0]

    small_w = (g_norm, w_sgu_spatial, b_sgu_spatial, g_sgu_v, g_mem, g_final)
    small_m = (m_g_norm, m_w_sgu_spatial, m_b_sgu_spatial, m_g_sgu_v, m_g_mem, m_g_final)
    small_v = (v_g_norm, v_w_sgu_spatial, v_b_sgu_spatial, v_g_sgu_v, v_g_mem, v_g_final)
    rows = lambda ws: [w.reshape(-1, LANES) for w in ws]
    small = [[z.reshape(w.shape) for z in four]
             for w, four in zip(small_w, _adamw_small(gr_small, rows(small_w), rows(small_m), rows(small_v)))]
    d_in, nm_in, nv_in = _adamw(t(w_in), gr_in, t(m_w_in), t(v_w_in), "adamw_w_in")
    gr_in, d_in, nm_in, nv_in = [jnp.swapaxes(z, 0, 1) for z in (gr_in, d_in, nm_in, nv_in)]
    d_kv, nm_kv, nv_kv = _adamw(w_mem_kv[0], gr_kv, m_w_mem_kv[0], v_w_mem_kv[0], "adamw_w_kv")
    d_out, nm_out, nv_out = _adamw(w_out[0], gr_out, m_w_out[0], v_w_out[0], "adamw_w_out")

    def leaves(kind, big_in, big_kv, big_out):
        s_norm, s_ws, s_bs, s_gv, s_gmem, s_gf = [four[kind] for four in small]
        return [s_norm, big_in[None], s_ws, s_bs, s_gv, s_gmem, big_kv[None], big_out[None], s_gf]

    return (loss, grad_x, *leaves(0, gr_in, gr_kv, gr_out), *leaves(1, d_in, d_kv, d_out),
            *leaves(2, nm_in, nm_kv, nm_out), *leaves(3, nv_in, nv_kv, nv_out))
```

```python
import functools

import jax
import jax.numpy as jnp
from jax import lax
from jax.experimental import pallas as pl
from jax.experimental.pallas import tpu as pltpu

F32 = jnp.float32
BF16 = jnp.bfloat16
MESH = pl.DeviceIdType.MESH

D_MODEL = 1024
ATTN_WIDTH = 512
SGU_WIDTH = 256
MEM_WIDTH = 256
N_MEM = 256
IN_COLS = 3328
QKV_COLS = 3 * ATTN_WIDTH
REST_COLS = IN_COLS - QKV_COLS
SGU_CHUNK = 128
N_SGU_GROUPS = 4
EPS = 1e-6
NEG_INF = -1e30
DILATIONS = (1, 4, 16)
RADIUS = 64
Q_BLOCK = 128
LANES = 128
HEAD_DIM = 64

ADAM_LR = 0.001
ADAM_B1 = 0.9
ADAM_B2 = 0.999
ADAM_EPS = 1e-08
ADAM_WD = 0.01
ADAM_STEP = 10

N_CHIPS = 4
VMEM_LIMIT = 56 * 1024 * 1024
SMALL_ROWS = 560


def _params(sem=None, vmem=VMEM_LIMIT):
    return pltpu.CompilerParams(dimension_semantics=sem, vmem_limit_bytes=vmem)


def _nn(a, b):
    return jnp.dot(a, b, preferred_element_type=F32)


def _nt(a, b):
    return lax.dot_general(a, b, (((1,), (1,)), ((), ())), preferred_element_type=F32)


def _tn(a, b):
    return lax.dot_general(a, b, (((0,), (0,)), ((), ())), preferred_element_type=F32)


def _rms(x):
    r = lax.rsqrt(jnp.mean(x * x, axis=-1, keepdims=True) + EPS)
    return r, x * r


def _head_masks():
    lane = lax.broadcasted_iota(jnp.int32, (1, LANES), 1)
    lo = lane < HEAD_DIM
    return lo, (lo.astype(F32), (~lo).astype(F32))


def _silu_parts(z):
    s = jax.nn.sigmoid(z)
    return z * s, s * (1.0 + z * (1.0 - s))


def _gelu_parts(x):
    c = 0.7978845608028654
    x2 = x * x
    s = jax.nn.sigmoid((2.0 * c) * (x + 0.044715 * (x * x2)))
    return x * s, s * (1.0 + x * (1.0 - s) * ((2.0 * c) * (1.0 + 3.0 * 0.044715 * x2)))


def _after(tokens):
    return [pl.BlockSpec(memory_space=pl.ANY)] * len(tokens)


def _inproj_fwd(x2d, g_norm, w_in_t, after=()):
    T = x2d.shape[0]
    tm = 512

    def body(x_ref, g_ref, w_ref, *rest):
        o_ref = rest[-1]
        _, xh = _rms(x_ref[...])
        h = (xh * g_ref[...]).astype(BF16)
        o_ref[...] = _nt(h, w_ref[...])

    return pl.pallas_call(
        body, grid=(T // tm,),
        in_specs=[pl.BlockSpec((tm, D_MODEL), lambda i: (i, 0)),
                  pl.BlockSpec((1, D_MODEL), lambda i: (0, 0)),
                  pl.BlockSpec((IN_COLS, D_MODEL), lambda i: (0, 0))] + _after(after),
        out_specs=pl.BlockSpec((tm, IN_COLS), lambda i: (i, 0)),
        out_shape=jax.ShapeDtypeStruct((T, IN_COLS), F32),
        compiler_params=_params(("arbitrary",)), name="inproj_fwd")(x2d, g_norm, w_in_t, *after)


def _kv_fwd(mem2d, g_mem, w_kv):
    Tm = mem2d.shape[0]

    def body(m_ref, g_ref, w_ref, o_ref):
        _, mh = _rms(m_ref[...])
        o_ref[...] = _nn((mh * g_ref[...]).astype(BF16), w_ref[...])

    return pl.pallas_call(
        body, out_shape=jax.ShapeDtypeStruct((Tm, 2 * MEM_WIDTH), F32),
        compiler_params=_params(), name="kv_fwd")(mem2d, g_mem, w_kv)


def _kv_bwd(mem2d, g_mem, w_kv, dkv):
    Tm = mem2d.shape[0]

    def body(m_ref, g_ref, w_ref, dkv_ref, dw_ref, dg_ref):
        _, mh = _rms(m_ref[...])
        memn = (mh * g_ref[...]).astype(BF16)
        dkvb = dkv_ref[...].astype(BF16)
        dw_ref[...] = _tn(memn, dkvb)
        dmemn = _nt(dkvb, w_ref[...])
        dg_ref[...] = jnp.sum(dmemn * mh, axis=0, keepdims=True)

    return pl.pallas_call(
        body, out_shape=(jax.ShapeDtypeStruct((D_MODEL, 2 * MEM_WIDTH), F32),
                         jax.ShapeDtypeStruct((1, D_MODEL), F32)),
        compiler_params=_params(), name="kv_bwd")(mem2d, g_mem, w_kv, dkv)


def _attn_geometry(S):
    geom = []
    for d in DILATIONS:
        L = S // d
        assert L % Q_BLOCK == 0
        geom.append((d, L, min(2 * Q_BLOCK, L), L // Q_BLOCK))
    return geom


def _init_bias(bias_scr, geom, hp):
    row = lax.broadcasted_iota(jnp.int32, (Q_BLOCK, 2 * Q_BLOCK), 0)
    col = lax.broadcasted_iota(jnp.int32, (Q_BLOCK, 2 * Q_BLOCK), 1)
    for j in (0, 1):
        bits = (126 - (2 * hp + j)) * (1 << 23)
        slope = lax.bitcast_convert_type(jnp.full((1, 1), bits, jnp.int32), F32)
        for di, (d, _, _, _) in enumerate(geom):
            for cls, off in enumerate((0, -RADIUS, -2 * RADIUS)):
                dist = jnp.abs(col - row + off)
                bias_scr[di * 6 + cls * 2 + j] = jnp.where(
                    dist <= RADIUS, -(slope * float(d)) * dist.astype(F32), NEG_INF)


SPLIT = 4
COPY_ROWS = 256


def _by4_rows(S, step):
    per_class = S // SPLIT // COPY_ROWS
    r, j = step // per_class, step % per_class
    return (pl.ds(r + SPLIT * j * COPY_ROWS, COPY_ROWS, stride=SPLIT),
            pl.ds(pl.multiple_of(r * (S // SPLIT) + j * COPY_ROWS, COPY_ROWS), COPY_ROWS))


def _to_by4(src, dst, S):
    def step(i, carry):
        natural, by4 = _by4_rows(S, i)
        dst[by4, :] = src[natural, :]
        return carry
    lax.fori_loop(0, S // COPY_ROWS, step, 0)


def _block_slices(d, L, KW, nqb, r, qb, S):
    qs = qb * Q_BLOCK
    ks = jnp.clip(qs - RADIUS, 0, L - KW)
    cls = jnp.where(qb == 0, 0, jnp.where(qb == nqb - 1, 2, 1))
    if d == 1:
        qsl = pl.ds(pl.multiple_of(qs, Q_BLOCK), Q_BLOCK)
        ksl = pl.ds(pl.multiple_of(ks, RADIUS), KW)
    elif d == SPLIT:
        qsl = pl.ds(pl.multiple_of(r * L + qs, Q_BLOCK), Q_BLOCK)
        ksl = pl.ds(pl.multiple_of(r * L + ks, RADIUS), KW)
    else:
        sub = d // SPLIT
        base = (r % SPLIT) * (S // SPLIT) + r // SPLIT
        qsl = pl.ds(base + qs * sub, Q_BLOCK, stride=sub)
        ksl = pl.ds(base + ks * sub, KW, stride=sub)
    return qsl, ksl, cls


def _for_groups(geom, S, group, fn):
    for di, (d, L, KW, nqb) in enumerate(geom):
        n = group[di]
        assert (d * nqb) % n == 0

        def step(it, carry, di=di, d=d, L=L, KW=KW, nqb=nqb, n=n):
            slices = []
            for g in range(n):
                i = it * n + g
                slices.append(_block_slices(d, L, KW, nqb, i // nqb, i % nqb, S))
            fn(di, KW, slices)
            return carry
        lax.fori_loop(0, d * nqb // n, step, 0)


def _attn_fwd(proj, B, S):
    T = B * S
    geom = _attn_geometry(S)
    n_pairs = ATTN_WIDTH // LANES

    def body(q_ref, k_ref, v_ref, a_ref, lse_ref, bias_scr, q4, k4, v4, *per_dilation):
        o_scr, m_scr, l_scr = per_dilation[0:3], per_dilation[3:6], per_dilation[6:9]
        lo, hm = _head_masks()
        pair = pl.program_id(0)

        @pl.when(pl.program_id(1) == 0)
        def _():
            _init_bias(bias_scr, geom, pair)
        for src, dst in ((q_ref, q4), (k_ref, k4), (v_ref, v4)):
            _to_by4(src, dst, S)

        def group(di, KW, slices):
            chains = [(g, j) for g in range(len(slices)) for j in (0, 1)]
            q_src, k_src, v_src = (q_ref, k_ref, v_ref) if di == 0 else (q4, k4, v4)
            q = [q_src[qsl, :] for qsl, _, _ in slices]
            kw = [k_src[ksl, :].astype(BF16) for _, ksl, _ in slices]
            vw = [v_src[ksl, :].astype(BF16) for _, ksl, _ in slices]
            s = {(g, j): _nt((q[g] * (hm[j] * 0.125)).astype(BF16), kw[g])
                 + bias_scr[di * 6 + slices[g][2] * 2 + j, :, pl.ds(0, KW)] for g, j in chains}
            m = {c: jnp.max(s[c], axis=1, keepdims=True) for c in chains}
            p = {c: jnp.exp(s[c] - m[c]) for c in chains}
            l = {c: jnp.sum(p[c], axis=1, keepdims=True) for c in chains}
            o = {(g, j): _nn(p[(g, j)].astype(BF16), vw[g]) for g, j in chains}
            for g, (qsl, _, _) in enumerate(slices):
                o_scr[di][qsl, :] = jnp.where(lo, o[(g, 0)], o[(g, 1)])
                m_scr[di][qsl, :] = jnp.where(lo, m[(g, 0)], m[(g, 1)])
                l_scr[di][qsl, :] = jnp.where(lo, l[(g, 0)], l[(g, 1)])

        _for_groups(geom, S, (16, 16, 16), group)

        def combine(i, carry):
            natural, by4 = _by4_rows(S, i)
            rows = [natural, by4, by4]
            ms = [m_scr[di][rows[di], :] for di in range(3)]
            mx = jnp.maximum(jnp.maximum(ms[0], ms[1]), ms[2])
            num = 0.0
            den = 0.0
            for di in range(3):
                w = jnp.exp(ms[di] - mx)
                num = num + w * o_scr[di][rows[di], :]
                den = den + w * l_scr[di][rows[di], :]
            a_ref[natural, :] = num / den
            lse_ref[natural, :] = mx + jnp.log(den)
            return carry

        lax.fori_loop(0, S // COPY_ROWS, combine, 0)

    blk = lambda off: pl.BlockSpec((S, LANES), lambda h, b, off=off: (b, off + h))
    out_blk = pl.BlockSpec((S, LANES), lambda h, b: (b, h))
    return pl.pallas_call(
        body, grid=(n_pairs, B),
        in_specs=[blk(0), blk(n_pairs), blk(2 * n_pairs)],
        out_specs=[out_blk, out_blk],
        out_shape=[jax.ShapeDtypeStruct((T, ATTN_WIDTH), F32)] * 2,
        scratch_shapes=[pltpu.VMEM((18, Q_BLOCK, 2 * Q_BLOCK), F32)] + [pltpu.VMEM((S, LANES), F32)] * 12,
        compiler_params=_params(("arbitrary", "arbitrary")), name="attn_fwd")(proj, proj, proj)


def _attn_bwd(proj, a, lse, da, B, S):
    T = B * S
    geom = _attn_geometry(S)
    n_pairs = ATTN_WIDTH // LANES

    def body(q_ref, k_ref, v_ref, a_ref, lse_ref, do_ref, dq_ref, dk_ref, dv_ref, bias_scr, *scr):
        acc = (scr[0:3], scr[3:6])
        natural_in = (q_ref, k_ref, v_ref, a_ref, lse_ref, do_ref)
        by4_in = scr[6:12]
        _, hm = _head_masks()
        pair = pl.program_id(0)

        @pl.when(pl.program_id(1) == 0)
        def _():
            _init_bias(bias_scr, geom, pair)
        for ref in scr[0:6]:
            ref[...] = jnp.zeros_like(ref)
        for src, dst in zip(natural_in, by4_in):
            _to_by4(src, dst, S)

        def group(di, KW, slices):
            n = len(slices)
            chains = [(g, j) for g in range(n) for j in (0, 1)]
            q_src, k_src, v_src, a_src, lse_src, do_src = natural_in if di == 0 else by4_in
            dq_scr, dk_scr, dv_scr = acc[0 if di == 0 else 1]
            q = [q_src[qsl, :] for qsl, _, _ in slices]
            do = [do_src[qsl, :] for qsl, _, _ in slices]
            doa = [do[g] * a_src[slices[g][0], :] for g in range(n)]
            lse_q = [lse_src[qsl, :] for qsl, _, _ in slices]
            kw = [k_src[ksl, :].astype(BF16) for _, ksl, _ in slices]
            vw = [v_src[ksl, :].astype(BF16) for _, ksl, _ in slices]
            qj = {(g, j): (q[g] * (hm[j] * 0.125)).astype(BF16) for g, j in chains}
            doj = {(g, j): (do[g] * hm[j]).astype(BF16) for g, j in chains}
            s = {(g, j): _nt(qj[(g, j)], kw[g])
                 + bias_scr[di * 6 + slices[g][2] * 2 + j, :, pl.ds(0, KW)] for g, j in chains}
            dp = {(g, j): _nt(doj[(g, j)], vw[g]) for g, j in chains}
            dsum = {(g, j): jnp.sum(doa[g] * hm[j], axis=1, keepdims=True) for g, j in chains}
            p = {(g, j): jnp.exp(s[(g, j)] - lse_q[g][:, HEAD_DIM * j:HEAD_DIM * j + 1]) for g, j in chains}
            ds = {c: (p[c] * (dp[c] - dsum[c])).astype(BF16) for c in chains}
            pb = {c: p[c].astype(BF16) for c in chains}
            dq = [_nn(ds[(g, 0)], kw[g]) * (hm[0] * 0.125) + _nn(ds[(g, 1)], kw[g]) * (hm[1] * 0.125)
                  for g in range(n)]
            both = lambda t, g: jnp.concatenate([t[(g, 0)], t[(g, 1)]], axis=0)
            dkw = [_tn(both(ds, g), both(qj, g)) for g in range(n)]
            dvw = [_tn(both(pb, g), both(doj, g)) for g in range(n)]
            for g, (qsl, ksl, _) in enumerate(slices):
                dq_scr[qsl, :] = dq_scr[qsl, :] + dq[g]
                dk_scr[ksl, :] = dk_scr[ksl, :] + dkw[g]
                dv_scr[ksl, :] = dv_scr[ksl, :] + dvw[g]

        _for_groups(geom, S, (4, 4, 8), group)

        def merge(i, carry):
            natural, by4 = _by4_rows(S, i)
            for nat, split in zip(*acc):
                nat[natural, :] = nat[natural, :] + split[by4, :]
            return carry
        lax.fori_loop(0, S // COPY_ROWS, merge, 0)
        for out, nat in zip((dq_ref, dk_ref, dv_ref), acc[0]):
            out[...] = nat[...].astype(BF16)

    blk = lambda off: pl.BlockSpec((S, LANES), lambda h, b, off=off: (b, off + h))
    return pl.pallas_call(
        body, grid=(n_pairs, B),
        in_specs=[blk(0), blk(n_pairs), blk(2 * n_pairs), blk(0), blk(0), blk(0)],
        out_specs=[blk(0), blk(0), blk(0)],
        out_shape=[jax.ShapeDtypeStruct((T, ATTN_WIDTH), BF16)] * 3,
        scratch_shapes=[pltpu.VMEM((18, Q_BLOCK, 2 * Q_BLOCK), F32)] + [pltpu.VMEM((S, LANES), F32)] * 12,
        compiler_params=_params(("arbitrary", "arbitrary")), name="attn_bwd")(proj, proj, proj, a, lse, da)


def _mid(x2d, t2d, a, proj, kv, w_s, w_sT, b_tab, g_v, w_out, g_final, B, S):
    T = B * S
    tm = 512
    nt = S // tm
    halves = 2
    hrows = tm // halves

    def body(x_ref, t_ref, a_ref, za_ref, ub_ref, vb_ref, zb_ref, qm_ref, zm_ref, kv_ref,
              ws_ref, wsT_ref, btab_ref, gv_ref, wout_ref, gf_ref,
              dx2_ref, da_ref, drest_ref, loss_ref, dwout_ref, dws_ref, dbs_ref, dgv_ref, dgf_ref, dkv_ref,
              dbtab_scr):
        b = pl.program_id(0)
        t = pl.program_id(1)
        first = jnp.logical_and(b == 0, t == 0)
        last = jnp.logical_and(b == B - 1, t == nt - 1)
        _, hm = _head_masks()
        lane_g = lax.broadcasted_iota(jnp.int32, (1, SGU_WIDTH), 1) // HEAD_DIM
        gm = [(lane_g == g).astype(F32) for g in range(N_SGU_GROUPS)]
        H = range(halves)
        rows = [pl.ds(h * hrows, hrows) for h in H]
        ld = lambda ref: [ref[r, :] for r in rows]
        cat = lambda parts, axis: jnp.concatenate(parts, axis=axis)
        chunks = [slice(ci * SGU_CHUNK, (ci + 1) * SGU_CHUNK) for ci in range(hrows // SGU_CHUNK)]
        pairs = [slice(pr * LANES, (pr + 1) * LANES) for pr in range(2)]
        heads = [(pr, j) for pr in range(2) for j in (0, 1)]

        @pl.when(first)
        def _():
            loss_ref[...] = jnp.zeros_like(loss_ref)
            dwout_ref[...] = jnp.zeros_like(dwout_ref)
            dws_ref[...] = jnp.zeros_like(dws_ref)
            dbs_ref[...] = jnp.zeros_like(dbs_ref)
            dgv_ref[...] = jnp.zeros_like(dgv_ref)
            dgf_ref[...] = jnp.zeros_like(dgf_ref)
            dbtab_scr[...] = jnp.zeros_like(dbtab_scr)

        @pl.when(t == 0)
        def _():
            dkv_ref[...] = jnp.zeros_like(dkv_ref)

        a_val = ld(a_ref)
        sil_a = [_silu_parts(z) for z in ld(za_ref)]
        gated_a = [s[0] * a for s, a in zip(sil_a, a_val)]
        u = [_gelu_parts(z) for z in ld(ub_ref)]
        vv = [_gelu_parts(z) for z in ld(vb_ref)]
        vnorm = [_rms(v[0]) for v in vv]
        gv = gv_ref[...]
        vn = [(n[1] * gv).astype(BF16) for n in vnorm]
        w_cat = cat([ws_ref[g].astype(BF16) for g in range(N_SGU_GROUPS)], 1)
        wT_cat = cat([wsT_ref[g].astype(BF16) for g in range(N_SGU_GROUPS)], 1)
        gmb = [m.astype(BF16) for m in gm]
        by_group = lambda chunk: cat([chunk * gmb[g] for g in range(N_SGU_GROUPS)], 0)
        btab = btab_ref[...]
        mixed = [cat([btab + _nn(w_cat, by_group(vn[h][c, :])) for c in chunks], 0) for h in H]
        sg = [u[h][0] * mixed[h] for h in H]
        sil_b = [_silu_parts(z) for z in ld(zb_ref)]
        gated_b = [sil_b[h][0] * sg[h] for h in H]

        kvv = kv_ref[...].astype(BF16)
        kp = [kvv[:, p] for p in pairs]
        vp = [kvv[:, MEM_WIDTH + pr * LANES:MEM_WIDTH + (pr + 1) * LANES] for pr in range(2)]
        qm = ld(qm_ref)
        qj = {(h, pr, j): (qm[h][:, pairs[pr]] * (hm[j] * 0.125)).astype(BF16) for h in H for pr, j in heads}
        sc = {k: _nt(qj[k], kp[k[1]]) for k in qj}
        ex = {k: jnp.exp(sc[k] - jnp.max(sc[k], axis=1, keepdims=True)) for k in qj}
        prob = {k: ex[k] * (1.0 / jnp.sum(ex[k], axis=1, keepdims=True)) for k in qj}
        probb = {k: prob[k].astype(BF16) for k in qj}
        mo = [cat([sum(_nn(probb[(h, pr, j)], vp[pr]) * hm[j] for j in (0, 1)) for pr in range(2)], 1) for h in H]
        sil_m = [_silu_parts(z) for z in ld(zm_ref)]
        gated_m = [sil_m[h][0] * mo[h] for h in H]

        gated = [cat([gated_a[h], gated_b[h], gated_m[h]], 1).astype(BF16) for h in H]
        wout = wout_ref[...]
        x_in = ld(x_ref)
        x2 = [x_in[h] + _nn(gated[h], wout) for h in H]
        fin = [_rms(z) for z in x2]
        gf = gf_ref[...]
        tgt = ld(t_ref)
        err = [fin[h][1] * gf - tgt[h] for h in H]
        loss_ref[...] += sum(jnp.sum(e * e) for e in err) * (0.5 / D_MODEL)

        dy = [e * (1.0 / D_MODEL) for e in err]
        dgf_ref[...] += sum(jnp.sum(dy[h] * fin[h][1], axis=0, keepdims=True) for h in H)
        gdy = [d * gf for d in dy]
        dx2 = [fin[h][0] * (gdy[h] - fin[h][1] * jnp.mean(gdy[h] * fin[h][1], axis=1, keepdims=True)) for h in H]
        for h in H:
            dx2_ref[rows[h], :] = dx2[h]
        dx2b = [d.astype(BF16) for d in dx2]
        dgated = [_nt(d, wout) for d in dx2b]
        dwout_ref[...] += _tn(cat(gated, 0), cat(dx2b, 0))
        dga = [d[:, 0:ATTN_WIDTH] for d in dgated]
        dgb = [d[:, ATTN_WIDTH:ATTN_WIDTH + SGU_WIDTH] for d in dgated]
        dgm = [d[:, ATTN_WIDTH + SGU_WIDTH:] for d in dgated]

        for h in H:
            da_ref[rows[h], :] = dga[h] * sil_a[h][0]
        dza = [dga[h] * a_val[h] * sil_a[h][1] for h in H]

        dsg = [dgb[h] * sil_b[h][0] for h in H]
        dzb = [dgb[h] * sg[h] * sil_b[h][1] for h in H]
        dub = [dsg[h] * mixed[h] * u[h][1] for h in H]
        dmixed = [dsg[h] * u[h][0] for h in H]
        dmixed_b = [d.astype(BF16) for d in dmixed]
        dvn = [cat([_nn(wT_cat, by_group(dmixed_b[h][c, :])) for c in chunks], 0) for h in H]
        for g in range(N_SGU_GROUPS):
            dws_ref[g] += sum(_nt((dmixed[h][c, :] * gm[g]).astype(BF16), vn[h][c, :]) for h in H for c in chunks)
        dbtab_scr[...] += sum(dmixed[h][c, :] for h in H for c in chunks)
        dgv_ref[...] += sum(jnp.sum(dvn[h] * vnorm[h][1], axis=0, keepdims=True) for h in H)
        tv = [d * gv for d in dvn]
        dvv = [vnorm[h][0] * (tv[h] - vnorm[h][1] * jnp.mean(tv[h] * vnorm[h][1], axis=1, keepdims=True)) for h in H]
        dvb = [dvv[h] * vv[h][1] for h in H]

        dmo = [dgm[h] * sil_m[h][0] for h in H]
        dzm = [dgm[h] * mo[h] * sil_m[h][1] for h in H]
        dmoj = {(h, pr, j): (dmo[h][:, pairs[pr]] * hm[j]).astype(BF16) for h in H for pr, j in heads}
        dp = {k: _nt(dmoj[k], vp[k[1]]) for k in qj}
        ds = {k: (prob[k] * (dp[k] - jnp.sum(dp[k] * prob[k], axis=1, keepdims=True))).astype(BF16) for k in qj}
        dqm = [cat([sum(_nn(ds[(h, pr, j)], kp[pr]) * (hm[j] * 0.125) for j in (0, 1)) for pr in range(2)], 1)
               for h in H]
        every = lambda tbl, pr: cat([tbl[(h, pr, j)] for h in H for j in (0, 1)], 0)
        dk = [_tn(every(ds, pr), every(qj, pr)) for pr in range(2)]
        dv = [_tn(every(probb, pr), every(dmoj, pr)) for pr in range(2)]
        dkv_ref[...] += cat(dk + dv, 1)

        for h in H:
            drest_ref[rows[h], :] = cat([dza[h], dub[h], dvb[h], dzb[h], dqm[h], dzm[h]], 1).astype(BF16)

        @pl.when(last)
        def _():
            lane = lax.broadcasted_iota(jnp.int32, (1, LANES), 1)
            dbt = dbtab_scr[...]
            out = jnp.zeros((SGU_CHUNK, LANES), F32)
            for g in range(N_SGU_GROUPS):
                out = out + jnp.where(lane == g, jnp.sum(dbt * gm[g], axis=1, keepdims=True), 0.0)
            dbs_ref[...] = out

    tile = lambda w, cb: pl.BlockSpec((tm, w), lambda b, t, cb=cb: (b * nt + t, cb))
    const = lambda shape: pl.BlockSpec(shape, lambda b, t, n=len(shape): (0,) * n)
    return pl.pallas_call(
        body, grid=(B, nt),
        in_specs=[tile(D_MODEL, 0), tile(D_MODEL, 0), tile(ATTN_WIDTH, 0),
                  tile(ATTN_WIDTH, 3),
                  tile(SGU_WIDTH, 8), tile(SGU_WIDTH, 9), tile(SGU_WIDTH, 10),
                  tile(MEM_WIDTH, 11), tile(MEM_WIDTH, 12),
                  pl.BlockSpec((N_MEM, 2 * MEM_WIDTH), lambda b, t: (b, 0)),
                  const((N_SGU_GROUPS, SGU_CHUNK, SGU_CHUNK)), const((N_SGU_GROUPS, SGU_CHUNK, SGU_CHUNK)),
                  const((SGU_CHUNK, SGU_WIDTH)), const((1, SGU_WIDTH)),
                  const((D_MODEL, D_MODEL)), const((1, D_MODEL))],
        out_specs=[tile(D_MODEL, 0), tile(ATTN_WIDTH, 0), tile(REST_COLS, 0),
                   const((8, LANES)), const((D_MODEL, D_MODEL)),
                   const((N_SGU_GROUPS, SGU_CHUNK, SGU_CHUNK)), const((SGU_CHUNK, LANES)),
                   const((1, SGU_WIDTH)), const((1, D_MODEL)),
                   pl.BlockSpec((N_MEM, 2 * MEM_WIDTH), lambda b, t: (b, 0))],
        out_shape=[jax.ShapeDtypeStruct((T, D_MODEL), F32), jax.ShapeDtypeStruct((T, ATTN_WIDTH), F32),
                   jax.ShapeDtypeStruct((T, REST_COLS), BF16),
                   jax.ShapeDtypeStruct((8, LANES), F32), jax.ShapeDtypeStruct((D_MODEL, D_MODEL), F32),
                   jax.ShapeDtypeStruct((N_SGU_GROUPS, SGU_CHUNK, SGU_CHUNK), F32),
                   jax.ShapeDtypeStruct((SGU_CHUNK, LANES), F32),
                   jax.ShapeDtypeStruct((1, SGU_WIDTH), F32), jax.ShapeDtypeStruct((1, D_MODEL), F32),
                   jax.ShapeDtypeStruct((B * N_MEM, 2 * MEM_WIDTH), F32)],
        scratch_shapes=[pltpu.VMEM((SGU_CHUNK, SGU_WIDTH), F32)],
        compiler_params=_params(("arbitrary", "arbitrary")), name="mid")(
            x2d, t2d, a, proj, proj, proj, proj, proj, proj, kv, w_s, w_sT, b_tab, g_v, w_out, g_final)


def _inproj_bwd_dx(dq, dk, dv, drest, x2d, dx2, g_norm, w_in_t, after=()):
    T = x2d.shape[0]
    tm = 512
    W = ATTN_WIDTH

    def body(dq_ref, dk_ref, dv_ref, dr_ref, x_ref, dx2_ref, g_ref, w_ref, *rest):
        gx_ref, dg_ref = rest[-2:]

        @pl.when(pl.program_id(0) == 0)
        def _():
            dg_ref[...] = jnp.zeros_like(dg_ref)

        halves = [pl.ds(h * (tm // 2), tm // 2) for h in (0, 1)]
        dh = [(_nn(dq_ref[r, :], w_ref[0:W, :]) + _nn(dk_ref[r, :], w_ref[W:2 * W, :])
               + _nn(dv_ref[r, :], w_ref[2 * W:3 * W, :]) + _nn(dr_ref[r, :], w_ref[QKV_COLS:IN_COLS, :]))
              for r in halves]
        nrm = [_rms(x_ref[r, :]) for r in halves]
        dg_ref[...] += sum(jnp.sum(d * n[1], axis=0, keepdims=True) for d, n in zip(dh, nrm))
        g = g_ref[...]
        for r, d, (rstd, xh) in zip(halves, dh, nrm):
            th = d * g
            gx_ref[r, :] = rstd * (th - xh * jnp.mean(th * xh, axis=1, keepdims=True)) + dx2_ref[r, :]

    tile = lambda w: pl.BlockSpec((tm, w), lambda i: (i, 0))
    return pl.pallas_call(
        body, grid=(T // tm,),
        in_specs=[tile(W), tile(W), tile(W), tile(REST_COLS), tile(D_MODEL), tile(D_MODEL),
                  pl.BlockSpec((1, D_MODEL), lambda i: (0, 0)),
                  pl.BlockSpec((IN_COLS, D_MODEL), lambda i: (0, 0))] + _after(after),
        out_specs=[tile(D_MODEL), pl.BlockSpec((1, D_MODEL), lambda i: (0, 0))],
        out_shape=[jax.ShapeDtypeStruct((T, D_MODEL), F32), jax.ShapeDtypeStruct((1, D_MODEL), F32)],
        compiler_params=_params(("arbitrary",)), name="inproj_bwd_dx")(
            dq, dk, dv, drest, x2d, dx2, g_norm, w_in_t, *after)


def _inproj_bwd_dw(dq, dk, dv, drest, x2d, g_norm, reduce_with=None):
    T = x2d.shape[0]
    tm = 512
    nt = T // tm
    W = ATTN_WIDTH
    fused = reduce_with is not None
    others = list(reduce_with) if fused else []
    ns = 1 + len(others)
    shard = IN_COLS // N_CHIPS
    halves = [shard // 2] + [s.shape[1] // 2 for s in others]
    cols = [D_MODEL] + [s.shape[2] for s in others]
    row_block = 32

    def body(dq_ref, dk_ref, dv_ref, dr_ref, x_ref, g_ref, *rest):
        if fused:
            stacks = rest[:ns - 1]
            sends, owns = rest[ns - 1:2 * ns - 1], rest[2 * ns - 1:3 * ns - 1]
            acc, ras, narrow = rest[3 * ns - 1], rest[3 * ns:4 * ns], rest[4 * ns]
            s_sem, r_sem = rest[4 * ns + 1], rest[4 * ns + 2]
            x, y, c, chip, peers, peer_chip = _place()
            sib = (x, y, 1 - c)

            def part(w, k, cc, r0=0, rows=None):
                n = halves[w]
                rows = n if rows is None else rows
                if w == 0:
                    return acc.at[pl.ds(pl.multiple_of(k * shard + cc * n + r0, 8), rows), :]
                return stacks[w - 1].at[k, pl.ds(pl.multiple_of(cc * n + r0, 8), rows), :]

            def swap_other(w):
                theirs = stacks[w - 1].at[:, pl.ds(pl.multiple_of((1 - c) * halves[w], 8), halves[w]), :]
                return _remote(theirs, ras[w], s_sem.at[N_CHIPS - 1 + w], r_sem.at[N_CHIPS - 1 + w], sib)

            def swap_win(k):
                return _remote(narrow.at[k], ras[0].at[k], s_sem.at[k], r_sem.at[k], sib)
        else:
            acc = rest[0]

        @pl.when(pl.program_id(0) == 0)
        def _():
            acc[...] = jnp.zeros_like(acc)
            for w in range(1, ns):
                swap_other(w).start()

        _, xh = _rms(x_ref[...])
        h = (xh * g_ref[...]).astype(BF16)
        acc[0:W, :] += _tn(dq_ref[...], h)
        acc[W:2 * W, :] += _tn(dk_ref[...], h)
        acc[2 * W:3 * W, :] += _tn(dv_ref[...], h)
        acc[QKV_COLS:IN_COLS, :] += _tn(dr_ref[...], h)

        if fused:
            @pl.when(pl.program_id(0) == nt - 1)
            def _():
                for k in range(N_CHIPS):
                    def to_bf16(i, carry, k=k):
                        r0 = pl.multiple_of(i * row_block, row_block)
                        narrow[k, pl.ds(r0, row_block), :] = part(0, k, 1 - c, r0, row_block)[...].astype(BF16)
                        return carry
                    lax.fori_loop(0, halves[0] // row_block, to_bf16, 0)
                    swap_win(k).start()
                for w in list(range(1, ns)) + [0]:
                    if w == 0:
                        for k in range(N_CHIPS):
                            swap_win(k).wait_recv()
                    else:
                        swap_other(w).wait_recv()

                    def sums(i, carry, w=w):
                        r0 = pl.multiple_of(i * row_block, row_block)
                        blk = pl.ds(r0, row_block)
                        for m in range(3):
                            k = peer_chip[m]
                            sends[w][m, blk, :] = (part(w, k, c, r0, row_block)[...]
                                                   + ras[w][k, blk, :].astype(F32)).astype(BF16)
                        owns[w][blk, :] = part(w, chip, c, r0, row_block)[...] + ras[w][chip, blk, :].astype(F32)
                        return carry
                    lax.fori_loop(0, halves[w] // row_block, sums, 0)
                for k in range(N_CHIPS):
                    swap_win(k).wait_send()
                for w in range(1, ns):
                    swap_other(w).wait_send()

    tile = lambda w: pl.BlockSpec((tm, w), lambda i: (i, 0))
    vmem = pl.BlockSpec(memory_space=pltpu.VMEM)
    in_specs = [tile(W), tile(W), tile(W), tile(REST_COLS), tile(D_MODEL), pl.BlockSpec((1, D_MODEL), lambda i: (0, 0))]
    if not fused:
        return pl.pallas_call(
            body, grid=(nt,), in_specs=in_specs,
            out_specs=pl.BlockSpec((IN_COLS, D_MODEL), lambda i: (0, 0)),
            out_shape=jax.ShapeDtypeStruct((IN_COLS, D_MODEL), F32),
            compiler_params=_params(("arbitrary",)), name="inproj_bwd_dw")(dq, dk, dv, drest, x2d, g_norm)
    outs = pl.pallas_call(
        body, grid=(nt,), in_specs=in_specs + [vmem] * (ns - 1), out_specs=[vmem] * (2 * ns),
        out_shape=[jax.ShapeDtypeStruct((3, n, cl), BF16) for n, cl in zip(halves, cols)]
        + [jax.ShapeDtypeStruct((n, cl), F32) for n, cl in zip(halves, cols)],
        scratch_shapes=[pltpu.VMEM((IN_COLS, D_MODEL), F32)]
        + [pltpu.VMEM((N_CHIPS, n, cl), BF16 if w == 0 else F32) for w, (n, cl) in enumerate(zip(halves, cols))]
        + [pltpu.VMEM((N_CHIPS, halves[0], D_MODEL), BF16)]
        + [pltpu.SemaphoreType.DMA((N_CHIPS - 1 + ns,)), pltpu.SemaphoreType.DMA((N_CHIPS - 1 + ns,))],
        compiler_params=_params(("arbitrary",)), name="inproj_bwd_dw_reduce")(
            dq, dk, dv, drest, x2d, g_norm, *others)
    return outs[:ns], outs[ns:]


def _adamw_update(w, g, m, v):
    nm = ADAM_B1 * m + (1.0 - ADAM_B1) * g
    nv = ADAM_B2 * v + (1.0 - ADAM_B2) * (g * g)
    m_hat = nm / (1.0 - ADAM_B1 ** ADAM_STEP)
    v_hat = nv / (1.0 - ADAM_B2 ** ADAM_STEP)
    return -ADAM_LR * (m_hat / (jnp.sqrt(v_hat) + ADAM_EPS) + ADAM_WD * w), nm, nv


def _adamw(w, g, m, v, name):
    R, C = w.shape
    br = max(r for r in range(8, 257, 8) if R % r == 0)

    def body(w_ref, g_ref, m_ref, v_ref, d_ref, nm_ref, nv_ref):
        d_ref[...], nm_ref[...], nv_ref[...] = _adamw_update(w_ref[...], g_ref[...], m_ref[...], v_ref[...])

    spec = pl.BlockSpec((br, C), lambda i: (i, 0))
    return pl.pallas_call(
        body, grid=(R // br,), in_specs=[spec] * 4, out_specs=[spec] * 3,
        out_shape=[jax.ShapeDtypeStruct((R, C), F32)] * 3,
        compiler_params=_params(("arbitrary",)), name=name)(w, g, m, v)


def _adamw_small(g_packed, ws, ms, vs):
    n = len(ws)

    def body(*refs):
        g_ref = refs[0]
        w_refs, m_refs, v_refs = refs[1:1 + n], refs[1 + n:1 + 2 * n], refs[1 + 2 * n:1 + 3 * n]
        outs = refs[1 + 3 * n:]
        off = 0
        for i, (_, used, padded) in enumerate(_SMALL_PARTS[:n]):
            g = g_ref[off:off + used, :]
            delta, nm, nv = _adamw_update(w_refs[i][...], g, m_refs[i][...], v_refs[i][...])
            outs[4 * i][...], outs[4 * i + 1][...], outs[4 * i + 2][...], outs[4 * i + 3][...] = g, delta, nm, nv
            off += padded

    outs = pl.pallas_call(
        body, out_shape=[jax.ShapeDtypeStruct(w.shape, F32) for w in ws for _ in range(4)],
        compiler_params=_params(), name="adamw_small")(g_packed, *ws, *ms, *vs)
    return [outs[4 * i:4 * i + 4] for i in range(n)]


def _place():
    x, y, c = lax.axis_index("x"), lax.axis_index("y"), lax.axis_index("c")
    chip = 2 * x + y
    peers = [(x, 1 - y), (1 - x, y), (1 - x, 1 - y)]
    peer_chip = [2 * px + py for px, py in peers]
    return x, y, c, chip, peers, peer_chip


def _remote(src, dst, send_sem, recv_sem, dev):
    return pltpu.make_async_remote_copy(src_ref=src, dst_ref=dst, send_sem=send_sem, recv_sem=recv_sem,
                                        device_id=dev, device_id_type=MESH)


def _ag_weights(weights, late=()):
    nw, nl = len(weights), len(late)

    def body(*refs):
        srcs, late_srcs = refs[:nw], refs[nw:nw + nl]
        outs, late_bf, late_land = (refs[nw + nl:2 * nw + nl], refs[2 * nw + nl:2 * nw + 2 * nl],
                                    refs[2 * nw + 2 * nl:2 * nw + 3 * nl])
        s_ici, r_ici, s_d2d, r_d2d = refs[2 * nw + 3 * nl:]
        x, y, c = lax.axis_index("x"), lax.axis_index("y"), lax.axis_index("c")
        chip = 2 * x + y
        sib = (x, y, 1 - c)
        first = ((x + 1 - c) % 2, (y + c) % 2)
        second = ((x + c) % 2, (y + 1 - c) % 2)
        first_chip, second_chip = 2 * first[0] + first[1], 2 * second[0] + second[1]
        diag_chip = 3 - chip
        for src, out in zip(srcs, outs):
            out[chip] = src[...].astype(BF16)

        def half(out, k, cc):
            rows = out.shape[1] // 2
            return out.at[k, pl.ds(pl.multiple_of(cc * rows, 16), rows), :]

        def ici(w, slot, out, k, dev):
            blk = half(out, k, c)
            return _remote(blk, blk, s_ici.at[nw * slot + w], r_ici.at[nw * slot + w], (dev[0], dev[1], c))

        def d2d(w, slot, out, k, cc):
            blk = half(out, k, cc)
            return _remote(blk, blk, s_d2d.at[nw * slot + w], r_d2d.at[nw * slot + w], sib)

        sent = []
        for slot, dev in enumerate((first, second)):
            for w, out in enumerate(outs):
                sent.append(ici(w, slot, out, chip, dev))
                sent[-1].start()
        for src, bf, land in zip(late_srcs, late_bf, late_land):
            bf[...] = src[...].astype(BF16)
            land[...] = jnp.zeros_like(land)
            land[chip] = bf[...]
        for slot, k, dev in ((0, first_chip, first), (1, second_chip, second), (2, diag_chip, second)):
            for w, out in enumerate(outs):
                ici(w, slot, out, k, dev).wait_recv()
                if slot == 0:
                    sent.append(ici(w, 2, out, k, second))
                    sent[-1].start()
                sent.append(d2d(w, slot, out, k, c))
                sent[-1].start()
        for slot, k in ((0, second_chip), (1, first_chip), (2, diag_chip)):
            for w, out in enumerate(outs):
                d2d(w, slot, out, k, 1 - c).wait_recv()
        for cp in sent:
            cp.wait_send()

    vmem = pl.BlockSpec(memory_space=pltpu.VMEM)
    outs = pl.pallas_call(
        body,
        out_shape=[jax.ShapeDtypeStruct((N_CHIPS,) + w.shape, BF16) for w in weights]
        + [jax.ShapeDtypeStruct(w.shape, BF16) for w in late]
        + [jax.ShapeDtypeStruct((N_CHIPS,) + w.shape, BF16) for w in late],
        in_specs=[vmem] * (nw + nl), out_specs=[vmem] * (nw + 2 * nl),
        scratch_shapes=[pltpu.SemaphoreType.DMA((3 * nw,))] * 4,
        compiler_params=pltpu.CompilerParams(vmem_limit_bytes=VMEM_LIMIT), name="ag_weights")(*weights, *late)
    return outs[:nw], outs[nw:nw + nl], outs[nw + nl:]


_HBM = pl.BlockSpec(memory_space=pltpu.HBM)
_SEM = pl.BlockSpec(memory_space=pltpu.SEMAPHORE)
_ANY = pl.BlockSpec(memory_space=pl.ANY)
_DATAFLOW = pltpu.SideEffectType.DATAFLOW_SIDE_EFFECTING


def _in_hbm(a):
    return pltpu.with_memory_space_constraint(a, pltpu.HBM)


def _exchange_copies(gather, srcs, lands, send_sems, recv_sems):
    nw = len(srcs)
    x, y, c, chip, peers, peer_chip = _place()
    pairs = []
    for m, (px, py) in enumerate(peers):
        for w in range(nw):
            sems = (send_sems.at[nw * m + w], recv_sems.at[nw * m + w], (px, py, c))
            if gather:
                pairs.append((_remote(srcs[w], lands[w].at[chip], *sems),
                              _remote(srcs[w], lands[w].at[peer_chip[m]], *sems)))
            else:
                pairs.append((_remote(srcs[w].at[m], lands[w].at[m], *sems),) * 2)
    return pairs


def _exchange_start(gather, srcs, after, name, lands=None):
    nw = len(srcs)
    n_copies = 3 * nw

    def body(*refs):
        send_sems, recv_sems = refs[2 * nw + 1], refs[2 * nw + 2]
        for start, _ in _exchange_copies(gather, refs[:nw], refs[nw:2 * nw], send_sems, recv_sems):
            start.start()
        refs[-1][...] = jnp.zeros_like(refs[-1])

    if lands is None:
        lands = [lax.empty(((N_CHIPS,) + s.shape) if gather else s.shape, s.dtype) for s in srcs]
    lands = [_in_hbm(l) for l in lands]
    return pl.pallas_call(
        body, name=name,
        out_shape=(pltpu.SemaphoreType.DMA((n_copies,)), pltpu.SemaphoreType.DMA((n_copies,)))
        + tuple(pltpu.HBM(s.shape, s.dtype) for s in srcs)
        + tuple(pltpu.HBM(l.shape, l.dtype) for l in lands)
        + (jax.ShapeDtypeStruct((8, LANES), F32),),
        in_specs=[_HBM] * (2 * nw) + [_ANY],
        out_specs=(_SEM, _SEM) + (_HBM,) * (2 * nw) + (pl.BlockSpec(memory_space=pltpu.VMEM),),
        input_output_aliases={i: 2 + i for i in range(2 * nw)},
        compiler_params=pltpu.CompilerParams(has_side_effects=_DATAFLOW),
    )(*[_in_hbm(s) for s in srcs], *lands, after)


def _exchange_wait(gather, started, after, name):
    nw = (len(started) - 3) // 2
    send_sems, recv_sems = started[0], started[1]
    thru = started[2:2 + 2 * nw]

    def body(*refs):
        for _, arrival in _exchange_copies(gather, refs[:nw], refs[nw:2 * nw], refs[2 * nw], refs[2 * nw + 1]):
            arrival.wait_send()
            arrival.wait_recv()

    outs = pl.pallas_call(
        body, name=name,
        out_shape=tuple(pltpu.HBM(t.shape, t.dtype) for t in thru),
        in_specs=[_HBM] * (2 * nw) + [_SEM, _SEM, _ANY], out_specs=(_HBM,) * (2 * nw),
        input_output_aliases={i: i for i in range(2 * nw)},
        compiler_params=pltpu.CompilerParams(has_side_effects=_DATAFLOW),
    )(*thru, send_sems, recv_sems, after)
    return outs[nw:]


def _reduce_last(owns, landed, g_small):
    ns = len(owns)
    row_block = 32
    hs = SMALL_ROWS // 2

    def body(*refs):
        own_refs, land_refs, gsm_ref = refs[:ns], refs[ns:2 * ns], refs[2 * ns]
        out_refs, osm_ref = refs[2 * ns + 1:3 * ns + 1], refs[3 * ns + 1]
        ra_sm, p_sm, s_sem, r_sem, sm_s, sm_r = refs[3 * ns + 2:]
        x, y, c, chip, peers, peer_chip = _place()
        sib = (x, y, 1 - c)
        half = lambda cc: pl.ds(pl.multiple_of(cc * hs, 8), hs)
        sm_a = _remote(gsm_ref.at[half(1 - c), :], ra_sm, sm_s.at[0], sm_r.at[0], sib)
        sm_a.start()
        swaps = [sm_a]
        for w in range(ns):
            n = own_refs[w].shape[0]

            def total(i, carry, w=w, n=n):
                r0 = pl.multiple_of(i * row_block, row_block)
                blk = pl.ds(r0, row_block)
                acc = own_refs[w][blk, :]
                for m in range(3):
                    acc = acc + land_refs[w][m, blk, :].astype(F32)
                out_refs[w][pl.ds(pl.multiple_of(c * n + r0, 8), row_block), :] = acc
                return carry
            lax.fori_loop(0, n // row_block, total, 0)
            mine = out_refs[w].at[pl.ds(pl.multiple_of(c * n, 8), n), :]
            swaps.append(_remote(mine, mine, s_sem.at[w], r_sem.at[w], sib))
            swaps[-1].start()
        sm_a.wait_recv()
        p_sm[chip] = gsm_ref[half(c), :] + ra_sm[...]
        for m, (px, py) in enumerate(peers):
            swaps.append(_remote(p_sm.at[chip], p_sm.at[chip], sm_s.at[1 + m], sm_r.at[1 + m], (px, py, c)))
            swaps[-1].start()
        for m, (px, py) in enumerate(peers):
            _remote(p_sm.at[chip], p_sm.at[peer_chip[m]], sm_s.at[1 + m], sm_r.at[1 + m], (px, py, c)).wait_recv()
        osm_ref[half(c), :] = (p_sm[0] + p_sm[1]) + (p_sm[2] + p_sm[3])
        swaps.append(_remote(osm_ref.at[half(c), :], osm_ref.at[half(c), :], sm_s.at[4], sm_r.at[4], sib))
        swaps[-1].start()
        for w in range(ns):
            n = own_refs[w].shape[0]
            theirs = out_refs[w].at[pl.ds(pl.multiple_of((1 - c) * n, 8), n), :]
            _remote(theirs, theirs, s_sem.at[w], r_sem.at[w], sib).wait_recv()
        _remote(osm_ref.at[half(1 - c), :], osm_ref.at[half(1 - c), :], sm_s.at[4], sm_r.at[4], sib).wait_recv()
        for cp in swaps:
            cp.wait_send()

    vmem = pl.BlockSpec(memory_space=pltpu.VMEM)
    return pl.pallas_call(
        body, out_shape=[jax.ShapeDtypeStruct((2 * o.shape[0], o.shape[1]), F32) for o in owns]
        + [jax.ShapeDtypeStruct((SMALL_ROWS, LANES), F32)],
        in_specs=[vmem] * (2 * ns + 1), out_specs=[vmem] * (ns + 1),
        scratch_shapes=[pltpu.VMEM((hs, LANES), F32), pltpu.VMEM((N_CHIPS, hs, LANES), F32),
                        pltpu.SemaphoreType.DMA((ns,)), pltpu.SemaphoreType.DMA((ns,)),
                        pltpu.SemaphoreType.DMA((5,)), pltpu.SemaphoreType.DMA((5,))],
        compiler_params=pltpu.CompilerParams(vmem_limit_bytes=VMEM_LIMIT),
        name="reduce_last")(*owns, *landed, g_small)


_SMALL_PARTS = (("g_norm", 8, 8), ("w_s", 512, 512), ("b_s", 4, 8), ("g_v", 2, 8), ("g_mem", 8, 8),
                ("g_final", 8, 8), ("loss", 1, 8))
_LOSS_ROW = SMALL_ROWS - 8
assert sum(p for _, _, p in _SMALL_PARTS) == SMALL_ROWS


def _pack_small(parts, loss=None):
    loss_row = jnp.zeros((1, LANES), F32) if loss is None else jnp.broadcast_to(loss.reshape(1, 1), (1, LANES))
    rows = []
    for (name, used, padded), p in zip(_SMALL_PARTS, list(parts) + [loss_row]):
        p = p.reshape(used, LANES)
        if padded > used:
            p = jnp.pad(p, ((0, padded - used), (0, 0)))
        rows.append(p)
    return jnp.concatenate(rows, axis=0)


def _local_step(x, mem, target, g_norm, w_in, w_s, b_s, g_v, g_mem, late_weights, g_final,
                fwd_token=None, on_dw=None):
    B, S, _ = x.shape
    x2d = x.reshape(B * S, D_MODEL)
    t2d = target.reshape(B * S, D_MODEL)
    mem2d = mem.reshape(B * N_MEM, D_MODEL)

    proj = _inproj_fwd(x2d, g_norm, w_in, after=() if fwd_token is None else (fwd_token,))
    w_kv, w_out = late_weights(proj)
    kv = _kv_fwd(mem2d, g_mem, w_kv)
    a, lse = _attn_fwd(proj, B, S)
    w_sT = jnp.swapaxes(w_s, 1, 2)
    b_tab = jnp.repeat(b_s.T, HEAD_DIM, axis=1)
    (dx2, da, drest, loss, d_wout, d_ws, d_bs, d_gv, d_gf, dkv) = _mid(
        x2d, t2d, a, proj, kv, w_s, w_sT, b_tab, g_v, w_out, g_final, B, S)
    d_wkv, d_gmem = _kv_bwd(mem2d, g_mem, w_kv, dkv)
    dq, dk, dv = _attn_bwd(proj, a, lse, da, B, S)
    if on_dw is None:
        d_win = _inproj_bwd_dw(dq, dk, dv, drest, x2d, g_norm)
        after = ()
    else:
        d_win = None
        by_chip = lambda g: g.reshape((N_CHIPS, g.shape[0] // N_CHIPS, g.shape[1]))
        after = (on_dw(*_inproj_bwd_dw(dq, dk, dv, drest, x2d, g_norm, reduce_with=[by_chip(d_wkv), by_chip(d_wout)])),)
    grad_x, d_gnorm = _inproj_bwd_dx(dq, dk, dv, drest, x2d, dx2, g_norm, w_in, after=after)
    d_bs = d_bs[:, :N_SGU_GROUPS].T
    return (loss[0, 0], grad_x.reshape(B, S, D_MODEL),
            dict(g_norm=d_gnorm, w_in=d_win, w_s=d_ws, b_s=d_bs, g_v=d_gv, g_mem=d_gmem, w_kv=d_wkv,
                 w_out=d_wout, g_final=d_gf))


def kernel(x, mem, g_norm, w_in, w_sgu_spatial, b_sgu_spatial, g_sgu_v, g_mem, w_mem_kv, w_out, g_final, loss_target, m_g_norm, m_w_in, m_w_sgu_spatial, m_b_sgu_spatial, m_g_sgu_v, m_g_mem, m_w_mem_kv, m_w_out, m_g_final, v_g_norm, v_w_in, v_w_sgu_spatial, v_b_sgu_spatial, v_g_sgu_v, v_g_mem, v_w_mem_kv, v_w_out, v_g_final):
    t = lambda w: jnp.swapaxes(w[0], 0, 1)
    (win_all,), late_shards, late_lands = _ag_weights([t(w_in)], [w_mem_kv[0], w_out[0]])
    w_in_full = win_all.reshape(-1, win_all.shape[-1])
    late = _exchange_start(True, list(late_shards), win_all, "gather_late_start", lands=late_lands)

    def late_weights(proj):
        return [z.reshape(-1, z.shape[-1]) for z in _exchange_wait(True, late, proj, "gather_late_wait")]

    scatter = {}

    def on_dw(sends, owns):
        scatter["own"] = owns
        scatter["started"] = _exchange_start(False, list(sends), owns[0], "scatter_start")
        return scatter["started"][-1]

    loss, grad_x, g = _local_step(
        x, mem, loss_target, g_norm, w_in_full, w_sgu_spatial[0], b_sgu_spatial[0], g_sgu_v, g_mem,
        late_weights, g_final.reshape(1, D_MODEL), fwd_token=late[-1], on_dw=on_dw)

    small_names = ("g_norm", "w_s", "b_s", "g_v", "g_mem", "g_final")
    g_small = _pack_small([g[n] for n in small_names], loss)
    landed = _exchange_wait(False, scatter["started"], g_small, "scatter_wait")
    gr_in, gr_kv, gr_out, gr_small = _reduce_last(scatter["own"], landed, g_small)
    loss = gr_small[_LOSS_ROW, 0]

    small_w = (g_norm, w_sgu_spatial, b_sgu_spatial, g_sgu_v, g_mem, g_final)
    small_m = (m_g_norm, m_w_sgu_spatial, m_b_sgu_spatial, m_g_sgu_v, m_g_mem, m_g_final)
    small_v = (v_g_norm, v_w_sgu_spatial, v_b_sgu_spatial, v_g_sgu_v, v_g_mem, v_g_final)
    rows = lambda ws: [w.reshape(-1, LANES) for w in ws]
    small = [[z.reshape(w.shape) for z in four]
             for w, four in zip(small_w, _adamw_small(gr_small, rows(small_w), rows(small_m), rows(small_v)))]
    d_in, nm_in, nv_in = _adamw(t(w_in), gr_in, t(m_w_in), t(v_w_in), "adamw_w_in")
    gr_in, d_in, nm_in, nv_in = [jnp.swapaxes(z, 0, 1) for z in (gr_in, d_in, nm_in, nv_in)]
    d_kv, nm_kv, nv_kv = _adamw(w_mem_kv[0], gr_kv, m_w_mem_kv[0], v_w_mem_kv[0], "adamw_w_kv")
    d_out, nm_out, nv_out = _adamw(w_out[0], gr_out, m_w_out[0], v_w_out[0], "adamw_w_out")

    def leaves(kind, big_in, big_kv, big_out):
        s_norm, s_ws, s_bs, s_gv, s_gmem, s_gf = [four[kind] for four in small]
        return [s_norm, big_in[None], s_ws, s_bs, s_gv, s_gmem, big_kv[None], big_out[None], s_gf]

    return (loss, grad_x, *leaves(0, gr_in, gr_kv, gr_out), *leaves(1, d_in, d_kv, d_out),
            *leaves(2, nm_in, nm_kv, nm_out), *leaves(3, nv_in, nv_kv, nv_out))
```

```python
import functools

import jax
import jax.numpy as jnp
from jax import lax
from jax.experimental import pallas as pl
from jax.experimental.pallas import tpu as pltpu

F32 = jnp.float32
BF16 = jnp.bfloat16
MESH = pl.DeviceIdType.MESH

D_MODEL = 1024
ATTN_WIDTH = 512
SGU_WIDTH = 256
MEM_WIDTH = 256
N_MEM = 256
IN_COLS = 3328
QKV_COLS = 3 * ATTN_WIDTH
REST_COLS = IN_COLS - QKV_COLS
SGU_CHUNK = 128
N_SGU_GROUPS = 4
EPS = 1e-6
NEG_INF = -1e30
DILATIONS = (1, 4, 16)
RADIUS = 64
Q_BLOCK = 128
LANES = 128
HEAD_DIM = 64

ADAM_LR = 0.001
ADAM_B1 = 0.9
ADAM_B2 = 0.999
ADAM_EPS = 1e-08
ADAM_WD = 0.01
ADAM_STEP = 10

N_CHIPS = 4
VMEM_LIMIT = 56 * 1024 * 1024
SMALL_ROWS = 560


def _params(sem=None, vmem=VMEM_LIMIT):
    return pltpu.CompilerParams(dimension_semantics=sem, vmem_limit_bytes=vmem)


def _nn(a, b):
    return jnp.dot(a, b, preferred_element_type=F32)


def _nt(a, b):
    return lax.dot_general(a, b, (((1,), (1,)), ((), ())), preferred_element_type=F32)


def _tn(a, b):
    return lax.dot_general(a, b, (((0,), (0,)), ((), ())), preferred_element_type=F32)


def _rms(x):
    r = lax.rsqrt(jnp.mean(x * x, axis=-1, keepdims=True) + EPS)
    return r, x * r


def _head_masks():
    lane = lax.broadcasted_iota(jnp.int32, (1, LANES), 1)
    lo = lane < HEAD_DIM
    return lo, (lo.astype(F32), (~lo).astype(F32))


def _silu_parts(z):
    s = jax.nn.sigmoid(z)
    return z * s, s * (1.0 + z * (1.0 - s))


def _gelu_parts(x):
    c = 0.7978845608028654
    x2 = x * x
    s = jax.nn.sigmoid((2.0 * c) * (x + 0.044715 * (x * x2)))
    return x * s, s * (1.0 + x * (1.0 - s) * ((2.0 * c) * (1.0 + 3.0 * 0.044715 * x2)))


def _after(tokens):
    return [pl.BlockSpec(memory_space=pl.ANY)] * len(tokens)


def _inproj_fwd(x2d, g_norm, w_in_t, after=()):
    T = x2d.shape[0]
    tm = 512

    def body(x_ref, g_ref, w_ref, *rest):
        o_ref = rest[-1]
        _, xh = _rms(x_ref[...])
        h = (xh * g_ref[...]).astype(BF16)
        o_ref[...] = _nt(h, w_ref[...])

    return pl.pallas_call(
        body, grid=(T // tm,),
        in_specs=[pl.BlockSpec((tm, D_MODEL), lambda i: (i, 0)),
                  pl.BlockSpec((1, D_MODEL), lambda i: (0, 0)),
                  pl.BlockSpec((IN_COLS, D_MODEL), lambda i: (0, 0))] + _after(after),
        out_specs=pl.BlockSpec((tm, IN_COLS), lambda i: (i, 0)),
        out_shape=jax.ShapeDtypeStruct((T, IN_COLS), F32),
        compiler_params=_params(("arbitrary",)), name="inproj_fwd")(x2d, g_norm, w_in_t, *after)


def _kv_fwd(mem2d, g_mem, w_kv):
    Tm = mem2d.shape[0]

    def body(m_ref, g_ref, w_ref, o_ref):
        _, mh = _rms(m_ref[...])
        o_ref[...] = _nn((mh * g_ref[...]).astype(BF16), w_ref[...])

    return pl.pallas_call(
        body, out_shape=jax.ShapeDtypeStruct((Tm, 2 * MEM_WIDTH), F32),
        compiler_params=_params(), name="kv_fwd")(mem2d, g_mem, w_kv)


def _kv_bwd(mem2d, g_mem, w_kv, dkv):
    Tm = mem2d.shape[0]

    def body(m_ref, g_ref, w_ref, dkv_ref, dw_ref, dg_ref):
        _, mh = _rms(m_ref[...])
        memn = (mh * g_ref[...]).astype(BF16)
        dkvb = dkv_ref[...].astype(BF16)
        dw_ref[...] = _tn(memn, dkvb)
        dmemn = _nt(dkvb, w_ref[...])
        dg_ref[...] = jnp.sum(dmemn * mh, axis=0, keepdims=True)

    return pl.pallas_call(
        body, out_shape=(jax.ShapeDtypeStruct((D_MODEL, 2 * MEM_WIDTH), F32),
                         jax.ShapeDtypeStruct((1, D_MODEL), F32)),
        compiler_params=_params(), name="kv_bwd")(mem2d, g_mem, w_kv, dkv)


def _attn_geometry(S):
    geom = []
    for d in DILATIONS:
        L = S // d
        assert L % Q_BLOCK == 0
        geom.append((d, L, min(2 * Q_BLOCK, L), L // Q_BLOCK))
    return geom


def _init_bias(bias_scr, geom, hp):
    row = lax.broadcasted_iota(jnp.int32, (Q_BLOCK, 2 * Q_BLOCK), 0)
    col = lax.broadcasted_iota(jnp.int32, (Q_BLOCK, 2 * Q_BLOCK), 1)
    for j in (0, 1):
        bits = (126 - (2 * hp + j)) * (1 << 23)
        slope = lax.bitcast_convert_type(jnp.full((1, 1), bits, jnp.int32), F32)
        for di, (d, _, _, _) in enumerate(geom):
            for cls, off in enumerate((0, -RADIUS, -2 * RADIUS)):
                dist = jnp.abs(col - row + off)
                bias_scr[di * 6 + cls * 2 + j] = jnp.where(
                    dist <= RADIUS, -(slope * float(d)) * dist.astype(F32), NEG_INF)


SPLIT = 4
COPY_ROWS = 256


def _by4_rows(S, step):
    per_class = S // SPLIT // COPY_ROWS
    r, j = step // per_class, step % per_class
    return (pl.ds(r + SPLIT * j * COPY_ROWS, COPY_ROWS, stride=SPLIT),
            pl.ds(pl.multiple_of(r * (S // SPLIT) + j * COPY_ROWS, COPY_ROWS), COPY_ROWS))


def _to_by4(src, dst, S):
    def step(i, carry):
        natural, by4 = _by4_rows(S, i)
        dst[by4, :] = src[natural, :]
        return carry
    lax.fori_loop(0, S // COPY_ROWS, step, 0)


def _block_slices(d, L, KW, nqb, r, qb, S):
    qs = qb * Q_BLOCK
    ks = jnp.clip(qs - RADIUS, 0, L - KW)
    cls = jnp.where(qb == 0, 0, jnp.where(qb == nqb - 1, 2, 1))
    if d == 1:
        qsl = pl.ds(pl.multiple_of(qs, Q_BLOCK), Q_BLOCK)
        ksl = pl.ds(pl.multiple_of(ks, RADIUS), KW)
    elif d == SPLIT:
        qsl = pl.ds(pl.multiple_of(r * L + qs, Q_BLOCK), Q_BLOCK)
        ksl = pl.ds(pl.multiple_of(r * L + ks, RADIUS), KW)
    else:
        sub = d // SPLIT
        base = (r % SPLIT) * (S // SPLIT) + r // SPLIT
        qsl = pl.ds(base + qs * sub, Q_BLOCK, stride=sub)
        ksl = pl.ds(base + ks * sub, KW, stride=sub)
    return qsl, ksl, cls


def _for_groups(geom, S, group, fn):
    for di, (d, L, KW, nqb) in enumerate(geom):
        n = group[di]
        assert (d * nqb) % n == 0

        def step(it, carry, di=di, d=d, L=L, KW=KW, nqb=nqb, n=n):
            slices = []
            for g in range(n):
                i = it * n + g
                slices.append(_block_slices(d, L, KW, nqb, i // nqb, i % nqb, S))
            fn(di, KW, slices)
            return carry
        lax.fori_loop(0, d * nqb // n, step, 0)


def _attn_fwd(proj, B, S):
    T = B * S
    geom = _attn_geometry(S)
    n_pairs = ATTN_WIDTH // LANES

    def body(q_ref, k_ref, v_ref, a_ref, lse_ref, bias_scr, q4, k4, v4, *per_dilation):
        o_scr, m_scr, l_scr = per_dilation[0:3], per_dilation[3:6], per_dilation[6:9]
        lo, hm = _head_masks()
        pair = pl.program_id(0)

        @pl.when(pl.program_id(1) == 0)
        def _():
            _init_bias(bias_scr, geom, pair)
        for src, dst in ((q_ref, q4), (k_ref, k4), (v_ref, v4)):
            _to_by4(src, dst, S)

        def group(di, KW, slices):
            chains = [(g, j) for g in range(len(slices)) for j in (0, 1)]
            q_src, k_src, v_src = (q_ref, k_ref, v_ref) if di == 0 else (q4, k4, v4)
            q = [q_src[qsl, :] for qsl, _, _ in slices]
            kw = [k_src[ksl, :].astype(BF16) for _, ksl, _ in slices]
            vw = [v_src[ksl, :].astype(BF16) for _, ksl, _ in slices]
            s = {(g, j): _nt((q[g] * (hm[j] * 0.125)).astype(BF16), kw[g])
                 + bias_scr[di * 6 + slices[g][2] * 2 + j, :, pl.ds(0, KW)] for g, j in chains}
            m = {c: jnp.max(s[c], axis=1, keepdims=True) for c in chains}
            p = {c: jnp.exp(s[c] - m[c]) for c in chains}
            l = {c: jnp.sum(p[c], axis=1, keepdims=True) for c in chains}
            o = {(g, j): _nn(p[(g, j)].astype(BF16), vw[g]) for g, j in chains}
            for g, (qsl, _, _) in enumerate(slices):
                o_scr[di][qsl, :] = jnp.where(lo, o[(g, 0)], o[(g, 1)])
                m_scr[di][qsl, :] = jnp.where(lo, m[(g, 0)], m[(g, 1)])
                l_scr[di][qsl, :] = jnp.where(lo, l[(g, 0)], l[(g, 1)])

        _for_groups(geom, S, (16, 16, 16), group)

        def combine(i, carry):
            natural, by4 = _by4_rows(S, i)
            rows = [natural, by4, by4]
            ms = [m_scr[di][rows[di], :] for di in range(3)]
            mx = jnp.maximum(jnp.maximum(ms[0], ms[1]), ms[2])
            num = 0.0
            den = 0.0
            for di in range(3):
                w = jnp.exp(ms[di] - mx)
                num = num + w * o_scr[di][rows[di], :]
                den = den + w * l_scr[di][rows[di], :]
            a_ref[natural, :] = num / den
            lse_ref[natural, :] = mx + jnp.log(den)
            return carry

        lax.fori_loop(0, S // COPY_ROWS, combine, 0)

    blk = lambda off: pl.BlockSpec((S, LANES), lambda h, b, off=off: (b, off + h))
    out_blk = pl.BlockSpec((S, LANES), lambda h, b: (b, h))
    return pl.pallas_call(
        body, grid=(n_pairs, B),
        in_specs=[blk(0), blk(n_pairs), blk(2 * n_pairs)],
        out_specs=[out_blk, out_blk],
        out_shape=[jax.ShapeDtypeStruct((T, ATTN_WIDTH), F32)] * 2,
        scratch_shapes=[pltpu.VMEM((18, Q_BLOCK, 2 * Q_BLOCK), F32)] + [pltpu.VMEM((S, LANES), F32)] * 12,
        compiler_params=_params(("arbitrary", "arbitrary")), name="attn_fwd")(proj, proj, proj)


def _attn_bwd(proj, a, lse, da, B, S):
    T = B * S
    geom = _attn_geometry(S)
    n_pairs = ATTN_WIDTH // LANES

    def body(q_ref, k_ref, v_ref, a_ref, lse_ref, do_ref, dq_ref, dk_ref, dv_ref, bias_scr, *scr):
        acc = (scr[0:3], scr[3:6])
        natural_in = (q_ref, k_ref, v_ref, a_ref, lse_ref, do_ref)
        by4_in = scr[6:12]
        _, hm = _head_masks()
        pair = pl.program_id(0)

        @pl.when(pl.program_id(1) == 0)
        def _():
            _init_bias(bias_scr, geom, pair)
        for ref in scr[0:6]:
            ref[...] = jnp.zeros_like(ref)
        for src, dst in zip(natural_in, by4_in):
            _to_by4(src, dst, S)

        def group(di, KW, slices):
            n = len(slices)
            chains = [(g, j) for g in range(n) for j in (0, 1)]
            q_src, k_src, v_src, a_src, lse_src, do_src = natural_in if di == 0 else by4_in
            dq_scr, dk_scr, dv_scr = acc[0 if di == 0 else 1]
            q = [q_src[qsl, :] for qsl, _, _ in slices]
            do = [do_src[qsl, :] for qsl, _, _ in slices]
            doa = [do[g] * a_src[slices[g][0], :] for g in range(n)]
            lse_q = [lse_src[qsl, :] for qsl, _, _ in slices]
            kw = [k_src[ksl, :].astype(BF16) for _, ksl, _ in slices]
            vw = [v_src[ksl, :].astype(BF16) for _, ksl, _ in slices]
            qj = {(g, j): (q[g] * (hm[j] * 0.125)).astype(BF16) for g, j in chains}
            doj = {(g, j): (do[g] * hm[j]).astype(BF16) for g, j in chains}
            s = {(g, j): _nt(qj[(g, j)], kw[g])
                 + bias_scr[di * 6 + slices[g][2] * 2 + j, :, pl.ds(0, KW)] for g, j in chains}
            dp = {(g, j): _nt(doj[(g, j)], vw[g]) for g, j in chains}
            dsum = {(g, j): jnp.sum(doa[g] * hm[j], axis=1, keepdims=True) for g, j in chains}
            p = {(g, j): jnp.exp(s[(g, j)] - lse_q[g][:, HEAD_DIM * j:HEAD_DIM * j + 1]) for g, j in chains}
            ds = {c: (p[c] * (dp[c] - dsum[c])).astype(BF16) for c in chains}
            pb = {c: p[c].astype(BF16) for c in chains}
            dq = [_nn(ds[(g, 0)], kw[g]) * (hm[0] * 0.125) + _nn(ds[(g, 1)], kw[g]) * (hm[1] * 0.125)
                  for g in range(n)]
            both = lambda t, g: jnp.concatenate([t[(g, 0)], t[(g, 1)]], axis=0)
            dkw = [_tn(both(ds, g), both(qj, g)) for g in range(n)]
            dvw = [_tn(both(pb, g), both(doj, g)) for g in range(n)]
            for g, (qsl, ksl, _) in enumerate(slices):
                dq_scr[qsl, :] = dq_scr[qsl, :] + dq[g]
                dk_scr[ksl, :] = dk_scr[ksl, :] + dkw[g]
                dv_scr[ksl, :] = dv_scr[ksl, :] + dvw[g]

        _for_groups(geom, S, (4, 4, 16), group)

        def merge(i, carry):
            natural, by4 = _by4_rows(S, i)
            for nat, split in zip(*acc):
                nat[natural, :] = nat[natural, :] + split[by4, :]
            return carry
        lax.fori_loop(0, S // COPY_ROWS, merge, 0)
        for out, nat in zip((dq_ref, dk_ref, dv_ref), acc[0]):
            out[...] = nat[...].astype(BF16)

    blk = lambda off: pl.BlockSpec((S, LANES), lambda h, b, off=off: (b, off + h))
    return pl.pallas_call(
        body, grid=(n_pairs, B),
        in_specs=[blk(0), blk(n_pairs), blk(2 * n_pairs), blk(0), blk(0), blk(0)],
        out_specs=[blk(0), blk(0), blk(0)],
        out_shape=[jax.ShapeDtypeStruct((T, ATTN_WIDTH), BF16)] * 3,
        scratch_shapes=[pltpu.VMEM((18, Q_BLOCK, 2 * Q_BLOCK), F32)] + [pltpu.VMEM((S, LANES), F32)] * 12,
        compiler_params=_params(("arbitrary", "arbitrary")), name="attn_bwd")(proj, proj, proj, a, lse, da)


def _mid(x2d, t2d, a, proj, kv, w_s, w_sT, b_tab, g_v, w_out, g_final, B, S):
    T = B * S
    tm = 512
    nt = S // tm
    halves = 2
    hrows = tm // halves

    def body(x_ref, t_ref, a_ref, za_ref, ub_ref, vb_ref, zb_ref, qm_ref, zm_ref, kv_ref,
              ws_ref, wsT_ref, btab_ref, gv_ref, wout_ref, gf_ref,
              dx2_ref, da_ref, drest_ref, loss_ref, dwout_ref, dws_ref, dbs_ref, dgv_ref, dgf_ref, dkv_ref,
              dbtab_scr):
        b = pl.program_id(0)
        t = pl.program_id(1)
        first = jnp.logical_and(b == 0, t == 0)
        last = jnp.logical_and(b == B - 1, t == nt - 1)
        _, hm = _head_masks()
        lane_g = lax.broadcasted_iota(jnp.int32, (1, SGU_WIDTH), 1) // HEAD_DIM
        gm = [(lane_g == g).astype(F32) for g in range(N_SGU_GROUPS)]
        H = range(halves)
        rows = [pl.ds(h * hrows, hrows) for h in H]
        ld = lambda ref: [ref[r, :] for r in rows]
        cat = lambda parts, axis: jnp.concatenate(parts, axis=axis)
        chunks = [slice(ci * SGU_CHUNK, (ci + 1) * SGU_CHUNK) for ci in range(hrows // SGU_CHUNK)]
        pairs = [slice(pr * LANES, (pr + 1) * LANES) for pr in range(2)]
        heads = [(pr, j) for pr in range(2) for j in (0, 1)]

        @pl.when(first)
        def _():
            loss_ref[...] = jnp.zeros_like(loss_ref)
            dwout_ref[...] = jnp.zeros_like(dwout_ref)
            dws_ref[...] = jnp.zeros_like(dws_ref)
            dbs_ref[...] = jnp.zeros_like(dbs_ref)
            dgv_ref[...] = jnp.zeros_like(dgv_ref)
            dgf_ref[...] = jnp.zeros_like(dgf_ref)
            dbtab_scr[...] = jnp.zeros_like(dbtab_scr)

        @pl.when(t == 0)
        def _():
            dkv_ref[...] = jnp.zeros_like(dkv_ref)

        a_val = ld(a_ref)
        sil_a = [_silu_parts(z) for z in ld(za_ref)]
        gated_a = [s[0] * a for s, a in zip(sil_a, a_val)]
        u = [_gelu_parts(z) for z in ld(ub_ref)]
        vv = [_gelu_parts(z) for z in ld(vb_ref)]
        vnorm = [_rms(v[0]) for v in vv]
        gv = gv_ref[...]
        vn = [(n[1] * gv).astype(BF16) for n in vnorm]
        w_cat = cat([ws_ref[g].astype(BF16) for g in range(N_SGU_GROUPS)], 1)
        wT_cat = cat([wsT_ref[g].astype(BF16) for g in range(N_SGU_GROUPS)], 1)
        gmb = [m.astype(BF16) for m in gm]
        by_group = lambda chunk: cat([chunk * gmb[g] for g in range(N_SGU_GROUPS)], 0)
        btab = btab_ref[...]
        mixed = [cat([btab + _nn(w_cat, by_group(vn[h][c, :])) for c in chunks], 0) for h in H]
        sg = [u[h][0] * mixed[h] for h in H]
        sil_b = [_silu_parts(z) for z in ld(zb_ref)]
        gated_b = [sil_b[h][0] * sg[h] for h in H]

        kvv = kv_ref[...].astype(BF16)
        kp = [kvv[:, p] for p in pairs]
        vp = [kvv[:, MEM_WIDTH + pr * LANES:MEM_WIDTH + (pr + 1) * LANES] for pr in range(2)]
        qm = ld(qm_ref)
        qj = {(h, pr, j): (qm[h][:, pairs[pr]] * (hm[j] * 0.125)).astype(BF16) for h in H for pr, j in heads}
        sc = {k: _nt(qj[k], kp[k[1]]) for k in qj}
        ex = {k: jnp.exp(sc[k] - jnp.max(sc[k], axis=1, keepdims=True)) for k in qj}
        prob = {k: ex[k] * (1.0 / jnp.sum(ex[k], axis=1, keepdims=True)) for k in qj}
        probb = {k: prob[k].astype(BF16) for k in qj}
        mo = [cat([sum(_nn(probb[(h, pr, j)], vp[pr]) * hm[j] for j in (0, 1)) for pr in range(2)], 1) for h in H]
        sil_m = [_silu_parts(z) for z in ld(zm_ref)]
        gated_m = [sil_m[h][0] * mo[h] for h in H]

        gated = [cat([gated_a[h], gated_b[h], gated_m[h]], 1).astype(BF16) for h in H]
        wout = wout_ref[...]
        x_in = ld(x_ref)
        x2 = [x_in[h] + _nn(gated[h], wout) for h in H]
        fin = [_rms(z) for z in x2]
        gf = gf_ref[...]
        tgt = ld(t_ref)
        err = [fin[h][1] * gf - tgt[h] for h in H]
        loss_ref[...] += sum(jnp.sum(e * e) for e in err) * (0.5 / D_MODEL)

        dy = [e * (1.0 / D_MODEL) for e in err]
        dgf_ref[...] += sum(jnp.sum(dy[h] * fin[h][1], axis=0, keepdims=True) for h in H)
        gdy = [d * gf for d in dy]
        dx2 = [fin[h][0] * (gdy[h] - fin[h][1] * jnp.mean(gdy[h] * fin[h][1], axis=1, keepdims=True)) for h in H]
        for h in H:
            dx2_ref[rows[h], :] = dx2[h]
        dx2b = [d.astype(BF16) for d in dx2]
        dgated = [_nt(d, wout) for d in dx2b]
        dwout_ref[...] += _tn(cat(gated, 0), cat(dx2b, 0))
        dga = [d[:, 0:ATTN_WIDTH] for d in dgated]
        dgb = [d[:, ATTN_WIDTH:ATTN_WIDTH + SGU_WIDTH] for d in dgated]
        dgm = [d[:, ATTN_WIDTH + SGU_WIDTH:] for d in dgated]

        for h in H:
            da_ref[rows[h], :] = dga[h] * sil_a[h][0]
        dza = [dga[h] * a_val[h] * sil_a[h][1] for h in H]

        dsg = [dgb[h] * sil_b[h][0] for h in H]
        dzb = [dgb[h] * sg[h] * sil_b[h][1] for h in H]
        dub = [dsg[h] * mixed[h] * u[h][1] for h in H]
        dmixed = [dsg[h] * u[h][0] for h in H]
        dmixed_b = [d.astype(BF16) for d in dmixed]
        dvn = [cat([_nn(wT_cat, by_group(dmixed_b[h][c, :])) for c in chunks], 0) for h in H]
        for g in range(N_SGU_GROUPS):
            dws_ref[g] += sum(_nt((dmixed[h][c, :] * gm[g]).astype(BF16), vn[h][c, :]) for h in H for c in chunks)
        dbtab_scr[...] += sum(dmixed[h][c, :] for h in H for c in chunks)
        dgv_ref[...] += sum(jnp.sum(dvn[h] * vnorm[h][1], axis=0, keepdims=True) for h in H)
        tv = [d * gv for d in dvn]
        dvv = [vnorm[h][0] * (tv[h] - vnorm[h][1] * jnp.mean(tv[h] * vnorm[h][1], axis=1, keepdims=True)) for h in H]
        dvb = [dvv[h] * vv[h][1] for h in H]

        dmo = [dgm[h] * sil_m[h][0] for h in H]
        dzm = [dgm[h] * mo[h] * sil_m[h][1] for h in H]
        dmoj = {(h, pr, j): (dmo[h][:, pairs[pr]] * hm[j]).astype(BF16) for h in H for pr, j in heads}
        dp = {k: _nt(dmoj[k], vp[k[1]]) for k in qj}
        ds = {k: (prob[k] * (dp[k] - jnp.sum(dp[k] * prob[k], axis=1, keepdims=True))).astype(BF16) for k in qj}
        dqm = [cat([sum(_nn(ds[(h, pr, j)], kp[pr]) * (hm[j] * 0.125) for j in (0, 1)) for pr in range(2)], 1)
               for h in H]
        every = lambda tbl, pr: cat([tbl[(h, pr, j)] for h in H for j in (0, 1)], 0)
        dk = [_tn(every(ds, pr), every(qj, pr)) for pr in range(2)]
        dv = [_tn(every(probb, pr), every(dmoj, pr)) for pr in range(2)]
        dkv_ref[...] += cat(dk + dv, 1)

        for h in H:
            drest_ref[rows[h], :] = cat([dza[h], dub[h], dvb[h], dzb[h], dqm[h], dzm[h]], 1).astype(BF16)

        @pl.when(last)
        def _():
            lane = lax.broadcasted_iota(jnp.int32, (1, LANES), 1)
            dbt = dbtab_scr[...]
            out = jnp.zeros((SGU_CHUNK, LANES), F32)
            for g in range(N_SGU_GROUPS):
                out = out + jnp.where(lane == g, jnp.sum(dbt * gm[g], axis=1, keepdims=True), 0.0)
            dbs_ref[...] = out

    tile = lambda w, cb: pl.BlockSpec((tm, w), lambda b, t, cb=cb: (b * nt + t, cb))
    const = lambda shape: pl.BlockSpec(shape, lambda b, t, n=len(shape): (0,) * n)
    return pl.pallas_call(
        body, grid=(B, nt),
        in_specs=[tile(D_MODEL, 0), tile(D_MODEL, 0), tile(ATTN_WIDTH, 0),
                  tile(ATTN_WIDTH, 3),
                  tile(SGU_WIDTH, 8), tile(SGU_WIDTH, 9), tile(SGU_WIDTH, 10),
                  tile(MEM_WIDTH, 11), tile(MEM_WIDTH, 12),
                  pl.BlockSpec((N_MEM, 2 * MEM_WIDTH), lambda b, t: (b, 0)),
                  const((N_SGU_GROUPS, SGU_CHUNK, SGU_CHUNK)), const((N_SGU_GROUPS, SGU_CHUNK, SGU_CHUNK)),
                  const((SGU_CHUNK, SGU_WIDTH)), const((1, SGU_WIDTH)),
                  const((D_MODEL, D_MODEL)), const((1, D_MODEL))],
        out_specs=[tile(D_MODEL, 0), tile(ATTN_WIDTH, 0), tile(REST_COLS, 0),
                   const((8, LANES)), const((D_MODEL, D_MODEL)),
                   const((N_SGU_GROUPS, SGU_CHUNK, SGU_CHUNK)), const((SGU_CHUNK, LANES)),
                   const((1, SGU_WIDTH)), const((1, D_MODEL)),
                   pl.BlockSpec((N_MEM, 2 * MEM_WIDTH), lambda b, t: (b, 0))],
        out_shape=[jax.ShapeDtypeStruct((T, D_MODEL), F32), jax.ShapeDtypeStruct((T, ATTN_WIDTH), F32),
                   jax.ShapeDtypeStruct((T, REST_COLS), BF16),
                   jax.ShapeDtypeStruct((8, LANES), F32), jax.ShapeDtypeStruct((D_MODEL, D_MODEL), F32),
                   jax.ShapeDtypeStruct((N_SGU_GROUPS, SGU_CHUNK, SGU_CHUNK), F32),
                   jax.ShapeDtypeStruct((SGU_CHUNK, LANES), F32),
                   jax.ShapeDtypeStruct((1, SGU_WIDTH), F32), jax.ShapeDtypeStruct((1, D_MODEL), F32),
                   jax.ShapeDtypeStruct((B * N_MEM, 2 * MEM_WIDTH), F32)],
        scratch_shapes=[pltpu.VMEM((SGU_CHUNK, SGU_WIDTH), F32)],
        compiler_params=_params(("arbitrary", "arbitrary")), name="mid")(
            x2d, t2d, a, proj, proj, proj, proj, proj, proj, kv, w_s, w_sT, b_tab, g_v, w_out, g_final)


def _inproj_bwd_dx(dq, dk, dv, drest, x2d, dx2, g_norm, w_in_t, after=()):
    T = x2d.shape[0]
    tm = 512
    W = ATTN_WIDTH

    def body(dq_ref, dk_ref, dv_ref, dr_ref, x_ref, dx2_ref, g_ref, w_ref, *rest):
        gx_ref, dg_ref = rest[-2:]

        @pl.when(pl.program_id(0) == 0)
        def _():
            dg_ref[...] = jnp.zeros_like(dg_ref)

        halves = [pl.ds(h * (tm // 2), tm // 2) for h in (0, 1)]
        dh = [(_nn(dq_ref[r, :], w_ref[0:W, :]) + _nn(dk_ref[r, :], w_ref[W:2 * W, :])
               + _nn(dv_ref[r, :], w_ref[2 * W:3 * W, :]) + _nn(dr_ref[r, :], w_ref[QKV_COLS:IN_COLS, :]))
              for r in halves]
        nrm = [_rms(x_ref[r, :]) for r in halves]
        dg_ref[...] += sum(jnp.sum(d * n[1], axis=0, keepdims=True) for d, n in zip(dh, nrm))
        g = g_ref[...]
        for r, d, (rstd, xh) in zip(halves, dh, nrm):
            th = d * g
            gx_ref[r, :] = rstd * (th - xh * jnp.mean(th * xh, axis=1, keepdims=True)) + dx2_ref[r, :]

    tile = lambda w: pl.BlockSpec((tm, w), lambda i: (i, 0))
    return pl.pallas_call(
        body, grid=(T // tm,),
        in_specs=[tile(W), tile(W), tile(W), tile(REST_COLS), tile(D_MODEL), tile(D_MODEL),
                  pl.BlockSpec((1, D_MODEL), lambda i: (0, 0)),
                  pl.BlockSpec((IN_COLS, D_MODEL), lambda i: (0, 0))] + _after(after),
        out_specs=[tile(D_MODEL), pl.BlockSpec((1, D_MODEL), lambda i: (0, 0))],
        out_shape=[jax.ShapeDtypeStruct((T, D_MODEL), F32), jax.ShapeDtypeStruct((1, D_MODEL), F32)],
        compiler_params=_params(("arbitrary",)), name="inproj_bwd_dx")(
            dq, dk, dv, drest, x2d, dx2, g_norm, w_in_t, *after)


def _inproj_bwd_dw(dq, dk, dv, drest, x2d, g_norm, reduce_with=None):
    T = x2d.shape[0]
    tm = 512
    nt = T // tm
    W = ATTN_WIDTH
    fused = reduce_with is not None
    others = list(reduce_with) if fused else []
    ns = 1 + len(others)
    shard = IN_COLS // N_CHIPS
    halves = [shard // 2] + [s.shape[1] // 2 for s in others]
    cols = [D_MODEL] + [s.shape[2] for s in others]
    row_block = 32

    def body(dq_ref, dk_ref, dv_ref, dr_ref, x_ref, g_ref, *rest):
        if fused:
            stacks = rest[:ns - 1]
            sends, owns = rest[ns - 1:2 * ns - 1], rest[2 * ns - 1:3 * ns - 1]
            acc, ras, narrow = rest[3 * ns - 1], rest[3 * ns:4 * ns], rest[4 * ns]
            s_sem, r_sem = rest[4 * ns + 1], rest[4 * ns + 2]
            x, y, c, chip, peers, peer_chip = _place()
            sib = (x, y, 1 - c)

            def part(w, k, cc, r0=0, rows=None):
                n = halves[w]
                rows = n if rows is None else rows
                if w == 0:
                    return acc.at[pl.ds(pl.multiple_of(k * shard + cc * n + r0, 8), rows), :]
                return stacks[w - 1].at[k, pl.ds(pl.multiple_of(cc * n + r0, 8), rows), :]

            def swap_other(w):
                theirs = stacks[w - 1].at[:, pl.ds(pl.multiple_of((1 - c) * halves[w], 8), halves[w]), :]
                return _remote(theirs, ras[w], s_sem.at[N_CHIPS - 1 + w], r_sem.at[N_CHIPS - 1 + w], sib)

            def swap_win(k):
                return _remote(narrow.at[k], ras[0].at[k], s_sem.at[k], r_sem.at[k], sib)
        else:
            acc = rest[0]

        @pl.when(pl.program_id(0) == 0)
        def _():
            acc[...] = jnp.zeros_like(acc)
            for w in range(1, ns):
                swap_other(w).start()

        _, xh = _rms(x_ref[...])
        h = (xh * g_ref[...]).astype(BF16)
        acc[0:W, :] += _tn(dq_ref[...], h)
        acc[W:2 * W, :] += _tn(dk_ref[...], h)
        acc[2 * W:3 * W, :] += _tn(dv_ref[...], h)
        acc[QKV_COLS:IN_COLS, :] += _tn(dr_ref[...], h)

        if fused:
            @pl.when(pl.program_id(0) == nt - 1)
            def _():
                for k in range(N_CHIPS):
                    def to_bf16(i, carry, k=k):
                        r0 = pl.multiple_of(i * row_block, row_block)
                        narrow[k, pl.ds(r0, row_block), :] = part(0, k, 1 - c, r0, row_block)[...].astype(BF16)
                        return carry
                    lax.fori_loop(0, halves[0] // row_block, to_bf16, 0)
                    swap_win(k).start()
                for w in list(range(1, ns)) + [0]:
                    if w == 0:
                        for k in range(N_CHIPS):
                            swap_win(k).wait_recv()
                    else:
                        swap_other(w).wait_recv()

                    def sums(i, carry, w=w):
                        r0 = pl.multiple_of(i * row_block, row_block)
                        blk = pl.ds(r0, row_block)
                        for m in range(3):
                            k = peer_chip[m]
                            sends[w][m, blk, :] = (part(w, k, c, r0, row_block)[...]
                                                   + ras[w][k, blk, :].astype(F32)).astype(BF16)
                        owns[w][blk, :] = part(w, chip, c, r0, row_block)[...] + ras[w][chip, blk, :].astype(F32)
                        return carry
                    lax.fori_loop(0, halves[w] // row_block, sums, 0)
                for k in range(N_CHIPS):
                    swap_win(k).wait_send()
                for w in range(1, ns):
                    swap_other(w).wait_send()

    tile = lambda w: pl.BlockSpec((tm, w), lambda i: (i, 0))
    vmem = pl.BlockSpec(memory_space=pltpu.VMEM)
    in_specs = [tile(W), tile(W), tile(W), tile(REST_COLS), tile(D_MODEL), pl.BlockSpec((1, D_MODEL), lambda i: (0, 0))]
    if not fused:
        return pl.pallas_call(
            body, grid=(nt,), in_specs=in_specs,
            out_specs=pl.BlockSpec((IN_COLS, D_MODEL), lambda i: (0, 0)),
            out_shape=jax.ShapeDtypeStruct((IN_COLS, D_MODEL), F32),
            compiler_params=_params(("arbitrary",)), name="inproj_bwd_dw")(dq, dk, dv, drest, x2d, g_norm)
    outs = pl.pallas_call(
        body, grid=(nt,), in_specs=in_specs + [vmem] * (ns - 1), out_specs=[vmem] * (2 * ns),
        out_shape=[jax.ShapeDtypeStruct((3, n, cl), BF16) for n, cl in zip(halves, cols)]
        + [jax.ShapeDtypeStruct((n, cl), F32) for n, cl in zip(halves, cols)],
        scratch_shapes=[pltpu.VMEM((IN_COLS, D_MODEL), F32)]
        + [pltpu.VMEM((N_CHIPS, n, cl), BF16 if w == 0 else F32) for w, (n, cl) in enumerate(zip(halves, cols))]
        + [pltpu.VMEM((N_CHIPS, halves[0], D_MODEL), BF16)]
        + [pltpu.SemaphoreType.DMA((N_CHIPS - 1 + ns,)), pltpu.SemaphoreType.DMA((N_CHIPS - 1 + ns,))],
        compiler_params=_params(("arbitrary",)), name="inproj_bwd_dw_reduce")(
            dq, dk, dv, drest, x2d, g_norm, *others)
    return outs[:ns], outs[ns:]


def _adamw_update(w, g, m, v):
    nm = ADAM_B1 * m + (1.0 - ADAM_B1) * g
    nv = ADAM_B2 * v + (1.0 - ADAM_B2) * (g * g)
    m_hat = nm / (1.0 - ADAM_B1 ** ADAM_STEP)
    v_hat = nv / (1.0 - ADAM_B2 ** ADAM_STEP)
    return -ADAM_LR * (m_hat / (jnp.sqrt(v_hat) + ADAM_EPS) + ADAM_WD * w), nm, nv


def _adamw(w, g, m, v, name):
    R, C = w.shape
    br = max(r for r in range(8, 257, 8) if R % r == 0)

    def body(w_ref, g_ref, m_ref, v_ref, d_ref, nm_ref, nv_ref):
        d_ref[...], nm_ref[...], nv_ref[...] = _adamw_update(w_ref[...], g_ref[...], m_ref[...], v_ref[...])

    spec = pl.BlockSpec((br, C), lambda i: (i, 0))
    return pl.pallas_call(
        body, grid=(R // br,), in_specs=[spec] * 4, out_specs=[spec] * 3,
        out_shape=[jax.ShapeDtypeStruct((R, C), F32)] * 3,
        compiler_params=_params(("arbitrary",)), name=name)(w, g, m, v)


def _adamw_small(g_packed, ws, ms, vs):
    n = len(ws)

    def body(*refs):
        g_ref = refs[0]
        w_refs, m_refs, v_refs = refs[1:1 + n], refs[1 + n:1 + 2 * n], refs[1 + 2 * n:1 + 3 * n]
        outs = refs[1 + 3 * n:]
        off = 0
        for i, (_, used, padded) in enumerate(_SMALL_PARTS[:n]):
            g = g_ref[off:off + used, :]
            delta, nm, nv = _adamw_update(w_refs[i][...], g, m_refs[i][...], v_refs[i][...])
            outs[4 * i][...], outs[4 * i + 1][...], outs[4 * i + 2][...], outs[4 * i + 3][...] = g, delta, nm, nv
            off += padded

    outs = pl.pallas_call(
        body, out_shape=[jax.ShapeDtypeStruct(w.shape, F32) for w in ws for _ in range(4)],
        compiler_params=_params(), name="adamw_small")(g_packed, *ws, *ms, *vs)
    return [outs[4 * i:4 * i + 4] for i in range(n)]


def _place():
    x, y, c = lax.axis_index("x"), lax.axis_index("y"), lax.axis_index("c")
    chip = 2 * x + y
    peers = [(x, 1 - y), (1 - x, y), (1 - x, 1 - y)]
    peer_chip = [2 * px + py for px, py in peers]
    return x, y, c, chip, peers, peer_chip


def _remote(src, dst, send_sem, recv_sem, dev):
    return pltpu.make_async_remote_copy(src_ref=src, dst_ref=dst, send_sem=send_sem, recv_sem=recv_sem,
                                        device_id=dev, device_id_type=MESH)


def _ag_weights(weights, late=()):
    nw, nl = len(weights), len(late)

    def body(*refs):
        srcs, late_srcs = refs[:nw], refs[nw:nw + nl]
        outs, late_bf, late_land = (refs[nw + nl:2 * nw + nl], refs[2 * nw + nl:2 * nw + 2 * nl],
                                    refs[2 * nw + 2 * nl:2 * nw + 3 * nl])
        s_ici, r_ici, s_d2d, r_d2d = refs[2 * nw + 3 * nl:]
        x, y, c = lax.axis_index("x"), lax.axis_index("y"), lax.axis_index("c")
        chip = 2 * x + y
        sib = (x, y, 1 - c)
        first = ((x + 1 - c) % 2, (y + c) % 2)
        second = ((x + c) % 2, (y + 1 - c) % 2)
        first_chip, second_chip = 2 * first[0] + first[1], 2 * second[0] + second[1]
        diag_chip = 3 - chip
        for src, out in zip(srcs, outs):
            out[chip] = src[...].astype(BF16)

        def half(out, k, cc):
            rows = out.shape[1] // 2
            return out.at[k, pl.ds(pl.multiple_of(cc * rows, 16), rows), :]

        def ici(w, slot, out, k, dev):
            blk = half(out, k, c)
            return _remote(blk, blk, s_ici.at[nw * slot + w], r_ici.at[nw * slot + w], (dev[0], dev[1], c))

        def d2d(w, slot, out, k, cc):
            blk = half(out, k, cc)
            return _remote(blk, blk, s_d2d.at[nw * slot + w], r_d2d.at[nw * slot + w], sib)

        sent = []
        for slot, dev in enumerate((first, second)):
            for w, out in enumerate(outs):
                sent.append(ici(w, slot, out, chip, dev))
                sent[-1].start()
        for src, bf, land in zip(late_srcs, late_bf, late_land):
            bf[...] = src[...].astype(BF16)
            land[...] = jnp.zeros_like(land)
            land[chip] = bf[...]
        for slot, k, dev in ((0, first_chip, first), (1, second_chip, second), (2, diag_chip, second)):
            for w, out in enumerate(outs):
                ici(w, slot, out, k, dev).wait_recv()
                if slot == 0:
                    sent.append(ici(w, 2, out, k, second))
                    sent[-1].start()
                sent.append(d2d(w, slot, out, k, c))
                sent[-1].start()
        for slot, k in ((0, second_chip), (1, first_chip), (2, diag_chip)):
            for w, out in enumerate(outs):
                d2d(w, slot, out, k, 1 - c).wait_recv()
        for cp in sent:
            cp.wait_send()

    vmem = pl.BlockSpec(memory_space=pltpu.VMEM)
    outs = pl.pallas_call(
        body,
        out_shape=[jax.ShapeDtypeStruct((N_CHIPS,) + w.shape, BF16) for w in weights]
        + [jax.ShapeDtypeStruct(w.shape, BF16) for w in late]
        + [jax.ShapeDtypeStruct((N_CHIPS,) + w.shape, BF16) for w in late],
        in_specs=[vmem] * (nw + nl), out_specs=[vmem] * (nw + 2 * nl),
        scratch_shapes=[pltpu.SemaphoreType.DMA((3 * nw,))] * 4,
        compiler_params=pltpu.CompilerParams(vmem_limit_bytes=VMEM_LIMIT), name="ag_weights")(*weights, *late)
    return outs[:nw], outs[nw:nw + nl], outs[nw + nl:]


_HBM = pl.BlockSpec(memory_space=pltpu.HBM)
_SEM = pl.BlockSpec(memory_space=pltpu.SEMAPHORE)
_ANY = pl.BlockSpec(memory_space=pl.ANY)
_DATAFLOW = pltpu.SideEffectType.DATAFLOW_SIDE_EFFECTING


def _in_hbm(a):
    return pltpu.with_memory_space_constraint(a, pltpu.HBM)


def _exchange_copies(gather, srcs, lands, send_sems, recv_sems):
    nw = len(srcs)
    x, y, c, chip, peers, peer_chip = _place()
    pairs = []
    for m, (px, py) in enumerate(peers):
        for w in range(nw):
            sems = (send_sems.at[nw * m + w], recv_sems.at[nw * m + w], (px, py, c))
            if gather:
                pairs.append((_remote(srcs[w], lands[w].at[chip], *sems),
                              _remote(srcs[w], lands[w].at[peer_chip[m]], *sems)))
            else:
                pairs.append((_remote(srcs[w].at[m], lands[w].at[m], *sems),) * 2)
    return pairs


def _exchange_start(gather, srcs, after, name, lands=None):
    nw = len(srcs)
    n_copies = 3 * nw

    def body(*refs):
        send_sems, recv_sems = refs[2 * nw + 1], refs[2 * nw + 2]
        for start, _ in _exchange_copies(gather, refs[:nw], refs[nw:2 * nw], send_sems, recv_sems):
            start.start()
        refs[-1][...] = jnp.zeros_like(refs[-1])

    if lands is None:
        lands = [lax.empty(((N_CHIPS,) + s.shape) if gather else s.shape, s.dtype) for s in srcs]
    lands = [_in_hbm(l) for l in lands]
    return pl.pallas_call(
        body, name=name,
        out_shape=(pltpu.SemaphoreType.DMA((n_copies,)), pltpu.SemaphoreType.DMA((n_copies,)))
        + tuple(pltpu.HBM(s.shape, s.dtype) for s in srcs)
        + tuple(pltpu.HBM(l.shape, l.dtype) for l in lands)
        + (jax.ShapeDtypeStruct((8, LANES), F32),),
        in_specs=[_HBM] * (2 * nw) + [_ANY],
        out_specs=(_SEM, _SEM) + (_HBM,) * (2 * nw) + (pl.BlockSpec(memory_space=pltpu.VMEM),),
        input_output_aliases={i: 2 + i for i in range(2 * nw)},
        compiler_params=pltpu.CompilerParams(has_side_effects=_DATAFLOW),
    )(*[_in_hbm(s) for s in srcs], *lands, after)


def _exchange_wait(gather, started, after, name):
    nw = (len(started) - 3) // 2
    send_sems, recv_sems = started[0], started[1]
    thru = started[2:2 + 2 * nw]

    def body(*refs):
        for _, arrival in _exchange_copies(gather, refs[:nw], refs[nw:2 * nw], refs[2 * nw], refs[2 * nw + 1]):
            arrival.wait_send()
            arrival.wait_recv()

    outs = pl.pallas_call(
        body, name=name,
        out_shape=tuple(pltpu.HBM(t.shape, t.dtype) for t in thru),
        in_specs=[_HBM] * (2 * nw) + [_SEM, _SEM, _ANY], out_specs=(_HBM,) * (2 * nw),
        input_output_aliases={i: i for i in range(2 * nw)},
        compiler_params=pltpu.CompilerParams(has_side_effects=_DATAFLOW),
    )(*thru, send_sems, recv_sems, after)
    return outs[nw:]


def _reduce_last(owns, landed, g_small):
    ns = len(owns)
    row_block = 32
    hs = SMALL_ROWS // 2

    def body(*refs):
        own_refs, land_refs, gsm_ref = refs[:ns], refs[ns:2 * ns], refs[2 * ns]
        out_refs, osm_ref = refs[2 * ns + 1:3 * ns + 1], refs[3 * ns + 1]
        ra_sm, p_sm, s_sem, r_sem, sm_s, sm_r = refs[3 * ns + 2:]
        x, y, c, chip, peers, peer_chip = _place()
        sib = (x, y, 1 - c)
        half = lambda cc: pl.ds(pl.multiple_of(cc * hs, 8), hs)
        sm_a = _remote(gsm_ref.at[half(1 - c), :], ra_sm, sm_s.at[0], sm_r.at[0], sib)
        sm_a.start()
        swaps = [sm_a]
        for w in range(ns):
            n = own_refs[w].shape[0]

            def total(i, carry, w=w, n=n):
                r0 = pl.multiple_of(i * row_block, row_block)
                blk = pl.ds(r0, row_block)
                acc = own_refs[w][blk, :]
                for m in range(3):
                    acc = acc + land_refs[w][m, blk, :].astype(F32)
                out_refs[w][pl.ds(pl.multiple_of(c * n + r0, 8), row_block), :] = acc
                return carry
            lax.fori_loop(0, n // row_block, total, 0)
            mine = out_refs[w].at[pl.ds(pl.multiple_of(c * n, 8), n), :]
            swaps.append(_remote(mine, mine, s_sem.at[w], r_sem.at[w], sib))
            swaps[-1].start()
        sm_a.wait_recv()
        p_sm[chip] = gsm_ref[half(c), :] + ra_sm[...]
        for m, (px, py) in enumerate(peers):
            swaps.append(_remote(p_sm.at[chip], p_sm.at[chip], sm_s.at[1 + m], sm_r.at[1 + m], (px, py, c)))
            swaps[-1].start()
        for m, (px, py) in enumerate(peers):
            _remote(p_sm.at[chip], p_sm.at[peer_chip[m]], sm_s.at[1 + m], sm_r.at[1 + m], (px, py, c)).wait_recv()
        osm_ref[half(c), :] = (p_sm[0] + p_sm[1]) + (p_sm[2] + p_sm[3])
        swaps.append(_remote(osm_ref.at[half(c), :], osm_ref.at[half(c), :], sm_s.at[4], sm_r.at[4], sib))
        swaps[-1].start()
        for w in range(ns):
            n = own_refs[w].shape[0]
            theirs = out_refs[w].at[pl.ds(pl.multiple_of((1 - c) * n, 8), n), :]
            _remote(theirs, theirs, s_sem.at[w], r_sem.at[w], sib).wait_recv()
        _remote(osm_ref.at[half(1 - c), :], osm_ref.at[half(1 - c), :], sm_s.at[4], sm_r.at[4], sib).wait_recv()
        for cp in swaps:
            cp.wait_send()

    vmem = pl.BlockSpec(memory_space=pltpu.VMEM)
    return pl.pallas_call(
        body, out_shape=[jax.ShapeDtypeStruct((2 * o.shape[0], o.shape[1]), F32) for o in owns]
        + [jax.ShapeDtypeStruct((SMALL_ROWS, LANES), F32)],
        in_specs=[vmem] * (2 * ns + 1), out_specs=[vmem] * (ns + 1),
        scratch_shapes=[pltpu.VMEM((hs, LANES), F32), pltpu.VMEM((N_CHIPS, hs, LANES), F32),
                        pltpu.SemaphoreType.DMA((ns,)), pltpu.SemaphoreType.DMA((ns,)),
                        pltpu.SemaphoreType.DMA((5,)), pltpu.SemaphoreType.DMA((5,))],
        compiler_params=pltpu.CompilerParams(vmem_limit_bytes=VMEM_LIMIT),
        name="reduce_last")(*owns, *landed, g_small)


_SMALL_PARTS = (("g_norm", 8, 8), ("w_s", 512, 512), ("b_s", 4, 8), ("g_v", 2, 8), ("g_mem", 8, 8),
                ("g_final", 8, 8), ("loss", 1, 8))
_LOSS_ROW = SMALL_ROWS - 8
assert sum(p for _, _, p in _SMALL_PARTS) == SMALL_ROWS


def _pack_small(parts, loss=None):
    loss_row = jnp.zeros((1, LANES), F32) if loss is None else jnp.broadcast_to(loss.reshape(1, 1), (1, LANES))
    rows = []
    for (name, used, padded), p in zip(_SMALL_PARTS, list(parts) + [loss_row]):
        p = p.reshape(used, LANES)
        if padded > used:
            p = jnp.pad(p, ((0, padded - used), (0, 0)))
        rows.append(p)
    return jnp.concatenate(rows, axis=0)


def _local_step(x, mem, target, g_norm, w_in, w_s, b_s, g_v, g_mem, late_weights, g_final,
                fwd_token=None, on_dw=None):
    B, S, _ = x.shape
    x2d = x.reshape(B * S, D_MODEL)
    t2d = target.reshape(B * S, D_MODEL)
    mem2d = mem.reshape(B * N_MEM, D_MODEL)

    proj = _inproj_fwd(x2d, g_norm, w_in, after=() if fwd_token is None else (fwd_token,))
    w_kv, w_out = late_weights(proj)
    kv = _kv_fwd(mem2d, g_mem, w_kv)
    a, lse = _attn_fwd(proj, B, S)
    w_sT = jnp.swapaxes(w_s, 1, 2)
    b_tab = jnp.repeat(b_s.T, HEAD_DIM, axis=1)
    (dx2, da, drest, loss, d_wout, d_ws, d_bs, d_gv, d_gf, dkv) = _mid(
        x2d, t2d, a, proj, kv, w_s, w_sT, b_tab, g_v, w_out, g_final, B, S)
    d_wkv, d_gmem = _kv_bwd(mem2d, g_mem, w_kv, dkv)
    dq, dk, dv = _attn_bwd(proj, a, lse, da, B, S)
    if on_dw is None:
        d_win = _inproj_bwd_dw(dq, dk, dv, drest, x2d, g_norm)
        after = ()
    else:
        d_win = None
        by_chip = lambda g: g.reshape((N_CHIPS, g.shape[0] // N_CHIPS, g.shape[1]))
        after = (on_dw(*_inproj_bwd_dw(dq, dk, dv, drest, x2d, g_norm, reduce_with=[by_chip(d_wkv), by_chip(d_wout)])),)
    grad_x, d_gnorm = _inproj_bwd_dx(dq, dk, dv, drest, x2d, dx2, g_norm, w_in, after=after)
    d_bs = d_bs[:, :N_SGU_GROUPS].T
    return (loss[0, 0], grad_x.reshape(B, S, D_MODEL),
            dict(g_norm=d_gnorm, w_in=d_win, w_s=d_ws, b_s=d_bs, g_v=d_gv, g_mem=d_gmem, w_kv=d_wkv,
                 w_out=d_wout, g_final=d_gf))


def kernel(x, mem, g_norm, w_in, w_sgu_spatial, b_sgu_spatial, g_sgu_v, g_mem, w_mem_kv, w_out, g_final, loss_target, m_g_norm, m_w_in, m_w_sgu_spatial, m_b_sgu_spatial, m_g_sgu_v, m_g_mem, m_w_mem_kv, m_w_out, m_g_final, v_g_norm, v_w_in, v_w_sgu_spatial, v_b_sgu_spatial, v_g_sgu_v, v_g_mem, v_w_mem_kv, v_w_out, v_g_final):
    t = lambda w: jnp.swapaxes(w[0], 0, 1)
    (win_all,), late_shards, late_lands = _ag_weights([t(w_in)], [w_mem_kv[0], w_out[0]])
    w_in_full = win_all.reshape(-1, win_all.shape[-1])
    late = _exchange_start(True, list(late_shards), win_all, "gather_late_start", lands=late_lands)

    def late_weights(proj):
        return [z.reshape(-1, z.shape[-1]) for z in _exchange_wait(True, late, proj, "gather_late_wait")]

    scatter = {}

    def on_dw(sends, owns):
        scatter["own"] = owns
        scatter["started"] = _exchange_start(False, list(sends), owns[0], "scatter_start")
        return scatter["started"][-1]

    loss, grad_x, g = _local_step(
        x, mem, loss_target, g_norm, w_in_full, w_sgu_spatial[0], b_sgu_spatial[0], g_sgu_v, g_mem,
        late_weights, g_final.reshape(1, D_MODEL), fwd_token=late[-1], on_dw=on_dw)

    small_names = ("g_norm", "w_s", "b_s", "g_v", "g_mem", "g_final")
    g_small = _pack_small([g[n] for n in small_names], loss)
    landed = _exchange_wait(False, scatter["started"], g_small, "scatter_wait")
    gr_in, gr_kv, gr_out, gr_small = _reduce_last(scatter["own"], landed, g_small)
    loss = gr_small[_LOSS_ROW, 0]

    small_w = (g_norm, w_sgu_spatial, b_sgu_spatial, g_sgu_v, g_mem, g_final)
    small_m = (m_g_norm, m_w_sgu_spatial, m_b_sgu_spatial, m_g_sgu_v, m_g_mem, m_g_final)
    small_v = (v_g_norm, v_w_sgu_spatial, v_b_sgu_spatial, v_g_sgu_v, v_g_mem, v_g_final)
    rows = lambda ws: [w.reshape(-1, LANES) for w in ws]
    small = [[z.reshape(w.shape) for z in four]
             for w, four in zip(small_w, _adamw_small(gr_small, rows(small_w), rows(small_m), rows(small_v)))]
    d_in, nm_in, nv_in = _adamw(t(w_in), gr_in, t(m_w_in), t(v_w_in), "adamw_w_in")
    gr_in, d_in, nm_in, nv_in = [jnp.swapaxes(z, 0, 1) for z in (gr_in, d_in, nm_in, nv_in)]
    d_kv, nm_kv, nv_kv = _adamw(w_mem_kv[0], gr_kv, m_w_mem_kv[0], v_w_mem_kv[0], "adamw_w_kv")
    d_out, nm_out, nv_out = _adamw(w_out[0], gr_out, m_w_out[0], v_w_out[0], "adamw_w_out")

    def leaves(kind, big_in, big_kv, big_out):
        s_norm, s_ws, s_bs, s_gv, s_gmem, s_gf = [four[kind] for four in small]
        return [s_norm, big_in[None], s_ws, s_bs, s_gv, s_gmem, big_kv[None], big_out[None], s_gf]

    return (loss, grad_x, *leaves(0, gr_in, gr_kv, gr_out), *leaves(1, d_in, d_kv, d_out),
            *leaves(2, nm_in, nm_kv, nm_out), *leaves(3, nv_in, nv_kv, nv_out))
```

```python
import functools

import jax
import jax.numpy as jnp
from jax import lax
from jax.experimental import pallas as pl
from jax.experimental.pallas import tpu as pltpu

F32 = jnp.float32
BF16 = jnp.bfloat16
MESH = pl.DeviceIdType.MESH

D_MODEL = 1024
ATTN_WIDTH = 512
SGU_WIDTH = 256
MEM_WIDTH = 256
N_MEM = 256
IN_COLS = 3328
QKV_COLS = 3 * ATTN_WIDTH
REST_COLS = IN_COLS - QKV_COLS
SGU_CHUNK = 128
N_SGU_GROUPS = 4
EPS = 1e-6
NEG_INF = -1e30
DILATIONS = (1, 4, 16)
RADIUS = 64
Q_BLOCK = 128
LANES = 128
HEAD_DIM = 64

ADAM_LR = 0.001
ADAM_B1 = 0.9
ADAM_B2 = 0.999
ADAM_EPS = 1e-08
ADAM_WD = 0.01
ADAM_STEP = 10

N_CHIPS = 4
VMEM_LIMIT = 56 * 1024 * 1024
SMALL_ROWS = 560


def _params(sem=None, vmem=VMEM_LIMIT):
    return pltpu.CompilerParams(dimension_semantics=sem, vmem_limit_bytes=vmem)


def _nn(a, b):
    return jnp.dot(a, b, preferred_element_type=F32)


def _nt(a, b):
    return lax.dot_general(a, b, (((1,), (1,)), ((), ())), preferred_element_type=F32)


def _tn(a, b):
    return lax.dot_general(a, b, (((0,), (0,)), ((), ())), preferred_element_type=F32)


def _rms(x):
    r = lax.rsqrt(jnp.mean(x * x, axis=-1, keepdims=True) + EPS)
    return r, x * r


def _head_masks():
    lane = lax.broadcasted_iota(jnp.int32, (1, LANES), 1)
    lo = lane < HEAD_DIM
    return lo, (lo.astype(F32), (~lo).astype(F32))


def _silu_parts(z):
    s = jax.nn.sigmoid(z)
    return z * s, s * (1.0 + z * (1.0 - s))


def _gelu_parts(x):
    c = 0.7978845608028654
    x2 = x * x
    s = jax.nn.sigmoid((2.0 * c) * (x + 0.044715 * (x * x2)))
    return x * s, s * (1.0 + x * (1.0 - s) * ((2.0 * c) * (1.0 + 3.0 * 0.044715 * x2)))


def _after(tokens):
    return [pl.BlockSpec(memory_space=pl.ANY)] * len(tokens)


def _inproj_fwd(x2d, g_norm, w_in_t, after=()):
    T = x2d.shape[0]
    tm = 512

    def body(x_ref, g_ref, w_ref, *rest):
        o_ref = rest[-1]
        _, xh = _rms(x_ref[...])
        h = (xh * g_ref[...]).astype(BF16)
        o_ref[...] = _nt(h, w_ref[...])

    return pl.pallas_call(
        body, grid=(T // tm,),
        in_specs=[pl.BlockSpec((tm, D_MODEL), lambda i: (i, 0)),
                  pl.BlockSpec((1, D_MODEL), lambda i: (0, 0)),
                  pl.BlockSpec((IN_COLS, D_MODEL), lambda i: (0, 0))] + _after(after),
        out_specs=pl.BlockSpec((tm, IN_COLS), lambda i: (i, 0)),
        out_shape=jax.ShapeDtypeStruct((T, IN_COLS), F32),
        compiler_params=_params(("arbitrary",)), name="inproj_fwd")(x2d, g_norm, w_in_t, *after)


def _kv_fwd(mem2d, g_mem, w_kv):
    Tm = mem2d.shape[0]

    def body(m_ref, g_ref, w_ref, o_ref):
        _, mh = _rms(m_ref[...])
        o_ref[...] = _nn((mh * g_ref[...]).astype(BF16), w_ref[...])

    return pl.pallas_call(
        body, out_shape=jax.ShapeDtypeStruct((Tm, 2 * MEM_WIDTH), F32),
        compiler_params=_params(), name="kv_fwd")(mem2d, g_mem, w_kv)


def _kv_bwd(mem2d, g_mem, w_kv, dkv):
    Tm = mem2d.shape[0]

    def body(m_ref, g_ref, w_ref, dkv_ref, dw_ref, dg_ref):
        _, mh = _rms(m_ref[...])
        memn = (mh * g_ref[...]).astype(BF16)
        dkvb = dkv_ref[...].astype(BF16)
        dw_ref[...] = _tn(memn, dkvb)
        dmemn = _nt(dkvb, w_ref[...])
        dg_ref[...] = jnp.sum(dmemn * mh, axis=0, keepdims=True)

    return pl.pallas_call(
        body, out_shape=(jax.ShapeDtypeStruct((D_MODEL, 2 * MEM_WIDTH), F32),
                         jax.ShapeDtypeStruct((1, D_MODEL), F32)),
        compiler_params=_params(), name="kv_bwd")(mem2d, g_mem, w_kv, dkv)


def _attn_geometry(S):
    geom = []
    for d in DILATIONS:
        L = S // d
        assert L % Q_BLOCK == 0
        geom.append((d, L, min(2 * Q_BLOCK, L), L // Q_BLOCK))
    return geom


def _init_bias(bias_scr, geom, hp):
    row = lax.broadcasted_iota(jnp.int32, (Q_BLOCK, 2 * Q_BLOCK), 0)
    col = lax.broadcasted_iota(jnp.int32, (Q_BLOCK, 2 * Q_BLOCK), 1)
    for j in (0, 1):
        bits = (126 - (2 * hp + j)) * (1 << 23)
        slope = lax.bitcast_convert_type(jnp.full((1, 1), bits, jnp.int32), F32)
        for di, (d, _, _, _) in enumerate(geom):
            for cls, off in enumerate((0, -RADIUS, -2 * RADIUS)):
                dist = jnp.abs(col - row + off)
                bias_scr[di * 6 + cls * 2 + j] = jnp.where(
                    dist <= RADIUS, -(slope * float(d)) * dist.astype(F32), NEG_INF)


SPLIT = 4
COPY_ROWS = 256


def _by4_rows(S, step):
    per_class = S // SPLIT // COPY_ROWS
    r, j = step // per_class, step % per_class
    return (pl.ds(r + SPLIT * j * COPY_ROWS, COPY_ROWS, stride=SPLIT),
            pl.ds(pl.multiple_of(r * (S // SPLIT) + j * COPY_ROWS, COPY_ROWS), COPY_ROWS))


def _to_by4(src, dst, S):
    def step(i, carry):
        natural, by4 = _by4_rows(S, i)
        dst[by4, :] = src[natural, :]
        return carry
    lax.fori_loop(0, S // COPY_ROWS, step, 0)


def _block_slices(d, L, KW, nqb, r, qb, S):
    qs = qb * Q_BLOCK
    ks = jnp.clip(qs - RADIUS, 0, L - KW)
    cls = jnp.where(qb == 0, 0, jnp.where(qb == nqb - 1, 2, 1))
    if d == 1:
        qsl = pl.ds(pl.multiple_of(qs, Q_BLOCK), Q_BLOCK)
        ksl = pl.ds(pl.multiple_of(ks, RADIUS), KW)
    elif d == SPLIT:
        qsl = pl.ds(pl.multiple_of(r * L + qs, Q_BLOCK), Q_BLOCK)
        ksl = pl.ds(pl.multiple_of(r * L + ks, RADIUS), KW)
    else:
        sub = d // SPLIT
        base = (r % SPLIT) * (S // SPLIT) + r // SPLIT
        qsl = pl.ds(base + qs * sub, Q_BLOCK, stride=sub)
        ksl = pl.ds(base + ks * sub, KW, stride=sub)
    return qsl, ksl, cls


def _for_groups(geom, S, group, fn):
    for di, (d, L, KW, nqb) in enumerate(geom):
        n = group[di]
        assert (d * nqb) % n == 0

        def step(it, carry, di=di, d=d, L=L, KW=KW, nqb=nqb, n=n):
            slices = []
            for g in range(n):
                i = it * n + g
                slices.append(_block_slices(d, L, KW, nqb, i // nqb, i % nqb, S))
            fn(di, KW, slices)
            return carry
        lax.fori_loop(0, d * nqb // n, step, 0)


def _attn_fwd(proj, B, S):
    T = B * S
    geom = _attn_geometry(S)
    n_pairs = ATTN_WIDTH // LANES

    def body(q_ref, k_ref, v_ref, a_ref, lse_ref, bias_scr, q4, k4, v4, *per_dilation):
        o_scr, m_scr, l_scr = per_dilation[0:3], per_dilation[3:6], per_dilation[6:9]
        lo, hm = _head_masks()
        pair = pl.program_id(0)

        @pl.when(pl.program_id(1) == 0)
        def _():
            _init_bias(bias_scr, geom, pair)
        for src, dst in ((q_ref, q4), (k_ref, k4), (v_ref, v4)):
            _to_by4(src, dst, S)

        def group(di, KW, slices):
            chains = [(g, j) for g in range(len(slices)) for j in (0, 1)]
            q_src, k_src, v_src = (q_ref, k_ref, v_ref) if di == 0 else (q4, k4, v4)
            q = [q_src[qsl, :] for qsl, _, _ in slices]
            kw = [k_src[ksl, :].astype(BF16) for _, ksl, _ in slices]
            vw = [v_src[ksl, :].astype(BF16) for _, ksl, _ in slices]
            s = {(g, j): _nt((q[g] * (hm[j] * 0.125)).astype(BF16), kw[g])
                 + bias_scr[di * 6 + slices[g][2] * 2 + j, :, pl.ds(0, KW)] for g, j in chains}
            m = {c: jnp.max(s[c], axis=1, keepdims=True) for c in chains}
            p = {c: jnp.exp(s[c] - m[c]) for c in chains}
            l = {c: jnp.sum(p[c], axis=1, keepdims=True) for c in chains}
            o = {(g, j): _nn(p[(g, j)].astype(BF16), vw[g]) for g, j in chains}
            for g, (qsl, _, _) in enumerate(slices):
                o_scr[di][qsl, :] = jnp.where(lo, o[(g, 0)], o[(g, 1)])
                m_scr[di][qsl, :] = jnp.where(lo, m[(g, 0)], m[(g, 1)])
                l_scr[di][qsl, :] = jnp.where(lo, l[(g, 0)], l[(g, 1)])

        _for_groups(geom, S, (16, 16, 16), group)

        def combine(i, carry):
            natural, by4 = _by4_rows(S, i)
            rows = [natural, by4, by4]
            ms = [m_scr[di][rows[di], :] for di in range(3)]
            mx = jnp.maximum(jnp.maximum(ms[0], ms[1]), ms[2])
            num = 0.0
            den = 0.0
            for di in range(3):
                w = jnp.exp(ms[di] - mx)
                num = num + w * o_scr[di][rows[di], :]
                den = den + w * l_scr[di][rows[di], :]
            a_ref[natural, :] = num / den
            lse_ref[natural, :] = mx + jnp.log(den)
            return carry

        lax.fori_loop(0, S // COPY_ROWS, combine, 0)

    blk = lambda off: pl.BlockSpec((S, LANES), lambda h, b, off=off: (b, off + h))
    out_blk = pl.BlockSpec((S, LANES), lambda h, b: (b, h))
    return pl.pallas_call(
        body, grid=(n_pairs, B),
        in_specs=[blk(0), blk(n_pairs), blk(2 * n_pairs)],
        out_specs=[out_blk, out_blk],
        out_shape=[jax.ShapeDtypeStruct((T, ATTN_WIDTH), F32)] * 2,
        scratch_shapes=[pltpu.VMEM((18, Q_BLOCK, 2 * Q_BLOCK), F32)] + [pltpu.VMEM((S, LANES), F32)] * 12,
        compiler_params=_params(("arbitrary", "arbitrary")), name="attn_fwd")(proj, proj, proj)


def _attn_bwd(proj, a, lse, da, B, S):
    T = B * S
    geom = _attn_geometry(S)
    n_pairs = ATTN_WIDTH // LANES

    def body(q_ref, k_ref, v_ref, a_ref, lse_ref, do_ref, dq_ref, dk_ref, dv_ref, bias_scr, *scr):
        acc = (scr[0:3], scr[3:6])
        natural_in = (q_ref, k_ref, v_ref, a_ref, lse_ref, do_ref)
        by4_in = scr[6:12]
        _, hm = _head_masks()
        pair = pl.program_id(0)

        @pl.when(pl.program_id(1) == 0)
        def _():
            _init_bias(bias_scr, geom, pair)
        for ref in scr[0:6]:
            ref[...] = jnp.zeros_like(ref)
        for src, dst in zip(natural_in, by4_in):
            _to_by4(src, dst, S)

        def group(di, KW, slices):
            n = len(slices)
            chains = [(g, j) for g in range(n) for j in (0, 1)]
            q_src, k_src, v_src, a_src, lse_src, do_src = natural_in if di == 0 else by4_in
            dq_scr, dk_scr, dv_scr = acc[0 if di == 0 else 1]
            q = [q_src[qsl, :] for qsl, _, _ in slices]
            do = [do_src[qsl, :] for qsl, _, _ in slices]
            doa = [do[g] * a_src[slices[g][0], :] for g in range(n)]
            lse_q = [lse_src[qsl, :] for qsl, _, _ in slices]
            kw = [k_src[ksl, :].astype(BF16) for _, ksl, _ in slices]
            vw = [v_src[ksl, :].astype(BF16) for _, ksl, _ in slices]
            qj = {(g, j): (q[g] * (hm[j] * 0.125)).astype(BF16) for g, j in chains}
            doj = {(g, j): (do[g] * hm[j]).astype(BF16) for g, j in chains}
            s = {(g, j): _nt(qj[(g, j)], kw[g])
                 + bias_scr[di * 6 + slices[g][2] * 2 + j, :, pl.ds(0, KW)] for g, j in chains}
            dp = {(g, j): _nt(doj[(g, j)], vw[g]) for g, j in chains}
            dsum = {(g, j): jnp.sum(doa[g] * hm[j], axis=1, keepdims=True) for g, j in chains}
            p = {(g, j): jnp.exp(s[(g, j)] - lse_q[g][:, HEAD_DIM * j:HEAD_DIM * j + 1]) for g, j in chains}
            ds = {c: (p[c] * (dp[c] - dsum[c])).astype(BF16) for c in chains}
            pb = {c: p[c].astype(BF16) for c in chains}
            dq = [_nn(ds[(g, 0)], kw[g]) * (hm[0] * 0.125) + _nn(ds[(g, 1)], kw[g]) * (hm[1] * 0.125)
                  for g in range(n)]
            both = lambda t, g: jnp.concatenate([t[(g, 0)], t[(g, 1)]], axis=0)
            dkw = [_tn(both(ds, g), both(qj, g)) for g in range(n)]
            dvw = [_tn(both(pb, g), both(doj, g)) for g in range(n)]
            for g, (qsl, ksl, _) in enumerate(slices):
                dq_scr[qsl, :] = dq_scr[qsl, :] + dq[g]
                dk_scr[ksl, :] = dk_scr[ksl, :] + dkw[g]
                dv_scr[ksl, :] = dv_scr[ksl, :] + dvw[g]

        _for_groups(geom, S, (4, 4, 16), group)

        def merge(i, carry):
            natural, by4 = _by4_rows(S, i)
            for nat, split in zip(*acc):
                nat[natural, :] = nat[natural, :] + split[by4, :]
            return carry
        lax.fori_loop(0, S // COPY_ROWS, merge, 0)
        for out, nat in zip((dq_ref, dk_ref, dv_ref), acc[0]):
            out[...] = nat[...].astype(BF16)

    blk = lambda off: pl.BlockSpec((S, LANES), lambda h, b, off=off: (b, off + h))
    return pl.pallas_call(
        body, grid=(n_pairs, B),
        in_specs=[blk(0), blk(n_pairs), blk(2 * n_pairs), blk(0), blk(0), blk(0)],
        out_specs=[blk(0), blk(0), blk(0)],
        out_shape=[jax.ShapeDtypeStruct((T, ATTN_WIDTH), BF16)] * 3,
        scratch_shapes=[pltpu.VMEM((18, Q_BLOCK, 2 * Q_BLOCK), F32)] + [pltpu.VMEM((S, LANES), F32)] * 12,
        compiler_params=_params(("arbitrary", "arbitrary")), name="attn_bwd")(proj, proj, proj, a, lse, da)


def _mid(x2d, t2d, a, proj, kv, w_s, w_sT, b_tab, g_v, w_out, g_final, B, S):
    T = B * S
    tm = 512
    nt = S // tm
    halves = 2
    hrows = tm // halves

    def body(x_ref, t_ref, a_ref, za_ref, ub_ref, vb_ref, zb_ref, qm_ref, zm_ref, kv_ref,
              ws_ref, wsT_ref, btab_ref, gv_ref, wout_ref, gf_ref,
              dx2_ref, da_ref, drest_ref, loss_ref, dwout_ref, dws_ref, dbs_ref, dgv_ref, dgf_ref, dkv_ref,
              dbtab_scr):
        b = pl.program_id(0)
        t = pl.program_id(1)
        first = jnp.logical_and(b == 0, t == 0)
        last = jnp.logical_and(b == B - 1, t == nt - 1)
        _, hm = _head_masks()
        lane_g = lax.broadcasted_iota(jnp.int32, (1, SGU_WIDTH), 1) // HEAD_DIM
        gm = [(lane_g == g).astype(F32) for g in range(N_SGU_GROUPS)]
        H = range(halves)
        rows = [pl.ds(h * hrows, hrows) for h in H]
        ld = lambda ref: [ref[r, :] for r in rows]
        cat = lambda parts, axis: jnp.concatenate(parts, axis=axis)
        chunks = [slice(ci * SGU_CHUNK, (ci + 1) * SGU_CHUNK) for ci in range(hrows // SGU_CHUNK)]
        pairs = [slice(pr * LANES, (pr + 1) * LANES) for pr in range(2)]
        heads = [(pr, j) for pr in range(2) for j in (0, 1)]

        @pl.when(first)
        def _():
            loss_ref[...] = jnp.zeros_like(loss_ref)
            dwout_ref[...] = jnp.zeros_like(dwout_ref)
            dws_ref[...] = jnp.zeros_like(dws_ref)
            dbs_ref[...] = jnp.zeros_like(dbs_ref)
            dgv_ref[...] = jnp.zeros_like(dgv_ref)
            dgf_ref[...] = jnp.zeros_like(dgf_ref)
            dbtab_scr[...] = jnp.zeros_like(dbtab_scr)

        @pl.when(t == 0)
        def _():
            dkv_ref[...] = jnp.zeros_like(dkv_ref)

        a_val = ld(a_ref)
        sil_a = [_silu_parts(z) for z in ld(za_ref)]
        gated_a = [s[0] * a for s, a in zip(sil_a, a_val)]
        u = [_gelu_parts(z) for z in ld(ub_ref)]
        vv = [_gelu_parts(z) for z in ld(vb_ref)]
        vnorm = [_rms(v[0]) for v in vv]
        gv = gv_ref[...]
        vn = [(n[1] * gv).astype(BF16) for n in vnorm]
        w_cat = cat([ws_ref[g].astype(BF16) for g in range(N_SGU_GROUPS)], 1)
        wT_cat = cat([wsT_ref[g].astype(BF16) for g in range(N_SGU_GROUPS)], 1)
        gmb = [m.astype(BF16) for m in gm]
        by_group = lambda chunk: cat([chunk * gmb[g] for g in range(N_SGU_GROUPS)], 0)
        btab = btab_ref[...]
        mixed = [cat([btab + _nn(w_cat, by_group(vn[h][c, :])) for c in chunks], 0) for h in H]
        sg = [u[h][0] * mixed[h] for h in H]
        sil_b = [_silu_parts(z) for z in ld(zb_ref)]
        gated_b = [sil_b[h][0] * sg[h] for h in H]

        kvv = kv_ref[...].astype(BF16)
        kp = [kvv[:, p] for p in pairs]
        vp = [kvv[:, MEM_WIDTH + pr * LANES:MEM_WIDTH + (pr + 1) * LANES] for pr in range(2)]
        qm = ld(qm_ref)
        qj = {(h, pr, j): (qm[h][:, pairs[pr]] * (hm[j] * 0.125)).astype(BF16) for h in H for pr, j in heads}
        sc = {k: _nt(qj[k], kp[k[1]]) for k in qj}
        ex = {k: jnp.exp(sc[k] - jnp.max(sc[k], axis=1, keepdims=True)) for k in qj}
        prob = {k: ex[k] * (1.0 / jnp.sum(ex[k], axis=1, keepdims=True)) for k in qj}
        probb = {k: prob[k].astype(BF16) for k in qj}
        mo = [cat([sum(_nn(probb[(h, pr, j)], vp[pr]) * hm[j] for j in (0, 1)) for pr in range(2)], 1) for h in H]
        sil_m = [_silu_parts(z) for z in ld(zm_ref)]
        gated_m = [sil_m[h][0] * mo[h] for h in H]

        gated = [cat([gated_a[h], gated_b[h], gated_m[h]], 1).astype(BF16) for h in H]
        wout = wout_ref[...]
        x_in = ld(x_ref)
        x2 = [x_in[h] + _nn(gated[h], wout) for h in H]
        fin = [_rms(z) for z in x2]
        gf = gf_ref[...]
        tgt = ld(t_ref)
        err = [fin[h][1] * gf - tgt[h] for h in H]
        loss_ref[...] += sum(jnp.sum(e * e) for e in err) * (0.5 / D_MODEL)

        dy = [e * (1.0 / D_MODEL) for e in err]
        dgf_ref[...] += sum(jnp.sum(dy[h] * fin[h][1], axis=0, keepdims=True) for h in H)
        gdy = [d * gf for d in dy]
        dx2 = [fin[h][0] * (gdy[h] - fin[h][1] * jnp.mean(gdy[h] * fin[h][1], axis=1, keepdims=True)) for h in H]
        for h in H:
            dx2_ref[rows[h], :] = dx2[h]
        dx2b = [d.astype(BF16) for d in dx2]
        dgated = [_nt(d, wout) for d in dx2b]
        dwout_ref[...] += _tn(cat(gated, 0), cat(dx2b, 0))
        dga = [d[:, 0:ATTN_WIDTH] for d in dgated]
        dgb = [d[:, ATTN_WIDTH:ATTN_WIDTH + SGU_WIDTH] for d in dgated]
        dgm = [d[:, ATTN_WIDTH + SGU_WIDTH:] for d in dgated]

        for h in H:
            da_ref[rows[h], :] = dga[h] * sil_a[h][0]
        dza = [dga[h] * a_val[h] * sil_a[h][1] for h in H]

        dsg = [dgb[h] * sil_b[h][0] for h in H]
        dzb = [dgb[h] * sg[h] * sil_b[h][1] for h in H]
        dub = [dsg[h] * mixed[h] * u[h][1] for h in H]
        dmixed = [dsg[h] * u[h][0] for h in H]
        dmixed_b = [d.astype(BF16) for d in dmixed]
        dvn = [cat([_nn(wT_cat, by_group(dmixed_b[h][c, :])) for c in chunks], 0) for h in H]
        for g in range(N_SGU_GROUPS):
            dws_ref[g] += sum(_nt((dmixed[h][c, :] * gm[g]).astype(BF16), vn[h][c, :]) for h in H for c in chunks)
        dbtab_scr[...] += sum(dmixed[h][c, :] for h in H for c in chunks)
        dgv_ref[...] += sum(jnp.sum(dvn[h] * vnorm[h][1], axis=0, keepdims=True) for h in H)
        tv = [d * gv for d in dvn]
        dvv = [vnorm[h][0] * (tv[h] - vnorm[h][1] * jnp.mean(tv[h] * vnorm[h][1], axis=1, keepdims=True)) for h in H]
        dvb = [dvv[h] * vv[h][1] for h in H]

        dmo = [dgm[h] * sil_m[h][0] for h in H]
        dzm = [dgm[h] * mo[h] * sil_m[h][1] for h in H]
        dmoj = {(h, pr, j): (dmo[h][:, pairs[pr]] * hm[j]).astype(BF16) for h in H for pr, j in heads}
        dp = {k: _nt(dmoj[k], vp[k[1]]) for k in qj}
        ds = {k: (prob[k] * (dp[k] - jnp.sum(dp[k] * prob[k], axis=1, keepdims=True))).astype(BF16) for k in qj}
        dqm = [cat([sum(_nn(ds[(h, pr, j)], kp[pr]) * (hm[j] * 0.125) for j in (0, 1)) for pr in range(2)], 1)
               for h in H]
        every = lambda tbl, pr: cat([tbl[(h, pr, j)] for h in H for j in (0, 1)], 0)
        dk = [_tn(every(ds, pr), every(qj, pr)) for pr in range(2)]
        dv = [_tn(every(probb, pr), every(dmoj, pr)) for pr in range(2)]
        dkv_ref[...] += cat(dk + dv, 1)

        for h in H:
            drest_ref[rows[h], :] = cat([dza[h], dub[h], dvb[h], dzb[h], dqm[h], dzm[h]], 1).astype(BF16)

        @pl.when(last)
        def _():
            lane = lax.broadcasted_iota(jnp.int32, (1, LANES), 1)
            dbt = dbtab_scr[...]
            out = jnp.zeros((SGU_CHUNK, LANES), F32)
            for g in range(N_SGU_GROUPS):
                out = out + jnp.where(lane == g, jnp.sum(dbt * gm[g], axis=1, keepdims=True), 0.0)
            dbs_ref[...] = out

    tile = lambda w, cb: pl.BlockSpec((tm, w), lambda b, t, cb=cb: (b * nt + t, cb))
    const = lambda shape: pl.BlockSpec(shape, lambda b, t, n=len(shape): (0,) * n)
    return pl.pallas_call(
        body, grid=(B, nt),
        in_specs=[tile(D_MODEL, 0), tile(D_MODEL, 0), tile(ATTN_WIDTH, 0),
                  tile(ATTN_WIDTH, 3),
                  tile(SGU_WIDTH, 8), tile(SGU_WIDTH, 9), tile(SGU_WIDTH, 10),
                  tile(MEM_WIDTH, 11), tile(MEM_WIDTH, 12),
                  pl.BlockSpec((N_MEM, 2 * MEM_WIDTH), lambda b, t: (b, 0)),
                  const((N_SGU_GROUPS, SGU_CHUNK, SGU_CHUNK)), const((N_SGU_GROUPS, SGU_CHUNK, SGU_CHUNK)),
                  const((SGU_CHUNK, SGU_WIDTH)), const((1, SGU_WIDTH)),
                  const((D_MODEL, D_MODEL)), const((1, D_MODEL))],
        out_specs=[tile(D_MODEL, 0), tile(ATTN_WIDTH, 0), tile(REST_COLS, 0),
                   const((8, LANES)), const((D_MODEL, D_MODEL)),
                   const((N_SGU_GROUPS, SGU_CHUNK, SGU_CHUNK)), const((SGU_CHUNK, LANES)),
                   const((1, SGU_WIDTH)), const((1, D_MODEL)),
                   pl.BlockSpec((N_MEM, 2 * MEM_WIDTH), lambda b, t: (b, 0))],
        out_shape=[jax.ShapeDtypeStruct((T, D_MODEL), F32), jax.ShapeDtypeStruct((T, ATTN_WIDTH), F32),
                   jax.ShapeDtypeStruct((T, REST_COLS), BF16),
                   jax.ShapeDtypeStruct((8, LANES), F32), jax.ShapeDtypeStruct((D_MODEL, D_MODEL), F32),
                   jax.ShapeDtypeStruct((N_SGU_GROUPS, SGU_CHUNK, SGU_CHUNK), F32),
                   jax.ShapeDtypeStruct((SGU_CHUNK, LANES), F32),
                   jax.ShapeDtypeStruct((1, SGU_WIDTH), F32), jax.ShapeDtypeStruct((1, D_MODEL), F32),
                   jax.ShapeDtypeStruct((B * N_MEM, 2 * MEM_WIDTH), F32)],
        scratch_shapes=[pltpu.VMEM((SGU_CHUNK, SGU_WIDTH), F32)],
        compiler_params=_params(("arbitrary", "arbitrary")), name="mid")(
            x2d, t2d, a, proj, proj, proj, proj, proj, proj, kv, w_s, w_sT, b_tab, g_v, w_out, g_final)


def _inproj_bwd_dx(dq, dk, dv, drest, x2d, dx2, g_norm, w_in_t, after=()):
    T = x2d.shape[0]
    tm = 512
    W = ATTN_WIDTH

    def body(dq_ref, dk_ref, dv_ref, dr_ref, x_ref, dx2_ref, g_ref, w_ref, *rest):
        gx_ref, dg_ref = rest[-2:]

        @pl.when(pl.program_id(0) == 0)
        def _():
            dg_ref[...] = jnp.zeros_like(dg_ref)

        halves = [pl.ds(h * (tm // 2), tm // 2) for h in (0, 1)]
        dh = [(_nn(dq_ref[r, :], w_ref[0:W, :]) + _nn(dk_ref[r, :], w_ref[W:2 * W, :])
               + _nn(dv_ref[r, :], w_ref[2 * W:3 * W, :]) + _nn(dr_ref[r, :], w_ref[QKV_COLS:IN_COLS, :]))
              for r in halves]
        nrm = [_rms(x_ref[r, :]) for r in halves]
        dg_ref[...] += sum(jnp.sum(d * n[1], axis=0, keepdims=True) for d, n in zip(dh, nrm))
        g = g_ref[...]
        for r, d, (rstd, xh) in zip(halves, dh, nrm):
            th = d * g
            gx_ref[r, :] = rstd * (th - xh * jnp.mean(th * xh, axis=1, keepdims=True)) + dx2_ref[r, :]

    tile = lambda w: pl.BlockSpec((tm, w), lambda i: (i, 0))
    return pl.pallas_call(
        body, grid=(T // tm,),
        in_specs=[tile(W), tile(W), tile(W), tile(REST_COLS), tile(D_MODEL), tile(D_MODEL),
                  pl.BlockSpec((1, D_MODEL), lambda i: (0, 0)),
                  pl.BlockSpec((IN_COLS, D_MODEL), lambda i: (0, 0))] + _after(after),
        out_specs=[tile(D_MODEL), pl.BlockSpec((1, D_MODEL), lambda i: (0, 0))],
        out_shape=[jax.ShapeDtypeStruct((T, D_MODEL), F32), jax.ShapeDtypeStruct((1, D_MODEL), F32)],
        compiler_params=_params(("arbitrary",)), name="inproj_bwd_dx")(
            dq, dk, dv, drest, x2d, dx2, g_norm, w_in_t, *after)


def _inproj_bwd_dw(dq, dk, dv, drest, x2d, g_norm, reduce_with=None):
    T = x2d.shape[0]
    tm = 512
    nt = T // tm
    W = ATTN_WIDTH
    fused = reduce_with is not None
    others = list(reduce_with) if fused else []
    ns = 1 + len(others)
    shard = IN_COLS // N_CHIPS
    halves = [shard // 2] + [s.shape[1] // 2 for s in others]
    cols = [D_MODEL] + [s.shape[2] for s in others]
    row_block = 32

    def body(dq_ref, dk_ref, dv_ref, dr_ref, x_ref, g_ref, *rest):
        if fused:
            stacks = rest[:ns - 1]
            sends, owns = rest[ns - 1:2 * ns - 1], rest[2 * ns - 1:3 * ns - 1]
            acc, ras, narrow = rest[3 * ns - 1], rest[3 * ns:4 * ns], rest[4 * ns]
            s_sem, r_sem = rest[4 * ns + 1], rest[4 * ns + 2]
            x, y, c, chip, peers, peer_chip = _place()
            sib = (x, y, 1 - c)

            def part(w, k, cc, r0=0, rows=None):
                n = halves[w]
                rows = n if rows is None else rows
                if w == 0:
                    return acc.at[pl.ds(pl.multiple_of(k * shard + cc * n + r0, 8), rows), :]
                return stacks[w - 1].at[k, pl.ds(pl.multiple_of(cc * n + r0, 8), rows), :]

            def swap_other(w):
                theirs = stacks[w - 1].at[:, pl.ds(pl.multiple_of((1 - c) * halves[w], 8), halves[w]), :]
                return _remote(theirs, ras[w], s_sem.at[N_CHIPS - 1 + w], r_sem.at[N_CHIPS - 1 + w], sib)

            def swap_win(k):
                return _remote(narrow.at[k], ras[0].at[k], s_sem.at[k], r_sem.at[k], sib)
        else:
            acc = rest[0]

        @pl.when(pl.program_id(0) == 0)
        def _():
            acc[...] = jnp.zeros_like(acc)
            for w in range(1, ns):
                swap_other(w).start()

        _, xh = _rms(x_ref[...])
        h = (xh * g_ref[...]).astype(BF16)
        acc[0:W, :] += _tn(dq_ref[...], h)
        acc[W:2 * W, :] += _tn(dk_ref[...], h)
        acc[2 * W:3 * W, :] += _tn(dv_ref[...], h)
        acc[QKV_COLS:IN_COLS, :] += _tn(dr_ref[...], h)

        if fused:
            @pl.when(pl.program_id(0) == nt - 1)
            def _():
                for k in range(N_CHIPS):
                    def to_bf16(i, carry, k=k):
                        r0 = pl.multiple_of(i * row_block, row_block)
                        narrow[k, pl.ds(r0, row_block), :] = part(0, k, 1 - c, r0, row_block)[...].astype(BF16)
                        return carry
                    lax.fori_loop(0, halves[0] // row_block, to_bf16, 0)
                    swap_win(k).start()
                for w in list(range(1, ns)) + [0]:
                    if w == 0:
                        for k in range(N_CHIPS):
                            swap_win(k).wait_recv()
                    else:
                        swap_other(w).wait_recv()

                    def sums(i, carry, w=w):
                        r0 = pl.multiple_of(i * row_block, row_block)
                        blk = pl.ds(r0, row_block)
                        for m in range(3):
                            k = peer_chip[m]
                            sends[w][m, blk, :] = (part(w, k, c, r0, row_block)[...]
                                                   + ras[w][k, blk, :].astype(F32)).astype(BF16)
                        owns[w][blk, :] = part(w, chip, c, r0, row_block)[...] + ras[w][chip, blk, :].astype(F32)
                        return carry
                    lax.fori_loop(0, halves[w] // row_block, sums, 0)
                for k in range(N_CHIPS):
                    swap_win(k).wait_send()
                for w in range(1, ns):
                    swap_other(w).wait_send()

    tile = lambda w: pl.BlockSpec((tm, w), lambda i: (i, 0))
    vmem = pl.BlockSpec(memory_space=pltpu.VMEM)
    in_specs = [tile(W), tile(W), tile(W), tile(REST_COLS), tile(D_MODEL), pl.BlockSpec((1, D_MODEL), lambda i: (0, 0))]
    if not fused:
        return pl.pallas_call(
            body, grid=(nt,), in_specs=in_specs,
            out_specs=pl.BlockSpec((IN_COLS, D_MODEL), lambda i: (0, 0)),
            out_shape=jax.ShapeDtypeStruct((IN_COLS, D_MODEL), F32),
            compiler_params=_params(("arbitrary",)), name="inproj_bwd_dw")(dq, dk, dv, drest, x2d, g_norm)
    outs = pl.pallas_call(
        body, grid=(nt,), in_specs=in_specs + [vmem] * (ns - 1), out_specs=[vmem] * (2 * ns),
        out_shape=[jax.ShapeDtypeStruct((3, n, cl), BF16) for n, cl in zip(halves, cols)]
        + [jax.ShapeDtypeStruct((n, cl), F32) for n, cl in zip(halves, cols)],
        scratch_shapes=[pltpu.VMEM((IN_COLS, D_MODEL), F32)]
        + [pltpu.VMEM((N_CHIPS, n, cl), BF16 if w == 0 else F32) for w, (n, cl) in enumerate(zip(halves, cols))]
        + [pltpu.VMEM((N_CHIPS, halves[0], D_MODEL), BF16)]
        + [pltpu.SemaphoreType.DMA((N_CHIPS - 1 + ns,)), pltpu.SemaphoreType.DMA((N_CHIPS - 1 + ns,))],
        compiler_params=_params(("arbitrary",)), name="inproj_bwd_dw_reduce")(
            dq, dk, dv, drest, x2d, g_norm, *others)
    return outs[:ns], outs[ns:]


def _adamw_update(w, g, m, v):
    nm = ADAM_B1 * m + (1.0 - ADAM_B1) * g
    nv = ADAM_B2 * v + (1.0 - ADAM_B2) * (g * g)
    m_hat = nm / (1.0 - ADAM_B1 ** ADAM_STEP)
    v_hat = nv / (1.0 - ADAM_B2 ** ADAM_STEP)
    return -ADAM_LR * (m_hat / (jnp.sqrt(v_hat) + ADAM_EPS) + ADAM_WD * w), nm, nv


def _adamw(w, g, m, v, name):
    R, C = w.shape
    br = max(r for r in range(8, 257, 8) if R % r == 0)

    def body(w_ref, g_ref, m_ref, v_ref, d_ref, nm_ref, nv_ref):
        d_ref[...], nm_ref[...], nv_ref[...] = _adamw_update(w_ref[...], g_ref[...], m_ref[...], v_ref[...])

    spec = pl.BlockSpec((br, C), lambda i: (i, 0))
    return pl.pallas_call(
        body, grid=(R // br,), in_specs=[spec] * 4, out_specs=[spec] * 3,
        out_shape=[jax.ShapeDtypeStruct((R, C), F32)] * 3,
        compiler_params=_params(("arbitrary",)), name=name)(w, g, m, v)


def _adamw_small(g_packed, ws, ms, vs):
    n = len(ws)

    def body(*refs):
        g_ref = refs[0]
        w_refs, m_refs, v_refs = refs[1:1 + n], refs[1 + n:1 + 2 * n], refs[1 + 2 * n:1 + 3 * n]
        outs = refs[1 + 3 * n:]
        off = 0
        for i, (_, used, padded) in enumerate(_SMALL_PARTS[:n]):
            g = g_ref[off:off + used, :]
            delta, nm, nv = _adamw_update(w_refs[i][...], g, m_refs[i][...], v_refs[i][...])
            outs[4 * i][...], outs[4 * i + 1][...], outs[4 * i + 2][...], outs[4 * i + 3][...] = g, delta, nm, nv
            off += padded

    outs = pl.pallas_call(
        body, out_shape=[jax.ShapeDtypeStruct(w.shape, F32) for w in ws for _ in range(4)],
        compiler_params=_params(), name="adamw_small")(g_packed, *ws, *ms, *vs)
    return [outs[4 * i:4 * i + 4] for i in range(n)]


def _place():
    x, y, c = lax.axis_index("x"), lax.axis_index("y"), lax.axis_index("c")
    chip = 2 * x + y
    peers = [(x, 1 - y), (1 - x, y), (1 - x, 1 - y)]
    peer_chip = [2 * px + py for px, py in peers]
    return x, y, c, chip, peers, peer_chip


def _remote(src, dst, send_sem, recv_sem, dev):
    return pltpu.make_async_remote_copy(src_ref=src, dst_ref=dst, send_sem=send_sem, recv_sem=recv_sem,
                                        device_id=dev, device_id_type=MESH)


def _ag_weights(weights, late=()):
    nw, nl = len(weights), len(late)
    pieces = 2

    def body(*refs):
        srcs, late_srcs = refs[:nw], refs[nw:nw + nl]
        outs, late_bf, late_land = (refs[nw + nl:2 * nw + nl], refs[2 * nw + nl:2 * nw + 2 * nl],
                                    refs[2 * nw + 2 * nl:2 * nw + 3 * nl])
        s_ici, r_ici, s_d2d, r_d2d = refs[2 * nw + 3 * nl:]
        x, y, c = lax.axis_index("x"), lax.axis_index("y"), lax.axis_index("c")
        chip = 2 * x + y
        sib = (x, y, 1 - c)
        first = ((x + 1 - c) % 2, (y + c) % 2)
        second = ((x + c) % 2, (y + 1 - c) % 2)
        first_chip, second_chip = 2 * first[0] + first[1], 2 * second[0] + second[1]
        diag_chip = 3 - chip
        for src, out in zip(srcs, outs):
            out[chip] = src[...].astype(BF16)

        parts = [(w, out, pc) for w, out in enumerate(outs) for pc in range(pieces)]

        def piece(out, k, cc, pc):
            rows = out.shape[1] // 2 // pieces
            return out.at[k, pl.ds(pl.multiple_of((cc * pieces + pc) * rows, 16), rows), :]

        def ici(w, slot, out, k, dev, pc):
            blk, sem = piece(out, k, c, pc), (nw * slot + w) * pieces + pc
            return _remote(blk, blk, s_ici.at[sem], r_ici.at[sem], (dev[0], dev[1], c))

        def d2d(w, slot, out, k, cc, pc):
            blk, sem = piece(out, k, cc, pc), (nw * slot + w) * pieces + pc
            return _remote(blk, blk, s_d2d.at[sem], r_d2d.at[sem], sib)

        sent = []
        for slot, dev in enumerate((first, second)):
            for w, out, pc in parts:
                sent.append(ici(w, slot, out, chip, dev, pc))
                sent[-1].start()
        for src, bf, land in zip(late_srcs, late_bf, late_land):
            bf[...] = src[...].astype(BF16)
            land[...] = jnp.zeros_like(land)
            land[chip] = bf[...]
        for slot, k, dev in ((0, first_chip, first), (1, second_chip, second), (2, diag_chip, second)):
            for w, out, pc in parts:
                ici(w, slot, out, k, dev, pc).wait_recv()
                if slot == 0:
                    sent.append(ici(w, 2, out, k, second, pc))
                    sent[-1].start()
                sent.append(d2d(w, slot, out, k, c, pc))
                sent[-1].start()
        for slot, k in ((0, second_chip), (1, first_chip), (2, diag_chip)):
            for w, out, pc in parts:
                d2d(w, slot, out, k, 1 - c, pc).wait_recv()
        for cp in sent:
            cp.wait_send()

    vmem = pl.BlockSpec(memory_space=pltpu.VMEM)
    outs = pl.pallas_call(
        body,
        out_shape=[jax.ShapeDtypeStruct((N_CHIPS,) + w.shape, BF16) for w in weights]
        + [jax.ShapeDtypeStruct(w.shape, BF16) for w in late]
        + [jax.ShapeDtypeStruct((N_CHIPS,) + w.shape, BF16) for w in late],
        in_specs=[vmem] * (nw + nl), out_specs=[vmem] * (nw + 2 * nl),
        scratch_shapes=[pltpu.SemaphoreType.DMA((3 * nw * pieces,))] * 4,
        compiler_params=pltpu.CompilerParams(vmem_limit_bytes=VMEM_LIMIT), name="ag_weights")(*weights, *late)
    return outs[:nw], outs[nw:nw + nl], outs[nw + nl:]


_HBM = pl.BlockSpec(memory_space=pltpu.HBM)
_SEM = pl.BlockSpec(memory_space=pltpu.SEMAPHORE)
_ANY = pl.BlockSpec(memory_space=pl.ANY)
_DATAFLOW = pltpu.SideEffectType.DATAFLOW_SIDE_EFFECTING


def _in_hbm(a):
    return pltpu.with_memory_space_constraint(a, pltpu.HBM)


def _exchange_copies(gather, srcs, lands, send_sems, recv_sems):
    nw = len(srcs)
    x, y, c, chip, peers, peer_chip = _place()
    pairs = []
    for m, (px, py) in enumerate(peers):
        for w in range(nw):
            sems = (send_sems.at[nw * m + w], recv_sems.at[nw * m + w], (px, py, c))
            if gather:
                pairs.append((_remote(srcs[w], lands[w].at[chip], *sems),
                              _remote(srcs[w], lands[w].at[peer_chip[m]], *sems)))
            else:
                pairs.append((_remote(srcs[w].at[m], lands[w].at[m], *sems),) * 2)
    return pairs


def _exchange_start(gather, srcs, after, name, lands=None):
    nw = len(srcs)
    n_copies = 3 * nw

    def body(*refs):
        send_sems, recv_sems = refs[2 * nw + 1], refs[2 * nw + 2]
        for start, _ in _exchange_copies(gather, refs[:nw], refs[nw:2 * nw], send_sems, recv_sems):
            start.start()
        refs[-1][...] = jnp.zeros_like(refs[-1])

    if lands is None:
        lands = [lax.empty(((N_CHIPS,) + s.shape) if gather else s.shape, s.dtype) for s in srcs]
    lands = [_in_hbm(l) for l in lands]
    return pl.pallas_call(
        body, name=name,
        out_shape=(pltpu.SemaphoreType.DMA((n_copies,)), pltpu.SemaphoreType.DMA((n_copies,)))
        + tuple(pltpu.HBM(s.shape, s.dtype) for s in srcs)
        + tuple(pltpu.HBM(l.shape, l.dtype) for l in lands)
        + (jax.ShapeDtypeStruct((8, LANES), F32),),
        in_specs=[_HBM] * (2 * nw) + [_ANY],
        out_specs=(_SEM, _SEM) + (_HBM,) * (2 * nw) + (pl.BlockSpec(memory_space=pltpu.VMEM),),
        input_output_aliases={i: 2 + i for i in range(2 * nw)},
        compiler_params=pltpu.CompilerParams(has_side_effects=_DATAFLOW),
    )(*[_in_hbm(s) for s in srcs], *lands, after)


def _exchange_wait(gather, started, after, name):
    nw = (len(started) - 3) // 2
    send_sems, recv_sems = started[0], started[1]
    thru = started[2:2 + 2 * nw]

    def body(*refs):
        for _, arrival in _exchange_copies(gather, refs[:nw], refs[nw:2 * nw], refs[2 * nw], refs[2 * nw + 1]):
            arrival.wait_send()
            arrival.wait_recv()

    outs = pl.pallas_call(
        body, name=name,
        out_shape=tuple(pltpu.HBM(t.shape, t.dtype) for t in thru),
        in_specs=[_HBM] * (2 * nw) + [_SEM, _SEM, _ANY], out_specs=(_HBM,) * (2 * nw),
        input_output_aliases={i: i for i in range(2 * nw)},
        compiler_params=pltpu.CompilerParams(has_side_effects=_DATAFLOW),
    )(*thru, send_sems, recv_sems, after)
    return outs[nw:]


def _reduce_last(owns, landed, g_small):
    ns = len(owns)
    row_block = 32
    hs = SMALL_ROWS // 2

    def body(*refs):
        own_refs, land_refs, gsm_ref = refs[:ns], refs[ns:2 * ns], refs[2 * ns]
        out_refs, osm_ref = refs[2 * ns + 1:3 * ns + 1], refs[3 * ns + 1]
        ra_sm, p_sm, s_sem, r_sem, sm_s, sm_r = refs[3 * ns + 2:]
        x, y, c, chip, peers, peer_chip = _place()
        sib = (x, y, 1 - c)
        half = lambda cc: pl.ds(pl.multiple_of(cc * hs, 8), hs)
        sm_a = _remote(gsm_ref.at[half(1 - c), :], ra_sm, sm_s.at[0], sm_r.at[0], sib)
        sm_a.start()
        swaps = [sm_a]
        for w in range(ns):
            n = own_refs[w].shape[0]

            def total(i, carry, w=w, n=n):
                r0 = pl.multiple_of(i * row_block, row_block)
                blk = pl.ds(r0, row_block)
                acc = own_refs[w][blk, :]
                for m in range(3):
                    acc = acc + land_refs[w][m, blk, :].astype(F32)
                out_refs[w][pl.ds(pl.multiple_of(c * n + r0, 8), row_block), :] = acc
                return carry
            lax.fori_loop(0, n // row_block, total, 0)
            mine = out_refs[w].at[pl.ds(pl.multiple_of(c * n, 8), n), :]
            swaps.append(_remote(mine, mine, s_sem.at[w], r_sem.at[w], sib))
            swaps[-1].start()
        sm_a.wait_recv()
        p_sm[chip] = gsm_ref[half(c), :] + ra_sm[...]
        for m, (px, py) in enumerate(peers):
            swaps.append(_remote(p_sm.at[chip], p_sm.at[chip], sm_s.at[1 + m], sm_r.at[1 + m], (px, py, c)))
            swaps[-1].start()
        for m, (px, py) in enumerate(peers):
            _remote(p_sm.at[chip], p_sm.at[peer_chip[m]], sm_s.at[1 + m], sm_r.at[1 + m], (px, py, c)).wait_recv()
        osm_ref[half(c), :] = (p_sm[0] + p_sm[1]) + (p_sm[2] + p_sm[3])
        swaps.append(_remote(osm_ref.at[half(c), :], osm_ref.at[half(c), :], sm_s.at[4], sm_r.at[4], sib))
        swaps[-1].start()
        for w in range(ns):
            n = own_refs[w].shape[0]
            theirs = out_refs[w].at[pl.ds(pl.multiple_of((1 - c) * n, 8), n), :]
            _remote(theirs, theirs, s_sem.at[w], r_sem.at[w], sib).wait_recv()
        _remote(osm_ref.at[half(1 - c), :], osm_ref.at[half(1 - c), :], sm_s.at[4], sm_r.at[4], sib).wait_recv()
        for cp in swaps:
            cp.wait_send()

    vmem = pl.BlockSpec(memory_space=pltpu.VMEM)
    return pl.pallas_call(
        body, out_shape=[jax.ShapeDtypeStruct((2 * o.shape[0], o.shape[1]), F32) for o in owns]
        + [jax.ShapeDtypeStruct((SMALL_ROWS, LANES), F32)],
        in_specs=[vmem] * (2 * ns + 1), out_specs=[vmem] * (ns + 1),
        scratch_shapes=[pltpu.VMEM((hs, LANES), F32), pltpu.VMEM((N_CHIPS, hs, LANES), F32),
                        pltpu.SemaphoreType.DMA((ns,)), pltpu.SemaphoreType.DMA((ns,)),
                        pltpu.SemaphoreType.DMA((5,)), pltpu.SemaphoreType.DMA((5,))],
        compiler_params=pltpu.CompilerParams(vmem_limit_bytes=VMEM_LIMIT),
        name="reduce_last")(*owns, *landed, g_small)


_SMALL_PARTS = (("g_norm", 8, 8), ("w_s", 512, 512), ("b_s", 4, 8), ("g_v", 2, 8), ("g_mem", 8, 8),
                ("g_final", 8, 8), ("loss", 1, 8))
_LOSS_ROW = SMALL_ROWS - 8
assert sum(p for _, _, p in _SMALL_PARTS) == SMALL_ROWS


def _pack_small(parts, loss=None):
    loss_row = jnp.zeros((1, LANES), F32) if loss is None else jnp.broadcast_to(loss.reshape(1, 1), (1, LANES))
    rows = []
    for (name, used, padded), p in zip(_SMALL_PARTS, list(parts) + [loss_row]):
        p = p.reshape(used, LANES)
        if padded > used:
            p = jnp.pad(p, ((0, padded - used), (0, 0)))
        rows.append(p)
    return jnp.concatenate(rows, axis=0)


def _local_step(x, mem, target, g_norm, w_in, w_s, b_s, g_v, g_mem, late_weights, g_final,
                fwd_token=None, on_dw=None):
    B, S, _ = x.shape
    x2d = x.reshape(B * S, D_MODEL)
    t2d = target.reshape(B * S, D_MODEL)
    mem2d = mem.reshape(B * N_MEM, D_MODEL)

    proj = _inproj_fwd(x2d, g_norm, w_in, after=() if fwd_token is None else (fwd_token,))
    w_kv, w_out = late_weights(proj)
    kv = _kv_fwd(mem2d, g_mem, w_kv)
    a, lse = _attn_fwd(proj, B, S)
    w_sT = jnp.swapaxes(w_s, 1, 2)
    b_tab = jnp.repeat(b_s.T, HEAD_DIM, axis=1)
    (dx2, da, drest, loss, d_wout, d_ws, d_bs, d_gv, d_gf, dkv) = _mid(
        x2d, t2d, a, proj, kv, w_s, w_sT, b_tab, g_v, w_out, g_final, B, S)
    d_wkv, d_gmem = _kv_bwd(mem2d, g_mem, w_kv, dkv)
    dq, dk, dv = _attn_bwd(proj, a, lse, da, B, S)
    if on_dw is None:
        d_win = _inproj_bwd_dw(dq, dk, dv, drest, x2d, g_norm)
        after = ()
    else:
        d_win = None
        by_chip = lambda g: g.reshape((N_CHIPS, g.shape[0] // N_CHIPS, g.shape[1]))
        after = (on_dw(*_inproj_bwd_dw(dq, dk, dv, drest, x2d, g_norm, reduce_with=[by_chip(d_wkv), by_chip(d_wout)])),)
    grad_x, d_gnorm = _inproj_bwd_dx(dq, dk, dv, drest, x2d, dx2, g_norm, w_in, after=after)
    d_bs = d_bs[:, :N_SGU_GROUPS].T
    return (loss[0, 0], grad_x.reshape(B, S, D_MODEL),
            dict(g_norm=d_gnorm, w_in=d_win, w_s=d_ws, b_s=d_bs, g_v=d_gv, g_mem=d_gmem, w_kv=d_wkv,
                 w_out=d_wout, g_final=d_gf))


def kernel(x, mem, g_norm, w_in, w_sgu_spatial, b_sgu_spatial, g_sgu_v, g_mem, w_mem_kv, w_out, g_final, loss_target, m_g_norm, m_w_in, m_w_sgu_spatial, m_b_sgu_spatial, m_g_sgu_v, m_g_mem, m_w_mem_kv, m_w_out, m_g_final, v_g_norm, v_w_in, v_w_sgu_spatial, v_b_sgu_spatial, v_g_sgu_v, v_g_mem, v_w_mem_kv, v_w_out, v_g_final):
    t = lambda w: jnp.swapaxes(w[0], 0, 1)
    (win_all,), late_shards, late_lands = _ag_weights([t(w_in)], [w_mem_kv[0], w_out[0]])
    w_in_full = win_all.reshape(-1, win_all.shape[-1])
    late = _exchange_start(True, list(late_shards), win_all, "gather_late_start", lands=late_lands)

    def late_weights(proj):
        return [z.reshape(-1, z.shape[-1]) for z in _exchange_wait(True, late, proj, "gather_late_wait")]

    scatter = {}

    def on_dw(sends, owns):
        scatter["own"] = owns
        scatter["started"] = _exchange_start(False, list(sends), owns[0], "scatter_start")
        return scatter["started"][-1]

    loss, grad_x, g = _local_step(
        x, mem, loss_target, g_norm, w_in_full, w_sgu_spatial[0], b_sgu_spatial[0], g_sgu_v, g_mem,
        late_weights, g_final.reshape(1, D_MODEL), fwd_token=late[-1], on_dw=on_dw)

    small_names = ("g_norm", "w_s", "b_s", "g_v", "g_mem", "g_final")
    g_small = _pack_small([g[n] for n in small_names], loss)
    landed = _exchange_wait(False, scatter["started"], g_small, "scatter_wait")
    gr_in, gr_kv, gr_out, gr_small = _reduce_last(scatter["own"], landed, g_small)
    loss = gr_small[_LOSS_ROW, 0]

    small_w = (g_norm, w_sgu_spatial, b_sgu_spatial, g_sgu_v, g_mem, g_final)
    small_m = (m_g_norm, m_w_sgu_spatial, m_b_sgu_spatial, m_g_sgu_v, m_g_mem, m_g_final)
    small_v = (v_g_norm, v_w_sgu_spatial, v_b_sgu_spatial, v_g_sgu_v, v_g_mem, v_g_final)
    rows = lambda ws: [w.reshape(-1, LANES) for w in ws]
    small = [[z.reshape(w.shape) for z in four]
             for w, four in zip(small_w, _adamw_small(gr_small, rows(small_w), rows(small_m), rows(small_v)))]
    d_in, nm_in, nv_in = _adamw(t(w_in), gr_in, t(m_w_in), t(v_w_in), "adamw_w_in")
    gr_in, d_in, nm_in, nv_in = [jnp.swapaxes(z, 0, 1) for z in (gr_in, d_in, nm_in, nv_in)]
    d_kv, nm_kv, nv_kv = _adamw(w_mem_kv[0], gr_kv, m_w_mem_kv[0], v_w_mem_kv[0], "adamw_w_kv")
    d_out, nm_out, nv_out = _adamw(w_out[0], gr_out, m_w_out[0], v_w_out[0], "adamw_w_out")

    def leaves(kind, big_in, big_kv, big_out):
        s_norm, s_ws, s_bs, s_gv, s_gmem, s_gf = [four[kind] for four in small]
        return [s_norm, big_in[None], s_ws, s_bs, s_gv, s_gmem, big_kv[None], big_out[None], s_gf]

    return (loss, grad_x, *leaves(0, gr_in, gr_kv, gr_out), *leaves(1, d_in, d_kv, d_out),
            *leaves(2, nm_in, nm_kv, nm_out), *leaves(3, nv_in, nv_kv, nv_out))
```

```python
import functools

import jax
import jax.numpy as jnp
from jax import lax
from jax.experimental import pallas as pl
from jax.experimental.pallas import tpu as pltpu

F32 = jnp.float32
BF16 = jnp.bfloat16
MESH = pl.DeviceIdType.MESH

D_MODEL = 1024
ATTN_WIDTH = 512
SGU_WIDTH = 256
MEM_WIDTH = 256
N_MEM = 256
IN_COLS = 3328
QKV_COLS = 3 * ATTN_WIDTH
REST_COLS = IN_COLS - QKV_COLS
SGU_CHUNK = 128
N_SGU_GROUPS = 4
EPS = 1e-6
NEG_INF = -1e30
DILATIONS = (1, 4, 16)
RADIUS = 64
Q_BLOCK = 128
LANES = 128
HEAD_DIM = 64

ADAM_LR = 0.001
ADAM_B1 = 0.9
ADAM_B2 = 0.999
ADAM_EPS = 1e-08
ADAM_WD = 0.01
ADAM_STEP = 10

N_CHIPS = 4
VMEM_LIMIT = 56 * 1024 * 1024
SMALL_ROWS = 560


def _params(sem=None, vmem=VMEM_LIMIT):
    return pltpu.CompilerParams(dimension_semantics=sem, vmem_limit_bytes=vmem)


def _nn(a, b):
    return jnp.dot(a, b, preferred_element_type=F32)


def _nt(a, b):
    return lax.dot_general(a, b, (((1,), (1,)), ((), ())), preferred_element_type=F32)


def _tn(a, b):
    return lax.dot_general(a, b, (((0,), (0,)), ((), ())), preferred_element_type=F32)


def _rms(x):
    r = lax.rsqrt(jnp.mean(x * x, axis=-1, keepdims=True) + EPS)
    return r, x * r


def _head_masks():
    lane = lax.broadcasted_iota(jnp.int32, (1, LANES), 1)
    lo = lane < HEAD_DIM
    return lo, (lo.astype(F32), (~lo).astype(F32))


def _silu_parts(z):
    s = jax.nn.sigmoid(z)
    return z * s, s * (1.0 + z * (1.0 - s))


def _gelu_parts(x):
    c = 0.7978845608028654
    x2 = x * x
    s = jax.nn.sigmoid((2.0 * c) * (x + 0.044715 * (x * x2)))
    return x * s, s * (1.0 + x * (1.0 - s) * ((2.0 * c) * (1.0 + 3.0 * 0.044715 * x2)))


def _after(tokens):
    return [pl.BlockSpec(memory_space=pl.ANY)] * len(tokens)


def _inproj_fwd(x2d, g_norm, w_in_t, after=()):
    T = x2d.shape[0]
    tm = 512

    def body(x_ref, g_ref, w_ref, *rest):
        o_ref = rest[-1]
        _, xh = _rms(x_ref[...])
        h = (xh * g_ref[...]).astype(BF16)
        o_ref[...] = _nt(h, w_ref[...])

    return pl.pallas_call(
        body, grid=(T // tm,),
        in_specs=[pl.BlockSpec((tm, D_MODEL), lambda i: (i, 0)),
                  pl.BlockSpec((1, D_MODEL), lambda i: (0, 0)),
                  pl.BlockSpec((IN_COLS, D_MODEL), lambda i: (0, 0))] + _after(after),
        out_specs=pl.BlockSpec((tm, IN_COLS), lambda i: (i, 0)),
        out_shape=jax.ShapeDtypeStruct((T, IN_COLS), F32),
        compiler_params=_params(("arbitrary",)), name="inproj_fwd")(x2d, g_norm, w_in_t, *after)


def _kv_fwd(mem2d, g_mem, w_kv):
    Tm = mem2d.shape[0]

    def body(m_ref, g_ref, w_ref, o_ref):
        _, mh = _rms(m_ref[...])
        o_ref[...] = _nn((mh * g_ref[...]).astype(BF16), w_ref[...])

    return pl.pallas_call(
        body, out_shape=jax.ShapeDtypeStruct((Tm, 2 * MEM_WIDTH), F32),
        compiler_params=_params(), name="kv_fwd")(mem2d, g_mem, w_kv)


def _kv_bwd(mem2d, g_mem, w_kv, dkv):
    Tm = mem2d.shape[0]

    def body(m_ref, g_ref, w_ref, dkv_ref, dw_ref, dg_ref):
        _, mh = _rms(m_ref[...])
        memn = (mh * g_ref[...]).astype(BF16)
        dkvb = dkv_ref[...].astype(BF16)
        dw_ref[...] = _tn(memn, dkvb)
        dmemn = _nt(dkvb, w_ref[...])
        dg_ref[...] = jnp.sum(dmemn * mh, axis=0, keepdims=True)

    return pl.pallas_call(
        body, out_shape=(jax.ShapeDtypeStruct((D_MODEL, 2 * MEM_WIDTH), F32),
                         jax.ShapeDtypeStruct((1, D_MODEL), F32)),
        compiler_params=_params(), name="kv_bwd")(mem2d, g_mem, w_kv, dkv)


def _attn_geometry(S):
    geom = []
    for d in DILATIONS:
        L = S // d
        assert L % Q_BLOCK == 0
        geom.append((d, L, min(2 * Q_BLOCK, L), L // Q_BLOCK))
    return geom


def _init_bias(bias_scr, geom, hp):
    row = lax.broadcasted_iota(jnp.int32, (Q_BLOCK, 2 * Q_BLOCK), 0)
    col = lax.broadcasted_iota(jnp.int32, (Q_BLOCK, 2 * Q_BLOCK), 1)
    for j in (0, 1):
        bits = (126 - (2 * hp + j)) * (1 << 23)
        slope = lax.bitcast_convert_type(jnp.full((1, 1), bits, jnp.int32), F32)
        for di, (d, _, _, _) in enumerate(geom):
            for cls, off in enumerate((0, -RADIUS, -2 * RADIUS)):
                dist = jnp.abs(col - row + off)
                bias_scr[di * 6 + cls * 2 + j] = jnp.where(
                    dist <= RADIUS, -(slope * float(d)) * dist.astype(F32), NEG_INF)


SPLIT = 4
COPY_ROWS = 256


def _by4_rows(S, step):
    per_class = S // SPLIT // COPY_ROWS
    r, j = step // per_class, step % per_class
    return (pl.ds(r + SPLIT * j * COPY_ROWS, COPY_ROWS, stride=SPLIT),
            pl.ds(pl.multiple_of(r * (S // SPLIT) + j * COPY_ROWS, COPY_ROWS), COPY_ROWS))


def _to_by4(src, dst, S):
    def step(i, carry):
        natural, by4 = _by4_rows(S, i)
        dst[by4, :] = src[natural, :]
        return carry
    lax.fori_loop(0, S // COPY_ROWS, step, 0)


def _block_slices(d, L, KW, nqb, r, qb, S):
    qs = qb * Q_BLOCK
    ks = jnp.clip(qs - RADIUS, 0, L - KW)
    cls = jnp.where(qb == 0, 0, jnp.where(qb == nqb - 1, 2, 1))
    if d == 1:
        qsl = pl.ds(pl.multiple_of(qs, Q_BLOCK), Q_BLOCK)
        ksl = pl.ds(pl.multiple_of(ks, RADIUS), KW)
    elif d == SPLIT:
        qsl = pl.ds(pl.multiple_of(r * L + qs, Q_BLOCK), Q_BLOCK)
        ksl = pl.ds(pl.multiple_of(r * L + ks, RADIUS), KW)
    else:
        sub = d // SPLIT
        base = (r % SPLIT) * (S // SPLIT) + r // SPLIT
        qsl = pl.ds(base + qs * sub, Q_BLOCK, stride=sub)
        ksl = pl.ds(base + ks * sub, KW, stride=sub)
    return qsl, ksl, cls


def _for_groups(geom, S, group, fn):
    for di, (d, L, KW, nqb) in enumerate(geom):
        n = group[di]
        assert (d * nqb) % n == 0

        def step(it, carry, di=di, d=d, L=L, KW=KW, nqb=nqb, n=n):
            slices = []
            for g in range(n):
                i = it * n + g
                slices.append(_block_slices(d, L, KW, nqb, i // nqb, i % nqb, S))
            fn(di, KW, slices)
            return carry
        lax.fori_loop(0, d * nqb // n, step, 0)


def _attn_fwd(proj, B, S):
    T = B * S
    geom = _attn_geometry(S)
    n_pairs = ATTN_WIDTH // LANES

    def body(q_ref, k_ref, v_ref, a_ref, lse_ref, bias_scr, q4, k4, v4, *per_dilation):
        o_scr, m_scr, l_scr = per_dilation[0:3], per_dilation[3:6], per_dilation[6:9]
        lo, hm = _head_masks()
        pair = pl.program_id(0)

        @pl.when(pl.program_id(1) == 0)
        def _():
            _init_bias(bias_scr, geom, pair)
        for src, dst in ((q_ref, q4), (k_ref, k4), (v_ref, v4)):
            _to_by4(src, dst, S)

        def group(di, KW, slices):
            chains = [(g, j) for g in range(len(slices)) for j in (0, 1)]
            q_src, k_src, v_src = (q_ref, k_ref, v_ref) if di == 0 else (q4, k4, v4)
            q = [q_src[qsl, :] for qsl, _, _ in slices]
            kw = [k_src[ksl, :].astype(BF16) for _, ksl, _ in slices]
            vw = [v_src[ksl, :].astype(BF16) for _, ksl, _ in slices]
            s = {(g, j): _nt((q[g] * (hm[j] * 0.125)).astype(BF16), kw[g])
                 + bias_scr[di * 6 + slices[g][2] * 2 + j, :, pl.ds(0, KW)] for g, j in chains}
            m = {c: jnp.max(s[c], axis=1, keepdims=True) for c in chains}
            p = {c: jnp.exp(s[c] - m[c]) for c in chains}
            l = {c: jnp.sum(p[c], axis=1, keepdims=True) for c in chains}
            o = {(g, j): _nn(p[(g, j)].astype(BF16), vw[g]) for g, j in chains}
            for g, (qsl, _, _) in enumerate(slices):
                o_scr[di][qsl, :] = jnp.where(lo, o[(g, 0)], o[(g, 1)])
                m_scr[di][qsl, :] = jnp.where(lo, m[(g, 0)], m[(g, 1)])
                l_scr[di][qsl, :] = jnp.where(lo, l[(g, 0)], l[(g, 1)])

        _for_groups(geom, S, (16, 16, 16), group)

        def combine(i, carry):
            natural, by4 = _by4_rows(S, i)
            rows = [natural, by4, by4]
            ms = [m_scr[di][rows[di], :] for di in range(3)]
            mx = jnp.maximum(jnp.maximum(ms[0], ms[1]), ms[2])
            num = 0.0
            den = 0.0
            for di in range(3):
                w = jnp.exp(ms[di] - mx)
                num = num + w * o_scr[di][rows[di], :]
                den = den + w * l_scr[di][rows[di], :]
            a_ref[natural, :] = num / den
            lse_ref[natural, :] = mx + jnp.log(den)
            return carry

        lax.fori_loop(0, S // COPY_ROWS, combine, 0)

    blk = lambda off: pl.BlockSpec((S, LANES), lambda h, b, off=off: (b, off + h))
    out_blk = pl.BlockSpec((S, LANES), lambda h, b: (b, h))
    return pl.pallas_call(
        body, grid=(n_pairs, B),
        in_specs=[blk(0), blk(n_pairs), blk(2 * n_pairs)],
        out_specs=[out_blk, out_blk],
        out_shape=[jax.ShapeDtypeStruct((T, ATTN_WIDTH), F32)] * 2,
        scratch_shapes=[pltpu.VMEM((18, Q_BLOCK, 2 * Q_BLOCK), F32)] + [pltpu.VMEM((S, LANES), F32)] * 12,
        compiler_params=_params(("arbitrary", "arbitrary")), name="attn_fwd")(proj, proj, proj)


def _attn_bwd(proj, a, lse, da, B, S):
    T = B * S
    geom = _attn_geometry(S)
    n_pairs = ATTN_WIDTH // LANES

    def body(q_ref, k_ref, v_ref, a_ref, lse_ref, do_ref, dq_ref, dk_ref, dv_ref, bias_scr, *scr):
        acc = (scr[0:3], scr[3:6])
        natural_in = (q_ref, k_ref, v_ref, a_ref, lse_ref, do_ref)
        by4_in = scr[6:12]
        _, hm = _head_masks()
        pair = pl.program_id(0)

        @pl.when(pl.program_id(1) == 0)
        def _():
            _init_bias(bias_scr, geom, pair)
        for ref in scr[0:6]:
            ref[...] = jnp.zeros_like(ref)
        for src, dst in zip(natural_in, by4_in):
            _to_by4(src, dst, S)

        def group(di, KW, slices):
            n = len(slices)
            chains = [(g, j) for g in range(n) for j in (0, 1)]
            q_src, k_src, v_src, a_src, lse_src, do_src = natural_in if di == 0 else by4_in
            dq_scr, dk_scr, dv_scr = acc[0 if di == 0 else 1]
            q = [q_src[qsl, :] for qsl, _, _ in slices]
            do = [do_src[qsl, :] for qsl, _, _ in slices]
            doa = [do[g] * a_src[slices[g][0], :] for g in range(n)]
            lse_q = [lse_src[qsl, :] for qsl, _, _ in slices]
            kw = [k_src[ksl, :].astype(BF16) for _, ksl, _ in slices]
            vw = [v_src[ksl, :].astype(BF16) for _, ksl, _ in slices]
            qj = {(g, j): (q[g] * (hm[j] * 0.125)).astype(BF16) for g, j in chains}
            doj = {(g, j): (do[g] * hm[j]).astype(BF16) for g, j in chains}
            s = {(g, j): _nt(qj[(g, j)], kw[g])
                 + bias_scr[di * 6 + slices[g][2] * 2 + j, :, pl.ds(0, KW)] for g, j in chains}
            dp = {(g, j): _nt(doj[(g, j)], vw[g]) for g, j in chains}
            dsum = {(g, j): jnp.sum(doa[g] * hm[j], axis=1, keepdims=True) for g, j in chains}
            p = {(g, j): jnp.exp(s[(g, j)] - lse_q[g][:, HEAD_DIM * j:HEAD_DIM * j + 1]) for g, j in chains}
            ds = {c: (p[c] * (dp[c] - dsum[c])).astype(BF16) for c in chains}
            pb = {c: p[c].astype(BF16) for c in chains}
            dq = [_nn(ds[(g, 0)], kw[g]) * (hm[0] * 0.125) + _nn(ds[(g, 1)], kw[g]) * (hm[1] * 0.125)
                  for g in range(n)]
            both = lambda t, g: jnp.concatenate([t[(g, 0)], t[(g, 1)]], axis=0)
            dkw = [_tn(both(ds, g), both(qj, g)) for g in range(n)]
            dvw = [_tn(both(pb, g), both(doj, g)) for g in range(n)]
            for g, (qsl, ksl, _) in enumerate(slices):
                dq_scr[qsl, :] = dq_scr[qsl, :] + dq[g]
                dk_scr[ksl, :] = dk_scr[ksl, :] + dkw[g]
                dv_scr[ksl, :] = dv_scr[ksl, :] + dvw[g]

        _for_groups(geom, S, (4, 4, 16), group)

        def merge(i, carry):
            natural, by4 = _by4_rows(S, i)
            for nat, split in zip(*acc):
                nat[natural, :] = nat[natural, :] + split[by4, :]
            return carry
        lax.fori_loop(0, S // COPY_ROWS, merge, 0)
        for out, nat in zip((dq_ref, dk_ref, dv_ref), acc[0]):
            out[...] = nat[...].astype(BF16)

    blk = lambda off: pl.BlockSpec((S, LANES), lambda h, b, off=off: (b, off + h))
    return pl.pallas_call(
        body, grid=(n_pairs, B),
        in_specs=[blk(0), blk(n_pairs), blk(2 * n_pairs), blk(0), blk(0), blk(0)],
        out_specs=[blk(0), blk(0), blk(0)],
        out_shape=[jax.ShapeDtypeStruct((T, ATTN_WIDTH), BF16)] * 3,
        scratch_shapes=[pltpu.VMEM((18, Q_BLOCK, 2 * Q_BLOCK), F32)] + [pltpu.VMEM((S, LANES), F32)] * 12,
        compiler_params=_params(("arbitrary", "arbitrary")), name="attn_bwd")(proj, proj, proj, a, lse, da)


def _mid(x2d, t2d, a, proj, kv, w_s, w_sT, b_tab, g_v, w_out, g_final, B, S):
    T = B * S
    tm = 512
    nt = S // tm
    halves = 2
    hrows = tm // halves

    def body(x_ref, t_ref, a_ref, za_ref, ub_ref, vb_ref, zb_ref, qm_ref, zm_ref, kv_ref,
              ws_ref, wsT_ref, btab_ref, gv_ref, wout_ref, gf_ref,
              dx2_ref, da_ref, drest_ref, loss_ref, dwout_ref, dws_ref, dbs_ref, dgv_ref, dgf_ref, dkv_ref,
              dbtab_scr):
        b = pl.program_id(0)
        t = pl.program_id(1)
        first = jnp.logical_and(b == 0, t == 0)
        last = jnp.logical_and(b == B - 1, t == nt - 1)
        _, hm = _head_masks()
        lane_g = lax.broadcasted_iota(jnp.int32, (1, SGU_WIDTH), 1) // HEAD_DIM
        gm = [(lane_g == g).astype(F32) for g in range(N_SGU_GROUPS)]
        H = range(halves)
        rows = [pl.ds(h * hrows, hrows) for h in H]
        ld = lambda ref: [ref[r, :] for r in rows]
        cat = lambda parts, axis: jnp.concatenate(parts, axis=axis)
        chunks = [slice(ci * SGU_CHUNK, (ci + 1) * SGU_CHUNK) for ci in range(hrows // SGU_CHUNK)]
        pairs = [slice(pr * LANES, (pr + 1) * LANES) for pr in range(2)]
        heads = [(pr, j) for pr in range(2) for j in (0, 1)]

        @pl.when(first)
        def _():
            loss_ref[...] = jnp.zeros_like(loss_ref)
            dwout_ref[...] = jnp.zeros_like(dwout_ref)
            dws_ref[...] = jnp.zeros_like(dws_ref)
            dbs_ref[...] = jnp.zeros_like(dbs_ref)
            dgv_ref[...] = jnp.zeros_like(dgv_ref)
            dgf_ref[...] = jnp.zeros_like(dgf_ref)
            dbtab_scr[...] = jnp.zeros_like(dbtab_scr)

        @pl.when(t == 0)
        def _():
            dkv_ref[...] = jnp.zeros_like(dkv_ref)

        a_val = ld(a_ref)
        sil_a = [_silu_parts(z) for z in ld(za_ref)]
        gated_a = [s[0] * a for s, a in zip(sil_a, a_val)]
        u = [_gelu_parts(z) for z in ld(ub_ref)]
        vv = [_gelu_parts(z) for z in ld(vb_ref)]
        vnorm = [_rms(v[0]) for v in vv]
        gv = gv_ref[...]
        vn = [(n[1] * gv).astype(BF16) for n in vnorm]
        w_cat = cat([ws_ref[g].astype(BF16) for g in range(N_SGU_GROUPS)], 1)
        wT_cat = cat([wsT_ref[g].astype(BF16) for g in range(N_SGU_GROUPS)], 1)
        gmb = [m.astype(BF16) for m in gm]
        by_group = lambda chunk: cat([chunk * gmb[g] for g in range(N_SGU_GROUPS)], 0)
        btab = btab_ref[...]
        mixed = [cat([btab + _nn(w_cat, by_group(vn[h][c, :])) for c in chunks], 0) for h in H]
        sg = [u[h][0] * mixed[h] for h in H]
        sil_b = [_silu_parts(z) for z in ld(zb_ref)]
        gated_b = [sil_b[h][0] * sg[h] for h in H]

        kvv = kv_ref[...].astype(BF16)
        kp = [kvv[:, p] for p in pairs]
        vp = [kvv[:, MEM_WIDTH + pr * LANES:MEM_WIDTH + (pr + 1) * LANES] for pr in range(2)]
        qm = ld(qm_ref)
        qj = {(h, pr, j): (qm[h][:, pairs[pr]] * (hm[j] * 0.125)).astype(BF16) for h in H for pr, j in heads}
        sc = {k: _nt(qj[k], kp[k[1]]) for k in qj}
        ex = {k: jnp.exp(sc[k] - jnp.max(sc[k], axis=1, keepdims=True)) for k in qj}
        prob = {k: ex[k] * (1.0 / jnp.sum(ex[k], axis=1, keepdims=True)) for k in qj}
        probb = {k: prob[k].astype(BF16) for k in qj}
        mo = [cat([sum(_nn(probb[(h, pr, j)], vp[pr]) * hm[j] for j in (0, 1)) for pr in range(2)], 1) for h in H]
        sil_m = [_silu_parts(z) for z in ld(zm_ref)]
        gated_m = [sil_m[h][0] * mo[h] for h in H]

        gated = [cat([gated_a[h], gated_b[h], gated_m[h]], 1).astype(BF16) for h in H]
        wout = wout_ref[...]
        x_in = ld(x_ref)
        x2 = [x_in[h] + _nn(gated[h], wout) for h in H]
        fin = [_rms(z) for z in x2]
        gf = gf_ref[...]
        tgt = ld(t_ref)
        err = [fin[h][1] * gf - tgt[h] for h in H]
        loss_ref[...] += sum(jnp.sum(e * e) for e in err) * (0.5 / D_MODEL)

        dy = [e * (1.0 / D_MODEL) for e in err]
        dgf_ref[...] += sum(jnp.sum(dy[h] * fin[h][1], axis=0, keepdims=True) for h in H)
        gdy = [d * gf for d in dy]
        dx2 = [fin[h][0] * (gdy[h] - fin[h][1] * jnp.mean(gdy[h] * fin[h][1], axis=1, keepdims=True)) for h in H]
        for h in H:
            dx2_ref[rows[h], :] = dx2[h]
        dx2b = [d.astype(BF16) for d in dx2]
        dgated = [_nt(d, wout) for d in dx2b]
        dwout_ref[...] += _tn(cat(gated, 0), cat(dx2b, 0))
        dga = [d[:, 0:ATTN_WIDTH] for d in dgated]
        dgb = [d[:, ATTN_WIDTH:ATTN_WIDTH + SGU_WIDTH] for d in dgated]
        dgm = [d[:, ATTN_WIDTH + SGU_WIDTH:] for d in dgated]

        for h in H:
            da_ref[rows[h], :] = dga[h] * sil_a[h][0]
        dza = [dga[h] * a_val[h] * sil_a[h][1] for h in H]

        dsg = [dgb[h] * sil_b[h][0] for h in H]
        dzb = [dgb[h] * sg[h] * sil_b[h][1] for h in H]
        dub = [dsg[h] * mixed[h] * u[h][1] for h in H]
        dmixed = [dsg[h] * u[h][0] for h in H]
        dmixed_b = [d.astype(BF16) for d in dmixed]
        dvn = [cat([_nn(wT_cat, by_group(dmixed_b[h][c, :])) for c in chunks], 0) for h in H]
        for g in range(N_SGU_GROUPS):
            dws_ref[g] += sum(_nt((dmixed[h][c, :] * gm[g]).astype(BF16), vn[h][c, :]) for h in H for c in chunks)
        dbtab_scr[...] += sum(dmixed[h][c, :] for h in H for c in chunks)
        dgv_ref[...] += sum(jnp.sum(dvn[h] * vnorm[h][1], axis=0, keepdims=True) for h in H)
        tv = [d * gv for d in dvn]
        dvv = [vnorm[h][0] * (tv[h] - vnorm[h][1] * jnp.mean(tv[h] * vnorm[h][1], axis=1, keepdims=True)) for h in H]
        dvb = [dvv[h] * vv[h][1] for h in H]

        dmo = [dgm[h] * sil_m[h][0] for h in H]
        dzm = [dgm[h] * mo[h] * sil_m[h][1] for h in H]
        dmoj = {(h, pr, j): (dmo[h][:, pairs[pr]] * hm[j]).astype(BF16) for h in H for pr, j in heads}
        dp = {k: _nt(dmoj[k], vp[k[1]]) for k in qj}
        ds = {k: (prob[k] * (dp[k] - jnp.sum(dp[k] * prob[k], axis=1, keepdims=True))).astype(BF16) for k in qj}
        dqm = [cat([sum(_nn(ds[(h, pr, j)], kp[pr]) * (hm[j] * 0.125) for j in (0, 1)) for pr in range(2)], 1)
               for h in H]
        every = lambda tbl, pr: cat([tbl[(h, pr, j)] for h in H for j in (0, 1)], 0)
        dk = [_tn(every(ds, pr), every(qj, pr)) for pr in range(2)]
        dv = [_tn(every(probb, pr), every(dmoj, pr)) for pr in range(2)]
        dkv_ref[...] += cat(dk + dv, 1)

        for h in H:
            drest_ref[rows[h], :] = cat([dza[h], dub[h], dvb[h], dzb[h], dqm[h], dzm[h]], 1).astype(BF16)

        @pl.when(last)
        def _():
            lane = lax.broadcasted_iota(jnp.int32, (1, LANES), 1)
            dbt = dbtab_scr[...]
            out = jnp.zeros((SGU_CHUNK, LANES), F32)
            for g in range(N_SGU_GROUPS):
                out = out + jnp.where(lane == g, jnp.sum(dbt * gm[g], axis=1, keepdims=True), 0.0)
            dbs_ref[...] = out

    tile = lambda w, cb: pl.BlockSpec((tm, w), lambda b, t, cb=cb: (b * nt + t, cb))
    const = lambda shape: pl.BlockSpec(shape, lambda b, t, n=len(shape): (0,) * n)
    return pl.pallas_call(
        body, grid=(B, nt),
        in_specs=[tile(D_MODEL, 0), tile(D_MODEL, 0), tile(ATTN_WIDTH, 0),
                  tile(ATTN_WIDTH, 3),
                  tile(SGU_WIDTH, 8), tile(SGU_WIDTH, 9), tile(SGU_WIDTH, 10),
                  tile(MEM_WIDTH, 11), tile(MEM_WIDTH, 12),
                  pl.BlockSpec((N_MEM, 2 * MEM_WIDTH), lambda b, t: (b, 0)),
                  const((N_SGU_GROUPS, SGU_CHUNK, SGU_CHUNK)), const((N_SGU_GROUPS, SGU_CHUNK, SGU_CHUNK)),
                  const((SGU_CHUNK, SGU_WIDTH)), const((1, SGU_WIDTH)),
                  const((D_MODEL, D_MODEL)), const((1, D_MODEL))],
        out_specs=[tile(D_MODEL, 0), tile(ATTN_WIDTH, 0), tile(REST_COLS, 0),
                   const((8, LANES)), const((D_MODEL, D_MODEL)),
                   const((N_SGU_GROUPS, SGU_CHUNK, SGU_CHUNK)), const((SGU_CHUNK, LANES)),
                   const((1, SGU_WIDTH)), const((1, D_MODEL)),
                   pl.BlockSpec((N_MEM, 2 * MEM_WIDTH), lambda b, t: (b, 0))],
        out_shape=[jax.ShapeDtypeStruct((T, D_MODEL), F32), jax.ShapeDtypeStruct((T, ATTN_WIDTH), F32),
                   jax.ShapeDtypeStruct((T, REST_COLS), BF16),
                   jax.ShapeDtypeStruct((8, LANES), F32), jax.ShapeDtypeStruct((D_MODEL, D_MODEL), F32),
                   jax.ShapeDtypeStruct((N_SGU_GROUPS, SGU_CHUNK, SGU_CHUNK), F32),
                   jax.ShapeDtypeStruct((SGU_CHUNK, LANES), F32),
                   jax.ShapeDtypeStruct((1, SGU_WIDTH), F32), jax.ShapeDtypeStruct((1, D_MODEL), F32),
                   jax.ShapeDtypeStruct((B * N_MEM, 2 * MEM_WIDTH), F32)],
        scratch_shapes=[pltpu.VMEM((SGU_CHUNK, SGU_WIDTH), F32)],
        compiler_params=_params(("arbitrary", "arbitrary")), name="mid")(
            x2d, t2d, a, proj, proj, proj, proj, proj, proj, kv, w_s, w_sT, b_tab, g_v, w_out, g_final)


def _inproj_bwd_dx(dq, dk, dv, drest, x2d, dx2, g_norm, w_in_t, after=()):
    T = x2d.shape[0]
    tm = 512
    W = ATTN_WIDTH

    def body(dq_ref, dk_ref, dv_ref, dr_ref, x_ref, dx2_ref, g_ref, w_ref, *rest):
        gx_ref, dg_ref = rest[-2:]

        @pl.when(pl.program_id(0) == 0)
        def _():
            dg_ref[...] = jnp.zeros_like(dg_ref)

        halves = [pl.ds(h * (tm // 2), tm // 2) for h in (0, 1)]
        dh = [(_nn(dq_ref[r, :], w_ref[0:W, :]) + _nn(dk_ref[r, :], w_ref[W:2 * W, :])
               + _nn(dv_ref[r, :], w_ref[2 * W:3 * W, :]) + _nn(dr_ref[r, :], w_ref[QKV_COLS:IN_COLS, :]))
              for r in halves]
        nrm = [_rms(x_ref[r, :]) for r in halves]
        dg_ref[...] += sum(jnp.sum(d * n[1], axis=0, keepdims=True) for d, n in zip(dh, nrm))
        g = g_ref[...]
        for r, d, (rstd, xh) in zip(halves, dh, nrm):
            th = d * g
            gx_ref[r, :] = rstd * (th - xh * jnp.mean(th * xh, axis=1, keepdims=True)) + dx2_ref[r, :]

    tile = lambda w: pl.BlockSpec((tm, w), lambda i: (i, 0))
    return pl.pallas_call(
        body, grid=(T // tm,),
        in_specs=[tile(W), tile(W), tile(W), tile(REST_COLS), tile(D_MODEL), tile(D_MODEL),
                  pl.BlockSpec((1, D_MODEL), lambda i: (0, 0)),
                  pl.BlockSpec((IN_COLS, D_MODEL), lambda i: (0, 0))] + _after(after),
        out_specs=[tile(D_MODEL), pl.BlockSpec((1, D_MODEL), lambda i: (0, 0))],
        out_shape=[jax.ShapeDtypeStruct((T, D_MODEL), F32), jax.ShapeDtypeStruct((1, D_MODEL), F32)],
        compiler_params=_params(("arbitrary",)), name="inproj_bwd_dx")(
            dq, dk, dv, drest, x2d, dx2, g_norm, w_in_t, *after)


def _inproj_bwd_dw(dq, dk, dv, drest, x2d, g_norm, reduce_with=None):
    T = x2d.shape[0]
    tm = 512
    nt = T // tm
    W = ATTN_WIDTH
    fused = reduce_with is not None
    others = list(reduce_with) if fused else []
    ns = 1 + len(others)
    shard = IN_COLS // N_CHIPS
    halves = [shard // 2] + [s.shape[1] // 2 for s in others]
    cols = [D_MODEL] + [s.shape[2] for s in others]
    row_block = 32

    def body(dq_ref, dk_ref, dv_ref, dr_ref, x_ref, g_ref, *rest):
        if fused:
            stacks = rest[:ns - 1]
            sends, owns = rest[ns - 1:2 * ns - 1], rest[2 * ns - 1:3 * ns - 1]
            acc, ras, narrow = rest[3 * ns - 1], rest[3 * ns:4 * ns], rest[4 * ns]
            s_sem, r_sem = rest[4 * ns + 1], rest[4 * ns + 2]
            x, y, c, chip, peers, peer_chip = _place()
            sib = (x, y, 1 - c)

            def part(w, k, cc, r0=0, rows=None):
                n = halves[w]
                rows = n if rows is None else rows
                if w == 0:
                    return acc.at[pl.ds(pl.multiple_of(k * shard + cc * n + r0, 8), rows), :]
                return stacks[w - 1].at[k, pl.ds(pl.multiple_of(cc * n + r0, 8), rows), :]

            def swap_other(w):
                theirs = stacks[w - 1].at[:, pl.ds(pl.multiple_of((1 - c) * halves[w], 8), halves[w]), :]
                return _remote(theirs, ras[w], s_sem.at[N_CHIPS - 1 + w], r_sem.at[N_CHIPS - 1 + w], sib)

            def swap_win(k):
                return _remote(narrow.at[k], ras[0].at[k], s_sem.at[k], r_sem.at[k], sib)
        else:
            acc = rest[0]

        @pl.when(pl.program_id(0) == 0)
        def _():
            acc[...] = jnp.zeros_like(acc)
            for w in range(1, ns):
                swap_other(w).start()

        _, xh = _rms(x_ref[...])
        h = (xh * g_ref[...]).astype(BF16)
        acc[0:W, :] += _tn(dq_ref[...], h)
        acc[W:2 * W, :] += _tn(dk_ref[...], h)
        acc[2 * W:3 * W, :] += _tn(dv_ref[...], h)
        acc[QKV_COLS:IN_COLS, :] += _tn(dr_ref[...], h)

        if fused:
            @pl.when(pl.program_id(0) == nt - 1)
            def _():
                for k in range(N_CHIPS):
                    def to_bf16(i, carry, k=k):
                        r0 = pl.multiple_of(i * row_block, row_block)
                        narrow[k, pl.ds(r0, row_block), :] = part(0, k, 1 - c, r0, row_block)[...].astype(BF16)
                        return carry
                    lax.fori_loop(0, halves[0] // row_block, to_bf16, 0)
                    swap_win(k).start()
                for w in list(range(1, ns)) + [0]:
                    if w == 0:
                        for k in range(N_CHIPS):
                            swap_win(k).wait_recv()
                    else:
                        swap_other(w).wait_recv()

                    def sums(i, carry, w=w):
                        r0 = pl.multiple_of(i * row_block, row_block)
                        blk = pl.ds(r0, row_block)
                        for m in range(3):
                            k = peer_chip[m]
                            sends[w][m, blk, :] = (part(w, k, c, r0, row_block)[...]
                                                   + ras[w][k, blk, :].astype(F32)).astype(BF16)
                        owns[w][blk, :] = part(w, chip, c, r0, row_block)[...] + ras[w][chip, blk, :].astype(F32)
                        return carry
                    lax.fori_loop(0, halves[w] // row_block, sums, 0)
                for k in range(N_CHIPS):
                    swap_win(k).wait_send()
                for w in range(1, ns):
                    swap_other(w).wait_send()

    tile = lambda w: pl.BlockSpec((tm, w), lambda i: (i, 0))
    vmem = pl.BlockSpec(memory_space=pltpu.VMEM)
    in_specs = [tile(W), tile(W), tile(W), tile(REST_COLS), tile(D_MODEL), pl.BlockSpec((1, D_MODEL), lambda i: (0, 0))]
    if not fused:
        return pl.pallas_call(
            body, grid=(nt,), in_specs=in_specs,
            out_specs=pl.BlockSpec((IN_COLS, D_MODEL), lambda i: (0, 0)),
            out_shape=jax.ShapeDtypeStruct((IN_COLS, D_MODEL), F32),
            compiler_params=_params(("arbitrary",)), name="inproj_bwd_dw")(dq, dk, dv, drest, x2d, g_norm)
    outs = pl.pallas_call(
        body, grid=(nt,), in_specs=in_specs + [vmem] * (ns - 1), out_specs=[vmem] * (2 * ns),
        out_shape=[jax.ShapeDtypeStruct((3, n, cl), BF16) for n, cl in zip(halves, cols)]
        + [jax.ShapeDtypeStruct((n, cl), F32) for n, cl in zip(halves, cols)],
        scratch_shapes=[pltpu.VMEM((IN_COLS, D_MODEL), F32)]
        + [pltpu.VMEM((N_CHIPS, n, cl), BF16 if w == 0 else F32) for w, (n, cl) in enumerate(zip(halves, cols))]
        + [pltpu.VMEM((N_CHIPS, halves[0], D_MODEL), BF16)]
        + [pltpu.SemaphoreType.DMA((N_CHIPS - 1 + ns,)), pltpu.SemaphoreType.DMA((N_CHIPS - 1 + ns,))],
        compiler_params=_params(("arbitrary",)), name="inproj_bwd_dw_reduce")(
            dq, dk, dv, drest, x2d, g_norm, *others)
    return outs[:ns], outs[ns:]


def _adamw_update(w, g, m, v):
    nm = ADAM_B1 * m + (1.0 - ADAM_B1) * g
    nv = ADAM_B2 * v + (1.0 - ADAM_B2) * (g * g)
    m_hat = nm / (1.0 - ADAM_B1 ** ADAM_STEP)
    v_hat = nv / (1.0 - ADAM_B2 ** ADAM_STEP)
    return -ADAM_LR * (m_hat / (jnp.sqrt(v_hat) + ADAM_EPS) + ADAM_WD * w), nm, nv


def _adamw(w, g, m, v, name):
    R, C = w.shape
    br = max(r for r in range(8, 257, 8) if R % r == 0)

    def body(w_ref, g_ref, m_ref, v_ref, d_ref, nm_ref, nv_ref):
        d_ref[...], nm_ref[...], nv_ref[...] = _adamw_update(w_ref[...], g_ref[...], m_ref[...], v_ref[...])

    spec = pl.BlockSpec((br, C), lambda i: (i, 0))
    return pl.pallas_call(
        body, grid=(R // br,), in_specs=[spec] * 4, out_specs=[spec] * 3,
        out_shape=[jax.ShapeDtypeStruct((R, C), F32)] * 3,
        compiler_params=_params(("arbitrary",)), name=name)(w, g, m, v)


def _adamw_small(g_packed, ws, ms, vs):
    n = len(ws)

    def body(*refs):
        g_ref = refs[0]
        w_refs, m_refs, v_refs = refs[1:1 + n], refs[1 + n:1 + 2 * n], refs[1 + 2 * n:1 + 3 * n]
        outs = refs[1 + 3 * n:]
        off = 0
        for i, (_, used, padded) in enumerate(_SMALL_PARTS[:n]):
            g = g_ref[off:off + used, :]
            delta, nm, nv = _adamw_update(w_refs[i][...], g, m_refs[i][...], v_refs[i][...])
            outs[4 * i][...], outs[4 * i + 1][...], outs[4 * i + 2][...], outs[4 * i + 3][...] = g, delta, nm, nv
            off += padded
        outs[4 * n][...] = g_ref[_LOSS_ROW:_LOSS_ROW + 1, 0:1]

    outs = pl.pallas_call(
        body, out_shape=[jax.ShapeDtypeStruct(w.shape, F32) for w in ws for _ in range(4)]
        + [jax.ShapeDtypeStruct((1, 1), F32)],
        compiler_params=_params(), name="adamw_small")(g_packed, *ws, *ms, *vs)
    return [outs[4 * i:4 * i + 4] for i in range(n)], outs[4 * n]


def _place():
    x, y, c = lax.axis_index("x"), lax.axis_index("y"), lax.axis_index("c")
    chip = 2 * x + y
    peers = [(x, 1 - y), (1 - x, y), (1 - x, 1 - y)]
    peer_chip = [2 * px + py for px, py in peers]
    return x, y, c, chip, peers, peer_chip


def _remote(src, dst, send_sem, recv_sem, dev):
    return pltpu.make_async_remote_copy(src_ref=src, dst_ref=dst, send_sem=send_sem, recv_sem=recv_sem,
                                        device_id=dev, device_id_type=MESH)


def _ag_weights(weights, late=()):
    nw, nl = len(weights), len(late)
    pieces = 2

    def body(*refs):
        srcs, late_srcs = refs[:nw], refs[nw:nw + nl]
        outs, late_bf, late_land = (refs[nw + nl:2 * nw + nl], refs[2 * nw + nl:2 * nw + 2 * nl],
                                    refs[2 * nw + 2 * nl:2 * nw + 3 * nl])
        s_ici, r_ici, s_d2d, r_d2d = refs[2 * nw + 3 * nl:]
        x, y, c = lax.axis_index("x"), lax.axis_index("y"), lax.axis_index("c")
        chip = 2 * x + y
        sib = (x, y, 1 - c)
        first = ((x + 1 - c) % 2, (y + c) % 2)
        second = ((x + c) % 2, (y + 1 - c) % 2)
        first_chip, second_chip = 2 * first[0] + first[1], 2 * second[0] + second[1]
        diag_chip = 3 - chip
        for src, out in zip(srcs, outs):
            out[chip] = src[...].astype(BF16)

        parts = [(w, out, pc) for w, out in enumerate(outs) for pc in range(pieces)]

        def piece(out, k, cc, pc):
            rows = out.shape[1] // 2 // pieces
            return out.at[k, pl.ds(pl.multiple_of((cc * pieces + pc) * rows, 16), rows), :]

        def ici(w, slot, out, k, dev, pc):
            blk, sem = piece(out, k, c, pc), (nw * slot + w) * pieces + pc
            return _remote(blk, blk, s_ici.at[sem], r_ici.at[sem], (dev[0], dev[1], c))

        def d2d(w, slot, out, k, cc, pc):
            blk, sem = piece(out, k, cc, pc), (nw * slot + w) * pieces + pc
            return _remote(blk, blk, s_d2d.at[sem], r_d2d.at[sem], sib)

        sent = []
        for slot, dev in enumerate((first, second)):
            for w, out, pc in parts:
                sent.append(ici(w, slot, out, chip, dev, pc))
                sent[-1].start()
        for src, bf, land in zip(late_srcs, late_bf, late_land):
            bf[...] = src[...].astype(BF16)
            land[...] = jnp.zeros_like(land)
            land[chip] = bf[...]
        for slot, k, dev in ((0, first_chip, first), (1, second_chip, second), (2, diag_chip, second)):
            for w, out, pc in parts:
                ici(w, slot, out, k, dev, pc).wait_recv()
                if slot == 0:
                    sent.append(ici(w, 2, out, k, second, pc))
                    sent[-1].start()
                sent.append(d2d(w, slot, out, k, c, pc))
                sent[-1].start()
        for slot, k in ((0, second_chip), (1, first_chip), (2, diag_chip)):
            for w, out, pc in parts:
                d2d(w, slot, out, k, 1 - c, pc).wait_recv()
        for cp in sent:
            cp.wait_send()

    vmem = pl.BlockSpec(memory_space=pltpu.VMEM)
    outs = pl.pallas_call(
        body,
        out_shape=[jax.ShapeDtypeStruct((N_CHIPS,) + w.shape, BF16) for w in weights]
        + [jax.ShapeDtypeStruct(w.shape, BF16) for w in late]
        + [jax.ShapeDtypeStruct((N_CHIPS,) + w.shape, BF16) for w in late],
        in_specs=[vmem] * (nw + nl), out_specs=[vmem] * (nw + 2 * nl),
        scratch_shapes=[pltpu.SemaphoreType.DMA((3 * nw * pieces,))] * 4,
        compiler_params=pltpu.CompilerParams(vmem_limit_bytes=VMEM_LIMIT), name="ag_weights")(*weights, *late)
    return outs[:nw], outs[nw:nw + nl], outs[nw + nl:]


_HBM = pl.BlockSpec(memory_space=pltpu.HBM)
_SEM = pl.BlockSpec(memory_space=pltpu.SEMAPHORE)
_ANY = pl.BlockSpec(memory_space=pl.ANY)
_DATAFLOW = pltpu.SideEffectType.DATAFLOW_SIDE_EFFECTING


def _in_hbm(a):
    return pltpu.with_memory_space_constraint(a, pltpu.HBM)


def _exchange_copies(gather, srcs, lands, send_sems, recv_sems):
    nw = len(srcs)
    x, y, c, chip, peers, peer_chip = _place()
    pairs = []
    for m, (px, py) in enumerate(peers):
        for w in range(nw):
            sems = (send_sems.at[nw * m + w], recv_sems.at[nw * m + w], (px, py, c))
            if gather:
                pairs.append((_remote(srcs[w], lands[w].at[chip], *sems),
                              _remote(srcs[w], lands[w].at[peer_chip[m]], *sems)))
            else:
                pairs.append((_remote(srcs[w].at[m], lands[w].at[m], *sems),) * 2)
    return pairs


def _exchange_start(gather, srcs, after, name, lands=None):
    nw = len(srcs)
    n_copies = 3 * nw

    def body(*refs):
        send_sems, recv_sems = refs[2 * nw + 1], refs[2 * nw + 2]
        for start, _ in _exchange_copies(gather, refs[:nw], refs[nw:2 * nw], send_sems, recv_sems):
            start.start()
        refs[-1][...] = jnp.zeros_like(refs[-1])

    if lands is None:
        lands = [lax.empty(((N_CHIPS,) + s.shape) if gather else s.shape, s.dtype) for s in srcs]
    lands = [_in_hbm(l) for l in lands]
    return pl.pallas_call(
        body, name=name,
        out_shape=(pltpu.SemaphoreType.DMA((n_copies,)), pltpu.SemaphoreType.DMA((n_copies,)))
        + tuple(pltpu.HBM(s.shape, s.dtype) for s in srcs)
        + tuple(pltpu.HBM(l.shape, l.dtype) for l in lands)
        + (jax.ShapeDtypeStruct((8, LANES), F32),),
        in_specs=[_HBM] * (2 * nw) + [_ANY],
        out_specs=(_SEM, _SEM) + (_HBM,) * (2 * nw) + (pl.BlockSpec(memory_space=pltpu.VMEM),),
        input_output_aliases={i: 2 + i for i in range(2 * nw)},
        compiler_params=pltpu.CompilerParams(has_side_effects=_DATAFLOW),
    )(*[_in_hbm(s) for s in srcs], *lands, after)


def _exchange_wait(gather, started, after, name):
    nw = (len(started) - 3) // 2
    send_sems, recv_sems = started[0], started[1]
    thru = started[2:2 + 2 * nw]

    def body(*refs):
        for _, arrival in _exchange_copies(gather, refs[:nw], refs[nw:2 * nw], refs[2 * nw], refs[2 * nw + 1]):
            arrival.wait_send()
            arrival.wait_recv()

    outs = pl.pallas_call(
        body, name=name,
        out_shape=tuple(pltpu.HBM(t.shape, t.dtype) for t in thru),
        in_specs=[_HBM] * (2 * nw) + [_SEM, _SEM, _ANY], out_specs=(_HBM,) * (2 * nw),
        input_output_aliases={i: i for i in range(2 * nw)},
        compiler_params=pltpu.CompilerParams(has_side_effects=_DATAFLOW),
    )(*thru, send_sems, recv_sems, after)
    return outs[nw:]


def _reduce_last(owns, landed, g_small):
    ns = len(owns)
    row_block = 32
    hs = SMALL_ROWS // 2

    def body(*refs):
        own_refs, land_refs, gsm_ref = refs[:ns], refs[ns:2 * ns], refs[2 * ns]
        out_refs, osm_ref = refs[2 * ns + 1:3 * ns + 1], refs[3 * ns + 1]
        ra_sm, p_sm, s_sem, r_sem, sm_s, sm_r = refs[3 * ns + 2:]
        x, y, c, chip, peers, peer_chip = _place()
        sib = (x, y, 1 - c)
        half = lambda cc: pl.ds(pl.multiple_of(cc * hs, 8), hs)
        sm_a = _remote(gsm_ref.at[half(1 - c), :], ra_sm, sm_s.at[0], sm_r.at[0], sib)
        sm_a.start()
        swaps = [sm_a]
        for w in range(ns):
            n = own_refs[w].shape[0]

            def total(i, carry, w=w, n=n):
                r0 = pl.multiple_of(i * row_block, row_block)
                blk = pl.ds(r0, row_block)
                acc = own_refs[w][blk, :]
                for m in range(3):
                    acc = acc + land_refs[w][m, blk, :].astype(F32)
                out_refs[w][pl.ds(pl.multiple_of(c * n + r0, 8), row_block), :] = acc
                return carry
            lax.fori_loop(0, n // row_block, total, 0)
            mine = out_refs[w].at[pl.ds(pl.multiple_of(c * n, 8), n), :]
            swaps.append(_remote(mine, mine, s_sem.at[w], r_sem.at[w], sib))
            swaps[-1].start()
        sm_a.wait_recv()
        p_sm[chip] = gsm_ref[half(c), :] + ra_sm[...]
        for m, (px, py) in enumerate(peers):
            swaps.append(_remote(p_sm.at[chip], p_sm.at[chip], sm_s.at[1 + m], sm_r.at[1 + m], (px, py, c)))
            swaps[-1].start()
        for m, (px, py) in enumerate(peers):
            _remote(p_sm.at[chip], p_sm.at[peer_chip[m]], sm_s.at[1 + m], sm_r.at[1 + m], (px, py, c)).wait_recv()
        osm_ref[half(c), :] = (p_sm[0] + p_sm[1]) + (p_sm[2] + p_sm[3])
        swaps.append(_remote(osm_ref.at[half(c), :], osm_ref.at[half(c), :], sm_s.at[4], sm_r.at[4], sib))
        swaps[-1].start()
        for w in range(ns):
            n = own_refs[w].shape[0]
            theirs = out_refs[w].at[pl.ds(pl.multiple_of((1 - c) * n, 8), n), :]
            _remote(theirs, theirs, s_sem.at[w], r_sem.at[w], sib).wait_recv()
        _remote(osm_ref.at[half(1 - c), :], osm_ref.at[half(1 - c), :], sm_s.at[4], sm_r.at[4], sib).wait_recv()
        for cp in swaps:
            cp.wait_send()

    vmem = pl.BlockSpec(memory_space=pltpu.VMEM)
    return pl.pallas_call(
        body, out_shape=[jax.ShapeDtypeStruct((2 * o.shape[0], o.shape[1]), F32) for o in owns]
        + [jax.ShapeDtypeStruct((SMALL_ROWS, LANES), F32)],
        in_specs=[vmem] * (2 * ns + 1), out_specs=[vmem] * (ns + 1),
        scratch_shapes=[pltpu.VMEM((hs, LANES), F32), pltpu.VMEM((N_CHIPS, hs, LANES), F32),
                        pltpu.SemaphoreType.DMA((ns,)), pltpu.SemaphoreType.DMA((ns,)),
                        pltpu.SemaphoreType.DMA((5,)), pltpu.SemaphoreType.DMA((5,))],
        compiler_params=pltpu.CompilerParams(vmem_limit_bytes=VMEM_LIMIT),
        name="reduce_last")(*owns, *landed, g_small)


_SMALL_PARTS = (("g_norm", 8, 8), ("w_s", 512, 512), ("b_s", 4, 8), ("g_v", 2, 8), ("g_mem", 8, 8),
                ("g_final", 8, 8), ("loss", 8, 8))
_LOSS_ROW = SMALL_ROWS - 8
assert sum(p for _, _, p in _SMALL_PARTS) == SMALL_ROWS


def _pack_small(parts, loss_block):
    rows = []
    for (name, used, padded), p in zip(_SMALL_PARTS, list(parts) + [loss_block]):
        p = p.reshape(used, LANES)
        if padded > used:
            p = jnp.pad(p, ((0, padded - used), (0, 0)))
        rows.append(p)
    return jnp.concatenate(rows, axis=0)


def _local_step(x, mem, target, g_norm, w_in, w_s, b_s, g_v, g_mem, late_weights, g_final,
                fwd_token=None, on_dw=None):
    B, S, _ = x.shape
    x2d = x.reshape(B * S, D_MODEL)
    t2d = target.reshape(B * S, D_MODEL)
    mem2d = mem.reshape(B * N_MEM, D_MODEL)

    proj = _inproj_fwd(x2d, g_norm, w_in, after=() if fwd_token is None else (fwd_token,))
    w_kv, w_out = late_weights(proj)
    kv = _kv_fwd(mem2d, g_mem, w_kv)
    a, lse = _attn_fwd(proj, B, S)
    w_sT = jnp.swapaxes(w_s, 1, 2)
    b_tab = jnp.repeat(b_s.T, HEAD_DIM, axis=1)
    (dx2, da, drest, loss, d_wout, d_ws, d_bs, d_gv, d_gf, dkv) = _mid(
        x2d, t2d, a, proj, kv, w_s, w_sT, b_tab, g_v, w_out, g_final, B, S)
    d_wkv, d_gmem = _kv_bwd(mem2d, g_mem, w_kv, dkv)
    dq, dk, dv = _attn_bwd(proj, a, lse, da, B, S)
    if on_dw is None:
        d_win = _inproj_bwd_dw(dq, dk, dv, drest, x2d, g_norm)
        after = ()
    else:
        d_win = None
        by_chip = lambda g: g.reshape((N_CHIPS, g.shape[0] // N_CHIPS, g.shape[1]))
        after = (on_dw(*_inproj_bwd_dw(dq, dk, dv, drest, x2d, g_norm, reduce_with=[by_chip(d_wkv), by_chip(d_wout)])),)
    grad_x, d_gnorm = _inproj_bwd_dx(dq, dk, dv, drest, x2d, dx2, g_norm, w_in, after=after)
    d_bs = d_bs[:, :N_SGU_GROUPS].T
    return (loss, grad_x.reshape(B, S, D_MODEL),
            dict(g_norm=d_gnorm, w_in=d_win, w_s=d_ws, b_s=d_bs, g_v=d_gv, g_mem=d_gmem, w_kv=d_wkv,
                 w_out=d_wout, g_final=d_gf))


def kernel(x, mem, g_norm, w_in, w_sgu_spatial, b_sgu_spatial, g_sgu_v, g_mem, w_mem_kv, w_out, g_final, loss_target, m_g_norm, m_w_in, m_w_sgu_spatial, m_b_sgu_spatial, m_g_sgu_v, m_g_mem, m_w_mem_kv, m_w_out, m_g_final, v_g_norm, v_w_in, v_w_sgu_spatial, v_b_sgu_spatial, v_g_sgu_v, v_g_mem, v_w_mem_kv, v_w_out, v_g_final):
    t = lambda w: jnp.swapaxes(w[0], 0, 1)
    (win_all,), late_shards, late_lands = _ag_weights([t(w_in)], [w_mem_kv[0], w_out[0]])
    w_in_full = win_all.reshape(-1, win_all.shape[-1])
    late = _exchange_start(True, list(late_shards), win_all, "gather_late_start", lands=late_lands)

    def late_weights(proj):
        return [z.reshape(-1, z.shape[-1]) for z in _exchange_wait(True, late, proj, "gather_late_wait")]

    scatter = {}

    def on_dw(sends, owns):
        scatter["own"] = owns
        scatter["started"] = _exchange_start(False, list(sends), owns[0], "scatter_start")
        return scatter["started"][-1]

    loss, grad_x, g = _local_step(
        x, mem, loss_target, g_norm, w_in_full, w_sgu_spatial[0], b_sgu_spatial[0], g_sgu_v, g_mem,
        late_weights, g_final.reshape(1, D_MODEL), fwd_token=late[-1], on_dw=on_dw)

    small_names = ("g_norm", "w_s", "b_s", "g_v", "g_mem", "g_final")
    g_small = _pack_small([g[n] for n in small_names], loss)
    landed = _exchange_wait(False, scatter["started"], g_small, "scatter_wait")
    gr_in, gr_kv, gr_out, gr_small = _reduce_last(scatter["own"], landed, g_small)

    small_w = (g_norm, w_sgu_spatial, b_sgu_spatial, g_sgu_v, g_mem, g_final)
    small_m = (m_g_norm, m_w_sgu_spatial, m_b_sgu_spatial, m_g_sgu_v, m_g_mem, m_g_final)
    small_v = (v_g_norm, v_w_sgu_spatial, v_b_sgu_spatial, v_g_sgu_v, v_g_mem, v_g_final)
    rows = lambda ws: [w.reshape(-1, LANES) for w in ws]
    small_new, loss = _adamw_small(gr_small, rows(small_w), rows(small_m), rows(small_v))
    loss = loss.reshape(())
    small = [[z.reshape(w.shape) for z in four] for w, four in zip(small_w, small_new)]
    d_in, nm_in, nv_in = _adamw(t(w_in), gr_in, t(m_w_in), t(v_w_in), "adamw_w_in")
    gr_in, d_in, nm_in, nv_in = [jnp.swapaxes(z, 0, 1) for z in (gr_in, d_in, nm_in, nv_in)]
    d_kv, nm_kv, nv_kv = _adamw(w_mem_kv[0], gr_kv, m_w_mem_kv[0], v_w_mem_kv[0], "adamw_w_kv")
    d_out, nm_out, nv_out = _adamw(w_out[0], gr_out, m_w_out[0], v_w_out[0], "adamw_w_out")

    def leaves(kind, big_in, big_kv, big_out):
        s_norm, s_ws, s_bs, s_gv, s_gmem, s_gf = [four[kind] for four in small]
        return [s_norm, big_in[None], s_ws, s_bs, s_gv, s_gmem, big_kv[None], big_out[None], s_gf]

    return (loss, grad_x, *leaves(0, gr_in, gr_kv, gr_out), *leaves(1, d_in, d_kv, d_out),
            *leaves(2, nm_in, nm_kv, nm_out), *leaves(3, nv_in, nv_kv, nv_out))
```

```python
import functools

import jax
import jax.numpy as jnp
from jax import lax
from jax.experimental import pallas as pl
from jax.experimental.pallas import tpu as pltpu

F32 = jnp.float32
BF16 = jnp.bfloat16
MESH = pl.DeviceIdType.MESH

D_MODEL = 1024
ATTN_WIDTH = 512
SGU_WIDTH = 256
MEM_WIDTH = 256
N_MEM = 256
IN_COLS = 3328
QKV_COLS = 3 * ATTN_WIDTH
REST_COLS = IN_COLS - QKV_COLS
SGU_CHUNK = 128
N_SGU_GROUPS = 4
EPS = 1e-6
NEG_INF = -1e30
DILATIONS = (1, 4, 16)
RADIUS = 64
Q_BLOCK = 128
LANES = 128
HEAD_DIM = 64

ADAM_LR = 0.001
ADAM_B1 = 0.9
ADAM_B2 = 0.999
ADAM_EPS = 1e-08
ADAM_WD = 0.01
ADAM_STEP = 10

N_CHIPS = 4
VMEM_LIMIT = 56 * 1024 * 1024
SMALL_ROWS = 560


def _params(sem=None, vmem=VMEM_LIMIT):
    return pltpu.CompilerParams(dimension_semantics=sem, vmem_limit_bytes=vmem)


def _nn(a, b):
    return jnp.dot(a, b, preferred_element_type=F32)


def _nt(a, b):
    return lax.dot_general(a, b, (((1,), (1,)), ((), ())), preferred_element_type=F32)


def _tn(a, b):
    return lax.dot_general(a, b, (((0,), (0,)), ((), ())), preferred_element_type=F32)


def _rms(x):
    r = lax.rsqrt(jnp.mean(x * x, axis=-1, keepdims=True) + EPS)
    return r, x * r


def _head_masks():
    lane = lax.broadcasted_iota(jnp.int32, (1, LANES), 1)
    lo = lane < HEAD_DIM
    return lo, (lo.astype(F32), (~lo).astype(F32))


def _silu_parts(z):
    s = jax.nn.sigmoid(z)
    return z * s, s * (1.0 + z * (1.0 - s))


def _gelu_parts(x):
    c = 0.7978845608028654
    x2 = x * x
    s = jax.nn.sigmoid((2.0 * c) * (x + 0.044715 * (x * x2)))
    return x * s, s * (1.0 + x * (1.0 - s) * ((2.0 * c) * (1.0 + 3.0 * 0.044715 * x2)))


def _after(tokens):
    return [pl.BlockSpec(memory_space=pl.ANY)] * len(tokens)


def _inproj_fwd(x2d, g_norm, w_in_t, after=()):
    T = x2d.shape[0]
    tm = 512

    def body(x_ref, g_ref, w_ref, *rest):
        o_ref = rest[-1]
        _, xh = _rms(x_ref[...])
        h = (xh * g_ref[...]).astype(BF16)
        o_ref[...] = _nt(h, w_ref[...])

    return pl.pallas_call(
        body, grid=(T // tm,),
        in_specs=[pl.BlockSpec((tm, D_MODEL), lambda i: (i, 0)),
                  pl.BlockSpec((1, D_MODEL), lambda i: (0, 0)),
                  pl.BlockSpec((IN_COLS, D_MODEL), lambda i: (0, 0))] + _after(after),
        out_specs=pl.BlockSpec((tm, IN_COLS), lambda i: (i, 0)),
        out_shape=jax.ShapeDtypeStruct((T, IN_COLS), F32),
        compiler_params=_params(("arbitrary",)), name="inproj_fwd")(x2d, g_norm, w_in_t, *after)


def _kv_fwd(mem2d, g_mem, w_kv):
    Tm = mem2d.shape[0]

    def body(m_ref, g_ref, w_ref, o_ref):
        _, mh = _rms(m_ref[...])
        o_ref[...] = _nn((mh * g_ref[...]).astype(BF16), w_ref[...])

    return pl.pallas_call(
        body, out_shape=jax.ShapeDtypeStruct((Tm, 2 * MEM_WIDTH), F32),
        compiler_params=_params(), name="kv_fwd")(mem2d, g_mem, w_kv)


def _kv_bwd(mem2d, g_mem, w_kv, dkv):
    Tm = mem2d.shape[0]

    def body(m_ref, g_ref, w_ref, dkv_ref, dw_ref, dg_ref):
        _, mh = _rms(m_ref[...])
        memn = (mh * g_ref[...]).astype(BF16)
        dkvb = dkv_ref[...].astype(BF16)
        dw_ref[...] = _tn(memn, dkvb)
        dmemn = _nt(dkvb, w_ref[...])
        dg_ref[...] = jnp.sum(dmemn * mh, axis=0, keepdims=True)

    return pl.pallas_call(
        body, out_shape=(jax.ShapeDtypeStruct((D_MODEL, 2 * MEM_WIDTH), F32),
                         jax.ShapeDtypeStruct((1, D_MODEL), F32)),
        compiler_params=_params(), name="kv_bwd")(mem2d, g_mem, w_kv, dkv)


def _attn_geometry(S):
    geom = []
    for d in DILATIONS:
        L = S // d
        assert L % Q_BLOCK == 0
        geom.append((d, L, min(2 * Q_BLOCK, L), L // Q_BLOCK))
    return geom


def _init_bias(bias_scr, geom, hp):
    row = lax.broadcasted_iota(jnp.int32, (Q_BLOCK, 2 * Q_BLOCK), 0)
    col = lax.broadcasted_iota(jnp.int32, (Q_BLOCK, 2 * Q_BLOCK), 1)
    for j in (0, 1):
        bits = (126 - (2 * hp + j)) * (1 << 23)
        slope = lax.bitcast_convert_type(jnp.full((1, 1), bits, jnp.int32), F32)
        for di, (d, _, _, _) in enumerate(geom):
            for cls, off in enumerate((0, -RADIUS, -2 * RADIUS)):
                dist = jnp.abs(col - row + off)
                bias_scr[di * 6 + cls * 2 + j] = jnp.where(
                    dist <= RADIUS, -(slope * float(d)) * dist.astype(F32), NEG_INF)


SPLIT = 4
COPY_ROWS = 256


def _by4_rows(S, step):
    per_class = S // SPLIT // COPY_ROWS
    r, j = step // per_class, step % per_class
    return (pl.ds(r + SPLIT * j * COPY_ROWS, COPY_ROWS, stride=SPLIT),
            pl.ds(pl.multiple_of(r * (S // SPLIT) + j * COPY_ROWS, COPY_ROWS), COPY_ROWS))


def _to_by4(src, dst, S):
    def step(i, carry):
        natural, by4 = _by4_rows(S, i)
        dst[by4, :] = src[natural, :]
        return carry
    lax.fori_loop(0, S // COPY_ROWS, step, 0)


def _block_slices(d, L, KW, nqb, r, qb, S):
    qs = qb * Q_BLOCK
    ks = jnp.clip(qs - RADIUS, 0, L - KW)
    cls = jnp.where(qb == 0, 0, jnp.where(qb == nqb - 1, 2, 1))
    if d == 1:
        qsl = pl.ds(pl.multiple_of(qs, Q_BLOCK), Q_BLOCK)
        ksl = pl.ds(pl.multiple_of(ks, RADIUS), KW)
    elif d == SPLIT:
        qsl = pl.ds(pl.multiple_of(r * L + qs, Q_BLOCK), Q_BLOCK)
        ksl = pl.ds(pl.multiple_of(r * L + ks, RADIUS), KW)
    else:
        sub = d // SPLIT
        base = (r % SPLIT) * (S // SPLIT) + r // SPLIT
        qsl = pl.ds(base + qs * sub, Q_BLOCK, stride=sub)
        ksl = pl.ds(base + ks * sub, KW, stride=sub)
    return qsl, ksl, cls


def _for_groups(geom, S, group, fn):
    for di, (d, L, KW, nqb) in enumerate(geom):
        n = group[di]
        assert (d * nqb) % n == 0

        def step(it, carry, di=di, d=d, L=L, KW=KW, nqb=nqb, n=n):
            slices = []
            for g in range(n):
                i = it * n + g
                slices.append(_block_slices(d, L, KW, nqb, i // nqb, i % nqb, S))
            fn(di, KW, slices)
            return carry
        lax.fori_loop(0, d * nqb // n, step, 0)


def _attn_fwd(proj, B, S):
    T = B * S
    geom = _attn_geometry(S)
    n_pairs = ATTN_WIDTH // LANES

    def body(q_ref, k_ref, v_ref, a_ref, lse_ref, bias_scr, q4, k4, v4, *per_dilation):
        o_scr, m_scr, l_scr = per_dilation[0:3], per_dilation[3:6], per_dilation[6:9]
        lo, hm = _head_masks()
        pair = pl.program_id(0)

        @pl.when(pl.program_id(1) == 0)
        def _():
            _init_bias(bias_scr, geom, pair)
        for src, dst in ((q_ref, q4), (k_ref, k4), (v_ref, v4)):
            _to_by4(src, dst, S)

        def group(di, KW, slices):
            chains = [(g, j) for g in range(len(slices)) for j in (0, 1)]
            q_src, k_src, v_src = (q_ref, k_ref, v_ref) if di == 0 else (q4, k4, v4)
            q = [q_src[qsl, :] for qsl, _, _ in slices]
            kw = [k_src[ksl, :].astype(BF16) for _, ksl, _ in slices]
            vw = [v_src[ksl, :].astype(BF16) for _, ksl, _ in slices]
            s = {(g, j): _nt((q[g] * (hm[j] * 0.125)).astype(BF16), kw[g])
                 + bias_scr[di * 6 + slices[g][2] * 2 + j, :, pl.ds(0, KW)] for g, j in chains}
            m = {c: jnp.max(s[c], axis=1, keepdims=True) for c in chains}
            p = {c: jnp.exp(s[c] - m[c]) for c in chains}
            l = {c: jnp.sum(p[c], axis=1, keepdims=True) for c in chains}
            o = {(g, j): _nn(p[(g, j)].astype(BF16), vw[g]) for g, j in chains}
            for g, (qsl, _, _) in enumerate(slices):
                o_scr[di][qsl, :] = jnp.where(lo, o[(g, 0)], o[(g, 1)])
                m_scr[di][qsl, :] = jnp.where(lo, m[(g, 0)], m[(g, 1)])
                l_scr[di][qsl, :] = jnp.where(lo, l[(g, 0)], l[(g, 1)])

        _for_groups(geom, S, (16, 16, 16), group)

        def combine(i, carry):
            natural, by4 = _by4_rows(S, i)
            rows = [natural, by4, by4]
            ms = [m_scr[di][rows[di], :] for di in range(3)]
            mx = jnp.maximum(jnp.maximum(ms[0], ms[1]), ms[2])
            num = 0.0
            den = 0.0
            for di in range(3):
                w = jnp.exp(ms[di] - mx)
                num = num + w * o_scr[di][rows[di], :]
                den = den + w * l_scr[di][rows[di], :]
            a_ref[natural, :] = num / den
            lse_ref[natural, :] = mx + jnp.log(den)
            return carry

        lax.fori_loop(0, S // COPY_ROWS, combine, 0)

    blk = lambda off: pl.BlockSpec((S, LANES), lambda h, b, off=off: (b, off + h))
    out_blk = pl.BlockSpec((S, LANES), lambda h, b: (b, h))
    return pl.pallas_call(
        body, grid=(n_pairs, B),
        in_specs=[blk(0), blk(n_pairs), blk(2 * n_pairs)],
        out_specs=[out_blk, out_blk],
        out_shape=[jax.ShapeDtypeStruct((T, ATTN_WIDTH), F32)] * 2,
        scratch_shapes=[pltpu.VMEM((18, Q_BLOCK, 2 * Q_BLOCK), F32)] + [pltpu.VMEM((S, LANES), F32)] * 12,
        compiler_params=_params(("arbitrary", "arbitrary")), name="attn_fwd")(proj, proj, proj)


def _attn_bwd(proj, a, lse, da, B, S):
    T = B * S
    geom = _attn_geometry(S)
    n_pairs = ATTN_WIDTH // LANES

    def body(q_ref, k_ref, v_ref, a_ref, lse_ref, do_ref, dq_ref, dk_ref, dv_ref, bias_scr, *scr):
        acc = (scr[0:3], scr[3:6])
        natural_in = (q_ref, k_ref, v_ref, a_ref, lse_ref, do_ref)
        by4_in = scr[6:12]
        _, hm = _head_masks()
        pair = pl.program_id(0)

        @pl.when(pl.program_id(1) == 0)
        def _():
            _init_bias(bias_scr, geom, pair)
        for ref in scr[0:6]:
            ref[...] = jnp.zeros_like(ref)
        for src, dst in zip(natural_in, by4_in):
            _to_by4(src, dst, S)

        def group(di, KW, slices):
            n = len(slices)
            chains = [(g, j) for g in range(n) for j in (0, 1)]
            q_src, k_src, v_src, a_src, lse_src, do_src = natural_in if di == 0 else by4_in
            dq_scr, dk_scr, dv_scr = acc[0 if di == 0 else 1]
            q = [q_src[qsl, :] for qsl, _, _ in slices]
            do = [do_src[qsl, :] for qsl, _, _ in slices]
            doa = [do[g] * a_src[slices[g][0], :] for g in range(n)]
            lse_q = [lse_src[qsl, :] for qsl, _, _ in slices]
            kw = [k_src[ksl, :].astype(BF16) for _, ksl, _ in slices]
            vw = [v_src[ksl, :].astype(BF16) for _, ksl, _ in slices]
            qj = {(g, j): (q[g] * (hm[j] * 0.125)).astype(BF16) for g, j in chains}
            doj = {(g, j): (do[g] * hm[j]).astype(BF16) for g, j in chains}
            s = {(g, j): _nt(qj[(g, j)], kw[g])
                 + bias_scr[di * 6 + slices[g][2] * 2 + j, :, pl.ds(0, KW)] for g, j in chains}
            dp = {(g, j): _nt(doj[(g, j)], vw[g]) for g, j in chains}
            dsum = {(g, j): jnp.sum(doa[g] * hm[j], axis=1, keepdims=True) for g, j in chains}
            p = {(g, j): jnp.exp(s[(g, j)] - lse_q[g][:, HEAD_DIM * j:HEAD_DIM * j + 1]) for g, j in chains}
            ds = {c: (p[c] * (dp[c] - dsum[c])).astype(BF16) for c in chains}
            pb = {c: p[c].astype(BF16) for c in chains}
            dq = [_nn(ds[(g, 0)], kw[g]) * (hm[0] * 0.125) + _nn(ds[(g, 1)], kw[g]) * (hm[1] * 0.125)
                  for g in range(n)]
            both = lambda t, g: jnp.concatenate([t[(g, 0)], t[(g, 1)]], axis=0)
            dkw = [_tn(both(ds, g), both(qj, g)) for g in range(n)]
            dvw = [_tn(both(pb, g), both(doj, g)) for g in range(n)]
            for g, (qsl, ksl, _) in enumerate(slices):
                dq_scr[qsl, :] = dq_scr[qsl, :] + dq[g]
                dk_scr[ksl, :] = dk_scr[ksl, :] + dkw[g]
                dv_scr[ksl, :] = dv_scr[ksl, :] + dvw[g]

        _for_groups(geom, S, (4, 4, 16), group)

        def merge(i, carry):
            natural, by4 = _by4_rows(S, i)
            for nat, split in zip(*acc):
                nat[natural, :] = nat[natural, :] + split[by4, :]
            return carry
        lax.fori_loop(0, S // COPY_ROWS, merge, 0)
        for out, nat in zip((dq_ref, dk_ref, dv_ref), acc[0]):
            out[...] = nat[...].astype(BF16)

    blk = lambda off: pl.BlockSpec((S, LANES), lambda h, b, off=off: (b, off + h))
    return pl.pallas_call(
        body, grid=(n_pairs, B),
        in_specs=[blk(0), blk(n_pairs), blk(2 * n_pairs), blk(0), blk(0), blk(0)],
        out_specs=[blk(0), blk(0), blk(0)],
        out_shape=[jax.ShapeDtypeStruct((T, ATTN_WIDTH), BF16)] * 3,
        scratch_shapes=[pltpu.VMEM((18, Q_BLOCK, 2 * Q_BLOCK), F32)] + [pltpu.VMEM((S, LANES), F32)] * 12,
        compiler_params=_params(("arbitrary", "arbitrary")), name="attn_bwd")(proj, proj, proj, a, lse, da)


def _mid(x2d, t2d, a, proj, kv, w_s, w_sT, b_tab, g_v, w_out, g_final, B, S):
    T = B * S
    tm = 512
    nt = S // tm
    halves = 2
    hrows = tm // halves

    def body(x_ref, t_ref, a_ref, za_ref, ub_ref, vb_ref, zb_ref, qm_ref, zm_ref, kv_ref,
              ws_ref, wsT_ref, btab_ref, gv_ref, wout_ref, gf_ref,
              dx2_ref, da_ref, drest_ref, loss_ref, dwout_ref, dws_ref, dbs_ref, dgv_ref, dgf_ref, dkv_ref,
              dbtab_scr):
        b = pl.program_id(0)
        t = pl.program_id(1)
        first = jnp.logical_and(b == 0, t == 0)
        last = jnp.logical_and(b == B - 1, t == nt - 1)
        _, hm = _head_masks()
        lane_g = lax.broadcasted_iota(jnp.int32, (1, SGU_WIDTH), 1) // HEAD_DIM
        gm = [(lane_g == g).astype(F32) for g in range(N_SGU_GROUPS)]
        H = range(halves)
        rows = [pl.ds(h * hrows, hrows) for h in H]
        ld = lambda ref: [ref[r, :] for r in rows]
        cat = lambda parts, axis: jnp.concatenate(parts, axis=axis)
        chunks = [slice(ci * SGU_CHUNK, (ci + 1) * SGU_CHUNK) for ci in range(hrows // SGU_CHUNK)]
        pairs = [slice(pr * LANES, (pr + 1) * LANES) for pr in range(2)]
        heads = [(pr, j) for pr in range(2) for j in (0, 1)]

        @pl.when(first)
        def _():
            loss_ref[...] = jnp.zeros_like(loss_ref)
            dwout_ref[...] = jnp.zeros_like(dwout_ref)
            dws_ref[...] = jnp.zeros_like(dws_ref)
            dbs_ref[...] = jnp.zeros_like(dbs_ref)
            dgv_ref[...] = jnp.zeros_like(dgv_ref)
            dgf_ref[...] = jnp.zeros_like(dgf_ref)
            dbtab_scr[...] = jnp.zeros_like(dbtab_scr)

        @pl.when(t == 0)
        def _():
            dkv_ref[...] = jnp.zeros_like(dkv_ref)

        a_val = ld(a_ref)
        sil_a = [_silu_parts(z) for z in ld(za_ref)]
        gated_a = [s[0] * a for s, a in zip(sil_a, a_val)]
        u = [_gelu_parts(z) for z in ld(ub_ref)]
        vv = [_gelu_parts(z) for z in ld(vb_ref)]
        vnorm = [_rms(v[0]) for v in vv]
        gv = gv_ref[...]
        vn = [(n[1] * gv).astype(BF16) for n in vnorm]
        w_cat = cat([ws_ref[g].astype(BF16) for g in range(N_SGU_GROUPS)], 1)
        wT_cat = cat([wsT_ref[g].astype(BF16) for g in range(N_SGU_GROUPS)], 1)
        gmb = [m.astype(BF16) for m in gm]
        by_group = lambda chunk: cat([chunk * gmb[g] for g in range(N_SGU_GROUPS)], 0)
        btab = btab_ref[...]
        mixed = [cat([btab + _nn(w_cat, by_group(vn[h][c, :])) for c in chunks], 0) for h in H]
        sg = [u[h][0] * mixed[h] for h in H]
        sil_b = [_silu_parts(z) for z in ld(zb_ref)]
        gated_b = [sil_b[h][0] * sg[h] for h in H]

        kvv = kv_ref[...].astype(BF16)
        kp = [kvv[:, p] for p in pairs]
        vp = [kvv[:, MEM_WIDTH + pr * LANES:MEM_WIDTH + (pr + 1) * LANES] for pr in range(2)]
        qm = ld(qm_ref)
        qj = {(h, pr, j): (qm[h][:, pairs[pr]] * (hm[j] * 0.125)).astype(BF16) for h in H for pr, j in heads}
        sc = {k: _nt(qj[k], kp[k[1]]) for k in qj}
        ex = {k: jnp.exp(sc[k] - jnp.max(sc[k], axis=1, keepdims=True)) for k in qj}
        prob = {k: ex[k] * (1.0 / jnp.sum(ex[k], axis=1, keepdims=True)) for k in qj}
        probb = {k: prob[k].astype(BF16) for k in qj}
        mo = [cat([sum(_nn(probb[(h, pr, j)], vp[pr]) * hm[j] for j in (0, 1)) for pr in range(2)], 1) for h in H]
        sil_m = [_silu_parts(z) for z in ld(zm_ref)]
        gated_m = [sil_m[h][0] * mo[h] for h in H]

        gated = [cat([gated_a[h], gated_b[h], gated_m[h]], 1).astype(BF16) for h in H]
        wout = wout_ref[...]
        x_in = ld(x_ref)
        x2 = [x_in[h] + _nn(gated[h], wout) for h in H]
        fin = [_rms(z) for z in x2]
        gf = gf_ref[...]
        tgt = ld(t_ref)
        err = [fin[h][1] * gf - tgt[h] for h in H]
        loss_ref[...] += sum(jnp.sum(e * e) for e in err) * (0.5 / D_MODEL)

        dy = [e * (1.0 / D_MODEL) for e in err]
        dgf_ref[...] += sum(jnp.sum(dy[h] * fin[h][1], axis=0, keepdims=True) for h in H)
        gdy = [d * gf for d in dy]
        dx2 = [fin[h][0] * (gdy[h] - fin[h][1] * jnp.mean(gdy[h] * fin[h][1], axis=1, keepdims=True)) for h in H]
        for h in H:
            dx2_ref[rows[h], :] = dx2[h]
        dx2b = [d.astype(BF16) for d in dx2]
        dgated = [_nt(d, wout) for d in dx2b]
        dwout_ref[...] += _tn(cat(gated, 0), cat(dx2b, 0))
        dga = [d[:, 0:ATTN_WIDTH] for d in dgated]
        dgb = [d[:, ATTN_WIDTH:ATTN_WIDTH + SGU_WIDTH] for d in dgated]
        dgm = [d[:, ATTN_WIDTH + SGU_WIDTH:] for d in dgated]

        for h in H:
            da_ref[rows[h], :] = dga[h] * sil_a[h][0]
        dza = [dga[h] * a_val[h] * sil_a[h][1] for h in H]

        dsg = [dgb[h] * sil_b[h][0] for h in H]
        dzb = [dgb[h] * sg[h] * sil_b[h][1] for h in H]
        dub = [dsg[h] * mixed[h] * u[h][1] for h in H]
        dmixed = [dsg[h] * u[h][0] for h in H]
        dmixed_b = [d.astype(BF16) for d in dmixed]
        dvn = [cat([_nn(wT_cat, by_group(dmixed_b[h][c, :])) for c in chunks], 0) for h in H]
        for g in range(N_SGU_GROUPS):
            dws_ref[g] += sum(_nt((dmixed[h][c, :] * gm[g]).astype(BF16), vn[h][c, :]) for h in H for c in chunks)
        dbtab_scr[...] += sum(dmixed[h][c, :] for h in H for c in chunks)
        dgv_ref[...] += sum(jnp.sum(dvn[h] * vnorm[h][1], axis=0, keepdims=True) for h in H)
        tv = [d * gv for d in dvn]
        dvv = [vnorm[h][0] * (tv[h] - vnorm[h][1] * jnp.mean(tv[h] * vnorm[h][1], axis=1, keepdims=True)) for h in H]
        dvb = [dvv[h] * vv[h][1] for h in H]

        dmo = [dgm[h] * sil_m[h][0] for h in H]
        dzm = [dgm[h] * mo[h] * sil_m[h][1] for h in H]
        dmoj = {(h, pr, j): (dmo[h][:, pairs[pr]] * hm[j]).astype(BF16) for h in H for pr, j in heads}
        dp = {k: _nt(dmoj[k], vp[k[1]]) for k in qj}
        ds = {k: (prob[k] * (dp[k] - jnp.sum(dp[k] * prob[k], axis=1, keepdims=True))).astype(BF16) for k in qj}
        dqm = [cat([sum(_nn(ds[(h, pr, j)], kp[pr]) * (hm[j] * 0.125) for j in (0, 1)) for pr in range(2)], 1)
               for h in H]
        every = lambda tbl, pr: cat([tbl[(h, pr, j)] for h in H for j in (0, 1)], 0)
        dk = [_tn(every(ds, pr), every(qj, pr)) for pr in range(2)]
        dv = [_tn(every(probb, pr), every(dmoj, pr)) for pr in range(2)]
        dkv_ref[...] += cat(dk + dv, 1)

        for h in H:
            drest_ref[rows[h], :] = cat([dza[h], dub[h], dvb[h], dzb[h], dqm[h], dzm[h]], 1).astype(BF16)

        @pl.when(last)
        def _():
            lane = lax.broadcasted_iota(jnp.int32, (1, LANES), 1)
            dbt = dbtab_scr[...]
            out = jnp.zeros((SGU_CHUNK, LANES), F32)
            for g in range(N_SGU_GROUPS):
                out = out + jnp.where(lane == g, jnp.sum(dbt * gm[g], axis=1, keepdims=True), 0.0)
            dbs_ref[...] = out

    tile = lambda w, cb: pl.BlockSpec((tm, w), lambda b, t, cb=cb: (b * nt + t, cb))
    const = lambda shape: pl.BlockSpec(shape, lambda b, t, n=len(shape): (0,) * n)
    return pl.pallas_call(
        body, grid=(B, nt),
        in_specs=[tile(D_MODEL, 0), tile(D_MODEL, 0), tile(ATTN_WIDTH, 0),
                  tile(ATTN_WIDTH, 3),
                  tile(SGU_WIDTH, 8), tile(SGU_WIDTH, 9), tile(SGU_WIDTH, 10),
                  tile(MEM_WIDTH, 11), tile(MEM_WIDTH, 12),
                  pl.BlockSpec((N_MEM, 2 * MEM_WIDTH), lambda b, t: (b, 0)),
                  const((N_SGU_GROUPS, SGU_CHUNK, SGU_CHUNK)), const((N_SGU_GROUPS, SGU_CHUNK, SGU_CHUNK)),
                  const((SGU_CHUNK, SGU_WIDTH)), const((1, SGU_WIDTH)),
                  const((D_MODEL, D_MODEL)), const((1, D_MODEL))],
        out_specs=[tile(D_MODEL, 0), tile(ATTN_WIDTH, 0), tile(REST_COLS, 0),
                   const((8, LANES)), const((D_MODEL, D_MODEL)),
                   const((N_SGU_GROUPS, SGU_CHUNK, SGU_CHUNK)), const((SGU_CHUNK, LANES)),
                   const((1, SGU_WIDTH)), const((1, D_MODEL)),
                   pl.BlockSpec((N_MEM, 2 * MEM_WIDTH), lambda b, t: (b, 0))],
        out_shape=[jax.ShapeDtypeStruct((T, D_MODEL), F32), jax.ShapeDtypeStruct((T, ATTN_WIDTH), F32),
                   jax.ShapeDtypeStruct((T, REST_COLS), BF16),
                   jax.ShapeDtypeStruct((8, LANES), F32), jax.ShapeDtypeStruct((D_MODEL, D_MODEL), F32),
                   jax.ShapeDtypeStruct((N_SGU_GROUPS, SGU_CHUNK, SGU_CHUNK), F32),
                   jax.ShapeDtypeStruct((SGU_CHUNK, LANES), F32),
                   jax.ShapeDtypeStruct((1, SGU_WIDTH), F32), jax.ShapeDtypeStruct((1, D_MODEL), F32),
                   jax.ShapeDtypeStruct((B * N_MEM, 2 * MEM_WIDTH), F32)],
        scratch_shapes=[pltpu.VMEM((SGU_CHUNK, SGU_WIDTH), F32)],
        compiler_params=_params(("arbitrary", "arbitrary")), name="mid")(
            x2d, t2d, a, proj, proj, proj, proj, proj, proj, kv, w_s, w_sT, b_tab, g_v, w_out, g_final)


def _inproj_bwd_dx(dq, dk, dv, drest, x2d, dx2, g_norm, w_in_t, after=()):
    T = x2d.shape[0]
    tm = 512
    W = ATTN_WIDTH

    def body(dq_ref, dk_ref, dv_ref, dr_ref, x_ref, dx2_ref, g_ref, w_ref, *rest):
        gx_ref, dg_ref = rest[-2:]

        @pl.when(pl.program_id(0) == 0)
        def _():
            dg_ref[...] = jnp.zeros_like(dg_ref)

        halves = [pl.ds(h * (tm // 2), tm // 2) for h in (0, 1)]
        dh = [(_nn(dq_ref[r, :], w_ref[0:W, :]) + _nn(dk_ref[r, :], w_ref[W:2 * W, :])
               + _nn(dv_ref[r, :], w_ref[2 * W:3 * W, :]) + _nn(dr_ref[r, :], w_ref[QKV_COLS:IN_COLS, :]))
              for r in halves]
        nrm = [_rms(x_ref[r, :]) for r in halves]
        dg_ref[...] += sum(jnp.sum(d * n[1], axis=0, keepdims=True) for d, n in zip(dh, nrm))
        g = g_ref[...]
        for r, d, (rstd, xh) in zip(halves, dh, nrm):
            th = d * g
            gx_ref[r, :] = rstd * (th - xh * jnp.mean(th * xh, axis=1, keepdims=True)) + dx2_ref[r, :]

    tile = lambda w: pl.BlockSpec((tm, w), lambda i: (i, 0))
    return pl.pallas_call(
        body, grid=(T // tm,),
        in_specs=[tile(W), tile(W), tile(W), tile(REST_COLS), tile(D_MODEL), tile(D_MODEL),
                  pl.BlockSpec((1, D_MODEL), lambda i: (0, 0)),
                  pl.BlockSpec((IN_COLS, D_MODEL), lambda i: (0, 0))] + _after(after),
        out_specs=[tile(D_MODEL), pl.BlockSpec((1, D_MODEL), lambda i: (0, 0))],
        out_shape=[jax.ShapeDtypeStruct((T, D_MODEL), F32), jax.ShapeDtypeStruct((1, D_MODEL), F32)],
        compiler_params=_params(("arbitrary",)), name="inproj_bwd_dx")(
            dq, dk, dv, drest, x2d, dx2, g_norm, w_in_t, *after)


def _inproj_bwd_dw(dq, dk, dv, drest, x2d, g_norm, reduce_with=None):
    T = x2d.shape[0]
    tm = 512
    nt = T // tm
    W = ATTN_WIDTH
    fused = reduce_with is not None
    others = list(reduce_with) if fused else []
    ns = 1 + len(others)
    shard = IN_COLS // N_CHIPS
    halves = [shard // 2] + [s.shape[1] // 2 for s in others]
    cols = [D_MODEL] + [s.shape[2] for s in others]
    row_block = 32

    def body(dq_ref, dk_ref, dv_ref, dr_ref, x_ref, g_ref, *rest):
        if fused:
            stacks = rest[:ns - 1]
            sends, owns = rest[ns - 1:2 * ns - 1], rest[2 * ns - 1:3 * ns - 1]
            acc, ras, narrow = rest[3 * ns - 1], rest[3 * ns:4 * ns], rest[4 * ns]
            s_sem, r_sem = rest[4 * ns + 1], rest[4 * ns + 2]
            x, y, c, chip, peers, peer_chip = _place()
            sib = (x, y, 1 - c)

            def part(w, k, cc, r0=0, rows=None):
                n = halves[w]
                rows = n if rows is None else rows
                if w == 0:
                    return acc.at[pl.ds(pl.multiple_of(k * shard + cc * n + r0, 8), rows), :]
                return stacks[w - 1].at[k, pl.ds(pl.multiple_of(cc * n + r0, 8), rows), :]

            def swap_other(w):
                theirs = stacks[w - 1].at[:, pl.ds(pl.multiple_of((1 - c) * halves[w], 8), halves[w]), :]
                return _remote(theirs, ras[w], s_sem.at[N_CHIPS - 1 + w], r_sem.at[N_CHIPS - 1 + w], sib)

            def swap_win(k):
                return _remote(narrow.at[k], ras[0].at[k], s_sem.at[k], r_sem.at[k], sib)
        else:
            acc = rest[0]

        @pl.when(pl.program_id(0) == 0)
        def _():
            acc[...] = jnp.zeros_like(acc)
            for w in range(1, ns):
                swap_other(w).start()

        _, xh = _rms(x_ref[...])
        h = (xh * g_ref[...]).astype(BF16)
        acc[0:W, :] += _tn(dq_ref[...], h)
        acc[W:2 * W, :] += _tn(dk_ref[...], h)
        acc[2 * W:3 * W, :] += _tn(dv_ref[...], h)
        acc[QKV_COLS:IN_COLS, :] += _tn(dr_ref[...], h)

        if fused:
            @pl.when(pl.program_id(0) == nt - 1)
            def _():
                for k in range(N_CHIPS):
                    def to_bf16(i, carry, k=k):
                        r0 = pl.multiple_of(i * row_block, row_block)
                        narrow[k, pl.ds(r0, row_block), :] = part(0, k, 1 - c, r0, row_block)[...].astype(BF16)
                        return carry
                    lax.fori_loop(0, halves[0] // row_block, to_bf16, 0)
                    swap_win(k).start()
                for w in list(range(1, ns)) + [0]:
                    if w == 0:
                        for k in range(N_CHIPS):
                            swap_win(k).wait_recv()
                    else:
                        swap_other(w).wait_recv()

                    def sums(i, carry, w=w):
                        r0 = pl.multiple_of(i * row_block, row_block)
                        blk = pl.ds(r0, row_block)
                        for m in range(3):
                            k = peer_chip[m]
                            sends[w][m, blk, :] = (part(w, k, c, r0, row_block)[...]
                                                   + ras[w][k, blk, :].astype(F32)).astype(BF16)
                        owns[w][blk, :] = part(w, chip, c, r0, row_block)[...] + ras[w][chip, blk, :].astype(F32)
                        return carry
                    lax.fori_loop(0, halves[w] // row_block, sums, 0)
                for k in range(N_CHIPS):
                    swap_win(k).wait_send()
                for w in range(1, ns):
                    swap_other(w).wait_send()

    tile = lambda w: pl.BlockSpec((tm, w), lambda i: (i, 0))
    vmem = pl.BlockSpec(memory_space=pltpu.VMEM)
    in_specs = [tile(W), tile(W), tile(W), tile(REST_COLS), tile(D_MODEL), pl.BlockSpec((1, D_MODEL), lambda i: (0, 0))]
    if not fused:
        return pl.pallas_call(
            body, grid=(nt,), in_specs=in_specs,
            out_specs=pl.BlockSpec((IN_COLS, D_MODEL), lambda i: (0, 0)),
            out_shape=jax.ShapeDtypeStruct((IN_COLS, D_MODEL), F32),
            compiler_params=_params(("arbitrary",)), name="inproj_bwd_dw")(dq, dk, dv, drest, x2d, g_norm)
    outs = pl.pallas_call(
        body, grid=(nt,), in_specs=in_specs + [vmem] * (ns - 1), out_specs=[vmem] * (2 * ns),
        out_shape=[jax.ShapeDtypeStruct((3, n, cl), BF16) for n, cl in zip(halves, cols)]
        + [jax.ShapeDtypeStruct((n, cl), F32) for n, cl in zip(halves, cols)],
        scratch_shapes=[pltpu.VMEM((IN_COLS, D_MODEL), F32)]
        + [pltpu.VMEM((N_CHIPS, n, cl), BF16 if w == 0 else F32) for w, (n, cl) in enumerate(zip(halves, cols))]
        + [pltpu.VMEM((N_CHIPS, halves[0], D_MODEL), BF16)]
        + [pltpu.SemaphoreType.DMA((N_CHIPS - 1 + ns,)), pltpu.SemaphoreType.DMA((N_CHIPS - 1 + ns,))],
        compiler_params=_params(("arbitrary",)), name="inproj_bwd_dw_reduce")(
            dq, dk, dv, drest, x2d, g_norm, *others)
    return outs[:ns], outs[ns:]


def _adamw_update(w, g, m, v):
    nm = ADAM_B1 * m + (1.0 - ADAM_B1) * g
    nv = ADAM_B2 * v + (1.0 - ADAM_B2) * (g * g)
    m_hat = nm / (1.0 - ADAM_B1 ** ADAM_STEP)
    v_hat = nv / (1.0 - ADAM_B2 ** ADAM_STEP)
    return -ADAM_LR * (m_hat / (jnp.sqrt(v_hat) + ADAM_EPS) + ADAM_WD * w), nm, nv


def _adamw(w, g, m, v, name):
    R, C = w.shape
    br = max(r for r in range(8, 257, 8) if R % r == 0)

    def body(w_ref, g_ref, m_ref, v_ref, d_ref, nm_ref, nv_ref):
        d_ref[...], nm_ref[...], nv_ref[...] = _adamw_update(w_ref[...], g_ref[...], m_ref[...], v_ref[...])

    spec = pl.BlockSpec((br, C), lambda i: (i, 0))
    return pl.pallas_call(
        body, grid=(R // br,), in_specs=[spec] * 4, out_specs=[spec] * 3,
        out_shape=[jax.ShapeDtypeStruct((R, C), F32)] * 3,
        compiler_params=_params(("arbitrary",)), name=name)(w, g, m, v)


def _adamw_small(g_packed, ws, ms, vs):
    n = len(ws)

    def body(*refs):
        g_ref = refs[0]
        w_refs, m_refs, v_refs = refs[1:1 + n], refs[1 + n:1 + 2 * n], refs[1 + 2 * n:1 + 3 * n]
        outs = refs[1 + 3 * n:]
        off = 0
        for i, (_, used, padded) in enumerate(_SMALL_PARTS[:n]):
            g = g_ref[off:off + used, :]
            delta, nm, nv = _adamw_update(w_refs[i][...], g, m_refs[i][...], v_refs[i][...])
            outs[4 * i][...], outs[4 * i + 1][...], outs[4 * i + 2][...], outs[4 * i + 3][...] = g, delta, nm, nv
            off += padded
        outs[4 * n][...] = g_ref[_LOSS_ROW:_LOSS_ROW + 1, 0:1]

    outs = pl.pallas_call(
        body, out_shape=[jax.ShapeDtypeStruct(w.shape, F32) for w in ws for _ in range(4)]
        + [jax.ShapeDtypeStruct((1, 1), F32)],
        compiler_params=_params(), name="adamw_small")(g_packed, *ws, *ms, *vs)
    return [outs[4 * i:4 * i + 4] for i in range(n)], outs[4 * n]


def _place():
    x, y, c = lax.axis_index("x"), lax.axis_index("y"), lax.axis_index("c")
    chip = 2 * x + y
    peers = [(x, 1 - y), (1 - x, y), (1 - x, 1 - y)]
    peer_chip = [2 * px + py for px, py in peers]
    return x, y, c, chip, peers, peer_chip


def _remote(src, dst, send_sem, recv_sem, dev):
    return pltpu.make_async_remote_copy(src_ref=src, dst_ref=dst, send_sem=send_sem, recv_sem=recv_sem,
                                        device_id=dev, device_id_type=MESH)


def _ag_weights(weights, late=()):
    nw, nl = len(weights), len(late)
    pieces = 13

    def body(*refs):
        srcs, late_srcs = refs[:nw], refs[nw:nw + nl]
        outs, late_bf, late_land = (refs[nw + nl:2 * nw + nl], refs[2 * nw + nl:2 * nw + 2 * nl],
                                    refs[2 * nw + 2 * nl:2 * nw + 3 * nl])
        s_ici, r_ici, s_d2d, r_d2d = refs[2 * nw + 3 * nl:]
        x, y, c = lax.axis_index("x"), lax.axis_index("y"), lax.axis_index("c")
        chip = 2 * x + y
        sib = (x, y, 1 - c)
        first = ((x + 1 - c) % 2, (y + c) % 2)
        second = ((x + c) % 2, (y + 1 - c) % 2)
        first_chip, second_chip = 2 * first[0] + first[1], 2 * second[0] + second[1]
        diag_chip = 3 - chip
        for src, out in zip(srcs, outs):
            out[chip] = src[...].astype(BF16)

        parts = [(w, out, pc) for w, out in enumerate(outs) for pc in range(pieces)]

        def piece(out, k, cc, pc):
            rows = out.shape[1] // 2 // pieces
            return out.at[k, pl.ds(pl.multiple_of((cc * pieces + pc) * rows, 16), rows), :]

        def ici(w, slot, out, k, dev, pc):
            blk, sem = piece(out, k, c, pc), (nw * slot + w) * pieces + pc
            return _remote(blk, blk, s_ici.at[sem], r_ici.at[sem], (dev[0], dev[1], c))

        def d2d(w, slot, out, k, cc, pc):
            blk, sem = piece(out, k, cc, pc), (nw * slot + w) * pieces + pc
            return _remote(blk, blk, s_d2d.at[sem], r_d2d.at[sem], sib)

        sent = []
        for slot, dev in enumerate((first, second)):
            for w, out, pc in parts:
                sent.append(ici(w, slot, out, chip, dev, pc))
                sent[-1].start()
        for src, bf, land in zip(late_srcs, late_bf, late_land):
            bf[...] = src[...].astype(BF16)
            land[...] = jnp.zeros_like(land)
            land[chip] = bf[...]
        for slot, k, dev in ((0, first_chip, first), (1, second_chip, second), (2, diag_chip, second)):
            for w, out, pc in parts:
                ici(w, slot, out, k, dev, pc).wait_recv()
                if slot == 0:
                    sent.append(ici(w, 2, out, k, second, pc))
                    sent[-1].start()
                sent.append(d2d(w, slot, out, k, c, pc))
                sent[-1].start()
        for slot, k in ((0, second_chip), (1, first_chip), (2, diag_chip)):
            for w, out, pc in parts:
                d2d(w, slot, out, k, 1 - c, pc).wait_recv()
        for cp in sent:
            cp.wait_send()

    vmem = pl.BlockSpec(memory_space=pltpu.VMEM)
    outs = pl.pallas_call(
        body,
        out_shape=[jax.ShapeDtypeStruct((N_CHIPS,) + w.shape, BF16) for w in weights]
        + [jax.ShapeDtypeStruct(w.shape, BF16) for w in late]
        + [jax.ShapeDtypeStruct((N_CHIPS,) + w.shape, BF16) for w in late],
        in_specs=[vmem] * (nw + nl), out_specs=[vmem] * (nw + 2 * nl),
        scratch_shapes=[pltpu.SemaphoreType.DMA((3 * nw * pieces,))] * 4,
        compiler_params=pltpu.CompilerParams(vmem_limit_bytes=VMEM_LIMIT), name="ag_weights")(*weights, *late)
    return outs[:nw], outs[nw:nw + nl], outs[nw + nl:]


_HBM = pl.BlockSpec(memory_space=pltpu.HBM)
_SEM = pl.BlockSpec(memory_space=pltpu.SEMAPHORE)
_ANY = pl.BlockSpec(memory_space=pl.ANY)
_DATAFLOW = pltpu.SideEffectType.DATAFLOW_SIDE_EFFECTING


def _in_hbm(a):
    return pltpu.with_memory_space_constraint(a, pltpu.HBM)


def _exchange_copies(gather, srcs, lands, send_sems, recv_sems):
    nw = len(srcs)
    x, y, c, chip, peers, peer_chip = _place()
    pairs = []
    for m, (px, py) in enumerate(peers):
        for w in range(nw):
            sems = (send_sems.at[nw * m + w], recv_sems.at[nw * m + w], (px, py, c))
            if gather:
                pairs.append((_remote(srcs[w], lands[w].at[chip], *sems),
                              _remote(srcs[w], lands[w].at[peer_chip[m]], *sems)))
            else:
                pairs.append((_remote(srcs[w].at[m], lands[w].at[m], *sems),) * 2)
    return pairs


def _exchange_start(gather, srcs, after, name, lands=None):
    nw = len(srcs)
    n_copies = 3 * nw

    def body(*refs):
        send_sems, recv_sems = refs[2 * nw + 1], refs[2 * nw + 2]
        for start, _ in _exchange_copies(gather, refs[:nw], refs[nw:2 * nw], send_sems, recv_sems):
            start.start()
        refs[-1][...] = jnp.zeros_like(refs[-1])

    if lands is None:
        lands = [lax.empty(((N_CHIPS,) + s.shape) if gather else s.shape, s.dtype) for s in srcs]
    lands = [_in_hbm(l) for l in lands]
    return pl.pallas_call(
        body, name=name,
        out_shape=(pltpu.SemaphoreType.DMA((n_copies,)), pltpu.SemaphoreType.DMA((n_copies,)))
        + tuple(pltpu.HBM(s.shape, s.dtype) for s in srcs)
        + tuple(pltpu.HBM(l.shape, l.dtype) for l in lands)
        + (jax.ShapeDtypeStruct((8, LANES), F32),),
        in_specs=[_HBM] * (2 * nw) + [_ANY],
        out_specs=(_SEM, _SEM) + (_HBM,) * (2 * nw) + (pl.BlockSpec(memory_space=pltpu.VMEM),),
        input_output_aliases={i: 2 + i for i in range(2 * nw)},
        compiler_params=pltpu.CompilerParams(has_side_effects=_DATAFLOW),
    )(*[_in_hbm(s) for s in srcs], *lands, after)


def _exchange_wait(gather, started, after, name):
    nw = (len(started) - 3) // 2
    send_sems, recv_sems = started[0], started[1]
    thru = started[2:2 + 2 * nw]

    def body(*refs):
        for _, arrival in _exchange_copies(gather, refs[:nw], refs[nw:2 * nw], refs[2 * nw], refs[2 * nw + 1]):
            arrival.wait_send()
            arrival.wait_recv()

    outs = pl.pallas_call(
        body, name=name,
        out_shape=tuple(pltpu.HBM(t.shape, t.dtype) for t in thru),
        in_specs=[_HBM] * (2 * nw) + [_SEM, _SEM, _ANY], out_specs=(_HBM,) * (2 * nw),
        input_output_aliases={i: i for i in range(2 * nw)},
        compiler_params=pltpu.CompilerParams(has_side_effects=_DATAFLOW),
    )(*thru, send_sems, recv_sems, after)
    return outs[nw:]


def _reduce_last(owns, landed, g_small):
    ns = len(owns)
    row_block = 32
    hs = SMALL_ROWS // 2

    def body(*refs):
        own_refs, land_refs, gsm_ref = refs[:ns], refs[ns:2 * ns], refs[2 * ns]
        out_refs, osm_ref = refs[2 * ns + 1:3 * ns + 1], refs[3 * ns + 1]
        ra_sm, p_sm, s_sem, r_sem, sm_s, sm_r = refs[3 * ns + 2:]
        x, y, c, chip, peers, peer_chip = _place()
        sib = (x, y, 1 - c)
        half = lambda cc: pl.ds(pl.multiple_of(cc * hs, 8), hs)
        sm_a = _remote(gsm_ref.at[half(1 - c), :], ra_sm, sm_s.at[0], sm_r.at[0], sib)
        sm_a.start()
        swaps = [sm_a]
        for w in range(ns):
            n = own_refs[w].shape[0]

            def total(i, carry, w=w, n=n):
                r0 = pl.multiple_of(i * row_block, row_block)
                blk = pl.ds(r0, row_block)
                acc = own_refs[w][blk, :]
                for m in range(3):
                    acc = acc + land_refs[w][m, blk, :].astype(F32)
                out_refs[w][pl.ds(pl.multiple_of(c * n + r0, 8), row_block), :] = acc
                return carry
            lax.fori_loop(0, n // row_block, total, 0)
            mine = out_refs[w].at[pl.ds(pl.multiple_of(c * n, 8), n), :]
            swaps.append(_remote(mine, mine, s_sem.at[w], r_sem.at[w], sib))
            swaps[-1].start()
        sm_a.wait_recv()
        p_sm[chip] = gsm_ref[half(c), :] + ra_sm[...]
        for m, (px, py) in enumerate(peers):
            swaps.append(_remote(p_sm.at[chip], p_sm.at[chip], sm_s.at[1 + m], sm_r.at[1 + m], (px, py, c)))
            swaps[-1].start()
        for m, (px, py) in enumerate(peers):
            _remote(p_sm.at[chip], p_sm.at[peer_chip[m]], sm_s.at[1 + m], sm_r.at[1 + m], (px, py, c)).wait_recv()
        osm_ref[half(c), :] = (p_sm[0] + p_sm[1]) + (p_sm[2] + p_sm[3])
        swaps.append(_remote(osm_ref.at[half(c), :], osm_ref.at[half(c), :], sm_s.at[4], sm_r.at[4], sib))
        swaps[-1].start()
        for w in range(ns):
            n = own_refs[w].shape[0]
            theirs = out_refs[w].at[pl.ds(pl.multiple_of((1 - c) * n, 8), n), :]
            _remote(theirs, theirs, s_sem.at[w], r_sem.at[w], sib).wait_recv()
        _remote(osm_ref.at[half(1 - c), :], osm_ref.at[half(1 - c), :], sm_s.at[4], sm_r.at[4], sib).wait_recv()
        for cp in swaps:
            cp.wait_send()

    vmem = pl.BlockSpec(memory_space=pltpu.VMEM)
    return pl.pallas_call(
        body, out_shape=[jax.ShapeDtypeStruct((2 * o.shape[0], o.shape[1]), F32) for o in owns]
        + [jax.ShapeDtypeStruct((SMALL_ROWS, LANES), F32)],
        in_specs=[vmem] * (2 * ns + 1), out_specs=[vmem] * (ns + 1),
        scratch_shapes=[pltpu.VMEM((hs, LANES), F32), pltpu.VMEM((N_CHIPS, hs, LANES), F32),
                        pltpu.SemaphoreType.DMA((ns,)), pltpu.SemaphoreType.DMA((ns,)),
                        pltpu.SemaphoreType.DMA((5,)), pltpu.SemaphoreType.DMA((5,))],
        compiler_params=pltpu.CompilerParams(vmem_limit_bytes=VMEM_LIMIT),
        name="reduce_last")(*owns, *landed, g_small)


_SMALL_PARTS = (("g_norm", 8, 8), ("w_s", 512, 512), ("b_s", 4, 8), ("g_v", 2, 8), ("g_mem", 8, 8),
                ("g_final", 8, 8), ("loss", 8, 8))
_LOSS_ROW = SMALL_ROWS - 8
assert sum(p for _, _, p in _SMALL_PARTS) == SMALL_ROWS


def _pack_small(parts, loss_block):
    rows = []
    for (name, used, padded), p in zip(_SMALL_PARTS, list(parts) + [loss_block]):
        p = p.reshape(used, LANES)
        if padded > used:
            p = jnp.pad(p, ((0, padded - used), (0, 0)))
        rows.append(p)
    return jnp.concatenate(rows, axis=0)


def _local_step(x, mem, target, g_norm, w_in, w_s, b_s, g_v, g_mem, late_weights, g_final,
                fwd_token=None, on_dw=None):
    B, S, _ = x.shape
    x2d = x.reshape(B * S, D_MODEL)
    t2d = target.reshape(B * S, D_MODEL)
    mem2d = mem.reshape(B * N_MEM, D_MODEL)

    proj = _inproj_fwd(x2d, g_norm, w_in, after=() if fwd_token is None else (fwd_token,))
    w_kv, w_out = late_weights(proj)
    kv = _kv_fwd(mem2d, g_mem, w_kv)
    a, lse = _attn_fwd(proj, B, S)
    w_sT = jnp.swapaxes(w_s, 1, 2)
    b_tab = jnp.repeat(b_s.T, HEAD_DIM, axis=1)
    (dx2, da, drest, loss, d_wout, d_ws, d_bs, d_gv, d_gf, dkv) = _mid(
        x2d, t2d, a, proj, kv, w_s, w_sT, b_tab, g_v, w_out, g_final, B, S)
    d_wkv, d_gmem = _kv_bwd(mem2d, g_mem, w_kv, dkv)
    dq, dk, dv = _attn_bwd(proj, a, lse, da, B, S)
    if on_dw is None:
        d_win = _inproj_bwd_dw(dq, dk, dv, drest, x2d, g_norm)
        after = ()
    else:
        d_win = None
        by_chip = lambda g: g.reshape((N_CHIPS, g.shape[0] // N_CHIPS, g.shape[1]))
        after = (on_dw(*_inproj_bwd_dw(dq, dk, dv, drest, x2d, g_norm, reduce_with=[by_chip(d_wkv), by_chip(d_wout)])),)
    grad_x, d_gnorm = _inproj_bwd_dx(dq, dk, dv, drest, x2d, dx2, g_norm, w_in, after=after)
    d_bs = d_bs[:, :N_SGU_GROUPS].T
    return (loss, grad_x.reshape(B, S, D_MODEL),
            dict(g_norm=d_gnorm, w_in=d_win, w_s=d_ws, b_s=d_bs, g_v=d_gv, g_mem=d_gmem, w_kv=d_wkv,
                 w_out=d_wout, g_final=d_gf))


def kernel(x, mem, g_norm, w_in, w_sgu_spatial, b_sgu_spatial, g_sgu_v, g_mem, w_mem_kv, w_out, g_final, loss_target, m_g_norm, m_w_in, m_w_sgu_spatial, m_b_sgu_spatial, m_g_sgu_v, m_g_mem, m_w_mem_kv, m_w_out, m_g_final, v_g_norm, v_w_in, v_w_sgu_spatial, v_b_sgu_spatial, v_g_sgu_v, v_g_mem, v_w_mem_kv, v_w_out, v_g_final):
    t = lambda w: jnp.swapaxes(w[0], 0, 1)
    (win_all,), late_shards, late_lands = _ag_weights([t(w_in)], [w_mem_kv[0], w_out[0]])
    w_in_full = win_all.reshape(-1, win_all.shape[-1])
    late = _exchange_start(True, list(late_shards), win_all, "gather_late_start", lands=late_lands)

    def late_weights(proj):
        return [z.reshape(-1, z.shape[-1]) for z in _exchange_wait(True, late, proj, "gather_late_wait")]

    scatter = {}

    def on_dw(sends, owns):
        scatter["own"] = owns
        scatter["started"] = _exchange_start(False, list(sends), owns[0], "scatter_start")
        return scatter["started"][-1]

    loss, grad_x, g = _local_step(
        x, mem, loss_target, g_norm, w_in_full, w_sgu_spatial[0], b_sgu_spatial[0], g_sgu_v, g_mem,
        late_weights, g_final.reshape(1, D_MODEL), fwd_token=late[-1], on_dw=on_dw)

    small_names = ("g_norm", "w_s", "b_s", "g_v", "g_mem", "g_final")
    g_small = _pack_small([g[n] for n in small_names], loss)
    landed = _exchange_wait(False, scatter["started"], g_small, "scatter_wait")
    gr_in, gr_kv, gr_out, gr_small = _reduce_last(scatter["own"], landed, g_small)

    small_w = (g_norm, w_sgu_spatial, b_sgu_spatial, g_sgu_v, g_mem, g_final)
    small_m = (m_g_norm, m_w_sgu_spatial, m_b_sgu_spatial, m_g_sgu_v, m_g_mem, m_g_final)
    small_v = (v_g_norm, v_w_sgu_spatial, v_b_sgu_spatial, v_g_sgu_v, v_g_mem, v_g_final)
    rows = lambda ws: [w.reshape(-1, LANES) for w in ws]
    small_new, loss = _adamw_small(gr_small, rows(small_w), rows(small_m), rows(small_v))
    loss = loss.reshape(())
    small = [[z.reshape(w.shape) for z in four] for w, four in zip(small_w, small_new)]
    d_in, nm_in, nv_in = _adamw(t(w_in), gr_in, t(m_w_in), t(v_w_in), "adamw_w_in")
    gr_in, d_in, nm_in, nv_in = [jnp.swapaxes(z, 0, 1) for z in (gr_in, d_in, nm_in, nv_in)]
    d_kv, nm_kv, nv_kv = _adamw(w_mem_kv[0], gr_kv, m_w_mem_kv[0], v_w_mem_kv[0], "adamw_w_kv")
    d_out, nm_out, nv_out = _adamw(w_out[0], gr_out, m_w_out[0], v_w_out[0], "adamw_w_out")

    def leaves(kind, big_in, big_kv, big_out):
        s_norm, s_ws, s_bs, s_gv, s_gmem, s_gf = [four[kind] for four in small]
        return [s_norm, big_in[None], s_ws, s_bs, s_gv, s_gmem, big_kv[None], big_out[None], s_gf]

    return (loss, grad_x, *leaves(0, gr_in, gr_kv, gr_out), *leaves(1, d_in, d_kv, d_out),
            *leaves(2, nm_in, nm_kv, nm_out), *leaves(3, nv_in, nv_kv, nv_out))
```

```python
import functools

import jax
import jax.numpy as jnp
from jax import lax
from jax.experimental import pallas as pl
from jax.experimental.pallas import tpu as pltpu

F32 = jnp.float32
BF16 = jnp.bfloat16
MESH = pl.DeviceIdType.MESH

D_MODEL = 1024
ATTN_WIDTH = 512
SGU_WIDTH = 256
MEM_WIDTH = 256
N_MEM = 256
IN_COLS = 3328
QKV_COLS = 3 * ATTN_WIDTH
REST_COLS = IN_COLS - QKV_COLS
SGU_CHUNK = 128
N_SGU_GROUPS = 4
EPS = 1e-6
NEG_INF = -1e30
DILATIONS = (1, 4, 16)
RADIUS = 64
Q_BLOCK = 128
LANES = 128
HEAD_DIM = 64

ADAM_LR = 0.001
ADAM_B1 = 0.9
ADAM_B2 = 0.999
ADAM_EPS = 1e-08
ADAM_WD = 0.01
ADAM_STEP = 10

N_CHIPS = 4
VMEM_LIMIT = 56 * 1024 * 1024
SMALL_ROWS = 560


def _params(sem=None, vmem=VMEM_LIMIT):
    return pltpu.CompilerParams(dimension_semantics=sem, vmem_limit_bytes=vmem)


def _nn(a, b):
    return jnp.dot(a, b, preferred_element_type=F32)


def _nt(a, b):
    return lax.dot_general(a, b, (((1,), (1,)), ((), ())), preferred_element_type=F32)


def _tn(a, b):
    return lax.dot_general(a, b, (((0,), (0,)), ((), ())), preferred_element_type=F32)


def _rms(x):
    r = lax.rsqrt(jnp.mean(x * x, axis=-1, keepdims=True) + EPS)
    return r, x * r


def _head_masks():
    lane = lax.broadcasted_iota(jnp.int32, (1, LANES), 1)
    lo = lane < HEAD_DIM
    return lo, (lo.astype(F32), (~lo).astype(F32))


def _silu_parts(z):
    s = jax.nn.sigmoid(z)
    return z * s, s * (1.0 + z * (1.0 - s))


def _gelu_parts(x):
    c = 0.7978845608028654
    x2 = x * x
    s = jax.nn.sigmoid((2.0 * c) * (x + 0.044715 * (x * x2)))
    return x * s, s * (1.0 + x * (1.0 - s) * ((2.0 * c) * (1.0 + 3.0 * 0.044715 * x2)))


def _after(tokens):
    return [pl.BlockSpec(memory_space=pl.ANY)] * len(tokens)


def _inproj_fwd(x2d, g_norm, w_in_t, after=()):
    T = x2d.shape[0]
    tm = 512

    def body(x_ref, g_ref, w_ref, *rest):
        o_ref = rest[-1]
        _, xh = _rms(x_ref[...])
        h = (xh * g_ref[...]).astype(BF16)
        o_ref[...] = _nt(h, w_ref[...])

    return pl.pallas_call(
        body, grid=(T // tm,),
        in_specs=[pl.BlockSpec((tm, D_MODEL), lambda i: (i, 0)),
                  pl.BlockSpec((1, D_MODEL), lambda i: (0, 0)),
                  pl.BlockSpec((IN_COLS, D_MODEL), lambda i: (0, 0))] + _after(after),
        out_specs=pl.BlockSpec((tm, IN_COLS), lambda i: (i, 0)),
        out_shape=jax.ShapeDtypeStruct((T, IN_COLS), F32),
        compiler_params=_params(("arbitrary",)), name="inproj_fwd")(x2d, g_norm, w_in_t, *after)


def _kv_fwd(mem2d, g_mem, w_kv):
    Tm = mem2d.shape[0]

    def body(m_ref, g_ref, w_ref, o_ref):
        _, mh = _rms(m_ref[...])
        o_ref[...] = _nn((mh * g_ref[...]).astype(BF16), w_ref[...])

    return pl.pallas_call(
        body, out_shape=jax.ShapeDtypeStruct((Tm, 2 * MEM_WIDTH), F32),
        compiler_params=_params(), name="kv_fwd")(mem2d, g_mem, w_kv)


def _kv_bwd(mem2d, g_mem, w_kv, dkv):
    Tm = mem2d.shape[0]

    def body(m_ref, g_ref, w_ref, dkv_ref, dw_ref, dg_ref):
        _, mh = _rms(m_ref[...])
        memn = (mh * g_ref[...]).astype(BF16)
        dkvb = dkv_ref[...].astype(BF16)
        dw_ref[...] = _tn(memn, dkvb)
        dmemn = _nt(dkvb, w_ref[...])
        dg_ref[...] = jnp.sum(dmemn * mh, axis=0, keepdims=True)

    return pl.pallas_call(
        body, out_shape=(jax.ShapeDtypeStruct((D_MODEL, 2 * MEM_WIDTH), F32),
                         jax.ShapeDtypeStruct((1, D_MODEL), F32)),
        compiler_params=_params(), name="kv_bwd")(mem2d, g_mem, w_kv, dkv)


def _attn_geometry(S):
    geom = []
    for d in DILATIONS:
        L = S // d
        assert L % Q_BLOCK == 0
        geom.append((d, L, min(2 * Q_BLOCK, L), L // Q_BLOCK))
    return geom


def _init_bias(bias_scr, geom, hp):
    row = lax.broadcasted_iota(jnp.int32, (Q_BLOCK, 2 * Q_BLOCK), 0)
    col = lax.broadcasted_iota(jnp.int32, (Q_BLOCK, 2 * Q_BLOCK), 1)
    for j in (0, 1):
        bits = (126 - (2 * hp + j)) * (1 << 23)
        slope = lax.bitcast_convert_type(jnp.full((1, 1), bits, jnp.int32), F32)
        for di, (d, _, _, _) in enumerate(geom):
            for cls, off in enumerate((0, -RADIUS, -2 * RADIUS)):
                dist = jnp.abs(col - row + off)
                bias_scr[di * 6 + cls * 2 + j] = jnp.where(
                    dist <= RADIUS, -(slope * float(d)) * dist.astype(F32), NEG_INF)


SPLIT = 4
COPY_ROWS = 256


def _by4_rows(S, step):
    per_class = S // SPLIT // COPY_ROWS
    r, j = step // per_class, step % per_class
    return (pl.ds(r + SPLIT * j * COPY_ROWS, COPY_ROWS, stride=SPLIT),
            pl.ds(pl.multiple_of(r * (S // SPLIT) + j * COPY_ROWS, COPY_ROWS), COPY_ROWS))


def _to_by4(src, dst, S):
    def step(i, carry):
        natural, by4 = _by4_rows(S, i)
        dst[by4, :] = src[natural, :]
        return carry
    lax.fori_loop(0, S // COPY_ROWS, step, 0)


def _block_slices(d, L, KW, nqb, r, qb, S):
    qs = qb * Q_BLOCK
    ks = jnp.clip(qs - RADIUS, 0, L - KW)
    cls = jnp.where(qb == 0, 0, jnp.where(qb == nqb - 1, 2, 1))
    if d == 1:
        qsl = pl.ds(pl.multiple_of(qs, Q_BLOCK), Q_BLOCK)
        ksl = pl.ds(pl.multiple_of(ks, RADIUS), KW)
    elif d == SPLIT:
        qsl = pl.ds(pl.multiple_of(r * L + qs, Q_BLOCK), Q_BLOCK)
        ksl = pl.ds(pl.multiple_of(r * L + ks, RADIUS), KW)
    else:
        sub = d // SPLIT
        base = (r % SPLIT) * (S // SPLIT) + r // SPLIT
        qsl = pl.ds(base + qs * sub, Q_BLOCK, stride=sub)
        ksl = pl.ds(base + ks * sub, KW, stride=sub)
    return qsl, ksl, cls


def _for_groups(geom, S, group, fn):
    for di, (d, L, KW, nqb) in enumerate(geom):
        n = group[di]
        assert (d * nqb) % n == 0

        def step(it, carry, di=di, d=d, L=L, KW=KW, nqb=nqb, n=n):
            slices = []
            for g in range(n):
                i = it * n + g
                slices.append(_block_slices(d, L, KW, nqb, i // nqb, i % nqb, S))
            fn(di, KW, slices)
            return carry
        lax.fori_loop(0, d * nqb // n, step, 0)


def _attn_fwd(proj, B, S):
    T = B * S
    geom = _attn_geometry(S)
    n_pairs = ATTN_WIDTH // LANES

    def body(q_ref, k_ref, v_ref, a_ref, lse_ref, bias_scr, q4, k4, v4, *per_dilation):
        o_scr, m_scr, l_scr = per_dilation[0:3], per_dilation[3:6], per_dilation[6:9]
        lo, hm = _head_masks()
        pair = pl.program_id(0)

        @pl.when(pl.program_id(1) == 0)
        def _():
            _init_bias(bias_scr, geom, pair)
        for src, dst in ((q_ref, q4), (k_ref, k4), (v_ref, v4)):
            _to_by4(src, dst, S)

        def group(di, KW, slices):
            chains = [(g, j) for g in range(len(slices)) for j in (0, 1)]
            q_src, k_src, v_src = (q_ref, k_ref, v_ref) if di == 0 else (q4, k4, v4)
            q = [q_src[qsl, :] for qsl, _, _ in slices]
            kw = [k_src[ksl, :].astype(BF16) for _, ksl, _ in slices]
            vw = [v_src[ksl, :].astype(BF16) for _, ksl, _ in slices]
            s = {(g, j): _nt((q[g] * (hm[j] * 0.125)).astype(BF16), kw[g])
                 + bias_scr[di * 6 + slices[g][2] * 2 + j, :, pl.ds(0, KW)] for g, j in chains}
            m = {c: jnp.max(s[c], axis=1, keepdims=True) for c in chains}
            p = {c: jnp.exp(s[c] - m[c]) for c in chains}
            l = {c: jnp.sum(p[c], axis=1, keepdims=True) for c in chains}
            o = {(g, j): _nn(p[(g, j)].astype(BF16), vw[g]) for g, j in chains}
            for g, (qsl, _, _) in enumerate(slices):
                o_scr[di][qsl, :] = jnp.where(lo, o[(g, 0)], o[(g, 1)])
                m_scr[di][qsl, :] = jnp.where(lo, m[(g, 0)], m[(g, 1)])
                l_scr[di][qsl, :] = jnp.where(lo, l[(g, 0)], l[(g, 1)])

        _for_groups(geom, S, (16, 16, 16), group)

        def combine(i, carry):
            natural, by4 = _by4_rows(S, i)
            rows = [natural, by4, by4]
            ms = [m_scr[di][rows[di], :] for di in range(3)]
            mx = jnp.maximum(jnp.maximum(ms[0], ms[1]), ms[2])
            num = 0.0
            den = 0.0
            for di in range(3):
                w = jnp.exp(ms[di] - mx)
                num = num + w * o_scr[di][rows[di], :]
                den = den + w * l_scr[di][rows[di], :]
            a_ref[natural, :] = num / den
            lse_ref[natural, :] = mx + jnp.log(den)
            return carry

        lax.fori_loop(0, S // COPY_ROWS, combine, 0)

    blk = lambda off: pl.BlockSpec((S, LANES), lambda h, b, off=off: (b, off + h))
    out_blk = pl.BlockSpec((S, LANES), lambda h, b: (b, h))
    return pl.pallas_call(
        body, grid=(n_pairs, B),
        in_specs=[blk(0), blk(n_pairs), blk(2 * n_pairs)],
        out_specs=[out_blk, out_blk],
        out_shape=[jax.ShapeDtypeStruct((T, ATTN_WIDTH), F32)] * 2,
        scratch_shapes=[pltpu.VMEM((18, Q_BLOCK, 2 * Q_BLOCK), F32)] + [pltpu.VMEM((S, LANES), F32)] * 12,
        compiler_params=_params(("arbitrary", "arbitrary")), name="attn_fwd")(proj, proj, proj)


def _attn_bwd(proj, a, lse, da, B, S):
    T = B * S
    geom = _attn_geometry(S)
    n_pairs = ATTN_WIDTH // LANES

    def body(q_ref, k_ref, v_ref, a_ref, lse_ref, do_ref, dq_ref, dk_ref, dv_ref, bias_scr, *scr):
        acc = (scr[0:3], scr[3:6])
        natural_in = (q_ref, k_ref, v_ref, a_ref, lse_ref, do_ref)
        by4_in = scr[6:12]
        _, hm = _head_masks()
        pair = pl.program_id(0)

        @pl.when(pl.program_id(1) == 0)
        def _():
            _init_bias(bias_scr, geom, pair)
        for ref in scr[0:6]:
            ref[...] = jnp.zeros_like(ref)
        for src, dst in zip(natural_in, by4_in):
            _to_by4(src, dst, S)

        def group(di, KW, slices):
            n = len(slices)
            chains = [(g, j) for g in range(n) for j in (0, 1)]
            q_src, k_src, v_src, a_src, lse_src, do_src = natural_in if di == 0 else by4_in
            dq_scr, dk_scr, dv_scr = acc[0 if di == 0 else 1]
            q = [q_src[qsl, :] for qsl, _, _ in slices]
            do = [do_src[qsl, :] for qsl, _, _ in slices]
            doa = [do[g] * a_src[slices[g][0], :] for g in range(n)]
            lse_q = [lse_src[qsl, :] for qsl, _, _ in slices]
            kw = [k_src[ksl, :].astype(BF16) for _, ksl, _ in slices]
            vw = [v_src[ksl, :].astype(BF16) for _, ksl, _ in slices]
            qj = {(g, j): (q[g] * (hm[j] * 0.125)).astype(BF16) for g, j in chains}
            doj = {(g, j): (do[g] * hm[j]).astype(BF16) for g, j in chains}
            s = {(g, j): _nt(qj[(g, j)], kw[g])
                 + bias_scr[di * 6 + slices[g][2] * 2 + j, :, pl.ds(0, KW)] for g, j in chains}
            dp = {(g, j): _nt(doj[(g, j)], vw[g]) for g, j in chains}
            dsum = {(g, j): jnp.sum(doa[g] * hm[j], axis=1, keepdims=True) for g, j in chains}
            p = {(g, j): jnp.exp(s[(g, j)] - lse_q[g][:, HEAD_DIM * j:HEAD_DIM * j + 1]) for g, j in chains}
            ds = {c: (p[c] * (dp[c] - dsum[c])).astype(BF16) for c in chains}
            pb = {c: p[c].astype(BF16) for c in chains}
            dq = [_nn(ds[(g, 0)], kw[g]) * (hm[0] * 0.125) + _nn(ds[(g, 1)], kw[g]) * (hm[1] * 0.125)
                  for g in range(n)]
            both = lambda t, g: jnp.concatenate([t[(g, 0)], t[(g, 1)]], axis=0)
            dkw = [_tn(both(ds, g), both(qj, g)) for g in range(n)]
            dvw = [_tn(both(pb, g), both(doj, g)) for g in range(n)]
            for g, (qsl, ksl, _) in enumerate(slices):
                dq_scr[qsl, :] = dq_scr[qsl, :] + dq[g]
                dk_scr[ksl, :] = dk_scr[ksl, :] + dkw[g]
                dv_scr[ksl, :] = dv_scr[ksl, :] + dvw[g]

        _for_groups(geom, S, (4, 4, 16), group)

        def merge(i, carry):
            natural, by4 = _by4_rows(S, i)
            for nat, split in zip(*acc):
                nat[natural, :] = nat[natural, :] + split[by4, :]
            return carry
        lax.fori_loop(0, S // COPY_ROWS, merge, 0)
        for out, nat in zip((dq_ref, dk_ref, dv_ref), acc[0]):
            out[...] = nat[...].astype(BF16)

    blk = lambda off: pl.BlockSpec((S, LANES), lambda h, b, off=off: (b, off + h))
    return pl.pallas_call(
        body, grid=(n_pairs, B),
        in_specs=[blk(0), blk(n_pairs), blk(2 * n_pairs), blk(0), blk(0), blk(0)],
        out_specs=[blk(0), blk(0), blk(0)],
        out_shape=[jax.ShapeDtypeStruct((T, ATTN_WIDTH), BF16)] * 3,
        scratch_shapes=[pltpu.VMEM((18, Q_BLOCK, 2 * Q_BLOCK), F32)] + [pltpu.VMEM((S, LANES), F32)] * 12,
        compiler_params=_params(("arbitrary", "arbitrary")), name="attn_bwd")(proj, proj, proj, a, lse, da)


def _mid(x2d, t2d, a, proj, kv, w_s, w_sT, b_tab, g_v, w_out, g_final, B, S):
    T = B * S
    tm = 512
    nt = S // tm
    halves = 2
    hrows = tm // halves

    def body(x_ref, t_ref, a_ref, za_ref, ub_ref, vb_ref, zb_ref, qm_ref, zm_ref, kv_ref,
              ws_ref, wsT_ref, btab_ref, gv_ref, wout_ref, gf_ref,
              dx2_ref, da_ref, drest_ref, loss_ref, dwout_ref, dws_ref, dbs_ref, dgv_ref, dgf_ref, dkv_ref,
              dbtab_scr):
        b = pl.program_id(0)
        t = pl.program_id(1)
        first = jnp.logical_and(b == 0, t == 0)
        last = jnp.logical_and(b == B - 1, t == nt - 1)
        _, hm = _head_masks()
        lane_g = lax.broadcasted_iota(jnp.int32, (1, SGU_WIDTH), 1) // HEAD_DIM
        gm = [(lane_g == g).astype(F32) for g in range(N_SGU_GROUPS)]
        H = range(halves)
        rows = [pl.ds(h * hrows, hrows) for h in H]
        ld = lambda ref: [ref[r, :] for r in rows]
        cat = lambda parts, axis: jnp.concatenate(parts, axis=axis)
        chunks = [slice(ci * SGU_CHUNK, (ci + 1) * SGU_CHUNK) for ci in range(hrows // SGU_CHUNK)]
        pairs = [slice(pr * LANES, (pr + 1) * LANES) for pr in range(2)]
        heads = [(pr, j) for pr in range(2) for j in (0, 1)]

        @pl.when(first)
        def _():
            loss_ref[...] = jnp.zeros_like(loss_ref)
            dwout_ref[...] = jnp.zeros_like(dwout_ref)
            dws_ref[...] = jnp.zeros_like(dws_ref)
            dbs_ref[...] = jnp.zeros_like(dbs_ref)
            dgv_ref[...] = jnp.zeros_like(dgv_ref)
            dgf_ref[...] = jnp.zeros_like(dgf_ref)
            dbtab_scr[...] = jnp.zeros_like(dbtab_scr)

        @pl.when(t == 0)
        def _():
            dkv_ref[...] = jnp.zeros_like(dkv_ref)

        a_val = ld(a_ref)
        sil_a = [_silu_parts(z) for z in ld(za_ref)]
        gated_a = [s[0] * a for s, a in zip(sil_a, a_val)]
        u = [_gelu_parts(z) for z in ld(ub_ref)]
        vv = [_gelu_parts(z) for z in ld(vb_ref)]
        vnorm = [_rms(v[0]) for v in vv]
        gv = gv_ref[...]
        vn = [(n[1] * gv).astype(BF16) for n in vnorm]
        w_cat = cat([ws_ref[g].astype(BF16) for g in range(N_SGU_GROUPS)], 1)
        wT_cat = cat([wsT_ref[g].astype(BF16) for g in range(N_SGU_GROUPS)], 1)
        gmb = [m.astype(BF16) for m in gm]
        by_group = lambda chunk: cat([chunk * gmb[g] for g in range(N_SGU_GROUPS)], 0)
        btab = btab_ref[...]
        mixed = [cat([btab + _nn(w_cat, by_group(vn[h][c, :])) for c in chunks], 0) for h in H]
        sg = [u[h][0] * mixed[h] for h in H]
        sil_b = [_silu_parts(z) for z in ld(zb_ref)]
        gated_b = [sil_b[h][0] * sg[h] for h in H]

        kvv = kv_ref[...].astype(BF16)
        kp = [kvv[:, p] for p in pairs]
        vp = [kvv[:, MEM_WIDTH + pr * LANES:MEM_WIDTH + (pr + 1) * LANES] for pr in range(2)]
        qm = ld(qm_ref)
        qj = {(h, pr, j): (qm[h][:, pairs[pr]] * (hm[j] * 0.125)).astype(BF16) for h in H for pr, j in heads}
        sc = {k: _nt(qj[k], kp[k[1]]) for k in qj}
        ex = {k: jnp.exp(sc[k] - jnp.max(sc[k], axis=1, keepdims=True)) for k in qj}
        prob = {k: ex[k] * (1.0 / jnp.sum(ex[k], axis=1, keepdims=True)) for k in qj}
        probb = {k: prob[k].astype(BF16) for k in qj}
        mo = [cat([sum(_nn(probb[(h, pr, j)], vp[pr]) * hm[j] for j in (0, 1)) for pr in range(2)], 1) for h in H]
        sil_m = [_silu_parts(z) for z in ld(zm_ref)]
        gated_m = [sil_m[h][0] * mo[h] for h in H]

        gated = [cat([gated_a[h], gated_b[h], gated_m[h]], 1).astype(BF16) for h in H]
        wout = wout_ref[...]
        x_in = ld(x_ref)
        x2 = [x_in[h] + _nn(gated[h], wout) for h in H]
        fin = [_rms(z) for z in x2]
        gf = gf_ref[...]
        tgt = ld(t_ref)
        err = [fin[h][1] * gf - tgt[h] for h in H]
        loss_ref[...] += sum(jnp.sum(e * e) for e in err) * (0.5 / D_MODEL)

        dy = [e * (1.0 / D_MODEL) for e in err]
        dgf_ref[...] += sum(jnp.sum(dy[h] * fin[h][1], axis=0, keepdims=True) for h in H)
        gdy = [d * gf for d in dy]
        dx2 = [fin[h][0] * (gdy[h] - fin[h][1] * jnp.mean(gdy[h] * fin[h][1], axis=1, keepdims=True)) for h in H]
        for h in H:
            dx2_ref[rows[h], :] = dx2[h]
        dx2b = [d.astype(BF16) for d in dx2]
        dgated = [_nt(d, wout) for d in dx2b]
        dwout_ref[...] += _tn(cat(gated, 0), cat(dx2b, 0))
        dga = [d[:, 0:ATTN_WIDTH] for d in dgated]
        dgb = [d[:, ATTN_WIDTH:ATTN_WIDTH + SGU_WIDTH] for d in dgated]
        dgm = [d[:, ATTN_WIDTH + SGU_WIDTH:] for d in dgated]

        for h in H:
            da_ref[rows[h], :] = dga[h] * sil_a[h][0]
        dza = [dga[h] * a_val[h] * sil_a[h][1] for h in H]

        dsg = [dgb[h] * sil_b[h][0] for h in H]
        dzb = [dgb[h] * sg[h] * sil_b[h][1] for h in H]
        dub = [dsg[h] * mixed[h] * u[h][1] for h in H]
        dmixed = [dsg[h] * u[h][0] for h in H]
        dmixed_b = [d.astype(BF16) for d in dmixed]
        dvn = [cat([_nn(wT_cat, by_group(dmixed_b[h][c, :])) for c in chunks], 0) for h in H]
        for g in range(N_SGU_GROUPS):
            dws_ref[g] += sum(_nt((dmixed[h][c, :] * gm[g]).astype(BF16), vn[h][c, :]) for h in H for c in chunks)
        dbtab_scr[...] += sum(dmixed[h][c, :] for h in H for c in chunks)
        dgv_ref[...] += sum(jnp.sum(dvn[h] * vnorm[h][1], axis=0, keepdims=True) for h in H)
        tv = [d * gv for d in dvn]
        dvv = [vnorm[h][0] * (tv[h] - vnorm[h][1] * jnp.mean(tv[h] * vnorm[h][1], axis=1, keepdims=True)) for h in H]
        dvb = [dvv[h] * vv[h][1] for h in H]

        dmo = [dgm[h] * sil_m[h][0] for h in H]
        dzm = [dgm[h] * mo[h] * sil_m[h][1] for h in H]
        dmoj = {(h, pr, j): (dmo[h][:, pairs[pr]] * hm[j]).astype(BF16) for h in H for pr, j in heads}
        dp = {k: _nt(dmoj[k], vp[k[1]]) for k in qj}
        ds = {k: (prob[k] * (dp[k] - jnp.sum(dp[k] * prob[k], axis=1, keepdims=True))).astype(BF16) for k in qj}
        dqm = [cat([sum(_nn(ds[(h, pr, j)], kp[pr]) * (hm[j] * 0.125) for j in (0, 1)) for pr in range(2)], 1)
               for h in H]
        every = lambda tbl, pr: cat([tbl[(h, pr, j)] for h in H for j in (0, 1)], 0)
        dk = [_tn(every(ds, pr), every(qj, pr)) for pr in range(2)]
        dv = [_tn(every(probb, pr), every(dmoj, pr)) for pr in range(2)]
        dkv_ref[...] += cat(dk + dv, 1)

        for h in H:
            drest_ref[rows[h], :] = cat([dza[h], dub[h], dvb[h], dzb[h], dqm[h], dzm[h]], 1).astype(BF16)

        @pl.when(last)
        def _():
            lane = lax.broadcasted_iota(jnp.int32, (1, LANES), 1)
            dbt = dbtab_scr[...]
            out = jnp.zeros((SGU_CHUNK, LANES), F32)
            for g in range(N_SGU_GROUPS):
                out = out + jnp.where(lane == g, jnp.sum(dbt * gm[g], axis=1, keepdims=True), 0.0)
            dbs_ref[...] = out

    tile = lambda w, cb: pl.BlockSpec((tm, w), lambda b, t, cb=cb: (b * nt + t, cb))
    const = lambda shape: pl.BlockSpec(shape, lambda b, t, n=len(shape): (0,) * n)
    return pl.pallas_call(
        body, grid=(B, nt),
        in_specs=[tile(D_MODEL, 0), tile(D_MODEL, 0), tile(ATTN_WIDTH, 0),
                  tile(ATTN_WIDTH, 3),
                  tile(SGU_WIDTH, 8), tile(SGU_WIDTH, 9), tile(SGU_WIDTH, 10),
                  tile(MEM_WIDTH, 11), tile(MEM_WIDTH, 12),
                  pl.BlockSpec((N_MEM, 2 * MEM_WIDTH), lambda b, t: (b, 0)),
                  const((N_SGU_GROUPS, SGU_CHUNK, SGU_CHUNK)), const((N_SGU_GROUPS, SGU_CHUNK, SGU_CHUNK)),
                  const((SGU_CHUNK, SGU_WIDTH)), const((1, SGU_WIDTH)),
                  const((D_MODEL, D_MODEL)), const((1, D_MODEL))],
        out_specs=[tile(D_MODEL, 0), tile(ATTN_WIDTH, 0), tile(REST_COLS, 0),
                   const((8, LANES)), const((D_MODEL, D_MODEL)),
                   const((N_SGU_GROUPS, SGU_CHUNK, SGU_CHUNK)), const((SGU_CHUNK, LANES)),
                   const((1, SGU_WIDTH)), const((1, D_MODEL)),
                   pl.BlockSpec((N_MEM, 2 * MEM_WIDTH), lambda b, t: (b, 0))],
        out_shape=[jax.ShapeDtypeStruct((T, D_MODEL), F32), jax.ShapeDtypeStruct((T, ATTN_WIDTH), F32),
                   jax.ShapeDtypeStruct((T, REST_COLS), BF16),
                   jax.ShapeDtypeStruct((8, LANES), F32), jax.ShapeDtypeStruct((D_MODEL, D_MODEL), F32),
                   jax.ShapeDtypeStruct((N_SGU_GROUPS, SGU_CHUNK, SGU_CHUNK), F32),
                   jax.ShapeDtypeStruct((SGU_CHUNK, LANES), F32),
                   jax.ShapeDtypeStruct((1, SGU_WIDTH), F32), jax.ShapeDtypeStruct((1, D_MODEL), F32),
                   jax.ShapeDtypeStruct((B * N_MEM, 2 * MEM_WIDTH), F32)],
        scratch_shapes=[pltpu.VMEM((SGU_CHUNK, SGU_WIDTH), F32)],
        compiler_params=_params(("arbitrary", "arbitrary")), name="mid")(
            x2d, t2d, a, proj, proj, proj, proj, proj, proj, kv, w_s, w_sT, b_tab, g_v, w_out, g_final)


def _inproj_bwd_dx(dq, dk, dv, drest, x2d, dx2, g_norm, w_in_t, after=()):
    T = x2d.shape[0]
    tm = 512
    W = ATTN_WIDTH

    def body(dq_ref, dk_ref, dv_ref, dr_ref, x_ref, dx2_ref, g_ref, w_ref, *rest):
        gx_ref, dg_ref = rest[-2:]

        @pl.when(pl.program_id(0) == 0)
        def _():
            dg_ref[...] = jnp.zeros_like(dg_ref)

        halves = [pl.ds(h * (tm // 2), tm // 2) for h in (0, 1)]
        dh = [(_nn(dq_ref[r, :], w_ref[0:W, :]) + _nn(dk_ref[r, :], w_ref[W:2 * W, :])
               + _nn(dv_ref[r, :], w_ref[2 * W:3 * W, :]) + _nn(dr_ref[r, :], w_ref[QKV_COLS:IN_COLS, :]))
              for r in halves]
        nrm = [_rms(x_ref[r, :]) for r in halves]
        dg_ref[...] += sum(jnp.sum(d * n[1], axis=0, keepdims=True) for d, n in zip(dh, nrm))
        g = g_ref[...]
        for r, d, (rstd, xh) in zip(halves, dh, nrm):
            th = d * g
            gx_ref[r, :] = rstd * (th - xh * jnp.mean(th * xh, axis=1, keepdims=True)) + dx2_ref[r, :]

    tile = lambda w: pl.BlockSpec((tm, w), lambda i: (i, 0))
    return pl.pallas_call(
        body, grid=(T // tm,),
        in_specs=[tile(W), tile(W), tile(W), tile(REST_COLS), tile(D_MODEL), tile(D_MODEL),
                  pl.BlockSpec((1, D_MODEL), lambda i: (0, 0)),
                  pl.BlockSpec((IN_COLS, D_MODEL), lambda i: (0, 0))] + _after(after),
        out_specs=[tile(D_MODEL), pl.BlockSpec((1, D_MODEL), lambda i: (0, 0))],
        out_shape=[jax.ShapeDtypeStruct((T, D_MODEL), F32), jax.ShapeDtypeStruct((1, D_MODEL), F32)],
        compiler_params=_params(("arbitrary",)), name="inproj_bwd_dx")(
            dq, dk, dv, drest, x2d, dx2, g_norm, w_in_t, *after)


def _inproj_bwd_dw(dq, dk, dv, drest, x2d, g_norm, reduce_with=None):
    T = x2d.shape[0]
    tm = 512
    nt = T // tm
    W = ATTN_WIDTH
    fused = reduce_with is not None
    others = list(reduce_with) if fused else []
    ns = 1 + len(others)
    shard = IN_COLS // N_CHIPS
    halves = [shard // 2] + [s.shape[1] // 2 for s in others]
    cols = [D_MODEL] + [s.shape[2] for s in others]
    row_block = 32

    def body(dq_ref, dk_ref, dv_ref, dr_ref, x_ref, g_ref, *rest):
        if fused:
            stacks = rest[:ns - 1]
            sends, owns = rest[ns - 1:2 * ns - 1], rest[2 * ns - 1:3 * ns - 1]
            acc, ras, narrow = rest[3 * ns - 1], rest[3 * ns:4 * ns], rest[4 * ns]
            s_sem, r_sem = rest[4 * ns + 1], rest[4 * ns + 2]
            x, y, c, chip, peers, peer_chip = _place()
            sib = (x, y, 1 - c)

            def part(w, k, cc, r0=0, rows=None):
                n = halves[w]
                rows = n if rows is None else rows
                if w == 0:
                    return acc.at[pl.ds(pl.multiple_of(k * shard + cc * n + r0, 8), rows), :]
                return stacks[w - 1].at[k, pl.ds(pl.multiple_of(cc * n + r0, 8), rows), :]

            def swap_other(w):
                theirs = stacks[w - 1].at[:, pl.ds(pl.multiple_of((1 - c) * halves[w], 8), halves[w]), :]
                return _remote(theirs, ras[w], s_sem.at[N_CHIPS - 1 + w], r_sem.at[N_CHIPS - 1 + w], sib)

            def swap_win(k):
                return _remote(narrow.at[k], ras[0].at[k], s_sem.at[k], r_sem.at[k], sib)
        else:
            acc = rest[0]

        @pl.when(pl.program_id(0) == 0)
        def _():
            acc[...] = jnp.zeros_like(acc)
            for w in range(1, ns):
                swap_other(w).start()

        _, xh = _rms(x_ref[...])
        h = (xh * g_ref[...]).astype(BF16)
        acc[0:W, :] += _tn(dq_ref[...], h)
        acc[W:2 * W, :] += _tn(dk_ref[...], h)
        acc[2 * W:3 * W, :] += _tn(dv_ref[...], h)
        acc[QKV_COLS:IN_COLS, :] += _tn(dr_ref[...], h)

        if fused:
            @pl.when(pl.program_id(0) == nt - 1)
            def _():
                for k in range(N_CHIPS):
                    def to_bf16(i, carry, k=k):
                        r0 = pl.multiple_of(i * row_block, row_block)
                        narrow[k, pl.ds(r0, row_block), :] = part(0, k, 1 - c, r0, row_block)[...].astype(BF16)
                        return carry
                    lax.fori_loop(0, halves[0] // row_block, to_bf16, 0)
                    swap_win(k).start()
                def chip_sum(w, k, r0):
                    blk = pl.ds(r0, row_block)
                    return part(w, k, c, r0, row_block)[...] + ras[w][k, blk, :].astype(F32)

                for w in range(1, ns):
                    swap_other(w).wait_recv()

                    def sums(i, carry, w=w):
                        r0 = pl.multiple_of(i * row_block, row_block)
                        for m in range(3):
                            sends[w][m, pl.ds(r0, row_block), :] = chip_sum(w, peer_chip[m], r0).astype(BF16)
                        owns[w][pl.ds(r0, row_block), :] = chip_sum(w, chip, r0)
                        return carry
                    lax.fori_loop(0, halves[w] // row_block, sums, 0)
                for k in range(N_CHIPS):
                    swap_win(k).wait_recv()

                    @pl.when(chip == k)
                    def _(k=k):
                        def own(i, carry):
                            r0 = pl.multiple_of(i * row_block, row_block)
                            owns[0][pl.ds(r0, row_block), :] = chip_sum(0, k, r0)
                            return carry
                        lax.fori_loop(0, halves[0] // row_block, own, 0)

                    @pl.when(chip != k)
                    def _(k=k):
                        def other(i, carry):
                            r0 = pl.multiple_of(i * row_block, row_block)
                            sends[0][(k ^ chip) - 1, pl.ds(r0, row_block), :] = chip_sum(0, k, r0).astype(BF16)
                            return carry
                        lax.fori_loop(0, halves[0] // row_block, other, 0)
                for k in range(N_CHIPS):
                    swap_win(k).wait_send()
                for w in range(1, ns):
                    swap_other(w).wait_send()

    tile = lambda w: pl.BlockSpec((tm, w), lambda i: (i, 0))
    vmem = pl.BlockSpec(memory_space=pltpu.VMEM)
    in_specs = [tile(W), tile(W), tile(W), tile(REST_COLS), tile(D_MODEL), pl.BlockSpec((1, D_MODEL), lambda i: (0, 0))]
    if not fused:
        return pl.pallas_call(
            body, grid=(nt,), in_specs=in_specs,
            out_specs=pl.BlockSpec((IN_COLS, D_MODEL), lambda i: (0, 0)),
            out_shape=jax.ShapeDtypeStruct((IN_COLS, D_MODEL), F32),
            compiler_params=_params(("arbitrary",)), name="inproj_bwd_dw")(dq, dk, dv, drest, x2d, g_norm)
    outs = pl.pallas_call(
        body, grid=(nt,), in_specs=in_specs + [vmem] * (ns - 1), out_specs=[vmem] * (2 * ns),
        out_shape=[jax.ShapeDtypeStruct((3, n, cl), BF16) for n, cl in zip(halves, cols)]
        + [jax.ShapeDtypeStruct((n, cl), F32) for n, cl in zip(halves, cols)],
        scratch_shapes=[pltpu.VMEM((IN_COLS, D_MODEL), F32)]
        + [pltpu.VMEM((N_CHIPS, n, cl), BF16 if w == 0 else F32) for w, (n, cl) in enumerate(zip(halves, cols))]
        + [pltpu.VMEM((N_CHIPS, halves[0], D_MODEL), BF16)]
        + [pltpu.SemaphoreType.DMA((N_CHIPS - 1 + ns,)), pltpu.SemaphoreType.DMA((N_CHIPS - 1 + ns,))],
        compiler_params=_params(("arbitrary",)), name="inproj_bwd_dw_reduce")(
            dq, dk, dv, drest, x2d, g_norm, *others)
    return outs[:ns], outs[ns:]


def _adamw_update(w, g, m, v):
    nm = ADAM_B1 * m + (1.0 - ADAM_B1) * g
    nv = ADAM_B2 * v + (1.0 - ADAM_B2) * (g * g)
    m_hat = nm / (1.0 - ADAM_B1 ** ADAM_STEP)
    v_hat = nv / (1.0 - ADAM_B2 ** ADAM_STEP)
    return -ADAM_LR * (m_hat / (jnp.sqrt(v_hat) + ADAM_EPS) + ADAM_WD * w), nm, nv


def _adamw(w, g, m, v, name):
    R, C = w.shape
    br = max(r for r in range(8, 257, 8) if R % r == 0)

    def body(w_ref, g_ref, m_ref, v_ref, g_out, d_ref, nm_ref, nv_ref):
        g = g_ref[...]
        g_out[...] = g
        d_ref[...], nm_ref[...], nv_ref[...] = _adamw_update(w_ref[...], g, m_ref[...], v_ref[...])

    spec = pl.BlockSpec((br, C), lambda i: (i, 0))
    return pl.pallas_call(
        body, grid=(R // br,), in_specs=[spec] * 4, out_specs=[spec] * 4,
        out_shape=[jax.ShapeDtypeStruct((R, C), F32)] * 4,
        compiler_params=_params(("arbitrary",)), name=name)(w, g, m, v)


def _adamw_small(g_packed, ws, ms, vs):
    n = len(ws)

    def body(*refs):
        g_ref = refs[0]
        w_refs, m_refs, v_refs = refs[1:1 + n], refs[1 + n:1 + 2 * n], refs[1 + 2 * n:1 + 3 * n]
        outs = refs[1 + 3 * n:]
        off = 0
        for i, (_, used, padded) in enumerate(_SMALL_PARTS[:n]):
            g = g_ref[off:off + used, :]
            delta, nm, nv = _adamw_update(w_refs[i][...], g, m_refs[i][...], v_refs[i][...])
            outs[4 * i][...], outs[4 * i + 1][...], outs[4 * i + 2][...], outs[4 * i + 3][...] = g, delta, nm, nv
            off += padded
        outs[4 * n][...] = g_ref[_LOSS_ROW:_LOSS_ROW + 1, 0:1]

    outs = pl.pallas_call(
        body, out_shape=[jax.ShapeDtypeStruct(w.shape, F32) for w in ws for _ in range(4)]
        + [jax.ShapeDtypeStruct((1, 1), F32)],
        compiler_params=_params(), name="adamw_small")(g_packed, *ws, *ms, *vs)
    return [outs[4 * i:4 * i + 4] for i in range(n)], outs[4 * n]


def _place():
    x, y, c = lax.axis_index("x"), lax.axis_index("y"), lax.axis_index("c")
    chip = 2 * x + y
    peers = [(x, 1 - y), (1 - x, y), (1 - x, 1 - y)]
    peer_chip = [2 * px + py for px, py in peers]
    return x, y, c, chip, peers, peer_chip


def _remote(src, dst, send_sem, recv_sem, dev):
    return pltpu.make_async_remote_copy(src_ref=src, dst_ref=dst, send_sem=send_sem, recv_sem=recv_sem,
                                        device_id=dev, device_id_type=MESH)


def _ag_weights(weights, late=()):
    nw, nl = len(weights), len(late)
    pieces = 2

    def body(*refs):
        srcs, late_srcs = refs[:nw], refs[nw:nw + nl]
        outs, late_bf, late_land = (refs[nw + nl:2 * nw + nl], refs[2 * nw + nl:2 * nw + 2 * nl],
                                    refs[2 * nw + 2 * nl:2 * nw + 3 * nl])
        s_ici, r_ici, s_d2d, r_d2d = refs[2 * nw + 3 * nl:]
        x, y, c = lax.axis_index("x"), lax.axis_index("y"), lax.axis_index("c")
        chip = 2 * x + y
        sib = (x, y, 1 - c)
        first = ((x + 1 - c) % 2, (y + c) % 2)
        second = ((x + c) % 2, (y + 1 - c) % 2)
        first_chip, second_chip = 2 * first[0] + first[1], 2 * second[0] + second[1]
        diag_chip = 3 - chip
        for src, out in zip(srcs, outs):
            out[chip] = src[...].astype(BF16)

        parts = [(w, out, pc) for w, out in enumerate(outs) for pc in range(pieces)]

        def piece(out, k, cc, pc):
            rows = out.shape[1] // 2 // pieces
            return out.at[k, pl.ds(pl.multiple_of((cc * pieces + pc) * rows, 16), rows), :]

        def ici(w, slot, out, k, dev, pc):
            blk, sem = piece(out, k, c, pc), (nw * slot + w) * pieces + pc
            return _remote(blk, blk, s_ici.at[sem], r_ici.at[sem], (dev[0], dev[1], c))

        def d2d(w, slot, out, k, cc, pc):
            blk, sem = piece(out, k, cc, pc), (nw * slot + w) * pieces + pc
            return _remote(blk, blk, s_d2d.at[sem], r_d2d.at[sem], sib)

        sent = []
        for slot, dev in enumerate((first, second)):
            for w, out, pc in parts:
                sent.append(ici(w, slot, out, chip, dev, pc))
                sent[-1].start()
        for src, bf, land in zip(late_srcs, late_bf, late_land):
            bf[...] = src[...].astype(BF16)
            land[...] = jnp.zeros_like(land)
            land[chip] = bf[...]
        for slot, k, dev in ((0, first_chip, first), (1, second_chip, second), (2, diag_chip, second)):
            for w, out, pc in parts:
                ici(w, slot, out, k, dev, pc).wait_recv()
                if slot == 0:
                    sent.append(ici(w, 2, out, k, second, pc))
                    sent[-1].start()
                sent.append(d2d(w, slot, out, k, c, pc))
                sent[-1].start()
        for slot, k in ((0, second_chip), (1, first_chip), (2, diag_chip)):
            for w, out, pc in parts:
                d2d(w, slot, out, k, 1 - c, pc).wait_recv()
        for cp in sent:
            cp.wait_send()

    vmem = pl.BlockSpec(memory_space=pltpu.VMEM)
    outs = pl.pallas_call(
        body,
        out_shape=[jax.ShapeDtypeStruct((N_CHIPS,) + w.shape, BF16) for w in weights]
        + [jax.ShapeDtypeStruct(w.shape, BF16) for w in late]
        + [jax.ShapeDtypeStruct((N_CHIPS,) + w.shape, BF16) for w in late],
        in_specs=[vmem] * (nw + nl), out_specs=[vmem] * (nw + 2 * nl),
        scratch_shapes=[pltpu.SemaphoreType.DMA((3 * nw * pieces,))] * 4,
        compiler_params=pltpu.CompilerParams(vmem_limit_bytes=VMEM_LIMIT), name="ag_weights")(*weights, *late)
    return outs[:nw], outs[nw:nw + nl], outs[nw + nl:]


_HBM = pl.BlockSpec(memory_space=pltpu.HBM)
_SEM = pl.BlockSpec(memory_space=pltpu.SEMAPHORE)
_ANY = pl.BlockSpec(memory_space=pl.ANY)
_DATAFLOW = pltpu.SideEffectType.DATAFLOW_SIDE_EFFECTING


def _in_hbm(a):
    return pltpu.with_memory_space_constraint(a, pltpu.HBM)


def _exchange_copies(gather, srcs, lands, send_sems, recv_sems):
    nw = len(srcs)
    x, y, c, chip, peers, peer_chip = _place()
    pairs = []
    for m, (px, py) in enumerate(peers):
        for w in range(nw):
            sems = (send_sems.at[nw * m + w], recv_sems.at[nw * m + w], (px, py, c))
            if gather:
                pairs.append((_remote(srcs[w], lands[w].at[chip], *sems),
                              _remote(srcs[w], lands[w].at[peer_chip[m]], *sems)))
            else:
                pairs.append((_remote(srcs[w].at[m], lands[w].at[m], *sems),) * 2)
    return pairs


def _exchange_start(gather, srcs, after, name, lands=None):
    nw = len(srcs)
    n_copies = 3 * nw

    def body(*refs):
        send_sems, recv_sems = refs[2 * nw + 1], refs[2 * nw + 2]
        for start, _ in _exchange_copies(gather, refs[:nw], refs[nw:2 * nw], send_sems, recv_sems):
            start.start()
        refs[-1][...] = jnp.zeros_like(refs[-1])

    if lands is None:
        lands = [lax.empty(((N_CHIPS,) + s.shape) if gather else s.shape, s.dtype) for s in srcs]
    lands = [_in_hbm(l) for l in lands]
    return pl.pallas_call(
        body, name=name,
        out_shape=(pltpu.SemaphoreType.DMA((n_copies,)), pltpu.SemaphoreType.DMA((n_copies,)))
        + tuple(pltpu.HBM(s.shape, s.dtype) for s in srcs)
        + tuple(pltpu.HBM(l.shape, l.dtype) for l in lands)
        + (jax.ShapeDtypeStruct((8, LANES), F32),),
        in_specs=[_HBM] * (2 * nw) + [_ANY],
        out_specs=(_SEM, _SEM) + (_HBM,) * (2 * nw) + (pl.BlockSpec(memory_space=pltpu.VMEM),),
        input_output_aliases={i: 2 + i for i in range(2 * nw)},
        compiler_params=pltpu.CompilerParams(has_side_effects=_DATAFLOW),
    )(*[_in_hbm(s) for s in srcs], *lands, after)


def _exchange_wait(gather, started, after, name):
    nw = (len(started) - 3) // 2
    send_sems, recv_sems = started[0], started[1]
    thru = started[2:2 + 2 * nw]

    def body(*refs):
        for _, arrival in _exchange_copies(gather, refs[:nw], refs[nw:2 * nw], refs[2 * nw], refs[2 * nw + 1]):
            arrival.wait_send()
            arrival.wait_recv()

    outs = pl.pallas_call(
        body, name=name,
        out_shape=tuple(pltpu.HBM(t.shape, t.dtype) for t in thru),
        in_specs=[_HBM] * (2 * nw) + [_SEM, _SEM, _ANY], out_specs=(_HBM,) * (2 * nw),
        input_output_aliases={i: i for i in range(2 * nw)},
        compiler_params=pltpu.CompilerParams(has_side_effects=_DATAFLOW),
    )(*thru, send_sems, recv_sems, after)
    return outs[nw:]


def _reduce_last(owns, landed, g_small):
    ns = len(owns)
    row_block = 32
    hs = SMALL_ROWS // 2

    def body(*refs):
        own_refs, land_refs, gsm_ref = refs[:ns], refs[ns:2 * ns], refs[2 * ns]
        out_refs, osm_ref = refs[2 * ns + 1:3 * ns + 1], refs[3 * ns + 1]
        ra_sm, p_sm, s_sem, r_sem, sm_s, sm_r = refs[3 * ns + 2:]
        x, y, c, chip, peers, peer_chip = _place()
        sib = (x, y, 1 - c)
        half = lambda cc: pl.ds(pl.multiple_of(cc * hs, 8), hs)
        sm_a = _remote(gsm_ref.at[half(1 - c), :], ra_sm, sm_s.at[0], sm_r.at[0], sib)
        sm_a.start()
        swaps = [sm_a]
        for w in range(ns):
            n = own_refs[w].shape[0]

            def total(i, carry, w=w, n=n):
                r0 = pl.multiple_of(i * row_block, row_block)
                blk = pl.ds(r0, row_block)
                acc = own_refs[w][blk, :]
                for m in range(3):
                    acc = acc + land_refs[w][m, blk, :].astype(F32)
                out_refs[w][pl.ds(pl.multiple_of(c * n + r0, 8), row_block), :] = acc
                return carry
            lax.fori_loop(0, n // row_block, total, 0)
            mine = out_refs[w].at[pl.ds(pl.multiple_of(c * n, 8), n), :]
            swaps.append(_remote(mine, mine, s_sem.at[w], r_sem.at[w], sib))
            swaps[-1].start()
        sm_a.wait_recv()
        p_sm[chip] = gsm_ref[half(c), :] + ra_sm[...]
        for m, (px, py) in enumerate(peers):
            swaps.append(_remote(p_sm.at[chip], p_sm.at[chip], sm_s.at[1 + m], sm_r.at[1 + m], (px, py, c)))
            swaps[-1].start()
        for m, (px, py) in enumerate(peers):
            _remote(p_sm.at[chip], p_sm.at[peer_chip[m]], sm_s.at[1 + m], sm_r.at[1 + m], (px, py, c)).wait_recv()
        osm_ref[half(c), :] = (p_sm[0] + p_sm[1]) + (p_sm[2] + p_sm[3])
        swaps.append(_remote(osm_ref.at[half(c), :], osm_ref.at[half(c), :], sm_s.at[4], sm_r.at[4], sib))
        swaps[-1].start()
        for w in range(ns):
            n = own_refs[w].shape[0]
            theirs = out_refs[w].at[pl.ds(pl.multiple_of((1 - c) * n, 8), n), :]
            _remote(theirs, theirs, s_sem.at[w], r_sem.at[w], sib).wait_recv()
        _remote(osm_ref.at[half(1 - c), :], osm_ref.at[half(1 - c), :], sm_s.at[4], sm_r.at[4], sib).wait_recv()
        for cp in swaps:
            cp.wait_send()

    vmem = pl.BlockSpec(memory_space=pltpu.VMEM)
    return pl.pallas_call(
        body, out_shape=[jax.ShapeDtypeStruct((2 * o.shape[0], o.shape[1]), F32) for o in owns]
        + [jax.ShapeDtypeStruct((SMALL_ROWS, LANES), F32)],
        in_specs=[vmem] * (2 * ns + 1), out_specs=[vmem] * (ns + 1),
        scratch_shapes=[pltpu.VMEM((hs, LANES), F32), pltpu.VMEM((N_CHIPS, hs, LANES), F32),
                        pltpu.SemaphoreType.DMA((ns,)), pltpu.SemaphoreType.DMA((ns,)),
                        pltpu.SemaphoreType.DMA((5,)), pltpu.SemaphoreType.DMA((5,))],
        compiler_params=pltpu.CompilerParams(vmem_limit_bytes=VMEM_LIMIT),
        name="reduce_last")(*owns, *landed, g_small)


_SMALL_PARTS = (("g_norm", 8, 8), ("w_s", 512, 512), ("b_s", 4, 8), ("g_v", 2, 8), ("g_mem", 8, 8),
                ("g_final", 8, 8), ("loss", 8, 8))
_LOSS_ROW = SMALL_ROWS - 8
assert sum(p for _, _, p in _SMALL_PARTS) == SMALL_ROWS


def _pack_small(parts, loss_block):
    rows = []
    for (name, used, padded), p in zip(_SMALL_PARTS, list(parts) + [loss_block]):
        p = p.reshape(used, LANES)
        if padded > used:
            p = jnp.pad(p, ((0, padded - used), (0, 0)))
        rows.append(p)
    return jnp.concatenate(rows, axis=0)


def _local_step(x, mem, target, g_norm, w_in, w_s, b_s, g_v, g_mem, late_weights, g_final,
                fwd_token=None, on_dw=None):
    B, S, _ = x.shape
    x2d = x.reshape(B * S, D_MODEL)
    t2d = target.reshape(B * S, D_MODEL)
    mem2d = mem.reshape(B * N_MEM, D_MODEL)

    proj = _inproj_fwd(x2d, g_norm, w_in, after=() if fwd_token is None else (fwd_token,))
    w_kv, w_out = late_weights(proj)
    kv = _kv_fwd(mem2d, g_mem, w_kv)
    a, lse = _attn_fwd(proj, B, S)
    w_sT = jnp.swapaxes(w_s, 1, 2)
    b_tab = jnp.repeat(b_s.T, HEAD_DIM, axis=1)
    (dx2, da, drest, loss, d_wout, d_ws, d_bs, d_gv, d_gf, dkv) = _mid(
        x2d, t2d, a, proj, kv, w_s, w_sT, b_tab, g_v, w_out, g_final, B, S)
    d_wkv, d_gmem = _kv_bwd(mem2d, g_mem, w_kv, dkv)
    dq, dk, dv = _attn_bwd(proj, a, lse, da, B, S)
    if on_dw is None:
        d_win = _inproj_bwd_dw(dq, dk, dv, drest, x2d, g_norm)
        after = ()
    else:
        d_win = None
        by_chip = lambda g: g.reshape((N_CHIPS, g.shape[0] // N_CHIPS, g.shape[1]))
        after = (on_dw(*_inproj_bwd_dw(dq, dk, dv, drest, x2d, g_norm, reduce_with=[by_chip(d_wkv), by_chip(d_wout)])),)
    grad_x, d_gnorm = _inproj_bwd_dx(dq, dk, dv, drest, x2d, dx2, g_norm, w_in, after=after)
    d_bs = d_bs[:, :N_SGU_GROUPS].T
    return (loss, grad_x.reshape(B, S, D_MODEL),
            dict(g_norm=d_gnorm, w_in=d_win, w_s=d_ws, b_s=d_bs, g_v=d_gv, g_mem=d_gmem, w_kv=d_wkv,
                 w_out=d_wout, g_final=d_gf))


def kernel(x, mem, g_norm, w_in, w_sgu_spatial, b_sgu_spatial, g_sgu_v, g_mem, w_mem_kv, w_out, g_final, loss_target, m_g_norm, m_w_in, m_w_sgu_spatial, m_b_sgu_spatial, m_g_sgu_v, m_g_mem, m_w_mem_kv, m_w_out, m_g_final, v_g_norm, v_w_in, v_w_sgu_spatial, v_b_sgu_spatial, v_g_sgu_v, v_g_mem, v_w_mem_kv, v_w_out, v_g_final):
    t = lambda w: jnp.swapaxes(w[0], 0, 1)
    (win_all,), late_shards, late_lands = _ag_weights([t(w_in)], [w_mem_kv[0], w_out[0]])
    w_in_full = win_all.reshape(-1, win_all.shape[-1])
    late = _exchange_start(True, list(late_shards), win_all, "gather_late_start", lands=late_lands)

    def late_weights(proj):
        return [z.reshape(-1, z.shape[-1]) for z in _exchange_wait(True, late, proj, "gather_late_wait")]

    scatter = {}

    def on_dw(sends, owns):
        scatter["own"] = owns
        scatter["started"] = _exchange_start(False, list(sends), owns[0], "scatter_start")
        return scatter["started"][-1]

    loss, grad_x, g = _local_step(
        x, mem, loss_target, g_norm, w_in_full, w_sgu_spatial[0], b_sgu_spatial[0], g_sgu_v, g_mem,
        late_weights, g_final.reshape(1, D_MODEL), fwd_token=late[-1], on_dw=on_dw)

    small_names = ("g_norm", "w_s", "b_s", "g_v", "g_mem", "g_final")
    g_small = _pack_small([g[n] for n in small_names], loss)
    landed = _exchange_wait(False, scatter["started"], g_small, "scatter_wait")
    gr_in, gr_kv, gr_out, gr_small = _reduce_last(scatter["own"], landed, g_small)

    small_w = (g_norm, w_sgu_spatial, b_sgu_spatial, g_sgu_v, g_mem, g_final)
    small_m = (m_g_norm, m_w_sgu_spatial, m_b_sgu_spatial, m_g_sgu_v, m_g_mem, m_g_final)
    small_v = (v_g_norm, v_w_sgu_spatial, v_b_sgu_spatial, v_g_sgu_v, v_g_mem, v_g_final)
    rows = lambda ws: [w.reshape(-1, LANES) for w in ws]
    small_new, loss = _adamw_small(gr_small, rows(small_w), rows(small_m), rows(small_v))
    loss = loss.reshape(())
    small = [[z.reshape(w.shape) for z in four] for w, four in zip(small_w, small_new)]
    gr_in, d_in, nm_in, nv_in = [jnp.swapaxes(z, 0, 1)
                                 for z in _adamw(t(w_in), gr_in, t(m_w_in), t(v_w_in), "adamw_w_in")]
    gr_kv, d_kv, nm_kv, nv_kv = _adamw(w_mem_kv[0], gr_kv, m_w_mem_kv[0], v_w_mem_kv[0], "adamw_w_kv")
    gr_out, d_out, nm_out, nv_out = _adamw(w_out[0], gr_out, m_w_out[0], v_w_out[0], "adamw_w_out")

    def leaves(kind, big_in, big_kv, big_out):
        s_norm, s_ws, s_bs, s_gv, s_gmem, s_gf = [four[kind] for four in small]
        return [s_norm, big_in[None], s_ws, s_bs, s_gv, s_gmem, big_kv[None], big_out[None], s_gf]

    return (loss, grad_x, *leaves(0, gr_in, gr_kv, gr_out), *leaves(1, d_in, d_kv, d_out),
            *leaves(2, nm_in, nm_kv, nm_out), *leaves(3, nv_in, nv_kv, nv_out))
```

```python
import functools

import jax
import jax.numpy as jnp
from jax import lax
from jax.experimental import pallas as pl
from jax.experimental.pallas import tpu as pltpu

F32 = jnp.float32
BF16 = jnp.bfloat16
MESH = pl.DeviceIdType.MESH

D_MODEL = 1024
ATTN_WIDTH = 512
SGU_WIDTH = 256
MEM_WIDTH = 256
N_MEM = 256
IN_COLS = 3328
QKV_COLS = 3 * ATTN_WIDTH
REST_COLS = IN_COLS - QKV_COLS
SGU_CHUNK = 128
N_SGU_GROUPS = 4
EPS = 1e-6
NEG_INF = -1e30
DILATIONS = (1, 4, 16)
RADIUS = 64
Q_BLOCK = 128
LANES = 128
HEAD_DIM = 64

ADAM_LR = 0.001
ADAM_B1 = 0.9
ADAM_B2 = 0.999
ADAM_EPS = 1e-08
ADAM_WD = 0.01
ADAM_STEP = 10

N_CHIPS = 4
VMEM_LIMIT = 56 * 1024 * 1024
SMALL_ROWS = 560


def _params(sem=None, vmem=VMEM_LIMIT):
    return pltpu.CompilerParams(dimension_semantics=sem, vmem_limit_bytes=vmem)


def _nn(a, b):
    return jnp.dot(a, b, preferred_element_type=F32)


def _nt(a, b):
    return lax.dot_general(a, b, (((1,), (1,)), ((), ())), preferred_element_type=F32)


def _tn(a, b):
    return lax.dot_general(a, b, (((0,), (0,)), ((), ())), preferred_element_type=F32)


def _rms(x):
    r = lax.rsqrt(jnp.mean(x * x, axis=-1, keepdims=True) + EPS)
    return r, x * r


def _head_masks():
    lane = lax.broadcasted_iota(jnp.int32, (1, LANES), 1)
    lo = lane < HEAD_DIM
    return lo, (lo.astype(F32), (~lo).astype(F32))


def _silu_parts(z):
    s = jax.nn.sigmoid(z)
    return z * s, s * (1.0 + z * (1.0 - s))


def _gelu_parts(x):
    c = 0.7978845608028654
    x2 = x * x
    s = jax.nn.sigmoid((2.0 * c) * (x + 0.044715 * (x * x2)))
    return x * s, s * (1.0 + x * (1.0 - s) * ((2.0 * c) * (1.0 + 3.0 * 0.044715 * x2)))


def _after(tokens):
    return [pl.BlockSpec(memory_space=pl.ANY)] * len(tokens)


def _inproj_fwd(x2d, g_norm, w_in_t, after=()):
    T = x2d.shape[0]
    tm = 512

    def body(x_ref, g_ref, w_ref, *rest):
        o_ref = rest[-1]
        _, xh = _rms(x_ref[...])
        h = (xh * g_ref[...]).astype(BF16)
        o_ref[...] = _nt(h, w_ref[...])

    return pl.pallas_call(
        body, grid=(T // tm,),
        in_specs=[pl.BlockSpec((tm, D_MODEL), lambda i: (i, 0)),
                  pl.BlockSpec((1, D_MODEL), lambda i: (0, 0)),
                  pl.BlockSpec((IN_COLS, D_MODEL), lambda i: (0, 0))] + _after(after),
        out_specs=pl.BlockSpec((tm, IN_COLS), lambda i: (i, 0)),
        out_shape=jax.ShapeDtypeStruct((T, IN_COLS), F32),
        compiler_params=_params(("arbitrary",)), name="inproj_fwd")(x2d, g_norm, w_in_t, *after)


def _kv_fwd(mem2d, g_mem, w_kv):
    Tm = mem2d.shape[0]

    def body(m_ref, g_ref, w_ref, o_ref):
        _, mh = _rms(m_ref[...])
        o_ref[...] = _nn((mh * g_ref[...]).astype(BF16), w_ref[...])

    return pl.pallas_call(
        body, out_shape=jax.ShapeDtypeStruct((Tm, 2 * MEM_WIDTH), F32),
        compiler_params=_params(), name="kv_fwd")(mem2d, g_mem, w_kv)


def _kv_bwd(mem2d, g_mem, w_kv, dkv):
    Tm = mem2d.shape[0]

    def body(m_ref, g_ref, w_ref, dkv_ref, dw_ref, dg_ref):
        _, mh = _rms(m_ref[...])
        memn = (mh * g_ref[...]).astype(BF16)
        dkvb = dkv_ref[...].astype(BF16)
        dw_ref[...] = _tn(memn, dkvb)
        dmemn = _nt(dkvb, w_ref[...])
        dg_ref[...] = jnp.sum(dmemn * mh, axis=0, keepdims=True)

    return pl.pallas_call(
        body, out_shape=(jax.ShapeDtypeStruct((D_MODEL, 2 * MEM_WIDTH), F32),
                         jax.ShapeDtypeStruct((1, D_MODEL), F32)),
        compiler_params=_params(), name="kv_bwd")(mem2d, g_mem, w_kv, dkv)


def _attn_geometry(S):
    geom = []
    for d in DILATIONS:
        L = S // d
        assert L % Q_BLOCK == 0
        geom.append((d, L, min(2 * Q_BLOCK, L), L // Q_BLOCK))
    return geom


def _init_bias(bias_scr, geom, hp):
    row = lax.broadcasted_iota(jnp.int32, (Q_BLOCK, 2 * Q_BLOCK), 0)
    col = lax.broadcasted_iota(jnp.int32, (Q_BLOCK, 2 * Q_BLOCK), 1)
    for j in (0, 1):
        bits = (126 - (2 * hp + j)) * (1 << 23)
        slope = lax.bitcast_convert_type(jnp.full((1, 1), bits, jnp.int32), F32)
        for di, (d, _, _, _) in enumerate(geom):
            for cls, off in enumerate((0, -RADIUS, -2 * RADIUS)):
                dist = jnp.abs(col - row + off)
                bias_scr[di * 6 + cls * 2 + j] = jnp.where(
                    dist <= RADIUS, -(slope * float(d)) * dist.astype(F32), NEG_INF)


SPLIT = 4
COPY_ROWS = 256


def _by4_rows(S, step):
    per_class = S // SPLIT // COPY_ROWS
    r, j = step // per_class, step % per_class
    return (pl.ds(r + SPLIT * j * COPY_ROWS, COPY_ROWS, stride=SPLIT),
            pl.ds(pl.multiple_of(r * (S // SPLIT) + j * COPY_ROWS, COPY_ROWS), COPY_ROWS))


def _to_by4(src, dst, S):
    def step(i, carry):
        natural, by4 = _by4_rows(S, i)
        dst[by4, :] = src[natural, :]
        return carry
    lax.fori_loop(0, S // COPY_ROWS, step, 0)


def _block_slices(d, L, KW, nqb, r, qb, S):
    qs = qb * Q_BLOCK
    ks = jnp.clip(qs - RADIUS, 0, L - KW)
    cls = jnp.where(qb == 0, 0, jnp.where(qb == nqb - 1, 2, 1))
    if d == 1:
        qsl = pl.ds(pl.multiple_of(qs, Q_BLOCK), Q_BLOCK)
        ksl = pl.ds(pl.multiple_of(ks, RADIUS), KW)
    elif d == SPLIT:
        qsl = pl.ds(pl.multiple_of(r * L + qs, Q_BLOCK), Q_BLOCK)
        ksl = pl.ds(pl.multiple_of(r * L + ks, RADIUS), KW)
    else:
        sub = d // SPLIT
        base = (r % SPLIT) * (S // SPLIT) + r // SPLIT
        qsl = pl.ds(base + qs * sub, Q_BLOCK, stride=sub)
        ksl = pl.ds(base + ks * sub, KW, stride=sub)
    return qsl, ksl, cls


def _for_groups(geom, S, group, fn):
    for di, (d, L, KW, nqb) in enumerate(geom):
        n = group[di]
        assert (d * nqb) % n == 0

        def step(it, carry, di=di, d=d, L=L, KW=KW, nqb=nqb, n=n):
            slices = []
            for g in range(n):
                i = it * n + g
                slices.append(_block_slices(d, L, KW, nqb, i // nqb, i % nqb, S))
            fn(di, KW, slices)
            return carry
        lax.fori_loop(0, d * nqb // n, step, 0)


def _attn_fwd(proj, B, S):
    T = B * S
    geom = _attn_geometry(S)
    n_pairs = ATTN_WIDTH // LANES

    def body(q_ref, k_ref, v_ref, a_ref, lse_ref, bias_scr, q4, k4, v4, *per_dilation):
        o_scr, m_scr, l_scr = per_dilation[0:3], per_dilation[3:6], per_dilation[6:9]
        lo, hm = _head_masks()
        pair = pl.program_id(0)

        @pl.when(pl.program_id(1) == 0)
        def _():
            _init_bias(bias_scr, geom, pair)
        for src, dst in ((q_ref, q4), (k_ref, k4), (v_ref, v4)):
            _to_by4(src, dst, S)

        def group(di, KW, all_slices):
            run = 8
            for first in range(0, len(all_slices), run):
                some(di, KW, all_slices[first:first + run])

        def some(di, KW, slices):
            chains = [(g, j) for g in range(len(slices)) for j in (0, 1)]
            q_src, k_src, v_src = (q_ref, k_ref, v_ref) if di == 0 else (q4, k4, v4)
            q = [q_src[qsl, :] for qsl, _, _ in slices]
            kw = [k_src[ksl, :].astype(BF16) for _, ksl, _ in slices]
            vw = [v_src[ksl, :].astype(BF16) for _, ksl, _ in slices]
            s = {(g, j): _nt((q[g] * (hm[j] * 0.125)).astype(BF16), kw[g])
                 + bias_scr[di * 6 + slices[g][2] * 2 + j, :, pl.ds(0, KW)] for g, j in chains}
            m = {c: jnp.max(s[c], axis=1, keepdims=True) for c in chains}
            p = {c: jnp.exp(s[c] - m[c]) for c in chains}
            l = {c: jnp.sum(p[c], axis=1, keepdims=True) for c in chains}
            o = {(g, j): _nn(p[(g, j)].astype(BF16), vw[g]) for g, j in chains}
            for g, (qsl, _, _) in enumerate(slices):
                o_scr[di][qsl, :] = jnp.where(lo, o[(g, 0)], o[(g, 1)])
                m_scr[di][qsl, :] = jnp.where(lo, m[(g, 0)], m[(g, 1)])
                l_scr[di][qsl, :] = jnp.where(lo, l[(g, 0)], l[(g, 1)])

        _for_groups(geom, S, (16, 16, 16), group)

        def combine(i, carry):
            natural, by4 = _by4_rows(S, i)
            rows = [natural, by4, by4]
            ms = [m_scr[di][rows[di], :] for di in range(3)]
            mx = jnp.maximum(jnp.maximum(ms[0], ms[1]), ms[2])
            num = 0.0
            den = 0.0
            for di in range(3):
                w = jnp.exp(ms[di] - mx)
                num = num + w * o_scr[di][rows[di], :]
                den = den + w * l_scr[di][rows[di], :]
            a_ref[natural, :] = num / den
            lse_ref[natural, :] = mx + jnp.log(den)
            return carry

        lax.fori_loop(0, S // COPY_ROWS, combine, 0)

    blk = lambda off: pl.BlockSpec((S, LANES), lambda h, b, off=off: (b, off + h))
    out_blk = pl.BlockSpec((S, LANES), lambda h, b: (b, h))
    return pl.pallas_call(
        body, grid=(n_pairs, B),
        in_specs=[blk(0), blk(n_pairs), blk(2 * n_pairs)],
        out_specs=[out_blk, out_blk],
        out_shape=[jax.ShapeDtypeStruct((T, ATTN_WIDTH), F32)] * 2,
        scratch_shapes=[pltpu.VMEM((18, Q_BLOCK, 2 * Q_BLOCK), F32)] + [pltpu.VMEM((S, LANES), F32)] * 12,
        compiler_params=_params(("arbitrary", "arbitrary")), name="attn_fwd")(proj, proj, proj)


def _attn_bwd(proj, a, lse, da, B, S):
    T = B * S
    geom = _attn_geometry(S)
    n_pairs = ATTN_WIDTH // LANES

    def body(q_ref, k_ref, v_ref, a_ref, lse_ref, do_ref, dq_ref, dk_ref, dv_ref, bias_scr, *scr):
        acc = (scr[0:3], scr[3:6])
        natural_in = (q_ref, k_ref, v_ref, a_ref, lse_ref, do_ref)
        by4_in = scr[6:12]
        _, hm = _head_masks()
        pair = pl.program_id(0)

        @pl.when(pl.program_id(1) == 0)
        def _():
            _init_bias(bias_scr, geom, pair)
        for ref in scr[0:6]:
            ref[...] = jnp.zeros_like(ref)
        for src, dst in zip(natural_in, by4_in):
            _to_by4(src, dst, S)

        def group(di, KW, all_slices):
            run = (4, 4, 16)[di]
            for first in range(0, len(all_slices), run):
                some(di, KW, all_slices[first:first + run])

        def some(di, KW, slices):
            n = len(slices)
            chains = [(g, j) for g in range(n) for j in (0, 1)]
            q_src, k_src, v_src, a_src, lse_src, do_src = natural_in if di == 0 else by4_in
            dq_scr, dk_scr, dv_scr = acc[0 if di == 0 else 1]
            q = [q_src[qsl, :] for qsl, _, _ in slices]
            do = [do_src[qsl, :] for qsl, _, _ in slices]
            doa = [do[g] * a_src[slices[g][0], :] for g in range(n)]
            lse_q = [lse_src[qsl, :] for qsl, _, _ in slices]
            kw = [k_src[ksl, :].astype(BF16) for _, ksl, _ in slices]
            vw = [v_src[ksl, :].astype(BF16) for _, ksl, _ in slices]
            qj = {(g, j): (q[g] * (hm[j] * 0.125)).astype(BF16) for g, j in chains}
            doj = {(g, j): (do[g] * hm[j]).astype(BF16) for g, j in chains}
            s = {(g, j): _nt(qj[(g, j)], kw[g])
                 + bias_scr[di * 6 + slices[g][2] * 2 + j, :, pl.ds(0, KW)] for g, j in chains}
            dp = {(g, j): _nt(doj[(g, j)], vw[g]) for g, j in chains}
            dsum = {(g, j): jnp.sum(doa[g] * hm[j], axis=1, keepdims=True) for g, j in chains}
            p = {(g, j): jnp.exp(s[(g, j)] - lse_q[g][:, HEAD_DIM * j:HEAD_DIM * j + 1]) for g, j in chains}
            ds = {c: (p[c] * (dp[c] - dsum[c])).astype(BF16) for c in chains}
            pb = {c: p[c].astype(BF16) for c in chains}
            dq = [_nn(ds[(g, 0)], kw[g]) * (hm[0] * 0.125) + _nn(ds[(g, 1)], kw[g]) * (hm[1] * 0.125)
                  for g in range(n)]
            both = lambda t, g: jnp.concatenate([t[(g, 0)], t[(g, 1)]], axis=0)
            dkw = [_tn(both(ds, g), both(qj, g)) for g in range(n)]
            dvw = [_tn(both(pb, g), both(doj, g)) for g in range(n)]
            for g, (qsl, ksl, _) in enumerate(slices):
                dq_scr[qsl, :] = dq_scr[qsl, :] + dq[g]
                dk_scr[ksl, :] = dk_scr[ksl, :] + dkw[g]
                dv_scr[ksl, :] = dv_scr[ksl, :] + dvw[g]

        _for_groups(geom, S, (16, 16, 16), group)

        def merge(i, carry):
            natural, by4 = _by4_rows(S, i)
            for nat, split in zip(*acc):
                nat[natural, :] = nat[natural, :] + split[by4, :]
            return carry
        lax.fori_loop(0, S // COPY_ROWS, merge, 0)
        for out, nat in zip((dq_ref, dk_ref, dv_ref), acc[0]):
            out[...] = nat[...].astype(BF16)

    blk = lambda off: pl.BlockSpec((S, LANES), lambda h, b, off=off: (b, off + h))
    return pl.pallas_call(
        body, grid=(n_pairs, B),
        in_specs=[blk(0), blk(n_pairs), blk(2 * n_pairs), blk(0), blk(0), blk(0)],
        out_specs=[blk(0), blk(0), blk(0)],
        out_shape=[jax.ShapeDtypeStruct((T, ATTN_WIDTH), BF16)] * 3,
        scratch_shapes=[pltpu.VMEM((18, Q_BLOCK, 2 * Q_BLOCK), F32)] + [pltpu.VMEM((S, LANES), F32)] * 12,
        compiler_params=_params(("arbitrary", "arbitrary")), name="attn_bwd")(proj, proj, proj, a, lse, da)


def _mid(x2d, t2d, a, proj, kv, w_s, w_sT, b_tab, g_v, w_out, g_final, B, S):
    T = B * S
    tm = 512
    nt = S // tm
    halves = 2
    hrows = tm // halves

    def body(x_ref, t_ref, a_ref, za_ref, ub_ref, vb_ref, zb_ref, qm_ref, zm_ref, kv_ref,
              ws_ref, wsT_ref, btab_ref, gv_ref, wout_ref, gf_ref,
              dx2_ref, da_ref, drest_ref, loss_ref, dwout_ref, dws_ref, dbs_ref, dgv_ref, dgf_ref, dkv_ref,
              dbtab_scr):
        b = pl.program_id(0)
        t = pl.program_id(1)
        first = jnp.logical_and(b == 0, t == 0)
        last = jnp.logical_and(b == B - 1, t == nt - 1)
        _, hm = _head_masks()
        lane_g = lax.broadcasted_iota(jnp.int32, (1, SGU_WIDTH), 1) // HEAD_DIM
        gm = [(lane_g == g).astype(F32) for g in range(N_SGU_GROUPS)]
        H = range(halves)
        rows = [pl.ds(h * hrows, hrows) for h in H]
        ld = lambda ref: [ref[r, :] for r in rows]
        cat = lambda parts, axis: jnp.concatenate(parts, axis=axis)
        chunks = [slice(ci * SGU_CHUNK, (ci + 1) * SGU_CHUNK) for ci in range(hrows // SGU_CHUNK)]
        pairs = [slice(pr * LANES, (pr + 1) * LANES) for pr in range(2)]
        heads = [(pr, j) for pr in range(2) for j in (0, 1)]

        @pl.when(first)
        def _():
            loss_ref[...] = jnp.zeros_like(loss_ref)
            dwout_ref[...] = jnp.zeros_like(dwout_ref)
            dws_ref[...] = jnp.zeros_like(dws_ref)
            dbs_ref[...] = jnp.zeros_like(dbs_ref)
            dgv_ref[...] = jnp.zeros_like(dgv_ref)
            dgf_ref[...] = jnp.zeros_like(dgf_ref)
            dbtab_scr[...] = jnp.zeros_like(dbtab_scr)

        @pl.when(t == 0)
        def _():
            dkv_ref[...] = jnp.zeros_like(dkv_ref)

        a_val = ld(a_ref)
        sil_a = [_silu_parts(z) for z in ld(za_ref)]
        gated_a = [s[0] * a for s, a in zip(sil_a, a_val)]
        u = [_gelu_parts(z) for z in ld(ub_ref)]
        vv = [_gelu_parts(z) for z in ld(vb_ref)]
        vnorm = [_rms(v[0]) for v in vv]
        gv = gv_ref[...]
        vn = [(n[1] * gv).astype(BF16) for n in vnorm]
        w_cat = cat([ws_ref[g].astype(BF16) for g in range(N_SGU_GROUPS)], 1)
        wT_cat = cat([wsT_ref[g].astype(BF16) for g in range(N_SGU_GROUPS)], 1)
        gmb = [m.astype(BF16) for m in gm]
        by_group = lambda chunk: cat([chunk * gmb[g] for g in range(N_SGU_GROUPS)], 0)
        btab = btab_ref[...]
        mixed = [cat([btab + _nn(w_cat, by_group(vn[h][c, :])) for c in chunks], 0) for h in H]
        sg = [u[h][0] * mixed[h] for h in H]
        sil_b = [_silu_parts(z) for z in ld(zb_ref)]
        gated_b = [sil_b[h][0] * sg[h] for h in H]

        kvv = kv_ref[...].astype(BF16)
        kp = [kvv[:, p] for p in pairs]
        vp = [kvv[:, MEM_WIDTH + pr * LANES:MEM_WIDTH + (pr + 1) * LANES] for pr in range(2)]
        qm = ld(qm_ref)
        qj = {(h, pr, j): (qm[h][:, pairs[pr]] * (hm[j] * 0.125)).astype(BF16) for h in H for pr, j in heads}
        sc = {k: _nt(qj[k], kp[k[1]]) for k in qj}
        ex = {k: jnp.exp(sc[k] - jnp.max(sc[k], axis=1, keepdims=True)) for k in qj}
        prob = {k: ex[k] * (1.0 / jnp.sum(ex[k], axis=1, keepdims=True)) for k in qj}
        probb = {k: prob[k].astype(BF16) for k in qj}
        mo = [cat([sum(_nn(probb[(h, pr, j)], vp[pr]) * hm[j] for j in (0, 1)) for pr in range(2)], 1) for h in H]
        sil_m = [_silu_parts(z) for z in ld(zm_ref)]
        gated_m = [sil_m[h][0] * mo[h] for h in H]

        gated = [cat([gated_a[h], gated_b[h], gated_m[h]], 1).astype(BF16) for h in H]
        wout = wout_ref[...]
        x_in = ld(x_ref)
        x2 = [x_in[h] + _nn(gated[h], wout) for h in H]
        fin = [_rms(z) for z in x2]
        gf = gf_ref[...]
        tgt = ld(t_ref)
        err = [fin[h][1] * gf - tgt[h] for h in H]
        loss_ref[...] += sum(jnp.sum(e * e) for e in err) * (0.5 / D_MODEL)

        dy = [e * (1.0 / D_MODEL) for e in err]
        dgf_ref[...] += sum(jnp.sum(dy[h] * fin[h][1], axis=0, keepdims=True) for h in H)
        gdy = [d * gf for d in dy]
        dx2 = [fin[h][0] * (gdy[h] - fin[h][1] * jnp.mean(gdy[h] * fin[h][1], axis=1, keepdims=True)) for h in H]
        for h in H:
            dx2_ref[rows[h], :] = dx2[h]
        dx2b = [d.astype(BF16) for d in dx2]
        dgated = [_nt(d, wout) for d in dx2b]
        dwout_ref[...] += _tn(cat(gated, 0), cat(dx2b, 0))
        dga = [d[:, 0:ATTN_WIDTH] for d in dgated]
        dgb = [d[:, ATTN_WIDTH:ATTN_WIDTH + SGU_WIDTH] for d in dgated]
        dgm = [d[:, ATTN_WIDTH + SGU_WIDTH:] for d in dgated]

        for h in H:
            da_ref[rows[h], :] = dga[h] * sil_a[h][0]
        dza = [dga[h] * a_val[h] * sil_a[h][1] for h in H]

        dsg = [dgb[h] * sil_b[h][0] for h in H]
        dzb = [dgb[h] * sg[h] * sil_b[h][1] for h in H]
        dub = [dsg[h] * mixed[h] * u[h][1] for h in H]
        dmixed = [dsg[h] * u[h][0] for h in H]
        dmixed_b = [d.astype(BF16) for d in dmixed]
        dvn = [cat([_nn(wT_cat, by_group(dmixed_b[h][c, :])) for c in chunks], 0) for h in H]
        for g in range(N_SGU_GROUPS):
            dws_ref[g] += sum(_nt((dmixed[h][c, :] * gm[g]).astype(BF16), vn[h][c, :]) for h in H for c in chunks)
        dbtab_scr[...] += sum(dmixed[h][c, :] for h in H for c in chunks)
        dgv_ref[...] += sum(jnp.sum(dvn[h] * vnorm[h][1], axis=0, keepdims=True) for h in H)
        tv = [d * gv for d in dvn]
        dvv = [vnorm[h][0] * (tv[h] - vnorm[h][1] * jnp.mean(tv[h] * vnorm[h][1], axis=1, keepdims=True)) for h in H]
        dvb = [dvv[h] * vv[h][1] for h in H]

        dmo = [dgm[h] * sil_m[h][0] for h in H]
        dzm = [dgm[h] * mo[h] * sil_m[h][1] for h in H]
        dmoj = {(h, pr, j): (dmo[h][:, pairs[pr]] * hm[j]).astype(BF16) for h in H for pr, j in heads}
        dp = {k: _nt(dmoj[k], vp[k[1]]) for k in qj}
        ds = {k: (prob[k] * (dp[k] - jnp.sum(dp[k] * prob[k], axis=1, keepdims=True))).astype(BF16) for k in qj}
        dqm = [cat([sum(_nn(ds[(h, pr, j)], kp[pr]) * (hm[j] * 0.125) for j in (0, 1)) for pr in range(2)], 1)
               for h in H]
        every = lambda tbl, pr: cat([tbl[(h, pr, j)] for h in H for j in (0, 1)], 0)
        dk = [_tn(every(ds, pr), every(qj, pr)) for pr in range(2)]
        dv = [_tn(every(probb, pr), every(dmoj, pr)) for pr in range(2)]
        dkv_ref[...] += cat(dk + dv, 1)

        for h in H:
            drest_ref[rows[h], :] = cat([dza[h], dub[h], dvb[h], dzb[h], dqm[h], dzm[h]], 1).astype(BF16)

        @pl.when(last)
        def _():
            lane = lax.broadcasted_iota(jnp.int32, (1, LANES), 1)
            dbt = dbtab_scr[...]
            out = jnp.zeros((SGU_CHUNK, LANES), F32)
            for g in range(N_SGU_GROUPS):
                out = out + jnp.where(lane == g, jnp.sum(dbt * gm[g], axis=1, keepdims=True), 0.0)
            dbs_ref[...] = out

    tile = lambda w, cb: pl.BlockSpec((tm, w), lambda b, t, cb=cb: (b * nt + t, cb))
    const = lambda shape: pl.BlockSpec(shape, lambda b, t, n=len(shape): (0,) * n)
    return pl.pallas_call(
        body, grid=(B, nt),
        in_specs=[tile(D_MODEL, 0), tile(D_MODEL, 0), tile(ATTN_WIDTH, 0),
                  tile(ATTN_WIDTH, 3),
                  tile(SGU_WIDTH, 8), tile(SGU_WIDTH, 9), tile(SGU_WIDTH, 10),
                  tile(MEM_WIDTH, 11), tile(MEM_WIDTH, 12),
                  pl.BlockSpec((N_MEM, 2 * MEM_WIDTH), lambda b, t: (b, 0)),
                  const((N_SGU_GROUPS, SGU_CHUNK, SGU_CHUNK)), const((N_SGU_GROUPS, SGU_CHUNK, SGU_CHUNK)),
                  const((SGU_CHUNK, SGU_WIDTH)), const((1, SGU_WIDTH)),
                  const((D_MODEL, D_MODEL)), const((1, D_MODEL))],
        out_specs=[tile(D_MODEL, 0), tile(ATTN_WIDTH, 0), tile(REST_COLS, 0),
                   const((8, LANES)), const((D_MODEL, D_MODEL)),
                   const((N_SGU_GROUPS, SGU_CHUNK, SGU_CHUNK)), const((SGU_CHUNK, LANES)),
                   const((1, SGU_WIDTH)), const((1, D_MODEL)),
                   pl.BlockSpec((N_MEM, 2 * MEM_WIDTH), lambda b, t: (b, 0))],
        out_shape=[jax.ShapeDtypeStruct((T, D_MODEL), F32), jax.ShapeDtypeStruct((T, ATTN_WIDTH), F32),
                   jax.ShapeDtypeStruct((T, REST_COLS), BF16),
                   jax.ShapeDtypeStruct((8, LANES), F32), jax.ShapeDtypeStruct((D_MODEL, D_MODEL), F32),
                   jax.ShapeDtypeStruct((N_SGU_GROUPS, SGU_CHUNK, SGU_CHUNK), F32),
                   jax.ShapeDtypeStruct((SGU_CHUNK, LANES), F32),
                   jax.ShapeDtypeStruct((1, SGU_WIDTH), F32), jax.ShapeDtypeStruct((1, D_MODEL), F32),
                   jax.ShapeDtypeStruct((B * N_MEM, 2 * MEM_WIDTH), F32)],
        scratch_shapes=[pltpu.VMEM((SGU_CHUNK, SGU_WIDTH), F32)],
        compiler_params=_params(("arbitrary", "arbitrary")), name="mid")(
            x2d, t2d, a, proj, proj, proj, proj, proj, proj, kv, w_s, w_sT, b_tab, g_v, w_out, g_final)


def _inproj_bwd_dx(dq, dk, dv, drest, x2d, dx2, g_norm, w_in_t, after=()):
    T = x2d.shape[0]
    tm = 512
    W = ATTN_WIDTH

    def body(dq_ref, dk_ref, dv_ref, dr_ref, x_ref, dx2_ref, g_ref, w_ref, *rest):
        gx_ref, dg_ref = rest[-2:]

        @pl.when(pl.program_id(0) == 0)
        def _():
            dg_ref[...] = jnp.zeros_like(dg_ref)

        halves = [pl.ds(h * (tm // 2), tm // 2) for h in (0, 1)]
        dh = [(_nn(dq_ref[r, :], w_ref[0:W, :]) + _nn(dk_ref[r, :], w_ref[W:2 * W, :])
               + _nn(dv_ref[r, :], w_ref[2 * W:3 * W, :]) + _nn(dr_ref[r, :], w_ref[QKV_COLS:IN_COLS, :]))
              for r in halves]
        nrm = [_rms(x_ref[r, :]) for r in halves]
        dg_ref[...] += sum(jnp.sum(d * n[1], axis=0, keepdims=True) for d, n in zip(dh, nrm))
        g = g_ref[...]
        for r, d, (rstd, xh) in zip(halves, dh, nrm):
            th = d * g
            gx_ref[r, :] = rstd * (th - xh * jnp.mean(th * xh, axis=1, keepdims=True)) + dx2_ref[r, :]

    tile = lambda w: pl.BlockSpec((tm, w), lambda i: (i, 0))
    return pl.pallas_call(
        body, grid=(T // tm,),
        in_specs=[tile(W), tile(W), tile(W), tile(REST_COLS), tile(D_MODEL), tile(D_MODEL),
                  pl.BlockSpec((1, D_MODEL), lambda i: (0, 0)),
                  pl.BlockSpec((IN_COLS, D_MODEL), lambda i: (0, 0))] + _after(after),
        out_specs=[tile(D_MODEL), pl.BlockSpec((1, D_MODEL), lambda i: (0, 0))],
        out_shape=[jax.ShapeDtypeStruct((T, D_MODEL), F32), jax.ShapeDtypeStruct((1, D_MODEL), F32)],
        compiler_params=_params(("arbitrary",)), name="inproj_bwd_dx")(
            dq, dk, dv, drest, x2d, dx2, g_norm, w_in_t, *after)


def _inproj_bwd_dw(dq, dk, dv, drest, x2d, g_norm, reduce_with=None):
    T = x2d.shape[0]
    tm = 512
    nt = T // tm
    W = ATTN_WIDTH
    fused = reduce_with is not None
    others = list(reduce_with) if fused else []
    ns = 1 + len(others)
    shard = IN_COLS // N_CHIPS
    halves = [shard // 2] + [s.shape[1] // 2 for s in others]
    cols = [D_MODEL] + [s.shape[2] for s in others]
    row_block = 32

    def body(dq_ref, dk_ref, dv_ref, dr_ref, x_ref, g_ref, *rest):
        if fused:
            stacks = rest[:ns - 1]
            sends, owns = rest[ns - 1:2 * ns - 1], rest[2 * ns - 1:3 * ns - 1]
            acc, ras, narrow = rest[3 * ns - 1], rest[3 * ns:4 * ns], rest[4 * ns]
            s_sem, r_sem = rest[4 * ns + 1], rest[4 * ns + 2]
            x, y, c, chip, peers, peer_chip = _place()
            sib = (x, y, 1 - c)

            def part(w, k, cc, r0=0, rows=None):
                n = halves[w]
                rows = n if rows is None else rows
                if w == 0:
                    return acc.at[pl.ds(pl.multiple_of(k * shard + cc * n + r0, 8), rows), :]
                return stacks[w - 1].at[k, pl.ds(pl.multiple_of(cc * n + r0, 8), rows), :]

            def swap_other(w):
                theirs = stacks[w - 1].at[:, pl.ds(pl.multiple_of((1 - c) * halves[w], 8), halves[w]), :]
                return _remote(theirs, ras[w], s_sem.at[N_CHIPS - 1 + w], r_sem.at[N_CHIPS - 1 + w], sib)

            def swap_win(k):
                return _remote(narrow.at[k], ras[0].at[k], s_sem.at[k], r_sem.at[k], sib)
        else:
            acc = rest[0]

        @pl.when(pl.program_id(0) == 0)
        def _():
            acc[...] = jnp.zeros_like(acc)
            for w in range(1, ns):
                swap_other(w).start()

        _, xh = _rms(x_ref[...])
        h = (xh * g_ref[...]).astype(BF16)
        acc[0:W, :] += _tn(dq_ref[...], h)
        acc[W:2 * W, :] += _tn(dk_ref[...], h)
        acc[2 * W:3 * W, :] += _tn(dv_ref[...], h)
        acc[QKV_COLS:IN_COLS, :] += _tn(dr_ref[...], h)

        if fused:
            @pl.when(pl.program_id(0) == nt - 1)
            def _():
                for k in range(N_CHIPS):
                    def to_bf16(i, carry, k=k):
                        r0 = pl.multiple_of(i * row_block, row_block)
                        narrow[k, pl.ds(r0, row_block), :] = part(0, k, 1 - c, r0, row_block)[...].astype(BF16)
                        return carry
                    lax.fori_loop(0, halves[0] // row_block, to_bf16, 0)
                    swap_win(k).start()
                def chip_sum(w, k, r0):
                    blk = pl.ds(r0, row_block)
                    return part(w, k, c, r0, row_block)[...] + ras[w][k, blk, :].astype(F32)

                for w in range(1, ns):
                    swap_other(w).wait_recv()

                    def sums(i, carry, w=w):
                        r0 = pl.multiple_of(i * row_block, row_block)
                        for m in range(3):
                            sends[w][m, pl.ds(r0, row_block), :] = chip_sum(w, peer_chip[m], r0).astype(BF16)
                        owns[w][pl.ds(r0, row_block), :] = chip_sum(w, chip, r0)
                        return carry
                    lax.fori_loop(0, halves[w] // row_block, sums, 0)
                for k in range(N_CHIPS):
                    swap_win(k).wait_recv()

                    @pl.when(chip == k)
                    def _(k=k):
                        def own(i, carry):
                            r0 = pl.multiple_of(i * row_block, row_block)
                            owns[0][pl.ds(r0, row_block), :] = chip_sum(0, k, r0)
                            return carry
                        lax.fori_loop(0, halves[0] // row_block, own, 0)

                    @pl.when(chip != k)
                    def _(k=k):
                        def other(i, carry):
                            r0 = pl.multiple_of(i * row_block, row_block)
                            sends[0][(k ^ chip) - 1, pl.ds(r0, row_block), :] = chip_sum(0, k, r0).astype(BF16)
                            return carry
                        lax.fori_loop(0, halves[0] // row_block, other, 0)
                for k in range(N_CHIPS):
                    swap_win(k).wait_send()
                for w in range(1, ns):
                    swap_other(w).wait_send()

    tile = lambda w: pl.BlockSpec((tm, w), lambda i: (i, 0))
    vmem = pl.BlockSpec(memory_space=pltpu.VMEM)
    in_specs = [tile(W), tile(W), tile(W), tile(REST_COLS), tile(D_MODEL), pl.BlockSpec((1, D_MODEL), lambda i: (0, 0))]
    if not fused:
        return pl.pallas_call(
            body, grid=(nt,), in_specs=in_specs,
            out_specs=pl.BlockSpec((IN_COLS, D_MODEL), lambda i: (0, 0)),
            out_shape=jax.ShapeDtypeStruct((IN_COLS, D_MODEL), F32),
            compiler_params=_params(("arbitrary",)), name="inproj_bwd_dw")(dq, dk, dv, drest, x2d, g_norm)
    outs = pl.pallas_call(
        body, grid=(nt,), in_specs=in_specs + [vmem] * (ns - 1), out_specs=[vmem] * (2 * ns),
        out_shape=[jax.ShapeDtypeStruct((3, n, cl), BF16) for n, cl in zip(halves, cols)]
        + [jax.ShapeDtypeStruct((n, cl), F32) for n, cl in zip(halves, cols)],
        scratch_shapes=[pltpu.VMEM((IN_COLS, D_MODEL), F32)]
        + [pltpu.VMEM((N_CHIPS, n, cl), BF16 if w == 0 else F32) for w, (n, cl) in enumerate(zip(halves, cols))]
        + [pltpu.VMEM((N_CHIPS, halves[0], D_MODEL), BF16)]
        + [pltpu.SemaphoreType.DMA((N_CHIPS - 1 + ns,)), pltpu.SemaphoreType.DMA((N_CHIPS - 1 + ns,))],
        compiler_params=_params(("arbitrary",)), name="inproj_bwd_dw_reduce")(
            dq, dk, dv, drest, x2d, g_norm, *others)
    return outs[:ns], outs[ns:]


def _adamw_update(w, g, m, v):
    nm = ADAM_B1 * m + (1.0 - ADAM_B1) * g
    nv = ADAM_B2 * v + (1.0 - ADAM_B2) * (g * g)
    m_hat = nm / (1.0 - ADAM_B1 ** ADAM_STEP)
    v_hat = nv / (1.0 - ADAM_B2 ** ADAM_STEP)
    return -ADAM_LR * (m_hat / (jnp.sqrt(v_hat) + ADAM_EPS) + ADAM_WD * w), nm, nv


def _adamw(w, g, m, v, name):
    R, C = w.shape
    br = max(r for r in range(8, 257, 8) if R % r == 0)

    def body(w_ref, g_ref, m_ref, v_ref, g_out, d_ref, nm_ref, nv_ref):
        g = g_ref[...]
        g_out[...] = g
        d_ref[...], nm_ref[...], nv_ref[...] = _adamw_update(w_ref[...], g, m_ref[...], v_ref[...])

    spec = pl.BlockSpec((br, C), lambda i: (i, 0))
    return pl.pallas_call(
        body, grid=(R // br,), in_specs=[spec] * 4, out_specs=[spec] * 4,
        out_shape=[jax.ShapeDtypeStruct((R, C), F32)] * 4,
        compiler_params=_params(("arbitrary",)), name=name)(w, g, m, v)


def _adamw_small(g_packed, ws, ms, vs):
    n = len(ws)

    def body(*refs):
        g_ref = refs[0]
        w_refs, m_refs, v_refs = refs[1:1 + n], refs[1 + n:1 + 2 * n], refs[1 + 2 * n:1 + 3 * n]
        outs = refs[1 + 3 * n:]
        off = 0
        for i, (_, used, padded) in enumerate(_SMALL_PARTS[:n]):
            g = g_ref[off:off + used, :]
            delta, nm, nv = _adamw_update(w_refs[i][...], g, m_refs[i][...], v_refs[i][...])
            outs[4 * i][...], outs[4 * i + 1][...], outs[4 * i + 2][...], outs[4 * i + 3][...] = g, delta, nm, nv
            off += padded
        outs[4 * n][...] = g_ref[_LOSS_ROW:_LOSS_ROW + 1, 0:1]

    outs = pl.pallas_call(
        body, out_shape=[jax.ShapeDtypeStruct(w.shape, F32) for w in ws for _ in range(4)]
        + [jax.ShapeDtypeStruct((1, 1), F32)],
        compiler_params=_params(), name="adamw_small")(g_packed, *ws, *ms, *vs)
    return [outs[4 * i:4 * i + 4] for i in range(n)], outs[4 * n]


def _place():
    x, y, c = lax.axis_index("x"), lax.axis_index("y"), lax.axis_index("c")
    chip = 2 * x + y
    peers = [(x, 1 - y), (1 - x, y), (1 - x, 1 - y)]
    peer_chip = [2 * px + py for px, py in peers]
    return x, y, c, chip, peers, peer_chip


def _remote(src, dst, send_sem, recv_sem, dev):
    return pltpu.make_async_remote_copy(src_ref=src, dst_ref=dst, send_sem=send_sem, recv_sem=recv_sem,
                                        device_id=dev, device_id_type=MESH)


def _ag_weights(weights, late=()):
    nw, nl = len(weights), len(late)
    pieces = 2

    def body(*refs):
        srcs, late_srcs = refs[:nw], refs[nw:nw + nl]
        outs, late_bf, late_land = (refs[nw + nl:2 * nw + nl], refs[2 * nw + nl:2 * nw + 2 * nl],
                                    refs[2 * nw + 2 * nl:2 * nw + 3 * nl])
        s_ici, r_ici, s_d2d, r_d2d = refs[2 * nw + 3 * nl:]
        x, y, c = lax.axis_index("x"), lax.axis_index("y"), lax.axis_index("c")
        chip = 2 * x + y
        sib = (x, y, 1 - c)
        first = ((x + 1 - c) % 2, (y + c) % 2)
        second = ((x + c) % 2, (y + 1 - c) % 2)
        first_chip, second_chip = 2 * first[0] + first[1], 2 * second[0] + second[1]
        diag_chip = 3 - chip
        for src, out in zip(srcs, outs):
            out[chip] = src[...].astype(BF16)

        parts = [(w, out, pc) for w, out in enumerate(outs) for pc in range(pieces)]

        def piece(out, k, cc, pc):
            rows = out.shape[1] // 2 // pieces
            return out.at[k, pl.ds(pl.multiple_of((cc * pieces + pc) * rows, 16), rows), :]

        def ici(w, slot, out, k, dev, pc):
            blk, sem = piece(out, k, c, pc), (nw * slot + w) * pieces + pc
            return _remote(blk, blk, s_ici.at[sem], r_ici.at[sem], (dev[0], dev[1], c))

        def d2d(w, slot, out, k, cc, pc):
            blk, sem = piece(out, k, cc, pc), (nw * slot + w) * pieces + pc
            return _remote(blk, blk, s_d2d.at[sem], r_d2d.at[sem], sib)

        sent = []
        for slot, dev in enumerate((first, second)):
            for w, out, pc in parts:
                sent.append(ici(w, slot, out, chip, dev, pc))
                sent[-1].start()
        for src, bf, land in zip(late_srcs, late_bf, late_land):
            bf[...] = src[...].astype(BF16)
            land[...] = jnp.zeros_like(land)
            land[chip] = bf[...]
        for slot, k, dev in ((0, first_chip, first), (1, second_chip, second), (2, diag_chip, second)):
            for w, out, pc in parts:
                ici(w, slot, out, k, dev, pc).wait_recv()
                if slot == 0:
                    sent.append(ici(w, 2, out, k, second, pc))
                    sent[-1].start()
                sent.append(d2d(w, slot, out, k, c, pc))
                sent[-1].start()
        for slot, k in ((0, second_chip), (1, first_chip), (2, diag_chip)):
            for w, out, pc in parts:
                d2d(w, slot, out, k, 1 - c, pc).wait_recv()
        for cp in sent:
            cp.wait_send()

    vmem = pl.BlockSpec(memory_space=pltpu.VMEM)
    outs = pl.pallas_call(
        body,
        out_shape=[jax.ShapeDtypeStruct((N_CHIPS,) + w.shape, BF16) for w in weights]
        + [jax.ShapeDtypeStruct(w.shape, BF16) for w in late]
        + [jax.ShapeDtypeStruct((N_CHIPS,) + w.shape, BF16) for w in late],
        in_specs=[vmem] * (nw + nl), out_specs=[vmem] * (nw + 2 * nl),
        scratch_shapes=[pltpu.SemaphoreType.DMA((3 * nw * pieces,))] * 4,
        compiler_params=pltpu.CompilerParams(vmem_limit_bytes=VMEM_LIMIT), name="ag_weights")(*weights, *late)
    return outs[:nw], outs[nw:nw + nl], outs[nw + nl:]


_HBM = pl.BlockSpec(memory_space=pltpu.HBM)
_SEM = pl.BlockSpec(memory_space=pltpu.SEMAPHORE)
_ANY = pl.BlockSpec(memory_space=pl.ANY)
_DATAFLOW = pltpu.SideEffectType.DATAFLOW_SIDE_EFFECTING


def _in_hbm(a):
    return pltpu.with_memory_space_constraint(a, pltpu.HBM)


def _exchange_copies(gather, srcs, lands, send_sems, recv_sems):
    nw = len(srcs)
    x, y, c, chip, peers, peer_chip = _place()
    pairs = []
    for m, (px, py) in enumerate(peers):
        for w in range(nw):
            sems = (send_sems.at[nw * m + w], recv_sems.at[nw * m + w], (px, py, c))
            if gather:
                pairs.append((_remote(srcs[w], lands[w].at[chip], *sems),
                              _remote(srcs[w], lands[w].at[peer_chip[m]], *sems)))
            else:
                pairs.append((_remote(srcs[w].at[m], lands[w].at[m], *sems),) * 2)
    return pairs


def _exchange_start(gather, srcs, after, name, lands=None):
    nw = len(srcs)
    n_copies = 3 * nw

    def body(*refs):
        send_sems, recv_sems = refs[2 * nw + 1], refs[2 * nw + 2]
        for start, _ in _exchange_copies(gather, refs[:nw], refs[nw:2 * nw], send_sems, recv_sems):
            start.start()
        refs[-1][...] = jnp.zeros_like(refs[-1])

    if lands is None:
        lands = [lax.empty(((N_CHIPS,) + s.shape) if gather else s.shape, s.dtype) for s in srcs]
    lands = [_in_hbm(l) for l in lands]
    return pl.pallas_call(
        body, name=name,
        out_shape=(pltpu.SemaphoreType.DMA((n_copies,)), pltpu.SemaphoreType.DMA((n_copies,)))
        + tuple(pltpu.HBM(s.shape, s.dtype) for s in srcs)
        + tuple(pltpu.HBM(l.shape, l.dtype) for l in lands)
        + (jax.ShapeDtypeStruct((8, LANES), F32),),
        in_specs=[_HBM] * (2 * nw) + [_ANY],
        out_specs=(_SEM, _SEM) + (_HBM,) * (2 * nw) + (pl.BlockSpec(memory_space=pltpu.VMEM),),
        input_output_aliases={i: 2 + i for i in range(2 * nw)},
        compiler_params=pltpu.CompilerParams(has_side_effects=_DATAFLOW),
    )(*[_in_hbm(s) for s in srcs], *lands, after)


def _exchange_wait(gather, started, after, name):
    nw = (len(started) - 3) // 2
    send_sems, recv_sems = started[0], started[1]
    thru = started[2:2 + 2 * nw]

    def body(*refs):
        for _, arrival in _exchange_copies(gather, refs[:nw], refs[nw:2 * nw], refs[2 * nw], refs[2 * nw + 1]):
            arrival.wait_send()
            arrival.wait_recv()

    outs = pl.pallas_call(
        body, name=name,
        out_shape=tuple(pltpu.HBM(t.shape, t.dtype) for t in thru),
        in_specs=[_HBM] * (2 * nw) + [_SEM, _SEM, _ANY], out_specs=(_HBM,) * (2 * nw),
        input_output_aliases={i: i for i in range(2 * nw)},
        compiler_params=pltpu.CompilerParams(has_side_effects=_DATAFLOW),
    )(*thru, send_sems, recv_sems, after)
    return outs[nw:]


def _reduce_last(owns, landed, g_small):
    ns = len(owns)
    row_block = 32
    hs = SMALL_ROWS // 2

    def body(*refs):
        own_refs, land_refs, gsm_ref = refs[:ns], refs[ns:2 * ns], refs[2 * ns]
        out_refs, osm_ref = refs[2 * ns + 1:3 * ns + 1], refs[3 * ns + 1]
        ra_sm, p_sm, s_sem, r_sem, sm_s, sm_r = refs[3 * ns + 2:]
        x, y, c, chip, peers, peer_chip = _place()
        sib = (x, y, 1 - c)
        half = lambda cc: pl.ds(pl.multiple_of(cc * hs, 8), hs)
        sm_a = _remote(gsm_ref.at[half(1 - c), :], ra_sm, sm_s.at[0], sm_r.at[0], sib)
        sm_a.start()
        swaps = [sm_a]
        for w in range(ns):
            n = own_refs[w].shape[0]

            def total(i, carry, w=w, n=n):
                r0 = pl.multiple_of(i * row_block, row_block)
                blk = pl.ds(r0, row_block)
                acc = own_refs[w][blk, :]
                for m in range(3):
                    acc = acc + land_refs[w][m, blk, :].astype(F32)
                out_refs[w][pl.ds(pl.multiple_of(c * n + r0, 8), row_block), :] = acc
                return carry
            lax.fori_loop(0, n // row_block, total, 0)
            mine = out_refs[w].at[pl.ds(pl.multiple_of(c * n, 8), n), :]
            swaps.append(_remote(mine, mine, s_sem.at[w], r_sem.at[w], sib))
            swaps[-1].start()
        sm_a.wait_recv()
        p_sm[chip] = gsm_ref[half(c), :] + ra_sm[...]
        for m, (px, py) in enumerate(peers):
            swaps.append(_remote(p_sm.at[chip], p_sm.at[chip], sm_s.at[1 + m], sm_r.at[1 + m], (px, py, c)))
            swaps[-1].start()
        for m, (px, py) in enumerate(peers):
            _remote(p_sm.at[chip], p_sm.at[peer_chip[m]], sm_s.at[1 + m], sm_r.at[1 + m], (px, py, c)).wait_recv()
        osm_ref[half(c), :] = (p_sm[0] + p_sm[1]) + (p_sm[2] + p_sm[3])
        swaps.append(_remote(osm_ref.at[half(c), :], osm_ref.at[half(c), :], sm_s.at[4], sm_r.at[4], sib))
        swaps[-1].start()
        for w in range(ns):
            n = own_refs[w].shape[0]
            theirs = out_refs[w].at[pl.ds(pl.multiple_of((1 - c) * n, 8), n), :]
            _remote(theirs, theirs, s_sem.at[w], r_sem.at[w], sib).wait_recv()
        _remote(osm_ref.at[half(1 - c), :], osm_ref.at[half(1 - c), :], sm_s.at[4], sm_r.at[4], sib).wait_recv()
        for cp in swaps:
            cp.wait_send()

    vmem = pl.BlockSpec(memory_space=pltpu.VMEM)
    return pl.pallas_call(
        body, out_shape=[jax.ShapeDtypeStruct((2 * o.shape[0], o.shape[1]), F32) for o in owns]
        + [jax.ShapeDtypeStruct((SMALL_ROWS, LANES), F32)],
        in_specs=[vmem] * (2 * ns + 1), out_specs=[vmem] * (ns + 1),
        scratch_shapes=[pltpu.VMEM((hs, LANES), F32), pltpu.VMEM((N_CHIPS, hs, LANES), F32),
                        pltpu.SemaphoreType.DMA((ns,)), pltpu.SemaphoreType.DMA((ns,)),
                        pltpu.SemaphoreType.DMA((5,)), pltpu.SemaphoreType.DMA((5,))],
        compiler_params=pltpu.CompilerParams(vmem_limit_bytes=VMEM_LIMIT),
        name="reduce_last")(*owns, *landed, g_small)


_SMALL_PARTS = (("g_norm", 8, 8), ("w_s", 512, 512), ("b_s", 4, 8), ("g_v", 2, 8), ("g_mem", 8, 8),
                ("g_final", 8, 8), ("loss", 8, 8))
_LOSS_ROW = SMALL_ROWS - 8
assert sum(p for _, _, p in _SMALL_PARTS) == SMALL_ROWS


def _pack_small(parts, loss_block):
    rows = []
    for (name, used, padded), p in zip(_SMALL_PARTS, list(parts) + [loss_block]):
        p = p.reshape(used, LANES)
        if padded > used:
            p = jnp.pad(p, ((0, padded - used), (0, 0)))
        rows.append(p)
    return jnp.concatenate(rows, axis=0)


def _local_step(x, mem, target, g_norm, w_in, w_s, b_s, g_v, g_mem, late_weights, g_final,
                fwd_token=None, on_dw=None):
    B, S, _ = x.shape
    x2d = x.reshape(B * S, D_MODEL)
    t2d = target.reshape(B * S, D_MODEL)
    mem2d = mem.reshape(B * N_MEM, D_MODEL)

    proj = _inproj_fwd(x2d, g_norm, w_in, after=() if fwd_token is None else (fwd_token,))
    w_kv, w_out = late_weights(proj)
    kv = _kv_fwd(mem2d, g_mem, w_kv)
    a, lse = _attn_fwd(proj, B, S)
    w_sT = jnp.swapaxes(w_s, 1, 2)
    b_tab = jnp.repeat(b_s.T, HEAD_DIM, axis=1)
    (dx2, da, drest, loss, d_wout, d_ws, d_bs, d_gv, d_gf, dkv) = _mid(
        x2d, t2d, a, proj, kv, w_s, w_sT, b_tab, g_v, w_out, g_final, B, S)
    d_wkv, d_gmem = _kv_bwd(mem2d, g_mem, w_kv, dkv)
    dq, dk, dv = _attn_bwd(proj, a, lse, da, B, S)
    if on_dw is None:
        d_win = _inproj_bwd_dw(dq, dk, dv, drest, x2d, g_norm)
        after = ()
    else:
        d_win = None
        by_chip = lambda g: g.reshape((N_CHIPS, g.shape[0] // N_CHIPS, g.shape[1]))
        after = (on_dw(*_inproj_bwd_dw(dq, dk, dv, drest, x2d, g_norm, reduce_with=[by_chip(d_wkv), by_chip(d_wout)])),)
    grad_x, d_gnorm = _inproj_bwd_dx(dq, dk, dv, drest, x2d, dx2, g_norm, w_in, after=after)
    d_bs = d_bs[:, :N_SGU_GROUPS].T
    return (loss, grad_x.reshape(B, S, D_MODEL),
            dict(g_norm=d_gnorm, w_in=d_win, w_s=d_ws, b_s=d_bs, g_v=d_gv, g_mem=d_gmem, w_kv=d_wkv,
                 w_out=d_wout, g_final=d_gf))


def kernel(x, mem, g_norm, w_in, w_sgu_spatial, b_sgu_spatial, g_sgu_v, g_mem, w_mem_kv, w_out, g_final, loss_target, m_g_norm, m_w_in, m_w_sgu_spatial, m_b_sgu_spatial, m_g_sgu_v, m_g_mem, m_w_mem_kv, m_w_out, m_g_final, v_g_norm, v_w_in, v_w_sgu_spatial, v_b_sgu_spatial, v_g_sgu_v, v_g_mem, v_w_mem_kv, v_w_out, v_g_final):
    t = lambda w: jnp.swapaxes(w[0], 0, 1)
    (win_all,), late_shards, late_lands = _ag_weights([t(w_in)], [w_mem_kv[0], w_out[0]])
    w_in_full = win_all.reshape(-1, win_all.shape[-1])
    late = _exchange_start(True, list(late_shards), win_all, "gather_late_start", lands=late_lands)

    def late_weights(proj):
        return [z.reshape(-1, z.shape[-1]) for z in _exchange_wait(True, late, proj, "gather_late_wait")]

    scatter = {}

    def on_dw(sends, owns):
        scatter["own"] = owns
        scatter["started"] = _exchange_start(False, list(sends), owns[0], "scatter_start")
        return scatter["started"][-1]

    loss, grad_x, g = _local_step(
        x, mem, loss_target, g_norm, w_in_full, w_sgu_spatial[0], b_sgu_spatial[0], g_sgu_v, g_mem,
        late_weights, g_final.reshape(1, D_MODEL), fwd_token=late[-1], on_dw=on_dw)

    small_names = ("g_norm", "w_s", "b_s", "g_v", "g_mem", "g_final")
    g_small = _pack_small([g[n] for n in small_names], loss)
    landed = _exchange_wait(False, scatter["started"], g_small, "scatter_wait")
    gr_in, gr_kv, gr_out, gr_small = _reduce_last(scatter["own"], landed, g_small)

    small_w = (g_norm, w_sgu_spatial, b_sgu_spatial, g_sgu_v, g_mem, g_final)
    small_m = (m_g_norm, m_w_sgu_spatial, m_b_sgu_spatial, m_g_sgu_v, m_g_mem, m_g_final)
    small_v = (v_g_norm, v_w_sgu_spatial, v_b_sgu_spatial, v_g_sgu_v, v_g_mem, v_g_final)
    rows = lambda ws: [w.reshape(-1, LANES) for w in ws]
    small_new, loss = _adamw_small(gr_small, rows(small_w), rows(small_m), rows(small_v))
    loss = loss.reshape(())
    small = [[z.reshape(w.shape) for z in four] for w, four in zip(small_w, small_new)]
    gr_in, d_in, nm_in, nv_in = [jnp.swapaxes(z, 0, 1)
                                 for z in _adamw(t(w_in), gr_in, t(m_w_in), t(v_w_in), "adamw_w_in")]
    gr_kv, d_kv, nm_kv, nv_kv = _adamw(w_mem_kv[0], gr_kv, m_w_mem_kv[0], v_w_mem_kv[0], "adamw_w_kv")
    gr_out, d_out, nm_out, nv_out = _adamw(w_out[0], gr_out, m_w_out[0], v_w_out[0], "adamw_w_out")

    def leaves(kind, big_in, big_kv, big_out):
        s_norm, s_ws, s_bs, s_gv, s_gmem, s_gf = [four[kind] for four in small]
        return [s_norm, big_in[None], s_ws, s_bs, s_gv, s_gmem, big_kv[None], big_out[None], s_gf]

    return (loss, grad_x, *leaves(0, gr_in, gr_kv, gr_out), *leaves(1, d_in, d_kv, d_out),
            *leaves(2, nm_in, nm_kv, nm_out), *leaves(3, nv_in, nv_kv, nv_out))
```

```python
import functools

import jax
import jax.numpy as jnp
from jax import lax
from jax.experimental import pallas as pl
from jax.experimental.pallas import tpu as pltpu

F32 = jnp.float32
BF16 = jnp.bfloat16
MESH = pl.DeviceIdType.MESH

D_MODEL = 1024
ATTN_WIDTH = 512
SGU_WIDTH = 256
MEM_WIDTH = 256
N_MEM = 256
IN_COLS = 3328
QKV_COLS = 3 * ATTN_WIDTH
REST_COLS = IN_COLS - QKV_COLS
SGU_CHUNK = 128
N_SGU_GROUPS = 4
EPS = 1e-6
NEG_INF = -1e30
DILATIONS = (1, 4, 16)
RADIUS = 64
Q_BLOCK = 128
LANES = 128
HEAD_DIM = 64

ADAM_LR = 0.001
ADAM_B1 = 0.9
ADAM_B2 = 0.999
ADAM_EPS = 1e-08
ADAM_WD = 0.01
ADAM_STEP = 10

N_CHIPS = 4
VMEM_LIMIT = 56 * 1024 * 1024
SMALL_ROWS = 560


def _params(sem=None, vmem=VMEM_LIMIT):
    return pltpu.CompilerParams(dimension_semantics=sem, vmem_limit_bytes=vmem)


def _nn(a, b):
    return jnp.dot(a, b, preferred_element_type=F32)


def _nt(a, b):
    return lax.dot_general(a, b, (((1,), (1,)), ((), ())), preferred_element_type=F32)


def _tn(a, b):
    return lax.dot_general(a, b, (((0,), (0,)), ((), ())), preferred_element_type=F32)


def _rms(x):
    r = lax.rsqrt(jnp.mean(x * x, axis=-1, keepdims=True) + EPS)
    return r, x * r


def _head_masks():
    lane = lax.broadcasted_iota(jnp.int32, (1, LANES), 1)
    lo = lane < HEAD_DIM
    return lo, (lo.astype(F32), (~lo).astype(F32))


def _silu_parts(z):
    s = jax.nn.sigmoid(z)
    return z * s, s * (1.0 + z * (1.0 - s))


def _gelu_parts(x):
    c = 0.7978845608028654
    x2 = x * x
    s = jax.nn.sigmoid((2.0 * c) * (x + 0.044715 * (x * x2)))
    return x * s, s * (1.0 + x * (1.0 - s) * ((2.0 * c) * (1.0 + 3.0 * 0.044715 * x2)))


def _after(tokens):
    return [pl.BlockSpec(memory_space=pl.ANY)] * len(tokens)


def _inproj_fwd(x2d, g_norm, w_in_t, after=()):
    T = x2d.shape[0]
    tm = 512

    def body(x_ref, g_ref, w_ref, *rest):
        o_ref = rest[-1]
        _, xh = _rms(x_ref[...])
        h = (xh * g_ref[...]).astype(BF16)
        o_ref[...] = _nt(h, w_ref[...])

    return pl.pallas_call(
        body, grid=(T // tm,),
        in_specs=[pl.BlockSpec((tm, D_MODEL), lambda i: (i, 0)),
                  pl.BlockSpec((1, D_MODEL), lambda i: (0, 0)),
                  pl.BlockSpec((IN_COLS, D_MODEL), lambda i: (0, 0))] + _after(after),
        out_specs=pl.BlockSpec((tm, IN_COLS), lambda i: (i, 0)),
        out_shape=jax.ShapeDtypeStruct((T, IN_COLS), F32),
        compiler_params=_params(("arbitrary",)), name="inproj_fwd")(x2d, g_norm, w_in_t, *after)


def _kv_fwd(mem2d, g_mem, w_kv):
    Tm = mem2d.shape[0]

    def body(m_ref, g_ref, w_ref, o_ref):
        _, mh = _rms(m_ref[...])
        o_ref[...] = _nn((mh * g_ref[...]).astype(BF16), w_ref[...])

    return pl.pallas_call(
        body, out_shape=jax.ShapeDtypeStruct((Tm, 2 * MEM_WIDTH), F32),
        compiler_params=_params(), name="kv_fwd")(mem2d, g_mem, w_kv)


def _kv_bwd(mem2d, g_mem, w_kv, dkv):
    Tm = mem2d.shape[0]

    def body(m_ref, g_ref, w_ref, dkv_ref, dw_ref, dg_ref):
        _, mh = _rms(m_ref[...])
        memn = (mh * g_ref[...]).astype(BF16)
        dkvb = dkv_ref[...].astype(BF16)
        dw_ref[...] = _tn(memn, dkvb)
        dmemn = _nt(dkvb, w_ref[...])
        dg_ref[...] = jnp.sum(dmemn * mh, axis=0, keepdims=True)

    return pl.pallas_call(
        body, out_shape=(jax.ShapeDtypeStruct((D_MODEL, 2 * MEM_WIDTH), F32),
                         jax.ShapeDtypeStruct((1, D_MODEL), F32)),
        compiler_params=_params(), name="kv_bwd")(mem2d, g_mem, w_kv, dkv)


def _attn_geometry(S):
    geom = []
    for d in DILATIONS:
        L = S // d
        assert L % Q_BLOCK == 0
        geom.append((d, L, min(2 * Q_BLOCK, L), L // Q_BLOCK))
    return geom


def _init_bias(bias_scr, geom, hp):
    row = lax.broadcasted_iota(jnp.int32, (Q_BLOCK, 2 * Q_BLOCK), 0)
    col = lax.broadcasted_iota(jnp.int32, (Q_BLOCK, 2 * Q_BLOCK), 1)
    for j in (0, 1):
        bits = (126 - (2 * hp + j)) * (1 << 23)
        slope = lax.bitcast_convert_type(jnp.full((1, 1), bits, jnp.int32), F32)
        for di, (d, _, _, _) in enumerate(geom):
            for cls, off in enumerate((0, -RADIUS, -2 * RADIUS)):
                dist = jnp.abs(col - row + off)
                bias_scr[di * 6 + cls * 2 + j] = jnp.where(
                    dist <= RADIUS, -(slope * float(d)) * dist.astype(F32), NEG_INF)


SPLIT = 4
COPY_ROWS = 256


def _by4_rows(S, step):
    per_class = S // SPLIT // COPY_ROWS
    r, j = step // per_class, step % per_class
    return (pl.ds(r + SPLIT * j * COPY_ROWS, COPY_ROWS, stride=SPLIT),
            pl.ds(r * (S // SPLIT) + j * COPY_ROWS, COPY_ROWS))


def _to_by4(src, dst, S):
    for i in range(S // COPY_ROWS):
        natural, by4 = _by4_rows(S, i)
        dst[by4, :] = src[natural, :]


def _block_slices(d, L, KW, nqb, r, qb, S):
    qs = qb * Q_BLOCK
    ks = jnp.clip(qs - RADIUS, 0, L - KW)
    cls = jnp.where(qb == 0, 0, jnp.where(qb == nqb - 1, 2, 1))
    if d == 1:
        qsl = pl.ds(pl.multiple_of(qs, Q_BLOCK), Q_BLOCK)
        ksl = pl.ds(pl.multiple_of(ks, RADIUS), KW)
    elif d == SPLIT:
        qsl = pl.ds(pl.multiple_of(r * L + qs, Q_BLOCK), Q_BLOCK)
        ksl = pl.ds(pl.multiple_of(r * L + ks, RADIUS), KW)
    else:
        sub = d // SPLIT
        base = (r % SPLIT) * (S // SPLIT) + r // SPLIT
        qsl = pl.ds(base + qs * sub, Q_BLOCK, stride=sub)
        ksl = pl.ds(base + ks * sub, KW, stride=sub)
    return qsl, ksl, cls


def _for_groups(geom, S, group, fn):
    for di, (d, L, KW, nqb) in enumerate(geom):
        n = group[di]
        assert (d * nqb) % n == 0

        def step(it, carry, di=di, d=d, L=L, KW=KW, nqb=nqb, n=n):
            slices = []
            for g in range(n):
                i = it * n + g
                slices.append(_block_slices(d, L, KW, nqb, i // nqb, i % nqb, S))
            fn(di, KW, slices)
            return carry
        lax.fori_loop(0, d * nqb // n, step, 0)


def _attn_fwd(proj, B, S):
    T = B * S
    geom = _attn_geometry(S)
    n_pairs = ATTN_WIDTH // LANES

    def body(q_ref, k_ref, v_ref, a_ref, lse_ref, bias_scr, q4, k4, v4, *per_dilation):
        o_scr, m_scr, l_scr = per_dilation[0:3], per_dilation[3:6], per_dilation[6:9]
        lo, hm = _head_masks()
        pair = pl.program_id(0)

        @pl.when(pl.program_id(1) == 0)
        def _():
            _init_bias(bias_scr, geom, pair)
        for src, dst in ((q_ref, q4), (k_ref, k4), (v_ref, v4)):
            _to_by4(src, dst, S)

        def group(di, KW, all_slices):
            run = 8
            for first in range(0, len(all_slices), run):
                some(di, KW, all_slices[first:first + run])

        def some(di, KW, slices):
            chains = [(g, j) for g in range(len(slices)) for j in (0, 1)]
            q_src, k_src, v_src = (q_ref, k_ref, v_ref) if di == 0 else (q4, k4, v4)
            q = [q_src[qsl, :] for qsl, _, _ in slices]
            kw = [k_src[ksl, :].astype(BF16) for _, ksl, _ in slices]
            vw = [v_src[ksl, :].astype(BF16) for _, ksl, _ in slices]
            s = {(g, j): _nt((q[g] * (hm[j] * 0.125)).astype(BF16), kw[g])
                 + bias_scr[di * 6 + slices[g][2] * 2 + j, :, pl.ds(0, KW)] for g, j in chains}
            m = {c: jnp.max(s[c], axis=1, keepdims=True) for c in chains}
            p = {c: jnp.exp(s[c] - m[c]) for c in chains}
            l = {c: jnp.sum(p[c], axis=1, keepdims=True) for c in chains}
            o = {(g, j): _nn(p[(g, j)].astype(BF16), vw[g]) for g, j in chains}
            for g, (qsl, _, _) in enumerate(slices):
                o_scr[di][qsl, :] = jnp.where(lo, o[(g, 0)], o[(g, 1)])
                m_scr[di][qsl, :] = jnp.where(lo, m[(g, 0)], m[(g, 1)])
                l_scr[di][qsl, :] = jnp.where(lo, l[(g, 0)], l[(g, 1)])

        _for_groups(geom, S, (16, 16, 16), group)

        for i in range(S // COPY_ROWS):
            natural, by4 = _by4_rows(S, i)
            rows = [natural, by4, by4]
            ms = [m_scr[di][rows[di], :] for di in range(3)]
            mx = jnp.maximum(jnp.maximum(ms[0], ms[1]), ms[2])
            num = 0.0
            den = 0.0
            for di in range(3):
                w = jnp.exp(ms[di] - mx)
                num = num + w * o_scr[di][rows[di], :]
                den = den + w * l_scr[di][rows[di], :]
            a_ref[natural, :] = num / den
            lse_ref[natural, :] = mx + jnp.log(den)

    blk = lambda off: pl.BlockSpec((S, LANES), lambda h, b, off=off: (b, off + h))
    out_blk = pl.BlockSpec((S, LANES), lambda h, b: (b, h))
    return pl.pallas_call(
        body, grid=(n_pairs, B),
        in_specs=[blk(0), blk(n_pairs), blk(2 * n_pairs)],
        out_specs=[out_blk, out_blk],
        out_shape=[jax.ShapeDtypeStruct((T, ATTN_WIDTH), F32)] * 2,
        scratch_shapes=[pltpu.VMEM((18, Q_BLOCK, 2 * Q_BLOCK), F32)] + [pltpu.VMEM((S, LANES), F32)] * 12,
        compiler_params=_params(("arbitrary", "arbitrary")), name="attn_fwd")(proj, proj, proj)


def _attn_bwd(proj, a, lse, da, B, S):
    T = B * S
    geom = _attn_geometry(S)
    n_pairs = ATTN_WIDTH // LANES

    def body(q_ref, k_ref, v_ref, a_ref, lse_ref, do_ref, dq_ref, dk_ref, dv_ref, bias_scr, *scr):
        acc = (scr[0:3], scr[3:6])
        natural_in = (q_ref, k_ref, v_ref, a_ref, lse_ref, do_ref)
        by4_in = scr[6:12]
        _, hm = _head_masks()
        pair = pl.program_id(0)

        @pl.when(pl.program_id(1) == 0)
        def _():
            _init_bias(bias_scr, geom, pair)
        for ref in scr[0:6]:
            ref[...] = jnp.zeros_like(ref)
        for src, dst in zip(natural_in, by4_in):
            _to_by4(src, dst, S)

        def group(di, KW, all_slices):
            run = (4, 4, 8)[di]
            for first in range(0, len(all_slices), run):
                some(di, KW, all_slices[first:first + run])

        def some(di, KW, slices):
            n = len(slices)
            chains = [(g, j) for g in range(n) for j in (0, 1)]
            q_src, k_src, v_src, a_src, lse_src, do_src = natural_in if di == 0 else by4_in
            dq_scr, dk_scr, dv_scr = acc[0 if di == 0 else 1]
            q = [q_src[qsl, :] for qsl, _, _ in slices]
            do = [do_src[qsl, :] for qsl, _, _ in slices]
            doa = [do[g] * a_src[slices[g][0], :] for g in range(n)]
            lse_q = [lse_src[qsl, :] for qsl, _, _ in slices]
            kw = [k_src[ksl, :].astype(BF16) for _, ksl, _ in slices]
            vw = [v_src[ksl, :].astype(BF16) for _, ksl, _ in slices]
            qj = {(g, j): (q[g] * (hm[j] * 0.125)).astype(BF16) for g, j in chains}
            doj = {(g, j): (do[g] * hm[j]).astype(BF16) for g, j in chains}
            s = {(g, j): _nt(qj[(g, j)], kw[g])
                 + bias_scr[di * 6 + slices[g][2] * 2 + j, :, pl.ds(0, KW)] for g, j in chains}
            dp = {(g, j): _nt(doj[(g, j)], vw[g]) for g, j in chains}
            dsum = {(g, j): jnp.sum(doa[g] * hm[j], axis=1, keepdims=True) for g, j in chains}
            p = {(g, j): jnp.exp(s[(g, j)] - lse_q[g][:, HEAD_DIM * j:HEAD_DIM * j + 1]) for g, j in chains}
            ds = {c: (p[c] * (dp[c] - dsum[c])).astype(BF16) for c in chains}
            pb = {c: p[c].astype(BF16) for c in chains}
            dq = [_nn(ds[(g, 0)], kw[g]) * (hm[0] * 0.125) + _nn(ds[(g, 1)], kw[g]) * (hm[1] * 0.125)
                  for g in range(n)]
            both = lambda t, g: jnp.concatenate([t[(g, 0)], t[(g, 1)]], axis=0)
            dkw = [_tn(both(ds, g), both(qj, g)) for g in range(n)]
            dvw = [_tn(both(pb, g), both(doj, g)) for g in range(n)]
            for g, (qsl, ksl, _) in enumerate(slices):
                dq_scr[qsl, :] = dq_scr[qsl, :] + dq[g]
                dk_scr[ksl, :] = dk_scr[ksl, :] + dkw[g]
                dv_scr[ksl, :] = dv_scr[ksl, :] + dvw[g]

        _for_groups(geom, S, (16, 16, 16), group)

        for i in range(S // COPY_ROWS):
            natural, by4 = _by4_rows(S, i)
            for nat, split in zip(*acc):
                nat[natural, :] = nat[natural, :] + split[by4, :]
        for out, nat in zip((dq_ref, dk_ref, dv_ref), acc[0]):
            out[...] = nat[...].astype(BF16)

    blk = lambda off: pl.BlockSpec((S, LANES), lambda h, b, off=off: (b, off + h))
    return pl.pallas_call(
        body, grid=(n_pairs, B),
        in_specs=[blk(0), blk(n_pairs), blk(2 * n_pairs), blk(0), blk(0), blk(0)],
        out_specs=[blk(0), blk(0), blk(0)],
        out_shape=[jax.ShapeDtypeStruct((T, ATTN_WIDTH), BF16)] * 3,
        scratch_shapes=[pltpu.VMEM((18, Q_BLOCK, 2 * Q_BLOCK), F32)] + [pltpu.VMEM((S, LANES), F32)] * 12,
        compiler_params=_params(("arbitrary", "arbitrary")), name="attn_bwd")(proj, proj, proj, a, lse, da)


def _mid(x2d, t2d, a, proj, kv, w_s, w_sT, b_tab, g_v, w_out, g_final, B, S):
    T = B * S
    tm = 512
    nt = S // tm
    halves = 2
    hrows = tm // halves

    def body(x_ref, t_ref, a_ref, za_ref, ub_ref, vb_ref, zb_ref, qm_ref, zm_ref, kv_ref,
              ws_ref, wsT_ref, btab_ref, gv_ref, wout_ref, gf_ref,
              dx2_ref, da_ref, drest_ref, loss_ref, dwout_ref, dws_ref, dbs_ref, dgv_ref, dgf_ref, dkv_ref,
              dbtab_scr):
        b = pl.program_id(0)
        t = pl.program_id(1)
        first = jnp.logical_and(b == 0, t == 0)
        last = jnp.logical_and(b == B - 1, t == nt - 1)
        _, hm = _head_masks()
        lane_g = lax.broadcasted_iota(jnp.int32, (1, SGU_WIDTH), 1) // HEAD_DIM
        gm = [(lane_g == g).astype(F32) for g in range(N_SGU_GROUPS)]
        H = range(halves)
        rows = [pl.ds(h * hrows, hrows) for h in H]
        ld = lambda ref: [ref[r, :] for r in rows]
        cat = lambda parts, axis: jnp.concatenate(parts, axis=axis)
        chunks = [slice(ci * SGU_CHUNK, (ci + 1) * SGU_CHUNK) for ci in range(hrows // SGU_CHUNK)]
        pairs = [slice(pr * LANES, (pr + 1) * LANES) for pr in range(2)]
        heads = [(pr, j) for pr in range(2) for j in (0, 1)]

        @pl.when(first)
        def _():
            loss_ref[...] = jnp.zeros_like(loss_ref)
            dwout_ref[...] = jnp.zeros_like(dwout_ref)
            dws_ref[...] = jnp.zeros_like(dws_ref)
            dbs_ref[...] = jnp.zeros_like(dbs_ref)
            dgv_ref[...] = jnp.zeros_like(dgv_ref)
            dgf_ref[...] = jnp.zeros_like(dgf_ref)
            dbtab_scr[...] = jnp.zeros_like(dbtab_scr)

        @pl.when(t == 0)
        def _():
            dkv_ref[...] = jnp.zeros_like(dkv_ref)

        a_val = ld(a_ref)
        sil_a = [_silu_parts(z) for z in ld(za_ref)]
        gated_a = [s[0] * a for s, a in zip(sil_a, a_val)]
        u = [_gelu_parts(z) for z in ld(ub_ref)]
        vv = [_gelu_parts(z) for z in ld(vb_ref)]
        vnorm = [_rms(v[0]) for v in vv]
        gv = gv_ref[...]
        vn = [(n[1] * gv).astype(BF16) for n in vnorm]
        w_cat = cat([ws_ref[g].astype(BF16) for g in range(N_SGU_GROUPS)], 1)
        wT_cat = cat([wsT_ref[g].astype(BF16) for g in range(N_SGU_GROUPS)], 1)
        gmb = [m.astype(BF16) for m in gm]
        by_group = lambda chunk: cat([chunk * gmb[g] for g in range(N_SGU_GROUPS)], 0)
        btab = btab_ref[...]
        mixed = [cat([btab + _nn(w_cat, by_group(vn[h][c, :])) for c in chunks], 0) for h in H]
        sg = [u[h][0] * mixed[h] for h in H]
        sil_b = [_silu_parts(z) for z in ld(zb_ref)]
        gated_b = [sil_b[h][0] * sg[h] for h in H]

        kvv = kv_ref[...].astype(BF16)
        kp = [kvv[:, p] for p in pairs]
        vp = [kvv[:, MEM_WIDTH + pr * LANES:MEM_WIDTH + (pr + 1) * LANES] for pr in range(2)]
        qm = ld(qm_ref)
        qj = {(h, pr, j): (qm[h][:, pairs[pr]] * (hm[j] * 0.125)).astype(BF16) for h in H for pr, j in heads}
        sc = {k: _nt(qj[k], kp[k[1]]) for k in qj}
        ex = {k: jnp.exp(sc[k] - jnp.max(sc[k], axis=1, keepdims=True)) for k in qj}
        prob = {k: ex[k] * (1.0 / jnp.sum(ex[k], axis=1, keepdims=True)) for k in qj}
        probb = {k: prob[k].astype(BF16) for k in qj}
        mo = [cat([sum(_nn(probb[(h, pr, j)], vp[pr]) * hm[j] for j in (0, 1)) for pr in range(2)], 1) for h in H]
        sil_m = [_silu_parts(z) for z in ld(zm_ref)]
        gated_m = [sil_m[h][0] * mo[h] for h in H]

        gated = [cat([gated_a[h], gated_b[h], gated_m[h]], 1).astype(BF16) for h in H]
        wout = wout_ref[...]
        x_in = ld(x_ref)
        x2 = [x_in[h] + _nn(gated[h], wout) for h in H]
        fin = [_rms(z) for z in x2]
        gf = gf_ref[...]
        tgt = ld(t_ref)
        err = [fin[h][1] * gf - tgt[h] for h in H]
        loss_ref[...] += sum(jnp.sum(e * e) for e in err) * (0.5 / D_MODEL)

        dy = [e * (1.0 / D_MODEL) for e in err]
        dgf_ref[...] += sum(jnp.sum(dy[h] * fin[h][1], axis=0, keepdims=True) for h in H)
        gdy = [d * gf for d in dy]
        dx2 = [fin[h][0] * (gdy[h] - fin[h][1] * jnp.mean(gdy[h] * fin[h][1], axis=1, keepdims=True)) for h in H]
        for h in H:
            dx2_ref[rows[h], :] = dx2[h]
        dx2b = [d.astype(BF16) for d in dx2]
        dgated = [_nt(d, wout) for d in dx2b]
        dwout_ref[...] += _tn(cat(gated, 0), cat(dx2b, 0))
        dga = [d[:, 0:ATTN_WIDTH] for d in dgated]
        dgb = [d[:, ATTN_WIDTH:ATTN_WIDTH + SGU_WIDTH] for d in dgated]
        dgm = [d[:, ATTN_WIDTH + SGU_WIDTH:] for d in dgated]

        for h in H:
            da_ref[rows[h], :] = dga[h] * sil_a[h][0]
        dza = [dga[h] * a_val[h] * sil_a[h][1] for h in H]

        dsg = [dgb[h] * sil_b[h][0] for h in H]
        dzb = [dgb[h] * sg[h] * sil_b[h][1] for h in H]
        dub = [dsg[h] * mixed[h] * u[h][1] for h in H]
        dmixed = [dsg[h] * u[h][0] for h in H]
        dmixed_b = [d.astype(BF16) for d in dmixed]
        dvn = [cat([_nn(wT_cat, by_group(dmixed_b[h][c, :])) for c in chunks], 0) for h in H]
        for g in range(N_SGU_GROUPS):
            dws_ref[g] += sum(_nt((dmixed[h][c, :] * gm[g]).astype(BF16), vn[h][c, :]) for h in H for c in chunks)
        dbtab_scr[...] += sum(dmixed[h][c, :] for h in H for c in chunks)
        dgv_ref[...] += sum(jnp.sum(dvn[h] * vnorm[h][1], axis=0, keepdims=True) for h in H)
        tv = [d * gv for d in dvn]
        dvv = [vnorm[h][0] * (tv[h] - vnorm[h][1] * jnp.mean(tv[h] * vnorm[h][1], axis=1, keepdims=True)) for h in H]
        dvb = [dvv[h] * vv[h][1] for h in H]

        dmo = [dgm[h] * sil_m[h][0] for h in H]
        dzm = [dgm[h] * mo[h] * sil_m[h][1] for h in H]
        dmoj = {(h, pr, j): (dmo[h][:, pairs[pr]] * hm[j]).astype(BF16) for h in H for pr, j in heads}
        dp = {k: _nt(dmoj[k], vp[k[1]]) for k in qj}
        ds = {k: (prob[k] * (dp[k] - jnp.sum(dp[k] * prob[k], axis=1, keepdims=True))).astype(BF16) for k in qj}
        dqm = [cat([sum(_nn(ds[(h, pr, j)], kp[pr]) * (hm[j] * 0.125) for j in (0, 1)) for pr in range(2)], 1)
               for h in H]
        every = lambda tbl, pr: cat([tbl[(h, pr, j)] for h in H for j in (0, 1)], 0)
        dk = [_tn(every(ds, pr), every(qj, pr)) for pr in range(2)]
        dv = [_tn(every(probb, pr), every(dmoj, pr)) for pr in range(2)]
        dkv_ref[...] += cat(dk + dv, 1)

        for h in H:
            drest_ref[rows[h], :] = cat([dza[h], dub[h], dvb[h], dzb[h], dqm[h], dzm[h]], 1).astype(BF16)

        @pl.when(last)
        def _():
            lane = lax.broadcasted_iota(jnp.int32, (1, LANES), 1)
            dbt = dbtab_scr[...]
            out = jnp.zeros((SGU_CHUNK, LANES), F32)
            for g in range(N_SGU_GROUPS):
                out = out + jnp.where(lane == g, jnp.sum(dbt * gm[g], axis=1, keepdims=True), 0.0)
            dbs_ref[...] = out

    tile = lambda w, cb: pl.BlockSpec((tm, w), lambda b, t, cb=cb: (b * nt + t, cb))
    const = lambda shape: pl.BlockSpec(shape, lambda b, t, n=len(shape): (0,) * n)
    return pl.pallas_call(
        body, grid=(B, nt),
        in_specs=[tile(D_MODEL, 0), tile(D_MODEL, 0), tile(ATTN_WIDTH, 0),
                  tile(ATTN_WIDTH, 3),
                  tile(SGU_WIDTH, 8), tile(SGU_WIDTH, 9), tile(SGU_WIDTH, 10),
                  tile(MEM_WIDTH, 11), tile(MEM_WIDTH, 12),
                  pl.BlockSpec((N_MEM, 2 * MEM_WIDTH), lambda b, t: (b, 0)),
                  const((N_SGU_GROUPS, SGU_CHUNK, SGU_CHUNK)), const((N_SGU_GROUPS, SGU_CHUNK, SGU_CHUNK)),
                  const((SGU_CHUNK, SGU_WIDTH)), const((1, SGU_WIDTH)),
                  const((D_MODEL, D_MODEL)), const((1, D_MODEL))],
        out_specs=[tile(D_MODEL, 0), tile(ATTN_WIDTH, 0), tile(REST_COLS, 0),
                   const((8, LANES)), const((D_MODEL, D_MODEL)),
                   const((N_SGU_GROUPS, SGU_CHUNK, SGU_CHUNK)), const((SGU_CHUNK, LANES)),
                   const((1, SGU_WIDTH)), const((1, D_MODEL)),
                   pl.BlockSpec((N_MEM, 2 * MEM_WIDTH), lambda b, t: (b, 0))],
        out_shape=[jax.ShapeDtypeStruct((T, D_MODEL), F32), jax.ShapeDtypeStruct((T, ATTN_WIDTH), F32),
                   jax.ShapeDtypeStruct((T, REST_COLS), BF16),
                   jax.ShapeDtypeStruct((8, LANES), F32), jax.ShapeDtypeStruct((D_MODEL, D_MODEL), F32),
                   jax.ShapeDtypeStruct((N_SGU_GROUPS, SGU_CHUNK, SGU_CHUNK), F32),
                   jax.ShapeDtypeStruct((SGU_CHUNK, LANES), F32),
                   jax.ShapeDtypeStruct((1, SGU_WIDTH), F32), jax.ShapeDtypeStruct((1, D_MODEL), F32),
                   jax.ShapeDtypeStruct((B * N_MEM, 2 * MEM_WIDTH), F32)],
        scratch_shapes=[pltpu.VMEM((SGU_CHUNK, SGU_WIDTH), F32)],
        compiler_params=_params(("arbitrary", "arbitrary")), name="mid")(
            x2d, t2d, a, proj, proj, proj, proj, proj, proj, kv, w_s, w_sT, b_tab, g_v, w_out, g_final)


def _inproj_bwd_dx(dq, dk, dv, drest, x2d, dx2, g_norm, w_in_t, after=()):
    T = x2d.shape[0]
    tm = 512
    W = ATTN_WIDTH

    def body(dq_ref, dk_ref, dv_ref, dr_ref, x_ref, dx2_ref, g_ref, w_ref, *rest):
        gx_ref, dg_ref = rest[-2:]

        @pl.when(pl.program_id(0) == 0)
        def _():
            dg_ref[...] = jnp.zeros_like(dg_ref)

        halves = [pl.ds(h * (tm // 2), tm // 2) for h in (0, 1)]
        dh = [(_nn(dq_ref[r, :], w_ref[0:W, :]) + _nn(dk_ref[r, :], w_ref[W:2 * W, :])
               + _nn(dv_ref[r, :], w_ref[2 * W:3 * W, :]) + _nn(dr_ref[r, :], w_ref[QKV_COLS:IN_COLS, :]))
              for r in halves]
        nrm = [_rms(x_ref[r, :]) for r in halves]
        dg_ref[...] += sum(jnp.sum(d * n[1], axis=0, keepdims=True) for d, n in zip(dh, nrm))
        g = g_ref[...]
        for r, d, (rstd, xh) in zip(halves, dh, nrm):
            th = d * g
            gx_ref[r, :] = rstd * (th - xh * jnp.mean(th * xh, axis=1, keepdims=True)) + dx2_ref[r, :]

    tile = lambda w: pl.BlockSpec((tm, w), lambda i: (i, 0))
    return pl.pallas_call(
        body, grid=(T // tm,),
        in_specs=[tile(W), tile(W), tile(W), tile(REST_COLS), tile(D_MODEL), tile(D_MODEL),
                  pl.BlockSpec((1, D_MODEL), lambda i: (0, 0)),
                  pl.BlockSpec((IN_COLS, D_MODEL), lambda i: (0, 0))] + _after(after),
        out_specs=[tile(D_MODEL), pl.BlockSpec((1, D_MODEL), lambda i: (0, 0))],
        out_shape=[jax.ShapeDtypeStruct((T, D_MODEL), F32), jax.ShapeDtypeStruct((1, D_MODEL), F32)],
        compiler_params=_params(("arbitrary",)), name="inproj_bwd_dx")(
            dq, dk, dv, drest, x2d, dx2, g_norm, w_in_t, *after)


def _inproj_bwd_dw(dq, dk, dv, drest, x2d, g_norm, reduce_with=None):
    T = x2d.shape[0]
    tm = 512
    nt = T // tm
    W = ATTN_WIDTH
    fused = reduce_with is not None
    others = list(reduce_with) if fused else []
    ns = 1 + len(others)
    shard = IN_COLS // N_CHIPS
    halves = [shard // 2] + [s.shape[1] // 2 for s in others]
    cols = [D_MODEL] + [s.shape[2] for s in others]
    row_block = 32

    def body(dq_ref, dk_ref, dv_ref, dr_ref, x_ref, g_ref, *rest):
        if fused:
            stacks = rest[:ns - 1]
            sends, owns = rest[ns - 1:2 * ns - 1], rest[2 * ns - 1:3 * ns - 1]
            acc, ras, narrow = rest[3 * ns - 1], rest[3 * ns:4 * ns], rest[4 * ns]
            s_sem, r_sem = rest[4 * ns + 1], rest[4 * ns + 2]
            x, y, c, chip, peers, peer_chip = _place()
            sib = (x, y, 1 - c)

            def part(w, k, cc, r0=0, rows=None):
                n = halves[w]
                rows = n if rows is None else rows
                if w == 0:
                    return acc.at[pl.ds(pl.multiple_of(k * shard + cc * n + r0, 8), rows), :]
                return stacks[w - 1].at[k, pl.ds(pl.multiple_of(cc * n + r0, 8), rows), :]

            def swap_other(w):
                theirs = stacks[w - 1].at[:, pl.ds(pl.multiple_of((1 - c) * halves[w], 8), halves[w]), :]
                return _remote(theirs, ras[w], s_sem.at[N_CHIPS - 1 + w], r_sem.at[N_CHIPS - 1 + w], sib)

            def swap_win(k):
                return _remote(narrow.at[k], ras[0].at[k], s_sem.at[k], r_sem.at[k], sib)
        else:
            acc = rest[0]

        @pl.when(pl.program_id(0) == 0)
        def _():
            acc[...] = jnp.zeros_like(acc)
            for w in range(1, ns):
                swap_other(w).start()

        _, xh = _rms(x_ref[...])
        h = (xh * g_ref[...]).astype(BF16)
        acc[0:W, :] += _tn(dq_ref[...], h)
        acc[W:2 * W, :] += _tn(dk_ref[...], h)
        acc[2 * W:3 * W, :] += _tn(dv_ref[...], h)
        acc[QKV_COLS:IN_COLS, :] += _tn(dr_ref[...], h)

        if fused:
            @pl.when(pl.program_id(0) == nt - 1)
            def _():
                for k in range(N_CHIPS):
                    def to_bf16(i, carry, k=k):
                        r0 = pl.multiple_of(i * row_block, row_block)
                        narrow[k, pl.ds(r0, row_block), :] = part(0, k, 1 - c, r0, row_block)[...].astype(BF16)
                        return carry
                    lax.fori_loop(0, halves[0] // row_block, to_bf16, 0)
                    swap_win(k).start()
                def chip_sum(w, k, r0):
                    blk = pl.ds(r0, row_block)
                    return part(w, k, c, r0, row_block)[...] + ras[w][k, blk, :].astype(F32)

                for w in range(1, ns):
                    swap_other(w).wait_recv()

                    def sums(i, carry, w=w):
                        r0 = pl.multiple_of(i * row_block, row_block)
                        for m in range(3):
                            sends[w][m, pl.ds(r0, row_block), :] = chip_sum(w, peer_chip[m], r0).astype(BF16)
                        owns[w][pl.ds(r0, row_block), :] = chip_sum(w, chip, r0)
                        return carry
                    lax.fori_loop(0, halves[w] // row_block, sums, 0)
                for k in range(N_CHIPS):
                    swap_win(k).wait_recv()

                    @pl.when(chip == k)
                    def _(k=k):
                        def own(i, carry):
                            r0 = pl.multiple_of(i * row_block, row_block)
                            owns[0][pl.ds(r0, row_block), :] = chip_sum(0, k, r0)
                            return carry
                        lax.fori_loop(0, halves[0] // row_block, own, 0)

                    @pl.when(chip != k)
                    def _(k=k):
                        def other(i, carry):
                            r0 = pl.multiple_of(i * row_block, row_block)
                            sends[0][(k ^ chip) - 1, pl.ds(r0, row_block), :] = chip_sum(0, k, r0).astype(BF16)
                            return carry
                        lax.fori_loop(0, halves[0] // row_block, other, 0)
                for k in range(N_CHIPS):
                    swap_win(k).wait_send()
                for w in range(1, ns):
                    swap_other(w).wait_send()

    tile = lambda w: pl.BlockSpec((tm, w), lambda i: (i, 0))
    vmem = pl.BlockSpec(memory_space=pltpu.VMEM)
    in_specs = [tile(W), tile(W), tile(W), tile(REST_COLS), tile(D_MODEL), pl.BlockSpec((1, D_MODEL), lambda i: (0, 0))]
    if not fused:
        return pl.pallas_call(
            body, grid=(nt,), in_specs=in_specs,
            out_specs=pl.BlockSpec((IN_COLS, D_MODEL), lambda i: (0, 0)),
            out_shape=jax.ShapeDtypeStruct((IN_COLS, D_MODEL), F32),
            compiler_params=_params(("arbitrary",)), name="inproj_bwd_dw")(dq, dk, dv, drest, x2d, g_norm)
    outs = pl.pallas_call(
        body, grid=(nt,), in_specs=in_specs + [vmem] * (ns - 1), out_specs=[vmem] * (2 * ns),
        out_shape=[jax.ShapeDtypeStruct((3, n, cl), BF16) for n, cl in zip(halves, cols)]
        + [jax.ShapeDtypeStruct((n, cl), F32) for n, cl in zip(halves, cols)],
        scratch_shapes=[pltpu.VMEM((IN_COLS, D_MODEL), F32)]
        + [pltpu.VMEM((N_CHIPS, n, cl), BF16 if w == 0 else F32) for w, (n, cl) in enumerate(zip(halves, cols))]
        + [pltpu.VMEM((N_CHIPS, halves[0], D_MODEL), BF16)]
        + [pltpu.SemaphoreType.DMA((N_CHIPS - 1 + ns,)), pltpu.SemaphoreType.DMA((N_CHIPS - 1 + ns,))],
        compiler_params=_params(("arbitrary",)), name="inproj_bwd_dw_reduce")(
            dq, dk, dv, drest, x2d, g_norm, *others)
    return outs[:ns], outs[ns:]


def _adamw_update(w, g, m, v):
    nm = ADAM_B1 * m + (1.0 - ADAM_B1) * g
    nv = ADAM_B2 * v + (1.0 - ADAM_B2) * (g * g)
    m_hat = nm / (1.0 - ADAM_B1 ** ADAM_STEP)
    v_hat = nv / (1.0 - ADAM_B2 ** ADAM_STEP)
    return -ADAM_LR * (m_hat / (jnp.sqrt(v_hat) + ADAM_EPS) + ADAM_WD * w), nm, nv


def _adamw(w, g, m, v, name):
    R, C = w.shape
    br = max(r for r in range(8, 257, 8) if R % r == 0)

    def body(w_ref, g_ref, m_ref, v_ref, g_out, d_ref, nm_ref, nv_ref):
        g = g_ref[...]
        g_out[...] = g
        d_ref[...], nm_ref[...], nv_ref[...] = _adamw_update(w_ref[...], g, m_ref[...], v_ref[...])

    spec = pl.BlockSpec((br, C), lambda i: (i, 0))
    return pl.pallas_call(
        body, grid=(R // br,), in_specs=[spec] * 4, out_specs=[spec] * 4,
        out_shape=[jax.ShapeDtypeStruct((R, C), F32)] * 4,
        compiler_params=_params(("arbitrary",)), name=name)(w, g, m, v)


def _adamw_small(g_packed, ws, ms, vs):
    n = len(ws)

    def body(*refs):
        g_ref = refs[0]
        w_refs, m_refs, v_refs = refs[1:1 + n], refs[1 + n:1 + 2 * n], refs[1 + 2 * n:1 + 3 * n]
        outs = refs[1 + 3 * n:]
        off = 0
        for i, (_, used, padded) in enumerate(_SMALL_PARTS[:n]):
            g = g_ref[off:off + used, :]
            delta, nm, nv = _adamw_update(w_refs[i][...], g, m_refs[i][...], v_refs[i][...])
            outs[4 * i][...], outs[4 * i + 1][...], outs[4 * i + 2][...], outs[4 * i + 3][...] = g, delta, nm, nv
            off += padded
        outs[4 * n][...] = g_ref[_LOSS_ROW:_LOSS_ROW + 1, 0:1]

    outs = pl.pallas_call(
        body, out_shape=[jax.ShapeDtypeStruct(w.shape, F32) for w in ws for _ in range(4)]
        + [jax.ShapeDtypeStruct((1, 1), F32)],
        compiler_params=_params(), name="adamw_small")(g_packed, *ws, *ms, *vs)
    return [outs[4 * i:4 * i + 4] for i in range(n)], outs[4 * n]


def _place():
    x, y, c = lax.axis_index("x"), lax.axis_index("y"), lax.axis_index("c")
    chip = 2 * x + y
    peers = [(x, 1 - y), (1 - x, y), (1 - x, 1 - y)]
    peer_chip = [2 * px + py for px, py in peers]
    return x, y, c, chip, peers, peer_chip


def _remote(src, dst, send_sem, recv_sem, dev):
    return pltpu.make_async_remote_copy(src_ref=src, dst_ref=dst, send_sem=send_sem, recv_sem=recv_sem,
                                        device_id=dev, device_id_type=MESH)


def _ag_weights(weights, late=()):
    nw, nl = len(weights), len(late)
    pieces = 2

    def body(*refs):
        srcs, late_srcs = refs[:nw], refs[nw:nw + nl]
        outs, late_bf, late_land = (refs[nw + nl:2 * nw + nl], refs[2 * nw + nl:2 * nw + 2 * nl],
                                    refs[2 * nw + 2 * nl:2 * nw + 3 * nl])
        s_ici, r_ici, s_d2d, r_d2d = refs[2 * nw + 3 * nl:]
        x, y, c = lax.axis_index("x"), lax.axis_index("y"), lax.axis_index("c")
        chip = 2 * x + y
        sib = (x, y, 1 - c)
        first = ((x + 1 - c) % 2, (y + c) % 2)
        second = ((x + c) % 2, (y + 1 - c) % 2)
        first_chip, second_chip = 2 * first[0] + first[1], 2 * second[0] + second[1]
        diag_chip = 3 - chip
        for src, out in zip(srcs, outs):
            out[chip] = src[...].astype(BF16)

        parts = [(w, out, pc) for w, out in enumerate(outs) for pc in range(pieces)]

        def piece(out, k, cc, pc):
            rows = out.shape[1] // 2 // pieces
            return out.at[k, pl.ds(pl.multiple_of((cc * pieces + pc) * rows, 16), rows), :]

        def ici(w, slot, out, k, dev, pc):
            blk, sem = piece(out, k, c, pc), (nw * slot + w) * pieces + pc
            return _remote(blk, blk, s_ici.at[sem], r_ici.at[sem], (dev[0], dev[1], c))

        def d2d(w, slot, out, k, cc, pc):
            blk, sem = piece(out, k, cc, pc), (nw * slot + w) * pieces + pc
            return _remote(blk, blk, s_d2d.at[sem], r_d2d.at[sem], sib)

        sent = []
        for slot, dev in enumerate((first, second)):
            for w, out, pc in parts:
                sent.append(ici(w, slot, out, chip, dev, pc))
                sent[-1].start()
        for src, bf, land in zip(late_srcs, late_bf, late_land):
            bf[...] = src[...].astype(BF16)
            land[...] = jnp.zeros_like(land)
            land[chip] = bf[...]
        for slot, k, dev in ((0, first_chip, first), (1, second_chip, second), (2, diag_chip, second)):
            for w, out, pc in parts:
                ici(w, slot, out, k, dev, pc).wait_recv()
                if slot == 0:
                    sent.append(ici(w, 2, out, k, second, pc))
                    sent[-1].start()
                sent.append(d2d(w, slot, out, k, c, pc))
                sent[-1].start()
        for slot, k in ((0, second_chip), (1, first_chip), (2, diag_chip)):
            for w, out, pc in parts:
                d2d(w, slot, out, k, 1 - c, pc).wait_recv()
        for cp in sent:
            cp.wait_send()

    vmem = pl.BlockSpec(memory_space=pltpu.VMEM)
    outs = pl.pallas_call(
        body,
        out_shape=[jax.ShapeDtypeStruct((N_CHIPS,) + w.shape, BF16) for w in weights]
        + [jax.ShapeDtypeStruct(w.shape, BF16) for w in late]
        + [jax.ShapeDtypeStruct((N_CHIPS,) + w.shape, BF16) for w in late],
        in_specs=[vmem] * (nw + nl), out_specs=[vmem] * (nw + 2 * nl),
        scratch_shapes=[pltpu.SemaphoreType.DMA((3 * nw * pieces,))] * 4,
        compiler_params=pltpu.CompilerParams(vmem_limit_bytes=VMEM_LIMIT), name="ag_weights")(*weights, *late)
    return outs[:nw], outs[nw:nw + nl], outs[nw + nl:]


_HBM = pl.BlockSpec(memory_space=pltpu.HBM)
_SEM = pl.BlockSpec(memory_space=pltpu.SEMAPHORE)
_ANY = pl.BlockSpec(memory_space=pl.ANY)
_DATAFLOW = pltpu.SideEffectType.DATAFLOW_SIDE_EFFECTING


def _in_hbm(a):
    return pltpu.with_memory_space_constraint(a, pltpu.HBM)


def _exchange_copies(gather, srcs, lands, send_sems, recv_sems):
    nw = len(srcs)
    x, y, c, chip, peers, peer_chip = _place()
    pairs = []
    for m, (px, py) in enumerate(peers):
        for w in range(nw):
            sems = (send_sems.at[nw * m + w], recv_sems.at[nw * m + w], (px, py, c))
            if gather:
                pairs.append((_remote(srcs[w], lands[w].at[chip], *sems),
                              _remote(srcs[w], lands[w].at[peer_chip[m]], *sems)))
            else:
                pairs.append((_remote(srcs[w].at[m], lands[w].at[m], *sems),) * 2)
    return pairs


def _exchange_start(gather, srcs, after, name, lands=None):
    nw = len(srcs)
    n_copies = 3 * nw

    def body(*refs):
        send_sems, recv_sems = refs[2 * nw + 1], refs[2 * nw + 2]
        for start, _ in _exchange_copies(gather, refs[:nw], refs[nw:2 * nw], send_sems, recv_sems):
            start.start()
        refs[-1][...] = jnp.zeros_like(refs[-1])

    if lands is None:
        lands = [lax.empty(((N_CHIPS,) + s.shape) if gather else s.shape, s.dtype) for s in srcs]
    lands = [_in_hbm(l) for l in lands]
    return pl.pallas_call(
        body, name=name,
        out_shape=(pltpu.SemaphoreType.DMA((n_copies,)), pltpu.SemaphoreType.DMA((n_copies,)))
        + tuple(pltpu.HBM(s.shape, s.dtype) for s in srcs)
        + tuple(pltpu.HBM(l.shape, l.dtype) for l in lands)
        + (jax.ShapeDtypeStruct((8, LANES), F32),),
        in_specs=[_HBM] * (2 * nw) + [_ANY],
        out_specs=(_SEM, _SEM) + (_HBM,) * (2 * nw) + (pl.BlockSpec(memory_space=pltpu.VMEM),),
        input_output_aliases={i: 2 + i for i in range(2 * nw)},
        compiler_params=pltpu.CompilerParams(has_side_effects=_DATAFLOW),
    )(*[_in_hbm(s) for s in srcs], *lands, after)


def _exchange_wait(gather, started, after, name):
    nw = (len(started) - 3) // 2
    send_sems, recv_sems = started[0], started[1]
    thru = started[2:2 + 2 * nw]

    def body(*refs):
        for _, arrival in _exchange_copies(gather, refs[:nw], refs[nw:2 * nw], refs[2 * nw], refs[2 * nw + 1]):
            arrival.wait_send()
            arrival.wait_recv()

    outs = pl.pallas_call(
        body, name=name,
        out_shape=tuple(pltpu.HBM(t.shape, t.dtype) for t in thru),
        in_specs=[_HBM] * (2 * nw) + [_SEM, _SEM, _ANY], out_specs=(_HBM,) * (2 * nw),
        input_output_aliases={i: i for i in range(2 * nw)},
        compiler_params=pltpu.CompilerParams(has_side_effects=_DATAFLOW),
    )(*thru, send_sems, recv_sems, after)
    return outs[nw:]


def _reduce_last(owns, landed, g_small):
    ns = len(owns)
    row_block = 32
    hs = SMALL_ROWS // 2

    def body(*refs):
        own_refs, land_refs, gsm_ref = refs[:ns], refs[ns:2 * ns], refs[2 * ns]
        out_refs, osm_ref = refs[2 * ns + 1:3 * ns + 1], refs[3 * ns + 1]
        ra_sm, p_sm, s_sem, r_sem, sm_s, sm_r = refs[3 * ns + 2:]
        x, y, c, chip, peers, peer_chip = _place()
        sib = (x, y, 1 - c)
        half = lambda cc: pl.ds(pl.multiple_of(cc * hs, 8), hs)
        sm_a = _remote(gsm_ref.at[half(1 - c), :], ra_sm, sm_s.at[0], sm_r.at[0], sib)
        sm_a.start()
        swaps = [sm_a]
        for w in range(ns):
            n = own_refs[w].shape[0]

            def total(i, carry, w=w, n=n):
                r0 = pl.multiple_of(i * row_block, row_block)
                blk = pl.ds(r0, row_block)
                acc = own_refs[w][blk, :]
                for m in range(3):
                    acc = acc + land_refs[w][m, blk, :].astype(F32)
                out_refs[w][pl.ds(pl.multiple_of(c * n + r0, 8), row_block), :] = acc
                return carry
            lax.fori_loop(0, n // row_block, total, 0)
            mine = out_refs[w].at[pl.ds(pl.multiple_of(c * n, 8), n), :]
            swaps.append(_remote(mine, mine, s_sem.at[w], r_sem.at[w], sib))
            swaps[-1].start()
        sm_a.wait_recv()
        p_sm[chip] = gsm_ref[half(c), :] + ra_sm[...]
        for m, (px, py) in enumerate(peers):
            swaps.append(_remote(p_sm.at[chip], p_sm.at[chip], sm_s.at[1 + m], sm_r.at[1 + m], (px, py, c)))
            swaps[-1].start()
        for m, (px, py) in enumerate(peers):
            _remote(p_sm.at[chip], p_sm.at[peer_chip[m]], sm_s.at[1 + m], sm_r.at[1 + m], (px, py, c)).wait_recv()
        osm_ref[half(c), :] = (p_sm[0] + p_sm[1]) + (p_sm[2] + p_sm[3])
        swaps.append(_remote(osm_ref.at[half(c), :], osm_ref.at[half(c), :], sm_s.at[4], sm_r.at[4], sib))
        swaps[-1].start()
        for w in range(ns):
            n = own_refs[w].shape[0]
            theirs = out_refs[w].at[pl.ds(pl.multiple_of((1 - c) * n, 8), n), :]
            _remote(theirs, theirs, s_sem.at[w], r_sem.at[w], sib).wait_recv()
        _remote(osm_ref.at[half(1 - c), :], osm_ref.at[half(1 - c), :], sm_s.at[4], sm_r.at[4], sib).wait_recv()
        for cp in swaps:
            cp.wait_send()

    vmem = pl.BlockSpec(memory_space=pltpu.VMEM)
    return pl.pallas_call(
        body, out_shape=[jax.ShapeDtypeStruct((2 * o.shape[0], o.shape[1]), F32) for o in owns]
        + [jax.ShapeDtypeStruct((SMALL_ROWS, LANES), F32)],
        in_specs=[vmem] * (2 * ns + 1), out_specs=[vmem] * (ns + 1),
        scratch_shapes=[pltpu.VMEM((hs, LANES), F32), pltpu.VMEM((N_CHIPS, hs, LANES), F32),
                        pltpu.SemaphoreType.DMA((ns,)), pltpu.SemaphoreType.DMA((ns,)),
                        pltpu.SemaphoreType.DMA((5,)), pltpu.SemaphoreType.DMA((5,))],
        compiler_params=pltpu.CompilerParams(vmem_limit_bytes=VMEM_LIMIT),
        name="reduce_last")(*owns, *landed, g_small)


_SMALL_PARTS = (("g_norm", 8, 8), ("w_s", 512, 512), ("b_s", 4, 8), ("g_v", 2, 8), ("g_mem", 8, 8),
                ("g_final", 8, 8), ("loss", 8, 8))
_LOSS_ROW = SMALL_ROWS - 8
assert sum(p for _, _, p in _SMALL_PARTS) == SMALL_ROWS


def _pack_small(parts, loss_block):
    rows = []
    for (name, used, padded), p in zip(_SMALL_PARTS, list(parts) + [loss_block]):
        p = p.reshape(used, LANES)
        if padded > used:
            p = jnp.pad(p, ((0, padded - used), (0, 0)))
        rows.append(p)
    return jnp.concatenate(rows, axis=0)


def _local_step(x, mem, target, g_norm, w_in, w_s, b_s, g_v, g_mem, late_weights, g_final,
                fwd_token=None, on_dw=None):
    B, S, _ = x.shape
    x2d = x.reshape(B * S, D_MODEL)
    t2d = target.reshape(B * S, D_MODEL)
    mem2d = mem.reshape(B * N_MEM, D_MODEL)

    proj = _inproj_fwd(x2d, g_norm, w_in, after=() if fwd_token is None else (fwd_token,))
    w_kv, w_out = late_weights(proj)
    kv = _kv_fwd(mem2d, g_mem, w_kv)
    a, lse = _attn_fwd(proj, B, S)
    w_sT = jnp.swapaxes(w_s, 1, 2)
    b_tab = jnp.repeat(b_s.T, HEAD_DIM, axis=1)
    (dx2, da, drest, loss, d_wout, d_ws, d_bs, d_gv, d_gf, dkv) = _mid(
        x2d, t2d, a, proj, kv, w_s, w_sT, b_tab, g_v, w_out, g_final, B, S)
    d_wkv, d_gmem = _kv_bwd(mem2d, g_mem, w_kv, dkv)
    dq, dk, dv = _attn_bwd(proj, a, lse, da, B, S)
    if on_dw is None:
        d_win = _inproj_bwd_dw(dq, dk, dv, drest, x2d, g_norm)
        after = ()
    else:
        d_win = None
        by_chip = lambda g: g.reshape((N_CHIPS, g.shape[0] // N_CHIPS, g.shape[1]))
        after = (on_dw(*_inproj_bwd_dw(dq, dk, dv, drest, x2d, g_norm, reduce_with=[by_chip(d_wkv), by_chip(d_wout)])),)
    grad_x, d_gnorm = _inproj_bwd_dx(dq, dk, dv, drest, x2d, dx2, g_norm, w_in, after=after)
    d_bs = d_bs[:, :N_SGU_GROUPS].T
    return (loss, grad_x.reshape(B, S, D_MODEL),
            dict(g_norm=d_gnorm, w_in=d_win, w_s=d_ws, b_s=d_bs, g_v=d_gv, g_mem=d_gmem, w_kv=d_wkv,
                 w_out=d_wout, g_final=d_gf))


def kernel(x, mem, g_norm, w_in, w_sgu_spatial, b_sgu_spatial, g_sgu_v, g_mem, w_mem_kv, w_out, g_final, loss_target, m_g_norm, m_w_in, m_w_sgu_spatial, m_b_sgu_spatial, m_g_sgu_v, m_g_mem, m_w_mem_kv, m_w_out, m_g_final, v_g_norm, v_w_in, v_w_sgu_spatial, v_b_sgu_spatial, v_g_sgu_v, v_g_mem, v_w_mem_kv, v_w_out, v_g_final):
    t = lambda w: jnp.swapaxes(w[0], 0, 1)
    (win_all,), late_shards, late_lands = _ag_weights([t(w_in)], [w_mem_kv[0], w_out[0]])
    w_in_full = win_all.reshape(-1, win_all.shape[-1])
    late = _exchange_start(True, list(late_shards), win_all, "gather_late_start", lands=late_lands)

    def late_weights(proj):
        return [z.reshape(-1, z.shape[-1]) for z in _exchange_wait(True, late, proj, "gather_late_wait")]

    scatter = {}

    def on_dw(sends, owns):
        scatter["own"] = owns
        scatter["started"] = _exchange_start(False, list(sends), owns[0], "scatter_start")
        return scatter["started"][-1]

    loss, grad_x, g = _local_step(
        x, mem, loss_target, g_norm, w_in_full, w_sgu_spatial[0], b_sgu_spatial[0], g_sgu_v, g_mem,
        late_weights, g_final.reshape(1, D_MODEL), fwd_token=late[-1], on_dw=on_dw)

    small_names = ("g_norm", "w_s", "b_s", "g_v", "g_mem", "g_final")
    g_small = _pack_small([g[n] for n in small_names], loss)
    landed = _exchange_wait(False, scatter["started"], g_small, "scatter_wait")
    gr_in, gr_kv, gr_out, gr_small = _reduce_last(scatter["own"], landed, g_small)

    small_w = (g_norm, w_sgu_spatial, b_sgu_spatial, g_sgu_v, g_mem, g_final)
    small_m = (m_g_norm, m_w_sgu_spatial, m_b_sgu_spatial, m_g_sgu_v, m_g_mem, m_g_final)
    small_v = (v_g_norm, v_w_sgu_spatial, v_b_sgu_spatial, v_g_sgu_v, v_g_mem, v_g_final)
    rows = lambda ws: [w.reshape(-1, LANES) for w in ws]
    small_new, loss = _adamw_small(gr_small, rows(small_w), rows(small_m), rows(small_v))
    loss = loss.reshape(())
    small = [[z.reshape(w.shape) for z in four] for w, four in zip(small_w, small_new)]
    gr_in, d_in, nm_in, nv_in = [jnp.swapaxes(z, 0, 1)
                                 for z in _adamw(t(w_in), gr_in, t(m_w_in), t(v_w_in), "adamw_w_in")]
    gr_kv, d_kv, nm_kv, nv_kv = _adamw(w_mem_kv[0], gr_kv, m_w_mem_kv[0], v_w_mem_kv[0], "adamw_w_kv")
    gr_out, d_out, nm_out, nv_out = _adamw(w_out[0], gr_out, m_w_out[0], v_w_out[0], "adamw_w_out")

    def leaves(kind, big_in, big_kv, big_out):
        s_norm, s_ws, s_bs, s_gv, s_gmem, s_gf = [four[kind] for four in small]
        return [s_norm, big_in[None], s_ws, s_bs, s_gv, s_gmem, big_kv[None], big_out[None], s_gf]

    return (loss, grad_x, *leaves(0, gr_in, gr_kv, gr_out), *leaves(1, d_in, d_kv, d_out),
            *leaves(2, nm_in, nm_kv, nm_out), *leaves(3, nv_in, nv_kv, nv_out))
```

```python
import functools

import jax
import jax.numpy as jnp
from jax import lax
from jax.experimental import pallas as pl
from jax.experimental.pallas import tpu as pltpu

F32 = jnp.float32
BF16 = jnp.bfloat16
MESH = pl.DeviceIdType.MESH

D_MODEL = 1024
ATTN_WIDTH = 512
SGU_WIDTH = 256
MEM_WIDTH = 256
N_MEM = 256
IN_COLS = 3328
QKV_COLS = 3 * ATTN_WIDTH
REST_COLS = IN_COLS - QKV_COLS
SGU_CHUNK = 128
N_SGU_GROUPS = 4
EPS = 1e-6
NEG_INF = -1e30
DILATIONS = (1, 4, 16)
RADIUS = 64
Q_BLOCK = 128
LANES = 128
HEAD_DIM = 64

ADAM_LR = 0.001
ADAM_B1 = 0.9
ADAM_B2 = 0.999
ADAM_EPS = 1e-08
ADAM_WD = 0.01
ADAM_STEP = 10

N_CHIPS = 4
VMEM_LIMIT = 56 * 1024 * 1024
SMALL_ROWS = 560


def _params(sem=None, vmem=VMEM_LIMIT):
    return pltpu.CompilerParams(dimension_semantics=sem, vmem_limit_bytes=vmem)


def _nn(a, b):
    return jnp.dot(a, b, preferred_element_type=F32)


def _nt(a, b):
    return lax.dot_general(a, b, (((1,), (1,)), ((), ())), preferred_element_type=F32)


def _tn(a, b):
    return lax.dot_general(a, b, (((0,), (0,)), ((), ())), preferred_element_type=F32)


def _rms(x):
    r = lax.rsqrt(jnp.mean(x * x, axis=-1, keepdims=True) + EPS)
    return r, x * r


def _head_masks():
    lane = lax.broadcasted_iota(jnp.int32, (1, LANES), 1)
    lo = lane < HEAD_DIM
    return lo, (lo.astype(F32), (~lo).astype(F32))


def _silu_parts(z):
    s = jax.nn.sigmoid(z)
    return z * s, s * (1.0 + z * (1.0 - s))


def _gelu_parts(x):
    c = 0.7978845608028654
    x2 = x * x
    s = jax.nn.sigmoid((2.0 * c) * (x + 0.044715 * (x * x2)))
    return x * s, s * (1.0 + x * (1.0 - s) * ((2.0 * c) * (1.0 + 3.0 * 0.044715 * x2)))


def _after(tokens):
    return [pl.BlockSpec(memory_space=pl.ANY)] * len(tokens)


def _inproj_fwd(x2d, g_norm, w_in_t, after=()):
    T = x2d.shape[0]
    tm = 512

    def body(x_ref, g_ref, w_ref, *rest):
        o_ref = rest[-1]
        _, xh = _rms(x_ref[...])
        h = (xh * g_ref[...]).astype(BF16)
        o_ref[...] = _nt(h, w_ref[...])

    return pl.pallas_call(
        body, grid=(T // tm,),
        in_specs=[pl.BlockSpec((tm, D_MODEL), lambda i: (i, 0)),
                  pl.BlockSpec((1, D_MODEL), lambda i: (0, 0)),
                  pl.BlockSpec((IN_COLS, D_MODEL), lambda i: (0, 0))] + _after(after),
        out_specs=pl.BlockSpec((tm, IN_COLS), lambda i: (i, 0)),
        out_shape=jax.ShapeDtypeStruct((T, IN_COLS), F32),
        compiler_params=_params(("arbitrary",)), name="inproj_fwd")(x2d, g_norm, w_in_t, *after)


def _kv_fwd(mem2d, g_mem, w_kv):
    Tm = mem2d.shape[0]

    def body(m_ref, g_ref, w_ref, o_ref):
        _, mh = _rms(m_ref[...])
        o_ref[...] = _nn((mh * g_ref[...]).astype(BF16), w_ref[...])

    return pl.pallas_call(
        body, out_shape=jax.ShapeDtypeStruct((Tm, 2 * MEM_WIDTH), F32),
        compiler_params=_params(), name="kv_fwd")(mem2d, g_mem, w_kv)


def _kv_bwd(mem2d, g_mem, w_kv, dkv):
    Tm = mem2d.shape[0]

    def body(m_ref, g_ref, w_ref, dkv_ref, dw_ref, dg_ref):
        _, mh = _rms(m_ref[...])
        memn = (mh * g_ref[...]).astype(BF16)
        dkvb = dkv_ref[...].astype(BF16)
        dw_ref[...] = _tn(memn, dkvb)
        dmemn = _nt(dkvb, w_ref[...])
        dg_ref[...] = jnp.sum(dmemn * mh, axis=0, keepdims=True)

    return pl.pallas_call(
        body, out_shape=(jax.ShapeDtypeStruct((D_MODEL, 2 * MEM_WIDTH), F32),
                         jax.ShapeDtypeStruct((1, D_MODEL), F32)),
        compiler_params=_params(), name="kv_bwd")(mem2d, g_mem, w_kv, dkv)


def _attn_geometry(S):
    geom = []
    for d in DILATIONS:
        L = S // d
        assert L % Q_BLOCK == 0
        geom.append((d, L, min(2 * Q_BLOCK, L), L // Q_BLOCK))
    return geom


def _init_bias(bias_scr, geom, hp):
    row = lax.broadcasted_iota(jnp.int32, (Q_BLOCK, 2 * Q_BLOCK), 0)
    col = lax.broadcasted_iota(jnp.int32, (Q_BLOCK, 2 * Q_BLOCK), 1)
    for j in (0, 1):
        bits = (126 - (2 * hp + j)) * (1 << 23)
        slope = lax.bitcast_convert_type(jnp.full((1, 1), bits, jnp.int32), F32)
        for di, (d, _, _, _) in enumerate(geom):
            for cls, off in enumerate((0, -RADIUS, -2 * RADIUS)):
                dist = jnp.abs(col - row + off)
                bias_scr[di * 6 + cls * 2 + j] = jnp.where(
                    dist <= RADIUS, -(slope * float(d)) * dist.astype(F32), NEG_INF)


SPLIT = 4
COPY_ROWS = 256


def _by4_rows(S, step):
    per_class = S // SPLIT // COPY_ROWS
    r, j = step // per_class, step % per_class
    return (pl.ds(r + SPLIT * j * COPY_ROWS, COPY_ROWS, stride=SPLIT),
            pl.ds(r * (S // SPLIT) + j * COPY_ROWS, COPY_ROWS))


def _to_by4(src, dst, S):
    for i in range(S // COPY_ROWS):
        natural, by4 = _by4_rows(S, i)
        dst[by4, :] = src[natural, :]


def _block_slices(d, L, KW, nqb, r, qb, S):
    qs = qb * Q_BLOCK
    ks = jnp.clip(qs - RADIUS, 0, L - KW)
    cls = jnp.where(qb == 0, 0, jnp.where(qb == nqb - 1, 2, 1))
    if d == 1:
        qsl = pl.ds(pl.multiple_of(qs, Q_BLOCK), Q_BLOCK)
        ksl = pl.ds(pl.multiple_of(ks, RADIUS), KW)
    elif d == SPLIT:
        qsl = pl.ds(pl.multiple_of(r * L + qs, Q_BLOCK), Q_BLOCK)
        ksl = pl.ds(pl.multiple_of(r * L + ks, RADIUS), KW)
    else:
        sub = d // SPLIT
        base = (r % SPLIT) * (S // SPLIT) + r // SPLIT
        qsl = pl.ds(base + qs * sub, Q_BLOCK, stride=sub)
        ksl = pl.ds(base + ks * sub, KW, stride=sub)
    return qsl, ksl, cls


def _for_groups(geom, S, group, fn):
    for di, (d, L, KW, nqb) in enumerate(geom):
        n = group[di]
        assert (d * nqb) % n == 0

        def step(it, carry, di=di, d=d, L=L, KW=KW, nqb=nqb, n=n):
            slices = []
            for g in range(n):
                i = it * n + g
                slices.append(_block_slices(d, L, KW, nqb, i // nqb, i % nqb, S))
            fn(di, KW, slices)
            return carry
        lax.fori_loop(0, d * nqb // n, step, 0)


def _attn_fwd(proj, B, S):
    T = B * S
    geom = _attn_geometry(S)
    n_pairs = ATTN_WIDTH // LANES

    def body(q_ref, k_ref, v_ref, a_ref, lse_ref, bias_scr, q4, k4, v4, *per_dilation):
        o_scr, m_scr, l_scr = per_dilation[0:3], per_dilation[3:6], per_dilation[6:9]
        lo, hm = _head_masks()
        pair = pl.program_id(0)

        @pl.when(pl.program_id(1) == 0)
        def _():
            _init_bias(bias_scr, geom, pair)
        for src, dst in ((q_ref, q4), (k_ref, k4), (v_ref, v4)):
            _to_by4(src, dst, S)

        def group(di, KW, all_slices):
            run = 8
            for first in range(0, len(all_slices), run):
                some(di, KW, all_slices[first:first + run])

        def some(di, KW, slices):
            chains = [(g, j) for g in range(len(slices)) for j in (0, 1)]
            q_src, k_src, v_src = (q_ref, k_ref, v_ref) if di == 0 else (q4, k4, v4)
            q = [q_src[qsl, :] for qsl, _, _ in slices]
            kw = [k_src[ksl, :].astype(BF16) for _, ksl, _ in slices]
            vw = [v_src[ksl, :].astype(BF16) for _, ksl, _ in slices]
            s = {(g, j): _nt((q[g] * (hm[j] * 0.125)).astype(BF16), kw[g])
                 + bias_scr[di * 6 + slices[g][2] * 2 + j, :, pl.ds(0, KW)] for g, j in chains}
            m = {c: jnp.max(s[c], axis=1, keepdims=True) for c in chains}
            p = {c: jnp.exp(s[c] - m[c]) for c in chains}
            l = {c: jnp.sum(p[c], axis=1, keepdims=True) for c in chains}
            o = {(g, j): _nn(p[(g, j)].astype(BF16), vw[g]) for g, j in chains}
            for g, (qsl, _, _) in enumerate(slices):
                o_scr[di][qsl, :] = jnp.where(lo, o[(g, 0)], o[(g, 1)])
                m_scr[di][qsl, :] = jnp.where(lo, m[(g, 0)], m[(g, 1)])
                l_scr[di][qsl, :] = jnp.where(lo, l[(g, 0)], l[(g, 1)])

        _for_groups(geom, S, (16, 16, 16), group)

        for i in range(S // COPY_ROWS):
            natural, by4 = _by4_rows(S, i)
            rows = [natural, by4, by4]
            ms = [m_scr[di][rows[di], :] for di in range(3)]
            mx = jnp.maximum(jnp.maximum(ms[0], ms[1]), ms[2])
            num = 0.0
            den = 0.0
            for di in range(3):
                w = jnp.exp(ms[di] - mx)
                num = num + w * o_scr[di][rows[di], :]
                den = den + w * l_scr[di][rows[di], :]
            a_ref[natural, :] = num / den
            lse_ref[natural, :] = mx + jnp.log(den)

    blk = lambda off: pl.BlockSpec((S, LANES), lambda h, b, off=off: (b, off + h))
    out_blk = pl.BlockSpec((S, LANES), lambda h, b: (b, h))
    return pl.pallas_call(
        body, grid=(n_pairs, B),
        in_specs=[blk(0), blk(n_pairs), blk(2 * n_pairs)],
        out_specs=[out_blk, out_blk],
        out_shape=[jax.ShapeDtypeStruct((T, ATTN_WIDTH), F32)] * 2,
        scratch_shapes=[pltpu.VMEM((18, Q_BLOCK, 2 * Q_BLOCK), F32)] + [pltpu.VMEM((S, LANES), F32)] * 12,
        compiler_params=_params(("arbitrary", "arbitrary")), name="attn_fwd")(proj, proj, proj)


def _attn_bwd(proj, a, lse, da, B, S):
    T = B * S
    geom = _attn_geometry(S)
    n_pairs = ATTN_WIDTH // LANES

    def body(q_ref, k_ref, v_ref, a_ref, lse_ref, do_ref, dq_ref, dk_ref, dv_ref, bias_scr, *scr):
        acc = (scr[0:3], scr[3:6])
        natural_in = (q_ref, k_ref, v_ref, a_ref, lse_ref, do_ref)
        by4_in = scr[6:12]
        _, hm = _head_masks()
        pair = pl.program_id(0)

        @pl.when(pl.program_id(1) == 0)
        def _():
            _init_bias(bias_scr, geom, pair)
        for ref in scr[0:6]:
            ref[...] = jnp.zeros_like(ref)
        for src, dst in zip(natural_in, by4_in):
            _to_by4(src, dst, S)

        def group(di, KW, all_slices):
            run = (4, 4, 8)[di]
            for first in range(0, len(all_slices), run):
                some(di, KW, all_slices[first:first + run])

        def some(di, KW, slices):
            n = len(slices)
            chains = [(g, j) for g in range(n) for j in (0, 1)]
            q_src, k_src, v_src, a_src, lse_src, do_src = natural_in if di == 0 else by4_in
            dq_scr, dk_scr, dv_scr = acc[0 if di == 0 else 1]
            q = [q_src[qsl, :] for qsl, _, _ in slices]
            do = [do_src[qsl, :] for qsl, _, _ in slices]
            doa = [do[g] * a_src[slices[g][0], :] for g in range(n)]
            lse_q = [lse_src[qsl, :] for qsl, _, _ in slices]
            kw = [k_src[ksl, :].astype(BF16) for _, ksl, _ in slices]
            vw = [v_src[ksl, :].astype(BF16) for _, ksl, _ in slices]
            qj = {(g, j): (q[g] * (hm[j] * 0.125)).astype(BF16) for g, j in chains}
            doj = {(g, j): (do[g] * hm[j]).astype(BF16) for g, j in chains}
            s = {(g, j): _nt(qj[(g, j)], kw[g])
                 + bias_scr[di * 6 + slices[g][2] * 2 + j, :, pl.ds(0, KW)] for g, j in chains}
            dp = {(g, j): _nt(doj[(g, j)], vw[g]) for g, j in chains}
            dsum = {(g, j): jnp.sum(doa[g] * hm[j], axis=1, keepdims=True) for g, j in chains}
            p = {(g, j): jnp.exp(s[(g, j)] - lse_q[g][:, HEAD_DIM * j:HEAD_DIM * j + 1]) for g, j in chains}
            ds = {c: (p[c] * (dp[c] - dsum[c])).astype(BF16) for c in chains}
            pb = {c: p[c].astype(BF16) for c in chains}
            dq = [_nn(ds[(g, 0)], kw[g]) * (hm[0] * 0.125) + _nn(ds[(g, 1)], kw[g]) * (hm[1] * 0.125)
                  for g in range(n)]
            both = lambda t, g: jnp.concatenate([t[(g, 0)], t[(g, 1)]], axis=0)
            dkw = [_tn(both(ds, g), both(qj, g)) for g in range(n)]
            dvw = [_tn(both(pb, g), both(doj, g)) for g in range(n)]
            for g, (qsl, ksl, _) in enumerate(slices):
                dq_scr[qsl, :] = dq_scr[qsl, :] + dq[g]
                dk_scr[ksl, :] = dk_scr[ksl, :] + dkw[g]
                dv_scr[ksl, :] = dv_scr[ksl, :] + dvw[g]

        _for_groups(geom, S, (16, 16, 16), group)

        for i in range(S // COPY_ROWS):
            natural, by4 = _by4_rows(S, i)
            for nat, split in zip(*acc):
                nat[natural, :] = nat[natural, :] + split[by4, :]
        for out, nat in zip((dq_ref, dk_ref, dv_ref), acc[0]):
            out[...] = nat[...].astype(BF16)

    blk = lambda off: pl.BlockSpec((S, LANES), lambda h, b, off=off: (b, off + h))
    return pl.pallas_call(
        body, grid=(n_pairs, B),
        in_specs=[blk(0), blk(n_pairs), blk(2 * n_pairs), blk(0), blk(0), blk(0)],
        out_specs=[blk(0), blk(0), blk(0)],
        out_shape=[jax.ShapeDtypeStruct((T, ATTN_WIDTH), BF16)] * 3,
        scratch_shapes=[pltpu.VMEM((18, Q_BLOCK, 2 * Q_BLOCK), F32)] + [pltpu.VMEM((S, LANES), F32)] * 12,
        compiler_params=_params(("arbitrary", "arbitrary")), name="attn_bwd")(proj, proj, proj, a, lse, da)


def _mid(x2d, t2d, a, proj, kv, w_s, w_sT, b_tab, g_v, w_out, g_final, B, S):
    T = B * S
    tm = 512
    nt = S // tm
    halves = 2
    hrows = tm // halves

    def body(x_ref, t_ref, a_ref, za_ref, ub_ref, vb_ref, zb_ref, qm_ref, zm_ref, kv_ref,
              ws_ref, wsT_ref, btab_ref, gv_ref, wout_ref, gf_ref,
              dx2_ref, da_ref, drest_ref, loss_ref, dwout_ref, dws_ref, dbs_ref, dgv_ref, dgf_ref, dkv_ref,
              dbtab_scr):
        b = pl.program_id(0)
        t = pl.program_id(1)
        first = jnp.logical_and(b == 0, t == 0)
        last = jnp.logical_and(b == B - 1, t == nt - 1)
        _, hm = _head_masks()
        lane_g = lax.broadcasted_iota(jnp.int32, (1, SGU_WIDTH), 1) // HEAD_DIM
        gm = [(lane_g == g).astype(F32) for g in range(N_SGU_GROUPS)]
        H = range(halves)
        rows = [pl.ds(h * hrows, hrows) for h in H]
        ld = lambda ref: [ref[r, :] for r in rows]
        cat = lambda parts, axis: jnp.concatenate(parts, axis=axis)
        chunks = [slice(ci * SGU_CHUNK, (ci + 1) * SGU_CHUNK) for ci in range(hrows // SGU_CHUNK)]
        pairs = [slice(pr * LANES, (pr + 1) * LANES) for pr in range(2)]
        heads = [(pr, j) for pr in range(2) for j in (0, 1)]

        @pl.when(first)
        def _():
            loss_ref[...] = jnp.zeros_like(loss_ref)
            dwout_ref[...] = jnp.zeros_like(dwout_ref)
            dws_ref[...] = jnp.zeros_like(dws_ref)
            dbs_ref[...] = jnp.zeros_like(dbs_ref)
            dgv_ref[...] = jnp.zeros_like(dgv_ref)
            dgf_ref[...] = jnp.zeros_like(dgf_ref)
            dbtab_scr[...] = jnp.zeros_like(dbtab_scr)

        @pl.when(t == 0)
        def _():
            dkv_ref[...] = jnp.zeros_like(dkv_ref)

        a_val = ld(a_ref)
        sil_a = [_silu_parts(z) for z in ld(za_ref)]
        gated_a = [s[0] * a for s, a in zip(sil_a, a_val)]
        u = [_gelu_parts(z) for z in ld(ub_ref)]
        vv = [_gelu_parts(z) for z in ld(vb_ref)]
        vnorm = [_rms(v[0]) for v in vv]
        gv = gv_ref[...]
        vn = [(n[1] * gv).astype(BF16) for n in vnorm]
        w_cat = cat([ws_ref[g].astype(BF16) for g in range(N_SGU_GROUPS)], 1)
        wT_cat = cat([wsT_ref[g].astype(BF16) for g in range(N_SGU_GROUPS)], 1)
        gmb = [m.astype(BF16) for m in gm]
        by_group = lambda chunk: cat([chunk * gmb[g] for g in range(N_SGU_GROUPS)], 0)
        btab = btab_ref[...]
        mixed = [cat([btab + _nn(w_cat, by_group(vn[h][c, :])) for c in chunks], 0) for h in H]
        sg = [u[h][0] * mixed[h] for h in H]
        sil_b = [_silu_parts(z) for z in ld(zb_ref)]
        gated_b = [sil_b[h][0] * sg[h] for h in H]

        kvv = kv_ref[...].astype(BF16)
        kp = [kvv[:, p] for p in pairs]
        vp = [kvv[:, MEM_WIDTH + pr * LANES:MEM_WIDTH + (pr + 1) * LANES] for pr in range(2)]
        qm = ld(qm_ref)
        qj = {(h, pr, j): (qm[h][:, pairs[pr]] * (hm[j] * 0.125)).astype(BF16) for h in H for pr, j in heads}
        sc = {k: _nt(qj[k], kp[k[1]]) for k in qj}
        ex = {k: jnp.exp(sc[k] - jnp.max(sc[k], axis=1, keepdims=True)) for k in qj}
        prob = {k: ex[k] * (1.0 / jnp.sum(ex[k], axis=1, keepdims=True)) for k in qj}
        probb = {k: prob[k].astype(BF16) for k in qj}
        mo = [cat([sum(_nn(probb[(h, pr, j)], vp[pr]) * hm[j] for j in (0, 1)) for pr in range(2)], 1) for h in H]
        sil_m = [_silu_parts(z) for z in ld(zm_ref)]
        gated_m = [sil_m[h][0] * mo[h] for h in H]

        gated = [cat([gated_a[h], gated_b[h], gated_m[h]], 1).astype(BF16) for h in H]
        wout = wout_ref[...]
        x_in = ld(x_ref)
        x2 = [x_in[h] + _nn(gated[h], wout) for h in H]
        fin = [_rms(z) for z in x2]
        gf = gf_ref[...]
        tgt = ld(t_ref)
        err = [fin[h][1] * gf - tgt[h] for h in H]
        loss_ref[...] += sum(jnp.sum(e * e) for e in err) * (0.5 / D_MODEL)

        dy = [e * (1.0 / D_MODEL) for e in err]
        dgf_ref[...] += sum(jnp.sum(dy[h] * fin[h][1], axis=0, keepdims=True) for h in H)
        gdy = [d * gf for d in dy]
        dx2 = [fin[h][0] * (gdy[h] - fin[h][1] * jnp.mean(gdy[h] * fin[h][1], axis=1, keepdims=True)) for h in H]
        for h in H:
            dx2_ref[rows[h], :] = dx2[h]
        dx2b = [d.astype(BF16) for d in dx2]
        dgated = [_nt(d, wout) for d in dx2b]
        dwout_ref[...] += _tn(cat(gated, 0), cat(dx2b, 0))
        dga = [d[:, 0:ATTN_WIDTH] for d in dgated]
        dgb = [d[:, ATTN_WIDTH:ATTN_WIDTH + SGU_WIDTH] for d in dgated]
        dgm = [d[:, ATTN_WIDTH + SGU_WIDTH:] for d in dgated]

        for h in H:
            da_ref[rows[h], :] = dga[h] * sil_a[h][0]
        dza = [dga[h] * a_val[h] * sil_a[h][1] for h in H]

        dsg = [dgb[h] * sil_b[h][0] for h in H]
        dzb = [dgb[h] * sg[h] * sil_b[h][1] for h in H]
        dub = [dsg[h] * mixed[h] * u[h][1] for h in H]
        dmixed = [dsg[h] * u[h][0] for h in H]
        dmixed_b = [d.astype(BF16) for d in dmixed]
        dvn = [cat([_nn(wT_cat, by_group(dmixed_b[h][c, :])) for c in chunks], 0) for h in H]
        for g in range(N_SGU_GROUPS):
            dws_ref[g] += sum(_nt((dmixed[h][c, :] * gm[g]).astype(BF16), vn[h][c, :]) for h in H for c in chunks)
        dbtab_scr[...] += sum(dmixed[h][c, :] for h in H for c in chunks)
        dgv_ref[...] += sum(jnp.sum(dvn[h] * vnorm[h][1], axis=0, keepdims=True) for h in H)
        tv = [d * gv for d in dvn]
        dvv = [vnorm[h][0] * (tv[h] - vnorm[h][1] * jnp.mean(tv[h] * vnorm[h][1], axis=1, keepdims=True)) for h in H]
        dvb = [dvv[h] * vv[h][1] for h in H]

        dmo = [dgm[h] * sil_m[h][0] for h in H]
        dzm = [dgm[h] * mo[h] * sil_m[h][1] for h in H]
        dmoj = {(h, pr, j): (dmo[h][:, pairs[pr]] * hm[j]).astype(BF16) for h in H for pr, j in heads}
        dp = {k: _nt(dmoj[k], vp[k[1]]) for k in qj}
        ds = {k: (prob[k] * (dp[k] - jnp.sum(dp[k] * prob[k], axis=1, keepdims=True))).astype(BF16) for k in qj}
        dqm = [cat([sum(_nn(ds[(h, pr, j)], kp[pr]) * (hm[j] * 0.125) for j in (0, 1)) for pr in range(2)], 1)
               for h in H]
        every = lambda tbl, pr: cat([tbl[(h, pr, j)] for h in H for j in (0, 1)], 0)
        dk = [_tn(every(ds, pr), every(qj, pr)) for pr in range(2)]
        dv = [_tn(every(probb, pr), every(dmoj, pr)) for pr in range(2)]
        dkv_ref[...] += cat(dk + dv, 1)

        for h in H:
            drest_ref[rows[h], :] = cat([dza[h], dub[h], dvb[h], dzb[h], dqm[h], dzm[h]], 1).astype(BF16)

        @pl.when(last)
        def _():
            lane = lax.broadcasted_iota(jnp.int32, (1, LANES), 1)
            dbt = dbtab_scr[...]
            out = jnp.zeros((SGU_CHUNK, LANES), F32)
            for g in range(N_SGU_GROUPS):
                out = out + jnp.where(lane == g, jnp.sum(dbt * gm[g], axis=1, keepdims=True), 0.0)
            dbs_ref[...] = out

    tile = lambda w, cb: pl.BlockSpec((tm, w), lambda b, t, cb=cb: (b * nt + t, cb))
    const = lambda shape: pl.BlockSpec(shape, lambda b, t, n=len(shape): (0,) * n)
    return pl.pallas_call(
        body, grid=(B, nt),
        in_specs=[tile(D_MODEL, 0), tile(D_MODEL, 0), tile(ATTN_WIDTH, 0),
                  tile(ATTN_WIDTH, 3),
                  tile(SGU_WIDTH, 8), tile(SGU_WIDTH, 9), tile(SGU_WIDTH, 10),
                  tile(MEM_WIDTH, 11), tile(MEM_WIDTH, 12),
                  pl.BlockSpec((N_MEM, 2 * MEM_WIDTH), lambda b, t: (b, 0)),
                  const((N_SGU_GROUPS, SGU_CHUNK, SGU_CHUNK)), const((N_SGU_GROUPS, SGU_CHUNK, SGU_CHUNK)),
                  const((SGU_CHUNK, SGU_WIDTH)), const((1, SGU_WIDTH)),
                  const((D_MODEL, D_MODEL)), const((1, D_MODEL))],
        out_specs=[tile(D_MODEL, 0), tile(ATTN_WIDTH, 0), tile(REST_COLS, 0),
                   const((8, LANES)), const((D_MODEL, D_MODEL)),
                   const((N_SGU_GROUPS, SGU_CHUNK, SGU_CHUNK)), const((SGU_CHUNK, LANES)),
                   const((1, SGU_WIDTH)), const((1, D_MODEL)),
                   pl.BlockSpec((N_MEM, 2 * MEM_WIDTH), lambda b, t: (b, 0))],
        out_shape=[jax.ShapeDtypeStruct((T, D_MODEL), F32), jax.ShapeDtypeStruct((T, ATTN_WIDTH), F32),
                   jax.ShapeDtypeStruct((T, REST_COLS), BF16),
                   jax.ShapeDtypeStruct((8, LANES), F32), jax.ShapeDtypeStruct((D_MODEL, D_MODEL), F32),
                   jax.ShapeDtypeStruct((N_SGU_GROUPS, SGU_CHUNK, SGU_CHUNK), F32),
                   jax.ShapeDtypeStruct((SGU_CHUNK, LANES), F32),
                   jax.ShapeDtypeStruct((1, SGU_WIDTH), F32), jax.ShapeDtypeStruct((1, D_MODEL), F32),
                   jax.ShapeDtypeStruct((B * N_MEM, 2 * MEM_WIDTH), F32)],
        scratch_shapes=[pltpu.VMEM((SGU_CHUNK, SGU_WIDTH), F32)],
        compiler_params=_params(("arbitrary", "arbitrary")), name="mid")(
            x2d, t2d, a, proj, proj, proj, proj, proj, proj, kv, w_s, w_sT, b_tab, g_v, w_out, g_final)


def _inproj_bwd_dx(dq, dk, dv, drest, x2d, dx2, g_norm, w_in_t, after=()):
    T = x2d.shape[0]
    tm = 512
    W = ATTN_WIDTH

    def body(dq_ref, dk_ref, dv_ref, dr_ref, x_ref, dx2_ref, g_ref, w_ref, *rest):
        gx_ref, dg_ref = rest[-2:]

        @pl.when(pl.program_id(0) == 0)
        def _():
            dg_ref[...] = jnp.zeros_like(dg_ref)

        halves = [pl.ds(h * (tm // 2), tm // 2) for h in (0, 1)]
        dh = [(_nn(dq_ref[r, :], w_ref[0:W, :]) + _nn(dk_ref[r, :], w_ref[W:2 * W, :])
               + _nn(dv_ref[r, :], w_ref[2 * W:3 * W, :]) + _nn(dr_ref[r, :], w_ref[QKV_COLS:IN_COLS, :]))
              for r in halves]
        nrm = [_rms(x_ref[r, :]) for r in halves]
        dg_ref[...] += sum(jnp.sum(d * n[1], axis=0, keepdims=True) for d, n in zip(dh, nrm))
        g = g_ref[...]
        for r, d, (rstd, xh) in zip(halves, dh, nrm):
            th = d * g
            gx_ref[r, :] = rstd * (th - xh * jnp.mean(th * xh, axis=1, keepdims=True)) + dx2_ref[r, :]

    tile = lambda w: pl.BlockSpec((tm, w), lambda i: (i, 0))
    return pl.pallas_call(
        body, grid=(T // tm,),
        in_specs=[tile(W), tile(W), tile(W), tile(REST_COLS), tile(D_MODEL), tile(D_MODEL),
                  pl.BlockSpec((1, D_MODEL), lambda i: (0, 0)),
                  pl.BlockSpec((IN_COLS, D_MODEL), lambda i: (0, 0))] + _after(after),
        out_specs=[tile(D_MODEL), pl.BlockSpec((1, D_MODEL), lambda i: (0, 0))],
        out_shape=[jax.ShapeDtypeStruct((T, D_MODEL), F32), jax.ShapeDtypeStruct((1, D_MODEL), F32)],
        compiler_params=_params(("arbitrary",)), name="inproj_bwd_dx")(
            dq, dk, dv, drest, x2d, dx2, g_norm, w_in_t, *after)


def _inproj_bwd_dw(dq, dk, dv, drest, x2d, g_norm, reduce_with=None):
    T = x2d.shape[0]
    tm = 512
    nt = T // tm
    W = ATTN_WIDTH
    fused = reduce_with is not None
    others = list(reduce_with) if fused else []
    ns = 1 + len(others)
    shard = IN_COLS // N_CHIPS
    halves = [shard // 2] + [s.shape[1] // 2 for s in others]
    cols = [D_MODEL] + [s.shape[2] for s in others]
    row_block = 32

    def body(dq_ref, dk_ref, dv_ref, dr_ref, x_ref, g_ref, *rest):
        if fused:
            stacks = rest[:ns - 1]
            sends, owns = rest[ns - 1:2 * ns - 1], rest[2 * ns - 1:3 * ns - 1]
            acc, ras, narrow = rest[3 * ns - 1], rest[3 * ns:4 * ns], rest[4 * ns]
            s_sem, r_sem = rest[4 * ns + 1], rest[4 * ns + 2]
            x, y, c, chip, peers, peer_chip = _place()
            sib = (x, y, 1 - c)

            def part(w, k, cc, r0=0, rows=None):
                n = halves[w]
                rows = n if rows is None else rows
                if w == 0:
                    return acc.at[pl.ds(pl.multiple_of(k * shard + cc * n + r0, 8), rows), :]
                return stacks[w - 1].at[k, pl.ds(pl.multiple_of(cc * n + r0, 8), rows), :]

            def swap_other(w):
                theirs = stacks[w - 1].at[:, pl.ds(pl.multiple_of((1 - c) * halves[w], 8), halves[w]), :]
                return _remote(theirs, ras[w], s_sem.at[N_CHIPS - 1 + w], r_sem.at[N_CHIPS - 1 + w], sib)

            def swap_win(k):
                return _remote(narrow.at[k], ras[0].at[k], s_sem.at[k], r_sem.at[k], sib)
        else:
            acc = rest[0]

        @pl.when(pl.program_id(0) == 0)
        def _():
            acc[...] = jnp.zeros_like(acc)
            for w in range(1, ns):
                swap_other(w).start()

        _, xh = _rms(x_ref[...])
        h = (xh * g_ref[...]).astype(BF16)
        acc[0:W, :] += _tn(dq_ref[...], h)
        acc[W:2 * W, :] += _tn(dk_ref[...], h)
        acc[2 * W:3 * W, :] += _tn(dv_ref[...], h)
        acc[QKV_COLS:IN_COLS, :] += _tn(dr_ref[...], h)

        if fused:
            @pl.when(pl.program_id(0) == nt - 1)
            def _():
                for k in range(N_CHIPS):
                    def to_bf16(i, carry, k=k):
                        r0 = pl.multiple_of(i * row_block, row_block)
                        narrow[k, pl.ds(r0, row_block), :] = part(0, k, 1 - c, r0, row_block)[...].astype(BF16)
                        return carry
                    lax.fori_loop(0, halves[0] // row_block, to_bf16, 0)
                    swap_win(k).start()
                def chip_sum(w, k, r0):
                    blk = pl.ds(r0, row_block)
                    return part(w, k, c, r0, row_block)[...] + ras[w][k, blk, :].astype(F32)

                for w in range(1, ns):
                    swap_other(w).wait_recv()

                    def sums(i, carry, w=w):
                        r0 = pl.multiple_of(i * row_block, row_block)
                        for m in range(3):
                            sends[w][m, pl.ds(r0, row_block), :] = chip_sum(w, peer_chip[m], r0).astype(BF16)
                        owns[w][pl.ds(r0, row_block), :] = chip_sum(w, chip, r0)
                        return carry
                    lax.fori_loop(0, halves[w] // row_block, sums, 0)
                for k in range(N_CHIPS):
                    swap_win(k).wait_recv()

                    @pl.when(chip == k)
                    def _(k=k):
                        def own(i, carry):
                            r0 = pl.multiple_of(i * row_block, row_block)
                            owns[0][pl.ds(r0, row_block), :] = chip_sum(0, k, r0)
                            return carry
                        lax.fori_loop(0, halves[0] // row_block, own, 0)

                    @pl.when(chip != k)
                    def _(k=k):
                        def other(i, carry):
                            r0 = pl.multiple_of(i * row_block, row_block)
                            sends[0][(k ^ chip) - 1, pl.ds(r0, row_block), :] = chip_sum(0, k, r0).astype(BF16)
                            return carry
                        lax.fori_loop(0, halves[0] // row_block, other, 0)
                for k in range(N_CHIPS):
                    swap_win(k).wait_send()
                for w in range(1, ns):
                    swap_other(w).wait_send()

    tile = lambda w: pl.BlockSpec((tm, w), lambda i: (i, 0))
    vmem = pl.BlockSpec(memory_space=pltpu.VMEM)
    in_specs = [tile(W), tile(W), tile(W), tile(REST_COLS), tile(D_MODEL), pl.BlockSpec((1, D_MODEL), lambda i: (0, 0))]
    if not fused:
        return pl.pallas_call(
            body, grid=(nt,), in_specs=in_specs,
            out_specs=pl.BlockSpec((IN_COLS, D_MODEL), lambda i: (0, 0)),
            out_shape=jax.ShapeDtypeStruct((IN_COLS, D_MODEL), F32),
            compiler_params=_params(("arbitrary",)), name="inproj_bwd_dw")(dq, dk, dv, drest, x2d, g_norm)
    outs = pl.pallas_call(
        body, grid=(nt,), in_specs=in_specs + [vmem] * (ns - 1), out_specs=[vmem] * (2 * ns),
        out_shape=[jax.ShapeDtypeStruct((3, n, cl), BF16) for n, cl in zip(halves, cols)]
        + [jax.ShapeDtypeStruct((n, cl), F32) for n, cl in zip(halves, cols)],
        scratch_shapes=[pltpu.VMEM((IN_COLS, D_MODEL), F32)]
        + [pltpu.VMEM((N_CHIPS, n, cl), BF16 if w == 0 else F32) for w, (n, cl) in enumerate(zip(halves, cols))]
        + [pltpu.VMEM((N_CHIPS, halves[0], D_MODEL), BF16)]
        + [pltpu.SemaphoreType.DMA((N_CHIPS - 1 + ns,)), pltpu.SemaphoreType.DMA((N_CHIPS - 1 + ns,))],
        compiler_params=_params(("arbitrary",)), name="inproj_bwd_dw_reduce")(
            dq, dk, dv, drest, x2d, g_norm, *others)
    return outs[:ns], outs[ns:]


def _adamw_update(w, g, m, v):
    nm = ADAM_B1 * m + (1.0 - ADAM_B1) * g
    nv = ADAM_B2 * v + (1.0 - ADAM_B2) * (g * g)
    m_hat = nm / (1.0 - ADAM_B1 ** ADAM_STEP)
    v_hat = nv / (1.0 - ADAM_B2 ** ADAM_STEP)
    return -ADAM_LR * (m_hat / (jnp.sqrt(v_hat) + ADAM_EPS) + ADAM_WD * w), nm, nv


def _adamw(w, g, m, v, name):
    R, C = w.shape
    br = max(r for r in range(8, 257, 8) if R % r == 0)

    def body(w_ref, g_ref, m_ref, v_ref, g_out, d_ref, nm_ref, nv_ref):
        g = g_ref[...]
        g_out[...] = g
        d_ref[...], nm_ref[...], nv_ref[...] = _adamw_update(w_ref[...], g, m_ref[...], v_ref[...])

    spec = pl.BlockSpec((br, C), lambda i: (i, 0))
    return pl.pallas_call(
        body, grid=(R // br,), in_specs=[spec] * 4, out_specs=[spec] * 4,
        out_shape=[jax.ShapeDtypeStruct((R, C), F32)] * 4,
        compiler_params=_params(("arbitrary",)), name=name)(w, g, m, v)


def _adamw_rest(g_packed, ws, ms, vs, whole):
    n = len(ws)
    nw = len(whole)
    row_block = 64

    def body(*refs):
        g_ref = refs[0]
        w_refs, m_refs, v_refs = refs[1:1 + n], refs[1 + n:1 + 2 * n], refs[1 + 2 * n:1 + 3 * n]
        whole_in = [refs[1 + 3 * n + 4 * i:5 + 3 * n + 4 * i] for i in range(nw)]
        outs = refs[1 + 3 * n + 4 * nw:]
        off = 0
        for i, (_, used, padded) in enumerate(_SMALL_PARTS[:n]):
            g = g_ref[off:off + used, :]
            delta, nm, nv = _adamw_update(w_refs[i][...], g, m_refs[i][...], v_refs[i][...])
            outs[4 * i][...], outs[4 * i + 1][...], outs[4 * i + 2][...], outs[4 * i + 3][...] = g, delta, nm, nv
            off += padded
        for i, (w_ref, gw_ref, m_ref, v_ref) in enumerate(whole_in):
            for r0 in range(0, w_ref.shape[0], row_block):
                blk = pl.ds(r0, row_block)
                g = gw_ref[blk, :]
                new = _adamw_update(w_ref[blk, :], g, m_ref[blk, :], v_ref[blk, :])
                for ref, val in zip(outs[4 * (n + i):4 * (n + i) + 4], (g,) + new):
                    ref[blk, :] = val
        outs[4 * (n + nw)][...] = g_ref[_LOSS_ROW:_LOSS_ROW + 1, 0:1]

    outs = pl.pallas_call(
        body, out_shape=[jax.ShapeDtypeStruct(w.shape, F32) for w in ws for _ in range(4)]
        + [jax.ShapeDtypeStruct(four[0].shape, F32) for four in whole for _ in range(4)]
        + [jax.ShapeDtypeStruct((1, 1), F32)],
        compiler_params=_params(), name="adamw_rest")(g_packed, *ws, *ms, *vs, *[a for four in whole for a in four])
    return ([outs[4 * i:4 * i + 4] for i in range(n)], [outs[4 * (n + i):4 * (n + i) + 4] for i in range(nw)],
            outs[4 * (n + nw)])


def _place():
    x, y, c = lax.axis_index("x"), lax.axis_index("y"), lax.axis_index("c")
    chip = 2 * x + y
    peers = [(x, 1 - y), (1 - x, y), (1 - x, 1 - y)]
    peer_chip = [2 * px + py for px, py in peers]
    return x, y, c, chip, peers, peer_chip


def _remote(src, dst, send_sem, recv_sem, dev):
    return pltpu.make_async_remote_copy(src_ref=src, dst_ref=dst, send_sem=send_sem, recv_sem=recv_sem,
                                        device_id=dev, device_id_type=MESH)


def _ag_weights(weights, late=()):
    nw, nl = len(weights), len(late)
    pieces = 2

    def body(*refs):
        srcs, late_srcs = refs[:nw], refs[nw:nw + nl]
        outs, late_bf, late_land = (refs[nw + nl:2 * nw + nl], refs[2 * nw + nl:2 * nw + 2 * nl],
                                    refs[2 * nw + 2 * nl:2 * nw + 3 * nl])
        s_ici, r_ici, s_d2d, r_d2d = refs[2 * nw + 3 * nl:]
        x, y, c = lax.axis_index("x"), lax.axis_index("y"), lax.axis_index("c")
        chip = 2 * x + y
        sib = (x, y, 1 - c)
        first = ((x + 1 - c) % 2, (y + c) % 2)
        second = ((x + c) % 2, (y + 1 - c) % 2)
        first_chip, second_chip = 2 * first[0] + first[1], 2 * second[0] + second[1]
        diag_chip = 3 - chip
        for src, out in zip(srcs, outs):
            out[chip] = src[...].astype(BF16)

        parts = [(w, out, pc) for w, out in enumerate(outs) for pc in range(pieces)]

        def piece(out, k, cc, pc):
            rows = out.shape[1] // 2 // pieces
            return out.at[k, pl.ds(pl.multiple_of((cc * pieces + pc) * rows, 16), rows), :]

        def ici(w, slot, out, k, dev, pc):
            blk, sem = piece(out, k, c, pc), (nw * slot + w) * pieces + pc
            return _remote(blk, blk, s_ici.at[sem], r_ici.at[sem], (dev[0], dev[1], c))

        def d2d(w, slot, out, k, cc, pc):
            blk, sem = piece(out, k, cc, pc), (nw * slot + w) * pieces + pc
            return _remote(blk, blk, s_d2d.at[sem], r_d2d.at[sem], sib)

        sent = []
        for slot, dev in enumerate((first, second)):
            for w, out, pc in parts:
                sent.append(ici(w, slot, out, chip, dev, pc))
                sent[-1].start()
        for src, bf, land in zip(late_srcs, late_bf, late_land):
            bf[...] = src[...].astype(BF16)
            land[...] = jnp.zeros_like(land)
            land[chip] = bf[...]
        for slot, k, dev in ((0, first_chip, first), (1, second_chip, second), (2, diag_chip, second)):
            for w, out, pc in parts:
                ici(w, slot, out, k, dev, pc).wait_recv()
                if slot == 0:
                    sent.append(ici(w, 2, out, k, second, pc))
                    sent[-1].start()
                sent.append(d2d(w, slot, out, k, c, pc))
                sent[-1].start()
        for slot, k in ((0, second_chip), (1, first_chip), (2, diag_chip)):
            for w, out, pc in parts:
                d2d(w, slot, out, k, 1 - c, pc).wait_recv()
        for cp in sent:
            cp.wait_send()

    vmem = pl.BlockSpec(memory_space=pltpu.VMEM)
    outs = pl.pallas_call(
        body,
        out_shape=[jax.ShapeDtypeStruct((N_CHIPS,) + w.shape, BF16) for w in weights]
        + [jax.ShapeDtypeStruct(w.shape, BF16) for w in late]
        + [jax.ShapeDtypeStruct((N_CHIPS,) + w.shape, BF16) for w in late],
        in_specs=[vmem] * (nw + nl), out_specs=[vmem] * (nw + 2 * nl),
        scratch_shapes=[pltpu.SemaphoreType.DMA((3 * nw * pieces,))] * 4,
        compiler_params=pltpu.CompilerParams(vmem_limit_bytes=VMEM_LIMIT), name="ag_weights")(*weights, *late)
    return outs[:nw], outs[nw:nw + nl], outs[nw + nl:]


_HBM = pl.BlockSpec(memory_space=pltpu.HBM)
_SEM = pl.BlockSpec(memory_space=pltpu.SEMAPHORE)
_ANY = pl.BlockSpec(memory_space=pl.ANY)
_DATAFLOW = pltpu.SideEffectType.DATAFLOW_SIDE_EFFECTING


def _in_hbm(a):
    return pltpu.with_memory_space_constraint(a, pltpu.HBM)


def _exchange_copies(gather, srcs, lands, send_sems, recv_sems):
    nw = len(srcs)
    x, y, c, chip, peers, peer_chip = _place()
    pairs = []
    for m, (px, py) in enumerate(peers):
        for w in range(nw):
            sems = (send_sems.at[nw * m + w], recv_sems.at[nw * m + w], (px, py, c))
            if gather:
                pairs.append((_remote(srcs[w], lands[w].at[chip], *sems),
                              _remote(srcs[w], lands[w].at[peer_chip[m]], *sems)))
            else:
                pairs.append((_remote(srcs[w].at[m], lands[w].at[m], *sems),) * 2)
    return pairs


def _exchange_start(gather, srcs, after, name, lands=None):
    nw = len(srcs)
    n_copies = 3 * nw

    def body(*refs):
        send_sems, recv_sems = refs[2 * nw + 1], refs[2 * nw + 2]
        for start, _ in _exchange_copies(gather, refs[:nw], refs[nw:2 * nw], send_sems, recv_sems):
            start.start()
        refs[-1][...] = jnp.zeros_like(refs[-1])

    if lands is None:
        lands = [lax.empty(((N_CHIPS,) + s.shape) if gather else s.shape, s.dtype) for s in srcs]
    lands = [_in_hbm(l) for l in lands]
    return pl.pallas_call(
        body, name=name,
        out_shape=(pltpu.SemaphoreType.DMA((n_copies,)), pltpu.SemaphoreType.DMA((n_copies,)))
        + tuple(pltpu.HBM(s.shape, s.dtype) for s in srcs)
        + tuple(pltpu.HBM(l.shape, l.dtype) for l in lands)
        + (jax.ShapeDtypeStruct((8, LANES), F32),),
        in_specs=[_HBM] * (2 * nw) + [_ANY],
        out_specs=(_SEM, _SEM) + (_HBM,) * (2 * nw) + (pl.BlockSpec(memory_space=pltpu.VMEM),),
        input_output_aliases={i: 2 + i for i in range(2 * nw)},
        compiler_params=pltpu.CompilerParams(has_side_effects=_DATAFLOW),
    )(*[_in_hbm(s) for s in srcs], *lands, after)


def _exchange_wait(gather, started, after, name):
    nw = (len(started) - 3) // 2
    send_sems, recv_sems = started[0], started[1]
    thru = started[2:2 + 2 * nw]

    def body(*refs):
        for _, arrival in _exchange_copies(gather, refs[:nw], refs[nw:2 * nw], refs[2 * nw], refs[2 * nw + 1]):
            arrival.wait_send()
            arrival.wait_recv()

    outs = pl.pallas_call(
        body, name=name,
        out_shape=tuple(pltpu.HBM(t.shape, t.dtype) for t in thru),
        in_specs=[_HBM] * (2 * nw) + [_SEM, _SEM, _ANY], out_specs=(_HBM,) * (2 * nw),
        input_output_aliases={i: i for i in range(2 * nw)},
        compiler_params=pltpu.CompilerParams(has_side_effects=_DATAFLOW),
    )(*thru, send_sems, recv_sems, after)
    return outs[nw:]


def _reduce_last(owns, landed, g_small):
    ns = len(owns)
    row_block = 32
    hs = SMALL_ROWS // 2

    def body(*refs):
        own_refs, land_refs, gsm_ref = refs[:ns], refs[ns:2 * ns], refs[2 * ns]
        out_refs, osm_ref = refs[2 * ns + 1:3 * ns + 1], refs[3 * ns + 1]
        ra_sm, p_sm, s_sem, r_sem, sm_s, sm_r = refs[3 * ns + 2:]
        x, y, c, chip, peers, peer_chip = _place()
        sib = (x, y, 1 - c)
        half = lambda cc: pl.ds(pl.multiple_of(cc * hs, 8), hs)
        sm_a = _remote(gsm_ref.at[half(1 - c), :], ra_sm, sm_s.at[0], sm_r.at[0], sib)
        sm_a.start()
        swaps = [sm_a]
        for w in range(ns):
            n = own_refs[w].shape[0]

            def total(i, carry, w=w, n=n):
                r0 = pl.multiple_of(i * row_block, row_block)
                blk = pl.ds(r0, row_block)
                acc = own_refs[w][blk, :]
                for m in range(3):
                    acc = acc + land_refs[w][m, blk, :].astype(F32)
                out_refs[w][pl.ds(pl.multiple_of(c * n + r0, 8), row_block), :] = acc
                return carry
            lax.fori_loop(0, n // row_block, total, 0)
            mine = out_refs[w].at[pl.ds(pl.multiple_of(c * n, 8), n), :]
            swaps.append(_remote(mine, mine, s_sem.at[w], r_sem.at[w], sib))
            swaps[-1].start()
        sm_a.wait_recv()
        p_sm[chip] = gsm_ref[half(c), :] + ra_sm[...]
        for m, (px, py) in enumerate(peers):
            swaps.append(_remote(p_sm.at[chip], p_sm.at[chip], sm_s.at[1 + m], sm_r.at[1 + m], (px, py, c)))
            swaps[-1].start()
        for m, (px, py) in enumerate(peers):
            _remote(p_sm.at[chip], p_sm.at[peer_chip[m]], sm_s.at[1 + m], sm_r.at[1 + m], (px, py, c)).wait_recv()
        osm_ref[half(c), :] = (p_sm[0] + p_sm[1]) + (p_sm[2] + p_sm[3])
        swaps.append(_remote(osm_ref.at[half(c), :], osm_ref.at[half(c), :], sm_s.at[4], sm_r.at[4], sib))
        swaps[-1].start()
        for w in range(ns):
            n = own_refs[w].shape[0]
            theirs = out_refs[w].at[pl.ds(pl.multiple_of((1 - c) * n, 8), n), :]
            _remote(theirs, theirs, s_sem.at[w], r_sem.at[w], sib).wait_recv()
        _remote(osm_ref.at[half(1 - c), :], osm_ref.at[half(1 - c), :], sm_s.at[4], sm_r.at[4], sib).wait_recv()
        for cp in swaps:
            cp.wait_send()

    vmem = pl.BlockSpec(memory_space=pltpu.VMEM)
    return pl.pallas_call(
        body, out_shape=[jax.ShapeDtypeStruct((2 * o.shape[0], o.shape[1]), F32) for o in owns]
        + [jax.ShapeDtypeStruct((SMALL_ROWS, LANES), F32)],
        in_specs=[vmem] * (2 * ns + 1), out_specs=[vmem] * (ns + 1),
        scratch_shapes=[pltpu.VMEM((hs, LANES), F32), pltpu.VMEM((N_CHIPS, hs, LANES), F32),
                        pltpu.SemaphoreType.DMA((ns,)), pltpu.SemaphoreType.DMA((ns,)),
                        pltpu.SemaphoreType.DMA((5,)), pltpu.SemaphoreType.DMA((5,))],
        compiler_params=pltpu.CompilerParams(vmem_limit_bytes=VMEM_LIMIT),
        name="reduce_last")(*owns, *landed, g_small)


_SMALL_PARTS = (("g_norm", 8, 8), ("w_s", 512, 512), ("b_s", 4, 8), ("g_v", 2, 8), ("g_mem", 8, 8),
                ("g_final", 8, 8), ("loss", 8, 8))
_LOSS_ROW = SMALL_ROWS - 8
assert sum(p for _, _, p in _SMALL_PARTS) == SMALL_ROWS


def _pack_small(parts, loss_block):
    rows = []
    for (name, used, padded), p in zip(_SMALL_PARTS, list(parts) + [loss_block]):
        p = p.reshape(used, LANES)
        if padded > used:
            p = jnp.pad(p, ((0, padded - used), (0, 0)))
        rows.append(p)
    return jnp.concatenate(rows, axis=0)


def _local_step(x, mem, target, g_norm, w_in, w_s, b_s, g_v, g_mem, late_weights, g_final,
                fwd_token=None, on_dw=None):
    B, S, _ = x.shape
    x2d = x.reshape(B * S, D_MODEL)
    t2d = target.reshape(B * S, D_MODEL)
    mem2d = mem.reshape(B * N_MEM, D_MODEL)

    proj = _inproj_fwd(x2d, g_norm, w_in, after=() if fwd_token is None else (fwd_token,))
    w_kv, w_out = late_weights(proj)
    kv = _kv_fwd(mem2d, g_mem, w_kv)
    a, lse = _attn_fwd(proj, B, S)
    w_sT = jnp.swapaxes(w_s, 1, 2)
    b_tab = jnp.repeat(b_s.T, HEAD_DIM, axis=1)
    (dx2, da, drest, loss, d_wout, d_ws, d_bs, d_gv, d_gf, dkv) = _mid(
        x2d, t2d, a, proj, kv, w_s, w_sT, b_tab, g_v, w_out, g_final, B, S)
    d_wkv, d_gmem = _kv_bwd(mem2d, g_mem, w_kv, dkv)
    dq, dk, dv = _attn_bwd(proj, a, lse, da, B, S)
    if on_dw is None:
        d_win = _inproj_bwd_dw(dq, dk, dv, drest, x2d, g_norm)
        after = ()
    else:
        d_win = None
        by_chip = lambda g: g.reshape((N_CHIPS, g.shape[0] // N_CHIPS, g.shape[1]))
        after = (on_dw(*_inproj_bwd_dw(dq, dk, dv, drest, x2d, g_norm, reduce_with=[by_chip(d_wkv), by_chip(d_wout)])),)
    grad_x, d_gnorm = _inproj_bwd_dx(dq, dk, dv, drest, x2d, dx2, g_norm, w_in, after=after)
    d_bs = d_bs[:, :N_SGU_GROUPS].T
    return (loss, grad_x.reshape(B, S, D_MODEL),
            dict(g_norm=d_gnorm, w_in=d_win, w_s=d_ws, b_s=d_bs, g_v=d_gv, g_mem=d_gmem, w_kv=d_wkv,
                 w_out=d_wout, g_final=d_gf))


def kernel(x, mem, g_norm, w_in, w_sgu_spatial, b_sgu_spatial, g_sgu_v, g_mem, w_mem_kv, w_out, g_final, loss_target, m_g_norm, m_w_in, m_w_sgu_spatial, m_b_sgu_spatial, m_g_sgu_v, m_g_mem, m_w_mem_kv, m_w_out, m_g_final, v_g_norm, v_w_in, v_w_sgu_spatial, v_b_sgu_spatial, v_g_sgu_v, v_g_mem, v_w_mem_kv, v_w_out, v_g_final):
    t = lambda w: jnp.swapaxes(w[0], 0, 1)
    (win_all,), late_shards, late_lands = _ag_weights([t(w_in)], [w_mem_kv[0], w_out[0]])
    w_in_full = win_all.reshape(-1, win_all.shape[-1])
    late = _exchange_start(True, list(late_shards), win_all, "gather_late_start", lands=late_lands)

    def late_weights(proj):
        return [z.reshape(-1, z.shape[-1]) for z in _exchange_wait(True, late, proj, "gather_late_wait")]

    scatter = {}

    def on_dw(sends, owns):
        scatter["own"] = owns
        scatter["started"] = _exchange_start(False, list(sends), owns[0], "scatter_start")
        return scatter["started"][-1]

    loss, grad_x, g = _local_step(
        x, mem, loss_target, g_norm, w_in_full, w_sgu_spatial[0], b_sgu_spatial[0], g_sgu_v, g_mem,
        late_weights, g_final.reshape(1, D_MODEL), fwd_token=late[-1], on_dw=on_dw)

    small_names = ("g_norm", "w_s", "b_s", "g_v", "g_mem", "g_final")
    g_small = _pack_small([g[n] for n in small_names], loss)
    landed = _exchange_wait(False, scatter["started"], g_small, "scatter_wait")
    gr_in, gr_kv, gr_out, gr_small = _reduce_last(scatter["own"], landed, g_small)

    small_w = (g_norm, w_sgu_spatial, b_sgu_spatial, g_sgu_v, g_mem, g_final)
    small_m = (m_g_norm, m_w_sgu_spatial, m_b_sgu_spatial, m_g_sgu_v, m_g_mem, m_g_final)
    small_v = (v_g_norm, v_w_sgu_spatial, v_b_sgu_spatial, v_g_sgu_v, v_g_mem, v_g_final)
    rows = lambda ws: [w.reshape(-1, LANES) for w in ws]
    small_new, ((gr_kv, d_kv, nm_kv, nv_kv), (gr_out, d_out, nm_out, nv_out)), loss = _adamw_rest(
        gr_small, rows(small_w), rows(small_m), rows(small_v),
        [(w_mem_kv[0], gr_kv, m_w_mem_kv[0], v_w_mem_kv[0]), (w_out[0], gr_out, m_w_out[0], v_w_out[0])])
    loss = loss.reshape(())
    small = [[z.reshape(w.shape) for z in four] for w, four in zip(small_w, small_new)]
    gr_in, d_in, nm_in, nv_in = [jnp.swapaxes(z, 0, 1)
                                 for z in _adamw(t(w_in), gr_in, t(m_w_in), t(v_w_in), "adamw_w_in")]

    def leaves(kind, big_in, big_kv, big_out):
        s_norm, s_ws, s_bs, s_gv, s_gmem, s_gf = [four[kind] for four in small]
        return [s_norm, big_in[None], s_ws, s_bs, s_gv, s_gmem, big_kv[None], big_out[None], s_gf]

    return (loss, grad_x, *leaves(0, gr_in, gr_kv, gr_out), *leaves(1, d_in, d_kv, d_out),
            *leaves(2, nm_in, nm_kv, nm_out), *leaves(3, nv_in, nv_kv, nv_out))
```

```python
import functools

import jax
import jax.numpy as jnp
from jax import lax
from jax.experimental import pallas as pl
from jax.experimental.pallas import tpu as pltpu

F32 = jnp.float32
BF16 = jnp.bfloat16
MESH = pl.DeviceIdType.MESH

D_MODEL = 1024
ATTN_WIDTH = 512
SGU_WIDTH = 256
MEM_WIDTH = 256
N_MEM = 256
IN_COLS = 3328
QKV_COLS = 3 * ATTN_WIDTH
REST_COLS = IN_COLS - QKV_COLS
SGU_CHUNK = 128
N_SGU_GROUPS = 4
EPS = 1e-6
NEG_INF = -1e30
DILATIONS = (1, 4, 16)
RADIUS = 64
Q_BLOCK = 128
LANES = 128
HEAD_DIM = 64

ADAM_LR = 0.001
ADAM_B1 = 0.9
ADAM_B2 = 0.999
ADAM_EPS = 1e-08
ADAM_WD = 0.01
ADAM_STEP = 10

N_CHIPS = 4
VMEM_LIMIT = 56 * 1024 * 1024
SMALL_ROWS = 560


def _params(sem=None, vmem=VMEM_LIMIT):
    return pltpu.CompilerParams(dimension_semantics=sem, vmem_limit_bytes=vmem)


def _nn(a, b):
    return jnp.dot(a, b, preferred_element_type=F32)


def _nt(a, b):
    return lax.dot_general(a, b, (((1,), (1,)), ((), ())), preferred_element_type=F32)


def _tn(a, b):
    return lax.dot_general(a, b, (((0,), (0,)), ((), ())), preferred_element_type=F32)


def _rms(x):
    r = lax.rsqrt(jnp.mean(x * x, axis=-1, keepdims=True) + EPS)
    return r, x * r


def _head_masks():
    lane = lax.broadcasted_iota(jnp.int32, (1, LANES), 1)
    lo = lane < HEAD_DIM
    return lo, (lo.astype(F32), (~lo).astype(F32))


def _silu_parts(z):
    s = jax.nn.sigmoid(z)
    return z * s, s * (1.0 + z * (1.0 - s))


def _gelu_parts(x):
    c = 0.7978845608028654
    x2 = x * x
    s = jax.nn.sigmoid((2.0 * c) * (x + 0.044715 * (x * x2)))
    return x * s, s * (1.0 + x * (1.0 - s) * ((2.0 * c) * (1.0 + 3.0 * 0.044715 * x2)))


def _after(tokens):
    return [pl.BlockSpec(memory_space=pl.ANY)] * len(tokens)


def _inproj_fwd(x2d, g_norm, w_in_t, after=()):
    T = x2d.shape[0]
    tm = 512

    def body(x_ref, g_ref, w_ref, *rest):
        o_ref = rest[-1]
        _, xh = _rms(x_ref[...])
        h = (xh * g_ref[...]).astype(BF16)
        o_ref[...] = _nt(h, w_ref[...])

    return pl.pallas_call(
        body, grid=(T // tm,),
        in_specs=[pl.BlockSpec((tm, D_MODEL), lambda i: (i, 0)),
                  pl.BlockSpec((1, D_MODEL), lambda i: (0, 0)),
                  pl.BlockSpec((IN_COLS, D_MODEL), lambda i: (0, 0))] + _after(after),
        out_specs=pl.BlockSpec((tm, IN_COLS), lambda i: (i, 0)),
        out_shape=jax.ShapeDtypeStruct((T, IN_COLS), F32),
        compiler_params=_params(("arbitrary",)), name="inproj_fwd")(x2d, g_norm, w_in_t, *after)


def _kv_fwd(mem2d, g_mem, w_kv):
    Tm = mem2d.shape[0]

    def body(m_ref, g_ref, w_ref, o_ref):
        _, mh = _rms(m_ref[...])
        o_ref[...] = _nn((mh * g_ref[...]).astype(BF16), w_ref[...])

    return pl.pallas_call(
        body, out_shape=jax.ShapeDtypeStruct((Tm, 2 * MEM_WIDTH), F32),
        compiler_params=_params(), name="kv_fwd")(mem2d, g_mem, w_kv)


def _kv_bwd(mem2d, g_mem, w_kv, dkv):
    Tm = mem2d.shape[0]

    def body(m_ref, g_ref, w_ref, dkv_ref, dw_ref, dg_ref):
        _, mh = _rms(m_ref[...])
        memn = (mh * g_ref[...]).astype(BF16)
        dkvb = dkv_ref[...].astype(BF16)
        dw_ref[...] = _tn(memn, dkvb).astype(BF16)
        dmemn = _nt(dkvb, w_ref[...])
        dg_ref[...] = jnp.sum(dmemn * mh, axis=0, keepdims=True)

    return pl.pallas_call(
        body, out_shape=(jax.ShapeDtypeStruct((D_MODEL, 2 * MEM_WIDTH), BF16),
                         jax.ShapeDtypeStruct((1, D_MODEL), F32)),
        compiler_params=_params(), name="kv_bwd")(mem2d, g_mem, w_kv, dkv)


def _attn_geometry(S):
    geom = []
    for d in DILATIONS:
        L = S // d
        assert L % Q_BLOCK == 0
        geom.append((d, L, min(2 * Q_BLOCK, L), L // Q_BLOCK))
    return geom


def _init_bias(bias_scr, geom, hp):
    row = lax.broadcasted_iota(jnp.int32, (Q_BLOCK, 2 * Q_BLOCK), 0)
    col = lax.broadcasted_iota(jnp.int32, (Q_BLOCK, 2 * Q_BLOCK), 1)
    for j in (0, 1):
        bits = (126 - (2 * hp + j)) * (1 << 23)
        slope = lax.bitcast_convert_type(jnp.full((1, 1), bits, jnp.int32), F32)
        for di, (d, _, _, _) in enumerate(geom):
            for cls, off in enumerate((0, -RADIUS, -2 * RADIUS)):
                dist = jnp.abs(col - row + off)
                bias_scr[di * 6 + cls * 2 + j] = jnp.where(
                    dist <= RADIUS, -(slope * float(d)) * dist.astype(F32), NEG_INF)


SPLIT = 4
COPY_ROWS = 256


def _by4_rows(S, step):
    per_class = S // SPLIT // COPY_ROWS
    r, j = step // per_class, step % per_class
    return (pl.ds(r + SPLIT * j * COPY_ROWS, COPY_ROWS, stride=SPLIT),
            pl.ds(r * (S // SPLIT) + j * COPY_ROWS, COPY_ROWS))


def _to_by4(src, dst, S):
    for i in range(S // COPY_ROWS):
        natural, by4 = _by4_rows(S, i)
        dst[by4, :] = src[natural, :]


def _block_slices(d, L, KW, nqb, r, qb, S):
    qs = qb * Q_BLOCK
    ks = jnp.clip(qs - RADIUS, 0, L - KW)
    cls = jnp.where(qb == 0, 0, jnp.where(qb == nqb - 1, 2, 1))
    if d == 1:
        qsl = pl.ds(pl.multiple_of(qs, Q_BLOCK), Q_BLOCK)
        ksl = pl.ds(pl.multiple_of(ks, RADIUS), KW)
    elif d == SPLIT:
        qsl = pl.ds(pl.multiple_of(r * L + qs, Q_BLOCK), Q_BLOCK)
        ksl = pl.ds(pl.multiple_of(r * L + ks, RADIUS), KW)
    else:
        sub = d // SPLIT
        base = (r % SPLIT) * (S // SPLIT) + r // SPLIT
        qsl = pl.ds(base + qs * sub, Q_BLOCK, stride=sub)
        ksl = pl.ds(base + ks * sub, KW, stride=sub)
    return qsl, ksl, cls


def _for_groups(geom, S, group, fn):
    for di, (d, L, KW, nqb) in enumerate(geom):
        n = group[di]
        assert (d * nqb) % n == 0

        def step(it, carry, di=di, d=d, L=L, KW=KW, nqb=nqb, n=n):
            slices = []
            for g in range(n):
                i = it * n + g
                slices.append(_block_slices(d, L, KW, nqb, i // nqb, i % nqb, S))
            fn(di, KW, slices)
            return carry
        lax.fori_loop(0, d * nqb // n, step, 0)


def _attn_fwd(proj, B, S):
    T = B * S
    geom = _attn_geometry(S)
    n_pairs = ATTN_WIDTH // LANES

    def body(q_ref, k_ref, v_ref, a_ref, lse_ref, bias_scr, q4, k4, v4, *per_dilation):
        o_scr, m_scr, l_scr = per_dilation[0:3], per_dilation[3:6], per_dilation[6:9]
        lo, hm = _head_masks()
        pair = pl.program_id(0)

        @pl.when(pl.program_id(1) == 0)
        def _():
            _init_bias(bias_scr, geom, pair)
        for src, dst in ((q_ref, q4), (k_ref, k4), (v_ref, v4)):
            _to_by4(src, dst, S)

        def group(di, KW, all_slices):
            run = 8
            for first in range(0, len(all_slices), run):
                some(di, KW, all_slices[first:first + run])

        def some(di, KW, slices):
            chains = [(g, j) for g in range(len(slices)) for j in (0, 1)]
            q_src, k_src, v_src = (q_ref, k_ref, v_ref) if di == 0 else (q4, k4, v4)
            q = [q_src[qsl, :] for qsl, _, _ in slices]
            kw = [k_src[ksl, :].astype(BF16) for _, ksl, _ in slices]
            vw = [v_src[ksl, :].astype(BF16) for _, ksl, _ in slices]
            s = {(g, j): _nt((q[g] * (hm[j] * 0.125)).astype(BF16), kw[g])
                 + bias_scr[di * 6 + slices[g][2] * 2 + j, :, pl.ds(0, KW)] for g, j in chains}
            m = {c: jnp.max(s[c], axis=1, keepdims=True) for c in chains}
            p = {c: jnp.exp(s[c] - m[c]) for c in chains}
            l = {c: jnp.sum(p[c], axis=1, keepdims=True) for c in chains}
            o = {(g, j): _nn(p[(g, j)].astype(BF16), vw[g]) for g, j in chains}
            for g, (qsl, _, _) in enumerate(slices):
                o_scr[di][qsl, :] = jnp.where(lo, o[(g, 0)], o[(g, 1)])
                m_scr[di][qsl, :] = jnp.where(lo, m[(g, 0)], m[(g, 1)])
                l_scr[di][qsl, :] = jnp.where(lo, l[(g, 0)], l[(g, 1)])

        _for_groups(geom, S, (16, 16, 16), group)

        for i in range(S // COPY_ROWS):
            natural, by4 = _by4_rows(S, i)
            rows = [natural, by4, by4]
            ms = [m_scr[di][rows[di], :] for di in range(3)]
            mx = jnp.maximum(jnp.maximum(ms[0], ms[1]), ms[2])
            num = 0.0
            den = 0.0
            for di in range(3):
                w = jnp.exp(ms[di] - mx)
                num = num + w * o_scr[di][rows[di], :]
                den = den + w * l_scr[di][rows[di], :]
            a_ref[natural, :] = num / den
            lse_ref[natural, :] = mx + jnp.log(den)

    blk = lambda off: pl.BlockSpec((S, LANES), lambda h, b, off=off: (b, off + h))
    out_blk = pl.BlockSpec((S, LANES), lambda h, b: (b, h))
    return pl.pallas_call(
        body, grid=(n_pairs, B),
        in_specs=[blk(0), blk(n_pairs), blk(2 * n_pairs)],
        out_specs=[out_blk, out_blk],
        out_shape=[jax.ShapeDtypeStruct((T, ATTN_WIDTH), F32)] * 2,
        scratch_shapes=[pltpu.VMEM((18, Q_BLOCK, 2 * Q_BLOCK), F32)] + [pltpu.VMEM((S, LANES), F32)] * 12,
        compiler_params=_params(("arbitrary", "arbitrary")), name="attn_fwd")(proj, proj, proj)


def _attn_bwd(proj, a, lse, da, B, S, after=()):
    T = B * S
    geom = _attn_geometry(S)
    n_pairs = ATTN_WIDTH // LANES

    def body(q_ref, k_ref, v_ref, a_ref, lse_ref, do_ref, *rest):
        dq_ref, dk_ref, dv_ref, bias_scr = rest[len(after):len(after) + 4]
        scr = rest[len(after) + 4:]
        acc = (scr[0:3], scr[3:6])
        natural_in = (q_ref, k_ref, v_ref, a_ref, lse_ref, do_ref)
        by4_in = scr[6:12]
        _, hm = _head_masks()
        pair = pl.program_id(0)

        @pl.when(pl.program_id(1) == 0)
        def _():
            _init_bias(bias_scr, geom, pair)
        for ref in scr[0:6]:
            ref[...] = jnp.zeros_like(ref)
        for src, dst in zip(natural_in, by4_in):
            _to_by4(src, dst, S)

        def group(di, KW, all_slices):
            run = (4, 4, 8)[di]
            for first in range(0, len(all_slices), run):
                some(di, KW, all_slices[first:first + run])

        def some(di, KW, slices):
            n = len(slices)
            chains = [(g, j) for g in range(n) for j in (0, 1)]
            q_src, k_src, v_src, a_src, lse_src, do_src = natural_in if di == 0 else by4_in
            dq_scr, dk_scr, dv_scr = acc[0 if di == 0 else 1]
            q = [q_src[qsl, :] for qsl, _, _ in slices]
            do = [do_src[qsl, :] for qsl, _, _ in slices]
            doa = [do[g] * a_src[slices[g][0], :] for g in range(n)]
            lse_q = [lse_src[qsl, :] for qsl, _, _ in slices]
            kw = [k_src[ksl, :].astype(BF16) for _, ksl, _ in slices]
            vw = [v_src[ksl, :].astype(BF16) for _, ksl, _ in slices]
            qj = {(g, j): (q[g] * (hm[j] * 0.125)).astype(BF16) for g, j in chains}
            doj = {(g, j): (do[g] * hm[j]).astype(BF16) for g, j in chains}
            s = {(g, j): _nt(qj[(g, j)], kw[g])
                 + bias_scr[di * 6 + slices[g][2] * 2 + j, :, pl.ds(0, KW)] for g, j in chains}
            dp = {(g, j): _nt(doj[(g, j)], vw[g]) for g, j in chains}
            dsum = {(g, j): jnp.sum(doa[g] * hm[j], axis=1, keepdims=True) for g, j in chains}
            p = {(g, j): jnp.exp(s[(g, j)] - lse_q[g][:, HEAD_DIM * j:HEAD_DIM * j + 1]) for g, j in chains}
            ds = {c: (p[c] * (dp[c] - dsum[c])).astype(BF16) for c in chains}
            pb = {c: p[c].astype(BF16) for c in chains}
            dq = [_nn(ds[(g, 0)], kw[g]) * (hm[0] * 0.125) + _nn(ds[(g, 1)], kw[g]) * (hm[1] * 0.125)
                  for g in range(n)]
            both = lambda t, g: jnp.concatenate([t[(g, 0)], t[(g, 1)]], axis=0)
            dkw = [_tn(both(ds, g), both(qj, g)) for g in range(n)]
            dvw = [_tn(both(pb, g), both(doj, g)) for g in range(n)]
            for g, (qsl, ksl, _) in enumerate(slices):
                dq_scr[qsl, :] = dq_scr[qsl, :] + dq[g]
                dk_scr[ksl, :] = dk_scr[ksl, :] + dkw[g]
                dv_scr[ksl, :] = dv_scr[ksl, :] + dvw[g]

        _for_groups(geom, S, (16, 16, 16), group)

        for i in range(S // COPY_ROWS):
            natural, by4 = _by4_rows(S, i)
            for nat, split in zip(*acc):
                nat[natural, :] = nat[natural, :] + split[by4, :]
        for out, nat in zip((dq_ref, dk_ref, dv_ref), acc[0]):
            out[...] = nat[...].astype(BF16)

    blk = lambda off: pl.BlockSpec((S, LANES), lambda h, b, off=off: (b, off + h))
    return pl.pallas_call(
        body, grid=(n_pairs, B),
        in_specs=[blk(0), blk(n_pairs), blk(2 * n_pairs), blk(0), blk(0), blk(0)] + _after(after),
        out_specs=[blk(0), blk(0), blk(0)],
        out_shape=[jax.ShapeDtypeStruct((T, ATTN_WIDTH), BF16)] * 3,
        scratch_shapes=[pltpu.VMEM((18, Q_BLOCK, 2 * Q_BLOCK), F32)] + [pltpu.VMEM((S, LANES), F32)] * 12,
        compiler_params=_params(("arbitrary", "arbitrary")), name="attn_bwd")(proj, proj, proj, a, lse, da, *after)


def _mid(x2d, t2d, a, proj, kv, w_s, w_sT, b_tab, g_v, w_out, g_final, B, S):
    T = B * S
    tm = 512
    nt = S // tm
    halves = 2
    hrows = tm // halves

    def body(x_ref, t_ref, a_ref, za_ref, ub_ref, vb_ref, zb_ref, qm_ref, zm_ref, kv_ref,
              ws_ref, wsT_ref, btab_ref, gv_ref, wout_ref, gf_ref,
              dx2_ref, da_ref, drest_ref, loss_ref, dwout_bf_ref, dws_ref, dbs_ref, dgv_ref, dgf_ref, dkv_ref,
              dbtab_scr, dwout_ref):
        b = pl.program_id(0)
        t = pl.program_id(1)
        first = jnp.logical_and(b == 0, t == 0)
        last = jnp.logical_and(b == B - 1, t == nt - 1)
        _, hm = _head_masks()
        lane_g = lax.broadcasted_iota(jnp.int32, (1, SGU_WIDTH), 1) // HEAD_DIM
        gm = [(lane_g == g).astype(F32) for g in range(N_SGU_GROUPS)]
        H = range(halves)
        rows = [pl.ds(h * hrows, hrows) for h in H]
        ld = lambda ref: [ref[r, :] for r in rows]
        cat = lambda parts, axis: jnp.concatenate(parts, axis=axis)
        chunks = [slice(ci * SGU_CHUNK, (ci + 1) * SGU_CHUNK) for ci in range(hrows // SGU_CHUNK)]
        pairs = [slice(pr * LANES, (pr + 1) * LANES) for pr in range(2)]
        heads = [(pr, j) for pr in range(2) for j in (0, 1)]

        @pl.when(first)
        def _():
            loss_ref[...] = jnp.zeros_like(loss_ref)
            dwout_ref[...] = jnp.zeros_like(dwout_ref)
            dws_ref[...] = jnp.zeros_like(dws_ref)
            dbs_ref[...] = jnp.zeros_like(dbs_ref)
            dgv_ref[...] = jnp.zeros_like(dgv_ref)
            dgf_ref[...] = jnp.zeros_like(dgf_ref)
            dbtab_scr[...] = jnp.zeros_like(dbtab_scr)

        @pl.when(t == 0)
        def _():
            dkv_ref[...] = jnp.zeros_like(dkv_ref)

        a_val = ld(a_ref)
        sil_a = [_silu_parts(z) for z in ld(za_ref)]
        gated_a = [s[0] * a for s, a in zip(sil_a, a_val)]
        u = [_gelu_parts(z) for z in ld(ub_ref)]
        vv = [_gelu_parts(z) for z in ld(vb_ref)]
        vnorm = [_rms(v[0]) for v in vv]
        gv = gv_ref[...]
        vn = [(n[1] * gv).astype(BF16) for n in vnorm]
        w_cat = cat([ws_ref[g].astype(BF16) for g in range(N_SGU_GROUPS)], 1)
        wT_cat = cat([wsT_ref[g].astype(BF16) for g in range(N_SGU_GROUPS)], 1)
        gmb = [m.astype(BF16) for m in gm]
        by_group = lambda chunk: cat([chunk * gmb[g] for g in range(N_SGU_GROUPS)], 0)
        btab = btab_ref[...]
        mixed = [cat([btab + _nn(w_cat, by_group(vn[h][c, :])) for c in chunks], 0) for h in H]
        sg = [u[h][0] * mixed[h] for h in H]
        sil_b = [_silu_parts(z) for z in ld(zb_ref)]
        gated_b = [sil_b[h][0] * sg[h] for h in H]

        kvv = kv_ref[...].astype(BF16)
        kp = [kvv[:, p] for p in pairs]
        vp = [kvv[:, MEM_WIDTH + pr * LANES:MEM_WIDTH + (pr + 1) * LANES] for pr in range(2)]
        qm = ld(qm_ref)
        qj = {(h, pr, j): (qm[h][:, pairs[pr]] * (hm[j] * 0.125)).astype(BF16) for h in H for pr, j in heads}
        sc = {k: _nt(qj[k], kp[k[1]]) for k in qj}
        ex = {k: jnp.exp(sc[k] - jnp.max(sc[k], axis=1, keepdims=True)) for k in qj}
        prob = {k: ex[k] * (1.0 / jnp.sum(ex[k], axis=1, keepdims=True)) for k in qj}
        probb = {k: prob[k].astype(BF16) for k in qj}
        mo = [cat([sum(_nn(probb[(h, pr, j)], vp[pr]) * hm[j] for j in (0, 1)) for pr in range(2)], 1) for h in H]
        sil_m = [_silu_parts(z) for z in ld(zm_ref)]
        gated_m = [sil_m[h][0] * mo[h] for h in H]

        gated = [cat([gated_a[h], gated_b[h], gated_m[h]], 1).astype(BF16) for h in H]
        wout = wout_ref[...]
        x_in = ld(x_ref)
        x2 = [x_in[h] + _nn(gated[h], wout) for h in H]
        fin = [_rms(z) for z in x2]
        gf = gf_ref[...]
        tgt = ld(t_ref)
        err = [fin[h][1] * gf - tgt[h] for h in H]
        loss_ref[...] += sum(jnp.sum(e * e) for e in err) * (0.5 / D_MODEL)

        dy = [e * (1.0 / D_MODEL) for e in err]
        dgf_ref[...] += sum(jnp.sum(dy[h] * fin[h][1], axis=0, keepdims=True) for h in H)
        gdy = [d * gf for d in dy]
        dx2 = [fin[h][0] * (gdy[h] - fin[h][1] * jnp.mean(gdy[h] * fin[h][1], axis=1, keepdims=True)) for h in H]
        for h in H:
            dx2_ref[rows[h], :] = dx2[h]
        dx2b = [d.astype(BF16) for d in dx2]
        dgated = [_nt(d, wout) for d in dx2b]
        dwout_ref[...] += _tn(cat(gated, 0), cat(dx2b, 0))
        dga = [d[:, 0:ATTN_WIDTH] for d in dgated]
        dgb = [d[:, ATTN_WIDTH:ATTN_WIDTH + SGU_WIDTH] for d in dgated]
        dgm = [d[:, ATTN_WIDTH + SGU_WIDTH:] for d in dgated]

        for h in H:
            da_ref[rows[h], :] = dga[h] * sil_a[h][0]
        dza = [dga[h] * a_val[h] * sil_a[h][1] for h in H]

        dsg = [dgb[h] * sil_b[h][0] for h in H]
        dzb = [dgb[h] * sg[h] * sil_b[h][1] for h in H]
        dub = [dsg[h] * mixed[h] * u[h][1] for h in H]
        dmixed = [dsg[h] * u[h][0] for h in H]
        dmixed_b = [d.astype(BF16) for d in dmixed]
        dvn = [cat([_nn(wT_cat, by_group(dmixed_b[h][c, :])) for c in chunks], 0) for h in H]
        for g in range(N_SGU_GROUPS):
            dws_ref[g] += sum(_nt((dmixed[h][c, :] * gm[g]).astype(BF16), vn[h][c, :]) for h in H for c in chunks)
        dbtab_scr[...] += sum(dmixed[h][c, :] for h in H for c in chunks)
        dgv_ref[...] += sum(jnp.sum(dvn[h] * vnorm[h][1], axis=0, keepdims=True) for h in H)
        tv = [d * gv for d in dvn]
        dvv = [vnorm[h][0] * (tv[h] - vnorm[h][1] * jnp.mean(tv[h] * vnorm[h][1], axis=1, keepdims=True)) for h in H]
        dvb = [dvv[h] * vv[h][1] for h in H]

        dmo = [dgm[h] * sil_m[h][0] for h in H]
        dzm = [dgm[h] * mo[h] * sil_m[h][1] for h in H]
        dmoj = {(h, pr, j): (dmo[h][:, pairs[pr]] * hm[j]).astype(BF16) for h in H for pr, j in heads}
        dp = {k: _nt(dmoj[k], vp[k[1]]) for k in qj}
        ds = {k: (prob[k] * (dp[k] - jnp.sum(dp[k] * prob[k], axis=1, keepdims=True))).astype(BF16) for k in qj}
        dqm = [cat([sum(_nn(ds[(h, pr, j)], kp[pr]) * (hm[j] * 0.125) for j in (0, 1)) for pr in range(2)], 1)
               for h in H]
        every = lambda tbl, pr: cat([tbl[(h, pr, j)] for h in H for j in (0, 1)], 0)
        dk = [_tn(every(ds, pr), every(qj, pr)) for pr in range(2)]
        dv = [_tn(every(probb, pr), every(dmoj, pr)) for pr in range(2)]
        dkv_ref[...] += cat(dk + dv, 1)

        for h in H:
            drest_ref[rows[h], :] = cat([dza[h], dub[h], dvb[h], dzb[h], dqm[h], dzm[h]], 1).astype(BF16)

        @pl.when(last)
        def _():
            lane = lax.broadcasted_iota(jnp.int32, (1, LANES), 1)
            dbt = dbtab_scr[...]
            out = jnp.zeros((SGU_CHUNK, LANES), F32)
            for g in range(N_SGU_GROUPS):
                out = out + jnp.where(lane == g, jnp.sum(dbt * gm[g], axis=1, keepdims=True), 0.0)
            dbs_ref[...] = out
            for r0 in range(0, D_MODEL, SGU_CHUNK):
                dwout_bf_ref[r0:r0 + SGU_CHUNK, :] = dwout_ref[r0:r0 + SGU_CHUNK, :].astype(BF16)

    tile = lambda w, cb: pl.BlockSpec((tm, w), lambda b, t, cb=cb: (b * nt + t, cb))
    const = lambda shape: pl.BlockSpec(shape, lambda b, t, n=len(shape): (0,) * n)
    return pl.pallas_call(
        body, grid=(B, nt),
        in_specs=[tile(D_MODEL, 0), tile(D_MODEL, 0), tile(ATTN_WIDTH, 0),
                  tile(ATTN_WIDTH, 3),
                  tile(SGU_WIDTH, 8), tile(SGU_WIDTH, 9), tile(SGU_WIDTH, 10),
                  tile(MEM_WIDTH, 11), tile(MEM_WIDTH, 12),
                  pl.BlockSpec((N_MEM, 2 * MEM_WIDTH), lambda b, t: (b, 0)),
                  const((N_SGU_GROUPS, SGU_CHUNK, SGU_CHUNK)), const((N_SGU_GROUPS, SGU_CHUNK, SGU_CHUNK)),
                  const((SGU_CHUNK, SGU_WIDTH)), const((1, SGU_WIDTH)),
                  const((D_MODEL, D_MODEL)), const((1, D_MODEL))],
        out_specs=[tile(D_MODEL, 0), tile(ATTN_WIDTH, 0), tile(REST_COLS, 0),
                   const((8, LANES)), const((D_MODEL, D_MODEL)),
                   const((N_SGU_GROUPS, SGU_CHUNK, SGU_CHUNK)), const((SGU_CHUNK, LANES)),
                   const((1, SGU_WIDTH)), const((1, D_MODEL)),
                   pl.BlockSpec((N_MEM, 2 * MEM_WIDTH), lambda b, t: (b, 0))],
        out_shape=[jax.ShapeDtypeStruct((T, D_MODEL), F32), jax.ShapeDtypeStruct((T, ATTN_WIDTH), F32),
                   jax.ShapeDtypeStruct((T, REST_COLS), BF16),
                   jax.ShapeDtypeStruct((8, LANES), F32), jax.ShapeDtypeStruct((D_MODEL, D_MODEL), BF16),
                   jax.ShapeDtypeStruct((N_SGU_GROUPS, SGU_CHUNK, SGU_CHUNK), F32),
                   jax.ShapeDtypeStruct((SGU_CHUNK, LANES), F32),
                   jax.ShapeDtypeStruct((1, SGU_WIDTH), F32), jax.ShapeDtypeStruct((1, D_MODEL), F32),
                   jax.ShapeDtypeStruct((B * N_MEM, 2 * MEM_WIDTH), F32)],
        scratch_shapes=[pltpu.VMEM((SGU_CHUNK, SGU_WIDTH), F32), pltpu.VMEM((D_MODEL, D_MODEL), F32)],
        compiler_params=_params(("arbitrary", "arbitrary"), vmem=VMEM_LIMIT + 2 * 1024 * 1024), name="mid")(
            x2d, t2d, a, proj, proj, proj, proj, proj, proj, kv, w_s, w_sT, b_tab, g_v, w_out, g_final)


def _inproj_bwd_dx(dq, dk, dv, drest, x2d, dx2, g_norm, w_in_t, after=()):
    T = x2d.shape[0]
    tm = 512
    W = ATTN_WIDTH

    def body(dq_ref, dk_ref, dv_ref, dr_ref, x_ref, dx2_ref, g_ref, w_ref, *rest):
        gx_ref, dg_ref = rest[-2:]

        @pl.when(pl.program_id(0) == 0)
        def _():
            dg_ref[...] = jnp.zeros_like(dg_ref)

        halves = [pl.ds(h * (tm // 2), tm // 2) for h in (0, 1)]
        dh = [(_nn(dq_ref[r, :], w_ref[0:W, :]) + _nn(dk_ref[r, :], w_ref[W:2 * W, :])
               + _nn(dv_ref[r, :], w_ref[2 * W:3 * W, :]) + _nn(dr_ref[r, :], w_ref[QKV_COLS:IN_COLS, :]))
              for r in halves]
        nrm = [_rms(x_ref[r, :]) for r in halves]
        dg_ref[...] += sum(jnp.sum(d * n[1], axis=0, keepdims=True) for d, n in zip(dh, nrm))
        g = g_ref[...]
        for r, d, (rstd, xh) in zip(halves, dh, nrm):
            th = d * g
            gx_ref[r, :] = rstd * (th - xh * jnp.mean(th * xh, axis=1, keepdims=True)) + dx2_ref[r, :]

    tile = lambda w: pl.BlockSpec((tm, w), lambda i: (i, 0))
    return pl.pallas_call(
        body, grid=(T // tm,),
        in_specs=[tile(W), tile(W), tile(W), tile(REST_COLS), tile(D_MODEL), tile(D_MODEL),
                  pl.BlockSpec((1, D_MODEL), lambda i: (0, 0)),
                  pl.BlockSpec((IN_COLS, D_MODEL), lambda i: (0, 0))] + _after(after),
        out_specs=[tile(D_MODEL), pl.BlockSpec((1, D_MODEL), lambda i: (0, 0))],
        out_shape=[jax.ShapeDtypeStruct((T, D_MODEL), F32), jax.ShapeDtypeStruct((1, D_MODEL), F32)],
        compiler_params=_params(("arbitrary",)), name="inproj_bwd_dx")(
            dq, dk, dv, drest, x2d, dx2, g_norm, w_in_t, *after)


def _inproj_bwd_dw(dq, dk, dv, drest, x2d, g_norm, reduce_with=None):
    T = x2d.shape[0]
    tm = 512
    nt = T // tm
    W = ATTN_WIDTH
    fused = reduce_with is not None
    others = list(reduce_with) if fused else []
    ns = 1 + len(others)
    shard = IN_COLS // N_CHIPS
    halves = [shard // 2] + [s.shape[1] // 2 for s in others]
    cols = [D_MODEL] + [s.shape[2] for s in others]
    row_block = 32

    def body(dq_ref, dk_ref, dv_ref, dr_ref, x_ref, g_ref, *rest):
        if fused:
            stacks = rest[:ns - 1]
            sends, owns = rest[ns - 1:2 * ns - 1], rest[2 * ns - 1:3 * ns - 1]
            acc, ras, narrow = rest[3 * ns - 1], rest[3 * ns:4 * ns], rest[4 * ns]
            s_sem, r_sem = rest[4 * ns + 1], rest[4 * ns + 2]
            x, y, c, chip, peers, peer_chip = _place()
            sib = (x, y, 1 - c)

            def part(w, k, cc, r0=0, rows=None):
                n = halves[w]
                rows = n if rows is None else rows
                if w == 0:
                    return acc.at[pl.ds(pl.multiple_of(k * shard + cc * n + r0, 8), rows), :]
                return stacks[w - 1].at[k, pl.ds(pl.multiple_of(cc * n + r0, 8), rows), :]

            def swap_other(w):
                theirs = stacks[w - 1].at[:, pl.ds(pl.multiple_of((1 - c) * halves[w], 8), halves[w]), :]
                return _remote(theirs, ras[w], s_sem.at[N_CHIPS - 1 + w], r_sem.at[N_CHIPS - 1 + w], sib)

            def swap_win(k):
                return _remote(narrow.at[k], ras[0].at[k], s_sem.at[k], r_sem.at[k], sib)
        else:
            acc = rest[0]

        @pl.when(pl.program_id(0) == 0)
        def _():
            acc[...] = jnp.zeros_like(acc)
            for w in range(1, ns):
                swap_other(w).start()

        _, xh = _rms(x_ref[...])
        h = (xh * g_ref[...]).astype(BF16)
        acc[0:W, :] += _tn(dq_ref[...], h)
        acc[W:2 * W, :] += _tn(dk_ref[...], h)
        acc[2 * W:3 * W, :] += _tn(dv_ref[...], h)
        acc[QKV_COLS:IN_COLS, :] += _tn(dr_ref[...], h)

        if fused:
            @pl.when(pl.program_id(0) == nt - 1)
            def _():
                for k in range(N_CHIPS):
                    def to_bf16(i, carry, k=k):
                        r0 = pl.multiple_of(i * row_block, row_block)
                        narrow[k, pl.ds(r0, row_block), :] = part(0, k, 1 - c, r0, row_block)[...].astype(BF16)
                        return carry
                    lax.fori_loop(0, halves[0] // row_block, to_bf16, 0)
                    swap_win(k).start()
                def chip_sum(w, k, r0):
                    blk = pl.ds(r0, row_block)
                    return part(w, k, c, r0, row_block)[...] + ras[w][k, blk, :].astype(F32)

                for w in range(1, ns):
                    swap_other(w).wait_recv()

                    def sums(i, carry, w=w):
                        r0 = pl.multiple_of(i * row_block, row_block)
                        for m in range(3):
                            sends[w][m, pl.ds(r0, row_block), :] = chip_sum(w, peer_chip[m], r0).astype(BF16)
                        owns[w][pl.ds(r0, row_block), :] = chip_sum(w, chip, r0)
                        return carry
                    lax.fori_loop(0, halves[w] // row_block, sums, 0)
                for k in range(N_CHIPS):
                    swap_win(k).wait_recv()

                    @pl.when(chip == k)
                    def _(k=k):
                        def own(i, carry):
                            r0 = pl.multiple_of(i * row_block, row_block)
                            owns[0][pl.ds(r0, row_block), :] = chip_sum(0, k, r0)
                            return carry
                        lax.fori_loop(0, halves[0] // row_block, own, 0)

                    @pl.when(chip != k)
                    def _(k=k):
                        def other(i, carry):
                            r0 = pl.multiple_of(i * row_block, row_block)
                            sends[0][(k ^ chip) - 1, pl.ds(r0, row_block), :] = chip_sum(0, k, r0).astype(BF16)
                            return carry
                        lax.fori_loop(0, halves[0] // row_block, other, 0)
                for k in range(N_CHIPS):
                    swap_win(k).wait_send()
                for w in range(1, ns):
                    swap_other(w).wait_send()

    tile = lambda w: pl.BlockSpec((tm, w), lambda i: (i, 0))
    vmem = pl.BlockSpec(memory_space=pltpu.VMEM)
    in_specs = [tile(W), tile(W), tile(W), tile(REST_COLS), tile(D_MODEL), pl.BlockSpec((1, D_MODEL), lambda i: (0, 0))]
    if not fused:
        return pl.pallas_call(
            body, grid=(nt,), in_specs=in_specs,
            out_specs=pl.BlockSpec((IN_COLS, D_MODEL), lambda i: (0, 0)),
            out_shape=jax.ShapeDtypeStruct((IN_COLS, D_MODEL), F32),
            compiler_params=_params(("arbitrary",)), name="inproj_bwd_dw")(dq, dk, dv, drest, x2d, g_norm)
    outs = pl.pallas_call(
        body, grid=(nt,), in_specs=in_specs + [vmem] * (ns - 1), out_specs=[vmem] * (2 * ns),
        out_shape=[jax.ShapeDtypeStruct((3, n, cl), BF16) for n, cl in zip(halves, cols)]
        + [jax.ShapeDtypeStruct((n, cl), F32) for n, cl in zip(halves, cols)],
        scratch_shapes=[pltpu.VMEM((IN_COLS, D_MODEL), F32)]
        + [pltpu.VMEM((N_CHIPS, n, cl), BF16 if w == 0 else F32) for w, (n, cl) in enumerate(zip(halves, cols))]
        + [pltpu.VMEM((N_CHIPS, halves[0], D_MODEL), BF16)]
        + [pltpu.SemaphoreType.DMA((N_CHIPS - 1 + ns,)), pltpu.SemaphoreType.DMA((N_CHIPS - 1 + ns,))],
        compiler_params=_params(("arbitrary",)), name="inproj_bwd_dw_reduce")(
            dq, dk, dv, drest, x2d, g_norm, *others)
    return outs[:ns], outs[ns:]


def _adamw_update(w, g, m, v):
    nm = ADAM_B1 * m + (1.0 - ADAM_B1) * g
    nv = ADAM_B2 * v + (1.0 - ADAM_B2) * (g * g)
    m_hat = nm / (1.0 - ADAM_B1 ** ADAM_STEP)
    v_hat = nv / (1.0 - ADAM_B2 ** ADAM_STEP)
    return -ADAM_LR * (m_hat / (jnp.sqrt(v_hat) + ADAM_EPS) + ADAM_WD * w), nm, nv


def _adamw(w, g, m, v, name):
    R, C = w.shape
    br = max(r for r in range(8, 257, 8) if R % r == 0)

    def body(w_ref, g_ref, m_ref, v_ref, g_out, d_ref, nm_ref, nv_ref):
        g = g_ref[...]
        g_out[...] = g
        d_ref[...], nm_ref[...], nv_ref[...] = _adamw_update(w_ref[...], g, m_ref[...], v_ref[...])

    spec = pl.BlockSpec((br, C), lambda i: (i, 0))
    return pl.pallas_call(
        body, grid=(R // br,), in_specs=[spec] * 4, out_specs=[spec] * 4,
        out_shape=[jax.ShapeDtypeStruct((R, C), F32)] * 4,
        compiler_params=_params(("arbitrary",)), name=name)(w, g, m, v)


def _adamw_rest(g_packed, ws, ms, vs, whole):
    n = len(ws)
    nw = len(whole)
    row_block = 64

    def body(*refs):
        g_ref = refs[0]
        w_refs, m_refs, v_refs = refs[1:1 + n], refs[1 + n:1 + 2 * n], refs[1 + 2 * n:1 + 3 * n]
        whole_in = [refs[1 + 3 * n + 4 * i:5 + 3 * n + 4 * i] for i in range(nw)]
        outs = refs[1 + 3 * n + 4 * nw:]
        off = 0
        for i, (_, used, padded) in enumerate(_SMALL_PARTS[:n]):
            g = g_ref[off:off + used, :]
            delta, nm, nv = _adamw_update(w_refs[i][...], g, m_refs[i][...], v_refs[i][...])
            outs[4 * i][...], outs[4 * i + 1][...], outs[4 * i + 2][...], outs[4 * i + 3][...] = g, delta, nm, nv
            off += padded
        for i, (w_ref, gw_ref, m_ref, v_ref) in enumerate(whole_in):
            for r0 in range(0, w_ref.shape[0], row_block):
                blk = pl.ds(r0, row_block)
                g = gw_ref[blk, :]
                new = _adamw_update(w_ref[blk, :], g, m_ref[blk, :], v_ref[blk, :])
                for ref, val in zip(outs[4 * (n + i):4 * (n + i) + 4], (g,) + new):
                    ref[blk, :] = val
        outs[4 * (n + nw)][...] = g_ref[_LOSS_ROW:_LOSS_ROW + 1, 0:1]

    outs = pl.pallas_call(
        body, out_shape=[jax.ShapeDtypeStruct(w.shape, F32) for w in ws for _ in range(4)]
        + [jax.ShapeDtypeStruct(four[0].shape, F32) for four in whole for _ in range(4)]
        + [jax.ShapeDtypeStruct((1, 1), F32)],
        compiler_params=_params(), name="adamw_rest")(g_packed, *ws, *ms, *vs, *[a for four in whole for a in four])
    return ([outs[4 * i:4 * i + 4] for i in range(n)], [outs[4 * (n + i):4 * (n + i) + 4] for i in range(nw)],
            outs[4 * (n + nw)])


def _place():
    x, y, c = lax.axis_index("x"), lax.axis_index("y"), lax.axis_index("c")
    chip = 2 * x + y
    peers = [(x, 1 - y), (1 - x, y), (1 - x, 1 - y)]
    peer_chip = [2 * px + py for px, py in peers]
    return x, y, c, chip, peers, peer_chip


def _remote(src, dst, send_sem, recv_sem, dev):
    return pltpu.make_async_remote_copy(src_ref=src, dst_ref=dst, send_sem=send_sem, recv_sem=recv_sem,
                                        device_id=dev, device_id_type=MESH)


def _ag_weights(weights, late=()):
    nw, nl = len(weights), len(late)
    pieces = 2

    def body(*refs):
        srcs, late_srcs = refs[:nw], refs[nw:nw + nl]
        outs, late_bf, late_land = (refs[nw + nl:2 * nw + nl], refs[2 * nw + nl:2 * nw + 2 * nl],
                                    refs[2 * nw + 2 * nl:2 * nw + 3 * nl])
        s_ici, r_ici, s_d2d, r_d2d = refs[2 * nw + 3 * nl:]
        x, y, c = lax.axis_index("x"), lax.axis_index("y"), lax.axis_index("c")
        chip = 2 * x + y
        sib = (x, y, 1 - c)
        first = ((x + 1 - c) % 2, (y + c) % 2)
        second = ((x + c) % 2, (y + 1 - c) % 2)
        first_chip, second_chip = 2 * first[0] + first[1], 2 * second[0] + second[1]
        diag_chip = 3 - chip
        for src, out in zip(srcs, outs):
            out[chip] = src[...].astype(BF16)

        parts = [(w, out, pc) for w, out in enumerate(outs) for pc in range(pieces)]

        def piece(out, k, cc, pc):
            rows = out.shape[1] // 2 // pieces
            return out.at[k, pl.ds(pl.multiple_of((cc * pieces + pc) * rows, 16), rows), :]

        def ici(w, slot, out, k, dev, pc):
            blk, sem = piece(out, k, c, pc), (nw * slot + w) * pieces + pc
            return _remote(blk, blk, s_ici.at[sem], r_ici.at[sem], (dev[0], dev[1], c))

        def d2d(w, slot, out, k, cc, pc):
            blk, sem = piece(out, k, cc, pc), (nw * slot + w) * pieces + pc
            return _remote(blk, blk, s_d2d.at[sem], r_d2d.at[sem], sib)

        sent = []
        for slot, dev in enumerate((first, second)):
            for w, out, pc in parts:
                sent.append(ici(w, slot, out, chip, dev, pc))
                sent[-1].start()
        for src, bf, land in zip(late_srcs, late_bf, late_land):
            bf[...] = src[...].astype(BF16)
            land[...] = jnp.zeros_like(land)
            land[chip] = bf[...]
        for slot, k, dev in ((0, first_chip, first), (1, second_chip, second), (2, diag_chip, second)):
            for w, out, pc in parts:
                ici(w, slot, out, k, dev, pc).wait_recv()
                if slot == 0:
                    sent.append(ici(w, 2, out, k, second, pc))
                    sent[-1].start()
                sent.append(d2d(w, slot, out, k, c, pc))
                sent[-1].start()
        for slot, k in ((0, second_chip), (1, first_chip), (2, diag_chip)):
            for w, out, pc in parts:
                d2d(w, slot, out, k, 1 - c, pc).wait_recv()
        for cp in sent:
            cp.wait_send()

    vmem = pl.BlockSpec(memory_space=pltpu.VMEM)
    outs = pl.pallas_call(
        body,
        out_shape=[jax.ShapeDtypeStruct((N_CHIPS,) + w.shape, BF16) for w in weights]
        + [jax.ShapeDtypeStruct(w.shape, BF16) for w in late]
        + [jax.ShapeDtypeStruct((N_CHIPS,) + w.shape, BF16) for w in late],
        in_specs=[vmem] * (nw + nl), out_specs=[vmem] * (nw + 2 * nl),
        scratch_shapes=[pltpu.SemaphoreType.DMA((3 * nw * pieces,))] * 4,
        compiler_params=pltpu.CompilerParams(vmem_limit_bytes=VMEM_LIMIT), name="ag_weights")(*weights, *late)
    return outs[:nw], outs[nw:nw + nl], outs[nw + nl:]


_HBM = pl.BlockSpec(memory_space=pltpu.HBM)
_SEM = pl.BlockSpec(memory_space=pltpu.SEMAPHORE)
_ANY = pl.BlockSpec(memory_space=pl.ANY)
_DATAFLOW = pltpu.SideEffectType.DATAFLOW_SIDE_EFFECTING


def _in_hbm(a):
    return pltpu.with_memory_space_constraint(a, pltpu.HBM)


_PEERS_OF = {"gather": 3, "scatter": 3, "direct": 7}


def _exchange_copies(mode, srcs, lands, send_sems, recv_sems):
    nw = len(srcs)
    x, y, c, chip, peers, peer_chip = _place()
    pairs = []
    if mode == "direct":
        targets = [((x, y), chip, 1)] + [(p, k, d) for p, k in zip(peers, peer_chip) for d in (0, 1)]
        for r, ((px, py), k, d) in enumerate(targets):
            core = (c + d) % 2
            for w in range(nw):
                n = srcs[w].shape[1] // 2
                src = srcs[w].at[k, pl.ds(pl.multiple_of(core * n, 16), n), :]
                sems = (send_sems.at[nw * r + w], recv_sems.at[nw * r + w], (px, py, core))
                pairs.append((_remote(src, lands[w].at[r], *sems),) * 2)
        return pairs
    gather = mode == "gather"
    for m, (px, py) in enumerate(peers):
        for w in range(nw):
            sems = (send_sems.at[nw * m + w], recv_sems.at[nw * m + w], (px, py, c))
            if gather:
                pairs.append((_remote(srcs[w], lands[w].at[chip], *sems),
                              _remote(srcs[w], lands[w].at[peer_chip[m]], *sems)))
            else:
                pairs.append((_remote(srcs[w].at[m], lands[w].at[m], *sems),) * 2)
    return pairs


def _exchange_start(mode, srcs, after, name, lands=None):
    nw = len(srcs)
    n_copies = _PEERS_OF[mode] * nw

    def body(*refs):
        send_sems, recv_sems = refs[2 * nw + 1], refs[2 * nw + 2]
        for start, _ in _exchange_copies(mode, refs[:nw], refs[nw:2 * nw], send_sems, recv_sems):
            start.start()
        refs[-1][...] = jnp.zeros_like(refs[-1])

    if lands is None:
        shape = {"gather": lambda s: (N_CHIPS,) + s.shape, "scatter": lambda s: s.shape,
                 "direct": lambda s: (_PEERS_OF["direct"], s.shape[1] // 2, s.shape[2])}[mode]
        lands = [lax.empty(shape(s), s.dtype) for s in srcs]
    lands = [_in_hbm(l) for l in lands]
    return pl.pallas_call(
        body, name=name,
        out_shape=(pltpu.SemaphoreType.DMA((n_copies,)), pltpu.SemaphoreType.DMA((n_copies,)))
        + tuple(pltpu.HBM(s.shape, s.dtype) for s in srcs)
        + tuple(pltpu.HBM(l.shape, l.dtype) for l in lands)
        + (jax.ShapeDtypeStruct((8, LANES), F32),),
        in_specs=[_HBM] * (2 * nw) + [_ANY],
        out_specs=(_SEM, _SEM) + (_HBM,) * (2 * nw) + (pl.BlockSpec(memory_space=pltpu.VMEM),),
        input_output_aliases={i: 2 + i for i in range(2 * nw)},
        compiler_params=pltpu.CompilerParams(has_side_effects=_DATAFLOW),
    )(*[_in_hbm(s) for s in srcs], *lands, after)


def _exchange_wait(mode, started, after, name):
    nw = (len(started) - 3) // 2
    send_sems, recv_sems = started[0], started[1]
    thru = started[2:2 + 2 * nw]

    def body(*refs):
        for _, arrival in _exchange_copies(mode, refs[:nw], refs[nw:2 * nw], refs[2 * nw], refs[2 * nw + 1]):
            arrival.wait_send()
            arrival.wait_recv()

    outs = pl.pallas_call(
        body, name=name,
        out_shape=tuple(pltpu.HBM(t.shape, t.dtype) for t in thru),
        in_specs=[_HBM] * (2 * nw) + [_SEM, _SEM, _ANY], out_specs=(_HBM,) * (2 * nw),
        input_output_aliases={i: i for i in range(2 * nw)},
        compiler_params=pltpu.CompilerParams(has_side_effects=_DATAFLOW),
    )(*thru, send_sems, recv_sems, after)
    return outs[:nw], outs[nw:]


def _reduce_last(owns, landed, g_small, direct_srcs=(), direct_landed=()):
    ns, nd = len(owns), len(direct_srcs)
    halves = [o.shape[0] for o in owns] + [s.shape[1] // 2 for s in direct_srcs]
    row_block = 32
    hs = SMALL_ROWS // 2

    def body(*refs):
        own_refs, land_refs, gsm_ref = refs[:ns], refs[ns:2 * ns], refs[2 * ns]
        dsrc_refs, dland_refs = refs[2 * ns + 1:2 * ns + 1 + nd], refs[2 * ns + 1 + nd:2 * ns + 1 + 2 * nd]
        n_in = 2 * ns + 1 + 2 * nd
        out_refs, osm_ref = refs[n_in:n_in + ns + nd], refs[n_in + ns + nd]
        ra_sm, p_sm, s_sem, r_sem, sm_s, sm_r = refs[n_in + ns + nd + 1:]
        x, y, c, chip, peers, peer_chip = _place()
        sib = (x, y, 1 - c)
        half = lambda cc: pl.ds(pl.multiple_of(cc * hs, 8), hs)
        sm_a = _remote(gsm_ref.at[half(1 - c), :], ra_sm, sm_s.at[0], sm_r.at[0], sib)
        sm_a.start()
        swaps = [sm_a]
        for w in range(ns + nd):
            n = halves[w]

            def total(i, carry, w=w, n=n):
                r0 = pl.multiple_of(i * row_block, row_block)
                blk = pl.ds(r0, row_block)
                mine = pl.ds(pl.multiple_of(c * n + r0, 16), row_block)
                if w < ns:
                    acc = own_refs[w][blk, :]
                    terms = [land_refs[w][m, blk, :] for m in range(_PEERS_OF["scatter"])]
                else:
                    acc = dsrc_refs[w - ns][chip, mine, :].astype(F32)
                    terms = [dland_refs[w - ns][r, blk, :] for r in range(_PEERS_OF["direct"])]
                for term in terms:
                    acc = acc + term.astype(F32)
                out_refs[w][mine, :] = acc
                return carry
            lax.fori_loop(0, n // row_block, total, 0)
            mine = out_refs[w].at[pl.ds(pl.multiple_of(c * n, 8), n), :]
            swaps.append(_remote(mine, mine, s_sem.at[w], r_sem.at[w], sib))
            swaps[-1].start()
        sm_a.wait_recv()
        p_sm[chip] = gsm_ref[half(c), :] + ra_sm[...]
        for m, (px, py) in enumerate(peers):
            swaps.append(_remote(p_sm.at[chip], p_sm.at[chip], sm_s.at[1 + m], sm_r.at[1 + m], (px, py, c)))
            swaps[-1].start()
        for m, (px, py) in enumerate(peers):
            _remote(p_sm.at[chip], p_sm.at[peer_chip[m]], sm_s.at[1 + m], sm_r.at[1 + m], (px, py, c)).wait_recv()
        osm_ref[half(c), :] = (p_sm[0] + p_sm[1]) + (p_sm[2] + p_sm[3])
        swaps.append(_remote(osm_ref.at[half(c), :], osm_ref.at[half(c), :], sm_s.at[4], sm_r.at[4], sib))
        swaps[-1].start()
        for w in range(ns + nd):
            n = halves[w]
            theirs = out_refs[w].at[pl.ds(pl.multiple_of((1 - c) * n, 8), n), :]
            _remote(theirs, theirs, s_sem.at[w], r_sem.at[w], sib).wait_recv()
        _remote(osm_ref.at[half(1 - c), :], osm_ref.at[half(1 - c), :], sm_s.at[4], sm_r.at[4], sib).wait_recv()
        for cp in swaps:
            cp.wait_send()

    vmem = pl.BlockSpec(memory_space=pltpu.VMEM)
    return pl.pallas_call(
        body, out_shape=[jax.ShapeDtypeStruct((2 * o.shape[0], o.shape[1]), F32) for o in owns]
        + [jax.ShapeDtypeStruct(s.shape[1:], F32) for s in direct_srcs]
        + [jax.ShapeDtypeStruct((SMALL_ROWS, LANES), F32)],
        in_specs=[vmem] * (2 * ns + 1 + 2 * nd), out_specs=[vmem] * (ns + nd + 1),
        scratch_shapes=[pltpu.VMEM((hs, LANES), F32), pltpu.VMEM((N_CHIPS, hs, LANES), F32),
                        pltpu.SemaphoreType.DMA((ns + nd,)), pltpu.SemaphoreType.DMA((ns + nd,)),
                        pltpu.SemaphoreType.DMA((5,)), pltpu.SemaphoreType.DMA((5,))],
        compiler_params=pltpu.CompilerParams(vmem_limit_bytes=VMEM_LIMIT),
        name="reduce_last")(*owns, *landed, g_small, *direct_srcs, *direct_landed)


_SMALL_PARTS = (("g_norm", 8, 8), ("w_s", 512, 512), ("b_s", 4, 8), ("g_v", 2, 8), ("g_mem", 8, 8),
                ("g_final", 8, 8), ("loss", 8, 8))
_LOSS_ROW = SMALL_ROWS - 8
assert sum(p for _, _, p in _SMALL_PARTS) == SMALL_ROWS


def _pack_small(parts, loss_block):
    rows = []
    for (name, used, padded), p in zip(_SMALL_PARTS, list(parts) + [loss_block]):
        p = p.reshape(used, LANES)
        if padded > used:
            p = jnp.pad(p, ((0, padded - used), (0, 0)))
        rows.append(p)
    return jnp.concatenate(rows, axis=0)


def _local_step(x, mem, target, g_norm, w_in, w_s, b_s, g_v, g_mem, late_weights, g_final,
                fwd_token=None, on_late=None, on_dw=None):
    B, S, _ = x.shape
    x2d = x.reshape(B * S, D_MODEL)
    t2d = target.reshape(B * S, D_MODEL)
    mem2d = mem.reshape(B * N_MEM, D_MODEL)

    proj = _inproj_fwd(x2d, g_norm, w_in, after=() if fwd_token is None else (fwd_token,))
    w_kv, w_out = late_weights(proj)
    kv = _kv_fwd(mem2d, g_mem, w_kv)
    a, lse = _attn_fwd(proj, B, S)
    w_sT = jnp.swapaxes(w_s, 1, 2)
    b_tab = jnp.repeat(b_s.T, HEAD_DIM, axis=1)
    (dx2, da, drest, loss, d_wout, d_ws, d_bs, d_gv, d_gf, dkv) = _mid(
        x2d, t2d, a, proj, kv, w_s, w_sT, b_tab, g_v, w_out, g_final, B, S)
    d_wkv, d_gmem = _kv_bwd(mem2d, g_mem, w_kv, dkv)
    dq, dk, dv = _attn_bwd(proj, a, lse, da, B, S, after=() if on_late is None else (on_late(d_wkv, d_wout),))
    if on_dw is None:
        d_win = _inproj_bwd_dw(dq, dk, dv, drest, x2d, g_norm)
        after = ()
    else:
        d_win = None
        after = (on_dw(*_inproj_bwd_dw(dq, dk, dv, drest, x2d, g_norm, reduce_with=[])),)
    grad_x, d_gnorm = _inproj_bwd_dx(dq, dk, dv, drest, x2d, dx2, g_norm, w_in, after=after)
    d_bs = d_bs[:, :N_SGU_GROUPS].T
    return (loss, grad_x.reshape(B, S, D_MODEL),
            dict(g_norm=d_gnorm, w_in=d_win, w_s=d_ws, b_s=d_bs, g_v=d_gv, g_mem=d_gmem, w_kv=d_wkv,
                 w_out=d_wout, g_final=d_gf))


def kernel(x, mem, g_norm, w_in, w_sgu_spatial, b_sgu_spatial, g_sgu_v, g_mem, w_mem_kv, w_out, g_final, loss_target, m_g_norm, m_w_in, m_w_sgu_spatial, m_b_sgu_spatial, m_g_sgu_v, m_g_mem, m_w_mem_kv, m_w_out, m_g_final, v_g_norm, v_w_in, v_w_sgu_spatial, v_b_sgu_spatial, v_g_sgu_v, v_g_mem, v_w_mem_kv, v_w_out, v_g_final):
    t = lambda w: jnp.swapaxes(w[0], 0, 1)
    (win_all,), late_shards, late_lands = _ag_weights([t(w_in)], [w_mem_kv[0], w_out[0]])
    w_in_full = win_all.reshape(-1, win_all.shape[-1])
    late = _exchange_start("gather", list(late_shards), win_all, "gather_late_start", lands=late_lands)

    def late_weights(proj):
        return [z.reshape(-1, z.shape[-1]) for z in _exchange_wait("gather", late, proj, "gather_late_wait")[1]]

    scatter = {}
    by_chip = lambda grad: grad.reshape((N_CHIPS, grad.shape[0] // N_CHIPS, grad.shape[1]))

    def on_late(d_wkv, d_wout):
        scatter["late"] = _exchange_start("direct", [by_chip(d_wkv), by_chip(d_wout)], d_wkv, "scatter_late_start")
        return scatter["late"][-1]

    def on_dw(sends, owns):
        scatter["own"] = owns
        scatter["started"] = _exchange_start("scatter", list(sends), owns[0], "scatter_start")
        return scatter["started"][-1]

    loss, grad_x, g = _local_step(
        x, mem, loss_target, g_norm, w_in_full, w_sgu_spatial[0], b_sgu_spatial[0], g_sgu_v, g_mem,
        late_weights, g_final.reshape(1, D_MODEL), fwd_token=late[-1], on_late=on_late, on_dw=on_dw)

    small_names = ("g_norm", "w_s", "b_s", "g_v", "g_mem", "g_final")
    g_small = _pack_small([g[n] for n in small_names], loss)
    late_srcs, late_landed = _exchange_wait("direct", scatter["late"], g_small, "scatter_late_wait")
    _, landed = _exchange_wait("scatter", scatter["started"], late_landed[0], "scatter_wait")
    gr_in, gr_kv, gr_out, gr_small = _reduce_last(scatter["own"], landed, g_small, late_srcs, late_landed)

    small_w = (g_norm, w_sgu_spatial, b_sgu_spatial, g_sgu_v, g_mem, g_final)
    small_m = (m_g_norm, m_w_sgu_spatial, m_b_sgu_spatial, m_g_sgu_v, m_g_mem, m_g_final)
    small_v = (v_g_norm, v_w_sgu_spatial, v_b_sgu_spatial, v_g_sgu_v, v_g_mem, v_g_final)
    rows = lambda ws: [w.reshape(-1, LANES) for w in ws]
    small_new, ((gr_kv, d_kv, nm_kv, nv_kv), (gr_out, d_out, nm_out, nv_out)), loss = _adamw_rest(
        gr_small, rows(small_w), rows(small_m), rows(small_v),
        [(w_mem_kv[0], gr_kv, m_w_mem_kv[0], v_w_mem_kv[0]), (w_out[0], gr_out, m_w_out[0], v_w_out[0])])
    loss = loss.reshape(())
    small = [[z.reshape(w.shape) for z in four] for w, four in zip(small_w, small_new)]
    gr_in, d_in, nm_in, nv_in = [jnp.swapaxes(z, 0, 1)
                                 for z in _adamw(t(w_in), gr_in, t(m_w_in), t(v_w_in), "adamw_w_in")]

    def leaves(kind, big_in, big_kv, big_out):
        s_norm, s_ws, s_bs, s_gv, s_gmem, s_gf = [four[kind] for four in small]
        return [s_norm, big_in[None], s_ws, s_bs, s_gv, s_gmem, big_kv[None], big_out[None], s_gf]

    return (loss, grad_x, *leaves(0, gr_in, gr_kv, gr_out), *leaves(1, d_in, d_kv, d_out),
            *leaves(2, nm_in, nm_kv, nm_out), *leaves(3, nv_in, nv_kv, nv_out))
```

```python
import functools

import jax
import jax.numpy as jnp
from jax import lax
from jax.experimental import pallas as pl
from jax.experimental.pallas import tpu as pltpu

F32 = jnp.float32
BF16 = jnp.bfloat16
MESH = pl.DeviceIdType.MESH

D_MODEL = 1024
ATTN_WIDTH = 512
SGU_WIDTH = 256
MEM_WIDTH = 256
N_MEM = 256
IN_COLS = 3328
QKV_COLS = 3 * ATTN_WIDTH
REST_COLS = IN_COLS - QKV_COLS
SGU_CHUNK = 128
N_SGU_GROUPS = 4
EPS = 1e-6
NEG_INF = -1e30
DILATIONS = (1, 4, 16)
RADIUS = 64
Q_BLOCK = 128
LANES = 128
HEAD_DIM = 64

ADAM_LR = 0.001
ADAM_B1 = 0.9
ADAM_B2 = 0.999
ADAM_EPS = 1e-08
ADAM_WD = 0.01
ADAM_STEP = 10

N_CHIPS = 4
VMEM_LIMIT = 56 * 1024 * 1024
SMALL_ROWS = 560


def _params(sem=None, vmem=VMEM_LIMIT):
    return pltpu.CompilerParams(dimension_semantics=sem, vmem_limit_bytes=vmem)


def _nn(a, b):
    return jnp.dot(a, b, preferred_element_type=F32)


def _nt(a, b):
    return lax.dot_general(a, b, (((1,), (1,)), ((), ())), preferred_element_type=F32)


def _tn(a, b):
    return lax.dot_general(a, b, (((0,), (0,)), ((), ())), preferred_element_type=F32)


def _rms(x):
    r = lax.rsqrt(jnp.mean(x * x, axis=-1, keepdims=True) + EPS)
    return r, x * r


def _head_masks():
    lane = lax.broadcasted_iota(jnp.int32, (1, LANES), 1)
    lo = lane < HEAD_DIM
    return lo, (lo.astype(F32), (~lo).astype(F32))


def _silu_parts(z):
    s = jax.nn.sigmoid(z)
    return z * s, s * (1.0 + z * (1.0 - s))


def _gelu_parts(x):
    c = 0.7978845608028654
    x2 = x * x
    s = jax.nn.sigmoid((2.0 * c) * (x + 0.044715 * (x * x2)))
    return x * s, s * (1.0 + x * (1.0 - s) * ((2.0 * c) * (1.0 + 3.0 * 0.044715 * x2)))


def _after(tokens):
    return [pl.BlockSpec(memory_space=pl.ANY)] * len(tokens)


def _inproj_fwd(x2d, g_norm, w_in_t, after=()):
    T = x2d.shape[0]
    tm = 512

    def body(x_ref, g_ref, w_ref, *rest):
        o_ref = rest[-1]
        _, xh = _rms(x_ref[...])
        h = (xh * g_ref[...]).astype(BF16)
        o_ref[...] = _nt(h, w_ref[...])

    return pl.pallas_call(
        body, grid=(T // tm,),
        in_specs=[pl.BlockSpec((tm, D_MODEL), lambda i: (i, 0)),
                  pl.BlockSpec((1, D_MODEL), lambda i: (0, 0)),
                  pl.BlockSpec((IN_COLS, D_MODEL), lambda i: (0, 0))] + _after(after),
        out_specs=pl.BlockSpec((tm, IN_COLS), lambda i: (i, 0)),
        out_shape=jax.ShapeDtypeStruct((T, IN_COLS), F32),
        compiler_params=_params(("arbitrary",)), name="inproj_fwd")(x2d, g_norm, w_in_t, *after)


def _kv_fwd(mem2d, g_mem, w_kv):
    Tm = mem2d.shape[0]

    def body(m_ref, g_ref, w_ref, o_ref):
        _, mh = _rms(m_ref[...])
        o_ref[...] = _nn((mh * g_ref[...]).astype(BF16), w_ref[...])

    return pl.pallas_call(
        body, out_shape=jax.ShapeDtypeStruct((Tm, 2 * MEM_WIDTH), F32),
        compiler_params=_params(), name="kv_fwd")(mem2d, g_mem, w_kv)


def _kv_bwd(mem2d, g_mem, w_kv, dkv):
    Tm = mem2d.shape[0]

    def body(m_ref, g_ref, w_ref, dkv_ref, dw_ref, dg_ref):
        _, mh = _rms(m_ref[...])
        memn = (mh * g_ref[...]).astype(BF16)
        dkvb = dkv_ref[...].astype(BF16)
        dw = _tn(memn, dkvb).astype(BF16)
        for k in range(N_CHIPS):
            dw_ref[k] = dw[k * (D_MODEL // N_CHIPS):(k + 1) * (D_MODEL // N_CHIPS), :]
        dmemn = _nt(dkvb, w_ref[...])
        dg_ref[...] = jnp.sum(dmemn * mh, axis=0, keepdims=True)

    return pl.pallas_call(
        body, out_shape=(jax.ShapeDtypeStruct((N_CHIPS, D_MODEL // N_CHIPS, 2 * MEM_WIDTH), BF16),
                         jax.ShapeDtypeStruct((1, D_MODEL), F32)),
        compiler_params=_params(), name="kv_bwd")(mem2d, g_mem, w_kv, dkv)


def _attn_geometry(S):
    geom = []
    for d in DILATIONS:
        L = S // d
        assert L % Q_BLOCK == 0
        geom.append((d, L, min(2 * Q_BLOCK, L), L // Q_BLOCK))
    return geom


def _init_bias(bias_scr, geom, hp):
    row = lax.broadcasted_iota(jnp.int32, (Q_BLOCK, 2 * Q_BLOCK), 0)
    col = lax.broadcasted_iota(jnp.int32, (Q_BLOCK, 2 * Q_BLOCK), 1)
    for j in (0, 1):
        bits = (126 - (2 * hp + j)) * (1 << 23)
        slope = lax.bitcast_convert_type(jnp.full((1, 1), bits, jnp.int32), F32)
        for di, (d, _, _, _) in enumerate(geom):
            for cls, off in enumerate((0, -RADIUS, -2 * RADIUS)):
                dist = jnp.abs(col - row + off)
                bias_scr[di * 6 + cls * 2 + j] = jnp.where(
                    dist <= RADIUS, -(slope * float(d)) * dist.astype(F32), NEG_INF)


SPLIT = 4
COPY_ROWS = 256


def _by4_rows(S, step):
    per_class = S // SPLIT // COPY_ROWS
    r, j = step // per_class, step % per_class
    return (pl.ds(r + SPLIT * j * COPY_ROWS, COPY_ROWS, stride=SPLIT),
            pl.ds(r * (S // SPLIT) + j * COPY_ROWS, COPY_ROWS))


def _to_by4(src, dst, S):
    for i in range(S // COPY_ROWS):
        natural, by4 = _by4_rows(S, i)
        dst[by4, :] = src[natural, :]


def _block_slices(d, L, KW, nqb, r, qb, S):
    qs = qb * Q_BLOCK
    ks = jnp.clip(qs - RADIUS, 0, L - KW)
    cls = jnp.where(qb == 0, 0, jnp.where(qb == nqb - 1, 2, 1))
    if d == 1:
        qsl = pl.ds(pl.multiple_of(qs, Q_BLOCK), Q_BLOCK)
        ksl = pl.ds(pl.multiple_of(ks, RADIUS), KW)
    elif d == SPLIT:
        qsl = pl.ds(pl.multiple_of(r * L + qs, Q_BLOCK), Q_BLOCK)
        ksl = pl.ds(pl.multiple_of(r * L + ks, RADIUS), KW)
    else:
        sub = d // SPLIT
        base = (r % SPLIT) * (S // SPLIT) + r // SPLIT
        qsl = pl.ds(base + qs * sub, Q_BLOCK, stride=sub)
        ksl = pl.ds(base + ks * sub, KW, stride=sub)
    return qsl, ksl, cls


def _for_groups(geom, S, group, fn):
    for di, (d, L, KW, nqb) in enumerate(geom):
        n = group[di]
        assert (d * nqb) % n == 0

        def step(it, carry, di=di, d=d, L=L, KW=KW, nqb=nqb, n=n):
            slices = []
            for g in range(n):
                i = it * n + g
                slices.append(_block_slices(d, L, KW, nqb, i // nqb, i % nqb, S))
            fn(di, KW, slices)
            return carry
        lax.fori_loop(0, d * nqb // n, step, 0)


def _attn_fwd(proj, B, S):
    T = B * S
    geom = _attn_geometry(S)
    n_pairs = ATTN_WIDTH // LANES

    def body(q_ref, k_ref, v_ref, a_ref, lse_ref, bias_scr, q4, k4, v4, *per_dilation):
        o_scr, m_scr, l_scr = per_dilation[0:3], per_dilation[3:6], per_dilation[6:9]
        lo, hm = _head_masks()
        pair = pl.program_id(0)

        @pl.when(pl.program_id(1) == 0)
        def _():
            _init_bias(bias_scr, geom, pair)
        for src, dst in ((q_ref, q4), (k_ref, k4), (v_ref, v4)):
            _to_by4(src, dst, S)

        def group(di, KW, all_slices):
            run = 8
            for first in range(0, len(all_slices), run):
                some(di, KW, all_slices[first:first + run])

        def some(di, KW, slices):
            chains = [(g, j) for g in range(len(slices)) for j in (0, 1)]
            q_src, k_src, v_src = (q_ref, k_ref, v_ref) if di == 0 else (q4, k4, v4)
            q = [q_src[qsl, :] for qsl, _, _ in slices]
            kw = [k_src[ksl, :].astype(BF16) for _, ksl, _ in slices]
            vw = [v_src[ksl, :].astype(BF16) for _, ksl, _ in slices]
            s = {(g, j): _nt((q[g] * (hm[j] * 0.125)).astype(BF16), kw[g])
                 + bias_scr[di * 6 + slices[g][2] * 2 + j, :, pl.ds(0, KW)] for g, j in chains}
            m = {c: jnp.max(s[c], axis=1, keepdims=True) for c in chains}
            p = {c: jnp.exp(s[c] - m[c]) for c in chains}
            l = {c: jnp.sum(p[c], axis=1, keepdims=True) for c in chains}
            o = {(g, j): _nn(p[(g, j)].astype(BF16), vw[g]) for g, j in chains}
            for g, (qsl, _, _) in enumerate(slices):
                o_scr[di][qsl, :] = jnp.where(lo, o[(g, 0)], o[(g, 1)])
                m_scr[di][qsl, :] = jnp.where(lo, m[(g, 0)], m[(g, 1)])
                l_scr[di][qsl, :] = jnp.where(lo, l[(g, 0)], l[(g, 1)])

        _for_groups(geom, S, (16, 16, 16), group)

        for i in range(S // COPY_ROWS):
            natural, by4 = _by4_rows(S, i)
            rows = [natural, by4, by4]
            ms = [m_scr[di][rows[di], :] for di in range(3)]
            mx = jnp.maximum(jnp.maximum(ms[0], ms[1]), ms[2])
            num = 0.0
            den = 0.0
            for di in range(3):
                w = jnp.exp(ms[di] - mx)
                num = num + w * o_scr[di][rows[di], :]
                den = den + w * l_scr[di][rows[di], :]
            a_ref[natural, :] = num / den
            lse_ref[natural, :] = mx + jnp.log(den)

    blk = lambda off: pl.BlockSpec((S, LANES), lambda h, b, off=off: (b, off + h))
    out_blk = pl.BlockSpec((S, LANES), lambda h, b: (b, h))
    return pl.pallas_call(
        body, grid=(n_pairs, B),
        in_specs=[blk(0), blk(n_pairs), blk(2 * n_pairs)],
        out_specs=[out_blk, out_blk],
        out_shape=[jax.ShapeDtypeStruct((T, ATTN_WIDTH), F32)] * 2,
        scratch_shapes=[pltpu.VMEM((18, Q_BLOCK, 2 * Q_BLOCK), F32)] + [pltpu.VMEM((S, LANES), F32)] * 12,
        compiler_params=_params(("arbitrary", "arbitrary")), name="attn_fwd")(proj, proj, proj)


def _attn_bwd(proj, a, lse, da, B, S, after=()):
    T = B * S
    geom = _attn_geometry(S)
    n_pairs = ATTN_WIDTH // LANES

    def body(q_ref, k_ref, v_ref, a_ref, lse_ref, do_ref, *rest):
        dq_ref, dk_ref, dv_ref, bias_scr = rest[len(after):len(after) + 4]
        scr = rest[len(after) + 4:]
        acc = (scr[0:3], scr[3:6])
        natural_in = (q_ref, k_ref, v_ref, a_ref, lse_ref, do_ref)
        by4_in = scr[6:12]
        _, hm = _head_masks()
        pair = pl.program_id(0)

        @pl.when(pl.program_id(1) == 0)
        def _():
            _init_bias(bias_scr, geom, pair)
        for ref in scr[0:6]:
            ref[...] = jnp.zeros_like(ref)
        for src, dst in zip(natural_in, by4_in):
            _to_by4(src, dst, S)

        def group(di, KW, all_slices):
            run = (4, 4, 8)[di]
            for first in range(0, len(all_slices), run):
                some(di, KW, all_slices[first:first + run])

        def some(di, KW, slices):
            n = len(slices)
            chains = [(g, j) for g in range(n) for j in (0, 1)]
            q_src, k_src, v_src, a_src, lse_src, do_src = natural_in if di == 0 else by4_in
            dq_scr, dk_scr, dv_scr = acc[0 if di == 0 else 1]
            q = [q_src[qsl, :] for qsl, _, _ in slices]
            do = [do_src[qsl, :] for qsl, _, _ in slices]
            doa = [do[g] * a_src[slices[g][0], :] for g in range(n)]
            lse_q = [lse_src[qsl, :] for qsl, _, _ in slices]
            kw = [k_src[ksl, :].astype(BF16) for _, ksl, _ in slices]
            vw = [v_src[ksl, :].astype(BF16) for _, ksl, _ in slices]
            qj = {(g, j): (q[g] * (hm[j] * 0.125)).astype(BF16) for g, j in chains}
            doj = {(g, j): (do[g] * hm[j]).astype(BF16) for g, j in chains}
            s = {(g, j): _nt(qj[(g, j)], kw[g])
                 + bias_scr[di * 6 + slices[g][2] * 2 + j, :, pl.ds(0, KW)] for g, j in chains}
            dp = {(g, j): _nt(doj[(g, j)], vw[g]) for g, j in chains}
            dsum = {(g, j): jnp.sum(doa[g] * hm[j], axis=1, keepdims=True) for g, j in chains}
            p = {(g, j): jnp.exp(s[(g, j)] - lse_q[g][:, HEAD_DIM * j:HEAD_DIM * j + 1]) for g, j in chains}
            ds = {c: (p[c] * (dp[c] - dsum[c])).astype(BF16) for c in chains}
            pb = {c: p[c].astype(BF16) for c in chains}
            dq = [_nn(ds[(g, 0)], kw[g]) * (hm[0] * 0.125) + _nn(ds[(g, 1)], kw[g]) * (hm[1] * 0.125)
                  for g in range(n)]
            both = lambda t, g: jnp.concatenate([t[(g, 0)], t[(g, 1)]], axis=0)
            dkw = [_tn(both(ds, g), both(qj, g)) for g in range(n)]
            dvw = [_tn(both(pb, g), both(doj, g)) for g in range(n)]
            for g, (qsl, ksl, _) in enumerate(slices):
                dq_scr[qsl, :] = dq_scr[qsl, :] + dq[g]
                dk_scr[ksl, :] = dk_scr[ksl, :] + dkw[g]
                dv_scr[ksl, :] = dv_scr[ksl, :] + dvw[g]

        _for_groups(geom, S, (16, 16, 16), group)

        for i in range(S // COPY_ROWS):
            natural, by4 = _by4_rows(S, i)
            for nat, split in zip(*acc):
                nat[natural, :] = nat[natural, :] + split[by4, :]
        for out, nat in zip((dq_ref, dk_ref, dv_ref), acc[0]):
            out[...] = nat[...].astype(BF16)

    blk = lambda off: pl.BlockSpec((S, LANES), lambda h, b, off=off: (b, off + h))
    return pl.pallas_call(
        body, grid=(n_pairs, B),
        in_specs=[blk(0), blk(n_pairs), blk(2 * n_pairs), blk(0), blk(0), blk(0)] + _after(after),
        out_specs=[blk(0), blk(0), blk(0)],
        out_shape=[jax.ShapeDtypeStruct((T, ATTN_WIDTH), BF16)] * 3,
        scratch_shapes=[pltpu.VMEM((18, Q_BLOCK, 2 * Q_BLOCK), F32)] + [pltpu.VMEM((S, LANES), F32)] * 12,
        compiler_params=_params(("arbitrary", "arbitrary")), name="attn_bwd")(proj, proj, proj, a, lse, da, *after)


def _mid(x2d, t2d, a, proj, kv, w_s, w_sT, b_tab, g_v, w_out, g_final, B, S):
    T = B * S
    tm = 512
    nt = S // tm
    halves = 2
    hrows = tm // halves

    def body(x_ref, t_ref, a_ref, za_ref, ub_ref, vb_ref, zb_ref, qm_ref, zm_ref, kv_ref,
              ws_ref, wsT_ref, btab_ref, gv_ref, wout_ref, gf_ref,
              dx2_ref, da_ref, drest_ref, loss_ref, dwout_bf_ref, dws_ref, dbs_ref, dgv_ref, dgf_ref, dkv_ref,
              dbtab_scr, dwout_ref):
        b = pl.program_id(0)
        t = pl.program_id(1)
        first = jnp.logical_and(b == 0, t == 0)
        last = jnp.logical_and(b == B - 1, t == nt - 1)
        _, hm = _head_masks()
        lane_g = lax.broadcasted_iota(jnp.int32, (1, SGU_WIDTH), 1) // HEAD_DIM
        gm = [(lane_g == g).astype(F32) for g in range(N_SGU_GROUPS)]
        H = range(halves)
        rows = [pl.ds(h * hrows, hrows) for h in H]
        ld = lambda ref: [ref[r, :] for r in rows]
        cat = lambda parts, axis: jnp.concatenate(parts, axis=axis)
        chunks = [slice(ci * SGU_CHUNK, (ci + 1) * SGU_CHUNK) for ci in range(hrows // SGU_CHUNK)]
        pairs = [slice(pr * LANES, (pr + 1) * LANES) for pr in range(2)]
        heads = [(pr, j) for pr in range(2) for j in (0, 1)]

        @pl.when(first)
        def _():
            loss_ref[...] = jnp.zeros_like(loss_ref)
            dwout_ref[...] = jnp.zeros_like(dwout_ref)
            dws_ref[...] = jnp.zeros_like(dws_ref)
            dbs_ref[...] = jnp.zeros_like(dbs_ref)
            dgv_ref[...] = jnp.zeros_like(dgv_ref)
            dgf_ref[...] = jnp.zeros_like(dgf_ref)
            dbtab_scr[...] = jnp.zeros_like(dbtab_scr)

        @pl.when(t == 0)
        def _():
            dkv_ref[...] = jnp.zeros_like(dkv_ref)

        a_val = ld(a_ref)
        sil_a = [_silu_parts(z) for z in ld(za_ref)]
        gated_a = [s[0] * a for s, a in zip(sil_a, a_val)]
        u = [_gelu_parts(z) for z in ld(ub_ref)]
        vv = [_gelu_parts(z) for z in ld(vb_ref)]
        vnorm = [_rms(v[0]) for v in vv]
        gv = gv_ref[...]
        vn = [(n[1] * gv).astype(BF16) for n in vnorm]
        w_cat = cat([ws_ref[g].astype(BF16) for g in range(N_SGU_GROUPS)], 1)
        wT_cat = cat([wsT_ref[g].astype(BF16) for g in range(N_SGU_GROUPS)], 1)
        gmb = [m.astype(BF16) for m in gm]
        by_group = lambda chunk: cat([chunk * gmb[g] for g in range(N_SGU_GROUPS)], 0)
        btab = btab_ref[...]
        mixed = [cat([btab + _nn(w_cat, by_group(vn[h][c, :])) for c in chunks], 0) for h in H]
        sg = [u[h][0] * mixed[h] for h in H]
        sil_b = [_silu_parts(z) for z in ld(zb_ref)]
        gated_b = [sil_b[h][0] * sg[h] for h in H]

        kvv = kv_ref[...].astype(BF16)
        kp = [kvv[:, p] for p in pairs]
        vp = [kvv[:, MEM_WIDTH + pr * LANES:MEM_WIDTH + (pr + 1) * LANES] for pr in range(2)]
        qm = ld(qm_ref)
        qj = {(h, pr, j): (qm[h][:, pairs[pr]] * (hm[j] * 0.125)).astype(BF16) for h in H for pr, j in heads}
        sc = {k: _nt(qj[k], kp[k[1]]) for k in qj}
        ex = {k: jnp.exp(sc[k] - jnp.max(sc[k], axis=1, keepdims=True)) for k in qj}
        prob = {k: ex[k] * (1.0 / jnp.sum(ex[k], axis=1, keepdims=True)) for k in qj}
        probb = {k: prob[k].astype(BF16) for k in qj}
        mo = [cat([sum(_nn(probb[(h, pr, j)], vp[pr]) * hm[j] for j in (0, 1)) for pr in range(2)], 1) for h in H]
        sil_m = [_silu_parts(z) for z in ld(zm_ref)]
        gated_m = [sil_m[h][0] * mo[h] for h in H]

        gated = [cat([gated_a[h], gated_b[h], gated_m[h]], 1).astype(BF16) for h in H]
        wout = wout_ref[...]
        x_in = ld(x_ref)
        x2 = [x_in[h] + _nn(gated[h], wout) for h in H]
        fin = [_rms(z) for z in x2]
        gf = gf_ref[...]
        tgt = ld(t_ref)
        err = [fin[h][1] * gf - tgt[h] for h in H]
        loss_ref[...] += sum(jnp.sum(e * e) for e in err) * (0.5 / D_MODEL)

        dy = [e * (1.0 / D_MODEL) for e in err]
        dgf_ref[...] += sum(jnp.sum(dy[h] * fin[h][1], axis=0, keepdims=True) for h in H)
        gdy = [d * gf for d in dy]
        dx2 = [fin[h][0] * (gdy[h] - fin[h][1] * jnp.mean(gdy[h] * fin[h][1], axis=1, keepdims=True)) for h in H]
        for h in H:
            dx2_ref[rows[h], :] = dx2[h]
        dx2b = [d.astype(BF16) for d in dx2]
        dgated = [_nt(d, wout) for d in dx2b]
        dwout_ref[...] += _tn(cat(gated, 0), cat(dx2b, 0))
        dga = [d[:, 0:ATTN_WIDTH] for d in dgated]
        dgb = [d[:, ATTN_WIDTH:ATTN_WIDTH + SGU_WIDTH] for d in dgated]
        dgm = [d[:, ATTN_WIDTH + SGU_WIDTH:] for d in dgated]

        for h in H:
            da_ref[rows[h], :] = dga[h] * sil_a[h][0]
        dza = [dga[h] * a_val[h] * sil_a[h][1] for h in H]

        dsg = [dgb[h] * sil_b[h][0] for h in H]
        dzb = [dgb[h] * sg[h] * sil_b[h][1] for h in H]
        dub = [dsg[h] * mixed[h] * u[h][1] for h in H]
        dmixed = [dsg[h] * u[h][0] for h in H]
        dmixed_b = [d.astype(BF16) for d in dmixed]
        dvn = [cat([_nn(wT_cat, by_group(dmixed_b[h][c, :])) for c in chunks], 0) for h in H]
        for g in range(N_SGU_GROUPS):
            dws_ref[g] += sum(_nt((dmixed[h][c, :] * gm[g]).astype(BF16), vn[h][c, :]) for h in H for c in chunks)
        dbtab_scr[...] += sum(dmixed[h][c, :] for h in H for c in chunks)
        dgv_ref[...] += sum(jnp.sum(dvn[h] * vnorm[h][1], axis=0, keepdims=True) for h in H)
        tv = [d * gv for d in dvn]
        dvv = [vnorm[h][0] * (tv[h] - vnorm[h][1] * jnp.mean(tv[h] * vnorm[h][1], axis=1, keepdims=True)) for h in H]
        dvb = [dvv[h] * vv[h][1] for h in H]

        dmo = [dgm[h] * sil_m[h][0] for h in H]
        dzm = [dgm[h] * mo[h] * sil_m[h][1] for h in H]
        dmoj = {(h, pr, j): (dmo[h][:, pairs[pr]] * hm[j]).astype(BF16) for h in H for pr, j in heads}
        dp = {k: _nt(dmoj[k], vp[k[1]]) for k in qj}
        ds = {k: (prob[k] * (dp[k] - jnp.sum(dp[k] * prob[k], axis=1, keepdims=True))).astype(BF16) for k in qj}
        dqm = [cat([sum(_nn(ds[(h, pr, j)], kp[pr]) * (hm[j] * 0.125) for j in (0, 1)) for pr in range(2)], 1)
               for h in H]
        every = lambda tbl, pr: cat([tbl[(h, pr, j)] for h in H for j in (0, 1)], 0)
        dk = [_tn(every(ds, pr), every(qj, pr)) for pr in range(2)]
        dv = [_tn(every(probb, pr), every(dmoj, pr)) for pr in range(2)]
        dkv_ref[...] += cat(dk + dv, 1)

        for h in H:
            drest_ref[rows[h], :] = cat([dza[h], dub[h], dvb[h], dzb[h], dqm[h], dzm[h]], 1).astype(BF16)

        @pl.when(last)
        def _():
            lane = lax.broadcasted_iota(jnp.int32, (1, LANES), 1)
            dbt = dbtab_scr[...]
            out = jnp.zeros((SGU_CHUNK, LANES), F32)
            for g in range(N_SGU_GROUPS):
                out = out + jnp.where(lane == g, jnp.sum(dbt * gm[g], axis=1, keepdims=True), 0.0)
            dbs_ref[...] = out
            for r0 in range(0, D_MODEL, SGU_CHUNK):
                k, row = divmod(r0, D_MODEL // N_CHIPS)
                dwout_bf_ref[k, row:row + SGU_CHUNK, :] = dwout_ref[r0:r0 + SGU_CHUNK, :].astype(BF16)

    tile = lambda w, cb: pl.BlockSpec((tm, w), lambda b, t, cb=cb: (b * nt + t, cb))
    const = lambda shape: pl.BlockSpec(shape, lambda b, t, n=len(shape): (0,) * n)
    return pl.pallas_call(
        body, grid=(B, nt),
        in_specs=[tile(D_MODEL, 0), tile(D_MODEL, 0), tile(ATTN_WIDTH, 0),
                  tile(ATTN_WIDTH, 3),
                  tile(SGU_WIDTH, 8), tile(SGU_WIDTH, 9), tile(SGU_WIDTH, 10),
                  tile(MEM_WIDTH, 11), tile(MEM_WIDTH, 12),
                  pl.BlockSpec((N_MEM, 2 * MEM_WIDTH), lambda b, t: (b, 0)),
                  const((N_SGU_GROUPS, SGU_CHUNK, SGU_CHUNK)), const((N_SGU_GROUPS, SGU_CHUNK, SGU_CHUNK)),
                  const((SGU_CHUNK, SGU_WIDTH)), const((1, SGU_WIDTH)),
                  const((D_MODEL, D_MODEL)), const((1, D_MODEL))],
        out_specs=[tile(D_MODEL, 0), tile(ATTN_WIDTH, 0), tile(REST_COLS, 0),
                   const((8, LANES)), const((N_CHIPS, D_MODEL // N_CHIPS, D_MODEL)),
                   const((N_SGU_GROUPS, SGU_CHUNK, SGU_CHUNK)), const((SGU_CHUNK, LANES)),
                   const((1, SGU_WIDTH)), const((1, D_MODEL)),
                   pl.BlockSpec((N_MEM, 2 * MEM_WIDTH), lambda b, t: (b, 0))],
        out_shape=[jax.ShapeDtypeStruct((T, D_MODEL), F32), jax.ShapeDtypeStruct((T, ATTN_WIDTH), F32),
                   jax.ShapeDtypeStruct((T, REST_COLS), BF16),
                   jax.ShapeDtypeStruct((8, LANES), F32),
                   jax.ShapeDtypeStruct((N_CHIPS, D_MODEL // N_CHIPS, D_MODEL), BF16),
                   jax.ShapeDtypeStruct((N_SGU_GROUPS, SGU_CHUNK, SGU_CHUNK), F32),
                   jax.ShapeDtypeStruct((SGU_CHUNK, LANES), F32),
                   jax.ShapeDtypeStruct((1, SGU_WIDTH), F32), jax.ShapeDtypeStruct((1, D_MODEL), F32),
                   jax.ShapeDtypeStruct((B * N_MEM, 2 * MEM_WIDTH), F32)],
        scratch_shapes=[pltpu.VMEM((SGU_CHUNK, SGU_WIDTH), F32), pltpu.VMEM((D_MODEL, D_MODEL), F32)],
        compiler_params=_params(("arbitrary", "arbitrary"), vmem=VMEM_LIMIT + 2 * 1024 * 1024), name="mid")(
            x2d, t2d, a, proj, proj, proj, proj, proj, proj, kv, w_s, w_sT, b_tab, g_v, w_out, g_final)


def _inproj_bwd_dx(dq, dk, dv, drest, x2d, dx2, g_norm, w_in_t, after=()):
    T = x2d.shape[0]
    tm = 512
    W = ATTN_WIDTH

    def body(dq_ref, dk_ref, dv_ref, dr_ref, x_ref, dx2_ref, g_ref, w_ref, *rest):
        gx_ref, dg_ref = rest[-2:]

        @pl.when(pl.program_id(0) == 0)
        def _():
            dg_ref[...] = jnp.zeros_like(dg_ref)

        halves = [pl.ds(h * (tm // 2), tm // 2) for h in (0, 1)]
        dh = [(_nn(dq_ref[r, :], w_ref[0:W, :]) + _nn(dk_ref[r, :], w_ref[W:2 * W, :])
               + _nn(dv_ref[r, :], w_ref[2 * W:3 * W, :]) + _nn(dr_ref[r, :], w_ref[QKV_COLS:IN_COLS, :]))
              for r in halves]
        nrm = [_rms(x_ref[r, :]) for r in halves]
        dg_ref[...] += sum(jnp.sum(d * n[1], axis=0, keepdims=True) for d, n in zip(dh, nrm))
        g = g_ref[...]
        for r, d, (rstd, xh) in zip(halves, dh, nrm):
            th = d * g
            gx_ref[r, :] = rstd * (th - xh * jnp.mean(th * xh, axis=1, keepdims=True)) + dx2_ref[r, :]

    tile = lambda w: pl.BlockSpec((tm, w), lambda i: (i, 0))
    return pl.pallas_call(
        body, grid=(T // tm,),
        in_specs=[tile(W), tile(W), tile(W), tile(REST_COLS), tile(D_MODEL), tile(D_MODEL),
                  pl.BlockSpec((1, D_MODEL), lambda i: (0, 0)),
                  pl.BlockSpec((IN_COLS, D_MODEL), lambda i: (0, 0))] + _after(after),
        out_specs=[tile(D_MODEL), pl.BlockSpec((1, D_MODEL), lambda i: (0, 0))],
        out_shape=[jax.ShapeDtypeStruct((T, D_MODEL), F32), jax.ShapeDtypeStruct((1, D_MODEL), F32)],
        compiler_params=_params(("arbitrary",)), name="inproj_bwd_dx")(
            dq, dk, dv, drest, x2d, dx2, g_norm, w_in_t, *after)


def _inproj_bwd_dw(dq, dk, dv, drest, x2d, g_norm, reduce_with=None):
    T = x2d.shape[0]
    tm = 512
    nt = T // tm
    W = ATTN_WIDTH
    fused = reduce_with is not None
    others = list(reduce_with) if fused else []
    ns = 1 + len(others)
    shard = IN_COLS // N_CHIPS
    halves = [shard // 2] + [s.shape[1] // 2 for s in others]
    cols = [D_MODEL] + [s.shape[2] for s in others]
    row_block = 32

    def body(dq_ref, dk_ref, dv_ref, dr_ref, x_ref, g_ref, *rest):
        if fused:
            stacks = rest[:ns - 1]
            sends, owns = rest[ns - 1:2 * ns - 1], rest[2 * ns - 1:3 * ns - 1]
            acc, ras, narrow = rest[3 * ns - 1], rest[3 * ns:4 * ns], rest[4 * ns]
            s_sem, r_sem = rest[4 * ns + 1], rest[4 * ns + 2]
            x, y, c, chip, peers, peer_chip = _place()
            sib = (x, y, 1 - c)

            def part(w, k, cc, r0=0, rows=None):
                n = halves[w]
                rows = n if rows is None else rows
                if w == 0:
                    return acc.at[pl.ds(pl.multiple_of(k * shard + cc * n + r0, 8), rows), :]
                return stacks[w - 1].at[k, pl.ds(pl.multiple_of(cc * n + r0, 8), rows), :]

            def swap_other(w):
                theirs = stacks[w - 1].at[:, pl.ds(pl.multiple_of((1 - c) * halves[w], 8), halves[w]), :]
                return _remote(theirs, ras[w], s_sem.at[N_CHIPS - 1 + w], r_sem.at[N_CHIPS - 1 + w], sib)

            def swap_win(k):
                return _remote(narrow.at[k], ras[0].at[k], s_sem.at[k], r_sem.at[k], sib)
        else:
            acc = rest[0]

        @pl.when(pl.program_id(0) == 0)
        def _():
            acc[...] = jnp.zeros_like(acc)
            for w in range(1, ns):
                swap_other(w).start()

        _, xh = _rms(x_ref[...])
        h = (xh * g_ref[...]).astype(BF16)
        acc[0:W, :] += _tn(dq_ref[...], h)
        acc[W:2 * W, :] += _tn(dk_ref[...], h)
        acc[2 * W:3 * W, :] += _tn(dv_ref[...], h)
        acc[QKV_COLS:IN_COLS, :] += _tn(dr_ref[...], h)

        if fused:
            @pl.when(pl.program_id(0) == nt - 1)
            def _():
                for k in range(N_CHIPS):
                    def to_bf16(i, carry, k=k):
                        r0 = pl.multiple_of(i * row_block, row_block)
                        narrow[k, pl.ds(r0, row_block), :] = part(0, k, 1 - c, r0, row_block)[...].astype(BF16)
                        return carry
                    lax.fori_loop(0, halves[0] // row_block, to_bf16, 0)
                    swap_win(k).start()
                def chip_sum(w, k, r0):
                    blk = pl.ds(r0, row_block)
                    return part(w, k, c, r0, row_block)[...] + ras[w][k, blk, :].astype(F32)

                for w in range(1, ns):
                    swap_other(w).wait_recv()

                    def sums(i, carry, w=w):
                        r0 = pl.multiple_of(i * row_block, row_block)
                        for m in range(3):
                            sends[w][m, pl.ds(r0, row_block), :] = chip_sum(w, peer_chip[m], r0).astype(BF16)
                        owns[w][pl.ds(r0, row_block), :] = chip_sum(w, chip, r0)
                        return carry
                    lax.fori_loop(0, halves[w] // row_block, sums, 0)
                for k in range(N_CHIPS):
                    swap_win(k).wait_recv()

                    @pl.when(chip == k)
                    def _(k=k):
                        def own(i, carry):
                            r0 = pl.multiple_of(i * row_block, row_block)
                            owns[0][pl.ds(r0, row_block), :] = chip_sum(0, k, r0)
                            return carry
                        lax.fori_loop(0, halves[0] // row_block, own, 0)

                    @pl.when(chip != k)
                    def _(k=k):
                        def other(i, carry):
                            r0 = pl.multiple_of(i * row_block, row_block)
                            sends[0][(k ^ chip) - 1, pl.ds(r0, row_block), :] = chip_sum(0, k, r0).astype(BF16)
                            return carry
                        lax.fori_loop(0, halves[0] // row_block, other, 0)
                for k in range(N_CHIPS):
                    swap_win(k).wait_send()
                for w in range(1, ns):
                    swap_other(w).wait_send()

    tile = lambda w: pl.BlockSpec((tm, w), lambda i: (i, 0))
    vmem = pl.BlockSpec(memory_space=pltpu.VMEM)
    in_specs = [tile(W), tile(W), tile(W), tile(REST_COLS), tile(D_MODEL), pl.BlockSpec((1, D_MODEL), lambda i: (0, 0))]
    if not fused:
        return pl.pallas_call(
            body, grid=(nt,), in_specs=in_specs,
            out_specs=pl.BlockSpec((IN_COLS, D_MODEL), lambda i: (0, 0)),
            out_shape=jax.ShapeDtypeStruct((IN_COLS, D_MODEL), F32),
            compiler_params=_params(("arbitrary",)), name="inproj_bwd_dw")(dq, dk, dv, drest, x2d, g_norm)
    outs = pl.pallas_call(
        body, grid=(nt,), in_specs=in_specs + [vmem] * (ns - 1), out_specs=[vmem] * (2 * ns),
        out_shape=[jax.ShapeDtypeStruct((3, n, cl), BF16) for n, cl in zip(halves, cols)]
        + [jax.ShapeDtypeStruct((n, cl), F32) for n, cl in zip(halves, cols)],
        scratch_shapes=[pltpu.VMEM((IN_COLS, D_MODEL), F32)]
        + [pltpu.VMEM((N_CHIPS, n, cl), BF16 if w == 0 else F32) for w, (n, cl) in enumerate(zip(halves, cols))]
        + [pltpu.VMEM((N_CHIPS, halves[0], D_MODEL), BF16)]
        + [pltpu.SemaphoreType.DMA((N_CHIPS - 1 + ns,)), pltpu.SemaphoreType.DMA((N_CHIPS - 1 + ns,))],
        compiler_params=_params(("arbitrary",)), name="inproj_bwd_dw_reduce")(
            dq, dk, dv, drest, x2d, g_norm, *others)
    return outs[:ns], outs[ns:]


def _adamw_update(w, g, m, v):
    nm = ADAM_B1 * m + (1.0 - ADAM_B1) * g
    nv = ADAM_B2 * v + (1.0 - ADAM_B2) * (g * g)
    m_hat = nm / (1.0 - ADAM_B1 ** ADAM_STEP)
    v_hat = nv / (1.0 - ADAM_B2 ** ADAM_STEP)
    return -ADAM_LR * (m_hat / (jnp.sqrt(v_hat) + ADAM_EPS) + ADAM_WD * w), nm, nv


def _adamw(w, g, m, v, name):
    R, C = w.shape
    br = max(r for r in range(8, 257, 8) if R % r == 0)

    def body(w_ref, g_ref, m_ref, v_ref, g_out, d_ref, nm_ref, nv_ref):
        g = g_ref[...]
        g_out[...] = g
        d_ref[...], nm_ref[...], nv_ref[...] = _adamw_update(w_ref[...], g, m_ref[...], v_ref[...])

    spec = pl.BlockSpec((br, C), lambda i: (i, 0))
    return pl.pallas_call(
        body, grid=(R // br,), in_specs=[spec] * 4, out_specs=[spec] * 4,
        out_shape=[jax.ShapeDtypeStruct((R, C), F32)] * 4,
        compiler_params=_params(("arbitrary",)), name=name)(w, g, m, v)


def _adamw_rest(g_packed, ws, ms, vs, whole):
    n = len(ws)
    nw = len(whole)
    row_block = 64

    def body(*refs):
        g_ref = refs[0]
        w_refs, m_refs, v_refs = refs[1:1 + n], refs[1 + n:1 + 2 * n], refs[1 + 2 * n:1 + 3 * n]
        whole_in = [refs[1 + 3 * n + 4 * i:5 + 3 * n + 4 * i] for i in range(nw)]
        outs = refs[1 + 3 * n + 4 * nw:]
        off = 0
        for i, (_, used, padded) in enumerate(_SMALL_PARTS[:n]):
            g = g_ref[off:off + used, :]
            delta, nm, nv = _adamw_update(w_refs[i][...], g, m_refs[i][...], v_refs[i][...])
            outs[4 * i][...], outs[4 * i + 1][...], outs[4 * i + 2][...], outs[4 * i + 3][...] = g, delta, nm, nv
            off += padded
        for i, (w_ref, gw_ref, m_ref, v_ref) in enumerate(whole_in):
            for r0 in range(0, w_ref.shape[0], row_block):
                blk = pl.ds(r0, row_block)
                g = gw_ref[blk, :]
                new = _adamw_update(w_ref[blk, :], g, m_ref[blk, :], v_ref[blk, :])
                for ref, val in zip(outs[4 * (n + i):4 * (n + i) + 4], (g,) + new):
                    ref[blk, :] = val
        outs[4 * (n + nw)][...] = g_ref[_LOSS_ROW:_LOSS_ROW + 1, 0:1]

    outs = pl.pallas_call(
        body, out_shape=[jax.ShapeDtypeStruct(w.shape, F32) for w in ws for _ in range(4)]
        + [jax.ShapeDtypeStruct(four[0].shape, F32) for four in whole for _ in range(4)]
        + [jax.ShapeDtypeStruct((1, 1), F32)],
        compiler_params=_params(), name="adamw_rest")(g_packed, *ws, *ms, *vs, *[a for four in whole for a in four])
    return ([outs[4 * i:4 * i + 4] for i in range(n)], [outs[4 * (n + i):4 * (n + i) + 4] for i in range(nw)],
            outs[4 * (n + nw)])


def _place():
    x, y, c = lax.axis_index("x"), lax.axis_index("y"), lax.axis_index("c")
    chip = 2 * x + y
    peers = [(x, 1 - y), (1 - x, y), (1 - x, 1 - y)]
    peer_chip = [2 * px + py for px, py in peers]
    return x, y, c, chip, peers, peer_chip


def _remote(src, dst, send_sem, recv_sem, dev):
    return pltpu.make_async_remote_copy(src_ref=src, dst_ref=dst, send_sem=send_sem, recv_sem=recv_sem,
                                        device_id=dev, device_id_type=MESH)


def _ag_weights(weights, late=()):
    nw, nl = len(weights), len(late)
    pieces = 2

    def body(*refs):
        srcs, late_srcs = refs[:nw], refs[nw:nw + nl]
        outs, late_bf, late_land = (refs[nw + nl:2 * nw + nl], refs[2 * nw + nl:2 * nw + 2 * nl],
                                    refs[2 * nw + 2 * nl:2 * nw + 3 * nl])
        s_ici, r_ici, s_d2d, r_d2d = refs[2 * nw + 3 * nl:]
        x, y, c = lax.axis_index("x"), lax.axis_index("y"), lax.axis_index("c")
        chip = 2 * x + y
        sib = (x, y, 1 - c)
        first = ((x + 1 - c) % 2, (y + c) % 2)
        second = ((x + c) % 2, (y + 1 - c) % 2)
        first_chip, second_chip = 2 * first[0] + first[1], 2 * second[0] + second[1]
        diag_chip = 3 - chip
        for src, out in zip(srcs, outs):
            out[chip] = src[...].astype(BF16)

        parts = [(w, out, pc) for w, out in enumerate(outs) for pc in range(pieces)]

        def piece(out, k, cc, pc):
            rows = out.shape[1] // 2 // pieces
            return out.at[k, pl.ds(pl.multiple_of((cc * pieces + pc) * rows, 16), rows), :]

        def ici(w, slot, out, k, dev, pc):
            blk, sem = piece(out, k, c, pc), (nw * slot + w) * pieces + pc
            return _remote(blk, blk, s_ici.at[sem], r_ici.at[sem], (dev[0], dev[1], c))

        def d2d(w, slot, out, k, cc, pc):
            blk, sem = piece(out, k, cc, pc), (nw * slot + w) * pieces + pc
            return _remote(blk, blk, s_d2d.at[sem], r_d2d.at[sem], sib)

        sent = []
        for slot, dev in enumerate((first, second)):
            for w, out, pc in parts:
                sent.append(ici(w, slot, out, chip, dev, pc))
                sent[-1].start()
        for src, bf, land in zip(late_srcs, late_bf, late_land):
            bf[...] = src[...].astype(BF16)
            land[...] = jnp.zeros_like(land)
            land[chip] = bf[...]
        for slot, k, dev in ((0, first_chip, first), (1, second_chip, second), (2, diag_chip, second)):
            for w, out, pc in parts:
                ici(w, slot, out, k, dev, pc).wait_recv()
                if slot == 0:
                    sent.append(ici(w, 2, out, k, second, pc))
                    sent[-1].start()
                sent.append(d2d(w, slot, out, k, c, pc))
                sent[-1].start()
        for slot, k in ((0, second_chip), (1, first_chip), (2, diag_chip)):
            for w, out, pc in parts:
                d2d(w, slot, out, k, 1 - c, pc).wait_recv()
        for cp in sent:
            cp.wait_send()

    vmem = pl.BlockSpec(memory_space=pltpu.VMEM)
    outs = pl.pallas_call(
        body,
        out_shape=[jax.ShapeDtypeStruct((N_CHIPS,) + w.shape, BF16) for w in weights]
        + [jax.ShapeDtypeStruct(w.shape, BF16) for w in late]
        + [jax.ShapeDtypeStruct((N_CHIPS,) + w.shape, BF16) for w in late],
        in_specs=[vmem] * (nw + nl), out_specs=[vmem] * (nw + 2 * nl),
        scratch_shapes=[pltpu.SemaphoreType.DMA((3 * nw * pieces,))] * 4,
        compiler_params=pltpu.CompilerParams(vmem_limit_bytes=VMEM_LIMIT), name="ag_weights")(*weights, *late)
    return outs[:nw], outs[nw:nw + nl], outs[nw + nl:]


_HBM = pl.BlockSpec(memory_space=pltpu.HBM)
_SEM = pl.BlockSpec(memory_space=pltpu.SEMAPHORE)
_ANY = pl.BlockSpec(memory_space=pl.ANY)
_DATAFLOW = pltpu.SideEffectType.DATAFLOW_SIDE_EFFECTING


def _in_hbm(a):
    return pltpu.with_memory_space_constraint(a, pltpu.HBM)


_PEERS_OF = {"gather": 3, "scatter": 3, "direct": 7}


def _exchange_copies(mode, srcs, lands, send_sems, recv_sems):
    nw = len(srcs)
    x, y, c, chip, peers, peer_chip = _place()
    pairs = []
    if mode == "direct":
        targets = [((x, y), chip, 1)] + [(p, k, d) for p, k in zip(peers, peer_chip) for d in (0, 1)]
        for r, ((px, py), k, d) in enumerate(targets):
            core = (c + d) % 2
            for w in range(nw):
                n = srcs[w].shape[1] // 2
                src = srcs[w].at[k, pl.ds(pl.multiple_of(core * n, 16), n), :]
                sems = (send_sems.at[nw * r + w], recv_sems.at[nw * r + w], (px, py, core))
                pairs.append((_remote(src, lands[w].at[r], *sems),) * 2)
        return pairs
    gather = mode == "gather"
    for m, (px, py) in enumerate(peers):
        for w in range(nw):
            sems = (send_sems.at[nw * m + w], recv_sems.at[nw * m + w], (px, py, c))
            if gather:
                pairs.append((_remote(srcs[w], lands[w].at[chip], *sems),
                              _remote(srcs[w], lands[w].at[peer_chip[m]], *sems)))
            else:
                pairs.append((_remote(srcs[w].at[m], lands[w].at[m], *sems),) * 2)
    return pairs


def _exchange_start(mode, srcs, after, name, lands=None):
    nw = len(srcs)
    n_copies = _PEERS_OF[mode] * nw

    def body(*refs):
        send_sems, recv_sems = refs[2 * nw + 1], refs[2 * nw + 2]
        for start, _ in _exchange_copies(mode, refs[:nw], refs[nw:2 * nw], send_sems, recv_sems):
            start.start()
        refs[-1][...] = jnp.zeros_like(refs[-1])

    if lands is None:
        shape = {"gather": lambda s: (N_CHIPS,) + s.shape, "scatter": lambda s: s.shape,
                 "direct": lambda s: (_PEERS_OF["direct"], s.shape[1] // 2, s.shape[2])}[mode]
        lands = [lax.empty(shape(s), s.dtype) for s in srcs]
    lands = [_in_hbm(l) for l in lands]
    return pl.pallas_call(
        body, name=name,
        out_shape=(pltpu.SemaphoreType.DMA((n_copies,)), pltpu.SemaphoreType.DMA((n_copies,)))
        + tuple(pltpu.HBM(s.shape, s.dtype) for s in srcs)
        + tuple(pltpu.HBM(l.shape, l.dtype) for l in lands)
        + (jax.ShapeDtypeStruct((8, LANES), F32),),
        in_specs=[_HBM] * (2 * nw) + [_ANY],
        out_specs=(_SEM, _SEM) + (_HBM,) * (2 * nw) + (pl.BlockSpec(memory_space=pltpu.VMEM),),
        input_output_aliases={i: 2 + i for i in range(2 * nw)},
        compiler_params=pltpu.CompilerParams(has_side_effects=_DATAFLOW),
    )(*[_in_hbm(s) for s in srcs], *lands, after)


def _exchange_wait(mode, started, after, name):
    nw = (len(started) - 3) // 2
    send_sems, recv_sems = started[0], started[1]
    thru = started[2:2 + 2 * nw]

    def body(*refs):
        for _, arrival in _exchange_copies(mode, refs[:nw], refs[nw:2 * nw], refs[2 * nw], refs[2 * nw + 1]):
            arrival.wait_send()
            arrival.wait_recv()

    outs = pl.pallas_call(
        body, name=name,
        out_shape=tuple(pltpu.HBM(t.shape, t.dtype) for t in thru),
        in_specs=[_HBM] * (2 * nw) + [_SEM, _SEM, _ANY], out_specs=(_HBM,) * (2 * nw),
        input_output_aliases={i: i for i in range(2 * nw)},
        compiler_params=pltpu.CompilerParams(has_side_effects=_DATAFLOW),
    )(*thru, send_sems, recv_sems, after)
    return outs[:nw], outs[nw:]


def _reduce_last(owns, landed, g_small, direct_srcs=(), direct_landed=()):
    ns, nd = len(owns), len(direct_srcs)
    halves = [o.shape[0] for o in owns] + [s.shape[1] // 2 for s in direct_srcs]
    row_block = 32
    hs = SMALL_ROWS // 2

    def body(*refs):
        own_refs, land_refs, gsm_ref = refs[:ns], refs[ns:2 * ns], refs[2 * ns]
        dsrc_refs, dland_refs = refs[2 * ns + 1:2 * ns + 1 + nd], refs[2 * ns + 1 + nd:2 * ns + 1 + 2 * nd]
        n_in = 2 * ns + 1 + 2 * nd
        out_refs, osm_ref = refs[n_in:n_in + ns + nd], refs[n_in + ns + nd]
        ra_sm, p_sm, s_sem, r_sem, sm_s, sm_r = refs[n_in + ns + nd + 1:]
        x, y, c, chip, peers, peer_chip = _place()
        sib = (x, y, 1 - c)
        half = lambda cc: pl.ds(pl.multiple_of(cc * hs, 8), hs)
        sm_a = _remote(gsm_ref.at[half(1 - c), :], ra_sm, sm_s.at[0], sm_r.at[0], sib)
        sm_a.start()
        swaps = [sm_a]
        for w in range(ns + nd):
            n = halves[w]

            def total(i, carry, w=w, n=n):
                r0 = pl.multiple_of(i * row_block, row_block)
                blk = pl.ds(r0, row_block)
                mine = pl.ds(pl.multiple_of(c * n + r0, 16), row_block)
                if w < ns:
                    acc = own_refs[w][blk, :]
                    terms = [land_refs[w][m, blk, :] for m in range(_PEERS_OF["scatter"])]
                else:
                    acc = dsrc_refs[w - ns][chip, mine, :].astype(F32)
                    terms = [dland_refs[w - ns][r, blk, :] for r in range(_PEERS_OF["direct"])]
                for term in terms:
                    acc = acc + term.astype(F32)
                out_refs[w][mine, :] = acc
                return carry
            lax.fori_loop(0, n // row_block, total, 0)
            mine = out_refs[w].at[pl.ds(pl.multiple_of(c * n, 8), n), :]
            swaps.append(_remote(mine, mine, s_sem.at[w], r_sem.at[w], sib))
            swaps[-1].start()
        sm_a.wait_recv()
        p_sm[chip] = gsm_ref[half(c), :] + ra_sm[...]
        for m, (px, py) in enumerate(peers):
            swaps.append(_remote(p_sm.at[chip], p_sm.at[chip], sm_s.at[1 + m], sm_r.at[1 + m], (px, py, c)))
            swaps[-1].start()
        for m, (px, py) in enumerate(peers):
            _remote(p_sm.at[chip], p_sm.at[peer_chip[m]], sm_s.at[1 + m], sm_r.at[1 + m], (px, py, c)).wait_recv()
        osm_ref[half(c), :] = (p_sm[0] + p_sm[1]) + (p_sm[2] + p_sm[3])
        swaps.append(_remote(osm_ref.at[half(c), :], osm_ref.at[half(c), :], sm_s.at[4], sm_r.at[4], sib))
        swaps[-1].start()
        for w in range(ns + nd):
            n = halves[w]
            theirs = out_refs[w].at[pl.ds(pl.multiple_of((1 - c) * n, 8), n), :]
            _remote(theirs, theirs, s_sem.at[w], r_sem.at[w], sib).wait_recv()
        _remote(osm_ref.at[half(1 - c), :], osm_ref.at[half(1 - c), :], sm_s.at[4], sm_r.at[4], sib).wait_recv()
        for cp in swaps:
            cp.wait_send()

    vmem = pl.BlockSpec(memory_space=pltpu.VMEM)
    return pl.pallas_call(
        body, out_shape=[jax.ShapeDtypeStruct((2 * o.shape[0], o.shape[1]), F32) for o in owns]
        + [jax.ShapeDtypeStruct(s.shape[1:], F32) for s in direct_srcs]
        + [jax.ShapeDtypeStruct((SMALL_ROWS, LANES), F32)],
        in_specs=[vmem] * (2 * ns + 1 + 2 * nd), out_specs=[vmem] * (ns + nd + 1),
        scratch_shapes=[pltpu.VMEM((hs, LANES), F32), pltpu.VMEM((N_CHIPS, hs, LANES), F32),
                        pltpu.SemaphoreType.DMA((ns + nd,)), pltpu.SemaphoreType.DMA((ns + nd,)),
                        pltpu.SemaphoreType.DMA((5,)), pltpu.SemaphoreType.DMA((5,))],
        compiler_params=pltpu.CompilerParams(vmem_limit_bytes=VMEM_LIMIT),
        name="reduce_last")(*owns, *landed, g_small, *direct_srcs, *direct_landed)


_SMALL_PARTS = (("g_norm", 8, 8), ("w_s", 512, 512), ("b_s", 4, 8), ("g_v", 2, 8), ("g_mem", 8, 8),
                ("g_final", 8, 8), ("loss", 8, 8))
_LOSS_ROW = SMALL_ROWS - 8
assert sum(p for _, _, p in _SMALL_PARTS) == SMALL_ROWS


def _pack_small(parts, loss_block):
    rows = []
    for (name, used, padded), p in zip(_SMALL_PARTS, list(parts) + [loss_block]):
        p = p.reshape(used, LANES)
        if padded > used:
            p = jnp.pad(p, ((0, padded - used), (0, 0)))
        rows.append(p)
    return jnp.concatenate(rows, axis=0)


def _local_step(x, mem, target, g_norm, w_in, w_s, b_s, g_v, g_mem, late_weights, g_final,
                fwd_token=None, on_late=None, on_dw=None):
    B, S, _ = x.shape
    x2d = x.reshape(B * S, D_MODEL)
    t2d = target.reshape(B * S, D_MODEL)
    mem2d = mem.reshape(B * N_MEM, D_MODEL)

    proj = _inproj_fwd(x2d, g_norm, w_in, after=() if fwd_token is None else (fwd_token,))
    w_kv, w_out = late_weights(proj)
    kv = _kv_fwd(mem2d, g_mem, w_kv)
    a, lse = _attn_fwd(proj, B, S)
    w_sT = jnp.swapaxes(w_s, 1, 2)
    b_tab = jnp.repeat(b_s.T, HEAD_DIM, axis=1)
    (dx2, da, drest, loss, d_wout, d_ws, d_bs, d_gv, d_gf, dkv) = _mid(
        x2d, t2d, a, proj, kv, w_s, w_sT, b_tab, g_v, w_out, g_final, B, S)
    d_wkv, d_gmem = _kv_bwd(mem2d, g_mem, w_kv, dkv)
    dq, dk, dv = _attn_bwd(proj, a, lse, da, B, S, after=() if on_late is None else (on_late(d_wkv, d_wout),))
    if on_dw is None:
        d_win = _inproj_bwd_dw(dq, dk, dv, drest, x2d, g_norm)
        after = ()
    else:
        d_win = None
        after = (on_dw(*_inproj_bwd_dw(dq, dk, dv, drest, x2d, g_norm, reduce_with=[])),)
    grad_x, d_gnorm = _inproj_bwd_dx(dq, dk, dv, drest, x2d, dx2, g_norm, w_in, after=after)
    d_bs = d_bs[:, :N_SGU_GROUPS].T
    return (loss, grad_x.reshape(B, S, D_MODEL),
            dict(g_norm=d_gnorm, w_in=d_win, w_s=d_ws, b_s=d_bs, g_v=d_gv, g_mem=d_gmem, w_kv=d_wkv,
                 w_out=d_wout, g_final=d_gf))


def kernel(x, mem, g_norm, w_in, w_sgu_spatial, b_sgu_spatial, g_sgu_v, g_mem, w_mem_kv, w_out, g_final, loss_target, m_g_norm, m_w_in, m_w_sgu_spatial, m_b_sgu_spatial, m_g_sgu_v, m_g_mem, m_w_mem_kv, m_w_out, m_g_final, v_g_norm, v_w_in, v_w_sgu_spatial, v_b_sgu_spatial, v_g_sgu_v, v_g_mem, v_w_mem_kv, v_w_out, v_g_final):
    t = lambda w: jnp.swapaxes(w[0], 0, 1)
    (win_all,), late_shards, late_lands = _ag_weights([t(w_in)], [w_mem_kv[0], w_out[0]])
    w_in_full = win_all.reshape(-1, win_all.shape[-1])
    late = _exchange_start("gather", list(late_shards), win_all, "gather_late_start", lands=late_lands)

    def late_weights(proj):
        return [z.reshape(-1, z.shape[-1]) for z in _exchange_wait("gather", late, proj, "gather_late_wait")[1]]

    scatter = {}

    def on_late(d_wkv, d_wout):
        scatter["late"] = _exchange_start("direct", [d_wkv, d_wout], d_wkv, "scatter_late_start")
        return scatter["late"][-1]

    def on_dw(sends, owns):
        scatter["own"] = owns
        scatter["started"] = _exchange_start("scatter", list(sends), owns[0], "scatter_start")
        return scatter["started"][-1]

    loss, grad_x, g = _local_step(
        x, mem, loss_target, g_norm, w_in_full, w_sgu_spatial[0], b_sgu_spatial[0], g_sgu_v, g_mem,
        late_weights, g_final.reshape(1, D_MODEL), fwd_token=late[-1], on_late=on_late, on_dw=on_dw)

    small_names = ("g_norm", "w_s", "b_s", "g_v", "g_mem", "g_final")
    g_small = _pack_small([g[n] for n in small_names], loss)
    late_srcs, late_landed = _exchange_wait("direct", scatter["late"], g_small, "scatter_late_wait")
    _, landed = _exchange_wait("scatter", scatter["started"], late_landed[0], "scatter_wait")
    gr_in, gr_kv, gr_out, gr_small = _reduce_last(scatter["own"], landed, g_small, late_srcs, late_landed)

    small_w = (g_norm, w_sgu_spatial, b_sgu_spatial, g_sgu_v, g_mem, g_final)
    small_m = (m_g_norm, m_w_sgu_spatial, m_b_sgu_spatial, m_g_sgu_v, m_g_mem, m_g_final)
    small_v = (v_g_norm, v_w_sgu_spatial, v_b_sgu_spatial, v_g_sgu_v, v_g_mem, v_g_final)
    rows = lambda ws: [w.reshape(-1, LANES) for w in ws]
    small_new, ((gr_kv, d_kv, nm_kv, nv_kv), (gr_out, d_out, nm_out, nv_out)), loss = _adamw_rest(
        gr_small, rows(small_w), rows(small_m), rows(small_v),
        [(w_mem_kv[0], gr_kv, m_w_mem_kv[0], v_w_mem_kv[0]), (w_out[0], gr_out, m_w_out[0], v_w_out[0])])
    loss = loss.reshape(())
    small = [[z.reshape(w.shape) for z in four] for w, four in zip(small_w, small_new)]
    gr_in, d_in, nm_in, nv_in = [jnp.swapaxes(z, 0, 1)
                                 for z in _adamw(t(w_in), gr_in, t(m_w_in), t(v_w_in), "adamw_w_in")]

    def leaves(kind, big_in, big_kv, big_out):
        s_norm, s_ws, s_bs, s_gv, s_gmem, s_gf = [four[kind] for four in small]
        return [s_norm, big_in[None], s_ws, s_bs, s_gv, s_gmem, big_kv[None], big_out[None], s_gf]

    return (loss, grad_x, *leaves(0, gr_in, gr_kv, gr_out), *leaves(1, d_in, d_kv, d_out),
            *leaves(2, nm_in, nm_kv, nm_out), *leaves(3, nv_in, nv_kv, nv_out))
```

```python
import functools

import jax
import jax.numpy as jnp
from jax import lax
from jax.experimental import pallas as pl
from jax.experimental.pallas import tpu as pltpu

F32 = jnp.float32
BF16 = jnp.bfloat16
MESH = pl.DeviceIdType.MESH

D_MODEL = 1024
ATTN_WIDTH = 512
SGU_WIDTH = 256
MEM_WIDTH = 256
N_MEM = 256
IN_COLS = 3328
QKV_COLS = 3 * ATTN_WIDTH
REST_COLS = IN_COLS - QKV_COLS
SGU_CHUNK = 128
N_SGU_GROUPS = 4
EPS = 1e-6
NEG_INF = -1e30
DILATIONS = (1, 4, 16)
RADIUS = 64
Q_BLOCK = 128
LANES = 128
HEAD_DIM = 64

ADAM_LR = 0.001
ADAM_B1 = 0.9
ADAM_B2 = 0.999
ADAM_EPS = 1e-08
ADAM_WD = 0.01
ADAM_STEP = 10

N_CHIPS = 4
VMEM_LIMIT = 56 * 1024 * 1024
SMALL_ROWS = 560


def _params(sem=None, vmem=VMEM_LIMIT):
    return pltpu.CompilerParams(dimension_semantics=sem, vmem_limit_bytes=vmem)


def _nn(a, b):
    return jnp.dot(a, b, preferred_element_type=F32)


def _nt(a, b):
    return lax.dot_general(a, b, (((1,), (1,)), ((), ())), preferred_element_type=F32)


def _tn(a, b):
    return lax.dot_general(a, b, (((0,), (0,)), ((), ())), preferred_element_type=F32)


def _rms(x):
    r = lax.rsqrt(jnp.mean(x * x, axis=-1, keepdims=True) + EPS)
    return r, x * r


def _head_masks():
    lane = lax.broadcasted_iota(jnp.int32, (1, LANES), 1)
    lo = lane < HEAD_DIM
    return lo, (lo.astype(F32), (~lo).astype(F32))


def _silu_parts(z):
    s = jax.nn.sigmoid(z)
    return z * s, s * (1.0 + z * (1.0 - s))


def _gelu_parts(x):
    c = 0.7978845608028654
    x2 = x * x
    s = jax.nn.sigmoid((2.0 * c) * (x + 0.044715 * (x * x2)))
    return x * s, s * (1.0 + x * (1.0 - s) * ((2.0 * c) * (1.0 + 3.0 * 0.044715 * x2)))


def _after(tokens):
    return [pl.BlockSpec(memory_space=pl.ANY)] * len(tokens)


def _inproj_fwd(x2d, g_norm, w_in_t, after=()):
    T = x2d.shape[0]
    tm = 512

    def body(x_ref, g_ref, w_ref, *rest):
        o_ref = rest[-1]
        _, xh = _rms(x_ref[...])
        h = (xh * g_ref[...]).astype(BF16)
        o_ref[...] = _nt(h, w_ref[...])

    return pl.pallas_call(
        body, grid=(T // tm,),
        in_specs=[pl.BlockSpec((tm, D_MODEL), lambda i: (i, 0)),
                  pl.BlockSpec((1, D_MODEL), lambda i: (0, 0)),
                  pl.BlockSpec((IN_COLS, D_MODEL), lambda i: (0, 0))] + _after(after),
        out_specs=pl.BlockSpec((tm, IN_COLS), lambda i: (i, 0)),
        out_shape=jax.ShapeDtypeStruct((T, IN_COLS), F32),
        compiler_params=_params(("arbitrary",)), name="inproj_fwd")(x2d, g_norm, w_in_t, *after)


def _kv_fwd(mem2d, g_mem, w_kv):
    Tm = mem2d.shape[0]

    def body(m_ref, g_ref, w_ref, o_ref):
        _, mh = _rms(m_ref[...])
        o_ref[...] = _nn((mh * g_ref[...]).astype(BF16), w_ref[...])

    return pl.pallas_call(
        body, out_shape=jax.ShapeDtypeStruct((Tm, 2 * MEM_WIDTH), F32),
        compiler_params=_params(), name="kv_fwd")(mem2d, g_mem, w_kv)


def _kv_bwd(mem2d, g_mem, w_kv, dkv):
    Tm = mem2d.shape[0]

    def body(m_ref, g_ref, w_ref, dkv_ref, dw_ref, dg_ref):
        _, mh = _rms(m_ref[...])
        memn = (mh * g_ref[...]).astype(BF16)
        dkvb = dkv_ref[...].astype(BF16)
        dw = _tn(memn, dkvb).astype(BF16)
        for k in range(N_CHIPS):
            dw_ref[k] = dw[k * (D_MODEL // N_CHIPS):(k + 1) * (D_MODEL // N_CHIPS), :]
        dmemn = _nt(dkvb, w_ref[...])
        dg_ref[...] = jnp.sum(dmemn * mh, axis=0, keepdims=True)

    return pl.pallas_call(
        body, out_shape=(jax.ShapeDtypeStruct((N_CHIPS, D_MODEL // N_CHIPS, 2 * MEM_WIDTH), BF16),
                         jax.ShapeDtypeStruct((1, D_MODEL), F32)),
        compiler_params=_params(), name="kv_bwd")(mem2d, g_mem, w_kv, dkv)


def _attn_geometry(S):
    geom = []
    for d in DILATIONS:
        L = S // d
        assert L % Q_BLOCK == 0
        geom.append((d, L, min(2 * Q_BLOCK, L), L // Q_BLOCK))
    return geom


def _init_bias(bias_scr, geom, hp):
    row = lax.broadcasted_iota(jnp.int32, (Q_BLOCK, 2 * Q_BLOCK), 0)
    col = lax.broadcasted_iota(jnp.int32, (Q_BLOCK, 2 * Q_BLOCK), 1)
    for j in (0, 1):
        bits = (126 - (2 * hp + j)) * (1 << 23)
        slope = lax.bitcast_convert_type(jnp.full((1, 1), bits, jnp.int32), F32)
        for di, (d, _, _, _) in enumerate(geom):
            for cls, off in enumerate((0, -RADIUS, -2 * RADIUS)):
                dist = jnp.abs(col - row + off)
                bias_scr[di * 6 + cls * 2 + j] = jnp.where(
                    dist <= RADIUS, -(slope * float(d)) * dist.astype(F32), NEG_INF)


SPLIT = 4
COPY_ROWS = 256


def _by4_rows(S, step):
    per_class = S // SPLIT // COPY_ROWS
    r, j = step // per_class, step % per_class
    return (pl.ds(r + SPLIT * j * COPY_ROWS, COPY_ROWS, stride=SPLIT),
            pl.ds(r * (S // SPLIT) + j * COPY_ROWS, COPY_ROWS))


def _to_by4(src, dst, S):
    for i in range(S // COPY_ROWS):
        natural, by4 = _by4_rows(S, i)
        dst[by4, :] = src[natural, :]


def _block_slices(d, L, KW, nqb, r, qb, S):
    qs = qb * Q_BLOCK
    ks = jnp.clip(qs - RADIUS, 0, L - KW)
    cls = jnp.where(qb == 0, 0, jnp.where(qb == nqb - 1, 2, 1))
    if d == 1:
        qsl = pl.ds(pl.multiple_of(qs, Q_BLOCK), Q_BLOCK)
        ksl = pl.ds(pl.multiple_of(ks, RADIUS), KW)
    elif d == SPLIT:
        qsl = pl.ds(pl.multiple_of(r * L + qs, Q_BLOCK), Q_BLOCK)
        ksl = pl.ds(pl.multiple_of(r * L + ks, RADIUS), KW)
    else:
        sub = d // SPLIT
        base = (r % SPLIT) * (S // SPLIT) + r // SPLIT
        qsl = pl.ds(base + qs * sub, Q_BLOCK, stride=sub)
        ksl = pl.ds(base + ks * sub, KW, stride=sub)
    return qsl, ksl, cls


def _for_groups(geom, S, group, fn):
    for di, (d, L, KW, nqb) in enumerate(geom):
        n = group[di]
        assert (d * nqb) % n == 0

        def step(it, carry, di=di, d=d, L=L, KW=KW, nqb=nqb, n=n):
            slices = []
            for g in range(n):
                i = it * n + g
                slices.append(_block_slices(d, L, KW, nqb, i // nqb, i % nqb, S))
            fn(di, KW, slices)
            return carry
        lax.fori_loop(0, d * nqb // n, step, 0)


def _attn_fwd(proj, B, S):
    T = B * S
    geom = _attn_geometry(S)
    n_pairs = ATTN_WIDTH // LANES

    def body(q_ref, k_ref, v_ref, a_ref, lse_ref, bias_scr, q4, k4, v4, *per_dilation):
        o_scr, m_scr, l_scr = per_dilation[0:3], per_dilation[3:6], per_dilation[6:9]
        lo, hm = _head_masks()
        pair = pl.program_id(0)

        @pl.when(pl.program_id(1) == 0)
        def _():
            _init_bias(bias_scr, geom, pair)
        for src, dst in ((q_ref, q4), (k_ref, k4), (v_ref, v4)):
            _to_by4(src, dst, S)

        def group(di, KW, all_slices):
            run = 8
            for first in range(0, len(all_slices), run):
                some(di, KW, all_slices[first:first + run])

        def some(di, KW, slices):
            chains = [(g, j) for g in range(len(slices)) for j in (0, 1)]
            q_src, k_src, v_src = (q_ref, k_ref, v_ref) if di == 0 else (q4, k4, v4)
            q = [q_src[qsl, :] for qsl, _, _ in slices]
            kw = [k_src[ksl, :].astype(BF16) for _, ksl, _ in slices]
            vw = [v_src[ksl, :].astype(BF16) for _, ksl, _ in slices]
            s = {(g, j): _nt((q[g] * (hm[j] * 0.125)).astype(BF16), kw[g])
                 + bias_scr[di * 6 + slices[g][2] * 2 + j, :, pl.ds(0, KW)] for g, j in chains}
            m = {c: jnp.max(s[c], axis=1, keepdims=True) for c in chains}
            p = {c: jnp.exp(s[c] - m[c]) for c in chains}
            l = {c: jnp.sum(p[c], axis=1, keepdims=True) for c in chains}
            o = {(g, j): _nn(p[(g, j)].astype(BF16), vw[g]) for g, j in chains}
            for g, (qsl, _, _) in enumerate(slices):
                o_scr[di][qsl, :] = jnp.where(lo, o[(g, 0)], o[(g, 1)])
                m_scr[di][qsl, :] = jnp.where(lo, m[(g, 0)], m[(g, 1)])
                l_scr[di][qsl, :] = jnp.where(lo, l[(g, 0)], l[(g, 1)])

        _for_groups(geom, S, (16, 16, 16), group)

        for i in range(S // COPY_ROWS):
            natural, by4 = _by4_rows(S, i)
            rows = [natural, by4, by4]
            ms = [m_scr[di][rows[di], :] for di in range(3)]
            mx = jnp.maximum(jnp.maximum(ms[0], ms[1]), ms[2])
            num = 0.0
            den = 0.0
            for di in range(3):
                w = jnp.exp(ms[di] - mx)
                num = num + w * o_scr[di][rows[di], :]
                den = den + w * l_scr[di][rows[di], :]
            a_ref[natural, :] = num / den
            lse_ref[natural, :] = mx + jnp.log(den)

    blk = lambda off: pl.BlockSpec((S, LANES), lambda h, b, off=off: (b, off + h))
    out_blk = pl.BlockSpec((S, LANES), lambda h, b: (b, h))
    return pl.pallas_call(
        body, grid=(n_pairs, B),
        in_specs=[blk(0), blk(n_pairs), blk(2 * n_pairs)],
        out_specs=[out_blk, out_blk],
        out_shape=[jax.ShapeDtypeStruct((T, ATTN_WIDTH), F32)] * 2,
        scratch_shapes=[pltpu.VMEM((18, Q_BLOCK, 2 * Q_BLOCK), F32)] + [pltpu.VMEM((S, LANES), F32)] * 12,
        compiler_params=_params(("arbitrary", "arbitrary")), name="attn_fwd")(proj, proj, proj)


def _attn_bwd(proj, a, lse, da, B, S, after=()):
    T = B * S
    geom = _attn_geometry(S)
    n_pairs = ATTN_WIDTH // LANES

    def body(q_ref, k_ref, v_ref, a_ref, lse_ref, do_ref, *rest):
        dq_ref, dk_ref, dv_ref, bias_scr = rest[len(after):len(after) + 4]
        scr = rest[len(after) + 4:]
        acc = (scr[0:3], scr[3:6])
        natural_in = (q_ref, k_ref, v_ref, a_ref, lse_ref, do_ref)
        by4_in = scr[6:12]
        _, hm = _head_masks()
        pair = pl.program_id(0)

        @pl.when(pl.program_id(1) == 0)
        def _():
            _init_bias(bias_scr, geom, pair)
        for ref in scr[0:6]:
            ref[...] = jnp.zeros_like(ref)
        for src, dst in zip(natural_in, by4_in):
            _to_by4(src, dst, S)

        def group(di, KW, all_slices):
            run = (4, 4, 8)[di]
            for first in range(0, len(all_slices), run):
                some(di, KW, all_slices[first:first + run])

        def some(di, KW, slices):
            n = len(slices)
            chains = [(g, j) for g in range(n) for j in (0, 1)]
            q_src, k_src, v_src, a_src, lse_src, do_src = natural_in if di == 0 else by4_in
            dq_scr, dk_scr, dv_scr = acc[0 if di == 0 else 1]
            q = [q_src[qsl, :] for qsl, _, _ in slices]
            do = [do_src[qsl, :] for qsl, _, _ in slices]
            doa = [do[g] * a_src[slices[g][0], :] for g in range(n)]
            lse_q = [lse_src[qsl, :] for qsl, _, _ in slices]
            kw = [k_src[ksl, :].astype(BF16) for _, ksl, _ in slices]
            vw = [v_src[ksl, :].astype(BF16) for _, ksl, _ in slices]
            qj = {(g, j): (q[g] * (hm[j] * 0.125)).astype(BF16) for g, j in chains}
            doj = {(g, j): (do[g] * hm[j]).astype(BF16) for g, j in chains}
            s = {(g, j): _nt(qj[(g, j)], kw[g])
                 + bias_scr[di * 6 + slices[g][2] * 2 + j, :, pl.ds(0, KW)] for g, j in chains}
            dp = {(g, j): _nt(doj[(g, j)], vw[g]) for g, j in chains}
            dsum = {(g, j): jnp.sum(doa[g] * hm[j], axis=1, keepdims=True) for g, j in chains}
            p = {(g, j): jnp.exp(s[(g, j)] - lse_q[g][:, HEAD_DIM * j:HEAD_DIM * j + 1]) for g, j in chains}
            ds = {c: (p[c] * (dp[c] - dsum[c])).astype(BF16) for c in chains}
            pb = {c: p[c].astype(BF16) for c in chains}
            dq = [_nn(ds[(g, 0)], kw[g]) * (hm[0] * 0.125) + _nn(ds[(g, 1)], kw[g]) * (hm[1] * 0.125)
                  for g in range(n)]
            both = lambda t, g: jnp.concatenate([t[(g, 0)], t[(g, 1)]], axis=0)
            dkw = [_tn(both(ds, g), both(qj, g)) for g in range(n)]
            dvw = [_tn(both(pb, g), both(doj, g)) for g in range(n)]
            for g, (qsl, ksl, _) in enumerate(slices):
                dq_scr[qsl, :] = dq_scr[qsl, :] + dq[g]
                dk_scr[ksl, :] = dk_scr[ksl, :] + dkw[g]
                dv_scr[ksl, :] = dv_scr[ksl, :] + dvw[g]

        _for_groups(geom, S, (16, 16, 16), group)

        for i in range(S // COPY_ROWS):
            natural, by4 = _by4_rows(S, i)
            for nat, split in zip(*acc):
                nat[natural, :] = nat[natural, :] + split[by4, :]
        for out, nat in zip((dq_ref, dk_ref, dv_ref), acc[0]):
            out[...] = nat[...].astype(BF16)

    blk = lambda off: pl.BlockSpec((S, LANES), lambda h, b, off=off: (b, off + h))
    return pl.pallas_call(
        body, grid=(n_pairs, B),
        in_specs=[blk(0), blk(n_pairs), blk(2 * n_pairs), blk(0), blk(0), blk(0)] + _after(after),
        out_specs=[blk(0), blk(0), blk(0)],
        out_shape=[jax.ShapeDtypeStruct((T, ATTN_WIDTH), BF16)] * 3,
        scratch_shapes=[pltpu.VMEM((18, Q_BLOCK, 2 * Q_BLOCK), F32)] + [pltpu.VMEM((S, LANES), F32)] * 12,
        compiler_params=_params(("arbitrary", "arbitrary")), name="attn_bwd")(proj, proj, proj, a, lse, da, *after)


def _mid(x2d, t2d, a, proj, kv, w_s, w_sT, b_tab, g_v, w_out, g_final, B, S):
    T = B * S
    tm = 512
    nt = S // tm
    halves = 2
    hrows = tm // halves

    def body(x_ref, t_ref, a_ref, za_ref, ub_ref, vb_ref, zb_ref, qm_ref, zm_ref, kv_ref,
              ws_ref, wsT_ref, btab_ref, gv_ref, wout_ref, gf_ref,
              dx2_ref, da_ref, drest_ref, loss_ref, dwout_bf_ref, dws_ref, dbs_ref, dgv_ref, dgf_ref, dkv_ref,
              dbtab_scr, dwout_ref):
        b = pl.program_id(0)
        t = pl.program_id(1)
        first = jnp.logical_and(b == 0, t == 0)
        last = jnp.logical_and(b == B - 1, t == nt - 1)
        _, hm = _head_masks()
        lane_g = lax.broadcasted_iota(jnp.int32, (1, SGU_WIDTH), 1) // HEAD_DIM
        gm = [(lane_g == g).astype(F32) for g in range(N_SGU_GROUPS)]
        H = range(halves)
        rows = [pl.ds(h * hrows, hrows) for h in H]
        ld = lambda ref: [ref[r, :] for r in rows]
        cat = lambda parts, axis: jnp.concatenate(parts, axis=axis)
        chunks = [slice(ci * SGU_CHUNK, (ci + 1) * SGU_CHUNK) for ci in range(hrows // SGU_CHUNK)]
        pairs = [slice(pr * LANES, (pr + 1) * LANES) for pr in range(2)]
        heads = [(pr, j) for pr in range(2) for j in (0, 1)]

        @pl.when(first)
        def _():
            loss_ref[...] = jnp.zeros_like(loss_ref)
            dwout_ref[...] = jnp.zeros_like(dwout_ref)
            dws_ref[...] = jnp.zeros_like(dws_ref)
            dbs_ref[...] = jnp.zeros_like(dbs_ref)
            dgv_ref[...] = jnp.zeros_like(dgv_ref)
            dgf_ref[...] = jnp.zeros_like(dgf_ref)
            dbtab_scr[...] = jnp.zeros_like(dbtab_scr)

        @pl.when(t == 0)
        def _():
            dkv_ref[...] = jnp.zeros_like(dkv_ref)

        a_val = ld(a_ref)
        sil_a = [_silu_parts(z) for z in ld(za_ref)]
        gated_a = [s[0] * a for s, a in zip(sil_a, a_val)]
        u = [_gelu_parts(z) for z in ld(ub_ref)]
        vv = [_gelu_parts(z) for z in ld(vb_ref)]
        vnorm = [_rms(v[0]) for v in vv]
        gv = gv_ref[...]
        vn = [(n[1] * gv).astype(BF16) for n in vnorm]
        w_cat = cat([ws_ref[g].astype(BF16) for g in range(N_SGU_GROUPS)], 1)
        wT_cat = cat([wsT_ref[g].astype(BF16) for g in range(N_SGU_GROUPS)], 1)
        gmb = [m.astype(BF16) for m in gm]
        by_group = lambda chunk: cat([chunk * gmb[g] for g in range(N_SGU_GROUPS)], 0)
        btab = btab_ref[...]
        mixed = [cat([btab + _nn(w_cat, by_group(vn[h][c, :])) for c in chunks], 0) for h in H]
        sg = [u[h][0] * mixed[h] for h in H]
        sil_b = [_silu_parts(z) for z in ld(zb_ref)]
        gated_b = [sil_b[h][0] * sg[h] for h in H]

        kvv = kv_ref[...].astype(BF16)
        kp = [kvv[:, p] for p in pairs]
        vp = [kvv[:, MEM_WIDTH + pr * LANES:MEM_WIDTH + (pr + 1) * LANES] for pr in range(2)]
        qm = ld(qm_ref)
        qj = {(h, pr, j): (qm[h][:, pairs[pr]] * (hm[j] * 0.125)).astype(BF16) for h in H for pr, j in heads}
        sc = {k: _nt(qj[k], kp[k[1]]) for k in qj}
        ex = {k: jnp.exp(sc[k] - jnp.max(sc[k], axis=1, keepdims=True)) for k in qj}
        prob = {k: ex[k] * (1.0 / jnp.sum(ex[k], axis=1, keepdims=True)) for k in qj}
        probb = {k: prob[k].astype(BF16) for k in qj}
        mo = [cat([sum(_nn(probb[(h, pr, j)], vp[pr]) * hm[j] for j in (0, 1)) for pr in range(2)], 1) for h in H]
        sil_m = [_silu_parts(z) for z in ld(zm_ref)]
        gated_m = [sil_m[h][0] * mo[h] for h in H]

        gated = [cat([gated_a[h], gated_b[h], gated_m[h]], 1).astype(BF16) for h in H]
        wout = wout_ref[...]
        x_in = ld(x_ref)
        x2 = [x_in[h] + _nn(gated[h], wout) for h in H]
        fin = [_rms(z) for z in x2]
        gf = gf_ref[...]
        tgt = ld(t_ref)
        err = [fin[h][1] * gf - tgt[h] for h in H]
        loss_ref[...] += sum(jnp.sum(e * e) for e in err) * (0.5 / D_MODEL)

        dy = [e * (1.0 / D_MODEL) for e in err]
        dgf_ref[...] += sum(jnp.sum(dy[h] * fin[h][1], axis=0, keepdims=True) for h in H)
        gdy = [d * gf for d in dy]
        dx2 = [fin[h][0] * (gdy[h] - fin[h][1] * jnp.mean(gdy[h] * fin[h][1], axis=1, keepdims=True)) for h in H]
        for h in H:
            dx2_ref[rows[h], :] = dx2[h]
        dx2b = [d.astype(BF16) for d in dx2]
        dgated = [_nt(d, wout) for d in dx2b]
        dwout_ref[...] += _tn(cat(gated, 0), cat(dx2b, 0))
        dga = [d[:, 0:ATTN_WIDTH] for d in dgated]
        dgb = [d[:, ATTN_WIDTH:ATTN_WIDTH + SGU_WIDTH] for d in dgated]
        dgm = [d[:, ATTN_WIDTH + SGU_WIDTH:] for d in dgated]

        for h in H:
            da_ref[rows[h], :] = dga[h] * sil_a[h][0]
        dza = [dga[h] * a_val[h] * sil_a[h][1] for h in H]

        dsg = [dgb[h] * sil_b[h][0] for h in H]
        dzb = [dgb[h] * sg[h] * sil_b[h][1] for h in H]
        dub = [dsg[h] * mixed[h] * u[h][1] for h in H]
        dmixed = [dsg[h] * u[h][0] for h in H]
        dmixed_b = [d.astype(BF16) for d in dmixed]
        dvn = [cat([_nn(wT_cat, by_group(dmixed_b[h][c, :])) for c in chunks], 0) for h in H]
        for g in range(N_SGU_GROUPS):
            dws_ref[g] += sum(_nt((dmixed[h][c, :] * gm[g]).astype(BF16), vn[h][c, :]) for h in H for c in chunks)
        dbtab_scr[...] += sum(dmixed[h][c, :] for h in H for c in chunks)
        dgv_ref[...] += sum(jnp.sum(dvn[h] * vnorm[h][1], axis=0, keepdims=True) for h in H)
        tv = [d * gv for d in dvn]
        dvv = [vnorm[h][0] * (tv[h] - vnorm[h][1] * jnp.mean(tv[h] * vnorm[h][1], axis=1, keepdims=True)) for h in H]
        dvb = [dvv[h] * vv[h][1] for h in H]

        dmo = [dgm[h] * sil_m[h][0] for h in H]
        dzm = [dgm[h] * mo[h] * sil_m[h][1] for h in H]
        dmoj = {(h, pr, j): (dmo[h][:, pairs[pr]] * hm[j]).astype(BF16) for h in H for pr, j in heads}
        dp = {k: _nt(dmoj[k], vp[k[1]]) for k in qj}
        ds = {k: (prob[k] * (dp[k] - jnp.sum(dp[k] * prob[k], axis=1, keepdims=True))).astype(BF16) for k in qj}
        dqm = [cat([sum(_nn(ds[(h, pr, j)], kp[pr]) * (hm[j] * 0.125) for j in (0, 1)) for pr in range(2)], 1)
               for h in H]
        every = lambda tbl, pr: cat([tbl[(h, pr, j)] for h in H for j in (0, 1)], 0)
        dk = [_tn(every(ds, pr), every(qj, pr)) for pr in range(2)]
        dv = [_tn(every(probb, pr), every(dmoj, pr)) for pr in range(2)]
        dkv_ref[...] += cat(dk + dv, 1)

        for h in H:
            drest_ref[rows[h], :] = cat([dza[h], dub[h], dvb[h], dzb[h], dqm[h], dzm[h]], 1).astype(BF16)

        @pl.when(last)
        def _():
            lane = lax.broadcasted_iota(jnp.int32, (1, LANES), 1)
            dbt = dbtab_scr[...]
            out = jnp.zeros((SGU_CHUNK, LANES), F32)
            for g in range(N_SGU_GROUPS):
                out = out + jnp.where(lane == g, jnp.sum(dbt * gm[g], axis=1, keepdims=True), 0.0)
            dbs_ref[...] = out
            for r0 in range(0, D_MODEL, SGU_CHUNK):
                k, row = divmod(r0, D_MODEL // N_CHIPS)
                dwout_bf_ref[k, row:row + SGU_CHUNK, :] = dwout_ref[r0:r0 + SGU_CHUNK, :].astype(BF16)

    tile = lambda w, cb: pl.BlockSpec((tm, w), lambda b, t, cb=cb: (b * nt + t, cb))
    const = lambda shape: pl.BlockSpec(shape, lambda b, t, n=len(shape): (0,) * n)
    return pl.pallas_call(
        body, grid=(B, nt),
        in_specs=[tile(D_MODEL, 0), tile(D_MODEL, 0), tile(ATTN_WIDTH, 0),
                  tile(ATTN_WIDTH, 3),
                  tile(SGU_WIDTH, 8), tile(SGU_WIDTH, 9), tile(SGU_WIDTH, 10),
                  tile(MEM_WIDTH, 11), tile(MEM_WIDTH, 12),
                  pl.BlockSpec((N_MEM, 2 * MEM_WIDTH), lambda b, t: (b, 0)),
                  const((N_SGU_GROUPS, SGU_CHUNK, SGU_CHUNK)), const((N_SGU_GROUPS, SGU_CHUNK, SGU_CHUNK)),
                  const((SGU_CHUNK, SGU_WIDTH)), const((1, SGU_WIDTH)),
                  const((D_MODEL, D_MODEL)), const((1, D_MODEL))],
        out_specs=[tile(D_MODEL, 0), tile(ATTN_WIDTH, 0), tile(REST_COLS, 0),
                   const((8, LANES)), const((N_CHIPS, D_MODEL // N_CHIPS, D_MODEL)),
                   const((N_SGU_GROUPS, SGU_CHUNK, SGU_CHUNK)), const((SGU_CHUNK, LANES)),
                   const((1, SGU_WIDTH)), const((1, D_MODEL)),
                   pl.BlockSpec((N_MEM, 2 * MEM_WIDTH), lambda b, t: (b, 0))],
        out_shape=[jax.ShapeDtypeStruct((T, D_MODEL), F32), jax.ShapeDtypeStruct((T, ATTN_WIDTH), F32),
                   jax.ShapeDtypeStruct((T, REST_COLS), BF16),
                   jax.ShapeDtypeStruct((8, LANES), F32),
                   jax.ShapeDtypeStruct((N_CHIPS, D_MODEL // N_CHIPS, D_MODEL), BF16),
                   jax.ShapeDtypeStruct((N_SGU_GROUPS, SGU_CHUNK, SGU_CHUNK), F32),
                   jax.ShapeDtypeStruct((SGU_CHUNK, LANES), F32),
                   jax.ShapeDtypeStruct((1, SGU_WIDTH), F32), jax.ShapeDtypeStruct((1, D_MODEL), F32),
                   jax.ShapeDtypeStruct((B * N_MEM, 2 * MEM_WIDTH), F32)],
        scratch_shapes=[pltpu.VMEM((SGU_CHUNK, SGU_WIDTH), F32), pltpu.VMEM((D_MODEL, D_MODEL), F32)],
        compiler_params=_params(("arbitrary", "arbitrary"), vmem=VMEM_LIMIT + 2 * 1024 * 1024), name="mid")(
            x2d, t2d, a, proj, proj, proj, proj, proj, proj, kv, w_s, w_sT, b_tab, g_v, w_out, g_final)


def _inproj_bwd_dx(dq, dk, dv, drest, x2d, dx2, g_norm, w_in_t, after=()):
    T = x2d.shape[0]
    tm = 512
    W = ATTN_WIDTH

    def body(dq_ref, dk_ref, dv_ref, dr_ref, x_ref, dx2_ref, g_ref, w_ref, *rest):
        gx_ref, dg_ref = rest[-2:]

        @pl.when(pl.program_id(0) == 0)
        def _():
            dg_ref[...] = jnp.zeros_like(dg_ref)

        halves = [pl.ds(h * (tm // 2), tm // 2) for h in (0, 1)]
        dh = [(_nn(dq_ref[r, :], w_ref[0:W, :]) + _nn(dk_ref[r, :], w_ref[W:2 * W, :])
               + _nn(dv_ref[r, :], w_ref[2 * W:3 * W, :]) + _nn(dr_ref[r, :], w_ref[QKV_COLS:IN_COLS, :]))
              for r in halves]
        nrm = [_rms(x_ref[r, :]) for r in halves]
        dg_ref[...] += sum(jnp.sum(d * n[1], axis=0, keepdims=True) for d, n in zip(dh, nrm))
        g = g_ref[...]
        for r, d, (rstd, xh) in zip(halves, dh, nrm):
            th = d * g
            gx_ref[r, :] = rstd * (th - xh * jnp.mean(th * xh, axis=1, keepdims=True)) + dx2_ref[r, :]

    tile = lambda w: pl.BlockSpec((tm, w), lambda i: (i, 0))
    return pl.pallas_call(
        body, grid=(T // tm,),
        in_specs=[tile(W), tile(W), tile(W), tile(REST_COLS), tile(D_MODEL), tile(D_MODEL),
                  pl.BlockSpec((1, D_MODEL), lambda i: (0, 0)),
                  pl.BlockSpec((IN_COLS, D_MODEL), lambda i: (0, 0))] + _after(after),
        out_specs=[tile(D_MODEL), pl.BlockSpec((1, D_MODEL), lambda i: (0, 0))],
        out_shape=[jax.ShapeDtypeStruct((T, D_MODEL), F32), jax.ShapeDtypeStruct((1, D_MODEL), F32)],
        compiler_params=_params(("arbitrary",)), name="inproj_bwd_dx")(
            dq, dk, dv, drest, x2d, dx2, g_norm, w_in_t, *after)


def _inproj_bwd_dw(dq, dk, dv, drest, x2d, g_norm, reduce_with=None):
    T = x2d.shape[0]
    tm = 512
    nt = T // tm
    W = ATTN_WIDTH
    fused = reduce_with is not None
    others = list(reduce_with) if fused else []
    ns = 1 + len(others)
    shard = IN_COLS // N_CHIPS
    halves = [shard // 2] + [s.shape[1] // 2 for s in others]
    cols = [D_MODEL] + [s.shape[2] for s in others]
    row_block = 32

    def body(dq_ref, dk_ref, dv_ref, dr_ref, x_ref, g_ref, *rest):
        if fused:
            stacks = rest[:ns - 1]
            sends, owns = rest[ns - 1:2 * ns - 1], rest[2 * ns - 1:3 * ns - 1]
            acc, ras, narrow = rest[3 * ns - 1], rest[3 * ns:4 * ns], rest[4 * ns]
            s_sem, r_sem = rest[4 * ns + 1], rest[4 * ns + 2]
            x, y, c, chip, peers, peer_chip = _place()
            sib = (x, y, 1 - c)

            def part(w, k, cc, r0=0, rows=None):
                n = halves[w]
                rows = n if rows is None else rows
                if w == 0:
                    return acc.at[pl.ds(pl.multiple_of(k * shard + cc * n + r0, 8), rows), :]
                return stacks[w - 1].at[k, pl.ds(pl.multiple_of(cc * n + r0, 8), rows), :]

            def swap_other(w):
                theirs = stacks[w - 1].at[:, pl.ds(pl.multiple_of((1 - c) * halves[w], 8), halves[w]), :]
                return _remote(theirs, ras[w], s_sem.at[N_CHIPS - 1 + w], r_sem.at[N_CHIPS - 1 + w], sib)

            def swap_win(k):
                return _remote(narrow.at[k], ras[0].at[k], s_sem.at[k], r_sem.at[k], sib)
        else:
            acc = rest[0]

        @pl.when(pl.program_id(0) == 0)
        def _():
            acc[...] = jnp.zeros_like(acc)
            for w in range(1, ns):
                swap_other(w).start()

        _, xh = _rms(x_ref[...])
        h = (xh * g_ref[...]).astype(BF16)
        acc[0:W, :] += _tn(dq_ref[...], h)
        acc[W:2 * W, :] += _tn(dk_ref[...], h)
        acc[2 * W:3 * W, :] += _tn(dv_ref[...], h)
        acc[QKV_COLS:IN_COLS, :] += _tn(dr_ref[...], h)

        if fused:
            @pl.when(pl.program_id(0) == nt - 1)
            def _():
                for k in range(N_CHIPS):
                    def to_bf16(i, carry, k=k):
                        r0 = pl.multiple_of(i * row_block, row_block)
                        narrow[k, pl.ds(r0, row_block), :] = part(0, k, 1 - c, r0, row_block)[...].astype(BF16)
                        return carry
                    lax.fori_loop(0, halves[0] // row_block, to_bf16, 0)
                    swap_win(k).start()
                def chip_sum(w, k, r0):
                    blk = pl.ds(r0, row_block)
                    return part(w, k, c, r0, row_block)[...] + ras[w][k, blk, :].astype(F32)

                for w in range(1, ns):
                    swap_other(w).wait_recv()

                    def sums(i, carry, w=w):
                        r0 = pl.multiple_of(i * row_block, row_block)
                        for m in range(3):
                            sends[w][m, pl.ds(r0, row_block), :] = chip_sum(w, peer_chip[m], r0).astype(BF16)
                        owns[w][pl.ds(r0, row_block), :] = chip_sum(w, chip, r0)
                        return carry
                    lax.fori_loop(0, halves[w] // row_block, sums, 0)
                for k in range(N_CHIPS):
                    swap_win(k).wait_recv()

                    @pl.when(chip == k)
                    def _(k=k):
                        def own(i, carry):
                            r0 = pl.multiple_of(i * row_block, row_block)
                            owns[0][pl.ds(r0, row_block), :] = chip_sum(0, k, r0)
                            return carry
                        lax.fori_loop(0, halves[0] // row_block, own, 0)

                    @pl.when(chip != k)
                    def _(k=k):
                        def other(i, carry):
                            r0 = pl.multiple_of(i * row_block, row_block)
                            sends[0][(k ^ chip) - 1, pl.ds(r0, row_block), :] = chip_sum(0, k, r0).astype(BF16)
                            return carry
                        lax.fori_loop(0, halves[0] // row_block, other, 0)
                for k in range(N_CHIPS):
                    swap_win(k).wait_send()
                for w in range(1, ns):
                    swap_other(w).wait_send()

    tile = lambda w: pl.BlockSpec((tm, w), lambda i: (i, 0))
    vmem = pl.BlockSpec(memory_space=pltpu.VMEM)
    in_specs = [tile(W), tile(W), tile(W), tile(REST_COLS), tile(D_MODEL), pl.BlockSpec((1, D_MODEL), lambda i: (0, 0))]
    if not fused:
        return pl.pallas_call(
            body, grid=(nt,), in_specs=in_specs,
            out_specs=pl.BlockSpec((IN_COLS, D_MODEL), lambda i: (0, 0)),
            out_shape=jax.ShapeDtypeStruct((IN_COLS, D_MODEL), F32),
            compiler_params=_params(("arbitrary",)), name="inproj_bwd_dw")(dq, dk, dv, drest, x2d, g_norm)
    outs = pl.pallas_call(
        body, grid=(nt,), in_specs=in_specs + [vmem] * (ns - 1), out_specs=[vmem] * (2 * ns),
        out_shape=[jax.ShapeDtypeStruct((3, n, cl), BF16) for n, cl in zip(halves, cols)]
        + [jax.ShapeDtypeStruct((n, cl), F32) for n, cl in zip(halves, cols)],
        scratch_shapes=[pltpu.VMEM((IN_COLS, D_MODEL), F32)]
        + [pltpu.VMEM((N_CHIPS, n, cl), BF16 if w == 0 else F32) for w, (n, cl) in enumerate(zip(halves, cols))]
        + [pltpu.VMEM((N_CHIPS, halves[0], D_MODEL), BF16)]
        + [pltpu.SemaphoreType.DMA((N_CHIPS - 1 + ns,)), pltpu.SemaphoreType.DMA((N_CHIPS - 1 + ns,))],
        compiler_params=_params(("arbitrary",)), name="inproj_bwd_dw_reduce")(
            dq, dk, dv, drest, x2d, g_norm, *others)
    return outs[:ns], outs[ns:]


def _adamw_update(w, g, m, v):
    nm = ADAM_B1 * m + (1.0 - ADAM_B1) * g
    nv = ADAM_B2 * v + (1.0 - ADAM_B2) * (g * g)
    m_hat = nm / (1.0 - ADAM_B1 ** ADAM_STEP)
    v_hat = nv / (1.0 - ADAM_B2 ** ADAM_STEP)
    return -ADAM_LR * (m_hat / (jnp.sqrt(v_hat) + ADAM_EPS) + ADAM_WD * w), nm, nv


def _adamw(w, g, m, v, name):
    R, C = w.shape
    br = max(r for r in range(8, 257, 8) if R % r == 0)

    def body(w_ref, g_ref, m_ref, v_ref, g_out, d_ref, nm_ref, nv_ref):
        g = g_ref[...]
        g_out[...] = g
        d_ref[...], nm_ref[...], nv_ref[...] = _adamw_update(w_ref[...], g, m_ref[...], v_ref[...])

    spec = pl.BlockSpec((br, C), lambda i: (i, 0))
    return pl.pallas_call(
        body, grid=(R // br,), in_specs=[spec] * 4, out_specs=[spec] * 4,
        out_shape=[jax.ShapeDtypeStruct((R, C), F32)] * 4,
        compiler_params=_params(("arbitrary",)), name=name)(w, g, m, v)


def _adamw_rest(g_packed, ws, ms, vs, whole):
    n = len(ws)
    nw = len(whole)
    row_block = 64

    def body(*refs):
        g_ref = refs[0]
        w_refs, m_refs, v_refs = refs[1:1 + n], refs[1 + n:1 + 2 * n], refs[1 + 2 * n:1 + 3 * n]
        whole_in = [refs[1 + 3 * n + 4 * i:5 + 3 * n + 4 * i] for i in range(nw)]
        outs = refs[1 + 3 * n + 4 * nw:]
        off = 0
        for i, (_, used, padded) in enumerate(_SMALL_PARTS[:n]):
            g = g_ref[off:off + used, :]
            delta, nm, nv = _adamw_update(w_refs[i][...], g, m_refs[i][...], v_refs[i][...])
            outs[4 * i][...], outs[4 * i + 1][...], outs[4 * i + 2][...], outs[4 * i + 3][...] = g, delta, nm, nv
            off += padded
        for i, (w_ref, gw_ref, m_ref, v_ref) in enumerate(whole_in):
            for r0 in range(0, w_ref.shape[0], row_block):
                blk = pl.ds(r0, row_block)
                g = gw_ref[blk, :]
                new = _adamw_update(w_ref[blk, :], g, m_ref[blk, :], v_ref[blk, :])
                for ref, val in zip(outs[4 * (n + i):4 * (n + i) + 4], (g,) + new):
                    ref[blk, :] = val
        outs[4 * (n + nw)][...] = g_ref[_LOSS_ROW:_LOSS_ROW + 1, 0:1]

    outs = pl.pallas_call(
        body, out_shape=[jax.ShapeDtypeStruct(w.shape, F32) for w in ws for _ in range(4)]
        + [jax.ShapeDtypeStruct(four[0].shape, F32) for four in whole for _ in range(4)]
        + [jax.ShapeDtypeStruct((1, 1), F32)],
        compiler_params=_params(), name="adamw_rest")(g_packed, *ws, *ms, *vs, *[a for four in whole for a in four])
    return ([outs[4 * i:4 * i + 4] for i in range(n)], [outs[4 * (n + i):4 * (n + i) + 4] for i in range(nw)],
            outs[4 * (n + nw)])


def _place():
    x, y, c = lax.axis_index("x"), lax.axis_index("y"), lax.axis_index("c")
    chip = 2 * x + y
    peers = [(x, 1 - y), (1 - x, y), (1 - x, 1 - y)]
    peer_chip = [2 * px + py for px, py in peers]
    return x, y, c, chip, peers, peer_chip


def _remote(src, dst, send_sem, recv_sem, dev):
    return pltpu.make_async_remote_copy(src_ref=src, dst_ref=dst, send_sem=send_sem, recv_sem=recv_sem,
                                        device_id=dev, device_id_type=MESH)


def _ag_weights(weights, late=()):
    nw, nl = len(weights), len(late)
    pieces = 2

    def body(*refs):
        srcs, late_srcs = refs[:nw], refs[nw:nw + nl]
        outs, late_bf, late_land = (refs[nw + nl:2 * nw + nl], refs[2 * nw + nl:2 * nw + 2 * nl],
                                    refs[2 * nw + 2 * nl:2 * nw + 3 * nl])
        s_ici, r_ici, s_d2d, r_d2d = refs[2 * nw + 3 * nl:]
        x, y, c = lax.axis_index("x"), lax.axis_index("y"), lax.axis_index("c")
        chip = 2 * x + y
        sib = (x, y, 1 - c)
        first = ((x + 1 - c) % 2, (y + c) % 2)
        second = ((x + c) % 2, (y + 1 - c) % 2)
        first_chip, second_chip = 2 * first[0] + first[1], 2 * second[0] + second[1]
        diag_chip = 3 - chip
        for src, out in zip(srcs, outs):
            out[chip] = src[...].astype(BF16)

        parts = [(w, out, pc) for w, out in enumerate(outs) for pc in range(pieces)]

        def piece(out, k, cc, pc):
            rows = out.shape[1] // 2 // pieces
            return out.at[k, pl.ds(pl.multiple_of((cc * pieces + pc) * rows, 16), rows), :]

        def ici(w, slot, out, k, dev, pc):
            blk, sem = piece(out, k, c, pc), (nw * slot + w) * pieces + pc
            return _remote(blk, blk, s_ici.at[sem], r_ici.at[sem], (dev[0], dev[1], c))

        def d2d(w, slot, out, k, cc, pc):
            blk, sem = piece(out, k, cc, pc), (nw * slot + w) * pieces + pc
            return _remote(blk, blk, s_d2d.at[sem], r_d2d.at[sem], sib)

        sent = []
        for slot, dev in enumerate((first, second)):
            for w, out, pc in parts:
                sent.append(ici(w, slot, out, chip, dev, pc))
                sent[-1].start()
        for src, bf, land in zip(late_srcs, late_bf, late_land):
            bf[...] = src[...].astype(BF16)
            land[...] = jnp.zeros_like(land)
            land[chip] = bf[...]
        for slot, k, dev in ((0, first_chip, first), (1, second_chip, second), (2, diag_chip, second)):
            for w, out, pc in parts:
                ici(w, slot, out, k, dev, pc).wait_recv()
                if slot == 0:
                    sent.append(ici(w, 2, out, k, second, pc))
                    sent[-1].start()
                sent.append(d2d(w, slot, out, k, c, pc))
                sent[-1].start()
        for slot, k in ((0, second_chip), (1, first_chip), (2, diag_chip)):
            for w, out, pc in parts:
                d2d(w, slot, out, k, 1 - c, pc).wait_recv()
        for cp in sent:
            cp.wait_send()

    vmem = pl.BlockSpec(memory_space=pltpu.VMEM)
    outs = pl.pallas_call(
        body,
        out_shape=[jax.ShapeDtypeStruct((N_CHIPS,) + w.shape, BF16) for w in weights]
        + [jax.ShapeDtypeStruct(w.shape, BF16) for w in late]
        + [jax.ShapeDtypeStruct((N_CHIPS,) + w.shape, BF16) for w in late],
        in_specs=[vmem] * (nw + nl), out_specs=[vmem] * (nw + 2 * nl),
        scratch_shapes=[pltpu.SemaphoreType.DMA((3 * nw * pieces,))] * 4,
        compiler_params=pltpu.CompilerParams(vmem_limit_bytes=VMEM_LIMIT), name="ag_weights")(*weights, *late)
    return outs[:nw], outs[nw:nw + nl], outs[nw + nl:]


_HBM = pl.BlockSpec(memory_space=pltpu.HBM)
_SEM = pl.BlockSpec(memory_space=pltpu.SEMAPHORE)
_ANY = pl.BlockSpec(memory_space=pl.ANY)
_DATAFLOW = pltpu.SideEffectType.DATAFLOW_SIDE_EFFECTING


def _in_hbm(a):
    return pltpu.with_memory_space_constraint(a, pltpu.HBM)


_PEERS_OF = {"gather": 3, "scatter": 3, "direct": 7}


def _exchange_copies(mode, srcs, lands, send_sems, recv_sems):
    nw = len(srcs)
    x, y, c, chip, peers, peer_chip = _place()
    pairs = []
    if mode == "direct":
        targets = [((x, y), chip, 1)] + [(p, k, d) for p, k in zip(peers, peer_chip) for d in (0, 1)]
        for r, ((px, py), k, d) in enumerate(targets):
            core = (c + d) % 2
            for w in range(nw):
                n = srcs[w].shape[1] // 2
                src = srcs[w].at[k, pl.ds(pl.multiple_of(core * n, 16), n), :]
                sems = (send_sems.at[nw * r + w], recv_sems.at[nw * r + w], (px, py, core))
                pairs.append((_remote(src, lands[w].at[r], *sems),) * 2)
        return pairs
    gather = mode == "gather"
    for m, (px, py) in enumerate(peers):
        for w in range(nw):
            sems = (send_sems.at[nw * m + w], recv_sems.at[nw * m + w], (px, py, c))
            if gather:
                pairs.append((_remote(srcs[w], lands[w].at[chip], *sems),
                              _remote(srcs[w], lands[w].at[peer_chip[m]], *sems)))
            else:
                pairs.append((_remote(srcs[w].at[m], lands[w].at[m], *sems),) * 2)
    return pairs


def _exchange_start(mode, srcs, after, name, lands=None):
    nw = len(srcs)
    n_copies = _PEERS_OF[mode] * nw

    after = tuple(after)

    def body(*refs):
        send_sems, recv_sems = refs[2 * nw + len(after)], refs[2 * nw + len(after) + 1]
        for start, _ in _exchange_copies(mode, refs[:nw], refs[nw:2 * nw], send_sems, recv_sems):
            start.start()
        refs[-1][...] = jnp.zeros_like(refs[-1])

    if lands is None:
        shape = {"gather": lambda s: (N_CHIPS,) + s.shape, "scatter": lambda s: s.shape,
                 "direct": lambda s: (_PEERS_OF["direct"], s.shape[1] // 2, s.shape[2])}[mode]
        lands = [lax.empty(shape(s), s.dtype) for s in srcs]
    lands = [_in_hbm(l) for l in lands]
    return pl.pallas_call(
        body, name=name,
        out_shape=(pltpu.SemaphoreType.DMA((n_copies,)), pltpu.SemaphoreType.DMA((n_copies,)))
        + tuple(pltpu.HBM(s.shape, s.dtype) for s in srcs)
        + tuple(pltpu.HBM(l.shape, l.dtype) for l in lands)
        + (jax.ShapeDtypeStruct((8, LANES), F32),),
        in_specs=[_HBM] * (2 * nw) + [_ANY] * len(after),
        out_specs=(_SEM, _SEM) + (_HBM,) * (2 * nw) + (pl.BlockSpec(memory_space=pltpu.VMEM),),
        input_output_aliases={i: 2 + i for i in range(2 * nw)},
        compiler_params=pltpu.CompilerParams(has_side_effects=_DATAFLOW),
    )(*[_in_hbm(s) for s in srcs], *lands, *after)


def _exchange_wait(mode, started, after, name):
    nw = (len(started) - 3) // 2
    send_sems, recv_sems = started[0], started[1]
    thru = started[2:2 + 2 * nw]

    def body(*refs):
        for _, arrival in _exchange_copies(mode, refs[:nw], refs[nw:2 * nw], refs[2 * nw], refs[2 * nw + 1]):
            arrival.wait_send()
            arrival.wait_recv()

    outs = pl.pallas_call(
        body, name=name,
        out_shape=tuple(pltpu.HBM(t.shape, t.dtype) for t in thru),
        in_specs=[_HBM] * (2 * nw) + [_SEM, _SEM, _ANY], out_specs=(_HBM,) * (2 * nw),
        input_output_aliases={i: i for i in range(2 * nw)},
        compiler_params=pltpu.CompilerParams(has_side_effects=_DATAFLOW),
    )(*thru, send_sems, recv_sems, after)
    return outs[:nw], outs[nw:]


def _reduce_last(owns, landed, g_small, direct_srcs=(), direct_landed=()):
    ns, nd = len(owns), len(direct_srcs)
    halves = [o.shape[0] for o in owns] + [s.shape[1] // 2 for s in direct_srcs]
    row_block = 32
    hs = SMALL_ROWS // 2

    def body(*refs):
        own_refs, land_refs, gsm_ref = refs[:ns], refs[ns:2 * ns], refs[2 * ns]
        dsrc_refs, dland_refs = refs[2 * ns + 1:2 * ns + 1 + nd], refs[2 * ns + 1 + nd:2 * ns + 1 + 2 * nd]
        n_in = 2 * ns + 1 + 2 * nd
        out_refs, osm_ref = refs[n_in:n_in + ns + nd], refs[n_in + ns + nd]
        ra_sm, p_sm, s_sem, r_sem, sm_s, sm_r = refs[n_in + ns + nd + 1:]
        x, y, c, chip, peers, peer_chip = _place()
        sib = (x, y, 1 - c)
        half = lambda cc: pl.ds(pl.multiple_of(cc * hs, 8), hs)
        sm_a = _remote(gsm_ref.at[half(1 - c), :], ra_sm, sm_s.at[0], sm_r.at[0], sib)
        sm_a.start()
        swaps = [sm_a]
        for w in range(ns + nd):
            n = halves[w]

            def total(i, carry, w=w, n=n):
                r0 = pl.multiple_of(i * row_block, row_block)
                blk = pl.ds(r0, row_block)
                mine = pl.ds(pl.multiple_of(c * n + r0, 16), row_block)
                if w < ns:
                    acc = own_refs[w][blk, :]
                    terms = [land_refs[w][m, blk, :] for m in range(_PEERS_OF["scatter"])]
                else:
                    acc = dsrc_refs[w - ns][chip, mine, :].astype(F32)
                    terms = [dland_refs[w - ns][r, blk, :] for r in range(_PEERS_OF["direct"])]
                for term in terms:
                    acc = acc + term.astype(F32)
                out_refs[w][mine, :] = acc
                return carry
            lax.fori_loop(0, n // row_block, total, 0)
            mine = out_refs[w].at[pl.ds(pl.multiple_of(c * n, 8), n), :]
            swaps.append(_remote(mine, mine, s_sem.at[w], r_sem.at[w], sib))
            swaps[-1].start()
        sm_a.wait_recv()
        p_sm[chip] = gsm_ref[half(c), :] + ra_sm[...]
        for m, (px, py) in enumerate(peers):
            swaps.append(_remote(p_sm.at[chip], p_sm.at[chip], sm_s.at[1 + m], sm_r.at[1 + m], (px, py, c)))
            swaps[-1].start()
        for m, (px, py) in enumerate(peers):
            _remote(p_sm.at[chip], p_sm.at[peer_chip[m]], sm_s.at[1 + m], sm_r.at[1 + m], (px, py, c)).wait_recv()
        osm_ref[half(c), :] = (p_sm[0] + p_sm[1]) + (p_sm[2] + p_sm[3])
        swaps.append(_remote(osm_ref.at[half(c), :], osm_ref.at[half(c), :], sm_s.at[4], sm_r.at[4], sib))
        swaps[-1].start()
        for w in range(ns + nd):
            n = halves[w]
            theirs = out_refs[w].at[pl.ds(pl.multiple_of((1 - c) * n, 8), n), :]
            _remote(theirs, theirs, s_sem.at[w], r_sem.at[w], sib).wait_recv()
        _remote(osm_ref.at[half(1 - c), :], osm_ref.at[half(1 - c), :], sm_s.at[4], sm_r.at[4], sib).wait_recv()
        for cp in swaps:
            cp.wait_send()

    vmem = pl.BlockSpec(memory_space=pltpu.VMEM)
    return pl.pallas_call(
        body, out_shape=[jax.ShapeDtypeStruct((2 * o.shape[0], o.shape[1]), F32) for o in owns]
        + [jax.ShapeDtypeStruct(s.shape[1:], F32) for s in direct_srcs]
        + [jax.ShapeDtypeStruct((SMALL_ROWS, LANES), F32)],
        in_specs=[vmem] * (2 * ns + 1 + 2 * nd), out_specs=[vmem] * (ns + nd + 1),
        scratch_shapes=[pltpu.VMEM((hs, LANES), F32), pltpu.VMEM((N_CHIPS, hs, LANES), F32),
                        pltpu.SemaphoreType.DMA((ns + nd,)), pltpu.SemaphoreType.DMA((ns + nd,)),
                        pltpu.SemaphoreType.DMA((5,)), pltpu.SemaphoreType.DMA((5,))],
        compiler_params=pltpu.CompilerParams(vmem_limit_bytes=VMEM_LIMIT),
        name="reduce_last")(*owns, *landed, g_small, *direct_srcs, *direct_landed)


_SMALL_PARTS = (("g_norm", 8, 8), ("w_s", 512, 512), ("b_s", 4, 8), ("g_v", 2, 8), ("g_mem", 8, 8),
                ("g_final", 8, 8), ("loss", 8, 8))
_LOSS_ROW = SMALL_ROWS - 8
assert sum(p for _, _, p in _SMALL_PARTS) == SMALL_ROWS


def _pack_small(parts, loss_block):
    rows = []
    for (name, used, padded), p in zip(_SMALL_PARTS, list(parts) + [loss_block]):
        p = p.reshape(used, LANES)
        if padded > used:
            p = jnp.pad(p, ((0, padded - used), (0, 0)))
        rows.append(p)
    return jnp.concatenate(rows, axis=0)


def _local_step(x, mem, target, g_norm, w_in, w_s, b_s, g_v, g_mem, late_weights, g_final,
                fwd_token=None, on_late=None, on_dw=None):
    B, S, _ = x.shape
    x2d = x.reshape(B * S, D_MODEL)
    t2d = target.reshape(B * S, D_MODEL)
    mem2d = mem.reshape(B * N_MEM, D_MODEL)

    proj = _inproj_fwd(x2d, g_norm, w_in, after=() if fwd_token is None else (fwd_token,))
    w_kv, w_out = late_weights(proj)
    kv = _kv_fwd(mem2d, g_mem, w_kv)
    a, lse = _attn_fwd(proj, B, S)
    w_sT = jnp.swapaxes(w_s, 1, 2)
    b_tab = jnp.repeat(b_s.T, HEAD_DIM, axis=1)
    (dx2, da, drest, loss, d_wout, d_ws, d_bs, d_gv, d_gf, dkv) = _mid(
        x2d, t2d, a, proj, kv, w_s, w_sT, b_tab, g_v, w_out, g_final, B, S)
    d_wkv, d_gmem = _kv_bwd(mem2d, g_mem, w_kv, dkv)
    dq, dk, dv = _attn_bwd(proj, a, lse, da, B, S, after=() if on_late is None else (on_late(d_wkv, d_wout),))
    if on_dw is None:
        d_win = _inproj_bwd_dw(dq, dk, dv, drest, x2d, g_norm)
        after = ()
    else:
        d_win = None
        after = (on_dw(*_inproj_bwd_dw(dq, dk, dv, drest, x2d, g_norm, reduce_with=[])),)
    grad_x, d_gnorm = _inproj_bwd_dx(dq, dk, dv, drest, x2d, dx2, g_norm, w_in, after=after)
    d_bs = d_bs[:, :N_SGU_GROUPS].T
    return (loss, grad_x.reshape(B, S, D_MODEL),
            dict(g_norm=d_gnorm, w_in=d_win, w_s=d_ws, b_s=d_bs, g_v=d_gv, g_mem=d_gmem, w_kv=d_wkv,
                 w_out=d_wout, g_final=d_gf))


def kernel(x, mem, g_norm, w_in, w_sgu_spatial, b_sgu_spatial, g_sgu_v, g_mem, w_mem_kv, w_out, g_final, loss_target, m_g_norm, m_w_in, m_w_sgu_spatial, m_b_sgu_spatial, m_g_sgu_v, m_g_mem, m_w_mem_kv, m_w_out, m_g_final, v_g_norm, v_w_in, v_w_sgu_spatial, v_b_sgu_spatial, v_g_sgu_v, v_g_mem, v_w_mem_kv, v_w_out, v_g_final):
    t = lambda w: jnp.swapaxes(w[0], 0, 1)
    (win_all,), late_shards, late_lands = _ag_weights([t(w_in)], [w_mem_kv[0], w_out[0]])
    w_in_full = win_all.reshape(-1, win_all.shape[-1])
    late = _exchange_start("gather", list(late_shards), (win_all,), "gather_late_start", lands=late_lands)

    def late_weights(proj):
        return [z.reshape(-1, z.shape[-1]) for z in _exchange_wait("gather", late, proj, "gather_late_wait")[1]]

    scatter = {}

    def on_late(d_wkv, d_wout):
        scatter["late"] = _exchange_start("direct", [d_wkv, d_wout], (), "scatter_late_start")
        return scatter["late"][-1]

    def on_dw(sends, owns):
        scatter["own"] = owns
        scatter["started"] = _exchange_start("scatter", list(sends), (owns[0],), "scatter_start")
        return scatter["started"][-1]

    loss, grad_x, g = _local_step(
        x, mem, loss_target, g_norm, w_in_full, w_sgu_spatial[0], b_sgu_spatial[0], g_sgu_v, g_mem,
        late_weights, g_final.reshape(1, D_MODEL), fwd_token=late[-1], on_late=on_late, on_dw=on_dw)

    small_names = ("g_norm", "w_s", "b_s", "g_v", "g_mem", "g_final")
    g_small = _pack_small([g[n] for n in small_names], loss)
    late_srcs, late_landed = _exchange_wait("direct", scatter["late"], g_small, "scatter_late_wait")
    _, landed = _exchange_wait("scatter", scatter["started"], late_landed[0], "scatter_wait")
    gr_in, gr_kv, gr_out, gr_small = _reduce_last(scatter["own"], landed, g_small, late_srcs, late_landed)

    small_w = (g_norm, w_sgu_spatial, b_sgu_spatial, g_sgu_v, g_mem, g_final)
    small_m = (m_g_norm, m_w_sgu_spatial, m_b_sgu_spatial, m_g_sgu_v, m_g_mem, m_g_final)
    small_v = (v_g_norm, v_w_sgu_spatial, v_b_sgu_spatial, v_g_sgu_v, v_g_mem, v_g_final)
    rows = lambda ws: [w.reshape(-1, LANES) for w in ws]
    small_new, ((gr_kv, d_kv, nm_kv, nv_kv), (gr_out, d_out, nm_out, nv_out)), loss = _adamw_rest(
        gr_small, rows(small_w), rows(small_m), rows(small_v),
        [(w_mem_kv[0], gr_kv, m_w_mem_kv[0], v_w_mem_kv[0]), (w_out[0], gr_out, m_w_out[0], v_w_out[0])])
    loss = loss.reshape(())
    small = [[z.reshape(w.shape) for z in four] for w, four in zip(small_w, small_new)]
    gr_in, d_in, nm_in, nv_in = [jnp.swapaxes(z, 0, 1)
                                 for z in _adamw(t(w_in), gr_in, t(m_w_in), t(v_w_in), "adamw_w_in")]

    def leaves(kind, big_in, big_kv, big_out):
        s_norm, s_ws, s_bs, s_gv, s_gmem, s_gf = [four[kind] for four in small]
        return [s_norm, big_in[None], s_ws, s_bs, s_gv, s_gmem, big_kv[None], big_out[None], s_gf]

    return (loss, grad_x, *leaves(0, gr_in, gr_kv, gr_out), *leaves(1, d_in, d_kv, d_out),
            *leaves(2, nm_in, nm_kv, nm_out), *leaves(3, nv_in, nv_kv, nv_out))
```

```python
import functools

import jax
import jax.numpy as jnp
from jax import lax
from jax.experimental import pallas as pl
from jax.experimental.pallas import tpu as pltpu

F32 = jnp.float32
BF16 = jnp.bfloat16
MESH = pl.DeviceIdType.MESH

D_MODEL = 1024
ATTN_WIDTH = 512
SGU_WIDTH = 256
MEM_WIDTH = 256
N_MEM = 256
IN_COLS = 3328
QKV_COLS = 3 * ATTN_WIDTH
REST_COLS = IN_COLS - QKV_COLS
SGU_CHUNK = 128
N_SGU_GROUPS = 4
EPS = 1e-6
NEG_INF = -1e30
DILATIONS = (1, 4, 16)
RADIUS = 64
Q_BLOCK = 128
LANES = 128
HEAD_DIM = 64

ADAM_LR = 0.001
ADAM_B1 = 0.9
ADAM_B2 = 0.999
ADAM_EPS = 1e-08
ADAM_WD = 0.01
ADAM_STEP = 10

N_CHIPS = 4
VMEM_LIMIT = 56 * 1024 * 1024
SMALL_ROWS = 560


def _params(sem=None, vmem=VMEM_LIMIT):
    return pltpu.CompilerParams(dimension_semantics=sem, vmem_limit_bytes=vmem)


def _nn(a, b):
    return jnp.dot(a, b, preferred_element_type=F32)


def _nt(a, b):
    return lax.dot_general(a, b, (((1,), (1,)), ((), ())), preferred_element_type=F32)


def _tn(a, b):
    return lax.dot_general(a, b, (((0,), (0,)), ((), ())), preferred_element_type=F32)


def _rms(x):
    r = lax.rsqrt(jnp.mean(x * x, axis=-1, keepdims=True) + EPS)
    return r, x * r


def _head_masks():
    lane = lax.broadcasted_iota(jnp.int32, (1, LANES), 1)
    lo = lane < HEAD_DIM
    return lo, (lo.astype(F32), (~lo).astype(F32))


def _silu_parts(z):
    s = jax.nn.sigmoid(z)
    return z * s, s * (1.0 + z * (1.0 - s))


def _gelu_parts(x):
    c = 0.7978845608028654
    x2 = x * x
    s = jax.nn.sigmoid((2.0 * c) * (x + 0.044715 * (x * x2)))
    return x * s, s * (1.0 + x * (1.0 - s) * ((2.0 * c) * (1.0 + 3.0 * 0.044715 * x2)))


def _after(tokens):
    return [pl.BlockSpec(memory_space=pl.ANY)] * len(tokens)


def _inproj_fwd(x2d, g_norm, w_in_t, after=()):
    T = x2d.shape[0]
    tm = 512

    def body(x_ref, g_ref, w_ref, *rest):
        o_ref = rest[-1]
        _, xh = _rms(x_ref[...])
        h = (xh * g_ref[...]).astype(BF16)
        o_ref[...] = _nt(h, w_ref[...])

    return pl.pallas_call(
        body, grid=(T // tm,),
        in_specs=[pl.BlockSpec((tm, D_MODEL), lambda i: (i, 0)),
                  pl.BlockSpec((1, D_MODEL), lambda i: (0, 0)),
                  pl.BlockSpec((IN_COLS, D_MODEL), lambda i: (0, 0))] + _after(after),
        out_specs=pl.BlockSpec((tm, IN_COLS), lambda i: (i, 0)),
        out_shape=jax.ShapeDtypeStruct((T, IN_COLS), F32),
        compiler_params=_params(("arbitrary",)), name="inproj_fwd")(x2d, g_norm, w_in_t, *after)


def _kv_fwd(mem2d, g_mem, w_kv):
    Tm = mem2d.shape[0]

    def body(m_ref, g_ref, w_ref, o_ref):
        _, mh = _rms(m_ref[...])
        o_ref[...] = _nn((mh * g_ref[...]).astype(BF16), w_ref[...])

    return pl.pallas_call(
        body, out_shape=jax.ShapeDtypeStruct((Tm, 2 * MEM_WIDTH), F32),
        compiler_params=_params(), name="kv_fwd")(mem2d, g_mem, w_kv)


def _kv_bwd(mem2d, g_mem, w_kv, dkv):
    Tm = mem2d.shape[0]

    def body(m_ref, g_ref, w_ref, dkv_ref, dw_ref, dg_ref):
        _, mh = _rms(m_ref[...])
        memn = (mh * g_ref[...]).astype(BF16)
        dkvb = dkv_ref[...].astype(BF16)
        dw = _tn(memn, dkvb).astype(BF16)
        for k in range(N_CHIPS):
            dw_ref[k] = dw[k * (D_MODEL // N_CHIPS):(k + 1) * (D_MODEL // N_CHIPS), :]
        dmemn = _nt(dkvb, w_ref[...])
        dg_ref[...] = jnp.sum(dmemn * mh, axis=0, keepdims=True)

    return pl.pallas_call(
        body, out_shape=(jax.ShapeDtypeStruct((N_CHIPS, D_MODEL // N_CHIPS, 2 * MEM_WIDTH), BF16),
                         jax.ShapeDtypeStruct((1, D_MODEL), F32)),
        compiler_params=_params(), name="kv_bwd")(mem2d, g_mem, w_kv, dkv)


def _attn_geometry(S):
    geom = []
    for d in DILATIONS:
        L = S // d
        assert L % Q_BLOCK == 0
        geom.append((d, L, min(2 * Q_BLOCK, L), L // Q_BLOCK))
    return geom


def _init_bias(bias_scr, geom, hp):
    row = lax.broadcasted_iota(jnp.int32, (Q_BLOCK, 2 * Q_BLOCK), 0)
    col = lax.broadcasted_iota(jnp.int32, (Q_BLOCK, 2 * Q_BLOCK), 1)
    for j in (0, 1):
        bits = (126 - (2 * hp + j)) * (1 << 23)
        slope = lax.bitcast_convert_type(jnp.full((1, 1), bits, jnp.int32), F32)
        for di, (d, _, _, _) in enumerate(geom):
            for cls, off in enumerate((0, -RADIUS, -2 * RADIUS)):
                dist = jnp.abs(col - row + off)
                bias_scr[di * 6 + cls * 2 + j] = jnp.where(
                    dist <= RADIUS, -(slope * float(d)) * dist.astype(F32), NEG_INF)


SPLIT = 4
COPY_ROWS = 256


def _by4_rows(S, step):
    per_class = S // SPLIT // COPY_ROWS
    r, j = step // per_class, step % per_class
    return (pl.ds(r + SPLIT * j * COPY_ROWS, COPY_ROWS, stride=SPLIT),
            pl.ds(r * (S // SPLIT) + j * COPY_ROWS, COPY_ROWS))


def _to_by4(src, dst, S):
    for i in range(S // COPY_ROWS):
        natural, by4 = _by4_rows(S, i)
        dst[by4, :] = src[natural, :]


def _block_slices(d, L, KW, nqb, r, qb, S):
    qs = qb * Q_BLOCK
    ks = jnp.clip(qs - RADIUS, 0, L - KW)
    cls = jnp.where(qb == 0, 0, jnp.where(qb == nqb - 1, 2, 1))
    if d == 1:
        qsl = pl.ds(pl.multiple_of(qs, Q_BLOCK), Q_BLOCK)
        ksl = pl.ds(pl.multiple_of(ks, RADIUS), KW)
    elif d == SPLIT:
        qsl = pl.ds(pl.multiple_of(r * L + qs, Q_BLOCK), Q_BLOCK)
        ksl = pl.ds(pl.multiple_of(r * L + ks, RADIUS), KW)
    else:
        sub = d // SPLIT
        base = (r % SPLIT) * (S // SPLIT) + r // SPLIT
        qsl = pl.ds(base + qs * sub, Q_BLOCK, stride=sub)
        ksl = pl.ds(base + ks * sub, KW, stride=sub)
    return qsl, ksl, cls


def _for_groups(geom, S, group, fn):
    for di, (d, L, KW, nqb) in enumerate(geom):
        n = group[di]
        assert (d * nqb) % n == 0

        def step(it, carry, di=di, d=d, L=L, KW=KW, nqb=nqb, n=n):
            slices = []
            for g in range(n):
                i = it * n + g
                slices.append(_block_slices(d, L, KW, nqb, i // nqb, i % nqb, S))
            fn(di, KW, slices)
            return carry
        lax.fori_loop(0, d * nqb // n, step, 0)


def _attn_fwd(proj, B, S):
    T = B * S
    geom = _attn_geometry(S)
    n_pairs = ATTN_WIDTH // LANES

    def body(q_ref, k_ref, v_ref, a_ref, lse_ref, bias_scr, q4, k4, v4, *per_dilation):
        o_scr, m_scr, l_scr = per_dilation[0:3], per_dilation[3:6], per_dilation[6:9]
        lo, hm = _head_masks()
        pair = pl.program_id(0)

        @pl.when(pl.program_id(1) == 0)
        def _():
            _init_bias(bias_scr, geom, pair)
        for src, dst in ((q_ref, q4), (k_ref, k4), (v_ref, v4)):
            _to_by4(src, dst, S)

        def group(di, KW, all_slices):
            run = 8
            for first in range(0, len(all_slices), run):
                some(di, KW, all_slices[first:first + run])

        def some(di, KW, slices):
            chains = [(g, j) for g in range(len(slices)) for j in (0, 1)]
            q_src, k_src, v_src = (q_ref, k_ref, v_ref) if di == 0 else (q4, k4, v4)
            q = [q_src[qsl, :] for qsl, _, _ in slices]
            kw = [k_src[ksl, :].astype(BF16) for _, ksl, _ in slices]
            vw = [v_src[ksl, :].astype(BF16) for _, ksl, _ in slices]
            s = {(g, j): _nt((q[g] * (hm[j] * 0.125)).astype(BF16), kw[g])
                 + bias_scr[di * 6 + slices[g][2] * 2 + j, :, pl.ds(0, KW)] for g, j in chains}
            m = {c: jnp.max(s[c], axis=1, keepdims=True) for c in chains}
            p = {c: jnp.exp(s[c] - m[c]) for c in chains}
            l = {c: jnp.sum(p[c], axis=1, keepdims=True) for c in chains}
            o = {(g, j): _nn(p[(g, j)].astype(BF16), vw[g]) for g, j in chains}
            for g, (qsl, _, _) in enumerate(slices):
                o_scr[di][qsl, :] = jnp.where(lo, o[(g, 0)], o[(g, 1)])
                m_scr[di][qsl, :] = jnp.where(lo, m[(g, 0)], m[(g, 1)])
                l_scr[di][qsl, :] = jnp.where(lo, l[(g, 0)], l[(g, 1)])

        _for_groups(geom, S, (16, 16, 16), group)

        for i in range(S // COPY_ROWS):
            natural, by4 = _by4_rows(S, i)
            rows = [natural, by4, by4]
            ms = [m_scr[di][rows[di], :] for di in range(3)]
            mx = jnp.maximum(jnp.maximum(ms[0], ms[1]), ms[2])
            num = 0.0
            den = 0.0
            for di in range(3):
                w = jnp.exp(ms[di] - mx)
                num = num + w * o_scr[di][rows[di], :]
                den = den + w * l_scr[di][rows[di], :]
            a_ref[natural, :] = num / den
            lse_ref[natural, :] = mx + jnp.log(den)

    blk = lambda off: pl.BlockSpec((S, LANES), lambda h, b, off=off: (b, off + h))
    out_blk = pl.BlockSpec((S, LANES), lambda h, b: (b, h))
    return pl.pallas_call(
        body, grid=(n_pairs, B),
        in_specs=[blk(0), blk(n_pairs), blk(2 * n_pairs)],
        out_specs=[out_blk, out_blk],
        out_shape=[jax.ShapeDtypeStruct((T, ATTN_WIDTH), F32)] * 2,
        scratch_shapes=[pltpu.VMEM((18, Q_BLOCK, 2 * Q_BLOCK), F32)] + [pltpu.VMEM((S, LANES), F32)] * 12,
        compiler_params=_params(("arbitrary", "arbitrary")), name="attn_fwd")(proj, proj, proj)


def _attn_bwd(proj, a, lse, da, B, S, after=()):
    T = B * S
    geom = _attn_geometry(S)
    n_pairs = ATTN_WIDTH // LANES

    def body(q_ref, k_ref, v_ref, a_ref, lse_ref, do_ref, *rest):
        dq_ref, dk_ref, dv_ref, bias_scr = rest[len(after):len(after) + 4]
        scr = rest[len(after) + 4:]
        acc = (scr[0:3], scr[3:6])
        natural_in = (q_ref, k_ref, v_ref, a_ref, lse_ref, do_ref)
        by4_in = scr[6:12]
        _, hm = _head_masks()
        pair = pl.program_id(0)

        @pl.when(pl.program_id(1) == 0)
        def _():
            _init_bias(bias_scr, geom, pair)
        for ref in scr[0:6]:
            ref[...] = jnp.zeros_like(ref)
        for src, dst in zip(natural_in, by4_in):
            _to_by4(src, dst, S)

        def group(di, KW, all_slices):
            run = (4, 4, 8)[di]
            for first in range(0, len(all_slices), run):
                some(di, KW, all_slices[first:first + run])

        def some(di, KW, slices):
            n = len(slices)
            chains = [(g, j) for g in range(n) for j in (0, 1)]
            q_src, k_src, v_src, a_src, lse_src, do_src = natural_in if di == 0 else by4_in
            dq_scr, dk_scr, dv_scr = acc[0 if di == 0 else 1]
            q = [q_src[qsl, :] for qsl, _, _ in slices]
            do = [do_src[qsl, :] for qsl, _, _ in slices]
            doa = [do[g] * a_src[slices[g][0], :] for g in range(n)]
            lse_q = [lse_src[qsl, :] for qsl, _, _ in slices]
            kw = [k_src[ksl, :].astype(BF16) for _, ksl, _ in slices]
            vw = [v_src[ksl, :].astype(BF16) for _, ksl, _ in slices]
            qj = {(g, j): (q[g] * (hm[j] * 0.125)).astype(BF16) for g, j in chains}
            doj = {(g, j): (do[g] * hm[j]).astype(BF16) for g, j in chains}
            s = {(g, j): _nt(qj[(g, j)], kw[g])
                 + bias_scr[di * 6 + slices[g][2] * 2 + j, :, pl.ds(0, KW)] for g, j in chains}
            dp = {(g, j): _nt(doj[(g, j)], vw[g]) for g, j in chains}
            dsum = {(g, j): jnp.sum(doa[g] * hm[j], axis=1, keepdims=True) for g, j in chains}
            p = {(g, j): jnp.exp(s[(g, j)] - lse_q[g][:, HEAD_DIM * j:HEAD_DIM * j + 1]) for g, j in chains}
            ds = {c: (p[c] * (dp[c] - dsum[c])).astype(BF16) for c in chains}
            pb = {c: p[c].astype(BF16) for c in chains}
            dq = [_nn(ds[(g, 0)], kw[g]) * (hm[0] * 0.125) + _nn(ds[(g, 1)], kw[g]) * (hm[1] * 0.125)
                  for g in range(n)]
            both = lambda t, g: jnp.concatenate([t[(g, 0)], t[(g, 1)]], axis=0)
            dkw = [_tn(both(ds, g), both(qj, g)) for g in range(n)]
            dvw = [_tn(both(pb, g), both(doj, g)) for g in range(n)]
            for g, (qsl, ksl, _) in enumerate(slices):
                dq_scr[qsl, :] = dq_scr[qsl, :] + dq[g]
                dk_scr[ksl, :] = dk_scr[ksl, :] + dkw[g]
                dv_scr[ksl, :] = dv_scr[ksl, :] + dvw[g]

        _for_groups(geom, S, (16, 16, 16), group)

        for i in range(S // COPY_ROWS):
            natural, by4 = _by4_rows(S, i)
            for nat, split in zip(*acc):
                nat[natural, :] = nat[natural, :] + split[by4, :]
        for out, nat in zip((dq_ref, dk_ref, dv_ref), acc[0]):
            out[...] = nat[...].astype(BF16)

    blk = lambda off: pl.BlockSpec((S, LANES), lambda h, b, off=off: (b, off + h))
    return pl.pallas_call(
        body, grid=(n_pairs, B),
        in_specs=[blk(0), blk(n_pairs), blk(2 * n_pairs), blk(0), blk(0), blk(0)] + _after(after),
        out_specs=[blk(0), blk(0), blk(0)],
        out_shape=[jax.ShapeDtypeStruct((T, ATTN_WIDTH), BF16)] * 3,
        scratch_shapes=[pltpu.VMEM((18, Q_BLOCK, 2 * Q_BLOCK), F32)] + [pltpu.VMEM((S, LANES), F32)] * 12,
        compiler_params=_params(("arbitrary", "arbitrary")), name="attn_bwd")(proj, proj, proj, a, lse, da, *after)


def _mid(x2d, t2d, a, proj, kv, w_s, w_sT, b_tab, g_v, w_out, g_final, B, S):
    T = B * S
    tm = 512
    nt = S // tm
    halves = 2
    hrows = tm // halves

    def body(x_ref, t_ref, a_ref, za_ref, ub_ref, vb_ref, zb_ref, qm_ref, zm_ref, kv_ref,
              ws_ref, wsT_ref, btab_ref, gv_ref, wout_ref, gf_ref,
              dx2_ref, da_ref, drest_ref, loss_ref, dwout_bf_ref, dws_ref, dbs_ref, dgv_ref, dgf_ref, dkv_ref,
              dbtab_scr, dwout_ref):
        b = pl.program_id(0)
        t = pl.program_id(1)
        first = jnp.logical_and(b == 0, t == 0)
        last = jnp.logical_and(b == B - 1, t == nt - 1)
        _, hm = _head_masks()
        lane_g = lax.broadcasted_iota(jnp.int32, (1, SGU_WIDTH), 1) // HEAD_DIM
        gm = [(lane_g == g).astype(F32) for g in range(N_SGU_GROUPS)]
        H = range(halves)
        rows = [pl.ds(h * hrows, hrows) for h in H]
        ld = lambda ref: [ref[r, :] for r in rows]
        cat = lambda parts, axis: jnp.concatenate(parts, axis=axis)
        chunks = [slice(ci * SGU_CHUNK, (ci + 1) * SGU_CHUNK) for ci in range(hrows // SGU_CHUNK)]
        pairs = [slice(pr * LANES, (pr + 1) * LANES) for pr in range(2)]
        heads = [(pr, j) for pr in range(2) for j in (0, 1)]

        @pl.when(first)
        def _():
            loss_ref[...] = jnp.zeros_like(loss_ref)
            dwout_ref[...] = jnp.zeros_like(dwout_ref)
            dws_ref[...] = jnp.zeros_like(dws_ref)
            dbs_ref[...] = jnp.zeros_like(dbs_ref)
            dgv_ref[...] = jnp.zeros_like(dgv_ref)
            dgf_ref[...] = jnp.zeros_like(dgf_ref)
            dbtab_scr[...] = jnp.zeros_like(dbtab_scr)

        @pl.when(t == 0)
        def _():
            dkv_ref[...] = jnp.zeros_like(dkv_ref)

        a_val = ld(a_ref)
        sil_a = [_silu_parts(z) for z in ld(za_ref)]
        gated_a = [s[0] * a for s, a in zip(sil_a, a_val)]
        u = [_gelu_parts(z) for z in ld(ub_ref)]
        vv = [_gelu_parts(z) for z in ld(vb_ref)]
        vnorm = [_rms(v[0]) for v in vv]
        gv = gv_ref[...]
        vn = [(n[1] * gv).astype(BF16) for n in vnorm]
        w_cat = cat([ws_ref[g].astype(BF16) for g in range(N_SGU_GROUPS)], 1)
        wT_cat = cat([wsT_ref[g].astype(BF16) for g in range(N_SGU_GROUPS)], 1)
        gmb = [m.astype(BF16) for m in gm]
        by_group = lambda chunk: cat([chunk * gmb[g] for g in range(N_SGU_GROUPS)], 0)
        btab = btab_ref[...]
        mixed = [cat([btab + _nn(w_cat, by_group(vn[h][c, :])) for c in chunks], 0) for h in H]
        sg = [u[h][0] * mixed[h] for h in H]
        sil_b = [_silu_parts(z) for z in ld(zb_ref)]
        gated_b = [sil_b[h][0] * sg[h] for h in H]

        kvv = kv_ref[...].astype(BF16)
        kp = [kvv[:, p] for p in pairs]
        vp = [kvv[:, MEM_WIDTH + pr * LANES:MEM_WIDTH + (pr + 1) * LANES] for pr in range(2)]
        qm = ld(qm_ref)
        qj = {(h, pr, j): (qm[h][:, pairs[pr]] * (hm[j] * 0.125)).astype(BF16) for h in H for pr, j in heads}
        sc = {k: _nt(qj[k], kp[k[1]]) for k in qj}
        ex = {k: jnp.exp(sc[k] - jnp.max(sc[k], axis=1, keepdims=True)) for k in qj}
        prob = {k: ex[k] * (1.0 / jnp.sum(ex[k], axis=1, keepdims=True)) for k in qj}
        probb = {k: prob[k].astype(BF16) for k in qj}
        mo = [cat([sum(_nn(probb[(h, pr, j)], vp[pr]) * hm[j] for j in (0, 1)) for pr in range(2)], 1) for h in H]
        sil_m = [_silu_parts(z) for z in ld(zm_ref)]
        gated_m = [sil_m[h][0] * mo[h] for h in H]

        gated = [cat([gated_a[h], gated_b[h], gated_m[h]], 1).astype(BF16) for h in H]
        wout = wout_ref[...]
        x_in = ld(x_ref)
        x2 = [x_in[h] + _nn(gated[h], wout) for h in H]
        fin = [_rms(z) for z in x2]
        gf = gf_ref[...]
        tgt = ld(t_ref)
        err = [fin[h][1] * gf - tgt[h] for h in H]
        loss_ref[...] += sum(jnp.sum(e * e) for e in err) * (0.5 / D_MODEL)

        dy = [e * (1.0 / D_MODEL) for e in err]
        dgf_ref[...] += sum(jnp.sum(dy[h] * fin[h][1], axis=0, keepdims=True) for h in H)
        gdy = [d * gf for d in dy]
        dx2 = [fin[h][0] * (gdy[h] - fin[h][1] * jnp.mean(gdy[h] * fin[h][1], axis=1, keepdims=True)) for h in H]
        for h in H:
            dx2_ref[rows[h], :] = dx2[h]
        dx2b = [d.astype(BF16) for d in dx2]
        dgated = [_nt(d, wout) for d in dx2b]
        dwout_ref[...] += _tn(cat(gated, 0), cat(dx2b, 0))
        dga = [d[:, 0:ATTN_WIDTH] for d in dgated]
        dgb = [d[:, ATTN_WIDTH:ATTN_WIDTH + SGU_WIDTH] for d in dgated]
        dgm = [d[:, ATTN_WIDTH + SGU_WIDTH:] for d in dgated]

        for h in H:
            da_ref[rows[h], :] = dga[h] * sil_a[h][0]
        dza = [dga[h] * a_val[h] * sil_a[h][1] for h in H]

        dsg = [dgb[h] * sil_b[h][0] for h in H]
        dzb = [dgb[h] * sg[h] * sil_b[h][1] for h in H]
        dub = [dsg[h] * mixed[h] * u[h][1] for h in H]
        dmixed = [dsg[h] * u[h][0] for h in H]
        dmixed_b = [d.astype(BF16) for d in dmixed]
        dvn = [cat([_nn(wT_cat, by_group(dmixed_b[h][c, :])) for c in chunks], 0) for h in H]
        for g in range(N_SGU_GROUPS):
            dws_ref[g] += sum(_nt((dmixed[h][c, :] * gm[g]).astype(BF16), vn[h][c, :]) for h in H for c in chunks)
        dbtab_scr[...] += sum(dmixed[h][c, :] for h in H for c in chunks)
        dgv_ref[...] += sum(jnp.sum(dvn[h] * vnorm[h][1], axis=0, keepdims=True) for h in H)
        tv = [d * gv for d in dvn]
        dvv = [vnorm[h][0] * (tv[h] - vnorm[h][1] * jnp.mean(tv[h] * vnorm[h][1], axis=1, keepdims=True)) for h in H]
        dvb = [dvv[h] * vv[h][1] for h in H]

        dmo = [dgm[h] * sil_m[h][0] for h in H]
        dzm = [dgm[h] * mo[h] * sil_m[h][1] for h in H]
        dmoj = {(h, pr, j): (dmo[h][:, pairs[pr]] * hm[j]).astype(BF16) for h in H for pr, j in heads}
        dp = {k: _nt(dmoj[k], vp[k[1]]) for k in qj}
        ds = {k: (prob[k] * (dp[k] - jnp.sum(dp[k] * prob[k], axis=1, keepdims=True))).astype(BF16) for k in qj}
        dqm = [cat([sum(_nn(ds[(h, pr, j)], kp[pr]) * (hm[j] * 0.125) for j in (0, 1)) for pr in range(2)], 1)
               for h in H]
        every = lambda tbl, pr: cat([tbl[(h, pr, j)] for h in H for j in (0, 1)], 0)
        dk = [_tn(every(ds, pr), every(qj, pr)) for pr in range(2)]
        dv = [_tn(every(probb, pr), every(dmoj, pr)) for pr in range(2)]
        dkv_ref[...] += cat(dk + dv, 1)

        for h in H:
            drest_ref[rows[h], :] = cat([dza[h], dub[h], dvb[h], dzb[h], dqm[h], dzm[h]], 1).astype(BF16)

        @pl.when(last)
        def _():
            lane = lax.broadcasted_iota(jnp.int32, (1, LANES), 1)
            dbt = dbtab_scr[...]
            out = jnp.zeros((SGU_CHUNK, LANES), F32)
            for g in range(N_SGU_GROUPS):
                out = out + jnp.where(lane == g, jnp.sum(dbt * gm[g], axis=1, keepdims=True), 0.0)
            dbs_ref[...] = out
            for r0 in range(0, D_MODEL, SGU_CHUNK):
                k, row = divmod(r0, D_MODEL // N_CHIPS)
                dwout_bf_ref[k, row:row + SGU_CHUNK, :] = dwout_ref[r0:r0 + SGU_CHUNK, :].astype(BF16)

    tile = lambda w, cb: pl.BlockSpec((tm, w), lambda b, t, cb=cb: (b * nt + t, cb))
    const = lambda shape: pl.BlockSpec(shape, lambda b, t, n=len(shape): (0,) * n)
    return pl.pallas_call(
        body, grid=(B, nt),
        in_specs=[tile(D_MODEL, 0), tile(D_MODEL, 0), tile(ATTN_WIDTH, 0),
                  tile(ATTN_WIDTH, 3),
                  tile(SGU_WIDTH, 8), tile(SGU_WIDTH, 9), tile(SGU_WIDTH, 10),
                  tile(MEM_WIDTH, 11), tile(MEM_WIDTH, 12),
                  pl.BlockSpec((N_MEM, 2 * MEM_WIDTH), lambda b, t: (b, 0)),
                  const((N_SGU_GROUPS, SGU_CHUNK, SGU_CHUNK)), const((N_SGU_GROUPS, SGU_CHUNK, SGU_CHUNK)),
                  const((SGU_CHUNK, SGU_WIDTH)), const((1, SGU_WIDTH)),
                  const((D_MODEL, D_MODEL)), const((1, D_MODEL))],
        out_specs=[tile(D_MODEL, 0), tile(ATTN_WIDTH, 0), tile(REST_COLS, 0),
                   const((8, LANES)), const((N_CHIPS, D_MODEL // N_CHIPS, D_MODEL)),
                   const((N_SGU_GROUPS, SGU_CHUNK, SGU_CHUNK)), const((SGU_CHUNK, LANES)),
                   const((1, SGU_WIDTH)), const((1, D_MODEL)),
                   pl.BlockSpec((N_MEM, 2 * MEM_WIDTH), lambda b, t: (b, 0))],
        out_shape=[jax.ShapeDtypeStruct((T, D_MODEL), F32), jax.ShapeDtypeStruct((T, ATTN_WIDTH), F32),
                   jax.ShapeDtypeStruct((T, REST_COLS), BF16),
                   jax.ShapeDtypeStruct((8, LANES), F32),
                   jax.ShapeDtypeStruct((N_CHIPS, D_MODEL // N_CHIPS, D_MODEL), BF16),
                   jax.ShapeDtypeStruct((N_SGU_GROUPS, SGU_CHUNK, SGU_CHUNK), F32),
                   jax.ShapeDtypeStruct((SGU_CHUNK, LANES), F32),
                   jax.ShapeDtypeStruct((1, SGU_WIDTH), F32), jax.ShapeDtypeStruct((1, D_MODEL), F32),
                   jax.ShapeDtypeStruct((B * N_MEM, 2 * MEM_WIDTH), F32)],
        scratch_shapes=[pltpu.VMEM((SGU_CHUNK, SGU_WIDTH), F32), pltpu.VMEM((D_MODEL, D_MODEL), F32)],
        compiler_params=_params(("arbitrary", "arbitrary"), vmem=VMEM_LIMIT + 2 * 1024 * 1024), name="mid")(
            x2d, t2d, a, proj, proj, proj, proj, proj, proj, kv, w_s, w_sT, b_tab, g_v, w_out, g_final)


def _inproj_bwd_dx(dq, dk, dv, drest, x2d, dx2, g_norm, w_in_t, after=()):
    T = x2d.shape[0]
    tm = 512
    W = ATTN_WIDTH

    def body(dq_ref, dk_ref, dv_ref, dr_ref, x_ref, dx2_ref, g_ref, w_ref, *rest):
        gx_ref, dg_ref = rest[-2:]

        @pl.when(pl.program_id(0) == 0)
        def _():
            dg_ref[...] = jnp.zeros_like(dg_ref)

        halves = [pl.ds(h * (tm // 2), tm // 2) for h in (0, 1)]
        dh = [(_nn(dq_ref[r, :], w_ref[0:W, :]) + _nn(dk_ref[r, :], w_ref[W:2 * W, :])
               + _nn(dv_ref[r, :], w_ref[2 * W:3 * W, :]) + _nn(dr_ref[r, :], w_ref[QKV_COLS:IN_COLS, :]))
              for r in halves]
        nrm = [_rms(x_ref[r, :]) for r in halves]
        dg_ref[...] += sum(jnp.sum(d * n[1], axis=0, keepdims=True) for d, n in zip(dh, nrm))
        g = g_ref[...]
        for r, d, (rstd, xh) in zip(halves, dh, nrm):
            th = d * g
            gx_ref[r, :] = rstd * (th - xh * jnp.mean(th * xh, axis=1, keepdims=True)) + dx2_ref[r, :]

    tile = lambda w: pl.BlockSpec((tm, w), lambda i: (i, 0))
    return pl.pallas_call(
        body, grid=(T // tm,),
        in_specs=[tile(W), tile(W), tile(W), tile(REST_COLS), tile(D_MODEL), tile(D_MODEL),
                  pl.BlockSpec((1, D_MODEL), lambda i: (0, 0)),
                  pl.BlockSpec((IN_COLS, D_MODEL), lambda i: (0, 0))] + _after(after),
        out_specs=[tile(D_MODEL), pl.BlockSpec((1, D_MODEL), lambda i: (0, 0))],
        out_shape=[jax.ShapeDtypeStruct((T, D_MODEL), F32), jax.ShapeDtypeStruct((1, D_MODEL), F32)],
        compiler_params=_params(("arbitrary",)), name="inproj_bwd_dx")(
            dq, dk, dv, drest, x2d, dx2, g_norm, w_in_t, *after)


def _inproj_bwd_dw(dq, dk, dv, drest, x2d, g_norm, reduce_with=None):
    T = x2d.shape[0]
    tm = 512
    nt = T // tm
    W = ATTN_WIDTH
    fused = reduce_with is not None
    others = list(reduce_with) if fused else []
    ns = 1 + len(others)
    shard = IN_COLS // N_CHIPS
    halves = [shard // 2] + [s.shape[1] // 2 for s in others]
    cols = [D_MODEL] + [s.shape[2] for s in others]
    row_block = 32

    def body(dq_ref, dk_ref, dv_ref, dr_ref, x_ref, g_ref, *rest):
        if fused:
            stacks = rest[:ns - 1]
            sends, owns = rest[ns - 1:2 * ns - 1], rest[2 * ns - 1:3 * ns - 1]
            acc, ras, narrow = rest[3 * ns - 1], rest[3 * ns:4 * ns], rest[4 * ns]
            s_sem, r_sem = rest[4 * ns + 1], rest[4 * ns + 2]
            x, y, c, chip, peers, peer_chip = _place()
            sib = (x, y, 1 - c)

            def part(w, k, cc, r0=0, rows=None):
                n = halves[w]
                rows = n if rows is None else rows
                if w == 0:
                    return acc.at[pl.ds(pl.multiple_of(k * shard + cc * n + r0, 8), rows), :]
                return stacks[w - 1].at[k, pl.ds(pl.multiple_of(cc * n + r0, 8), rows), :]

            def swap_other(w):
                theirs = stacks[w - 1].at[:, pl.ds(pl.multiple_of((1 - c) * halves[w], 8), halves[w]), :]
                return _remote(theirs, ras[w], s_sem.at[N_CHIPS - 1 + w], r_sem.at[N_CHIPS - 1 + w], sib)

            def swap_win(k):
                return _remote(narrow.at[k], ras[0].at[k], s_sem.at[k], r_sem.at[k], sib)
        else:
            acc = rest[0]

        @pl.when(pl.program_id(0) == 0)
        def _():
            acc[...] = jnp.zeros_like(acc)
            for w in range(1, ns):
                swap_other(w).start()

        _, xh = _rms(x_ref[...])
        h = (xh * g_ref[...]).astype(BF16)
        acc[0:W, :] += _tn(dq_ref[...], h)
        acc[W:2 * W, :] += _tn(dk_ref[...], h)
        acc[2 * W:3 * W, :] += _tn(dv_ref[...], h)
        acc[QKV_COLS:IN_COLS, :] += _tn(dr_ref[...], h)

        if fused:
            @pl.when(pl.program_id(0) == nt - 1)
            def _():
                for k in range(N_CHIPS):
                    def to_bf16(i, carry, k=k):
                        r0 = pl.multiple_of(i * row_block, row_block)
                        narrow[k, pl.ds(r0, row_block), :] = part(0, k, 1 - c, r0, row_block)[...].astype(BF16)
                        return carry
                    lax.fori_loop(0, halves[0] // row_block, to_bf16, 0)
                    swap_win(k).start()
                def chip_sum(w, k, r0):
                    blk = pl.ds(r0, row_block)
                    return part(w, k, c, r0, row_block)[...] + ras[w][k, blk, :].astype(F32)

                for w in range(1, ns):
                    swap_other(w).wait_recv()

                    def sums(i, carry, w=w):
                        r0 = pl.multiple_of(i * row_block, row_block)
                        for m in range(3):
                            sends[w][m, pl.ds(r0, row_block), :] = chip_sum(w, peer_chip[m], r0).astype(BF16)
                        owns[w][pl.ds(r0, row_block), :] = chip_sum(w, chip, r0)
                        return carry
                    lax.fori_loop(0, halves[w] // row_block, sums, 0)
                for k in range(N_CHIPS):
                    swap_win(k).wait_recv()

                    @pl.when(chip == k)
                    def _(k=k):
                        def own(i, carry):
                            r0 = pl.multiple_of(i * row_block, row_block)
                            owns[0][pl.ds(r0, row_block), :] = chip_sum(0, k, r0)
                            return carry
                        lax.fori_loop(0, halves[0] // row_block, own, 0)

                    @pl.when(chip != k)
                    def _(k=k):
                        def other(i, carry):
                            r0 = pl.multiple_of(i * row_block, row_block)
                            sends[0][(k ^ chip) - 1, pl.ds(r0, row_block), :] = chip_sum(0, k, r0).astype(BF16)
                            return carry
                        lax.fori_loop(0, halves[0] // row_block, other, 0)
                for k in range(N_CHIPS):
                    swap_win(k).wait_send()
                for w in range(1, ns):
                    swap_other(w).wait_send()

    tile = lambda w: pl.BlockSpec((tm, w), lambda i: (i, 0))
    vmem = pl.BlockSpec(memory_space=pltpu.VMEM)
    in_specs = [tile(W), tile(W), tile(W), tile(REST_COLS), tile(D_MODEL), pl.BlockSpec((1, D_MODEL), lambda i: (0, 0))]
    if not fused:
        return pl.pallas_call(
            body, grid=(nt,), in_specs=in_specs,
            out_specs=pl.BlockSpec((IN_COLS, D_MODEL), lambda i: (0, 0)),
            out_shape=jax.ShapeDtypeStruct((IN_COLS, D_MODEL), F32),
            compiler_params=_params(("arbitrary",)), name="inproj_bwd_dw")(dq, dk, dv, drest, x2d, g_norm)
    outs = pl.pallas_call(
        body, grid=(nt,), in_specs=in_specs + [vmem] * (ns - 1), out_specs=[vmem] * (2 * ns),
        out_shape=[jax.ShapeDtypeStruct((3, n, cl), BF16) for n, cl in zip(halves, cols)]
        + [jax.ShapeDtypeStruct((n, cl), F32) for n, cl in zip(halves, cols)],
        scratch_shapes=[pltpu.VMEM((IN_COLS, D_MODEL), F32)]
        + [pltpu.VMEM((N_CHIPS, n, cl), BF16 if w == 0 else F32) for w, (n, cl) in enumerate(zip(halves, cols))]
        + [pltpu.VMEM((N_CHIPS, halves[0], D_MODEL), BF16)]
        + [pltpu.SemaphoreType.DMA((N_CHIPS - 1 + ns,)), pltpu.SemaphoreType.DMA((N_CHIPS - 1 + ns,))],
        compiler_params=_params(("arbitrary",)), name="inproj_bwd_dw_reduce")(
            dq, dk, dv, drest, x2d, g_norm, *others)
    return outs[:ns], outs[ns:]


def _adamw_update(w, g, m, v):
    nm = ADAM_B1 * m + (1.0 - ADAM_B1) * g
    nv = ADAM_B2 * v + (1.0 - ADAM_B2) * (g * g)
    m_hat = nm / (1.0 - ADAM_B1 ** ADAM_STEP)
    v_hat = nv / (1.0 - ADAM_B2 ** ADAM_STEP)
    return -ADAM_LR * (m_hat / (jnp.sqrt(v_hat) + ADAM_EPS) + ADAM_WD * w), nm, nv


def _adamw(w, g, m, v, name):
    R, C = w.shape
    br = max(r for r in range(8, 257, 8) if R % r == 0)

    def body(w_ref, g_ref, m_ref, v_ref, g_out, d_ref, nm_ref, nv_ref):
        g = g_ref[...]
        g_out[...] = g
        d_ref[...], nm_ref[...], nv_ref[...] = _adamw_update(w_ref[...], g, m_ref[...], v_ref[...])

    spec = pl.BlockSpec((br, C), lambda i: (i, 0))
    return pl.pallas_call(
        body, grid=(R // br,), in_specs=[spec] * 4, out_specs=[spec] * 4,
        out_shape=[jax.ShapeDtypeStruct((R, C), F32)] * 4,
        compiler_params=_params(("arbitrary",)), name=name)(w, g, m, v)


def _adamw_rest(g_packed, ws, ms, vs, whole):
    n = len(ws)
    nw = len(whole)
    row_block = 64

    def body(*refs):
        g_ref = refs[0]
        w_refs, m_refs, v_refs = refs[1:1 + n], refs[1 + n:1 + 2 * n], refs[1 + 2 * n:1 + 3 * n]
        whole_in = [refs[1 + 3 * n + 4 * i:5 + 3 * n + 4 * i] for i in range(nw)]
        outs = refs[1 + 3 * n + 4 * nw:]
        off = 0
        for i, (_, used, padded) in enumerate(_SMALL_PARTS[:n]):
            g = g_ref[off:off + used, :]
            delta, nm, nv = _adamw_update(w_refs[i][...], g, m_refs[i][...], v_refs[i][...])
            outs[4 * i][...], outs[4 * i + 1][...], outs[4 * i + 2][...], outs[4 * i + 3][...] = g, delta, nm, nv
            off += padded
        for i, (w_ref, gw_ref, m_ref, v_ref) in enumerate(whole_in):
            for r0 in range(0, w_ref.shape[0], row_block):
                blk = pl.ds(r0, row_block)
                g = gw_ref[blk, :]
                new = _adamw_update(w_ref[blk, :], g, m_ref[blk, :], v_ref[blk, :])
                for ref, val in zip(outs[4 * (n + i):4 * (n + i) + 4], (g,) + new):
                    ref[blk, :] = val
        outs[4 * (n + nw)][...] = g_ref[_LOSS_ROW:_LOSS_ROW + 1, 0:1]

    outs = pl.pallas_call(
        body, out_shape=[jax.ShapeDtypeStruct(w.shape, F32) for w in ws for _ in range(4)]
        + [jax.ShapeDtypeStruct(four[0].shape, F32) for four in whole for _ in range(4)]
        + [jax.ShapeDtypeStruct((1, 1), F32)],
        compiler_params=_params(), name="adamw_rest")(g_packed, *ws, *ms, *vs, *[a for four in whole for a in four])
    return ([outs[4 * i:4 * i + 4] for i in range(n)], [outs[4 * (n + i):4 * (n + i) + 4] for i in range(nw)],
            outs[4 * (n + nw)])


def _place():
    x, y, c = lax.axis_index("x"), lax.axis_index("y"), lax.axis_index("c")
    chip = 2 * x + y
    peers = [(x, 1 - y), (1 - x, y), (1 - x, 1 - y)]
    peer_chip = [2 * px + py for px, py in peers]
    return x, y, c, chip, peers, peer_chip


def _remote(src, dst, send_sem, recv_sem, dev):
    return pltpu.make_async_remote_copy(src_ref=src, dst_ref=dst, send_sem=send_sem, recv_sem=recv_sem,
                                        device_id=dev, device_id_type=MESH)


def _ag_weights(weights, late=()):
    nw, nl = len(weights), len(late)
    pieces = 2

    def body(*refs):
        srcs, late_srcs = refs[:nw], refs[nw:nw + nl]
        outs, late_bf, late_land = (refs[nw + nl:2 * nw + nl], refs[2 * nw + nl:2 * nw + 2 * nl],
                                    refs[2 * nw + 2 * nl:2 * nw + 3 * nl])
        s_ici, r_ici, s_d2d, r_d2d = refs[2 * nw + 3 * nl:]
        x, y, c = lax.axis_index("x"), lax.axis_index("y"), lax.axis_index("c")
        chip = 2 * x + y
        sib = (x, y, 1 - c)
        first = ((x + 1 - c) % 2, (y + c) % 2)
        second = ((x + c) % 2, (y + 1 - c) % 2)
        first_chip, second_chip = 2 * first[0] + first[1], 2 * second[0] + second[1]
        diag_chip = 3 - chip
        for src, out in zip(srcs, outs):
            out[chip] = src[...].astype(BF16)

        parts = [(w, out, pc) for w, out in enumerate(outs) for pc in range(pieces)]

        def piece(out, k, cc, pc):
            rows = out.shape[1] // 2 // pieces
            return out.at[k, pl.ds(pl.multiple_of((cc * pieces + pc) * rows, 16), rows), :]

        def ici(w, slot, out, k, dev, pc):
            blk, sem = piece(out, k, c, pc), (nw * slot + w) * pieces + pc
            return _remote(blk, blk, s_ici.at[sem], r_ici.at[sem], (dev[0], dev[1], c))

        def d2d(w, slot, out, k, cc, pc):
            blk, sem = piece(out, k, cc, pc), (nw * slot + w) * pieces + pc
            return _remote(blk, blk, s_d2d.at[sem], r_d2d.at[sem], sib)

        sent = []
        for slot, dev in enumerate((first, second)):
            for w, out, pc in parts:
                sent.append(ici(w, slot, out, chip, dev, pc))
                sent[-1].start()
        for src, bf, land in zip(late_srcs, late_bf, late_land):
            bf[...] = src[...].astype(BF16)
            land[...] = jnp.zeros_like(land)
            land[chip] = bf[...]
        for slot, k, dev in ((0, first_chip, first), (1, second_chip, second), (2, diag_chip, second)):
            for w, out, pc in parts:
                ici(w, slot, out, k, dev, pc).wait_recv()
                if slot == 0:
                    sent.append(ici(w, 2, out, k, second, pc))
                    sent[-1].start()
                sent.append(d2d(w, slot, out, k, c, pc))
                sent[-1].start()
        for slot, k in ((0, second_chip), (1, first_chip), (2, diag_chip)):
            for w, out, pc in parts:
                d2d(w, slot, out, k, 1 - c, pc).wait_recv()
        for cp in sent:
            cp.wait_send()

    vmem = pl.BlockSpec(memory_space=pltpu.VMEM)
    outs = pl.pallas_call(
        body,
        out_shape=[jax.ShapeDtypeStruct((N_CHIPS,) + w.shape, BF16) for w in weights]
        + [jax.ShapeDtypeStruct(w.shape, BF16) for w in late]
        + [jax.ShapeDtypeStruct((N_CHIPS,) + w.shape, BF16) for w in late],
        in_specs=[vmem] * (nw + nl), out_specs=[vmem] * (nw + 2 * nl),
        scratch_shapes=[pltpu.SemaphoreType.DMA((3 * nw * pieces,))] * 4,
        compiler_params=pltpu.CompilerParams(vmem_limit_bytes=VMEM_LIMIT), name="ag_weights")(*weights, *late)
    return outs[:nw], outs[nw:nw + nl], outs[nw + nl:]


_HBM = pl.BlockSpec(memory_space=pltpu.HBM)
_SEM = pl.BlockSpec(memory_space=pltpu.SEMAPHORE)
_ANY = pl.BlockSpec(memory_space=pl.ANY)
_DATAFLOW = pltpu.SideEffectType.DATAFLOW_SIDE_EFFECTING


def _in_hbm(a):
    return pltpu.with_memory_space_constraint(a, pltpu.HBM)


_PEERS_OF = {"gather": 3, "scatter": 3, "direct": 7}


def _exchange_copies(mode, srcs, lands, send_sems, recv_sems):
    nw = len(srcs)
    x, y, c, chip, peers, peer_chip = _place()
    pairs = []
    if mode == "direct":
        targets = [((x, y), chip, 1)] + [(p, k, d) for p, k in zip(peers, peer_chip) for d in (0, 1)]
        for r, ((px, py), k, d) in enumerate(targets):
            core = (c + d) % 2
            for w in range(nw):
                n = srcs[w].shape[1] // 2
                src = srcs[w].at[k, pl.ds(pl.multiple_of(core * n, 16), n), :]
                sems = (send_sems.at[nw * r + w], recv_sems.at[nw * r + w], (px, py, core))
                pairs.append((_remote(src, lands[w].at[r], *sems),) * 2)
        return pairs
    gather = mode == "gather"
    for m, (px, py) in enumerate(peers):
        for w in range(nw):
            sems = (send_sems.at[nw * m + w], recv_sems.at[nw * m + w], (px, py, c))
            if gather:
                pairs.append((_remote(srcs[w], lands[w].at[chip], *sems),
                              _remote(srcs[w], lands[w].at[peer_chip[m]], *sems)))
            else:
                pairs.append((_remote(srcs[w].at[m], lands[w].at[m], *sems),) * 2)
    return pairs


def _exchange_start(mode, srcs, after, name, lands=None):
    nw = len(srcs)
    n_copies = _PEERS_OF[mode] * nw

    after = tuple(after)

    def body(*refs):
        send_sems, recv_sems = refs[2 * nw + len(after)], refs[2 * nw + len(after) + 1]
        for start, _ in _exchange_copies(mode, refs[:nw], refs[nw:2 * nw], send_sems, recv_sems):
            start.start()
        refs[-1][...] = jnp.zeros_like(refs[-1])

    if lands is None:
        shape = {"gather": lambda s: (N_CHIPS,) + s.shape, "scatter": lambda s: s.shape,
                 "direct": lambda s: (_PEERS_OF["direct"], s.shape[1] // 2, s.shape[2])}[mode]
        lands = [lax.empty(shape(s), s.dtype) for s in srcs]
    lands = [_in_hbm(l) for l in lands]
    return pl.pallas_call(
        body, name=name,
        out_shape=(pltpu.SemaphoreType.DMA((n_copies,)), pltpu.SemaphoreType.DMA((n_copies,)))
        + tuple(pltpu.HBM(s.shape, s.dtype) for s in srcs)
        + tuple(pltpu.HBM(l.shape, l.dtype) for l in lands)
        + (jax.ShapeDtypeStruct((8, LANES), F32),),
        in_specs=[_HBM] * (2 * nw) + [_ANY] * len(after),
        out_specs=(_SEM, _SEM) + (_HBM,) * (2 * nw) + (pl.BlockSpec(memory_space=pltpu.VMEM),),
        input_output_aliases={i: 2 + i for i in range(2 * nw)},
        compiler_params=pltpu.CompilerParams(has_side_effects=_DATAFLOW),
    )(*[_in_hbm(s) for s in srcs], *lands, *after)


def _exchange_wait(mode, started, after, name):
    nw = (len(started) - 3) // 2
    send_sems, recv_sems = started[0], started[1]
    thru = started[2:2 + 2 * nw]

    def body(*refs):
        for _, arrival in _exchange_copies(mode, refs[:nw], refs[nw:2 * nw], refs[2 * nw], refs[2 * nw + 1]):
            arrival.wait_send()
            arrival.wait_recv()

    outs = pl.pallas_call(
        body, name=name,
        out_shape=tuple(pltpu.HBM(t.shape, t.dtype) for t in thru),
        in_specs=[_HBM] * (2 * nw) + [_SEM, _SEM, _ANY], out_specs=(_HBM,) * (2 * nw),
        input_output_aliases={i: i for i in range(2 * nw)},
        compiler_params=pltpu.CompilerParams(has_side_effects=_DATAFLOW),
    )(*thru, send_sems, recv_sems, after)
    return outs[:nw], outs[nw:]


def _reduce_last(owns, landed, g_small, direct_srcs=(), direct_landed=()):
    ns, nd = len(owns), len(direct_srcs)
    halves = [o.shape[0] for o in owns] + [s.shape[1] // 2 for s in direct_srcs]
    row_block = 16
    hs = SMALL_ROWS // 2
    jobs = [(w, p * (halves[w] // 2), halves[w] // 2) for w in range(ns) for p in range(2)]
    jobs += [(ns + d, 0, halves[ns + d]) for d in range(nd)]

    def body(*refs):
        own_refs, land_refs, gsm_ref = refs[:ns], refs[ns:2 * ns], refs[2 * ns]
        dsrc_refs, dland_refs = refs[2 * ns + 1:2 * ns + 1 + nd], refs[2 * ns + 1 + nd:2 * ns + 1 + 2 * nd]
        n_in = 2 * ns + 1 + 2 * nd
        out_refs, osm_ref = refs[n_in:n_in + ns + nd], refs[n_in + ns + nd]
        scr = refs[n_in + ns + nd + 1:]
        own_scr, land_scr, dsrc_scr, dland_scr = (scr[:ns], scr[ns:2 * ns], scr[2 * ns:2 * ns + nd],
                                                  scr[2 * ns + nd:2 * ns + 2 * nd])
        ra_sm, p_sm, in_sem, s_sem, r_sem, sm_s, sm_r = scr[2 * ns + 2 * nd:]
        x, y, c, chip, peers, peer_chip = _place()
        sib = (x, y, 1 - c)
        half = lambda cc: pl.ds(pl.multiple_of(cc * hs, 8), hs)
        sm_a = _remote(gsm_ref.at[half(1 - c), :], ra_sm, sm_s.at[0], sm_r.at[0], sib)
        sm_a.start()
        swaps = [sm_a]

        def reads(j):
            w, r0, n = jobs[j]
            rows = pl.ds(r0, n)
            if w < ns:
                pairs = [(own_refs[w].at[rows, :], own_scr[w].at[rows, :]),
                         (land_refs[w].at[:, rows, :], land_scr[w].at[:, rows, :])]
            else:
                mine = pl.ds(pl.multiple_of(c * n, 16), n)
                pairs = [(dsrc_refs[w - ns].at[chip, mine, :], dsrc_scr[w - ns]),
                         (dland_refs[w - ns], dland_scr[w - ns])]
            return [pltpu.make_async_copy(s, d, in_sem.at[2 * j + i]) for i, (s, d) in enumerate(pairs)]

        for j in range(len(jobs)):
            for cp in reads(j):
                cp.start()
        sm_a.wait_recv()
        p_sm[chip] = gsm_ref[half(c), :] + ra_sm[...]
        for m, (px, py) in enumerate(peers):
            swaps.append(_remote(p_sm.at[chip], p_sm.at[chip], sm_s.at[1 + m], sm_r.at[1 + m], (px, py, c)))
            swaps[-1].start()

        def mine_of(j):
            w, r0, n = jobs[j]
            return out_refs[w].at[pl.ds(pl.multiple_of(c * halves[w] + r0, 8), n), :]

        for j, (w, r0, n) in enumerate(jobs):
            for cp in reads(j):
                cp.wait()

            def total(i, carry, w=w, r0=r0):
                rr = pl.multiple_of(r0 + i * row_block, row_block)
                blk = pl.ds(rr, row_block)
                if w < ns:
                    acc = own_scr[w][blk, :]
                    terms = [land_scr[w][m, blk, :] for m in range(_PEERS_OF["scatter"])]
                else:
                    acc = dsrc_scr[w - ns][blk, :].astype(F32)
                    terms = [dland_scr[w - ns][r, blk, :] for r in range(_PEERS_OF["direct"])]
                for term in terms:
                    acc = acc + term.astype(F32)
                out_refs[w][pl.ds(pl.multiple_of(c * halves[w] + rr, row_block), row_block), :] = acc
                return carry
            lax.fori_loop(0, n // row_block, total, 0)
            swaps.append(_remote(mine_of(j), mine_of(j), s_sem.at[j], r_sem.at[j], sib))
            swaps[-1].start()
        for m, (px, py) in enumerate(peers):
            _remote(p_sm.at[chip], p_sm.at[peer_chip[m]], sm_s.at[1 + m], sm_r.at[1 + m], (px, py, c)).wait_recv()
        osm_ref[half(c), :] = (p_sm[0] + p_sm[1]) + (p_sm[2] + p_sm[3])
        swaps.append(_remote(osm_ref.at[half(c), :], osm_ref.at[half(c), :], sm_s.at[4], sm_r.at[4], sib))
        swaps[-1].start()
        for j, (w, r0, n) in enumerate(jobs):
            theirs = out_refs[w].at[pl.ds(pl.multiple_of((1 - c) * halves[w] + r0, 8), n), :]
            _remote(theirs, theirs, s_sem.at[j], r_sem.at[j], sib).wait_recv()
        _remote(osm_ref.at[half(1 - c), :], osm_ref.at[half(1 - c), :], sm_s.at[4], sm_r.at[4], sib).wait_recv()
        for cp in swaps:
            cp.wait_send()

    vmem = pl.BlockSpec(memory_space=pltpu.VMEM)
    far = [pl.BlockSpec(memory_space=pl.ANY)]
    return pl.pallas_call(
        body, out_shape=[jax.ShapeDtypeStruct((2 * o.shape[0], o.shape[1]), F32) for o in owns]
        + [jax.ShapeDtypeStruct(s.shape[1:], F32) for s in direct_srcs]
        + [jax.ShapeDtypeStruct((SMALL_ROWS, LANES), F32)],
        in_specs=far * (2 * ns) + [vmem] + far * (2 * nd), out_specs=[vmem] * (ns + nd + 1),
        scratch_shapes=[pltpu.VMEM(o.shape, o.dtype) for o in owns] + [pltpu.VMEM(l.shape, l.dtype) for l in landed]
        + [pltpu.VMEM((s.shape[1] // 2, s.shape[2]), s.dtype) for s in direct_srcs]
        + [pltpu.VMEM(l.shape, l.dtype) for l in direct_landed]
        + [pltpu.VMEM((hs, LANES), F32), pltpu.VMEM((N_CHIPS, hs, LANES), F32),
           pltpu.SemaphoreType.DMA((2 * len(jobs),)),
           pltpu.SemaphoreType.DMA((len(jobs),)), pltpu.SemaphoreType.DMA((len(jobs),)),
           pltpu.SemaphoreType.DMA((5,)), pltpu.SemaphoreType.DMA((5,))],
        compiler_params=pltpu.CompilerParams(vmem_limit_bytes=VMEM_LIMIT),
        name="reduce_last")(*owns, *landed, g_small, *direct_srcs, *direct_landed)


_SMALL_PARTS = (("g_norm", 8, 8), ("w_s", 512, 512), ("b_s", 4, 8), ("g_v", 2, 8), ("g_mem", 8, 8),
                ("g_final", 8, 8), ("loss", 8, 8))
_LOSS_ROW = SMALL_ROWS - 8
assert sum(p for _, _, p in _SMALL_PARTS) == SMALL_ROWS


def _pack_small(parts, loss_block):
    rows = []
    for (name, used, padded), p in zip(_SMALL_PARTS, list(parts) + [loss_block]):
        p = p.reshape(used, LANES)
        if padded > used:
            p = jnp.pad(p, ((0, padded - used), (0, 0)))
        rows.append(p)
    return jnp.concatenate(rows, axis=0)


def _local_step(x, mem, target, g_norm, w_in, w_s, b_s, g_v, g_mem, late_weights, g_final,
                fwd_token=None, on_late=None, on_dw=None):
    B, S, _ = x.shape
    x2d = x.reshape(B * S, D_MODEL)
    t2d = target.reshape(B * S, D_MODEL)
    mem2d = mem.reshape(B * N_MEM, D_MODEL)

    proj = _inproj_fwd(x2d, g_norm, w_in, after=() if fwd_token is None else (fwd_token,))
    w_kv, w_out = late_weights(proj)
    kv = _kv_fwd(mem2d, g_mem, w_kv)
    a, lse = _attn_fwd(proj, B, S)
    w_sT = jnp.swapaxes(w_s, 1, 2)
    b_tab = jnp.repeat(b_s.T, HEAD_DIM, axis=1)
    (dx2, da, drest, loss, d_wout, d_ws, d_bs, d_gv, d_gf, dkv) = _mid(
        x2d, t2d, a, proj, kv, w_s, w_sT, b_tab, g_v, w_out, g_final, B, S)
    d_wkv, d_gmem = _kv_bwd(mem2d, g_mem, w_kv, dkv)
    dq, dk, dv = _attn_bwd(proj, a, lse, da, B, S, after=() if on_late is None else (on_late(d_wkv, d_wout),))
    if on_dw is None:
        d_win = _inproj_bwd_dw(dq, dk, dv, drest, x2d, g_norm)
        after = ()
    else:
        d_win = None
        after = (on_dw(*_inproj_bwd_dw(dq, dk, dv, drest, x2d, g_norm, reduce_with=[])),)
    grad_x, d_gnorm = _inproj_bwd_dx(dq, dk, dv, drest, x2d, dx2, g_norm, w_in, after=after)
    d_bs = d_bs[:, :N_SGU_GROUPS].T
    return (loss, grad_x.reshape(B, S, D_MODEL),
            dict(g_norm=d_gnorm, w_in=d_win, w_s=d_ws, b_s=d_bs, g_v=d_gv, g_mem=d_gmem, w_kv=d_wkv,
                 w_out=d_wout, g_final=d_gf))


def kernel(x, mem, g_norm, w_in, w_sgu_spatial, b_sgu_spatial, g_sgu_v, g_mem, w_mem_kv, w_out, g_final, loss_target, m_g_norm, m_w_in, m_w_sgu_spatial, m_b_sgu_spatial, m_g_sgu_v, m_g_mem, m_w_mem_kv, m_w_out, m_g_final, v_g_norm, v_w_in, v_w_sgu_spatial, v_b_sgu_spatial, v_g_sgu_v, v_g_mem, v_w_mem_kv, v_w_out, v_g_final):
    t = lambda w: jnp.swapaxes(w[0], 0, 1)
    (win_all,), late_shards, late_lands = _ag_weights([t(w_in)], [w_mem_kv[0], w_out[0]])
    w_in_full = win_all.reshape(-1, win_all.shape[-1])
    late = _exchange_start("gather", list(late_shards), (win_all,), "gather_late_start", lands=late_lands)

    def late_weights(proj):
        return [z.reshape(-1, z.shape[-1]) for z in _exchange_wait("gather", late, proj, "gather_late_wait")[1]]

    scatter = {}

    def on_late(d_wkv, d_wout):
        scatter["late"] = _exchange_start("direct", [d_wkv, d_wout], (), "scatter_late_start")
        return scatter["late"][-1]

    def on_dw(sends, owns):
        scatter["own"] = owns
        scatter["started"] = _exchange_start("scatter", list(sends), (owns[0],), "scatter_start")
        return scatter["started"][-1]

    loss, grad_x, g = _local_step(
        x, mem, loss_target, g_norm, w_in_full, w_sgu_spatial[0], b_sgu_spatial[0], g_sgu_v, g_mem,
        late_weights, g_final.reshape(1, D_MODEL), fwd_token=late[-1], on_late=on_late, on_dw=on_dw)

    small_names = ("g_norm", "w_s", "b_s", "g_v", "g_mem", "g_final")
    g_small = _pack_small([g[n] for n in small_names], loss)
    late_srcs, late_landed = _exchange_wait("direct", scatter["late"], g_small, "scatter_late_wait")
    _, landed = _exchange_wait("scatter", scatter["started"], late_landed[0], "scatter_wait")
    gr_in, gr_kv, gr_out, gr_small = _reduce_last(scatter["own"], landed, g_small, late_srcs, late_landed)

    small_w = (g_norm, w_sgu_spatial, b_sgu_spatial, g_sgu_v, g_mem, g_final)
    small_m = (m_g_norm, m_w_sgu_spatial, m_b_sgu_spatial, m_g_sgu_v, m_g_mem, m_g_final)
    small_v = (v_g_norm, v_w_sgu_spatial, v_b_sgu_spatial, v_g_sgu_v, v_g_mem, v_g_final)
    rows = lambda ws: [w.reshape(-1, LANES) for w in ws]
    small_new, ((gr_kv, d_kv, nm_kv, nv_kv), (gr_out, d_out, nm_out, nv_out)), loss = _adamw_rest(
        gr_small, rows(small_w), rows(small_m), rows(small_v),
        [(w_mem_kv[0], gr_kv, m_w_mem_kv[0], v_w_mem_kv[0]), (w_out[0], gr_out, m_w_out[0], v_w_out[0])])
    loss = loss.reshape(())
    small = [[z.reshape(w.shape) for z in four] for w, four in zip(small_w, small_new)]
    gr_in, d_in, nm_in, nv_in = [jnp.swapaxes(z, 0, 1)
                                 for z in _adamw(t(w_in), gr_in, t(m_w_in), t(v_w_in), "adamw_w_in")]

    def leaves(kind, big_in, big_kv, big_out):
        s_norm, s_ws, s_bs, s_gv, s_gmem, s_gf = [four[kind] for four in small]
        return [s_norm, big_in[None], s_ws, s_bs, s_gv, s_gmem, big_kv[None], big_out[None], s_gf]

    return (loss, grad_x, *leaves(0, gr_in, gr_kv, gr_out), *leaves(1, d_in, d_kv, d_out),
            *leaves(2, nm_in, nm_kv, nm_out), *leaves(3, nv_in, nv_kv, nv_out))
```

```python
import functools

import jax
import jax.numpy as jnp
from jax import lax
from jax.experimental import pallas as pl
from jax.experimental.pallas import tpu as pltpu

F32 = jnp.float32
BF16 = jnp.bfloat16
MESH = pl.DeviceIdType.MESH

D_MODEL = 1024
ATTN_WIDTH = 512
SGU_WIDTH = 256
MEM_WIDTH = 256
N_MEM = 256
IN_COLS = 3328
QKV_COLS = 3 * ATTN_WIDTH
REST_COLS = IN_COLS - QKV_COLS
SGU_CHUNK = 128
N_SGU_GROUPS = 4
EPS = 1e-6
NEG_INF = -1e30
DILATIONS = (1, 4, 16)
RADIUS = 64
Q_BLOCK = 128
LANES = 128
HEAD_DIM = 64

ADAM_LR = 0.001
ADAM_B1 = 0.9
ADAM_B2 = 0.999
ADAM_EPS = 1e-08
ADAM_WD = 0.01
ADAM_STEP = 10

N_CHIPS = 4
VMEM_LIMIT = 56 * 1024 * 1024
SMALL_ROWS = 560


def _params(sem=None, vmem=VMEM_LIMIT):
    return pltpu.CompilerParams(dimension_semantics=sem, vmem_limit_bytes=vmem)


def _nn(a, b):
    return jnp.dot(a, b, preferred_element_type=F32)


def _nt(a, b):
    return lax.dot_general(a, b, (((1,), (1,)), ((), ())), preferred_element_type=F32)


def _tn(a, b):
    return lax.dot_general(a, b, (((0,), (0,)), ((), ())), preferred_element_type=F32)


def _rms(x):
    r = lax.rsqrt(jnp.mean(x * x, axis=-1, keepdims=True) + EPS)
    return r, x * r


def _head_masks():
    lane = lax.broadcasted_iota(jnp.int32, (1, LANES), 1)
    lo = lane < HEAD_DIM
    return lo, (lo.astype(F32), (~lo).astype(F32))


def _silu_parts(z):
    s = jax.nn.sigmoid(z)
    return z * s, s * (1.0 + z * (1.0 - s))


def _gelu_parts(x):
    c = 0.7978845608028654
    x2 = x * x
    s = jax.nn.sigmoid((2.0 * c) * (x + 0.044715 * (x * x2)))
    return x * s, s * (1.0 + x * (1.0 - s) * ((2.0 * c) * (1.0 + 3.0 * 0.044715 * x2)))


def _after(tokens):
    return [pl.BlockSpec(memory_space=pl.ANY)] * len(tokens)


def _inproj_fwd(x2d, g_norm, w_in_t, after=()):
    T = x2d.shape[0]
    tm = 512

    def body(x_ref, g_ref, w_ref, *rest):
        o_ref = rest[-1]
        _, xh = _rms(x_ref[...])
        h = (xh * g_ref[...]).astype(BF16)
        o_ref[...] = _nt(h, w_ref[...])

    return pl.pallas_call(
        body, grid=(T // tm,),
        in_specs=[pl.BlockSpec((tm, D_MODEL), lambda i: (i, 0)),
                  pl.BlockSpec((1, D_MODEL), lambda i: (0, 0)),
                  pl.BlockSpec((IN_COLS, D_MODEL), lambda i: (0, 0))] + _after(after),
        out_specs=pl.BlockSpec((tm, IN_COLS), lambda i: (i, 0)),
        out_shape=jax.ShapeDtypeStruct((T, IN_COLS), F32),
        compiler_params=_params(("arbitrary",)), name="inproj_fwd")(x2d, g_norm, w_in_t, *after)


def _kv_fwd(mem2d, g_mem, w_kv):
    Tm = mem2d.shape[0]

    def body(m_ref, g_ref, w_ref, o_ref):
        _, mh = _rms(m_ref[...])
        o_ref[...] = _nn((mh * g_ref[...]).astype(BF16), w_ref[...])

    return pl.pallas_call(
        body, out_shape=jax.ShapeDtypeStruct((Tm, 2 * MEM_WIDTH), F32),
        compiler_params=_params(), name="kv_fwd")(mem2d, g_mem, w_kv)


def _kv_bwd(mem2d, g_mem, w_kv, dkv):
    Tm = mem2d.shape[0]

    def body(m_ref, g_ref, w_ref, dkv_ref, dw_ref, dg_ref):
        _, mh = _rms(m_ref[...])
        memn = (mh * g_ref[...]).astype(BF16)
        dkvb = dkv_ref[...].astype(BF16)
        dw = _tn(memn, dkvb).astype(BF16)
        for k in range(N_CHIPS):
            dw_ref[k] = dw[k * (D_MODEL // N_CHIPS):(k + 1) * (D_MODEL // N_CHIPS), :]
        dmemn = _nt(dkvb, w_ref[...])
        dg_ref[...] = jnp.sum(dmemn * mh, axis=0, keepdims=True)

    return pl.pallas_call(
        body, out_shape=(jax.ShapeDtypeStruct((N_CHIPS, D_MODEL // N_CHIPS, 2 * MEM_WIDTH), BF16),
                         jax.ShapeDtypeStruct((1, D_MODEL), F32)),
        compiler_params=_params(), name="kv_bwd")(mem2d, g_mem, w_kv, dkv)


def _attn_geometry(S):
    geom = []
    for d in DILATIONS:
        L = S // d
        assert L % Q_BLOCK == 0
        geom.append((d, L, min(2 * Q_BLOCK, L), L // Q_BLOCK))
    return geom


def _init_bias(bias_scr, geom, hp):
    row = lax.broadcasted_iota(jnp.int32, (Q_BLOCK, 2 * Q_BLOCK), 0)
    col = lax.broadcasted_iota(jnp.int32, (Q_BLOCK, 2 * Q_BLOCK), 1)
    for j in (0, 1):
        bits = (126 - (2 * hp + j)) * (1 << 23)
        slope = lax.bitcast_convert_type(jnp.full((1, 1), bits, jnp.int32), F32)
        for di, (d, _, _, _) in enumerate(geom):
            for cls, off in enumerate((0, -RADIUS, -2 * RADIUS)):
                dist = jnp.abs(col - row + off)
                bias_scr[di * 6 + cls * 2 + j] = jnp.where(
                    dist <= RADIUS, -(slope * float(d)) * dist.astype(F32), NEG_INF)


SPLIT = 4
COPY_ROWS = 256


def _by4_rows(S, step):
    per_class = S // SPLIT // COPY_ROWS
    r, j = step // per_class, step % per_class
    return (pl.ds(r + SPLIT * j * COPY_ROWS, COPY_ROWS, stride=SPLIT),
            pl.ds(r * (S // SPLIT) + j * COPY_ROWS, COPY_ROWS))


def _to_by4(src, dst, S):
    for i in range(S // COPY_ROWS):
        natural, by4 = _by4_rows(S, i)
        dst[by4, :] = src[natural, :]


def _block_slices(d, L, KW, nqb, r, qb, S):
    qs = qb * Q_BLOCK
    ks = jnp.clip(qs - RADIUS, 0, L - KW)
    cls = jnp.where(qb == 0, 0, jnp.where(qb == nqb - 1, 2, 1))
    if d == 1:
        qsl = pl.ds(pl.multiple_of(qs, Q_BLOCK), Q_BLOCK)
        ksl = pl.ds(pl.multiple_of(ks, RADIUS), KW)
    elif d == SPLIT:
        qsl = pl.ds(pl.multiple_of(r * L + qs, Q_BLOCK), Q_BLOCK)
        ksl = pl.ds(pl.multiple_of(r * L + ks, RADIUS), KW)
    else:
        sub = d // SPLIT
        base = (r % SPLIT) * (S // SPLIT) + r // SPLIT
        qsl = pl.ds(base + qs * sub, Q_BLOCK, stride=sub)
        ksl = pl.ds(base + ks * sub, KW, stride=sub)
    return qsl, ksl, cls


def _for_groups(geom, S, group, fn):
    for di, (d, L, KW, nqb) in enumerate(geom):
        n = group[di]
        assert (d * nqb) % n == 0

        def step(it, carry, di=di, d=d, L=L, KW=KW, nqb=nqb, n=n):
            slices = []
            for g in range(n):
                i = it * n + g
                slices.append(_block_slices(d, L, KW, nqb, i // nqb, i % nqb, S))
            fn(di, KW, slices)
            return carry
        lax.fori_loop(0, d * nqb // n, step, 0)


def _attn_fwd(proj, B, S):
    T = B * S
    geom = _attn_geometry(S)
    n_pairs = ATTN_WIDTH // LANES

    def body(q_ref, k_ref, v_ref, a_ref, lse_ref, bias_scr, q4, k4, v4, *per_dilation):
        o_scr, m_scr, l_scr = per_dilation[0:3], per_dilation[3:6], per_dilation[6:9]
        lo, hm = _head_masks()
        pair = pl.program_id(0)

        @pl.when(pl.program_id(1) == 0)
        def _():
            _init_bias(bias_scr, geom, pair)
        for src, dst in ((q_ref, q4), (k_ref, k4), (v_ref, v4)):
            _to_by4(src, dst, S)

        def group(di, KW, all_slices):
            run = 8
            for first in range(0, len(all_slices), run):
                some(di, KW, all_slices[first:first + run])

        def some(di, KW, slices):
            chains = [(g, j) for g in range(len(slices)) for j in (0, 1)]
            q_src, k_src, v_src = (q_ref, k_ref, v_ref) if di == 0 else (q4, k4, v4)
            q = [q_src[qsl, :] for qsl, _, _ in slices]
            kw = [k_src[ksl, :].astype(BF16) for _, ksl, _ in slices]
            vw = [v_src[ksl, :].astype(BF16) for _, ksl, _ in slices]
            s = {(g, j): _nt((q[g] * (hm[j] * 0.125)).astype(BF16), kw[g])
                 + bias_scr[di * 6 + slices[g][2] * 2 + j, :, pl.ds(0, KW)] for g, j in chains}
            m = {c: jnp.max(s[c], axis=1, keepdims=True) for c in chains}
            p = {c: jnp.exp(s[c] - m[c]) for c in chains}
            l = {c: jnp.sum(p[c], axis=1, keepdims=True) for c in chains}
            o = {(g, j): _nn(p[(g, j)].astype(BF16), vw[g]) for g, j in chains}
            for g, (qsl, _, _) in enumerate(slices):
                o_scr[di][qsl, :] = jnp.where(lo, o[(g, 0)], o[(g, 1)])
                m_scr[di][qsl, :] = jnp.where(lo, m[(g, 0)], m[(g, 1)])
                l_scr[di][qsl, :] = jnp.where(lo, l[(g, 0)], l[(g, 1)])

        _for_groups(geom, S, (16, 16, 16), group)

        for i in range(S // COPY_ROWS):
            natural, by4 = _by4_rows(S, i)
            rows = [natural, by4, by4]
            ms = [m_scr[di][rows[di], :] for di in range(3)]
            mx = jnp.maximum(jnp.maximum(ms[0], ms[1]), ms[2])
            num = 0.0
            den = 0.0
            for di in range(3):
                w = jnp.exp(ms[di] - mx)
                num = num + w * o_scr[di][rows[di], :]
                den = den + w * l_scr[di][rows[di], :]
            a_ref[natural, :] = num / den
            lse_ref[natural, :] = mx + jnp.log(den)

    blk = lambda off: pl.BlockSpec((S, LANES), lambda h, b, off=off: (b, off + h))
    out_blk = pl.BlockSpec((S, LANES), lambda h, b: (b, h))
    return pl.pallas_call(
        body, grid=(n_pairs, B),
        in_specs=[blk(0), blk(n_pairs), blk(2 * n_pairs)],
        out_specs=[out_blk, out_blk],
        out_shape=[jax.ShapeDtypeStruct((T, ATTN_WIDTH), F32)] * 2,
        scratch_shapes=[pltpu.VMEM((18, Q_BLOCK, 2 * Q_BLOCK), F32)] + [pltpu.VMEM((S, LANES), F32)] * 12,
        compiler_params=_params(("arbitrary", "arbitrary")), name="attn_fwd")(proj, proj, proj)


def _attn_bwd(proj, a, lse, da, B, S, after=()):
    T = B * S
    geom = _attn_geometry(S)
    n_pairs = ATTN_WIDTH // LANES

    def body(q_ref, k_ref, v_ref, a_ref, lse_ref, do_ref, *rest):
        dq_ref, dk_ref, dv_ref, bias_scr = rest[len(after):len(after) + 4]
        scr = rest[len(after) + 4:]
        acc = (scr[0:3], scr[3:6])
        natural_in = (q_ref, k_ref, v_ref, a_ref, lse_ref, do_ref)
        by4_in = scr[6:12]
        _, hm = _head_masks()
        pair = pl.program_id(0)

        @pl.when(pl.program_id(1) == 0)
        def _():
            _init_bias(bias_scr, geom, pair)
        for ref in scr[0:6]:
            ref[...] = jnp.zeros_like(ref)
        for src, dst in zip(natural_in, by4_in):
            _to_by4(src, dst, S)

        def group(di, KW, all_slices):
            run = (4, 4, 8)[di]
            for first in range(0, len(all_slices), run):
                some(di, KW, all_slices[first:first + run])

        def some(di, KW, slices):
            n = len(slices)
            chains = [(g, j) for g in range(n) for j in (0, 1)]
            q_src, k_src, v_src, a_src, lse_src, do_src = natural_in if di == 0 else by4_in
            dq_scr, dk_scr, dv_scr = acc[0 if di == 0 else 1]
            q = [q_src[qsl, :] for qsl, _, _ in slices]
            do = [do_src[qsl, :] for qsl, _, _ in slices]
            doa = [do[g] * a_src[slices[g][0], :] for g in range(n)]
            lse_q = [lse_src[qsl, :] for qsl, _, _ in slices]
            kw = [k_src[ksl, :].astype(BF16) for _, ksl, _ in slices]
            vw = [v_src[ksl, :].astype(BF16) for _, ksl, _ in slices]
            qj = {(g, j): (q[g] * (hm[j] * 0.125)).astype(BF16) for g, j in chains}
            doj = {(g, j): (do[g] * hm[j]).astype(BF16) for g, j in chains}
            s = {(g, j): _nt(qj[(g, j)], kw[g])
                 + bias_scr[di * 6 + slices[g][2] * 2 + j, :, pl.ds(0, KW)] for g, j in chains}
            dp = {(g, j): _nt(doj[(g, j)], vw[g]) for g, j in chains}
            dsum = {(g, j): jnp.sum(doa[g] * hm[j], axis=1, keepdims=True) for g, j in chains}
            p = {(g, j): jnp.exp(s[(g, j)] - lse_q[g][:, HEAD_DIM * j:HEAD_DIM * j + 1]) for g, j in chains}
            ds = {c: (p[c] * (dp[c] - dsum[c])).astype(BF16) for c in chains}
            pb = {c: p[c].astype(BF16) for c in chains}
            dq = [_nn(ds[(g, 0)], kw[g]) * (hm[0] * 0.125) + _nn(ds[(g, 1)], kw[g]) * (hm[1] * 0.125)
                  for g in range(n)]
            both = lambda t, g: jnp.concatenate([t[(g, 0)], t[(g, 1)]], axis=0)
            dkw = [_tn(both(ds, g), both(qj, g)) for g in range(n)]
            dvw = [_tn(both(pb, g), both(doj, g)) for g in range(n)]
            for g, (qsl, ksl, _) in enumerate(slices):
                dq_scr[qsl, :] = dq_scr[qsl, :] + dq[g]
                dk_scr[ksl, :] = dk_scr[ksl, :] + dkw[g]
                dv_scr[ksl, :] = dv_scr[ksl, :] + dvw[g]

        _for_groups(geom, S, (16, 16, 16), group)

        for i in range(S // COPY_ROWS):
            natural, by4 = _by4_rows(S, i)
            for nat, split in zip(*acc):
                nat[natural, :] = nat[natural, :] + split[by4, :]
        for out, nat in zip((dq_ref, dk_ref, dv_ref), acc[0]):
            out[...] = nat[...].astype(BF16)

    blk = lambda off: pl.BlockSpec((S, LANES), lambda h, b, off=off: (b, off + h))
    return pl.pallas_call(
        body, grid=(n_pairs, B),
        in_specs=[blk(0), blk(n_pairs), blk(2 * n_pairs), blk(0), blk(0), blk(0)] + _after(after),
        out_specs=[blk(0), blk(0), blk(0)],
        out_shape=[jax.ShapeDtypeStruct((T, ATTN_WIDTH), BF16)] * 3,
        scratch_shapes=[pltpu.VMEM((18, Q_BLOCK, 2 * Q_BLOCK), F32)] + [pltpu.VMEM((S, LANES), F32)] * 12,
        compiler_params=_params(("arbitrary", "arbitrary")), name="attn_bwd")(proj, proj, proj, a, lse, da, *after)


def _mid(x2d, t2d, a, proj, kv, w_s, w_sT, b_tab, g_v, w_out, g_final, B, S):
    T = B * S
    tm = 512
    nt = S // tm
    halves = 2
    hrows = tm // halves

    def body(x_ref, t_ref, a_ref, za_ref, ub_ref, vb_ref, zb_ref, qm_ref, zm_ref, kv_ref,
              ws_ref, wsT_ref, btab_ref, gv_ref, wout_ref, gf_ref,
              dx2_ref, da_ref, drest_ref, loss_ref, dwout_bf_ref, dws_ref, dbs_ref, dgv_ref, dgf_ref, dkv_ref,
              dbtab_scr, dwout_ref):
        b = pl.program_id(0)
        t = pl.program_id(1)
        first = jnp.logical_and(b == 0, t == 0)
        last = jnp.logical_and(b == B - 1, t == nt - 1)
        _, hm = _head_masks()
        lane_g = lax.broadcasted_iota(jnp.int32, (1, SGU_WIDTH), 1) // HEAD_DIM
        gm = [(lane_g == g).astype(F32) for g in range(N_SGU_GROUPS)]
        H = range(halves)
        rows = [pl.ds(h * hrows, hrows) for h in H]
        ld = lambda ref: [ref[r, :] for r in rows]
        cat = lambda parts, axis: jnp.concatenate(parts, axis=axis)
        chunks = [slice(ci * SGU_CHUNK, (ci + 1) * SGU_CHUNK) for ci in range(hrows // SGU_CHUNK)]
        pairs = [slice(pr * LANES, (pr + 1) * LANES) for pr in range(2)]
        heads = [(pr, j) for pr in range(2) for j in (0, 1)]

        @pl.when(first)
        def _():
            loss_ref[...] = jnp.zeros_like(loss_ref)
            dwout_ref[...] = jnp.zeros_like(dwout_ref)
            dws_ref[...] = jnp.zeros_like(dws_ref)
            dbs_ref[...] = jnp.zeros_like(dbs_ref)
            dgv_ref[...] = jnp.zeros_like(dgv_ref)
            dgf_ref[...] = jnp.zeros_like(dgf_ref)
            dbtab_scr[...] = jnp.zeros_like(dbtab_scr)

        @pl.when(t == 0)
        def _():
            dkv_ref[...] = jnp.zeros_like(dkv_ref)

        a_val = ld(a_ref)
        sil_a = [_silu_parts(z) for z in ld(za_ref)]
        gated_a = [s[0] * a for s, a in zip(sil_a, a_val)]
        u = [_gelu_parts(z) for z in ld(ub_ref)]
        vv = [_gelu_parts(z) for z in ld(vb_ref)]
        vnorm = [_rms(v[0]) for v in vv]
        gv = gv_ref[...]
        vn = [(n[1] * gv).astype(BF16) for n in vnorm]
        w_cat = cat([ws_ref[g].astype(BF16) for g in range(N_SGU_GROUPS)], 1)
        wT_cat = cat([wsT_ref[g].astype(BF16) for g in range(N_SGU_GROUPS)], 1)
        gmb = [m.astype(BF16) for m in gm]
        by_group = lambda chunk: cat([chunk * gmb[g] for g in range(N_SGU_GROUPS)], 0)
        btab = btab_ref[...]
        mixed = [cat([btab + _nn(w_cat, by_group(vn[h][c, :])) for c in chunks], 0) for h in H]
        sg = [u[h][0] * mixed[h] for h in H]
        sil_b = [_silu_parts(z) for z in ld(zb_ref)]
        gated_b = [sil_b[h][0] * sg[h] for h in H]

        kvv = kv_ref[...].astype(BF16)
        kp = [kvv[:, p] for p in pairs]
        vp = [kvv[:, MEM_WIDTH + pr * LANES:MEM_WIDTH + (pr + 1) * LANES] for pr in range(2)]
        qm = ld(qm_ref)
        qj = {(h, pr, j): (qm[h][:, pairs[pr]] * (hm[j] * 0.125)).astype(BF16) for h in H for pr, j in heads}
        sc = {k: _nt(qj[k], kp[k[1]]) for k in qj}
        ex = {k: jnp.exp(sc[k] - jnp.max(sc[k], axis=1, keepdims=True)) for k in qj}
        prob = {k: ex[k] * (1.0 / jnp.sum(ex[k], axis=1, keepdims=True)) for k in qj}
        probb = {k: prob[k].astype(BF16) for k in qj}
        mo = [cat([sum(_nn(probb[(h, pr, j)], vp[pr]) * hm[j] for j in (0, 1)) for pr in range(2)], 1) for h in H]
        sil_m = [_silu_parts(z) for z in ld(zm_ref)]
        gated_m = [sil_m[h][0] * mo[h] for h in H]

        gated = [cat([gated_a[h], gated_b[h], gated_m[h]], 1).astype(BF16) for h in H]
        wout = wout_ref[...]
        x_in = ld(x_ref)
        x2 = [x_in[h] + _nn(gated[h], wout) for h in H]
        fin = [_rms(z) for z in x2]
        gf = gf_ref[...]
        tgt = ld(t_ref)
        err = [fin[h][1] * gf - tgt[h] for h in H]
        loss_ref[...] += sum(jnp.sum(e * e) for e in err) * (0.5 / D_MODEL)

        dy = [e * (1.0 / D_MODEL) for e in err]
        dgf_ref[...] += sum(jnp.sum(dy[h] * fin[h][1], axis=0, keepdims=True) for h in H)
        gdy = [d * gf for d in dy]
        dx2 = [fin[h][0] * (gdy[h] - fin[h][1] * jnp.mean(gdy[h] * fin[h][1], axis=1, keepdims=True)) for h in H]
        for h in H:
            dx2_ref[rows[h], :] = dx2[h]
        dx2b = [d.astype(BF16) for d in dx2]
        dgated = [_nt(d, wout) for d in dx2b]
        dwout_ref[...] += _tn(cat(gated, 0), cat(dx2b, 0))
        dga = [d[:, 0:ATTN_WIDTH] for d in dgated]
        dgb = [d[:, ATTN_WIDTH:ATTN_WIDTH + SGU_WIDTH] for d in dgated]
        dgm = [d[:, ATTN_WIDTH + SGU_WIDTH:] for d in dgated]

        for h in H:
            da_ref[rows[h], :] = dga[h] * sil_a[h][0]
        dza = [dga[h] * a_val[h] * sil_a[h][1] for h in H]

        dsg = [dgb[h] * sil_b[h][0] for h in H]
        dzb = [dgb[h] * sg[h] * sil_b[h][1] for h in H]
        dub = [dsg[h] * mixed[h] * u[h][1] for h in H]
        dmixed = [dsg[h] * u[h][0] for h in H]
        dmixed_b = [d.astype(BF16) for d in dmixed]
        dvn = [cat([_nn(wT_cat, by_group(dmixed_b[h][c, :])) for c in chunks], 0) for h in H]
        for g in range(N_SGU_GROUPS):
            dws_ref[g] += sum(_nt((dmixed[h][c, :] * gm[g]).astype(BF16), vn[h][c, :]) for h in H for c in chunks)
        dbtab_scr[...] += sum(dmixed[h][c, :] for h in H for c in chunks)
        dgv_ref[...] += sum(jnp.sum(dvn[h] * vnorm[h][1], axis=0, keepdims=True) for h in H)
        tv = [d * gv for d in dvn]
        dvv = [vnorm[h][0] * (tv[h] - vnorm[h][1] * jnp.mean(tv[h] * vnorm[h][1], axis=1, keepdims=True)) for h in H]
        dvb = [dvv[h] * vv[h][1] for h in H]

        dmo = [dgm[h] * sil_m[h][0] for h in H]
        dzm = [dgm[h] * mo[h] * sil_m[h][1] for h in H]
        dmoj = {(h, pr, j): (dmo[h][:, pairs[pr]] * hm[j]).astype(BF16) for h in H for pr, j in heads}
        dp = {k: _nt(dmoj[k], vp[k[1]]) for k in qj}
        ds = {k: (prob[k] * (dp[k] - jnp.sum(dp[k] * prob[k], axis=1, keepdims=True))).astype(BF16) for k in qj}
        dqm = [cat([sum(_nn(ds[(h, pr, j)], kp[pr]) * (hm[j] * 0.125) for j in (0, 1)) for pr in range(2)], 1)
               for h in H]
        every = lambda tbl, pr: cat([tbl[(h, pr, j)] for h in H for j in (0, 1)], 0)
        dk = [_tn(every(ds, pr), every(qj, pr)) for pr in range(2)]
        dv = [_tn(every(probb, pr), every(dmoj, pr)) for pr in range(2)]
        dkv_ref[...] += cat(dk + dv, 1)

        for h in H:
            drest_ref[rows[h], :] = cat([dza[h], dub[h], dvb[h], dzb[h], dqm[h], dzm[h]], 1).astype(BF16)

        @pl.when(last)
        def _():
            lane = lax.broadcasted_iota(jnp.int32, (1, LANES), 1)
            dbt = dbtab_scr[...]
            out = jnp.zeros((SGU_CHUNK, LANES), F32)
            for g in range(N_SGU_GROUPS):
                out = out + jnp.where(lane == g, jnp.sum(dbt * gm[g], axis=1, keepdims=True), 0.0)
            dbs_ref[...] = out
            for r0 in range(0, D_MODEL, SGU_CHUNK):
                k, row = divmod(r0, D_MODEL // N_CHIPS)
                dwout_bf_ref[k, row:row + SGU_CHUNK, :] = dwout_ref[r0:r0 + SGU_CHUNK, :].astype(BF16)

    tile = lambda w, cb: pl.BlockSpec((tm, w), lambda b, t, cb=cb: (b * nt + t, cb))
    const = lambda shape: pl.BlockSpec(shape, lambda b, t, n=len(shape): (0,) * n)
    return pl.pallas_call(
        body, grid=(B, nt),
        in_specs=[tile(D_MODEL, 0), tile(D_MODEL, 0), tile(ATTN_WIDTH, 0),
                  tile(ATTN_WIDTH, 3),
                  tile(SGU_WIDTH, 8), tile(SGU_WIDTH, 9), tile(SGU_WIDTH, 10),
                  tile(MEM_WIDTH, 11), tile(MEM_WIDTH, 12),
                  pl.BlockSpec((N_MEM, 2 * MEM_WIDTH), lambda b, t: (b, 0)),
                  const((N_SGU_GROUPS, SGU_CHUNK, SGU_CHUNK)), const((N_SGU_GROUPS, SGU_CHUNK, SGU_CHUNK)),
                  const((SGU_CHUNK, SGU_WIDTH)), const((1, SGU_WIDTH)),
                  const((D_MODEL, D_MODEL)), const((1, D_MODEL))],
        out_specs=[tile(D_MODEL, 0), tile(ATTN_WIDTH, 0), tile(REST_COLS, 0),
                   const((8, LANES)), const((N_CHIPS, D_MODEL // N_CHIPS, D_MODEL)),
                   const((N_SGU_GROUPS, SGU_CHUNK, SGU_CHUNK)), const((SGU_CHUNK, LANES)),
                   const((1, SGU_WIDTH)), const((1, D_MODEL)),
                   pl.BlockSpec((N_MEM, 2 * MEM_WIDTH), lambda b, t: (b, 0))],
        out_shape=[jax.ShapeDtypeStruct((T, D_MODEL), F32), jax.ShapeDtypeStruct((T, ATTN_WIDTH), F32),
                   jax.ShapeDtypeStruct((T, REST_COLS), BF16),
                   jax.ShapeDtypeStruct((8, LANES), F32),
                   jax.ShapeDtypeStruct((N_CHIPS, D_MODEL // N_CHIPS, D_MODEL), BF16),
                   jax.ShapeDtypeStruct((N_SGU_GROUPS, SGU_CHUNK, SGU_CHUNK), F32),
                   jax.ShapeDtypeStruct((SGU_CHUNK, LANES), F32),
                   jax.ShapeDtypeStruct((1, SGU_WIDTH), F32), jax.ShapeDtypeStruct((1, D_MODEL), F32),
                   jax.ShapeDtypeStruct((B * N_MEM, 2 * MEM_WIDTH), F32)],
        scratch_shapes=[pltpu.VMEM((SGU_CHUNK, SGU_WIDTH), F32), pltpu.VMEM((D_MODEL, D_MODEL), F32)],
        compiler_params=_params(("arbitrary", "arbitrary"), vmem=VMEM_LIMIT + 2 * 1024 * 1024), name="mid")(
            x2d, t2d, a, proj, proj, proj, proj, proj, proj, kv, w_s, w_sT, b_tab, g_v, w_out, g_final)


def _inproj_bwd_dx(dq, dk, dv, drest, x2d, dx2, g_norm, w_in_t, after=()):
    T = x2d.shape[0]
    tm = 512
    W = ATTN_WIDTH

    def body(dq_ref, dk_ref, dv_ref, dr_ref, x_ref, dx2_ref, g_ref, w_ref, *rest):
        gx_ref, dg_ref = rest[-2:]

        @pl.when(pl.program_id(0) == 0)
        def _():
            dg_ref[...] = jnp.zeros_like(dg_ref)

        halves = [pl.ds(h * (tm // 2), tm // 2) for h in (0, 1)]
        dh = [(_nn(dq_ref[r, :], w_ref[0:W, :]) + _nn(dk_ref[r, :], w_ref[W:2 * W, :])
               + _nn(dv_ref[r, :], w_ref[2 * W:3 * W, :]) + _nn(dr_ref[r, :], w_ref[QKV_COLS:IN_COLS, :]))
              for r in halves]
        nrm = [_rms(x_ref[r, :]) for r in halves]
        dg_ref[...] += sum(jnp.sum(d * n[1], axis=0, keepdims=True) for d, n in zip(dh, nrm))
        g = g_ref[...]
        for r, d, (rstd, xh) in zip(halves, dh, nrm):
            th = d * g
            gx_ref[r, :] = rstd * (th - xh * jnp.mean(th * xh, axis=1, keepdims=True)) + dx2_ref[r, :]

    tile = lambda w: pl.BlockSpec((tm, w), lambda i: (i, 0))
    return pl.pallas_call(
        body, grid=(T // tm,),
        in_specs=[tile(W), tile(W), tile(W), tile(REST_COLS), tile(D_MODEL), tile(D_MODEL),
                  pl.BlockSpec((1, D_MODEL), lambda i: (0, 0)),
                  pl.BlockSpec((IN_COLS, D_MODEL), lambda i: (0, 0))] + _after(after),
        out_specs=[tile(D_MODEL), pl.BlockSpec((1, D_MODEL), lambda i: (0, 0))],
        out_shape=[jax.ShapeDtypeStruct((T, D_MODEL), F32), jax.ShapeDtypeStruct((1, D_MODEL), F32)],
        compiler_params=_params(("arbitrary",)), name="inproj_bwd_dx")(
            dq, dk, dv, drest, x2d, dx2, g_norm, w_in_t, *after)


def _inproj_bwd_dw(dq, dk, dv, drest, x2d, g_norm, reduce_with=None):
    T = x2d.shape[0]
    tm = 512
    nt = T // tm
    W = ATTN_WIDTH
    fused = reduce_with is not None
    others = list(reduce_with) if fused else []
    ns = 1 + len(others)
    shard = IN_COLS // N_CHIPS
    halves = [shard // 2] + [s.shape[1] // 2 for s in others]
    cols = [D_MODEL] + [s.shape[2] for s in others]
    row_block = 32

    def body(dq_ref, dk_ref, dv_ref, dr_ref, x_ref, g_ref, *rest):
        if fused:
            stacks = rest[:ns - 1]
            sends, owns = rest[ns - 1:2 * ns - 1], rest[2 * ns - 1:3 * ns - 1]
            acc, ras, narrow = rest[3 * ns - 1], rest[3 * ns:4 * ns], rest[4 * ns]
            s_sem, r_sem = rest[4 * ns + 1], rest[4 * ns + 2]
            x, y, c, chip, peers, peer_chip = _place()
            sib = (x, y, 1 - c)

            def part(w, k, cc, r0=0, rows=None):
                n = halves[w]
                rows = n if rows is None else rows
                if w == 0:
                    return acc.at[pl.ds(pl.multiple_of(k * shard + cc * n + r0, 8), rows), :]
                return stacks[w - 1].at[k, pl.ds(pl.multiple_of(cc * n + r0, 8), rows), :]

            def swap_other(w):
                theirs = stacks[w - 1].at[:, pl.ds(pl.multiple_of((1 - c) * halves[w], 8), halves[w]), :]
                return _remote(theirs, ras[w], s_sem.at[N_CHIPS - 1 + w], r_sem.at[N_CHIPS - 1 + w], sib)

            def swap_win(k):
                return _remote(narrow.at[k], ras[0].at[k], s_sem.at[k], r_sem.at[k], sib)
        else:
            acc = rest[0]

        @pl.when(pl.program_id(0) == 0)
        def _():
            acc[...] = jnp.zeros_like(acc)
            for w in range(1, ns):
                swap_other(w).start()

        _, xh = _rms(x_ref[...])
        h = (xh * g_ref[...]).astype(BF16)
        acc[0:W, :] += _tn(dq_ref[...], h)
        acc[W:2 * W, :] += _tn(dk_ref[...], h)
        acc[2 * W:3 * W, :] += _tn(dv_ref[...], h)
        acc[QKV_COLS:IN_COLS, :] += _tn(dr_ref[...], h)

        if fused:
            @pl.when(pl.program_id(0) == nt - 1)
            def _():
                for k in range(N_CHIPS):
                    def to_bf16(i, carry, k=k):
                        r0 = pl.multiple_of(i * row_block, row_block)
                        narrow[k, pl.ds(r0, row_block), :] = part(0, k, 1 - c, r0, row_block)[...].astype(BF16)
                        return carry
                    lax.fori_loop(0, halves[0] // row_block, to_bf16, 0)
                    swap_win(k).start()
                def chip_sum(w, k, r0):
                    blk = pl.ds(r0, row_block)
                    return part(w, k, c, r0, row_block)[...] + ras[w][k, blk, :].astype(F32)

                for w in range(1, ns):
                    swap_other(w).wait_recv()

                    def sums(i, carry, w=w):
                        r0 = pl.multiple_of(i * row_block, row_block)
                        for m in range(3):
                            sends[w][m, pl.ds(r0, row_block), :] = chip_sum(w, peer_chip[m], r0).astype(BF16)
                        owns[w][pl.ds(r0, row_block), :] = chip_sum(w, chip, r0)
                        return carry
                    lax.fori_loop(0, halves[w] // row_block, sums, 0)
                for k in range(N_CHIPS):
                    swap_win(k).wait_recv()

                    @pl.when(chip == k)
                    def _(k=k):
                        def own(i, carry):
                            r0 = pl.multiple_of(i * row_block, row_block)
                            owns[0][pl.ds(r0, row_block), :] = chip_sum(0, k, r0)
                            return carry
                        lax.fori_loop(0, halves[0] // row_block, own, 0)

                    @pl.when(chip != k)
                    def _(k=k):
                        def other(i, carry):
                            r0 = pl.multiple_of(i * row_block, row_block)
                            sends[0][(k ^ chip) - 1, pl.ds(r0, row_block), :] = chip_sum(0, k, r0).astype(BF16)
                            return carry
                        lax.fori_loop(0, halves[0] // row_block, other, 0)
                for k in range(N_CHIPS):
                    swap_win(k).wait_send()
                for w in range(1, ns):
                    swap_other(w).wait_send()

    tile = lambda w: pl.BlockSpec((tm, w), lambda i: (i, 0))
    vmem = pl.BlockSpec(memory_space=pltpu.VMEM)
    in_specs = [tile(W), tile(W), tile(W), tile(REST_COLS), tile(D_MODEL), pl.BlockSpec((1, D_MODEL), lambda i: (0, 0))]
    if not fused:
        return pl.pallas_call(
            body, grid=(nt,), in_specs=in_specs,
            out_specs=pl.BlockSpec((IN_COLS, D_MODEL), lambda i: (0, 0)),
            out_shape=jax.ShapeDtypeStruct((IN_COLS, D_MODEL), F32),
            compiler_params=_params(("arbitrary",)), name="inproj_bwd_dw")(dq, dk, dv, drest, x2d, g_norm)
    outs = pl.pallas_call(
        body, grid=(nt,), in_specs=in_specs + [vmem] * (ns - 1), out_specs=[vmem] * (2 * ns),
        out_shape=[jax.ShapeDtypeStruct((3, n, cl), BF16) for n, cl in zip(halves, cols)]
        + [jax.ShapeDtypeStruct((n, cl), F32) for n, cl in zip(halves, cols)],
        scratch_shapes=[pltpu.VMEM((IN_COLS, D_MODEL), F32)]
        + [pltpu.VMEM((N_CHIPS, n, cl), BF16 if w == 0 else F32) for w, (n, cl) in enumerate(zip(halves, cols))]
        + [pltpu.VMEM((N_CHIPS, halves[0], D_MODEL), BF16)]
        + [pltpu.SemaphoreType.DMA((N_CHIPS - 1 + ns,)), pltpu.SemaphoreType.DMA((N_CHIPS - 1 + ns,))],
        compiler_params=_params(("arbitrary",)), name="inproj_bwd_dw_reduce")(
            dq, dk, dv, drest, x2d, g_norm, *others)
    return outs[:ns], outs[ns:]


def _adamw_update(w, g, m, v):
    nm = ADAM_B1 * m + (1.0 - ADAM_B1) * g
    nv = ADAM_B2 * v + (1.0 - ADAM_B2) * (g * g)
    m_hat = nm / (1.0 - ADAM_B1 ** ADAM_STEP)
    v_hat = nv / (1.0 - ADAM_B2 ** ADAM_STEP)
    return -ADAM_LR * (m_hat / (jnp.sqrt(v_hat) + ADAM_EPS) + ADAM_WD * w), nm, nv


def _adamw(w, g, m, v, name):
    R, C = w.shape
    br = max(r for r in range(8, 257, 8) if R % r == 0)

    def body(w_ref, g_ref, m_ref, v_ref, g_out, d_ref, nm_ref, nv_ref):
        g = g_ref[...]
        g_out[...] = g
        d_ref[...], nm_ref[...], nv_ref[...] = _adamw_update(w_ref[...], g, m_ref[...], v_ref[...])

    spec = pl.BlockSpec((br, C), lambda i: (i, 0))
    return pl.pallas_call(
        body, grid=(R // br,), in_specs=[spec] * 4, out_specs=[spec] * 4,
        out_shape=[jax.ShapeDtypeStruct((R, C), F32)] * 4,
        compiler_params=_params(("arbitrary",)), name=name)(w, g, m, v)


def _adamw_rest(g_packed, ws, ms, vs, whole):
    n = len(ws)
    nw = len(whole)
    row_block = 64

    def body(*refs):
        g_ref = refs[0]
        w_refs, m_refs, v_refs = refs[1:1 + n], refs[1 + n:1 + 2 * n], refs[1 + 2 * n:1 + 3 * n]
        whole_in = [refs[1 + 3 * n + 4 * i:5 + 3 * n + 4 * i] for i in range(nw)]
        outs = refs[1 + 3 * n + 4 * nw:]
        off = 0
        for i, (_, used, padded) in enumerate(_SMALL_PARTS[:n]):
            g = g_ref[off:off + used, :]
            delta, nm, nv = _adamw_update(w_refs[i][...], g, m_refs[i][...], v_refs[i][...])
            outs[4 * i][...], outs[4 * i + 1][...], outs[4 * i + 2][...], outs[4 * i + 3][...] = g, delta, nm, nv
            off += padded
        for i, (w_ref, gw_ref, m_ref, v_ref) in enumerate(whole_in):
            for r0 in range(0, w_ref.shape[0], row_block):
                blk = pl.ds(r0, row_block)
                g = gw_ref[blk, :]
                new = _adamw_update(w_ref[blk, :], g, m_ref[blk, :], v_ref[blk, :])
                for ref, val in zip(outs[4 * (n + i):4 * (n + i) + 4], (g,) + new):
                    ref[blk, :] = val
        outs[4 * (n + nw)][...] = g_ref[_LOSS_ROW:_LOSS_ROW + 1, 0:1]

    outs = pl.pallas_call(
        body, out_shape=[jax.ShapeDtypeStruct(w.shape, F32) for w in ws for _ in range(4)]
        + [jax.ShapeDtypeStruct(four[0].shape, F32) for four in whole for _ in range(4)]
        + [jax.ShapeDtypeStruct((1, 1), F32)],
        compiler_params=_params(), name="adamw_rest")(g_packed, *ws, *ms, *vs, *[a for four in whole for a in four])
    return ([outs[4 * i:4 * i + 4] for i in range(n)], [outs[4 * (n + i):4 * (n + i) + 4] for i in range(nw)],
            outs[4 * (n + nw)])


def _place():
    x, y, c = lax.axis_index("x"), lax.axis_index("y"), lax.axis_index("c")
    chip = 2 * x + y
    peers = [(x, 1 - y), (1 - x, y), (1 - x, 1 - y)]
    peer_chip = [2 * px + py for px, py in peers]
    return x, y, c, chip, peers, peer_chip


def _remote(src, dst, send_sem, recv_sem, dev):
    return pltpu.make_async_remote_copy(src_ref=src, dst_ref=dst, send_sem=send_sem, recv_sem=recv_sem,
                                        device_id=dev, device_id_type=MESH)


def _ag_weights(weights, late=()):
    nw, nl = len(weights), len(late)
    pieces = 2

    def body(*refs):
        srcs, late_srcs = refs[:nw], refs[nw:nw + nl]
        outs, late_bf, late_land = (refs[nw + nl:2 * nw + nl], refs[2 * nw + nl:2 * nw + 2 * nl],
                                    refs[2 * nw + 2 * nl:2 * nw + 3 * nl])
        s_ici, r_ici, s_d2d, r_d2d = refs[2 * nw + 3 * nl:]
        x, y, c = lax.axis_index("x"), lax.axis_index("y"), lax.axis_index("c")
        chip = 2 * x + y
        sib = (x, y, 1 - c)
        first = ((x + 1 - c) % 2, (y + c) % 2)
        second = ((x + c) % 2, (y + 1 - c) % 2)
        first_chip, second_chip = 2 * first[0] + first[1], 2 * second[0] + second[1]
        diag_chip = 3 - chip
        for src, out in zip(srcs, outs):
            out[chip] = src[...].astype(BF16)

        parts = [(w, out, pc) for w, out in enumerate(outs) for pc in range(pieces)]

        def piece(out, k, cc, pc):
            rows = out.shape[1] // 2 // pieces
            return out.at[k, pl.ds(pl.multiple_of((cc * pieces + pc) * rows, 16), rows), :]

        def ici(w, slot, out, k, dev, pc):
            blk, sem = piece(out, k, c, pc), (nw * slot + w) * pieces + pc
            return _remote(blk, blk, s_ici.at[sem], r_ici.at[sem], (dev[0], dev[1], c))

        def d2d(w, slot, out, k, cc, pc):
            blk, sem = piece(out, k, cc, pc), (nw * slot + w) * pieces + pc
            return _remote(blk, blk, s_d2d.at[sem], r_d2d.at[sem], sib)

        sent = []
        for slot, dev in enumerate((first, second)):
            for w, out, pc in parts:
                sent.append(ici(w, slot, out, chip, dev, pc))
                sent[-1].start()
        for src, bf, land in zip(late_srcs, late_bf, late_land):
            bf[...] = src[...].astype(BF16)
            land[...] = jnp.zeros_like(land)
            land[chip] = bf[...]
        for slot, k, dev in ((0, first_chip, first), (1, second_chip, second), (2, diag_chip, second)):
            for w, out, pc in parts:
                ici(w, slot, out, k, dev, pc).wait_recv()
                if slot == 0:
                    sent.append(ici(w, 2, out, k, second, pc))
                    sent[-1].start()
                sent.append(d2d(w, slot, out, k, c, pc))
                sent[-1].start()
        for slot, k in ((0, second_chip), (1, first_chip), (2, diag_chip)):
            for w, out, pc in parts:
                d2d(w, slot, out, k, 1 - c, pc).wait_recv()
        for cp in sent:
            cp.wait_send()

    vmem = pl.BlockSpec(memory_space=pltpu.VMEM)
    outs = pl.pallas_call(
        body,
        out_shape=[jax.ShapeDtypeStruct((N_CHIPS,) + w.shape, BF16) for w in weights]
        + [jax.ShapeDtypeStruct(w.shape, BF16) for w in late]
        + [jax.ShapeDtypeStruct((N_CHIPS,) + w.shape, BF16) for w in late],
        in_specs=[vmem] * (nw + nl), out_specs=[vmem] * (nw + 2 * nl),
        scratch_shapes=[pltpu.SemaphoreType.DMA((3 * nw * pieces,))] * 4,
        compiler_params=pltpu.CompilerParams(vmem_limit_bytes=VMEM_LIMIT), name="ag_weights")(*weights, *late)
    return outs[:nw], outs[nw:nw + nl], outs[nw + nl:]


_HBM = pl.BlockSpec(memory_space=pltpu.HBM)
_SEM = pl.BlockSpec(memory_space=pltpu.SEMAPHORE)
_ANY = pl.BlockSpec(memory_space=pl.ANY)
_DATAFLOW = pltpu.SideEffectType.DATAFLOW_SIDE_EFFECTING


def _in_hbm(a):
    return pltpu.with_memory_space_constraint(a, pltpu.HBM)


_PEERS_OF = {"gather": 3, "scatter": 3, "direct": 7}


def _exchange_copies(mode, srcs, lands, send_sems, recv_sems):
    nw = len(srcs)
    x, y, c, chip, peers, peer_chip = _place()
    pairs = []
    if mode == "direct":
        targets = [((x, y), chip, 1)] + [(p, k, d) for p, k in zip(peers, peer_chip) for d in (0, 1)]
        for r, ((px, py), k, d) in enumerate(targets):
            for w in range(nw):
                sems = (send_sems.at[nw * r + w], recv_sems.at[nw * r + w], (px, py, (c + d) % 2))
                pairs.append((_remote(srcs[w].at[k], lands[w].at[r], *sems),) * 2)
        return pairs
    gather = mode == "gather"
    for m, (px, py) in enumerate(peers):
        for w in range(nw):
            sems = (send_sems.at[nw * m + w], recv_sems.at[nw * m + w], (px, py, c))
            if gather:
                pairs.append((_remote(srcs[w], lands[w].at[chip], *sems),
                              _remote(srcs[w], lands[w].at[peer_chip[m]], *sems)))
            else:
                pairs.append((_remote(srcs[w].at[m], lands[w].at[m], *sems),) * 2)
    return pairs


def _exchange_start(mode, srcs, after, name, lands=None):
    nw = len(srcs)
    n_copies = _PEERS_OF[mode] * nw

    after = tuple(after)

    def body(*refs):
        send_sems, recv_sems = refs[2 * nw + len(after)], refs[2 * nw + len(after) + 1]
        for start, _ in _exchange_copies(mode, refs[:nw], refs[nw:2 * nw], send_sems, recv_sems):
            start.start()
        refs[-1][...] = jnp.zeros_like(refs[-1])

    if lands is None:
        shape = {"gather": lambda s: (N_CHIPS,) + s.shape, "scatter": lambda s: s.shape,
                 "direct": lambda s: (_PEERS_OF["direct"],) + s.shape[1:]}[mode]
        lands = [lax.empty(shape(s), s.dtype) for s in srcs]
    lands = [_in_hbm(l) for l in lands]
    return pl.pallas_call(
        body, name=name,
        out_shape=(pltpu.SemaphoreType.DMA((n_copies,)), pltpu.SemaphoreType.DMA((n_copies,)))
        + tuple(pltpu.HBM(s.shape, s.dtype) for s in srcs)
        + tuple(pltpu.HBM(l.shape, l.dtype) for l in lands)
        + (jax.ShapeDtypeStruct((8, LANES), F32),),
        in_specs=[_HBM] * (2 * nw) + [_ANY] * len(after),
        out_specs=(_SEM, _SEM) + (_HBM,) * (2 * nw) + (pl.BlockSpec(memory_space=pltpu.VMEM),),
        input_output_aliases={i: 2 + i for i in range(2 * nw)},
        compiler_params=pltpu.CompilerParams(has_side_effects=_DATAFLOW),
    )(*[_in_hbm(s) for s in srcs], *lands, *after)


def _exchange_wait(mode, started, after, name):
    nw = (len(started) - 3) // 2
    send_sems, recv_sems = started[0], started[1]
    thru = started[2:2 + 2 * nw]

    def body(*refs):
        for _, arrival in _exchange_copies(mode, refs[:nw], refs[nw:2 * nw], refs[2 * nw], refs[2 * nw + 1]):
            arrival.wait_send()
            arrival.wait_recv()

    outs = pl.pallas_call(
        body, name=name,
        out_shape=tuple(pltpu.HBM(t.shape, t.dtype) for t in thru),
        in_specs=[_HBM] * (2 * nw) + [_SEM, _SEM, _ANY], out_specs=(_HBM,) * (2 * nw),
        input_output_aliases={i: i for i in range(2 * nw)},
        compiler_params=pltpu.CompilerParams(has_side_effects=_DATAFLOW),
    )(*thru, send_sems, recv_sems, after)
    return outs[:nw], outs[nw:]


def _reduce_last(owns, landed, g_small, direct_srcs=(), direct_landed=()):
    ns, nd = len(owns), len(direct_srcs)
    halves = [o.shape[0] for o in owns]
    row_block = 16
    hs = SMALL_ROWS // 2
    jobs = [(w, p * (halves[w] // 2), halves[w] // 2) for w in range(ns) for p in range(2)]
    n_swaps = len(jobs)
    jobs += [(ns + d, 0, direct_srcs[d].shape[1]) for d in range(nd)]

    def body(*refs):
        own_refs, land_refs, gsm_ref = refs[:ns], refs[ns:2 * ns], refs[2 * ns]
        dsrc_refs, dland_refs = refs[2 * ns + 1:2 * ns + 1 + nd], refs[2 * ns + 1 + nd:2 * ns + 1 + 2 * nd]
        n_in = 2 * ns + 1 + 2 * nd
        out_refs, osm_ref = refs[n_in:n_in + ns + nd], refs[n_in + ns + nd]
        scr = refs[n_in + ns + nd + 1:]
        own_scr, land_scr, dsrc_scr, dland_scr = (scr[:ns], scr[ns:2 * ns], scr[2 * ns:2 * ns + nd],
                                                  scr[2 * ns + nd:2 * ns + 2 * nd])
        ra_sm, p_sm, in_sem, s_sem, r_sem, sm_s, sm_r = scr[2 * ns + 2 * nd:]
        x, y, c, chip, peers, peer_chip = _place()
        sib = (x, y, 1 - c)
        half = lambda cc: pl.ds(pl.multiple_of(cc * hs, 8), hs)
        sm_a = _remote(gsm_ref.at[half(1 - c), :], ra_sm, sm_s.at[0], sm_r.at[0], sib)
        sm_a.start()
        swaps = [sm_a]

        def reads(j):
            w, r0, n = jobs[j]
            rows = pl.ds(r0, n)
            if w < ns:
                pairs = [(own_refs[w].at[rows, :], own_scr[w].at[rows, :]),
                         (land_refs[w].at[:, rows, :], land_scr[w].at[:, rows, :])]
            else:
                pairs = [(dsrc_refs[w - ns].at[chip], dsrc_scr[w - ns]), (dland_refs[w - ns], dland_scr[w - ns])]
            return [pltpu.make_async_copy(s, d, in_sem.at[2 * j + i]) for i, (s, d) in enumerate(pairs)]

        for j in range(len(jobs)):
            for cp in reads(j):
                cp.start()
        sm_a.wait_recv()
        p_sm[chip] = gsm_ref[half(c), :] + ra_sm[...]
        for m, (px, py) in enumerate(peers):
            swaps.append(_remote(p_sm.at[chip], p_sm.at[chip], sm_s.at[1 + m], sm_r.at[1 + m], (px, py, c)))
            swaps[-1].start()

        def mine_of(j):
            w, r0, n = jobs[j]
            return out_refs[w].at[pl.ds(pl.multiple_of(c * halves[w] + r0, 8), n), :]

        for j, (w, r0, n) in enumerate(jobs):
            for cp in reads(j):
                cp.wait()

            def total(i, carry, w=w, r0=r0):
                rr = pl.multiple_of(r0 + i * row_block, row_block)
                blk = pl.ds(rr, row_block)
                if w < ns:
                    acc = own_scr[w][blk, :]
                    for m in range(_PEERS_OF["scatter"]):
                        acc = acc + land_scr[w][m, blk, :].astype(F32)
                    out_refs[w][pl.ds(pl.multiple_of(c * halves[w] + rr, row_block), row_block), :] = acc
                else:
                    theirs = lambda r: dland_scr[w - ns][r, blk, :].astype(F32)
                    acc = dsrc_scr[w - ns][blk, :].astype(F32) + theirs(0)
                    for m in range(N_CHIPS - 1):
                        acc = acc + (theirs(1 + 2 * m) + theirs(2 + 2 * m))
                    out_refs[w][blk, :] = acc
                return carry
            lax.fori_loop(0, n // row_block, total, 0)
            if w < ns:
                swaps.append(_remote(mine_of(j), mine_of(j), s_sem.at[j], r_sem.at[j], sib))
                swaps[-1].start()
        for m, (px, py) in enumerate(peers):
            _remote(p_sm.at[chip], p_sm.at[peer_chip[m]], sm_s.at[1 + m], sm_r.at[1 + m], (px, py, c)).wait_recv()
        osm_ref[half(c), :] = (p_sm[0] + p_sm[1]) + (p_sm[2] + p_sm[3])
        swaps.append(_remote(osm_ref.at[half(c), :], osm_ref.at[half(c), :], sm_s.at[4], sm_r.at[4], sib))
        swaps[-1].start()
        for j, (w, r0, n) in enumerate(jobs[:n_swaps]):
            theirs = out_refs[w].at[pl.ds(pl.multiple_of((1 - c) * halves[w] + r0, 8), n), :]
            _remote(theirs, theirs, s_sem.at[j], r_sem.at[j], sib).wait_recv()
        _remote(osm_ref.at[half(1 - c), :], osm_ref.at[half(1 - c), :], sm_s.at[4], sm_r.at[4], sib).wait_recv()
        for cp in swaps:
            cp.wait_send()

    vmem = pl.BlockSpec(memory_space=pltpu.VMEM)
    far = [pl.BlockSpec(memory_space=pl.ANY)]
    return pl.pallas_call(
        body, out_shape=[jax.ShapeDtypeStruct((2 * o.shape[0], o.shape[1]), F32) for o in owns]
        + [jax.ShapeDtypeStruct(s.shape[1:], F32) for s in direct_srcs]
        + [jax.ShapeDtypeStruct((SMALL_ROWS, LANES), F32)],
        in_specs=far * (2 * ns) + [vmem] + far * (2 * nd), out_specs=[vmem] * (ns + nd + 1),
        scratch_shapes=[pltpu.VMEM(o.shape, o.dtype) for o in owns] + [pltpu.VMEM(l.shape, l.dtype) for l in landed]
        + [pltpu.VMEM(s.shape[1:], s.dtype) for s in direct_srcs]
        + [pltpu.VMEM(l.shape, l.dtype) for l in direct_landed]
        + [pltpu.VMEM((hs, LANES), F32), pltpu.VMEM((N_CHIPS, hs, LANES), F32),
           pltpu.SemaphoreType.DMA((2 * len(jobs),)),
           pltpu.SemaphoreType.DMA((n_swaps,)), pltpu.SemaphoreType.DMA((n_swaps,)),
           pltpu.SemaphoreType.DMA((5,)), pltpu.SemaphoreType.DMA((5,))],
        compiler_params=pltpu.CompilerParams(vmem_limit_bytes=VMEM_LIMIT),
        name="reduce_last")(*owns, *landed, g_small, *direct_srcs, *direct_landed)


_SMALL_PARTS = (("g_norm", 8, 8), ("w_s", 512, 512), ("b_s", 4, 8), ("g_v", 2, 8), ("g_mem", 8, 8),
                ("g_final", 8, 8), ("loss", 8, 8))
_LOSS_ROW = SMALL_ROWS - 8
assert sum(p for _, _, p in _SMALL_PARTS) == SMALL_ROWS


def _pack_small(parts, loss_block):
    rows = []
    for (name, used, padded), p in zip(_SMALL_PARTS, list(parts) + [loss_block]):
        p = p.reshape(used, LANES)
        if padded > used:
            p = jnp.pad(p, ((0, padded - used), (0, 0)))
        rows.append(p)
    return jnp.concatenate(rows, axis=0)


def _local_step(x, mem, target, g_norm, w_in, w_s, b_s, g_v, g_mem, late_weights, g_final,
                fwd_token=None, on_late=None, on_dw=None):
    B, S, _ = x.shape
    x2d = x.reshape(B * S, D_MODEL)
    t2d = target.reshape(B * S, D_MODEL)
    mem2d = mem.reshape(B * N_MEM, D_MODEL)

    proj = _inproj_fwd(x2d, g_norm, w_in, after=() if fwd_token is None else (fwd_token,))
    w_kv, w_out = late_weights(proj)
    kv = _kv_fwd(mem2d, g_mem, w_kv)
    a, lse = _attn_fwd(proj, B, S)
    w_sT = jnp.swapaxes(w_s, 1, 2)
    b_tab = jnp.repeat(b_s.T, HEAD_DIM, axis=1)
    (dx2, da, drest, loss, d_wout, d_ws, d_bs, d_gv, d_gf, dkv) = _mid(
        x2d, t2d, a, proj, kv, w_s, w_sT, b_tab, g_v, w_out, g_final, B, S)
    d_wkv, d_gmem = _kv_bwd(mem2d, g_mem, w_kv, dkv)
    dq, dk, dv = _attn_bwd(proj, a, lse, da, B, S, after=() if on_late is None else (on_late(d_wkv, d_wout),))
    if on_dw is None:
        d_win = _inproj_bwd_dw(dq, dk, dv, drest, x2d, g_norm)
        after = ()
    else:
        d_win = None
        after = (on_dw(*_inproj_bwd_dw(dq, dk, dv, drest, x2d, g_norm, reduce_with=[])),)
    grad_x, d_gnorm = _inproj_bwd_dx(dq, dk, dv, drest, x2d, dx2, g_norm, w_in, after=after)
    d_bs = d_bs[:, :N_SGU_GROUPS].T
    return (loss, grad_x.reshape(B, S, D_MODEL),
            dict(g_norm=d_gnorm, w_in=d_win, w_s=d_ws, b_s=d_bs, g_v=d_gv, g_mem=d_gmem, w_kv=d_wkv,
                 w_out=d_wout, g_final=d_gf))


def kernel(x, mem, g_norm, w_in, w_sgu_spatial, b_sgu_spatial, g_sgu_v, g_mem, w_mem_kv, w_out, g_final, loss_target, m_g_norm, m_w_in, m_w_sgu_spatial, m_b_sgu_spatial, m_g_sgu_v, m_g_mem, m_w_mem_kv, m_w_out, m_g_final, v_g_norm, v_w_in, v_w_sgu_spatial, v_b_sgu_spatial, v_g_sgu_v, v_g_mem, v_w_mem_kv, v_w_out, v_g_final):
    t = lambda w: jnp.swapaxes(w[0], 0, 1)
    (win_all,), late_shards, late_lands = _ag_weights([t(w_in)], [w_mem_kv[0], w_out[0]])
    w_in_full = win_all.reshape(-1, win_all.shape[-1])
    late = _exchange_start("gather", list(late_shards), (win_all,), "gather_late_start", lands=late_lands)

    def late_weights(proj):
        return [z.reshape(-1, z.shape[-1]) for z in _exchange_wait("gather", late, proj, "gather_late_wait")[1]]

    scatter = {}

    def on_late(d_wkv, d_wout):
        scatter["late"] = _exchange_start("direct", [d_wkv, d_wout], (), "scatter_late_start")
        return scatter["late"][-1]

    def on_dw(sends, owns):
        scatter["own"] = owns
        scatter["started"] = _exchange_start("scatter", list(sends), (owns[0],), "scatter_start")
        return scatter["started"][-1]

    loss, grad_x, g = _local_step(
        x, mem, loss_target, g_norm, w_in_full, w_sgu_spatial[0], b_sgu_spatial[0], g_sgu_v, g_mem,
        late_weights, g_final.reshape(1, D_MODEL), fwd_token=late[-1], on_late=on_late, on_dw=on_dw)

    small_names = ("g_norm", "w_s", "b_s", "g_v", "g_mem", "g_final")
    g_small = _pack_small([g[n] for n in small_names], loss)
    late_srcs, late_landed = _exchange_wait("direct", scatter["late"], g_small, "scatter_late_wait")
    _, landed = _exchange_wait("scatter", scatter["started"], late_landed[0], "scatter_wait")
    gr_in, gr_kv, gr_out, gr_small = _reduce_last(scatter["own"], landed, g_small, late_srcs, late_landed)

    small_w = (g_norm, w_sgu_spatial, b_sgu_spatial, g_sgu_v, g_mem, g_final)
    small_m = (m_g_norm, m_w_sgu_spatial, m_b_sgu_spatial, m_g_sgu_v, m_g_mem, m_g_final)
    small_v = (v_g_norm, v_w_sgu_spatial, v_b_sgu_spatial, v_g_sgu_v, v_g_mem, v_g_final)
    rows = lambda ws: [w.reshape(-1, LANES) for w in ws]
    small_new, ((gr_kv, d_kv, nm_kv, nv_kv), (gr_out, d_out, nm_out, nv_out)), loss = _adamw_rest(
        gr_small, rows(small_w), rows(small_m), rows(small_v),
        [(w_mem_kv[0], gr_kv, m_w_mem_kv[0], v_w_mem_kv[0]), (w_out[0], gr_out, m_w_out[0], v_w_out[0])])
    loss = loss.reshape(())
    small = [[z.reshape(w.shape) for z in four] for w, four in zip(small_w, small_new)]
    gr_in, d_in, nm_in, nv_in = [jnp.swapaxes(z, 0, 1)
                                 for z in _adamw(t(w_in), gr_in, t(m_w_in), t(v_w_in), "adamw_w_in")]

    def leaves(kind, big_in, big_kv, big_out):
        s_norm, s_ws, s_bs, s_gv, s_gmem, s_gf = [four[kind] for four in small]
        return [s_norm, big_in[None], s_ws, s_bs, s_gv, s_gmem, big_kv[None], big_out[None], s_gf]

    return (loss, grad_x, *leaves(0, gr_in, gr_kv, gr_out), *leaves(1, d_in, d_kv, d_out),
            *leaves(2, nm_in, nm_kv, nm_out), *leaves(3, nv_in, nv_kv, nv_out))
```

```python
import functools

import jax
import jax.numpy as jnp
from jax import lax
from jax.experimental import pallas as pl
from jax.experimental.pallas import tpu as pltpu

F32 = jnp.float32
BF16 = jnp.bfloat16
MESH = pl.DeviceIdType.MESH

D_MODEL = 1024
ATTN_WIDTH = 512
SGU_WIDTH = 256
MEM_WIDTH = 256
N_MEM = 256
IN_COLS = 3328
QKV_COLS = 3 * ATTN_WIDTH
REST_COLS = IN_COLS - QKV_COLS
SGU_CHUNK = 128
N_SGU_GROUPS = 4
EPS = 1e-6
NEG_INF = -1e30
DILATIONS = (1, 4, 16)
RADIUS = 64
Q_BLOCK = 128
LANES = 128
HEAD_DIM = 64

ADAM_LR = 0.001
ADAM_B1 = 0.9
ADAM_B2 = 0.999
ADAM_EPS = 1e-08
ADAM_WD = 0.01
ADAM_STEP = 10

N_CHIPS = 4
VMEM_LIMIT = 56 * 1024 * 1024
SMALL_ROWS = 560


def _params(sem=None, vmem=VMEM_LIMIT):
    return pltpu.CompilerParams(dimension_semantics=sem, vmem_limit_bytes=vmem)


def _nn(a, b):
    return jnp.dot(a, b, preferred_element_type=F32)


def _nt(a, b):
    return lax.dot_general(a, b, (((1,), (1,)), ((), ())), preferred_element_type=F32)


def _tn(a, b):
    return lax.dot_general(a, b, (((0,), (0,)), ((), ())), preferred_element_type=F32)


def _rms(x):
    r = lax.rsqrt(jnp.mean(x * x, axis=-1, keepdims=True) + EPS)
    return r, x * r


def _head_masks():
    lane = lax.broadcasted_iota(jnp.int32, (1, LANES), 1)
    lo = lane < HEAD_DIM
    return lo, (lo.astype(F32), (~lo).astype(F32))


def _silu_parts(z):
    s = jax.nn.sigmoid(z)
    return z * s, s * (1.0 + z * (1.0 - s))


def _gelu_parts(x):
    c = 0.7978845608028654
    x2 = x * x
    s = jax.nn.sigmoid((2.0 * c) * (x + 0.044715 * (x * x2)))
    return x * s, s * (1.0 + x * (1.0 - s) * ((2.0 * c) * (1.0 + 3.0 * 0.044715 * x2)))


def _after(tokens):
    return [pl.BlockSpec(memory_space=pl.ANY)] * len(tokens)


def _inproj_fwd(x2d, g_norm, w_in_t, after=()):
    T = x2d.shape[0]
    tm = 512

    def body(x_ref, g_ref, w_ref, *rest):
        o_ref = rest[-1]
        _, xh = _rms(x_ref[...])
        h = (xh * g_ref[...]).astype(BF16)
        o_ref[...] = _nt(h, w_ref[...])

    return pl.pallas_call(
        body, grid=(T // tm,),
        in_specs=[pl.BlockSpec((tm, D_MODEL), lambda i: (i, 0)),
                  pl.BlockSpec((1, D_MODEL), lambda i: (0, 0)),
                  pl.BlockSpec((IN_COLS, D_MODEL), lambda i: (0, 0))] + _after(after),
        out_specs=pl.BlockSpec((tm, IN_COLS), lambda i: (i, 0)),
        out_shape=jax.ShapeDtypeStruct((T, IN_COLS), F32),
        compiler_params=_params(("arbitrary",)), name="inproj_fwd")(x2d, g_norm, w_in_t, *after)


def _kv_fwd(mem2d, g_mem, w_kv):
    Tm = mem2d.shape[0]

    def body(m_ref, g_ref, w_ref, o_ref):
        _, mh = _rms(m_ref[...])
        o_ref[...] = _nn((mh * g_ref[...]).astype(BF16), w_ref[...])

    return pl.pallas_call(
        body, out_shape=jax.ShapeDtypeStruct((Tm, 2 * MEM_WIDTH), F32),
        compiler_params=_params(), name="kv_fwd")(mem2d, g_mem, w_kv)


def _kv_bwd(mem2d, g_mem, w_kv, dkv):
    Tm = mem2d.shape[0]

    def body(m_ref, g_ref, w_ref, dkv_ref, dw_ref, dg_ref):
        _, mh = _rms(m_ref[...])
        memn = (mh * g_ref[...]).astype(BF16)
        dkvb = dkv_ref[...].astype(BF16)
        dw = _tn(memn, dkvb).astype(BF16)
        for k in range(N_CHIPS):
            dw_ref[k] = dw[k * (D_MODEL // N_CHIPS):(k + 1) * (D_MODEL // N_CHIPS), :]
        dmemn = _nt(dkvb, w_ref[...])
        dg_ref[...] = jnp.sum(dmemn * mh, axis=0, keepdims=True)

    return pl.pallas_call(
        body, out_shape=(jax.ShapeDtypeStruct((N_CHIPS, D_MODEL // N_CHIPS, 2 * MEM_WIDTH), BF16),
                         jax.ShapeDtypeStruct((1, D_MODEL), F32)),
        compiler_params=_params(), name="kv_bwd")(mem2d, g_mem, w_kv, dkv)


def _attn_geometry(S):
    geom = []
    for d in DILATIONS:
        L = S // d
        assert L % Q_BLOCK == 0
        geom.append((d, L, min(2 * Q_BLOCK, L), L // Q_BLOCK))
    return geom


def _init_bias(bias_scr, geom, hp):
    row = lax.broadcasted_iota(jnp.int32, (Q_BLOCK, 2 * Q_BLOCK), 0)
    col = lax.broadcasted_iota(jnp.int32, (Q_BLOCK, 2 * Q_BLOCK), 1)
    for j in (0, 1):
        bits = (126 - (2 * hp + j)) * (1 << 23)
        slope = lax.bitcast_convert_type(jnp.full((1, 1), bits, jnp.int32), F32)
        for di, (d, _, _, _) in enumerate(geom):
            for cls, off in enumerate((0, -RADIUS, -2 * RADIUS)):
                dist = jnp.abs(col - row + off)
                bias_scr[di * 6 + cls * 2 + j] = jnp.where(
                    dist <= RADIUS, -(slope * float(d)) * dist.astype(F32), NEG_INF)


SPLIT = 4
COPY_ROWS = 256


def _by4_rows(S, step):
    per_class = S // SPLIT // COPY_ROWS
    r, j = step // per_class, step % per_class
    return (pl.ds(r + SPLIT * j * COPY_ROWS, COPY_ROWS, stride=SPLIT),
            pl.ds(r * (S // SPLIT) + j * COPY_ROWS, COPY_ROWS))


def _to_by4(src, dst, S):
    for i in range(S // COPY_ROWS):
        natural, by4 = _by4_rows(S, i)
        dst[by4, :] = src[natural, :]


def _block_slices(d, L, KW, nqb, r, qb, S):
    qs = qb * Q_BLOCK
    ks = jnp.clip(qs - RADIUS, 0, L - KW)
    cls = jnp.where(qb == 0, 0, jnp.where(qb == nqb - 1, 2, 1))
    if d == 1:
        qsl = pl.ds(pl.multiple_of(qs, Q_BLOCK), Q_BLOCK)
        ksl = pl.ds(pl.multiple_of(ks, RADIUS), KW)
    elif d == SPLIT:
        qsl = pl.ds(pl.multiple_of(r * L + qs, Q_BLOCK), Q_BLOCK)
        ksl = pl.ds(pl.multiple_of(r * L + ks, RADIUS), KW)
    else:
        sub = d // SPLIT
        base = (r % SPLIT) * (S // SPLIT) + r // SPLIT
        qsl = pl.ds(base + qs * sub, Q_BLOCK, stride=sub)
        ksl = pl.ds(base + ks * sub, KW, stride=sub)
    return qsl, ksl, cls


def _for_groups(geom, S, group, fn):
    for di, (d, L, KW, nqb) in enumerate(geom):
        n = group[di]
        assert (d * nqb) % n == 0

        def step(it, carry, di=di, d=d, L=L, KW=KW, nqb=nqb, n=n):
            slices = []
            for g in range(n):
                i = it * n + g
                slices.append(_block_slices(d, L, KW, nqb, i // nqb, i % nqb, S))
            fn(di, KW, slices)
            return carry
        lax.fori_loop(0, d * nqb // n, step, 0)


def _attn_fwd(proj, B, S):
    T = B * S
    geom = _attn_geometry(S)
    n_pairs = ATTN_WIDTH // LANES

    def body(q_ref, k_ref, v_ref, a_ref, lse_ref, bias_scr, q4, k4, v4, *per_dilation):
        o_scr, m_scr, l_scr = per_dilation[0:3], per_dilation[3:6], per_dilation[6:9]
        lo, hm = _head_masks()
        pair = pl.program_id(0)

        @pl.when(pl.program_id(1) == 0)
        def _():
            _init_bias(bias_scr, geom, pair)
        for src, dst in ((q_ref, q4), (k_ref, k4), (v_ref, v4)):
            _to_by4(src, dst, S)

        def group(di, KW, all_slices):
            run = 8
            for first in range(0, len(all_slices), run):
                some(di, KW, all_slices[first:first + run])

        def some(di, KW, slices):
            chains = [(g, j) for g in range(len(slices)) for j in (0, 1)]
            q_src, k_src, v_src = (q_ref, k_ref, v_ref) if di == 0 else (q4, k4, v4)
            q = [q_src[qsl, :] for qsl, _, _ in slices]
            kw = [k_src[ksl, :].astype(BF16) for _, ksl, _ in slices]
            vw = [v_src[ksl, :].astype(BF16) for _, ksl, _ in slices]
            s = {(g, j): _nt((q[g] * (hm[j] * 0.125)).astype(BF16), kw[g])
                 + bias_scr[di * 6 + slices[g][2] * 2 + j, :, pl.ds(0, KW)] for g, j in chains}
            m = {c: jnp.max(s[c], axis=1, keepdims=True) for c in chains}
            p = {c: jnp.exp(s[c] - m[c]) for c in chains}
            l = {c: jnp.sum(p[c], axis=1, keepdims=True) for c in chains}
            o = {(g, j): _nn(p[(g, j)].astype(BF16), vw[g]) for g, j in chains}
            for g, (qsl, _, _) in enumerate(slices):
                o_scr[di][qsl, :] = jnp.where(lo, o[(g, 0)], o[(g, 1)])
                m_scr[di][qsl, :] = jnp.where(lo, m[(g, 0)], m[(g, 1)])
                l_scr[di][qsl, :] = jnp.where(lo, l[(g, 0)], l[(g, 1)])

        _for_groups(geom, S, (16, 16, 16), group)

        for i in range(S // COPY_ROWS):
            natural, by4 = _by4_rows(S, i)
            rows = [natural, by4, by4]
            ms = [m_scr[di][rows[di], :] for di in range(3)]
            mx = jnp.maximum(jnp.maximum(ms[0], ms[1]), ms[2])
            num = 0.0
            den = 0.0
            for di in range(3):
                w = jnp.exp(ms[di] - mx)
                num = num + w * o_scr[di][rows[di], :]
                den = den + w * l_scr[di][rows[di], :]
            a_ref[natural, :] = num / den
            lse_ref[natural, :] = mx + jnp.log(den)

    blk = lambda off: pl.BlockSpec((S, LANES), lambda h, b, off=off: (b, off + h))
    out_blk = pl.BlockSpec((S, LANES), lambda h, b: (b, h))
    return pl.pallas_call(
        body, grid=(n_pairs, B),
        in_specs=[blk(0), blk(n_pairs), blk(2 * n_pairs)],
        out_specs=[out_blk, out_blk],
        out_shape=[jax.ShapeDtypeStruct((T, ATTN_WIDTH), F32)] * 2,
        scratch_shapes=[pltpu.VMEM((18, Q_BLOCK, 2 * Q_BLOCK), F32)] + [pltpu.VMEM((S, LANES), F32)] * 12,
        compiler_params=_params(("arbitrary", "arbitrary")), name="attn_fwd")(proj, proj, proj)


def _attn_bwd(proj, a, lse, da, B, S, after=()):
    T = B * S
    geom = _attn_geometry(S)
    n_pairs = ATTN_WIDTH // LANES

    def body(q_ref, k_ref, v_ref, a_ref, lse_ref, do_ref, *rest):
        dq_ref, dk_ref, dv_ref, bias_scr = rest[len(after):len(after) + 4]
        scr = rest[len(after) + 4:]
        acc = (scr[0:3], scr[3:6])
        natural_in = (q_ref, k_ref, v_ref, a_ref, lse_ref, do_ref)
        by4_in = scr[6:12]
        _, hm = _head_masks()
        pair = pl.program_id(0)

        @pl.when(pl.program_id(1) == 0)
        def _():
            _init_bias(bias_scr, geom, pair)
        for ref in scr[0:6]:
            ref[...] = jnp.zeros_like(ref)
        for src, dst in zip(natural_in, by4_in):
            _to_by4(src, dst, S)

        def group(di, KW, all_slices):
            run = (4, 4, 8)[di]
            for first in range(0, len(all_slices), run):
                some(di, KW, all_slices[first:first + run])

        def some(di, KW, slices):
            n = len(slices)
            chains = [(g, j) for g in range(n) for j in (0, 1)]
            q_src, k_src, v_src, a_src, lse_src, do_src = natural_in if di == 0 else by4_in
            dq_scr, dk_scr, dv_scr = acc[0 if di == 0 else 1]
            q = [q_src[qsl, :] for qsl, _, _ in slices]
            do = [do_src[qsl, :] for qsl, _, _ in slices]
            doa = [do[g] * a_src[slices[g][0], :] for g in range(n)]
            lse_q = [lse_src[qsl, :] for qsl, _, _ in slices]
            kw = [k_src[ksl, :].astype(BF16) for _, ksl, _ in slices]
            vw = [v_src[ksl, :].astype(BF16) for _, ksl, _ in slices]
            qj = {(g, j): (q[g] * (hm[j] * 0.125)).astype(BF16) for g, j in chains}
            doj = {(g, j): (do[g] * hm[j]).astype(BF16) for g, j in chains}
            s = {(g, j): _nt(qj[(g, j)], kw[g])
                 + bias_scr[di * 6 + slices[g][2] * 2 + j, :, pl.ds(0, KW)] for g, j in chains}
            dp = {(g, j): _nt(doj[(g, j)], vw[g]) for g, j in chains}
            dsum = {(g, j): jnp.sum(doa[g] * hm[j], axis=1, keepdims=True) for g, j in chains}
            p = {(g, j): jnp.exp(s[(g, j)] - lse_q[g][:, HEAD_DIM * j:HEAD_DIM * j + 1]) for g, j in chains}
            ds = {c: (p[c] * (dp[c] - dsum[c])).astype(BF16) for c in chains}
            pb = {c: p[c].astype(BF16) for c in chains}
            dq = [_nn(ds[(g, 0)], kw[g]) * (hm[0] * 0.125) + _nn(ds[(g, 1)], kw[g]) * (hm[1] * 0.125)
                  for g in range(n)]
            both = lambda t, g: jnp.concatenate([t[(g, 0)], t[(g, 1)]], axis=0)
            dkw = [_tn(both(ds, g), both(qj, g)) for g in range(n)]
            dvw = [_tn(both(pb, g), both(doj, g)) for g in range(n)]
            for g, (qsl, ksl, _) in enumerate(slices):
                dq_scr[qsl, :] = dq_scr[qsl, :] + dq[g]
                dk_scr[ksl, :] = dk_scr[ksl, :] + dkw[g]
                dv_scr[ksl, :] = dv_scr[ksl, :] + dvw[g]

        _for_groups(geom, S, (16, 16, 16), group)

        for i in range(S // COPY_ROWS):
            natural, by4 = _by4_rows(S, i)
            for nat, split in zip(*acc):
                nat[natural, :] = nat[natural, :] + split[by4, :]
        for out, nat in zip((dq_ref, dk_ref, dv_ref), acc[0]):
            out[...] = nat[...].astype(BF16)

    blk = lambda off: pl.BlockSpec((S, LANES), lambda h, b, off=off: (b, off + h))
    return pl.pallas_call(
        body, grid=(n_pairs, B),
        in_specs=[blk(0), blk(n_pairs), blk(2 * n_pairs), blk(0), blk(0), blk(0)] + _after(after),
        out_specs=[blk(0), blk(0), blk(0)],
        out_shape=[jax.ShapeDtypeStruct((T, ATTN_WIDTH), BF16)] * 3,
        scratch_shapes=[pltpu.VMEM((18, Q_BLOCK, 2 * Q_BLOCK), F32)] + [pltpu.VMEM((S, LANES), F32)] * 12,
        compiler_params=_params(("arbitrary", "arbitrary")), name="attn_bwd")(proj, proj, proj, a, lse, da, *after)


def _mid(x2d, t2d, a, proj, kv, w_s, w_sT, b_tab, g_v, w_out, g_final, B, S):
    T = B * S
    tm = 512
    nt = S // tm
    halves = 2
    hrows = tm // halves

    def body(x_ref, t_ref, a_ref, za_ref, ub_ref, vb_ref, zb_ref, qm_ref, zm_ref, kv_ref,
              ws_ref, wsT_ref, btab_ref, gv_ref, wout_ref, gf_ref,
              dx2_ref, da_ref, drest_ref, loss_ref, dwout_bf_ref, dws_ref, dbs_ref, dgv_ref, dgf_ref, dkv_ref,
              dbtab_scr, dwout_ref):
        b = pl.program_id(0)
        t = pl.program_id(1)
        first = jnp.logical_and(b == 0, t == 0)
        last = jnp.logical_and(b == B - 1, t == nt - 1)
        _, hm = _head_masks()
        lane_g = lax.broadcasted_iota(jnp.int32, (1, SGU_WIDTH), 1) // HEAD_DIM
        gm = [(lane_g == g).astype(F32) for g in range(N_SGU_GROUPS)]
        H = range(halves)
        rows = [pl.ds(h * hrows, hrows) for h in H]
        ld = lambda ref: [ref[r, :] for r in rows]
        cat = lambda parts, axis: jnp.concatenate(parts, axis=axis)
        chunks = [slice(ci * SGU_CHUNK, (ci + 1) * SGU_CHUNK) for ci in range(hrows // SGU_CHUNK)]
        pairs = [slice(pr * LANES, (pr + 1) * LANES) for pr in range(2)]
        heads = [(pr, j) for pr in range(2) for j in (0, 1)]

        @pl.when(first)
        def _():
            loss_ref[...] = jnp.zeros_like(loss_ref)
            dwout_ref[...] = jnp.zeros_like(dwout_ref)
            dws_ref[...] = jnp.zeros_like(dws_ref)
            dbs_ref[...] = jnp.zeros_like(dbs_ref)
            dgv_ref[...] = jnp.zeros_like(dgv_ref)
            dgf_ref[...] = jnp.zeros_like(dgf_ref)
            dbtab_scr[...] = jnp.zeros_like(dbtab_scr)

        @pl.when(t == 0)
        def _():
            dkv_ref[...] = jnp.zeros_like(dkv_ref)

        a_val = ld(a_ref)
        sil_a = [_silu_parts(z) for z in ld(za_ref)]
        gated_a = [s[0] * a for s, a in zip(sil_a, a_val)]
        u = [_gelu_parts(z) for z in ld(ub_ref)]
        vv = [_gelu_parts(z) for z in ld(vb_ref)]
        vnorm = [_rms(v[0]) for v in vv]
        gv = gv_ref[...]
        vn = [(n[1] * gv).astype(BF16) for n in vnorm]
        w_cat = cat([ws_ref[g].astype(BF16) for g in range(N_SGU_GROUPS)], 1)
        wT_cat = cat([wsT_ref[g].astype(BF16) for g in range(N_SGU_GROUPS)], 1)
        gmb = [m.astype(BF16) for m in gm]
        by_group = lambda chunk: cat([chunk * gmb[g] for g in range(N_SGU_GROUPS)], 0)
        btab = btab_ref[...]
        mixed = [cat([btab + _nn(w_cat, by_group(vn[h][c, :])) for c in chunks], 0) for h in H]
        sg = [u[h][0] * mixed[h] for h in H]
        sil_b = [_silu_parts(z) for z in ld(zb_ref)]
        gated_b = [sil_b[h][0] * sg[h] for h in H]

        kvv = kv_ref[...].astype(BF16)
        kp = [kvv[:, p] for p in pairs]
        vp = [kvv[:, MEM_WIDTH + pr * LANES:MEM_WIDTH + (pr + 1) * LANES] for pr in range(2)]
        qm = ld(qm_ref)
        qj = {(h, pr, j): (qm[h][:, pairs[pr]] * (hm[j] * 0.125)).astype(BF16) for h in H for pr, j in heads}
        sc = {k: _nt(qj[k], kp[k[1]]) for k in qj}
        ex = {k: jnp.exp(sc[k] - jnp.max(sc[k], axis=1, keepdims=True)) for k in qj}
        prob = {k: ex[k] * (1.0 / jnp.sum(ex[k], axis=1, keepdims=True)) for k in qj}
        probb = {k: prob[k].astype(BF16) for k in qj}
        mo = [cat([sum(_nn(probb[(h, pr, j)], vp[pr]) * hm[j] for j in (0, 1)) for pr in range(2)], 1) for h in H]
        sil_m = [_silu_parts(z) for z in ld(zm_ref)]
        gated_m = [sil_m[h][0] * mo[h] for h in H]

        gated = [cat([gated_a[h], gated_b[h], gated_m[h]], 1).astype(BF16) for h in H]
        wout = wout_ref[...]
        x_in = ld(x_ref)
        x2 = [x_in[h] + _nn(gated[h], wout) for h in H]
        fin = [_rms(z) for z in x2]
        gf = gf_ref[...]
        tgt = ld(t_ref)
        err = [fin[h][1] * gf - tgt[h] for h in H]
        loss_ref[...] += sum(jnp.sum(e * e) for e in err) * (0.5 / D_MODEL)

        dy = [e * (1.0 / D_MODEL) for e in err]
        dgf_ref[...] += sum(jnp.sum(dy[h] * fin[h][1], axis=0, keepdims=True) for h in H)
        gdy = [d * gf for d in dy]
        dx2 = [fin[h][0] * (gdy[h] - fin[h][1] * jnp.mean(gdy[h] * fin[h][1], axis=1, keepdims=True)) for h in H]
        for h in H:
            dx2_ref[rows[h], :] = dx2[h]
        dx2b = [d.astype(BF16) for d in dx2]
        dgated = [_nt(d, wout) for d in dx2b]
        dwout_ref[...] += _tn(cat(gated, 0), cat(dx2b, 0))
        dga = [d[:, 0:ATTN_WIDTH] for d in dgated]
        dgb = [d[:, ATTN_WIDTH:ATTN_WIDTH + SGU_WIDTH] for d in dgated]
        dgm = [d[:, ATTN_WIDTH + SGU_WIDTH:] for d in dgated]

        for h in H:
            da_ref[rows[h], :] = dga[h] * sil_a[h][0]
        dza = [dga[h] * a_val[h] * sil_a[h][1] for h in H]

        dsg = [dgb[h] * sil_b[h][0] for h in H]
        dzb = [dgb[h] * sg[h] * sil_b[h][1] for h in H]
        dub = [dsg[h] * mixed[h] * u[h][1] for h in H]
        dmixed = [dsg[h] * u[h][0] for h in H]
        dmixed_b = [d.astype(BF16) for d in dmixed]
        dvn = [cat([_nn(wT_cat, by_group(dmixed_b[h][c, :])) for c in chunks], 0) for h in H]
        for g in range(N_SGU_GROUPS):
            dws_ref[g] += sum(_nt((dmixed[h][c, :] * gm[g]).astype(BF16), vn[h][c, :]) for h in H for c in chunks)
        dbtab_scr[...] += sum(dmixed[h][c, :] for h in H for c in chunks)
        dgv_ref[...] += sum(jnp.sum(dvn[h] * vnorm[h][1], axis=0, keepdims=True) for h in H)
        tv = [d * gv for d in dvn]
        dvv = [vnorm[h][0] * (tv[h] - vnorm[h][1] * jnp.mean(tv[h] * vnorm[h][1], axis=1, keepdims=True)) for h in H]
        dvb = [dvv[h] * vv[h][1] for h in H]

        dmo = [dgm[h] * sil_m[h][0] for h in H]
        dzm = [dgm[h] * mo[h] * sil_m[h][1] for h in H]
        dmoj = {(h, pr, j): (dmo[h][:, pairs[pr]] * hm[j]).astype(BF16) for h in H for pr, j in heads}
        dp = {k: _nt(dmoj[k], vp[k[1]]) for k in qj}
        ds = {k: (prob[k] * (dp[k] - jnp.sum(dp[k] * prob[k], axis=1, keepdims=True))).astype(BF16) for k in qj}
        dqm = [cat([sum(_nn(ds[(h, pr, j)], kp[pr]) * (hm[j] * 0.125) for j in (0, 1)) for pr in range(2)], 1)
               for h in H]
        every = lambda tbl, pr: cat([tbl[(h, pr, j)] for h in H for j in (0, 1)], 0)
        dk = [_tn(every(ds, pr), every(qj, pr)) for pr in range(2)]
        dv = [_tn(every(probb, pr), every(dmoj, pr)) for pr in range(2)]
        dkv_ref[...] += cat(dk + dv, 1)

        for h in H:
            drest_ref[rows[h], :] = cat([dza[h], dub[h], dvb[h], dzb[h], dqm[h], dzm[h]], 1).astype(BF16)

        @pl.when(last)
        def _():
            lane = lax.broadcasted_iota(jnp.int32, (1, LANES), 1)
            dbt = dbtab_scr[...]
            out = jnp.zeros((SGU_CHUNK, LANES), F32)
            for g in range(N_SGU_GROUPS):
                out = out + jnp.where(lane == g, jnp.sum(dbt * gm[g], axis=1, keepdims=True), 0.0)
            dbs_ref[...] = out
            for r0 in range(0, D_MODEL, SGU_CHUNK):
                k, row = divmod(r0, D_MODEL // N_CHIPS)
                dwout_bf_ref[k, row:row + SGU_CHUNK, :] = dwout_ref[r0:r0 + SGU_CHUNK, :].astype(BF16)

    tile = lambda w, cb: pl.BlockSpec((tm, w), lambda b, t, cb=cb: (b * nt + t, cb))
    const = lambda shape: pl.BlockSpec(shape, lambda b, t, n=len(shape): (0,) * n)
    return pl.pallas_call(
        body, grid=(B, nt),
        in_specs=[tile(D_MODEL, 0), tile(D_MODEL, 0), tile(ATTN_WIDTH, 0),
                  tile(ATTN_WIDTH, 3),
                  tile(SGU_WIDTH, 8), tile(SGU_WIDTH, 9), tile(SGU_WIDTH, 10),
                  tile(MEM_WIDTH, 11), tile(MEM_WIDTH, 12),
                  pl.BlockSpec((N_MEM, 2 * MEM_WIDTH), lambda b, t: (b, 0)),
                  const((N_SGU_GROUPS, SGU_CHUNK, SGU_CHUNK)), const((N_SGU_GROUPS, SGU_CHUNK, SGU_CHUNK)),
                  const((SGU_CHUNK, SGU_WIDTH)), const((1, SGU_WIDTH)),
                  const((D_MODEL, D_MODEL)), const((1, D_MODEL))],
        out_specs=[tile(D_MODEL, 0), tile(ATTN_WIDTH, 0), tile(REST_COLS, 0),
                   const((8, LANES)), const((N_CHIPS, D_MODEL // N_CHIPS, D_MODEL)),
                   const((N_SGU_GROUPS, SGU_CHUNK, SGU_CHUNK)), const((SGU_CHUNK, LANES)),
                   const((1, SGU_WIDTH)), const((1, D_MODEL)),
                   pl.BlockSpec((N_MEM, 2 * MEM_WIDTH), lambda b, t: (b, 0))],
        out_shape=[jax.ShapeDtypeStruct((T, D_MODEL), F32), jax.ShapeDtypeStruct((T, ATTN_WIDTH), F32),
                   jax.ShapeDtypeStruct((T, REST_COLS), BF16),
                   jax.ShapeDtypeStruct((8, LANES), F32),
                   jax.ShapeDtypeStruct((N_CHIPS, D_MODEL // N_CHIPS, D_MODEL), BF16),
                   jax.ShapeDtypeStruct((N_SGU_GROUPS, SGU_CHUNK, SGU_CHUNK), F32),
                   jax.ShapeDtypeStruct((SGU_CHUNK, LANES), F32),
                   jax.ShapeDtypeStruct((1, SGU_WIDTH), F32), jax.ShapeDtypeStruct((1, D_MODEL), F32),
                   jax.ShapeDtypeStruct((B * N_MEM, 2 * MEM_WIDTH), F32)],
        scratch_shapes=[pltpu.VMEM((SGU_CHUNK, SGU_WIDTH), F32), pltpu.VMEM((D_MODEL, D_MODEL), F32)],
        compiler_params=_params(("arbitrary", "arbitrary"), vmem=VMEM_LIMIT + 2 * 1024 * 1024), name="mid")(
            x2d, t2d, a, proj, proj, proj, proj, proj, proj, kv, w_s, w_sT, b_tab, g_v, w_out, g_final)


def _inproj_bwd_dx(dq, dk, dv, drest, x2d, dx2, g_norm, w_in_t, after=()):
    T = x2d.shape[0]
    tm = 512
    W = ATTN_WIDTH

    def body(dq_ref, dk_ref, dv_ref, dr_ref, x_ref, dx2_ref, g_ref, w_ref, *rest):
        gx_ref, dg_ref = rest[-2:]

        @pl.when(pl.program_id(0) == 0)
        def _():
            dg_ref[...] = jnp.zeros_like(dg_ref)

        halves = [pl.ds(h * (tm // 2), tm // 2) for h in (0, 1)]
        dh = [(_nn(dq_ref[r, :], w_ref[0:W, :]) + _nn(dk_ref[r, :], w_ref[W:2 * W, :])
               + _nn(dv_ref[r, :], w_ref[2 * W:3 * W, :]) + _nn(dr_ref[r, :], w_ref[QKV_COLS:IN_COLS, :]))
              for r in halves]
        nrm = [_rms(x_ref[r, :]) for r in halves]
        dg_ref[...] += sum(jnp.sum(d * n[1], axis=0, keepdims=True) for d, n in zip(dh, nrm))
        g = g_ref[...]
        for r, d, (rstd, xh) in zip(halves, dh, nrm):
            th = d * g
            gx_ref[r, :] = rstd * (th - xh * jnp.mean(th * xh, axis=1, keepdims=True)) + dx2_ref[r, :]

    tile = lambda w: pl.BlockSpec((tm, w), lambda i: (i, 0))
    return pl.pallas_call(
        body, grid=(T // tm,),
        in_specs=[tile(W), tile(W), tile(W), tile(REST_COLS), tile(D_MODEL), tile(D_MODEL),
                  pl.BlockSpec((1, D_MODEL), lambda i: (0, 0)),
                  pl.BlockSpec((IN_COLS, D_MODEL), lambda i: (0, 0))] + _after(after),
        out_specs=[tile(D_MODEL), pl.BlockSpec((1, D_MODEL), lambda i: (0, 0))],
        out_shape=[jax.ShapeDtypeStruct((T, D_MODEL), F32), jax.ShapeDtypeStruct((1, D_MODEL), F32)],
        compiler_params=_params(("arbitrary",)), name="inproj_bwd_dx")(
            dq, dk, dv, drest, x2d, dx2, g_norm, w_in_t, *after)


def _inproj_bwd_dw(dq, dk, dv, drest, x2d, g_norm, reduce_with=None):
    T = x2d.shape[0]
    tm = 512
    nt = T // tm
    W = ATTN_WIDTH
    fused = reduce_with is not None
    others = list(reduce_with) if fused else []
    ns = 1 + len(others)
    shard = IN_COLS // N_CHIPS
    halves = [shard // 2] + [s.shape[1] // 2 for s in others]
    cols = [D_MODEL] + [s.shape[2] for s in others]
    row_block = 32

    def body(dq_ref, dk_ref, dv_ref, dr_ref, x_ref, g_ref, *rest):
        if fused:
            stacks = rest[:ns - 1]
            sends, owns = rest[ns - 1:2 * ns - 1], rest[2 * ns - 1:3 * ns - 1]
            acc, ras, narrow = rest[3 * ns - 1], rest[3 * ns:4 * ns], rest[4 * ns]
            s_sem, r_sem = rest[4 * ns + 1], rest[4 * ns + 2]
            x, y, c, chip, peers, peer_chip = _place()
            sib = (x, y, 1 - c)

            def part(w, k, cc, r0=0, rows=None):
                n = halves[w]
                rows = n if rows is None else rows
                if w == 0:
                    return acc.at[pl.ds(pl.multiple_of(k * shard + cc * n + r0, 8), rows), :]
                return stacks[w - 1].at[k, pl.ds(pl.multiple_of(cc * n + r0, 8), rows), :]

            def swap_other(w):
                theirs = stacks[w - 1].at[:, pl.ds(pl.multiple_of((1 - c) * halves[w], 8), halves[w]), :]
                return _remote(theirs, ras[w], s_sem.at[N_CHIPS - 1 + w], r_sem.at[N_CHIPS - 1 + w], sib)

            def swap_win(k):
                return _remote(narrow.at[k], ras[0].at[k], s_sem.at[k], r_sem.at[k], sib)
        else:
            acc = rest[0]

        @pl.when(pl.program_id(0) == 0)
        def _():
            acc[...] = jnp.zeros_like(acc)
            for w in range(1, ns):
                swap_other(w).start()

        _, xh = _rms(x_ref[...])
        h = (xh * g_ref[...]).astype(BF16)
        acc[0:W, :] += _tn(dq_ref[...], h)
        acc[W:2 * W, :] += _tn(dk_ref[...], h)
        acc[2 * W:3 * W, :] += _tn(dv_ref[...], h)
        acc[QKV_COLS:IN_COLS, :] += _tn(dr_ref[...], h)

        if fused:
            @pl.when(pl.program_id(0) == nt - 1)
            def _():
                for k in range(N_CHIPS):
                    def to_bf16(i, carry, k=k):
                        r0 = pl.multiple_of(i * row_block, row_block)
                        narrow[k, pl.ds(r0, row_block), :] = part(0, k, 1 - c, r0, row_block)[...].astype(BF16)
                        return carry
                    lax.fori_loop(0, halves[0] // row_block, to_bf16, 0)
                    swap_win(k).start()
                def chip_sum(w, k, r0):
                    blk = pl.ds(r0, row_block)
                    return part(w, k, c, r0, row_block)[...] + ras[w][k, blk, :].astype(F32)

                for w in range(1, ns):
                    swap_other(w).wait_recv()

                    def sums(i, carry, w=w):
                        r0 = pl.multiple_of(i * row_block, row_block)
                        for m in range(3):
                            sends[w][m, pl.ds(r0, row_block), :] = chip_sum(w, peer_chip[m], r0).astype(BF16)
                        owns[w][pl.ds(r0, row_block), :] = chip_sum(w, chip, r0)
                        return carry
                    lax.fori_loop(0, halves[w] // row_block, sums, 0)
                for k in range(N_CHIPS):
                    swap_win(k).wait_recv()

                    @pl.when(chip == k)
                    def _(k=k):
                        def own(i, carry):
                            r0 = pl.multiple_of(i * row_block, row_block)
                            owns[0][pl.ds(r0, row_block), :] = chip_sum(0, k, r0)
                            return carry
                        lax.fori_loop(0, halves[0] // row_block, own, 0)

                    @pl.when(chip != k)
                    def _(k=k):
                        def other(i, carry):
                            r0 = pl.multiple_of(i * row_block, row_block)
                            sends[0][(k ^ chip) - 1, pl.ds(r0, row_block), :] = chip_sum(0, k, r0).astype(BF16)
                            return carry
                        lax.fori_loop(0, halves[0] // row_block, other, 0)
                for k in range(N_CHIPS):
                    swap_win(k).wait_send()
                for w in range(1, ns):
                    swap_other(w).wait_send()

    tile = lambda w: pl.BlockSpec((tm, w), lambda i: (i, 0))
    vmem = pl.BlockSpec(memory_space=pltpu.VMEM)
    in_specs = [tile(W), tile(W), tile(W), tile(REST_COLS), tile(D_MODEL), pl.BlockSpec((1, D_MODEL), lambda i: (0, 0))]
    if not fused:
        return pl.pallas_call(
            body, grid=(nt,), in_specs=in_specs,
            out_specs=pl.BlockSpec((IN_COLS, D_MODEL), lambda i: (0, 0)),
            out_shape=jax.ShapeDtypeStruct((IN_COLS, D_MODEL), F32),
            compiler_params=_params(("arbitrary",)), name="inproj_bwd_dw")(dq, dk, dv, drest, x2d, g_norm)
    outs = pl.pallas_call(
        body, grid=(nt,), in_specs=in_specs + [vmem] * (ns - 1), out_specs=[vmem] * (2 * ns),
        out_shape=[jax.ShapeDtypeStruct((3, n, cl), BF16) for n, cl in zip(halves, cols)]
        + [jax.ShapeDtypeStruct((n, cl), F32) for n, cl in zip(halves, cols)],
        scratch_shapes=[pltpu.VMEM((IN_COLS, D_MODEL), F32)]
        + [pltpu.VMEM((N_CHIPS, n, cl), BF16 if w == 0 else F32) for w, (n, cl) in enumerate(zip(halves, cols))]
        + [pltpu.VMEM((N_CHIPS, halves[0], D_MODEL), BF16)]
        + [pltpu.SemaphoreType.DMA((N_CHIPS - 1 + ns,)), pltpu.SemaphoreType.DMA((N_CHIPS - 1 + ns,))],
        compiler_params=_params(("arbitrary",)), name="inproj_bwd_dw_reduce")(
            dq, dk, dv, drest, x2d, g_norm, *others)
    return outs[:ns], outs[ns:]


def _adamw_update(w, g, m, v):
    nm = ADAM_B1 * m + (1.0 - ADAM_B1) * g
    nv = ADAM_B2 * v + (1.0 - ADAM_B2) * (g * g)
    m_hat = nm / (1.0 - ADAM_B1 ** ADAM_STEP)
    v_hat = nv / (1.0 - ADAM_B2 ** ADAM_STEP)
    return -ADAM_LR * (m_hat / (jnp.sqrt(v_hat) + ADAM_EPS) + ADAM_WD * w), nm, nv


def _adamw(w, g, m, v, name):
    R, C = w.shape
    br = max(r for r in range(8, 257, 8) if R % r == 0)

    def body(w_ref, g_ref, m_ref, v_ref, g_out, d_ref, nm_ref, nv_ref):
        g = g_ref[...]
        g_out[...] = g
        d_ref[...], nm_ref[...], nv_ref[...] = _adamw_update(w_ref[...], g, m_ref[...], v_ref[...])

    spec = pl.BlockSpec((br, C), lambda i: (i, 0))
    return pl.pallas_call(
        body, grid=(R // br,), in_specs=[spec] * 4, out_specs=[spec] * 4,
        out_shape=[jax.ShapeDtypeStruct((R, C), F32)] * 4,
        compiler_params=_params(("arbitrary",)), name=name)(w, g, m, v)


def _adamw_rest(g_packed, ws, ms, vs, whole):
    n = len(ws)
    nw = len(whole)
    row_block = 64

    def body(*refs):
        g_ref = refs[0]
        w_refs, m_refs, v_refs = refs[1:1 + n], refs[1 + n:1 + 2 * n], refs[1 + 2 * n:1 + 3 * n]
        whole_in = [refs[1 + 3 * n + 4 * i:5 + 3 * n + 4 * i] for i in range(nw)]
        outs = refs[1 + 3 * n + 4 * nw:]
        off = 0
        for i, (_, used, padded) in enumerate(_SMALL_PARTS[:n]):
            g = g_ref[off:off + used, :]
            delta, nm, nv = _adamw_update(w_refs[i][...], g, m_refs[i][...], v_refs[i][...])
            outs[4 * i][...], outs[4 * i + 1][...], outs[4 * i + 2][...], outs[4 * i + 3][...] = g, delta, nm, nv
            off += padded
        for i, (w_ref, gw_ref, m_ref, v_ref) in enumerate(whole_in):
            for r0 in range(0, w_ref.shape[0], row_block):
                blk = pl.ds(r0, row_block)
                g = gw_ref[blk, :]
                new = _adamw_update(w_ref[blk, :], g, m_ref[blk, :], v_ref[blk, :])
                for ref, val in zip(outs[4 * (n + i):4 * (n + i) + 4], (g,) + new):
                    ref[blk, :] = val
        outs[4 * (n + nw)][...] = g_ref[_LOSS_ROW:_LOSS_ROW + 1, 0:1]

    outs = pl.pallas_call(
        body, out_shape=[jax.ShapeDtypeStruct(w.shape, F32) for w in ws for _ in range(4)]
        + [jax.ShapeDtypeStruct(four[0].shape, F32) for four in whole for _ in range(4)]
        + [jax.ShapeDtypeStruct((1, 1), F32)],
        compiler_params=_params(), name="adamw_rest")(g_packed, *ws, *ms, *vs, *[a for four in whole for a in four])
    return ([outs[4 * i:4 * i + 4] for i in range(n)], [outs[4 * (n + i):4 * (n + i) + 4] for i in range(nw)],
            outs[4 * (n + nw)])


def _place():
    x, y, c = lax.axis_index("x"), lax.axis_index("y"), lax.axis_index("c")
    chip = 2 * x + y
    peers = [(x, 1 - y), (1 - x, y), (1 - x, 1 - y)]
    peer_chip = [2 * px + py for px, py in peers]
    return x, y, c, chip, peers, peer_chip


def _remote(src, dst, send_sem, recv_sem, dev):
    return pltpu.make_async_remote_copy(src_ref=src, dst_ref=dst, send_sem=send_sem, recv_sem=recv_sem,
                                        device_id=dev, device_id_type=MESH)


def _ag_weights(weights, late=()):
    nw, nl = len(weights), len(late)
    pieces = 2

    def body(*refs):
        srcs, late_srcs = refs[:nw], refs[nw:nw + nl]
        outs, late_bf, late_land = (refs[nw + nl:2 * nw + nl], refs[2 * nw + nl:2 * nw + 2 * nl],
                                    refs[2 * nw + 2 * nl:2 * nw + 3 * nl])
        s_ici, r_ici, s_d2d, r_d2d = refs[2 * nw + 3 * nl:]
        x, y, c = lax.axis_index("x"), lax.axis_index("y"), lax.axis_index("c")
        chip = 2 * x + y
        sib = (x, y, 1 - c)
        first = ((x + 1 - c) % 2, (y + c) % 2)
        second = ((x + c) % 2, (y + 1 - c) % 2)
        first_chip, second_chip = 2 * first[0] + first[1], 2 * second[0] + second[1]
        diag_chip = 3 - chip
        for src, out in zip(srcs, outs):
            out[chip] = src[...].astype(BF16)

        parts = [(w, out, pc) for w, out in enumerate(outs) for pc in range(pieces)]

        def piece(out, k, cc, pc):
            rows = out.shape[1] // 2 // pieces
            return out.at[k, pl.ds(pl.multiple_of((cc * pieces + pc) * rows, 16), rows), :]

        def ici(w, slot, out, k, dev, pc):
            blk, sem = piece(out, k, c, pc), (nw * slot + w) * pieces + pc
            return _remote(blk, blk, s_ici.at[sem], r_ici.at[sem], (dev[0], dev[1], c))

        def d2d(w, slot, out, k, cc, pc):
            blk, sem = piece(out, k, cc, pc), (nw * slot + w) * pieces + pc
            return _remote(blk, blk, s_d2d.at[sem], r_d2d.at[sem], sib)

        sent = []
        for slot, dev in enumerate((first, second)):
            for w, out, pc in parts:
                sent.append(ici(w, slot, out, chip, dev, pc))
                sent[-1].start()
        for src, bf, land in zip(late_srcs, late_bf, late_land):
            bf[...] = src[...].astype(BF16)
            land[...] = jnp.zeros_like(land)
            land[chip] = bf[...]
        for slot, k, dev in ((0, first_chip, first), (1, second_chip, second), (2, diag_chip, second)):
            for w, out, pc in parts:
                ici(w, slot, out, k, dev, pc).wait_recv()
                if slot == 0:
                    sent.append(ici(w, 2, out, k, second, pc))
                    sent[-1].start()
                sent.append(d2d(w, slot, out, k, c, pc))
                sent[-1].start()
        for slot, k in ((0, second_chip), (1, first_chip), (2, diag_chip)):
            for w, out, pc in parts:
                d2d(w, slot, out, k, 1 - c, pc).wait_recv()
        for cp in sent:
            cp.wait_send()

    vmem = pl.BlockSpec(memory_space=pltpu.VMEM)
    outs = pl.pallas_call(
        body,
        out_shape=[jax.ShapeDtypeStruct((N_CHIPS,) + w.shape, BF16) for w in weights]
        + [jax.ShapeDtypeStruct(w.shape, BF16) for w in late]
        + [jax.ShapeDtypeStruct((N_CHIPS,) + w.shape, BF16) for w in late],
        in_specs=[vmem] * (nw + nl), out_specs=[vmem] * (nw + 2 * nl),
        scratch_shapes=[pltpu.SemaphoreType.DMA((3 * nw * pieces,))] * 4,
        compiler_params=pltpu.CompilerParams(vmem_limit_bytes=VMEM_LIMIT), name="ag_weights")(*weights, *late)
    return outs[:nw], outs[nw:nw + nl], outs[nw + nl:]


_HBM = pl.BlockSpec(memory_space=pltpu.HBM)
_SEM = pl.BlockSpec(memory_space=pltpu.SEMAPHORE)
_ANY = pl.BlockSpec(memory_space=pl.ANY)
_DATAFLOW = pltpu.SideEffectType.DATAFLOW_SIDE_EFFECTING


def _in_hbm(a):
    return pltpu.with_memory_space_constraint(a, pltpu.HBM)


_PEERS_OF = {"gather": 3, "scatter": 3, "direct": 7}


def _exchange_copies(mode, srcs, lands, send_sems, recv_sems):
    nw = len(srcs)
    x, y, c, chip, peers, peer_chip = _place()
    pairs = []
    if mode == "direct":
        targets = [((x, y), chip, 1)] + [(p, k, d) for p, k in zip(peers, peer_chip) for d in (0, 1)]
        for r, ((px, py), k, d) in enumerate(targets):
            for w in range(nw):
                sems = (send_sems.at[nw * r + w], recv_sems.at[nw * r + w], (px, py, (c + d) % 2))
                share = srcs[w].at[k if srcs[w].shape[0] > 1 else 0]
                pairs.append((_remote(share, lands[w].at[r], *sems),) * 2)
        return pairs
    gather = mode == "gather"
    for m, (px, py) in enumerate(peers):
        for w in range(nw):
            sems = (send_sems.at[nw * m + w], recv_sems.at[nw * m + w], (px, py, c))
            if gather:
                pairs.append((_remote(srcs[w], lands[w].at[chip], *sems),
                              _remote(srcs[w], lands[w].at[peer_chip[m]], *sems)))
            else:
                pairs.append((_remote(srcs[w].at[m], lands[w].at[m], *sems),) * 2)
    return pairs


def _exchange_start(mode, srcs, after, name, lands=None):
    nw = len(srcs)
    n_copies = _PEERS_OF[mode] * nw

    after = tuple(after)

    def body(*refs):
        send_sems, recv_sems = refs[2 * nw + len(after)], refs[2 * nw + len(after) + 1]
        for start, _ in _exchange_copies(mode, refs[:nw], refs[nw:2 * nw], send_sems, recv_sems):
            start.start()
        refs[-1][...] = jnp.zeros_like(refs[-1])

    if lands is None:
        shape = {"gather": lambda s: (N_CHIPS,) + s.shape, "scatter": lambda s: s.shape,
                 "direct": lambda s: (_PEERS_OF["direct"],) + s.shape[1:]}[mode]
        lands = [lax.empty(shape(s), s.dtype) for s in srcs]
    lands = [_in_hbm(l) for l in lands]
    return pl.pallas_call(
        body, name=name,
        out_shape=(pltpu.SemaphoreType.DMA((n_copies,)), pltpu.SemaphoreType.DMA((n_copies,)))
        + tuple(pltpu.HBM(s.shape, s.dtype) for s in srcs)
        + tuple(pltpu.HBM(l.shape, l.dtype) for l in lands)
        + (jax.ShapeDtypeStruct((8, LANES), F32),),
        in_specs=[_HBM] * (2 * nw) + [_ANY] * len(after),
        out_specs=(_SEM, _SEM) + (_HBM,) * (2 * nw) + (pl.BlockSpec(memory_space=pltpu.VMEM),),
        input_output_aliases={i: 2 + i for i in range(2 * nw)},
        compiler_params=pltpu.CompilerParams(has_side_effects=_DATAFLOW),
    )(*[_in_hbm(s) for s in srcs], *lands, *after)


def _exchange_wait(mode, started, after, name):
    nw = (len(started) - 3) // 2
    send_sems, recv_sems = started[0], started[1]
    thru = started[2:2 + 2 * nw]

    def body(*refs):
        for _, arrival in _exchange_copies(mode, refs[:nw], refs[nw:2 * nw], refs[2 * nw], refs[2 * nw + 1]):
            arrival.wait_send()
            arrival.wait_recv()

    outs = pl.pallas_call(
        body, name=name,
        out_shape=tuple(pltpu.HBM(t.shape, t.dtype) for t in thru),
        in_specs=[_HBM] * (2 * nw) + [_SEM, _SEM, _ANY], out_specs=(_HBM,) * (2 * nw),
        input_output_aliases={i: i for i in range(2 * nw)},
        compiler_params=pltpu.CompilerParams(has_side_effects=_DATAFLOW),
    )(*thru, send_sems, recv_sems, after)
    return outs[:nw], outs[nw:]


def _reduce_last(owns, landed, g_small, direct_srcs, direct_landed, spread_row0):
    ns, nd = len(owns), len(direct_srcs)
    halves = [o.shape[0] for o in owns]
    row_block = 16
    spread_rows = direct_srcs[-1].shape[1]
    rest0, rest1 = spread_row0, SMALL_ROWS - spread_row0 - spread_rows
    hs = (rest0 + rest1) // 2
    jobs = [(w, p * (halves[w] // 2), halves[w] // 2) for w in range(ns) for p in range(2)]
    n_swaps = len(jobs)
    jobs += [(ns + d, 0, direct_srcs[d].shape[1]) for d in range(nd)]

    def body(*refs):
        own_refs, land_refs, gsm_ref = refs[:ns], refs[ns:2 * ns], refs[2 * ns]
        dsrc_refs, dland_refs = refs[2 * ns + 1:2 * ns + 1 + nd], refs[2 * ns + 1 + nd:2 * ns + 1 + 2 * nd]
        n_in = 2 * ns + 1 + 2 * nd
        out_refs, osm_ref = refs[n_in:n_in + ns + nd - 1], refs[n_in + ns + nd - 1]
        scr = refs[n_in + ns + nd:]
        own_scr, land_scr, dsrc_scr, dland_scr = (scr[:ns], scr[ns:2 * ns], scr[2 * ns:2 * ns + nd],
                                                  scr[2 * ns + nd:2 * ns + 2 * nd])
        o_rest, ra_sm, p_sm, in_sem, s_sem, r_sem, sm_s, sm_r = scr[2 * ns + 2 * nd:]
        x, y, c, chip, peers, peer_chip = _place()
        sib = (x, y, 1 - c)
        half = lambda cc: pl.ds(pl.multiple_of(cc * hs, 8), hs)
        sm_a = _remote(gsm_ref.at[half(1 - c), :], ra_sm, sm_s.at[0], sm_r.at[0], sib)
        sm_a.start()
        swaps = [sm_a]

        def reads(j):
            w, r0, n = jobs[j]
            rows = pl.ds(r0, n)
            if w < ns:
                pairs = [(own_refs[w].at[rows, :], own_scr[w].at[rows, :]),
                         (land_refs[w].at[:, rows, :], land_scr[w].at[:, rows, :])]
            else:
                share = dsrc_refs[w - ns].at[chip if w < ns + nd - 1 else 0]
                pairs = [(share, dsrc_scr[w - ns]), (dland_refs[w - ns], dland_scr[w - ns])]
            return [pltpu.make_async_copy(s, d, in_sem.at[2 * j + i]) for i, (s, d) in enumerate(pairs)]

        for j in range(len(jobs)):
            for cp in reads(j):
                cp.start()
        sm_a.wait_recv()
        p_sm[chip] = gsm_ref[half(c), :] + ra_sm[...]
        for m, (px, py) in enumerate(peers):
            swaps.append(_remote(p_sm.at[chip], p_sm.at[chip], sm_s.at[1 + m], sm_r.at[1 + m], (px, py, c)))
            swaps[-1].start()

        def mine_of(j):
            w, r0, n = jobs[j]
            return out_refs[w].at[pl.ds(pl.multiple_of(c * halves[w] + r0, 8), n), :]

        for j, (w, r0, n) in enumerate(jobs):
            for cp in reads(j):
                cp.wait()

            def total(i, carry, w=w, r0=r0):
                rr = pl.multiple_of(r0 + i * row_block, row_block)
                blk = pl.ds(rr, row_block)
                if w < ns:
                    acc = own_scr[w][blk, :]
                    for m in range(_PEERS_OF["scatter"]):
                        acc = acc + land_scr[w][m, blk, :].astype(F32)
                    out_refs[w][pl.ds(pl.multiple_of(c * halves[w] + rr, row_block), row_block), :] = acc
                else:
                    theirs = lambda r: dland_scr[w - ns][r, blk, :].astype(F32)
                    acc = ((dsrc_scr[w - ns][blk, :].astype(F32) + theirs(0)) + (theirs(1) + theirs(2))) + (
                        (theirs(3) + theirs(4)) + (theirs(5) + theirs(6)))
                    if w < ns + nd - 1:
                        out_refs[w][blk, :] = acc
                    else:
                        osm_ref[pl.ds(pl.multiple_of(spread_row0 + rr, 8), row_block), :] = acc
                return carry
            lax.fori_loop(0, n // row_block, total, 0)
            if w < ns:
                swaps.append(_remote(mine_of(j), mine_of(j), s_sem.at[j], r_sem.at[j], sib))
                swaps[-1].start()
        for m, (px, py) in enumerate(peers):
            _remote(p_sm.at[chip], p_sm.at[peer_chip[m]], sm_s.at[1 + m], sm_r.at[1 + m], (px, py, c)).wait_recv()
        o_rest[half(c), :] = (p_sm[0] + p_sm[1]) + (p_sm[2] + p_sm[3])
        swaps.append(_remote(o_rest.at[half(c), :], o_rest.at[half(c), :], sm_s.at[4], sm_r.at[4], sib))
        swaps[-1].start()
        for j, (w, r0, n) in enumerate(jobs[:n_swaps]):
            theirs = out_refs[w].at[pl.ds(pl.multiple_of((1 - c) * halves[w] + r0, 8), n), :]
            _remote(theirs, theirs, s_sem.at[j], r_sem.at[j], sib).wait_recv()
        _remote(o_rest.at[half(1 - c), :], o_rest.at[half(1 - c), :], sm_s.at[4], sm_r.at[4], sib).wait_recv()
        osm_ref[0:rest0, :] = o_rest[0:rest0, :]
        osm_ref[SMALL_ROWS - rest1:SMALL_ROWS, :] = o_rest[rest0:rest0 + rest1, :]
        for cp in swaps:
            cp.wait_send()

    vmem = pl.BlockSpec(memory_space=pltpu.VMEM)
    far = [pl.BlockSpec(memory_space=pl.ANY)]
    return pl.pallas_call(
        body, out_shape=[jax.ShapeDtypeStruct((2 * o.shape[0], o.shape[1]), F32) for o in owns]
        + [jax.ShapeDtypeStruct(s.shape[1:], F32) for s in direct_srcs[:-1]]
        + [jax.ShapeDtypeStruct((SMALL_ROWS, LANES), F32)],
        in_specs=far * (2 * ns) + [vmem] + far * (2 * nd), out_specs=[vmem] * (ns + nd),
        scratch_shapes=[pltpu.VMEM(o.shape, o.dtype) for o in owns] + [pltpu.VMEM(l.shape, l.dtype) for l in landed]
        + [pltpu.VMEM(s.shape[1:], s.dtype) for s in direct_srcs]
        + [pltpu.VMEM(l.shape, l.dtype) for l in direct_landed]
        + [pltpu.VMEM((2 * hs, LANES), F32), pltpu.VMEM((hs, LANES), F32), pltpu.VMEM((N_CHIPS, hs, LANES), F32),
           pltpu.SemaphoreType.DMA((2 * len(jobs),)),
           pltpu.SemaphoreType.DMA((n_swaps,)), pltpu.SemaphoreType.DMA((n_swaps,)),
           pltpu.SemaphoreType.DMA((5,)), pltpu.SemaphoreType.DMA((5,))],
        compiler_params=pltpu.CompilerParams(vmem_limit_bytes=VMEM_LIMIT),
        name="reduce_last")(*owns, *landed, g_small, *direct_srcs, *direct_landed)


_SMALL_PARTS = (("g_norm", 8, 8), ("w_s", 512, 512), ("b_s", 4, 8), ("g_v", 2, 8), ("g_mem", 8, 8),
                ("g_final", 8, 8), ("loss", 8, 8))
_LOSS_ROW = SMALL_ROWS - 8
_W_S_ROW = 8
assert sum(p for _, _, p in _SMALL_PARTS) == SMALL_ROWS and _SMALL_PARTS[1][0] == "w_s"


def _pack_small(parts, loss_block):
    rows = []
    parts = dict(parts, loss=loss_block)
    for name, used, padded in _SMALL_PARTS:
        if name == "w_s":
            continue
        p = parts[name].reshape(used, LANES)
        if padded > used:
            p = jnp.pad(p, ((0, padded - used), (0, 0)))
        rows.append(p)
    return jnp.concatenate(rows, axis=0)


def _local_step(x, mem, target, g_norm, w_in, w_s, b_s, g_v, g_mem, late_weights, g_final,
                fwd_token=None, on_late=None, on_dw=None):
    B, S, _ = x.shape
    x2d = x.reshape(B * S, D_MODEL)
    t2d = target.reshape(B * S, D_MODEL)
    mem2d = mem.reshape(B * N_MEM, D_MODEL)

    proj = _inproj_fwd(x2d, g_norm, w_in, after=() if fwd_token is None else (fwd_token,))
    w_kv, w_out = late_weights(proj)
    kv = _kv_fwd(mem2d, g_mem, w_kv)
    a, lse = _attn_fwd(proj, B, S)
    w_sT = jnp.swapaxes(w_s, 1, 2)
    b_tab = jnp.repeat(b_s.T, HEAD_DIM, axis=1)
    (dx2, da, drest, loss, d_wout, d_ws, d_bs, d_gv, d_gf, dkv) = _mid(
        x2d, t2d, a, proj, kv, w_s, w_sT, b_tab, g_v, w_out, g_final, B, S)
    d_wkv, d_gmem = _kv_bwd(mem2d, g_mem, w_kv, dkv)
    dq, dk, dv = _attn_bwd(proj, a, lse, da, B, S, after=() if on_late is None else (on_late(d_wkv, d_wout, d_ws),))
    if on_dw is None:
        d_win = _inproj_bwd_dw(dq, dk, dv, drest, x2d, g_norm)
        after = ()
    else:
        d_win = None
        after = (on_dw(*_inproj_bwd_dw(dq, dk, dv, drest, x2d, g_norm, reduce_with=[])),)
    grad_x, d_gnorm = _inproj_bwd_dx(dq, dk, dv, drest, x2d, dx2, g_norm, w_in, after=after)
    d_bs = d_bs[:, :N_SGU_GROUPS].T
    return (loss, grad_x.reshape(B, S, D_MODEL),
            dict(g_norm=d_gnorm, w_in=d_win, w_s=d_ws, b_s=d_bs, g_v=d_gv, g_mem=d_gmem, w_kv=d_wkv,
                 w_out=d_wout, g_final=d_gf))


def kernel(x, mem, g_norm, w_in, w_sgu_spatial, b_sgu_spatial, g_sgu_v, g_mem, w_mem_kv, w_out, g_final, loss_target, m_g_norm, m_w_in, m_w_sgu_spatial, m_b_sgu_spatial, m_g_sgu_v, m_g_mem, m_w_mem_kv, m_w_out, m_g_final, v_g_norm, v_w_in, v_w_sgu_spatial, v_b_sgu_spatial, v_g_sgu_v, v_g_mem, v_w_mem_kv, v_w_out, v_g_final):
    t = lambda w: jnp.swapaxes(w[0], 0, 1)
    (win_all,), late_shards, late_lands = _ag_weights([t(w_in)], [w_mem_kv[0], w_out[0]])
    w_in_full = win_all.reshape(-1, win_all.shape[-1])
    late = _exchange_start("gather", list(late_shards), (win_all,), "gather_late_start", lands=late_lands)

    def late_weights(proj):
        return [z.reshape(-1, z.shape[-1]) for z in _exchange_wait("gather", late, proj, "gather_late_wait")[1]]

    scatter = {}

    def on_late(d_wkv, d_wout, d_ws):
        d_ws = d_ws.reshape(1, -1, LANES)
        scatter["late"] = _exchange_start("direct", [d_wkv, d_wout, d_ws], (), "scatter_late_start")
        return scatter["late"][-1]

    def on_dw(sends, owns):
        scatter["own"] = owns
        scatter["started"] = _exchange_start("scatter", list(sends), (owns[0],), "scatter_start")
        return scatter["started"][-1]

    loss, grad_x, g = _local_step(
        x, mem, loss_target, g_norm, w_in_full, w_sgu_spatial[0], b_sgu_spatial[0], g_sgu_v, g_mem,
        late_weights, g_final.reshape(1, D_MODEL), fwd_token=late[-1], on_late=on_late, on_dw=on_dw)

    small_names = ("g_norm", "w_s", "b_s", "g_v", "g_mem", "g_final")
    g_small = _pack_small({n: g[n] for n in small_names if n != "w_s"}, loss)
    late_srcs, late_landed = _exchange_wait("direct", scatter["late"], g_small, "scatter_late_wait")
    _, landed = _exchange_wait("scatter", scatter["started"], late_landed[0], "scatter_wait")
    gr_in, gr_kv, gr_out, gr_small = _reduce_last(scatter["own"], landed, g_small, late_srcs, late_landed, _W_S_ROW)

    small_w = (g_norm, w_sgu_spatial, b_sgu_spatial, g_sgu_v, g_mem, g_final)
    small_m = (m_g_norm, m_w_sgu_spatial, m_b_sgu_spatial, m_g_sgu_v, m_g_mem, m_g_final)
    small_v = (v_g_norm, v_w_sgu_spatial, v_b_sgu_spatial, v_g_sgu_v, v_g_mem, v_g_final)
    rows = lambda ws: [w.reshape(-1, LANES) for w in ws]
    small_new, ((gr_kv, d_kv, nm_kv, nv_kv), (gr_out, d_out, nm_out, nv_out)), loss = _adamw_rest(
        gr_small, rows(small_w), rows(small_m), rows(small_v),
        [(w_mem_kv[0], gr_kv, m_w_mem_kv[0], v_w_mem_kv[0]), (w_out[0], gr_out, m_w_out[0], v_w_out[0])])
    loss = loss.reshape(())
    small = [[z.reshape(w.shape) for z in four] for w, four in zip(small_w, small_new)]
    gr_in, d_in, nm_in, nv_in = [jnp.swapaxes(z, 0, 1)
                                 for z in _adamw(t(w_in), gr_in, t(m_w_in), t(v_w_in), "adamw_w_in")]

    def leaves(kind, big_in, big_kv, big_out):
        s_norm, s_ws, s_bs, s_gv, s_gmem, s_gf = [four[kind] for four in small]
        return [s_norm, big_in[None], s_ws, s_bs, s_gv, s_gmem, big_kv[None], big_out[None], s_gf]

    return (loss, grad_x, *leaves(0, gr_in, gr_kv, gr_out), *leaves(1, d_in, d_kv, d_out),
            *leaves(2, nm_in, nm_kv, nm_out), *leaves(3, nv_in, nv_kv, nv_out))
```

```python
import functools

import jax
import jax.numpy as jnp
from jax import lax
from jax.experimental import pallas as pl
from jax.experimental.pallas import tpu as pltpu

F32 = jnp.float32
BF16 = jnp.bfloat16
MESH = pl.DeviceIdType.MESH

D_MODEL = 1024
ATTN_WIDTH = 512
SGU_WIDTH = 256
MEM_WIDTH = 256
N_MEM = 256
IN_COLS = 3328
QKV_COLS = 3 * ATTN_WIDTH
REST_COLS = IN_COLS - QKV_COLS
SGU_CHUNK = 128
N_SGU_GROUPS = 4
EPS = 1e-6
NEG_INF = -1e30
DILATIONS = (1, 4, 16)
RADIUS = 64
Q_BLOCK = 128
LANES = 128
HEAD_DIM = 64

ADAM_LR = 0.001
ADAM_B1 = 0.9
ADAM_B2 = 0.999
ADAM_EPS = 1e-08
ADAM_WD = 0.01
ADAM_STEP = 10

N_CHIPS = 4
VMEM_LIMIT = 56 * 1024 * 1024
SMALL_ROWS = 560


def _params(sem=None, vmem=VMEM_LIMIT):
    return pltpu.CompilerParams(dimension_semantics=sem, vmem_limit_bytes=vmem)


def _nn(a, b):
    return jnp.dot(a, b, preferred_element_type=F32)


def _nt(a, b):
    return lax.dot_general(a, b, (((1,), (1,)), ((), ())), preferred_element_type=F32)


def _tn(a, b):
    return lax.dot_general(a, b, (((0,), (0,)), ((), ())), preferred_element_type=F32)


def _rms(x):
    r = lax.rsqrt(jnp.mean(x * x, axis=-1, keepdims=True) + EPS)
    return r, x * r


def _head_masks():
    lane = lax.broadcasted_iota(jnp.int32, (1, LANES), 1)
    lo = lane < HEAD_DIM
    return lo, (lo.astype(F32), (~lo).astype(F32))


def _silu_parts(z):
    s = jax.nn.sigmoid(z)
    return z * s, s * (1.0 + z * (1.0 - s))


def _gelu_parts(x):
    c = 0.7978845608028654
    x2 = x * x
    s = jax.nn.sigmoid((2.0 * c) * (x + 0.044715 * (x * x2)))
    return x * s, s * (1.0 + x * (1.0 - s) * ((2.0 * c) * (1.0 + 3.0 * 0.044715 * x2)))


def _after(tokens):
    return [pl.BlockSpec(memory_space=pl.ANY)] * len(tokens)


def _inproj_fwd(x2d, g_norm, w_in_t, after=()):
    T = x2d.shape[0]
    tm = 512

    def body(x_ref, g_ref, w_ref, *rest):
        o_ref = rest[-1]
        _, xh = _rms(x_ref[...])
        h = (xh * g_ref[...]).astype(BF16)
        o_ref[...] = _nt(h, w_ref[...])

    return pl.pallas_call(
        body, grid=(T // tm,),
        in_specs=[pl.BlockSpec((tm, D_MODEL), lambda i: (i, 0)),
                  pl.BlockSpec((1, D_MODEL), lambda i: (0, 0)),
                  pl.BlockSpec((IN_COLS, D_MODEL), lambda i: (0, 0))] + _after(after),
        out_specs=pl.BlockSpec((tm, IN_COLS), lambda i: (i, 0)),
        out_shape=jax.ShapeDtypeStruct((T, IN_COLS), F32),
        compiler_params=_params(("arbitrary",)), name="inproj_fwd")(x2d, g_norm, w_in_t, *after)


def _kv_fwd(mem2d, g_mem, w_kv):
    Tm = mem2d.shape[0]

    def body(m_ref, g_ref, w_ref, o_ref):
        _, mh = _rms(m_ref[...])
        o_ref[...] = _nn((mh * g_ref[...]).astype(BF16), w_ref[...])

    return pl.pallas_call(
        body, out_shape=jax.ShapeDtypeStruct((Tm, 2 * MEM_WIDTH), F32),
        compiler_params=_params(), name="kv_fwd")(mem2d, g_mem, w_kv)


def _kv_bwd(mem2d, g_mem, w_kv, dkv):
    Tm = mem2d.shape[0]

    def body(m_ref, g_ref, w_ref, dkv_ref, dw_ref, dg_ref):
        _, mh = _rms(m_ref[...])
        memn = (mh * g_ref[...]).astype(BF16)
        dkvb = dkv_ref[...].astype(BF16)
        dw = _tn(memn, dkvb).astype(BF16)
        for k in range(N_CHIPS):
            dw_ref[k] = dw[k * (D_MODEL // N_CHIPS):(k + 1) * (D_MODEL // N_CHIPS), :]
        dmemn = _nt(dkvb, w_ref[...])
        dg_ref[...] = jnp.sum(dmemn * mh, axis=0, keepdims=True)

    return pl.pallas_call(
        body, out_shape=(jax.ShapeDtypeStruct((N_CHIPS, D_MODEL // N_CHIPS, 2 * MEM_WIDTH), BF16),
                         jax.ShapeDtypeStruct((1, D_MODEL), F32)),
        compiler_params=_params(), name="kv_bwd")(mem2d, g_mem, w_kv, dkv)


def _attn_geometry(S):
    geom = []
    for d in DILATIONS:
        L = S // d
        assert L % Q_BLOCK == 0
        geom.append((d, L, min(2 * Q_BLOCK, L), L // Q_BLOCK))
    return geom


def _init_bias(bias_scr, geom, hp):
    row = lax.broadcasted_iota(jnp.int32, (Q_BLOCK, 2 * Q_BLOCK), 0)
    col = lax.broadcasted_iota(jnp.int32, (Q_BLOCK, 2 * Q_BLOCK), 1)
    for j in (0, 1):
        bits = (126 - (2 * hp + j)) * (1 << 23)
        slope = lax.bitcast_convert_type(jnp.full((1, 1), bits, jnp.int32), F32)
        for di, (d, _, _, _) in enumerate(geom):
            for cls, off in enumerate((0, -RADIUS, -2 * RADIUS)):
                dist = jnp.abs(col - row + off)
                bias_scr[di * 6 + cls * 2 + j] = jnp.where(
                    dist <= RADIUS, -(slope * float(d)) * dist.astype(F32), NEG_INF)


SPLIT = 4
COPY_ROWS = 256


def _by4_rows(S, step):
    per_class = S // SPLIT // COPY_ROWS
    r, j = step // per_class, step % per_class
    return (pl.ds(r + SPLIT * j * COPY_ROWS, COPY_ROWS, stride=SPLIT),
            pl.ds(r * (S // SPLIT) + j * COPY_ROWS, COPY_ROWS))


def _to_by4(src, dst, S):
    for i in range(S // COPY_ROWS):
        natural, by4 = _by4_rows(S, i)
        dst[by4, :] = src[natural, :]


def _block_slices(d, L, KW, nqb, r, qb, S):
    qs = qb * Q_BLOCK
    ks = jnp.clip(qs - RADIUS, 0, L - KW)
    cls = jnp.where(qb == 0, 0, jnp.where(qb == nqb - 1, 2, 1))
    if d == 1:
        qsl = pl.ds(pl.multiple_of(qs, Q_BLOCK), Q_BLOCK)
        ksl = pl.ds(pl.multiple_of(ks, RADIUS), KW)
    elif d == SPLIT:
        qsl = pl.ds(pl.multiple_of(r * L + qs, Q_BLOCK), Q_BLOCK)
        ksl = pl.ds(pl.multiple_of(r * L + ks, RADIUS), KW)
    else:
        sub = d // SPLIT
        base = (r % SPLIT) * (S // SPLIT) + r // SPLIT
        qsl = pl.ds(base + qs * sub, Q_BLOCK, stride=sub)
        ksl = pl.ds(base + ks * sub, KW, stride=sub)
    return qsl, ksl, cls


def _for_groups(geom, S, group, fn):
    for di, (d, L, KW, nqb) in enumerate(geom):
        n = group[di]
        assert (d * nqb) % n == 0

        def step(it, carry, di=di, d=d, L=L, KW=KW, nqb=nqb, n=n):
            slices = []
            for g in range(n):
                i = it * n + g
                slices.append(_block_slices(d, L, KW, nqb, i // nqb, i % nqb, S))
            fn(di, KW, slices)
            return carry
        lax.fori_loop(0, d * nqb // n, step, 0)


def _attn_fwd(proj, B, S):
    T = B * S
    geom = _attn_geometry(S)
    n_pairs = ATTN_WIDTH // LANES

    def body(q_ref, k_ref, v_ref, a_ref, lse_ref, bias_scr, q4, k4, v4, *per_dilation):
        o_scr, m_scr, l_scr = per_dilation[0:3], per_dilation[3:6], per_dilation[6:9]
        lo, hm = _head_masks()
        pair = pl.program_id(0)

        @pl.when(pl.program_id(1) == 0)
        def _():
            _init_bias(bias_scr, geom, pair)
        for src, dst in ((q_ref, q4), (k_ref, k4), (v_ref, v4)):
            _to_by4(src, dst, S)

        def group(di, KW, all_slices):
            run = 8
            for first in range(0, len(all_slices), run):
                some(di, KW, all_slices[first:first + run])

        def some(di, KW, slices):
            chains = [(g, j) for g in range(len(slices)) for j in (0, 1)]
            q_src, k_src, v_src = (q_ref, k_ref, v_ref) if di == 0 else (q4, k4, v4)
            q = [q_src[qsl, :] for qsl, _, _ in slices]
            kw = [k_src[ksl, :].astype(BF16) for _, ksl, _ in slices]
            vw = [v_src[ksl, :].astype(BF16) for _, ksl, _ in slices]
            s = {(g, j): _nt((q[g] * (hm[j] * 0.125)).astype(BF16), kw[g])
                 + bias_scr[di * 6 + slices[g][2] * 2 + j, :, pl.ds(0, KW)] for g, j in chains}
            m = {c: jnp.max(s[c], axis=1, keepdims=True) for c in chains}
            p = {c: jnp.exp(s[c] - m[c]) for c in chains}
            l = {c: jnp.sum(p[c], axis=1, keepdims=True) for c in chains}
            o = {(g, j): _nn(p[(g, j)].astype(BF16), vw[g]) for g, j in chains}
            for g, (qsl, _, _) in enumerate(slices):
                o_scr[di][qsl, :] = jnp.where(lo, o[(g, 0)], o[(g, 1)])
                m_scr[di][qsl, :] = jnp.where(lo, m[(g, 0)], m[(g, 1)])
                l_scr[di][qsl, :] = jnp.where(lo, l[(g, 0)], l[(g, 1)])

        _for_groups(geom, S, (16, 16, 16), group)

        for i in range(S // COPY_ROWS):
            natural, by4 = _by4_rows(S, i)
            rows = [natural, by4, by4]
            ms = [m_scr[di][rows[di], :] for di in range(3)]
            mx = jnp.maximum(jnp.maximum(ms[0], ms[1]), ms[2])
            num = 0.0
            den = 0.0
            for di in range(3):
                w = jnp.exp(ms[di] - mx)
                num = num + w * o_scr[di][rows[di], :]
                den = den + w * l_scr[di][rows[di], :]
            a_ref[natural, :] = num / den
            lse_ref[natural, :] = mx + jnp.log(den)

    blk = lambda off: pl.BlockSpec((S, LANES), lambda h, b, off=off: (b, off + h))
    out_blk = pl.BlockSpec((S, LANES), lambda h, b: (b, h))
    return pl.pallas_call(
        body, grid=(n_pairs, B),
        in_specs=[blk(0), blk(n_pairs), blk(2 * n_pairs)],
        out_specs=[out_blk, out_blk],
        out_shape=[jax.ShapeDtypeStruct((T, ATTN_WIDTH), F32)] * 2,
        scratch_shapes=[pltpu.VMEM((18, Q_BLOCK, 2 * Q_BLOCK), F32)] + [pltpu.VMEM((S, LANES), F32)] * 12,
        compiler_params=_params(("arbitrary", "arbitrary")), name="attn_fwd")(proj, proj, proj)


def _attn_bwd(proj, a, lse, da, B, S, after=()):
    T = B * S
    geom = _attn_geometry(S)
    n_pairs = ATTN_WIDTH // LANES

    def body(q_ref, k_ref, v_ref, a_ref, lse_ref, do_ref, *rest):
        dq_ref, dk_ref, dv_ref, bias_scr = rest[len(after):len(after) + 4]
        scr = rest[len(after) + 4:]
        acc = (scr[0:3], scr[3:6])
        natural_in = (q_ref, k_ref, v_ref, a_ref, lse_ref, do_ref)
        by4_in = scr[6:12]
        _, hm = _head_masks()
        pair = pl.program_id(0)

        @pl.when(pl.program_id(1) == 0)
        def _():
            _init_bias(bias_scr, geom, pair)
        for ref in scr[0:6]:
            ref[...] = jnp.zeros_like(ref)
        for src, dst in zip(natural_in, by4_in):
            _to_by4(src, dst, S)

        def group(di, KW, all_slices):
            run = (4, 4, 8)[di]
            for first in range(0, len(all_slices), run):
                some(di, KW, all_slices[first:first + run])

        def some(di, KW, slices):
            n = len(slices)
            chains = [(g, j) for g in range(n) for j in (0, 1)]
            q_src, k_src, v_src, a_src, lse_src, do_src = natural_in if di == 0 else by4_in
            dq_scr, dk_scr, dv_scr = acc[0 if di == 0 else 1]
            q = [q_src[qsl, :] for qsl, _, _ in slices]
            do = [do_src[qsl, :] for qsl, _, _ in slices]
            doa = [do[g] * a_src[slices[g][0], :] for g in range(n)]
            lse_q = [lse_src[qsl, :] for qsl, _, _ in slices]
            kw = [k_src[ksl, :].astype(BF16) for _, ksl, _ in slices]
            vw = [v_src[ksl, :].astype(BF16) for _, ksl, _ in slices]
            qj = {(g, j): (q[g] * (hm[j] * 0.125)).astype(BF16) for g, j in chains}
            doj = {(g, j): (do[g] * hm[j]).astype(BF16) for g, j in chains}
            s = {(g, j): _nt(qj[(g, j)], kw[g])
                 + bias_scr[di * 6 + slices[g][2] * 2 + j, :, pl.ds(0, KW)] for g, j in chains}
            dp = {(g, j): _nt(doj[(g, j)], vw[g]) for g, j in chains}
            dsum = {(g, j): jnp.sum(doa[g] * hm[j], axis=1, keepdims=True) for g, j in chains}
            p = {(g, j): jnp.exp(s[(g, j)] - lse_q[g][:, HEAD_DIM * j:HEAD_DIM * j + 1]) for g, j in chains}
            ds = {c: (p[c] * (dp[c] - dsum[c])).astype(BF16) for c in chains}
            pb = {c: p[c].astype(BF16) for c in chains}
            dq = [_nn(ds[(g, 0)], kw[g]) * (hm[0] * 0.125) + _nn(ds[(g, 1)], kw[g]) * (hm[1] * 0.125)
                  for g in range(n)]
            both = lambda t, g: jnp.concatenate([t[(g, 0)], t[(g, 1)]], axis=0)
            dkw = [_tn(both(ds, g), both(qj, g)) for g in range(n)]
            dvw = [_tn(both(pb, g), both(doj, g)) for g in range(n)]
            for g, (qsl, ksl, _) in enumerate(slices):
                dq_scr[qsl, :] = dq_scr[qsl, :] + dq[g]
                dk_scr[ksl, :] = dk_scr[ksl, :] + dkw[g]
                dv_scr[ksl, :] = dv_scr[ksl, :] + dvw[g]

        _for_groups(geom, S, (16, 16, 16), group)

        for i in range(S // COPY_ROWS):
            natural, by4 = _by4_rows(S, i)
            for nat, split in zip(*acc):
                nat[natural, :] = nat[natural, :] + split[by4, :]
        for out, nat in zip((dq_ref, dk_ref, dv_ref), acc[0]):
            out[...] = nat[...].astype(BF16)

    blk = lambda off: pl.BlockSpec((S, LANES), lambda h, b, off=off: (b, off + h))
    return pl.pallas_call(
        body, grid=(n_pairs, B),
        in_specs=[blk(0), blk(n_pairs), blk(2 * n_pairs), blk(0), blk(0), blk(0)] + _after(after),
        out_specs=[blk(0), blk(0), blk(0)],
        out_shape=[jax.ShapeDtypeStruct((T, ATTN_WIDTH), BF16)] * 3,
        scratch_shapes=[pltpu.VMEM((18, Q_BLOCK, 2 * Q_BLOCK), F32)] + [pltpu.VMEM((S, LANES), F32)] * 12,
        compiler_params=_params(("arbitrary", "arbitrary")), name="attn_bwd")(proj, proj, proj, a, lse, da, *after)


def _mid(x2d, t2d, a, proj, kv, w_s, w_sT, b_tab, g_v, w_out, g_final, B, S):
    T = B * S
    tm = 512
    nt = S // tm
    halves = 2
    hrows = tm // halves

    def body(x_ref, t_ref, a_ref, za_ref, ub_ref, vb_ref, zb_ref, qm_ref, zm_ref, kv_ref,
              ws_ref, wsT_ref, btab_ref, gv_ref, wout_ref, gf_ref,
              dx2_ref, da_ref, drest_ref, loss_ref, dwout_bf_ref, dws_ref, dbs_ref, dgv_ref, dgf_ref, dkv_ref,
              dbtab_scr, dwout_ref):
        b = pl.program_id(0)
        t = pl.program_id(1)
        first = jnp.logical_and(b == 0, t == 0)
        last = jnp.logical_and(b == B - 1, t == nt - 1)
        _, hm = _head_masks()
        lane_g = lax.broadcasted_iota(jnp.int32, (1, SGU_WIDTH), 1) // HEAD_DIM
        gm = [(lane_g == g).astype(F32) for g in range(N_SGU_GROUPS)]
        H = range(halves)
        rows = [pl.ds(h * hrows, hrows) for h in H]
        ld = lambda ref: [ref[r, :] for r in rows]
        cat = lambda parts, axis: jnp.concatenate(parts, axis=axis)
        chunks = [slice(ci * SGU_CHUNK, (ci + 1) * SGU_CHUNK) for ci in range(hrows // SGU_CHUNK)]
        pairs = [slice(pr * LANES, (pr + 1) * LANES) for pr in range(2)]
        heads = [(pr, j) for pr in range(2) for j in (0, 1)]

        @pl.when(first)
        def _():
            loss_ref[...] = jnp.zeros_like(loss_ref)
            dwout_ref[...] = jnp.zeros_like(dwout_ref)
            dws_ref[...] = jnp.zeros_like(dws_ref)
            dbs_ref[...] = jnp.zeros_like(dbs_ref)
            dgv_ref[...] = jnp.zeros_like(dgv_ref)
            dgf_ref[...] = jnp.zeros_like(dgf_ref)
            dbtab_scr[...] = jnp.zeros_like(dbtab_scr)

        @pl.when(t == 0)
        def _():
            dkv_ref[...] = jnp.zeros_like(dkv_ref)

        a_val = ld(a_ref)
        sil_a = [_silu_parts(z) for z in ld(za_ref)]
        gated_a = [s[0] * a for s, a in zip(sil_a, a_val)]
        u = [_gelu_parts(z) for z in ld(ub_ref)]
        vv = [_gelu_parts(z) for z in ld(vb_ref)]
        vnorm = [_rms(v[0]) for v in vv]
        gv = gv_ref[...]
        vn = [(n[1] * gv).astype(BF16) for n in vnorm]
        w_cat = cat([ws_ref[g].astype(BF16) for g in range(N_SGU_GROUPS)], 1)
        wT_cat = cat([wsT_ref[g].astype(BF16) for g in range(N_SGU_GROUPS)], 1)
        gmb = [m.astype(BF16) for m in gm]
        by_group = lambda chunk: cat([chunk * gmb[g] for g in range(N_SGU_GROUPS)], 0)
        btab = btab_ref[...]
        mixed = [cat([btab + _nn(w_cat, by_group(vn[h][c, :])) for c in chunks], 0) for h in H]
        sg = [u[h][0] * mixed[h] for h in H]
        sil_b = [_silu_parts(z) for z in ld(zb_ref)]
        gated_b = [sil_b[h][0] * sg[h] for h in H]

        kvv = kv_ref[...].astype(BF16)
        kp = [kvv[:, p] for p in pairs]
        vp = [kvv[:, MEM_WIDTH + pr * LANES:MEM_WIDTH + (pr + 1) * LANES] for pr in range(2)]
        qm = ld(qm_ref)
        qj = {(h, pr, j): (qm[h][:, pairs[pr]] * (hm[j] * 0.125)).astype(BF16) for h in H for pr, j in heads}
        sc = {k: _nt(qj[k], kp[k[1]]) for k in qj}
        ex = {k: jnp.exp(sc[k] - jnp.max(sc[k], axis=1, keepdims=True)) for k in qj}
        prob = {k: ex[k] * (1.0 / jnp.sum(ex[k], axis=1, keepdims=True)) for k in qj}
        probb = {k: prob[k].astype(BF16) for k in qj}
        mo = [cat([sum(_nn(probb[(h, pr, j)], vp[pr]) * hm[j] for j in (0, 1)) for pr in range(2)], 1) for h in H]
        sil_m = [_silu_parts(z) for z in ld(zm_ref)]
        gated_m = [sil_m[h][0] * mo[h] for h in H]

        gated = [cat([gated_a[h], gated_b[h], gated_m[h]], 1).astype(BF16) for h in H]
        wout = wout_ref[...]
        x_in = ld(x_ref)
        x2 = [x_in[h] + _nn(gated[h], wout) for h in H]
        fin = [_rms(z) for z in x2]
        gf = gf_ref[...]
        tgt = ld(t_ref)
        err = [fin[h][1] * gf - tgt[h] for h in H]
        loss_ref[...] += sum(jnp.sum(e * e) for e in err) * (0.5 / D_MODEL)

        dy = [e * (1.0 / D_MODEL) for e in err]
        dgf_ref[...] += sum(jnp.sum(dy[h] * fin[h][1], axis=0, keepdims=True) for h in H)
        gdy = [d * gf for d in dy]
        dx2 = [fin[h][0] * (gdy[h] - fin[h][1] * jnp.mean(gdy[h] * fin[h][1], axis=1, keepdims=True)) for h in H]
        for h in H:
            dx2_ref[rows[h], :] = dx2[h]
        dx2b = [d.astype(BF16) for d in dx2]
        dgated = [_nt(d, wout) for d in dx2b]
        dwout_ref[...] += _tn(cat(gated, 0), cat(dx2b, 0))
        dga = [d[:, 0:ATTN_WIDTH] for d in dgated]
        dgb = [d[:, ATTN_WIDTH:ATTN_WIDTH + SGU_WIDTH] for d in dgated]
        dgm = [d[:, ATTN_WIDTH + SGU_WIDTH:] for d in dgated]

        for h in H:
            da_ref[rows[h], :] = dga[h] * sil_a[h][0]
        dza = [dga[h] * a_val[h] * sil_a[h][1] for h in H]

        dsg = [dgb[h] * sil_b[h][0] for h in H]
        dzb = [dgb[h] * sg[h] * sil_b[h][1] for h in H]
        dub = [dsg[h] * mixed[h] * u[h][1] for h in H]
        dmixed = [dsg[h] * u[h][0] for h in H]
        dmixed_b = [d.astype(BF16) for d in dmixed]
        dvn = [cat([_nn(wT_cat, by_group(dmixed_b[h][c, :])) for c in chunks], 0) for h in H]
        for g in range(N_SGU_GROUPS):
            dws_ref[g] += sum(_nt((dmixed[h][c, :] * gm[g]).astype(BF16), vn[h][c, :]) for h in H for c in chunks)
        dbtab_scr[...] += sum(dmixed[h][c, :] for h in H for c in chunks)
        dgv_ref[...] += sum(jnp.sum(dvn[h] * vnorm[h][1], axis=0, keepdims=True) for h in H)
        tv = [d * gv for d in dvn]
        dvv = [vnorm[h][0] * (tv[h] - vnorm[h][1] * jnp.mean(tv[h] * vnorm[h][1], axis=1, keepdims=True)) for h in H]
        dvb = [dvv[h] * vv[h][1] for h in H]

        dmo = [dgm[h] * sil_m[h][0] for h in H]
        dzm = [dgm[h] * mo[h] * sil_m[h][1] for h in H]
        dmoj = {(h, pr, j): (dmo[h][:, pairs[pr]] * hm[j]).astype(BF16) for h in H for pr, j in heads}
        dp = {k: _nt(dmoj[k], vp[k[1]]) for k in qj}
        ds = {k: (prob[k] * (dp[k] - jnp.sum(dp[k] * prob[k], axis=1, keepdims=True))).astype(BF16) for k in qj}
        dqm = [cat([sum(_nn(ds[(h, pr, j)], kp[pr]) * (hm[j] * 0.125) for j in (0, 1)) for pr in range(2)], 1)
               for h in H]
        every = lambda tbl, pr: cat([tbl[(h, pr, j)] for h in H for j in (0, 1)], 0)
        dk = [_tn(every(ds, pr), every(qj, pr)) for pr in range(2)]
        dv = [_tn(every(probb, pr), every(dmoj, pr)) for pr in range(2)]
        dkv_ref[...] += cat(dk + dv, 1)

        for h in H:
            drest_ref[rows[h], :] = cat([dza[h], dub[h], dvb[h], dzb[h], dqm[h], dzm[h]], 1).astype(BF16)

        @pl.when(last)
        def _():
            lane = lax.broadcasted_iota(jnp.int32, (1, LANES), 1)
            dbt = dbtab_scr[...]
            out = jnp.zeros((SGU_CHUNK, LANES), F32)
            for g in range(N_SGU_GROUPS):
                out = out + jnp.where(lane == g, jnp.sum(dbt * gm[g], axis=1, keepdims=True), 0.0)
            dbs_ref[...] = out
            for r0 in range(0, D_MODEL, SGU_CHUNK):
                k, row = divmod(r0, D_MODEL // N_CHIPS)
                dwout_bf_ref[k, row:row + SGU_CHUNK, :] = dwout_ref[r0:r0 + SGU_CHUNK, :].astype(BF16)

    tile = lambda w, cb: pl.BlockSpec((tm, w), lambda b, t, cb=cb: (b * nt + t, cb))
    const = lambda shape: pl.BlockSpec(shape, lambda b, t, n=len(shape): (0,) * n)
    return pl.pallas_call(
        body, grid=(B, nt),
        in_specs=[tile(D_MODEL, 0), tile(D_MODEL, 0), tile(ATTN_WIDTH, 0),
                  tile(ATTN_WIDTH, 3),
                  tile(SGU_WIDTH, 8), tile(SGU_WIDTH, 9), tile(SGU_WIDTH, 10),
                  tile(MEM_WIDTH, 11), tile(MEM_WIDTH, 12),
                  pl.BlockSpec((N_MEM, 2 * MEM_WIDTH), lambda b, t: (b, 0)),
                  const((N_SGU_GROUPS, SGU_CHUNK, SGU_CHUNK)), const((N_SGU_GROUPS, SGU_CHUNK, SGU_CHUNK)),
                  const((SGU_CHUNK, SGU_WIDTH)), const((1, SGU_WIDTH)),
                  const((D_MODEL, D_MODEL)), const((1, D_MODEL))],
        out_specs=[tile(D_MODEL, 0), tile(ATTN_WIDTH, 0), tile(REST_COLS, 0),
                   const((8, LANES)), const((N_CHIPS, D_MODEL // N_CHIPS, D_MODEL)),
                   const((N_SGU_GROUPS, SGU_CHUNK, SGU_CHUNK)), const((SGU_CHUNK, LANES)),
                   const((1, SGU_WIDTH)), const((1, D_MODEL)),
                   pl.BlockSpec((N_MEM, 2 * MEM_WIDTH), lambda b, t: (b, 0))],
        out_shape=[jax.ShapeDtypeStruct((T, D_MODEL), F32), jax.ShapeDtypeStruct((T, ATTN_WIDTH), F32),
                   jax.ShapeDtypeStruct((T, REST_COLS), BF16),
                   jax.ShapeDtypeStruct((8, LANES), F32),
                   jax.ShapeDtypeStruct((N_CHIPS, D_MODEL // N_CHIPS, D_MODEL), BF16),
                   jax.ShapeDtypeStruct((N_SGU_GROUPS, SGU_CHUNK, SGU_CHUNK), F32),
                   jax.ShapeDtypeStruct((SGU_CHUNK, LANES), F32),
                   jax.ShapeDtypeStruct((1, SGU_WIDTH), F32), jax.ShapeDtypeStruct((1, D_MODEL), F32),
                   jax.ShapeDtypeStruct((B * N_MEM, 2 * MEM_WIDTH), F32)],
        scratch_shapes=[pltpu.VMEM((SGU_CHUNK, SGU_WIDTH), F32), pltpu.VMEM((D_MODEL, D_MODEL), F32)],
        compiler_params=_params(("arbitrary", "arbitrary"), vmem=VMEM_LIMIT + 2 * 1024 * 1024), name="mid")(
            x2d, t2d, a, proj, proj, proj, proj, proj, proj, kv, w_s, w_sT, b_tab, g_v, w_out, g_final)


def _inproj_bwd_dx(dq, dk, dv, drest, x2d, dx2, g_norm, w_in_t, after=()):
    T = x2d.shape[0]
    tm = 512
    W = ATTN_WIDTH

    def body(dq_ref, dk_ref, dv_ref, dr_ref, x_ref, dx2_ref, g_ref, w_ref, *rest):
        gx_ref, dg_ref = rest[-2:]

        @pl.when(pl.program_id(0) == 0)
        def _():
            dg_ref[...] = jnp.zeros_like(dg_ref)

        halves = [pl.ds(h * (tm // 2), tm // 2) for h in (0, 1)]
        dh = [(_nn(dq_ref[r, :], w_ref[0:W, :]) + _nn(dk_ref[r, :], w_ref[W:2 * W, :])
               + _nn(dv_ref[r, :], w_ref[2 * W:3 * W, :]) + _nn(dr_ref[r, :], w_ref[QKV_COLS:IN_COLS, :]))
              for r in halves]
        nrm = [_rms(x_ref[r, :]) for r in halves]
        dg_ref[...] += sum(jnp.sum(d * n[1], axis=0, keepdims=True) for d, n in zip(dh, nrm))
        g = g_ref[...]
        for r, d, (rstd, xh) in zip(halves, dh, nrm):
            th = d * g
            gx_ref[r, :] = rstd * (th - xh * jnp.mean(th * xh, axis=1, keepdims=True)) + dx2_ref[r, :]

    tile = lambda w: pl.BlockSpec((tm, w), lambda i: (i, 0))
    return pl.pallas_call(
        body, grid=(T // tm,),
        in_specs=[tile(W), tile(W), tile(W), tile(REST_COLS), tile(D_MODEL), tile(D_MODEL),
                  pl.BlockSpec((1, D_MODEL), lambda i: (0, 0)),
                  pl.BlockSpec((IN_COLS, D_MODEL), lambda i: (0, 0))] + _after(after),
        out_specs=[tile(D_MODEL), pl.BlockSpec((1, D_MODEL), lambda i: (0, 0))],
        out_shape=[jax.ShapeDtypeStruct((T, D_MODEL), F32), jax.ShapeDtypeStruct((1, D_MODEL), F32)],
        compiler_params=_params(("arbitrary",)), name="inproj_bwd_dx")(
            dq, dk, dv, drest, x2d, dx2, g_norm, w_in_t, *after)


def _inproj_bwd_dw(dq, dk, dv, drest, x2d, g_norm, reduce_with=None):
    T = x2d.shape[0]
    tm = 512
    nt = T // tm
    W = ATTN_WIDTH
    fused = reduce_with is not None
    others = list(reduce_with) if fused else []
    ns = 1 + len(others)
    shard = IN_COLS // N_CHIPS
    halves = [shard // 2] + [s.shape[1] // 2 for s in others]
    cols = [D_MODEL] + [s.shape[2] for s in others]
    row_block = 32

    def body(dq_ref, dk_ref, dv_ref, dr_ref, x_ref, g_ref, *rest):
        if fused:
            stacks = rest[:ns - 1]
            sends, owns = rest[ns - 1:2 * ns - 1], rest[2 * ns - 1:3 * ns - 1]
            acc, ras, narrow = rest[3 * ns - 1], rest[3 * ns:4 * ns], rest[4 * ns]
            s_sem, r_sem = rest[4 * ns + 1], rest[4 * ns + 2]
            x, y, c, chip, peers, peer_chip = _place()
            sib = (x, y, 1 - c)

            def part(w, k, cc, r0=0, rows=None):
                n = halves[w]
                rows = n if rows is None else rows
                if w == 0:
                    return acc.at[pl.ds(pl.multiple_of(k * shard + cc * n + r0, 8), rows), :]
                return stacks[w - 1].at[k, pl.ds(pl.multiple_of(cc * n + r0, 8), rows), :]

            def swap_other(w):
                theirs = stacks[w - 1].at[:, pl.ds(pl.multiple_of((1 - c) * halves[w], 8), halves[w]), :]
                return _remote(theirs, ras[w], s_sem.at[N_CHIPS - 1 + w], r_sem.at[N_CHIPS - 1 + w], sib)

            def swap_win(k):
                return _remote(narrow.at[k], ras[0].at[k], s_sem.at[k], r_sem.at[k], sib)
        else:
            acc = rest[0]

        @pl.when(pl.program_id(0) == 0)
        def _():
            acc[...] = jnp.zeros_like(acc)
            for w in range(1, ns):
                swap_other(w).start()

        _, xh = _rms(x_ref[...])
        h = (xh * g_ref[...]).astype(BF16)
        acc[0:W, :] += _tn(dq_ref[...], h)
        acc[W:2 * W, :] += _tn(dk_ref[...], h)
        acc[2 * W:3 * W, :] += _tn(dv_ref[...], h)
        acc[QKV_COLS:IN_COLS, :] += _tn(dr_ref[...], h)

        if fused:
            @pl.when(pl.program_id(0) == nt - 1)
            def _():
                for k in range(N_CHIPS):
                    def to_bf16(i, carry, k=k):
                        r0 = pl.multiple_of(i * row_block, row_block)
                        narrow[k, pl.ds(r0, row_block), :] = part(0, k, 1 - c, r0, row_block)[...].astype(BF16)
                        return carry
                    lax.fori_loop(0, halves[0] // row_block, to_bf16, 0)
                    swap_win(k).start()
                def chip_sum(w, k, r0):
                    blk = pl.ds(r0, row_block)
                    return part(w, k, c, r0, row_block)[...] + ras[w][k, blk, :].astype(F32)

                for w in range(1, ns):
                    swap_other(w).wait_recv()

                    def sums(i, carry, w=w):
                        r0 = pl.multiple_of(i * row_block, row_block)
                        for m in range(3):
                            sends[w][m, pl.ds(r0, row_block), :] = chip_sum(w, peer_chip[m], r0).astype(BF16)
                        owns[w][pl.ds(r0, row_block), :] = chip_sum(w, chip, r0)
                        return carry
                    lax.fori_loop(0, halves[w] // row_block, sums, 0)
                for k in range(N_CHIPS):
                    swap_win(k).wait_recv()

                    @pl.when(chip == k)
                    def _(k=k):
                        def own(i, carry):
                            r0 = pl.multiple_of(i * row_block, row_block)
                            owns[0][pl.ds(r0, row_block), :] = chip_sum(0, k, r0)
                            return carry
                        lax.fori_loop(0, halves[0] // row_block, own, 0)

                    @pl.when(chip != k)
                    def _(k=k):
                        def other(i, carry):
                            r0 = pl.multiple_of(i * row_block, row_block)
                            sends[0][(k ^ chip) - 1, pl.ds(r0, row_block), :] = chip_sum(0, k, r0).astype(BF16)
                            return carry
                        lax.fori_loop(0, halves[0] // row_block, other, 0)
                for k in range(N_CHIPS):
                    swap_win(k).wait_send()
                for w in range(1, ns):
                    swap_other(w).wait_send()

    tile = lambda w: pl.BlockSpec((tm, w), lambda i: (i, 0))
    vmem = pl.BlockSpec(memory_space=pltpu.VMEM)
    in_specs = [tile(W), tile(W), tile(W), tile(REST_COLS), tile(D_MODEL), pl.BlockSpec((1, D_MODEL), lambda i: (0, 0))]
    if not fused:
        return pl.pallas_call(
            body, grid=(nt,), in_specs=in_specs,
            out_specs=pl.BlockSpec((IN_COLS, D_MODEL), lambda i: (0, 0)),
            out_shape=jax.ShapeDtypeStruct((IN_COLS, D_MODEL), F32),
            compiler_params=_params(("arbitrary",)), name="inproj_bwd_dw")(dq, dk, dv, drest, x2d, g_norm)
    outs = pl.pallas_call(
        body, grid=(nt,), in_specs=in_specs + [vmem] * (ns - 1), out_specs=[vmem] * (2 * ns),
        out_shape=[jax.ShapeDtypeStruct((3, n, cl), BF16) for n, cl in zip(halves, cols)]
        + [jax.ShapeDtypeStruct((n, cl), F32) for n, cl in zip(halves, cols)],
        scratch_shapes=[pltpu.VMEM((IN_COLS, D_MODEL), F32)]
        + [pltpu.VMEM((N_CHIPS, n, cl), BF16 if w == 0 else F32) for w, (n, cl) in enumerate(zip(halves, cols))]
        + [pltpu.VMEM((N_CHIPS, halves[0], D_MODEL), BF16)]
        + [pltpu.SemaphoreType.DMA((N_CHIPS - 1 + ns,)), pltpu.SemaphoreType.DMA((N_CHIPS - 1 + ns,))],
        compiler_params=_params(("arbitrary",)), name="inproj_bwd_dw_reduce")(
            dq, dk, dv, drest, x2d, g_norm, *others)
    return outs[:ns], outs[ns:]


def _adamw_update(w, g, m, v):
    nm = ADAM_B1 * m + (1.0 - ADAM_B1) * g
    nv = ADAM_B2 * v + (1.0 - ADAM_B2) * (g * g)
    m_hat = nm / (1.0 - ADAM_B1 ** ADAM_STEP)
    v_hat = nv / (1.0 - ADAM_B2 ** ADAM_STEP)
    return -ADAM_LR * (m_hat / (jnp.sqrt(v_hat) + ADAM_EPS) + ADAM_WD * w), nm, nv


def _adamw(w, g, m, v, name):
    R, C = w.shape
    br = max(r for r in range(8, 257, 8) if R % r == 0)

    def body(w_ref, g_ref, m_ref, v_ref, g_out, d_ref, nm_ref, nv_ref):
        g = g_ref[...]
        g_out[...] = g
        d_ref[...], nm_ref[...], nv_ref[...] = _adamw_update(w_ref[...], g, m_ref[...], v_ref[...])

    spec = pl.BlockSpec((br, C), lambda i: (i, 0))
    return pl.pallas_call(
        body, grid=(R // br,), in_specs=[spec] * 4, out_specs=[spec] * 4,
        out_shape=[jax.ShapeDtypeStruct((R, C), F32)] * 4,
        compiler_params=_params(("arbitrary",)), name=name)(w, g, m, v)


def _adamw_rest(g_packed, ws, ms, vs, whole):
    n = len(ws)
    nw = len(whole)
    row_block = 64

    def body(*refs):
        g_ref = refs[0]
        w_refs, m_refs, v_refs = refs[1:1 + n], refs[1 + n:1 + 2 * n], refs[1 + 2 * n:1 + 3 * n]
        whole_in = [refs[1 + 3 * n + 4 * i:5 + 3 * n + 4 * i] for i in range(nw)]
        outs = refs[1 + 3 * n + 4 * nw:]
        off = 0
        for i, (_, used, padded) in enumerate(_SMALL_PARTS[:n]):
            g = g_ref[off:off + used, :]
            delta, nm, nv = _adamw_update(w_refs[i][...], g, m_refs[i][...], v_refs[i][...])
            outs[4 * i][...], outs[4 * i + 1][...], outs[4 * i + 2][...], outs[4 * i + 3][...] = g, delta, nm, nv
            off += padded
        for i, (w_ref, gw_ref, m_ref, v_ref) in enumerate(whole_in):
            for r0 in range(0, w_ref.shape[0], row_block):
                blk = pl.ds(r0, row_block)
                g = gw_ref[blk, :]
                new = _adamw_update(w_ref[blk, :], g, m_ref[blk, :], v_ref[blk, :])
                for ref, val in zip(outs[4 * (n + i):4 * (n + i) + 4], (g,) + new):
                    ref[blk, :] = val
        outs[4 * (n + nw)][...] = g_ref[_LOSS_ROW:_LOSS_ROW + 1, 0:1]

    outs = pl.pallas_call(
        body, out_shape=[jax.ShapeDtypeStruct(w.shape, F32) for w in ws for _ in range(4)]
        + [jax.ShapeDtypeStruct(four[0].shape, F32) for four in whole for _ in range(4)]
        + [jax.ShapeDtypeStruct((1, 1), F32)],
        compiler_params=_params(), name="adamw_rest")(g_packed, *ws, *ms, *vs, *[a for four in whole for a in four])
    return ([outs[4 * i:4 * i + 4] for i in range(n)], [outs[4 * (n + i):4 * (n + i) + 4] for i in range(nw)],
            outs[4 * (n + nw)])


def _place():
    x, y, c = lax.axis_index("x"), lax.axis_index("y"), lax.axis_index("c")
    chip = 2 * x + y
    peers = [(x, 1 - y), (1 - x, y), (1 - x, 1 - y)]
    peer_chip = [2 * px + py for px, py in peers]
    return x, y, c, chip, peers, peer_chip


def _remote(src, dst, send_sem, recv_sem, dev):
    return pltpu.make_async_remote_copy(src_ref=src, dst_ref=dst, send_sem=send_sem, recv_sem=recv_sem,
                                        device_id=dev, device_id_type=MESH)


def _ag_weights(weights, late=()):
    nw, nl = len(weights), len(late)
    pieces = 2

    def body(*refs):
        srcs, late_srcs = refs[:nw], refs[nw:nw + nl]
        outs, late_bf, late_land = (refs[nw + nl:2 * nw + nl], refs[2 * nw + nl:2 * nw + 2 * nl],
                                    refs[2 * nw + 2 * nl:2 * nw + 3 * nl])
        scr = refs[2 * nw + 3 * nl:]
        wide, narrow = scr[:nw], scr[nw:2 * nw]
        in_sem, put_sem, s_ici, r_ici, s_d2d, r_d2d = scr[2 * nw:]
        x, y, c = lax.axis_index("x"), lax.axis_index("y"), lax.axis_index("c")
        chip = 2 * x + y
        sib = (x, y, 1 - c)
        first = ((x + 1 - c) % 2, (y + c) % 2)
        second = ((x + c) % 2, (y + 1 - c) % 2)
        first_chip, second_chip = 2 * first[0] + first[1], 2 * second[0] + second[1]
        diag_chip = 3 - chip

        parts = [(w, out, pc) for w, out in enumerate(outs) for pc in range(pieces)]

        def rows_of(out, cc, pc):
            rows = out.shape[1] // 2 // pieces
            return pl.ds(pl.multiple_of((cc * pieces + pc) * rows, 16), rows)

        def piece(out, k, cc, pc):
            return out.at[k, rows_of(out, cc, pc), :]

        def ici(w, slot, out, k, dev, pc, src=None):
            blk, sem = piece(out, k, c, pc), (nw * slot + w) * pieces + pc
            return _remote(blk if src is None else src, blk, s_ici.at[sem], r_ici.at[sem], (dev[0], dev[1], c))

        def d2d(w, slot, out, k, cc, pc):
            blk, sem = piece(out, k, cc, pc), (nw * slot + w) * pieces + pc
            return _remote(blk, blk, s_d2d.at[sem], r_d2d.at[sem], sib)

        def read(w, cc, pc):
            rows = rows_of(outs[w], cc, pc)
            return pltpu.make_async_copy(srcs[w].at[rows, :], wide[w].at[rows, :],
                                         in_sem.at[(w * 2 + cc) * pieces + pc])

        order = [(w, cc, pc) for cc in (0, 1) for w in range(nw) for pc in range(pieces)]
        for w, cc, pc in order:
            read(w, (c + cc) % 2, pc).start()
        sent = []
        for w, cc, pc in order[:nw * pieces]:
            rows = rows_of(outs[w], c, pc)
            read(w, c, pc).wait()
            narrow[w][rows, :] = wide[w][rows, :].astype(BF16)
            for slot, dev in enumerate((first, second)):
                sent.append(ici(w, slot, outs[w], chip, dev, pc, src=narrow[w].at[rows, :]))
                sent[-1].start()
        for src, bf, land in zip(late_srcs, late_bf, late_land):
            bf[...] = src[...].astype(BF16)
            land[...] = jnp.zeros_like(land)
            land[chip] = bf[...]
        for w, cc, pc in order[nw * pieces:]:
            rows = rows_of(outs[w], 1 - c, pc)
            read(w, 1 - c, pc).wait()
            narrow[w][rows, :] = wide[w][rows, :].astype(BF16)
        puts = [pltpu.make_async_copy(narrow[w], outs[w].at[chip], put_sem.at[w]) for w in range(nw)]
        for cp in puts:
            cp.start()
        for slot, k, dev in ((0, first_chip, first), (1, second_chip, second), (2, diag_chip, second)):
            for w, out, pc in parts:
                ici(w, slot, out, k, dev, pc).wait_recv()
                if slot == 0:
                    sent.append(ici(w, 2, out, k, second, pc))
                    sent[-1].start()
                sent.append(d2d(w, slot, out, k, c, pc))
                sent[-1].start()
        for slot, k in ((0, second_chip), (1, first_chip), (2, diag_chip)):
            for w, out, pc in parts:
                d2d(w, slot, out, k, 1 - c, pc).wait_recv()
        for cp in sent:
            cp.wait_send()
        for cp in puts:
            cp.wait()

    vmem = pl.BlockSpec(memory_space=pltpu.VMEM)
    far = pl.BlockSpec(memory_space=pl.ANY)
    outs = pl.pallas_call(
        body,
        out_shape=[jax.ShapeDtypeStruct((N_CHIPS,) + w.shape, BF16) for w in weights]
        + [jax.ShapeDtypeStruct(w.shape, BF16) for w in late]
        + [jax.ShapeDtypeStruct((N_CHIPS,) + w.shape, BF16) for w in late],
        in_specs=[far] * nw + [vmem] * nl, out_specs=[far] * nw + [vmem] * (2 * nl),
        scratch_shapes=[pltpu.VMEM(w.shape, F32) for w in weights] + [pltpu.VMEM(w.shape, BF16) for w in weights]
        + [pltpu.SemaphoreType.DMA((2 * nw * pieces,)), pltpu.SemaphoreType.DMA((nw,))]
        + [pltpu.SemaphoreType.DMA((3 * nw * pieces,))] * 4,
        compiler_params=pltpu.CompilerParams(vmem_limit_bytes=VMEM_LIMIT), name="ag_weights")(*weights, *late)
    return outs[:nw], outs[nw:nw + nl], outs[nw + nl:]


_HBM = pl.BlockSpec(memory_space=pltpu.HBM)
_SEM = pl.BlockSpec(memory_space=pltpu.SEMAPHORE)
_ANY = pl.BlockSpec(memory_space=pl.ANY)
_DATAFLOW = pltpu.SideEffectType.DATAFLOW_SIDE_EFFECTING


def _in_hbm(a):
    return pltpu.with_memory_space_constraint(a, pltpu.HBM)


_PEERS_OF = {"gather": 3, "scatter": 3, "direct": 7}


def _exchange_copies(mode, srcs, lands, send_sems, recv_sems):
    nw = len(srcs)
    x, y, c, chip, peers, peer_chip = _place()
    pairs = []
    if mode == "direct":
        targets = [((x, y), chip, 1)] + [(p, k, d) for p, k in zip(peers, peer_chip) for d in (0, 1)]
        for r, ((px, py), k, d) in enumerate(targets):
            for w in range(nw):
                sems = (send_sems.at[nw * r + w], recv_sems.at[nw * r + w], (px, py, (c + d) % 2))
                share = srcs[w].at[k if srcs[w].shape[0] > 1 else 0]
                pairs.append((_remote(share, lands[w].at[r], *sems),) * 2)
        return pairs
    gather = mode == "gather"
    for m, (px, py) in enumerate(peers):
        for w in range(nw):
            sems = (send_sems.at[nw * m + w], recv_sems.at[nw * m + w], (px, py, c))
            if gather:
                pairs.append((_remote(srcs[w], lands[w].at[chip], *sems),
                              _remote(srcs[w], lands[w].at[peer_chip[m]], *sems)))
            else:
                pairs.append((_remote(srcs[w].at[m], lands[w].at[m], *sems),) * 2)
    return pairs


def _exchange_start(mode, srcs, after, name, lands=None):
    nw = len(srcs)
    n_copies = _PEERS_OF[mode] * nw

    after = tuple(after)

    def body(*refs):
        send_sems, recv_sems = refs[2 * nw + len(after)], refs[2 * nw + len(after) + 1]
        for start, _ in _exchange_copies(mode, refs[:nw], refs[nw:2 * nw], send_sems, recv_sems):
            start.start()
        refs[-1][...] = jnp.zeros_like(refs[-1])

    if lands is None:
        shape = {"gather": lambda s: (N_CHIPS,) + s.shape, "scatter": lambda s: s.shape,
                 "direct": lambda s: (_PEERS_OF["direct"],) + s.shape[1:]}[mode]
        lands = [lax.empty(shape(s), s.dtype) for s in srcs]
    lands = [_in_hbm(l) for l in lands]
    return pl.pallas_call(
        body, name=name,
        out_shape=(pltpu.SemaphoreType.DMA((n_copies,)), pltpu.SemaphoreType.DMA((n_copies,)))
        + tuple(pltpu.HBM(s.shape, s.dtype) for s in srcs)
        + tuple(pltpu.HBM(l.shape, l.dtype) for l in lands)
        + (jax.ShapeDtypeStruct((8, LANES), F32),),
        in_specs=[_HBM] * (2 * nw) + [_ANY] * len(after),
        out_specs=(_SEM, _SEM) + (_HBM,) * (2 * nw) + (pl.BlockSpec(memory_space=pltpu.VMEM),),
        input_output_aliases={i: 2 + i for i in range(2 * nw)},
        compiler_params=pltpu.CompilerParams(has_side_effects=_DATAFLOW),
    )(*[_in_hbm(s) for s in srcs], *lands, *after)


def _exchange_wait(mode, started, after, name):
    nw = (len(started) - 3) // 2
    send_sems, recv_sems = started[0], started[1]
    thru = started[2:2 + 2 * nw]

    def body(*refs):
        for _, arrival in _exchange_copies(mode, refs[:nw], refs[nw:2 * nw], refs[2 * nw], refs[2 * nw + 1]):
            arrival.wait_send()
            arrival.wait_recv()

    outs = pl.pallas_call(
        body, name=name,
        out_shape=tuple(pltpu.HBM(t.shape, t.dtype) for t in thru),
        in_specs=[_HBM] * (2 * nw) + [_SEM, _SEM, _ANY], out_specs=(_HBM,) * (2 * nw),
        input_output_aliases={i: i for i in range(2 * nw)},
        compiler_params=pltpu.CompilerParams(has_side_effects=_DATAFLOW),
    )(*thru, send_sems, recv_sems, after)
    return outs[:nw], outs[nw:]


def _reduce_last(owns, landed, g_small, direct_srcs, direct_landed, spread_row0):
    ns, nd = len(owns), len(direct_srcs)
    halves = [o.shape[0] for o in owns]
    row_block = 16
    spread_rows = direct_srcs[-1].shape[1]
    rest0, rest1 = spread_row0, SMALL_ROWS - spread_row0 - spread_rows
    hs = (rest0 + rest1) // 2
    jobs = [(w, p * (halves[w] // 2), halves[w] // 2) for w in range(ns) for p in range(2)]
    n_swaps = len(jobs)
    jobs += [(ns + d, 0, direct_srcs[d].shape[1]) for d in range(nd)]

    def body(*refs):
        own_refs, land_refs, gsm_ref = refs[:ns], refs[ns:2 * ns], refs[2 * ns]
        dsrc_refs, dland_refs = refs[2 * ns + 1:2 * ns + 1 + nd], refs[2 * ns + 1 + nd:2 * ns + 1 + 2 * nd]
        n_in = 2 * ns + 1 + 2 * nd
        out_refs, osm_ref = refs[n_in:n_in + ns + nd - 1], refs[n_in + ns + nd - 1]
        scr = refs[n_in + ns + nd:]
        own_scr, land_scr, dsrc_scr, dland_scr = (scr[:ns], scr[ns:2 * ns], scr[2 * ns:2 * ns + nd],
                                                  scr[2 * ns + nd:2 * ns + 2 * nd])
        o_rest, ra_sm, p_sm, in_sem, s_sem, r_sem, sm_s, sm_r = scr[2 * ns + 2 * nd:]
        x, y, c, chip, peers, peer_chip = _place()
        sib = (x, y, 1 - c)
        half = lambda cc: pl.ds(pl.multiple_of(cc * hs, 8), hs)
        sm_a = _remote(gsm_ref.at[half(1 - c), :], ra_sm, sm_s.at[0], sm_r.at[0], sib)
        sm_a.start()
        swaps = [sm_a]

        def reads(j):
            w, r0, n = jobs[j]
            rows = pl.ds(r0, n)
            if w < ns:
                pairs = [(own_refs[w].at[rows, :], own_scr[w].at[rows, :]),
                         (land_refs[w].at[:, rows, :], land_scr[w].at[:, rows, :])]
            else:
                share = dsrc_refs[w - ns].at[chip if w < ns + nd - 1 else 0]
                pairs = [(share, dsrc_scr[w - ns]), (dland_refs[w - ns], dland_scr[w - ns])]
            return [pltpu.make_async_copy(s, d, in_sem.at[2 * j + i]) for i, (s, d) in enumerate(pairs)]

        for j in range(len(jobs)):
            for cp in reads(j):
                cp.start()
        sm_a.wait_recv()
        p_sm[chip] = gsm_ref[half(c), :] + ra_sm[...]
        for m, (px, py) in enumerate(peers):
            swaps.append(_remote(p_sm.at[chip], p_sm.at[chip], sm_s.at[1 + m], sm_r.at[1 + m], (px, py, c)))
            swaps[-1].start()

        def mine_of(j):
            w, r0, n = jobs[j]
            return out_refs[w].at[pl.ds(pl.multiple_of(c * halves[w] + r0, 8), n), :]

        for j, (w, r0, n) in enumerate(jobs):
            for cp in reads(j):
                cp.wait()

            def total(i, carry, w=w, r0=r0):
                rr = pl.multiple_of(r0 + i * row_block, row_block)
                blk = pl.ds(rr, row_block)
                if w < ns:
                    acc = own_scr[w][blk, :]
                    for m in range(_PEERS_OF["scatter"]):
                        acc = acc + land_scr[w][m, blk, :].astype(F32)
                    out_refs[w][pl.ds(pl.multiple_of(c * halves[w] + rr, row_block), row_block), :] = acc
                else:
                    theirs = lambda r: dland_scr[w - ns][r, blk, :].astype(F32)
                    acc = ((dsrc_scr[w - ns][blk, :].astype(F32) + theirs(0)) + (theirs(1) + theirs(2))) + (
                        (theirs(3) + theirs(4)) + (theirs(5) + theirs(6)))
                    if w < ns + nd - 1:
                        out_refs[w][blk, :] = acc
                    else:
                        osm_ref[pl.ds(pl.multiple_of(spread_row0 + rr, 8), row_block), :] = acc
                return carry
            lax.fori_loop(0, n // row_block, total, 0)
            if w < ns:
                swaps.append(_remote(mine_of(j), mine_of(j), s_sem.at[j], r_sem.at[j], sib))
                swaps[-1].start()
        for m, (px, py) in enumerate(peers):
            _remote(p_sm.at[chip], p_sm.at[peer_chip[m]], sm_s.at[1 + m], sm_r.at[1 + m], (px, py, c)).wait_recv()
        o_rest[half(c), :] = (p_sm[0] + p_sm[1]) + (p_sm[2] + p_sm[3])
        swaps.append(_remote(o_rest.at[half(c), :], o_rest.at[half(c), :], sm_s.at[4], sm_r.at[4], sib))
        swaps[-1].start()
        for j, (w, r0, n) in enumerate(jobs[:n_swaps]):
            theirs = out_refs[w].at[pl.ds(pl.multiple_of((1 - c) * halves[w] + r0, 8), n), :]
            _remote(theirs, theirs, s_sem.at[j], r_sem.at[j], sib).wait_recv()
        _remote(o_rest.at[half(1 - c), :], o_rest.at[half(1 - c), :], sm_s.at[4], sm_r.at[4], sib).wait_recv()
        osm_ref[0:rest0, :] = o_rest[0:rest0, :]
        osm_ref[SMALL_ROWS - rest1:SMALL_ROWS, :] = o_rest[rest0:rest0 + rest1, :]
        for cp in swaps:
            cp.wait_send()

    vmem = pl.BlockSpec(memory_space=pltpu.VMEM)
    far = [pl.BlockSpec(memory_space=pl.ANY)]
    return pl.pallas_call(
        body, out_shape=[jax.ShapeDtypeStruct((2 * o.shape[0], o.shape[1]), F32) for o in owns]
        + [jax.ShapeDtypeStruct(s.shape[1:], F32) for s in direct_srcs[:-1]]
        + [jax.ShapeDtypeStruct((SMALL_ROWS, LANES), F32)],
        in_specs=far * (2 * ns) + [vmem] + far * (2 * nd), out_specs=[vmem] * (ns + nd),
        scratch_shapes=[pltpu.VMEM(o.shape, o.dtype) for o in owns] + [pltpu.VMEM(l.shape, l.dtype) for l in landed]
        + [pltpu.VMEM(s.shape[1:], s.dtype) for s in direct_srcs]
        + [pltpu.VMEM(l.shape, l.dtype) for l in direct_landed]
        + [pltpu.VMEM((2 * hs, LANES), F32), pltpu.VMEM((hs, LANES), F32), pltpu.VMEM((N_CHIPS, hs, LANES), F32),
           pltpu.SemaphoreType.DMA((2 * len(jobs),)),
           pltpu.SemaphoreType.DMA((n_swaps,)), pltpu.SemaphoreType.DMA((n_swaps,)),
           pltpu.SemaphoreType.DMA((5,)), pltpu.SemaphoreType.DMA((5,))],
        compiler_params=pltpu.CompilerParams(vmem_limit_bytes=VMEM_LIMIT),
        name="reduce_last")(*owns, *landed, g_small, *direct_srcs, *direct_landed)


_SMALL_PARTS = (("g_norm", 8, 8), ("w_s", 512, 512), ("b_s", 4, 8), ("g_v", 2, 8), ("g_mem", 8, 8),
                ("g_final", 8, 8), ("loss", 8, 8))
_LOSS_ROW = SMALL_ROWS - 8
_W_S_ROW = 8
assert sum(p for _, _, p in _SMALL_PARTS) == SMALL_ROWS and _SMALL_PARTS[1][0] == "w_s"


def _pack_small(parts, loss_block):
    rows = []
    parts = dict(parts, loss=loss_block)
    for name, used, padded in _SMALL_PARTS:
        if name == "w_s":
            continue
        p = parts[name].reshape(used, LANES)
        if padded > used:
            p = jnp.pad(p, ((0, padded - used), (0, 0)))
        rows.append(p)
    return jnp.concatenate(rows, axis=0)


def _local_step(x, mem, target, g_norm, w_in, w_s, b_s, g_v, g_mem, late_weights, g_final,
                fwd_token=None, on_late=None, on_dw=None):
    B, S, _ = x.shape
    x2d = x.reshape(B * S, D_MODEL)
    t2d = target.reshape(B * S, D_MODEL)
    mem2d = mem.reshape(B * N_MEM, D_MODEL)

    proj = _inproj_fwd(x2d, g_norm, w_in, after=() if fwd_token is None else (fwd_token,))
    w_kv, w_out = late_weights(proj)
    kv = _kv_fwd(mem2d, g_mem, w_kv)
    a, lse = _attn_fwd(proj, B, S)
    w_sT = jnp.swapaxes(w_s, 1, 2)
    b_tab = jnp.repeat(b_s.T, HEAD_DIM, axis=1)
    (dx2, da, drest, loss, d_wout, d_ws, d_bs, d_gv, d_gf, dkv) = _mid(
        x2d, t2d, a, proj, kv, w_s, w_sT, b_tab, g_v, w_out, g_final, B, S)
    d_wkv, d_gmem = _kv_bwd(mem2d, g_mem, w_kv, dkv)
    dq, dk, dv = _attn_bwd(proj, a, lse, da, B, S, after=() if on_late is None else (on_late(d_wkv, d_wout, d_ws),))
    if on_dw is None:
        d_win = _inproj_bwd_dw(dq, dk, dv, drest, x2d, g_norm)
        after = ()
    else:
        d_win = None
        after = (on_dw(*_inproj_bwd_dw(dq, dk, dv, drest, x2d, g_norm, reduce_with=[])),)
    grad_x, d_gnorm = _inproj_bwd_dx(dq, dk, dv, drest, x2d, dx2, g_norm, w_in, after=after)
    d_bs = d_bs[:, :N_SGU_GROUPS].T
    return (loss, grad_x.reshape(B, S, D_MODEL),
            dict(g_norm=d_gnorm, w_in=d_win, w_s=d_ws, b_s=d_bs, g_v=d_gv, g_mem=d_gmem, w_kv=d_wkv,
                 w_out=d_wout, g_final=d_gf))


def kernel(x, mem, g_norm, w_in, w_sgu_spatial, b_sgu_spatial, g_sgu_v, g_mem, w_mem_kv, w_out, g_final, loss_target, m_g_norm, m_w_in, m_w_sgu_spatial, m_b_sgu_spatial, m_g_sgu_v, m_g_mem, m_w_mem_kv, m_w_out, m_g_final, v_g_norm, v_w_in, v_w_sgu_spatial, v_b_sgu_spatial, v_g_sgu_v, v_g_mem, v_w_mem_kv, v_w_out, v_g_final):
    t = lambda w: jnp.swapaxes(w[0], 0, 1)
    (win_all,), late_shards, late_lands = _ag_weights([t(w_in)], [w_mem_kv[0], w_out[0]])
    w_in_full = win_all.reshape(-1, win_all.shape[-1])
    late = _exchange_start("gather", list(late_shards), (win_all,), "gather_late_start", lands=late_lands)

    def late_weights(proj):
        return [z.reshape(-1, z.shape[-1]) for z in _exchange_wait("gather", late, proj, "gather_late_wait")[1]]

    scatter = {}

    def on_late(d_wkv, d_wout, d_ws):
        d_ws = d_ws.reshape(1, -1, LANES)
        scatter["late"] = _exchange_start("direct", [d_wkv, d_wout, d_ws], (), "scatter_late_start")
        return scatter["late"][-1]

    def on_dw(sends, owns):
        scatter["own"] = owns
        scatter["started"] = _exchange_start("scatter", list(sends), (owns[0],), "scatter_start")
        return scatter["started"][-1]

    loss, grad_x, g = _local_step(
        x, mem, loss_target, g_norm, w_in_full, w_sgu_spatial[0], b_sgu_spatial[0], g_sgu_v, g_mem,
        late_weights, g_final.reshape(1, D_MODEL), fwd_token=late[-1], on_late=on_late, on_dw=on_dw)

    small_names = ("g_norm", "w_s", "b_s", "g_v", "g_mem", "g_final")
    g_small = _pack_small({n: g[n] for n in small_names if n != "w_s"}, loss)
    late_srcs, late_landed = _exchange_wait("direct", scatter["late"], g_small, "scatter_late_wait")
    _, landed = _exchange_wait("scatter", scatter["started"], late_landed[0], "scatter_wait")
    gr_in, gr_kv, gr_out, gr_small = _reduce_last(scatter["own"], landed, g_small, late_srcs, late_landed, _W_S_ROW)

    small_w = (g_norm, w_sgu_spatial, b_sgu_spatial, g_sgu_v, g_mem, g_final)
    small_m = (m_g_norm, m_w_sgu_spatial, m_b_sgu_spatial, m_g_sgu_v, m_g_mem, m_g_final)
    small_v = (v_g_norm, v_w_sgu_spatial, v_b_sgu_spatial, v_g_sgu_v, v_g_mem, v_g_final)
    rows = lambda ws: [w.reshape(-1, LANES) for w in ws]
    small_new, ((gr_kv, d_kv, nm_kv, nv_kv), (gr_out, d_out, nm_out, nv_out)), loss = _adamw_rest(
        gr_small, rows(small_w), rows(small_m), rows(small_v),
        [(w_mem_kv[0], gr_kv, m_w_mem_kv[0], v_w_mem_kv[0]), (w_out[0], gr_out, m_w_out[0], v_w_out[0])])
    loss = loss.reshape(())
    small = [[z.reshape(w.shape) for z in four] for w, four in zip(small_w, small_new)]
    gr_in, d_in, nm_in, nv_in = [jnp.swapaxes(z, 0, 1)
                                 for z in _adamw(t(w_in), gr_in, t(m_w_in), t(v_w_in), "adamw_w_in")]

    def leaves(kind, big_in, big_kv, big_out):
        s_norm, s_ws, s_bs, s_gv, s_gmem, s_gf = [four[kind] for four in small]
        return [s_norm, big_in[None], s_ws, s_bs, s_gv, s_gmem, big_kv[None], big_out[None], s_gf]

    return (loss, grad_x, *leaves(0, gr_in, gr_kv, gr_out), *leaves(1, d_in, d_kv, d_out),
            *leaves(2, nm_in, nm_kv, nm_out), *leaves(3, nv_in, nv_kv, nv_out))
```

```python
import functools

import jax
import jax.numpy as jnp
from jax import lax
from jax.experimental import pallas as pl
from jax.experimental.pallas import tpu as pltpu

F32 = jnp.float32
BF16 = jnp.bfloat16
MESH = pl.DeviceIdType.MESH

D_MODEL = 1024
ATTN_WIDTH = 512
SGU_WIDTH = 256
MEM_WIDTH = 256
N_MEM = 256
IN_COLS = 3328
QKV_COLS = 3 * ATTN_WIDTH
REST_COLS = IN_COLS - QKV_COLS
SGU_CHUNK = 128
N_SGU_GROUPS = 4
EPS = 1e-6
NEG_INF = -1e30
DILATIONS = (1, 4, 16)
RADIUS = 64
Q_BLOCK = 128
LANES = 128
HEAD_DIM = 64

ADAM_LR = 0.001
ADAM_B1 = 0.9
ADAM_B2 = 0.999
ADAM_EPS = 1e-08
ADAM_WD = 0.01
ADAM_STEP = 10

N_CHIPS = 4
VMEM_LIMIT = 56 * 1024 * 1024
SMALL_ROWS = 560


def _params(sem=None, vmem=VMEM_LIMIT):
    return pltpu.CompilerParams(dimension_semantics=sem, vmem_limit_bytes=vmem)


def _nn(a, b):
    return jnp.dot(a, b, preferred_element_type=F32)


def _nt(a, b):
    return lax.dot_general(a, b, (((1,), (1,)), ((), ())), preferred_element_type=F32)


def _tn(a, b):
    return lax.dot_general(a, b, (((0,), (0,)), ((), ())), preferred_element_type=F32)


def _rms(x):
    r = lax.rsqrt(jnp.mean(x * x, axis=-1, keepdims=True) + EPS)
    return r, x * r


def _head_masks():
    lane = lax.broadcasted_iota(jnp.int32, (1, LANES), 1)
    lo = lane < HEAD_DIM
    return lo, (lo.astype(F32), (~lo).astype(F32))


def _silu_parts(z):
    s = jax.nn.sigmoid(z)
    return z * s, s * (1.0 + z * (1.0 - s))


def _gelu_parts(x):
    c = 0.7978845608028654
    x2 = x * x
    s = jax.nn.sigmoid((2.0 * c) * (x + 0.044715 * (x * x2)))
    return x * s, s * (1.0 + x * (1.0 - s) * ((2.0 * c) * (1.0 + 3.0 * 0.044715 * x2)))


def _after(tokens):
    return [pl.BlockSpec(memory_space=pl.ANY)] * len(tokens)


def _inproj_fwd(x2d, g_norm, w_in_t, after=()):
    T = x2d.shape[0]
    tm = 512

    def body(x_ref, g_ref, w_ref, *rest):
        o_ref = rest[-1]
        _, xh = _rms(x_ref[...])
        h = (xh * g_ref[...]).astype(BF16)
        o_ref[...] = _nt(h, w_ref[...])

    return pl.pallas_call(
        body, grid=(T // tm,),
        in_specs=[pl.BlockSpec((tm, D_MODEL), lambda i: (i, 0)),
                  pl.BlockSpec((1, D_MODEL), lambda i: (0, 0)),
                  pl.BlockSpec((IN_COLS, D_MODEL), lambda i: (0, 0))] + _after(after),
        out_specs=pl.BlockSpec((tm, IN_COLS), lambda i: (i, 0)),
        out_shape=jax.ShapeDtypeStruct((T, IN_COLS), F32),
        compiler_params=_params(("arbitrary",)), name="inproj_fwd")(x2d, g_norm, w_in_t, *after)


def _kv_fwd(mem2d, g_mem, w_kv):
    Tm = mem2d.shape[0]

    def body(m_ref, g_ref, w_ref, o_ref):
        _, mh = _rms(m_ref[...])
        o_ref[...] = _nn((mh * g_ref[...]).astype(BF16), w_ref[...])

    return pl.pallas_call(
        body, out_shape=jax.ShapeDtypeStruct((Tm, 2 * MEM_WIDTH), F32),
        compiler_params=_params(), name="kv_fwd")(mem2d, g_mem, w_kv)


def _kv_bwd(mem2d, g_mem, w_kv, dkv):
    Tm = mem2d.shape[0]

    def body(m_ref, g_ref, w_ref, dkv_ref, dw_ref, dg_ref):
        _, mh = _rms(m_ref[...])
        memn = (mh * g_ref[...]).astype(BF16)
        dkvb = dkv_ref[...].astype(BF16)
        dw = _tn(memn, dkvb).astype(BF16)
        for k in range(N_CHIPS):
            dw_ref[k] = dw[k * (D_MODEL // N_CHIPS):(k + 1) * (D_MODEL // N_CHIPS), :]
        dmemn = _nt(dkvb, w_ref[...])
        dg_ref[...] = jnp.sum(dmemn * mh, axis=0, keepdims=True)

    return pl.pallas_call(
        body, out_shape=(jax.ShapeDtypeStruct((N_CHIPS, D_MODEL // N_CHIPS, 2 * MEM_WIDTH), BF16),
                         jax.ShapeDtypeStruct((1, D_MODEL), F32)),
        compiler_params=_params(), name="kv_bwd")(mem2d, g_mem, w_kv, dkv)


def _attn_geometry(S):
    geom = []
    for d in DILATIONS:
        L = S // d
        assert L % Q_BLOCK == 0
        geom.append((d, L, min(2 * Q_BLOCK, L), L // Q_BLOCK))
    return geom


def _init_bias(bias_scr, geom, hp):
    row = lax.broadcasted_iota(jnp.int32, (Q_BLOCK, 2 * Q_BLOCK), 0)
    col = lax.broadcasted_iota(jnp.int32, (Q_BLOCK, 2 * Q_BLOCK), 1)
    for j in (0, 1):
        bits = (126 - (2 * hp + j)) * (1 << 23)
        slope = lax.bitcast_convert_type(jnp.full((1, 1), bits, jnp.int32), F32)
        for di, (d, _, _, _) in enumerate(geom):
            for cls, off in enumerate((0, -RADIUS, -2 * RADIUS)):
                dist = jnp.abs(col - row + off)
                bias_scr[di * 6 + cls * 2 + j] = jnp.where(
                    dist <= RADIUS, -(slope * float(d)) * dist.astype(F32), NEG_INF)


SPLIT = 4
COPY_ROWS = 256


def _by4_rows(S, step):
    per_class = S // SPLIT // COPY_ROWS
    r, j = step // per_class, step % per_class
    return (pl.ds(r + SPLIT * j * COPY_ROWS, COPY_ROWS, stride=SPLIT),
            pl.ds(r * (S // SPLIT) + j * COPY_ROWS, COPY_ROWS))


def _to_by4(src, dst, S):
    for i in range(S // COPY_ROWS):
        natural, by4 = _by4_rows(S, i)
        dst[by4, :] = src[natural, :]


def _block_slices(d, L, KW, nqb, r, qb, S):
    qs = qb * Q_BLOCK
    ks = jnp.clip(qs - RADIUS, 0, L - KW)
    cls = jnp.where(qb == 0, 0, jnp.where(qb == nqb - 1, 2, 1))
    if d == 1:
        qsl = pl.ds(pl.multiple_of(qs, Q_BLOCK), Q_BLOCK)
        ksl = pl.ds(pl.multiple_of(ks, RADIUS), KW)
    elif d == SPLIT:
        qsl = pl.ds(pl.multiple_of(r * L + qs, Q_BLOCK), Q_BLOCK)
        ksl = pl.ds(pl.multiple_of(r * L + ks, RADIUS), KW)
    else:
        sub = d // SPLIT
        base = (r % SPLIT) * (S // SPLIT) + r // SPLIT
        qsl = pl.ds(base + qs * sub, Q_BLOCK, stride=sub)
        ksl = pl.ds(base + ks * sub, KW, stride=sub)
    return qsl, ksl, cls


def _for_groups(geom, S, group, fn):
    for di, (d, L, KW, nqb) in enumerate(geom):
        n = group[di]
        assert (d * nqb) % n == 0

        def step(it, carry, di=di, d=d, L=L, KW=KW, nqb=nqb, n=n):
            slices = []
            for g in range(n):
                i = it * n + g
                slices.append(_block_slices(d, L, KW, nqb, i // nqb, i % nqb, S))
            fn(di, KW, slices)
            return carry
        lax.fori_loop(0, d * nqb // n, step, 0)


def _attn_fwd(proj, B, S):
    T = B * S
    geom = _attn_geometry(S)
    n_pairs = ATTN_WIDTH // LANES

    def body(q_ref, k_ref, v_ref, a_ref, lse_ref, bias_scr, q4, k4, v4, *per_dilation):
        o_scr, m_scr, l_scr = per_dilation[0:3], per_dilation[3:6], per_dilation[6:9]
        lo, hm = _head_masks()
        pair = pl.program_id(0)

        @pl.when(pl.program_id(1) == 0)
        def _():
            _init_bias(bias_scr, geom, pair)
        for src, dst in ((q_ref, q4), (k_ref, k4), (v_ref, v4)):
            _to_by4(src, dst, S)

        def group(di, KW, all_slices):
            run = 8
            for first in range(0, len(all_slices), run):
                some(di, KW, all_slices[first:first + run])

        def some(di, KW, slices):
            chains = [(g, j) for g in range(len(slices)) for j in (0, 1)]
            q_src, k_src, v_src = (q_ref, k_ref, v_ref) if di == 0 else (q4, k4, v4)
            q = [q_src[qsl, :] for qsl, _, _ in slices]
            kw = [k_src[ksl, :].astype(BF16) for _, ksl, _ in slices]
            vw = [v_src[ksl, :].astype(BF16) for _, ksl, _ in slices]
            s = {(g, j): _nt((q[g] * (hm[j] * 0.125)).astype(BF16), kw[g])
                 + bias_scr[di * 6 + slices[g][2] * 2 + j, :, pl.ds(0, KW)] for g, j in chains}
            m = {c: jnp.max(s[c], axis=1, keepdims=True) for c in chains}
            p = {c: jnp.exp(s[c] - m[c]) for c in chains}
            l = {c: jnp.sum(p[c], axis=1, keepdims=True) for c in chains}
            o = {(g, j): _nn(p[(g, j)].astype(BF16), vw[g]) for g, j in chains}
            for g, (qsl, _, _) in enumerate(slices):
                o_scr[di][qsl, :] = jnp.where(lo, o[(g, 0)], o[(g, 1)])
                m_scr[di][qsl, :] = jnp.where(lo, m[(g, 0)], m[(g, 1)])
                l_scr[di][qsl, :] = jnp.where(lo, l[(g, 0)], l[(g, 1)])

        _for_groups(geom, S, (16, 16, 16), group)

        for i in range(S // COPY_ROWS):
            natural, by4 = _by4_rows(S, i)
            rows = [natural, by4, by4]
            ms = [m_scr[di][rows[di], :] for di in range(3)]
            mx = jnp.maximum(jnp.maximum(ms[0], ms[1]), ms[2])
            num = 0.0
            den = 0.0
            for di in range(3):
                w = jnp.exp(ms[di] - mx)
                num = num + w * o_scr[di][rows[di], :]
                den = den + w * l_scr[di][rows[di], :]
            a_ref[natural, :] = num / den
            lse_ref[natural, :] = mx + jnp.log(den)

    blk = lambda off: pl.BlockSpec((S, LANES), lambda h, b, off=off: (b, off + h))
    out_blk = pl.BlockSpec((S, LANES), lambda h, b: (b, h))
    return pl.pallas_call(
        body, grid=(n_pairs, B),
        in_specs=[blk(0), blk(n_pairs), blk(2 * n_pairs)],
        out_specs=[out_blk, out_blk],
        out_shape=[jax.ShapeDtypeStruct((T, ATTN_WIDTH), F32)] * 2,
        scratch_shapes=[pltpu.VMEM((18, Q_BLOCK, 2 * Q_BLOCK), F32)] + [pltpu.VMEM((S, LANES), F32)] * 12,
        compiler_params=_params(("arbitrary", "arbitrary")), name="attn_fwd")(proj, proj, proj)


def _attn_bwd(proj, a, lse, da, B, S, after=()):
    T = B * S
    geom = _attn_geometry(S)
    n_pairs = ATTN_WIDTH // LANES

    def body(q_ref, k_ref, v_ref, a_ref, lse_ref, do_ref, *rest):
        dq_ref, dk_ref, dv_ref, bias_scr = rest[len(after):len(after) + 4]
        scr = rest[len(after) + 4:]
        acc = (scr[0:3], scr[3:6])
        natural_in = (q_ref, k_ref, v_ref, a_ref, lse_ref, do_ref)
        by4_in = scr[6:12]
        _, hm = _head_masks()
        pair = pl.program_id(0)

        @pl.when(pl.program_id(1) == 0)
        def _():
            _init_bias(bias_scr, geom, pair)
        for ref in scr[0:6]:
            ref[...] = jnp.zeros_like(ref)
        for src, dst in zip(natural_in, by4_in):
            _to_by4(src, dst, S)

        def group(di, KW, all_slices):
            run = (4, 4, 8)[di]
            for first in range(0, len(all_slices), run):
                some(di, KW, all_slices[first:first + run])

        def some(di, KW, slices):
            n = len(slices)
            chains = [(g, j) for g in range(n) for j in (0, 1)]
            q_src, k_src, v_src, a_src, lse_src, do_src = natural_in if di == 0 else by4_in
            dq_scr, dk_scr, dv_scr = acc[0 if di == 0 else 1]
            q = [q_src[qsl, :] for qsl, _, _ in slices]
            do = [do_src[qsl, :] for qsl, _, _ in slices]
            doa = [do[g] * a_src[slices[g][0], :] for g in range(n)]
            lse_q = [lse_src[qsl, :] for qsl, _, _ in slices]
            kw = [k_src[ksl, :].astype(BF16) for _, ksl, _ in slices]
            vw = [v_src[ksl, :].astype(BF16) for _, ksl, _ in slices]
            qj = {(g, j): (q[g] * (hm[j] * 0.125)).astype(BF16) for g, j in chains}
            doj = {(g, j): (do[g] * hm[j]).astype(BF16) for g, j in chains}
            s = {(g, j): _nt(qj[(g, j)], kw[g])
                 + bias_scr[di * 6 + slices[g][2] * 2 + j, :, pl.ds(0, KW)] for g, j in chains}
            dp = {(g, j): _nt(doj[(g, j)], vw[g]) for g, j in chains}
            dsum = {(g, j): jnp.sum(doa[g] * hm[j], axis=1, keepdims=True) for g, j in chains}
            p = {(g, j): jnp.exp(s[(g, j)] - lse_q[g][:, HEAD_DIM * j:HEAD_DIM * j + 1]) for g, j in chains}
            ds = {c: (p[c] * (dp[c] - dsum[c])).astype(BF16) for c in chains}
            pb = {c: p[c].astype(BF16) for c in chains}
            dq = [_nn(ds[(g, 0)], kw[g]) * (hm[0] * 0.125) + _nn(ds[(g, 1)], kw[g]) * (hm[1] * 0.125)
                  for g in range(n)]
            both = lambda t, g: jnp.concatenate([t[(g, 0)], t[(g, 1)]], axis=0)
            dkw = [_tn(both(ds, g), both(qj, g)) for g in range(n)]
            dvw = [_tn(both(pb, g), both(doj, g)) for g in range(n)]
            for g, (qsl, ksl, _) in enumerate(slices):
                dq_scr[qsl, :] = dq_scr[qsl, :] + dq[g]
                dk_scr[ksl, :] = dk_scr[ksl, :] + dkw[g]
                dv_scr[ksl, :] = dv_scr[ksl, :] + dvw[g]

        _for_groups(geom, S, (16, 16, 16), group)

        for i in range(S // COPY_ROWS):
            natural, by4 = _by4_rows(S, i)
            for nat, split in zip(*acc):
                nat[natural, :] = nat[natural, :] + split[by4, :]
        for out, nat in zip((dq_ref, dk_ref, dv_ref), acc[0]):
            out[...] = nat[...].astype(BF16)

    blk = lambda off: pl.BlockSpec((S, LANES), lambda h, b, off=off: (b, off + h))
    return pl.pallas_call(
        body, grid=(n_pairs, B),
        in_specs=[blk(0), blk(n_pairs), blk(2 * n_pairs), blk(0), blk(0), blk(0)] + _after(after),
        out_specs=[blk(0), blk(0), blk(0)],
        out_shape=[jax.ShapeDtypeStruct((T, ATTN_WIDTH), BF16)] * 3,
        scratch_shapes=[pltpu.VMEM((18, Q_BLOCK, 2 * Q_BLOCK), F32)] + [pltpu.VMEM((S, LANES), F32)] * 12,
        compiler_params=_params(("arbitrary", "arbitrary")), name="attn_bwd")(proj, proj, proj, a, lse, da, *after)


def _mid(x2d, t2d, a, proj, kv, w_s, w_sT, b_tab, g_v, w_out, g_final, B, S):
    T = B * S
    tm = 512
    nt = S // tm
    halves = 2
    hrows = tm // halves

    def body(x_ref, t_ref, a_ref, za_ref, ub_ref, vb_ref, zb_ref, qm_ref, zm_ref, kv_ref,
              ws_ref, wsT_ref, btab_ref, gv_ref, wout_ref, gf_ref,
              dx2_ref, da_ref, drest_ref, loss_ref, dwout_bf_ref, dws_ref, dbs_ref, dgv_ref, dgf_ref, dkv_ref,
              dbtab_scr, dwout_ref):
        b = pl.program_id(0)
        t = pl.program_id(1)
        first = jnp.logical_and(b == 0, t == 0)
        last = jnp.logical_and(b == B - 1, t == nt - 1)
        _, hm = _head_masks()
        lane_g = lax.broadcasted_iota(jnp.int32, (1, SGU_WIDTH), 1) // HEAD_DIM
        gm = [(lane_g == g).astype(F32) for g in range(N_SGU_GROUPS)]
        H = range(halves)
        rows = [pl.ds(h * hrows, hrows) for h in H]
        ld = lambda ref: [ref[r, :] for r in rows]
        cat = lambda parts, axis: jnp.concatenate(parts, axis=axis)
        chunks = [slice(ci * SGU_CHUNK, (ci + 1) * SGU_CHUNK) for ci in range(hrows // SGU_CHUNK)]
        pairs = [slice(pr * LANES, (pr + 1) * LANES) for pr in range(2)]
        heads = [(pr, j) for pr in range(2) for j in (0, 1)]

        @pl.when(first)
        def _():
            loss_ref[...] = jnp.zeros_like(loss_ref)
            dwout_ref[...] = jnp.zeros_like(dwout_ref)
            dws_ref[...] = jnp.zeros_like(dws_ref)
            dbs_ref[...] = jnp.zeros_like(dbs_ref)
            dgv_ref[...] = jnp.zeros_like(dgv_ref)
            dgf_ref[...] = jnp.zeros_like(dgf_ref)
            dbtab_scr[...] = jnp.zeros_like(dbtab_scr)

        @pl.when(t == 0)
        def _():
            dkv_ref[...] = jnp.zeros_like(dkv_ref)

        a_val = ld(a_ref)
        sil_a = [_silu_parts(z) for z in ld(za_ref)]
        gated_a = [s[0] * a for s, a in zip(sil_a, a_val)]
        u = [_gelu_parts(z) for z in ld(ub_ref)]
        vv = [_gelu_parts(z) for z in ld(vb_ref)]
        vnorm = [_rms(v[0]) for v in vv]
        gv = gv_ref[...]
        vn = [(n[1] * gv).astype(BF16) for n in vnorm]
        w_cat = cat([ws_ref[g].astype(BF16) for g in range(N_SGU_GROUPS)], 1)
        wT_cat = cat([wsT_ref[g].astype(BF16) for g in range(N_SGU_GROUPS)], 1)
        gmb = [m.astype(BF16) for m in gm]
        by_group = lambda chunk: cat([chunk * gmb[g] for g in range(N_SGU_GROUPS)], 0)
        btab = btab_ref[...]
        mixed = [cat([btab + _nn(w_cat, by_group(vn[h][c, :])) for c in chunks], 0) for h in H]
        sg = [u[h][0] * mixed[h] for h in H]
        sil_b = [_silu_parts(z) for z in ld(zb_ref)]
        gated_b = [sil_b[h][0] * sg[h] for h in H]

        kvv = kv_ref[...].astype(BF16)
        kp = [kvv[:, p] for p in pairs]
        vp = [kvv[:, MEM_WIDTH + pr * LANES:MEM_WIDTH + (pr + 1) * LANES] for pr in range(2)]
        qm = ld(qm_ref)
        qj = {(h, pr, j): (qm[h][:, pairs[pr]] * (hm[j] * 0.125)).astype(BF16) for h in H for pr, j in heads}
        sc = {k: _nt(qj[k], kp[k[1]]) for k in qj}
        ex = {k: jnp.exp(sc[k] - jnp.max(sc[k], axis=1, keepdims=True)) for k in qj}
        prob = {k: ex[k] * (1.0 / jnp.sum(ex[k], axis=1, keepdims=True)) for k in qj}
        probb = {k: prob[k].astype(BF16) for k in qj}
        mo = [cat([sum(_nn(probb[(h, pr, j)], vp[pr]) * hm[j] for j in (0, 1)) for pr in range(2)], 1) for h in H]
        sil_m = [_silu_parts(z) for z in ld(zm_ref)]
        gated_m = [sil_m[h][0] * mo[h] for h in H]

        gated = [cat([gated_a[h], gated_b[h], gated_m[h]], 1).astype(BF16) for h in H]
        wout = wout_ref[...]
        x_in = ld(x_ref)
        x2 = [x_in[h] + _nn(gated[h], wout) for h in H]
        fin = [_rms(z) for z in x2]
        gf = gf_ref[...]
        tgt = ld(t_ref)
        err = [fin[h][1] * gf - tgt[h] for h in H]
        loss_ref[...] += sum(jnp.sum(e * e) for e in err) * (0.5 / D_MODEL)

        dy = [e * (1.0 / D_MODEL) for e in err]
        dgf_ref[...] += sum(jnp.sum(dy[h] * fin[h][1], axis=0, keepdims=True) for h in H)
        gdy = [d * gf for d in dy]
        dx2 = [fin[h][0] * (gdy[h] - fin[h][1] * jnp.mean(gdy[h] * fin[h][1], axis=1, keepdims=True)) for h in H]
        for h in H:
            dx2_ref[rows[h], :] = dx2[h]
        dx2b = [d.astype(BF16) for d in dx2]
        dgated = [_nt(d, wout) for d in dx2b]
        dwout_ref[...] += _tn(cat(gated, 0), cat(dx2b, 0))
        dga = [d[:, 0:ATTN_WIDTH] for d in dgated]
        dgb = [d[:, ATTN_WIDTH:ATTN_WIDTH + SGU_WIDTH] for d in dgated]
        dgm = [d[:, ATTN_WIDTH + SGU_WIDTH:] for d in dgated]

        for h in H:
            da_ref[rows[h], :] = dga[h] * sil_a[h][0]
        dza = [dga[h] * a_val[h] * sil_a[h][1] for h in H]

        dsg = [dgb[h] * sil_b[h][0] for h in H]
        dzb = [dgb[h] * sg[h] * sil_b[h][1] for h in H]
        dub = [dsg[h] * mixed[h] * u[h][1] for h in H]
        dmixed = [dsg[h] * u[h][0] for h in H]
        dmixed_b = [d.astype(BF16) for d in dmixed]
        dvn = [cat([_nn(wT_cat, by_group(dmixed_b[h][c, :])) for c in chunks], 0) for h in H]
        for g in range(N_SGU_GROUPS):
            dws_ref[g] += sum(_nt((dmixed[h][c, :] * gm[g]).astype(BF16), vn[h][c, :]) for h in H for c in chunks)
        dbtab_scr[...] += sum(dmixed[h][c, :] for h in H for c in chunks)
        dgv_ref[...] += sum(jnp.sum(dvn[h] * vnorm[h][1], axis=0, keepdims=True) for h in H)
        tv = [d * gv for d in dvn]
        dvv = [vnorm[h][0] * (tv[h] - vnorm[h][1] * jnp.mean(tv[h] * vnorm[h][1], axis=1, keepdims=True)) for h in H]
        dvb = [dvv[h] * vv[h][1] for h in H]

        dmo = [dgm[h] * sil_m[h][0] for h in H]
        dzm = [dgm[h] * mo[h] * sil_m[h][1] for h in H]
        dmoj = {(h, pr, j): (dmo[h][:, pairs[pr]] * hm[j]).astype(BF16) for h in H for pr, j in heads}
        dp = {k: _nt(dmoj[k], vp[k[1]]) for k in qj}
        ds = {k: (prob[k] * (dp[k] - jnp.sum(dp[k] * prob[k], axis=1, keepdims=True))).astype(BF16) for k in qj}
        dqm = [cat([sum(_nn(ds[(h, pr, j)], kp[pr]) * (hm[j] * 0.125) for j in (0, 1)) for pr in range(2)], 1)
               for h in H]
        every = lambda tbl, pr: cat([tbl[(h, pr, j)] for h in H for j in (0, 1)], 0)
        dk = [_tn(every(ds, pr), every(qj, pr)) for pr in range(2)]
        dv = [_tn(every(probb, pr), every(dmoj, pr)) for pr in range(2)]
        dkv_ref[...] += cat(dk + dv, 1)

        for h in H:
            drest_ref[rows[h], :] = cat([dza[h], dub[h], dvb[h], dzb[h], dqm[h], dzm[h]], 1).astype(BF16)

        @pl.when(last)
        def _():
            lane = lax.broadcasted_iota(jnp.int32, (1, LANES), 1)
            dbt = dbtab_scr[...]
            out = jnp.zeros((SGU_CHUNK, LANES), F32)
            for g in range(N_SGU_GROUPS):
                out = out + jnp.where(lane == g, jnp.sum(dbt * gm[g], axis=1, keepdims=True), 0.0)
            dbs_ref[...] = out
            for r0 in range(0, D_MODEL, SGU_CHUNK):
                k, row = divmod(r0, D_MODEL // N_CHIPS)
                dwout_bf_ref[k, row:row + SGU_CHUNK, :] = dwout_ref[r0:r0 + SGU_CHUNK, :].astype(BF16)

    tile = lambda w, cb: pl.BlockSpec((tm, w), lambda b, t, cb=cb: (b * nt + t, cb))
    const = lambda shape: pl.BlockSpec(shape, lambda b, t, n=len(shape): (0,) * n)
    return pl.pallas_call(
        body, grid=(B, nt),
        in_specs=[tile(D_MODEL, 0), tile(D_MODEL, 0), tile(ATTN_WIDTH, 0),
                  tile(ATTN_WIDTH, 3),
                  tile(SGU_WIDTH, 8), tile(SGU_WIDTH, 9), tile(SGU_WIDTH, 10),
                  tile(MEM_WIDTH, 11), tile(MEM_WIDTH, 12),
                  pl.BlockSpec((N_MEM, 2 * MEM_WIDTH), lambda b, t: (b, 0)),
                  const((N_SGU_GROUPS, SGU_CHUNK, SGU_CHUNK)), const((N_SGU_GROUPS, SGU_CHUNK, SGU_CHUNK)),
                  const((SGU_CHUNK, SGU_WIDTH)), const((1, SGU_WIDTH)),
                  const((D_MODEL, D_MODEL)), const((1, D_MODEL))],
        out_specs=[tile(D_MODEL, 0), tile(ATTN_WIDTH, 0), tile(REST_COLS, 0),
                   const((8, LANES)), const((N_CHIPS, D_MODEL // N_CHIPS, D_MODEL)),
                   const((N_SGU_GROUPS, SGU_CHUNK, SGU_CHUNK)), const((SGU_CHUNK, LANES)),
                   const((1, SGU_WIDTH)), const((1, D_MODEL)),
                   pl.BlockSpec((N_MEM, 2 * MEM_WIDTH), lambda b, t: (b, 0))],
        out_shape=[jax.ShapeDtypeStruct((T, D_MODEL), F32), jax.ShapeDtypeStruct((T, ATTN_WIDTH), F32),
                   jax.ShapeDtypeStruct((T, REST_COLS), BF16),
                   jax.ShapeDtypeStruct((8, LANES), F32),
                   jax.ShapeDtypeStruct((N_CHIPS, D_MODEL // N_CHIPS, D_MODEL), BF16),
                   jax.ShapeDtypeStruct((N_SGU_GROUPS, SGU_CHUNK, SGU_CHUNK), F32),
                   jax.ShapeDtypeStruct((SGU_CHUNK, LANES), F32),
                   jax.ShapeDtypeStruct((1, SGU_WIDTH), F32), jax.ShapeDtypeStruct((1, D_MODEL), F32),
                   jax.ShapeDtypeStruct((B * N_MEM, 2 * MEM_WIDTH), F32)],
        scratch_shapes=[pltpu.VMEM((SGU_CHUNK, SGU_WIDTH), F32), pltpu.VMEM((D_MODEL, D_MODEL), F32)],
        compiler_params=_params(("arbitrary", "arbitrary"), vmem=VMEM_LIMIT + 2 * 1024 * 1024), name="mid")(
            x2d, t2d, a, proj, proj, proj, proj, proj, proj, kv, w_s, w_sT, b_tab, g_v, w_out, g_final)


def _inproj_bwd_dx(dq, dk, dv, drest, x2d, dx2, g_norm, w_in_t, after=()):
    T = x2d.shape[0]
    tm = 512
    W = ATTN_WIDTH

    def body(dq_ref, dk_ref, dv_ref, dr_ref, x_ref, dx2_ref, g_ref, w_ref, *rest):
        gx_ref, dg_ref = rest[-2:]

        @pl.when(pl.program_id(0) == 0)
        def _():
            dg_ref[...] = jnp.zeros_like(dg_ref)

        halves = [pl.ds(h * (tm // 2), tm // 2) for h in (0, 1)]
        dh = [(_nn(dq_ref[r, :], w_ref[0:W, :]) + _nn(dk_ref[r, :], w_ref[W:2 * W, :])
               + _nn(dv_ref[r, :], w_ref[2 * W:3 * W, :]) + _nn(dr_ref[r, :], w_ref[QKV_COLS:IN_COLS, :]))
              for r in halves]
        nrm = [_rms(x_ref[r, :]) for r in halves]
        dg_ref[...] += sum(jnp.sum(d * n[1], axis=0, keepdims=True) for d, n in zip(dh, nrm))
        g = g_ref[...]
        for r, d, (rstd, xh) in zip(halves, dh, nrm):
            th = d * g
            gx_ref[r, :] = rstd * (th - xh * jnp.mean(th * xh, axis=1, keepdims=True)) + dx2_ref[r, :]

    tile = lambda w: pl.BlockSpec((tm, w), lambda i: (i, 0))
    return pl.pallas_call(
        body, grid=(T // tm,),
        in_specs=[tile(W), tile(W), tile(W), tile(REST_COLS), tile(D_MODEL), tile(D_MODEL),
                  pl.BlockSpec((1, D_MODEL), lambda i: (0, 0)),
                  pl.BlockSpec((IN_COLS, D_MODEL), lambda i: (0, 0))] + _after(after),
        out_specs=[tile(D_MODEL), pl.BlockSpec((1, D_MODEL), lambda i: (0, 0))],
        out_shape=[jax.ShapeDtypeStruct((T, D_MODEL), F32), jax.ShapeDtypeStruct((1, D_MODEL), F32)],
        compiler_params=_params(("arbitrary",)), name="inproj_bwd_dx")(
            dq, dk, dv, drest, x2d, dx2, g_norm, w_in_t, *after)


def _inproj_bwd_dw(dq, dk, dv, drest, x2d, g_norm, reduce_with=None):
    T = x2d.shape[0]
    tm = 512
    nt = T // tm
    W = ATTN_WIDTH
    fused = reduce_with is not None
    others = list(reduce_with) if fused else []
    ns = 1 + len(others)
    shard = IN_COLS // N_CHIPS
    halves = [shard // 2] + [s.shape[1] // 2 for s in others]
    cols = [D_MODEL] + [s.shape[2] for s in others]
    row_block = 32

    def body(dq_ref, dk_ref, dv_ref, dr_ref, x_ref, g_ref, *rest):
        if fused:
            stacks = rest[:ns - 1]
            sends, owns = rest[ns - 1:2 * ns - 1], rest[2 * ns - 1:3 * ns - 1]
            acc, ras, narrow = rest[3 * ns - 1], rest[3 * ns:4 * ns], rest[4 * ns]
            s_sem, r_sem = rest[4 * ns + 1], rest[4 * ns + 2]
            x, y, c, chip, peers, peer_chip = _place()
            sib = (x, y, 1 - c)

            def part(w, k, cc, r0=0, rows=None):
                n = halves[w]
                rows = n if rows is None else rows
                if w == 0:
                    return acc.at[pl.ds(pl.multiple_of(k * shard + cc * n + r0, 8), rows), :]
                return stacks[w - 1].at[k, pl.ds(pl.multiple_of(cc * n + r0, 8), rows), :]

            def swap_other(w):
                theirs = stacks[w - 1].at[:, pl.ds(pl.multiple_of((1 - c) * halves[w], 8), halves[w]), :]
                return _remote(theirs, ras[w], s_sem.at[N_CHIPS - 1 + w], r_sem.at[N_CHIPS - 1 + w], sib)

            def swap_win(k):
                return _remote(narrow.at[k], ras[0].at[k], s_sem.at[k], r_sem.at[k], sib)
        else:
            acc = rest[0]

        @pl.when(pl.program_id(0) == 0)
        def _():
            acc[...] = jnp.zeros_like(acc)
            for w in range(1, ns):
                swap_other(w).start()

        _, xh = _rms(x_ref[...])
        h = (xh * g_ref[...]).astype(BF16)
        acc[0:W, :] += _tn(dq_ref[...], h)
        acc[W:2 * W, :] += _tn(dk_ref[...], h)
        acc[2 * W:3 * W, :] += _tn(dv_ref[...], h)
        acc[QKV_COLS:IN_COLS, :] += _tn(dr_ref[...], h)

        if fused:
            @pl.when(pl.program_id(0) == nt - 1)
            def _():
                for k in range(N_CHIPS):
                    def to_bf16(i, carry, k=k):
                        r0 = pl.multiple_of(i * row_block, row_block)
                        narrow[k, pl.ds(r0, row_block), :] = part(0, k, 1 - c, r0, row_block)[...].astype(BF16)
                        return carry
                    lax.fori_loop(0, halves[0] // row_block, to_bf16, 0)
                    swap_win(k).start()
                def chip_sum(w, k, r0):
                    blk = pl.ds(r0, row_block)
                    return part(w, k, c, r0, row_block)[...] + ras[w][k, blk, :].astype(F32)

                for w in range(1, ns):
                    swap_other(w).wait_recv()

                    def sums(i, carry, w=w):
                        r0 = pl.multiple_of(i * row_block, row_block)
                        for m in range(3):
                            sends[w][m, pl.ds(r0, row_block), :] = chip_sum(w, peer_chip[m], r0).astype(BF16)
                        owns[w][pl.ds(r0, row_block), :] = chip_sum(w, chip, r0)
                        return carry
                    lax.fori_loop(0, halves[w] // row_block, sums, 0)
                for k in range(N_CHIPS):
                    swap_win(k).wait_recv()

                    @pl.when(chip == k)
                    def _(k=k):
                        def own(i, carry):
                            r0 = pl.multiple_of(i * row_block, row_block)
                            owns[0][pl.ds(r0, row_block), :] = chip_sum(0, k, r0)
                            return carry
                        lax.fori_loop(0, halves[0] // row_block, own, 0)

                    @pl.when(chip != k)
                    def _(k=k):
                        def other(i, carry):
                            r0 = pl.multiple_of(i * row_block, row_block)
                            sends[0][(k ^ chip) - 1, pl.ds(r0, row_block), :] = chip_sum(0, k, r0).astype(BF16)
                            return carry
                        lax.fori_loop(0, halves[0] // row_block, other, 0)
                for k in range(N_CHIPS):
                    swap_win(k).wait_send()
                for w in range(1, ns):
                    swap_other(w).wait_send()

    tile = lambda w: pl.BlockSpec((tm, w), lambda i: (i, 0))
    vmem = pl.BlockSpec(memory_space=pltpu.VMEM)
    in_specs = [tile(W), tile(W), tile(W), tile(REST_COLS), tile(D_MODEL), pl.BlockSpec((1, D_MODEL), lambda i: (0, 0))]
    if not fused:
        return pl.pallas_call(
            body, grid=(nt,), in_specs=in_specs,
            out_specs=pl.BlockSpec((IN_COLS, D_MODEL), lambda i: (0, 0)),
            out_shape=jax.ShapeDtypeStruct((IN_COLS, D_MODEL), F32),
            compiler_params=_params(("arbitrary",)), name="inproj_bwd_dw")(dq, dk, dv, drest, x2d, g_norm)
    outs = pl.pallas_call(
        body, grid=(nt,), in_specs=in_specs + [vmem] * (ns - 1), out_specs=[vmem] * (2 * ns),
        out_shape=[jax.ShapeDtypeStruct((3, n, cl), BF16) for n, cl in zip(halves, cols)]
        + [jax.ShapeDtypeStruct((n, cl), F32) for n, cl in zip(halves, cols)],
        scratch_shapes=[pltpu.VMEM((IN_COLS, D_MODEL), F32)]
        + [pltpu.VMEM((N_CHIPS, n, cl), BF16 if w == 0 else F32) for w, (n, cl) in enumerate(zip(halves, cols))]
        + [pltpu.VMEM((N_CHIPS, halves[0], D_MODEL), BF16)]
        + [pltpu.SemaphoreType.DMA((N_CHIPS - 1 + ns,)), pltpu.SemaphoreType.DMA((N_CHIPS - 1 + ns,))],
        compiler_params=_params(("arbitrary",)), name="inproj_bwd_dw_reduce")(
            dq, dk, dv, drest, x2d, g_norm, *others)
    return outs[:ns], outs[ns:]


def _adamw_update(w, g, m, v):
    nm = ADAM_B1 * m + (1.0 - ADAM_B1) * g
    nv = ADAM_B2 * v + (1.0 - ADAM_B2) * (g * g)
    m_hat = nm / (1.0 - ADAM_B1 ** ADAM_STEP)
    v_hat = nv / (1.0 - ADAM_B2 ** ADAM_STEP)
    return -ADAM_LR * (m_hat / (jnp.sqrt(v_hat) + ADAM_EPS) + ADAM_WD * w), nm, nv


def _adamw(w, g, m, v, name):
    R, C = w.shape
    br = max(r for r in range(8, 257, 8) if R % r == 0)

    def body(w_ref, g_ref, m_ref, v_ref, g_out, d_ref, nm_ref, nv_ref):
        g = g_ref[...]
        g_out[...] = g
        d_ref[...], nm_ref[...], nv_ref[...] = _adamw_update(w_ref[...], g, m_ref[...], v_ref[...])

    spec = pl.BlockSpec((br, C), lambda i: (i, 0))
    return pl.pallas_call(
        body, grid=(R // br,), in_specs=[spec] * 4, out_specs=[spec] * 4,
        out_shape=[jax.ShapeDtypeStruct((R, C), F32)] * 4,
        compiler_params=_params(("arbitrary",)), name=name)(w, g, m, v)


def _adamw_rest(g_packed, ws, ms, vs, whole):
    n = len(ws)
    nw = len(whole)
    steps = 4

    def body(*refs):
        g_ref = refs[0]
        w_refs, m_refs, v_refs = refs[1:1 + n], refs[1 + n:1 + 2 * n], refs[1 + 2 * n:1 + 3 * n]
        whole_in = [refs[1 + 3 * n + 4 * i:5 + 3 * n + 4 * i] for i in range(nw)]
        outs = refs[1 + 3 * n + 4 * nw:]

        @pl.when(pl.program_id(0) == 0)
        def _():
            off = 0
            for i, (_, used, padded) in enumerate(_SMALL_PARTS[:n]):
                g = g_ref[off:off + used, :]
                delta, nm, nv = _adamw_update(w_refs[i][...], g, m_refs[i][...], v_refs[i][...])
                outs[4 * i][...], outs[4 * i + 1][...], outs[4 * i + 2][...], outs[4 * i + 3][...] = g, delta, nm, nv
                off += padded
            outs[4 * (n + nw)][...] = g_ref[_LOSS_ROW:_LOSS_ROW + 1, 0:1]

        for i, (w_ref, gw_ref, m_ref, v_ref) in enumerate(whole_in):
            g = gw_ref[...]
            new = _adamw_update(w_ref[...], g, m_ref[...], v_ref[...])
            for ref, val in zip(outs[4 * (n + i):4 * (n + i) + 4], (g,) + new):
                ref[...] = val

    kept = lambda a: pl.BlockSpec(a.shape, lambda i: (0, 0))
    rows = lambda a: pl.BlockSpec((a.shape[0] // steps, a.shape[1]), lambda i: (i, 0))
    outs = pl.pallas_call(
        body, grid=(steps,),
        in_specs=[kept(g_packed)] + [kept(a) for a in list(ws) + list(ms) + list(vs)]
        + [rows(a) for four in whole for a in four],
        out_specs=[kept(w) for w in ws for _ in range(4)] + [rows(four[0]) for four in whole for _ in range(4)]
        + [pl.BlockSpec((1, 1), lambda i: (0, 0))],
        out_shape=[jax.ShapeDtypeStruct(w.shape, F32) for w in ws for _ in range(4)]
        + [jax.ShapeDtypeStruct(four[0].shape, F32) for four in whole for _ in range(4)]
        + [jax.ShapeDtypeStruct((1, 1), F32)],
        compiler_params=_params(("arbitrary",)), name="adamw_rest")(
            g_packed, *ws, *ms, *vs, *[a for four in whole for a in four])
    return ([outs[4 * i:4 * i + 4] for i in range(n)], [outs[4 * (n + i):4 * (n + i) + 4] for i in range(nw)],
            outs[4 * (n + nw)])


def _place():
    x, y, c = lax.axis_index("x"), lax.axis_index("y"), lax.axis_index("c")
    chip = 2 * x + y
    peers = [(x, 1 - y), (1 - x, y), (1 - x, 1 - y)]
    peer_chip = [2 * px + py for px, py in peers]
    return x, y, c, chip, peers, peer_chip


def _remote(src, dst, send_sem, recv_sem, dev):
    return pltpu.make_async_remote_copy(src_ref=src, dst_ref=dst, send_sem=send_sem, recv_sem=recv_sem,
                                        device_id=dev, device_id_type=MESH)


def _ag_weights(weights, late=()):
    nw, nl = len(weights), len(late)
    pieces = 2

    def body(*refs):
        srcs, late_srcs = refs[:nw], refs[nw:nw + nl]
        outs, late_bf, late_land = (refs[nw + nl:2 * nw + nl], refs[2 * nw + nl:2 * nw + 2 * nl],
                                    refs[2 * nw + 2 * nl:2 * nw + 3 * nl])
        scr = refs[2 * nw + 3 * nl:]
        wide, narrow = scr[:nw], scr[nw:2 * nw]
        in_sem, put_sem, s_ici, r_ici, s_d2d, r_d2d = scr[2 * nw:]
        x, y, c = lax.axis_index("x"), lax.axis_index("y"), lax.axis_index("c")
        chip = 2 * x + y
        sib = (x, y, 1 - c)
        first = ((x + 1 - c) % 2, (y + c) % 2)
        second = ((x + c) % 2, (y + 1 - c) % 2)
        first_chip, second_chip = 2 * first[0] + first[1], 2 * second[0] + second[1]
        diag_chip = 3 - chip

        parts = [(w, out, pc) for w, out in enumerate(outs) for pc in range(pieces)]

        def rows_of(out, cc, pc):
            rows = out.shape[1] // 2 // pieces
            return pl.ds(pl.multiple_of((cc * pieces + pc) * rows, 16), rows)

        def piece(out, k, cc, pc):
            return out.at[k, rows_of(out, cc, pc), :]

        def ici(w, slot, out, k, dev, pc, src=None):
            blk, sem = piece(out, k, c, pc), (nw * slot + w) * pieces + pc
            return _remote(blk if src is None else src, blk, s_ici.at[sem], r_ici.at[sem], (dev[0], dev[1], c))

        def d2d(w, slot, out, k, cc, pc):
            blk, sem = piece(out, k, cc, pc), (nw * slot + w) * pieces + pc
            return _remote(blk, blk, s_d2d.at[sem], r_d2d.at[sem], sib)

        def read(w, cc, pc):
            rows = rows_of(outs[w], cc, pc)
            return pltpu.make_async_copy(srcs[w].at[rows, :], wide[w].at[rows, :],
                                         in_sem.at[(w * 2 + cc) * pieces + pc])

        order = [(w, cc, pc) for cc in (0, 1) for w in range(nw) for pc in range(pieces)]
        for w, cc, pc in order:
            read(w, (c + cc) % 2, pc).start()
        sent = []
        for w, cc, pc in order[:nw * pieces]:
            rows = rows_of(outs[w], c, pc)
            read(w, c, pc).wait()
            narrow[w][rows, :] = wide[w][rows, :].astype(BF16)
            for slot, dev in enumerate((first, second)):
                sent.append(ici(w, slot, outs[w], chip, dev, pc, src=narrow[w].at[rows, :]))
                sent[-1].start()
        for src, bf, land in zip(late_srcs, late_bf, late_land):
            bf[...] = src[...].astype(BF16)
            land[...] = jnp.zeros_like(land)
            land[chip] = bf[...]
        for w, cc, pc in order[nw * pieces:]:
            rows = rows_of(outs[w], 1 - c, pc)
            read(w, 1 - c, pc).wait()
            narrow[w][rows, :] = wide[w][rows, :].astype(BF16)
        puts = [pltpu.make_async_copy(narrow[w], outs[w].at[chip], put_sem.at[w]) for w in range(nw)]
        for cp in puts:
            cp.start()
        for slot, k, dev in ((0, first_chip, first), (1, second_chip, second), (2, diag_chip, second)):
            for w, out, pc in parts:
                ici(w, slot, out, k, dev, pc).wait_recv()
                if slot == 0:
                    sent.append(ici(w, 2, out, k, second, pc))
                    sent[-1].start()
                sent.append(d2d(w, slot, out, k, c, pc))
                sent[-1].start()
        for slot, k in ((0, second_chip), (1, first_chip), (2, diag_chip)):
            for w, out, pc in parts:
                d2d(w, slot, out, k, 1 - c, pc).wait_recv()
        for cp in sent:
            cp.wait_send()
        for cp in puts:
            cp.wait()

    vmem = pl.BlockSpec(memory_space=pltpu.VMEM)
    far = pl.BlockSpec(memory_space=pl.ANY)
    outs = pl.pallas_call(
        body,
        out_shape=[jax.ShapeDtypeStruct((N_CHIPS,) + w.shape, BF16) for w in weights]
        + [jax.ShapeDtypeStruct(w.shape, BF16) for w in late]
        + [jax.ShapeDtypeStruct((N_CHIPS,) + w.shape, BF16) for w in late],
        in_specs=[far] * nw + [vmem] * nl, out_specs=[far] * nw + [vmem] * (2 * nl),
        scratch_shapes=[pltpu.VMEM(w.shape, F32) for w in weights] + [pltpu.VMEM(w.shape, BF16) for w in weights]
        + [pltpu.SemaphoreType.DMA((2 * nw * pieces,)), pltpu.SemaphoreType.DMA((nw,))]
        + [pltpu.SemaphoreType.DMA((3 * nw * pieces,))] * 4,
        compiler_params=pltpu.CompilerParams(vmem_limit_bytes=VMEM_LIMIT), name="ag_weights")(*weights, *late)
    return outs[:nw], outs[nw:nw + nl], outs[nw + nl:]


_HBM = pl.BlockSpec(memory_space=pltpu.HBM)
_SEM = pl.BlockSpec(memory_space=pltpu.SEMAPHORE)
_ANY = pl.BlockSpec(memory_space=pl.ANY)
_DATAFLOW = pltpu.SideEffectType.DATAFLOW_SIDE_EFFECTING


def _in_hbm(a):
    return pltpu.with_memory_space_constraint(a, pltpu.HBM)


_PEERS_OF = {"gather": 3, "scatter": 3, "direct": 7}


def _exchange_copies(mode, srcs, lands, send_sems, recv_sems):
    nw = len(srcs)
    x, y, c, chip, peers, peer_chip = _place()
    pairs = []
    if mode == "direct":
        targets = [((x, y), chip, 1)] + [(p, k, d) for p, k in zip(peers, peer_chip) for d in (0, 1)]
        for r, ((px, py), k, d) in enumerate(targets):
            for w in range(nw):
                sems = (send_sems.at[nw * r + w], recv_sems.at[nw * r + w], (px, py, (c + d) % 2))
                share = srcs[w].at[k if srcs[w].shape[0] > 1 else 0]
                pairs.append((_remote(share, lands[w].at[r], *sems),) * 2)
        return pairs
    gather = mode == "gather"
    for m, (px, py) in enumerate(peers):
        for w in range(nw):
            sems = (send_sems.at[nw * m + w], recv_sems.at[nw * m + w], (px, py, c))
            if gather:
                pairs.append((_remote(srcs[w], lands[w].at[chip], *sems),
                              _remote(srcs[w], lands[w].at[peer_chip[m]], *sems)))
            else:
                pairs.append((_remote(srcs[w].at[m], lands[w].at[m], *sems),) * 2)
    return pairs


def _exchange_start(mode, srcs, after, name, lands=None):
    nw = len(srcs)
    n_copies = _PEERS_OF[mode] * nw

    after = tuple(after)

    def body(*refs):
        send_sems, recv_sems = refs[2 * nw + len(after)], refs[2 * nw + len(after) + 1]
        for start, _ in _exchange_copies(mode, refs[:nw], refs[nw:2 * nw], send_sems, recv_sems):
            start.start()
        refs[-1][...] = jnp.zeros_like(refs[-1])

    if lands is None:
        shape = {"gather": lambda s: (N_CHIPS,) + s.shape, "scatter": lambda s: s.shape,
                 "direct": lambda s: (_PEERS_OF["direct"],) + s.shape[1:]}[mode]
        lands = [lax.empty(shape(s), s.dtype) for s in srcs]
    lands = [_in_hbm(l) for l in lands]
    return pl.pallas_call(
        body, name=name,
        out_shape=(pltpu.SemaphoreType.DMA((n_copies,)), pltpu.SemaphoreType.DMA((n_copies,)))
        + tuple(pltpu.HBM(s.shape, s.dtype) for s in srcs)
        + tuple(pltpu.HBM(l.shape, l.dtype) for l in lands)
        + (jax.ShapeDtypeStruct((8, LANES), F32),),
        in_specs=[_HBM] * (2 * nw) + [_ANY] * len(after),
        out_specs=(_SEM, _SEM) + (_HBM,) * (2 * nw) + (pl.BlockSpec(memory_space=pltpu.VMEM),),
        input_output_aliases={i: 2 + i for i in range(2 * nw)},
        compiler_params=pltpu.CompilerParams(has_side_effects=_DATAFLOW),
    )(*[_in_hbm(s) for s in srcs], *lands, *after)


def _exchange_wait(mode, started, after, name):
    nw = (len(started) - 3) // 2
    send_sems, recv_sems = started[0], started[1]
    thru = started[2:2 + 2 * nw]

    def body(*refs):
        for _, arrival in _exchange_copies(mode, refs[:nw], refs[nw:2 * nw], refs[2 * nw], refs[2 * nw + 1]):
            arrival.wait_send()
            arrival.wait_recv()

    outs = pl.pallas_call(
        body, name=name,
        out_shape=tuple(pltpu.HBM(t.shape, t.dtype) for t in thru),
        in_specs=[_HBM] * (2 * nw) + [_SEM, _SEM, _ANY], out_specs=(_HBM,) * (2 * nw),
        input_output_aliases={i: i for i in range(2 * nw)},
        compiler_params=pltpu.CompilerParams(has_side_effects=_DATAFLOW),
    )(*thru, send_sems, recv_sems, after)
    return outs[:nw], outs[nw:]


def _reduce_last(owns, landed, g_small, direct_srcs, direct_landed, spread_row0):
    ns, nd = len(owns), len(direct_srcs)
    halves = [o.shape[0] for o in owns]
    row_block = 16
    spread_rows = direct_srcs[-1].shape[1]
    rest0, rest1 = spread_row0, SMALL_ROWS - spread_row0 - spread_rows
    hs = (rest0 + rest1) // 2
    jobs = [(w, p * (halves[w] // 2), halves[w] // 2) for w in range(ns) for p in range(2)]
    n_swaps = len(jobs)
    jobs += [(ns + d, 0, direct_srcs[d].shape[1]) for d in range(nd)]

    def body(*refs):
        own_refs, land_refs, gsm_ref = refs[:ns], refs[ns:2 * ns], refs[2 * ns]
        dsrc_refs, dland_refs = refs[2 * ns + 1:2 * ns + 1 + nd], refs[2 * ns + 1 + nd:2 * ns + 1 + 2 * nd]
        n_in = 2 * ns + 1 + 2 * nd
        out_refs, osm_ref = refs[n_in:n_in + ns + nd - 1], refs[n_in + ns + nd - 1]
        scr = refs[n_in + ns + nd:]
        own_scr, land_scr, dsrc_scr, dland_scr = (scr[:ns], scr[ns:2 * ns], scr[2 * ns:2 * ns + nd],
                                                  scr[2 * ns + nd:2 * ns + 2 * nd])
        o_rest, ra_sm, p_sm, in_sem, s_sem, r_sem, sm_s, sm_r = scr[2 * ns + 2 * nd:]
        x, y, c, chip, peers, peer_chip = _place()
        sib = (x, y, 1 - c)
        half = lambda cc: pl.ds(pl.multiple_of(cc * hs, 8), hs)
        sm_a = _remote(gsm_ref.at[half(1 - c), :], ra_sm, sm_s.at[0], sm_r.at[0], sib)
        sm_a.start()
        swaps = [sm_a]

        def reads(j):
            w, r0, n = jobs[j]
            rows = pl.ds(r0, n)
            if w < ns:
                pairs = [(own_refs[w].at[rows, :], own_scr[w].at[rows, :]),
                         (land_refs[w].at[:, rows, :], land_scr[w].at[:, rows, :])]
            else:
                share = dsrc_refs[w - ns].at[chip if w < ns + nd - 1 else 0]
                pairs = [(share, dsrc_scr[w - ns]), (dland_refs[w - ns], dland_scr[w - ns])]
            return [pltpu.make_async_copy(s, d, in_sem.at[2 * j + i]) for i, (s, d) in enumerate(pairs)]

        for j in range(len(jobs)):
            for cp in reads(j):
                cp.start()
        sm_a.wait_recv()
        p_sm[chip] = gsm_ref[half(c), :] + ra_sm[...]
        for m, (px, py) in enumerate(peers):
            swaps.append(_remote(p_sm.at[chip], p_sm.at[chip], sm_s.at[1 + m], sm_r.at[1 + m], (px, py, c)))
            swaps[-1].start()

        def mine_of(j):
            w, r0, n = jobs[j]
            return out_refs[w].at[pl.ds(pl.multiple_of(c * halves[w] + r0, 8), n), :]

        for j, (w, r0, n) in enumerate(jobs):
            for cp in reads(j):
                cp.wait()

            def total(i, carry, w=w, r0=r0):
                rr = pl.multiple_of(r0 + i * row_block, row_block)
                blk = pl.ds(rr, row_block)
                if w < ns:
                    acc = own_scr[w][blk, :]
                    for m in range(_PEERS_OF["scatter"]):
                        acc = acc + land_scr[w][m, blk, :].astype(F32)
                    out_refs[w][pl.ds(pl.multiple_of(c * halves[w] + rr, row_block), row_block), :] = acc
                else:
                    theirs = lambda r: dland_scr[w - ns][r, blk, :].astype(F32)
                    acc = ((dsrc_scr[w - ns][blk, :].astype(F32) + theirs(0)) + (theirs(1) + theirs(2))) + (
                        (theirs(3) + theirs(4)) + (theirs(5) + theirs(6)))
                    if w < ns + nd - 1:
                        out_refs[w][blk, :] = acc
                    else:
                        osm_ref[pl.ds(pl.multiple_of(spread_row0 + rr, 8), row_block), :] = acc
                return carry
            lax.fori_loop(0, n // row_block, total, 0)
            if w < ns:
                swaps.append(_remote(mine_of(j), mine_of(j), s_sem.at[j], r_sem.at[j], sib))
                swaps[-1].start()
        for m, (px, py) in enumerate(peers):
            _remote(p_sm.at[chip], p_sm.at[peer_chip[m]], sm_s.at[1 + m], sm_r.at[1 + m], (px, py, c)).wait_recv()
        o_rest[half(c), :] = (p_sm[0] + p_sm[1]) + (p_sm[2] + p_sm[3])
        swaps.append(_remote(o_rest.at[half(c), :], o_rest.at[half(c), :], sm_s.at[4], sm_r.at[4], sib))
        swaps[-1].start()
        for j, (w, r0, n) in enumerate(jobs[:n_swaps]):
            theirs = out_refs[w].at[pl.ds(pl.multiple_of((1 - c) * halves[w] + r0, 8), n), :]
            _remote(theirs, theirs, s_sem.at[j], r_sem.at[j], sib).wait_recv()
        _remote(o_rest.at[half(1 - c), :], o_rest.at[half(1 - c), :], sm_s.at[4], sm_r.at[4], sib).wait_recv()
        osm_ref[0:rest0, :] = o_rest[0:rest0, :]
        osm_ref[SMALL_ROWS - rest1:SMALL_ROWS, :] = o_rest[rest0:rest0 + rest1, :]
        for cp in swaps:
            cp.wait_send()

    vmem = pl.BlockSpec(memory_space=pltpu.VMEM)
    far = [pl.BlockSpec(memory_space=pl.ANY)]
    return pl.pallas_call(
        body, out_shape=[jax.ShapeDtypeStruct((2 * o.shape[0], o.shape[1]), F32) for o in owns]
        + [jax.ShapeDtypeStruct(s.shape[1:], F32) for s in direct_srcs[:-1]]
        + [jax.ShapeDtypeStruct((SMALL_ROWS, LANES), F32)],
        in_specs=far * (2 * ns) + [vmem] + far * (2 * nd), out_specs=[vmem] * (ns + nd),
        scratch_shapes=[pltpu.VMEM(o.shape, o.dtype) for o in owns] + [pltpu.VMEM(l.shape, l.dtype) for l in landed]
        + [pltpu.VMEM(s.shape[1:], s.dtype) for s in direct_srcs]
        + [pltpu.VMEM(l.shape, l.dtype) for l in direct_landed]
        + [pltpu.VMEM((2 * hs, LANES), F32), pltpu.VMEM((hs, LANES), F32), pltpu.VMEM((N_CHIPS, hs, LANES), F32),
           pltpu.SemaphoreType.DMA((2 * len(jobs),)),
           pltpu.SemaphoreType.DMA((n_swaps,)), pltpu.SemaphoreType.DMA((n_swaps,)),
           pltpu.SemaphoreType.DMA((5,)), pltpu.SemaphoreType.DMA((5,))],
        compiler_params=pltpu.CompilerParams(vmem_limit_bytes=VMEM_LIMIT),
        name="reduce_last")(*owns, *landed, g_small, *direct_srcs, *direct_landed)


_SMALL_PARTS = (("g_norm", 8, 8), ("w_s", 512, 512), ("b_s", 4, 8), ("g_v", 2, 8), ("g_mem", 8, 8),
                ("g_final", 8, 8), ("loss", 8, 8))
_LOSS_ROW = SMALL_ROWS - 8
_W_S_ROW = 8
assert sum(p for _, _, p in _SMALL_PARTS) == SMALL_ROWS and _SMALL_PARTS[1][0] == "w_s"


def _pack_small(parts, loss_block):
    rows = []
    parts = dict(parts, loss=loss_block)
    for name, used, padded in _SMALL_PARTS:
        if name == "w_s":
            continue
        p = parts[name].reshape(used, LANES)
        if padded > used:
            p = jnp.pad(p, ((0, padded - used), (0, 0)))
        rows.append(p)
    return jnp.concatenate(rows, axis=0)


def _local_step(x, mem, target, g_norm, w_in, w_s, b_s, g_v, g_mem, late_weights, g_final,
                fwd_token=None, on_late=None, on_dw=None):
    B, S, _ = x.shape
    x2d = x.reshape(B * S, D_MODEL)
    t2d = target.reshape(B * S, D_MODEL)
    mem2d = mem.reshape(B * N_MEM, D_MODEL)

    proj = _inproj_fwd(x2d, g_norm, w_in, after=() if fwd_token is None else (fwd_token,))
    w_kv, w_out = late_weights(proj)
    kv = _kv_fwd(mem2d, g_mem, w_kv)
    a, lse = _attn_fwd(proj, B, S)
    w_sT = jnp.swapaxes(w_s, 1, 2)
    b_tab = jnp.repeat(b_s.T, HEAD_DIM, axis=1)
    (dx2, da, drest, loss, d_wout, d_ws, d_bs, d_gv, d_gf, dkv) = _mid(
        x2d, t2d, a, proj, kv, w_s, w_sT, b_tab, g_v, w_out, g_final, B, S)
    d_wkv, d_gmem = _kv_bwd(mem2d, g_mem, w_kv, dkv)
    dq, dk, dv = _attn_bwd(proj, a, lse, da, B, S, after=() if on_late is None else (on_late(d_wkv, d_wout, d_ws),))
    if on_dw is None:
        d_win = _inproj_bwd_dw(dq, dk, dv, drest, x2d, g_norm)
        after = ()
    else:
        d_win = None
        after = (on_dw(*_inproj_bwd_dw(dq, dk, dv, drest, x2d, g_norm, reduce_with=[])),)
    grad_x, d_gnorm = _inproj_bwd_dx(dq, dk, dv, drest, x2d, dx2, g_norm, w_in, after=after)
    d_bs = d_bs[:, :N_SGU_GROUPS].T
    return (loss, grad_x.reshape(B, S, D_MODEL),
            dict(g_norm=d_gnorm, w_in=d_win, w_s=d_ws, b_s=d_bs, g_v=d_gv, g_mem=d_gmem, w_kv=d_wkv,
                 w_out=d_wout, g_final=d_gf))


def kernel(x, mem, g_norm, w_in, w_sgu_spatial, b_sgu_spatial, g_sgu_v, g_mem, w_mem_kv, w_out, g_final, loss_target, m_g_norm, m_w_in, m_w_sgu_spatial, m_b_sgu_spatial, m_g_sgu_v, m_g_mem, m_w_mem_kv, m_w_out, m_g_final, v_g_norm, v_w_in, v_w_sgu_spatial, v_b_sgu_spatial, v_g_sgu_v, v_g_mem, v_w_mem_kv, v_w_out, v_g_final):
    t = lambda w: jnp.swapaxes(w[0], 0, 1)
    (win_all,), late_shards, late_lands = _ag_weights([t(w_in)], [w_mem_kv[0], w_out[0]])
    w_in_full = win_all.reshape(-1, win_all.shape[-1])
    late = _exchange_start("gather", list(late_shards), (win_all,), "gather_late_start", lands=late_lands)

    def late_weights(proj):
        return [z.reshape(-1, z.shape[-1]) for z in _exchange_wait("gather", late, proj, "gather_late_wait")[1]]

    scatter = {}

    def on_late(d_wkv, d_wout, d_ws):
        d_ws = d_ws.reshape(1, -1, LANES)
        scatter["late"] = _exchange_start("direct", [d_wkv, d_wout, d_ws], (), "scatter_late_start")
        return scatter["late"][-1]

    def on_dw(sends, owns):
        scatter["own"] = owns
        scatter["started"] = _exchange_start("scatter", list(sends), (owns[0],), "scatter_start")
        return scatter["started"][-1]

    loss, grad_x, g = _local_step(
        x, mem, loss_target, g_norm, w_in_full, w_sgu_spatial[0], b_sgu_spatial[0], g_sgu_v, g_mem,
        late_weights, g_final.reshape(1, D_MODEL), fwd_token=late[-1], on_late=on_late, on_dw=on_dw)

    small_names = ("g_norm", "w_s", "b_s", "g_v", "g_mem", "g_final")
    g_small = _pack_small({n: g[n] for n in small_names if n != "w_s"}, loss)
    late_srcs, late_landed = _exchange_wait("direct", scatter["late"], g_small, "scatter_late_wait")
    _, landed = _exchange_wait("scatter", scatter["started"], late_landed[0], "scatter_wait")
    gr_in, gr_kv, gr_out, gr_small = _reduce_last(scatter["own"], landed, g_small, late_srcs, late_landed, _W_S_ROW)

    small_w = (g_norm, w_sgu_spatial, b_sgu_spatial, g_sgu_v, g_mem, g_final)
    small_m = (m_g_norm, m_w_sgu_spatial, m_b_sgu_spatial, m_g_sgu_v, m_g_mem, m_g_final)
    small_v = (v_g_norm, v_w_sgu_spatial, v_b_sgu_spatial, v_g_sgu_v, v_g_mem, v_g_final)
    rows = lambda ws: [w.reshape(-1, LANES) for w in ws]
    small_new, ((gr_kv, d_kv, nm_kv, nv_kv), (gr_out, d_out, nm_out, nv_out)), loss = _adamw_rest(
        gr_small, rows(small_w), rows(small_m), rows(small_v),
        [(w_mem_kv[0], gr_kv, m_w_mem_kv[0], v_w_mem_kv[0]), (w_out[0], gr_out, m_w_out[0], v_w_out[0])])
    loss = loss.reshape(())
    small = [[z.reshape(w.shape) for z in four] for w, four in zip(small_w, small_new)]
    gr_in, d_in, nm_in, nv_in = [jnp.swapaxes(z, 0, 1)
                                 for z in _adamw(t(w_in), gr_in, t(m_w_in), t(v_w_in), "adamw_w_in")]

    def leaves(kind, big_in, big_kv, big_out):
        s_norm, s_ws, s_bs, s_gv, s_gmem, s_gf = [four[kind] for four in small]
        return [s_norm, big_in[None], s_ws, s_bs, s_gv, s_gmem, big_kv[None], big_out[None], s_gf]

    return (loss, grad_x, *leaves(0, gr_in, gr_kv, gr_out), *leaves(1, d_in, d_kv, d_out),
            *leaves(2, nm_in, nm_kv, nm_out), *leaves(3, nv_in, nv_kv, nv_out))
```

```python
import functools

import jax
import jax.numpy as jnp
from jax import lax
from jax.experimental import pallas as pl
from jax.experimental.pallas import tpu as pltpu

F32 = jnp.float32
BF16 = jnp.bfloat16
MESH = pl.DeviceIdType.MESH

D_MODEL = 1024
ATTN_WIDTH = 512
SGU_WIDTH = 256
MEM_WIDTH = 256
N_MEM = 256
IN_COLS = 3328
QKV_COLS = 3 * ATTN_WIDTH
REST_COLS = IN_COLS - QKV_COLS
SGU_CHUNK = 128
N_SGU_GROUPS = 4
EPS = 1e-6
NEG_INF = -1e30
DILATIONS = (1, 4, 16)
RADIUS = 64
Q_BLOCK = 128
LANES = 128
HEAD_DIM = 64

ADAM_LR = 0.001
ADAM_B1 = 0.9
ADAM_B2 = 0.999
ADAM_EPS = 1e-08
ADAM_WD = 0.01
ADAM_STEP = 10

N_CHIPS = 4
VMEM_LIMIT = 56 * 1024 * 1024
SMALL_ROWS = 560


def _params(sem=None, vmem=VMEM_LIMIT):
    return pltpu.CompilerParams(dimension_semantics=sem, vmem_limit_bytes=vmem)


def _nn(a, b):
    return jnp.dot(a, b, preferred_element_type=F32)


def _nt(a, b):
    return lax.dot_general(a, b, (((1,), (1,)), ((), ())), preferred_element_type=F32)


def _tn(a, b):
    return lax.dot_general(a, b, (((0,), (0,)), ((), ())), preferred_element_type=F32)


def _rms(x):
    r = lax.rsqrt(jnp.mean(x * x, axis=-1, keepdims=True) + EPS)
    return r, x * r


def _head_masks():
    lane = lax.broadcasted_iota(jnp.int32, (1, LANES), 1)
    lo = lane < HEAD_DIM
    return lo, (lo.astype(F32), (~lo).astype(F32))


def _silu_parts(z):
    s = jax.nn.sigmoid(z)
    return z * s, s * (1.0 + z * (1.0 - s))


def _gelu_parts(x):
    c = 0.7978845608028654
    x2 = x * x
    s = jax.nn.sigmoid((2.0 * c) * (x + 0.044715 * (x * x2)))
    return x * s, s * (1.0 + x * (1.0 - s) * ((2.0 * c) * (1.0 + 3.0 * 0.044715 * x2)))


def _after(tokens):
    return [pl.BlockSpec(memory_space=pl.ANY)] * len(tokens)


def _inproj_fwd(x2d, g_norm, w_in_t, after=()):
    T = x2d.shape[0]
    tm = 512

    def body(x_ref, g_ref, w_ref, *rest):
        o_ref = rest[-1]
        _, xh = _rms(x_ref[...])
        h = (xh * g_ref[...]).astype(BF16)
        o_ref[...] = _nt(h, w_ref[...])

    return pl.pallas_call(
        body, grid=(T // tm,),
        in_specs=[pl.BlockSpec((tm, D_MODEL), lambda i: (i, 0)),
                  pl.BlockSpec((1, D_MODEL), lambda i: (0, 0)),
                  pl.BlockSpec((IN_COLS, D_MODEL), lambda i: (0, 0))] + _after(after),
        out_specs=pl.BlockSpec((tm, IN_COLS), lambda i: (i, 0)),
        out_shape=jax.ShapeDtypeStruct((T, IN_COLS), F32),
        compiler_params=_params(("arbitrary",)), name="inproj_fwd")(x2d, g_norm, w_in_t, *after)


def _kv_fwd(mem2d, g_mem, w_kv):
    Tm = mem2d.shape[0]

    def body(m_ref, g_ref, w_ref, o_ref):
        _, mh = _rms(m_ref[...])
        o_ref[...] = _nn((mh * g_ref[...]).astype(BF16), w_ref[...])

    return pl.pallas_call(
        body, out_shape=jax.ShapeDtypeStruct((Tm, 2 * MEM_WIDTH), F32),
        compiler_params=_params(), name="kv_fwd")(mem2d, g_mem, w_kv)


def _kv_bwd(mem2d, g_mem, w_kv, dkv):
    Tm = mem2d.shape[0]

    def body(m_ref, g_ref, w_ref, dkv_ref, dw_ref, dg_ref):
        _, mh = _rms(m_ref[...])
        memn = (mh * g_ref[...]).astype(BF16)
        dkvb = dkv_ref[...].astype(BF16)
        dw = _tn(memn, dkvb).astype(BF16)
        for k in range(N_CHIPS):
            dw_ref[k] = dw[k * (D_MODEL // N_CHIPS):(k + 1) * (D_MODEL // N_CHIPS), :]
        dmemn = _nt(dkvb, w_ref[...])
        dg_ref[...] = jnp.sum(dmemn * mh, axis=0, keepdims=True)

    return pl.pallas_call(
        body, out_shape=(jax.ShapeDtypeStruct((N_CHIPS, D_MODEL // N_CHIPS, 2 * MEM_WIDTH), BF16),
                         jax.ShapeDtypeStruct((1, D_MODEL), F32)),
        compiler_params=_params(), name="kv_bwd")(mem2d, g_mem, w_kv, dkv)


def _attn_geometry(S):
    geom = []
    for d in DILATIONS:
        L = S // d
        assert L % Q_BLOCK == 0
        geom.append((d, L, min(2 * Q_BLOCK, L), L // Q_BLOCK))
    return geom


def _init_bias(bias_scr, geom, hp):
    row = lax.broadcasted_iota(jnp.int32, (Q_BLOCK, 2 * Q_BLOCK), 0)
    col = lax.broadcasted_iota(jnp.int32, (Q_BLOCK, 2 * Q_BLOCK), 1)
    for j in (0, 1):
        bits = (126 - (2 * hp + j)) * (1 << 23)
        slope = lax.bitcast_convert_type(jnp.full((1, 1), bits, jnp.int32), F32)
        for di, (d, _, _, _) in enumerate(geom):
            for cls, off in enumerate((0, -RADIUS, -2 * RADIUS)):
                dist = jnp.abs(col - row + off)
                bias_scr[di * 6 + cls * 2 + j] = jnp.where(
                    dist <= RADIUS, -(slope * float(d)) * dist.astype(F32), NEG_INF)


SPLIT = 4
COPY_ROWS = 256


def _by4_rows(S, step):
    per_class = S // SPLIT // COPY_ROWS
    r, j = step // per_class, step % per_class
    return (pl.ds(r + SPLIT * j * COPY_ROWS, COPY_ROWS, stride=SPLIT),
            pl.ds(r * (S // SPLIT) + j * COPY_ROWS, COPY_ROWS))


def _to_by4(src, dst, S):
    for i in range(S // COPY_ROWS):
        natural, by4 = _by4_rows(S, i)
        dst[by4, :] = src[natural, :]


def _block_slices(d, L, KW, nqb, r, qb, S):
    qs = qb * Q_BLOCK
    ks = jnp.clip(qs - RADIUS, 0, L - KW)
    cls = jnp.where(qb == 0, 0, jnp.where(qb == nqb - 1, 2, 1))
    if d == 1:
        qsl = pl.ds(pl.multiple_of(qs, Q_BLOCK), Q_BLOCK)
        ksl = pl.ds(pl.multiple_of(ks, RADIUS), KW)
    elif d == SPLIT:
        qsl = pl.ds(pl.multiple_of(r * L + qs, Q_BLOCK), Q_BLOCK)
        ksl = pl.ds(pl.multiple_of(r * L + ks, RADIUS), KW)
    else:
        sub = d // SPLIT
        base = (r % SPLIT) * (S // SPLIT) + r // SPLIT
        qsl = pl.ds(base + qs * sub, Q_BLOCK, stride=sub)
        ksl = pl.ds(base + ks * sub, KW, stride=sub)
    return qsl, ksl, cls


def _for_groups(geom, S, group, fn):
    for di, (d, L, KW, nqb) in enumerate(geom):
        n = group[di]
        assert (d * nqb) % n == 0

        def step(it, carry, di=di, d=d, L=L, KW=KW, nqb=nqb, n=n):
            slices = []
            for g in range(n):
                i = it * n + g
                slices.append(_block_slices(d, L, KW, nqb, i // nqb, i % nqb, S))
            fn(di, KW, slices)
            return carry
        lax.fori_loop(0, d * nqb // n, step, 0)


def _attn_fwd(proj, B, S):
    T = B * S
    geom = _attn_geometry(S)
    n_pairs = ATTN_WIDTH // LANES

    def body(q_ref, k_ref, v_ref, a_ref, lse_ref, q4, k4, v4, a4_ref, lse4_ref, bias_scr, *per_dilation):
        o_scr, m_scr, l_scr = per_dilation[0:3], per_dilation[3:6], per_dilation[6:9]
        lo, hm = _head_masks()
        pair = pl.program_id(0)

        @pl.when(pl.program_id(1) == 0)
        def _():
            _init_bias(bias_scr, geom, pair)
        for src, dst in ((q_ref, q4), (k_ref, k4), (v_ref, v4)):
            _to_by4(src, dst, S)

        def group(di, KW, all_slices):
            run = 8
            for first in range(0, len(all_slices), run):
                some(di, KW, all_slices[first:first + run])

        def some(di, KW, slices):
            chains = [(g, j) for g in range(len(slices)) for j in (0, 1)]
            q_src, k_src, v_src = (q_ref, k_ref, v_ref) if di == 0 else (q4, k4, v4)
            q = [q_src[qsl, :] for qsl, _, _ in slices]
            kw = [k_src[ksl, :].astype(BF16) for _, ksl, _ in slices]
            vw = [v_src[ksl, :].astype(BF16) for _, ksl, _ in slices]
            s = {(g, j): _nt((q[g] * (hm[j] * 0.125)).astype(BF16), kw[g])
                 + bias_scr[di * 6 + slices[g][2] * 2 + j, :, pl.ds(0, KW)] for g, j in chains}
            m = {c: jnp.max(s[c], axis=1, keepdims=True) for c in chains}
            p = {c: jnp.exp(s[c] - m[c]) for c in chains}
            l = {c: jnp.sum(p[c], axis=1, keepdims=True) for c in chains}
            o = {(g, j): _nn(p[(g, j)].astype(BF16), vw[g]) for g, j in chains}
            for g, (qsl, _, _) in enumerate(slices):
                o_scr[di][qsl, :] = jnp.where(lo, o[(g, 0)], o[(g, 1)])
                m_scr[di][qsl, :] = jnp.where(lo, m[(g, 0)], m[(g, 1)])
                l_scr[di][qsl, :] = jnp.where(lo, l[(g, 0)], l[(g, 1)])

        _for_groups(geom, S, (16, 16, 16), group)

        for i in range(S // COPY_ROWS):
            natural, by4 = _by4_rows(S, i)
            rows = [natural, by4, by4]
            ms = [m_scr[di][rows[di], :] for di in range(3)]
            mx = jnp.maximum(jnp.maximum(ms[0], ms[1]), ms[2])
            num = 0.0
            den = 0.0
            for di in range(3):
                w = jnp.exp(ms[di] - mx)
                num = num + w * o_scr[di][rows[di], :]
                den = den + w * l_scr[di][rows[di], :]
            a_ref[natural, :] = a4_ref[by4, :] = num / den
            lse_ref[natural, :] = lse4_ref[by4, :] = mx + jnp.log(den)

    blk = lambda off: pl.BlockSpec((S, LANES), lambda h, b, off=off: (b, off + h))
    out_blk = pl.BlockSpec((S, LANES), lambda h, b: (b, h))
    return pl.pallas_call(
        body, grid=(n_pairs, B),
        in_specs=[blk(0), blk(n_pairs), blk(2 * n_pairs)],
        out_specs=[out_blk] * 7,
        out_shape=[jax.ShapeDtypeStruct((T, ATTN_WIDTH), F32)] * 7,
        scratch_shapes=[pltpu.VMEM((18, Q_BLOCK, 2 * Q_BLOCK), F32)] + [pltpu.VMEM((S, LANES), F32)] * 9,
        compiler_params=_params(("arbitrary", "arbitrary")), name="attn_fwd")(proj, proj, proj)


def _attn_bwd(proj, a, lse, da, by4, B, S, after=()):
    T = B * S
    geom = _attn_geometry(S)
    n_pairs = ATTN_WIDTH // LANES

    def body(q_ref, k_ref, v_ref, a_ref, lse_ref, do_ref, q4, k4, v4, a4, lse4, *rest):
        dq_ref, dk_ref, dv_ref, bias_scr = rest[len(after):len(after) + 4]
        scr = rest[len(after) + 4:]
        acc = (scr[0:3], scr[3:6])
        natural_in = (q_ref, k_ref, v_ref, a_ref, lse_ref, do_ref)
        by4_in = (q4, k4, v4, a4, lse4, scr[6])
        _, hm = _head_masks()
        pair = pl.program_id(0)

        @pl.when(pl.program_id(1) == 0)
        def _():
            _init_bias(bias_scr, geom, pair)
        for ref in scr[0:6]:
            ref[...] = jnp.zeros_like(ref)
        _to_by4(do_ref, scr[6], S)

        def group(di, KW, all_slices):
            run = (4, 4, 8)[di]
            for first in range(0, len(all_slices), run):
                some(di, KW, all_slices[first:first + run])

        def some(di, KW, slices):
            n = len(slices)
            chains = [(g, j) for g in range(n) for j in (0, 1)]
            q_src, k_src, v_src, a_src, lse_src, do_src = natural_in if di == 0 else by4_in
            dq_scr, dk_scr, dv_scr = acc[0 if di == 0 else 1]
            q = [q_src[qsl, :] for qsl, _, _ in slices]
            do = [do_src[qsl, :] for qsl, _, _ in slices]
            doa = [do[g] * a_src[slices[g][0], :] for g in range(n)]
            lse_q = [lse_src[qsl, :] for qsl, _, _ in slices]
            kw = [k_src[ksl, :].astype(BF16) for _, ksl, _ in slices]
            vw = [v_src[ksl, :].astype(BF16) for _, ksl, _ in slices]
            qj = {(g, j): (q[g] * (hm[j] * 0.125)).astype(BF16) for g, j in chains}
            doj = {(g, j): (do[g] * hm[j]).astype(BF16) for g, j in chains}
            s = {(g, j): _nt(qj[(g, j)], kw[g])
                 + bias_scr[di * 6 + slices[g][2] * 2 + j, :, pl.ds(0, KW)] for g, j in chains}
            dp = {(g, j): _nt(doj[(g, j)], vw[g]) for g, j in chains}
            dsum = {(g, j): jnp.sum(doa[g] * hm[j], axis=1, keepdims=True) for g, j in chains}
            p = {(g, j): jnp.exp(s[(g, j)] - lse_q[g][:, HEAD_DIM * j:HEAD_DIM * j + 1]) for g, j in chains}
            ds = {c: (p[c] * (dp[c] - dsum[c])).astype(BF16) for c in chains}
            pb = {c: p[c].astype(BF16) for c in chains}
            dq = [_nn(ds[(g, 0)], kw[g]) * (hm[0] * 0.125) + _nn(ds[(g, 1)], kw[g]) * (hm[1] * 0.125)
                  for g in range(n)]
            both = lambda t, g: jnp.concatenate([t[(g, 0)], t[(g, 1)]], axis=0)
            dkw = [_tn(both(ds, g), both(qj, g)) for g in range(n)]
            dvw = [_tn(both(pb, g), both(doj, g)) for g in range(n)]
            for g, (qsl, ksl, _) in enumerate(slices):
                dq_scr[qsl, :] = dq_scr[qsl, :] + dq[g]
                dk_scr[ksl, :] = dk_scr[ksl, :] + dkw[g]
                dv_scr[ksl, :] = dv_scr[ksl, :] + dvw[g]

        _for_groups(geom, S, (16, 16, 16), group)

        for i in range(S // COPY_ROWS):
            natural, by4 = _by4_rows(S, i)
            for nat, split in zip(*acc):
                nat[natural, :] = nat[natural, :] + split[by4, :]
        for out, nat in zip((dq_ref, dk_ref, dv_ref), acc[0]):
            out[...] = nat[...].astype(BF16)

    blk = lambda off: pl.BlockSpec((S, LANES), lambda h, b, off=off: (b, off + h))
    return pl.pallas_call(
        body, grid=(n_pairs, B),
        in_specs=[blk(0), blk(n_pairs), blk(2 * n_pairs)] + [blk(0)] * 8 + _after(after),
        out_specs=[blk(0), blk(0), blk(0)],
        out_shape=[jax.ShapeDtypeStruct((T, ATTN_WIDTH), BF16)] * 3,
        scratch_shapes=[pltpu.VMEM((18, Q_BLOCK, 2 * Q_BLOCK), F32)] + [pltpu.VMEM((S, LANES), F32)] * 7,
        compiler_params=_params(("arbitrary", "arbitrary")), name="attn_bwd")(
            proj, proj, proj, a, lse, da, *by4, *after)


def _mid(x2d, t2d, a, proj, kv, w_s, w_sT, b_tab, g_v, w_out, g_final, B, S):
    T = B * S
    tm = 512
    nt = S // tm
    halves = 2
    hrows = tm // halves

    def body(x_ref, t_ref, a_ref, za_ref, ub_ref, vb_ref, zb_ref, qm_ref, zm_ref, kv_ref,
              ws_ref, wsT_ref, btab_ref, gv_ref, wout_ref, gf_ref,
              dx2_ref, da_ref, drest_ref, loss_ref, dwout_bf_ref, dws_ref, dbs_ref, dgv_ref, dgf_ref, dkv_ref,
              dbtab_scr, dwout_ref):
        b = pl.program_id(0)
        t = pl.program_id(1)
        first = jnp.logical_and(b == 0, t == 0)
        last = jnp.logical_and(b == B - 1, t == nt - 1)
        _, hm = _head_masks()
        lane_g = lax.broadcasted_iota(jnp.int32, (1, SGU_WIDTH), 1) // HEAD_DIM
        gm = [(lane_g == g).astype(F32) for g in range(N_SGU_GROUPS)]
        H = range(halves)
        rows = [pl.ds(h * hrows, hrows) for h in H]
        ld = lambda ref: [ref[r, :] for r in rows]
        cat = lambda parts, axis: jnp.concatenate(parts, axis=axis)
        chunks = [slice(ci * SGU_CHUNK, (ci + 1) * SGU_CHUNK) for ci in range(hrows // SGU_CHUNK)]
        pairs = [slice(pr * LANES, (pr + 1) * LANES) for pr in range(2)]
        heads = [(pr, j) for pr in range(2) for j in (0, 1)]

        @pl.when(first)
        def _():
            loss_ref[...] = jnp.zeros_like(loss_ref)
            dwout_ref[...] = jnp.zeros_like(dwout_ref)
            dws_ref[...] = jnp.zeros_like(dws_ref)
            dbs_ref[...] = jnp.zeros_like(dbs_ref)
            dgv_ref[...] = jnp.zeros_like(dgv_ref)
            dgf_ref[...] = jnp.zeros_like(dgf_ref)
            dbtab_scr[...] = jnp.zeros_like(dbtab_scr)

        @pl.when(t == 0)
        def _():
            dkv_ref[...] = jnp.zeros_like(dkv_ref)

        a_val = ld(a_ref)
        sil_a = [_silu_parts(z) for z in ld(za_ref)]
        gated_a = [s[0] * a for s, a in zip(sil_a, a_val)]
        u = [_gelu_parts(z) for z in ld(ub_ref)]
        vv = [_gelu_parts(z) for z in ld(vb_ref)]
        vnorm = [_rms(v[0]) for v in vv]
        gv = gv_ref[...]
        vn = [(n[1] * gv).astype(BF16) for n in vnorm]
        w_cat = cat([ws_ref[g].astype(BF16) for g in range(N_SGU_GROUPS)], 1)
        wT_cat = cat([wsT_ref[g].astype(BF16) for g in range(N_SGU_GROUPS)], 1)
        gmb = [m.astype(BF16) for m in gm]
        by_group = lambda chunk: cat([chunk * gmb[g] for g in range(N_SGU_GROUPS)], 0)
        btab = btab_ref[...]
        mixed = [cat([btab + _nn(w_cat, by_group(vn[h][c, :])) for c in chunks], 0) for h in H]
        sg = [u[h][0] * mixed[h] for h in H]
        sil_b = [_silu_parts(z) for z in ld(zb_ref)]
        gated_b = [sil_b[h][0] * sg[h] for h in H]

        kvv = kv_ref[...].astype(BF16)
        kp = [kvv[:, p] for p in pairs]
        vp = [kvv[:, MEM_WIDTH + pr * LANES:MEM_WIDTH + (pr + 1) * LANES] for pr in range(2)]
        qm = ld(qm_ref)
        qj = {(h, pr, j): (qm[h][:, pairs[pr]] * (hm[j] * 0.125)).astype(BF16) for h in H for pr, j in heads}
        sc = {k: _nt(qj[k], kp[k[1]]) for k in qj}
        ex = {k: jnp.exp(sc[k] - jnp.max(sc[k], axis=1, keepdims=True)) for k in qj}
        prob = {k: ex[k] * (1.0 / jnp.sum(ex[k], axis=1, keepdims=True)) for k in qj}
        probb = {k: prob[k].astype(BF16) for k in qj}
        mo = [cat([sum(_nn(probb[(h, pr, j)], vp[pr]) * hm[j] for j in (0, 1)) for pr in range(2)], 1) for h in H]
        sil_m = [_silu_parts(z) for z in ld(zm_ref)]
        gated_m = [sil_m[h][0] * mo[h] for h in H]

        gated = [cat([gated_a[h], gated_b[h], gated_m[h]], 1).astype(BF16) for h in H]
        wout = wout_ref[...]
        x_in = ld(x_ref)
        x2 = [x_in[h] + _nn(gated[h], wout) for h in H]
        fin = [_rms(z) for z in x2]
        gf = gf_ref[...]
        tgt = ld(t_ref)
        err = [fin[h][1] * gf - tgt[h] for h in H]
        loss_ref[...] += sum(jnp.sum(e * e) for e in err) * (0.5 / D_MODEL)

        dy = [e * (1.0 / D_MODEL) for e in err]
        dgf_ref[...] += sum(jnp.sum(dy[h] * fin[h][1], axis=0, keepdims=True) for h in H)
        gdy = [d * gf for d in dy]
        dx2 = [fin[h][0] * (gdy[h] - fin[h][1] * jnp.mean(gdy[h] * fin[h][1], axis=1, keepdims=True)) for h in H]
        for h in H:
            dx2_ref[rows[h], :] = dx2[h]
        dx2b = [d.astype(BF16) for d in dx2]
        dgated = [_nt(d, wout) for d in dx2b]
        dwout_ref[...] += _tn(cat(gated, 0), cat(dx2b, 0))
        dga = [d[:, 0:ATTN_WIDTH] for d in dgated]
        dgb = [d[:, ATTN_WIDTH:ATTN_WIDTH + SGU_WIDTH] for d in dgated]
        dgm = [d[:, ATTN_WIDTH + SGU_WIDTH:] for d in dgated]

        for h in H:
            da_ref[rows[h], :] = dga[h] * sil_a[h][0]
        dza = [dga[h] * a_val[h] * sil_a[h][1] for h in H]

        dsg = [dgb[h] * sil_b[h][0] for h in H]
        dzb = [dgb[h] * sg[h] * sil_b[h][1] for h in H]
        dub = [dsg[h] * mixed[h] * u[h][1] for h in H]
        dmixed = [dsg[h] * u[h][0] for h in H]
        dmixed_b = [d.astype(BF16) for d in dmixed]
        dvn = [cat([_nn(wT_cat, by_group(dmixed_b[h][c, :])) for c in chunks], 0) for h in H]
        for g in range(N_SGU_GROUPS):
            dws_ref[g] += sum(_nt((dmixed[h][c, :] * gm[g]).astype(BF16), vn[h][c, :]) for h in H for c in chunks)
        dbtab_scr[...] += sum(dmixed[h][c, :] for h in H for c in chunks)
        dgv_ref[...] += sum(jnp.sum(dvn[h] * vnorm[h][1], axis=0, keepdims=True) for h in H)
        tv = [d * gv for d in dvn]
        dvv = [vnorm[h][0] * (tv[h] - vnorm[h][1] * jnp.mean(tv[h] * vnorm[h][1], axis=1, keepdims=True)) for h in H]
        dvb = [dvv[h] * vv[h][1] for h in H]

        dmo = [dgm[h] * sil_m[h][0] for h in H]
        dzm = [dgm[h] * mo[h] * sil_m[h][1] for h in H]
        dmoj = {(h, pr, j): (dmo[h][:, pairs[pr]] * hm[j]).astype(BF16) for h in H for pr, j in heads}
        dp = {k: _nt(dmoj[k], vp[k[1]]) for k in qj}
        ds = {k: (prob[k] * (dp[k] - jnp.sum(dp[k] * prob[k], axis=1, keepdims=True))).astype(BF16) for k in qj}
        dqm = [cat([sum(_nn(ds[(h, pr, j)], kp[pr]) * (hm[j] * 0.125) for j in (0, 1)) for pr in range(2)], 1)
               for h in H]
        every = lambda tbl, pr: cat([tbl[(h, pr, j)] for h in H for j in (0, 1)], 0)
        dk = [_tn(every(ds, pr), every(qj, pr)) for pr in range(2)]
        dv = [_tn(every(probb, pr), every(dmoj, pr)) for pr in range(2)]
        dkv_ref[...] += cat(dk + dv, 1)

        for h in H:
            drest_ref[rows[h], :] = cat([dza[h], dub[h], dvb[h], dzb[h], dqm[h], dzm[h]], 1).astype(BF16)

        @pl.when(last)
        def _():
            lane = lax.broadcasted_iota(jnp.int32, (1, LANES), 1)
            dbt = dbtab_scr[...]
            out = jnp.zeros((SGU_CHUNK, LANES), F32)
            for g in range(N_SGU_GROUPS):
                out = out + jnp.where(lane == g, jnp.sum(dbt * gm[g], axis=1, keepdims=True), 0.0)
            dbs_ref[...] = out
            for r0 in range(0, D_MODEL, SGU_CHUNK):
                k, row = divmod(r0, D_MODEL // N_CHIPS)
                dwout_bf_ref[k, row:row + SGU_CHUNK, :] = dwout_ref[r0:r0 + SGU_CHUNK, :].astype(BF16)

    tile = lambda w, cb: pl.BlockSpec((tm, w), lambda b, t, cb=cb: (b * nt + t, cb))
    const = lambda shape: pl.BlockSpec(shape, lambda b, t, n=len(shape): (0,) * n)
    return pl.pallas_call(
        body, grid=(B, nt),
        in_specs=[tile(D_MODEL, 0), tile(D_MODEL, 0), tile(ATTN_WIDTH, 0),
                  tile(ATTN_WIDTH, 3),
                  tile(SGU_WIDTH, 8), tile(SGU_WIDTH, 9), tile(SGU_WIDTH, 10),
                  tile(MEM_WIDTH, 11), tile(MEM_WIDTH, 12),
                  pl.BlockSpec((N_MEM, 2 * MEM_WIDTH), lambda b, t: (b, 0)),
                  const((N_SGU_GROUPS, SGU_CHUNK, SGU_CHUNK)), const((N_SGU_GROUPS, SGU_CHUNK, SGU_CHUNK)),
                  const((SGU_CHUNK, SGU_WIDTH)), const((1, SGU_WIDTH)),
                  const((D_MODEL, D_MODEL)), const((1, D_MODEL))],
        out_specs=[tile(D_MODEL, 0), tile(ATTN_WIDTH, 0), tile(REST_COLS, 0),
                   const((8, LANES)), const((N_CHIPS, D_MODEL // N_CHIPS, D_MODEL)),
                   const((N_SGU_GROUPS, SGU_CHUNK, SGU_CHUNK)), const((SGU_CHUNK, LANES)),
                   const((1, SGU_WIDTH)), const((1, D_MODEL)),
                   pl.BlockSpec((N_MEM, 2 * MEM_WIDTH), lambda b, t: (b, 0))],
        out_shape=[jax.ShapeDtypeStruct((T, D_MODEL), F32), jax.ShapeDtypeStruct((T, ATTN_WIDTH), F32),
                   jax.ShapeDtypeStruct((T, REST_COLS), BF16),
                   jax.ShapeDtypeStruct((8, LANES), F32),
                   jax.ShapeDtypeStruct((N_CHIPS, D_MODEL // N_CHIPS, D_MODEL), BF16),
                   jax.ShapeDtypeStruct((N_SGU_GROUPS, SGU_CHUNK, SGU_CHUNK), F32),
                   jax.ShapeDtypeStruct((SGU_CHUNK, LANES), F32),
                   jax.ShapeDtypeStruct((1, SGU_WIDTH), F32), jax.ShapeDtypeStruct((1, D_MODEL), F32),
                   jax.ShapeDtypeStruct((B * N_MEM, 2 * MEM_WIDTH), F32)],
        scratch_shapes=[pltpu.VMEM((SGU_CHUNK, SGU_WIDTH), F32), pltpu.VMEM((D_MODEL, D_MODEL), F32)],
        compiler_params=_params(("arbitrary", "arbitrary"), vmem=VMEM_LIMIT + 2 * 1024 * 1024), name="mid")(
            x2d, t2d, a, proj, proj, proj, proj, proj, proj, kv, w_s, w_sT, b_tab, g_v, w_out, g_final)


def _inproj_bwd_dx(dq, dk, dv, drest, x2d, dx2, g_norm, w_in_t, after=()):
    T = x2d.shape[0]
    tm = 512
    W = ATTN_WIDTH

    def body(dq_ref, dk_ref, dv_ref, dr_ref, x_ref, dx2_ref, g_ref, w_ref, *rest):
        gx_ref, dg_ref = rest[-2:]

        @pl.when(pl.program_id(0) == 0)
        def _():
            dg_ref[...] = jnp.zeros_like(dg_ref)

        halves = [pl.ds(h * (tm // 2), tm // 2) for h in (0, 1)]
        dh = [(_nn(dq_ref[r, :], w_ref[0:W, :]) + _nn(dk_ref[r, :], w_ref[W:2 * W, :])
               + _nn(dv_ref[r, :], w_ref[2 * W:3 * W, :]) + _nn(dr_ref[r, :], w_ref[QKV_COLS:IN_COLS, :]))
              for r in halves]
        nrm = [_rms(x_ref[r, :]) for r in halves]
        dg_ref[...] += sum(jnp.sum(d * n[1], axis=0, keepdims=True) for d, n in zip(dh, nrm))
        g = g_ref[...]
        for r, d, (rstd, xh) in zip(halves, dh, nrm):
            th = d * g
            gx_ref[r, :] = rstd * (th - xh * jnp.mean(th * xh, axis=1, keepdims=True)) + dx2_ref[r, :]

    tile = lambda w: pl.BlockSpec((tm, w), lambda i: (i, 0))
    return pl.pallas_call(
        body, grid=(T // tm,),
        in_specs=[tile(W), tile(W), tile(W), tile(REST_COLS), tile(D_MODEL), tile(D_MODEL),
                  pl.BlockSpec((1, D_MODEL), lambda i: (0, 0)),
                  pl.BlockSpec((IN_COLS, D_MODEL), lambda i: (0, 0))] + _after(after),
        out_specs=[tile(D_MODEL), pl.BlockSpec((1, D_MODEL), lambda i: (0, 0))],
        out_shape=[jax.ShapeDtypeStruct((T, D_MODEL), F32), jax.ShapeDtypeStruct((1, D_MODEL), F32)],
        compiler_params=_params(("arbitrary",)), name="inproj_bwd_dx")(
            dq, dk, dv, drest, x2d, dx2, g_norm, w_in_t, *after)


def _inproj_bwd_dw(dq, dk, dv, drest, x2d, g_norm, reduce=False):
    T = x2d.shape[0]
    tm = 512
    nt = T // tm
    W = ATTN_WIDTH
    shard = IN_COLS // N_CHIPS
    half = shard // 2
    row_block = 32

    def body(dq_ref, dk_ref, dv_ref, dr_ref, x_ref, g_ref, *rest):
        if reduce:
            sends, own, acc, ras, narrow, s_sem, r_sem = rest
            x, y, c, chip, peers, peer_chip = _place()
            sib = (x, y, 1 - c)

            def part(k, cc, r0):
                return acc.at[pl.ds(pl.multiple_of(k * shard + cc * half + r0, 8), row_block), :]

            def swap_win(k):
                return _remote(narrow.at[k], ras.at[k], s_sem.at[k], r_sem.at[k], sib)
        else:
            acc = rest[0]

        @pl.when(pl.program_id(0) == 0)
        def _():
            acc[...] = jnp.zeros_like(acc)

        _, xh = _rms(x_ref[...])
        h = (xh * g_ref[...]).astype(BF16)
        acc[0:W, :] += _tn(dq_ref[...], h)
        acc[W:2 * W, :] += _tn(dk_ref[...], h)
        acc[2 * W:3 * W, :] += _tn(dv_ref[...], h)
        acc[QKV_COLS:IN_COLS, :] += _tn(dr_ref[...], h)

        if reduce:
            @pl.when(pl.program_id(0) == nt - 1)
            def _():
                for k in range(N_CHIPS):
                    def to_bf16(i, carry, k=k):
                        r0 = pl.multiple_of(i * row_block, row_block)
                        narrow[k, pl.ds(r0, row_block), :] = part(k, 1 - c, r0)[...].astype(BF16)
                        return carry
                    lax.fori_loop(0, half // row_block, to_bf16, 0)
                    swap_win(k).start()

                def chip_sum(k, r0):
                    return part(k, c, r0)[...] + ras[k, pl.ds(r0, row_block), :].astype(F32)

                for k in range(N_CHIPS):
                    swap_win(k).wait_recv()

                    @pl.when(chip == k)
                    def _(k=k):
                        def mine(i, carry):
                            r0 = pl.multiple_of(i * row_block, row_block)
                            own[pl.ds(r0, row_block), :] = chip_sum(k, r0)
                            return carry
                        lax.fori_loop(0, half // row_block, mine, 0)

                    @pl.when(chip != k)
                    def _(k=k):
                        def other(i, carry):
                            r0 = pl.multiple_of(i * row_block, row_block)
                            sends[(k ^ chip) - 1, pl.ds(r0, row_block), :] = chip_sum(k, r0).astype(BF16)
                            return carry
                        lax.fori_loop(0, half // row_block, other, 0)
                for k in range(N_CHIPS):
                    swap_win(k).wait_send()

    tile = lambda w: pl.BlockSpec((tm, w), lambda i: (i, 0))
    vmem = pl.BlockSpec(memory_space=pltpu.VMEM)
    in_specs = [tile(W), tile(W), tile(W), tile(REST_COLS), tile(D_MODEL), pl.BlockSpec((1, D_MODEL), lambda i: (0, 0))]
    if not reduce:
        return pl.pallas_call(
            body, grid=(nt,), in_specs=in_specs,
            out_specs=pl.BlockSpec((IN_COLS, D_MODEL), lambda i: (0, 0)),
            out_shape=jax.ShapeDtypeStruct((IN_COLS, D_MODEL), F32),
            compiler_params=_params(("arbitrary",)), name="inproj_bwd_dw")(dq, dk, dv, drest, x2d, g_norm)
    quarters = pltpu.VMEM((N_CHIPS, half, D_MODEL), BF16)
    return pl.pallas_call(
        body, grid=(nt,), in_specs=in_specs, out_specs=[vmem] * 2,
        out_shape=[jax.ShapeDtypeStruct((N_CHIPS - 1, half, D_MODEL), BF16),
                   jax.ShapeDtypeStruct((half, D_MODEL), F32)],
        scratch_shapes=[pltpu.VMEM((IN_COLS, D_MODEL), F32), quarters, quarters,
                        pltpu.SemaphoreType.DMA((N_CHIPS,)), pltpu.SemaphoreType.DMA((N_CHIPS,))],
        compiler_params=_params(("arbitrary",)), name="inproj_bwd_dw_reduce")(dq, dk, dv, drest, x2d, g_norm)


def _adamw_update(w, g, m, v):
    nm = ADAM_B1 * m + (1.0 - ADAM_B1) * g
    nv = ADAM_B2 * v + (1.0 - ADAM_B2) * (g * g)
    m_hat = nm / (1.0 - ADAM_B1 ** ADAM_STEP)
    v_hat = nv / (1.0 - ADAM_B2 ** ADAM_STEP)
    return -ADAM_LR * (m_hat / (jnp.sqrt(v_hat) + ADAM_EPS) + ADAM_WD * w), nm, nv


def _adamw(w, g, m, v, name):
    R, C = w.shape
    br = max(r for r in range(8, 257, 8) if R % r == 0)

    def body(w_ref, g_ref, m_ref, v_ref, g_out, d_ref, nm_ref, nv_ref):
        g = g_ref[...]
        g_out[...] = g
        d_ref[...], nm_ref[...], nv_ref[...] = _adamw_update(w_ref[...], g, m_ref[...], v_ref[...])

    spec = pl.BlockSpec((br, C), lambda i: (i, 0))
    return pl.pallas_call(
        body, grid=(R // br,), in_specs=[spec] * 4, out_specs=[spec] * 4,
        out_shape=[jax.ShapeDtypeStruct((R, C), F32)] * 4,
        compiler_params=_params(("arbitrary",)), name=name)(w, g, m, v)


def _adamw_rest(g_packed, ws, ms, vs, whole):
    n = len(ws)
    nw = len(whole)
    row_block = 64

    def body(*refs):
        g_ref = refs[0]
        w_refs, m_refs, v_refs = refs[1:1 + n], refs[1 + n:1 + 2 * n], refs[1 + 2 * n:1 + 3 * n]
        whole_in = [refs[1 + 3 * n + 4 * i:5 + 3 * n + 4 * i] for i in range(nw)]
        outs = refs[1 + 3 * n + 4 * nw:]
        off = 0
        for i, (_, used, padded) in enumerate(_SMALL_PARTS[:n]):
            g = g_ref[off:off + used, :]
            delta, nm, nv = _adamw_update(w_refs[i][...], g, m_refs[i][...], v_refs[i][...])
            outs[4 * i][...], outs[4 * i + 1][...], outs[4 * i + 2][...], outs[4 * i + 3][...] = g, delta, nm, nv
            off += padded
        for i, (w_ref, gw_ref, m_ref, v_ref) in enumerate(whole_in):
            for r0 in range(0, w_ref.shape[0], row_block):
                blk = pl.ds(r0, row_block)
                g = gw_ref[blk, :]
                new = _adamw_update(w_ref[blk, :], g, m_ref[blk, :], v_ref[blk, :])
                for ref, val in zip(outs[4 * (n + i):4 * (n + i) + 4], (g,) + new):
                    ref[blk, :] = val
        outs[4 * (n + nw)][...] = g_ref[_LOSS_ROW:_LOSS_ROW + 1, 0:1]

    outs = pl.pallas_call(
        body, out_shape=[jax.ShapeDtypeStruct(w.shape, F32) for w in ws for _ in range(4)]
        + [jax.ShapeDtypeStruct(four[0].shape, F32) for four in whole for _ in range(4)]
        + [jax.ShapeDtypeStruct((1, 1), F32)],
        compiler_params=_params(), name="adamw_rest")(g_packed, *ws, *ms, *vs, *[a for four in whole for a in four])
    return ([outs[4 * i:4 * i + 4] for i in range(n)], [outs[4 * (n + i):4 * (n + i) + 4] for i in range(nw)],
            outs[4 * (n + nw)])


def _place():
    x, y, c = lax.axis_index("x"), lax.axis_index("y"), lax.axis_index("c")
    chip = 2 * x + y
    peers = [(x, 1 - y), (1 - x, y), (1 - x, 1 - y)]
    peer_chip = [2 * px + py for px, py in peers]
    return x, y, c, chip, peers, peer_chip


def _remote(src, dst, send_sem, recv_sem, dev):
    return pltpu.make_async_remote_copy(src_ref=src, dst_ref=dst, send_sem=send_sem, recv_sem=recv_sem,
                                        device_id=dev, device_id_type=MESH)


def _ag_weights(weights, late=()):
    nw, nl = len(weights), len(late)
    pieces = 2

    def body(*refs):
        srcs, late_srcs = refs[:nw], refs[nw:nw + nl]
        outs, late_bf, late_land = (refs[nw + nl:2 * nw + nl], refs[2 * nw + nl:2 * nw + 2 * nl],
                                    refs[2 * nw + 2 * nl:2 * nw + 3 * nl])
        scr = refs[2 * nw + 3 * nl:]
        wide, narrow = scr[:nw], scr[nw:2 * nw]
        in_sem, put_sem, s_ici, r_ici, s_d2d, r_d2d = scr[2 * nw:]
        x, y, c = lax.axis_index("x"), lax.axis_index("y"), lax.axis_index("c")
        chip = 2 * x + y
        sib = (x, y, 1 - c)
        first = ((x + 1 - c) % 2, (y + c) % 2)
        second = ((x + c) % 2, (y + 1 - c) % 2)
        first_chip, second_chip = 2 * first[0] + first[1], 2 * second[0] + second[1]
        diag_chip = 3 - chip

        parts = [(w, out, pc) for w, out in enumerate(outs) for pc in range(pieces)]

        def rows_of(out, cc, pc):
            rows = out.shape[1] // 2 // pieces
            return pl.ds(pl.multiple_of((cc * pieces + pc) * rows, 16), rows)

        def piece(out, k, cc, pc):
            return out.at[k, rows_of(out, cc, pc), :]

        def ici(w, slot, out, k, dev, pc, src=None):
            blk, sem = piece(out, k, c, pc), (nw * slot + w) * pieces + pc
            return _remote(blk if src is None else src, blk, s_ici.at[sem], r_ici.at[sem], (dev[0], dev[1], c))

        def d2d(w, slot, out, k, cc, pc):
            blk, sem = piece(out, k, cc, pc), (nw * slot + w) * pieces + pc
            return _remote(blk, blk, s_d2d.at[sem], r_d2d.at[sem], sib)

        def read(w, cc, pc):
            rows = rows_of(outs[w], cc, pc)
            return pltpu.make_async_copy(srcs[w].at[rows, :], wide[w].at[rows, :],
                                         in_sem.at[(w * 2 + cc) * pieces + pc])

        order = [(w, cc, pc) for cc in (0, 1) for w in range(nw) for pc in range(pieces)]
        for w, cc, pc in order:
            read(w, (c + cc) % 2, pc).start()
        sent = []
        for w, cc, pc in order[:nw * pieces]:
            rows = rows_of(outs[w], c, pc)
            read(w, c, pc).wait()
            narrow[w][rows, :] = wide[w][rows, :].astype(BF16)
            for slot, dev in enumerate((first, second)):
                sent.append(ici(w, slot, outs[w], chip, dev, pc, src=narrow[w].at[rows, :]))
                sent[-1].start()
        for src, bf, land in zip(late_srcs, late_bf, late_land):
            bf[...] = src[...].astype(BF16)
            land[...] = jnp.zeros_like(land)
            land[chip] = bf[...]
        for w, cc, pc in order[nw * pieces:]:
            rows = rows_of(outs[w], 1 - c, pc)
            read(w, 1 - c, pc).wait()
            narrow[w][rows, :] = wide[w][rows, :].astype(BF16)
        puts = [pltpu.make_async_copy(narrow[w], outs[w].at[chip], put_sem.at[w]) for w in range(nw)]
        for cp in puts:
            cp.start()
        for slot, k, dev in ((0, first_chip, first), (1, second_chip, second), (2, diag_chip, second)):
            for w, out, pc in parts:
                ici(w, slot, out, k, dev, pc).wait_recv()
                if slot == 0:
                    sent.append(ici(w, 2, out, k, second, pc))
                    sent[-1].start()
                sent.append(d2d(w, slot, out, k, c, pc))
                sent[-1].start()
        for slot, k in ((0, second_chip), (1, first_chip), (2, diag_chip)):
            for w, out, pc in parts:
                d2d(w, slot, out, k, 1 - c, pc).wait_recv()
        for cp in sent:
            cp.wait_send()
        for cp in puts:
            cp.wait()

    vmem = pl.BlockSpec(memory_space=pltpu.VMEM)
    far = pl.BlockSpec(memory_space=pl.ANY)
    outs = pl.pallas_call(
        body,
        out_shape=[jax.ShapeDtypeStruct((N_CHIPS,) + w.shape, BF16) for w in weights]
        + [jax.ShapeDtypeStruct(w.shape, BF16) for w in late]
        + [jax.ShapeDtypeStruct((N_CHIPS,) + w.shape, BF16) for w in late],
        in_specs=[far] * nw + [vmem] * nl, out_specs=[far] * nw + [vmem] * (2 * nl),
        scratch_shapes=[pltpu.VMEM(w.shape, F32) for w in weights] + [pltpu.VMEM(w.shape, BF16) for w in weights]
        + [pltpu.SemaphoreType.DMA((2 * nw * pieces,)), pltpu.SemaphoreType.DMA((nw,))]
        + [pltpu.SemaphoreType.DMA((3 * nw * pieces,))] * 4,
        compiler_params=pltpu.CompilerParams(vmem_limit_bytes=VMEM_LIMIT), name="ag_weights")(*weights, *late)
    return outs[:nw], outs[nw:nw + nl], outs[nw + nl:]


_HBM = pl.BlockSpec(memory_space=pltpu.HBM)
_SEM = pl.BlockSpec(memory_space=pltpu.SEMAPHORE)
_ANY = pl.BlockSpec(memory_space=pl.ANY)
_DATAFLOW = pltpu.SideEffectType.DATAFLOW_SIDE_EFFECTING


def _in_hbm(a):
    return pltpu.with_memory_space_constraint(a, pltpu.HBM)


_PEERS_OF = {"gather": 3, "scatter": 3, "direct": 7}


def _exchange_copies(mode, srcs, lands, send_sems, recv_sems):
    nw = len(srcs)
    x, y, c, chip, peers, peer_chip = _place()
    pairs = []
    if mode == "direct":
        targets = [((x, y), chip, 1)] + [(p, k, d) for p, k in zip(peers, peer_chip) for d in (0, 1)]
        for r, ((px, py), k, d) in enumerate(targets):
            for w in range(nw):
                sems = (send_sems.at[nw * r + w], recv_sems.at[nw * r + w], (px, py, (c + d) % 2))
                share = srcs[w].at[k if srcs[w].shape[0] > 1 else 0]
                pairs.append((_remote(share, lands[w].at[r], *sems),) * 2)
        return pairs
    gather = mode == "gather"
    for m, (px, py) in enumerate(peers):
        for w in range(nw):
            sems = (send_sems.at[nw * m + w], recv_sems.at[nw * m + w], (px, py, c))
            if gather:
                pairs.append((_remote(srcs[w], lands[w].at[chip], *sems),
                              _remote(srcs[w], lands[w].at[peer_chip[m]], *sems)))
            else:
                pairs.append((_remote(srcs[w].at[m], lands[w].at[m], *sems),) * 2)
    return pairs


def _exchange_start(mode, srcs, after, name, lands=None):
    nw = len(srcs)
    n_copies = _PEERS_OF[mode] * nw

    after = tuple(after)

    def body(*refs):
        send_sems, recv_sems = refs[2 * nw + len(after)], refs[2 * nw + len(after) + 1]
        for start, _ in _exchange_copies(mode, refs[:nw], refs[nw:2 * nw], send_sems, recv_sems):
            start.start()
        refs[-1][...] = jnp.zeros_like(refs[-1])

    if lands is None:
        shape = {"gather": lambda s: (N_CHIPS,) + s.shape, "scatter": lambda s: s.shape,
                 "direct": lambda s: (_PEERS_OF["direct"],) + s.shape[1:]}[mode]
        lands = [lax.empty(shape(s), s.dtype) for s in srcs]
    lands = [_in_hbm(l) for l in lands]
    return pl.pallas_call(
        body, name=name,
        out_shape=(pltpu.SemaphoreType.DMA((n_copies,)), pltpu.SemaphoreType.DMA((n_copies,)))
        + tuple(pltpu.HBM(s.shape, s.dtype) for s in srcs)
        + tuple(pltpu.HBM(l.shape, l.dtype) for l in lands)
        + (jax.ShapeDtypeStruct((8, LANES), F32),),
        in_specs=[_HBM] * (2 * nw) + [_ANY] * len(after),
        out_specs=(_SEM, _SEM) + (_HBM,) * (2 * nw) + (pl.BlockSpec(memory_space=pltpu.VMEM),),
        input_output_aliases={i: 2 + i for i in range(2 * nw)},
        compiler_params=pltpu.CompilerParams(has_side_effects=_DATAFLOW),
    )(*[_in_hbm(s) for s in srcs], *lands, *after)


def _exchange_wait(mode, started, after, name):
    nw = (len(started) - 3) // 2
    send_sems, recv_sems = started[0], started[1]
    thru = started[2:2 + 2 * nw]

    def body(*refs):
        for _, arrival in _exchange_copies(mode, refs[:nw], refs[nw:2 * nw], refs[2 * nw], refs[2 * nw + 1]):
            arrival.wait_send()
            arrival.wait_recv()

    outs = pl.pallas_call(
        body, name=name,
        out_shape=tuple(pltpu.HBM(t.shape, t.dtype) for t in thru),
        in_specs=[_HBM] * (2 * nw) + [_SEM, _SEM, _ANY], out_specs=(_HBM,) * (2 * nw),
        input_output_aliases={i: i for i in range(2 * nw)},
        compiler_params=pltpu.CompilerParams(has_side_effects=_DATAFLOW),
    )(*thru, send_sems, recv_sems, after)
    return outs[:nw], outs[nw:]


def _reduce_last(owns, landed, g_small, direct_srcs, direct_landed, spread_row0):
    ns, nd = len(owns), len(direct_srcs)
    halves = [o.shape[0] for o in owns]
    row_block = 16
    spread_rows = direct_srcs[-1].shape[1]
    rest0, rest1 = spread_row0, SMALL_ROWS - spread_row0 - spread_rows
    hs = (rest0 + rest1) // 2
    jobs = [(w, p * (halves[w] // 2), halves[w] // 2) for w in range(ns) for p in range(2)]
    n_swaps = len(jobs)
    jobs += [(ns + d, 0, direct_srcs[d].shape[1]) for d in range(nd)]

    def body(*refs):
        own_refs, land_refs, gsm_ref = refs[:ns], refs[ns:2 * ns], refs[2 * ns]
        dsrc_refs, dland_refs = refs[2 * ns + 1:2 * ns + 1 + nd], refs[2 * ns + 1 + nd:2 * ns + 1 + 2 * nd]
        n_in = 2 * ns + 1 + 2 * nd
        out_refs, osm_ref = refs[n_in:n_in + ns + nd - 1], refs[n_in + ns + nd - 1]
        scr = refs[n_in + ns + nd:]
        own_scr, land_scr, dsrc_scr, dland_scr = (scr[:ns], scr[ns:2 * ns], scr[2 * ns:2 * ns + nd],
                                                  scr[2 * ns + nd:2 * ns + 2 * nd])
        o_rest, ra_sm, p_sm, in_sem, s_sem, r_sem, sm_s, sm_r = scr[2 * ns + 2 * nd:]
        x, y, c, chip, peers, peer_chip = _place()
        sib = (x, y, 1 - c)
        half = lambda cc: pl.ds(pl.multiple_of(cc * hs, 8), hs)
        sm_a = _remote(gsm_ref.at[half(1 - c), :], ra_sm, sm_s.at[0], sm_r.at[0], sib)
        sm_a.start()
        swaps = [sm_a]

        def reads(j):
            w, r0, n = jobs[j]
            rows = pl.ds(r0, n)
            if w < ns:
                pairs = [(own_refs[w].at[rows, :], own_scr[w].at[rows, :]),
                         (land_refs[w].at[:, rows, :], land_scr[w].at[:, rows, :])]
            else:
                share = dsrc_refs[w - ns].at[chip if w < ns + nd - 1 else 0]
                pairs = [(share, dsrc_scr[w - ns]), (dland_refs[w - ns], dland_scr[w - ns])]
            return [pltpu.make_async_copy(s, d, in_sem.at[2 * j + i]) for i, (s, d) in enumerate(pairs)]

        for j in range(len(jobs)):
            for cp in reads(j):
                cp.start()
        sm_a.wait_recv()
        p_sm[chip] = gsm_ref[half(c), :] + ra_sm[...]
        for m, (px, py) in enumerate(peers):
            swaps.append(_remote(p_sm.at[chip], p_sm.at[chip], sm_s.at[1 + m], sm_r.at[1 + m], (px, py, c)))
            swaps[-1].start()

        def mine_of(j):
            w, r0, n = jobs[j]
            return out_refs[w].at[pl.ds(pl.multiple_of(c * halves[w] + r0, 8), n), :]

        for j, (w, r0, n) in enumerate(jobs):
            for cp in reads(j):
                cp.wait()

            def total(i, carry, w=w, r0=r0):
                rr = pl.multiple_of(r0 + i * row_block, row_block)
                blk = pl.ds(rr, row_block)
                if w < ns:
                    acc = own_scr[w][blk, :]
                    for m in range(_PEERS_OF["scatter"]):
                        acc = acc + land_scr[w][m, blk, :].astype(F32)
                    out_refs[w][pl.ds(pl.multiple_of(c * halves[w] + rr, row_block), row_block), :] = acc
                else:
                    theirs = lambda r: dland_scr[w - ns][r, blk, :].astype(F32)
                    acc = ((dsrc_scr[w - ns][blk, :].astype(F32) + theirs(0)) + (theirs(1) + theirs(2))) + (
                        (theirs(3) + theirs(4)) + (theirs(5) + theirs(6)))
                    if w < ns + nd - 1:
                        out_refs[w][blk, :] = acc
                    else:
                        osm_ref[pl.ds(pl.multiple_of(spread_row0 + rr, 8), row_block), :] = acc
                return carry
            lax.fori_loop(0, n // row_block, total, 0)
            if w < ns:
                swaps.append(_remote(mine_of(j), mine_of(j), s_sem.at[j], r_sem.at[j], sib))
                swaps[-1].start()
        for m, (px, py) in enumerate(peers):
            _remote(p_sm.at[chip], p_sm.at[peer_chip[m]], sm_s.at[1 + m], sm_r.at[1 + m], (px, py, c)).wait_recv()
        o_rest[half(c), :] = (p_sm[0] + p_sm[1]) + (p_sm[2] + p_sm[3])
        swaps.append(_remote(o_rest.at[half(c), :], o_rest.at[half(c), :], sm_s.at[4], sm_r.at[4], sib))
        swaps[-1].start()
        for j, (w, r0, n) in enumerate(jobs[:n_swaps]):
            theirs = out_refs[w].at[pl.ds(pl.multiple_of((1 - c) * halves[w] + r0, 8), n), :]
            _remote(theirs, theirs, s_sem.at[j], r_sem.at[j], sib).wait_recv()
        _remote(o_rest.at[half(1 - c), :], o_rest.at[half(1 - c), :], sm_s.at[4], sm_r.at[4], sib).wait_recv()
        osm_ref[0:rest0, :] = o_rest[0:rest0, :]
        osm_ref[SMALL_ROWS - rest1:SMALL_ROWS, :] = o_rest[rest0:rest0 + rest1, :]
        for cp in swaps:
            cp.wait_send()

    vmem = pl.BlockSpec(memory_space=pltpu.VMEM)
    far = [pl.BlockSpec(memory_space=pl.ANY)]
    return pl.pallas_call(
        body, out_shape=[jax.ShapeDtypeStruct((2 * o.shape[0], o.shape[1]), F32) for o in owns]
        + [jax.ShapeDtypeStruct(s.shape[1:], F32) for s in direct_srcs[:-1]]
        + [jax.ShapeDtypeStruct((SMALL_ROWS, LANES), F32)],
        in_specs=far * (2 * ns) + [vmem] + far * (2 * nd), out_specs=[vmem] * (ns + nd),
        scratch_shapes=[pltpu.VMEM(o.shape, o.dtype) for o in owns] + [pltpu.VMEM(l.shape, l.dtype) for l in landed]
        + [pltpu.VMEM(s.shape[1:], s.dtype) for s in direct_srcs]
        + [pltpu.VMEM(l.shape, l.dtype) for l in direct_landed]
        + [pltpu.VMEM((2 * hs, LANES), F32), pltpu.VMEM((hs, LANES), F32), pltpu.VMEM((N_CHIPS, hs, LANES), F32),
           pltpu.SemaphoreType.DMA((2 * len(jobs),)),
           pltpu.SemaphoreType.DMA((n_swaps,)), pltpu.SemaphoreType.DMA((n_swaps,)),
           pltpu.SemaphoreType.DMA((5,)), pltpu.SemaphoreType.DMA((5,))],
        compiler_params=pltpu.CompilerParams(vmem_limit_bytes=VMEM_LIMIT),
        name="reduce_last")(*owns, *landed, g_small, *direct_srcs, *direct_landed)


_SMALL_PARTS = (("g_norm", 8, 8), ("w_s", 512, 512), ("b_s", 4, 8), ("g_v", 2, 8), ("g_mem", 8, 8),
                ("g_final", 8, 8), ("loss", 8, 8))
_LOSS_ROW = SMALL_ROWS - 8
_W_S_ROW = 8
assert sum(p for _, _, p in _SMALL_PARTS) == SMALL_ROWS and _SMALL_PARTS[1][0] == "w_s"


def _pack_small(parts, loss_block):
    rows = []
    parts = dict(parts, loss=loss_block)
    for name, used, padded in _SMALL_PARTS:
        if name == "w_s":
            continue
        p = parts[name].reshape(used, LANES)
        if padded > used:
            p = jnp.pad(p, ((0, padded - used), (0, 0)))
        rows.append(p)
    return jnp.concatenate(rows, axis=0)


def _local_step(x, mem, target, g_norm, w_in, w_s, b_s, g_v, g_mem, late_weights, g_final,
                fwd_token=None, on_late=None, on_dw=None):
    B, S, _ = x.shape
    x2d = x.reshape(B * S, D_MODEL)
    t2d = target.reshape(B * S, D_MODEL)
    mem2d = mem.reshape(B * N_MEM, D_MODEL)

    proj = _inproj_fwd(x2d, g_norm, w_in, after=() if fwd_token is None else (fwd_token,))
    w_kv, w_out = late_weights(proj)
    kv = _kv_fwd(mem2d, g_mem, w_kv)
    a, lse, *by4 = _attn_fwd(proj, B, S)
    w_sT = jnp.swapaxes(w_s, 1, 2)
    b_tab = jnp.repeat(b_s.T, HEAD_DIM, axis=1)
    (dx2, da, drest, loss, d_wout, d_ws, d_bs, d_gv, d_gf, dkv) = _mid(
        x2d, t2d, a, proj, kv, w_s, w_sT, b_tab, g_v, w_out, g_final, B, S)
    d_wkv, d_gmem = _kv_bwd(mem2d, g_mem, w_kv, dkv)
    dq, dk, dv = _attn_bwd(proj, a, lse, da, by4, B, S, after=() if on_late is None else (on_late(d_wkv, d_wout, d_ws),))
    if on_dw is None:
        d_win = _inproj_bwd_dw(dq, dk, dv, drest, x2d, g_norm)
        after = ()
    else:
        d_win = None
        after = (on_dw(*_inproj_bwd_dw(dq, dk, dv, drest, x2d, g_norm, reduce=True)),)
    grad_x, d_gnorm = _inproj_bwd_dx(dq, dk, dv, drest, x2d, dx2, g_norm, w_in, after=after)
    d_bs = d_bs[:, :N_SGU_GROUPS].T
    return (loss, grad_x.reshape(B, S, D_MODEL),
            dict(g_norm=d_gnorm, w_in=d_win, w_s=d_ws, b_s=d_bs, g_v=d_gv, g_mem=d_gmem, w_kv=d_wkv,
                 w_out=d_wout, g_final=d_gf))


def kernel(x, mem, g_norm, w_in, w_sgu_spatial, b_sgu_spatial, g_sgu_v, g_mem, w_mem_kv, w_out, g_final, loss_target, m_g_norm, m_w_in, m_w_sgu_spatial, m_b_sgu_spatial, m_g_sgu_v, m_g_mem, m_w_mem_kv, m_w_out, m_g_final, v_g_norm, v_w_in, v_w_sgu_spatial, v_b_sgu_spatial, v_g_sgu_v, v_g_mem, v_w_mem_kv, v_w_out, v_g_final):
    t = lambda w: jnp.swapaxes(w[0], 0, 1)
    (win_all,), late_shards, late_lands = _ag_weights([t(w_in)], [w_mem_kv[0], w_out[0]])
    w_in_full = win_all.reshape(-1, win_all.shape[-1])
    late = _exchange_start("gather", list(late_shards), (win_all,), "gather_late_start", lands=late_lands)

    def late_weights(proj):
        return [z.reshape(-1, z.shape[-1]) for z in _exchange_wait("gather", late, proj, "gather_late_wait")[1]]

    scatter = {}

    def on_late(d_wkv, d_wout, d_ws):
        d_ws = d_ws.reshape(1, -1, LANES)
        scatter["late"] = _exchange_start("direct", [d_wkv, d_wout, d_ws], (), "scatter_late_start")
        return scatter["late"][-1]

    def on_dw(sends, own):
        scatter["own"] = own
        scatter["started"] = _exchange_start("scatter", [sends], (own,), "scatter_start")
        return scatter["started"][-1]

    loss, grad_x, g = _local_step(
        x, mem, loss_target, g_norm, w_in_full, w_sgu_spatial[0], b_sgu_spatial[0], g_sgu_v, g_mem,
        late_weights, g_final.reshape(1, D_MODEL), fwd_token=late[-1], on_late=on_late, on_dw=on_dw)

    small_names = ("g_norm", "w_s", "b_s", "g_v", "g_mem", "g_final")
    g_small = _pack_small({n: g[n] for n in small_names if n != "w_s"}, loss)
    late_srcs, late_landed = _exchange_wait("direct", scatter["late"], g_small, "scatter_late_wait")
    _, landed = _exchange_wait("scatter", scatter["started"], late_landed[0], "scatter_wait")
    gr_in, gr_kv, gr_out, gr_small = _reduce_last([scatter["own"]], landed, g_small, late_srcs, late_landed, _W_S_ROW)

    small_w = (g_norm, w_sgu_spatial, b_sgu_spatial, g_sgu_v, g_mem, g_final)
    small_m = (m_g_norm, m_w_sgu_spatial, m_b_sgu_spatial, m_g_sgu_v, m_g_mem, m_g_final)
    small_v = (v_g_norm, v_w_sgu_spatial, v_b_sgu_spatial, v_g_sgu_v, v_g_mem, v_g_final)
    rows = lambda ws: [w.reshape(-1, LANES) for w in ws]
    small_new, ((gr_kv, d_kv, nm_kv, nv_kv), (gr_out, d_out, nm_out, nv_out)), loss = _adamw_rest(
        gr_small, rows(small_w), rows(small_m), rows(small_v),
        [(w_mem_kv[0], gr_kv, m_w_mem_kv[0], v_w_mem_kv[0]), (w_out[0], gr_out, m_w_out[0], v_w_out[0])])
    loss = loss.reshape(())
    small = [[z.reshape(w.shape) for z in four] for w, four in zip(small_w, small_new)]
    gr_in, d_in, nm_in, nv_in = [jnp.swapaxes(z, 0, 1)
                                 for z in _adamw(t(w_in), gr_in, t(m_w_in), t(v_w_in), "adamw_w_in")]

    def leaves(kind, big_in, big_kv, big_out):
        s_norm, s_ws, s_bs, s_gv, s_gmem, s_gf = [four[kind] for four in small]
        return [s_norm, big_in[None], s_ws, s_bs, s_gv, s_gmem, big_kv[None], big_out[None], s_gf]

    return (loss, grad_x, *leaves(0, gr_in, gr_kv, gr_out), *leaves(1, d_in, d_kv, d_out),
            *leaves(2, nm_in, nm_kv, nm_out), *leaves(3, nv_in, nv_kv, nv_out))
```

```python
import functools

import jax
import jax.numpy as jnp
from jax import lax
from jax.experimental import pallas as pl
from jax.experimental.pallas import tpu as pltpu

F32 = jnp.float32
BF16 = jnp.bfloat16
MESH = pl.DeviceIdType.MESH

D_MODEL = 1024
ATTN_WIDTH = 512
SGU_WIDTH = 256
MEM_WIDTH = 256
N_MEM = 256
IN_COLS = 3328
QKV_COLS = 3 * ATTN_WIDTH
REST_COLS = IN_COLS - QKV_COLS
SGU_CHUNK = 128
N_SGU_GROUPS = 4
EPS = 1e-6
NEG_INF = -1e30
DILATIONS = (1, 4, 16)
RADIUS = 64
Q_BLOCK = 128
LANES = 128
HEAD_DIM = 64

ADAM_LR = 0.001
ADAM_B1 = 0.9
ADAM_B2 = 0.999
ADAM_EPS = 1e-08
ADAM_WD = 0.01
ADAM_STEP = 10

N_CHIPS = 4
VMEM_LIMIT = 56 * 1024 * 1024
SMALL_ROWS = 560


def _params(sem=None, vmem=VMEM_LIMIT):
    return pltpu.CompilerParams(dimension_semantics=sem, vmem_limit_bytes=vmem)


def _nn(a, b):
    return jnp.dot(a, b, preferred_element_type=F32)


def _nt(a, b):
    return lax.dot_general(a, b, (((1,), (1,)), ((), ())), preferred_element_type=F32)


def _tn(a, b):
    return lax.dot_general(a, b, (((0,), (0,)), ((), ())), preferred_element_type=F32)


def _rms(x):
    r = lax.rsqrt(jnp.mean(x * x, axis=-1, keepdims=True) + EPS)
    return r, x * r


def _head_masks():
    lane = lax.broadcasted_iota(jnp.int32, (1, LANES), 1)
    lo = lane < HEAD_DIM
    return lo, (lo.astype(F32), (~lo).astype(F32))


def _silu_parts(z):
    s = jax.nn.sigmoid(z)
    return z * s, s * (1.0 + z * (1.0 - s))


def _gelu_parts(x):
    c = 0.7978845608028654
    x2 = x * x
    s = jax.nn.sigmoid((2.0 * c) * (x + 0.044715 * (x * x2)))
    return x * s, s * (1.0 + x * (1.0 - s) * ((2.0 * c) * (1.0 + 3.0 * 0.044715 * x2)))


def _after(tokens):
    return [pl.BlockSpec(memory_space=pl.ANY)] * len(tokens)


def _weight_copies(w_hbm, w_scr, sems, bounds):
    return [pltpu.make_async_copy(w_hbm.at[pl.ds(r0, r1 - r0), :], w_scr.at[pl.ds(r0, r1 - r0), :], sems.at[j])
            for j, (r0, r1) in enumerate(zip(bounds[:-1], bounds[1:]))]


def _inproj_fwd(x2d, g_norm, w_in_t, after=()):
    T = x2d.shape[0]
    tm = 512
    bounds = (0, 896, 1792, 2560, IN_COLS)

    def body(x_ref, g_ref, w_hbm, *rest):
        o_ref, w_scr, sems = rest[-3:]
        first = pl.program_id(0) == 0
        copies = _weight_copies(w_hbm, w_scr, sems, bounds)

        @pl.when(first)
        def _():
            for cp in copies:
                cp.start()
        _, xh = _rms(x_ref[...])
        h = (xh * g_ref[...]).astype(BF16)

        @pl.when(first)
        def _():
            for cp, r0, r1 in zip(copies, bounds[:-1], bounds[1:]):
                cp.wait()
                o_ref[:, r0:r1] = _nt(h, w_scr[r0:r1, :])

        @pl.when(jnp.logical_not(first))
        def _():
            o_ref[...] = _nt(h, w_scr[...])

    return pl.pallas_call(
        body, grid=(T // tm,),
        in_specs=[pl.BlockSpec((tm, D_MODEL), lambda i: (i, 0)),
                  pl.BlockSpec((1, D_MODEL), lambda i: (0, 0)),
                  pl.BlockSpec(memory_space=pl.ANY)] + _after(after),
        out_specs=pl.BlockSpec((tm, IN_COLS), lambda i: (i, 0)),
        out_shape=jax.ShapeDtypeStruct((T, IN_COLS), F32),
        scratch_shapes=[pltpu.VMEM((IN_COLS, D_MODEL), BF16), pltpu.SemaphoreType.DMA((len(bounds) - 1,))],
        compiler_params=_params(("arbitrary",)), name="inproj_fwd")(x2d, g_norm, w_in_t, *after)


def _kv_fwd(mem2d, g_mem, w_kv):
    Tm = mem2d.shape[0]

    def body(m_ref, g_ref, w_ref, o_ref):
        _, mh = _rms(m_ref[...])
        o_ref[...] = _nn((mh * g_ref[...]).astype(BF16), w_ref[...])

    return pl.pallas_call(
        body, out_shape=jax.ShapeDtypeStruct((Tm, 2 * MEM_WIDTH), F32),
        compiler_params=_params(), name="kv_fwd")(mem2d, g_mem, w_kv)


def _kv_bwd(mem2d, g_mem, w_kv, dkv):
    Tm = mem2d.shape[0]

    def body(m_ref, g_ref, w_ref, dkv_ref, dw_ref, dg_ref):
        _, mh = _rms(m_ref[...])
        memn = (mh * g_ref[...]).astype(BF16)
        dkvb = dkv_ref[...].astype(BF16)
        dw = _tn(memn, dkvb).astype(BF16)
        for k in range(N_CHIPS):
            dw_ref[k] = dw[k * (D_MODEL // N_CHIPS):(k + 1) * (D_MODEL // N_CHIPS), :]
        dmemn = _nt(dkvb, w_ref[...])
        dg_ref[...] = jnp.sum(dmemn * mh, axis=0, keepdims=True)

    return pl.pallas_call(
        body, out_shape=(jax.ShapeDtypeStruct((N_CHIPS, D_MODEL // N_CHIPS, 2 * MEM_WIDTH), BF16),
                         jax.ShapeDtypeStruct((1, D_MODEL), F32)),
        compiler_params=_params(), name="kv_bwd")(mem2d, g_mem, w_kv, dkv)


def _attn_geometry(S):
    geom = []
    for d in DILATIONS:
        L = S // d
        assert L % Q_BLOCK == 0
        geom.append((d, L, min(2 * Q_BLOCK, L), L // Q_BLOCK))
    return geom


def _init_bias(bias_scr, geom, hp):
    row = lax.broadcasted_iota(jnp.int32, (Q_BLOCK, 2 * Q_BLOCK), 0)
    col = lax.broadcasted_iota(jnp.int32, (Q_BLOCK, 2 * Q_BLOCK), 1)
    for j in (0, 1):
        bits = (126 - (2 * hp + j)) * (1 << 23)
        slope = lax.bitcast_convert_type(jnp.full((1, 1), bits, jnp.int32), F32)
        for di, (d, _, _, _) in enumerate(geom):
            for cls, off in enumerate((0, -RADIUS, -2 * RADIUS)):
                dist = jnp.abs(col - row + off)
                bias_scr[di * 6 + cls * 2 + j] = jnp.where(
                    dist <= RADIUS, -(slope * float(d)) * dist.astype(F32), NEG_INF)


SPLIT = 4
COPY_ROWS = 256


def _by4_rows(S, step):
    per_class = S // SPLIT // COPY_ROWS
    r, j = step // per_class, step % per_class
    return (pl.ds(r + SPLIT * j * COPY_ROWS, COPY_ROWS, stride=SPLIT),
            pl.ds(r * (S // SPLIT) + j * COPY_ROWS, COPY_ROWS))


def _to_by4(src, dst, S):
    for i in range(S // COPY_ROWS):
        natural, by4 = _by4_rows(S, i)
        dst[by4, :] = src[natural, :]


def _block_slices(d, L, KW, nqb, r, qb, S):
    qs = qb * Q_BLOCK
    ks = jnp.clip(qs - RADIUS, 0, L - KW)
    cls = jnp.where(qb == 0, 0, jnp.where(qb == nqb - 1, 2, 1))
    if d == 1:
        qsl = pl.ds(pl.multiple_of(qs, Q_BLOCK), Q_BLOCK)
        ksl = pl.ds(pl.multiple_of(ks, RADIUS), KW)
    elif d == SPLIT:
        qsl = pl.ds(pl.multiple_of(r * L + qs, Q_BLOCK), Q_BLOCK)
        ksl = pl.ds(pl.multiple_of(r * L + ks, RADIUS), KW)
    else:
        sub = d // SPLIT
        base = (r % SPLIT) * (S // SPLIT) + r // SPLIT
        qsl = pl.ds(base + qs * sub, Q_BLOCK, stride=sub)
        ksl = pl.ds(base + ks * sub, KW, stride=sub)
    return qsl, ksl, cls


def _for_groups(geom, S, group, fn):
    for di, (d, L, KW, nqb) in enumerate(geom):
        n = group[di]
        assert (d * nqb) % n == 0

        def step(it, carry, di=di, d=d, L=L, KW=KW, nqb=nqb, n=n):
            slices = []
            for g in range(n):
                i = it * n + g
                slices.append(_block_slices(d, L, KW, nqb, i // nqb, i % nqb, S))
            fn(di, KW, slices)
            return carry
        lax.fori_loop(0, d * nqb // n, step, 0)


def _attn_fwd(proj, B, S):
    T = B * S
    geom = _attn_geometry(S)
    n_pairs = ATTN_WIDTH // LANES

    def body(q_ref, k_ref, v_ref, a_ref, lse_ref, bias_scr, q4, k4, v4, *per_dilation):
        o_scr, m_scr, l_scr = per_dilation[0:3], per_dilation[3:6], per_dilation[6:9]
        lo, hm = _head_masks()
        pair = pl.program_id(0)

        @pl.when(pl.program_id(1) == 0)
        def _():
            _init_bias(bias_scr, geom, pair)
        for src, dst in ((q_ref, q4), (k_ref, k4), (v_ref, v4)):
            _to_by4(src, dst, S)

        def group(di, KW, all_slices):
            run = 8
            for first in range(0, len(all_slices), run):
                some(di, KW, all_slices[first:first + run])

        def some(di, KW, slices):
            chains = [(g, j) for g in range(len(slices)) for j in (0, 1)]
            q_src, k_src, v_src = (q_ref, k_ref, v_ref) if di == 0 else (q4, k4, v4)
            q = [q_src[qsl, :] for qsl, _, _ in slices]
            kw = [k_src[ksl, :].astype(BF16) for _, ksl, _ in slices]
            vw = [v_src[ksl, :].astype(BF16) for _, ksl, _ in slices]
            s = {(g, j): _nt((q[g] * (hm[j] * 0.125)).astype(BF16), kw[g])
                 + bias_scr[di * 6 + slices[g][2] * 2 + j, :, pl.ds(0, KW)] for g, j in chains}
            m = {c: jnp.max(s[c], axis=1, keepdims=True) for c in chains}
            p = {c: jnp.exp(s[c] - m[c]) for c in chains}
            l = {c: jnp.sum(p[c], axis=1, keepdims=True) for c in chains}
            o = {(g, j): _nn(p[(g, j)].astype(BF16), vw[g]) for g, j in chains}
            for g, (qsl, _, _) in enumerate(slices):
                o_scr[di][qsl, :] = jnp.where(lo, o[(g, 0)], o[(g, 1)])
                m_scr[di][qsl, :] = jnp.where(lo, m[(g, 0)], m[(g, 1)])
                l_scr[di][qsl, :] = jnp.where(lo, l[(g, 0)], l[(g, 1)])

        _for_groups(geom, S, (16, 16, 16), group)

        for i in range(S // COPY_ROWS):
            natural, by4 = _by4_rows(S, i)
            rows = [natural, by4, by4]
            ms = [m_scr[di][rows[di], :] for di in range(3)]
            mx = jnp.maximum(jnp.maximum(ms[0], ms[1]), ms[2])
            num = 0.0
            den = 0.0
            for di in range(3):
                w = jnp.exp(ms[di] - mx)
                num = num + w * o_scr[di][rows[di], :]
                den = den + w * l_scr[di][rows[di], :]
            a_ref[natural, :] = num / den
            lse_ref[natural, :] = mx + jnp.log(den)

    blk = lambda off: pl.BlockSpec((S, LANES), lambda h, b, off=off: (b, off + h))
    out_blk = pl.BlockSpec((S, LANES), lambda h, b: (b, h))
    return pl.pallas_call(
        body, grid=(n_pairs, B),
        in_specs=[blk(0), blk(n_pairs), blk(2 * n_pairs)],
        out_specs=[out_blk, out_blk],
        out_shape=[jax.ShapeDtypeStruct((T, ATTN_WIDTH), F32)] * 2,
        scratch_shapes=[pltpu.VMEM((18, Q_BLOCK, 2 * Q_BLOCK), F32)] + [pltpu.VMEM((S, LANES), F32)] * 12,
        compiler_params=_params(("arbitrary", "arbitrary")), name="attn_fwd")(proj, proj, proj)


def _attn_bwd(proj, a, lse, da, B, S, after=()):
    T = B * S
    geom = _attn_geometry(S)
    n_pairs = ATTN_WIDTH // LANES

    def body(q_ref, k_ref, v_ref, a_ref, lse_ref, do_ref, *rest):
        dq_ref, dk_ref, dv_ref, bias_scr = rest[len(after):len(after) + 4]
        scr = rest[len(after) + 4:]
        acc = (scr[0:3], scr[3:6])
        natural_in = (q_ref, k_ref, v_ref, a_ref, lse_ref, do_ref)
        by4_in = scr[6:12]
        _, hm = _head_masks()
        pair = pl.program_id(0)

        @pl.when(pl.program_id(1) == 0)
        def _():
            _init_bias(bias_scr, geom, pair)
        for ref in scr[0:6]:
            ref[...] = jnp.zeros_like(ref)
        for src, dst in zip(natural_in, by4_in):
            _to_by4(src, dst, S)

        def group(di, KW, all_slices):
            run = (4, 4, 8)[di]
            for first in range(0, len(all_slices), run):
                some(di, KW, all_slices[first:first + run])

        def some(di, KW, slices):
            n = len(slices)
            chains = [(g, j) for g in range(n) for j in (0, 1)]
            q_src, k_src, v_src, a_src, lse_src, do_src = natural_in if di == 0 else by4_in
            dq_scr, dk_scr, dv_scr = acc[0 if di == 0 else 1]
            q = [q_src[qsl, :] for qsl, _, _ in slices]
            do = [do_src[qsl, :] for qsl, _, _ in slices]
            doa = [do[g] * a_src[slices[g][0], :] for g in range(n)]
            lse_q = [lse_src[qsl, :] for qsl, _, _ in slices]
            kw = [k_src[ksl, :].astype(BF16) for _, ksl, _ in slices]
            vw = [v_src[ksl, :].astype(BF16) for _, ksl, _ in slices]
            qj = {(g, j): (q[g] * (hm[j] * 0.125)).astype(BF16) for g, j in chains}
            doj = {(g, j): (do[g] * hm[j]).astype(BF16) for g, j in chains}
            s = {(g, j): _nt(qj[(g, j)], kw[g])
                 + bias_scr[di * 6 + slices[g][2] * 2 + j, :, pl.ds(0, KW)] for g, j in chains}
            dp = {(g, j): _nt(doj[(g, j)], vw[g]) for g, j in chains}
            dsum = {(g, j): jnp.sum(doa[g] * hm[j], axis=1, keepdims=True) for g, j in chains}
            p = {(g, j): jnp.exp(s[(g, j)] - lse_q[g][:, HEAD_DIM * j:HEAD_DIM * j + 1]) for g, j in chains}
            ds = {c: (p[c] * (dp[c] - dsum[c])).astype(BF16) for c in chains}
            pb = {c: p[c].astype(BF16) for c in chains}
            dq = [_nn(ds[(g, 0)], kw[g]) * (hm[0] * 0.125) + _nn(ds[(g, 1)], kw[g]) * (hm[1] * 0.125)
                  for g in range(n)]
            both = lambda t, g: jnp.concatenate([t[(g, 0)], t[(g, 1)]], axis=0)
            dkw = [_tn(both(ds, g), both(qj, g)) for g in range(n)]
            dvw = [_tn(both(pb, g), both(doj, g)) for g in range(n)]
            for g, (qsl, ksl, _) in enumerate(slices):
                dq_scr[qsl, :] = dq_scr[qsl, :] + dq[g]
                dk_scr[ksl, :] = dk_scr[ksl, :] + dkw[g]
                dv_scr[ksl, :] = dv_scr[ksl, :] + dvw[g]

        _for_groups(geom, S, (16, 16, 16), group)

        for i in range(S // COPY_ROWS):
            natural, by4 = _by4_rows(S, i)
            for nat, split in zip(*acc):
                nat[natural, :] = nat[natural, :] + split[by4, :]
        for out, nat in zip((dq_ref, dk_ref, dv_ref), acc[0]):
            out[...] = nat[...].astype(BF16)

    blk = lambda off: pl.BlockSpec((S, LANES), lambda h, b, off=off: (b, off + h))
    return pl.pallas_call(
        body, grid=(n_pairs, B),
        in_specs=[blk(0), blk(n_pairs), blk(2 * n_pairs), blk(0), blk(0), blk(0)] + _after(after),
        out_specs=[blk(0), blk(0), blk(0)],
        out_shape=[jax.ShapeDtypeStruct((T, ATTN_WIDTH), BF16)] * 3,
        scratch_shapes=[pltpu.VMEM((18, Q_BLOCK, 2 * Q_BLOCK), F32)] + [pltpu.VMEM((S, LANES), F32)] * 12,
        compiler_params=_params(("arbitrary", "arbitrary")), name="attn_bwd")(proj, proj, proj, a, lse, da, *after)


def _mid(x2d, t2d, a, proj, kv, w_s, w_sT, b_tab, g_v, w_out, g_final, B, S):
    T = B * S
    tm = 512
    nt = S // tm
    halves = 2
    hrows = tm // halves

    def body(x_ref, t_ref, a_ref, za_ref, ub_ref, vb_ref, zb_ref, qm_ref, zm_ref, kv_ref,
              ws_ref, wsT_ref, btab_ref, gv_ref, wout_ref, gf_ref,
              dx2_ref, da_ref, drest_ref, loss_ref, dwout_bf_ref, dws_ref, dbs_ref, dgv_ref, dgf_ref, dkv_ref,
              dbtab_scr, dwout_ref):
        b = pl.program_id(0)
        t = pl.program_id(1)
        first = jnp.logical_and(b == 0, t == 0)
        last = jnp.logical_and(b == B - 1, t == nt - 1)
        _, hm = _head_masks()
        lane_g = lax.broadcasted_iota(jnp.int32, (1, SGU_WIDTH), 1) // HEAD_DIM
        gm = [(lane_g == g).astype(F32) for g in range(N_SGU_GROUPS)]
        H = range(halves)
        rows = [pl.ds(h * hrows, hrows) for h in H]
        ld = lambda ref: [ref[r, :] for r in rows]
        cat = lambda parts, axis: jnp.concatenate(parts, axis=axis)
        chunks = [slice(ci * SGU_CHUNK, (ci + 1) * SGU_CHUNK) for ci in range(hrows // SGU_CHUNK)]
        pairs = [slice(pr * LANES, (pr + 1) * LANES) for pr in range(2)]
        heads = [(pr, j) for pr in range(2) for j in (0, 1)]

        @pl.when(first)
        def _():
            loss_ref[...] = jnp.zeros_like(loss_ref)
            dwout_ref[...] = jnp.zeros_like(dwout_ref)
            dws_ref[...] = jnp.zeros_like(dws_ref)
            dbs_ref[...] = jnp.zeros_like(dbs_ref)
            dgv_ref[...] = jnp.zeros_like(dgv_ref)
            dgf_ref[...] = jnp.zeros_like(dgf_ref)
            dbtab_scr[...] = jnp.zeros_like(dbtab_scr)

        @pl.when(t == 0)
        def _():
            dkv_ref[...] = jnp.zeros_like(dkv_ref)

        a_val = ld(a_ref)
        sil_a = [_silu_parts(z) for z in ld(za_ref)]
        gated_a = [s[0] * a for s, a in zip(sil_a, a_val)]
        u = [_gelu_parts(z) for z in ld(ub_ref)]
        vv = [_gelu_parts(z) for z in ld(vb_ref)]
        vnorm = [_rms(v[0]) for v in vv]
        gv = gv_ref[...]
        vn = [(n[1] * gv).astype(BF16) for n in vnorm]
        w_cat = cat([ws_ref[g].astype(BF16) for g in range(N_SGU_GROUPS)], 1)
        wT_cat = cat([wsT_ref[g].astype(BF16) for g in range(N_SGU_GROUPS)], 1)
        gmb = [m.astype(BF16) for m in gm]
        by_group = lambda chunk: cat([chunk * gmb[g] for g in range(N_SGU_GROUPS)], 0)
        btab = btab_ref[...]
        mixed = [cat([btab + _nn(w_cat, by_group(vn[h][c, :])) for c in chunks], 0) for h in H]
        sg = [u[h][0] * mixed[h] for h in H]
        sil_b = [_silu_parts(z) for z in ld(zb_ref)]
        gated_b = [sil_b[h][0] * sg[h] for h in H]

        kvv = kv_ref[...].astype(BF16)
        kp = [kvv[:, p] for p in pairs]
        vp = [kvv[:, MEM_WIDTH + pr * LANES:MEM_WIDTH + (pr + 1) * LANES] for pr in range(2)]
        qm = ld(qm_ref)
        qj = {(h, pr, j): (qm[h][:, pairs[pr]] * (hm[j] * 0.125)).astype(BF16) for h in H for pr, j in heads}
        sc = {k: _nt(qj[k], kp[k[1]]) for k in qj}
        ex = {k: jnp.exp(sc[k] - jnp.max(sc[k], axis=1, keepdims=True)) for k in qj}
        prob = {k: ex[k] * (1.0 / jnp.sum(ex[k], axis=1, keepdims=True)) for k in qj}
        probb = {k: prob[k].astype(BF16) for k in qj}
        mo = [cat([sum(_nn(probb[(h, pr, j)], vp[pr]) * hm[j] for j in (0, 1)) for pr in range(2)], 1) for h in H]
        sil_m = [_silu_parts(z) for z in ld(zm_ref)]
        gated_m = [sil_m[h][0] * mo[h] for h in H]

        gated = [cat([gated_a[h], gated_b[h], gated_m[h]], 1).astype(BF16) for h in H]
        wout = wout_ref[...]
        x_in = ld(x_ref)
        x2 = [x_in[h] + _nn(gated[h], wout) for h in H]
        fin = [_rms(z) for z in x2]
        gf = gf_ref[...]
        tgt = ld(t_ref)
        err = [fin[h][1] * gf - tgt[h] for h in H]
        loss_ref[...] += sum(jnp.sum(e * e) for e in err) * (0.5 / D_MODEL)

        dy = [e * (1.0 / D_MODEL) for e in err]
        dgf_ref[...] += sum(jnp.sum(dy[h] * fin[h][1], axis=0, keepdims=True) for h in H)
        gdy = [d * gf for d in dy]
        dx2 = [fin[h][0] * (gdy[h] - fin[h][1] * jnp.mean(gdy[h] * fin[h][1], axis=1, keepdims=True)) for h in H]
        for h in H:
            dx2_ref[rows[h], :] = dx2[h]
        dx2b = [d.astype(BF16) for d in dx2]
        dgated = [_nt(d, wout) for d in dx2b]
        dwout_ref[...] += _tn(cat(gated, 0), cat(dx2b, 0))
        dga = [d[:, 0:ATTN_WIDTH] for d in dgated]
        dgb = [d[:, ATTN_WIDTH:ATTN_WIDTH + SGU_WIDTH] for d in dgated]
        dgm = [d[:, ATTN_WIDTH + SGU_WIDTH:] for d in dgated]

        for h in H:
            da_ref[rows[h], :] = dga[h] * sil_a[h][0]
        dza = [dga[h] * a_val[h] * sil_a[h][1] for h in H]

        dsg = [dgb[h] * sil_b[h][0] for h in H]
        dzb = [dgb[h] * sg[h] * sil_b[h][1] for h in H]
        dub = [dsg[h] * mixed[h] * u[h][1] for h in H]
        dmixed = [dsg[h] * u[h][0] for h in H]
        dmixed_b = [d.astype(BF16) for d in dmixed]
        dvn = [cat([_nn(wT_cat, by_group(dmixed_b[h][c, :])) for c in chunks], 0) for h in H]
        for g in range(N_SGU_GROUPS):
            dws_ref[g] += sum(_nt((dmixed[h][c, :] * gm[g]).astype(BF16), vn[h][c, :]) for h in H for c in chunks)
        dbtab_scr[...] += sum(dmixed[h][c, :] for h in H for c in chunks)
        dgv_ref[...] += sum(jnp.sum(dvn[h] * vnorm[h][1], axis=0, keepdims=True) for h in H)
        tv = [d * gv for d in dvn]
        dvv = [vnorm[h][0] * (tv[h] - vnorm[h][1] * jnp.mean(tv[h] * vnorm[h][1], axis=1, keepdims=True)) for h in H]
        dvb = [dvv[h] * vv[h][1] for h in H]

        dmo = [dgm[h] * sil_m[h][0] for h in H]
        dzm = [dgm[h] * mo[h] * sil_m[h][1] for h in H]
        dmoj = {(h, pr, j): (dmo[h][:, pairs[pr]] * hm[j]).astype(BF16) for h in H for pr, j in heads}
        dp = {k: _nt(dmoj[k], vp[k[1]]) for k in qj}
        ds = {k: (prob[k] * (dp[k] - jnp.sum(dp[k] * prob[k], axis=1, keepdims=True))).astype(BF16) for k in qj}
        dqm = [cat([sum(_nn(ds[(h, pr, j)], kp[pr]) * (hm[j] * 0.125) for j in (0, 1)) for pr in range(2)], 1)
               for h in H]
        every = lambda tbl, pr: cat([tbl[(h, pr, j)] for h in H for j in (0, 1)], 0)
        dk = [_tn(every(ds, pr), every(qj, pr)) for pr in range(2)]
        dv = [_tn(every(probb, pr), every(dmoj, pr)) for pr in range(2)]
        dkv_ref[...] += cat(dk + dv, 1)

        for h in H:
            drest_ref[rows[h], :] = cat([dza[h], dub[h], dvb[h], dzb[h], dqm[h], dzm[h]], 1).astype(BF16)

        @pl.when(last)
        def _():
            lane = lax.broadcasted_iota(jnp.int32, (1, LANES), 1)
            dbt = dbtab_scr[...]
            out = jnp.zeros((SGU_CHUNK, LANES), F32)
            for g in range(N_SGU_GROUPS):
                out = out + jnp.where(lane == g, jnp.sum(dbt * gm[g], axis=1, keepdims=True), 0.0)
            dbs_ref[...] = out
            for r0 in range(0, D_MODEL, SGU_CHUNK):
                k, row = divmod(r0, D_MODEL // N_CHIPS)
                dwout_bf_ref[k, row:row + SGU_CHUNK, :] = dwout_ref[r0:r0 + SGU_CHUNK, :].astype(BF16)

    tile = lambda w, cb: pl.BlockSpec((tm, w), lambda b, t, cb=cb: (b * nt + t, cb))
    const = lambda shape: pl.BlockSpec(shape, lambda b, t, n=len(shape): (0,) * n)
    return pl.pallas_call(
        body, grid=(B, nt),
        in_specs=[tile(D_MODEL, 0), tile(D_MODEL, 0), tile(ATTN_WIDTH, 0),
                  tile(ATTN_WIDTH, 3),
                  tile(SGU_WIDTH, 8), tile(SGU_WIDTH, 9), tile(SGU_WIDTH, 10),
                  tile(MEM_WIDTH, 11), tile(MEM_WIDTH, 12),
                  pl.BlockSpec((N_MEM, 2 * MEM_WIDTH), lambda b, t: (b, 0)),
                  const((N_SGU_GROUPS, SGU_CHUNK, SGU_CHUNK)), const((N_SGU_GROUPS, SGU_CHUNK, SGU_CHUNK)),
                  const((SGU_CHUNK, SGU_WIDTH)), const((1, SGU_WIDTH)),
                  const((D_MODEL, D_MODEL)), const((1, D_MODEL))],
        out_specs=[tile(D_MODEL, 0), tile(ATTN_WIDTH, 0), tile(REST_COLS, 0),
                   const((8, LANES)), const((N_CHIPS, D_MODEL // N_CHIPS, D_MODEL)),
                   const((N_SGU_GROUPS, SGU_CHUNK, SGU_CHUNK)), const((SGU_CHUNK, LANES)),
                   const((1, SGU_WIDTH)), const((1, D_MODEL)),
                   pl.BlockSpec((N_MEM, 2 * MEM_WIDTH), lambda b, t: (b, 0))],
        out_shape=[jax.ShapeDtypeStruct((T, D_MODEL), F32), jax.ShapeDtypeStruct((T, ATTN_WIDTH), F32),
                   jax.ShapeDtypeStruct((T, REST_COLS), BF16),
                   jax.ShapeDtypeStruct((8, LANES), F32),
                   jax.ShapeDtypeStruct((N_CHIPS, D_MODEL // N_CHIPS, D_MODEL), BF16),
                   jax.ShapeDtypeStruct((N_SGU_GROUPS, SGU_CHUNK, SGU_CHUNK), F32),
                   jax.ShapeDtypeStruct((SGU_CHUNK, LANES), F32),
                   jax.ShapeDtypeStruct((1, SGU_WIDTH), F32), jax.ShapeDtypeStruct((1, D_MODEL), F32),
                   jax.ShapeDtypeStruct((B * N_MEM, 2 * MEM_WIDTH), F32)],
        scratch_shapes=[pltpu.VMEM((SGU_CHUNK, SGU_WIDTH), F32), pltpu.VMEM((D_MODEL, D_MODEL), F32)],
        compiler_params=_params(("arbitrary", "arbitrary"), vmem=VMEM_LIMIT + 2 * 1024 * 1024), name="mid")(
            x2d, t2d, a, proj, proj, proj, proj, proj, proj, kv, w_s, w_sT, b_tab, g_v, w_out, g_final)


def _inproj_bwd_dx(dq, dk, dv, drest, x2d, dx2, g_norm, w_in_t, after=()):
    T = x2d.shape[0]
    tm = 512
    W = ATTN_WIDTH

    bounds = (0, W, 2 * W, QKV_COLS, QKV_COLS + REST_COLS // 2, IN_COLS)

    def body(dq_ref, dk_ref, dv_ref, dr_ref, x_ref, dx2_ref, g_ref, w_hbm, *rest):
        gx_ref, dg_ref, w_scr, sems = rest[-4:]
        first = pl.program_id(0) == 0
        copies = _weight_copies(w_hbm, w_scr, sems, bounds)
        halves = [pl.ds(h * (tm // 2), tm // 2) for h in (0, 1)]

        def finish(dh):
            nrm = [_rms(x_ref[r, :]) for r in halves]
            dg_ref[...] += sum(jnp.sum(d * n[1], axis=0, keepdims=True) for d, n in zip(dh, nrm))
            g = g_ref[...]
            for r, d, (rstd, xh) in zip(halves, dh, nrm):
                th = d * g
                gx_ref[r, :] = rstd * (th - xh * jnp.mean(th * xh, axis=1, keepdims=True)) + dx2_ref[r, :]

        @pl.when(first)
        def _():
            for cp in copies:
                cp.start()
            dg_ref[...] = jnp.zeros_like(dg_ref)
            half_rest = REST_COLS // 2
            lhs = [lambda r: dq_ref[r, :], lambda r: dk_ref[r, :], lambda r: dv_ref[r, :],
                   lambda r: dr_ref[r, 0:half_rest], lambda r: dr_ref[r, half_rest:REST_COLS]]
            dh = [0.0, 0.0]
            for cp, part, r0, r1 in zip(copies, lhs, bounds[:-1], bounds[1:]):
                cp.wait()
                dh = [d + _nn(part(r), w_scr[r0:r1, :]) for d, r in zip(dh, halves)]
            finish(dh)

        @pl.when(jnp.logical_not(first))
        def _():
            finish([(_nn(dq_ref[r, :], w_scr[0:W, :]) + _nn(dk_ref[r, :], w_scr[W:2 * W, :])
                     + _nn(dv_ref[r, :], w_scr[2 * W:3 * W, :]) + _nn(dr_ref[r, :], w_scr[QKV_COLS:IN_COLS, :]))
                    for r in halves])

    tile = lambda w: pl.BlockSpec((tm, w), lambda i: (i, 0))
    return pl.pallas_call(
        body, grid=(T // tm,),
        in_specs=[tile(W), tile(W), tile(W), tile(REST_COLS), tile(D_MODEL), tile(D_MODEL),
                  pl.BlockSpec((1, D_MODEL), lambda i: (0, 0)),
                  pl.BlockSpec(memory_space=pl.ANY)] + _after(after),
        out_specs=[tile(D_MODEL), pl.BlockSpec((1, D_MODEL), lambda i: (0, 0))],
        out_shape=[jax.ShapeDtypeStruct((T, D_MODEL), F32), jax.ShapeDtypeStruct((1, D_MODEL), F32)],
        scratch_shapes=[pltpu.VMEM((IN_COLS, D_MODEL), BF16), pltpu.SemaphoreType.DMA((len(bounds) - 1,))],
        compiler_params=_params(("arbitrary",)), name="inproj_bwd_dx")(
            dq, dk, dv, drest, x2d, dx2, g_norm, w_in_t, *after)


def _inproj_bwd_dw(dq, dk, dv, drest, x2d, g_norm, reduce=False):
    T = x2d.shape[0]
    tm = 512
    nt = T // tm
    W = ATTN_WIDTH
    shard = IN_COLS // N_CHIPS
    half = shard // 2
    row_block = 32

    def body(dq_ref, dk_ref, dv_ref, dr_ref, x_ref, g_ref, *rest):
        if reduce:
            sends, own, acc, ras, narrow, s_sem, r_sem = rest
            x, y, c, chip, peers, peer_chip = _place()
            sib = (x, y, 1 - c)

            def part(k, cc, r0):
                return acc.at[pl.ds(pl.multiple_of(k * shard + cc * half + r0, 8), row_block), :]

            def swap_win(k):
                return _remote(narrow.at[k], ras.at[k], s_sem.at[k], r_sem.at[k], sib)
        else:
            acc = rest[0]

        @pl.when(pl.program_id(0) == 0)
        def _():
            acc[...] = jnp.zeros_like(acc)

        _, xh = _rms(x_ref[...])
        h = (xh * g_ref[...]).astype(BF16)
        acc[0:W, :] += _tn(dq_ref[...], h)
        acc[W:2 * W, :] += _tn(dk_ref[...], h)
        acc[2 * W:3 * W, :] += _tn(dv_ref[...], h)
        acc[QKV_COLS:IN_COLS, :] += _tn(dr_ref[...], h)

        if reduce:
            @pl.when(pl.program_id(0) == nt - 1)
            def _():
                for k in range(N_CHIPS):
                    def to_bf16(i, carry, k=k):
                        r0 = pl.multiple_of(i * row_block, row_block)
                        narrow[k, pl.ds(r0, row_block), :] = part(k, 1 - c, r0)[...].astype(BF16)
                        return carry
                    lax.fori_loop(0, half // row_block, to_bf16, 0)
                    swap_win(k).start()

                def chip_sum(k, r0):
                    return part(k, c, r0)[...] + ras[k, pl.ds(r0, row_block), :].astype(F32)

                for k in range(N_CHIPS):
                    swap_win(k).wait_recv()

                    @pl.when(chip == k)
                    def _(k=k):
                        def mine(i, carry):
                            r0 = pl.multiple_of(i * row_block, row_block)
                            own[pl.ds(r0, row_block), :] = chip_sum(k, r0)
                            return carry
                        lax.fori_loop(0, half // row_block, mine, 0)

                    @pl.when(chip != k)
                    def _(k=k):
                        def other(i, carry):
                            r0 = pl.multiple_of(i * row_block, row_block)
                            sends[(k ^ chip) - 1, pl.ds(r0, row_block), :] = chip_sum(k, r0).astype(BF16)
                            return carry
                        lax.fori_loop(0, half // row_block, other, 0)
                for k in range(N_CHIPS):
                    swap_win(k).wait_send()

    tile = lambda w: pl.BlockSpec((tm, w), lambda i: (i, 0))
    vmem = pl.BlockSpec(memory_space=pltpu.VMEM)
    in_specs = [tile(W), tile(W), tile(W), tile(REST_COLS), tile(D_MODEL), pl.BlockSpec((1, D_MODEL), lambda i: (0, 0))]
    if not reduce:
        return pl.pallas_call(
            body, grid=(nt,), in_specs=in_specs,
            out_specs=pl.BlockSpec((IN_COLS, D_MODEL), lambda i: (0, 0)),
            out_shape=jax.ShapeDtypeStruct((IN_COLS, D_MODEL), F32),
            compiler_params=_params(("arbitrary",)), name="inproj_bwd_dw")(dq, dk, dv, drest, x2d, g_norm)
    quarters = pltpu.VMEM((N_CHIPS, half, D_MODEL), BF16)
    return pl.pallas_call(
        body, grid=(nt,), in_specs=in_specs, out_specs=[vmem] * 2,
        out_shape=[jax.ShapeDtypeStruct((N_CHIPS - 1, half, D_MODEL), BF16),
                   jax.ShapeDtypeStruct((half, D_MODEL), F32)],
        scratch_shapes=[pltpu.VMEM((IN_COLS, D_MODEL), F32), quarters, quarters,
                        pltpu.SemaphoreType.DMA((N_CHIPS,)), pltpu.SemaphoreType.DMA((N_CHIPS,))],
        compiler_params=_params(("arbitrary",)), name="inproj_bwd_dw_reduce")(dq, dk, dv, drest, x2d, g_norm)


def _adamw_update(w, g, m, v):
    nm = ADAM_B1 * m + (1.0 - ADAM_B1) * g
    nv = ADAM_B2 * v + (1.0 - ADAM_B2) * (g * g)
    m_hat = nm / (1.0 - ADAM_B1 ** ADAM_STEP)
    v_hat = nv / (1.0 - ADAM_B2 ** ADAM_STEP)
    return -ADAM_LR * (m_hat / (jnp.sqrt(v_hat) + ADAM_EPS) + ADAM_WD * w), nm, nv


def _adamw(w, g, m, v, name):
    R, C = w.shape
    br = max(r for r in range(8, 257, 8) if R % r == 0)

    def body(w_ref, g_ref, m_ref, v_ref, g_out, d_ref, nm_ref, nv_ref):
        g = g_ref[...]
        g_out[...] = g
        d_ref[...], nm_ref[...], nv_ref[...] = _adamw_update(w_ref[...], g, m_ref[...], v_ref[...])

    spec = pl.BlockSpec((br, C), lambda i: (i, 0))
    return pl.pallas_call(
        body, grid=(R // br,), in_specs=[spec] * 4, out_specs=[spec] * 4,
        out_shape=[jax.ShapeDtypeStruct((R, C), F32)] * 4,
        compiler_params=_params(("arbitrary",)), name=name)(w, g, m, v)


def _adamw_rest(g_packed, ws, ms, vs, whole):
    n = len(ws)
    nw = len(whole)
    row_block = 64

    def body(*refs):
        g_ref = refs[0]
        w_refs, m_refs, v_refs = refs[1:1 + n], refs[1 + n:1 + 2 * n], refs[1 + 2 * n:1 + 3 * n]
        whole_in = [refs[1 + 3 * n + 4 * i:5 + 3 * n + 4 * i] for i in range(nw)]
        outs = refs[1 + 3 * n + 4 * nw:]
        off = 0
        for i, (_, used, padded) in enumerate(_SMALL_PARTS[:n]):
            g = g_ref[off:off + used, :]
            delta, nm, nv = _adamw_update(w_refs[i][...], g, m_refs[i][...], v_refs[i][...])
            outs[4 * i][...], outs[4 * i + 1][...], outs[4 * i + 2][...], outs[4 * i + 3][...] = g, delta, nm, nv
            off += padded
        for i, (w_ref, gw_ref, m_ref, v_ref) in enumerate(whole_in):
            for r0 in range(0, w_ref.shape[0], row_block):
                blk = pl.ds(r0, row_block)
                g = gw_ref[blk, :]
                new = _adamw_update(w_ref[blk, :], g, m_ref[blk, :], v_ref[blk, :])
                for ref, val in zip(outs[4 * (n + i):4 * (n + i) + 4], (g,) + new):
                    ref[blk, :] = val
        outs[4 * (n + nw)][...] = g_ref[_LOSS_ROW:_LOSS_ROW + 1, 0:1]

    outs = pl.pallas_call(
        body, out_shape=[jax.ShapeDtypeStruct(w.shape, F32) for w in ws for _ in range(4)]
        + [jax.ShapeDtypeStruct(four[0].shape, F32) for four in whole for _ in range(4)]
        + [jax.ShapeDtypeStruct((1, 1), F32)],
        compiler_params=_params(), name="adamw_rest")(g_packed, *ws, *ms, *vs, *[a for four in whole for a in four])
    return ([outs[4 * i:4 * i + 4] for i in range(n)], [outs[4 * (n + i):4 * (n + i) + 4] for i in range(nw)],
            outs[4 * (n + nw)])


def _place():
    x, y, c = lax.axis_index("x"), lax.axis_index("y"), lax.axis_index("c")
    chip = 2 * x + y
    peers = [(x, 1 - y), (1 - x, y), (1 - x, 1 - y)]
    peer_chip = [2 * px + py for px, py in peers]
    return x, y, c, chip, peers, peer_chip


def _remote(src, dst, send_sem, recv_sem, dev):
    return pltpu.make_async_remote_copy(src_ref=src, dst_ref=dst, send_sem=send_sem, recv_sem=recv_sem,
                                        device_id=dev, device_id_type=MESH)


def _ag_weights(weights, late=()):
    nw, nl = len(weights), len(late)
    pieces = 2

    def body(*refs):
        srcs, late_srcs = refs[:nw], refs[nw:nw + nl]
        outs, late_bf, late_land = (refs[nw + nl:2 * nw + nl], refs[2 * nw + nl:2 * nw + 2 * nl],
                                    refs[2 * nw + 2 * nl:2 * nw + 3 * nl])
        scr = refs[2 * nw + 3 * nl:]
        wide, narrow = scr[:nw], scr[nw:2 * nw]
        in_sem, put_sem, s_ici, r_ici, s_d2d, r_d2d = scr[2 * nw:]
        x, y, c = lax.axis_index("x"), lax.axis_index("y"), lax.axis_index("c")
        chip = 2 * x + y
        sib = (x, y, 1 - c)
        first = ((x + 1 - c) % 2, (y + c) % 2)
        second = ((x + c) % 2, (y + 1 - c) % 2)
        first_chip, second_chip = 2 * first[0] + first[1], 2 * second[0] + second[1]
        diag_chip = 3 - chip

        parts = [(w, out, pc) for w, out in enumerate(outs) for pc in range(pieces)]

        def rows_of(out, cc, pc):
            rows = out.shape[1] // 2 // pieces
            return pl.ds(pl.multiple_of((cc * pieces + pc) * rows, 16), rows)

        def piece(out, k, cc, pc):
            return out.at[k, rows_of(out, cc, pc), :]

        def ici(w, slot, out, k, dev, pc, src=None):
            blk, sem = piece(out, k, c, pc), (nw * slot + w) * pieces + pc
            return _remote(blk if src is None else src, blk, s_ici.at[sem], r_ici.at[sem], (dev[0], dev[1], c))

        def d2d(w, slot, out, k, cc, pc):
            blk, sem = piece(out, k, cc, pc), (nw * slot + w) * pieces + pc
            return _remote(blk, blk, s_d2d.at[sem], r_d2d.at[sem], sib)

        def read(w, cc, pc):
            rows = rows_of(outs[w], cc, pc)
            return pltpu.make_async_copy(srcs[w].at[rows, :], wide[w].at[rows, :],
                                         in_sem.at[(w * 2 + cc) * pieces + pc])

        order = [(w, cc, pc) for cc in (0, 1) for w in range(nw) for pc in range(pieces)]
        for w, cc, pc in order:
            read(w, (c + cc) % 2, pc).start()
        sent = []
        for w, cc, pc in order[:nw * pieces]:
            rows = rows_of(outs[w], c, pc)
            read(w, c, pc).wait()
            narrow[w][rows, :] = wide[w][rows, :].astype(BF16)
            for slot, dev in enumerate((first, second)):
                sent.append(ici(w, slot, outs[w], chip, dev, pc, src=narrow[w].at[rows, :]))
                sent[-1].start()
        for src, bf, land in zip(late_srcs, late_bf, late_land):
            bf[...] = src[...].astype(BF16)
            land[...] = jnp.zeros_like(land)
            land[chip] = bf[...]
        for w, cc, pc in order[nw * pieces:]:
            rows = rows_of(outs[w], 1 - c, pc)
            read(w, 1 - c, pc).wait()
            narrow[w][rows, :] = wide[w][rows, :].astype(BF16)
        puts = [pltpu.make_async_copy(narrow[w], outs[w].at[chip], put_sem.at[w]) for w in range(nw)]
        for cp in puts:
            cp.start()
        for slot, k, dev in ((0, first_chip, first), (1, second_chip, second), (2, diag_chip, second)):
            for w, out, pc in parts:
                ici(w, slot, out, k, dev, pc).wait_recv()
                if slot == 0:
                    sent.append(ici(w, 2, out, k, second, pc))
                    sent[-1].start()
                sent.append(d2d(w, slot, out, k, c, pc))
                sent[-1].start()
        for slot, k in ((0, second_chip), (1, first_chip), (2, diag_chip)):
            for w, out, pc in parts:
                d2d(w, slot, out, k, 1 - c, pc).wait_recv()
        for cp in sent:
            cp.wait_send()
        for cp in puts:
            cp.wait()

    vmem = pl.BlockSpec(memory_space=pltpu.VMEM)
    far = pl.BlockSpec(memory_space=pl.ANY)
    outs = pl.pallas_call(
        body,
        out_shape=[jax.ShapeDtypeStruct((N_CHIPS,) + w.shape, BF16) for w in weights]
        + [jax.ShapeDtypeStruct(w.shape, BF16) for w in late]
        + [jax.ShapeDtypeStruct((N_CHIPS,) + w.shape, BF16) for w in late],
        in_specs=[far] * nw + [vmem] * nl, out_specs=[far] * nw + [vmem] * (2 * nl),
        scratch_shapes=[pltpu.VMEM(w.shape, F32) for w in weights] + [pltpu.VMEM(w.shape, BF16) for w in weights]
        + [pltpu.SemaphoreType.DMA((2 * nw * pieces,)), pltpu.SemaphoreType.DMA((nw,))]
        + [pltpu.SemaphoreType.DMA((3 * nw * pieces,))] * 4,
        compiler_params=pltpu.CompilerParams(vmem_limit_bytes=VMEM_LIMIT), name="ag_weights")(*weights, *late)
    return outs[:nw], outs[nw:nw + nl], outs[nw + nl:]


_HBM = pl.BlockSpec(memory_space=pltpu.HBM)
_SEM = pl.BlockSpec(memory_space=pltpu.SEMAPHORE)
_ANY = pl.BlockSpec(memory_space=pl.ANY)
_DATAFLOW = pltpu.SideEffectType.DATAFLOW_SIDE_EFFECTING


def _in_hbm(a):
    return pltpu.with_memory_space_constraint(a, pltpu.HBM)


_PEERS_OF = {"gather": 3, "scatter": 3, "direct": 7}


def _exchange_copies(mode, srcs, lands, send_sems, recv_sems):
    nw = len(srcs)
    x, y, c, chip, peers, peer_chip = _place()
    pairs = []
    if mode == "direct":
        targets = [((x, y), chip, 1)] + [(p, k, d) for p, k in zip(peers, peer_chip) for d in (0, 1)]
        for r, ((px, py), k, d) in enumerate(targets):
            for w in range(nw):
                sems = (send_sems.at[nw * r + w], recv_sems.at[nw * r + w], (px, py, (c + d) % 2))
                share = srcs[w].at[k if srcs[w].shape[0] > 1 else 0]
                pairs.append((_remote(share, lands[w].at[r], *sems),) * 2)
        return pairs
    gather = mode == "gather"
    for m, (px, py) in enumerate(peers):
        for w in range(nw):
            sems = (send_sems.at[nw * m + w], recv_sems.at[nw * m + w], (px, py, c))
            if gather:
                pairs.append((_remote(srcs[w], lands[w].at[chip], *sems),
                              _remote(srcs[w], lands[w].at[peer_chip[m]], *sems)))
            else:
                pairs.append((_remote(srcs[w].at[m], lands[w].at[m], *sems),) * 2)
    return pairs


def _exchange_start(mode, srcs, after, name, lands=None):
    nw = len(srcs)
    n_copies = _PEERS_OF[mode] * nw

    after = tuple(after)

    def body(*refs):
        send_sems, recv_sems = refs[2 * nw + len(after)], refs[2 * nw + len(after) + 1]
        for start, _ in _exchange_copies(mode, refs[:nw], refs[nw:2 * nw], send_sems, recv_sems):
            start.start()
        refs[-1][...] = jnp.zeros_like(refs[-1])

    if lands is None:
        shape = {"gather": lambda s: (N_CHIPS,) + s.shape, "scatter": lambda s: s.shape,
                 "direct": lambda s: (_PEERS_OF["direct"],) + s.shape[1:]}[mode]
        lands = [lax.empty(shape(s), s.dtype) for s in srcs]
    lands = [_in_hbm(l) for l in lands]
    return pl.pallas_call(
        body, name=name,
        out_shape=(pltpu.SemaphoreType.DMA((n_copies,)), pltpu.SemaphoreType.DMA((n_copies,)))
        + tuple(pltpu.HBM(s.shape, s.dtype) for s in srcs)
        + tuple(pltpu.HBM(l.shape, l.dtype) for l in lands)
        + (jax.ShapeDtypeStruct((8, LANES), F32),),
        in_specs=[_HBM] * (2 * nw) + [_ANY] * len(after),
        out_specs=(_SEM, _SEM) + (_HBM,) * (2 * nw) + (pl.BlockSpec(memory_space=pltpu.VMEM),),
        input_output_aliases={i: 2 + i for i in range(2 * nw)},
        compiler_params=pltpu.CompilerParams(has_side_effects=_DATAFLOW),
    )(*[_in_hbm(s) for s in srcs], *lands, *after)


def _exchange_wait(mode, started, after, name):
    nw = (len(started) - 3) // 2
    send_sems, recv_sems = started[0], started[1]
    thru = started[2:2 + 2 * nw]

    def body(*refs):
        for _, arrival in _exchange_copies(mode, refs[:nw], refs[nw:2 * nw], refs[2 * nw], refs[2 * nw + 1]):
            arrival.wait_send()
            arrival.wait_recv()

    outs = pl.pallas_call(
        body, name=name,
        out_shape=tuple(pltpu.HBM(t.shape, t.dtype) for t in thru),
        in_specs=[_HBM] * (2 * nw) + [_SEM, _SEM, _ANY], out_specs=(_HBM,) * (2 * nw),
        input_output_aliases={i: i for i in range(2 * nw)},
        compiler_params=pltpu.CompilerParams(has_side_effects=_DATAFLOW),
    )(*thru, send_sems, recv_sems, after)
    return outs[:nw], outs[nw:]


def _reduce_last(owns, landed, g_small, direct_srcs, direct_landed, spread_row0):
    ns, nd = len(owns), len(direct_srcs)
    halves = [o.shape[0] for o in owns]
    row_block = 16
    spread_rows = direct_srcs[-1].shape[1]
    rest0, rest1 = spread_row0, SMALL_ROWS - spread_row0 - spread_rows
    hs = (rest0 + rest1) // 2
    jobs = [(w, p * (halves[w] // 2), halves[w] // 2) for w in range(ns) for p in range(2)]
    n_swaps = len(jobs)
    jobs += [(ns + d, 0, direct_srcs[d].shape[1]) for d in range(nd)]

    def body(*refs):
        own_refs, land_refs, gsm_ref = refs[:ns], refs[ns:2 * ns], refs[2 * ns]
        dsrc_refs, dland_refs = refs[2 * ns + 1:2 * ns + 1 + nd], refs[2 * ns + 1 + nd:2 * ns + 1 + 2 * nd]
        n_in = 2 * ns + 1 + 2 * nd
        out_refs, osm_ref = refs[n_in:n_in + ns + nd - 1], refs[n_in + ns + nd - 1]
        scr = refs[n_in + ns + nd:]
        own_scr, land_scr, dsrc_scr, dland_scr = (scr[:ns], scr[ns:2 * ns], scr[2 * ns:2 * ns + nd],
                                                  scr[2 * ns + nd:2 * ns + 2 * nd])
        o_rest, ra_sm, p_sm, in_sem, s_sem, r_sem, sm_s, sm_r = scr[2 * ns + 2 * nd:]
        x, y, c, chip, peers, peer_chip = _place()
        sib = (x, y, 1 - c)
        half = lambda cc: pl.ds(pl.multiple_of(cc * hs, 8), hs)
        sm_a = _remote(gsm_ref.at[half(1 - c), :], ra_sm, sm_s.at[0], sm_r.at[0], sib)
        sm_a.start()
        swaps = [sm_a]

        def reads(j):
            w, r0, n = jobs[j]
            rows = pl.ds(r0, n)
            if w < ns:
                pairs = [(own_refs[w].at[rows, :], own_scr[w].at[rows, :]),
                         (land_refs[w].at[:, rows, :], land_scr[w].at[:, rows, :])]
            else:
                share = dsrc_refs[w - ns].at[chip if w < ns + nd - 1 else 0]
                pairs = [(share, dsrc_scr[w - ns]), (dland_refs[w - ns], dland_scr[w - ns])]
            return [pltpu.make_async_copy(s, d, in_sem.at[2 * j + i]) for i, (s, d) in enumerate(pairs)]

        for j in range(len(jobs)):
            for cp in reads(j):
                cp.start()
        sm_a.wait_recv()
        p_sm[chip] = gsm_ref[half(c), :] + ra_sm[...]
        for m, (px, py) in enumerate(peers):
            swaps.append(_remote(p_sm.at[chip], p_sm.at[chip], sm_s.at[1 + m], sm_r.at[1 + m], (px, py, c)))
            swaps[-1].start()

        def mine_of(j):
            w, r0, n = jobs[j]
            return out_refs[w].at[pl.ds(pl.multiple_of(c * halves[w] + r0, 8), n), :]

        for j, (w, r0, n) in enumerate(jobs):
            for cp in reads(j):
                cp.wait()

            def total(i, carry, w=w, r0=r0):
                rr = pl.multiple_of(r0 + i * row_block, row_block)
                blk = pl.ds(rr, row_block)
                if w < ns:
                    acc = own_scr[w][blk, :]
                    for m in range(_PEERS_OF["scatter"]):
                        acc = acc + land_scr[w][m, blk, :].astype(F32)
                    out_refs[w][pl.ds(pl.multiple_of(c * halves[w] + rr, row_block), row_block), :] = acc
                else:
                    theirs = lambda r: dland_scr[w - ns][r, blk, :].astype(F32)
                    acc = ((dsrc_scr[w - ns][blk, :].astype(F32) + theirs(0)) + (theirs(1) + theirs(2))) + (
                        (theirs(3) + theirs(4)) + (theirs(5) + theirs(6)))
                    if w < ns + nd - 1:
                        out_refs[w][blk, :] = acc
                    else:
                        osm_ref[pl.ds(pl.multiple_of(spread_row0 + rr, 8), row_block), :] = acc
                return carry
            lax.fori_loop(0, n // row_block, total, 0)
            if w < ns:
                swaps.append(_remote(mine_of(j), mine_of(j), s_sem.at[j], r_sem.at[j], sib))
                swaps[-1].start()
        for m, (px, py) in enumerate(peers):
            _remote(p_sm.at[chip], p_sm.at[peer_chip[m]], sm_s.at[1 + m], sm_r.at[1 + m], (px, py, c)).wait_recv()
        o_rest[half(c), :] = (p_sm[0] + p_sm[1]) + (p_sm[2] + p_sm[3])
        swaps.append(_remote(o_rest.at[half(c), :], o_rest.at[half(c), :], sm_s.at[4], sm_r.at[4], sib))
        swaps[-1].start()
        for j, (w, r0, n) in enumerate(jobs[:n_swaps]):
            theirs = out_refs[w].at[pl.ds(pl.multiple_of((1 - c) * halves[w] + r0, 8), n), :]
            _remote(theirs, theirs, s_sem.at[j], r_sem.at[j], sib).wait_recv()
        _remote(o_rest.at[half(1 - c), :], o_rest.at[half(1 - c), :], sm_s.at[4], sm_r.at[4], sib).wait_recv()
        osm_ref[0:rest0, :] = o_rest[0:rest0, :]
        osm_ref[SMALL_ROWS - rest1:SMALL_ROWS, :] = o_rest[rest0:rest0 + rest1, :]
        for cp in swaps:
            cp.wait_send()

    vmem = pl.BlockSpec(memory_space=pltpu.VMEM)
    far = [pl.BlockSpec(memory_space=pl.ANY)]
    return pl.pallas_call(
        body, out_shape=[jax.ShapeDtypeStruct((2 * o.shape[0], o.shape[1]), F32) for o in owns]
        + [jax.ShapeDtypeStruct(s.shape[1:], F32) for s in direct_srcs[:-1]]
        + [jax.ShapeDtypeStruct((SMALL_ROWS, LANES), F32)],
        in_specs=far * (2 * ns) + [vmem] + far * (2 * nd), out_specs=[vmem] * (ns + nd),
        scratch_shapes=[pltpu.VMEM(o.shape, o.dtype) for o in owns] + [pltpu.VMEM(l.shape, l.dtype) for l in landed]
        + [pltpu.VMEM(s.shape[1:], s.dtype) for s in direct_srcs]
        + [pltpu.VMEM(l.shape, l.dtype) for l in direct_landed]
        + [pltpu.VMEM((2 * hs, LANES), F32), pltpu.VMEM((hs, LANES), F32), pltpu.VMEM((N_CHIPS, hs, LANES), F32),
           pltpu.SemaphoreType.DMA((2 * len(jobs),)),
           pltpu.SemaphoreType.DMA((n_swaps,)), pltpu.SemaphoreType.DMA((n_swaps,)),
           pltpu.SemaphoreType.DMA((5,)), pltpu.SemaphoreType.DMA((5,))],
        compiler_params=pltpu.CompilerParams(vmem_limit_bytes=VMEM_LIMIT),
        name="reduce_last")(*owns, *landed, g_small, *direct_srcs, *direct_landed)


_SMALL_PARTS = (("g_norm", 8, 8), ("w_s", 512, 512), ("b_s", 4, 8), ("g_v", 2, 8), ("g_mem", 8, 8),
                ("g_final", 8, 8), ("loss", 8, 8))
_LOSS_ROW = SMALL_ROWS - 8
_W_S_ROW = 8
assert sum(p for _, _, p in _SMALL_PARTS) == SMALL_ROWS and _SMALL_PARTS[1][0] == "w_s"


def _pack_small(parts, loss_block):
    rows = []
    parts = dict(parts, loss=loss_block)
    for name, used, padded in _SMALL_PARTS:
        if name == "w_s":
            continue
        p = parts[name].reshape(used, LANES)
        if padded > used:
            p = jnp.pad(p, ((0, padded - used), (0, 0)))
        rows.append(p)
    return jnp.concatenate(rows, axis=0)


def _local_step(x, mem, target, g_norm, w_in, w_s, b_s, g_v, g_mem, late_weights, g_final,
                fwd_token=None, on_late=None, on_dw=None):
    B, S, _ = x.shape
    x2d = x.reshape(B * S, D_MODEL)
    t2d = target.reshape(B * S, D_MODEL)
    mem2d = mem.reshape(B * N_MEM, D_MODEL)

    proj = _inproj_fwd(x2d, g_norm, w_in, after=() if fwd_token is None else (fwd_token,))
    w_kv, w_out = late_weights(proj)
    kv = _kv_fwd(mem2d, g_mem, w_kv)
    a, lse = _attn_fwd(proj, B, S)
    w_sT = jnp.swapaxes(w_s, 1, 2)
    b_tab = jnp.repeat(b_s.T, HEAD_DIM, axis=1)
    (dx2, da, drest, loss, d_wout, d_ws, d_bs, d_gv, d_gf, dkv) = _mid(
        x2d, t2d, a, proj, kv, w_s, w_sT, b_tab, g_v, w_out, g_final, B, S)
    d_wkv, d_gmem = _kv_bwd(mem2d, g_mem, w_kv, dkv)
    dq, dk, dv = _attn_bwd(proj, a, lse, da, B, S, after=() if on_late is None else (on_late(d_wkv, d_wout, d_ws),))
    if on_dw is None:
        d_win = _inproj_bwd_dw(dq, dk, dv, drest, x2d, g_norm)
        after = ()
    else:
        d_win = None
        after = (on_dw(*_inproj_bwd_dw(dq, dk, dv, drest, x2d, g_norm, reduce=True)),)
    grad_x, d_gnorm = _inproj_bwd_dx(dq, dk, dv, drest, x2d, dx2, g_norm, w_in, after=after)
    d_bs = d_bs[:, :N_SGU_GROUPS].T
    return (loss, grad_x.reshape(B, S, D_MODEL),
            dict(g_norm=d_gnorm, w_in=d_win, w_s=d_ws, b_s=d_bs, g_v=d_gv, g_mem=d_gmem, w_kv=d_wkv,
                 w_out=d_wout, g_final=d_gf))


def kernel(x, mem, g_norm, w_in, w_sgu_spatial, b_sgu_spatial, g_sgu_v, g_mem, w_mem_kv, w_out, g_final, loss_target, m_g_norm, m_w_in, m_w_sgu_spatial, m_b_sgu_spatial, m_g_sgu_v, m_g_mem, m_w_mem_kv, m_w_out, m_g_final, v_g_norm, v_w_in, v_w_sgu_spatial, v_b_sgu_spatial, v_g_sgu_v, v_g_mem, v_w_mem_kv, v_w_out, v_g_final):
    t = lambda w: jnp.swapaxes(w[0], 0, 1)
    (win_all,), late_shards, late_lands = _ag_weights([t(w_in)], [w_mem_kv[0], w_out[0]])
    w_in_full = win_all.reshape(-1, win_all.shape[-1])
    late = _exchange_start("gather", list(late_shards), (win_all,), "gather_late_start", lands=late_lands)

    def late_weights(proj):
        return [z.reshape(-1, z.shape[-1]) for z in _exchange_wait("gather", late, proj, "gather_late_wait")[1]]

    scatter = {}

    def on_late(d_wkv, d_wout, d_ws):
        d_ws = d_ws.reshape(1, -1, LANES)
        scatter["late"] = _exchange_start("direct", [d_wkv, d_wout, d_ws], (), "scatter_late_start")
        return scatter["late"][-1]

    def on_dw(sends, own):
        scatter["own"] = own
        scatter["started"] = _exchange_start("scatter", [sends], (own,), "scatter_start")
        return scatter["started"][-1]

    loss, grad_x, g = _local_step(
        x, mem, loss_target, g_norm, w_in_full, w_sgu_spatial[0], b_sgu_spatial[0], g_sgu_v, g_mem,
        late_weights, g_final.reshape(1, D_MODEL), fwd_token=late[-1], on_late=on_late, on_dw=on_dw)

    small_names = ("g_norm", "w_s", "b_s", "g_v", "g_mem", "g_final")
    g_small = _pack_small({n: g[n] for n in small_names if n != "w_s"}, loss)
    late_srcs, late_landed = _exchange_wait("direct", scatter["late"], g_small, "scatter_late_wait")
    _, landed = _exchange_wait("scatter", scatter["started"], late_landed[0], "scatter_wait")
    gr_in, gr_kv, gr_out, gr_small = _reduce_last([scatter["own"]], landed, g_small, late_srcs, late_landed, _W_S_ROW)

    small_w = (g_norm, w_sgu_spatial, b_sgu_spatial, g_sgu_v, g_mem, g_final)
    small_m = (m_g_norm, m_w_sgu_spatial, m_b_sgu_spatial, m_g_sgu_v, m_g_mem, m_g_final)
    small_v = (v_g_norm, v_w_sgu_spatial, v_b_sgu_spatial, v_g_sgu_v, v_g_mem, v_g_final)
    rows = lambda ws: [w.reshape(-1, LANES) for w in ws]
    small_new, ((gr_kv, d_kv, nm_kv, nv_kv), (gr_out, d_out, nm_out, nv_out)), loss = _adamw_rest(
        gr_small, rows(small_w), rows(small_m), rows(small_v),
        [(w_mem_kv[0], gr_kv, m_w_mem_kv[0], v_w_mem_kv[0]), (w_out[0], gr_out, m_w_out[0], v_w_out[0])])
    loss = loss.reshape(())
    small = [[z.reshape(w.shape) for z in four] for w, four in zip(small_w, small_new)]
    gr_in, d_in, nm_in, nv_in = [jnp.swapaxes(z, 0, 1)
                                 for z in _adamw(t(w_in), gr_in, t(m_w_in), t(v_w_in), "adamw_w_in")]

    def leaves(kind, big_in, big_kv, big_out):
        s_norm, s_ws, s_bs, s_gv, s_gmem, s_gf = [four[kind] for four in small]
        return [s_norm, big_in[None], s_ws, s_bs, s_gv, s_gmem, big_kv[None], big_out[None], s_gf]

    return (loss, grad_x, *leaves(0, gr_in, gr_kv, gr_out), *leaves(1, d_in, d_kv, d_out),
            *leaves(2, nm_in, nm_kv, nm_out), *leaves(3, nv_in, nv_kv, nv_out))
```

```python
import functools

import jax
import jax.numpy as jnp
from jax import lax
from jax.experimental import pallas as pl
from jax.experimental.pallas import tpu as pltpu

F32 = jnp.float32
BF16 = jnp.bfloat16
MESH = pl.DeviceIdType.MESH

D_MODEL = 1024
ATTN_WIDTH = 512
SGU_WIDTH = 256
MEM_WIDTH = 256
N_MEM = 256
IN_COLS = 3328
QKV_COLS = 3 * ATTN_WIDTH
REST_COLS = IN_COLS - QKV_COLS
SGU_CHUNK = 128
N_SGU_GROUPS = 4
EPS = 1e-6
NEG_INF = -1e30
DILATIONS = (1, 4, 16)
RADIUS = 64
Q_BLOCK = 128
LANES = 128
HEAD_DIM = 64

ADAM_LR = 0.001
ADAM_B1 = 0.9
ADAM_B2 = 0.999
ADAM_EPS = 1e-08
ADAM_WD = 0.01
ADAM_STEP = 10

N_CHIPS = 4
VMEM_LIMIT = 56 * 1024 * 1024
SMALL_ROWS = 560


def _params(sem=None, vmem=VMEM_LIMIT):
    return pltpu.CompilerParams(dimension_semantics=sem, vmem_limit_bytes=vmem)


def _nn(a, b):
    return jnp.dot(a, b, preferred_element_type=F32)


def _nt(a, b):
    return lax.dot_general(a, b, (((1,), (1,)), ((), ())), preferred_element_type=F32)


def _tn(a, b):
    return lax.dot_general(a, b, (((0,), (0,)), ((), ())), preferred_element_type=F32)


def _rms(x):
    r = lax.rsqrt(jnp.mean(x * x, axis=-1, keepdims=True) + EPS)
    return r, x * r


def _head_masks():
    lane = lax.broadcasted_iota(jnp.int32, (1, LANES), 1)
    lo = lane < HEAD_DIM
    return lo, (lo.astype(F32), (~lo).astype(F32))


def _silu_parts(z):
    s = jax.nn.sigmoid(z)
    return z * s, s * (1.0 + z * (1.0 - s))


def _gelu_parts(x):
    c = 0.7978845608028654
    x2 = x * x
    s = jax.nn.sigmoid((2.0 * c) * (x + 0.044715 * (x * x2)))
    return x * s, s * (1.0 + x * (1.0 - s) * ((2.0 * c) * (1.0 + 3.0 * 0.044715 * x2)))


def _after(tokens):
    return [pl.BlockSpec(memory_space=pl.ANY)] * len(tokens)


def _inproj_fwd(x2d, g_norm, w_in_t, after=()):
    T = x2d.shape[0]
    tm = 512

    def body(x_ref, g_ref, w_ref, *rest):
        o_ref = rest[-1]
        _, xh = _rms(x_ref[...])
        h = (xh * g_ref[...]).astype(BF16)
        o_ref[...] = _nt(h, w_ref[...])

    return pl.pallas_call(
        body, grid=(T // tm,),
        in_specs=[pl.BlockSpec((tm, D_MODEL), lambda i: (i, 0)),
                  pl.BlockSpec((1, D_MODEL), lambda i: (0, 0)),
                  pl.BlockSpec((IN_COLS, D_MODEL), lambda i: (0, 0))] + _after(after),
        out_specs=pl.BlockSpec((tm, IN_COLS), lambda i: (i, 0)),
        out_shape=jax.ShapeDtypeStruct((T, IN_COLS), F32),
        compiler_params=_params(("arbitrary",)), name="inproj_fwd")(x2d, g_norm, w_in_t, *after)


def _kv_fwd(mem2d, g_mem, w_kv):
    Tm = mem2d.shape[0]

    def body(m_ref, g_ref, w_ref, o_ref):
        _, mh = _rms(m_ref[...])
        o_ref[...] = _nn((mh * g_ref[...]).astype(BF16), w_ref[...])

    return pl.pallas_call(
        body, out_shape=jax.ShapeDtypeStruct((Tm, 2 * MEM_WIDTH), F32),
        compiler_params=_params(), name="kv_fwd")(mem2d, g_mem, w_kv)


def _kv_bwd(mem2d, g_mem, w_kv, dkv):
    Tm = mem2d.shape[0]

    def body(m_ref, g_ref, w_ref, dkv_ref, dw_ref, dg_ref):
        _, mh = _rms(m_ref[...])
        memn = (mh * g_ref[...]).astype(BF16)
        dkvb = dkv_ref[...].astype(BF16)
        dw = _tn(memn, dkvb).astype(BF16)
        for k in range(N_CHIPS):
            dw_ref[k] = dw[k * (D_MODEL // N_CHIPS):(k + 1) * (D_MODEL // N_CHIPS), :]
        dmemn = _nt(dkvb, w_ref[...])
        dg_ref[...] = jnp.sum(dmemn * mh, axis=0, keepdims=True)

    return pl.pallas_call(
        body, out_shape=(jax.ShapeDtypeStruct((N_CHIPS, D_MODEL // N_CHIPS, 2 * MEM_WIDTH), BF16),
                         jax.ShapeDtypeStruct((1, D_MODEL), F32)),
        compiler_params=_params(), name="kv_bwd")(mem2d, g_mem, w_kv, dkv)


def _attn_geometry(S):
    geom = []
    for d in DILATIONS:
        L = S // d
        assert L % Q_BLOCK == 0
        geom.append((d, L, min(2 * Q_BLOCK, L), L // Q_BLOCK))
    return geom


def _init_bias(bias_scr, geom, hp):
    row = lax.broadcasted_iota(jnp.int32, (Q_BLOCK, 2 * Q_BLOCK), 0)
    col = lax.broadcasted_iota(jnp.int32, (Q_BLOCK, 2 * Q_BLOCK), 1)
    for j in (0, 1):
        bits = (126 - (2 * hp + j)) * (1 << 23)
        slope = lax.bitcast_convert_type(jnp.full((1, 1), bits, jnp.int32), F32)
        for di, (d, _, _, _) in enumerate(geom):
            for cls, off in enumerate((0, -RADIUS, -2 * RADIUS)):
                dist = jnp.abs(col - row + off)
                bias_scr[di * 6 + cls * 2 + j] = jnp.where(
                    dist <= RADIUS, -(slope * float(d)) * dist.astype(F32), NEG_INF)


SPLIT = 4
COPY_ROWS = 256


def _by4_rows(S, step):
    per_class = S // SPLIT // COPY_ROWS
    r, j = step // per_class, step % per_class
    return (pl.ds(r + SPLIT * j * COPY_ROWS, COPY_ROWS, stride=SPLIT),
            pl.ds(r * (S // SPLIT) + j * COPY_ROWS, COPY_ROWS))


def _to_by4(src, dst, S):
    for i in range(S // COPY_ROWS):
        natural, by4 = _by4_rows(S, i)
        dst[by4, :] = src[natural, :]


def _block_slices(d, L, KW, nqb, r, qb, S):
    qs = qb * Q_BLOCK
    ks = jnp.clip(qs - RADIUS, 0, L - KW)
    cls = jnp.where(qb == 0, 0, jnp.where(qb == nqb - 1, 2, 1))
    if d == 1:
        qsl = pl.ds(pl.multiple_of(qs, Q_BLOCK), Q_BLOCK)
        ksl = pl.ds(pl.multiple_of(ks, RADIUS), KW)
    elif d == SPLIT:
        qsl = pl.ds(pl.multiple_of(r * L + qs, Q_BLOCK), Q_BLOCK)
        ksl = pl.ds(pl.multiple_of(r * L + ks, RADIUS), KW)
    else:
        sub = d // SPLIT
        base = (r % SPLIT) * (S // SPLIT) + r // SPLIT
        qsl = pl.ds(base + qs * sub, Q_BLOCK, stride=sub)
        ksl = pl.ds(base + ks * sub, KW, stride=sub)
    return qsl, ksl, cls


def _for_groups(geom, S, group, fn):
    for di, (d, L, KW, nqb) in enumerate(geom):
        n = group[di]
        assert (d * nqb) % n == 0

        def step(it, carry, di=di, d=d, L=L, KW=KW, nqb=nqb, n=n):
            slices = []
            for g in range(n):
                i = it * n + g
                slices.append(_block_slices(d, L, KW, nqb, i // nqb, i % nqb, S))
            fn(di, KW, slices)
            return carry
        lax.fori_loop(0, d * nqb // n, step, 0)


def _attn_fwd(proj, B, S):
    T = B * S
    geom = _attn_geometry(S)
    n_pairs = ATTN_WIDTH // LANES

    def body(q_ref, k_ref, v_ref, a_ref, lse_ref, bias_scr, q4, k4, v4, *per_dilation):
        o_scr, m_scr, l_scr = per_dilation[0:3], per_dilation[3:6], per_dilation[6:9]
        lo, hm = _head_masks()
        pair = pl.program_id(0)

        @pl.when(pl.program_id(1) == 0)
        def _():
            _init_bias(bias_scr, geom, pair)
        for src, dst in ((q_ref, q4), (k_ref, k4), (v_ref, v4)):
            _to_by4(src, dst, S)

        def group(di, KW, all_slices):
            run = 8
            for first in range(0, len(all_slices), run):
                some(di, KW, all_slices[first:first + run])

        def some(di, KW, slices):
            chains = [(g, j) for g in range(len(slices)) for j in (0, 1)]
            q_src, k_src, v_src = (q_ref, k_ref, v_ref) if di == 0 else (q4, k4, v4)
            q = [q_src[qsl, :] for qsl, _, _ in slices]
            kw = [k_src[ksl, :].astype(BF16) for _, ksl, _ in slices]
            vw = [v_src[ksl, :].astype(BF16) for _, ksl, _ in slices]
            s = {(g, j): _nt((q[g] * (hm[j] * 0.125)).astype(BF16), kw[g])
                 + bias_scr[di * 6 + slices[g][2] * 2 + j, :, pl.ds(0, KW)] for g, j in chains}
            m = {c: jnp.max(s[c], axis=1, keepdims=True) for c in chains}
            p = {c: jnp.exp(s[c] - m[c]) for c in chains}
            l = {c: jnp.sum(p[c], axis=1, keepdims=True) for c in chains}
            o = {(g, j): _nn(p[(g, j)].astype(BF16), vw[g]) for g, j in chains}
            for g, (qsl, _, _) in enumerate(slices):
                o_scr[di][qsl, :] = jnp.where(lo, o[(g, 0)], o[(g, 1)])
                m_scr[di][qsl, :] = jnp.where(lo, m[(g, 0)], m[(g, 1)])
                l_scr[di][qsl, :] = jnp.where(lo, l[(g, 0)], l[(g, 1)])

        _for_groups(geom, S, (16, 16, 16), group)

        for i in range(S // COPY_ROWS):
            natural, by4 = _by4_rows(S, i)
            rows = [natural, by4, by4]
            ms = [m_scr[di][rows[di], :] for di in range(3)]
            mx = jnp.maximum(jnp.maximum(ms[0], ms[1]), ms[2])
            num = 0.0
            den = 0.0
            for di in range(3):
                w = jnp.exp(ms[di] - mx)
                num = num + w * o_scr[di][rows[di], :]
                den = den + w * l_scr[di][rows[di], :]
            a_ref[natural, :] = num / den
            lse_ref[natural, :] = mx + jnp.log(den)

    blk = lambda off: pl.BlockSpec((S, LANES), lambda h, b, off=off: (b, off + h))
    out_blk = pl.BlockSpec((S, LANES), lambda h, b: (b, h))
    return pl.pallas_call(
        body, grid=(n_pairs, B),
        in_specs=[blk(0), blk(n_pairs), blk(2 * n_pairs)],
        out_specs=[out_blk, out_blk],
        out_shape=[jax.ShapeDtypeStruct((T, ATTN_WIDTH), F32)] * 2,
        scratch_shapes=[pltpu.VMEM((18, Q_BLOCK, 2 * Q_BLOCK), F32)] + [pltpu.VMEM((S, LANES), F32)] * 12,
        compiler_params=_params(("arbitrary", "arbitrary")), name="attn_fwd")(proj, proj, proj)


def _attn_bwd(proj, a, lse, da, B, S, after=()):
    T = B * S
    geom = _attn_geometry(S)
    n_pairs = ATTN_WIDTH // LANES

    def body(q_ref, k_ref, v_ref, a_ref, lse_ref, do_ref, *rest):
        dq_ref, dk_ref, dv_ref, bias_scr = rest[len(after):len(after) + 4]
        scr = rest[len(after) + 4:]
        acc = (scr[0:3], scr[3:6])
        natural_in = (q_ref, k_ref, v_ref, a_ref, lse_ref, do_ref)
        by4_in = scr[6:12]
        _, hm = _head_masks()
        pair = pl.program_id(0)

        @pl.when(pl.program_id(1) == 0)
        def _():
            _init_bias(bias_scr, geom, pair)
        for ref in scr[0:6]:
            ref[...] = jnp.zeros_like(ref)
        for src, dst in zip(natural_in, by4_in):
            _to_by4(src, dst, S)

        def group(di, KW, all_slices):
            run = (4, 4, 8)[di]
            for first in range(0, len(all_slices), run):
                some(di, KW, all_slices[first:first + run])

        def some(di, KW, slices):
            n = len(slices)
            chains = [(g, j) for g in range(n) for j in (0, 1)]
            q_src, k_src, v_src, a_src, lse_src, do_src = natural_in if di == 0 else by4_in
            dq_scr, dk_scr, dv_scr = acc[0 if di == 0 else 1]
            q = [q_src[qsl, :] for qsl, _, _ in slices]
            do = [do_src[qsl, :] for qsl, _, _ in slices]
            doa = [do[g] * a_src[slices[g][0], :] for g in range(n)]
            lse_q = [lse_src[qsl, :] for qsl, _, _ in slices]
            kw = [k_src[ksl, :].astype(BF16) for _, ksl, _ in slices]
            vw = [v_src[ksl, :].astype(BF16) for _, ksl, _ in slices]
            qj = {(g, j): (q[g] * (hm[j] * 0.125)).astype(BF16) for g, j in chains}
            doj = {(g, j): (do[g] * hm[j]).astype(BF16) for g, j in chains}
            s = {(g, j): _nt(qj[(g, j)], kw[g])
                 + bias_scr[di * 6 + slices[g][2] * 2 + j, :, pl.ds(0, KW)] for g, j in chains}
            dp = {(g, j): _nt(doj[(g, j)], vw[g]) for g, j in chains}
            dsum = {(g, j): jnp.sum(doa[g] * hm[j], axis=1, keepdims=True) for g, j in chains}
            p = {(g, j): jnp.exp(s[(g, j)] - lse_q[g][:, HEAD_DIM * j:HEAD_DIM * j + 1]) for g, j in chains}
            ds = {c: (p[c] * (dp[c] - dsum[c])).astype(BF16) for c in chains}
            pb = {c: p[c].astype(BF16) for c in chains}
            dq = [_nn(ds[(g, 0)], kw[g]) * (hm[0] * 0.125) + _nn(ds[(g, 1)], kw[g]) * (hm[1] * 0.125)
                  for g in range(n)]
            both = lambda t, g: jnp.concatenate([t[(g, 0)], t[(g, 1)]], axis=0)
            dkw = [_tn(both(ds, g), both(qj, g)) for g in range(n)]
            dvw = [_tn(both(pb, g), both(doj, g)) for g in range(n)]
            for g, (qsl, ksl, _) in enumerate(slices):
                dq_scr[qsl, :] = dq_scr[qsl, :] + dq[g]
                dk_scr[ksl, :] = dk_scr[ksl, :] + dkw[g]
                dv_scr[ksl, :] = dv_scr[ksl, :] + dvw[g]

        _for_groups(geom, S, (16, 16, 16), group)

        for i in range(S // COPY_ROWS):
            natural, by4 = _by4_rows(S, i)
            for nat, split in zip(*acc):
                nat[natural, :] = nat[natural, :] + split[by4, :]
        for out, nat in zip((dq_ref, dk_ref, dv_ref), acc[0]):
            out[...] = nat[...].astype(BF16)

    blk = lambda off: pl.BlockSpec((S, LANES), lambda h, b, off=off: (b, off + h))
    return pl.pallas_call(
        body, grid=(n_pairs, B),
        in_specs=[blk(0), blk(n_pairs), blk(2 * n_pairs), blk(0), blk(0), blk(0)] + _after(after),
        out_specs=[blk(0), blk(0), blk(0)],
        out_shape=[jax.ShapeDtypeStruct((T, ATTN_WIDTH), BF16)] * 3,
        scratch_shapes=[pltpu.VMEM((18, Q_BLOCK, 2 * Q_BLOCK), F32)] + [pltpu.VMEM((S, LANES), F32)] * 12,
        compiler_params=_params(("arbitrary", "arbitrary")), name="attn_bwd")(proj, proj, proj, a, lse, da, *after)


def _mid(x2d, t2d, a, proj, kv, w_s, w_sT, b_tab, g_v, w_out, g_final, B, S):
    T = B * S
    tm = 512
    nt = S // tm
    halves = 2
    hrows = tm // halves

    def body(x_ref, t_ref, a_ref, za_ref, ub_ref, vb_ref, zb_ref, qm_ref, zm_ref, kv_ref,
              ws_ref, wsT_ref, btab_ref, gv_ref, wout_ref, gf_ref,
              dx2_ref, da_ref, drest_ref, loss_ref, dwout_bf_ref, dws_ref, dbs_ref, dgv_ref, dgf_ref, dkv_ref,
              dbtab_scr, dwout_ref):
        b = pl.program_id(0)
        t = pl.program_id(1)
        first = jnp.logical_and(b == 0, t == 0)
        last = jnp.logical_and(b == B - 1, t == nt - 1)
        _, hm = _head_masks()
        lane_g = lax.broadcasted_iota(jnp.int32, (1, SGU_WIDTH), 1) // HEAD_DIM
        gm = [(lane_g == g).astype(F32) for g in range(N_SGU_GROUPS)]
        H = range(halves)
        rows = [pl.ds(h * hrows, hrows) for h in H]
        ld = lambda ref: [ref[r, :] for r in rows]
        cat = lambda parts, axis: jnp.concatenate(parts, axis=axis)
        chunks = [slice(ci * SGU_CHUNK, (ci + 1) * SGU_CHUNK) for ci in range(hrows // SGU_CHUNK)]
        pairs = [slice(pr * LANES, (pr + 1) * LANES) for pr in range(2)]
        heads = [(pr, j) for pr in range(2) for j in (0, 1)]

        @pl.when(first)
        def _():
            loss_ref[...] = jnp.zeros_like(loss_ref)
            dwout_ref[...] = jnp.zeros_like(dwout_ref)
            dws_ref[...] = jnp.zeros_like(dws_ref)
            dbs_ref[...] = jnp.zeros_like(dbs_ref)
            dgv_ref[...] = jnp.zeros_like(dgv_ref)
            dgf_ref[...] = jnp.zeros_like(dgf_ref)
            dbtab_scr[...] = jnp.zeros_like(dbtab_scr)

        @pl.when(t == 0)
        def _():
            dkv_ref[...] = jnp.zeros_like(dkv_ref)

        a_val = ld(a_ref)
        sil_a = [_silu_parts(z) for z in ld(za_ref)]
        gated_a = [s[0] * a for s, a in zip(sil_a, a_val)]
        u = [_gelu_parts(z) for z in ld(ub_ref)]
        vv = [_gelu_parts(z) for z in ld(vb_ref)]
        vnorm = [_rms(v[0]) for v in vv]
        gv = gv_ref[...]
        vn = [(n[1] * gv).astype(BF16) for n in vnorm]
        w_cat = cat([ws_ref[g].astype(BF16) for g in range(N_SGU_GROUPS)], 1)
        wT_cat = cat([wsT_ref[g].astype(BF16) for g in range(N_SGU_GROUPS)], 1)
        gmb = [m.astype(BF16) for m in gm]
        by_group = lambda chunk: cat([chunk * gmb[g] for g in range(N_SGU_GROUPS)], 0)
        btab = btab_ref[...]
        mixed = [cat([btab + _nn(w_cat, by_group(vn[h][c, :])) for c in chunks], 0) for h in H]
        sg = [u[h][0] * mixed[h] for h in H]
        sil_b = [_silu_parts(z) for z in ld(zb_ref)]
        gated_b = [sil_b[h][0] * sg[h] for h in H]

        kvv = kv_ref[...].astype(BF16)
        kp = [kvv[:, p] for p in pairs]
        vp = [kvv[:, MEM_WIDTH + pr * LANES:MEM_WIDTH + (pr + 1) * LANES] for pr in range(2)]
        qm = ld(qm_ref)
        qj = {(h, pr, j): (qm[h][:, pairs[pr]] * (hm[j] * 0.125)).astype(BF16) for h in H for pr, j in heads}
        sc = {k: _nt(qj[k], kp[k[1]]) for k in qj}
        ex = {k: jnp.exp(sc[k] - jnp.max(sc[k], axis=1, keepdims=True)) for k in qj}
        prob = {k: ex[k] * (1.0 / jnp.sum(ex[k], axis=1, keepdims=True)) for k in qj}
        probb = {k: prob[k].astype(BF16) for k in qj}
        mo = [cat([sum(_nn(probb[(h, pr, j)], vp[pr]) * hm[j] for j in (0, 1)) for pr in range(2)], 1) for h in H]
        sil_m = [_silu_parts(z) for z in ld(zm_ref)]
        gated_m = [sil_m[h][0] * mo[h] for h in H]

        gated = [cat([gated_a[h], gated_b[h], gated_m[h]], 1).astype(BF16) for h in H]
        wout = wout_ref[...]
        x_in = ld(x_ref)
        x2 = [x_in[h] + _nn(gated[h], wout) for h in H]
        fin = [_rms(z) for z in x2]
        gf = gf_ref[...]
        tgt = ld(t_ref)
        err = [fin[h][1] * gf - tgt[h] for h in H]
        loss_ref[...] += sum(jnp.sum(e * e) for e in err) * (0.5 / D_MODEL)

        dy = [e * (1.0 / D_MODEL) for e in err]
        dgf_ref[...] += sum(jnp.sum(dy[h] * fin[h][1], axis=0, keepdims=True) for h in H)
        gdy = [d * gf for d in dy]
        dx2 = [fin[h][0] * (gdy[h] - fin[h][1] * jnp.mean(gdy[h] * fin[h][1], axis=1, keepdims=True)) for h in H]
        for h in H:
            dx2_ref[rows[h], :] = dx2[h]
        dx2b = [d.astype(BF16) for d in dx2]
        dgated = [_nt(d, wout) for d in dx2b]
        dwout_ref[...] += _tn(cat(gated, 0), cat(dx2b, 0))
        dga = [d[:, 0:ATTN_WIDTH] for d in dgated]
        dgb = [d[:, ATTN_WIDTH:ATTN_WIDTH + SGU_WIDTH] for d in dgated]
        dgm = [d[:, ATTN_WIDTH + SGU_WIDTH:] for d in dgated]

        for h in H:
            da_ref[rows[h], :] = dga[h] * sil_a[h][0]
        dza = [dga[h] * a_val[h] * sil_a[h][1] for h in H]

        dsg = [dgb[h] * sil_b[h][0] for h in H]
        dzb = [dgb[h] * sg[h] * sil_b[h][1] for h in H]
        dub = [dsg[h] * mixed[h] * u[h][1] for h in H]
        dmixed = [dsg[h] * u[h][0] for h in H]
        dmixed_b = [d.astype(BF16) for d in dmixed]
        dvn = [cat([_nn(wT_cat, by_group(dmixed_b[h][c, :])) for c in chunks], 0) for h in H]
        for g in range(N_SGU_GROUPS):
            dws_ref[g] += sum(_nt((dmixed[h][c, :] * gm[g]).astype(BF16), vn[h][c, :]) for h in H for c in chunks)
        dbtab_scr[...] += sum(dmixed[h][c, :] for h in H for c in chunks)
        dgv_ref[...] += sum(jnp.sum(dvn[h] * vnorm[h][1], axis=0, keepdims=True) for h in H)
        tv = [d * gv for d in dvn]
        dvv = [vnorm[h][0] * (tv[h] - vnorm[h][1] * jnp.mean(tv[h] * vnorm[h][1], axis=1, keepdims=True)) for h in H]
        dvb = [dvv[h] * vv[h][1] for h in H]

        dmo = [dgm[h] * sil_m[h][0] for h in H]
        dzm = [dgm[h] * mo[h] * sil_m[h][1] for h in H]
        dmoj = {(h, pr, j): (dmo[h][:, pairs[pr]] * hm[j]).astype(BF16) for h in H for pr, j in heads}
        dp = {k: _nt(dmoj[k], vp[k[1]]) for k in qj}
        ds = {k: (prob[k] * (dp[k] - jnp.sum(dp[k] * prob[k], axis=1, keepdims=True))).astype(BF16) for k in qj}
        dqm = [cat([sum(_nn(ds[(h, pr, j)], kp[pr]) * (hm[j] * 0.125) for j in (0, 1)) for pr in range(2)], 1)
               for h in H]
        every = lambda tbl, pr: cat([tbl[(h, pr, j)] for h in H for j in (0, 1)], 0)
        dk = [_tn(every(ds, pr), every(qj, pr)) for pr in range(2)]
        dv = [_tn(every(probb, pr), every(dmoj, pr)) for pr in range(2)]
        dkv_ref[...] += cat(dk + dv, 1)

        for h in H:
            drest_ref[rows[h], :] = cat([dza[h], dub[h], dvb[h], dzb[h], dqm[h], dzm[h]], 1).astype(BF16)

        @pl.when(last)
        def _():
            lane = lax.broadcasted_iota(jnp.int32, (1, LANES), 1)
            dbt = dbtab_scr[...]
            out = jnp.zeros((SGU_CHUNK, LANES), F32)
            for g in range(N_SGU_GROUPS):
                out = out + jnp.where(lane == g, jnp.sum(dbt * gm[g], axis=1, keepdims=True), 0.0)
            dbs_ref[...] = out
            for r0 in range(0, D_MODEL, SGU_CHUNK):
                k, row = divmod(r0, D_MODEL // N_CHIPS)
                dwout_bf_ref[k, row:row + SGU_CHUNK, :] = dwout_ref[r0:r0 + SGU_CHUNK, :].astype(BF16)

    tile = lambda w, cb: pl.BlockSpec((tm, w), lambda b, t, cb=cb: (b * nt + t, cb))
    const = lambda shape: pl.BlockSpec(shape, lambda b, t, n=len(shape): (0,) * n)
    return pl.pallas_call(
        body, grid=(B, nt),
        in_specs=[tile(D_MODEL, 0), tile(D_MODEL, 0), tile(ATTN_WIDTH, 0),
                  tile(ATTN_WIDTH, 3),
                  tile(SGU_WIDTH, 8), tile(SGU_WIDTH, 9), tile(SGU_WIDTH, 10),
                  tile(MEM_WIDTH, 11), tile(MEM_WIDTH, 12),
                  pl.BlockSpec((N_MEM, 2 * MEM_WIDTH), lambda b, t: (b, 0)),
                  const((N_SGU_GROUPS, SGU_CHUNK, SGU_CHUNK)), const((N_SGU_GROUPS, SGU_CHUNK, SGU_CHUNK)),
                  const((SGU_CHUNK, SGU_WIDTH)), const((1, SGU_WIDTH)),
                  const((D_MODEL, D_MODEL)), const((1, D_MODEL))],
        out_specs=[tile(D_MODEL, 0), tile(ATTN_WIDTH, 0), tile(REST_COLS, 0),
                   const((8, LANES)), const((N_CHIPS, D_MODEL // N_CHIPS, D_MODEL)),
                   const((N_SGU_GROUPS, SGU_CHUNK, SGU_CHUNK)), const((SGU_CHUNK, LANES)),
                   const((1, SGU_WIDTH)), const((1, D_MODEL)),
                   pl.BlockSpec((N_MEM, 2 * MEM_WIDTH), lambda b, t: (b, 0))],
        out_shape=[jax.ShapeDtypeStruct((T, D_MODEL), F32), jax.ShapeDtypeStruct((T, ATTN_WIDTH), F32),
                   jax.ShapeDtypeStruct((T, REST_COLS), BF16),
                   jax.ShapeDtypeStruct((8, LANES), F32),
                   jax.ShapeDtypeStruct((N_CHIPS, D_MODEL // N_CHIPS, D_MODEL), BF16),
                   jax.ShapeDtypeStruct((N_SGU_GROUPS, SGU_CHUNK, SGU_CHUNK), F32),
                   jax.ShapeDtypeStruct((SGU_CHUNK, LANES), F32),
                   jax.ShapeDtypeStruct((1, SGU_WIDTH), F32), jax.ShapeDtypeStruct((1, D_MODEL), F32),
                   jax.ShapeDtypeStruct((B * N_MEM, 2 * MEM_WIDTH), F32)],
        scratch_shapes=[pltpu.VMEM((SGU_CHUNK, SGU_WIDTH), F32), pltpu.VMEM((D_MODEL, D_MODEL), F32)],
        compiler_params=_params(("arbitrary", "arbitrary"), vmem=VMEM_LIMIT + 2 * 1024 * 1024), name="mid")(
            x2d, t2d, a, proj, proj, proj, proj, proj, proj, kv, w_s, w_sT, b_tab, g_v, w_out, g_final)


def _inproj_bwd_dx(dq, dk, dv, drest, x2d, dx2, g_norm, w_in_t, after=()):
    T = x2d.shape[0]
    tm = 512
    W = ATTN_WIDTH

    def body(dq_ref, dk_ref, dv_ref, dr_ref, x_ref, dx2_ref, g_ref, w_ref, *rest):
        gx_ref, dg_ref = rest[-2:]

        @pl.when(pl.program_id(0) == 0)
        def _():
            dg_ref[...] = jnp.zeros_like(dg_ref)

        halves = [pl.ds(h * (tm // 2), tm // 2) for h in (0, 1)]
        dh = [(_nn(dq_ref[r, :], w_ref[0:W, :]) + _nn(dk_ref[r, :], w_ref[W:2 * W, :])
               + _nn(dv_ref[r, :], w_ref[2 * W:3 * W, :]) + _nn(dr_ref[r, :], w_ref[QKV_COLS:IN_COLS, :]))
              for r in halves]
        nrm = [_rms(x_ref[r, :]) for r in halves]
        dg_ref[...] += sum(jnp.sum(d * n[1], axis=0, keepdims=True) for d, n in zip(dh, nrm))
        g = g_ref[...]
        for r, d, (rstd, xh) in zip(halves, dh, nrm):
            th = d * g
            gx_ref[r, :] = rstd * (th - xh * jnp.mean(th * xh, axis=1, keepdims=True)) + dx2_ref[r, :]

    tile = lambda w: pl.BlockSpec((tm, w), lambda i: (i, 0))
    return pl.pallas_call(
        body, grid=(T // tm,),
        in_specs=[tile(W), tile(W), tile(W), tile(REST_COLS), tile(D_MODEL), tile(D_MODEL),
                  pl.BlockSpec((1, D_MODEL), lambda i: (0, 0)),
                  pl.BlockSpec((IN_COLS, D_MODEL), lambda i: (0, 0))] + _after(after),
        out_specs=[tile(D_MODEL), pl.BlockSpec((1, D_MODEL), lambda i: (0, 0))],
        out_shape=[jax.ShapeDtypeStruct((T, D_MODEL), F32), jax.ShapeDtypeStruct((1, D_MODEL), F32)],
        compiler_params=_params(("arbitrary",)), name="inproj_bwd_dx")(
            dq, dk, dv, drest, x2d, dx2, g_norm, w_in_t, *after)


def _inproj_bwd_dw(dq, dk, dv, drest, x2d, g_norm, reduce=False):
    T = x2d.shape[0]
    tm = 512
    nt = T // tm
    W = ATTN_WIDTH
    shard = IN_COLS // N_CHIPS
    half = shard // 2
    row_block = 32

    def body(dq_ref, dk_ref, dv_ref, dr_ref, x_ref, g_ref, *rest):
        if reduce:
            sends, own, acc, ras, narrow, s_sem, r_sem = rest
            x, y, c, chip, peers, peer_chip = _place()
            sib = (x, y, 1 - c)

            def part(k, cc, r0):
                return acc.at[pl.ds(pl.multiple_of(k * shard + cc * half + r0, 8), row_block), :]

            def swap_win(k):
                return _remote(narrow.at[k], ras.at[k], s_sem.at[k], r_sem.at[k], sib)
        else:
            acc = rest[0]

        @pl.when(pl.program_id(0) == 0)
        def _():
            acc[...] = jnp.zeros_like(acc)

        _, xh = _rms(x_ref[...])
        h = (xh * g_ref[...]).astype(BF16)
        acc[0:W, :] += _tn(dq_ref[...], h)
        acc[W:2 * W, :] += _tn(dk_ref[...], h)
        acc[2 * W:3 * W, :] += _tn(dv_ref[...], h)
        acc[QKV_COLS:IN_COLS, :] += _tn(dr_ref[...], h)

        if reduce:
            @pl.when(pl.program_id(0) == nt - 1)
            def _():
                for k in range(N_CHIPS):
                    def to_bf16(i, carry, k=k):
                        r0 = pl.multiple_of(i * row_block, row_block)
                        narrow[k, pl.ds(r0, row_block), :] = part(k, 1 - c, r0)[...].astype(BF16)
                        return carry
                    lax.fori_loop(0, half // row_block, to_bf16, 0)
                    swap_win(k).start()

                def chip_sum(k, r0):
                    return part(k, c, r0)[...] + ras[k, pl.ds(r0, row_block), :].astype(F32)

                for k in range(N_CHIPS):
                    swap_win(k).wait_recv()

                    @pl.when(chip == k)
                    def _(k=k):
                        def mine(i, carry):
                            r0 = pl.multiple_of(i * row_block, row_block)
                            own[pl.ds(r0, row_block), :] = chip_sum(k, r0)
                            return carry
                        lax.fori_loop(0, half // row_block, mine, 0)

                    @pl.when(chip != k)
                    def _(k=k):
                        def other(i, carry):
                            r0 = pl.multiple_of(i * row_block, row_block)
                            sends[(k ^ chip) - 1, pl.ds(r0, row_block), :] = chip_sum(k, r0).astype(BF16)
                            return carry
                        lax.fori_loop(0, half // row_block, other, 0)
                for k in range(N_CHIPS):
                    swap_win(k).wait_send()

    tile = lambda w: pl.BlockSpec((tm, w), lambda i: (i, 0))
    vmem = pl.BlockSpec(memory_space=pltpu.VMEM)
    in_specs = [tile(W), tile(W), tile(W), tile(REST_COLS), tile(D_MODEL), pl.BlockSpec((1, D_MODEL), lambda i: (0, 0))]
    if not reduce:
        return pl.pallas_call(
            body, grid=(nt,), in_specs=in_specs,
            out_specs=pl.BlockSpec((IN_COLS, D_MODEL), lambda i: (0, 0)),
            out_shape=jax.ShapeDtypeStruct((IN_COLS, D_MODEL), F32),
            compiler_params=_params(("arbitrary",)), name="inproj_bwd_dw")(dq, dk, dv, drest, x2d, g_norm)
    quarters = pltpu.VMEM((N_CHIPS, half, D_MODEL), BF16)
    return pl.pallas_call(
        body, grid=(nt,), in_specs=in_specs, out_specs=[vmem] * 2,
        out_shape=[jax.ShapeDtypeStruct((N_CHIPS - 1, half, D_MODEL), BF16),
                   jax.ShapeDtypeStruct((half, D_MODEL), F32)],
        scratch_shapes=[pltpu.VMEM((IN_COLS, D_MODEL), F32), quarters, quarters,
                        pltpu.SemaphoreType.DMA((N_CHIPS,)), pltpu.SemaphoreType.DMA((N_CHIPS,))],
        compiler_params=_params(("arbitrary",)), name="inproj_bwd_dw_reduce")(dq, dk, dv, drest, x2d, g_norm)


def _adamw_update(w, g, m, v):
    nm = ADAM_B1 * m + (1.0 - ADAM_B1) * g
    nv = ADAM_B2 * v + (1.0 - ADAM_B2) * (g * g)
    m_hat = nm / (1.0 - ADAM_B1 ** ADAM_STEP)
    v_hat = nv / (1.0 - ADAM_B2 ** ADAM_STEP)
    return -ADAM_LR * (m_hat / (jnp.sqrt(v_hat) + ADAM_EPS) + ADAM_WD * w), nm, nv


def _adamw_all(g_packed, ws, ms, vs, large):
    n, nl = len(ws), len(large)
    row_block = 8
    chunk_bytes = 512 * 1024

    def chunk_rows(w):
        R, C = w.shape
        return max(r for r in range(8, R + 1, 8) if R % r == 0 and r * C * 4 <= chunk_bytes)

    jobs = [(b, r0, chunk_rows(four[0])) for b, four in enumerate(large)
            for r0 in range(0, four[0].shape[0], chunk_rows(four[0]))]

    def body(*refs):
        g_ref = refs[0]
        w_refs, m_refs, v_refs = refs[1:1 + n], refs[1 + n:1 + 2 * n], refs[1 + 2 * n:1 + 3 * n]
        far_in = refs[1 + 3 * n:1 + 3 * n + 4 * nl]
        outs = refs[1 + 3 * n + 4 * nl:1 + 7 * n + 4 * nl]
        far_out = refs[1 + 7 * n + 4 * nl:1 + 7 * n + 8 * nl]
        loss_ref = refs[1 + 7 * n + 8 * nl]
        scr = refs[2 + 7 * n + 8 * nl:]
        in_scr, out_scr, in_sem, out_sem = scr[:4 * nl], scr[4 * nl:7 * nl], scr[7 * nl], scr[7 * nl + 1]

        def read(j, k):
            b, r0, rows = jobs[j]
            blk = pl.ds(r0, rows)
            return pltpu.make_async_copy(far_in[4 * b + k].at[blk, :], in_scr[4 * b + k].at[blk, :],
                                         in_sem.at[4 * j + k])

        def write(j, k):
            b, r0, rows = jobs[j]
            blk = pl.ds(r0, rows)
            src = in_scr[4 * b + 1] if k == 0 else out_scr[3 * b + k - 1]
            return pltpu.make_async_copy(src.at[blk, :], far_out[4 * b + k].at[blk, :], out_sem.at[4 * j + k])

        for j in range(len(jobs)):
            for k in range(4):
                read(j, k).start()
        off = 0
        for i, (_, used, padded) in enumerate(_SMALL_PARTS[:n]):
            g = g_ref[off:off + used, :]
            delta, nm, nv = _adamw_update(w_refs[i][...], g, m_refs[i][...], v_refs[i][...])
            outs[4 * i][...], outs[4 * i + 1][...], outs[4 * i + 2][...], outs[4 * i + 3][...] = g, delta, nm, nv
            off += padded
        loss_ref[...] = g_ref[_LOSS_ROW:_LOSS_ROW + 1, 0:1]
        for j, (b, r0, rows) in enumerate(jobs):
            for k in range(4):
                read(j, k).wait()

            def update(i, carry, b=b, r0=r0):
                blk = pl.ds(pl.multiple_of(r0 + i * row_block, row_block), row_block)
                w, g, m, v = [in_scr[4 * b + k][blk, :] for k in range(4)]
                for k, val in enumerate(_adamw_update(w, g, m, v)):
                    out_scr[3 * b + k][blk, :] = val
                return carry
            lax.fori_loop(0, rows // row_block, update, 0)
            for k in range(4):
                write(j, k).start()
        for j in range(len(jobs)):
            for k in range(4):
                write(j, k).wait()

    vmem = pl.BlockSpec(memory_space=pltpu.VMEM)
    far = pl.BlockSpec(memory_space=pl.ANY)
    flat = [a for four in large for a in four]
    outs = pl.pallas_call(
        body, in_specs=[vmem] * (1 + 3 * n) + [far] * (4 * nl),
        out_specs=[vmem] * (4 * n) + [far] * (4 * nl) + [vmem],
        out_shape=[jax.ShapeDtypeStruct(w.shape, F32) for w in ws for _ in range(4)]
        + [jax.ShapeDtypeStruct(four[0].shape, F32) for four in large for _ in range(4)]
        + [jax.ShapeDtypeStruct((1, 1), F32)],
        scratch_shapes=[pltpu.VMEM(a.shape, F32) for a in flat]
        + [pltpu.VMEM(four[0].shape, F32) for four in large for _ in range(3)]
        + [pltpu.SemaphoreType.DMA((4 * len(jobs),)), pltpu.SemaphoreType.DMA((4 * len(jobs),))],
        compiler_params=_params(), name="adamw_all")(g_packed, *ws, *ms, *vs, *flat)
    return ([outs[4 * i:4 * i + 4] for i in range(n)], [outs[4 * (n + i):4 * (n + i) + 4] for i in range(nl)],
            outs[4 * (n + nl)])


def _place():
    x, y, c = lax.axis_index("x"), lax.axis_index("y"), lax.axis_index("c")
    chip = 2 * x + y
    peers = [(x, 1 - y), (1 - x, y), (1 - x, 1 - y)]
    peer_chip = [2 * px + py for px, py in peers]
    return x, y, c, chip, peers, peer_chip


def _remote(src, dst, send_sem, recv_sem, dev):
    return pltpu.make_async_remote_copy(src_ref=src, dst_ref=dst, send_sem=send_sem, recv_sem=recv_sem,
                                        device_id=dev, device_id_type=MESH)


def _ag_weights(weights, late=()):
    nw, nl = len(weights), len(late)
    pieces = 2

    def body(*refs):
        srcs, late_srcs = refs[:nw], refs[nw:nw + nl]
        outs, late_bf, late_land = (refs[nw + nl:2 * nw + nl], refs[2 * nw + nl:2 * nw + 2 * nl],
                                    refs[2 * nw + 2 * nl:2 * nw + 3 * nl])
        scr = refs[2 * nw + 3 * nl:]
        wide, narrow = scr[:nw], scr[nw:2 * nw]
        in_sem, put_sem, s_ici, r_ici, s_d2d, r_d2d = scr[2 * nw:]
        x, y, c = lax.axis_index("x"), lax.axis_index("y"), lax.axis_index("c")
        chip = 2 * x + y
        sib = (x, y, 1 - c)
        first = ((x + 1 - c) % 2, (y + c) % 2)
        second = ((x + c) % 2, (y + 1 - c) % 2)
        first_chip, second_chip = 2 * first[0] + first[1], 2 * second[0] + second[1]
        diag_chip = 3 - chip

        parts = [(w, out, pc) for w, out in enumerate(outs) for pc in range(pieces)]

        def rows_of(out, cc, pc):
            rows = out.shape[1] // 2 // pieces
            return pl.ds(pl.multiple_of((cc * pieces + pc) * rows, 16), rows)

        def piece(out, k, cc, pc):
            return out.at[k, rows_of(out, cc, pc), :]

        def ici(w, slot, out, k, dev, pc, src=None):
            blk, sem = piece(out, k, c, pc), (nw * slot + w) * pieces + pc
            return _remote(blk if src is None else src, blk, s_ici.at[sem], r_ici.at[sem], (dev[0], dev[1], c))

        def d2d(w, slot, out, k, cc, pc):
            blk, sem = piece(out, k, cc, pc), (nw * slot + w) * pieces + pc
            return _remote(blk, blk, s_d2d.at[sem], r_d2d.at[sem], sib)

        def read(w, cc, pc):
            rows = rows_of(outs[w], cc, pc)
            return pltpu.make_async_copy(srcs[w].at[rows, :], wide[w].at[rows, :],
                                         in_sem.at[(w * 2 + cc) * pieces + pc])

        order = [(w, cc, pc) for cc in (0, 1) for w in range(nw) for pc in range(pieces)]
        for w, cc, pc in order:
            read(w, (c + cc) % 2, pc).start()
        sent = []
        for w, cc, pc in order[:nw * pieces]:
            rows = rows_of(outs[w], c, pc)
            read(w, c, pc).wait()
            narrow[w][rows, :] = wide[w][rows, :].astype(BF16)
            for slot, dev in enumerate((first, second)):
                sent.append(ici(w, slot, outs[w], chip, dev, pc, src=narrow[w].at[rows, :]))
                sent[-1].start()
        for src, bf, land in zip(late_srcs, late_bf, late_land):
            bf[...] = src[...].astype(BF16)
            land[...] = jnp.zeros_like(land)
            land[chip] = bf[...]
        for w, cc, pc in order[nw * pieces:]:
            rows = rows_of(outs[w], 1 - c, pc)
            read(w, 1 - c, pc).wait()
            narrow[w][rows, :] = wide[w][rows, :].astype(BF16)
        puts = [pltpu.make_async_copy(narrow[w], outs[w].at[chip], put_sem.at[w]) for w in range(nw)]
        for cp in puts:
            cp.start()
        for slot, k, dev in ((0, first_chip, first), (1, second_chip, second), (2, diag_chip, second)):
            for w, out, pc in parts:
                ici(w, slot, out, k, dev, pc).wait_recv()
                if slot == 0:
                    sent.append(ici(w, 2, out, k, second, pc))
                    sent[-1].start()
                sent.append(d2d(w, slot, out, k, c, pc))
                sent[-1].start()
        for slot, k in ((0, second_chip), (1, first_chip), (2, diag_chip)):
            for w, out, pc in parts:
                d2d(w, slot, out, k, 1 - c, pc).wait_recv()
        for cp in sent:
            cp.wait_send()
        for cp in puts:
            cp.wait()

    vmem = pl.BlockSpec(memory_space=pltpu.VMEM)
    far = pl.BlockSpec(memory_space=pl.ANY)
    outs = pl.pallas_call(
        body,
        out_shape=[jax.ShapeDtypeStruct((N_CHIPS,) + w.shape, BF16) for w in weights]
        + [jax.ShapeDtypeStruct(w.shape, BF16) for w in late]
        + [jax.ShapeDtypeStruct((N_CHIPS,) + w.shape, BF16) for w in late],
        in_specs=[far] * nw + [vmem] * nl, out_specs=[far] * nw + [vmem] * (2 * nl),
        scratch_shapes=[pltpu.VMEM(w.shape, F32) for w in weights] + [pltpu.VMEM(w.shape, BF16) for w in weights]
        + [pltpu.SemaphoreType.DMA((2 * nw * pieces,)), pltpu.SemaphoreType.DMA((nw,))]
        + [pltpu.SemaphoreType.DMA((3 * nw * pieces,))] * 4,
        compiler_params=pltpu.CompilerParams(vmem_limit_bytes=VMEM_LIMIT), name="ag_weights")(*weights, *late)
    return outs[:nw], outs[nw:nw + nl], outs[nw + nl:]


_HBM = pl.BlockSpec(memory_space=pltpu.HBM)
_SEM = pl.BlockSpec(memory_space=pltpu.SEMAPHORE)
_ANY = pl.BlockSpec(memory_space=pl.ANY)
_DATAFLOW = pltpu.SideEffectType.DATAFLOW_SIDE_EFFECTING


def _in_hbm(a):
    return pltpu.with_memory_space_constraint(a, pltpu.HBM)


_PEERS_OF = {"gather": 3, "scatter": 3, "direct": 7}


def _exchange_copies(mode, srcs, lands, send_sems, recv_sems):
    nw = len(srcs)
    x, y, c, chip, peers, peer_chip = _place()
    pairs = []
    if mode == "direct":
        targets = [((x, y), chip, 1)] + [(p, k, d) for p, k in zip(peers, peer_chip) for d in (0, 1)]
        for r, ((px, py), k, d) in enumerate(targets):
            for w in range(nw):
                sems = (send_sems.at[nw * r + w], recv_sems.at[nw * r + w], (px, py, (c + d) % 2))
                share = srcs[w].at[k if srcs[w].shape[0] > 1 else 0]
                pairs.append((_remote(share, lands[w].at[r], *sems),) * 2)
        return pairs
    gather = mode == "gather"
    for m, (px, py) in enumerate(peers):
        for w in range(nw):
            sems = (send_sems.at[nw * m + w], recv_sems.at[nw * m + w], (px, py, c))
            if gather:
                pairs.append((_remote(srcs[w], lands[w].at[chip], *sems),
                              _remote(srcs[w], lands[w].at[peer_chip[m]], *sems)))
            else:
                pairs.append((_remote(srcs[w].at[m], lands[w].at[m], *sems),) * 2)
    return pairs


def _exchange_start(mode, srcs, after, name, lands=None):
    nw = len(srcs)
    n_copies = _PEERS_OF[mode] * nw

    after = tuple(after)

    def body(*refs):
        send_sems, recv_sems = refs[2 * nw + len(after)], refs[2 * nw + len(after) + 1]
        for start, _ in _exchange_copies(mode, refs[:nw], refs[nw:2 * nw], send_sems, recv_sems):
            start.start()
        refs[-1][...] = jnp.zeros_like(refs[-1])

    if lands is None:
        shape = {"gather": lambda s: (N_CHIPS,) + s.shape, "scatter": lambda s: s.shape,
                 "direct": lambda s: (_PEERS_OF["direct"],) + s.shape[1:]}[mode]
        lands = [lax.empty(shape(s), s.dtype) for s in srcs]
    lands = [_in_hbm(l) for l in lands]
    return pl.pallas_call(
        body, name=name,
        out_shape=(pltpu.SemaphoreType.DMA((n_copies,)), pltpu.SemaphoreType.DMA((n_copies,)))
        + tuple(pltpu.HBM(s.shape, s.dtype) for s in srcs)
        + tuple(pltpu.HBM(l.shape, l.dtype) for l in lands)
        + (jax.ShapeDtypeStruct((8, LANES), F32),),
        in_specs=[_HBM] * (2 * nw) + [_ANY] * len(after),
        out_specs=(_SEM, _SEM) + (_HBM,) * (2 * nw) + (pl.BlockSpec(memory_space=pltpu.VMEM),),
        input_output_aliases={i: 2 + i for i in range(2 * nw)},
        compiler_params=pltpu.CompilerParams(has_side_effects=_DATAFLOW),
    )(*[_in_hbm(s) for s in srcs], *lands, *after)


def _exchange_wait(mode, started, after, name):
    nw = (len(started) - 3) // 2
    send_sems, recv_sems = started[0], started[1]
    thru = started[2:2 + 2 * nw]

    def body(*refs):
        for _, arrival in _exchange_copies(mode, refs[:nw], refs[nw:2 * nw], refs[2 * nw], refs[2 * nw + 1]):
            arrival.wait_send()
            arrival.wait_recv()

    outs = pl.pallas_call(
        body, name=name,
        out_shape=tuple(pltpu.HBM(t.shape, t.dtype) for t in thru),
        in_specs=[_HBM] * (2 * nw) + [_SEM, _SEM, _ANY], out_specs=(_HBM,) * (2 * nw),
        input_output_aliases={i: i for i in range(2 * nw)},
        compiler_params=pltpu.CompilerParams(has_side_effects=_DATAFLOW),
    )(*thru, send_sems, recv_sems, after)
    return outs[:nw], outs[nw:]


def _reduce_last(owns, landed, g_small, direct_srcs, direct_landed, spread_row0):
    ns, nd = len(owns), len(direct_srcs)
    halves = [o.shape[0] for o in owns]
    row_block = 16
    spread_rows = direct_srcs[-1].shape[1]
    rest0, rest1 = spread_row0, SMALL_ROWS - spread_row0 - spread_rows
    hs = (rest0 + rest1) // 2
    jobs = [(w, p * (halves[w] // 2), halves[w] // 2) for w in range(ns) for p in range(2)]
    n_swaps = len(jobs)
    jobs += [(ns + d, 0, direct_srcs[d].shape[1]) for d in range(nd)]

    def body(*refs):
        own_refs, land_refs, gsm_ref = refs[:ns], refs[ns:2 * ns], refs[2 * ns]
        dsrc_refs, dland_refs = refs[2 * ns + 1:2 * ns + 1 + nd], refs[2 * ns + 1 + nd:2 * ns + 1 + 2 * nd]
        n_in = 2 * ns + 1 + 2 * nd
        out_refs, osm_ref = refs[n_in:n_in + ns + nd - 1], refs[n_in + ns + nd - 1]
        scr = refs[n_in + ns + nd:]
        own_scr, land_scr, dsrc_scr, dland_scr = (scr[:ns], scr[ns:2 * ns], scr[2 * ns:2 * ns + nd],
                                                  scr[2 * ns + nd:2 * ns + 2 * nd])
        o_rest, ra_sm, p_sm, in_sem, s_sem, r_sem, sm_s, sm_r = scr[2 * ns + 2 * nd:]
        x, y, c, chip, peers, peer_chip = _place()
        sib = (x, y, 1 - c)
        half = lambda cc: pl.ds(pl.multiple_of(cc * hs, 8), hs)
        sm_a = _remote(gsm_ref.at[half(1 - c), :], ra_sm, sm_s.at[0], sm_r.at[0], sib)
        sm_a.start()
        swaps = [sm_a]

        def reads(j):
            w, r0, n = jobs[j]
            rows = pl.ds(r0, n)
            if w < ns:
                pairs = [(own_refs[w].at[rows, :], own_scr[w].at[rows, :]),
                         (land_refs[w].at[:, rows, :], land_scr[w].at[:, rows, :])]
            else:
                share = dsrc_refs[w - ns].at[chip if w < ns + nd - 1 else 0]
                pairs = [(share, dsrc_scr[w - ns]), (dland_refs[w - ns], dland_scr[w - ns])]
            return [pltpu.make_async_copy(s, d, in_sem.at[2 * j + i]) for i, (s, d) in enumerate(pairs)]

        for j in range(len(jobs)):
            for cp in reads(j):
                cp.start()
        sm_a.wait_recv()
        p_sm[chip] = gsm_ref[half(c), :] + ra_sm[...]
        for m, (px, py) in enumerate(peers):
            swaps.append(_remote(p_sm.at[chip], p_sm.at[chip], sm_s.at[1 + m], sm_r.at[1 + m], (px, py, c)))
            swaps[-1].start()

        def mine_of(j):
            w, r0, n = jobs[j]
            return out_refs[w].at[pl.ds(pl.multiple_of(c * halves[w] + r0, 8), n), :]

        for j, (w, r0, n) in enumerate(jobs):
            for cp in reads(j):
                cp.wait()

            def total(i, carry, w=w, r0=r0):
                rr = pl.multiple_of(r0 + i * row_block, row_block)
                blk = pl.ds(rr, row_block)
                if w < ns:
                    acc = own_scr[w][blk, :]
                    for m in range(_PEERS_OF["scatter"]):
                        acc = acc + land_scr[w][m, blk, :].astype(F32)
                    out_refs[w][pl.ds(pl.multiple_of(c * halves[w] + rr, row_block), row_block), :] = acc
                else:
                    theirs = lambda r: dland_scr[w - ns][r, blk, :].astype(F32)
                    acc = ((dsrc_scr[w - ns][blk, :].astype(F32) + theirs(0)) + (theirs(1) + theirs(2))) + (
                        (theirs(3) + theirs(4)) + (theirs(5) + theirs(6)))
                    if w < ns + nd - 1:
                        out_refs[w][blk, :] = acc
                    else:
                        osm_ref[pl.ds(pl.multiple_of(spread_row0 + rr, 8), row_block), :] = acc
                return carry
            lax.fori_loop(0, n // row_block, total, 0)
            if w < ns:
                swaps.append(_remote(mine_of(j), mine_of(j), s_sem.at[j], r_sem.at[j], sib))
                swaps[-1].start()
        for m, (px, py) in enumerate(peers):
            _remote(p_sm.at[chip], p_sm.at[peer_chip[m]], sm_s.at[1 + m], sm_r.at[1 + m], (px, py, c)).wait_recv()
        o_rest[half(c), :] = (p_sm[0] + p_sm[1]) + (p_sm[2] + p_sm[3])
        swaps.append(_remote(o_rest.at[half(c), :], o_rest.at[half(c), :], sm_s.at[4], sm_r.at[4], sib))
        swaps[-1].start()
        for j, (w, r0, n) in enumerate(jobs[:n_swaps]):
            theirs = out_refs[w].at[pl.ds(pl.multiple_of((1 - c) * halves[w] + r0, 8), n), :]
            _remote(theirs, theirs, s_sem.at[j], r_sem.at[j], sib).wait_recv()
        _remote(o_rest.at[half(1 - c), :], o_rest.at[half(1 - c), :], sm_s.at[4], sm_r.at[4], sib).wait_recv()
        osm_ref[0:rest0, :] = o_rest[0:rest0, :]
        osm_ref[SMALL_ROWS - rest1:SMALL_ROWS, :] = o_rest[rest0:rest0 + rest1, :]
        for cp in swaps:
            cp.wait_send()

    vmem = pl.BlockSpec(memory_space=pltpu.VMEM)
    far = [pl.BlockSpec(memory_space=pl.ANY)]
    return pl.pallas_call(
        body, out_shape=[jax.ShapeDtypeStruct((2 * o.shape[0], o.shape[1]), F32) for o in owns]
        + [jax.ShapeDtypeStruct(s.shape[1:], F32) for s in direct_srcs[:-1]]
        + [jax.ShapeDtypeStruct((SMALL_ROWS, LANES), F32)],
        in_specs=far * (2 * ns) + [vmem] + far * (2 * nd), out_specs=[vmem] * (ns + nd),
        scratch_shapes=[pltpu.VMEM(o.shape, o.dtype) for o in owns] + [pltpu.VMEM(l.shape, l.dtype) for l in landed]
        + [pltpu.VMEM(s.shape[1:], s.dtype) for s in direct_srcs]
        + [pltpu.VMEM(l.shape, l.dtype) for l in direct_landed]
        + [pltpu.VMEM((2 * hs, LANES), F32), pltpu.VMEM((hs, LANES), F32), pltpu.VMEM((N_CHIPS, hs, LANES), F32),
           pltpu.SemaphoreType.DMA((2 * len(jobs),)),
           pltpu.SemaphoreType.DMA((n_swaps,)), pltpu.SemaphoreType.DMA((n_swaps,)),
           pltpu.SemaphoreType.DMA((5,)), pltpu.SemaphoreType.DMA((5,))],
        compiler_params=pltpu.CompilerParams(vmem_limit_bytes=VMEM_LIMIT),
        name="reduce_last")(*owns, *landed, g_small, *direct_srcs, *direct_landed)


_SMALL_PARTS = (("g_norm", 8, 8), ("w_s", 512, 512), ("b_s", 4, 8), ("g_v", 2, 8), ("g_mem", 8, 8),
                ("g_final", 8, 8), ("loss", 8, 8))
_LOSS_ROW = SMALL_ROWS - 8
_W_S_ROW = 8
assert sum(p for _, _, p in _SMALL_PARTS) == SMALL_ROWS and _SMALL_PARTS[1][0] == "w_s"


def _pack_small(parts, loss_block):
    rows = []
    parts = dict(parts, loss=loss_block)
    for name, used, padded in _SMALL_PARTS:
        if name == "w_s":
            continue
        p = parts[name].reshape(used, LANES)
        if padded > used:
            p = jnp.pad(p, ((0, padded - used), (0, 0)))
        rows.append(p)
    return jnp.concatenate(rows, axis=0)


def _local_step(x, mem, target, g_norm, w_in, w_s, b_s, g_v, g_mem, late_weights, g_final,
                fwd_token=None, on_late=None, on_dw=None):
    B, S, _ = x.shape
    x2d = x.reshape(B * S, D_MODEL)
    t2d = target.reshape(B * S, D_MODEL)
    mem2d = mem.reshape(B * N_MEM, D_MODEL)

    proj = _inproj_fwd(x2d, g_norm, w_in, after=() if fwd_token is None else (fwd_token,))
    w_kv, w_out = late_weights(proj)
    kv = _kv_fwd(mem2d, g_mem, w_kv)
    a, lse = _attn_fwd(proj, B, S)
    w_sT = jnp.swapaxes(w_s, 1, 2)
    b_tab = jnp.repeat(b_s.T, HEAD_DIM, axis=1)
    (dx2, da, drest, loss, d_wout, d_ws, d_bs, d_gv, d_gf, dkv) = _mid(
        x2d, t2d, a, proj, kv, w_s, w_sT, b_tab, g_v, w_out, g_final, B, S)
    d_wkv, d_gmem = _kv_bwd(mem2d, g_mem, w_kv, dkv)
    dq, dk, dv = _attn_bwd(proj, a, lse, da, B, S, after=() if on_late is None else (on_late(d_wkv, d_wout, d_ws),))
    if on_dw is None:
        d_win = _inproj_bwd_dw(dq, dk, dv, drest, x2d, g_norm)
        after = ()
    else:
        d_win = None
        after = (on_dw(*_inproj_bwd_dw(dq, dk, dv, drest, x2d, g_norm, reduce=True)),)
    grad_x, d_gnorm = _inproj_bwd_dx(dq, dk, dv, drest, x2d, dx2, g_norm, w_in, after=after)
    d_bs = d_bs[:, :N_SGU_GROUPS].T
    return (loss, grad_x.reshape(B, S, D_MODEL),
            dict(g_norm=d_gnorm, w_in=d_win, w_s=d_ws, b_s=d_bs, g_v=d_gv, g_mem=d_gmem, w_kv=d_wkv,
                 w_out=d_wout, g_final=d_gf))


def kernel(x, mem, g_norm, w_in, w_sgu_spatial, b_sgu_spatial, g_sgu_v, g_mem, w_mem_kv, w_out, g_final, loss_target, m_g_norm, m_w_in, m_w_sgu_spatial, m_b_sgu_spatial, m_g_sgu_v, m_g_mem, m_w_mem_kv, m_w_out, m_g_final, v_g_norm, v_w_in, v_w_sgu_spatial, v_b_sgu_spatial, v_g_sgu_v, v_g_mem, v_w_mem_kv, v_w_out, v_g_final):
    t = lambda w: jnp.swapaxes(w[0], 0, 1)
    (win_all,), late_shards, late_lands = _ag_weights([t(w_in)], [w_mem_kv[0], w_out[0]])
    w_in_full = win_all.reshape(-1, win_all.shape[-1])
    late = _exchange_start("gather", list(late_shards), (win_all,), "gather_late_start", lands=late_lands)

    def late_weights(proj):
        return [z.reshape(-1, z.shape[-1]) for z in _exchange_wait("gather", late, proj, "gather_late_wait")[1]]

    scatter = {}

    def on_late(d_wkv, d_wout, d_ws):
        d_ws = d_ws.reshape(1, -1, LANES)
        scatter["late"] = _exchange_start("direct", [d_wkv, d_wout, d_ws], (), "scatter_late_start")
        return scatter["late"][-1]

    def on_dw(sends, own):
        scatter["own"] = own
        scatter["started"] = _exchange_start("scatter", [sends], (own,), "scatter_start")
        return scatter["started"][-1]

    loss, grad_x, g = _local_step(
        x, mem, loss_target, g_norm, w_in_full, w_sgu_spatial[0], b_sgu_spatial[0], g_sgu_v, g_mem,
        late_weights, g_final.reshape(1, D_MODEL), fwd_token=late[-1], on_late=on_late, on_dw=on_dw)

    small_names = ("g_norm", "w_s", "b_s", "g_v", "g_mem", "g_final")
    g_small = _pack_small({n: g[n] for n in small_names if n != "w_s"}, loss)
    late_srcs, late_landed = _exchange_wait("direct", scatter["late"], g_small, "scatter_late_wait")
    _, landed = _exchange_wait("scatter", scatter["started"], late_landed[0], "scatter_wait")
    gr_in, gr_kv, gr_out, gr_small = _reduce_last([scatter["own"]], landed, g_small, late_srcs, late_landed, _W_S_ROW)

    small_w = (g_norm, w_sgu_spatial, b_sgu_spatial, g_sgu_v, g_mem, g_final)
    small_m = (m_g_norm, m_w_sgu_spatial, m_b_sgu_spatial, m_g_sgu_v, m_g_mem, m_g_final)
    small_v = (v_g_norm, v_w_sgu_spatial, v_b_sgu_spatial, v_g_sgu_v, v_g_mem, v_g_final)
    rows = lambda ws: [w.reshape(-1, LANES) for w in ws]
    small_new, (of_in, (gr_kv, d_kv, nm_kv, nv_kv), (gr_out, d_out, nm_out, nv_out)), loss = _adamw_all(
        gr_small, rows(small_w), rows(small_m), rows(small_v),
        [(t(w_in), gr_in, t(m_w_in), t(v_w_in)), (w_mem_kv[0], gr_kv, m_w_mem_kv[0], v_w_mem_kv[0]),
         (w_out[0], gr_out, m_w_out[0], v_w_out[0])])
    loss = loss.reshape(())
    small = [[z.reshape(w.shape) for z in four] for w, four in zip(small_w, small_new)]
    gr_in, d_in, nm_in, nv_in = [jnp.swapaxes(z, 0, 1) for z in of_in]

    def leaves(kind, big_in, big_kv, big_out):
        s_norm, s_ws, s_bs, s_gv, s_gmem, s_gf = [four[kind] for four in small]
        return [s_norm, big_in[None], s_ws, s_bs, s_gv, s_gmem, big_kv[None], big_out[None], s_gf]

    return (loss, grad_x, *leaves(0, gr_in, gr_kv, gr_out), *leaves(1, d_in, d_kv, d_out),
            *leaves(2, nm_in, nm_kv, nm_out), *leaves(3, nv_in, nv_kv, nv_out))
```

```python
import functools

import jax
import jax.numpy as jnp
from jax import lax
from jax.experimental import pallas as pl
from jax.experimental.pallas import tpu as pltpu

F32 = jnp.float32
BF16 = jnp.bfloat16
MESH = pl.DeviceIdType.MESH

D_MODEL = 1024
ATTN_WIDTH = 512
SGU_WIDTH = 256
MEM_WIDTH = 256
N_MEM = 256
IN_COLS = 3328
QKV_COLS = 3 * ATTN_WIDTH
REST_COLS = IN_COLS - QKV_COLS
SGU_CHUNK = 128
N_SGU_GROUPS = 4
EPS = 1e-6
NEG_INF = -1e30
DILATIONS = (1, 4, 16)
RADIUS = 64
Q_BLOCK = 128
LANES = 128
HEAD_DIM = 64

ADAM_LR = 0.001
ADAM_B1 = 0.9
ADAM_B2 = 0.999
ADAM_EPS = 1e-08
ADAM_WD = 0.01
ADAM_STEP = 10

N_CHIPS = 4
VMEM_LIMIT = 56 * 1024 * 1024
SMALL_ROWS = 560


def _params(sem=None, vmem=VMEM_LIMIT):
    return pltpu.CompilerParams(dimension_semantics=sem, vmem_limit_bytes=vmem)


def _nn(a, b):
    return jnp.dot(a, b, preferred_element_type=F32)


def _nt(a, b):
    return lax.dot_general(a, b, (((1,), (1,)), ((), ())), preferred_element_type=F32)


def _tn(a, b):
    return lax.dot_general(a, b, (((0,), (0,)), ((), ())), preferred_element_type=F32)


def _rms(x):
    r = lax.rsqrt(jnp.mean(x * x, axis=-1, keepdims=True) + EPS)
    return r, x * r


def _head_masks():
    lane = lax.broadcasted_iota(jnp.int32, (1, LANES), 1)
    lo = lane < HEAD_DIM
    return lo, (lo.astype(F32), (~lo).astype(F32))


def _silu_parts(z):
    s = jax.nn.sigmoid(z)
    return z * s, s * (1.0 + z * (1.0 - s))


def _gelu_parts(x):
    c = 0.7978845608028654
    x2 = x * x
    s = jax.nn.sigmoid((2.0 * c) * (x + 0.044715 * (x * x2)))
    return x * s, s * (1.0 + x * (1.0 - s) * ((2.0 * c) * (1.0 + 3.0 * 0.044715 * x2)))


def _after(tokens):
    return [pl.BlockSpec(memory_space=pl.ANY)] * len(tokens)


def _inproj_fwd(x2d, g_norm, w_in_t, after=()):
    T = x2d.shape[0]
    tm = 512

    def body(x_ref, g_ref, w_ref, *rest):
        o_ref = rest[-1]
        _, xh = _rms(x_ref[...])
        h = (xh * g_ref[...]).astype(BF16)
        o_ref[...] = _nt(h, w_ref[...])

    return pl.pallas_call(
        body, grid=(T // tm,),
        in_specs=[pl.BlockSpec((tm, D_MODEL), lambda i: (i, 0)),
                  pl.BlockSpec((1, D_MODEL), lambda i: (0, 0)),
                  pl.BlockSpec((IN_COLS, D_MODEL), lambda i: (0, 0))] + _after(after),
        out_specs=pl.BlockSpec((tm, IN_COLS), lambda i: (i, 0)),
        out_shape=jax.ShapeDtypeStruct((T, IN_COLS), F32),
        compiler_params=_params(("arbitrary",)), name="inproj_fwd")(x2d, g_norm, w_in_t, *after)


def _kv_fwd(mem2d, g_mem, w_kv):
    Tm = mem2d.shape[0]

    def body(m_ref, g_ref, w_ref, o_ref):
        _, mh = _rms(m_ref[...])
        o_ref[...] = _nn((mh * g_ref[...]).astype(BF16), w_ref[...])

    return pl.pallas_call(
        body, out_shape=jax.ShapeDtypeStruct((Tm, 2 * MEM_WIDTH), F32),
        compiler_params=_params(), name="kv_fwd")(mem2d, g_mem, w_kv)


def _kv_bwd(mem2d, g_mem, w_kv, dkv):
    Tm = mem2d.shape[0]

    def body(m_ref, g_ref, w_ref, dkv_ref, dw_ref, dg_ref):
        _, mh = _rms(m_ref[...])
        memn = (mh * g_ref[...]).astype(BF16)
        dkvb = dkv_ref[...].astype(BF16)
        dw = _tn(memn, dkvb).astype(BF16)
        for k in range(N_CHIPS):
            dw_ref[k] = dw[k * (D_MODEL // N_CHIPS):(k + 1) * (D_MODEL // N_CHIPS), :]
        dmemn = _nt(dkvb, w_ref[...])
        dg_ref[...] = jnp.sum(dmemn * mh, axis=0, keepdims=True)

    return pl.pallas_call(
        body, out_shape=(jax.ShapeDtypeStruct((N_CHIPS, D_MODEL // N_CHIPS, 2 * MEM_WIDTH), BF16),
                         jax.ShapeDtypeStruct((1, D_MODEL), F32)),
        compiler_params=_params(), name="kv_bwd")(mem2d, g_mem, w_kv, dkv)


def _attn_geometry(S):
    geom = []
    for d in DILATIONS:
        L = S // d
        assert L % Q_BLOCK == 0
        geom.append((d, L, min(2 * Q_BLOCK, L), L // Q_BLOCK))
    return geom


def _init_bias(bias_scr, geom, hp):
    row = lax.broadcasted_iota(jnp.int32, (Q_BLOCK, 2 * Q_BLOCK), 0)
    col = lax.broadcasted_iota(jnp.int32, (Q_BLOCK, 2 * Q_BLOCK), 1)
    for j in (0, 1):
        bits = (126 - (2 * hp + j)) * (1 << 23)
        slope = lax.bitcast_convert_type(jnp.full((1, 1), bits, jnp.int32), F32)
        for di, (d, _, _, _) in enumerate(geom):
            for cls, off in enumerate((0, -RADIUS, -2 * RADIUS)):
                dist = jnp.abs(col - row + off)
                bias_scr[di * 6 + cls * 2 + j] = jnp.where(
                    dist <= RADIUS, -(slope * float(d)) * dist.astype(F32), NEG_INF)


SPLIT = 4
COPY_ROWS = 256


def _by4_rows(S, step):
    per_class = S // SPLIT // COPY_ROWS
    r, j = step // per_class, step % per_class
    return (pl.ds(r + SPLIT * j * COPY_ROWS, COPY_ROWS, stride=SPLIT),
            pl.ds(r * (S // SPLIT) + j * COPY_ROWS, COPY_ROWS))


def _to_by4(src, dst, S):
    for i in range(S // COPY_ROWS):
        natural, by4 = _by4_rows(S, i)
        dst[by4, :] = src[natural, :]


def _block_slices(d, L, KW, nqb, r, qb, S):
    qs = qb * Q_BLOCK
    ks = jnp.clip(qs - RADIUS, 0, L - KW)
    cls = jnp.where(qb == 0, 0, jnp.where(qb == nqb - 1, 2, 1))
    if d == 1:
        qsl = pl.ds(pl.multiple_of(qs, Q_BLOCK), Q_BLOCK)
        ksl = pl.ds(pl.multiple_of(ks, RADIUS), KW)
    elif d == SPLIT:
        qsl = pl.ds(pl.multiple_of(r * L + qs, Q_BLOCK), Q_BLOCK)
        ksl = pl.ds(pl.multiple_of(r * L + ks, RADIUS), KW)
    else:
        sub = d // SPLIT
        base = (r % SPLIT) * (S // SPLIT) + r // SPLIT
        qsl = pl.ds(base + qs * sub, Q_BLOCK, stride=sub)
        ksl = pl.ds(base + ks * sub, KW, stride=sub)
    return qsl, ksl, cls


def _for_groups(geom, S, group, fn):
    for di, (d, L, KW, nqb) in enumerate(geom):
        n = group[di]
        assert (d * nqb) % n == 0

        def step(it, carry, di=di, d=d, L=L, KW=KW, nqb=nqb, n=n):
            slices = []
            for g in range(n):
                i = it * n + g
                slices.append(_block_slices(d, L, KW, nqb, i // nqb, i % nqb, S))
            fn(di, KW, slices)
            return carry
        lax.fori_loop(0, d * nqb // n, step, 0)


def _attn_fwd(proj, B, S):
    T = B * S
    geom = _attn_geometry(S)
    n_pairs = ATTN_WIDTH // LANES

    def body(q_ref, k_ref, v_ref, a_ref, lse_ref, bias_scr, q4, k4, v4, *per_dilation):
        o_scr, m_scr, l_scr = per_dilation[0:3], per_dilation[3:6], per_dilation[6:9]
        lo, hm = _head_masks()
        pair = pl.program_id(0)

        @pl.when(pl.program_id(1) == 0)
        def _():
            _init_bias(bias_scr, geom, pair)
        for src, dst in ((q_ref, q4), (k_ref, k4), (v_ref, v4)):
            _to_by4(src, dst, S)

        def group(di, KW, all_slices):
            run = 8
            for first in range(0, len(all_slices), run):
                some(di, KW, all_slices[first:first + run])

        def some(di, KW, slices):
            chains = [(g, j) for g in range(len(slices)) for j in (0, 1)]
            q_src, k_src, v_src = (q_ref, k_ref, v_ref) if di == 0 else (q4, k4, v4)
            q = [q_src[qsl, :] for qsl, _, _ in slices]
            kw = [k_src[ksl, :].astype(BF16) for _, ksl, _ in slices]
            vw = [v_src[ksl, :].astype(BF16) for _, ksl, _ in slices]
            s = {(g, j): _nt((q[g] * (hm[j] * 0.125)).astype(BF16), kw[g])
                 + bias_scr[di * 6 + slices[g][2] * 2 + j, :, pl.ds(0, KW)] for g, j in chains}
            m = {c: jnp.max(s[c], axis=1, keepdims=True) for c in chains}
            p = {c: jnp.exp(s[c] - m[c]) for c in chains}
            l = {c: jnp.sum(p[c], axis=1, keepdims=True) for c in chains}
            o = {(g, j): _nn(p[(g, j)].astype(BF16), vw[g]) for g, j in chains}
            for g, (qsl, _, _) in enumerate(slices):
                o_scr[di][qsl, :] = jnp.where(lo, o[(g, 0)], o[(g, 1)])
                m_scr[di][qsl, :] = jnp.where(lo, m[(g, 0)], m[(g, 1)])
                l_scr[di][qsl, :] = jnp.where(lo, l[(g, 0)], l[(g, 1)])

        _for_groups(geom, S, (16, 16, 16), group)

        for i in range(S // COPY_ROWS):
            natural, by4 = _by4_rows(S, i)
            rows = [natural, by4, by4]
            ms = [m_scr[di][rows[di], :] for di in range(3)]
            mx = jnp.maximum(jnp.maximum(ms[0], ms[1]), ms[2])
            num = 0.0
            den = 0.0
            for di in range(3):
                w = jnp.exp(ms[di] - mx)
                num = num + w * o_scr[di][rows[di], :]
                den = den + w * l_scr[di][rows[di], :]
            a_ref[natural, :] = num / den
            lse_ref[natural, :] = mx + jnp.log(den)

    blk = lambda off: pl.BlockSpec((S, LANES), lambda h, b, off=off: (b, off + h))
    out_blk = pl.BlockSpec((S, LANES), lambda h, b: (b, h))
    return pl.pallas_call(
        body, grid=(n_pairs, B),
        in_specs=[blk(0), blk(n_pairs), blk(2 * n_pairs)],
        out_specs=[out_blk, out_blk],
        out_shape=[jax.ShapeDtypeStruct((T, ATTN_WIDTH), F32)] * 2,
        scratch_shapes=[pltpu.VMEM((18, Q_BLOCK, 2 * Q_BLOCK), F32)] + [pltpu.VMEM((S, LANES), F32)] * 12,
        compiler_params=_params(("arbitrary", "arbitrary")), name="attn_fwd")(proj, proj, proj)


def _attn_bwd(proj, a, lse, da, B, S, after=()):
    T = B * S
    geom = _attn_geometry(S)
    n_pairs = ATTN_WIDTH // LANES

    def body(q_ref, k_ref, v_ref, a_ref, lse_ref, do_ref, *rest):
        dq_ref, dk_ref, dv_ref, bias_scr = rest[len(after):len(after) + 4]
        scr = rest[len(after) + 4:]
        acc = (scr[0:3], scr[3:6])
        natural_in = (q_ref, k_ref, v_ref, a_ref, lse_ref, do_ref)
        by4_in = scr[6:12]
        _, hm = _head_masks()
        pair = pl.program_id(0)

        @pl.when(pl.program_id(1) == 0)
        def _():
            _init_bias(bias_scr, geom, pair)
        for ref in scr[0:6]:
            ref[...] = jnp.zeros_like(ref)
        for src, dst in zip(natural_in, by4_in):
            _to_by4(src, dst, S)

        def group(di, KW, all_slices):
            run = (4, 4, 8)[di]
            for first in range(0, len(all_slices), run):
                some(di, KW, all_slices[first:first + run])

        def some(di, KW, slices):
            n = len(slices)
            chains = [(g, j) for g in range(n) for j in (0, 1)]
            q_src, k_src, v_src, a_src, lse_src, do_src = natural_in if di == 0 else by4_in
            dq_scr, dk_scr, dv_scr = acc[0 if di == 0 else 1]
            q = [q_src[qsl, :] for qsl, _, _ in slices]
            do = [do_src[qsl, :] for qsl, _, _ in slices]
            doa = [do[g] * a_src[slices[g][0], :] for g in range(n)]
            lse_q = [lse_src[qsl, :] for qsl, _, _ in slices]
            kw = [k_src[ksl, :].astype(BF16) for _, ksl, _ in slices]
            vw = [v_src[ksl, :].astype(BF16) for _, ksl, _ in slices]
            qj = {(g, j): (q[g] * (hm[j] * 0.125)).astype(BF16) for g, j in chains}
            doj = {(g, j): (do[g] * hm[j]).astype(BF16) for g, j in chains}
            s = {(g, j): _nt(qj[(g, j)], kw[g])
                 + bias_scr[di * 6 + slices[g][2] * 2 + j, :, pl.ds(0, KW)] for g, j in chains}
            dp = {(g, j): _nt(doj[(g, j)], vw[g]) for g, j in chains}
            dsum = {(g, j): jnp.sum(doa[g] * hm[j], axis=1, keepdims=True) for g, j in chains}
            p = {(g, j): jnp.exp(s[(g, j)] - lse_q[g][:, HEAD_DIM * j:HEAD_DIM * j + 1]) for g, j in chains}
            ds = {c: (p[c] * (dp[c] - dsum[c])).astype(BF16) for c in chains}
            pb = {c: p[c].astype(BF16) for c in chains}
            dq = [_nn(ds[(g, 0)], kw[g]) * (hm[0] * 0.125) + _nn(ds[(g, 1)], kw[g]) * (hm[1] * 0.125)
                  for g in range(n)]
            both = lambda t, g: jnp.concatenate([t[(g, 0)], t[(g, 1)]], axis=0)
            dkw = [_tn(both(ds, g), both(qj, g)) for g in range(n)]
            dvw = [_tn(both(pb, g), both(doj, g)) for g in range(n)]
            for g, (qsl, ksl, _) in enumerate(slices):
                dq_scr[qsl, :] = dq_scr[qsl, :] + dq[g]
                dk_scr[ksl, :] = dk_scr[ksl, :] + dkw[g]
                dv_scr[ksl, :] = dv_scr[ksl, :] + dvw[g]

        _for_groups(geom, S, (16, 16, 16), group)

        for i in range(S // COPY_ROWS):
            natural, by4 = _by4_rows(S, i)
            for nat, split in zip(*acc):
                nat[natural, :] = nat[natural, :] + split[by4, :]
        for out, nat in zip((dq_ref, dk_ref, dv_ref), acc[0]):
            out[...] = nat[...].astype(BF16)

    blk = lambda off: pl.BlockSpec((S, LANES), lambda h, b, off=off: (b, off + h))
    return pl.pallas_call(
        body, grid=(n_pairs, B),
        in_specs=[blk(0), blk(n_pairs), blk(2 * n_pairs), blk(0), blk(0), blk(0)] + _after(after),
        out_specs=[blk(0), blk(0), blk(0)],
        out_shape=[jax.ShapeDtypeStruct((T, ATTN_WIDTH), BF16)] * 3,
        scratch_shapes=[pltpu.VMEM((18, Q_BLOCK, 2 * Q_BLOCK), F32)] + [pltpu.VMEM((S, LANES), F32)] * 12,
        compiler_params=_params(("arbitrary", "arbitrary")), name="attn_bwd")(proj, proj, proj, a, lse, da, *after)


def _mid(x2d, t2d, a, proj, kv, w_s, w_sT, b_tab, g_v, w_out, g_final, B, S):
    T = B * S
    tm = 512
    nt = S // tm
    halves = 2
    hrows = tm // halves

    def body(x_ref, t_ref, a_ref, za_ref, ub_ref, vb_ref, zb_ref, qm_ref, zm_ref, kv_ref,
              ws_ref, wsT_ref, btab_ref, gv_ref, wout_ref, gf_ref,
              dx2_ref, da_ref, drest_ref, loss_ref, dwout_bf_ref, dws_ref, dbs_ref, dgv_ref, dgf_ref, dkv_ref,
              dbtab_scr, dwout_ref):
        b = pl.program_id(0)
        t = pl.program_id(1)
        first = jnp.logical_and(b == 0, t == 0)
        last = jnp.logical_and(b == B - 1, t == nt - 1)
        _, hm = _head_masks()
        lane_g = lax.broadcasted_iota(jnp.int32, (1, SGU_WIDTH), 1) // HEAD_DIM
        gm = [(lane_g == g).astype(F32) for g in range(N_SGU_GROUPS)]
        H = range(halves)
        rows = [pl.ds(h * hrows, hrows) for h in H]
        ld = lambda ref: [ref[r, :] for r in rows]
        cat = lambda parts, axis: jnp.concatenate(parts, axis=axis)
        chunks = [slice(ci * SGU_CHUNK, (ci + 1) * SGU_CHUNK) for ci in range(hrows // SGU_CHUNK)]
        pairs = [slice(pr * LANES, (pr + 1) * LANES) for pr in range(2)]
        heads = [(pr, j) for pr in range(2) for j in (0, 1)]

        @pl.when(first)
        def _():
            loss_ref[...] = jnp.zeros_like(loss_ref)
            dwout_ref[...] = jnp.zeros_like(dwout_ref)
            dws_ref[...] = jnp.zeros_like(dws_ref)
            dbs_ref[...] = jnp.zeros_like(dbs_ref)
            dgv_ref[...] = jnp.zeros_like(dgv_ref)
            dgf_ref[...] = jnp.zeros_like(dgf_ref)
            dbtab_scr[...] = jnp.zeros_like(dbtab_scr)

        @pl.when(t == 0)
        def _():
            dkv_ref[...] = jnp.zeros_like(dkv_ref)

        a_val = ld(a_ref)
        sil_a = [_silu_parts(z) for z in ld(za_ref)]
        gated_a = [s[0] * a for s, a in zip(sil_a, a_val)]
        u = [_gelu_parts(z) for z in ld(ub_ref)]
        vv = [_gelu_parts(z) for z in ld(vb_ref)]
        vnorm = [_rms(v[0]) for v in vv]
        gv = gv_ref[...]
        vn = [(n[1] * gv).astype(BF16) for n in vnorm]
        w_cat = cat([ws_ref[g].astype(BF16) for g in range(N_SGU_GROUPS)], 1)
        wT_cat = cat([wsT_ref[g].astype(BF16) for g in range(N_SGU_GROUPS)], 1)
        gmb = [m.astype(BF16) for m in gm]
        by_group = lambda chunk: cat([chunk * gmb[g] for g in range(N_SGU_GROUPS)], 0)
        btab = btab_ref[...]
        mixed = [cat([btab + _nn(w_cat, by_group(vn[h][c, :])) for c in chunks], 0) for h in H]
        sg = [u[h][0] * mixed[h] for h in H]
        sil_b = [_silu_parts(z) for z in ld(zb_ref)]
        gated_b = [sil_b[h][0] * sg[h] for h in H]

        kvv = kv_ref[...].astype(BF16)
        kp = [kvv[:, p] for p in pairs]
        vp = [kvv[:, MEM_WIDTH + pr * LANES:MEM_WIDTH + (pr + 1) * LANES] for pr in range(2)]
        qm = ld(qm_ref)
        qj = {(h, pr, j): (qm[h][:, pairs[pr]] * (hm[j] * 0.125)).astype(BF16) for h in H for pr, j in heads}
        sc = {k: _nt(qj[k], kp[k[1]]) for k in qj}
        ex = {k: jnp.exp(sc[k] - jnp.max(sc[k], axis=1, keepdims=True)) for k in qj}
        prob = {k: ex[k] * (1.0 / jnp.sum(ex[k], axis=1, keepdims=True)) for k in qj}
        probb = {k: prob[k].astype(BF16) for k in qj}
        mo = [cat([sum(_nn(probb[(h, pr, j)], vp[pr]) * hm[j] for j in (0, 1)) for pr in range(2)], 1) for h in H]
        sil_m = [_silu_parts(z) for z in ld(zm_ref)]
        gated_m = [sil_m[h][0] * mo[h] for h in H]

        gated = [cat([gated_a[h], gated_b[h], gated_m[h]], 1).astype(BF16) for h in H]
        wout = wout_ref[...]
        x_in = ld(x_ref)
        x2 = [x_in[h] + _nn(gated[h], wout) for h in H]
        fin = [_rms(z) for z in x2]
        gf = gf_ref[...]
        tgt = ld(t_ref)
        err = [fin[h][1] * gf - tgt[h] for h in H]
        loss_ref[...] += sum(jnp.sum(e * e) for e in err) * (0.5 / D_MODEL)

        dy = [e * (1.0 / D_MODEL) for e in err]
        dgf_ref[...] += sum(jnp.sum(dy[h] * fin[h][1], axis=0, keepdims=True) for h in H)
        gdy = [d * gf for d in dy]
        dx2 = [fin[h][0] * (gdy[h] - fin[h][1] * jnp.mean(gdy[h] * fin[h][1], axis=1, keepdims=True)) for h in H]
        for h in H:
            dx2_ref[rows[h], :] = dx2[h]
        dx2b = [d.astype(BF16) for d in dx2]
        dgated = [_nt(d, wout) for d in dx2b]
        dwout_ref[...] += _tn(cat(gated, 0), cat(dx2b, 0))
        dga = [d[:, 0:ATTN_WIDTH] for d in dgated]
        dgb = [d[:, ATTN_WIDTH:ATTN_WIDTH + SGU_WIDTH] for d in dgated]
        dgm = [d[:, ATTN_WIDTH + SGU_WIDTH:] for d in dgated]

        for h in H:
            da_ref[rows[h], :] = dga[h] * sil_a[h][0]
        dza = [dga[h] * a_val[h] * sil_a[h][1] for h in H]

        dsg = [dgb[h] * sil_b[h][0] for h in H]
        dzb = [dgb[h] * sg[h] * sil_b[h][1] for h in H]
        dub = [dsg[h] * mixed[h] * u[h][1] for h in H]
        dmixed = [dsg[h] * u[h][0] for h in H]
        dmixed_b = [d.astype(BF16) for d in dmixed]
        dvn = [cat([_nn(wT_cat, by_group(dmixed_b[h][c, :])) for c in chunks], 0) for h in H]
        for g in range(N_SGU_GROUPS):
            dws_ref[g] += sum(_nt((dmixed[h][c, :] * gm[g]).astype(BF16), vn[h][c, :]) for h in H for c in chunks)
        dbtab_scr[...] += sum(dmixed[h][c, :] for h in H for c in chunks)
        dgv_ref[...] += sum(jnp.sum(dvn[h] * vnorm[h][1], axis=0, keepdims=True) for h in H)
        tv = [d * gv for d in dvn]
        dvv = [vnorm[h][0] * (tv[h] - vnorm[h][1] * jnp.mean(tv[h] * vnorm[h][1], axis=1, keepdims=True)) for h in H]
        dvb = [dvv[h] * vv[h][1] for h in H]

        dmo = [dgm[h] * sil_m[h][0] for h in H]
        dzm = [dgm[h] * mo[h] * sil_m[h][1] for h in H]
        dmoj = {(h, pr, j): (dmo[h][:, pairs[pr]] * hm[j]).astype(BF16) for h in H for pr, j in heads}
        dp = {k: _nt(dmoj[k], vp[k[1]]) for k in qj}
        ds = {k: (prob[k] * (dp[k] - jnp.sum(dp[k] * prob[k], axis=1, keepdims=True))).astype(BF16) for k in qj}
        dqm = [cat([sum(_nn(ds[(h, pr, j)], kp[pr]) * (hm[j] * 0.125) for j in (0, 1)) for pr in range(2)], 1)
               for h in H]
        every = lambda tbl, pr: cat([tbl[(h, pr, j)] for h in H for j in (0, 1)], 0)
        dk = [_tn(every(ds, pr), every(qj, pr)) for pr in range(2)]
        dv = [_tn(every(probb, pr), every(dmoj, pr)) for pr in range(2)]
        dkv_ref[...] += cat(dk + dv, 1)

        for h in H:
            drest_ref[rows[h], :] = cat([dza[h], dub[h], dvb[h], dzb[h], dqm[h], dzm[h]], 1).astype(BF16)

        @pl.when(last)
        def _():
            lane = lax.broadcasted_iota(jnp.int32, (1, LANES), 1)
            dbt = dbtab_scr[...]
            out = jnp.zeros((SGU_CHUNK, LANES), F32)
            for g in range(N_SGU_GROUPS):
                out = out + jnp.where(lane == g, jnp.sum(dbt * gm[g], axis=1, keepdims=True), 0.0)
            dbs_ref[...] = out
            for r0 in range(0, D_MODEL, SGU_CHUNK):
                k, row = divmod(r0, D_MODEL // N_CHIPS)
                dwout_bf_ref[k, row:row + SGU_CHUNK, :] = dwout_ref[r0:r0 + SGU_CHUNK, :].astype(BF16)

    tile = lambda w, cb: pl.BlockSpec((tm, w), lambda b, t, cb=cb: (b * nt + t, cb))
    const = lambda shape: pl.BlockSpec(shape, lambda b, t, n=len(shape): (0,) * n)
    return pl.pallas_call(
        body, grid=(B, nt),
        in_specs=[tile(D_MODEL, 0), tile(D_MODEL, 0), tile(ATTN_WIDTH, 0),
                  tile(ATTN_WIDTH, 3),
                  tile(SGU_WIDTH, 8), tile(SGU_WIDTH, 9), tile(SGU_WIDTH, 10),
                  tile(MEM_WIDTH, 11), tile(MEM_WIDTH, 12),
                  pl.BlockSpec((N_MEM, 2 * MEM_WIDTH), lambda b, t: (b, 0)),
                  const((N_SGU_GROUPS, SGU_CHUNK, SGU_CHUNK)), const((N_SGU_GROUPS, SGU_CHUNK, SGU_CHUNK)),
                  const((SGU_CHUNK, SGU_WIDTH)), const((1, SGU_WIDTH)),
                  const((D_MODEL, D_MODEL)), const((1, D_MODEL))],
        out_specs=[tile(D_MODEL, 0), tile(ATTN_WIDTH, 0), tile(REST_COLS, 0),
                   const((8, LANES)), const((N_CHIPS, D_MODEL // N_CHIPS, D_MODEL)),
                   const((N_SGU_GROUPS, SGU_CHUNK, SGU_CHUNK)), const((SGU_CHUNK, LANES)),
                   const((1, SGU_WIDTH)), const((1, D_MODEL)),
                   pl.BlockSpec((N_MEM, 2 * MEM_WIDTH), lambda b, t: (b, 0))],
        out_shape=[jax.ShapeDtypeStruct((T, D_MODEL), F32), jax.ShapeDtypeStruct((T, ATTN_WIDTH), F32),
                   jax.ShapeDtypeStruct((T, REST_COLS), BF16),
                   jax.ShapeDtypeStruct((8, LANES), F32),
                   jax.ShapeDtypeStruct((N_CHIPS, D_MODEL // N_CHIPS, D_MODEL), BF16),
                   jax.ShapeDtypeStruct((N_SGU_GROUPS, SGU_CHUNK, SGU_CHUNK), F32),
                   jax.ShapeDtypeStruct((SGU_CHUNK, LANES), F32),
                   jax.ShapeDtypeStruct((1, SGU_WIDTH), F32), jax.ShapeDtypeStruct((1, D_MODEL), F32),
                   jax.ShapeDtypeStruct((B * N_MEM, 2 * MEM_WIDTH), F32)],
        scratch_shapes=[pltpu.VMEM((SGU_CHUNK, SGU_WIDTH), F32), pltpu.VMEM((D_MODEL, D_MODEL), F32)],
        compiler_params=_params(("arbitrary", "arbitrary"), vmem=VMEM_LIMIT + 2 * 1024 * 1024), name="mid")(
            x2d, t2d, a, proj, proj, proj, proj, proj, proj, kv, w_s, w_sT, b_tab, g_v, w_out, g_final)


def _inproj_bwd_dx(dq, dk, dv, drest, x2d, dx2, g_norm, w_in_t, after=()):
    T = x2d.shape[0]
    tm = 512
    W = ATTN_WIDTH

    def body(dq_ref, dk_ref, dv_ref, dr_ref, x_ref, dx2_ref, g_ref, w_ref, *rest):
        gx_ref, dg_ref = rest[-2:]

        @pl.when(pl.program_id(0) == 0)
        def _():
            dg_ref[...] = jnp.zeros_like(dg_ref)

        halves = [pl.ds(h * (tm // 2), tm // 2) for h in (0, 1)]
        dh = [(_nn(dq_ref[r, :], w_ref[0:W, :]) + _nn(dk_ref[r, :], w_ref[W:2 * W, :])
               + _nn(dv_ref[r, :], w_ref[2 * W:3 * W, :]) + _nn(dr_ref[r, :], w_ref[QKV_COLS:IN_COLS, :]))
              for r in halves]
        nrm = [_rms(x_ref[r, :]) for r in halves]
        dg_ref[...] += sum(jnp.sum(d * n[1], axis=0, keepdims=True) for d, n in zip(dh, nrm))
        g = g_ref[...]
        for r, d, (rstd, xh) in zip(halves, dh, nrm):
            th = d * g
            gx_ref[r, :] = rstd * (th - xh * jnp.mean(th * xh, axis=1, keepdims=True)) + dx2_ref[r, :]

    tile = lambda w: pl.BlockSpec((tm, w), lambda i: (i, 0))
    return pl.pallas_call(
        body, grid=(T // tm,),
        in_specs=[tile(W), tile(W), tile(W), tile(REST_COLS), tile(D_MODEL), tile(D_MODEL),
                  pl.BlockSpec((1, D_MODEL), lambda i: (0, 0)),
                  pl.BlockSpec((IN_COLS, D_MODEL), lambda i: (0, 0))] + _after(after),
        out_specs=[tile(D_MODEL), pl.BlockSpec((1, D_MODEL), lambda i: (0, 0))],
        out_shape=[jax.ShapeDtypeStruct((T, D_MODEL), F32), jax.ShapeDtypeStruct((1, D_MODEL), F32)],
        compiler_params=_params(("arbitrary",)), name="inproj_bwd_dx")(
            dq, dk, dv, drest, x2d, dx2, g_norm, w_in_t, *after)


def _inproj_bwd_dw(dq, dk, dv, drest, x2d, g_norm, reduce=False):
    T = x2d.shape[0]
    tm = 512
    nt = T // tm
    W = ATTN_WIDTH
    shard = IN_COLS // N_CHIPS
    half = shard // 2
    row_block = 32

    def body(dq_ref, dk_ref, dv_ref, dr_ref, x_ref, g_ref, *rest):
        if reduce:
            sends, own, acc, ras, narrow, s_sem, r_sem = rest
            x, y, c, chip, peers, peer_chip = _place()
            sib = (x, y, 1 - c)

            def part(k, cc, r0):
                return acc.at[pl.ds(pl.multiple_of(k * shard + cc * half + r0, 8), row_block), :]

            def swap_win(k):
                return _remote(narrow.at[k], ras.at[k], s_sem.at[k], r_sem.at[k], sib)
        else:
            acc = rest[0]

        @pl.when(pl.program_id(0) == 0)
        def _():
            acc[...] = jnp.zeros_like(acc)

        _, xh = _rms(x_ref[...])
        h = (xh * g_ref[...]).astype(BF16)
        acc[0:W, :] += _tn(dq_ref[...], h)
        acc[W:2 * W, :] += _tn(dk_ref[...], h)
        acc[2 * W:3 * W, :] += _tn(dv_ref[...], h)
        acc[QKV_COLS:IN_COLS, :] += _tn(dr_ref[...], h)

        if reduce:
            @pl.when(pl.program_id(0) == nt - 1)
            def _():
                for k in range(N_CHIPS):
                    def to_bf16(i, carry, k=k):
                        r0 = pl.multiple_of(i * row_block, row_block)
                        narrow[k, pl.ds(r0, row_block), :] = part(k, 1 - c, r0)[...].astype(BF16)
                        return carry
                    lax.fori_loop(0, half // row_block, to_bf16, 0)
                    swap_win(k).start()

                def chip_sum(k, r0):
                    return part(k, c, r0)[...] + ras[k, pl.ds(r0, row_block), :].astype(F32)

                for k in range(N_CHIPS):
                    swap_win(k).wait_recv()

                    @pl.when(chip == k)
                    def _(k=k):
                        def mine(i, carry):
                            r0 = pl.multiple_of(i * row_block, row_block)
                            own[pl.ds(r0, row_block), :] = chip_sum(k, r0)
                            return carry
                        lax.fori_loop(0, half // row_block, mine, 0)

                    @pl.when(chip != k)
                    def _(k=k):
                        def other(i, carry):
                            r0 = pl.multiple_of(i * row_block, row_block)
                            sends[(k ^ chip) - 1, pl.ds(r0, row_block), :] = chip_sum(k, r0).astype(BF16)
                            return carry
                        lax.fori_loop(0, half // row_block, other, 0)
                for k in range(N_CHIPS):
                    swap_win(k).wait_send()

    tile = lambda w: pl.BlockSpec((tm, w), lambda i: (i, 0))
    vmem = pl.BlockSpec(memory_space=pltpu.VMEM)
    in_specs = [tile(W), tile(W), tile(W), tile(REST_COLS), tile(D_MODEL), pl.BlockSpec((1, D_MODEL), lambda i: (0, 0))]
    if not reduce:
        return pl.pallas_call(
            body, grid=(nt,), in_specs=in_specs,
            out_specs=pl.BlockSpec((IN_COLS, D_MODEL), lambda i: (0, 0)),
            out_shape=jax.ShapeDtypeStruct((IN_COLS, D_MODEL), F32),
            compiler_params=_params(("arbitrary",)), name="inproj_bwd_dw")(dq, dk, dv, drest, x2d, g_norm)
    quarters = pltpu.VMEM((N_CHIPS, half, D_MODEL), BF16)
    return pl.pallas_call(
        body, grid=(nt,), in_specs=in_specs, out_specs=[vmem] * 2,
        out_shape=[jax.ShapeDtypeStruct((N_CHIPS - 1, half, D_MODEL), BF16),
                   jax.ShapeDtypeStruct((half, D_MODEL), F32)],
        scratch_shapes=[pltpu.VMEM((IN_COLS, D_MODEL), F32), quarters, quarters,
                        pltpu.SemaphoreType.DMA((N_CHIPS,)), pltpu.SemaphoreType.DMA((N_CHIPS,))],
        compiler_params=_params(("arbitrary",)), name="inproj_bwd_dw_reduce")(dq, dk, dv, drest, x2d, g_norm)


def _adamw_update(w, g, m, v):
    nm = ADAM_B1 * m + (1.0 - ADAM_B1) * g
    nv = ADAM_B2 * v + (1.0 - ADAM_B2) * (g * g)
    m_hat = nm / (1.0 - ADAM_B1 ** ADAM_STEP)
    v_hat = nv / (1.0 - ADAM_B2 ** ADAM_STEP)
    return -ADAM_LR * (m_hat / (jnp.sqrt(v_hat) + ADAM_EPS) + ADAM_WD * w), nm, nv


def _adamw_all(g_packed, ws, ms, vs, large):
    n, nl = len(ws), len(large)
    row_block = 8
    chunk_bytes = 512 * 1024

    def chunk_rows(w):
        R, C = w.shape
        return max(r for r in range(8, R + 1, 8) if R % r == 0 and r * C * 4 <= chunk_bytes)

    jobs = [(b, r0, chunk_rows(four[0])) for b, four in enumerate(large)
            for r0 in range(0, four[0].shape[0], chunk_rows(four[0]))]

    def body(*refs):
        g_ref = refs[0]
        w_refs, m_refs, v_refs = refs[1:1 + n], refs[1 + n:1 + 2 * n], refs[1 + 2 * n:1 + 3 * n]
        far_in = refs[1 + 3 * n:1 + 3 * n + 4 * nl]
        outs = refs[1 + 3 * n + 4 * nl:1 + 7 * n + 4 * nl]
        far_out = refs[1 + 7 * n + 4 * nl:1 + 7 * n + 8 * nl]
        loss_ref = refs[1 + 7 * n + 8 * nl]
        scr = refs[2 + 7 * n + 8 * nl:]
        in_scr, out_scr, in_sem, out_sem = scr[:4 * nl], scr[4 * nl:7 * nl], scr[7 * nl], scr[7 * nl + 1]

        def read(j, k):
            b, r0, rows = jobs[j]
            blk = pl.ds(r0, rows)
            return pltpu.make_async_copy(far_in[4 * b + k].at[blk, :], in_scr[4 * b + k].at[blk, :],
                                         in_sem.at[4 * j + k])

        def write(j, k):
            b, r0, rows = jobs[j]
            blk = pl.ds(r0, rows)
            src = in_scr[4 * b + 1] if k == 0 else out_scr[3 * b + k - 1]
            return pltpu.make_async_copy(src.at[blk, :], far_out[4 * b + k].at[blk, :], out_sem.at[4 * j + k])

        for j in range(len(jobs)):
            for k in range(4):
                read(j, k).start()
        off = 0
        for i, (_, used, padded) in enumerate(_SMALL_PARTS[:n]):
            g = g_ref[off:off + used, :]
            delta, nm, nv = _adamw_update(w_refs[i][...], g, m_refs[i][...], v_refs[i][...])
            outs[4 * i][...], outs[4 * i + 1][...], outs[4 * i + 2][...], outs[4 * i + 3][...] = g, delta, nm, nv
            off += padded
        loss_ref[...] = g_ref[_LOSS_ROW:_LOSS_ROW + 1, 0:1]
        for j, (b, r0, rows) in enumerate(jobs):
            for k in range(4):
                read(j, k).wait()

            def update(i, carry, b=b, r0=r0):
                blk = pl.ds(pl.multiple_of(r0 + i * row_block, row_block), row_block)
                w, g, m, v = [in_scr[4 * b + k][blk, :] for k in range(4)]
                for k, val in enumerate(_adamw_update(w, g, m, v)):
                    out_scr[3 * b + k][blk, :] = val
                return carry
            lax.fori_loop(0, rows // row_block, update, 0)
            for k in range(4):
                write(j, k).start()
        for j in range(len(jobs)):
            for k in range(4):
                write(j, k).wait()

    vmem = pl.BlockSpec(memory_space=pltpu.VMEM)
    far = pl.BlockSpec(memory_space=pl.ANY)
    flat = [a for four in large for a in four]
    outs = pl.pallas_call(
        body, in_specs=[vmem] * (1 + 3 * n) + [far] * (4 * nl),
        out_specs=[vmem] * (4 * n) + [far] * (4 * nl) + [vmem],
        out_shape=[jax.ShapeDtypeStruct(w.shape, F32) for w in ws for _ in range(4)]
        + [jax.ShapeDtypeStruct(four[0].shape, F32) for four in large for _ in range(4)]
        + [jax.ShapeDtypeStruct((1, 1), F32)],
        scratch_shapes=[pltpu.VMEM(a.shape, F32) for a in flat]
        + [pltpu.VMEM(four[0].shape, F32) for four in large for _ in range(3)]
        + [pltpu.SemaphoreType.DMA((4 * len(jobs),)), pltpu.SemaphoreType.DMA((4 * len(jobs),))],
        compiler_params=_params(), name="adamw_all")(g_packed, *ws, *ms, *vs, *flat)
    return ([outs[4 * i:4 * i + 4] for i in range(n)], [outs[4 * (n + i):4 * (n + i) + 4] for i in range(nl)],
            outs[4 * (n + nl)])


def _place():
    x, y, c = lax.axis_index("x"), lax.axis_index("y"), lax.axis_index("c")
    chip = 2 * x + y
    peers = [(x, 1 - y), (1 - x, y), (1 - x, 1 - y)]
    peer_chip = [2 * px + py for px, py in peers]
    return x, y, c, chip, peers, peer_chip


def _remote(src, dst, send_sem, recv_sem, dev):
    return pltpu.make_async_remote_copy(src_ref=src, dst_ref=dst, send_sem=send_sem, recv_sem=recv_sem,
                                        device_id=dev, device_id_type=MESH)


def _ag_weights(weights, late=()):
    nw, nl = len(weights), len(late)
    pieces = 2

    def body(*refs):
        srcs, late_srcs = refs[:nw], refs[nw:nw + nl]
        outs, late_bf, late_land = (refs[nw + nl:2 * nw + nl], refs[2 * nw + nl:2 * nw + 2 * nl],
                                    refs[2 * nw + 2 * nl:2 * nw + 3 * nl])
        scr = refs[2 * nw + 3 * nl:]
        wide, narrow = scr[:nw], scr[nw:2 * nw]
        in_sem, put_sem, s_ici, r_ici, s_d2d, r_d2d = scr[2 * nw:]
        x, y, c = lax.axis_index("x"), lax.axis_index("y"), lax.axis_index("c")
        chip = 2 * x + y
        sib = (x, y, 1 - c)
        first = ((x + 1 - c) % 2, (y + c) % 2)
        second = ((x + c) % 2, (y + 1 - c) % 2)
        first_chip, second_chip = 2 * first[0] + first[1], 2 * second[0] + second[1]
        diag_chip = 3 - chip

        parts = [(w, out, pc) for w, out in enumerate(outs) for pc in range(pieces)]

        def rows_of(out, cc, pc):
            rows = out.shape[1] // 2 // pieces
            return pl.ds(pl.multiple_of((cc * pieces + pc) * rows, 16), rows)

        def piece(out, k, cc, pc):
            return out.at[k, rows_of(out, cc, pc), :]

        def ici(w, slot, out, k, dev, pc, src=None):
            blk, sem = piece(out, k, c, pc), (nw * slot + w) * pieces + pc
            return _remote(blk if src is None else src, blk, s_ici.at[sem], r_ici.at[sem], (dev[0], dev[1], c))

        def d2d(w, slot, out, k, cc, pc):
            blk, sem = piece(out, k, cc, pc), (nw * slot + w) * pieces + pc
            return _remote(blk, blk, s_d2d.at[sem], r_d2d.at[sem], sib)

        def read(w, cc, pc):
            rows = rows_of(outs[w], cc, pc)
            return pltpu.make_async_copy(srcs[w].at[rows, :], wide[w].at[rows, :],
                                         in_sem.at[(w * 2 + cc) * pieces + pc])

        order = [(w, cc, pc) for cc in (0, 1) for w in range(nw) for pc in range(pieces)]
        for w, cc, pc in order:
            read(w, (c + cc) % 2, pc).start()
        sent = []
        for w, cc, pc in order[:nw * pieces]:
            rows = rows_of(outs[w], c, pc)
            read(w, c, pc).wait()
            narrow[w][rows, :] = wide[w][rows, :].astype(BF16)
            for slot, dev in enumerate((first, second)):
                sent.append(ici(w, slot, outs[w], chip, dev, pc, src=narrow[w].at[rows, :]))
                sent[-1].start()
        for src, bf, land in zip(late_srcs, late_bf, late_land):
            bf[...] = src[...].astype(BF16)
            land[...] = jnp.zeros_like(land)
            land[chip] = bf[...]
        for w, cc, pc in order[nw * pieces:]:
            rows = rows_of(outs[w], 1 - c, pc)
            read(w, 1 - c, pc).wait()
            narrow[w][rows, :] = wide[w][rows, :].astype(BF16)
        puts = [pltpu.make_async_copy(narrow[w], outs[w].at[chip], put_sem.at[w]) for w in range(nw)]
        for cp in puts:
            cp.start()
        for slot, k, dev in ((0, first_chip, first), (1, second_chip, second), (2, diag_chip, second)):
            for w, out, pc in parts:
                ici(w, slot, out, k, dev, pc).wait_recv()
                if slot == 0:
                    sent.append(ici(w, 2, out, k, second, pc))
                    sent[-1].start()
                sent.append(d2d(w, slot, out, k, c, pc))
                sent[-1].start()
        for slot, k in ((0, second_chip), (1, first_chip), (2, diag_chip)):
            for w, out, pc in parts:
                d2d(w, slot, out, k, 1 - c, pc).wait_recv()
        for cp in sent:
            cp.wait_send()
        for cp in puts:
            cp.wait()

    vmem = pl.BlockSpec(memory_space=pltpu.VMEM)
    far = pl.BlockSpec(memory_space=pl.ANY)
    outs = pl.pallas_call(
        body,
        out_shape=[jax.ShapeDtypeStruct((N_CHIPS,) + w.shape, BF16) for w in weights]
        + [jax.ShapeDtypeStruct(w.shape, BF16) for w in late]
        + [jax.ShapeDtypeStruct((N_CHIPS,) + w.shape, BF16) for w in late],
        in_specs=[far] * nw + [vmem] * nl, out_specs=[far] * nw + [vmem] * (2 * nl),
        scratch_shapes=[pltpu.VMEM(w.shape, F32) for w in weights] + [pltpu.VMEM(w.shape, BF16) for w in weights]
        + [pltpu.SemaphoreType.DMA((2 * nw * pieces,)), pltpu.SemaphoreType.DMA((nw,))]
        + [pltpu.SemaphoreType.DMA((3 * nw * pieces,))] * 4,
        compiler_params=pltpu.CompilerParams(vmem_limit_bytes=VMEM_LIMIT), name="ag_weights")(*weights, *late)
    return outs[:nw], outs[nw:nw + nl], outs[nw + nl:]


_HBM = pl.BlockSpec(memory_space=pltpu.HBM)
_SEM = pl.BlockSpec(memory_space=pltpu.SEMAPHORE)
_ANY = pl.BlockSpec(memory_space=pl.ANY)
_DATAFLOW = pltpu.SideEffectType.DATAFLOW_SIDE_EFFECTING


def _in_hbm(a):
    return pltpu.with_memory_space_constraint(a, pltpu.HBM)


_PEERS_OF = {"gather": 3, "scatter": 3, "direct": 7}


def _exchange_copies(mode, srcs, lands, send_sems, recv_sems):
    nw = len(srcs)
    x, y, c, chip, peers, peer_chip = _place()
    pairs = []
    if mode == "direct":
        targets = [((x, y), chip, 1)] + [(p, k, d) for p, k in zip(peers, peer_chip) for d in (0, 1)]
        for r, ((px, py), k, d) in enumerate(targets):
            for w in range(nw):
                sems = (send_sems.at[nw * r + w], recv_sems.at[nw * r + w], (px, py, (c + d) % 2))
                share = srcs[w].at[k if srcs[w].shape[0] > 1 else 0]
                pairs.append((_remote(share, lands[w].at[r], *sems),) * 2)
        return pairs
    gather = mode == "gather"
    for m, (px, py) in enumerate(peers):
        for w in range(nw):
            sems = (send_sems.at[nw * m + w], recv_sems.at[nw * m + w], (px, py, c))
            if gather:
                pairs.append((_remote(srcs[w], lands[w].at[chip], *sems),
                              _remote(srcs[w], lands[w].at[peer_chip[m]], *sems)))
            else:
                pairs.append((_remote(srcs[w].at[m], lands[w].at[m], *sems),) * 2)
    return pairs


def _exchange_start(mode, srcs, after, name, lands=None):
    nw = len(srcs)
    n_copies = _PEERS_OF[mode] * nw

    after = tuple(after)

    def body(*refs):
        send_sems, recv_sems = refs[2 * nw + len(after)], refs[2 * nw + len(after) + 1]
        for start, _ in _exchange_copies(mode, refs[:nw], refs[nw:2 * nw], send_sems, recv_sems):
            start.start()
        refs[-1][...] = jnp.zeros_like(refs[-1])

    if lands is None:
        shape = {"gather": lambda s: (N_CHIPS,) + s.shape, "scatter": lambda s: s.shape,
                 "direct": lambda s: (_PEERS_OF["direct"],) + s.shape[1:]}[mode]
        lands = [lax.empty(shape(s), s.dtype) for s in srcs]
    lands = [_in_hbm(l) for l in lands]
    return pl.pallas_call(
        body, name=name,
        out_shape=(pltpu.SemaphoreType.DMA((n_copies,)), pltpu.SemaphoreType.DMA((n_copies,)))
        + tuple(pltpu.HBM(s.shape, s.dtype) for s in srcs)
        + tuple(pltpu.HBM(l.shape, l.dtype) for l in lands)
        + (jax.ShapeDtypeStruct((8, LANES), F32),),
        in_specs=[_HBM] * (2 * nw) + [_ANY] * len(after),
        out_specs=(_SEM, _SEM) + (_HBM,) * (2 * nw) + (pl.BlockSpec(memory_space=pltpu.VMEM),),
        input_output_aliases={i: 2 + i for i in range(2 * nw)},
        compiler_params=pltpu.CompilerParams(has_side_effects=_DATAFLOW),
    )(*[_in_hbm(s) for s in srcs], *lands, *after)


def _exchange_wait(mode, started, after, name):
    nw = (len(started) - 3) // 2
    send_sems, recv_sems = started[0], started[1]
    thru = started[2:2 + 2 * nw]

    def body(*refs):
        for _, arrival in _exchange_copies(mode, refs[:nw], refs[nw:2 * nw], refs[2 * nw], refs[2 * nw + 1]):
            arrival.wait_send()
            arrival.wait_recv()

    outs = pl.pallas_call(
        body, name=name,
        out_shape=tuple(pltpu.HBM(t.shape, t.dtype) for t in thru),
        in_specs=[_HBM] * (2 * nw) + [_SEM, _SEM, _ANY], out_specs=(_HBM,) * (2 * nw),
        input_output_aliases={i: i for i in range(2 * nw)},
        compiler_params=pltpu.CompilerParams(has_side_effects=_DATAFLOW),
    )(*thru, send_sems, recv_sems, after)
    return outs[:nw], outs[nw:]


def _reduce_last(owns, landed, g_small, direct_srcs, direct_landed, spread_row0):
    ns, nd = len(owns), len(direct_srcs)
    halves = [o.shape[0] for o in owns]
    row_block = 16
    spread_rows = direct_srcs[-1].shape[1]
    rest0, rest1 = spread_row0, SMALL_ROWS - spread_row0 - spread_rows
    hs = (rest0 + rest1) // 2
    jobs = [(w, p * (halves[w] // 2), halves[w] // 2) for w in range(ns) for p in range(2)]
    n_swaps = len(jobs)
    jobs += [(ns + d, 0, direct_srcs[d].shape[1]) for d in range(nd)]

    def body(*refs):
        own_refs, land_refs, gsm_ref = refs[:ns], refs[ns:2 * ns], refs[2 * ns]
        dsrc_refs, dland_refs = refs[2 * ns + 1:2 * ns + 1 + nd], refs[2 * ns + 1 + nd:2 * ns + 1 + 2 * nd]
        n_in = 2 * ns + 1 + 2 * nd
        out_refs, osm_ref = refs[n_in:n_in + ns + nd - 1], refs[n_in + ns + nd - 1]
        scr = refs[n_in + ns + nd:]
        own_scr, land_scr, dsrc_scr, dland_scr = (scr[:ns], scr[ns:2 * ns], scr[2 * ns:2 * ns + nd],
                                                  scr[2 * ns + nd:2 * ns + 2 * nd])
        res = scr[2 * ns + 2 * nd:3 * ns + 3 * nd - 1]
        o_rest, ra_sm, p_sm, put_sem, in_sem, s_sem, r_sem, sm_s, sm_r = scr[3 * ns + 3 * nd - 1:]
        x, y, c, chip, peers, peer_chip = _place()
        sib = (x, y, 1 - c)
        half = lambda cc: pl.ds(pl.multiple_of(cc * hs, 8), hs)
        sm_a = _remote(gsm_ref.at[half(1 - c), :], ra_sm, sm_s.at[0], sm_r.at[0], sib)
        sm_a.start()
        swaps = [sm_a]

        def reads(j):
            w, r0, n = jobs[j]
            rows = pl.ds(r0, n)
            if w < ns:
                pairs = [(own_refs[w].at[rows, :], own_scr[w].at[rows, :]),
                         (land_refs[w].at[:, rows, :], land_scr[w].at[:, rows, :])]
            else:
                share = dsrc_refs[w - ns].at[chip if w < ns + nd - 1 else 0]
                pairs = [(share, dsrc_scr[w - ns]), (dland_refs[w - ns], dland_scr[w - ns])]
            return [pltpu.make_async_copy(s, d, in_sem.at[2 * j + i]) for i, (s, d) in enumerate(pairs)]

        for j in range(len(jobs)):
            for cp in reads(j):
                cp.start()
        sm_a.wait_recv()
        p_sm[chip] = gsm_ref[half(c), :] + ra_sm[...]
        for m, (px, py) in enumerate(peers):
            swaps.append(_remote(p_sm.at[chip], p_sm.at[chip], sm_s.at[1 + m], sm_r.at[1 + m], (px, py, c)))
            swaps[-1].start()

        def mine_of(j):
            w, r0, n = jobs[j]
            return out_refs[w].at[pl.ds(pl.multiple_of(c * halves[w] + r0, 8), n), :]

        def sum_of(j):
            w, r0, n = jobs[j]
            return res[w].at[pl.ds(r0, n), :]

        puts = []
        for j, (w, r0, n) in enumerate(jobs):
            for cp in reads(j):
                cp.wait()

            def total(i, carry, w=w, r0=r0):
                rr = pl.multiple_of(r0 + i * row_block, row_block)
                blk = pl.ds(rr, row_block)
                if w < ns:
                    acc = own_scr[w][blk, :]
                    for m in range(_PEERS_OF["scatter"]):
                        acc = acc + land_scr[w][m, blk, :].astype(F32)
                    res[w][blk, :] = acc
                else:
                    theirs = lambda r: dland_scr[w - ns][r, blk, :].astype(F32)
                    acc = ((dsrc_scr[w - ns][blk, :].astype(F32) + theirs(0)) + (theirs(1) + theirs(2))) + (
                        (theirs(3) + theirs(4)) + (theirs(5) + theirs(6)))
                    if w < ns + nd - 1:
                        res[w][blk, :] = acc
                    else:
                        osm_ref[pl.ds(pl.multiple_of(spread_row0 + rr, 8), row_block), :] = acc
                return carry
            lax.fori_loop(0, n // row_block, total, 0)
            if w < ns:
                puts.append(pltpu.make_async_copy(sum_of(j), mine_of(j), put_sem.at[j]))
                swaps.append(_remote(sum_of(j), mine_of(j), s_sem.at[j], r_sem.at[j], sib))
                swaps[-1].start()
            elif w < ns + nd - 1:
                puts.append(pltpu.make_async_copy(res[w], out_refs[w], put_sem.at[j]))
            if w < ns + nd - 1:
                puts[-1].start()
        for m, (px, py) in enumerate(peers):
            _remote(p_sm.at[chip], p_sm.at[peer_chip[m]], sm_s.at[1 + m], sm_r.at[1 + m], (px, py, c)).wait_recv()
        o_rest[half(c), :] = (p_sm[0] + p_sm[1]) + (p_sm[2] + p_sm[3])
        swaps.append(_remote(o_rest.at[half(c), :], o_rest.at[half(c), :], sm_s.at[4], sm_r.at[4], sib))
        swaps[-1].start()
        for j, (w, r0, n) in enumerate(jobs[:n_swaps]):
            theirs = out_refs[w].at[pl.ds(pl.multiple_of((1 - c) * halves[w] + r0, 8), n), :]
            _remote(sum_of(j), theirs, s_sem.at[j], r_sem.at[j], sib).wait_recv()
        _remote(o_rest.at[half(1 - c), :], o_rest.at[half(1 - c), :], sm_s.at[4], sm_r.at[4], sib).wait_recv()
        osm_ref[0:rest0, :] = o_rest[0:rest0, :]
        osm_ref[SMALL_ROWS - rest1:SMALL_ROWS, :] = o_rest[rest0:rest0 + rest1, :]
        for cp in swaps:
            cp.wait_send()
        for cp in puts:
            cp.wait()

    vmem = pl.BlockSpec(memory_space=pltpu.VMEM)
    far = [pl.BlockSpec(memory_space=pl.ANY)]
    return pl.pallas_call(
        body, out_shape=[jax.ShapeDtypeStruct((2 * o.shape[0], o.shape[1]), F32) for o in owns]
        + [jax.ShapeDtypeStruct(s.shape[1:], F32) for s in direct_srcs[:-1]]
        + [jax.ShapeDtypeStruct((SMALL_ROWS, LANES), F32)],
        in_specs=far * (2 * ns) + [vmem] + far * (2 * nd), out_specs=far * (ns + nd - 1) + [vmem],
        scratch_shapes=[pltpu.VMEM(o.shape, o.dtype) for o in owns] + [pltpu.VMEM(l.shape, l.dtype) for l in landed]
        + [pltpu.VMEM(s.shape[1:], s.dtype) for s in direct_srcs]
        + [pltpu.VMEM(l.shape, l.dtype) for l in direct_landed]
        + [pltpu.VMEM(o.shape, F32) for o in owns] + [pltpu.VMEM(s.shape[1:], F32) for s in direct_srcs[:-1]]
        + [pltpu.VMEM((2 * hs, LANES), F32), pltpu.VMEM((hs, LANES), F32), pltpu.VMEM((N_CHIPS, hs, LANES), F32),
           pltpu.SemaphoreType.DMA((len(jobs),)), pltpu.SemaphoreType.DMA((2 * len(jobs),)),
           pltpu.SemaphoreType.DMA((n_swaps,)), pltpu.SemaphoreType.DMA((n_swaps,)),
           pltpu.SemaphoreType.DMA((5,)), pltpu.SemaphoreType.DMA((5,))],
        compiler_params=pltpu.CompilerParams(vmem_limit_bytes=VMEM_LIMIT),
        name="reduce_last")(*owns, *landed, g_small, *direct_srcs, *direct_landed)


_SMALL_PARTS = (("g_norm", 8, 8), ("w_s", 512, 512), ("b_s", 4, 8), ("g_v", 2, 8), ("g_mem", 8, 8),
                ("g_final", 8, 8), ("loss", 8, 8))
_LOSS_ROW = SMALL_ROWS - 8
_W_S_ROW = 8
assert sum(p for _, _, p in _SMALL_PARTS) == SMALL_ROWS and _SMALL_PARTS[1][0] == "w_s"


def _pack_small(parts, loss_block):
    rows = []
    parts = dict(parts, loss=loss_block)
    for name, used, padded in _SMALL_PARTS:
        if name == "w_s":
            continue
        p = parts[name].reshape(used, LANES)
        if padded > used:
            p = jnp.pad(p, ((0, padded - used), (0, 0)))
        rows.append(p)
    return jnp.concatenate(rows, axis=0)


def _local_step(x, mem, target, g_norm, w_in, w_s, b_s, g_v, g_mem, late_weights, g_final,
                fwd_token=None, on_late=None, on_dw=None):
    B, S, _ = x.shape
    x2d = x.reshape(B * S, D_MODEL)
    t2d = target.reshape(B * S, D_MODEL)
    mem2d = mem.reshape(B * N_MEM, D_MODEL)

    proj = _inproj_fwd(x2d, g_norm, w_in, after=() if fwd_token is None else (fwd_token,))
    w_kv, w_out = late_weights(proj)
    kv = _kv_fwd(mem2d, g_mem, w_kv)
    a, lse = _attn_fwd(proj, B, S)
    w_sT = jnp.swapaxes(w_s, 1, 2)
    b_tab = jnp.repeat(b_s.T, HEAD_DIM, axis=1)
    (dx2, da, drest, loss, d_wout, d_ws, d_bs, d_gv, d_gf, dkv) = _mid(
        x2d, t2d, a, proj, kv, w_s, w_sT, b_tab, g_v, w_out, g_final, B, S)
    d_wkv, d_gmem = _kv_bwd(mem2d, g_mem, w_kv, dkv)
    dq, dk, dv = _attn_bwd(proj, a, lse, da, B, S, after=() if on_late is None else (on_late(d_wkv, d_wout, d_ws),))
    if on_dw is None:
        d_win = _inproj_bwd_dw(dq, dk, dv, drest, x2d, g_norm)
        after = ()
    else:
        d_win = None
        after = (on_dw(*_inproj_bwd_dw(dq, dk, dv, drest, x2d, g_norm, reduce=True)),)
    grad_x, d_gnorm = _inproj_bwd_dx(dq, dk, dv, drest, x2d, dx2, g_norm, w_in, after=after)
    d_bs = d_bs[:, :N_SGU_GROUPS].T
    return (loss, grad_x.reshape(B, S, D_MODEL),
            dict(g_norm=d_gnorm, w_in=d_win, w_s=d_ws, b_s=d_bs, g_v=d_gv, g_mem=d_gmem, w_kv=d_wkv,
                 w_out=d_wout, g_final=d_gf))


def kernel(x, mem, g_norm, w_in, w_sgu_spatial, b_sgu_spatial, g_sgu_v, g_mem, w_mem_kv, w_out, g_final, loss_target, m_g_norm, m_w_in, m_w_sgu_spatial, m_b_sgu_spatial, m_g_sgu_v, m_g_mem, m_w_mem_kv, m_w_out, m_g_final, v_g_norm, v_w_in, v_w_sgu_spatial, v_b_sgu_spatial, v_g_sgu_v, v_g_mem, v_w_mem_kv, v_w_out, v_g_final):
    t = lambda w: jnp.swapaxes(w[0], 0, 1)
    (win_all,), late_shards, late_lands = _ag_weights([t(w_in)], [w_mem_kv[0], w_out[0]])
    w_in_full = win_all.reshape(-1, win_all.shape[-1])
    late = _exchange_start("gather", list(late_shards), (win_all,), "gather_late_start", lands=late_lands)

    def late_weights(proj):
        return [z.reshape(-1, z.shape[-1]) for z in _exchange_wait("gather", late, proj, "gather_late_wait")[1]]

    scatter = {}

    def on_late(d_wkv, d_wout, d_ws):
        d_ws = d_ws.reshape(1, -1, LANES)
        scatter["late"] = _exchange_start("direct", [d_wkv, d_wout, d_ws], (), "scatter_late_start")
        return scatter["late"][-1]

    def on_dw(sends, own):
        scatter["own"] = own
        scatter["started"] = _exchange_start("scatter", [sends], (own,), "scatter_start")
        return scatter["started"][-1]

    loss, grad_x, g = _local_step(
        x, mem, loss_target, g_norm, w_in_full, w_sgu_spatial[0], b_sgu_spatial[0], g_sgu_v, g_mem,
        late_weights, g_final.reshape(1, D_MODEL), fwd_token=late[-1], on_late=on_late, on_dw=on_dw)

    small_names = ("g_norm", "w_s", "b_s", "g_v", "g_mem", "g_final")
    g_small = _pack_small({n: g[n] for n in small_names if n != "w_s"}, loss)
    late_srcs, late_landed = _exchange_wait("direct", scatter["late"], g_small, "scatter_late_wait")
    _, landed = _exchange_wait("scatter", scatter["started"], late_landed[0], "scatter_wait")
    gr_in, gr_kv, gr_out, gr_small = _reduce_last([scatter["own"]], landed, g_small, late_srcs, late_landed, _W_S_ROW)

    small_w = (g_norm, w_sgu_spatial, b_sgu_spatial, g_sgu_v, g_mem, g_final)
    small_m = (m_g_norm, m_w_sgu_spatial, m_b_sgu_spatial, m_g_sgu_v, m_g_mem, m_g_final)
    small_v = (v_g_norm, v_w_sgu_spatial, v_b_sgu_spatial, v_g_sgu_v, v_g_mem, v_g_final)
    rows = lambda ws: [w.reshape(-1, LANES) for w in ws]
    small_new, (of_in, (gr_kv, d_kv, nm_kv, nv_kv), (gr_out, d_out, nm_out, nv_out)), loss = _adamw_all(
        gr_small, rows(small_w), rows(small_m), rows(small_v),
        [(t(w_in), gr_in, t(m_w_in), t(v_w_in)), (w_mem_kv[0], gr_kv, m_w_mem_kv[0], v_w_mem_kv[0]),
         (w_out[0], gr_out, m_w_out[0], v_w_out[0])])
    loss = loss.reshape(())
    small = [[z.reshape(w.shape) for z in four] for w, four in zip(small_w, small_new)]
    gr_in, d_in, nm_in, nv_in = [jnp.swapaxes(z, 0, 1) for z in of_in]

    def leaves(kind, big_in, big_kv, big_out):
        s_norm, s_ws, s_bs, s_gv, s_gmem, s_gf = [four[kind] for four in small]
        return [s_norm, big_in[None], s_ws, s_bs, s_gv, s_gmem, big_kv[None], big_out[None], s_gf]

    return (loss, grad_x, *leaves(0, gr_in, gr_kv, gr_out), *leaves(1, d_in, d_kv, d_out),
            *leaves(2, nm_in, nm_kv, nm_out), *leaves(3, nv_in, nv_kv, nv_out))
```

```python
import functools

import jax
import jax.numpy as jnp
from jax import lax
from jax.experimental import pallas as pl
from jax.experimental.pallas import tpu as pltpu

F32 = jnp.float32
BF16 = jnp.bfloat16
MESH = pl.DeviceIdType.MESH

D_MODEL = 1024
ATTN_WIDTH = 512
SGU_WIDTH = 256
MEM_WIDTH = 256
N_MEM = 256
IN_COLS = 3328
QKV_COLS = 3 * ATTN_WIDTH
REST_COLS = IN_COLS - QKV_COLS
SGU_CHUNK = 128
N_SGU_GROUPS = 4
EPS = 1e-6
NEG_INF = -1e30
DILATIONS = (1, 4, 16)
RADIUS = 64
Q_BLOCK = 128
LANES = 128
HEAD_DIM = 64

ADAM_LR = 0.001
ADAM_B1 = 0.9
ADAM_B2 = 0.999
ADAM_EPS = 1e-08
ADAM_WD = 0.01
ADAM_STEP = 10

N_CHIPS = 4
VMEM_LIMIT = 56 * 1024 * 1024
SMALL_ROWS = 560


def _params(sem=None, vmem=VMEM_LIMIT):
    return pltpu.CompilerParams(dimension_semantics=sem, vmem_limit_bytes=vmem)


def _nn(a, b):
    return jnp.dot(a, b, preferred_element_type=F32)


def _nt(a, b):
    return lax.dot_general(a, b, (((1,), (1,)), ((), ())), preferred_element_type=F32)


def _tn(a, b):
    return lax.dot_general(a, b, (((0,), (0,)), ((), ())), preferred_element_type=F32)


def _rms(x):
    r = lax.rsqrt(jnp.mean(x * x, axis=-1, keepdims=True) + EPS)
    return r, x * r


def _head_masks():
    lane = lax.broadcasted_iota(jnp.int32, (1, LANES), 1)
    lo = lane < HEAD_DIM
    return lo, (lo.astype(F32), (~lo).astype(F32))


def _silu_parts(z):
    s = jax.nn.sigmoid(z)
    return z * s, s * (1.0 + z * (1.0 - s))


def _gelu_parts(x):
    c = 0.7978845608028654
    x2 = x * x
    s = jax.nn.sigmoid((2.0 * c) * (x + 0.044715 * (x * x2)))
    return x * s, s * (1.0 + x * (1.0 - s) * ((2.0 * c) * (1.0 + 3.0 * 0.044715 * x2)))


def _after(tokens):
    return [pl.BlockSpec(memory_space=pl.ANY)] * len(tokens)


def _inproj_fwd(x2d, g_norm, w_in_t, after=()):
    T = x2d.shape[0]
    tm = 512

    def body(x_ref, g_ref, w_ref, *rest):
        o_ref = rest[-1]
        _, xh = _rms(x_ref[...])
        h = (xh * g_ref[...]).astype(BF16)
        o_ref[...] = _nt(h, w_ref[...])

    return pl.pallas_call(
        body, grid=(T // tm,),
        in_specs=[pl.BlockSpec((tm, D_MODEL), lambda i: (i, 0)),
                  pl.BlockSpec((1, D_MODEL), lambda i: (0, 0)),
                  pl.BlockSpec((IN_COLS, D_MODEL), lambda i: (0, 0))] + _after(after),
        out_specs=pl.BlockSpec((tm, IN_COLS), lambda i: (i, 0)),
        out_shape=jax.ShapeDtypeStruct((T, IN_COLS), F32),
        compiler_params=_params(("arbitrary",)), name="inproj_fwd")(x2d, g_norm, w_in_t, *after)


def _kv_fwd(mem2d, g_mem, w_kv):
    Tm = mem2d.shape[0]

    def body(m_ref, g_ref, w_ref, o_ref):
        _, mh = _rms(m_ref[...])
        o_ref[...] = _nn((mh * g_ref[...]).astype(BF16), w_ref[...])

    return pl.pallas_call(
        body, out_shape=jax.ShapeDtypeStruct((Tm, 2 * MEM_WIDTH), F32),
        compiler_params=_params(), name="kv_fwd")(mem2d, g_mem, w_kv)


def _kv_bwd(mem2d, g_mem, w_kv, dkv):
    Tm = mem2d.shape[0]

    def body(m_ref, g_ref, w_ref, dkv_ref, dw_ref, dg_ref):
        _, mh = _rms(m_ref[...])
        memn = (mh * g_ref[...]).astype(BF16)
        dkvb = dkv_ref[...].astype(BF16)
        dw = _tn(memn, dkvb).astype(BF16)
        for k in range(N_CHIPS):
            dw_ref[k] = dw[k * (D_MODEL // N_CHIPS):(k + 1) * (D_MODEL // N_CHIPS), :]
        dmemn = _nt(dkvb, w_ref[...])
        dg_ref[...] = jnp.sum(dmemn * mh, axis=0, keepdims=True)

    return pl.pallas_call(
        body, out_shape=(jax.ShapeDtypeStruct((N_CHIPS, D_MODEL // N_CHIPS, 2 * MEM_WIDTH), BF16),
                         jax.ShapeDtypeStruct((1, D_MODEL), F32)),
        compiler_params=_params(), name="kv_bwd")(mem2d, g_mem, w_kv, dkv)


def _attn_geometry(S):
    geom = []
    for d in DILATIONS:
        L = S // d
        assert L % Q_BLOCK == 0
        geom.append((d, L, min(2 * Q_BLOCK, L), L // Q_BLOCK))
    return geom


def _init_bias(bias_scr, geom, hp):
    row = lax.broadcasted_iota(jnp.int32, (Q_BLOCK, 2 * Q_BLOCK), 0)
    col = lax.broadcasted_iota(jnp.int32, (Q_BLOCK, 2 * Q_BLOCK), 1)
    for j in (0, 1):
        bits = (126 - (2 * hp + j)) * (1 << 23)
        slope = lax.bitcast_convert_type(jnp.full((1, 1), bits, jnp.int32), F32)
        for di, (d, _, _, _) in enumerate(geom):
            for cls, off in enumerate((0, -RADIUS, -2 * RADIUS)):
                dist = jnp.abs(col - row + off)
                bias_scr[di * 6 + cls * 2 + j] = jnp.where(
                    dist <= RADIUS, -(slope * float(d)) * dist.astype(F32), NEG_INF)


SPLIT = 4
COPY_ROWS = 256


def _by4_rows(S, step):
    per_class = S // SPLIT // COPY_ROWS
    r, j = step // per_class, step % per_class
    return (pl.ds(r + SPLIT * j * COPY_ROWS, COPY_ROWS, stride=SPLIT),
            pl.ds(r * (S // SPLIT) + j * COPY_ROWS, COPY_ROWS))


def _to_by4(src, dst, S):
    for i in range(S // COPY_ROWS):
        natural, by4 = _by4_rows(S, i)
        dst[by4, :] = src[natural, :]


def _block_slices(d, L, KW, nqb, r, qb, S):
    qs = qb * Q_BLOCK
    ks = jnp.clip(qs - RADIUS, 0, L - KW)
    cls = jnp.where(qb == 0, 0, jnp.where(qb == nqb - 1, 2, 1))
    if d == 1:
        qsl = pl.ds(pl.multiple_of(qs, Q_BLOCK), Q_BLOCK)
        ksl = pl.ds(pl.multiple_of(ks, RADIUS), KW)
    elif d == SPLIT:
        qsl = pl.ds(pl.multiple_of(r * L + qs, Q_BLOCK), Q_BLOCK)
        ksl = pl.ds(pl.multiple_of(r * L + ks, RADIUS), KW)
    else:
        sub = d // SPLIT
        base = (r % SPLIT) * (S // SPLIT) + r // SPLIT
        qsl = pl.ds(base + qs * sub, Q_BLOCK, stride=sub)
        ksl = pl.ds(base + ks * sub, KW, stride=sub)
    return qsl, ksl, cls


def _for_groups(geom, S, group, fn):
    for di, (d, L, KW, nqb) in enumerate(geom):
        n = group[di]
        assert (d * nqb) % n == 0

        def step(it, carry, di=di, d=d, L=L, KW=KW, nqb=nqb, n=n):
            slices = []
            for g in range(n):
                i = it * n + g
                slices.append(_block_slices(d, L, KW, nqb, i // nqb, i % nqb, S))
            fn(di, KW, slices)
            return carry
        lax.fori_loop(0, d * nqb // n, step, 0)


def _attn_fwd(proj, B, S):
    T = B * S
    geom = _attn_geometry(S)
    n_pairs = ATTN_WIDTH // LANES

    def body(q_ref, k_ref, v_ref, a_ref, lse_ref, bias_scr, q4, k4, v4, *per_dilation):
        o_scr, m_scr, l_scr = per_dilation[0:3], per_dilation[3:6], per_dilation[6:9]
        lo, hm = _head_masks()
        pair = pl.program_id(0)

        @pl.when(pl.program_id(1) == 0)
        def _():
            _init_bias(bias_scr, geom, pair)
        for src, dst in ((q_ref, q4), (k_ref, k4), (v_ref, v4)):
            _to_by4(src, dst, S)

        def group(di, KW, all_slices):
            run = 8
            for first in range(0, len(all_slices), run):
                some(di, KW, all_slices[first:first + run])

        def some(di, KW, slices):
            chains = [(g, j) for g in range(len(slices)) for j in (0, 1)]
            q_src, k_src, v_src = (q_ref, k_ref, v_ref) if di == 0 else (q4, k4, v4)
            q = [q_src[qsl, :] for qsl, _, _ in slices]
            kw = [k_src[ksl, :].astype(BF16) for _, ksl, _ in slices]
            vw = [v_src[ksl, :].astype(BF16) for _, ksl, _ in slices]
            s = {(g, j): _nt((q[g] * (hm[j] * 0.125)).astype(BF16), kw[g])
                 + bias_scr[di * 6 + slices[g][2] * 2 + j, :, pl.ds(0, KW)] for g, j in chains}
            m = {c: jnp.max(s[c], axis=1, keepdims=True) for c in chains}
            p = {c: jnp.exp(s[c] - m[c]) for c in chains}
            l = {c: jnp.sum(p[c], axis=1, keepdims=True) for c in chains}
            o = {(g, j): _nn(p[(g, j)].astype(BF16), vw[g]) for g, j in chains}
            for g, (qsl, _, _) in enumerate(slices):
                o_scr[di][qsl, :] = jnp.where(lo, o[(g, 0)], o[(g, 1)])
                m_scr[di][qsl, :] = jnp.where(lo, m[(g, 0)], m[(g, 1)])
                l_scr[di][qsl, :] = jnp.where(lo, l[(g, 0)], l[(g, 1)])

        _for_groups(geom, S, (16, 16, 16), group)

        for i in range(S // COPY_ROWS):
            natural, by4 = _by4_rows(S, i)
            rows = [natural, by4, by4]
            ms = [m_scr[di][rows[di], :] for di in range(3)]
            mx = jnp.maximum(jnp.maximum(ms[0], ms[1]), ms[2])
            num = 0.0
            den = 0.0
            for di in range(3):
                w = jnp.exp(ms[di] - mx)
                num = num + w * o_scr[di][rows[di], :]
                den = den + w * l_scr[di][rows[di], :]
            a_ref[natural, :] = num / den
            lse_ref[natural, :] = mx + jnp.log(den)

    blk = lambda off: pl.BlockSpec((S, LANES), lambda h, b, off=off: (b, off + h))
    out_blk = pl.BlockSpec((S, LANES), lambda h, b: (b, h))
    return pl.pallas_call(
        body, grid=(n_pairs, B),
        in_specs=[blk(0), blk(n_pairs), blk(2 * n_pairs)],
        out_specs=[out_blk, out_blk],
        out_shape=[jax.ShapeDtypeStruct((T, ATTN_WIDTH), F32)] * 2,
        scratch_shapes=[pltpu.VMEM((18, Q_BLOCK, 2 * Q_BLOCK), F32)] + [pltpu.VMEM((S, LANES), F32)] * 12,
        compiler_params=_params(("arbitrary", "arbitrary")), name="attn_fwd")(proj, proj, proj)


def _attn_bwd(proj, a, lse, da, B, S, after=()):
    T = B * S
    geom = _attn_geometry(S)
    n_pairs = ATTN_WIDTH // LANES

    def body(q_ref, k_ref, v_ref, a_ref, lse_ref, do_ref, *rest):
        dq_ref, dk_ref, dv_ref, bias_scr = rest[len(after):len(after) + 4]
        scr = rest[len(after) + 4:]
        acc = (scr[0:3], scr[3:6])
        natural_in = (q_ref, k_ref, v_ref, a_ref, lse_ref, do_ref)
        by4_in = scr[6:12]
        _, hm = _head_masks()
        pair = pl.program_id(0)

        @pl.when(pl.program_id(1) == 0)
        def _():
            _init_bias(bias_scr, geom, pair)
        for ref in scr[0:6]:
            ref[...] = jnp.zeros_like(ref)
        for src, dst in zip(natural_in, by4_in):
            _to_by4(src, dst, S)

        def group(di, KW, all_slices):
            run = (4, 4, 8)[di]
            for first in range(0, len(all_slices), run):
                some(di, KW, all_slices[first:first + run])

        def some(di, KW, slices):
            n = len(slices)
            chains = [(g, j) for g in range(n) for j in (0, 1)]
            q_src, k_src, v_src, a_src, lse_src, do_src = natural_in if di == 0 else by4_in
            dq_scr, dk_scr, dv_scr = acc[0 if di == 0 else 1]
            q = [q_src[qsl, :] for qsl, _, _ in slices]
            do = [do_src[qsl, :] for qsl, _, _ in slices]
            doa = [do[g] * a_src[slices[g][0], :] for g in range(n)]
            lse_q = [lse_src[qsl, :] for qsl, _, _ in slices]
            kw = [k_src[ksl, :].astype(BF16) for _, ksl, _ in slices]
            vw = [v_src[ksl, :].astype(BF16) for _, ksl, _ in slices]
            qj = {(g, j): (q[g] * (hm[j] * 0.125)).astype(BF16) for g, j in chains}
            doj = {(g, j): (do[g] * hm[j]).astype(BF16) for g, j in chains}
            s = {(g, j): _nt(qj[(g, j)], kw[g])
                 + bias_scr[di * 6 + slices[g][2] * 2 + j, :, pl.ds(0, KW)] for g, j in chains}
            dp = {(g, j): _nt(doj[(g, j)], vw[g]) for g, j in chains}
            dsum = {(g, j): jnp.sum(doa[g] * hm[j], axis=1, keepdims=True) for g, j in chains}
            p = {(g, j): jnp.exp(s[(g, j)] - lse_q[g][:, HEAD_DIM * j:HEAD_DIM * j + 1]) for g, j in chains}
            ds = {c: (p[c] * (dp[c] - dsum[c])).astype(BF16) for c in chains}
            pb = {c: p[c].astype(BF16) for c in chains}
            dq = [_nn(ds[(g, 0)], kw[g]) * (hm[0] * 0.125) + _nn(ds[(g, 1)], kw[g]) * (hm[1] * 0.125)
                  for g in range(n)]
            both = lambda t, g: jnp.concatenate([t[(g, 0)], t[(g, 1)]], axis=0)
            dkw = [_tn(both(ds, g), both(qj, g)) for g in range(n)]
            dvw = [_tn(both(pb, g), both(doj, g)) for g in range(n)]
            for g, (qsl, ksl, _) in enumerate(slices):
                dq_scr[qsl, :] = dq_scr[qsl, :] + dq[g]
                dk_scr[ksl, :] = dk_scr[ksl, :] + dkw[g]
                dv_scr[ksl, :] = dv_scr[ksl, :] + dvw[g]

        _for_groups(geom, S, (16, 16, 16), group)

        for i in range(S // COPY_ROWS):
            natural, by4 = _by4_rows(S, i)
            for nat, split in zip(*acc):
                nat[natural, :] = nat[natural, :] + split[by4, :]
        for out, nat in zip((dq_ref, dk_ref, dv_ref), acc[0]):
            out[...] = nat[...].astype(BF16)

    blk = lambda off: pl.BlockSpec((S, LANES), lambda h, b, off=off: (b, off + h))
    return pl.pallas_call(
        body, grid=(n_pairs, B),
        in_specs=[blk(0), blk(n_pairs), blk(2 * n_pairs), blk(0), blk(0), blk(0)] + _after(after),
        out_specs=[blk(0), blk(0), blk(0)],
        out_shape=[jax.ShapeDtypeStruct((T, ATTN_WIDTH), BF16)] * 3,
        scratch_shapes=[pltpu.VMEM((18, Q_BLOCK, 2 * Q_BLOCK), F32)] + [pltpu.VMEM((S, LANES), F32)] * 12,
        compiler_params=_params(("arbitrary", "arbitrary")), name="attn_bwd")(proj, proj, proj, a, lse, da, *after)


def _mid(x2d, t2d, a, proj, kv, w_s, w_sT, b_tab, g_v, w_out, g_final, B, S):
    T = B * S
    tm = 512
    nt = S // tm
    halves = 2
    hrows = tm // halves

    def body(x_ref, t_ref, a_ref, za_ref, ub_ref, vb_ref, zb_ref, qm_ref, zm_ref, kv_ref,
              ws_ref, wsT_ref, btab_ref, gv_ref, wout_ref, gf_ref,
              dx2_ref, da_ref, drest_ref, loss_ref, dwout_bf_ref, dws_ref, dbs_ref, dgv_ref, dgf_ref, dkv_ref,
              dbtab_scr, dwout_ref):
        b = pl.program_id(0)
        t = pl.program_id(1)
        first = jnp.logical_and(b == 0, t == 0)
        last = jnp.logical_and(b == B - 1, t == nt - 1)
        _, hm = _head_masks()
        lane_g = lax.broadcasted_iota(jnp.int32, (1, SGU_WIDTH), 1) // HEAD_DIM
        gm = [(lane_g == g).astype(F32) for g in range(N_SGU_GROUPS)]
        H = range(halves)
        rows = [pl.ds(h * hrows, hrows) for h in H]
        ld = lambda ref: [ref[r, :] for r in rows]
        cat = lambda parts, axis: jnp.concatenate(parts, axis=axis)
        chunks = [slice(ci * SGU_CHUNK, (ci + 1) * SGU_CHUNK) for ci in range(hrows // SGU_CHUNK)]
        pairs = [slice(pr * LANES, (pr + 1) * LANES) for pr in range(2)]
        heads = [(pr, j) for pr in range(2) for j in (0, 1)]

        @pl.when(first)
        def _():
            loss_ref[...] = jnp.zeros_like(loss_ref)
            dwout_ref[...] = jnp.zeros_like(dwout_ref)
            dws_ref[...] = jnp.zeros_like(dws_ref)
            dbs_ref[...] = jnp.zeros_like(dbs_ref)
            dgv_ref[...] = jnp.zeros_like(dgv_ref)
            dgf_ref[...] = jnp.zeros_like(dgf_ref)
            dbtab_scr[...] = jnp.zeros_like(dbtab_scr)

        @pl.when(t == 0)
        def _():
            dkv_ref[...] = jnp.zeros_like(dkv_ref)

        a_val = ld(a_ref)
        sil_a = [_silu_parts(z) for z in ld(za_ref)]
        gated_a = [s[0] * a for s, a in zip(sil_a, a_val)]
        u = [_gelu_parts(z) for z in ld(ub_ref)]
        vv = [_gelu_parts(z) for z in ld(vb_ref)]
        vnorm = [_rms(v[0]) for v in vv]
        gv = gv_ref[...]
        vn = [(n[1] * gv).astype(BF16) for n in vnorm]
        w_cat = cat([ws_ref[g].astype(BF16) for g in range(N_SGU_GROUPS)], 1)
        wT_cat = cat([wsT_ref[g].astype(BF16) for g in range(N_SGU_GROUPS)], 1)
        gmb = [m.astype(BF16) for m in gm]
        by_group = lambda chunk: cat([chunk * gmb[g] for g in range(N_SGU_GROUPS)], 0)
        btab = btab_ref[...]
        mixed = [cat([btab + _nn(w_cat, by_group(vn[h][c, :])) for c in chunks], 0) for h in H]
        sg = [u[h][0] * mixed[h] for h in H]
        sil_b = [_silu_parts(z) for z in ld(zb_ref)]
        gated_b = [sil_b[h][0] * sg[h] for h in H]

        kvv = kv_ref[...].astype(BF16)
        kp = [kvv[:, p] for p in pairs]
        vp = [kvv[:, MEM_WIDTH + pr * LANES:MEM_WIDTH + (pr + 1) * LANES] for pr in range(2)]
        qm = ld(qm_ref)
        qj = {(h, pr, j): (qm[h][:, pairs[pr]] * (hm[j] * 0.125)).astype(BF16) for h in H for pr, j in heads}
        sc = {k: _nt(qj[k], kp[k[1]]) for k in qj}
        ex = {k: jnp.exp(sc[k] - jnp.max(sc[k], axis=1, keepdims=True)) for k in qj}
        prob = {k: ex[k] * (1.0 / jnp.sum(ex[k], axis=1, keepdims=True)) for k in qj}
        probb = {k: prob[k].astype(BF16) for k in qj}
        mo = [cat([sum(_nn(probb[(h, pr, j)], vp[pr]) * hm[j] for j in (0, 1)) for pr in range(2)], 1) for h in H]
        sil_m = [_silu_parts(z) for z in ld(zm_ref)]
        gated_m = [sil_m[h][0] * mo[h] for h in H]

        gated = [cat([gated_a[h], gated_b[h], gated_m[h]], 1).astype(BF16) for h in H]
        wout = wout_ref[...]
        x_in = ld(x_ref)
        x2 = [x_in[h] + _nn(gated[h], wout) for h in H]
        fin = [_rms(z) for z in x2]
        gf = gf_ref[...]
        tgt = ld(t_ref)
        err = [fin[h][1] * gf - tgt[h] for h in H]
        loss_ref[...] += sum(jnp.sum(e * e) for e in err) * (0.5 / D_MODEL)

        dy = [e * (1.0 / D_MODEL) for e in err]
        dgf_ref[...] += sum(jnp.sum(dy[h] * fin[h][1], axis=0, keepdims=True) for h in H)
        gdy = [d * gf for d in dy]
        dx2 = [fin[h][0] * (gdy[h] - fin[h][1] * jnp.mean(gdy[h] * fin[h][1], axis=1, keepdims=True)) for h in H]
        for h in H:
            dx2_ref[rows[h], :] = dx2[h]
        dx2b = [d.astype(BF16) for d in dx2]
        dgated = [_nt(d, wout) for d in dx2b]
        dwout_ref[...] += _tn(cat(gated, 0), cat(dx2b, 0))
        dga = [d[:, 0:ATTN_WIDTH] for d in dgated]
        dgb = [d[:, ATTN_WIDTH:ATTN_WIDTH + SGU_WIDTH] for d in dgated]
        dgm = [d[:, ATTN_WIDTH + SGU_WIDTH:] for d in dgated]

        for h in H:
            da_ref[rows[h], :] = dga[h] * sil_a[h][0]
        dza = [dga[h] * a_val[h] * sil_a[h][1] for h in H]

        dsg = [dgb[h] * sil_b[h][0] for h in H]
        dzb = [dgb[h] * sg[h] * sil_b[h][1] for h in H]
        dub = [dsg[h] * mixed[h] * u[h][1] for h in H]
        dmixed = [dsg[h] * u[h][0] for h in H]
        dmixed_b = [d.astype(BF16) for d in dmixed]
        dvn = [cat([_nn(wT_cat, by_group(dmixed_b[h][c, :])) for c in chunks], 0) for h in H]
        for g in range(N_SGU_GROUPS):
            dws_ref[g] += sum(_nt((dmixed[h][c, :] * gm[g]).astype(BF16), vn[h][c, :]) for h in H for c in chunks)
        dbtab_scr[...] += sum(dmixed[h][c, :] for h in H for c in chunks)
        dgv_ref[...] += sum(jnp.sum(dvn[h] * vnorm[h][1], axis=0, keepdims=True) for h in H)
        tv = [d * gv for d in dvn]
        dvv = [vnorm[h][0] * (tv[h] - vnorm[h][1] * jnp.mean(tv[h] * vnorm[h][1], axis=1, keepdims=True)) for h in H]
        dvb = [dvv[h] * vv[h][1] for h in H]

        dmo = [dgm[h] * sil_m[h][0] for h in H]
        dzm = [dgm[h] * mo[h] * sil_m[h][1] for h in H]
        dmoj = {(h, pr, j): (dmo[h][:, pairs[pr]] * hm[j]).astype(BF16) for h in H for pr, j in heads}
        dp = {k: _nt(dmoj[k], vp[k[1]]) for k in qj}
        ds = {k: (prob[k] * (dp[k] - jnp.sum(dp[k] * prob[k], axis=1, keepdims=True))).astype(BF16) for k in qj}
        dqm = [cat([sum(_nn(ds[(h, pr, j)], kp[pr]) * (hm[j] * 0.125) for j in (0, 1)) for pr in range(2)], 1)
               for h in H]
        every = lambda tbl, pr: cat([tbl[(h, pr, j)] for h in H for j in (0, 1)], 0)
        dk = [_tn(every(ds, pr), every(qj, pr)) for pr in range(2)]
        dv = [_tn(every(probb, pr), every(dmoj, pr)) for pr in range(2)]
        dkv_ref[...] += cat(dk + dv, 1)

        for h in H:
            drest_ref[rows[h], :] = cat([dza[h], dub[h], dvb[h], dzb[h], dqm[h], dzm[h]], 1).astype(BF16)

        @pl.when(last)
        def _():
            lane = lax.broadcasted_iota(jnp.int32, (1, LANES), 1)
            dbt = dbtab_scr[...]
            out = jnp.zeros((SGU_CHUNK, LANES), F32)
            for g in range(N_SGU_GROUPS):
                out = out + jnp.where(lane == g, jnp.sum(dbt * gm[g], axis=1, keepdims=True), 0.0)
            dbs_ref[...] = out
            for r0 in range(0, D_MODEL, SGU_CHUNK):
                k, row = divmod(r0, D_MODEL // N_CHIPS)
                dwout_bf_ref[k, row:row + SGU_CHUNK, :] = dwout_ref[r0:r0 + SGU_CHUNK, :].astype(BF16)

    tile = lambda w, cb: pl.BlockSpec((tm, w), lambda b, t, cb=cb: (b * nt + t, cb))
    const = lambda shape: pl.BlockSpec(shape, lambda b, t, n=len(shape): (0,) * n)
    return pl.pallas_call(
        body, grid=(B, nt),
        in_specs=[tile(D_MODEL, 0), tile(D_MODEL, 0), tile(ATTN_WIDTH, 0),
                  tile(ATTN_WIDTH, 3),
                  tile(SGU_WIDTH, 8), tile(SGU_WIDTH, 9), tile(SGU_WIDTH, 10),
                  tile(MEM_WIDTH, 11), tile(MEM_WIDTH, 12),
                  pl.BlockSpec((N_MEM, 2 * MEM_WIDTH), lambda b, t: (b, 0)),
                  const((N_SGU_GROUPS, SGU_CHUNK, SGU_CHUNK)), const((N_SGU_GROUPS, SGU_CHUNK, SGU_CHUNK)),
                  const((SGU_CHUNK, SGU_WIDTH)), const((1, SGU_WIDTH)),
                  const((D_MODEL, D_MODEL)), const((1, D_MODEL))],
        out_specs=[tile(D_MODEL, 0), tile(ATTN_WIDTH, 0), tile(REST_COLS, 0),
                   const((8, LANES)), const((N_CHIPS, D_MODEL // N_CHIPS, D_MODEL)),
                   const((N_SGU_GROUPS, SGU_CHUNK, SGU_CHUNK)), const((SGU_CHUNK, LANES)),
                   const((1, SGU_WIDTH)), const((1, D_MODEL)),
                   pl.BlockSpec((N_MEM, 2 * MEM_WIDTH), lambda b, t: (b, 0))],
        out_shape=[jax.ShapeDtypeStruct((T, D_MODEL), F32), jax.ShapeDtypeStruct((T, ATTN_WIDTH), F32),
                   jax.ShapeDtypeStruct((T, REST_COLS), BF16),
                   jax.ShapeDtypeStruct((8, LANES), F32),
                   jax.ShapeDtypeStruct((N_CHIPS, D_MODEL // N_CHIPS, D_MODEL), BF16),
                   jax.ShapeDtypeStruct((N_SGU_GROUPS, SGU_CHUNK, SGU_CHUNK), F32),
                   jax.ShapeDtypeStruct((SGU_CHUNK, LANES), F32),
                   jax.ShapeDtypeStruct((1, SGU_WIDTH), F32), jax.ShapeDtypeStruct((1, D_MODEL), F32),
                   jax.ShapeDtypeStruct((B * N_MEM, 2 * MEM_WIDTH), F32)],
        scratch_shapes=[pltpu.VMEM((SGU_CHUNK, SGU_WIDTH), F32), pltpu.VMEM((D_MODEL, D_MODEL), F32)],
        compiler_params=_params(("arbitrary", "arbitrary"), vmem=VMEM_LIMIT + 2 * 1024 * 1024), name="mid")(
            x2d, t2d, a, proj, proj, proj, proj, proj, proj, kv, w_s, w_sT, b_tab, g_v, w_out, g_final)


def _inproj_bwd_dx(dq, dk, dv, drest, x2d, dx2, g_norm, w_in_t, after=()):
    T = x2d.shape[0]
    tm = 512
    W = ATTN_WIDTH

    def body(dq_ref, dk_ref, dv_ref, dr_ref, x_ref, dx2_ref, g_ref, w_ref, *rest):
        gx_ref, dg_ref = rest[-2:]

        @pl.when(pl.program_id(0) == 0)
        def _():
            dg_ref[...] = jnp.zeros_like(dg_ref)

        halves = [pl.ds(h * (tm // 2), tm // 2) for h in (0, 1)]
        dh = [(_nn(dq_ref[r, :], w_ref[0:W, :]) + _nn(dk_ref[r, :], w_ref[W:2 * W, :])
               + _nn(dv_ref[r, :], w_ref[2 * W:3 * W, :]) + _nn(dr_ref[r, :], w_ref[QKV_COLS:IN_COLS, :]))
              for r in halves]
        nrm = [_rms(x_ref[r, :]) for r in halves]
        dg_ref[...] += sum(jnp.sum(d * n[1], axis=0, keepdims=True) for d, n in zip(dh, nrm))
        g = g_ref[...]
        for r, d, (rstd, xh) in zip(halves, dh, nrm):
            th = d * g
            gx_ref[r, :] = rstd * (th - xh * jnp.mean(th * xh, axis=1, keepdims=True)) + dx2_ref[r, :]

    tile = lambda w: pl.BlockSpec((tm, w), lambda i: (i, 0))
    return pl.pallas_call(
        body, grid=(T // tm,),
        in_specs=[tile(W), tile(W), tile(W), tile(REST_COLS), tile(D_MODEL), tile(D_MODEL),
                  pl.BlockSpec((1, D_MODEL), lambda i: (0, 0)),
                  pl.BlockSpec((IN_COLS, D_MODEL), lambda i: (0, 0))] + _after(after),
        out_specs=[tile(D_MODEL), pl.BlockSpec((1, D_MODEL), lambda i: (0, 0))],
        out_shape=[jax.ShapeDtypeStruct((T, D_MODEL), F32), jax.ShapeDtypeStruct((1, D_MODEL), F32)],
        compiler_params=_params(("arbitrary",)), name="inproj_bwd_dx")(
            dq, dk, dv, drest, x2d, dx2, g_norm, w_in_t, *after)


def _inproj_bwd_dw(dq, dk, dv, drest, x2d, g_norm, reduce=False):
    T = x2d.shape[0]
    tm = 512
    nt = T // tm
    W = ATTN_WIDTH
    shard = IN_COLS // N_CHIPS
    half = shard // 2
    row_block = 32

    def body(dq_ref, dk_ref, dv_ref, dr_ref, x_ref, g_ref, *rest):
        if reduce:
            sends_out, own_out, acc, ras, narrow, sends, own, s_sem, r_sem, put_sem = rest
            x, y, c, chip, peers, peer_chip = _place()
            sib = (x, y, 1 - c)

            def part(k, cc, r0):
                return acc.at[pl.ds(pl.multiple_of(k * shard + cc * half + r0, 8), row_block), :]

            def swap_win(k):
                return _remote(narrow.at[k], ras.at[k], s_sem.at[k], r_sem.at[k], sib)

            def put_send(m):
                return pltpu.make_async_copy(sends.at[m], sends_out.at[m], put_sem.at[m])

            put_own = pltpu.make_async_copy(own, own_out, put_sem.at[N_CHIPS - 1])
        else:
            acc = rest[0]

        @pl.when(pl.program_id(0) == 0)
        def _():
            acc[...] = jnp.zeros_like(acc)

        _, xh = _rms(x_ref[...])
        h = (xh * g_ref[...]).astype(BF16)
        acc[0:W, :] += _tn(dq_ref[...], h)
        acc[W:2 * W, :] += _tn(dk_ref[...], h)
        acc[2 * W:3 * W, :] += _tn(dv_ref[...], h)
        acc[QKV_COLS:IN_COLS, :] += _tn(dr_ref[...], h)

        if reduce:
            @pl.when(pl.program_id(0) == nt - 1)
            def _():
                for k in range(N_CHIPS):
                    def to_bf16(i, carry, k=k):
                        r0 = pl.multiple_of(i * row_block, row_block)
                        narrow[k, pl.ds(r0, row_block), :] = part(k, 1 - c, r0)[...].astype(BF16)
                        return carry
                    lax.fori_loop(0, half // row_block, to_bf16, 0)
                    swap_win(k).start()

                def chip_sum(k, r0):
                    return part(k, c, r0)[...] + ras[k, pl.ds(r0, row_block), :].astype(F32)

                for k in range(N_CHIPS):
                    swap_win(k).wait_recv()

                    @pl.when(chip == k)
                    def _(k=k):
                        def mine(i, carry):
                            r0 = pl.multiple_of(i * row_block, row_block)
                            own[pl.ds(r0, row_block), :] = chip_sum(k, r0)
                            return carry
                        lax.fori_loop(0, half // row_block, mine, 0)
                        put_own.start()

                    @pl.when(chip != k)
                    def _(k=k):
                        def other(i, carry):
                            r0 = pl.multiple_of(i * row_block, row_block)
                            sends[(k ^ chip) - 1, pl.ds(r0, row_block), :] = chip_sum(k, r0).astype(BF16)
                            return carry
                        lax.fori_loop(0, half // row_block, other, 0)
                        put_send((k ^ chip) - 1).start()
                for k in range(N_CHIPS):
                    swap_win(k).wait_send()
                for m in range(N_CHIPS - 1):
                    put_send(m).wait()
                put_own.wait()

    tile = lambda w: pl.BlockSpec((tm, w), lambda i: (i, 0))
    vmem = pl.BlockSpec(memory_space=pltpu.VMEM)
    in_specs = [tile(W), tile(W), tile(W), tile(REST_COLS), tile(D_MODEL), pl.BlockSpec((1, D_MODEL), lambda i: (0, 0))]
    if not reduce:
        return pl.pallas_call(
            body, grid=(nt,), in_specs=in_specs,
            out_specs=pl.BlockSpec((IN_COLS, D_MODEL), lambda i: (0, 0)),
            out_shape=jax.ShapeDtypeStruct((IN_COLS, D_MODEL), F32),
            compiler_params=_params(("arbitrary",)), name="inproj_bwd_dw")(dq, dk, dv, drest, x2d, g_norm)
    quarters = pltpu.VMEM((N_CHIPS, half, D_MODEL), BF16)
    far = pl.BlockSpec(memory_space=pl.ANY)
    return pl.pallas_call(
        body, grid=(nt,), in_specs=in_specs, out_specs=[far] * 2,
        out_shape=[jax.ShapeDtypeStruct((N_CHIPS - 1, half, D_MODEL), BF16),
                   jax.ShapeDtypeStruct((half, D_MODEL), F32)],
        scratch_shapes=[pltpu.VMEM((IN_COLS, D_MODEL), F32), quarters, quarters,
                        pltpu.VMEM((N_CHIPS - 1, half, D_MODEL), BF16), pltpu.VMEM((half, D_MODEL), F32),
                        pltpu.SemaphoreType.DMA((N_CHIPS,)), pltpu.SemaphoreType.DMA((N_CHIPS,)),
                        pltpu.SemaphoreType.DMA((N_CHIPS,))],
        compiler_params=_params(("arbitrary",)), name="inproj_bwd_dw_reduce")(dq, dk, dv, drest, x2d, g_norm)


def _adamw_update(w, g, m, v):
    nm = ADAM_B1 * m + (1.0 - ADAM_B1) * g
    nv = ADAM_B2 * v + (1.0 - ADAM_B2) * (g * g)
    m_hat = nm / (1.0 - ADAM_B1 ** ADAM_STEP)
    v_hat = nv / (1.0 - ADAM_B2 ** ADAM_STEP)
    return -ADAM_LR * (m_hat / (jnp.sqrt(v_hat) + ADAM_EPS) + ADAM_WD * w), nm, nv


def _adamw_all(g_packed, ws, ms, vs, large):
    n, nl = len(ws), len(large)
    row_block = 8
    chunk_bytes = 512 * 1024

    def chunk_rows(w):
        R, C = w.shape
        return max(r for r in range(8, R + 1, 8) if R % r == 0 and r * C * 4 <= chunk_bytes)

    jobs = [(b, r0, chunk_rows(four[0])) for b, four in enumerate(large)
            for r0 in range(0, four[0].shape[0], chunk_rows(four[0]))]

    def body(*refs):
        g_ref = refs[0]
        w_refs, m_refs, v_refs = refs[1:1 + n], refs[1 + n:1 + 2 * n], refs[1 + 2 * n:1 + 3 * n]
        far_in = refs[1 + 3 * n:1 + 3 * n + 4 * nl]
        outs = refs[1 + 3 * n + 4 * nl:1 + 7 * n + 4 * nl]
        far_out = refs[1 + 7 * n + 4 * nl:1 + 7 * n + 8 * nl]
        loss_ref = refs[1 + 7 * n + 8 * nl]
        scr = refs[2 + 7 * n + 8 * nl:]
        in_scr, out_scr, in_sem, out_sem = scr[:4 * nl], scr[4 * nl:7 * nl], scr[7 * nl], scr[7 * nl + 1]

        def read(j, k):
            b, r0, rows = jobs[j]
            blk = pl.ds(r0, rows)
            return pltpu.make_async_copy(far_in[4 * b + k].at[blk, :], in_scr[4 * b + k].at[blk, :],
                                         in_sem.at[4 * j + k])

        def write(j, k):
            b, r0, rows = jobs[j]
            blk = pl.ds(r0, rows)
            src = in_scr[4 * b + 1] if k == 0 else out_scr[3 * b + k - 1]
            return pltpu.make_async_copy(src.at[blk, :], far_out[4 * b + k].at[blk, :], out_sem.at[4 * j + k])

        for j in range(len(jobs)):
            for k in range(4):
                read(j, k).start()
        off = 0
        for i, (_, used, padded) in enumerate(_SMALL_PARTS[:n]):
            g = g_ref[off:off + used, :]
            delta, nm, nv = _adamw_update(w_refs[i][...], g, m_refs[i][...], v_refs[i][...])
            outs[4 * i][...], outs[4 * i + 1][...], outs[4 * i + 2][...], outs[4 * i + 3][...] = g, delta, nm, nv
            off += padded
        loss_ref[...] = g_ref[_LOSS_ROW:_LOSS_ROW + 1, 0:1]
        for j, (b, r0, rows) in enumerate(jobs):
            for k in range(4):
                read(j, k).wait()

            def update(i, carry, b=b, r0=r0):
                blk = pl.ds(pl.multiple_of(r0 + i * row_block, row_block), row_block)
                w, g, m, v = [in_scr[4 * b + k][blk, :] for k in range(4)]
                for k, val in enumerate(_adamw_update(w, g, m, v)):
                    out_scr[3 * b + k][blk, :] = val
                return carry
            lax.fori_loop(0, rows // row_block, update, 0)
            for k in range(4):
                write(j, k).start()
        for j in range(len(jobs)):
            for k in range(4):
                write(j, k).wait()

    vmem = pl.BlockSpec(memory_space=pltpu.VMEM)
    far = pl.BlockSpec(memory_space=pl.ANY)
    flat = [a for four in large for a in four]
    outs = pl.pallas_call(
        body, in_specs=[vmem] * (1 + 3 * n) + [far] * (4 * nl),
        out_specs=[vmem] * (4 * n) + [far] * (4 * nl) + [vmem],
        out_shape=[jax.ShapeDtypeStruct(w.shape, F32) for w in ws for _ in range(4)]
        + [jax.ShapeDtypeStruct(four[0].shape, F32) for four in large for _ in range(4)]
        + [jax.ShapeDtypeStruct((1, 1), F32)],
        scratch_shapes=[pltpu.VMEM(a.shape, F32) for a in flat]
        + [pltpu.VMEM(four[0].shape, F32) for four in large for _ in range(3)]
        + [pltpu.SemaphoreType.DMA((4 * len(jobs),)), pltpu.SemaphoreType.DMA((4 * len(jobs),))],
        compiler_params=_params(), name="adamw_all")(g_packed, *ws, *ms, *vs, *flat)
    return ([outs[4 * i:4 * i + 4] for i in range(n)], [outs[4 * (n + i):4 * (n + i) + 4] for i in range(nl)],
            outs[4 * (n + nl)])


def _place():
    x, y, c = lax.axis_index("x"), lax.axis_index("y"), lax.axis_index("c")
    chip = 2 * x + y
    peers = [(x, 1 - y), (1 - x, y), (1 - x, 1 - y)]
    peer_chip = [2 * px + py for px, py in peers]
    return x, y, c, chip, peers, peer_chip


def _remote(src, dst, send_sem, recv_sem, dev):
    return pltpu.make_async_remote_copy(src_ref=src, dst_ref=dst, send_sem=send_sem, recv_sem=recv_sem,
                                        device_id=dev, device_id_type=MESH)


def _ag_weights(weights, late=()):
    nw, nl = len(weights), len(late)
    pieces = 2

    def body(*refs):
        srcs, late_srcs = refs[:nw], refs[nw:nw + nl]
        outs, late_bf, late_land = (refs[nw + nl:2 * nw + nl], refs[2 * nw + nl:2 * nw + 2 * nl],
                                    refs[2 * nw + 2 * nl:2 * nw + 3 * nl])
        scr = refs[2 * nw + 3 * nl:]
        wide, narrow = scr[:nw], scr[nw:2 * nw]
        late_scr = scr[2 * nw:2 * nw + nl]
        in_sem, put_sem, late_sem, s_ici, r_ici, s_d2d, r_d2d = scr[2 * nw + nl:]
        x, y, c = lax.axis_index("x"), lax.axis_index("y"), lax.axis_index("c")
        chip = 2 * x + y
        sib = (x, y, 1 - c)
        first = ((x + 1 - c) % 2, (y + c) % 2)
        second = ((x + c) % 2, (y + 1 - c) % 2)
        first_chip, second_chip = 2 * first[0] + first[1], 2 * second[0] + second[1]
        diag_chip = 3 - chip

        parts = [(w, out, pc) for w, out in enumerate(outs) for pc in range(pieces)]

        def rows_of(out, cc, pc):
            rows = out.shape[1] // 2 // pieces
            return pl.ds(pl.multiple_of((cc * pieces + pc) * rows, 16), rows)

        def piece(out, k, cc, pc):
            return out.at[k, rows_of(out, cc, pc), :]

        def ici(w, slot, out, k, dev, pc, src=None):
            blk, sem = piece(out, k, c, pc), (nw * slot + w) * pieces + pc
            return _remote(blk if src is None else src, blk, s_ici.at[sem], r_ici.at[sem], (dev[0], dev[1], c))

        def d2d(w, slot, out, k, cc, pc):
            blk, sem = piece(out, k, cc, pc), (nw * slot + w) * pieces + pc
            return _remote(blk, blk, s_d2d.at[sem], r_d2d.at[sem], sib)

        def read(w, cc, pc):
            rows = rows_of(outs[w], cc, pc)
            return pltpu.make_async_copy(srcs[w].at[rows, :], wide[w].at[rows, :],
                                         in_sem.at[(w * 2 + cc) * pieces + pc])

        order = [(w, cc, pc) for cc in (0, 1) for w in range(nw) for pc in range(pieces)]
        for w, cc, pc in order:
            read(w, (c + cc) % 2, pc).start()
        sent = []
        for w, cc, pc in order[:nw * pieces]:
            rows = rows_of(outs[w], c, pc)
            read(w, c, pc).wait()
            narrow[w][rows, :] = wide[w][rows, :].astype(BF16)
            for slot, dev in enumerate((first, second)):
                sent.append(ici(w, slot, outs[w], chip, dev, pc, src=narrow[w].at[rows, :]))
                sent[-1].start()
        late_puts = []
        for i, (src, scr_bf, bf, land) in enumerate(zip(late_srcs, late_scr, late_bf, late_land)):
            scr_bf[...] = src[...].astype(BF16)
            for k, dst in enumerate([bf] + [land.at[s] for s in range(N_CHIPS)]):
                late_puts.append(pltpu.make_async_copy(scr_bf, dst, late_sem.at[i * (N_CHIPS + 1) + k]))
                late_puts[-1].start()
        for w, cc, pc in order[nw * pieces:]:
            rows = rows_of(outs[w], 1 - c, pc)
            read(w, 1 - c, pc).wait()
            narrow[w][rows, :] = wide[w][rows, :].astype(BF16)
        puts = [pltpu.make_async_copy(narrow[w], outs[w].at[chip], put_sem.at[w]) for w in range(nw)]
        for cp in puts:
            cp.start()
        for slot, k, dev in ((0, first_chip, first), (1, second_chip, second), (2, diag_chip, second)):
            for w, out, pc in parts:
                ici(w, slot, out, k, dev, pc).wait_recv()
                if slot == 0:
                    sent.append(ici(w, 2, out, k, second, pc))
                    sent[-1].start()
                sent.append(d2d(w, slot, out, k, c, pc))
                sent[-1].start()
        for slot, k in ((0, second_chip), (1, first_chip), (2, diag_chip)):
            for w, out, pc in parts:
                d2d(w, slot, out, k, 1 - c, pc).wait_recv()
        for cp in sent:
            cp.wait_send()
        for cp in puts + late_puts:
            cp.wait()

    vmem = pl.BlockSpec(memory_space=pltpu.VMEM)
    far = pl.BlockSpec(memory_space=pl.ANY)
    outs = pl.pallas_call(
        body,
        out_shape=[jax.ShapeDtypeStruct((N_CHIPS,) + w.shape, BF16) for w in weights]
        + [jax.ShapeDtypeStruct(w.shape, BF16) for w in late]
        + [jax.ShapeDtypeStruct((N_CHIPS,) + w.shape, BF16) for w in late],
        in_specs=[far] * nw + [vmem] * nl, out_specs=[far] * (nw + 2 * nl),
        scratch_shapes=[pltpu.VMEM(w.shape, F32) for w in weights] + [pltpu.VMEM(w.shape, BF16) for w in weights]
        + [pltpu.VMEM(w.shape, BF16) for w in late]
        + [pltpu.SemaphoreType.DMA((2 * nw * pieces,)), pltpu.SemaphoreType.DMA((nw,)),
           pltpu.SemaphoreType.DMA((nl * (N_CHIPS + 1),))]
        + [pltpu.SemaphoreType.DMA((3 * nw * pieces,))] * 4,
        compiler_params=pltpu.CompilerParams(vmem_limit_bytes=VMEM_LIMIT), name="ag_weights")(*weights, *late)
    return outs[:nw], outs[nw:nw + nl], outs[nw + nl:]


_HBM = pl.BlockSpec(memory_space=pltpu.HBM)
_SEM = pl.BlockSpec(memory_space=pltpu.SEMAPHORE)
_ANY = pl.BlockSpec(memory_space=pl.ANY)
_DATAFLOW = pltpu.SideEffectType.DATAFLOW_SIDE_EFFECTING


def _in_hbm(a):
    return pltpu.with_memory_space_constraint(a, pltpu.HBM)


_PEERS_OF = {"gather": 3, "scatter": 3, "direct": 7}


def _exchange_copies(mode, srcs, lands, send_sems, recv_sems):
    nw = len(srcs)
    x, y, c, chip, peers, peer_chip = _place()
    pairs = []
    if mode == "direct":
        targets = [((x, y), chip, 1)] + [(p, k, d) for p, k in zip(peers, peer_chip) for d in (0, 1)]
        for r, ((px, py), k, d) in enumerate(targets):
            for w in range(nw):
                sems = (send_sems.at[nw * r + w], recv_sems.at[nw * r + w], (px, py, (c + d) % 2))
                share = srcs[w].at[k if srcs[w].shape[0] > 1 else 0]
                pairs.append((_remote(share, lands[w].at[r], *sems),) * 2)
        return pairs
    gather = mode == "gather"
    for m, (px, py) in enumerate(peers):
        for w in range(nw):
            sems = (send_sems.at[nw * m + w], recv_sems.at[nw * m + w], (px, py, c))
            if gather:
                pairs.append((_remote(srcs[w], lands[w].at[chip], *sems),
                              _remote(srcs[w], lands[w].at[peer_chip[m]], *sems)))
            else:
                pairs.append((_remote(srcs[w].at[m], lands[w].at[m], *sems),) * 2)
    return pairs


def _exchange_start(mode, srcs, after, name, lands=None):
    nw = len(srcs)
    n_copies = _PEERS_OF[mode] * nw

    after = tuple(after)

    def body(*refs):
        send_sems, recv_sems = refs[2 * nw + len(after)], refs[2 * nw + len(after) + 1]
        for start, _ in _exchange_copies(mode, refs[:nw], refs[nw:2 * nw], send_sems, recv_sems):
            start.start()
        refs[-1][...] = jnp.zeros_like(refs[-1])

    if lands is None:
        shape = {"gather": lambda s: (N_CHIPS,) + s.shape, "scatter": lambda s: s.shape,
                 "direct": lambda s: (_PEERS_OF["direct"],) + s.shape[1:]}[mode]
        lands = [lax.empty(shape(s), s.dtype) for s in srcs]
    lands = [_in_hbm(l) for l in lands]
    return pl.pallas_call(
        body, name=name,
        out_shape=(pltpu.SemaphoreType.DMA((n_copies,)), pltpu.SemaphoreType.DMA((n_copies,)))
        + tuple(pltpu.HBM(s.shape, s.dtype) for s in srcs)
        + tuple(pltpu.HBM(l.shape, l.dtype) for l in lands)
        + (jax.ShapeDtypeStruct((8, LANES), F32),),
        in_specs=[_HBM] * (2 * nw) + [_ANY] * len(after),
        out_specs=(_SEM, _SEM) + (_HBM,) * (2 * nw) + (pl.BlockSpec(memory_space=pltpu.VMEM),),
        input_output_aliases={i: 2 + i for i in range(2 * nw)},
        compiler_params=pltpu.CompilerParams(has_side_effects=_DATAFLOW),
    )(*[_in_hbm(s) for s in srcs], *lands, *after)


def _exchange_wait(mode, started, after, name):
    nw = (len(started) - 3) // 2
    send_sems, recv_sems = started[0], started[1]
    thru = started[2:2 + 2 * nw]

    def body(*refs):
        for _, arrival in _exchange_copies(mode, refs[:nw], refs[nw:2 * nw], refs[2 * nw], refs[2 * nw + 1]):
            arrival.wait_send()
            arrival.wait_recv()

    outs = pl.pallas_call(
        body, name=name,
        out_shape=tuple(pltpu.HBM(t.shape, t.dtype) for t in thru),
        in_specs=[_HBM] * (2 * nw) + [_SEM, _SEM, _ANY], out_specs=(_HBM,) * (2 * nw),
        input_output_aliases={i: i for i in range(2 * nw)},
        compiler_params=pltpu.CompilerParams(has_side_effects=_DATAFLOW),
    )(*thru, send_sems, recv_sems, after)
    return outs[:nw], outs[nw:]


def _reduce_last(owns, landed, g_small, direct_srcs, direct_landed, spread_row0):
    ns, nd = len(owns), len(direct_srcs)
    halves = [o.shape[0] for o in owns]
    row_block = 16
    spread_rows = direct_srcs[-1].shape[1]
    rest0, rest1 = spread_row0, SMALL_ROWS - spread_row0 - spread_rows
    hs = (rest0 + rest1) // 2
    jobs = [(w, p * (halves[w] // 2), halves[w] // 2) for w in range(ns) for p in range(2)]
    n_swaps = len(jobs)
    jobs += [(ns + d, 0, direct_srcs[d].shape[1]) for d in range(nd)]

    def body(*refs):
        own_refs, land_refs, gsm_ref = refs[:ns], refs[ns:2 * ns], refs[2 * ns]
        dsrc_refs, dland_refs = refs[2 * ns + 1:2 * ns + 1 + nd], refs[2 * ns + 1 + nd:2 * ns + 1 + 2 * nd]
        n_in = 2 * ns + 1 + 2 * nd
        out_refs, osm_ref = refs[n_in:n_in + ns + nd - 1], refs[n_in + ns + nd - 1]
        scr = refs[n_in + ns + nd:]
        own_scr, land_scr, dsrc_scr, dland_scr = (scr[:ns], scr[ns:2 * ns], scr[2 * ns:2 * ns + nd],
                                                  scr[2 * ns + nd:2 * ns + 2 * nd])
        res = scr[2 * ns + 2 * nd:3 * ns + 3 * nd - 1]
        o_rest, ra_sm, p_sm, put_sem, in_sem, s_sem, r_sem, sm_s, sm_r = scr[3 * ns + 3 * nd - 1:]
        x, y, c, chip, peers, peer_chip = _place()
        sib = (x, y, 1 - c)
        half = lambda cc: pl.ds(pl.multiple_of(cc * hs, 8), hs)
        sm_a = _remote(gsm_ref.at[half(1 - c), :], ra_sm, sm_s.at[0], sm_r.at[0], sib)
        sm_a.start()
        swaps = [sm_a]

        def reads(j):
            w, r0, n = jobs[j]
            rows = pl.ds(r0, n)
            if w < ns:
                pairs = [(own_refs[w].at[rows, :], own_scr[w].at[rows, :]),
                         (land_refs[w].at[:, rows, :], land_scr[w].at[:, rows, :])]
            else:
                share = dsrc_refs[w - ns].at[chip if w < ns + nd - 1 else 0]
                pairs = [(share, dsrc_scr[w - ns]), (dland_refs[w - ns], dland_scr[w - ns])]
            return [pltpu.make_async_copy(s, d, in_sem.at[2 * j + i]) for i, (s, d) in enumerate(pairs)]

        for j in range(len(jobs)):
            for cp in reads(j):
                cp.start()
        sm_a.wait_recv()
        p_sm[chip] = gsm_ref[half(c), :] + ra_sm[...]
        for m, (px, py) in enumerate(peers):
            swaps.append(_remote(p_sm.at[chip], p_sm.at[chip], sm_s.at[1 + m], sm_r.at[1 + m], (px, py, c)))
            swaps[-1].start()

        def mine_of(j):
            w, r0, n = jobs[j]
            return out_refs[w].at[pl.ds(pl.multiple_of(c * halves[w] + r0, 8), n), :]

        def sum_of(j):
            w, r0, n = jobs[j]
            return res[w].at[pl.ds(r0, n), :]

        puts = []
        for j, (w, r0, n) in enumerate(jobs):
            for cp in reads(j):
                cp.wait()

            def total(i, carry, w=w, r0=r0):
                rr = pl.multiple_of(r0 + i * row_block, row_block)
                blk = pl.ds(rr, row_block)
                if w < ns:
                    acc = own_scr[w][blk, :]
                    for m in range(_PEERS_OF["scatter"]):
                        acc = acc + land_scr[w][m, blk, :].astype(F32)
                    res[w][blk, :] = acc
                else:
                    theirs = lambda r: dland_scr[w - ns][r, blk, :].astype(F32)
                    acc = ((dsrc_scr[w - ns][blk, :].astype(F32) + theirs(0)) + (theirs(1) + theirs(2))) + (
                        (theirs(3) + theirs(4)) + (theirs(5) + theirs(6)))
                    if w < ns + nd - 1:
                        res[w][blk, :] = acc
                    else:
                        osm_ref[pl.ds(pl.multiple_of(spread_row0 + rr, 8), row_block), :] = acc
                return carry
            lax.fori_loop(0, n // row_block, total, 0)
            if w < ns:
                puts.append(pltpu.make_async_copy(sum_of(j), mine_of(j), put_sem.at[j]))
                swaps.append(_remote(sum_of(j), mine_of(j), s_sem.at[j], r_sem.at[j], sib))
                swaps[-1].start()
            elif w < ns + nd - 1:
                puts.append(pltpu.make_async_copy(res[w], out_refs[w], put_sem.at[j]))
            if w < ns + nd - 1:
                puts[-1].start()
        for m, (px, py) in enumerate(peers):
            _remote(p_sm.at[chip], p_sm.at[peer_chip[m]], sm_s.at[1 + m], sm_r.at[1 + m], (px, py, c)).wait_recv()
        o_rest[half(c), :] = (p_sm[0] + p_sm[1]) + (p_sm[2] + p_sm[3])
        swaps.append(_remote(o_rest.at[half(c), :], o_rest.at[half(c), :], sm_s.at[4], sm_r.at[4], sib))
        swaps[-1].start()
        for j, (w, r0, n) in enumerate(jobs[:n_swaps]):
            theirs = out_refs[w].at[pl.ds(pl.multiple_of((1 - c) * halves[w] + r0, 8), n), :]
            _remote(sum_of(j), theirs, s_sem.at[j], r_sem.at[j], sib).wait_recv()
        _remote(o_rest.at[half(1 - c), :], o_rest.at[half(1 - c), :], sm_s.at[4], sm_r.at[4], sib).wait_recv()
        osm_ref[0:rest0, :] = o_rest[0:rest0, :]
        osm_ref[SMALL_ROWS - rest1:SMALL_ROWS, :] = o_rest[rest0:rest0 + rest1, :]
        for cp in swaps:
            cp.wait_send()
        for cp in puts:
            cp.wait()

    vmem = pl.BlockSpec(memory_space=pltpu.VMEM)
    far = [pl.BlockSpec(memory_space=pl.ANY)]
    return pl.pallas_call(
        body, out_shape=[jax.ShapeDtypeStruct((2 * o.shape[0], o.shape[1]), F32) for o in owns]
        + [jax.ShapeDtypeStruct(s.shape[1:], F32) for s in direct_srcs[:-1]]
        + [jax.ShapeDtypeStruct((SMALL_ROWS, LANES), F32)],
        in_specs=far * (2 * ns) + [vmem] + far * (2 * nd), out_specs=far * (ns + nd - 1) + [vmem],
        scratch_shapes=[pltpu.VMEM(o.shape, o.dtype) for o in owns] + [pltpu.VMEM(l.shape, l.dtype) for l in landed]
        + [pltpu.VMEM(s.shape[1:], s.dtype) for s in direct_srcs]
        + [pltpu.VMEM(l.shape, l.dtype) for l in direct_landed]
        + [pltpu.VMEM(o.shape, F32) for o in owns] + [pltpu.VMEM(s.shape[1:], F32) for s in direct_srcs[:-1]]
        + [pltpu.VMEM((2 * hs, LANES), F32), pltpu.VMEM((hs, LANES), F32), pltpu.VMEM((N_CHIPS, hs, LANES), F32),
           pltpu.SemaphoreType.DMA((len(jobs),)), pltpu.SemaphoreType.DMA((2 * len(jobs),)),
           pltpu.SemaphoreType.DMA((n_swaps,)), pltpu.SemaphoreType.DMA((n_swaps,)),
           pltpu.SemaphoreType.DMA((5,)), pltpu.SemaphoreType.DMA((5,))],
        compiler_params=pltpu.CompilerParams(vmem_limit_bytes=VMEM_LIMIT),
        name="reduce_last")(*owns, *landed, g_small, *direct_srcs, *direct_landed)


_SMALL_PARTS = (("g_norm", 8, 8), ("w_s", 512, 512), ("b_s", 4, 8), ("g_v", 2, 8), ("g_mem", 8, 8),
                ("g_final", 8, 8), ("loss", 8, 8))
_LOSS_ROW = SMALL_ROWS - 8
_W_S_ROW = 8
assert sum(p for _, _, p in _SMALL_PARTS) == SMALL_ROWS and _SMALL_PARTS[1][0] == "w_s"


def _pack_small(parts, loss_block):
    rows = []
    parts = dict(parts, loss=loss_block)
    for name, used, padded in _SMALL_PARTS:
        if name == "w_s":
            continue
        p = parts[name].reshape(used, LANES)
        if padded > used:
            p = jnp.pad(p, ((0, padded - used), (0, 0)))
        rows.append(p)
    return jnp.concatenate(rows, axis=0)


def _local_step(x, mem, target, g_norm, w_in, w_s, b_s, g_v, g_mem, late_weights, g_final,
                fwd_token=None, on_late=None, on_dw=None):
    B, S, _ = x.shape
    x2d = x.reshape(B * S, D_MODEL)
    t2d = target.reshape(B * S, D_MODEL)
    mem2d = mem.reshape(B * N_MEM, D_MODEL)

    proj = _inproj_fwd(x2d, g_norm, w_in, after=() if fwd_token is None else (fwd_token,))
    w_kv, w_out = late_weights(proj)
    kv = _kv_fwd(mem2d, g_mem, w_kv)
    a, lse = _attn_fwd(proj, B, S)
    w_sT = jnp.swapaxes(w_s, 1, 2)
    b_tab = jnp.repeat(b_s.T, HEAD_DIM, axis=1)
    (dx2, da, drest, loss, d_wout, d_ws, d_bs, d_gv, d_gf, dkv) = _mid(
        x2d, t2d, a, proj, kv, w_s, w_sT, b_tab, g_v, w_out, g_final, B, S)
    d_wkv, d_gmem = _kv_bwd(mem2d, g_mem, w_kv, dkv)
    dq, dk, dv = _attn_bwd(proj, a, lse, da, B, S, after=() if on_late is None else (on_late(d_wkv, d_wout, d_ws),))
    if on_dw is None:
        d_win = _inproj_bwd_dw(dq, dk, dv, drest, x2d, g_norm)
        after = ()
    else:
        d_win = None
        after = (on_dw(*_inproj_bwd_dw(dq, dk, dv, drest, x2d, g_norm, reduce=True)),)
    grad_x, d_gnorm = _inproj_bwd_dx(dq, dk, dv, drest, x2d, dx2, g_norm, w_in, after=after)
    d_bs = d_bs[:, :N_SGU_GROUPS].T
    return (loss, grad_x.reshape(B, S, D_MODEL),
            dict(g_norm=d_gnorm, w_in=d_win, w_s=d_ws, b_s=d_bs, g_v=d_gv, g_mem=d_gmem, w_kv=d_wkv,
                 w_out=d_wout, g_final=d_gf))


def kernel(x, mem, g_norm, w_in, w_sgu_spatial, b_sgu_spatial, g_sgu_v, g_mem, w_mem_kv, w_out, g_final, loss_target, m_g_norm, m_w_in, m_w_sgu_spatial, m_b_sgu_spatial, m_g_sgu_v, m_g_mem, m_w_mem_kv, m_w_out, m_g_final, v_g_norm, v_w_in, v_w_sgu_spatial, v_b_sgu_spatial, v_g_sgu_v, v_g_mem, v_w_mem_kv, v_w_out, v_g_final):
    t = lambda w: jnp.swapaxes(w[0], 0, 1)
    (win_all,), late_shards, late_lands = _ag_weights([t(w_in)], [w_mem_kv[0], w_out[0]])
    w_in_full = win_all.reshape(-1, win_all.shape[-1])
    late = _exchange_start("gather", list(late_shards), (win_all,), "gather_late_start", lands=late_lands)

    def late_weights(proj):
        return [z.reshape(-1, z.shape[-1]) for z in _exchange_wait("gather", late, proj, "gather_late_wait")[1]]

    scatter = {}

    def on_late(d_wkv, d_wout, d_ws):
        d_ws = d_ws.reshape(1, -1, LANES)
        scatter["late"] = _exchange_start("direct", [d_wkv, d_wout, d_ws], (), "scatter_late_start")
        return scatter["late"][-1]

    def on_dw(sends, own):
        scatter["own"] = own
        scatter["started"] = _exchange_start("scatter", [sends], (own,), "scatter_start")
        return scatter["started"][-1]

    loss, grad_x, g = _local_step(
        x, mem, loss_target, g_norm, w_in_full, w_sgu_spatial[0], b_sgu_spatial[0], g_sgu_v, g_mem,
        late_weights, g_final.reshape(1, D_MODEL), fwd_token=late[-1], on_late=on_late, on_dw=on_dw)

    small_names = ("g_norm", "w_s", "b_s", "g_v", "g_mem", "g_final")
    g_small = _pack_small({n: g[n] for n in small_names if n != "w_s"}, loss)
    late_srcs, late_landed = _exchange_wait("direct", scatter["late"], g_small, "scatter_late_wait")
    _, landed = _exchange_wait("scatter", scatter["started"], late_landed[0], "scatter_wait")
    gr_in, gr_kv, gr_out, gr_small = _reduce_last([scatter["own"]], landed, g_small, late_srcs, late_landed, _W_S_ROW)

    small_w = (g_norm, w_sgu_spatial, b_sgu_spatial, g_sgu_v, g_mem, g_final)
    small_m = (m_g_norm, m_w_sgu_spatial, m_b_sgu_spatial, m_g_sgu_v, m_g_mem, m_g_final)
    small_v = (v_g_norm, v_w_sgu_spatial, v_b_sgu_spatial, v_g_sgu_v, v_g_mem, v_g_final)
    rows = lambda ws: [w.reshape(-1, LANES) for w in ws]
    small_new, (of_in, (gr_kv, d_kv, nm_kv, nv_kv), (gr_out, d_out, nm_out, nv_out)), loss = _adamw_all(
        gr_small, rows(small_w), rows(small_m), rows(small_v),
        [(t(w_in), gr_in, t(m_w_in), t(v_w_in)), (w_mem_kv[0], gr_kv, m_w_mem_kv[0], v_w_mem_kv[0]),
         (w_out[0], gr_out, m_w_out[0], v_w_out[0])])
    loss = loss.reshape(())
    small = [[z.reshape(w.shape) for z in four] for w, four in zip(small_w, small_new)]
    gr_in, d_in, nm_in, nv_in = [jnp.swapaxes(z, 0, 1) for z in of_in]

    def leaves(kind, big_in, big_kv, big_out):
        s_norm, s_ws, s_bs, s_gv, s_gmem, s_gf = [four[kind] for four in small]
        return [s_norm, big_in[None], s_ws, s_bs, s_gv, s_gmem, big_kv[None], big_out[None], s_gf]

    return (loss, grad_x, *leaves(0, gr_in, gr_kv, gr_out), *leaves(1, d_in, d_kv, d_out),
            *leaves(2, nm_in, nm_kv, nm_out), *leaves(3, nv_in, nv_kv, nv_out))
```

```python
import functools

import jax
import jax.numpy as jnp
from jax import lax
from jax.experimental import pallas as pl
from jax.experimental.pallas import tpu as pltpu

F32 = jnp.float32
BF16 = jnp.bfloat16
MESH = pl.DeviceIdType.MESH

D_MODEL = 1024
ATTN_WIDTH = 512
SGU_WIDTH = 256
MEM_WIDTH = 256
N_MEM = 256
IN_COLS = 3328
QKV_COLS = 3 * ATTN_WIDTH
REST_COLS = IN_COLS - QKV_COLS
SGU_CHUNK = 128
N_SGU_GROUPS = 4
EPS = 1e-6
NEG_INF = -1e30
DILATIONS = (1, 4, 16)
RADIUS = 64
Q_BLOCK = 128
LANES = 128
HEAD_DIM = 64

ADAM_LR = 0.001
ADAM_B1 = 0.9
ADAM_B2 = 0.999
ADAM_EPS = 1e-08
ADAM_WD = 0.01
ADAM_STEP = 10

N_CHIPS = 4
VMEM_LIMIT = 56 * 1024 * 1024
SMALL_ROWS = 560


def _params(sem=None, vmem=VMEM_LIMIT):
    return pltpu.CompilerParams(dimension_semantics=sem, vmem_limit_bytes=vmem)


def _nn(a, b):
    return jnp.dot(a, b, preferred_element_type=F32)


def _nt(a, b):
    return lax.dot_general(a, b, (((1,), (1,)), ((), ())), preferred_element_type=F32)


def _tn(a, b):
    return lax.dot_general(a, b, (((0,), (0,)), ((), ())), preferred_element_type=F32)


def _rms(x):
    r = lax.rsqrt(jnp.mean(x * x, axis=-1, keepdims=True) + EPS)
    return r, x * r


def _head_masks():
    lane = lax.broadcasted_iota(jnp.int32, (1, LANES), 1)
    lo = lane < HEAD_DIM
    return lo, (lo.astype(F32), (~lo).astype(F32))


def _silu_parts(z):
    s = jax.nn.sigmoid(z)
    return z * s, s * (1.0 + z * (1.0 - s))


def _gelu_parts(x):
    c = 0.7978845608028654
    x2 = x * x
    s = jax.nn.sigmoid((2.0 * c) * (x + 0.044715 * (x * x2)))
    return x * s, s * (1.0 + x * (1.0 - s) * ((2.0 * c) * (1.0 + 3.0 * 0.044715 * x2)))


def _after(tokens):
    return [pl.BlockSpec(memory_space=pl.ANY)] * len(tokens)


def _inproj_fwd(x2d, g_norm, w_in_t, after=()):
    T = x2d.shape[0]
    tm = 512

    def body(x_ref, g_ref, w_ref, *rest):
        o_ref = rest[-1]
        _, xh = _rms(x_ref[...])
        h = (xh * g_ref[...]).astype(BF16)
        o_ref[...] = _nt(h, w_ref[...])

    return pl.pallas_call(
        body, grid=(T // tm,),
        in_specs=[pl.BlockSpec((tm, D_MODEL), lambda i: (i, 0)),
                  pl.BlockSpec((1, D_MODEL), lambda i: (0, 0)),
                  pl.BlockSpec((IN_COLS, D_MODEL), lambda i: (0, 0))] + _after(after),
        out_specs=pl.BlockSpec((tm, IN_COLS), lambda i: (i, 0)),
        out_shape=jax.ShapeDtypeStruct((T, IN_COLS), F32),
        compiler_params=_params(("arbitrary",)), name="inproj_fwd")(x2d, g_norm, w_in_t, *after)


def _kv_fwd(mem2d, g_mem, w_kv):
    Tm = mem2d.shape[0]

    def body(m_ref, g_ref, w_ref, o_ref):
        _, mh = _rms(m_ref[...])
        o_ref[...] = _nn((mh * g_ref[...]).astype(BF16), w_ref[...])

    return pl.pallas_call(
        body, out_shape=jax.ShapeDtypeStruct((Tm, 2 * MEM_WIDTH), F32),
        compiler_params=_params(), name="kv_fwd")(mem2d, g_mem, w_kv)


def _kv_bwd(mem2d, g_mem, w_kv, dkv):
    Tm = mem2d.shape[0]

    share = D_MODEL // N_CHIPS

    def body(m_ref, g_ref, w_hbm, dkv_ref, dw_hbm, dg_ref, w_scr, dw_scr, sems):
        get_w = pltpu.make_async_copy(w_hbm, w_scr, sems.at[0])
        get_w.start()
        _, mh = _rms(m_ref[...])
        memn = (mh * g_ref[...]).astype(BF16)
        dkvb = dkv_ref[...].astype(BF16)
        dw = _tn(memn, dkvb).astype(BF16)
        for k in range(N_CHIPS):
            dw_scr[k] = dw[k * share:(k + 1) * share, :]
        put_dw = pltpu.make_async_copy(dw_scr, dw_hbm, sems.at[1])
        put_dw.start()
        get_w.wait()
        dmemn = _nt(dkvb, w_scr[...])
        dg_ref[...] = jnp.sum(dmemn * mh, axis=0, keepdims=True)
        put_dw.wait()

    vmem = pl.BlockSpec(memory_space=pltpu.VMEM)
    far = pl.BlockSpec(memory_space=pl.ANY)
    stack = (N_CHIPS, share, 2 * MEM_WIDTH)
    return pl.pallas_call(
        body, in_specs=[vmem, vmem, far, vmem], out_specs=(far, vmem),
        out_shape=(jax.ShapeDtypeStruct(stack, BF16), jax.ShapeDtypeStruct((1, D_MODEL), F32)),
        scratch_shapes=[pltpu.VMEM(w_kv.shape, w_kv.dtype), pltpu.VMEM(stack, BF16), pltpu.SemaphoreType.DMA((2,))],
        compiler_params=_params(), name="kv_bwd")(mem2d, g_mem, w_kv, dkv)


def _attn_geometry(S):
    geom = []
    for d in DILATIONS:
        L = S // d
        assert L % Q_BLOCK == 0
        geom.append((d, L, min(2 * Q_BLOCK, L), L // Q_BLOCK))
    return geom


def _init_bias(bias_scr, geom, hp):
    row = lax.broadcasted_iota(jnp.int32, (Q_BLOCK, 2 * Q_BLOCK), 0)
    col = lax.broadcasted_iota(jnp.int32, (Q_BLOCK, 2 * Q_BLOCK), 1)
    for j in (0, 1):
        bits = (126 - (2 * hp + j)) * (1 << 23)
        slope = lax.bitcast_convert_type(jnp.full((1, 1), bits, jnp.int32), F32)
        for di, (d, _, _, _) in enumerate(geom):
            for cls, off in enumerate((0, -RADIUS, -2 * RADIUS)):
                dist = jnp.abs(col - row + off)
                bias_scr[di * 6 + cls * 2 + j] = jnp.where(
                    dist <= RADIUS, -(slope * float(d)) * dist.astype(F32), NEG_INF)


SPLIT = 4
COPY_ROWS = 256


def _by4_rows(S, step):
    per_class = S // SPLIT // COPY_ROWS
    r, j = step // per_class, step % per_class
    return (pl.ds(r + SPLIT * j * COPY_ROWS, COPY_ROWS, stride=SPLIT),
            pl.ds(r * (S // SPLIT) + j * COPY_ROWS, COPY_ROWS))


def _to_by4(src, dst, S):
    for i in range(S // COPY_ROWS):
        natural, by4 = _by4_rows(S, i)
        dst[by4, :] = src[natural, :]


def _block_slices(d, L, KW, nqb, r, qb, S):
    qs = qb * Q_BLOCK
    ks = jnp.clip(qs - RADIUS, 0, L - KW)
    cls = jnp.where(qb == 0, 0, jnp.where(qb == nqb - 1, 2, 1))
    if d == 1:
        qsl = pl.ds(pl.multiple_of(qs, Q_BLOCK), Q_BLOCK)
        ksl = pl.ds(pl.multiple_of(ks, RADIUS), KW)
    elif d == SPLIT:
        qsl = pl.ds(pl.multiple_of(r * L + qs, Q_BLOCK), Q_BLOCK)
        ksl = pl.ds(pl.multiple_of(r * L + ks, RADIUS), KW)
    else:
        sub = d // SPLIT
        base = (r % SPLIT) * (S // SPLIT) + r // SPLIT
        qsl = pl.ds(base + qs * sub, Q_BLOCK, stride=sub)
        ksl = pl.ds(base + ks * sub, KW, stride=sub)
    return qsl, ksl, cls


def _for_groups(geom, S, group, fn):
    for di, (d, L, KW, nqb) in enumerate(geom):
        n = group[di]
        assert (d * nqb) % n == 0

        def step(it, carry, di=di, d=d, L=L, KW=KW, nqb=nqb, n=n):
            slices = []
            for g in range(n):
                i = it * n + g
                slices.append(_block_slices(d, L, KW, nqb, i // nqb, i % nqb, S))
            fn(di, KW, slices)
            return carry
        lax.fori_loop(0, d * nqb // n, step, 0)


def _attn_fwd(proj, B, S):
    T = B * S
    geom = _attn_geometry(S)
    n_pairs = ATTN_WIDTH // LANES

    def body(q_ref, k_ref, v_ref, a_ref, lse_ref, bias_scr, q4, k4, v4, *per_dilation):
        o_scr, m_scr, l_scr = per_dilation[0:3], per_dilation[3:6], per_dilation[6:9]
        lo, hm = _head_masks()
        pair = pl.program_id(0)

        @pl.when(pl.program_id(1) == 0)
        def _():
            _init_bias(bias_scr, geom, pair)
        for src, dst in ((q_ref, q4), (k_ref, k4), (v_ref, v4)):
            _to_by4(src, dst, S)

        def group(di, KW, all_slices):
            run = 8
            for first in range(0, len(all_slices), run):
                some(di, KW, all_slices[first:first + run])

        def some(di, KW, slices):
            chains = [(g, j) for g in range(len(slices)) for j in (0, 1)]
            q_src, k_src, v_src = (q_ref, k_ref, v_ref) if di == 0 else (q4, k4, v4)
            q = [q_src[qsl, :] for qsl, _, _ in slices]
            kw = [k_src[ksl, :].astype(BF16) for _, ksl, _ in slices]
            vw = [v_src[ksl, :].astype(BF16) for _, ksl, _ in slices]
            s = {(g, j): _nt((q[g] * (hm[j] * 0.125)).astype(BF16), kw[g])
                 + bias_scr[di * 6 + slices[g][2] * 2 + j, :, pl.ds(0, KW)] for g, j in chains}
            m = {c: jnp.max(s[c], axis=1, keepdims=True) for c in chains}
            p = {c: jnp.exp(s[c] - m[c]) for c in chains}
            l = {c: jnp.sum(p[c], axis=1, keepdims=True) for c in chains}
            o = {(g, j): _nn(p[(g, j)].astype(BF16), vw[g]) for g, j in chains}
            for g, (qsl, _, _) in enumerate(slices):
                o_scr[di][qsl, :] = jnp.where(lo, o[(g, 0)], o[(g, 1)])
                m_scr[di][qsl, :] = jnp.where(lo, m[(g, 0)], m[(g, 1)])
                l_scr[di][qsl, :] = jnp.where(lo, l[(g, 0)], l[(g, 1)])

        _for_groups(geom, S, (16, 16, 16), group)

        for i in range(S // COPY_ROWS):
            natural, by4 = _by4_rows(S, i)
            rows = [natural, by4, by4]
            ms = [m_scr[di][rows[di], :] for di in range(3)]
            mx = jnp.maximum(jnp.maximum(ms[0], ms[1]), ms[2])
            num = 0.0
            den = 0.0
            for di in range(3):
                w = jnp.exp(ms[di] - mx)
                num = num + w * o_scr[di][rows[di], :]
                den = den + w * l_scr[di][rows[di], :]
            a_ref[natural, :] = num / den
            lse_ref[natural, :] = mx + jnp.log(den)

    blk = lambda off: pl.BlockSpec((S, LANES), lambda h, b, off=off: (b, off + h))
    out_blk = pl.BlockSpec((S, LANES), lambda h, b: (b, h))
    return pl.pallas_call(
        body, grid=(n_pairs, B),
        in_specs=[blk(0), blk(n_pairs), blk(2 * n_pairs)],
        out_specs=[out_blk, out_blk],
        out_shape=[jax.ShapeDtypeStruct((T, ATTN_WIDTH), F32)] * 2,
        scratch_shapes=[pltpu.VMEM((18, Q_BLOCK, 2 * Q_BLOCK), F32)] + [pltpu.VMEM((S, LANES), F32)] * 12,
        compiler_params=_params(("arbitrary", "arbitrary")), name="attn_fwd")(proj, proj, proj)


def _attn_bwd(proj, a, lse, da, B, S, after=()):
    T = B * S
    geom = _attn_geometry(S)
    n_pairs = ATTN_WIDTH // LANES

    def body(q_ref, k_ref, v_ref, a_ref, lse_ref, do_ref, *rest):
        dq_ref, dk_ref, dv_ref, bias_scr = rest[len(after):len(after) + 4]
        scr = rest[len(after) + 4:]
        acc = (scr[0:3], scr[3:6])
        natural_in = (q_ref, k_ref, v_ref, a_ref, lse_ref, do_ref)
        by4_in = scr[6:12]
        _, hm = _head_masks()
        pair = pl.program_id(0)

        @pl.when(pl.program_id(1) == 0)
        def _():
            _init_bias(bias_scr, geom, pair)
        for ref in scr[0:6]:
            ref[...] = jnp.zeros_like(ref)
        for src, dst in zip(natural_in, by4_in):
            _to_by4(src, dst, S)

        def group(di, KW, all_slices):
            run = (4, 4, 8)[di]
            for first in range(0, len(all_slices), run):
                some(di, KW, all_slices[first:first + run])

        def some(di, KW, slices):
            n = len(slices)
            chains = [(g, j) for g in range(n) for j in (0, 1)]
            q_src, k_src, v_src, a_src, lse_src, do_src = natural_in if di == 0 else by4_in
            dq_scr, dk_scr, dv_scr = acc[0 if di == 0 else 1]
            q = [q_src[qsl, :] for qsl, _, _ in slices]
            do = [do_src[qsl, :] for qsl, _, _ in slices]
            doa = [do[g] * a_src[slices[g][0], :] for g in range(n)]
            lse_q = [lse_src[qsl, :] for qsl, _, _ in slices]
            kw = [k_src[ksl, :].astype(BF16) for _, ksl, _ in slices]
            vw = [v_src[ksl, :].astype(BF16) for _, ksl, _ in slices]
            qj = {(g, j): (q[g] * (hm[j] * 0.125)).astype(BF16) for g, j in chains}
            doj = {(g, j): (do[g] * hm[j]).astype(BF16) for g, j in chains}
            s = {(g, j): _nt(qj[(g, j)], kw[g])
                 + bias_scr[di * 6 + slices[g][2] * 2 + j, :, pl.ds(0, KW)] for g, j in chains}
            dp = {(g, j): _nt(doj[(g, j)], vw[g]) for g, j in chains}
            dsum = {(g, j): jnp.sum(doa[g] * hm[j], axis=1, keepdims=True) for g, j in chains}
            p = {(g, j): jnp.exp(s[(g, j)] - lse_q[g][:, HEAD_DIM * j:HEAD_DIM * j + 1]) for g, j in chains}
            ds = {c: (p[c] * (dp[c] - dsum[c])).astype(BF16) for c in chains}
            pb = {c: p[c].astype(BF16) for c in chains}
            dq = [_nn(ds[(g, 0)], kw[g]) * (hm[0] * 0.125) + _nn(ds[(g, 1)], kw[g]) * (hm[1] * 0.125)
                  for g in range(n)]
            both = lambda t, g: jnp.concatenate([t[(g, 0)], t[(g, 1)]], axis=0)
            dkw = [_tn(both(ds, g), both(qj, g)) for g in range(n)]
            dvw = [_tn(both(pb, g), both(doj, g)) for g in range(n)]
            for g, (qsl, ksl, _) in enumerate(slices):
                dq_scr[qsl, :] = dq_scr[qsl, :] + dq[g]
                dk_scr[ksl, :] = dk_scr[ksl, :] + dkw[g]
                dv_scr[ksl, :] = dv_scr[ksl, :] + dvw[g]

        _for_groups(geom, S, (16, 16, 16), group)

        for i in range(S // COPY_ROWS):
            natural, by4 = _by4_rows(S, i)
            for nat, split in zip(*acc):
                nat[natural, :] = nat[natural, :] + split[by4, :]
        for out, nat in zip((dq_ref, dk_ref, dv_ref), acc[0]):
            out[...] = nat[...].astype(BF16)

    blk = lambda off: pl.BlockSpec((S, LANES), lambda h, b, off=off: (b, off + h))
    return pl.pallas_call(
        body, grid=(n_pairs, B),
        in_specs=[blk(0), blk(n_pairs), blk(2 * n_pairs), blk(0), blk(0), blk(0)] + _after(after),
        out_specs=[blk(0), blk(0), blk(0)],
        out_shape=[jax.ShapeDtypeStruct((T, ATTN_WIDTH), BF16)] * 3,
        scratch_shapes=[pltpu.VMEM((18, Q_BLOCK, 2 * Q_BLOCK), F32)] + [pltpu.VMEM((S, LANES), F32)] * 12,
        compiler_params=_params(("arbitrary", "arbitrary")), name="attn_bwd")(proj, proj, proj, a, lse, da, *after)


def _mid(x2d, t2d, a, proj, kv, w_s, w_sT, b_tab, g_v, w_out, g_final, B, S):
    T = B * S
    tm = 512
    nt = S // tm
    halves = 2
    hrows = tm // halves

    def body(x_ref, t_ref, a_ref, za_ref, ub_ref, vb_ref, zb_ref, qm_ref, zm_ref, kv_ref,
              ws_ref, wsT_ref, btab_ref, gv_ref, wout_ref, gf_ref,
              dx2_ref, da_ref, drest_ref, loss_ref, dwout_bf_ref, dws_ref, dbs_ref, dgv_ref, dgf_ref, dkv_ref,
              dbtab_scr, dwout_ref, dwout_bf_scr, put_sem):
        b = pl.program_id(0)
        t = pl.program_id(1)
        first = jnp.logical_and(b == 0, t == 0)
        last = jnp.logical_and(b == B - 1, t == nt - 1)
        _, hm = _head_masks()
        lane_g = lax.broadcasted_iota(jnp.int32, (1, SGU_WIDTH), 1) // HEAD_DIM
        gm = [(lane_g == g).astype(F32) for g in range(N_SGU_GROUPS)]
        H = range(halves)
        rows = [pl.ds(h * hrows, hrows) for h in H]
        ld = lambda ref: [ref[r, :] for r in rows]
        cat = lambda parts, axis: jnp.concatenate(parts, axis=axis)
        chunks = [slice(ci * SGU_CHUNK, (ci + 1) * SGU_CHUNK) for ci in range(hrows // SGU_CHUNK)]
        pairs = [slice(pr * LANES, (pr + 1) * LANES) for pr in range(2)]
        heads = [(pr, j) for pr in range(2) for j in (0, 1)]

        @pl.when(first)
        def _():
            loss_ref[...] = jnp.zeros_like(loss_ref)
            dwout_ref[...] = jnp.zeros_like(dwout_ref)
            dws_ref[...] = jnp.zeros_like(dws_ref)
            dbs_ref[...] = jnp.zeros_like(dbs_ref)
            dgv_ref[...] = jnp.zeros_like(dgv_ref)
            dgf_ref[...] = jnp.zeros_like(dgf_ref)
            dbtab_scr[...] = jnp.zeros_like(dbtab_scr)

        @pl.when(t == 0)
        def _():
            dkv_ref[...] = jnp.zeros_like(dkv_ref)

        a_val = ld(a_ref)
        sil_a = [_silu_parts(z) for z in ld(za_ref)]
        gated_a = [s[0] * a for s, a in zip(sil_a, a_val)]
        u = [_gelu_parts(z) for z in ld(ub_ref)]
        vv = [_gelu_parts(z) for z in ld(vb_ref)]
        vnorm = [_rms(v[0]) for v in vv]
        gv = gv_ref[...]
        vn = [(n[1] * gv).astype(BF16) for n in vnorm]
        w_cat = cat([ws_ref[g].astype(BF16) for g in range(N_SGU_GROUPS)], 1)
        wT_cat = cat([wsT_ref[g].astype(BF16) for g in range(N_SGU_GROUPS)], 1)
        gmb = [m.astype(BF16) for m in gm]
        by_group = lambda chunk: cat([chunk * gmb[g] for g in range(N_SGU_GROUPS)], 0)
        btab = btab_ref[...]
        mixed = [cat([btab + _nn(w_cat, by_group(vn[h][c, :])) for c in chunks], 0) for h in H]
        sg = [u[h][0] * mixed[h] for h in H]
        sil_b = [_silu_parts(z) for z in ld(zb_ref)]
        gated_b = [sil_b[h][0] * sg[h] for h in H]

        kvv = kv_ref[...].astype(BF16)
        kp = [kvv[:, p] for p in pairs]
        vp = [kvv[:, MEM_WIDTH + pr * LANES:MEM_WIDTH + (pr + 1) * LANES] for pr in range(2)]
        qm = ld(qm_ref)
        qj = {(h, pr, j): (qm[h][:, pairs[pr]] * (hm[j] * 0.125)).astype(BF16) for h in H for pr, j in heads}
        sc = {k: _nt(qj[k], kp[k[1]]) for k in qj}
        ex = {k: jnp.exp(sc[k] - jnp.max(sc[k], axis=1, keepdims=True)) for k in qj}
        prob = {k: ex[k] * (1.0 / jnp.sum(ex[k], axis=1, keepdims=True)) for k in qj}
        probb = {k: prob[k].astype(BF16) for k in qj}
        mo = [cat([sum(_nn(probb[(h, pr, j)], vp[pr]) * hm[j] for j in (0, 1)) for pr in range(2)], 1) for h in H]
        sil_m = [_silu_parts(z) for z in ld(zm_ref)]
        gated_m = [sil_m[h][0] * mo[h] for h in H]

        gated = [cat([gated_a[h], gated_b[h], gated_m[h]], 1).astype(BF16) for h in H]
        wout = wout_ref[...]
        x_in = ld(x_ref)
        x2 = [x_in[h] + _nn(gated[h], wout) for h in H]
        fin = [_rms(z) for z in x2]
        gf = gf_ref[...]
        tgt = ld(t_ref)
        err = [fin[h][1] * gf - tgt[h] for h in H]
        loss_ref[...] += sum(jnp.sum(e * e) for e in err) * (0.5 / D_MODEL)

        dy = [e * (1.0 / D_MODEL) for e in err]
        dgf_ref[...] += sum(jnp.sum(dy[h] * fin[h][1], axis=0, keepdims=True) for h in H)
        gdy = [d * gf for d in dy]
        dx2 = [fin[h][0] * (gdy[h] - fin[h][1] * jnp.mean(gdy[h] * fin[h][1], axis=1, keepdims=True)) for h in H]
        for h in H:
            dx2_ref[rows[h], :] = dx2[h]
        dx2b = [d.astype(BF16) for d in dx2]
        dgated = [_nt(d, wout) for d in dx2b]
        dwout_ref[...] += _tn(cat(gated, 0), cat(dx2b, 0))
        dga = [d[:, 0:ATTN_WIDTH] for d in dgated]
        dgb = [d[:, ATTN_WIDTH:ATTN_WIDTH + SGU_WIDTH] for d in dgated]
        dgm = [d[:, ATTN_WIDTH + SGU_WIDTH:] for d in dgated]

        for h in H:
            da_ref[rows[h], :] = dga[h] * sil_a[h][0]
        dza = [dga[h] * a_val[h] * sil_a[h][1] for h in H]

        dsg = [dgb[h] * sil_b[h][0] for h in H]
        dzb = [dgb[h] * sg[h] * sil_b[h][1] for h in H]
        dub = [dsg[h] * mixed[h] * u[h][1] for h in H]
        dmixed = [dsg[h] * u[h][0] for h in H]
        dmixed_b = [d.astype(BF16) for d in dmixed]
        dvn = [cat([_nn(wT_cat, by_group(dmixed_b[h][c, :])) for c in chunks], 0) for h in H]
        for g in range(N_SGU_GROUPS):
            dws_ref[g] += sum(_nt((dmixed[h][c, :] * gm[g]).astype(BF16), vn[h][c, :]) for h in H for c in chunks)
        dbtab_scr[...] += sum(dmixed[h][c, :] for h in H for c in chunks)
        dgv_ref[...] += sum(jnp.sum(dvn[h] * vnorm[h][1], axis=0, keepdims=True) for h in H)
        tv = [d * gv for d in dvn]
        dvv = [vnorm[h][0] * (tv[h] - vnorm[h][1] * jnp.mean(tv[h] * vnorm[h][1], axis=1, keepdims=True)) for h in H]
        dvb = [dvv[h] * vv[h][1] for h in H]

        dmo = [dgm[h] * sil_m[h][0] for h in H]
        dzm = [dgm[h] * mo[h] * sil_m[h][1] for h in H]
        dmoj = {(h, pr, j): (dmo[h][:, pairs[pr]] * hm[j]).astype(BF16) for h in H for pr, j in heads}
        dp = {k: _nt(dmoj[k], vp[k[1]]) for k in qj}
        ds = {k: (prob[k] * (dp[k] - jnp.sum(dp[k] * prob[k], axis=1, keepdims=True))).astype(BF16) for k in qj}
        dqm = [cat([sum(_nn(ds[(h, pr, j)], kp[pr]) * (hm[j] * 0.125) for j in (0, 1)) for pr in range(2)], 1)
               for h in H]
        every = lambda tbl, pr: cat([tbl[(h, pr, j)] for h in H for j in (0, 1)], 0)
        dk = [_tn(every(ds, pr), every(qj, pr)) for pr in range(2)]
        dv = [_tn(every(probb, pr), every(dmoj, pr)) for pr in range(2)]
        dkv_ref[...] += cat(dk + dv, 1)

        for h in H:
            drest_ref[rows[h], :] = cat([dza[h], dub[h], dvb[h], dzb[h], dqm[h], dzm[h]], 1).astype(BF16)

        @pl.when(last)
        def _():
            lane = lax.broadcasted_iota(jnp.int32, (1, LANES), 1)
            dbt = dbtab_scr[...]
            out = jnp.zeros((SGU_CHUNK, LANES), F32)
            for g in range(N_SGU_GROUPS):
                out = out + jnp.where(lane == g, jnp.sum(dbt * gm[g], axis=1, keepdims=True), 0.0)
            dbs_ref[...] = out
            share = D_MODEL // N_CHIPS
            puts = [pltpu.make_async_copy(dwout_bf_scr.at[k], dwout_bf_ref.at[k], put_sem.at[k])
                    for k in range(N_CHIPS)]
            for k in range(N_CHIPS):
                for row in range(0, share, SGU_CHUNK):
                    r0 = k * share + row
                    dwout_bf_scr[k, row:row + SGU_CHUNK, :] = dwout_ref[r0:r0 + SGU_CHUNK, :].astype(BF16)
                puts[k].start()
            for cp in puts:
                cp.wait()

    tile = lambda w, cb: pl.BlockSpec((tm, w), lambda b, t, cb=cb: (b * nt + t, cb))
    const = lambda shape: pl.BlockSpec(shape, lambda b, t, n=len(shape): (0,) * n)
    return pl.pallas_call(
        body, grid=(B, nt),
        in_specs=[tile(D_MODEL, 0), tile(D_MODEL, 0), tile(ATTN_WIDTH, 0),
                  tile(ATTN_WIDTH, 3),
                  tile(SGU_WIDTH, 8), tile(SGU_WIDTH, 9), tile(SGU_WIDTH, 10),
                  tile(MEM_WIDTH, 11), tile(MEM_WIDTH, 12),
                  pl.BlockSpec((N_MEM, 2 * MEM_WIDTH), lambda b, t: (b, 0)),
                  const((N_SGU_GROUPS, SGU_CHUNK, SGU_CHUNK)), const((N_SGU_GROUPS, SGU_CHUNK, SGU_CHUNK)),
                  const((SGU_CHUNK, SGU_WIDTH)), const((1, SGU_WIDTH)),
                  const((D_MODEL, D_MODEL)), const((1, D_MODEL))],
        out_specs=[tile(D_MODEL, 0), tile(ATTN_WIDTH, 0), tile(REST_COLS, 0),
                   const((8, LANES)), pl.BlockSpec(memory_space=pl.ANY),
                   const((N_SGU_GROUPS, SGU_CHUNK, SGU_CHUNK)), const((SGU_CHUNK, LANES)),
                   const((1, SGU_WIDTH)), const((1, D_MODEL)),
                   pl.BlockSpec((N_MEM, 2 * MEM_WIDTH), lambda b, t: (b, 0))],
        out_shape=[jax.ShapeDtypeStruct((T, D_MODEL), F32), jax.ShapeDtypeStruct((T, ATTN_WIDTH), F32),
                   jax.ShapeDtypeStruct((T, REST_COLS), BF16),
                   jax.ShapeDtypeStruct((8, LANES), F32),
                   jax.ShapeDtypeStruct((N_CHIPS, D_MODEL // N_CHIPS, D_MODEL), BF16),
                   jax.ShapeDtypeStruct((N_SGU_GROUPS, SGU_CHUNK, SGU_CHUNK), F32),
                   jax.ShapeDtypeStruct((SGU_CHUNK, LANES), F32),
                   jax.ShapeDtypeStruct((1, SGU_WIDTH), F32), jax.ShapeDtypeStruct((1, D_MODEL), F32),
                   jax.ShapeDtypeStruct((B * N_MEM, 2 * MEM_WIDTH), F32)],
        scratch_shapes=[pltpu.VMEM((SGU_CHUNK, SGU_WIDTH), F32), pltpu.VMEM((D_MODEL, D_MODEL), F32),
                        pltpu.VMEM((N_CHIPS, D_MODEL // N_CHIPS, D_MODEL), BF16), pltpu.SemaphoreType.DMA((N_CHIPS,))],
        compiler_params=_params(("arbitrary", "arbitrary"), vmem=VMEM_LIMIT + 2 * 1024 * 1024), name="mid")(
            x2d, t2d, a, proj, proj, proj, proj, proj, proj, kv, w_s, w_sT, b_tab, g_v, w_out, g_final)


def _inproj_bwd_dx(dq, dk, dv, drest, x2d, dx2, g_norm, w_in_t, after=()):
    T = x2d.shape[0]
    tm = 512
    W = ATTN_WIDTH

    def body(dq_ref, dk_ref, dv_ref, dr_ref, x_ref, dx2_ref, g_ref, w_ref, *rest):
        gx_ref, dg_ref = rest[-2:]

        @pl.when(pl.program_id(0) == 0)
        def _():
            dg_ref[...] = jnp.zeros_like(dg_ref)

        halves = [pl.ds(h * (tm // 2), tm // 2) for h in (0, 1)]
        dh = [(_nn(dq_ref[r, :], w_ref[0:W, :]) + _nn(dk_ref[r, :], w_ref[W:2 * W, :])
               + _nn(dv_ref[r, :], w_ref[2 * W:3 * W, :]) + _nn(dr_ref[r, :], w_ref[QKV_COLS:IN_COLS, :]))
              for r in halves]
        nrm = [_rms(x_ref[r, :]) for r in halves]
        dg_ref[...] += sum(jnp.sum(d * n[1], axis=0, keepdims=True) for d, n in zip(dh, nrm))
        g = g_ref[...]
        for r, d, (rstd, xh) in zip(halves, dh, nrm):
            th = d * g
            gx_ref[r, :] = rstd * (th - xh * jnp.mean(th * xh, axis=1, keepdims=True)) + dx2_ref[r, :]

    tile = lambda w: pl.BlockSpec((tm, w), lambda i: (i, 0))
    return pl.pallas_call(
        body, grid=(T // tm,),
        in_specs=[tile(W), tile(W), tile(W), tile(REST_COLS), tile(D_MODEL), tile(D_MODEL),
                  pl.BlockSpec((1, D_MODEL), lambda i: (0, 0)),
                  pl.BlockSpec((IN_COLS, D_MODEL), lambda i: (0, 0))] + _after(after),
        out_specs=[tile(D_MODEL), pl.BlockSpec((1, D_MODEL), lambda i: (0, 0))],
        out_shape=[jax.ShapeDtypeStruct((T, D_MODEL), F32), jax.ShapeDtypeStruct((1, D_MODEL), F32)],
        compiler_params=_params(("arbitrary",)), name="inproj_bwd_dx")(
            dq, dk, dv, drest, x2d, dx2, g_norm, w_in_t, *after)


def _inproj_bwd_dw(dq, dk, dv, drest, x2d, g_norm, reduce=False):
    T = x2d.shape[0]
    tm = 512
    nt = T // tm
    W = ATTN_WIDTH
    shard = IN_COLS // N_CHIPS
    half = shard // 2
    row_block = 32

    def body(dq_ref, dk_ref, dv_ref, dr_ref, x_ref, g_ref, *rest):
        if reduce:
            sends_out, own_out, acc, ras, narrow, sends, own, s_sem, r_sem, put_sem = rest
            x, y, c, chip, peers, peer_chip = _place()
            sib = (x, y, 1 - c)

            def part(k, cc, r0):
                return acc.at[pl.ds(pl.multiple_of(k * shard + cc * half + r0, 8), row_block), :]

            def swap_win(k):
                return _remote(narrow.at[k], ras.at[k], s_sem.at[k], r_sem.at[k], sib)

            def put_send(m):
                return pltpu.make_async_copy(sends.at[m], sends_out.at[m], put_sem.at[m])

            put_own = pltpu.make_async_copy(own, own_out, put_sem.at[N_CHIPS - 1])
        else:
            acc = rest[0]

        @pl.when(pl.program_id(0) == 0)
        def _():
            acc[...] = jnp.zeros_like(acc)

        _, xh = _rms(x_ref[...])
        h = (xh * g_ref[...]).astype(BF16)
        acc[0:W, :] += _tn(dq_ref[...], h)
        acc[W:2 * W, :] += _tn(dk_ref[...], h)
        acc[2 * W:3 * W, :] += _tn(dv_ref[...], h)
        acc[QKV_COLS:IN_COLS, :] += _tn(dr_ref[...], h)

        if reduce:
            @pl.when(pl.program_id(0) == nt - 1)
            def _():
                for k in range(N_CHIPS):
                    def to_bf16(i, carry, k=k):
                        r0 = pl.multiple_of(i * row_block, row_block)
                        narrow[k, pl.ds(r0, row_block), :] = part(k, 1 - c, r0)[...].astype(BF16)
                        return carry
                    lax.fori_loop(0, half // row_block, to_bf16, 0)
                    swap_win(k).start()

                def chip_sum(k, r0):
                    return part(k, c, r0)[...] + ras[k, pl.ds(r0, row_block), :].astype(F32)

                for k in range(N_CHIPS):
                    swap_win(k).wait_recv()

                    @pl.when(chip == k)
                    def _(k=k):
                        def mine(i, carry):
                            r0 = pl.multiple_of(i * row_block, row_block)
                            own[pl.ds(r0, row_block), :] = chip_sum(k, r0)
                            return carry
                        lax.fori_loop(0, half // row_block, mine, 0)
                        put_own.start()

                    @pl.when(chip != k)
                    def _(k=k):
                        def other(i, carry):
                            r0 = pl.multiple_of(i * row_block, row_block)
                            sends[(k ^ chip) - 1, pl.ds(r0, row_block), :] = chip_sum(k, r0).astype(BF16)
                            return carry
                        lax.fori_loop(0, half // row_block, other, 0)
                        put_send((k ^ chip) - 1).start()
                for k in range(N_CHIPS):
                    swap_win(k).wait_send()
                for m in range(N_CHIPS - 1):
                    put_send(m).wait()
                put_own.wait()

    tile = lambda w: pl.BlockSpec((tm, w), lambda i: (i, 0))
    vmem = pl.BlockSpec(memory_space=pltpu.VMEM)
    in_specs = [tile(W), tile(W), tile(W), tile(REST_COLS), tile(D_MODEL), pl.BlockSpec((1, D_MODEL), lambda i: (0, 0))]
    if not reduce:
        return pl.pallas_call(
            body, grid=(nt,), in_specs=in_specs,
            out_specs=pl.BlockSpec((IN_COLS, D_MODEL), lambda i: (0, 0)),
            out_shape=jax.ShapeDtypeStruct((IN_COLS, D_MODEL), F32),
            compiler_params=_params(("arbitrary",)), name="inproj_bwd_dw")(dq, dk, dv, drest, x2d, g_norm)
    quarters = pltpu.VMEM((N_CHIPS, half, D_MODEL), BF16)
    far = pl.BlockSpec(memory_space=pl.ANY)
    return pl.pallas_call(
        body, grid=(nt,), in_specs=in_specs, out_specs=[far] * 2,
        out_shape=[jax.ShapeDtypeStruct((N_CHIPS - 1, half, D_MODEL), BF16),
                   jax.ShapeDtypeStruct((half, D_MODEL), F32)],
        scratch_shapes=[pltpu.VMEM((IN_COLS, D_MODEL), F32), quarters, quarters,
                        pltpu.VMEM((N_CHIPS - 1, half, D_MODEL), BF16), pltpu.VMEM((half, D_MODEL), F32),
                        pltpu.SemaphoreType.DMA((N_CHIPS,)), pltpu.SemaphoreType.DMA((N_CHIPS,)),
                        pltpu.SemaphoreType.DMA((N_CHIPS,))],
        compiler_params=_params(("arbitrary",)), name="inproj_bwd_dw_reduce")(dq, dk, dv, drest, x2d, g_norm)


def _adamw_update(w, g, m, v):
    nm = ADAM_B1 * m + (1.0 - ADAM_B1) * g
    nv = ADAM_B2 * v + (1.0 - ADAM_B2) * (g * g)
    m_hat = nm / (1.0 - ADAM_B1 ** ADAM_STEP)
    v_hat = nv / (1.0 - ADAM_B2 ** ADAM_STEP)
    return -ADAM_LR * (m_hat / (jnp.sqrt(v_hat) + ADAM_EPS) + ADAM_WD * w), nm, nv


def _adamw_all(g_packed, ws, ms, vs, large):
    n, nl = len(ws), len(large)
    row_block = 8
    chunk_bytes = 512 * 1024

    def chunk_rows(w):
        R, C = w.shape
        return max(r for r in range(8, R + 1, 8) if R % r == 0 and r * C * 4 <= chunk_bytes)

    jobs = [(b, r0, chunk_rows(four[0])) for b, four in enumerate(large)
            for r0 in range(0, four[0].shape[0], chunk_rows(four[0]))]

    def body(*refs):
        g_ref = refs[0]
        w_refs, m_refs, v_refs = refs[1:1 + n], refs[1 + n:1 + 2 * n], refs[1 + 2 * n:1 + 3 * n]
        far_in = refs[1 + 3 * n:1 + 3 * n + 4 * nl]
        outs = refs[1 + 3 * n + 4 * nl:1 + 7 * n + 4 * nl]
        far_out = refs[1 + 7 * n + 4 * nl:1 + 7 * n + 8 * nl]
        loss_ref = refs[1 + 7 * n + 8 * nl]
        scr = refs[2 + 7 * n + 8 * nl:]
        in_scr, out_scr, in_sem, out_sem = scr[:4 * nl], scr[4 * nl:7 * nl], scr[7 * nl], scr[7 * nl + 1]

        def read(j, k):
            b, r0, rows = jobs[j]
            blk = pl.ds(r0, rows)
            return pltpu.make_async_copy(far_in[4 * b + k].at[blk, :], in_scr[4 * b + k].at[blk, :],
                                         in_sem.at[4 * j + k])

        def write(j, k):
            b, r0, rows = jobs[j]
            blk = pl.ds(r0, rows)
            src = in_scr[4 * b + 1] if k == 0 else out_scr[3 * b + k - 1]
            return pltpu.make_async_copy(src.at[blk, :], far_out[4 * b + k].at[blk, :], out_sem.at[4 * j + k])

        for j in range(len(jobs)):
            for k in range(4):
                read(j, k).start()
        off = 0
        for i, (_, used, padded) in enumerate(_SMALL_PARTS[:n]):
            g = g_ref[off:off + used, :]
            delta, nm, nv = _adamw_update(w_refs[i][...], g, m_refs[i][...], v_refs[i][...])
            outs[4 * i][...], outs[4 * i + 1][...], outs[4 * i + 2][...], outs[4 * i + 3][...] = g, delta, nm, nv
            off += padded
        loss_ref[...] = g_ref[_LOSS_ROW:_LOSS_ROW + 1, 0:1]
        for j, (b, r0, rows) in enumerate(jobs):
            for k in range(4):
                read(j, k).wait()

            def update(i, carry, b=b, r0=r0):
                blk = pl.ds(pl.multiple_of(r0 + i * row_block, row_block), row_block)
                w, g, m, v = [in_scr[4 * b + k][blk, :] for k in range(4)]
                for k, val in enumerate(_adamw_update(w, g, m, v)):
                    out_scr[3 * b + k][blk, :] = val
                return carry
            lax.fori_loop(0, rows // row_block, update, 0)
            for k in range(4):
                write(j, k).start()
        for j in range(len(jobs)):
            for k in range(4):
                write(j, k).wait()

    vmem = pl.BlockSpec(memory_space=pltpu.VMEM)
    far = pl.BlockSpec(memory_space=pl.ANY)
    flat = [a for four in large for a in four]
    outs = pl.pallas_call(
        body, in_specs=[vmem] * (1 + 3 * n) + [far] * (4 * nl),
        out_specs=[vmem] * (4 * n) + [far] * (4 * nl) + [vmem],
        out_shape=[jax.ShapeDtypeStruct(w.shape, F32) for w in ws for _ in range(4)]
        + [jax.ShapeDtypeStruct(four[0].shape, F32) for four in large for _ in range(4)]
        + [jax.ShapeDtypeStruct((1, 1), F32)],
        scratch_shapes=[pltpu.VMEM(a.shape, F32) for a in flat]
        + [pltpu.VMEM(four[0].shape, F32) for four in large for _ in range(3)]
        + [pltpu.SemaphoreType.DMA((4 * len(jobs),)), pltpu.SemaphoreType.DMA((4 * len(jobs),))],
        compiler_params=_params(), name="adamw_all")(g_packed, *ws, *ms, *vs, *flat)
    return ([outs[4 * i:4 * i + 4] for i in range(n)], [outs[4 * (n + i):4 * (n + i) + 4] for i in range(nl)],
            outs[4 * (n + nl)])


def _place():
    x, y, c = lax.axis_index("x"), lax.axis_index("y"), lax.axis_index("c")
    chip = 2 * x + y
    peers = [(x, 1 - y), (1 - x, y), (1 - x, 1 - y)]
    peer_chip = [2 * px + py for px, py in peers]
    return x, y, c, chip, peers, peer_chip


def _remote(src, dst, send_sem, recv_sem, dev):
    return pltpu.make_async_remote_copy(src_ref=src, dst_ref=dst, send_sem=send_sem, recv_sem=recv_sem,
                                        device_id=dev, device_id_type=MESH)


def _ag_weights(weights, late=()):
    nw, nl = len(weights), len(late)
    pieces = 2

    def body(*refs):
        srcs, late_srcs = refs[:nw], refs[nw:nw + nl]
        outs, late_bf, late_land = (refs[nw + nl:2 * nw + nl], refs[2 * nw + nl:2 * nw + 2 * nl],
                                    refs[2 * nw + 2 * nl:2 * nw + 3 * nl])
        scr = refs[2 * nw + 3 * nl:]
        wide, narrow = scr[:nw], scr[nw:2 * nw]
        late_scr = scr[2 * nw:2 * nw + nl]
        in_sem, put_sem, late_sem, s_ici, r_ici, s_d2d, r_d2d = scr[2 * nw + nl:]
        x, y, c = lax.axis_index("x"), lax.axis_index("y"), lax.axis_index("c")
        chip = 2 * x + y
        sib = (x, y, 1 - c)
        first = ((x + 1 - c) % 2, (y + c) % 2)
        second = ((x + c) % 2, (y + 1 - c) % 2)
        first_chip, second_chip = 2 * first[0] + first[1], 2 * second[0] + second[1]
        diag_chip = 3 - chip

        parts = [(w, out, pc) for w, out in enumerate(outs) for pc in range(pieces)]

        def rows_of(out, cc, pc):
            rows = out.shape[1] // 2 // pieces
            return pl.ds(pl.multiple_of((cc * pieces + pc) * rows, 16), rows)

        def piece(out, k, cc, pc):
            return out.at[k, rows_of(out, cc, pc), :]

        def ici(w, slot, out, k, dev, pc, src=None):
            blk, sem = piece(out, k, c, pc), (nw * slot + w) * pieces + pc
            return _remote(blk if src is None else src, blk, s_ici.at[sem], r_ici.at[sem], (dev[0], dev[1], c))

        def d2d(w, slot, out, k, cc, pc):
            blk, sem = piece(out, k, cc, pc), (nw * slot + w) * pieces + pc
            return _remote(blk, blk, s_d2d.at[sem], r_d2d.at[sem], sib)

        def read(w, cc, pc):
            rows = rows_of(outs[w], cc, pc)
            return pltpu.make_async_copy(srcs[w].at[rows, :], wide[w].at[rows, :],
                                         in_sem.at[(w * 2 + cc) * pieces + pc])

        order = [(w, cc, pc) for cc in (0, 1) for w in range(nw) for pc in range(pieces)]
        for w, cc, pc in order:
            read(w, (c + cc) % 2, pc).start()
        sent = []
        for w, cc, pc in order[:nw * pieces]:
            rows = rows_of(outs[w], c, pc)
            read(w, c, pc).wait()
            narrow[w][rows, :] = wide[w][rows, :].astype(BF16)
            for slot, dev in enumerate((first, second)):
                sent.append(ici(w, slot, outs[w], chip, dev, pc, src=narrow[w].at[rows, :]))
                sent[-1].start()
        late_puts = []
        for i, (src, scr_bf, bf, land) in enumerate(zip(late_srcs, late_scr, late_bf, late_land)):
            scr_bf[...] = src[...].astype(BF16)
            for k, dst in enumerate([bf] + [land.at[s] for s in range(N_CHIPS)]):
                late_puts.append(pltpu.make_async_copy(scr_bf, dst, late_sem.at[i * (N_CHIPS + 1) + k]))
                late_puts[-1].start()
        for w, cc, pc in order[nw * pieces:]:
            rows = rows_of(outs[w], 1 - c, pc)
            read(w, 1 - c, pc).wait()
            narrow[w][rows, :] = wide[w][rows, :].astype(BF16)
        puts = [pltpu.make_async_copy(narrow[w], outs[w].at[chip], put_sem.at[w]) for w in range(nw)]
        for cp in puts:
            cp.start()
        for slot, k, dev in ((0, first_chip, first), (1, second_chip, second), (2, diag_chip, second)):
            for w, out, pc in parts:
                ici(w, slot, out, k, dev, pc).wait_recv()
                if slot == 0:
                    sent.append(ici(w, 2, out, k, second, pc))
                    sent[-1].start()
                sent.append(d2d(w, slot, out, k, c, pc))
                sent[-1].start()
        for slot, k in ((0, second_chip), (1, first_chip), (2, diag_chip)):
            for w, out, pc in parts:
                d2d(w, slot, out, k, 1 - c, pc).wait_recv()
        for cp in sent:
            cp.wait_send()
        for cp in puts + late_puts:
            cp.wait()

    vmem = pl.BlockSpec(memory_space=pltpu.VMEM)
    far = pl.BlockSpec(memory_space=pl.ANY)
    outs = pl.pallas_call(
        body,
        out_shape=[jax.ShapeDtypeStruct((N_CHIPS,) + w.shape, BF16) for w in weights]
        + [jax.ShapeDtypeStruct(w.shape, BF16) for w in late]
        + [jax.ShapeDtypeStruct((N_CHIPS,) + w.shape, BF16) for w in late],
        in_specs=[far] * nw + [vmem] * nl, out_specs=[far] * (nw + 2 * nl),
        scratch_shapes=[pltpu.VMEM(w.shape, F32) for w in weights] + [pltpu.VMEM(w.shape, BF16) for w in weights]
        + [pltpu.VMEM(w.shape, BF16) for w in late]
        + [pltpu.SemaphoreType.DMA((2 * nw * pieces,)), pltpu.SemaphoreType.DMA((nw,)),
           pltpu.SemaphoreType.DMA((nl * (N_CHIPS + 1),))]
        + [pltpu.SemaphoreType.DMA((3 * nw * pieces,))] * 4,
        compiler_params=pltpu.CompilerParams(vmem_limit_bytes=VMEM_LIMIT), name="ag_weights")(*weights, *late)
    return outs[:nw], outs[nw:nw + nl], outs[nw + nl:]


_HBM = pl.BlockSpec(memory_space=pltpu.HBM)
_SEM = pl.BlockSpec(memory_space=pltpu.SEMAPHORE)
_ANY = pl.BlockSpec(memory_space=pl.ANY)
_DATAFLOW = pltpu.SideEffectType.DATAFLOW_SIDE_EFFECTING


def _in_hbm(a):
    return pltpu.with_memory_space_constraint(a, pltpu.HBM)


_PEERS_OF = {"gather": 3, "scatter": 3, "direct": 7}


def _exchange_copies(mode, srcs, lands, send_sems, recv_sems):
    nw = len(srcs)
    x, y, c, chip, peers, peer_chip = _place()
    pairs = []
    if mode == "direct":
        targets = [((x, y), chip, 1)] + [(p, k, d) for p, k in zip(peers, peer_chip) for d in (0, 1)]
        for r, ((px, py), k, d) in enumerate(targets):
            for w in range(nw):
                sems = (send_sems.at[nw * r + w], recv_sems.at[nw * r + w], (px, py, (c + d) % 2))
                share = srcs[w].at[k if srcs[w].shape[0] > 1 else 0]
                pairs.append((_remote(share, lands[w].at[r], *sems),) * 2)
        return pairs
    gather = mode == "gather"
    for m, (px, py) in enumerate(peers):
        for w in range(nw):
            sems = (send_sems.at[nw * m + w], recv_sems.at[nw * m + w], (px, py, c))
            if gather:
                pairs.append((_remote(srcs[w], lands[w].at[chip], *sems),
                              _remote(srcs[w], lands[w].at[peer_chip[m]], *sems)))
            else:
                pairs.append((_remote(srcs[w].at[m], lands[w].at[m], *sems),) * 2)
    return pairs


def _exchange_start(mode, srcs, after, name, lands=None):
    nw = len(srcs)
    n_copies = _PEERS_OF[mode] * nw

    after = tuple(after)

    def body(*refs):
        send_sems, recv_sems = refs[2 * nw + len(after)], refs[2 * nw + len(after) + 1]
        for start, _ in _exchange_copies(mode, refs[:nw], refs[nw:2 * nw], send_sems, recv_sems):
            start.start()
        refs[-1][...] = jnp.zeros_like(refs[-1])

    if lands is None:
        shape = {"gather": lambda s: (N_CHIPS,) + s.shape, "scatter": lambda s: s.shape,
                 "direct": lambda s: (_PEERS_OF["direct"],) + s.shape[1:]}[mode]
        lands = [lax.empty(shape(s), s.dtype) for s in srcs]
    lands = [_in_hbm(l) for l in lands]
    return pl.pallas_call(
        body, name=name,
        out_shape=(pltpu.SemaphoreType.DMA((n_copies,)), pltpu.SemaphoreType.DMA((n_copies,)))
        + tuple(pltpu.HBM(s.shape, s.dtype) for s in srcs)
        + tuple(pltpu.HBM(l.shape, l.dtype) for l in lands)
        + (jax.ShapeDtypeStruct((8, LANES), F32),),
        in_specs=[_HBM] * (2 * nw) + [_ANY] * len(after),
        out_specs=(_SEM, _SEM) + (_HBM,) * (2 * nw) + (pl.BlockSpec(memory_space=pltpu.VMEM),),
        input_output_aliases={i: 2 + i for i in range(2 * nw)},
        compiler_params=pltpu.CompilerParams(has_side_effects=_DATAFLOW),
    )(*[_in_hbm(s) for s in srcs], *lands, *after)


def _exchange_wait(mode, started, after, name):
    nw = (len(started) - 3) // 2
    send_sems, recv_sems = started[0], started[1]
    thru = started[2:2 + 2 * nw]

    def body(*refs):
        for _, arrival in _exchange_copies(mode, refs[:nw], refs[nw:2 * nw], refs[2 * nw], refs[2 * nw + 1]):
            arrival.wait_send()
            arrival.wait_recv()

    outs = pl.pallas_call(
        body, name=name,
        out_shape=tuple(pltpu.HBM(t.shape, t.dtype) for t in thru),
        in_specs=[_HBM] * (2 * nw) + [_SEM, _SEM, _ANY], out_specs=(_HBM,) * (2 * nw),
        input_output_aliases={i: i for i in range(2 * nw)},
        compiler_params=pltpu.CompilerParams(has_side_effects=_DATAFLOW),
    )(*thru, send_sems, recv_sems, after)
    return outs[:nw], outs[nw:]


def _reduce_last(owns, landed, g_small, direct_srcs, direct_landed, spread_row0):
    ns, nd = len(owns), len(direct_srcs)
    halves = [o.shape[0] for o in owns]
    row_block = 16
    spread_rows = direct_srcs[-1].shape[1]
    rest0, rest1 = spread_row0, SMALL_ROWS - spread_row0 - spread_rows
    hs = (rest0 + rest1) // 2
    jobs = [(w, p * (halves[w] // 2), halves[w] // 2) for w in range(ns) for p in range(2)]
    n_swaps = len(jobs)
    jobs += [(ns + d, 0, direct_srcs[d].shape[1]) for d in range(nd)]

    def body(*refs):
        own_refs, land_refs, gsm_ref = refs[:ns], refs[ns:2 * ns], refs[2 * ns]
        dsrc_refs, dland_refs = refs[2 * ns + 1:2 * ns + 1 + nd], refs[2 * ns + 1 + nd:2 * ns + 1 + 2 * nd]
        n_in = 2 * ns + 1 + 2 * nd
        out_refs, osm_ref = refs[n_in:n_in + ns + nd - 1], refs[n_in + ns + nd - 1]
        scr = refs[n_in + ns + nd:]
        own_scr, land_scr, dsrc_scr, dland_scr = (scr[:ns], scr[ns:2 * ns], scr[2 * ns:2 * ns + nd],
                                                  scr[2 * ns + nd:2 * ns + 2 * nd])
        res = scr[2 * ns + 2 * nd:3 * ns + 3 * nd - 1]
        o_rest, ra_sm, p_sm, put_sem, in_sem, s_sem, r_sem, sm_s, sm_r = scr[3 * ns + 3 * nd - 1:]
        x, y, c, chip, peers, peer_chip = _place()
        sib = (x, y, 1 - c)
        half = lambda cc: pl.ds(pl.multiple_of(cc * hs, 8), hs)
        sm_a = _remote(gsm_ref.at[half(1 - c), :], ra_sm, sm_s.at[0], sm_r.at[0], sib)
        sm_a.start()
        swaps = [sm_a]

        def reads(j):
            w, r0, n = jobs[j]
            rows = pl.ds(r0, n)
            if w < ns:
                pairs = [(own_refs[w].at[rows, :], own_scr[w].at[rows, :]),
                         (land_refs[w].at[:, rows, :], land_scr[w].at[:, rows, :])]
            else:
                share = dsrc_refs[w - ns].at[chip if w < ns + nd - 1 else 0]
                pairs = [(share, dsrc_scr[w - ns]), (dland_refs[w - ns], dland_scr[w - ns])]
            return [pltpu.make_async_copy(s, d, in_sem.at[2 * j + i]) for i, (s, d) in enumerate(pairs)]

        for j in range(len(jobs)):
            for cp in reads(j):
                cp.start()
        sm_a.wait_recv()
        p_sm[chip] = gsm_ref[half(c), :] + ra_sm[...]
        for m, (px, py) in enumerate(peers):
            swaps.append(_remote(p_sm.at[chip], p_sm.at[chip], sm_s.at[1 + m], sm_r.at[1 + m], (px, py, c)))
            swaps[-1].start()

        def mine_of(j):
            w, r0, n = jobs[j]
            return out_refs[w].at[pl.ds(pl.multiple_of(c * halves[w] + r0, 8), n), :]

        def sum_of(j):
            w, r0, n = jobs[j]
            return res[w].at[pl.ds(r0, n), :]

        puts = []
        for j, (w, r0, n) in enumerate(jobs):
            for cp in reads(j):
                cp.wait()

            def total(i, carry, w=w, r0=r0):
                rr = pl.multiple_of(r0 + i * row_block, row_block)
                blk = pl.ds(rr, row_block)
                if w < ns:
                    acc = own_scr[w][blk, :]
                    for m in range(_PEERS_OF["scatter"]):
                        acc = acc + land_scr[w][m, blk, :].astype(F32)
                    res[w][blk, :] = acc
                else:
                    theirs = lambda r: dland_scr[w - ns][r, blk, :].astype(F32)
                    acc = ((dsrc_scr[w - ns][blk, :].astype(F32) + theirs(0)) + (theirs(1) + theirs(2))) + (
                        (theirs(3) + theirs(4)) + (theirs(5) + theirs(6)))
                    if w < ns + nd - 1:
                        res[w][blk, :] = acc
                    else:
                        osm_ref[pl.ds(pl.multiple_of(spread_row0 + rr, 8), row_block), :] = acc
                return carry
            lax.fori_loop(0, n // row_block, total, 0)
            if w < ns:
                puts.append(pltpu.make_async_copy(sum_of(j), mine_of(j), put_sem.at[j]))
                swaps.append(_remote(sum_of(j), mine_of(j), s_sem.at[j], r_sem.at[j], sib))
                swaps[-1].start()
            elif w < ns + nd - 1:
                puts.append(pltpu.make_async_copy(res[w], out_refs[w], put_sem.at[j]))
            if w < ns + nd - 1:
                puts[-1].start()
        for m, (px, py) in enumerate(peers):
            _remote(p_sm.at[chip], p_sm.at[peer_chip[m]], sm_s.at[1 + m], sm_r.at[1 + m], (px, py, c)).wait_recv()
        o_rest[half(c), :] = (p_sm[0] + p_sm[1]) + (p_sm[2] + p_sm[3])
        swaps.append(_remote(o_rest.at[half(c), :], o_rest.at[half(c), :], sm_s.at[4], sm_r.at[4], sib))
        swaps[-1].start()
        for j, (w, r0, n) in enumerate(jobs[:n_swaps]):
            theirs = out_refs[w].at[pl.ds(pl.multiple_of((1 - c) * halves[w] + r0, 8), n), :]
            _remote(sum_of(j), theirs, s_sem.at[j], r_sem.at[j], sib).wait_recv()
        _remote(o_rest.at[half(1 - c), :], o_rest.at[half(1 - c), :], sm_s.at[4], sm_r.at[4], sib).wait_recv()
        osm_ref[0:rest0, :] = o_rest[0:rest0, :]
        osm_ref[SMALL_ROWS - rest1:SMALL_ROWS, :] = o_rest[rest0:rest0 + rest1, :]
        for cp in swaps:
            cp.wait_send()
        for cp in puts:
            cp.wait()

    vmem = pl.BlockSpec(memory_space=pltpu.VMEM)
    far = [pl.BlockSpec(memory_space=pl.ANY)]
    return pl.pallas_call(
        body, out_shape=[jax.ShapeDtypeStruct((2 * o.shape[0], o.shape[1]), F32) for o in owns]
        + [jax.ShapeDtypeStruct(s.shape[1:], F32) for s in direct_srcs[:-1]]
        + [jax.ShapeDtypeStruct((SMALL_ROWS, LANES), F32)],
        in_specs=far * (2 * ns) + [vmem] + far * (2 * nd), out_specs=far * (ns + nd - 1) + [vmem],
        scratch_shapes=[pltpu.VMEM(o.shape, o.dtype) for o in owns] + [pltpu.VMEM(l.shape, l.dtype) for l in landed]
        + [pltpu.VMEM(s.shape[1:], s.dtype) for s in direct_srcs]
        + [pltpu.VMEM(l.shape, l.dtype) for l in direct_landed]
        + [pltpu.VMEM(o.shape, F32) for o in owns] + [pltpu.VMEM(s.shape[1:], F32) for s in direct_srcs[:-1]]
        + [pltpu.VMEM((2 * hs, LANES), F32), pltpu.VMEM((hs, LANES), F32), pltpu.VMEM((N_CHIPS, hs, LANES), F32),
           pltpu.SemaphoreType.DMA((len(jobs),)), pltpu.SemaphoreType.DMA((2 * len(jobs),)),
           pltpu.SemaphoreType.DMA((n_swaps,)), pltpu.SemaphoreType.DMA((n_swaps,)),
           pltpu.SemaphoreType.DMA((5,)), pltpu.SemaphoreType.DMA((5,))],
        compiler_params=pltpu.CompilerParams(vmem_limit_bytes=VMEM_LIMIT),
        name="reduce_last")(*owns, *landed, g_small, *direct_srcs, *direct_landed)


_SMALL_PARTS = (("g_norm", 8, 8), ("w_s", 512, 512), ("b_s", 4, 8), ("g_v", 2, 8), ("g_mem", 8, 8),
                ("g_final", 8, 8), ("loss", 8, 8))
_LOSS_ROW = SMALL_ROWS - 8
_W_S_ROW = 8
assert sum(p for _, _, p in _SMALL_PARTS) == SMALL_ROWS and _SMALL_PARTS[1][0] == "w_s"


def _pack_small(parts, loss_block):
    rows = []
    parts = dict(parts, loss=loss_block)
    for name, used, padded in _SMALL_PARTS:
        if name == "w_s":
            continue
        p = parts[name].reshape(used, LANES)
        if padded > used:
            p = jnp.pad(p, ((0, padded - used), (0, 0)))
        rows.append(p)
    return jnp.concatenate(rows, axis=0)


def _local_step(x, mem, target, g_norm, w_in, w_s, b_s, g_v, g_mem, late_weights, g_final,
                fwd_token=None, on_late=None, on_dw=None):
    B, S, _ = x.shape
    x2d = x.reshape(B * S, D_MODEL)
    t2d = target.reshape(B * S, D_MODEL)
    mem2d = mem.reshape(B * N_MEM, D_MODEL)

    proj = _inproj_fwd(x2d, g_norm, w_in, after=() if fwd_token is None else (fwd_token,))
    w_kv, w_out = late_weights(proj)
    kv = _kv_fwd(mem2d, g_mem, w_kv)
    a, lse = _attn_fwd(proj, B, S)
    w_sT = jnp.swapaxes(w_s, 1, 2)
    b_tab = jnp.repeat(b_s.T, HEAD_DIM, axis=1)
    (dx2, da, drest, loss, d_wout, d_ws, d_bs, d_gv, d_gf, dkv) = _mid(
        x2d, t2d, a, proj, kv, w_s, w_sT, b_tab, g_v, w_out, g_final, B, S)
    d_wkv, d_gmem = _kv_bwd(mem2d, g_mem, w_kv, dkv)
    dq, dk, dv = _attn_bwd(proj, a, lse, da, B, S, after=() if on_late is None else (on_late(d_wkv, d_wout, d_ws),))
    if on_dw is None:
        d_win = _inproj_bwd_dw(dq, dk, dv, drest, x2d, g_norm)
        after = ()
    else:
        d_win = None
        after = (on_dw(*_inproj_bwd_dw(dq, dk, dv, drest, x2d, g_norm, reduce=True)),)
    grad_x, d_gnorm = _inproj_bwd_dx(dq, dk, dv, drest, x2d, dx2, g_norm, w_in, after=after)
    d_bs = d_bs[:, :N_SGU_GROUPS].T
    return (loss, grad_x.reshape(B, S, D_MODEL),
            dict(g_norm=d_gnorm, w_in=d_win, w_s=d_ws, b_s=d_bs, g_v=d_gv, g_mem=d_gmem, w_kv=d_wkv,
                 w_out=d_wout, g_final=d_gf))


def kernel(x, mem, g_norm, w_in, w_sgu_spatial, b_sgu_spatial, g_sgu_v, g_mem, w_mem_kv, w_out, g_final, loss_target, m_g_norm, m_w_in, m_w_sgu_spatial, m_b_sgu_spatial, m_g_sgu_v, m_g_mem, m_w_mem_kv, m_w_out, m_g_final, v_g_norm, v_w_in, v_w_sgu_spatial, v_b_sgu_spatial, v_g_sgu_v, v_g_mem, v_w_mem_kv, v_w_out, v_g_final):
    t = lambda w: jnp.swapaxes(w[0], 0, 1)
    (win_all,), late_shards, late_lands = _ag_weights([t(w_in)], [w_mem_kv[0], w_out[0]])
    w_in_full = win_all.reshape(-1, win_all.shape[-1])
    late = _exchange_start("gather", list(late_shards), (win_all,), "gather_late_start", lands=late_lands)

    def late_weights(proj):
        return [z.reshape(-1, z.shape[-1]) for z in _exchange_wait("gather", late, proj, "gather_late_wait")[1]]

    scatter = {}

    def on_late(d_wkv, d_wout, d_ws):
        d_ws = d_ws.reshape(1, -1, LANES)
        scatter["late"] = _exchange_start("direct", [d_wkv, d_wout, d_ws], (), "scatter_late_start")
        return scatter["late"][-1]

    def on_dw(sends, own):
        scatter["own"] = own
        scatter["started"] = _exchange_start("scatter", [sends], (own,), "scatter_start")
        return scatter["started"][-1]

    loss, grad_x, g = _local_step(
        x, mem, loss_target, g_norm, w_in_full, w_sgu_spatial[0], b_sgu_spatial[0], g_sgu_v, g_mem,
        late_weights, g_final.reshape(1, D_MODEL), fwd_token=late[-1], on_late=on_late, on_dw=on_dw)

    small_names = ("g_norm", "w_s", "b_s", "g_v", "g_mem", "g_final")
    g_small = _pack_small({n: g[n] for n in small_names if n != "w_s"}, loss)
    late_srcs, late_landed = _exchange_wait("direct", scatter["late"], g_small, "scatter_late_wait")
    _, landed = _exchange_wait("scatter", scatter["started"], late_landed[0], "scatter_wait")
    gr_in, gr_kv, gr_out, gr_small = _reduce_last([scatter["own"]], landed, g_small, late_srcs, late_landed, _W_S_ROW)

    small_w = (g_norm, w_sgu_spatial, b_sgu_spatial, g_sgu_v, g_mem, g_final)
    small_m = (m_g_norm, m_w_sgu_spatial, m_b_sgu_spatial, m_g_sgu_v, m_g_mem, m_g_final)
    small_v = (v_g_norm, v_w_sgu_spatial, v_b_sgu_spatial, v_g_sgu_v, v_g_mem, v_g_final)
    rows = lambda ws: [w.reshape(-1, LANES) for w in ws]
    small_new, (of_in, (gr_kv, d_kv, nm_kv, nv_kv), (gr_out, d_out, nm_out, nv_out)), loss = _adamw_all(
        gr_small, rows(small_w), rows(small_m), rows(small_v),
        [(t(w_in), gr_in, t(m_w_in), t(v_w_in)), (w_mem_kv[0], gr_kv, m_w_mem_kv[0], v_w_mem_kv[0]),
         (w_out[0], gr_out, m_w_out[0], v_w_out[0])])
    loss = loss.reshape(())
    small = [[z.reshape(w.shape) for z in four] for w, four in zip(small_w, small_new)]
    gr_in, d_in, nm_in, nv_in = [jnp.swapaxes(z, 0, 1) for z in of_in]

    def leaves(kind, big_in, big_kv, big_out):
        s_norm, s_ws, s_bs, s_gv, s_gmem, s_gf = [four[kind] for four in small]
        return [s_norm, big_in[None], s_ws, s_bs, s_gv, s_gmem, big_kv[None], big_out[None], s_gf]

    return (loss, grad_x, *leaves(0, gr_in, gr_kv, gr_out), *leaves(1, d_in, d_kv, d_out),
            *leaves(2, nm_in, nm_kv, nm_out), *leaves(3, nv_in, nv_kv, nv_out))
```

```python
import functools

import jax
import jax.numpy as jnp
from jax import lax
from jax.experimental import pallas as pl
from jax.experimental.pallas import tpu as pltpu

F32 = jnp.float32
BF16 = jnp.bfloat16
MESH = pl.DeviceIdType.MESH

D_MODEL = 1024
ATTN_WIDTH = 512
SGU_WIDTH = 256
MEM_WIDTH = 256
N_MEM = 256
IN_COLS = 3328
QKV_COLS = 3 * ATTN_WIDTH
REST_COLS = IN_COLS - QKV_COLS
SGU_CHUNK = 128
N_SGU_GROUPS = 4
EPS = 1e-6
NEG_INF = -1e30
DILATIONS = (1, 4, 16)
RADIUS = 64
Q_BLOCK = 128
LANES = 128
HEAD_DIM = 64

ADAM_LR = 0.001
ADAM_B1 = 0.9
ADAM_B2 = 0.999
ADAM_EPS = 1e-08
ADAM_WD = 0.01
ADAM_STEP = 10

N_CHIPS = 4
VMEM_LIMIT = 56 * 1024 * 1024
SMALL_ROWS = 560


def _params(sem=None, vmem=VMEM_LIMIT):
    return pltpu.CompilerParams(dimension_semantics=sem, vmem_limit_bytes=vmem)


def _nn(a, b):
    return jnp.dot(a, b, preferred_element_type=F32)


def _nt(a, b):
    return lax.dot_general(a, b, (((1,), (1,)), ((), ())), preferred_element_type=F32)


def _tn(a, b):
    return lax.dot_general(a, b, (((0,), (0,)), ((), ())), preferred_element_type=F32)


def _rms(x):
    r = lax.rsqrt(jnp.mean(x * x, axis=-1, keepdims=True) + EPS)
    return r, x * r


def _head_masks():
    lane = lax.broadcasted_iota(jnp.int32, (1, LANES), 1)
    lo = lane < HEAD_DIM
    return lo, (lo.astype(F32), (~lo).astype(F32))


def _silu_parts(z):
    s = jax.nn.sigmoid(z)
    return z * s, s * (1.0 + z * (1.0 - s))


def _gelu_parts(x):
    c = 0.7978845608028654
    x2 = x * x
    s = jax.nn.sigmoid((2.0 * c) * (x + 0.044715 * (x * x2)))
    return x * s, s * (1.0 + x * (1.0 - s) * ((2.0 * c) * (1.0 + 3.0 * 0.044715 * x2)))


def _after(tokens):
    return [pl.BlockSpec(memory_space=pl.ANY)] * len(tokens)


def _inproj_fwd(x2d, g_norm, w_in_t, after=()):
    T = x2d.shape[0]
    tm = 512

    def body(x_ref, g_ref, w_ref, *rest):
        o_ref = rest[-1]
        _, xh = _rms(x_ref[...])
        h = (xh * g_ref[...]).astype(BF16)
        o_ref[...] = _nt(h, w_ref[...])

    return pl.pallas_call(
        body, grid=(T // tm,),
        in_specs=[pl.BlockSpec((tm, D_MODEL), lambda i: (i, 0)),
                  pl.BlockSpec((1, D_MODEL), lambda i: (0, 0)),
                  pl.BlockSpec((IN_COLS, D_MODEL), lambda i: (0, 0))] + _after(after),
        out_specs=pl.BlockSpec((tm, IN_COLS), lambda i: (i, 0)),
        out_shape=jax.ShapeDtypeStruct((T, IN_COLS), F32),
        compiler_params=_params(("arbitrary",)), name="inproj_fwd")(x2d, g_norm, w_in_t, *after)


def _kv_fwd(mem2d, g_mem, w_kv):
    Tm = mem2d.shape[0]

    def body(m_ref, g_ref, w_ref, o_ref):
        _, mh = _rms(m_ref[...])
        o_ref[...] = _nn((mh * g_ref[...]).astype(BF16), w_ref[...])

    return pl.pallas_call(
        body, out_shape=jax.ShapeDtypeStruct((Tm, 2 * MEM_WIDTH), F32),
        compiler_params=_params(), name="kv_fwd")(mem2d, g_mem, w_kv)


def _kv_bwd(mem2d, g_mem, w_kv, dkv):
    Tm = mem2d.shape[0]

    def body(m_ref, g_ref, w_ref, dkv_ref, dw_ref, dg_ref):
        _, mh = _rms(m_ref[...])
        memn = (mh * g_ref[...]).astype(BF16)
        dkvb = dkv_ref[...].astype(BF16)
        dw = _tn(memn, dkvb).astype(BF16)
        for k in range(N_CHIPS):
            dw_ref[k] = dw[k * (D_MODEL // N_CHIPS):(k + 1) * (D_MODEL // N_CHIPS), :]
        dmemn = _nt(dkvb, w_ref[...])
        dg_ref[...] = jnp.sum(dmemn * mh, axis=0, keepdims=True)

    return pl.pallas_call(
        body, out_shape=(jax.ShapeDtypeStruct((N_CHIPS, D_MODEL // N_CHIPS, 2 * MEM_WIDTH), BF16),
                         jax.ShapeDtypeStruct((1, D_MODEL), F32)),
        compiler_params=_params(), name="kv_bwd")(mem2d, g_mem, w_kv, dkv)


def _attn_geometry(S):
    geom = []
    for d in DILATIONS:
        L = S // d
        assert L % Q_BLOCK == 0
        geom.append((d, L, min(2 * Q_BLOCK, L), L // Q_BLOCK))
    return geom


def _init_bias(bias_scr, geom, hp):
    row = lax.broadcasted_iota(jnp.int32, (Q_BLOCK, 2 * Q_BLOCK), 0)
    col = lax.broadcasted_iota(jnp.int32, (Q_BLOCK, 2 * Q_BLOCK), 1)
    for j in (0, 1):
        bits = (126 - (2 * hp + j)) * (1 << 23)
        slope = lax.bitcast_convert_type(jnp.full((1, 1), bits, jnp.int32), F32)
        for di, (d, _, _, _) in enumerate(geom):
            for cls, off in enumerate((0, -RADIUS, -2 * RADIUS)):
                dist = jnp.abs(col - row + off)
                bias_scr[di * 6 + cls * 2 + j] = jnp.where(
                    dist <= RADIUS, -(slope * float(d)) * dist.astype(F32), NEG_INF)


SPLIT = 4
COPY_ROWS = 256


def _by4_rows(S, step):
    per_class = S // SPLIT // COPY_ROWS
    r, j = step // per_class, step % per_class
    return (pl.ds(r + SPLIT * j * COPY_ROWS, COPY_ROWS, stride=SPLIT),
            pl.ds(r * (S // SPLIT) + j * COPY_ROWS, COPY_ROWS))


def _to_by4(src, dst, S):
    for i in range(S // COPY_ROWS):
        natural, by4 = _by4_rows(S, i)
        dst[by4, :] = src[natural, :]


def _block_slices(d, L, KW, nqb, r, qb, S):
    qs = qb * Q_BLOCK
    ks = jnp.clip(qs - RADIUS, 0, L - KW)
    cls = jnp.where(qb == 0, 0, jnp.where(qb == nqb - 1, 2, 1))
    if d == 1:
        qsl = pl.ds(pl.multiple_of(qs, Q_BLOCK), Q_BLOCK)
        ksl = pl.ds(pl.multiple_of(ks, RADIUS), KW)
    elif d == SPLIT:
        qsl = pl.ds(pl.multiple_of(r * L + qs, Q_BLOCK), Q_BLOCK)
        ksl = pl.ds(pl.multiple_of(r * L + ks, RADIUS), KW)
    else:
        sub = d // SPLIT
        base = (r % SPLIT) * (S // SPLIT) + r // SPLIT
        qsl = pl.ds(base + qs * sub, Q_BLOCK, stride=sub)
        ksl = pl.ds(base + ks * sub, KW, stride=sub)
    return qsl, ksl, cls


def _for_groups(geom, S, group, fn):
    for di, (d, L, KW, nqb) in enumerate(geom):
        n = group[di]
        assert (d * nqb) % n == 0

        def step(it, carry, di=di, d=d, L=L, KW=KW, nqb=nqb, n=n):
            slices = []
            for g in range(n):
                i = it * n + g
                slices.append(_block_slices(d, L, KW, nqb, i // nqb, i % nqb, S))
            fn(di, KW, slices)
            return carry
        lax.fori_loop(0, d * nqb // n, step, 0)


def _attn_fwd(proj, B, S):
    T = B * S
    geom = _attn_geometry(S)
    n_pairs = ATTN_WIDTH // LANES

    def body(q_ref, k_ref, v_ref, a_ref, lse_ref, bias_scr, q4, k4, v4, *per_dilation):
        o_scr, m_scr, l_scr = per_dilation[0:3], per_dilation[3:6], per_dilation[6:9]
        lo, hm = _head_masks()
        pair = pl.program_id(0)

        @pl.when(pl.program_id(1) == 0)
        def _():
            _init_bias(bias_scr, geom, pair)
        for src, dst in ((q_ref, q4), (k_ref, k4), (v_ref, v4)):
            _to_by4(src, dst, S)

        def group(di, KW, all_slices):
            run = 8
            for first in range(0, len(all_slices), run):
                some(di, KW, all_slices[first:first + run])

        def some(di, KW, slices):
            chains = [(g, j) for g in range(len(slices)) for j in (0, 1)]
            q_src, k_src, v_src = (q_ref, k_ref, v_ref) if di == 0 else (q4, k4, v4)
            q = [q_src[qsl, :] for qsl, _, _ in slices]
            kw = [k_src[ksl, :].astype(BF16) for _, ksl, _ in slices]
            vw = [v_src[ksl, :].astype(BF16) for _, ksl, _ in slices]
            s = {(g, j): _nt((q[g] * (hm[j] * 0.125)).astype(BF16), kw[g])
                 + bias_scr[di * 6 + slices[g][2] * 2 + j, :, pl.ds(0, KW)] for g, j in chains}
            m = {c: jnp.max(s[c], axis=1, keepdims=True) for c in chains}
            p = {c: jnp.exp(s[c] - m[c]) for c in chains}
            l = {c: jnp.sum(p[c], axis=1, keepdims=True) for c in chains}
            o = {(g, j): _nn(p[(g, j)].astype(BF16), vw[g]) for g, j in chains}
            for g, (qsl, _, _) in enumerate(slices):
                o_scr[di][qsl, :] = jnp.where(lo, o[(g, 0)], o[(g, 1)])
                m_scr[di][qsl, :] = jnp.where(lo, m[(g, 0)], m[(g, 1)])
                l_scr[di][qsl, :] = jnp.where(lo, l[(g, 0)], l[(g, 1)])

        _for_groups(geom, S, (16, 16, 16), group)

        for i in range(S // COPY_ROWS):
            natural, by4 = _by4_rows(S, i)
            rows = [natural, by4, by4]
            ms = [m_scr[di][rows[di], :] for di in range(3)]
            mx = jnp.maximum(jnp.maximum(ms[0], ms[1]), ms[2])
            num = 0.0
            den = 0.0
            for di in range(3):
                w = jnp.exp(ms[di] - mx)
                num = num + w * o_scr[di][rows[di], :]
                den = den + w * l_scr[di][rows[di], :]
            a_ref[natural, :] = num / den
            lse_ref[natural, :] = mx + jnp.log(den)

    blk = lambda off: pl.BlockSpec((S, LANES), lambda h, b, off=off: (b, off + h))
    out_blk = pl.BlockSpec((S, LANES), lambda h, b: (b, h))
    return pl.pallas_call(
        body, grid=(n_pairs, B),
        in_specs=[blk(0), blk(n_pairs), blk(2 * n_pairs)],
        out_specs=[out_blk, out_blk],
        out_shape=[jax.ShapeDtypeStruct((T, ATTN_WIDTH), F32)] * 2,
        scratch_shapes=[pltpu.VMEM((18, Q_BLOCK, 2 * Q_BLOCK), F32)] + [pltpu.VMEM((S, LANES), F32)] * 12,
        compiler_params=_params(("arbitrary", "arbitrary")), name="attn_fwd")(proj, proj, proj)


def _attn_bwd(proj, a, lse, da, B, S, after=()):
    T = B * S
    geom = _attn_geometry(S)
    n_pairs = ATTN_WIDTH // LANES

    def body(q_ref, k_ref, v_ref, a_ref, lse_ref, do_ref, *rest):
        dq_ref, dk_ref, dv_ref, bias_scr = rest[len(after):len(after) + 4]
        scr = rest[len(after) + 4:]
        acc = (scr[0:3], scr[3:6])
        natural_in = (q_ref, k_ref, v_ref, a_ref, lse_ref, do_ref)
        by4_in = scr[6:12]
        _, hm = _head_masks()
        pair = pl.program_id(0)

        @pl.when(pl.program_id(1) == 0)
        def _():
            _init_bias(bias_scr, geom, pair)
        for ref in scr[0:6]:
            ref[...] = jnp.zeros_like(ref)
        for src, dst in zip(natural_in, by4_in):
            _to_by4(src, dst, S)

        def group(di, KW, all_slices):
            run = (4, 4, 8)[di]
            for first in range(0, len(all_slices), run):
                some(di, KW, all_slices[first:first + run])

        def some(di, KW, slices):
            n = len(slices)
            chains = [(g, j) for g in range(n) for j in (0, 1)]
            q_src, k_src, v_src, a_src, lse_src, do_src = natural_in if di == 0 else by4_in
            dq_scr, dk_scr, dv_scr = acc[0 if di == 0 else 1]
            q = [q_src[qsl, :] for qsl, _, _ in slices]
            do = [do_src[qsl, :] for qsl, _, _ in slices]
            doa = [do[g] * a_src[slices[g][0], :] for g in range(n)]
            lse_q = [lse_src[qsl, :] for qsl, _, _ in slices]
            kw = [k_src[ksl, :].astype(BF16) for _, ksl, _ in slices]
            vw = [v_src[ksl, :].astype(BF16) for _, ksl, _ in slices]
            qj = {(g, j): (q[g] * (hm[j] * 0.125)).astype(BF16) for g, j in chains}
            doj = {(g, j): (do[g] * hm[j]).astype(BF16) for g, j in chains}
            s = {(g, j): _nt(qj[(g, j)], kw[g])
                 + bias_scr[di * 6 + slices[g][2] * 2 + j, :, pl.ds(0, KW)] for g, j in chains}
            dp = {(g, j): _nt(doj[(g, j)], vw[g]) for g, j in chains}
            dsum = {(g, j): jnp.sum(doa[g] * hm[j], axis=1, keepdims=True) for g, j in chains}
            p = {(g, j): jnp.exp(s[(g, j)] - lse_q[g][:, HEAD_DIM * j:HEAD_DIM * j + 1]) for g, j in chains}
            ds = {c: (p[c] * (dp[c] - dsum[c])).astype(BF16) for c in chains}
            pb = {c: p[c].astype(BF16) for c in chains}
            dq = [_nn(ds[(g, 0)], kw[g]) * (hm[0] * 0.125) + _nn(ds[(g, 1)], kw[g]) * (hm[1] * 0.125)
                  for g in range(n)]
            both = lambda t, g: jnp.concatenate([t[(g, 0)], t[(g, 1)]], axis=0)
            dkw = [_tn(both(ds, g), both(qj, g)) for g in range(n)]
            dvw = [_tn(both(pb, g), both(doj, g)) for g in range(n)]
            for g, (qsl, ksl, _) in enumerate(slices):
                dq_scr[qsl, :] = dq_scr[qsl, :] + dq[g]
                dk_scr[ksl, :] = dk_scr[ksl, :] + dkw[g]
                dv_scr[ksl, :] = dv_scr[ksl, :] + dvw[g]

        _for_groups(geom, S, (16, 16, 16), group)

        for i in range(S // COPY_ROWS):
            natural, by4 = _by4_rows(S, i)
            for nat, split in zip(*acc):
                nat[natural, :] = nat[natural, :] + split[by4, :]
        for out, nat in zip((dq_ref, dk_ref, dv_ref), acc[0]):
            out[...] = nat[...].astype(BF16)

    blk = lambda off: pl.BlockSpec((S, LANES), lambda h, b, off=off: (b, off + h))
    return pl.pallas_call(
        body, grid=(n_pairs, B),
        in_specs=[blk(0), blk(n_pairs), blk(2 * n_pairs), blk(0), blk(0), blk(0)] + _after(after),
        out_specs=[blk(0), blk(0), blk(0)],
        out_shape=[jax.ShapeDtypeStruct((T, ATTN_WIDTH), BF16)] * 3,
        scratch_shapes=[pltpu.VMEM((18, Q_BLOCK, 2 * Q_BLOCK), F32)] + [pltpu.VMEM((S, LANES), F32)] * 12,
        compiler_params=_params(("arbitrary", "arbitrary")), name="attn_bwd")(proj, proj, proj, a, lse, da, *after)


def _mid(x2d, t2d, a, proj, kv, w_s, w_sT, b_tab, g_v, w_out, g_final, B, S):
    T = B * S
    tm = 512
    nt = S // tm
    halves = 2
    hrows = tm // halves

    def body(x_ref, t_ref, a_ref, za_ref, ub_ref, vb_ref, zb_ref, qm_ref, zm_ref, kv_ref,
              ws_ref, wsT_ref, btab_ref, gv_ref, wout_ref, gf_ref,
              dx2_ref, da_ref, drest_ref, loss_ref, dwout_bf_ref, dws_ref, dbs_ref, dgv_ref, dgf_ref, dkv_ref,
              dbtab_scr, dwout_ref):
        b = pl.program_id(0)
        t = pl.program_id(1)
        first = jnp.logical_and(b == 0, t == 0)
        last = jnp.logical_and(b == B - 1, t == nt - 1)
        _, hm = _head_masks()
        lane_g = lax.broadcasted_iota(jnp.int32, (1, SGU_WIDTH), 1) // HEAD_DIM
        gm = [(lane_g == g).astype(F32) for g in range(N_SGU_GROUPS)]
        H = range(halves)
        rows = [pl.ds(h * hrows, hrows) for h in H]
        ld = lambda ref: [ref[r, :] for r in rows]
        cat = lambda parts, axis: jnp.concatenate(parts, axis=axis)
        chunks = [slice(ci * SGU_CHUNK, (ci + 1) * SGU_CHUNK) for ci in range(hrows // SGU_CHUNK)]
        pairs = [slice(pr * LANES, (pr + 1) * LANES) for pr in range(2)]
        heads = [(pr, j) for pr in range(2) for j in (0, 1)]

        @pl.when(first)
        def _():
            loss_ref[...] = jnp.zeros_like(loss_ref)
            dwout_ref[...] = jnp.zeros_like(dwout_ref)
            dws_ref[...] = jnp.zeros_like(dws_ref)
            dbs_ref[...] = jnp.zeros_like(dbs_ref)
            dgv_ref[...] = jnp.zeros_like(dgv_ref)
            dgf_ref[...] = jnp.zeros_like(dgf_ref)
            dbtab_scr[...] = jnp.zeros_like(dbtab_scr)

        @pl.when(t == 0)
        def _():
            dkv_ref[...] = jnp.zeros_like(dkv_ref)

        a_val = ld(a_ref)
        sil_a = [_silu_parts(z) for z in ld(za_ref)]
        gated_a = [s[0] * a for s, a in zip(sil_a, a_val)]
        u = [_gelu_parts(z) for z in ld(ub_ref)]
        vv = [_gelu_parts(z) for z in ld(vb_ref)]
        vnorm = [_rms(v[0]) for v in vv]
        gv = gv_ref[...]
        vn = [(n[1] * gv).astype(BF16) for n in vnorm]
        w_cat = cat([ws_ref[g].astype(BF16) for g in range(N_SGU_GROUPS)], 1)
        wT_cat = cat([wsT_ref[g].astype(BF16) for g in range(N_SGU_GROUPS)], 1)
        gmb = [m.astype(BF16) for m in gm]
        by_group = lambda chunk: cat([chunk * gmb[g] for g in range(N_SGU_GROUPS)], 0)
        btab = btab_ref[...]
        mixed = [cat([btab + _nn(w_cat, by_group(vn[h][c, :])) for c in chunks], 0) for h in H]
        sg = [u[h][0] * mixed[h] for h in H]
        sil_b = [_silu_parts(z) for z in ld(zb_ref)]
        gated_b = [sil_b[h][0] * sg[h] for h in H]

        kvv = kv_ref[...].astype(BF16)
        kp = [kvv[:, p] for p in pairs]
        vp = [kvv[:, MEM_WIDTH + pr * LANES:MEM_WIDTH + (pr + 1) * LANES] for pr in range(2)]
        qm = ld(qm_ref)
        qj = {(h, pr, j): (qm[h][:, pairs[pr]] * (hm[j] * 0.125)).astype(BF16) for h in H for pr, j in heads}
        sc = {k: _nt(qj[k], kp[k[1]]) for k in qj}
        ex = {k: jnp.exp(sc[k] - jnp.max(sc[k], axis=1, keepdims=True)) for k in qj}
        prob = {k: ex[k] * (1.0 / jnp.sum(ex[k], axis=1, keepdims=True)) for k in qj}
        probb = {k: prob[k].astype(BF16) for k in qj}
        mo = [cat([sum(_nn(probb[(h, pr, j)], vp[pr]) * hm[j] for j in (0, 1)) for pr in range(2)], 1) for h in H]
        sil_m = [_silu_parts(z) for z in ld(zm_ref)]
        gated_m = [sil_m[h][0] * mo[h] for h in H]

        gated = [cat([gated_a[h], gated_b[h], gated_m[h]], 1).astype(BF16) for h in H]
        wout = wout_ref[...]
        x_in = ld(x_ref)
        x2 = [x_in[h] + _nn(gated[h], wout) for h in H]
        fin = [_rms(z) for z in x2]
        gf = gf_ref[...]
        tgt = ld(t_ref)
        err = [fin[h][1] * gf - tgt[h] for h in H]
        loss_ref[...] += sum(jnp.sum(e * e) for e in err) * (0.5 / D_MODEL)

        dy = [e * (1.0 / D_MODEL) for e in err]
        dgf_ref[...] += sum(jnp.sum(dy[h] * fin[h][1], axis=0, keepdims=True) for h in H)
        gdy = [d * gf for d in dy]
        dx2 = [fin[h][0] * (gdy[h] - fin[h][1] * jnp.mean(gdy[h] * fin[h][1], axis=1, keepdims=True)) for h in H]
        for h in H:
            dx2_ref[rows[h], :] = dx2[h]
        dx2b = [d.astype(BF16) for d in dx2]
        dgated = [_nt(d, wout) for d in dx2b]
        dwout_ref[...] += _tn(cat(gated, 0), cat(dx2b, 0))
        dga = [d[:, 0:ATTN_WIDTH] for d in dgated]
        dgb = [d[:, ATTN_WIDTH:ATTN_WIDTH + SGU_WIDTH] for d in dgated]
        dgm = [d[:, ATTN_WIDTH + SGU_WIDTH:] for d in dgated]

        for h in H:
            da_ref[rows[h], :] = dga[h] * sil_a[h][0]
        dza = [dga[h] * a_val[h] * sil_a[h][1] for h in H]

        dsg = [dgb[h] * sil_b[h][0] for h in H]
        dzb = [dgb[h] * sg[h] * sil_b[h][1] for h in H]
        dub = [dsg[h] * mixed[h] * u[h][1] for h in H]
        dmixed = [dsg[h] * u[h][0] for h in H]
        dmixed_b = [d.astype(BF16) for d in dmixed]
        dvn = [cat([_nn(wT_cat, by_group(dmixed_b[h][c, :])) for c in chunks], 0) for h in H]
        for g in range(N_SGU_GROUPS):
            dws_ref[g] += sum(_nt((dmixed[h][c, :] * gm[g]).astype(BF16), vn[h][c, :]) for h in H for c in chunks)
        dbtab_scr[...] += sum(dmixed[h][c, :] for h in H for c in chunks)
        dgv_ref[...] += sum(jnp.sum(dvn[h] * vnorm[h][1], axis=0, keepdims=True) for h in H)
        tv = [d * gv for d in dvn]
        dvv = [vnorm[h][0] * (tv[h] - vnorm[h][1] * jnp.mean(tv[h] * vnorm[h][1], axis=1, keepdims=True)) for h in H]
        dvb = [dvv[h] * vv[h][1] for h in H]

        dmo = [dgm[h] * sil_m[h][0] for h in H]
        dzm = [dgm[h] * mo[h] * sil_m[h][1] for h in H]
        dmoj = {(h, pr, j): (dmo[h][:, pairs[pr]] * hm[j]).astype(BF16) for h in H for pr, j in heads}
        dp = {k: _nt(dmoj[k], vp[k[1]]) for k in qj}
        ds = {k: (prob[k] * (dp[k] - jnp.sum(dp[k] * prob[k], axis=1, keepdims=True))).astype(BF16) for k in qj}
        dqm = [cat([sum(_nn(ds[(h, pr, j)], kp[pr]) * (hm[j] * 0.125) for j in (0, 1)) for pr in range(2)], 1)
               for h in H]
        every = lambda tbl, pr: cat([tbl[(h, pr, j)] for h in H for j in (0, 1)], 0)
        dk = [_tn(every(ds, pr), every(qj, pr)) for pr in range(2)]
        dv = [_tn(every(probb, pr), every(dmoj, pr)) for pr in range(2)]
        dkv_ref[...] += cat(dk + dv, 1)

        for h in H:
            drest_ref[rows[h], :] = cat([dza[h], dub[h], dvb[h], dzb[h], dqm[h], dzm[h]], 1).astype(BF16)

        @pl.when(last)
        def _():
            lane = lax.broadcasted_iota(jnp.int32, (1, LANES), 1)
            dbt = dbtab_scr[...]
            out = jnp.zeros((SGU_CHUNK, LANES), F32)
            for g in range(N_SGU_GROUPS):
                out = out + jnp.where(lane == g, jnp.sum(dbt * gm[g], axis=1, keepdims=True), 0.0)
            dbs_ref[...] = out
            for r0 in range(0, D_MODEL, SGU_CHUNK):
                k, row = divmod(r0, D_MODEL // N_CHIPS)
                dwout_bf_ref[k, row:row + SGU_CHUNK, :] = dwout_ref[r0:r0 + SGU_CHUNK, :].astype(BF16)

    tile = lambda w, cb: pl.BlockSpec((tm, w), lambda b, t, cb=cb: (b * nt + t, cb))
    const = lambda shape: pl.BlockSpec(shape, lambda b, t, n=len(shape): (0,) * n)
    return pl.pallas_call(
        body, grid=(B, nt),
        in_specs=[tile(D_MODEL, 0), tile(D_MODEL, 0), tile(ATTN_WIDTH, 0),
                  tile(ATTN_WIDTH, 3),
                  tile(SGU_WIDTH, 8), tile(SGU_WIDTH, 9), tile(SGU_WIDTH, 10),
                  tile(MEM_WIDTH, 11), tile(MEM_WIDTH, 12),
                  pl.BlockSpec((N_MEM, 2 * MEM_WIDTH), lambda b, t: (b, 0)),
                  const((N_SGU_GROUPS, SGU_CHUNK, SGU_CHUNK)), const((N_SGU_GROUPS, SGU_CHUNK, SGU_CHUNK)),
                  const((SGU_CHUNK, SGU_WIDTH)), const((1, SGU_WIDTH)),
                  const((D_MODEL, D_MODEL)), const((1, D_MODEL))],
        out_specs=[tile(D_MODEL, 0), tile(ATTN_WIDTH, 0), tile(REST_COLS, 0),
                   const((8, LANES)), const((N_CHIPS, D_MODEL // N_CHIPS, D_MODEL)),
                   const((N_SGU_GROUPS, SGU_CHUNK, SGU_CHUNK)), const((SGU_CHUNK, LANES)),
                   const((1, SGU_WIDTH)), const((1, D_MODEL)),
                   pl.BlockSpec((N_MEM, 2 * MEM_WIDTH), lambda b, t: (b, 0))],
        out_shape=[jax.ShapeDtypeStruct((T, D_MODEL), F32), jax.ShapeDtypeStruct((T, ATTN_WIDTH), F32),
                   jax.ShapeDtypeStruct((T, REST_COLS), BF16),
                   jax.ShapeDtypeStruct((8, LANES), F32),
                   jax.ShapeDtypeStruct((N_CHIPS, D_MODEL // N_CHIPS, D_MODEL), BF16),
                   jax.ShapeDtypeStruct((N_SGU_GROUPS, SGU_CHUNK, SGU_CHUNK), F32),
                   jax.ShapeDtypeStruct((SGU_CHUNK, LANES), F32),
                   jax.ShapeDtypeStruct((1, SGU_WIDTH), F32), jax.ShapeDtypeStruct((1, D_MODEL), F32),
                   jax.ShapeDtypeStruct((B * N_MEM, 2 * MEM_WIDTH), F32)],
        scratch_shapes=[pltpu.VMEM((SGU_CHUNK, SGU_WIDTH), F32), pltpu.VMEM((D_MODEL, D_MODEL), F32)],
        compiler_params=_params(("arbitrary", "arbitrary"), vmem=VMEM_LIMIT + 2 * 1024 * 1024), name="mid")(
            x2d, t2d, a, proj, proj, proj, proj, proj, proj, kv, w_s, w_sT, b_tab, g_v, w_out, g_final)


def _inproj_bwd_dx(dq, dk, dv, drest, x2d, dx2, g_norm, w_in_t, after=()):
    T = x2d.shape[0]
    tm = 512
    W = ATTN_WIDTH

    def body(dq_ref, dk_ref, dv_ref, dr_ref, x_ref, dx2_ref, g_ref, w_ref, *rest):
        gx_ref, dg_ref = rest[-2:]

        @pl.when(pl.program_id(0) == 0)
        def _():
            dg_ref[...] = jnp.zeros_like(dg_ref)

        halves = [pl.ds(h * (tm // 2), tm // 2) for h in (0, 1)]
        dh = [(_nn(dq_ref[r, :], w_ref[0:W, :]) + _nn(dk_ref[r, :], w_ref[W:2 * W, :])
               + _nn(dv_ref[r, :], w_ref[2 * W:3 * W, :]) + _nn(dr_ref[r, :], w_ref[QKV_COLS:IN_COLS, :]))
              for r in halves]
        nrm = [_rms(x_ref[r, :]) for r in halves]
        dg_ref[...] += sum(jnp.sum(d * n[1], axis=0, keepdims=True) for d, n in zip(dh, nrm))
        g = g_ref[...]
        for r, d, (rstd, xh) in zip(halves, dh, nrm):
            th = d * g
            gx_ref[r, :] = rstd * (th - xh * jnp.mean(th * xh, axis=1, keepdims=True)) + dx2_ref[r, :]

    tile = lambda w: pl.BlockSpec((tm, w), lambda i: (i, 0))
    return pl.pallas_call(
        body, grid=(T // tm,),
        in_specs=[tile(W), tile(W), tile(W), tile(REST_COLS), tile(D_MODEL), tile(D_MODEL),
                  pl.BlockSpec((1, D_MODEL), lambda i: (0, 0)),
                  pl.BlockSpec((IN_COLS, D_MODEL), lambda i: (0, 0))] + _after(after),
        out_specs=[tile(D_MODEL), pl.BlockSpec((1, D_MODEL), lambda i: (0, 0))],
        out_shape=[jax.ShapeDtypeStruct((T, D_MODEL), F32), jax.ShapeDtypeStruct((1, D_MODEL), F32)],
        compiler_params=_params(("arbitrary",)), name="inproj_bwd_dx")(
            dq, dk, dv, drest, x2d, dx2, g_norm, w_in_t, *after)


def _inproj_bwd_dw(dq, dk, dv, drest, x2d, g_norm, reduce=False):
    T = x2d.shape[0]
    tm = 512
    nt = T // tm
    W = ATTN_WIDTH
    shard = IN_COLS // N_CHIPS
    half = shard // 2
    row_block = 32

    def body(dq_ref, dk_ref, dv_ref, dr_ref, x_ref, g_ref, *rest):
        if reduce:
            sends_out, own_out, acc, ras, narrow, sends, own, s_sem, r_sem, put_sem = rest
            x, y, c, chip, peers, peer_chip = _place()
            sib = (x, y, 1 - c)

            def part(k, cc, r0):
                return acc.at[pl.ds(pl.multiple_of(k * shard + cc * half + r0, 8), row_block), :]

            def swap_win(k):
                return _remote(narrow.at[k], ras.at[k], s_sem.at[k], r_sem.at[k], sib)

            def put_send(m):
                return pltpu.make_async_copy(sends.at[m], sends_out.at[m], put_sem.at[m])

            put_own = pltpu.make_async_copy(own, own_out, put_sem.at[N_CHIPS - 1])
        else:
            acc = rest[0]

        @pl.when(pl.program_id(0) == 0)
        def _():
            acc[...] = jnp.zeros_like(acc)

        _, xh = _rms(x_ref[...])
        h = (xh * g_ref[...]).astype(BF16)
        acc[0:W, :] += _tn(dq_ref[...], h)
        acc[W:2 * W, :] += _tn(dk_ref[...], h)
        acc[2 * W:3 * W, :] += _tn(dv_ref[...], h)
        acc[QKV_COLS:IN_COLS, :] += _tn(dr_ref[...], h)

        if reduce:
            @pl.when(pl.program_id(0) == nt - 1)
            def _():
                for k in range(N_CHIPS):
                    def to_bf16(i, carry, k=k):
                        r0 = pl.multiple_of(i * row_block, row_block)
                        narrow[k, pl.ds(r0, row_block), :] = part(k, 1 - c, r0)[...].astype(BF16)
                        return carry
                    lax.fori_loop(0, half // row_block, to_bf16, 0)
                    swap_win(k).start()

                def chip_sum(k, r0):
                    return part(k, c, r0)[...] + ras[k, pl.ds(r0, row_block), :].astype(F32)

                for k in range(N_CHIPS):
                    swap_win(k).wait_recv()

                    @pl.when(chip == k)
                    def _(k=k):
                        def mine(i, carry):
                            r0 = pl.multiple_of(i * row_block, row_block)
                            own[pl.ds(r0, row_block), :] = chip_sum(k, r0)
                            return carry
                        lax.fori_loop(0, half // row_block, mine, 0)
                        put_own.start()

                    @pl.when(chip != k)
                    def _(k=k):
                        def other(i, carry):
                            r0 = pl.multiple_of(i * row_block, row_block)
                            sends[(k ^ chip) - 1, pl.ds(r0, row_block), :] = chip_sum(k, r0).astype(BF16)
                            return carry
                        lax.fori_loop(0, half // row_block, other, 0)
                        put_send((k ^ chip) - 1).start()
                for k in range(N_CHIPS):
                    swap_win(k).wait_send()
                for m in range(N_CHIPS - 1):
                    put_send(m).wait()
                put_own.wait()

    tile = lambda w: pl.BlockSpec((tm, w), lambda i: (i, 0))
    vmem = pl.BlockSpec(memory_space=pltpu.VMEM)
    in_specs = [tile(W), tile(W), tile(W), tile(REST_COLS), tile(D_MODEL), pl.BlockSpec((1, D_MODEL), lambda i: (0, 0))]
    if not reduce:
        return pl.pallas_call(
            body, grid=(nt,), in_specs=in_specs,
            out_specs=pl.BlockSpec((IN_COLS, D_MODEL), lambda i: (0, 0)),
            out_shape=jax.ShapeDtypeStruct((IN_COLS, D_MODEL), F32),
            compiler_params=_params(("arbitrary",)), name="inproj_bwd_dw")(dq, dk, dv, drest, x2d, g_norm)
    quarters = pltpu.VMEM((N_CHIPS, half, D_MODEL), BF16)
    far = pl.BlockSpec(memory_space=pl.ANY)
    return pl.pallas_call(
        body, grid=(nt,), in_specs=in_specs, out_specs=[far] * 2,
        out_shape=[jax.ShapeDtypeStruct((N_CHIPS - 1, half, D_MODEL), BF16),
                   jax.ShapeDtypeStruct((half, D_MODEL), F32)],
        scratch_shapes=[pltpu.VMEM((IN_COLS, D_MODEL), F32), quarters, quarters,
                        pltpu.VMEM((N_CHIPS - 1, half, D_MODEL), BF16), pltpu.VMEM((half, D_MODEL), F32),
                        pltpu.SemaphoreType.DMA((N_CHIPS,)), pltpu.SemaphoreType.DMA((N_CHIPS,)),
                        pltpu.SemaphoreType.DMA((N_CHIPS,))],
        compiler_params=_params(("arbitrary",)), name="inproj_bwd_dw_reduce")(dq, dk, dv, drest, x2d, g_norm)


def _adamw_update(w, g, m, v):
    nm = ADAM_B1 * m + (1.0 - ADAM_B1) * g
    nv = ADAM_B2 * v + (1.0 - ADAM_B2) * (g * g)
    m_hat = nm / (1.0 - ADAM_B1 ** ADAM_STEP)
    v_hat = nv / (1.0 - ADAM_B2 ** ADAM_STEP)
    return -ADAM_LR * (m_hat / (jnp.sqrt(v_hat) + ADAM_EPS) + ADAM_WD * w), nm, nv


def _adamw_all(g_packed, ws, ms, vs, large):
    n, nl = len(ws), len(large)
    row_block = 8
    chunk_bytes = 512 * 1024

    def chunk_rows(w):
        R, C = w.shape
        return max(r for r in range(8, R + 1, 8) if R % r == 0 and r * C * 4 <= chunk_bytes)

    jobs = [(b, r0, chunk_rows(four[0])) for b, four in enumerate(large)
            for r0 in range(0, four[0].shape[0], chunk_rows(four[0]))]

    def body(*refs):
        g_ref = refs[0]
        w_refs, m_refs, v_refs = refs[1:1 + n], refs[1 + n:1 + 2 * n], refs[1 + 2 * n:1 + 3 * n]
        far_in = refs[1 + 3 * n:1 + 3 * n + 4 * nl]
        outs = refs[1 + 3 * n + 4 * nl:1 + 7 * n + 4 * nl]
        far_out = refs[1 + 7 * n + 4 * nl:1 + 7 * n + 8 * nl]
        loss_ref = refs[1 + 7 * n + 8 * nl]
        scr = refs[2 + 7 * n + 8 * nl:]
        in_scr, out_scr, in_sem, out_sem = scr[:4 * nl], scr[4 * nl:7 * nl], scr[7 * nl], scr[7 * nl + 1]

        def read(j, k):
            b, r0, rows = jobs[j]
            blk = pl.ds(r0, rows)
            return pltpu.make_async_copy(far_in[4 * b + k].at[blk, :], in_scr[4 * b + k].at[blk, :],
                                         in_sem.at[4 * j + k])

        def write(j, k):
            b, r0, rows = jobs[j]
            blk = pl.ds(r0, rows)
            src = in_scr[4 * b + 1] if k == 0 else out_scr[3 * b + k - 1]
            return pltpu.make_async_copy(src.at[blk, :], far_out[4 * b + k].at[blk, :], out_sem.at[4 * j + k])

        for j in range(len(jobs)):
            for k in range(4):
                read(j, k).start()
        off = 0
        for i, (_, used, padded) in enumerate(_SMALL_PARTS[:n]):
            g = g_ref[off:off + used, :]
            delta, nm, nv = _adamw_update(w_refs[i][...], g, m_refs[i][...], v_refs[i][...])
            outs[4 * i][...], outs[4 * i + 1][...], outs[4 * i + 2][...], outs[4 * i + 3][...] = g, delta, nm, nv
            off += padded
        loss_ref[...] = g_ref[_LOSS_ROW:_LOSS_ROW + 1, 0:1]
        for j, (b, r0, rows) in enumerate(jobs):
            for k in range(4):
                read(j, k).wait()

            def update(i, carry, b=b, r0=r0):
                blk = pl.ds(pl.multiple_of(r0 + i * row_block, row_block), row_block)
                w, g, m, v = [in_scr[4 * b + k][blk, :] for k in range(4)]
                for k, val in enumerate(_adamw_update(w, g, m, v)):
                    out_scr[3 * b + k][blk, :] = val
                return carry
            lax.fori_loop(0, rows // row_block, update, 0, unroll=True)
            for k in range(4):
                write(j, k).start()
        for j in range(len(jobs)):
            for k in range(4):
                write(j, k).wait()

    vmem = pl.BlockSpec(memory_space=pltpu.VMEM)
    far = pl.BlockSpec(memory_space=pl.ANY)
    flat = [a for four in large for a in four]
    outs = pl.pallas_call(
        body, in_specs=[vmem] * (1 + 3 * n) + [far] * (4 * nl),
        out_specs=[vmem] * (4 * n) + [far] * (4 * nl) + [vmem],
        out_shape=[jax.ShapeDtypeStruct(w.shape, F32) for w in ws for _ in range(4)]
        + [jax.ShapeDtypeStruct(four[0].shape, F32) for four in large for _ in range(4)]
        + [jax.ShapeDtypeStruct((1, 1), F32)],
        scratch_shapes=[pltpu.VMEM(a.shape, F32) for a in flat]
        + [pltpu.VMEM(four[0].shape, F32) for four in large for _ in range(3)]
        + [pltpu.SemaphoreType.DMA((4 * len(jobs),)), pltpu.SemaphoreType.DMA((4 * len(jobs),))],
        compiler_params=_params(), name="adamw_all")(g_packed, *ws, *ms, *vs, *flat)
    return ([outs[4 * i:4 * i + 4] for i in range(n)], [outs[4 * (n + i):4 * (n + i) + 4] for i in range(nl)],
            outs[4 * (n + nl)])


def _place():
    x, y, c = lax.axis_index("x"), lax.axis_index("y"), lax.axis_index("c")
    chip = 2 * x + y
    peers = [(x, 1 - y), (1 - x, y), (1 - x, 1 - y)]
    peer_chip = [2 * px + py for px, py in peers]
    return x, y, c, chip, peers, peer_chip


def _remote(src, dst, send_sem, recv_sem, dev):
    return pltpu.make_async_remote_copy(src_ref=src, dst_ref=dst, send_sem=send_sem, recv_sem=recv_sem,
                                        device_id=dev, device_id_type=MESH)


def _ag_weights(weights, late=()):
    nw, nl = len(weights), len(late)
    pieces = 2

    def body(*refs):
        srcs, late_srcs = refs[:nw], refs[nw:nw + nl]
        outs, late_bf, late_land = (refs[nw + nl:2 * nw + nl], refs[2 * nw + nl:2 * nw + 2 * nl],
                                    refs[2 * nw + 2 * nl:2 * nw + 3 * nl])
        scr = refs[2 * nw + 3 * nl:]
        wide, narrow = scr[:nw], scr[nw:2 * nw]
        late_scr = scr[2 * nw:2 * nw + nl]
        in_sem, put_sem, late_sem, s_ici, r_ici, s_d2d, r_d2d = scr[2 * nw + nl:]
        x, y, c = lax.axis_index("x"), lax.axis_index("y"), lax.axis_index("c")
        chip = 2 * x + y
        sib = (x, y, 1 - c)
        first = ((x + 1 - c) % 2, (y + c) % 2)
        second = ((x + c) % 2, (y + 1 - c) % 2)
        first_chip, second_chip = 2 * first[0] + first[1], 2 * second[0] + second[1]
        diag_chip = 3 - chip

        parts = [(w, out, pc) for w, out in enumerate(outs) for pc in range(pieces)]

        def rows_of(out, cc, pc):
            rows = out.shape[1] // 2 // pieces
            return pl.ds(pl.multiple_of((cc * pieces + pc) * rows, 16), rows)

        def piece(out, k, cc, pc):
            return out.at[k, rows_of(out, cc, pc), :]

        def ici(w, slot, out, k, dev, pc, src=None):
            blk, sem = piece(out, k, c, pc), (nw * slot + w) * pieces + pc
            return _remote(blk if src is None else src, blk, s_ici.at[sem], r_ici.at[sem], (dev[0], dev[1], c))

        def d2d(w, slot, out, k, cc, pc):
            blk, sem = piece(out, k, cc, pc), (nw * slot + w) * pieces + pc
            return _remote(blk, blk, s_d2d.at[sem], r_d2d.at[sem], sib)

        def read(w, cc, pc):
            rows = rows_of(outs[w], cc, pc)
            return pltpu.make_async_copy(srcs[w].at[rows, :], wide[w].at[rows, :],
                                         in_sem.at[(w * 2 + cc) * pieces + pc])

        order = [(w, cc, pc) for cc in (0, 1) for w in range(nw) for pc in range(pieces)]
        for w, cc, pc in order:
            read(w, (c + cc) % 2, pc).start()
        sent = []
        for w, cc, pc in order[:nw * pieces]:
            rows = rows_of(outs[w], c, pc)
            read(w, c, pc).wait()
            narrow[w][rows, :] = wide[w][rows, :].astype(BF16)
            for slot, dev in enumerate((first, second)):
                sent.append(ici(w, slot, outs[w], chip, dev, pc, src=narrow[w].at[rows, :]))
                sent[-1].start()
        late_puts = []
        for i, (src, scr_bf, bf, land) in enumerate(zip(late_srcs, late_scr, late_bf, late_land)):
            scr_bf[...] = src[...].astype(BF16)
            for k, dst in enumerate([bf] + [land.at[s] for s in range(N_CHIPS)]):
                late_puts.append(pltpu.make_async_copy(scr_bf, dst, late_sem.at[i * (N_CHIPS + 1) + k]))
                late_puts[-1].start()
        for w, cc, pc in order[nw * pieces:]:
            rows = rows_of(outs[w], 1 - c, pc)
            read(w, 1 - c, pc).wait()
            narrow[w][rows, :] = wide[w][rows, :].astype(BF16)
        puts = [pltpu.make_async_copy(narrow[w], outs[w].at[chip], put_sem.at[w]) for w in range(nw)]
        for cp in puts:
            cp.start()
        for slot, k, dev in ((0, first_chip, first), (1, second_chip, second), (2, diag_chip, second)):
            for w, out, pc in parts:
                ici(w, slot, out, k, dev, pc).wait_recv()
                if slot == 0:
                    sent.append(ici(w, 2, out, k, second, pc))
                    sent[-1].start()
                sent.append(d2d(w, slot, out, k, c, pc))
                sent[-1].start()
        for slot, k in ((0, second_chip), (1, first_chip), (2, diag_chip)):
            for w, out, pc in parts:
                d2d(w, slot, out, k, 1 - c, pc).wait_recv()
        for cp in sent:
            cp.wait_send()
        for cp in puts + late_puts:
            cp.wait()

    vmem = pl.BlockSpec(memory_space=pltpu.VMEM)
    far = pl.BlockSpec(memory_space=pl.ANY)
    outs = pl.pallas_call(
        body,
        out_shape=[jax.ShapeDtypeStruct((N_CHIPS,) + w.shape, BF16) for w in weights]
        + [jax.ShapeDtypeStruct(w.shape, BF16) for w in late]
        + [jax.ShapeDtypeStruct((N_CHIPS,) + w.shape, BF16) for w in late],
        in_specs=[far] * nw + [vmem] * nl, out_specs=[far] * (nw + 2 * nl),
        scratch_shapes=[pltpu.VMEM(w.shape, F32) for w in weights] + [pltpu.VMEM(w.shape, BF16) for w in weights]
        + [pltpu.VMEM(w.shape, BF16) for w in late]
        + [pltpu.SemaphoreType.DMA((2 * nw * pieces,)), pltpu.SemaphoreType.DMA((nw,)),
           pltpu.SemaphoreType.DMA((nl * (N_CHIPS + 1),))]
        + [pltpu.SemaphoreType.DMA((3 * nw * pieces,))] * 4,
        compiler_params=pltpu.CompilerParams(vmem_limit_bytes=VMEM_LIMIT), name="ag_weights")(*weights, *late)
    return outs[:nw], outs[nw:nw + nl], outs[nw + nl:]


_HBM = pl.BlockSpec(memory_space=pltpu.HBM)
_SEM = pl.BlockSpec(memory_space=pltpu.SEMAPHORE)
_ANY = pl.BlockSpec(memory_space=pl.ANY)
_DATAFLOW = pltpu.SideEffectType.DATAFLOW_SIDE_EFFECTING


def _in_hbm(a):
    return pltpu.with_memory_space_constraint(a, pltpu.HBM)


_PEERS_OF = {"gather": 3, "scatter": 3, "direct": 7}


def _exchange_copies(mode, srcs, lands, send_sems, recv_sems):
    nw = len(srcs)
    x, y, c, chip, peers, peer_chip = _place()
    pairs = []
    if mode == "direct":
        targets = [((x, y), chip, 1)] + [(p, k, d) for p, k in zip(peers, peer_chip) for d in (0, 1)]
        for r, ((px, py), k, d) in enumerate(targets):
            for w in range(nw):
                sems = (send_sems.at[nw * r + w], recv_sems.at[nw * r + w], (px, py, (c + d) % 2))
                share = srcs[w].at[k if srcs[w].shape[0] > 1 else 0]
                pairs.append((_remote(share, lands[w].at[r], *sems),) * 2)
        return pairs
    gather = mode == "gather"
    for m, (px, py) in enumerate(peers):
        for w in range(nw):
            sems = (send_sems.at[nw * m + w], recv_sems.at[nw * m + w], (px, py, c))
            if gather:
                pairs.append((_remote(srcs[w], lands[w].at[chip], *sems),
                              _remote(srcs[w], lands[w].at[peer_chip[m]], *sems)))
            else:
                pairs.append((_remote(srcs[w].at[m], lands[w].at[m], *sems),) * 2)
    return pairs


def _exchange_start(mode, srcs, after, name, lands=None):
    nw = len(srcs)
    n_copies = _PEERS_OF[mode] * nw

    after = tuple(after)

    def body(*refs):
        send_sems, recv_sems = refs[2 * nw + len(after)], refs[2 * nw + len(after) + 1]
        for start, _ in _exchange_copies(mode, refs[:nw], refs[nw:2 * nw], send_sems, recv_sems):
            start.start()
        refs[-1][...] = jnp.zeros_like(refs[-1])

    if lands is None:
        shape = {"gather": lambda s: (N_CHIPS,) + s.shape, "scatter": lambda s: s.shape,
                 "direct": lambda s: (_PEERS_OF["direct"],) + s.shape[1:]}[mode]
        lands = [lax.empty(shape(s), s.dtype) for s in srcs]
    lands = [_in_hbm(l) for l in lands]
    return pl.pallas_call(
        body, name=name,
        out_shape=(pltpu.SemaphoreType.DMA((n_copies,)), pltpu.SemaphoreType.DMA((n_copies,)))
        + tuple(pltpu.HBM(s.shape, s.dtype) for s in srcs)
        + tuple(pltpu.HBM(l.shape, l.dtype) for l in lands)
        + (jax.ShapeDtypeStruct((8, LANES), F32),),
        in_specs=[_HBM] * (2 * nw) + [_ANY] * len(after),
        out_specs=(_SEM, _SEM) + (_HBM,) * (2 * nw) + (pl.BlockSpec(memory_space=pltpu.VMEM),),
        input_output_aliases={i: 2 + i for i in range(2 * nw)},
        compiler_params=pltpu.CompilerParams(has_side_effects=_DATAFLOW),
    )(*[_in_hbm(s) for s in srcs], *lands, *after)


def _exchange_wait(mode, started, after, name):
    nw = (len(started) - 3) // 2
    send_sems, recv_sems = started[0], started[1]
    thru = started[2:2 + 2 * nw]

    def body(*refs):
        for _, arrival in _exchange_copies(mode, refs[:nw], refs[nw:2 * nw], refs[2 * nw], refs[2 * nw + 1]):
            arrival.wait_send()
            arrival.wait_recv()

    outs = pl.pallas_call(
        body, name=name,
        out_shape=tuple(pltpu.HBM(t.shape, t.dtype) for t in thru),
        in_specs=[_HBM] * (2 * nw) + [_SEM, _SEM, _ANY], out_specs=(_HBM,) * (2 * nw),
        input_output_aliases={i: i for i in range(2 * nw)},
        compiler_params=pltpu.CompilerParams(has_side_effects=_DATAFLOW),
    )(*thru, send_sems, recv_sems, after)
    return outs[:nw], outs[nw:]


def _reduce_last(owns, landed, g_small, direct_srcs, direct_landed, spread_row0):
    ns, nd = len(owns), len(direct_srcs)
    halves = [o.shape[0] for o in owns]
    row_block = 16
    spread_rows = direct_srcs[-1].shape[1]
    rest0, rest1 = spread_row0, SMALL_ROWS - spread_row0 - spread_rows
    hs = (rest0 + rest1) // 2
    jobs = [(w, p * (halves[w] // 2), halves[w] // 2) for w in range(ns) for p in range(2)]
    n_swaps = len(jobs)
    jobs += [(ns + d, 0, direct_srcs[d].shape[1]) for d in range(nd)]

    def body(*refs):
        own_refs, land_refs, gsm_ref = refs[:ns], refs[ns:2 * ns], refs[2 * ns]
        dsrc_refs, dland_refs = refs[2 * ns + 1:2 * ns + 1 + nd], refs[2 * ns + 1 + nd:2 * ns + 1 + 2 * nd]
        n_in = 2 * ns + 1 + 2 * nd
        out_refs, osm_ref = refs[n_in:n_in + ns + nd - 1], refs[n_in + ns + nd - 1]
        scr = refs[n_in + ns + nd:]
        own_scr, land_scr, dsrc_scr, dland_scr = (scr[:ns], scr[ns:2 * ns], scr[2 * ns:2 * ns + nd],
                                                  scr[2 * ns + nd:2 * ns + 2 * nd])
        res = scr[2 * ns + 2 * nd:3 * ns + 3 * nd - 1]
        o_rest, ra_sm, p_sm, put_sem, in_sem, s_sem, r_sem, sm_s, sm_r = scr[3 * ns + 3 * nd - 1:]
        x, y, c, chip, peers, peer_chip = _place()
        sib = (x, y, 1 - c)
        half = lambda cc: pl.ds(pl.multiple_of(cc * hs, 8), hs)
        sm_a = _remote(gsm_ref.at[half(1 - c), :], ra_sm, sm_s.at[0], sm_r.at[0], sib)
        sm_a.start()
        swaps = [sm_a]

        def reads(j):
            w, r0, n = jobs[j]
            rows = pl.ds(r0, n)
            if w < ns:
                pairs = [(own_refs[w].at[rows, :], own_scr[w].at[rows, :]),
                         (land_refs[w].at[:, rows, :], land_scr[w].at[:, rows, :])]
            else:
                share = dsrc_refs[w - ns].at[chip if w < ns + nd - 1 else 0]
                pairs = [(share, dsrc_scr[w - ns]), (dland_refs[w - ns], dland_scr[w - ns])]
            return [pltpu.make_async_copy(s, d, in_sem.at[2 * j + i]) for i, (s, d) in enumerate(pairs)]

        for j in range(len(jobs)):
            for cp in reads(j):
                cp.start()
        sm_a.wait_recv()
        p_sm[chip] = gsm_ref[half(c), :] + ra_sm[...]
        for m, (px, py) in enumerate(peers):
            swaps.append(_remote(p_sm.at[chip], p_sm.at[chip], sm_s.at[1 + m], sm_r.at[1 + m], (px, py, c)))
            swaps[-1].start()

        def mine_of(j):
            w, r0, n = jobs[j]
            return out_refs[w].at[pl.ds(pl.multiple_of(c * halves[w] + r0, 8), n), :]

        def sum_of(j):
            w, r0, n = jobs[j]
            return res[w].at[pl.ds(r0, n), :]

        puts = []
        for j, (w, r0, n) in enumerate(jobs):
            for cp in reads(j):
                cp.wait()

            def total(i, carry, w=w, r0=r0):
                rr = pl.multiple_of(r0 + i * row_block, row_block)
                blk = pl.ds(rr, row_block)
                if w < ns:
                    acc = own_scr[w][blk, :]
                    for m in range(_PEERS_OF["scatter"]):
                        acc = acc + land_scr[w][m, blk, :].astype(F32)
                    res[w][blk, :] = acc
                else:
                    theirs = lambda r: dland_scr[w - ns][r, blk, :].astype(F32)
                    acc = ((dsrc_scr[w - ns][blk, :].astype(F32) + theirs(0)) + (theirs(1) + theirs(2))) + (
                        (theirs(3) + theirs(4)) + (theirs(5) + theirs(6)))
                    if w < ns + nd - 1:
                        res[w][blk, :] = acc
                    else:
                        osm_ref[pl.ds(pl.multiple_of(spread_row0 + rr, 8), row_block), :] = acc
                return carry
            lax.fori_loop(0, n // row_block, total, 0)
            if w < ns:
                puts.append(pltpu.make_async_copy(sum_of(j), mine_of(j), put_sem.at[j]))
                swaps.append(_remote(sum_of(j), mine_of(j), s_sem.at[j], r_sem.at[j], sib))
                swaps[-1].start()
            elif w < ns + nd - 1:
                puts.append(pltpu.make_async_copy(res[w], out_refs[w], put_sem.at[j]))
            if w < ns + nd - 1:
                puts[-1].start()
        for m, (px, py) in enumerate(peers):
            _remote(p_sm.at[chip], p_sm.at[peer_chip[m]], sm_s.at[1 + m], sm_r.at[1 + m], (px, py, c)).wait_recv()
        o_rest[half(c), :] = (p_sm[0] + p_sm[1]) + (p_sm[2] + p_sm[3])
        swaps.append(_remote(o_rest.at[half(c), :], o_rest.at[half(c), :], sm_s.at[4], sm_r.at[4], sib))
        swaps[-1].start()
        for j, (w, r0, n) in enumerate(jobs[:n_swaps]):
            theirs = out_refs[w].at[pl.ds(pl.multiple_of((1 - c) * halves[w] + r0, 8), n), :]
            _remote(sum_of(j), theirs, s_sem.at[j], r_sem.at[j], sib).wait_recv()
        _remote(o_rest.at[half(1 - c), :], o_rest.at[half(1 - c), :], sm_s.at[4], sm_r.at[4], sib).wait_recv()
        osm_ref[0:rest0, :] = o_rest[0:rest0, :]
        osm_ref[SMALL_ROWS - rest1:SMALL_ROWS, :] = o_rest[rest0:rest0 + rest1, :]
        for cp in swaps:
            cp.wait_send()
        for cp in puts:
            cp.wait()

    vmem = pl.BlockSpec(memory_space=pltpu.VMEM)
    far = [pl.BlockSpec(memory_space=pl.ANY)]
    return pl.pallas_call(
        body, out_shape=[jax.ShapeDtypeStruct((2 * o.shape[0], o.shape[1]), F32) for o in owns]
        + [jax.ShapeDtypeStruct(s.shape[1:], F32) for s in direct_srcs[:-1]]
        + [jax.ShapeDtypeStruct((SMALL_ROWS, LANES), F32)],
        in_specs=far * (2 * ns) + [vmem] + far * (2 * nd), out_specs=far * (ns + nd - 1) + [vmem],
        scratch_shapes=[pltpu.VMEM(o.shape, o.dtype) for o in owns] + [pltpu.VMEM(l.shape, l.dtype) for l in landed]
        + [pltpu.VMEM(s.shape[1:], s.dtype) for s in direct_srcs]
        + [pltpu.VMEM(l.shape, l.dtype) for l in direct_landed]
        + [pltpu.VMEM(o.shape, F32) for o in owns] + [pltpu.VMEM(s.shape[1:], F32) for s in direct_srcs[:-1]]
        + [pltpu.VMEM((2 * hs, LANES), F32), pltpu.VMEM((hs, LANES), F32), pltpu.VMEM((N_CHIPS, hs, LANES), F32),
           pltpu.SemaphoreType.DMA((len(jobs),)), pltpu.SemaphoreType.DMA((2 * len(jobs),)),
           pltpu.SemaphoreType.DMA((n_swaps,)), pltpu.SemaphoreType.DMA((n_swaps,)),
           pltpu.SemaphoreType.DMA((5,)), pltpu.SemaphoreType.DMA((5,))],
        compiler_params=pltpu.CompilerParams(vmem_limit_bytes=VMEM_LIMIT),
        name="reduce_last")(*owns, *landed, g_small, *direct_srcs, *direct_landed)


_SMALL_PARTS = (("g_norm", 8, 8), ("w_s", 512, 512), ("b_s", 4, 8), ("g_v", 2, 8), ("g_mem", 8, 8),
                ("g_final", 8, 8), ("loss", 8, 8))
_LOSS_ROW = SMALL_ROWS - 8
_W_S_ROW = 8
assert sum(p for _, _, p in _SMALL_PARTS) == SMALL_ROWS and _SMALL_PARTS[1][0] == "w_s"


def _pack_small(parts, loss_block):
    rows = []
    parts = dict(parts, loss=loss_block)
    for name, used, padded in _SMALL_PARTS:
        if name == "w_s":
            continue
        p = parts[name].reshape(used, LANES)
        if padded > used:
            p = jnp.pad(p, ((0, padded - used), (0, 0)))
        rows.append(p)
    return jnp.concatenate(rows, axis=0)


def _local_step(x, mem, target, g_norm, w_in, w_s, b_s, g_v, g_mem, late_weights, g_final,
                fwd_token=None, on_late=None, on_dw=None):
    B, S, _ = x.shape
    x2d = x.reshape(B * S, D_MODEL)
    t2d = target.reshape(B * S, D_MODEL)
    mem2d = mem.reshape(B * N_MEM, D_MODEL)

    proj = _inproj_fwd(x2d, g_norm, w_in, after=() if fwd_token is None else (fwd_token,))
    w_kv, w_out = late_weights(proj)
    kv = _kv_fwd(mem2d, g_mem, w_kv)
    a, lse = _attn_fwd(proj, B, S)
    w_sT = jnp.swapaxes(w_s, 1, 2)
    b_tab = jnp.repeat(b_s.T, HEAD_DIM, axis=1)
    (dx2, da, drest, loss, d_wout, d_ws, d_bs, d_gv, d_gf, dkv) = _mid(
        x2d, t2d, a, proj, kv, w_s, w_sT, b_tab, g_v, w_out, g_final, B, S)
    d_wkv, d_gmem = _kv_bwd(mem2d, g_mem, w_kv, dkv)
    dq, dk, dv = _attn_bwd(proj, a, lse, da, B, S, after=() if on_late is None else (on_late(d_wkv, d_wout, d_ws),))
    if on_dw is None:
        d_win = _inproj_bwd_dw(dq, dk, dv, drest, x2d, g_norm)
        after = ()
    else:
        d_win = None
        after = (on_dw(*_inproj_bwd_dw(dq, dk, dv, drest, x2d, g_norm, reduce=True)),)
    grad_x, d_gnorm = _inproj_bwd_dx(dq, dk, dv, drest, x2d, dx2, g_norm, w_in, after=after)
    d_bs = d_bs[:, :N_SGU_GROUPS].T
    return (loss, grad_x.reshape(B, S, D_MODEL),
            dict(g_norm=d_gnorm, w_in=d_win, w_s=d_ws, b_s=d_bs, g_v=d_gv, g_mem=d_gmem, w_kv=d_wkv,
                 w_out=d_wout, g_final=d_gf))


def kernel(x, mem, g_norm, w_in, w_sgu_spatial, b_sgu_spatial, g_sgu_v, g_mem, w_mem_kv, w_out, g_final, loss_target, m_g_norm, m_w_in, m_w_sgu_spatial, m_b_sgu_spatial, m_g_sgu_v, m_g_mem, m_w_mem_kv, m_w_out, m_g_final, v_g_norm, v_w_in, v_w_sgu_spatial, v_b_sgu_spatial, v_g_sgu_v, v_g_mem, v_w_mem_kv, v_w_out, v_g_final):
    t = lambda w: jnp.swapaxes(w[0], 0, 1)
    (win_all,), late_shards, late_lands = _ag_weights([t(w_in)], [w_mem_kv[0], w_out[0]])
    w_in_full = win_all.reshape(-1, win_all.shape[-1])
    late = _exchange_start("gather", list(late_shards), (win_all,), "gather_late_start", lands=late_lands)

    def late_weights(proj):
        return [z.reshape(-1, z.shape[-1]) for z in _exchange_wait("gather", late, proj, "gather_late_wait")[1]]

    scatter = {}

    def on_late(d_wkv, d_wout, d_ws):
        d_ws = d_ws.reshape(1, -1, LANES)
        scatter["late"] = _exchange_start("direct", [d_wkv, d_wout, d_ws], (), "scatter_late_start")
        return scatter["late"][-1]

    def on_dw(sends, own):
        scatter["own"] = own
        scatter["started"] = _exchange_start("scatter", [sends], (own,), "scatter_start")
        return scatter["started"][-1]

    loss, grad_x, g = _local_step(
        x, mem, loss_target, g_norm, w_in_full, w_sgu_spatial[0], b_sgu_spatial[0], g_sgu_v, g_mem,
        late_weights, g_final.reshape(1, D_MODEL), fwd_token=late[-1], on_late=on_late, on_dw=on_dw)

    small_names = ("g_norm", "w_s", "b_s", "g_v", "g_mem", "g_final")
    g_small = _pack_small({n: g[n] for n in small_names if n != "w_s"}, loss)
    late_srcs, late_landed = _exchange_wait("direct", scatter["late"], g_small, "scatter_late_wait")
    _, landed = _exchange_wait("scatter", scatter["started"], late_landed[0], "scatter_wait")
    gr_in, gr_kv, gr_out, gr_small = _reduce_last([scatter["own"]], landed, g_small, late_srcs, late_landed, _W_S_ROW)

    small_w = (g_norm, w_sgu_spatial, b_sgu_spatial, g_sgu_v, g_mem, g_final)
    small_m = (m_g_norm, m_w_sgu_spatial, m_b_sgu_spatial, m_g_sgu_v, m_g_mem, m_g_final)
    small_v = (v_g_norm, v_w_sgu_spatial, v_b_sgu_spatial, v_g_sgu_v, v_g_mem, v_g_final)
    rows = lambda ws: [w.reshape(-1, LANES) for w in ws]
    small_new, (of_in, (gr_kv, d_kv, nm_kv, nv_kv), (gr_out, d_out, nm_out, nv_out)), loss = _adamw_all(
        gr_small, rows(small_w), rows(small_m), rows(small_v),
        [(t(w_in), gr_in, t(m_w_in), t(v_w_in)), (w_mem_kv[0], gr_kv, m_w_mem_kv[0], v_w_mem_kv[0]),
         (w_out[0], gr_out, m_w_out[0], v_w_out[0])])
    loss = loss.reshape(())
    small = [[z.reshape(w.shape) for z in four] for w, four in zip(small_w, small_new)]
    gr_in, d_in, nm_in, nv_in = [jnp.swapaxes(z, 0, 1) for z in of_in]

    def leaves(kind, big_in, big_kv, big_out):
        s_norm, s_ws, s_bs, s_gv, s_gmem, s_gf = [four[kind] for four in small]
        return [s_norm, big_in[None], s_ws, s_bs, s_gv, s_gmem, big_kv[None], big_out[None], s_gf]

    return (loss, grad_x, *leaves(0, gr_in, gr_kv, gr_out), *leaves(1, d_in, d_kv, d_out),
            *leaves(2, nm_in, nm_kv, nm_out), *leaves(3, nv_in, nv_kv, nv_out))
```

```python
import functools

import jax
import jax.numpy as jnp
from jax import lax
from jax.experimental import pallas as pl
from jax.experimental.pallas import tpu as pltpu

F32 = jnp.float32
BF16 = jnp.bfloat16
MESH = pl.DeviceIdType.MESH

D_MODEL = 1024
ATTN_WIDTH = 512
SGU_WIDTH = 256
MEM_WIDTH = 256
N_MEM = 256
IN_COLS = 3328
QKV_COLS = 3 * ATTN_WIDTH
REST_COLS = IN_COLS - QKV_COLS
SGU_CHUNK = 128
N_SGU_GROUPS = 4
EPS = 1e-6
NEG_INF = -1e30
DILATIONS = (1, 4, 16)
RADIUS = 64
Q_BLOCK = 128
LANES = 128
HEAD_DIM = 64

ADAM_LR = 0.001
ADAM_B1 = 0.9
ADAM_B2 = 0.999
ADAM_EPS = 1e-08
ADAM_WD = 0.01
ADAM_STEP = 10

N_CHIPS = 4
VMEM_LIMIT = 56 * 1024 * 1024
SMALL_ROWS = 560


def _params(sem=None, vmem=VMEM_LIMIT):
    return pltpu.CompilerParams(dimension_semantics=sem, vmem_limit_bytes=vmem)


def _nn(a, b):
    return jnp.dot(a, b, preferred_element_type=F32)


def _nt(a, b):
    return lax.dot_general(a, b, (((1,), (1,)), ((), ())), preferred_element_type=F32)


def _tn(a, b):
    return lax.dot_general(a, b, (((0,), (0,)), ((), ())), preferred_element_type=F32)


def _rms(x):
    r = lax.rsqrt(jnp.mean(x * x, axis=-1, keepdims=True) + EPS)
    return r, x * r


def _head_masks():
    lane = lax.broadcasted_iota(jnp.int32, (1, LANES), 1)
    lo = lane < HEAD_DIM
    return lo, (lo.astype(F32), (~lo).astype(F32))


def _silu_parts(z):
    s = jax.nn.sigmoid(z)
    return z * s, s * (1.0 + z * (1.0 - s))


def _gelu_parts(x):
    c = 0.7978845608028654
    x2 = x * x
    s = jax.nn.sigmoid((2.0 * c) * (x + 0.044715 * (x * x2)))
    return x * s, s * (1.0 + x * (1.0 - s) * ((2.0 * c) * (1.0 + 3.0 * 0.044715 * x2)))


def _after(tokens):
    return [pl.BlockSpec(memory_space=pl.ANY)] * len(tokens)


def _inproj_fwd(x2d, g_norm, w_in_t, after=()):
    T = x2d.shape[0]
    tm = 512

    def body(x_ref, g_ref, w_ref, *rest):
        o_ref = rest[-1]
        _, xh = _rms(x_ref[...])
        h = (xh * g_ref[...]).astype(BF16)
        o_ref[...] = _nt(h, w_ref[...])

    return pl.pallas_call(
        body, grid=(T // tm,),
        in_specs=[pl.BlockSpec((tm, D_MODEL), lambda i: (i, 0)),
                  pl.BlockSpec((1, D_MODEL), lambda i: (0, 0)),
                  pl.BlockSpec((IN_COLS, D_MODEL), lambda i: (0, 0))] + _after(after),
        out_specs=pl.BlockSpec((tm, IN_COLS), lambda i: (i, 0)),
        out_shape=jax.ShapeDtypeStruct((T, IN_COLS), F32),
        compiler_params=_params(("arbitrary",)), name="inproj_fwd")(x2d, g_norm, w_in_t, *after)


def _kv_fwd(mem2d, g_mem, w_kv):
    Tm = mem2d.shape[0]

    def body(m_ref, g_ref, w_ref, o_ref):
        _, mh = _rms(m_ref[...])
        o_ref[...] = _nn((mh * g_ref[...]).astype(BF16), w_ref[...])

    return pl.pallas_call(
        body, out_shape=jax.ShapeDtypeStruct((Tm, 2 * MEM_WIDTH), F32),
        compiler_params=_params(), name="kv_fwd")(mem2d, g_mem, w_kv)


def _kv_bwd(mem2d, g_mem, w_kv, dkv):
    Tm = mem2d.shape[0]

    def body(m_ref, g_ref, w_ref, dkv_ref, dw_ref, dg_ref):
        _, mh = _rms(m_ref[...])
        memn = (mh * g_ref[...]).astype(BF16)
        dkvb = dkv_ref[...].astype(BF16)
        dw = _tn(memn, dkvb).astype(BF16)
        for k in range(N_CHIPS):
            dw_ref[k] = dw[k * (D_MODEL // N_CHIPS):(k + 1) * (D_MODEL // N_CHIPS), :]
        dmemn = _nt(dkvb, w_ref[...])
        dg_ref[...] = jnp.sum(dmemn * mh, axis=0, keepdims=True)

    return pl.pallas_call(
        body, out_shape=(jax.ShapeDtypeStruct((N_CHIPS, D_MODEL // N_CHIPS, 2 * MEM_WIDTH), BF16),
                         jax.ShapeDtypeStruct((1, D_MODEL), F32)),
        compiler_params=_params(), name="kv_bwd")(mem2d, g_mem, w_kv, dkv)


def _attn_geometry(S):
    geom = []
    for d in DILATIONS:
        L = S // d
        assert L % Q_BLOCK == 0
        geom.append((d, L, min(2 * Q_BLOCK, L), L // Q_BLOCK))
    return geom


def _init_bias(bias_scr, geom, hp):
    row = lax.broadcasted_iota(jnp.int32, (Q_BLOCK, 2 * Q_BLOCK), 0)
    col = lax.broadcasted_iota(jnp.int32, (Q_BLOCK, 2 * Q_BLOCK), 1)
    for j in (0, 1):
        bits = (126 - (2 * hp + j)) * (1 << 23)
        slope = lax.bitcast_convert_type(jnp.full((1, 1), bits, jnp.int32), F32)
        for di, (d, _, _, _) in enumerate(geom):
            for cls, off in enumerate((0, -RADIUS, -2 * RADIUS)):
                dist = jnp.abs(col - row + off)
                bias_scr[di * 6 + cls * 2 + j] = jnp.where(
                    dist <= RADIUS, -(slope * float(d)) * dist.astype(F32), NEG_INF)


SPLIT = 4
COPY_ROWS = 256


def _by4_rows(S, step):
    per_class = S // SPLIT // COPY_ROWS
    r, j = step // per_class, step % per_class
    return (pl.ds(r + SPLIT * j * COPY_ROWS, COPY_ROWS, stride=SPLIT),
            pl.ds(r * (S // SPLIT) + j * COPY_ROWS, COPY_ROWS))


def _to_by4(src, dst, S):
    for i in range(S // COPY_ROWS):
        natural, by4 = _by4_rows(S, i)
        dst[by4, :] = src[natural, :]


def _block_slices(d, L, KW, nqb, r, qb, S):
    qs = qb * Q_BLOCK
    ks = jnp.clip(qs - RADIUS, 0, L - KW)
    cls = jnp.where(qb == 0, 0, jnp.where(qb == nqb - 1, 2, 1))
    if d == 1:
        qsl = pl.ds(pl.multiple_of(qs, Q_BLOCK), Q_BLOCK)
        ksl = pl.ds(pl.multiple_of(ks, RADIUS), KW)
    elif d == SPLIT:
        qsl = pl.ds(pl.multiple_of(r * L + qs, Q_BLOCK), Q_BLOCK)
        ksl = pl.ds(pl.multiple_of(r * L + ks, RADIUS), KW)
    else:
        sub = d // SPLIT
        base = (r % SPLIT) * (S // SPLIT) + r // SPLIT
        qsl = pl.ds(base + qs * sub, Q_BLOCK, stride=sub)
        ksl = pl.ds(base + ks * sub, KW, stride=sub)
    return qsl, ksl, cls


def _for_groups(geom, S, group, fn):
    for di, (d, L, KW, nqb) in enumerate(geom):
        n = group[di]
        assert (d * nqb) % n == 0

        def step(it, carry, di=di, d=d, L=L, KW=KW, nqb=nqb, n=n):
            slices = []
            for g in range(n):
                i = it * n + g
                slices.append(_block_slices(d, L, KW, nqb, i // nqb, i % nqb, S))
            fn(di, KW, slices)
            return carry
        lax.fori_loop(0, d * nqb // n, step, 0)


def _attn_fwd(proj, B, S):
    T = B * S
    geom = _attn_geometry(S)
    n_pairs = ATTN_WIDTH // LANES

    def body(q_ref, k_ref, v_ref, a_ref, lse_ref, bias_scr, q4, k4, v4, *per_dilation):
        o_scr, m_scr, l_scr = per_dilation[0:3], per_dilation[3:6], per_dilation[6:9]
        lo, hm = _head_masks()
        pair = pl.program_id(0)

        @pl.when(pl.program_id(1) == 0)
        def _():
            _init_bias(bias_scr, geom, pair)
        for src, dst in ((q_ref, q4), (k_ref, k4), (v_ref, v4)):
            _to_by4(src, dst, S)

        def group(di, KW, all_slices):
            run = 8
            for first in range(0, len(all_slices), run):
                some(di, KW, all_slices[first:first + run])

        def some(di, KW, slices):
            chains = [(g, j) for g in range(len(slices)) for j in (0, 1)]
            q_src, k_src, v_src = (q_ref, k_ref, v_ref) if di == 0 else (q4, k4, v4)
            q = [q_src[qsl, :] for qsl, _, _ in slices]
            kw = [k_src[ksl, :].astype(BF16) for _, ksl, _ in slices]
            vw = [v_src[ksl, :].astype(BF16) for _, ksl, _ in slices]
            s = {(g, j): _nt((q[g] * (hm[j] * 0.125)).astype(BF16), kw[g])
                 + bias_scr[di * 6 + slices[g][2] * 2 + j, :, pl.ds(0, KW)] for g, j in chains}
            m = {c: jnp.max(s[c], axis=1, keepdims=True) for c in chains}
            p = {c: jnp.exp(s[c] - m[c]) for c in chains}
            l = {c: jnp.sum(p[c], axis=1, keepdims=True) for c in chains}
            o = {(g, j): _nn(p[(g, j)].astype(BF16), vw[g]) for g, j in chains}
            for g, (qsl, _, _) in enumerate(slices):
                o_scr[di][qsl, :] = jnp.where(lo, o[(g, 0)], o[(g, 1)])
                m_scr[di][qsl, :] = jnp.where(lo, m[(g, 0)], m[(g, 1)])
                l_scr[di][qsl, :] = jnp.where(lo, l[(g, 0)], l[(g, 1)])

        _for_groups(geom, S, (16, 16, 16), group)

        for i in range(S // COPY_ROWS):
            natural, by4 = _by4_rows(S, i)
            rows = [natural, by4, by4]
            ms = [m_scr[di][rows[di], :] for di in range(3)]
            mx = jnp.maximum(jnp.maximum(ms[0], ms[1]), ms[2])
            num = 0.0
            den = 0.0
            for di in range(3):
                w = jnp.exp(ms[di] - mx)
                num = num + w * o_scr[di][rows[di], :]
                den = den + w * l_scr[di][rows[di], :]
            a_ref[natural, :] = num / den
            lse_ref[natural, :] = mx + jnp.log(den)

    blk = lambda off: pl.BlockSpec((S, LANES), lambda h, b, off=off: (b, off + h))
    out_blk = pl.BlockSpec((S, LANES), lambda h, b: (b, h))
    return pl.pallas_call(
        body, grid=(n_pairs, B),
        in_specs=[blk(0), blk(n_pairs), blk(2 * n_pairs)],
        out_specs=[out_blk, out_blk],
        out_shape=[jax.ShapeDtypeStruct((T, ATTN_WIDTH), F32)] * 2,
        scratch_shapes=[pltpu.VMEM((18, Q_BLOCK, 2 * Q_BLOCK), F32)] + [pltpu.VMEM((S, LANES), F32)] * 12,
        compiler_params=_params(("arbitrary", "arbitrary")), name="attn_fwd")(proj, proj, proj)


def _attn_bwd(proj, a, lse, da, B, S, after=()):
    T = B * S
    geom = _attn_geometry(S)
    n_pairs = ATTN_WIDTH // LANES

    def body(q_ref, k_ref, v_ref, a_ref, lse_ref, do_ref, *rest):
        dq_ref, dk_ref, dv_ref, bias_scr = rest[len(after):len(after) + 4]
        scr = rest[len(after) + 4:]
        acc = (scr[0:3], scr[3:6])
        natural_in = (q_ref, k_ref, v_ref, a_ref, lse_ref, do_ref)
        by4_in = scr[6:12]
        _, hm = _head_masks()
        pair = pl.program_id(0)

        @pl.when(pl.program_id(1) == 0)
        def _():
            _init_bias(bias_scr, geom, pair)
        for ref in scr[0:6]:
            ref[...] = jnp.zeros_like(ref)
        for src, dst in zip(natural_in, by4_in):
            _to_by4(src, dst, S)

        def group(di, KW, all_slices):
            run = (4, 4, 8)[di]
            for first in range(0, len(all_slices), run):
                some(di, KW, all_slices[first:first + run])

        def some(di, KW, slices):
            n = len(slices)
            chains = [(g, j) for g in range(n) for j in (0, 1)]
            q_src, k_src, v_src, a_src, lse_src, do_src = natural_in if di == 0 else by4_in
            dq_scr, dk_scr, dv_scr = acc[0 if di == 0 else 1]
            q = [q_src[qsl, :] for qsl, _, _ in slices]
            do = [do_src[qsl, :] for qsl, _, _ in slices]
            doa = [do[g] * a_src[slices[g][0], :] for g in range(n)]
            lse_q = [lse_src[qsl, :] for qsl, _, _ in slices]
            kw = [k_src[ksl, :].astype(BF16) for _, ksl, _ in slices]
            vw = [v_src[ksl, :].astype(BF16) for _, ksl, _ in slices]
            qj = {(g, j): (q[g] * (hm[j] * 0.125)).astype(BF16) for g, j in chains}
            doj = {(g, j): (do[g] * hm[j]).astype(BF16) for g, j in chains}
            s = {(g, j): _nt(qj[(g, j)], kw[g])
                 + bias_scr[di * 6 + slices[g][2] * 2 + j, :, pl.ds(0, KW)] for g, j in chains}
            dp = {(g, j): _nt(doj[(g, j)], vw[g]) for g, j in chains}
            dsum = {(g, j): jnp.sum(doa[g] * hm[j], axis=1, keepdims=True) for g, j in chains}
            p = {(g, j): jnp.exp(s[(g, j)] - lse_q[g][:, HEAD_DIM * j:HEAD_DIM * j + 1]) for g, j in chains}
            ds = {c: (p[c] * (dp[c] - dsum[c])).astype(BF16) for c in chains}
            pb = {c: p[c].astype(BF16) for c in chains}
            dq = [_nn(ds[(g, 0)], kw[g]) * (hm[0] * 0.125) + _nn(ds[(g, 1)], kw[g]) * (hm[1] * 0.125)
                  for g in range(n)]
            both = lambda t, g: jnp.concatenate([t[(g, 0)], t[(g, 1)]], axis=0)
            dkw = [_tn(both(ds, g), both(qj, g)) for g in range(n)]
            dvw = [_tn(both(pb, g), both(doj, g)) for g in range(n)]
            for g, (qsl, ksl, _) in enumerate(slices):
                dq_scr[qsl, :] = dq_scr[qsl, :] + dq[g]
                dk_scr[ksl, :] = dk_scr[ksl, :] + dkw[g]
                dv_scr[ksl, :] = dv_scr[ksl, :] + dvw[g]

        _for_groups(geom, S, (16, 16, 16), group)

        for i in range(S // COPY_ROWS):
            natural, by4 = _by4_rows(S, i)
            for nat, split in zip(*acc):
                nat[natural, :] = nat[natural, :] + split[by4, :]
        for out, nat in zip((dq_ref, dk_ref, dv_ref), acc[0]):
            out[...] = nat[...].astype(BF16)

    blk = lambda off: pl.BlockSpec((S, LANES), lambda h, b, off=off: (b, off + h))
    return pl.pallas_call(
        body, grid=(n_pairs, B),
        in_specs=[blk(0), blk(n_pairs), blk(2 * n_pairs), blk(0), blk(0), blk(0)] + _after(after),
        out_specs=[blk(0), blk(0), blk(0)],
        out_shape=[jax.ShapeDtypeStruct((T, ATTN_WIDTH), BF16)] * 3,
        scratch_shapes=[pltpu.VMEM((18, Q_BLOCK, 2 * Q_BLOCK), F32)] + [pltpu.VMEM((S, LANES), F32)] * 12,
        compiler_params=_params(("arbitrary", "arbitrary")), name="attn_bwd")(proj, proj, proj, a, lse, da, *after)


def _mid(x2d, t2d, a, proj, kv, w_s, w_sT, b_tab, g_v, w_out, g_final, B, S):
    T = B * S
    tm = 512
    nt = S // tm
    halves = 2
    hrows = tm // halves

    def body(x_ref, t_ref, a_ref, za_ref, ub_ref, vb_ref, zb_ref, qm_ref, zm_ref, kv_ref,
              ws_ref, wsT_ref, btab_ref, gv_ref, wout_ref, gf_ref,
              dx2_ref, da_ref, drest_ref, loss_ref, dwout_bf_ref, dws_ref, dbs_ref, dgv_ref, dgf_ref, dkv_ref,
              dbtab_scr, dwout_ref):
        b = pl.program_id(0)
        t = pl.program_id(1)
        first = jnp.logical_and(b == 0, t == 0)
        last = jnp.logical_and(b == B - 1, t == nt - 1)
        _, hm = _head_masks()
        lane_g = lax.broadcasted_iota(jnp.int32, (1, SGU_WIDTH), 1) // HEAD_DIM
        gm = [(lane_g == g).astype(F32) for g in range(N_SGU_GROUPS)]
        H = range(halves)
        rows = [pl.ds(h * hrows, hrows) for h in H]
        ld = lambda ref: [ref[r, :] for r in rows]
        cat = lambda parts, axis: jnp.concatenate(parts, axis=axis)
        chunks = [slice(ci * SGU_CHUNK, (ci + 1) * SGU_CHUNK) for ci in range(hrows // SGU_CHUNK)]
        pairs = [slice(pr * LANES, (pr + 1) * LANES) for pr in range(2)]
        heads = [(pr, j) for pr in range(2) for j in (0, 1)]

        @pl.when(first)
        def _():
            loss_ref[...] = jnp.zeros_like(loss_ref)
            dwout_ref[...] = jnp.zeros_like(dwout_ref)
            dws_ref[...] = jnp.zeros_like(dws_ref)
            dbs_ref[...] = jnp.zeros_like(dbs_ref)
            dgv_ref[...] = jnp.zeros_like(dgv_ref)
            dgf_ref[...] = jnp.zeros_like(dgf_ref)
            dbtab_scr[...] = jnp.zeros_like(dbtab_scr)

        @pl.when(t == 0)
        def _():
            dkv_ref[...] = jnp.zeros_like(dkv_ref)

        a_val = ld(a_ref)
        sil_a = [_silu_parts(z) for z in ld(za_ref)]
        gated_a = [s[0] * a for s, a in zip(sil_a, a_val)]
        u = [_gelu_parts(z) for z in ld(ub_ref)]
        vv = [_gelu_parts(z) for z in ld(vb_ref)]
        vnorm = [_rms(v[0]) for v in vv]
        gv = gv_ref[...]
        vn = [(n[1] * gv).astype(BF16) for n in vnorm]
        w_cat = cat([ws_ref[g].astype(BF16) for g in range(N_SGU_GROUPS)], 1)
        wT_cat = cat([wsT_ref[g].astype(BF16) for g in range(N_SGU_GROUPS)], 1)
        gmb = [m.astype(BF16) for m in gm]
        by_group = lambda chunk: cat([chunk * gmb[g] for g in range(N_SGU_GROUPS)], 0)
        btab = btab_ref[...]
        mixed = [cat([btab + _nn(w_cat, by_group(vn[h][c, :])) for c in chunks], 0) for h in H]
        sg = [u[h][0] * mixed[h] for h in H]
        sil_b = [_silu_parts(z) for z in ld(zb_ref)]
        gated_b = [sil_b[h][0] * sg[h] for h in H]

        kvv = kv_ref[...].astype(BF16)
        kp = [kvv[:, p] for p in pairs]
        vp = [kvv[:, MEM_WIDTH + pr * LANES:MEM_WIDTH + (pr + 1) * LANES] for pr in range(2)]
        qm = ld(qm_ref)
        qj = {(h, pr, j): (qm[h][:, pairs[pr]] * (hm[j] * 0.125)).astype(BF16) for h in H for pr, j in heads}
        sc = {k: _nt(qj[k], kp[k[1]]) for k in qj}
        ex = {k: jnp.exp(sc[k] - jnp.max(sc[k], axis=1, keepdims=True)) for k in qj}
        prob = {k: ex[k] * (1.0 / jnp.sum(ex[k], axis=1, keepdims=True)) for k in qj}
        probb = {k: prob[k].astype(BF16) for k in qj}
        mo = [cat([sum(_nn(probb[(h, pr, j)], vp[pr]) * hm[j] for j in (0, 1)) for pr in range(2)], 1) for h in H]
        sil_m = [_silu_parts(z) for z in ld(zm_ref)]
        gated_m = [sil_m[h][0] * mo[h] for h in H]

        gated = [cat([gated_a[h], gated_b[h], gated_m[h]], 1).astype(BF16) for h in H]
        wout = wout_ref[...]
        x_in = ld(x_ref)
        x2 = [x_in[h] + _nn(gated[h], wout) for h in H]
        fin = [_rms(z) for z in x2]
        gf = gf_ref[...]
        tgt = ld(t_ref)
        err = [fin[h][1] * gf - tgt[h] for h in H]
        loss_ref[...] += sum(jnp.sum(e * e) for e in err) * (0.5 / D_MODEL)

        dy = [e * (1.0 / D_MODEL) for e in err]
        dgf_ref[...] += sum(jnp.sum(dy[h] * fin[h][1], axis=0, keepdims=True) for h in H)
        gdy = [d * gf for d in dy]
        dx2 = [fin[h][0] * (gdy[h] - fin[h][1] * jnp.mean(gdy[h] * fin[h][1], axis=1, keepdims=True)) for h in H]
        for h in H:
            dx2_ref[rows[h], :] = dx2[h]
        dx2b = [d.astype(BF16) for d in dx2]
        dgated = [_nt(d, wout) for d in dx2b]
        dwout_ref[...] += _tn(cat(gated, 0), cat(dx2b, 0))
        dga = [d[:, 0:ATTN_WIDTH] for d in dgated]
        dgb = [d[:, ATTN_WIDTH:ATTN_WIDTH + SGU_WIDTH] for d in dgated]
        dgm = [d[:, ATTN_WIDTH + SGU_WIDTH:] for d in dgated]

        for h in H:
            da_ref[rows[h], :] = dga[h] * sil_a[h][0]
        dza = [dga[h] * a_val[h] * sil_a[h][1] for h in H]

        dsg = [dgb[h] * sil_b[h][0] for h in H]
        dzb = [dgb[h] * sg[h] * sil_b[h][1] for h in H]
        dub = [dsg[h] * mixed[h] * u[h][1] for h in H]
        dmixed = [dsg[h] * u[h][0] for h in H]
        dmixed_b = [d.astype(BF16) for d in dmixed]
        dvn = [cat([_nn(wT_cat, by_group(dmixed_b[h][c, :])) for c in chunks], 0) for h in H]
        for g in range(N_SGU_GROUPS):
            dws_ref[g] += sum(_nt((dmixed[h][c, :] * gm[g]).astype(BF16), vn[h][c, :]) for h in H for c in chunks)
        dbtab_scr[...] += sum(dmixed[h][c, :] for h in H for c in chunks)
        dgv_ref[...] += sum(jnp.sum(dvn[h] * vnorm[h][1], axis=0, keepdims=True) for h in H)
        tv = [d * gv for d in dvn]
        dvv = [vnorm[h][0] * (tv[h] - vnorm[h][1] * jnp.mean(tv[h] * vnorm[h][1], axis=1, keepdims=True)) for h in H]
        dvb = [dvv[h] * vv[h][1] for h in H]

        dmo = [dgm[h] * sil_m[h][0] for h in H]
        dzm = [dgm[h] * mo[h] * sil_m[h][1] for h in H]
        dmoj = {(h, pr, j): (dmo[h][:, pairs[pr]] * hm[j]).astype(BF16) for h in H for pr, j in heads}
        dp = {k: _nt(dmoj[k], vp[k[1]]) for k in qj}
        ds = {k: (prob[k] * (dp[k] - jnp.sum(dp[k] * prob[k], axis=1, keepdims=True))).astype(BF16) for k in qj}
        dqm = [cat([sum(_nn(ds[(h, pr, j)], kp[pr]) * (hm[j] * 0.125) for j in (0, 1)) for pr in range(2)], 1)
               for h in H]
        every = lambda tbl, pr: cat([tbl[(h, pr, j)] for h in H for j in (0, 1)], 0)
        dk = [_tn(every(ds, pr), every(qj, pr)) for pr in range(2)]
        dv = [_tn(every(probb, pr), every(dmoj, pr)) for pr in range(2)]
        dkv_ref[...] += cat(dk + dv, 1)

        for h in H:
            drest_ref[rows[h], :] = cat([dza[h], dub[h], dvb[h], dzb[h], dqm[h], dzm[h]], 1).astype(BF16)

        @pl.when(last)
        def _():
            lane = lax.broadcasted_iota(jnp.int32, (1, LANES), 1)
            dbt = dbtab_scr[...]
            out = jnp.zeros((SGU_CHUNK, LANES), F32)
            for g in range(N_SGU_GROUPS):
                out = out + jnp.where(lane == g, jnp.sum(dbt * gm[g], axis=1, keepdims=True), 0.0)
            dbs_ref[...] = out
            for r0 in range(0, D_MODEL, SGU_CHUNK):
                k, row = divmod(r0, D_MODEL // N_CHIPS)
                dwout_bf_ref[k, row:row + SGU_CHUNK, :] = dwout_ref[r0:r0 + SGU_CHUNK, :].astype(BF16)

    tile = lambda w, cb: pl.BlockSpec((tm, w), lambda b, t, cb=cb: (b * nt + t, cb))
    const = lambda shape: pl.BlockSpec(shape, lambda b, t, n=len(shape): (0,) * n)
    return pl.pallas_call(
        body, grid=(B, nt),
        in_specs=[tile(D_MODEL, 0), tile(D_MODEL, 0), tile(ATTN_WIDTH, 0),
                  tile(ATTN_WIDTH, 3),
                  tile(SGU_WIDTH, 8), tile(SGU_WIDTH, 9), tile(SGU_WIDTH, 10),
                  tile(MEM_WIDTH, 11), tile(MEM_WIDTH, 12),
                  pl.BlockSpec((N_MEM, 2 * MEM_WIDTH), lambda b, t: (b, 0)),
                  const((N_SGU_GROUPS, SGU_CHUNK, SGU_CHUNK)), const((N_SGU_GROUPS, SGU_CHUNK, SGU_CHUNK)),
                  const((SGU_CHUNK, SGU_WIDTH)), const((1, SGU_WIDTH)),
                  const((D_MODEL, D_MODEL)), const((1, D_MODEL))],
        out_specs=[tile(D_MODEL, 0), tile(ATTN_WIDTH, 0), tile(REST_COLS, 0),
                   const((8, LANES)), const((N_CHIPS, D_MODEL // N_CHIPS, D_MODEL)),
                   const((N_SGU_GROUPS, SGU_CHUNK, SGU_CHUNK)), const((SGU_CHUNK, LANES)),
                   const((1, SGU_WIDTH)), const((1, D_MODEL)),
                   pl.BlockSpec((N_MEM, 2 * MEM_WIDTH), lambda b, t: (b, 0))],
        out_shape=[jax.ShapeDtypeStruct((T, D_MODEL), F32), jax.ShapeDtypeStruct((T, ATTN_WIDTH), F32),
                   jax.ShapeDtypeStruct((T, REST_COLS), BF16),
                   jax.ShapeDtypeStruct((8, LANES), F32),
                   jax.ShapeDtypeStruct((N_CHIPS, D_MODEL // N_CHIPS, D_MODEL), BF16),
                   jax.ShapeDtypeStruct((N_SGU_GROUPS, SGU_CHUNK, SGU_CHUNK), F32),
                   jax.ShapeDtypeStruct((SGU_CHUNK, LANES), F32),
                   jax.ShapeDtypeStruct((1, SGU_WIDTH), F32), jax.ShapeDtypeStruct((1, D_MODEL), F32),
                   jax.ShapeDtypeStruct((B * N_MEM, 2 * MEM_WIDTH), F32)],
        scratch_shapes=[pltpu.VMEM((SGU_CHUNK, SGU_WIDTH), F32), pltpu.VMEM((D_MODEL, D_MODEL), F32)],
        compiler_params=_params(("arbitrary", "arbitrary"), vmem=VMEM_LIMIT + 2 * 1024 * 1024), name="mid")(
            x2d, t2d, a, proj, proj, proj, proj, proj, proj, kv, w_s, w_sT, b_tab, g_v, w_out, g_final)


def _inproj_bwd_dx(dq, dk, dv, drest, x2d, dx2, g_norm, w_in_t, after=()):
    T = x2d.shape[0]
    tm = 512
    W = ATTN_WIDTH

    def body(dq_ref, dk_ref, dv_ref, dr_ref, x_ref, dx2_ref, g_ref, w_ref, *rest):
        gx_ref, dg_ref = rest[-2:]

        @pl.when(pl.program_id(0) == 0)
        def _():
            dg_ref[...] = jnp.zeros_like(dg_ref)

        halves = [pl.ds(h * (tm // 2), tm // 2) for h in (0, 1)]
        dh = [(_nn(dq_ref[r, :], w_ref[0:W, :]) + _nn(dk_ref[r, :], w_ref[W:2 * W, :])
               + _nn(dv_ref[r, :], w_ref[2 * W:3 * W, :]) + _nn(dr_ref[r, :], w_ref[QKV_COLS:IN_COLS, :]))
              for r in halves]
        nrm = [_rms(x_ref[r, :]) for r in halves]
        dg_ref[...] += sum(jnp.sum(d * n[1], axis=0, keepdims=True) for d, n in zip(dh, nrm))
        g = g_ref[...]
        for r, d, (rstd, xh) in zip(halves, dh, nrm):
            th = d * g
            gx_ref[r, :] = rstd * (th - xh * jnp.mean(th * xh, axis=1, keepdims=True)) + dx2_ref[r, :]

    tile = lambda w: pl.BlockSpec((tm, w), lambda i: (i, 0))
    return pl.pallas_call(
        body, grid=(T // tm,),
        in_specs=[tile(W), tile(W), tile(W), tile(REST_COLS), tile(D_MODEL), tile(D_MODEL),
                  pl.BlockSpec((1, D_MODEL), lambda i: (0, 0)),
                  pl.BlockSpec((IN_COLS, D_MODEL), lambda i: (0, 0))] + _after(after),
        out_specs=[tile(D_MODEL), pl.BlockSpec((1, D_MODEL), lambda i: (0, 0))],
        out_shape=[jax.ShapeDtypeStruct((T, D_MODEL), F32), jax.ShapeDtypeStruct((1, D_MODEL), F32)],
        compiler_params=_params(("arbitrary",)), name="inproj_bwd_dx")(
            dq, dk, dv, drest, x2d, dx2, g_norm, w_in_t, *after)


def _inproj_bwd_dw(dq, dk, dv, drest, x2d, g_norm, reduce=False):
    T = x2d.shape[0]
    tm = 512
    nt = T // tm
    W = ATTN_WIDTH
    shard = IN_COLS // N_CHIPS
    half = shard // 2
    row_block = 32

    def body(dq_ref, dk_ref, dv_ref, dr_ref, x_ref, g_ref, *rest):
        if reduce:
            sends_out, own_out, acc, ras, narrow, sends, own, s_sem, r_sem, put_sem = rest
            x, y, c, chip, peers, peer_chip = _place()
            sib = (x, y, 1 - c)

            def part(k, cc, r0):
                return acc.at[pl.ds(pl.multiple_of(k * shard + cc * half + r0, 8), row_block), :]

            def swap_win(k):
                return _remote(narrow.at[k], ras.at[k], s_sem.at[k], r_sem.at[k], sib)

            def put_send(m):
                return pltpu.make_async_copy(sends.at[m], sends_out.at[m], put_sem.at[m])

            put_own = pltpu.make_async_copy(own, own_out, put_sem.at[N_CHIPS - 1])
        else:
            acc = rest[0]

        @pl.when(pl.program_id(0) == 0)
        def _():
            acc[...] = jnp.zeros_like(acc)

        _, xh = _rms(x_ref[...])
        h = (xh * g_ref[...]).astype(BF16)
        acc[0:W, :] += _tn(dq_ref[...], h)
        acc[W:2 * W, :] += _tn(dk_ref[...], h)
        acc[2 * W:3 * W, :] += _tn(dv_ref[...], h)
        acc[QKV_COLS:IN_COLS, :] += _tn(dr_ref[...], h)

        if reduce:
            @pl.when(pl.program_id(0) == nt - 1)
            def _():
                for k in range(N_CHIPS):
                    def to_bf16(i, carry, k=k):
                        r0 = pl.multiple_of(i * row_block, row_block)
                        narrow[k, pl.ds(r0, row_block), :] = part(k, 1 - c, r0)[...].astype(BF16)
                        return carry
                    lax.fori_loop(0, half // row_block, to_bf16, 0)
                    swap_win(k).start()

                def chip_sum(k, r0):
                    return part(k, c, r0)[...] + ras[k, pl.ds(r0, row_block), :].astype(F32)

                for k in range(N_CHIPS):
                    swap_win(k).wait_recv()

                    @pl.when(chip == k)
                    def _(k=k):
                        def mine(i, carry):
                            r0 = pl.multiple_of(i * row_block, row_block)
                            own[pl.ds(r0, row_block), :] = chip_sum(k, r0)
                            return carry
                        lax.fori_loop(0, half // row_block, mine, 0)
                        put_own.start()

                    @pl.when(chip != k)
                    def _(k=k):
                        def other(i, carry):
                            r0 = pl.multiple_of(i * row_block, row_block)
                            sends[(k ^ chip) - 1, pl.ds(r0, row_block), :] = chip_sum(k, r0).astype(BF16)
                            return carry
                        lax.fori_loop(0, half // row_block, other, 0)
                        put_send((k ^ chip) - 1).start()
                for k in range(N_CHIPS):
                    swap_win(k).wait_send()
                for m in range(N_CHIPS - 1):
                    put_send(m).wait()
                put_own.wait()

    tile = lambda w: pl.BlockSpec((tm, w), lambda i: (i, 0))
    vmem = pl.BlockSpec(memory_space=pltpu.VMEM)
    in_specs = [tile(W), tile(W), tile(W), tile(REST_COLS), tile(D_MODEL), pl.BlockSpec((1, D_MODEL), lambda i: (0, 0))]
    if not reduce:
        return pl.pallas_call(
            body, grid=(nt,), in_specs=in_specs,
            out_specs=pl.BlockSpec((IN_COLS, D_MODEL), lambda i: (0, 0)),
            out_shape=jax.ShapeDtypeStruct((IN_COLS, D_MODEL), F32),
            compiler_params=_params(("arbitrary",)), name="inproj_bwd_dw")(dq, dk, dv, drest, x2d, g_norm)
    quarters = pltpu.VMEM((N_CHIPS, half, D_MODEL), BF16)
    far = pl.BlockSpec(memory_space=pl.ANY)
    return pl.pallas_call(
        body, grid=(nt,), in_specs=in_specs, out_specs=[far] * 2,
        out_shape=[jax.ShapeDtypeStruct((N_CHIPS - 1, half, D_MODEL), BF16),
                   jax.ShapeDtypeStruct((half, D_MODEL), F32)],
        scratch_shapes=[pltpu.VMEM((IN_COLS, D_MODEL), F32), quarters, quarters,
                        pltpu.VMEM((N_CHIPS - 1, half, D_MODEL), BF16), pltpu.VMEM((half, D_MODEL), F32),
                        pltpu.SemaphoreType.DMA((N_CHIPS,)), pltpu.SemaphoreType.DMA((N_CHIPS,)),
                        pltpu.SemaphoreType.DMA((N_CHIPS,))],
        compiler_params=_params(("arbitrary",)), name="inproj_bwd_dw_reduce")(dq, dk, dv, drest, x2d, g_norm)


def _adamw_update(w, g, m, v):
    nm = ADAM_B1 * m + (1.0 - ADAM_B1) * g
    nv = ADAM_B2 * v + (1.0 - ADAM_B2) * (g * g)
    m_hat = nm / (1.0 - ADAM_B1 ** ADAM_STEP)
    v_hat = nv / (1.0 - ADAM_B2 ** ADAM_STEP)
    return -ADAM_LR * (m_hat / (jnp.sqrt(v_hat) + ADAM_EPS) + ADAM_WD * w), nm, nv


def _adamw_all(g_packed, ws, ms, vs, large):
    n, nl = len(ws), len(large)
    row_block = 8
    chunk_bytes = 512 * 1024

    def chunk_rows(w):
        R, C = w.shape
        return max(r for r in range(8, R + 1, 8) if R % r == 0 and r * C * 4 <= chunk_bytes)

    jobs = [(b, r0, chunk_rows(four[0])) for b, four in enumerate(large)
            for r0 in range(0, four[0].shape[0], chunk_rows(four[0]))]

    def body(*refs):
        g_ref = refs[0]
        w_refs, m_refs, v_refs = refs[1:1 + n], refs[1 + n:1 + 2 * n], refs[1 + 2 * n:1 + 3 * n]
        far_in = refs[1 + 3 * n:1 + 3 * n + 4 * nl]
        outs = refs[1 + 3 * n + 4 * nl:1 + 7 * n + 4 * nl]
        far_out = refs[1 + 7 * n + 4 * nl:1 + 7 * n + 8 * nl]
        loss_ref = refs[1 + 7 * n + 8 * nl]
        scr = refs[2 + 7 * n + 8 * nl:]
        in_scr, out_scr, in_sem, out_sem = scr[:4 * nl], scr[4 * nl:7 * nl], scr[7 * nl], scr[7 * nl + 1]

        def read(j, k):
            b, r0, rows = jobs[j]
            blk = pl.ds(r0, rows)
            return pltpu.make_async_copy(far_in[4 * b + k].at[blk, :], in_scr[4 * b + k].at[blk, :],
                                         in_sem.at[4 * j + k])

        def write(j, k):
            b, r0, rows = jobs[j]
            blk = pl.ds(r0, rows)
            src = in_scr[4 * b + 1] if k == 0 else out_scr[3 * b + k - 1]
            return pltpu.make_async_copy(src.at[blk, :], far_out[4 * b + k].at[blk, :], out_sem.at[4 * j + k])

        for j in range(len(jobs)):
            for k in range(4):
                read(j, k).start()
        off = 0
        for i, (_, used, padded) in enumerate(_SMALL_PARTS[:n]):
            g = g_ref[off:off + used, :]
            delta, nm, nv = _adamw_update(w_refs[i][...], g, m_refs[i][...], v_refs[i][...])
            outs[4 * i][...], outs[4 * i + 1][...], outs[4 * i + 2][...], outs[4 * i + 3][...] = g, delta, nm, nv
            off += padded
        loss_ref[...] = g_ref[_LOSS_ROW:_LOSS_ROW + 1, 0:1]
        for j, (b, r0, rows) in enumerate(jobs):
            for k in range(4):
                read(j, k).wait()

            def update(i, carry, b=b, r0=r0):
                blk = pl.ds(pl.multiple_of(r0 + i * row_block, row_block), row_block)
                w, g, m, v = [in_scr[4 * b + k][blk, :] for k in range(4)]
                for k, val in enumerate(_adamw_update(w, g, m, v)):
                    out_scr[3 * b + k][blk, :] = val
                return carry
            lax.fori_loop(0, rows // row_block, update, 0)
            for k in range(4):
                write(j, k).start()
        for j in range(len(jobs)):
            for k in range(4):
                write(j, k).wait()

    vmem = pl.BlockSpec(memory_space=pltpu.VMEM)
    far = pl.BlockSpec(memory_space=pl.ANY)
    flat = [a for four in large for a in four]
    outs = pl.pallas_call(
        body, in_specs=[vmem] * (1 + 3 * n) + [far] * (4 * nl),
        out_specs=[vmem] * (4 * n) + [far] * (4 * nl) + [vmem],
        out_shape=[jax.ShapeDtypeStruct(w.shape, F32) for w in ws for _ in range(4)]
        + [jax.ShapeDtypeStruct(four[0].shape, F32) for four in large for _ in range(4)]
        + [jax.ShapeDtypeStruct((1, 1), F32)],
        scratch_shapes=[pltpu.VMEM(a.shape, F32) for a in flat]
        + [pltpu.VMEM(four[0].shape, F32) for four in large for _ in range(3)]
        + [pltpu.SemaphoreType.DMA((4 * len(jobs),)), pltpu.SemaphoreType.DMA((4 * len(jobs),))],
        compiler_params=_params(), name="adamw_all")(g_packed, *ws, *ms, *vs, *flat)
    return ([outs[4 * i:4 * i + 4] for i in range(n)], [outs[4 * (n + i):4 * (n + i) + 4] for i in range(nl)],
            outs[4 * (n + nl)])


def _place():
    x, y, c = lax.axis_index("x"), lax.axis_index("y"), lax.axis_index("c")
    chip = 2 * x + y
    peers = [(x, 1 - y), (1 - x, y), (1 - x, 1 - y)]
    peer_chip = [2 * px + py for px, py in peers]
    return x, y, c, chip, peers, peer_chip


def _remote(src, dst, send_sem, recv_sem, dev):
    return pltpu.make_async_remote_copy(src_ref=src, dst_ref=dst, send_sem=send_sem, recv_sem=recv_sem,
                                        device_id=dev, device_id_type=MESH)


def _ag_weights(weights, late=()):
    nw, nl = len(weights), len(late)
    pieces = 2

    def body(*refs):
        srcs, late_srcs = refs[:nw], refs[nw:nw + nl]
        outs, late_bf, late_land = (refs[nw + nl:2 * nw + nl], refs[2 * nw + nl:2 * nw + 2 * nl],
                                    refs[2 * nw + 2 * nl:2 * nw + 3 * nl])
        scr = refs[2 * nw + 3 * nl:]
        wide, narrow = scr[:nw], scr[nw:2 * nw]
        late_scr, late_wide = scr[2 * nw:2 * nw + nl], scr[2 * nw + nl:2 * nw + 2 * nl]
        in_sem, put_sem, late_sem, get_sem, s_ici, r_ici, s_d2d, r_d2d = scr[2 * nw + 2 * nl:]
        x, y, c = lax.axis_index("x"), lax.axis_index("y"), lax.axis_index("c")
        chip = 2 * x + y
        sib = (x, y, 1 - c)
        first = ((x + 1 - c) % 2, (y + c) % 2)
        second = ((x + c) % 2, (y + 1 - c) % 2)
        first_chip, second_chip = 2 * first[0] + first[1], 2 * second[0] + second[1]
        diag_chip = 3 - chip

        parts = [(w, out, pc) for w, out in enumerate(outs) for pc in range(pieces)]

        def rows_of(out, cc, pc):
            rows = out.shape[1] // 2 // pieces
            return pl.ds(pl.multiple_of((cc * pieces + pc) * rows, 16), rows)

        def piece(out, k, cc, pc):
            return out.at[k, rows_of(out, cc, pc), :]

        def ici(w, slot, out, k, dev, pc, src=None):
            blk, sem = piece(out, k, c, pc), (nw * slot + w) * pieces + pc
            return _remote(blk if src is None else src, blk, s_ici.at[sem], r_ici.at[sem], (dev[0], dev[1], c))

        def d2d(w, slot, out, k, cc, pc):
            blk, sem = piece(out, k, cc, pc), (nw * slot + w) * pieces + pc
            return _remote(blk, blk, s_d2d.at[sem], r_d2d.at[sem], sib)

        def read(w, cc, pc):
            rows = rows_of(outs[w], cc, pc)
            return pltpu.make_async_copy(srcs[w].at[rows, :], wide[w].at[rows, :],
                                         in_sem.at[(w * 2 + cc) * pieces + pc])

        order = [(w, cc, pc) for cc in (0, 1) for w in range(nw) for pc in range(pieces)]
        for w, cc, pc in order:
            read(w, (c + cc) % 2, pc).start()
        late_gets = [pltpu.make_async_copy(src, dst, get_sem.at[i])
                     for i, (src, dst) in enumerate(zip(late_srcs, late_wide))]
        for cp in late_gets:
            cp.start()
        sent = []
        for w, cc, pc in order[:nw * pieces]:
            rows = rows_of(outs[w], c, pc)
            read(w, c, pc).wait()
            narrow[w][rows, :] = wide[w][rows, :].astype(BF16)
            for slot, dev in enumerate((first, second)):
                sent.append(ici(w, slot, outs[w], chip, dev, pc, src=narrow[w].at[rows, :]))
                sent[-1].start()
        late_puts = []
        for i, (src, scr_bf, bf, land) in enumerate(zip(late_wide, late_scr, late_bf, late_land)):
            late_gets[i].wait()
            scr_bf[...] = src[...].astype(BF16)
            for k, dst in enumerate([bf] + [land.at[s] for s in range(N_CHIPS)]):
                late_puts.append(pltpu.make_async_copy(scr_bf, dst, late_sem.at[i * (N_CHIPS + 1) + k]))
                late_puts[-1].start()
        for w, cc, pc in order[nw * pieces:]:
            rows = rows_of(outs[w], 1 - c, pc)
            read(w, 1 - c, pc).wait()
            narrow[w][rows, :] = wide[w][rows, :].astype(BF16)
        puts = [pltpu.make_async_copy(narrow[w], outs[w].at[chip], put_sem.at[w]) for w in range(nw)]
        for cp in puts:
            cp.start()
        for slot, k, dev in ((0, first_chip, first), (1, second_chip, second), (2, diag_chip, second)):
            for w, out, pc in parts:
                ici(w, slot, out, k, dev, pc).wait_recv()
                if slot == 0:
                    sent.append(ici(w, 2, out, k, second, pc))
                    sent[-1].start()
                sent.append(d2d(w, slot, out, k, c, pc))
                sent[-1].start()
        for slot, k in ((0, second_chip), (1, first_chip), (2, diag_chip)):
            for w, out, pc in parts:
                d2d(w, slot, out, k, 1 - c, pc).wait_recv()
        for cp in sent:
            cp.wait_send()
        for cp in puts + late_puts:
            cp.wait()

    vmem = pl.BlockSpec(memory_space=pltpu.VMEM)
    far = pl.BlockSpec(memory_space=pl.ANY)
    outs = pl.pallas_call(
        body,
        out_shape=[jax.ShapeDtypeStruct((N_CHIPS,) + w.shape, BF16) for w in weights]
        + [jax.ShapeDtypeStruct(w.shape, BF16) for w in late]
        + [jax.ShapeDtypeStruct((N_CHIPS,) + w.shape, BF16) for w in late],
        in_specs=[far] * (nw + nl), out_specs=[far] * (nw + 2 * nl),
        scratch_shapes=[pltpu.VMEM(w.shape, F32) for w in weights] + [pltpu.VMEM(w.shape, BF16) for w in weights]
        + [pltpu.VMEM(w.shape, BF16) for w in late] + [pltpu.VMEM(w.shape, F32) for w in late]
        + [pltpu.SemaphoreType.DMA((2 * nw * pieces,)), pltpu.SemaphoreType.DMA((nw,)),
           pltpu.SemaphoreType.DMA((nl * (N_CHIPS + 1),)), pltpu.SemaphoreType.DMA((nl,))]
        + [pltpu.SemaphoreType.DMA((3 * nw * pieces,))] * 4,
        compiler_params=pltpu.CompilerParams(vmem_limit_bytes=VMEM_LIMIT), name="ag_weights")(*weights, *late)
    return outs[:nw], outs[nw:nw + nl], outs[nw + nl:]


_HBM = pl.BlockSpec(memory_space=pltpu.HBM)
_SEM = pl.BlockSpec(memory_space=pltpu.SEMAPHORE)
_ANY = pl.BlockSpec(memory_space=pl.ANY)
_DATAFLOW = pltpu.SideEffectType.DATAFLOW_SIDE_EFFECTING


def _in_hbm(a):
    return pltpu.with_memory_space_constraint(a, pltpu.HBM)


_PEERS_OF = {"gather": 3, "scatter": 3, "direct": 7}


def _exchange_copies(mode, srcs, lands, send_sems, recv_sems):
    nw = len(srcs)
    x, y, c, chip, peers, peer_chip = _place()
    pairs = []
    if mode == "direct":
        targets = [((x, y), chip, 1)] + [(p, k, d) for p, k in zip(peers, peer_chip) for d in (0, 1)]
        for r, ((px, py), k, d) in enumerate(targets):
            for w in range(nw):
                sems = (send_sems.at[nw * r + w], recv_sems.at[nw * r + w], (px, py, (c + d) % 2))
                share = srcs[w].at[k if srcs[w].shape[0] > 1 else 0]
                pairs.append((_remote(share, lands[w].at[r], *sems),) * 2)
        return pairs
    gather = mode == "gather"
    for m, (px, py) in enumerate(peers):
        for w in range(nw):
            sems = (send_sems.at[nw * m + w], recv_sems.at[nw * m + w], (px, py, c))
            if gather:
                pairs.append((_remote(srcs[w], lands[w].at[chip], *sems),
                              _remote(srcs[w], lands[w].at[peer_chip[m]], *sems)))
            else:
                pairs.append((_remote(srcs[w].at[m], lands[w].at[m], *sems),) * 2)
    return pairs


def _exchange_start(mode, srcs, after, name, lands=None):
    nw = len(srcs)
    n_copies = _PEERS_OF[mode] * nw

    after = tuple(after)

    def body(*refs):
        send_sems, recv_sems = refs[2 * nw + len(after)], refs[2 * nw + len(after) + 1]
        for start, _ in _exchange_copies(mode, refs[:nw], refs[nw:2 * nw], send_sems, recv_sems):
            start.start()
        refs[-1][...] = jnp.zeros_like(refs[-1])

    if lands is None:
        shape = {"gather": lambda s: (N_CHIPS,) + s.shape, "scatter": lambda s: s.shape,
                 "direct": lambda s: (_PEERS_OF["direct"],) + s.shape[1:]}[mode]
        lands = [lax.empty(shape(s), s.dtype) for s in srcs]
    lands = [_in_hbm(l) for l in lands]
    return pl.pallas_call(
        body, name=name,
        out_shape=(pltpu.SemaphoreType.DMA((n_copies,)), pltpu.SemaphoreType.DMA((n_copies,)))
        + tuple(pltpu.HBM(s.shape, s.dtype) for s in srcs)
        + tuple(pltpu.HBM(l.shape, l.dtype) for l in lands)
        + (jax.ShapeDtypeStruct((8, LANES), F32),),
        in_specs=[_HBM] * (2 * nw) + [_ANY] * len(after),
        out_specs=(_SEM, _SEM) + (_HBM,) * (2 * nw) + (pl.BlockSpec(memory_space=pltpu.VMEM),),
        input_output_aliases={i: 2 + i for i in range(2 * nw)},
        compiler_params=pltpu.CompilerParams(has_side_effects=_DATAFLOW),
    )(*[_in_hbm(s) for s in srcs], *lands, *after)


def _exchange_wait(mode, started, after, name):
    nw = (len(started) - 3) // 2
    send_sems, recv_sems = started[0], started[1]
    thru = started[2:2 + 2 * nw]

    def body(*refs):
        for _, arrival in _exchange_copies(mode, refs[:nw], refs[nw:2 * nw], refs[2 * nw], refs[2 * nw + 1]):
            arrival.wait_send()
            arrival.wait_recv()

    outs = pl.pallas_call(
        body, name=name,
        out_shape=tuple(pltpu.HBM(t.shape, t.dtype) for t in thru),
        in_specs=[_HBM] * (2 * nw) + [_SEM, _SEM, _ANY], out_specs=(_HBM,) * (2 * nw),
        input_output_aliases={i: i for i in range(2 * nw)},
        compiler_params=pltpu.CompilerParams(has_side_effects=_DATAFLOW),
    )(*thru, send_sems, recv_sems, after)
    return outs[:nw], outs[nw:]


def _reduce_last(owns, landed, g_small, direct_srcs, direct_landed, spread_row0):
    ns, nd = len(owns), len(direct_srcs)
    halves = [o.shape[0] for o in owns]
    row_block = 16
    spread_rows = direct_srcs[-1].shape[1]
    rest0, rest1 = spread_row0, SMALL_ROWS - spread_row0 - spread_rows
    hs = (rest0 + rest1) // 2
    jobs = [(w, p * (halves[w] // 2), halves[w] // 2) for w in range(ns) for p in range(2)]
    n_swaps = len(jobs)
    jobs += [(ns + d, 0, direct_srcs[d].shape[1]) for d in range(nd)]

    def body(*refs):
        own_refs, land_refs, gsm_ref = refs[:ns], refs[ns:2 * ns], refs[2 * ns]
        dsrc_refs, dland_refs = refs[2 * ns + 1:2 * ns + 1 + nd], refs[2 * ns + 1 + nd:2 * ns + 1 + 2 * nd]
        n_in = 2 * ns + 1 + 2 * nd
        out_refs, osm_ref = refs[n_in:n_in + ns + nd - 1], refs[n_in + ns + nd - 1]
        scr = refs[n_in + ns + nd:]
        own_scr, land_scr, dsrc_scr, dland_scr = (scr[:ns], scr[ns:2 * ns], scr[2 * ns:2 * ns + nd],
                                                  scr[2 * ns + nd:2 * ns + 2 * nd])
        res = scr[2 * ns + 2 * nd:3 * ns + 3 * nd - 1]
        o_rest, ra_sm, p_sm, put_sem, in_sem, s_sem, r_sem, sm_s, sm_r = scr[3 * ns + 3 * nd - 1:]
        x, y, c, chip, peers, peer_chip = _place()
        sib = (x, y, 1 - c)
        half = lambda cc: pl.ds(pl.multiple_of(cc * hs, 8), hs)
        sm_a = _remote(gsm_ref.at[half(1 - c), :], ra_sm, sm_s.at[0], sm_r.at[0], sib)
        sm_a.start()
        swaps = [sm_a]

        def reads(j):
            w, r0, n = jobs[j]
            rows = pl.ds(r0, n)
            if w < ns:
                pairs = [(own_refs[w].at[rows, :], own_scr[w].at[rows, :]),
                         (land_refs[w].at[:, rows, :], land_scr[w].at[:, rows, :])]
            else:
                share = dsrc_refs[w - ns].at[chip if w < ns + nd - 1 else 0]
                pairs = [(share, dsrc_scr[w - ns]), (dland_refs[w - ns], dland_scr[w - ns])]
            return [pltpu.make_async_copy(s, d, in_sem.at[2 * j + i]) for i, (s, d) in enumerate(pairs)]

        for j in range(len(jobs)):
            for cp in reads(j):
                cp.start()
        sm_a.wait_recv()
        p_sm[chip] = gsm_ref[half(c), :] + ra_sm[...]
        for m, (px, py) in enumerate(peers):
            swaps.append(_remote(p_sm.at[chip], p_sm.at[chip], sm_s.at[1 + m], sm_r.at[1 + m], (px, py, c)))
            swaps[-1].start()

        def mine_of(j):
            w, r0, n = jobs[j]
            return out_refs[w].at[pl.ds(pl.multiple_of(c * halves[w] + r0, 8), n), :]

        def sum_of(j):
            w, r0, n = jobs[j]
            return res[w].at[pl.ds(r0, n), :]

        puts = []
        for j, (w, r0, n) in enumerate(jobs):
            for cp in reads(j):
                cp.wait()

            def total(i, carry, w=w, r0=r0):
                rr = pl.multiple_of(r0 + i * row_block, row_block)
                blk = pl.ds(rr, row_block)
                if w < ns:
                    acc = own_scr[w][blk, :]
                    for m in range(_PEERS_OF["scatter"]):
                        acc = acc + land_scr[w][m, blk, :].astype(F32)
                    res[w][blk, :] = acc
                else:
                    theirs = lambda r: dland_scr[w - ns][r, blk, :].astype(F32)
                    acc = ((dsrc_scr[w - ns][blk, :].astype(F32) + theirs(0)) + (theirs(1) + theirs(2))) + (
                        (theirs(3) + theirs(4)) + (theirs(5) + theirs(6)))
                    if w < ns + nd - 1:
                        res[w][blk, :] = acc
                    else:
                        osm_ref[pl.ds(pl.multiple_of(spread_row0 + rr, 8), row_block), :] = acc
                return carry
            lax.fori_loop(0, n // row_block, total, 0)
            if w < ns:
                puts.append(pltpu.make_async_copy(sum_of(j), mine_of(j), put_sem.at[j]))
                swaps.append(_remote(sum_of(j), mine_of(j), s_sem.at[j], r_sem.at[j], sib))
                swaps[-1].start()
            elif w < ns + nd - 1:
                puts.append(pltpu.make_async_copy(res[w], out_refs[w], put_sem.at[j]))
            if w < ns + nd - 1:
                puts[-1].start()
        for m, (px, py) in enumerate(peers):
            _remote(p_sm.at[chip], p_sm.at[peer_chip[m]], sm_s.at[1 + m], sm_r.at[1 + m], (px, py, c)).wait_recv()
        o_rest[half(c), :] = (p_sm[0] + p_sm[1]) + (p_sm[2] + p_sm[3])
        swaps.append(_remote(o_rest.at[half(c), :], o_rest.at[half(c), :], sm_s.at[4], sm_r.at[4], sib))
        swaps[-1].start()
        for j, (w, r0, n) in enumerate(jobs[:n_swaps]):
            theirs = out_refs[w].at[pl.ds(pl.multiple_of((1 - c) * halves[w] + r0, 8), n), :]
            _remote(sum_of(j), theirs, s_sem.at[j], r_sem.at[j], sib).wait_recv()
        _remote(o_rest.at[half(1 - c), :], o_rest.at[half(1 - c), :], sm_s.at[4], sm_r.at[4], sib).wait_recv()
        osm_ref[0:rest0, :] = o_rest[0:rest0, :]
        osm_ref[SMALL_ROWS - rest1:SMALL_ROWS, :] = o_rest[rest0:rest0 + rest1, :]
        for cp in swaps:
            cp.wait_send()
        for cp in puts:
            cp.wait()

    vmem = pl.BlockSpec(memory_space=pltpu.VMEM)
    far = [pl.BlockSpec(memory_space=pl.ANY)]
    return pl.pallas_call(
        body, out_shape=[jax.ShapeDtypeStruct((2 * o.shape[0], o.shape[1]), F32) for o in owns]
        + [jax.ShapeDtypeStruct(s.shape[1:], F32) for s in direct_srcs[:-1]]
        + [jax.ShapeDtypeStruct((SMALL_ROWS, LANES), F32)],
        in_specs=far * (2 * ns) + [vmem] + far * (2 * nd), out_specs=far * (ns + nd - 1) + [vmem],
        scratch_shapes=[pltpu.VMEM(o.shape, o.dtype) for o in owns] + [pltpu.VMEM(l.shape, l.dtype) for l in landed]
        + [pltpu.VMEM(s.shape[1:], s.dtype) for s in direct_srcs]
        + [pltpu.VMEM(l.shape, l.dtype) for l in direct_landed]
        + [pltpu.VMEM(o.shape, F32) for o in owns] + [pltpu.VMEM(s.shape[1:], F32) for s in direct_srcs[:-1]]
        + [pltpu.VMEM((2 * hs, LANES), F32), pltpu.VMEM((hs, LANES), F32), pltpu.VMEM((N_CHIPS, hs, LANES), F32),
           pltpu.SemaphoreType.DMA((len(jobs),)), pltpu.SemaphoreType.DMA((2 * len(jobs),)),
           pltpu.SemaphoreType.DMA((n_swaps,)), pltpu.SemaphoreType.DMA((n_swaps,)),
           pltpu.SemaphoreType.DMA((5,)), pltpu.SemaphoreType.DMA((5,))],
        compiler_params=pltpu.CompilerParams(vmem_limit_bytes=VMEM_LIMIT),
        name="reduce_last")(*owns, *landed, g_small, *direct_srcs, *direct_landed)


_SMALL_PARTS = (("g_norm", 8, 8), ("w_s", 512, 512), ("b_s", 4, 8), ("g_v", 2, 8), ("g_mem", 8, 8),
                ("g_final", 8, 8), ("loss", 8, 8))
_LOSS_ROW = SMALL_ROWS - 8
_W_S_ROW = 8
assert sum(p for _, _, p in _SMALL_PARTS) == SMALL_ROWS and _SMALL_PARTS[1][0] == "w_s"


def _pack_small(parts, loss_block):
    rows = []
    parts = dict(parts, loss=loss_block)
    for name, used, padded in _SMALL_PARTS:
        if name == "w_s":
            continue
        p = parts[name].reshape(used, LANES)
        if padded > used:
            p = jnp.pad(p, ((0, padded - used), (0, 0)))
        rows.append(p)
    return jnp.concatenate(rows, axis=0)


def _local_step(x, mem, target, g_norm, w_in, w_s, b_s, g_v, g_mem, late_weights, g_final,
                fwd_token=None, on_late=None, on_dw=None):
    B, S, _ = x.shape
    x2d = x.reshape(B * S, D_MODEL)
    t2d = target.reshape(B * S, D_MODEL)
    mem2d = mem.reshape(B * N_MEM, D_MODEL)

    proj = _inproj_fwd(x2d, g_norm, w_in, after=() if fwd_token is None else (fwd_token,))
    w_kv, w_out = late_weights(proj)
    kv = _kv_fwd(mem2d, g_mem, w_kv)
    a, lse = _attn_fwd(proj, B, S)
    w_sT = jnp.swapaxes(w_s, 1, 2)
    b_tab = jnp.repeat(b_s.T, HEAD_DIM, axis=1)
    (dx2, da, drest, loss, d_wout, d_ws, d_bs, d_gv, d_gf, dkv) = _mid(
        x2d, t2d, a, proj, kv, w_s, w_sT, b_tab, g_v, w_out, g_final, B, S)
    d_wkv, d_gmem = _kv_bwd(mem2d, g_mem, w_kv, dkv)
    dq, dk, dv = _attn_bwd(proj, a, lse, da, B, S, after=() if on_late is None else (on_late(d_wkv, d_wout, d_ws),))
    if on_dw is None:
        d_win = _inproj_bwd_dw(dq, dk, dv, drest, x2d, g_norm)
        after = ()
    else:
        d_win = None
        after = (on_dw(*_inproj_bwd_dw(dq, dk, dv, drest, x2d, g_norm, reduce=True)),)
    grad_x, d_gnorm = _inproj_bwd_dx(dq, dk, dv, drest, x2d, dx2, g_norm, w_in, after=after)
    d_bs = d_bs[:, :N_SGU_GROUPS].T
    return (loss, grad_x.reshape(B, S, D_MODEL),
            dict(g_norm=d_gnorm, w_in=d_win, w_s=d_ws, b_s=d_bs, g_v=d_gv, g_mem=d_gmem, w_kv=d_wkv,
                 w_out=d_wout, g_final=d_gf))


def kernel(x, mem, g_norm, w_in, w_sgu_spatial, b_sgu_spatial, g_sgu_v, g_mem, w_mem_kv, w_out, g_final, loss_target, m_g_norm, m_w_in, m_w_sgu_spatial, m_b_sgu_spatial, m_g_sgu_v, m_g_mem, m_w_mem_kv, m_w_out, m_g_final, v_g_norm, v_w_in, v_w_sgu_spatial, v_b_sgu_spatial, v_g_sgu_v, v_g_mem, v_w_mem_kv, v_w_out, v_g_final):
    t = lambda w: jnp.swapaxes(w[0], 0, 1)
    (win_all,), late_shards, late_lands = _ag_weights([t(w_in)], [w_mem_kv[0], w_out[0]])
    w_in_full = win_all.reshape(-1, win_all.shape[-1])
    late = _exchange_start("gather", list(late_shards), (win_all,), "gather_late_start", lands=late_lands)

    def late_weights(proj):
        return [z.reshape(-1, z.shape[-1]) for z in _exchange_wait("gather", late, proj, "gather_late_wait")[1]]

    scatter = {}

    def on_late(d_wkv, d_wout, d_ws):
        d_ws = d_ws.reshape(1, -1, LANES)
        scatter["late"] = _exchange_start("direct", [d_wkv, d_wout, d_ws], (), "scatter_late_start")
        return scatter["late"][-1]

    def on_dw(sends, own):
        scatter["own"] = own
        scatter["started"] = _exchange_start("scatter", [sends], (own,), "scatter_start")
        return scatter["started"][-1]

    loss, grad_x, g = _local_step(
        x, mem, loss_target, g_norm, w_in_full, w_sgu_spatial[0], b_sgu_spatial[0], g_sgu_v, g_mem,
        late_weights, g_final.reshape(1, D_MODEL), fwd_token=late[-1], on_late=on_late, on_dw=on_dw)

    small_names = ("g_norm", "w_s", "b_s", "g_v", "g_mem", "g_final")
    g_small = _pack_small({n: g[n] for n in small_names if n != "w_s"}, loss)
    late_srcs, late_landed = _exchange_wait("direct", scatter["late"], g_small, "scatter_late_wait")
    _, landed = _exchange_wait("scatter", scatter["started"], late_landed[0], "scatter_wait")
    gr_in, gr_kv, gr_out, gr_small = _reduce_last([scatter["own"]], landed, g_small, late_srcs, late_landed, _W_S_ROW)

    small_w = (g_norm, w_sgu_spatial, b_sgu_spatial, g_sgu_v, g_mem, g_final)
    small_m = (m_g_norm, m_w_sgu_spatial, m_b_sgu_spatial, m_g_sgu_v, m_g_mem, m_g_final)
    small_v = (v_g_norm, v_w_sgu_spatial, v_b_sgu_spatial, v_g_sgu_v, v_g_mem, v_g_final)
    rows = lambda ws: [w.reshape(-1, LANES) for w in ws]
    small_new, (of_in, (gr_kv, d_kv, nm_kv, nv_kv), (gr_out, d_out, nm_out, nv_out)), loss = _adamw_all(
        gr_small, rows(small_w), rows(small_m), rows(small_v),
        [(t(w_in), gr_in, t(m_w_in), t(v_w_in)), (w_mem_kv[0], gr_kv, m_w_mem_kv[0], v_w_mem_kv[0]),
         (w_out[0], gr_out, m_w_out[0], v_w_out[0])])
    loss = loss.reshape(())
    small = [[z.reshape(w.shape) for z in four] for w, four in zip(small_w, small_new)]
    gr_in, d_in, nm_in, nv_in = [jnp.swapaxes(z, 0, 1) for z in of_in]

    def leaves(kind, big_in, big_kv, big_out):
        s_norm, s_ws, s_bs, s_gv, s_gmem, s_gf = [four[kind] for four in small]
        return [s_norm, big_in[None], s_ws, s_bs, s_gv, s_gmem, big_kv[None], big_out[None], s_gf]

    return (loss, grad_x, *leaves(0, gr_in, gr_kv, gr_out), *leaves(1, d_in, d_kv, d_out),
            *leaves(2, nm_in, nm_kv, nm_out), *leaves(3, nv_in, nv_kv, nv_out))
```

```python
import functools

import jax
import jax.numpy as jnp
from jax import lax
from jax.experimental import pallas as pl
from jax.experimental.pallas import tpu as pltpu

F32 = jnp.float32
BF16 = jnp.bfloat16
MESH = pl.DeviceIdType.MESH

D_MODEL = 1024
ATTN_WIDTH = 512
SGU_WIDTH = 256
MEM_WIDTH = 256
N_MEM = 256
IN_COLS = 3328
QKV_COLS = 3 * ATTN_WIDTH
REST_COLS = IN_COLS - QKV_COLS
SGU_CHUNK = 128
N_SGU_GROUPS = 4
EPS = 1e-6
NEG_INF = -1e30
DILATIONS = (1, 4, 16)
RADIUS = 64
Q_BLOCK = 128
LANES = 128
HEAD_DIM = 64

ADAM_LR = 0.001
ADAM_B1 = 0.9
ADAM_B2 = 0.999
ADAM_EPS = 1e-08
ADAM_WD = 0.01
ADAM_STEP = 10

N_CHIPS = 4
VMEM_LIMIT = 56 * 1024 * 1024
SMALL_ROWS = 560


def _params(sem=None, vmem=VMEM_LIMIT):
    return pltpu.CompilerParams(dimension_semantics=sem, vmem_limit_bytes=vmem)


def _nn(a, b):
    return jnp.dot(a, b, preferred_element_type=F32)


def _nt(a, b):
    return lax.dot_general(a, b, (((1,), (1,)), ((), ())), preferred_element_type=F32)


def _tn(a, b):
    return lax.dot_general(a, b, (((0,), (0,)), ((), ())), preferred_element_type=F32)


def _rms(x):
    r = lax.rsqrt(jnp.mean(x * x, axis=-1, keepdims=True) + EPS)
    return r, x * r


def _head_masks():
    lane = lax.broadcasted_iota(jnp.int32, (1, LANES), 1)
    lo = lane < HEAD_DIM
    return lo, (lo.astype(F32), (~lo).astype(F32))


def _silu_parts(z):
    s = jax.nn.sigmoid(z)
    return z * s, s * (1.0 + z * (1.0 - s))


def _gelu_parts(x):
    c = 0.7978845608028654
    x2 = x * x
    s = jax.nn.sigmoid((2.0 * c) * (x + 0.044715 * (x * x2)))
    return x * s, s * (1.0 + x * (1.0 - s) * ((2.0 * c) * (1.0 + 3.0 * 0.044715 * x2)))


def _after(tokens):
    return [pl.BlockSpec(memory_space=pl.ANY)] * len(tokens)


def _inproj_fwd(x2d, g_norm, w_in_t, after=()):
    T = x2d.shape[0]
    tm = 512

    def body(x_ref, g_ref, w_ref, *rest):
        o_ref = rest[-1]
        _, xh = _rms(x_ref[...])
        h = (xh * g_ref[...]).astype(BF16)
        o_ref[...] = _nt(h, w_ref[...])

    return pl.pallas_call(
        body, grid=(T // tm,),
        in_specs=[pl.BlockSpec((tm, D_MODEL), lambda i: (i, 0)),
                  pl.BlockSpec((1, D_MODEL), lambda i: (0, 0)),
                  pl.BlockSpec((IN_COLS, D_MODEL), lambda i: (0, 0))] + _after(after),
        out_specs=pl.BlockSpec((tm, IN_COLS), lambda i: (i, 0)),
        out_shape=jax.ShapeDtypeStruct((T, IN_COLS), F32),
        compiler_params=_params(("arbitrary",)), name="inproj_fwd")(x2d, g_norm, w_in_t, *after)


def _kv_fwd(mem2d, g_mem, w_kv):
    Tm = mem2d.shape[0]

    def body(m_ref, g_ref, w_ref, o_ref):
        _, mh = _rms(m_ref[...])
        o_ref[...] = _nn((mh * g_ref[...]).astype(BF16), w_ref[...])

    return pl.pallas_call(
        body, out_shape=jax.ShapeDtypeStruct((Tm, 2 * MEM_WIDTH), F32),
        compiler_params=_params(), name="kv_fwd")(mem2d, g_mem, w_kv)


def _kv_bwd(mem2d, g_mem, w_kv, dkv):
    Tm = mem2d.shape[0]

    def body(m_ref, g_ref, w_ref, dkv_ref, dw_ref, dg_ref):
        _, mh = _rms(m_ref[...])
        memn = (mh * g_ref[...]).astype(BF16)
        dkvb = dkv_ref[...].astype(BF16)
        dw = _tn(memn, dkvb).astype(BF16)
        for k in range(N_CHIPS):
            dw_ref[k] = dw[k * (D_MODEL // N_CHIPS):(k + 1) * (D_MODEL // N_CHIPS), :]
        dmemn = _nt(dkvb, w_ref[...])
        dg_ref[...] = jnp.sum(dmemn * mh, axis=0, keepdims=True)

    return pl.pallas_call(
        body, out_shape=(jax.ShapeDtypeStruct((N_CHIPS, D_MODEL // N_CHIPS, 2 * MEM_WIDTH), BF16),
                         jax.ShapeDtypeStruct((1, D_MODEL), F32)),
        compiler_params=_params(), name="kv_bwd")(mem2d, g_mem, w_kv, dkv)


def _attn_geometry(S):
    geom = []
    for d in DILATIONS:
        L = S // d
        assert L % Q_BLOCK == 0
        geom.append((d, L, min(2 * Q_BLOCK, L), L // Q_BLOCK))
    return geom


def _init_bias(bias_scr, geom, hp):
    row = lax.broadcasted_iota(jnp.int32, (Q_BLOCK, 2 * Q_BLOCK), 0)
    col = lax.broadcasted_iota(jnp.int32, (Q_BLOCK, 2 * Q_BLOCK), 1)
    for j in (0, 1):
        bits = (126 - (2 * hp + j)) * (1 << 23)
        slope = lax.bitcast_convert_type(jnp.full((1, 1), bits, jnp.int32), F32)
        for di, (d, _, _, _) in enumerate(geom):
            for cls, off in enumerate((0, -RADIUS, -2 * RADIUS)):
                dist = jnp.abs(col - row + off)
                bias_scr[di * 6 + cls * 2 + j] = jnp.where(
                    dist <= RADIUS, -(slope * float(d)) * dist.astype(F32), NEG_INF)


SPLIT = 4
COPY_ROWS = 256


def _by4_rows(S, step):
    per_class = S // SPLIT // COPY_ROWS
    r, j = step // per_class, step % per_class
    return (pl.ds(r + SPLIT * j * COPY_ROWS, COPY_ROWS, stride=SPLIT),
            pl.ds(r * (S // SPLIT) + j * COPY_ROWS, COPY_ROWS))


def _to_by4(src, dst, S):
    for i in range(S // COPY_ROWS):
        natural, by4 = _by4_rows(S, i)
        dst[by4, :] = src[natural, :]


def _block_slices(d, L, KW, nqb, r, qb, S):
    qs = qb * Q_BLOCK
    ks = jnp.clip(qs - RADIUS, 0, L - KW)
    cls = jnp.where(qb == 0, 0, jnp.where(qb == nqb - 1, 2, 1))
    if d == 1:
        qsl = pl.ds(pl.multiple_of(qs, Q_BLOCK), Q_BLOCK)
        ksl = pl.ds(pl.multiple_of(ks, RADIUS), KW)
    elif d == SPLIT:
        qsl = pl.ds(pl.multiple_of(r * L + qs, Q_BLOCK), Q_BLOCK)
        ksl = pl.ds(pl.multiple_of(r * L + ks, RADIUS), KW)
    else:
        sub = d // SPLIT
        base = (r % SPLIT) * (S // SPLIT) + r // SPLIT
        qsl = pl.ds(base + qs * sub, Q_BLOCK, stride=sub)
        ksl = pl.ds(base + ks * sub, KW, stride=sub)
    return qsl, ksl, cls


def _for_groups(geom, S, group, fn):
    for di, (d, L, KW, nqb) in enumerate(geom):
        n = group[di]
        assert (d * nqb) % n == 0

        def step(it, carry, di=di, d=d, L=L, KW=KW, nqb=nqb, n=n):
            slices = []
            for g in range(n):
                i = it * n + g
                slices.append(_block_slices(d, L, KW, nqb, i // nqb, i % nqb, S))
            fn(di, KW, slices)
            return carry
        lax.fori_loop(0, d * nqb // n, step, 0)


def _attn_fwd(proj, B, S):
    T = B * S
    geom = _attn_geometry(S)
    n_pairs = ATTN_WIDTH // LANES

    def body(q_ref, k_ref, v_ref, a_ref, lse_ref, bias_scr, q4, k4, v4, *per_dilation):
        o_scr, m_scr, l_scr = per_dilation[0:3], per_dilation[3:6], per_dilation[6:9]
        lo, hm = _head_masks()
        pair = pl.program_id(0)

        @pl.when(pl.program_id(1) == 0)
        def _():
            _init_bias(bias_scr, geom, pair)
        for src, dst in ((q_ref, q4), (k_ref, k4), (v_ref, v4)):
            _to_by4(src, dst, S)

        def group(di, KW, all_slices):
            run = 8
            for first in range(0, len(all_slices), run):
                some(di, KW, all_slices[first:first + run])

        def some(di, KW, slices):
            chains = [(g, j) for g in range(len(slices)) for j in (0, 1)]
            q_src, k_src, v_src = (q_ref, k_ref, v_ref) if di == 0 else (q4, k4, v4)
            q = [q_src[qsl, :] for qsl, _, _ in slices]
            kw = [k_src[ksl, :].astype(BF16) for _, ksl, _ in slices]
            vw = [v_src[ksl, :].astype(BF16) for _, ksl, _ in slices]
            s = {(g, j): _nt((q[g] * (hm[j] * 0.125)).astype(BF16), kw[g])
                 + bias_scr[di * 6 + slices[g][2] * 2 + j, :, pl.ds(0, KW)] for g, j in chains}
            m = {c: jnp.max(s[c], axis=1, keepdims=True) for c in chains}
            p = {c: jnp.exp(s[c] - m[c]) for c in chains}
            l = {c: jnp.sum(p[c], axis=1, keepdims=True) for c in chains}
            o = {(g, j): _nn(p[(g, j)].astype(BF16), vw[g]) for g, j in chains}
            for g, (qsl, _, _) in enumerate(slices):
                o_scr[di][qsl, :] = jnp.where(lo, o[(g, 0)], o[(g, 1)])
                m_scr[di][qsl, :] = jnp.where(lo, m[(g, 0)], m[(g, 1)])
                l_scr[di][qsl, :] = jnp.where(lo, l[(g, 0)], l[(g, 1)])

        _for_groups(geom, S, (16, 16, 16), group)

        for i in range(S // COPY_ROWS):
            natural, by4 = _by4_rows(S, i)
            rows = [natural, by4, by4]
            ms = [m_scr[di][rows[di], :] for di in range(3)]
            mx = jnp.maximum(jnp.maximum(ms[0], ms[1]), ms[2])
            num = 0.0
            den = 0.0
            for di in range(3):
                w = jnp.exp(ms[di] - mx)
                num = num + w * o_scr[di][rows[di], :]
                den = den + w * l_scr[di][rows[di], :]
            a_ref[natural, :] = num / den
            lse_ref[natural, :] = mx + jnp.log(den)

    blk = lambda off: pl.BlockSpec((S, LANES), lambda h, b, off=off: (b, off + h))
    out_blk = pl.BlockSpec((S, LANES), lambda h, b: (b, h))
    return pl.pallas_call(
        body, grid=(n_pairs, B),
        in_specs=[blk(0), blk(n_pairs), blk(2 * n_pairs)],
        out_specs=[out_blk, out_blk],
        out_shape=[jax.ShapeDtypeStruct((T, ATTN_WIDTH), F32)] * 2,
        scratch_shapes=[pltpu.VMEM((18, Q_BLOCK, 2 * Q_BLOCK), F32)] + [pltpu.VMEM((S, LANES), F32)] * 12,
        compiler_params=_params(("arbitrary", "arbitrary")), name="attn_fwd")(proj, proj, proj)


def _attn_bwd(proj, a, lse, da, B, S, after=()):
    T = B * S
    geom = _attn_geometry(S)
    n_pairs = ATTN_WIDTH // LANES

    def body(q_ref, k_ref, v_ref, a_ref, lse_ref, do_ref, *rest):
        dq_ref, dk_ref, dv_ref, bias_scr = rest[len(after):len(after) + 4]
        scr = rest[len(after) + 4:]
        acc = (scr[0:3], scr[3:6])
        natural_in = (q_ref, k_ref, v_ref, a_ref, lse_ref, do_ref)
        by4_in = scr[6:12]
        _, hm = _head_masks()
        pair = pl.program_id(0)

        @pl.when(pl.program_id(1) == 0)
        def _():
            _init_bias(bias_scr, geom, pair)
        for ref in scr[0:6]:
            ref[...] = jnp.zeros_like(ref)
        for src, dst in zip(natural_in, by4_in):
            _to_by4(src, dst, S)

        def group(di, KW, all_slices):
            run = (4, 4, 8)[di]
            for first in range(0, len(all_slices), run):
                some(di, KW, all_slices[first:first + run])

        def some(di, KW, slices):
            n = len(slices)
            chains = [(g, j) for g in range(n) for j in (0, 1)]
            q_src, k_src, v_src, a_src, lse_src, do_src = natural_in if di == 0 else by4_in
            dq_scr, dk_scr, dv_scr = acc[0 if di == 0 else 1]
            q = [q_src[qsl, :] for qsl, _, _ in slices]
            do = [do_src[qsl, :] for qsl, _, _ in slices]
            doa = [do[g] * a_src[slices[g][0], :] for g in range(n)]
            lse_q = [lse_src[qsl, :] for qsl, _, _ in slices]
            kw = [k_src[ksl, :].astype(BF16) for _, ksl, _ in slices]
            vw = [v_src[ksl, :].astype(BF16) for _, ksl, _ in slices]
            qj = {(g, j): (q[g] * (hm[j] * 0.125)).astype(BF16) for g, j in chains}
            doj = {(g, j): (do[g] * hm[j]).astype(BF16) for g, j in chains}
            s = {(g, j): _nt(qj[(g, j)], kw[g])
                 + bias_scr[di * 6 + slices[g][2] * 2 + j, :, pl.ds(0, KW)] for g, j in chains}
            dp = {(g, j): _nt(doj[(g, j)], vw[g]) for g, j in chains}
            dsum = {(g, j): jnp.sum(doa[g] * hm[j], axis=1, keepdims=True) for g, j in chains}
            p = {(g, j): jnp.exp(s[(g, j)] - lse_q[g][:, HEAD_DIM * j:HEAD_DIM * j + 1]) for g, j in chains}
            ds = {c: (p[c] * (dp[c] - dsum[c])).astype(BF16) for c in chains}
            pb = {c: p[c].astype(BF16) for c in chains}
            dq = [_nn(ds[(g, 0)], kw[g]) * (hm[0] * 0.125) + _nn(ds[(g, 1)], kw[g]) * (hm[1] * 0.125)
                  for g in range(n)]
            both = lambda t, g: jnp.concatenate([t[(g, 0)], t[(g, 1)]], axis=0)
            dkw = [_tn(both(ds, g), both(qj, g)) for g in range(n)]
            dvw = [_tn(both(pb, g), both(doj, g)) for g in range(n)]
            for g, (qsl, ksl, _) in enumerate(slices):
                dq_scr[qsl, :] = dq_scr[qsl, :] + dq[g]
                dk_scr[ksl, :] = dk_scr[ksl, :] + dkw[g]
                dv_scr[ksl, :] = dv_scr[ksl, :] + dvw[g]

        _for_groups(geom, S, (16, 16, 16), group)

        for i in range(S // COPY_ROWS):
            natural, by4 = _by4_rows(S, i)
            for nat, split in zip(*acc):
                nat[natural, :] = nat[natural, :] + split[by4, :]
        for out, nat in zip((dq_ref, dk_ref, dv_ref), acc[0]):
            out[...] = nat[...].astype(BF16)

    blk = lambda off: pl.BlockSpec((S, LANES), lambda h, b, off=off: (b, off + h))
    return pl.pallas_call(
        body, grid=(n_pairs, B),
        in_specs=[blk(0), blk(n_pairs), blk(2 * n_pairs), blk(0), blk(0), blk(0)] + _after(after),
        out_specs=[blk(0), blk(0), blk(0)],
        out_shape=[jax.ShapeDtypeStruct((T, ATTN_WIDTH), BF16)] * 3,
        scratch_shapes=[pltpu.VMEM((18, Q_BLOCK, 2 * Q_BLOCK), F32)] + [pltpu.VMEM((S, LANES), F32)] * 12,
        compiler_params=_params(("arbitrary", "arbitrary")), name="attn_bwd")(proj, proj, proj, a, lse, da, *after)


def _mid(x2d, t2d, a, proj, kv, w_s, w_sT, b_tab, g_v, w_out, g_final, B, S):
    T = B * S
    tm = 512
    nt = S // tm
    halves = 2
    hrows = tm // halves

    def body(x_ref, t_ref, a_ref, za_ref, ub_ref, vb_ref, zb_ref, qm_ref, zm_ref, kv_ref,
              ws_ref, wsT_ref, btab_ref, gv_ref, wout_ref, gf_ref,
              dx2_ref, da_ref, drest_ref, loss_ref, dwout_bf_ref, dws_ref, dbs_ref, dgv_ref, dgf_ref, dkv_ref,
              dbtab_scr, dwout_ref):
        b = pl.program_id(0)
        t = pl.program_id(1)
        first = jnp.logical_and(b == 0, t == 0)
        last = jnp.logical_and(b == B - 1, t == nt - 1)
        _, hm = _head_masks()
        lane_g = lax.broadcasted_iota(jnp.int32, (1, SGU_WIDTH), 1) // HEAD_DIM
        gm = [(lane_g == g).astype(F32) for g in range(N_SGU_GROUPS)]
        H = range(halves)
        rows = [pl.ds(h * hrows, hrows) for h in H]
        ld = lambda ref: [ref[r, :] for r in rows]
        cat = lambda parts, axis: jnp.concatenate(parts, axis=axis)
        chunks = [slice(ci * SGU_CHUNK, (ci + 1) * SGU_CHUNK) for ci in range(hrows // SGU_CHUNK)]
        pairs = [slice(pr * LANES, (pr + 1) * LANES) for pr in range(2)]
        heads = [(pr, j) for pr in range(2) for j in (0, 1)]

        @pl.when(first)
        def _():
            loss_ref[...] = jnp.zeros_like(loss_ref)
            dwout_ref[...] = jnp.zeros_like(dwout_ref)
            dws_ref[...] = jnp.zeros_like(dws_ref)
            dbs_ref[...] = jnp.zeros_like(dbs_ref)
            dgv_ref[...] = jnp.zeros_like(dgv_ref)
            dgf_ref[...] = jnp.zeros_like(dgf_ref)
            dbtab_scr[...] = jnp.zeros_like(dbtab_scr)

        @pl.when(t == 0)
        def _():
            dkv_ref[...] = jnp.zeros_like(dkv_ref)

        a_val = ld(a_ref)
        sil_a = [_silu_parts(z) for z in ld(za_ref)]
        gated_a = [s[0] * a for s, a in zip(sil_a, a_val)]
        u = [_gelu_parts(z) for z in ld(ub_ref)]
        vv = [_gelu_parts(z) for z in ld(vb_ref)]
        vnorm = [_rms(v[0]) for v in vv]
        gv = gv_ref[...]
        vn = [(n[1] * gv).astype(BF16) for n in vnorm]
        w_cat = cat([ws_ref[g].astype(BF16) for g in range(N_SGU_GROUPS)], 1)
        wT_cat = cat([wsT_ref[g].astype(BF16) for g in range(N_SGU_GROUPS)], 1)
        gmb = [m.astype(BF16) for m in gm]
        by_group = lambda chunk: cat([chunk * gmb[g] for g in range(N_SGU_GROUPS)], 0)
        btab = btab_ref[...]
        mixed = [cat([btab + _nn(w_cat, by_group(vn[h][c, :])) for c in chunks], 0) for h in H]
        sg = [u[h][0] * mixed[h] for h in H]
        sil_b = [_silu_parts(z) for z in ld(zb_ref)]
        gated_b = [sil_b[h][0] * sg[h] for h in H]

        kvv = kv_ref[...].astype(BF16)
        kp = [kvv[:, p] for p in pairs]
        vp = [kvv[:, MEM_WIDTH + pr * LANES:MEM_WIDTH + (pr + 1) * LANES] for pr in range(2)]
        qm = ld(qm_ref)
        qj = {(h, pr, j): (qm[h][:, pairs[pr]] * (hm[j] * 0.125)).astype(BF16) for h in H for pr, j in heads}
        sc = {k: _nt(qj[k], kp[k[1]]) for k in qj}
        ex = {k: jnp.exp(sc[k] - jnp.max(sc[k], axis=1, keepdims=True)) for k in qj}
        prob = {k: ex[k] * (1.0 / jnp.sum(ex[k], axis=1, keepdims=True)) for k in qj}
        probb = {k: prob[k].astype(BF16) for k in qj}
        mo = [cat([sum(_nn(probb[(h, pr, j)], vp[pr]) * hm[j] for j in (0, 1)) for pr in range(2)], 1) for h in H]
        sil_m = [_silu_parts(z) for z in ld(zm_ref)]
        gated_m = [sil_m[h][0] * mo[h] for h in H]

        gated = [cat([gated_a[h], gated_b[h], gated_m[h]], 1).astype(BF16) for h in H]
        wout = wout_ref[...]
        x_in = ld(x_ref)
        x2 = [x_in[h] + _nn(gated[h], wout) for h in H]
        fin = [_rms(z) for z in x2]
        gf = gf_ref[...]
        tgt = ld(t_ref)
        err = [fin[h][1] * gf - tgt[h] for h in H]
        loss_ref[...] += sum(jnp.sum(e * e) for e in err) * (0.5 / D_MODEL)

        dy = [e * (1.0 / D_MODEL) for e in err]
        dgf_ref[...] += sum(jnp.sum(dy[h] * fin[h][1], axis=0, keepdims=True) for h in H)
        gdy = [d * gf for d in dy]
        dx2 = [fin[h][0] * (gdy[h] - fin[h][1] * jnp.mean(gdy[h] * fin[h][1], axis=1, keepdims=True)) for h in H]
        for h in H:
            dx2_ref[rows[h], :] = dx2[h]
        dx2b = [d.astype(BF16) for d in dx2]
        dgated = [_nt(d, wout) for d in dx2b]
        dwout_ref[...] += _tn(cat(gated, 0), cat(dx2b, 0))
        dga = [d[:, 0:ATTN_WIDTH] for d in dgated]
        dgb = [d[:, ATTN_WIDTH:ATTN_WIDTH + SGU_WIDTH] for d in dgated]
        dgm = [d[:, ATTN_WIDTH + SGU_WIDTH:] for d in dgated]

        for h in H:
            da_ref[rows[h], :] = dga[h] * sil_a[h][0]
        dza = [dga[h] * a_val[h] * sil_a[h][1] for h in H]

        dsg = [dgb[h] * sil_b[h][0] for h in H]
        dzb = [dgb[h] * sg[h] * sil_b[h][1] for h in H]
        dub = [dsg[h] * mixed[h] * u[h][1] for h in H]
        dmixed = [dsg[h] * u[h][0] for h in H]
        dmixed_b = [d.astype(BF16) for d in dmixed]
        dvn = [cat([_nn(wT_cat, by_group(dmixed_b[h][c, :])) for c in chunks], 0) for h in H]
        for g in range(N_SGU_GROUPS):
            dws_ref[g] += sum(_nt((dmixed[h][c, :] * gm[g]).astype(BF16), vn[h][c, :]) for h in H for c in chunks)
        dbtab_scr[...] += sum(dmixed[h][c, :] for h in H for c in chunks)
        dgv_ref[...] += sum(jnp.sum(dvn[h] * vnorm[h][1], axis=0, keepdims=True) for h in H)
        tv = [d * gv for d in dvn]
        dvv = [vnorm[h][0] * (tv[h] - vnorm[h][1] * jnp.mean(tv[h] * vnorm[h][1], axis=1, keepdims=True)) for h in H]
        dvb = [dvv[h] * vv[h][1] for h in H]

        dmo = [dgm[h] * sil_m[h][0] for h in H]
        dzm = [dgm[h] * mo[h] * sil_m[h][1] for h in H]
        dmoj = {(h, pr, j): (dmo[h][:, pairs[pr]] * hm[j]).astype(BF16) for h in H for pr, j in heads}
        dp = {k: _nt(dmoj[k], vp[k[1]]) for k in qj}
        ds = {k: (prob[k] * (dp[k] - jnp.sum(dp[k] * prob[k], axis=1, keepdims=True))).astype(BF16) for k in qj}
        dqm = [cat([sum(_nn(ds[(h, pr, j)], kp[pr]) * (hm[j] * 0.125) for j in (0, 1)) for pr in range(2)], 1)
               for h in H]
        every = lambda tbl, pr: cat([tbl[(h, pr, j)] for h in H for j in (0, 1)], 0)
        dk = [_tn(every(ds, pr), every(qj, pr)) for pr in range(2)]
        dv = [_tn(every(probb, pr), every(dmoj, pr)) for pr in range(2)]
        dkv_ref[...] += cat(dk + dv, 1)

        for h in H:
            drest_ref[rows[h], :] = cat([dza[h], dub[h], dvb[h], dzb[h], dqm[h], dzm[h]], 1).astype(BF16)

        @pl.when(last)
        def _():
            lane = lax.broadcasted_iota(jnp.int32, (1, LANES), 1)
            dbt = dbtab_scr[...]
            out = jnp.zeros((SGU_CHUNK, LANES), F32)
            for g in range(N_SGU_GROUPS):
                out = out + jnp.where(lane == g, jnp.sum(dbt * gm[g], axis=1, keepdims=True), 0.0)
            dbs_ref[...] = out
            for r0 in range(0, D_MODEL, SGU_CHUNK):
                k, row = divmod(r0, D_MODEL // N_CHIPS)
                dwout_bf_ref[k, row:row + SGU_CHUNK, :] = dwout_ref[r0:r0 + SGU_CHUNK, :].astype(BF16)

    tile = lambda w, cb: pl.BlockSpec((tm, w), lambda b, t, cb=cb: (b * nt + t, cb))
    const = lambda shape: pl.BlockSpec(shape, lambda b, t, n=len(shape): (0,) * n)
    return pl.pallas_call(
        body, grid=(B, nt),
        in_specs=[tile(D_MODEL, 0), tile(D_MODEL, 0), tile(ATTN_WIDTH, 0),
                  tile(ATTN_WIDTH, 3),
                  tile(SGU_WIDTH, 8), tile(SGU_WIDTH, 9), tile(SGU_WIDTH, 10),
                  tile(MEM_WIDTH, 11), tile(MEM_WIDTH, 12),
                  pl.BlockSpec((N_MEM, 2 * MEM_WIDTH), lambda b, t: (b, 0)),
                  const((N_SGU_GROUPS, SGU_CHUNK, SGU_CHUNK)), const((N_SGU_GROUPS, SGU_CHUNK, SGU_CHUNK)),
                  const((SGU_CHUNK, SGU_WIDTH)), const((1, SGU_WIDTH)),
                  const((D_MODEL, D_MODEL)), const((1, D_MODEL))],
        out_specs=[tile(D_MODEL, 0), tile(ATTN_WIDTH, 0), tile(REST_COLS, 0),
                   const((8, LANES)), const((N_CHIPS, D_MODEL // N_CHIPS, D_MODEL)),
                   const((N_SGU_GROUPS, SGU_CHUNK, SGU_CHUNK)), const((SGU_CHUNK, LANES)),
                   const((1, SGU_WIDTH)), const((1, D_MODEL)),
                   pl.BlockSpec((N_MEM, 2 * MEM_WIDTH), lambda b, t: (b, 0))],
        out_shape=[jax.ShapeDtypeStruct((T, D_MODEL), F32), jax.ShapeDtypeStruct((T, ATTN_WIDTH), F32),
                   jax.ShapeDtypeStruct((T, REST_COLS), BF16),
                   jax.ShapeDtypeStruct((8, LANES), F32),
                   jax.ShapeDtypeStruct((N_CHIPS, D_MODEL // N_CHIPS, D_MODEL), BF16),
                   jax.ShapeDtypeStruct((N_SGU_GROUPS, SGU_CHUNK, SGU_CHUNK), F32),
                   jax.ShapeDtypeStruct((SGU_CHUNK, LANES), F32),
                   jax.ShapeDtypeStruct((1, SGU_WIDTH), F32), jax.ShapeDtypeStruct((1, D_MODEL), F32),
                   jax.ShapeDtypeStruct((B * N_MEM, 2 * MEM_WIDTH), F32)],
        scratch_shapes=[pltpu.VMEM((SGU_CHUNK, SGU_WIDTH), F32), pltpu.VMEM((D_MODEL, D_MODEL), F32)],
        compiler_params=_params(("arbitrary", "arbitrary"), vmem=VMEM_LIMIT + 2 * 1024 * 1024), name="mid")(
            x2d, t2d, a, proj, proj, proj, proj, proj, proj, kv, w_s, w_sT, b_tab, g_v, w_out, g_final)


def _inproj_bwd_dx(dq, dk, dv, drest, x2d, dx2, g_norm, w_in_t, after=()):
    T = x2d.shape[0]
    tm = 512
    W = ATTN_WIDTH

    def body(dq_ref, dk_ref, dv_ref, dr_ref, x_ref, dx2_ref, g_ref, w_ref, *rest):
        gx_ref, dg_ref = rest[-2:]

        @pl.when(pl.program_id(0) == 0)
        def _():
            dg_ref[...] = jnp.zeros_like(dg_ref)

        halves = [pl.ds(h * (tm // 2), tm // 2) for h in (0, 1)]
        dh = [(_nn(dq_ref[r, :], w_ref[0:W, :]) + _nn(dk_ref[r, :], w_ref[W:2 * W, :])
               + _nn(dv_ref[r, :], w_ref[2 * W:3 * W, :]) + _nn(dr_ref[r, :], w_ref[QKV_COLS:IN_COLS, :]))
              for r in halves]
        nrm = [_rms(x_ref[r, :]) for r in halves]
        dg_ref[...] += sum(jnp.sum(d * n[1], axis=0, keepdims=True) for d, n in zip(dh, nrm))
        g = g_ref[...]
        for r, d, (rstd, xh) in zip(halves, dh, nrm):
            th = d * g
            gx_ref[r, :] = rstd * (th - xh * jnp.mean(th * xh, axis=1, keepdims=True)) + dx2_ref[r, :]

    tile = lambda w: pl.BlockSpec((tm, w), lambda i: (i, 0))
    return pl.pallas_call(
        body, grid=(T // tm,),
        in_specs=[tile(W), tile(W), tile(W), tile(REST_COLS), tile(D_MODEL), tile(D_MODEL),
                  pl.BlockSpec((1, D_MODEL), lambda i: (0, 0)),
                  pl.BlockSpec((IN_COLS, D_MODEL), lambda i: (0, 0))] + _after(after),
        out_specs=[tile(D_MODEL), pl.BlockSpec((1, D_MODEL), lambda i: (0, 0))],
        out_shape=[jax.ShapeDtypeStruct((T, D_MODEL), F32), jax.ShapeDtypeStruct((1, D_MODEL), F32)],
        compiler_params=_params(("arbitrary",)), name="inproj_bwd_dx")(
            dq, dk, dv, drest, x2d, dx2, g_norm, w_in_t, *after)


def _inproj_bwd_dw(dq, dk, dv, drest, x2d, g_norm, reduce=False):
    T = x2d.shape[0]
    tm = 512
    nt = T // tm
    W = ATTN_WIDTH
    shard = IN_COLS // N_CHIPS
    half = shard // 2
    row_block = 32

    def body(dq_ref, dk_ref, dv_ref, dr_ref, x_ref, g_ref, *rest):
        if reduce:
            sends_out, own_out, acc, ras, narrow, sends, own, s_sem, r_sem, put_sem = rest
            x, y, c, chip, peers, peer_chip = _place()
            sib = (x, y, 1 - c)

            def part(k, cc, r0):
                return acc.at[pl.ds(pl.multiple_of(k * shard + cc * half + r0, 8), row_block), :]

            def swap_win(k):
                return _remote(narrow.at[k], ras.at[k], s_sem.at[k], r_sem.at[k], sib)

            def put_send(m):
                return pltpu.make_async_copy(sends.at[m], sends_out.at[m], put_sem.at[m])

            put_own = pltpu.make_async_copy(own, own_out, put_sem.at[N_CHIPS - 1])
        else:
            acc = rest[0]

        @pl.when(pl.program_id(0) == 0)
        def _():
            acc[...] = jnp.zeros_like(acc)

        _, xh = _rms(x_ref[...])
        h = (xh * g_ref[...]).astype(BF16)
        acc[0:W, :] += _tn(dq_ref[...], h)
        acc[W:2 * W, :] += _tn(dk_ref[...], h)
        acc[2 * W:3 * W, :] += _tn(dv_ref[...], h)
        acc[QKV_COLS:IN_COLS, :] += _tn(dr_ref[...], h)

        if reduce:
            @pl.when(pl.program_id(0) == nt - 1)
            def _():
                for k in range(N_CHIPS):
                    def to_bf16(i, carry, k=k):
                        r0 = pl.multiple_of(i * row_block, row_block)
                        narrow[k, pl.ds(r0, row_block), :] = part(k, 1 - c, r0)[...].astype(BF16)
                        return carry
                    lax.fori_loop(0, half // row_block, to_bf16, 0)
                    swap_win(k).start()

                def chip_sum(k, r0):
                    return part(k, c, r0)[...] + ras[k, pl.ds(r0, row_block), :].astype(F32)

                for k in range(N_CHIPS):
                    swap_win(k).wait_recv()

                    @pl.when(chip == k)
                    def _(k=k):
                        def mine(i, carry):
                            r0 = pl.multiple_of(i * row_block, row_block)
                            own[pl.ds(r0, row_block), :] = chip_sum(k, r0)
                            return carry
                        lax.fori_loop(0, half // row_block, mine, 0)
                        put_own.start()

                    @pl.when(chip != k)
                    def _(k=k):
                        def other(i, carry):
                            r0 = pl.multiple_of(i * row_block, row_block)
                            sends[(k ^ chip) - 1, pl.ds(r0, row_block), :] = chip_sum(k, r0).astype(BF16)
                            return carry
                        lax.fori_loop(0, half // row_block, other, 0)
                        put_send((k ^ chip) - 1).start()
                for k in range(N_CHIPS):
                    swap_win(k).wait_send()
                for m in range(N_CHIPS - 1):
                    put_send(m).wait()
                put_own.wait()

    tile = lambda w: pl.BlockSpec((tm, w), lambda i: (i, 0))
    vmem = pl.BlockSpec(memory_space=pltpu.VMEM)
    in_specs = [tile(W), tile(W), tile(W), tile(REST_COLS), tile(D_MODEL), pl.BlockSpec((1, D_MODEL), lambda i: (0, 0))]
    if not reduce:
        return pl.pallas_call(
            body, grid=(nt,), in_specs=in_specs,
            out_specs=pl.BlockSpec((IN_COLS, D_MODEL), lambda i: (0, 0)),
            out_shape=jax.ShapeDtypeStruct((IN_COLS, D_MODEL), F32),
            compiler_params=_params(("arbitrary",)), name="inproj_bwd_dw")(dq, dk, dv, drest, x2d, g_norm)
    quarters = pltpu.VMEM((N_CHIPS, half, D_MODEL), BF16)
    far = pl.BlockSpec(memory_space=pl.ANY)
    return pl.pallas_call(
        body, grid=(nt,), in_specs=in_specs, out_specs=[far] * 2,
        out_shape=[jax.ShapeDtypeStruct((N_CHIPS - 1, half, D_MODEL), BF16),
                   jax.ShapeDtypeStruct((half, D_MODEL), F32)],
        scratch_shapes=[pltpu.VMEM((IN_COLS, D_MODEL), F32), quarters, quarters,
                        pltpu.VMEM((N_CHIPS - 1, half, D_MODEL), BF16), pltpu.VMEM((half, D_MODEL), F32),
                        pltpu.SemaphoreType.DMA((N_CHIPS,)), pltpu.SemaphoreType.DMA((N_CHIPS,)),
                        pltpu.SemaphoreType.DMA((N_CHIPS,))],
        compiler_params=_params(("arbitrary",)), name="inproj_bwd_dw_reduce")(dq, dk, dv, drest, x2d, g_norm)


def _adamw_update(w, g, m, v):
    nm = ADAM_B1 * m + (1.0 - ADAM_B1) * g
    nv = ADAM_B2 * v + (1.0 - ADAM_B2) * (g * g)
    m_hat = nm / (1.0 - ADAM_B1 ** ADAM_STEP)
    v_hat = nv / (1.0 - ADAM_B2 ** ADAM_STEP)
    return -ADAM_LR * (m_hat / (jnp.sqrt(v_hat) + ADAM_EPS) + ADAM_WD * w), nm, nv


def _adamw_all(g_packed, ws, ms, vs, large):
    n, nl = len(ws), len(large)
    row_block = 8
    chunk_bytes = 256 * 1024

    def chunk_rows(w):
        R, C = w.shape
        return max(r for r in range(8, R + 1, 8) if R % r == 0 and r * C * 4 <= chunk_bytes)

    jobs = [(b, r0, chunk_rows(four[0])) for b, four in enumerate(large)
            for r0 in range(0, four[0].shape[0], chunk_rows(four[0]))]

    def body(*refs):
        g_ref = refs[0]
        w_refs, m_refs, v_refs = refs[1:1 + n], refs[1 + n:1 + 2 * n], refs[1 + 2 * n:1 + 3 * n]
        far_in = refs[1 + 3 * n:1 + 3 * n + 4 * nl]
        outs = refs[1 + 3 * n + 4 * nl:1 + 7 * n + 4 * nl]
        far_out = refs[1 + 7 * n + 4 * nl:1 + 7 * n + 8 * nl]
        loss_ref = refs[1 + 7 * n + 8 * nl]
        scr = refs[2 + 7 * n + 8 * nl:]
        in_scr, out_scr, in_sem, out_sem = scr[:4 * nl], scr[4 * nl:7 * nl], scr[7 * nl], scr[7 * nl + 1]

        def read(j, k):
            b, r0, rows = jobs[j]
            blk = pl.ds(r0, rows)
            return pltpu.make_async_copy(far_in[4 * b + k].at[blk, :], in_scr[4 * b + k].at[blk, :],
                                         in_sem.at[4 * j + k])

        def write(j, k):
            b, r0, rows = jobs[j]
            blk = pl.ds(r0, rows)
            src = in_scr[4 * b + 1] if k == 0 else out_scr[3 * b + k - 1]
            return pltpu.make_async_copy(src.at[blk, :], far_out[4 * b + k].at[blk, :], out_sem.at[4 * j + k])

        for j in range(len(jobs)):
            for k in range(4):
                read(j, k).start()
        off = 0
        for i, (_, used, padded) in enumerate(_SMALL_PARTS[:n]):
            g = g_ref[off:off + used, :]
            delta, nm, nv = _adamw_update(w_refs[i][...], g, m_refs[i][...], v_refs[i][...])
            outs[4 * i][...], outs[4 * i + 1][...], outs[4 * i + 2][...], outs[4 * i + 3][...] = g, delta, nm, nv
            off += padded
        loss_ref[...] = g_ref[_LOSS_ROW:_LOSS_ROW + 1, 0:1]
        for j, (b, r0, rows) in enumerate(jobs):
            for k in range(4):
                read(j, k).wait()

            def update(i, carry, b=b, r0=r0):
                blk = pl.ds(pl.multiple_of(r0 + i * row_block, row_block), row_block)
                w, g, m, v = [in_scr[4 * b + k][blk, :] for k in range(4)]
                for k, val in enumerate(_adamw_update(w, g, m, v)):
                    out_scr[3 * b + k][blk, :] = val
                return carry
            lax.fori_loop(0, rows // row_block, update, 0)
            for k in range(4):
                write(j, k).start()
        for j in range(len(jobs)):
            for k in range(4):
                write(j, k).wait()

    vmem = pl.BlockSpec(memory_space=pltpu.VMEM)
    far = pl.BlockSpec(memory_space=pl.ANY)
    flat = [a for four in large for a in four]
    outs = pl.pallas_call(
        body, in_specs=[vmem] * (1 + 3 * n) + [far] * (4 * nl),
        out_specs=[vmem] * (4 * n) + [far] * (4 * nl) + [vmem],
        out_shape=[jax.ShapeDtypeStruct(w.shape, F32) for w in ws for _ in range(4)]
        + [jax.ShapeDtypeStruct(four[0].shape, F32) for four in large for _ in range(4)]
        + [jax.ShapeDtypeStruct((1, 1), F32)],
        scratch_shapes=[pltpu.VMEM(a.shape, F32) for a in flat]
        + [pltpu.VMEM(four[0].shape, F32) for four in large for _ in range(3)]
        + [pltpu.SemaphoreType.DMA((4 * len(jobs),)), pltpu.SemaphoreType.DMA((4 * len(jobs),))],
        compiler_params=_params(), name="adamw_all")(g_packed, *ws, *ms, *vs, *flat)
    return ([outs[4 * i:4 * i + 4] for i in range(n)], [outs[4 * (n + i):4 * (n + i) + 4] for i in range(nl)],
            outs[4 * (n + nl)])


def _place():
    x, y, c = lax.axis_index("x"), lax.axis_index("y"), lax.axis_index("c")
    chip = 2 * x + y
    peers = [(x, 1 - y), (1 - x, y), (1 - x, 1 - y)]
    peer_chip = [2 * px + py for px, py in peers]
    return x, y, c, chip, peers, peer_chip


def _remote(src, dst, send_sem, recv_sem, dev):
    return pltpu.make_async_remote_copy(src_ref=src, dst_ref=dst, send_sem=send_sem, recv_sem=recv_sem,
                                        device_id=dev, device_id_type=MESH)


def _ag_weights(weights, late=()):
    nw, nl = len(weights), len(late)
    pieces = 2

    def body(*refs):
        srcs, late_srcs = refs[:nw], refs[nw:nw + nl]
        outs, late_bf, late_land = (refs[nw + nl:2 * nw + nl], refs[2 * nw + nl:2 * nw + 2 * nl],
                                    refs[2 * nw + 2 * nl:2 * nw + 3 * nl])
        scr = refs[2 * nw + 3 * nl:]
        wide, narrow = scr[:nw], scr[nw:2 * nw]
        late_scr, late_wide = scr[2 * nw:2 * nw + nl], scr[2 * nw + nl:2 * nw + 2 * nl]
        in_sem, put_sem, late_sem, get_sem, s_ici, r_ici, s_d2d, r_d2d = scr[2 * nw + 2 * nl:]
        x, y, c = lax.axis_index("x"), lax.axis_index("y"), lax.axis_index("c")
        chip = 2 * x + y
        sib = (x, y, 1 - c)
        first = ((x + 1 - c) % 2, (y + c) % 2)
        second = ((x + c) % 2, (y + 1 - c) % 2)
        first_chip, second_chip = 2 * first[0] + first[1], 2 * second[0] + second[1]
        diag_chip = 3 - chip

        parts = [(w, out, pc) for w, out in enumerate(outs) for pc in range(pieces)]

        def rows_of(out, cc, pc):
            rows = out.shape[1] // 2 // pieces
            return pl.ds(pl.multiple_of((cc * pieces + pc) * rows, 16), rows)

        def piece(out, k, cc, pc):
            return out.at[k, rows_of(out, cc, pc), :]

        def ici(w, slot, out, k, dev, pc, src=None):
            blk, sem = piece(out, k, c, pc), (nw * slot + w) * pieces + pc
            return _remote(blk if src is None else src, blk, s_ici.at[sem], r_ici.at[sem], (dev[0], dev[1], c))

        def d2d(w, slot, out, k, cc, pc):
            blk, sem = piece(out, k, cc, pc), (nw * slot + w) * pieces + pc
            return _remote(blk, blk, s_d2d.at[sem], r_d2d.at[sem], sib)

        def read(w, cc, pc):
            rows = rows_of(outs[w], cc, pc)
            return pltpu.make_async_copy(srcs[w].at[rows, :], wide[w].at[rows, :],
                                         in_sem.at[(w * 2 + cc) * pieces + pc])

        order = [(w, cc, pc) for cc in (0, 1) for w in range(nw) for pc in range(pieces)]
        for w, cc, pc in order:
            read(w, (c + cc) % 2, pc).start()
        late_gets = [pltpu.make_async_copy(src, dst, get_sem.at[i])
                     for i, (src, dst) in enumerate(zip(late_srcs, late_wide))]
        for cp in late_gets:
            cp.start()
        sent = []
        for w, cc, pc in order[:nw * pieces]:
            rows = rows_of(outs[w], c, pc)
            read(w, c, pc).wait()
            narrow[w][rows, :] = wide[w][rows, :].astype(BF16)
            for slot, dev in enumerate((first, second)):
                sent.append(ici(w, slot, outs[w], chip, dev, pc, src=narrow[w].at[rows, :]))
                sent[-1].start()
        late_puts = []
        for i, (src, scr_bf, bf, land) in enumerate(zip(late_wide, late_scr, late_bf, late_land)):
            late_gets[i].wait()
            scr_bf[...] = src[...].astype(BF16)
            for k, dst in enumerate([bf] + [land.at[s] for s in range(N_CHIPS)]):
                late_puts.append(pltpu.make_async_copy(scr_bf, dst, late_sem.at[i * (N_CHIPS + 1) + k]))
                late_puts[-1].start()
        for w, cc, pc in order[nw * pieces:]:
            rows = rows_of(outs[w], 1 - c, pc)
            read(w, 1 - c, pc).wait()
            narrow[w][rows, :] = wide[w][rows, :].astype(BF16)
        puts = [pltpu.make_async_copy(narrow[w], outs[w].at[chip], put_sem.at[w]) for w in range(nw)]
        for cp in puts:
            cp.start()
        for slot, k, dev in ((0, first_chip, first), (1, second_chip, second), (2, diag_chip, second)):
            for w, out, pc in parts:
                ici(w, slot, out, k, dev, pc).wait_recv()
                if slot == 0:
                    sent.append(ici(w, 2, out, k, second, pc))
                    sent[-1].start()
                sent.append(d2d(w, slot, out, k, c, pc))
                sent[-1].start()
        for slot, k in ((0, second_chip), (1, first_chip), (2, diag_chip)):
            for w, out, pc in parts:
                d2d(w, slot, out, k, 1 - c, pc).wait_recv()
        for cp in sent:
            cp.wait_send()
        for cp in puts + late_puts:
            cp.wait()

    vmem = pl.BlockSpec(memory_space=pltpu.VMEM)
    far = pl.BlockSpec(memory_space=pl.ANY)
    outs = pl.pallas_call(
        body,
        out_shape=[jax.ShapeDtypeStruct((N_CHIPS,) + w.shape, BF16) for w in weights]
        + [jax.ShapeDtypeStruct(w.shape, BF16) for w in late]
        + [jax.ShapeDtypeStruct((N_CHIPS,) + w.shape, BF16) for w in late],
        in_specs=[far] * (nw + nl), out_specs=[far] * (nw + 2 * nl),
        scratch_shapes=[pltpu.VMEM(w.shape, F32) for w in weights] + [pltpu.VMEM(w.shape, BF16) for w in weights]
        + [pltpu.VMEM(w.shape, BF16) for w in late] + [pltpu.VMEM(w.shape, F32) for w in late]
        + [pltpu.SemaphoreType.DMA((2 * nw * pieces,)), pltpu.SemaphoreType.DMA((nw,)),
           pltpu.SemaphoreType.DMA((nl * (N_CHIPS + 1),)), pltpu.SemaphoreType.DMA((nl,))]
        + [pltpu.SemaphoreType.DMA((3 * nw * pieces,))] * 4,
        compiler_params=pltpu.CompilerParams(vmem_limit_bytes=VMEM_LIMIT), name="ag_weights")(*weights, *late)
    return outs[:nw], outs[nw:nw + nl], outs[nw + nl:]


_HBM = pl.BlockSpec(memory_space=pltpu.HBM)
_SEM = pl.BlockSpec(memory_space=pltpu.SEMAPHORE)
_ANY = pl.BlockSpec(memory_space=pl.ANY)
_DATAFLOW = pltpu.SideEffectType.DATAFLOW_SIDE_EFFECTING


def _in_hbm(a):
    return pltpu.with_memory_space_constraint(a, pltpu.HBM)


_PEERS_OF = {"gather": 3, "scatter": 3, "direct": 7}


def _exchange_copies(mode, srcs, lands, send_sems, recv_sems):
    nw = len(srcs)
    x, y, c, chip, peers, peer_chip = _place()
    pairs = []
    if mode == "direct":
        targets = [((x, y), chip, 1)] + [(p, k, d) for p, k in zip(peers, peer_chip) for d in (0, 1)]
        for r, ((px, py), k, d) in enumerate(targets):
            for w in range(nw):
                sems = (send_sems.at[nw * r + w], recv_sems.at[nw * r + w], (px, py, (c + d) % 2))
                share = srcs[w].at[k if srcs[w].shape[0] > 1 else 0]
                pairs.append((_remote(share, lands[w].at[r], *sems),) * 2)
        return pairs
    gather = mode == "gather"
    for m, (px, py) in enumerate(peers):
        for w in range(nw):
            sems = (send_sems.at[nw * m + w], recv_sems.at[nw * m + w], (px, py, c))
            if gather:
                pairs.append((_remote(srcs[w], lands[w].at[chip], *sems),
                              _remote(srcs[w], lands[w].at[peer_chip[m]], *sems)))
            else:
                pairs.append((_remote(srcs[w].at[m], lands[w].at[m], *sems),) * 2)
    return pairs


def _exchange_start(mode, srcs, after, name, lands=None):
    nw = len(srcs)
    n_copies = _PEERS_OF[mode] * nw

    after = tuple(after)

    def body(*refs):
        send_sems, recv_sems = refs[2 * nw + len(after)], refs[2 * nw + len(after) + 1]
        for start, _ in _exchange_copies(mode, refs[:nw], refs[nw:2 * nw], send_sems, recv_sems):
            start.start()
        refs[-1][...] = jnp.zeros_like(refs[-1])

    if lands is None:
        shape = {"gather": lambda s: (N_CHIPS,) + s.shape, "scatter": lambda s: s.shape,
                 "direct": lambda s: (_PEERS_OF["direct"],) + s.shape[1:]}[mode]
        lands = [lax.empty(shape(s), s.dtype) for s in srcs]
    lands = [_in_hbm(l) for l in lands]
    return pl.pallas_call(
        body, name=name,
        out_shape=(pltpu.SemaphoreType.DMA((n_copies,)), pltpu.SemaphoreType.DMA((n_copies,)))
        + tuple(pltpu.HBM(s.shape, s.dtype) for s in srcs)
        + tuple(pltpu.HBM(l.shape, l.dtype) for l in lands)
        + (jax.ShapeDtypeStruct((8, LANES), F32),),
        in_specs=[_HBM] * (2 * nw) + [_ANY] * len(after),
        out_specs=(_SEM, _SEM) + (_HBM,) * (2 * nw) + (pl.BlockSpec(memory_space=pltpu.VMEM),),
        input_output_aliases={i: 2 + i for i in range(2 * nw)},
        compiler_params=pltpu.CompilerParams(has_side_effects=_DATAFLOW),
    )(*[_in_hbm(s) for s in srcs], *lands, *after)


def _exchange_wait(mode, started, after, name):
    nw = (len(started) - 3) // 2
    send_sems, recv_sems = started[0], started[1]
    thru = started[2:2 + 2 * nw]

    def body(*refs):
        for _, arrival in _exchange_copies(mode, refs[:nw], refs[nw:2 * nw], refs[2 * nw], refs[2 * nw + 1]):
            arrival.wait_send()
            arrival.wait_recv()

    outs = pl.pallas_call(
        body, name=name,
        out_shape=tuple(pltpu.HBM(t.shape, t.dtype) for t in thru),
        in_specs=[_HBM] * (2 * nw) + [_SEM, _SEM, _ANY], out_specs=(_HBM,) * (2 * nw),
        input_output_aliases={i: i for i in range(2 * nw)},
        compiler_params=pltpu.CompilerParams(has_side_effects=_DATAFLOW),
    )(*thru, send_sems, recv_sems, after)
    return outs[:nw], outs[nw:]


def _reduce_last(owns, landed, g_small, direct_srcs, direct_landed, spread_row0):
    ns, nd = len(owns), len(direct_srcs)
    halves = [o.shape[0] for o in owns]
    row_block = 16
    spread_rows = direct_srcs[-1].shape[1]
    rest0, rest1 = spread_row0, SMALL_ROWS - spread_row0 - spread_rows
    hs = (rest0 + rest1) // 2
    jobs = [(w, p * (halves[w] // 2), halves[w] // 2) for w in range(ns) for p in range(2)]
    n_swaps = len(jobs)
    jobs += [(ns + d, 0, direct_srcs[d].shape[1]) for d in range(nd)]

    def body(*refs):
        own_refs, land_refs, gsm_ref = refs[:ns], refs[ns:2 * ns], refs[2 * ns]
        dsrc_refs, dland_refs = refs[2 * ns + 1:2 * ns + 1 + nd], refs[2 * ns + 1 + nd:2 * ns + 1 + 2 * nd]
        n_in = 2 * ns + 1 + 2 * nd
        out_refs, osm_ref = refs[n_in:n_in + ns + nd - 1], refs[n_in + ns + nd - 1]
        scr = refs[n_in + ns + nd:]
        own_scr, land_scr, dsrc_scr, dland_scr = (scr[:ns], scr[ns:2 * ns], scr[2 * ns:2 * ns + nd],
                                                  scr[2 * ns + nd:2 * ns + 2 * nd])
        res = scr[2 * ns + 2 * nd:3 * ns + 3 * nd - 1]
        o_rest, ra_sm, p_sm, put_sem, in_sem, s_sem, r_sem, sm_s, sm_r = scr[3 * ns + 3 * nd - 1:]
        x, y, c, chip, peers, peer_chip = _place()
        sib = (x, y, 1 - c)
        half = lambda cc: pl.ds(pl.multiple_of(cc * hs, 8), hs)
        sm_a = _remote(gsm_ref.at[half(1 - c), :], ra_sm, sm_s.at[0], sm_r.at[0], sib)
        sm_a.start()
        swaps = [sm_a]

        def reads(j):
            w, r0, n = jobs[j]
            rows = pl.ds(r0, n)
            if w < ns:
                pairs = [(own_refs[w].at[rows, :], own_scr[w].at[rows, :]),
                         (land_refs[w].at[:, rows, :], land_scr[w].at[:, rows, :])]
            else:
                share = dsrc_refs[w - ns].at[chip if w < ns + nd - 1 else 0]
                pairs = [(share, dsrc_scr[w - ns]), (dland_refs[w - ns], dland_scr[w - ns])]
            return [pltpu.make_async_copy(s, d, in_sem.at[2 * j + i]) for i, (s, d) in enumerate(pairs)]

        for j in range(len(jobs)):
            for cp in reads(j):
                cp.start()
        sm_a.wait_recv()
        p_sm[chip] = gsm_ref[half(c), :] + ra_sm[...]
        for m, (px, py) in enumerate(peers):
            swaps.append(_remote(p_sm.at[chip], p_sm.at[chip], sm_s.at[1 + m], sm_r.at[1 + m], (px, py, c)))
            swaps[-1].start()

        def mine_of(j):
            w, r0, n = jobs[j]
            return out_refs[w].at[pl.ds(pl.multiple_of(c * halves[w] + r0, 8), n), :]

        def sum_of(j):
            w, r0, n = jobs[j]
            return res[w].at[pl.ds(r0, n), :]

        puts = []
        for j, (w, r0, n) in enumerate(jobs):
            for cp in reads(j):
                cp.wait()

            def total(i, carry, w=w, r0=r0):
                rr = pl.multiple_of(r0 + i * row_block, row_block)
                blk = pl.ds(rr, row_block)
                if w < ns:
                    acc = own_scr[w][blk, :]
                    for m in range(_PEERS_OF["scatter"]):
                        acc = acc + land_scr[w][m, blk, :].astype(F32)
                    res[w][blk, :] = acc
                else:
                    theirs = lambda r: dland_scr[w - ns][r, blk, :].astype(F32)
                    acc = ((dsrc_scr[w - ns][blk, :].astype(F32) + theirs(0)) + (theirs(1) + theirs(2))) + (
                        (theirs(3) + theirs(4)) + (theirs(5) + theirs(6)))
                    if w < ns + nd - 1:
                        res[w][blk, :] = acc
                    else:
                        osm_ref[pl.ds(pl.multiple_of(spread_row0 + rr, 8), row_block), :] = acc
                return carry
            lax.fori_loop(0, n // row_block, total, 0)
            if w < ns:
                puts.append(pltpu.make_async_copy(sum_of(j), mine_of(j), put_sem.at[j]))
                swaps.append(_remote(sum_of(j), mine_of(j), s_sem.at[j], r_sem.at[j], sib))
                swaps[-1].start()
            elif w < ns + nd - 1:
                puts.append(pltpu.make_async_copy(res[w], out_refs[w], put_sem.at[j]))
            if w < ns + nd - 1:
                puts[-1].start()
        for m, (px, py) in enumerate(peers):
            _remote(p_sm.at[chip], p_sm.at[peer_chip[m]], sm_s.at[1 + m], sm_r.at[1 + m], (px, py, c)).wait_recv()
        o_rest[half(c), :] = (p_sm[0] + p_sm[1]) + (p_sm[2] + p_sm[3])
        swaps.append(_remote(o_rest.at[half(c), :], o_rest.at[half(c), :], sm_s.at[4], sm_r.at[4], sib))
        swaps[-1].start()
        for j, (w, r0, n) in enumerate(jobs[:n_swaps]):
            theirs = out_refs[w].at[pl.ds(pl.multiple_of((1 - c) * halves[w] + r0, 8), n), :]
            _remote(sum_of(j), theirs, s_sem.at[j], r_sem.at[j], sib).wait_recv()
        _remote(o_rest.at[half(1 - c), :], o_rest.at[half(1 - c), :], sm_s.at[4], sm_r.at[4], sib).wait_recv()
        osm_ref[0:rest0, :] = o_rest[0:rest0, :]
        osm_ref[SMALL_ROWS - rest1:SMALL_ROWS, :] = o_rest[rest0:rest0 + rest1, :]
        for cp in swaps:
            cp.wait_send()
        for cp in puts:
            cp.wait()

    vmem = pl.BlockSpec(memory_space=pltpu.VMEM)
    far = [pl.BlockSpec(memory_space=pl.ANY)]
    return pl.pallas_call(
        body, out_shape=[jax.ShapeDtypeStruct((2 * o.shape[0], o.shape[1]), F32) for o in owns]
        + [jax.ShapeDtypeStruct(s.shape[1:], F32) for s in direct_srcs[:-1]]
        + [jax.ShapeDtypeStruct((SMALL_ROWS, LANES), F32)],
        in_specs=far * (2 * ns) + [vmem] + far * (2 * nd), out_specs=far * (ns + nd - 1) + [vmem],
        scratch_shapes=[pltpu.VMEM(o.shape, o.dtype) for o in owns] + [pltpu.VMEM(l.shape, l.dtype) for l in landed]
        + [pltpu.VMEM(s.shape[1:], s.dtype) for s in direct_srcs]
        + [pltpu.VMEM(l.shape, l.dtype) for l in direct_landed]
        + [pltpu.VMEM(o.shape, F32) for o in owns] + [pltpu.VMEM(s.shape[1:], F32) for s in direct_srcs[:-1]]
        + [pltpu.VMEM((2 * hs, LANES), F32), pltpu.VMEM((hs, LANES), F32), pltpu.VMEM((N_CHIPS, hs, LANES), F32),
           pltpu.SemaphoreType.DMA((len(jobs),)), pltpu.SemaphoreType.DMA((2 * len(jobs),)),
           pltpu.SemaphoreType.DMA((n_swaps,)), pltpu.SemaphoreType.DMA((n_swaps,)),
           pltpu.SemaphoreType.DMA((5,)), pltpu.SemaphoreType.DMA((5,))],
        compiler_params=pltpu.CompilerParams(vmem_limit_bytes=VMEM_LIMIT),
        name="reduce_last")(*owns, *landed, g_small, *direct_srcs, *direct_landed)


_SMALL_PARTS = (("g_norm", 8, 8), ("w_s", 512, 512), ("b_s", 4, 8), ("g_v", 2, 8), ("g_mem", 8, 8),
                ("g_final", 8, 8), ("loss", 8, 8))
_LOSS_ROW = SMALL_ROWS - 8
_W_S_ROW = 8
assert sum(p for _, _, p in _SMALL_PARTS) == SMALL_ROWS and _SMALL_PARTS[1][0] == "w_s"


def _pack_small(parts, loss_block):
    rows = []
    parts = dict(parts, loss=loss_block)
    for name, used, padded in _SMALL_PARTS:
        if name == "w_s":
            continue
        p = parts[name].reshape(used, LANES)
        if padded > used:
            p = jnp.pad(p, ((0, padded - used), (0, 0)))
        rows.append(p)
    return jnp.concatenate(rows, axis=0)


def _local_step(x, mem, target, g_norm, w_in, w_s, b_s, g_v, g_mem, late_weights, g_final,
                fwd_token=None, on_late=None, on_dw=None):
    B, S, _ = x.shape
    x2d = x.reshape(B * S, D_MODEL)
    t2d = target.reshape(B * S, D_MODEL)
    mem2d = mem.reshape(B * N_MEM, D_MODEL)

    proj = _inproj_fwd(x2d, g_norm, w_in, after=() if fwd_token is None else (fwd_token,))
    w_kv, w_out = late_weights(proj)
    kv = _kv_fwd(mem2d, g_mem, w_kv)
    a, lse = _attn_fwd(proj, B, S)
    w_sT = jnp.swapaxes(w_s, 1, 2)
    b_tab = jnp.repeat(b_s.T, HEAD_DIM, axis=1)
    (dx2, da, drest, loss, d_wout, d_ws, d_bs, d_gv, d_gf, dkv) = _mid(
        x2d, t2d, a, proj, kv, w_s, w_sT, b_tab, g_v, w_out, g_final, B, S)
    d_wkv, d_gmem = _kv_bwd(mem2d, g_mem, w_kv, dkv)
    dq, dk, dv = _attn_bwd(proj, a, lse, da, B, S, after=() if on_late is None else (on_late(d_wkv, d_wout, d_ws),))
    if on_dw is None:
        d_win = _inproj_bwd_dw(dq, dk, dv, drest, x2d, g_norm)
        after = ()
    else:
        d_win = None
        after = (on_dw(*_inproj_bwd_dw(dq, dk, dv, drest, x2d, g_norm, reduce=True)),)
    grad_x, d_gnorm = _inproj_bwd_dx(dq, dk, dv, drest, x2d, dx2, g_norm, w_in, after=after)
    d_bs = d_bs[:, :N_SGU_GROUPS].T
    return (loss, grad_x.reshape(B, S, D_MODEL),
            dict(g_norm=d_gnorm, w_in=d_win, w_s=d_ws, b_s=d_bs, g_v=d_gv, g_mem=d_gmem, w_kv=d_wkv,
                 w_out=d_wout, g_final=d_gf))


def kernel(x, mem, g_norm, w_in, w_sgu_spatial, b_sgu_spatial, g_sgu_v, g_mem, w_mem_kv, w_out, g_final, loss_target, m_g_norm, m_w_in, m_w_sgu_spatial, m_b_sgu_spatial, m_g_sgu_v, m_g_mem, m_w_mem_kv, m_w_out, m_g_final, v_g_norm, v_w_in, v_w_sgu_spatial, v_b_sgu_spatial, v_g_sgu_v, v_g_mem, v_w_mem_kv, v_w_out, v_g_final):
    t = lambda w: jnp.swapaxes(w[0], 0, 1)
    (win_all,), late_shards, late_lands = _ag_weights([t(w_in)], [w_mem_kv[0], w_out[0]])
    w_in_full = win_all.reshape(-1, win_all.shape[-1])
    late = _exchange_start("gather", list(late_shards), (win_all,), "gather_late_start", lands=late_lands)

    def late_weights(proj):
        return [z.reshape(-1, z.shape[-1]) for z in _exchange_wait("gather", late, proj, "gather_late_wait")[1]]

    scatter = {}

    def on_late(d_wkv, d_wout, d_ws):
        d_ws = d_ws.reshape(1, -1, LANES)
        scatter["late"] = _exchange_start("direct", [d_wkv, d_wout, d_ws], (), "scatter_late_start")
        return scatter["late"][-1]

    def on_dw(sends, own):
        scatter["own"] = own
        scatter["started"] = _exchange_start("scatter", [sends], (own,), "scatter_start")
        return scatter["started"][-1]

    loss, grad_x, g = _local_step(
        x, mem, loss_target, g_norm, w_in_full, w_sgu_spatial[0], b_sgu_spatial[0], g_sgu_v, g_mem,
        late_weights, g_final.reshape(1, D_MODEL), fwd_token=late[-1], on_late=on_late, on_dw=on_dw)

    small_names = ("g_norm", "w_s", "b_s", "g_v", "g_mem", "g_final")
    g_small = _pack_small({n: g[n] for n in small_names if n != "w_s"}, loss)
    late_srcs, late_landed = _exchange_wait("direct", scatter["late"], g_small, "scatter_late_wait")
    _, landed = _exchange_wait("scatter", scatter["started"], late_landed[0], "scatter_wait")
    gr_in, gr_kv, gr_out, gr_small = _reduce_last([scatter["own"]], landed, g_small, late_srcs, late_landed, _W_S_ROW)

    small_w = (g_norm, w_sgu_spatial, b_sgu_spatial, g_sgu_v, g_mem, g_final)
    small_m = (m_g_norm, m_w_sgu_spatial, m_b_sgu_spatial, m_g_sgu_v, m_g_mem, m_g_final)
    small_v = (v_g_norm, v_w_sgu_spatial, v_b_sgu_spatial, v_g_sgu_v, v_g_mem, v_g_final)
    rows = lambda ws: [w.reshape(-1, LANES) for w in ws]
    small_new, (of_in, (gr_kv, d_kv, nm_kv, nv_kv), (gr_out, d_out, nm_out, nv_out)), loss = _adamw_all(
        gr_small, rows(small_w), rows(small_m), rows(small_v),
        [(t(w_in), gr_in, t(m_w_in), t(v_w_in)), (w_mem_kv[0], gr_kv, m_w_mem_kv[0], v_w_mem_kv[0]),
         (w_out[0], gr_out, m_w_out[0], v_w_out[0])])
    loss = loss.reshape(())
    small = [[z.reshape(w.shape) for z in four] for w, four in zip(small_w, small_new)]
    gr_in, d_in, nm_in, nv_in = [jnp.swapaxes(z, 0, 1) for z in of_in]

    def leaves(kind, big_in, big_kv, big_out):
        s_norm, s_ws, s_bs, s_gv, s_gmem, s_gf = [four[kind] for four in small]
        return [s_norm, big_in[None], s_ws, s_bs, s_gv, s_gmem, big_kv[None], big_out[None], s_gf]

    return (loss, grad_x, *leaves(0, gr_in, gr_kv, gr_out), *leaves(1, d_in, d_kv, d_out),
            *leaves(2, nm_in, nm_kv, nm_out), *leaves(3, nv_in, nv_kv, nv_out))
```

```python
import functools

import jax
import jax.numpy as jnp
from jax import lax
from jax.experimental import pallas as pl
from jax.experimental.pallas import tpu as pltpu

F32 = jnp.float32
BF16 = jnp.bfloat16
MESH = pl.DeviceIdType.MESH

D_MODEL = 1024
ATTN_WIDTH = 512
SGU_WIDTH = 256
MEM_WIDTH = 256
N_MEM = 256
IN_COLS = 3328
QKV_COLS = 3 * ATTN_WIDTH
REST_COLS = IN_COLS - QKV_COLS
SGU_CHUNK = 128
N_SGU_GROUPS = 4
EPS = 1e-6
NEG_INF = -1e30
DILATIONS = (1, 4, 16)
RADIUS = 64
Q_BLOCK = 128
LANES = 128
HEAD_DIM = 64

ADAM_LR = 0.001
ADAM_B1 = 0.9
ADAM_B2 = 0.999
ADAM_EPS = 1e-08
ADAM_WD = 0.01
ADAM_STEP = 10

N_CHIPS = 4
VMEM_LIMIT = 56 * 1024 * 1024
SMALL_ROWS = 560


def _params(sem=None, vmem=VMEM_LIMIT):
    return pltpu.CompilerParams(dimension_semantics=sem, vmem_limit_bytes=vmem)


def _nn(a, b):
    return jnp.dot(a, b, preferred_element_type=F32)


def _nt(a, b):
    return lax.dot_general(a, b, (((1,), (1,)), ((), ())), preferred_element_type=F32)


def _tn(a, b):
    return lax.dot_general(a, b, (((0,), (0,)), ((), ())), preferred_element_type=F32)


def _rms(x):
    r = lax.rsqrt(jnp.mean(x * x, axis=-1, keepdims=True) + EPS)
    return r, x * r


def _head_masks():
    lane = lax.broadcasted_iota(jnp.int32, (1, LANES), 1)
    lo = lane < HEAD_DIM
    return lo, (lo.astype(F32), (~lo).astype(F32))


def _silu_parts(z):
    s = jax.nn.sigmoid(z)
    return z * s, s * (1.0 + z * (1.0 - s))


def _gelu_parts(x):
    c = 0.7978845608028654
    x2 = x * x
    s = jax.nn.sigmoid((2.0 * c) * (x + 0.044715 * (x * x2)))
    return x * s, s * (1.0 + x * (1.0 - s) * ((2.0 * c) * (1.0 + 3.0 * 0.044715 * x2)))


def _after(tokens):
    return [pl.BlockSpec(memory_space=pl.ANY)] * len(tokens)


def _inproj_fwd(x2d, g_norm, w_in_t, after=()):
    T = x2d.shape[0]
    tm = 512

    def body(x_ref, g_ref, w_ref, *rest):
        o_ref = rest[-1]
        _, xh = _rms(x_ref[...])
        h = (xh * g_ref[...]).astype(BF16)
        o_ref[...] = _nt(h, w_ref[...])

    return pl.pallas_call(
        body, grid=(T // tm,),
        in_specs=[pl.BlockSpec((tm, D_MODEL), lambda i: (i, 0)),
                  pl.BlockSpec((1, D_MODEL), lambda i: (0, 0)),
                  pl.BlockSpec((IN_COLS, D_MODEL), lambda i: (0, 0))] + _after(after),
        out_specs=pl.BlockSpec((tm, IN_COLS), lambda i: (i, 0)),
        out_shape=jax.ShapeDtypeStruct((T, IN_COLS), F32),
        compiler_params=_params(("arbitrary",)), name="inproj_fwd")(x2d, g_norm, w_in_t, *after)


def _kv_fwd(mem2d, g_mem, w_kv):
    Tm = mem2d.shape[0]

    def body(m_ref, g_ref, w_ref, o_ref):
        _, mh = _rms(m_ref[...])
        o_ref[...] = _nn((mh * g_ref[...]).astype(BF16), w_ref[...])

    return pl.pallas_call(
        body, out_shape=jax.ShapeDtypeStruct((Tm, 2 * MEM_WIDTH), F32),
        compiler_params=_params(), name="kv_fwd")(mem2d, g_mem, w_kv)


def _kv_bwd(mem2d, g_mem, w_kv, dkv):
    Tm = mem2d.shape[0]

    def body(m_ref, g_ref, w_ref, dkv_ref, dw_ref, dg_ref):
        _, mh = _rms(m_ref[...])
        memn = (mh * g_ref[...]).astype(BF16)
        dkvb = dkv_ref[...].astype(BF16)
        dw = _tn(memn, dkvb).astype(BF16)
        for k in range(N_CHIPS):
            dw_ref[k] = dw[k * (D_MODEL // N_CHIPS):(k + 1) * (D_MODEL // N_CHIPS), :]
        dmemn = _nt(dkvb, w_ref[...])
        dg_ref[...] = jnp.sum(dmemn * mh, axis=0, keepdims=True)

    return pl.pallas_call(
        body, out_shape=(jax.ShapeDtypeStruct((N_CHIPS, D_MODEL // N_CHIPS, 2 * MEM_WIDTH), BF16),
                         jax.ShapeDtypeStruct((1, D_MODEL), F32)),
        compiler_params=_params(), name="kv_bwd")(mem2d, g_mem, w_kv, dkv)


def _attn_geometry(S):
    geom = []
    for d in DILATIONS:
        L = S // d
        assert L % Q_BLOCK == 0
        geom.append((d, L, min(2 * Q_BLOCK, L), L // Q_BLOCK))
    return geom


def _init_bias(bias_scr, geom, hp):
    row = lax.broadcasted_iota(jnp.int32, (Q_BLOCK, 2 * Q_BLOCK), 0)
    col = lax.broadcasted_iota(jnp.int32, (Q_BLOCK, 2 * Q_BLOCK), 1)
    for j in (0, 1):
        bits = (126 - (2 * hp + j)) * (1 << 23)
        slope = lax.bitcast_convert_type(jnp.full((1, 1), bits, jnp.int32), F32)
        for di, (d, _, _, _) in enumerate(geom):
            for cls, off in enumerate((0, -RADIUS, -2 * RADIUS)):
                dist = jnp.abs(col - row + off)
                bias_scr[di * 6 + cls * 2 + j] = jnp.where(
                    dist <= RADIUS, -(slope * float(d)) * dist.astype(F32), NEG_INF)


SPLIT = 4
COPY_ROWS = 256


def _by4_rows(S, step):
    per_class = S // SPLIT // COPY_ROWS
    r, j = step // per_class, step % per_class
    return (pl.ds(r + SPLIT * j * COPY_ROWS, COPY_ROWS, stride=SPLIT),
            pl.ds(r * (S // SPLIT) + j * COPY_ROWS, COPY_ROWS))


def _to_by4(src, dst, S):
    for i in range(S // COPY_ROWS):
        natural, by4 = _by4_rows(S, i)
        dst[by4, :] = src[natural, :]


def _block_slices(d, L, KW, nqb, r, qb, S):
    qs = qb * Q_BLOCK
    ks = jnp.clip(qs - RADIUS, 0, L - KW)
    cls = jnp.where(qb == 0, 0, jnp.where(qb == nqb - 1, 2, 1))
    if d == 1:
        qsl = pl.ds(pl.multiple_of(qs, Q_BLOCK), Q_BLOCK)
        ksl = pl.ds(pl.multiple_of(ks, RADIUS), KW)
    elif d == SPLIT:
        qsl = pl.ds(pl.multiple_of(r * L + qs, Q_BLOCK), Q_BLOCK)
        ksl = pl.ds(pl.multiple_of(r * L + ks, RADIUS), KW)
    else:
        sub = d // SPLIT
        base = (r % SPLIT) * (S // SPLIT) + r // SPLIT
        qsl = pl.ds(base + qs * sub, Q_BLOCK, stride=sub)
        ksl = pl.ds(base + ks * sub, KW, stride=sub)
    return qsl, ksl, cls


def _for_groups(geom, S, group, fn):
    for di, (d, L, KW, nqb) in enumerate(geom):
        n = group[di]
        assert (d * nqb) % n == 0

        def step(it, carry, di=di, d=d, L=L, KW=KW, nqb=nqb, n=n):
            slices = []
            for g in range(n):
                i = it * n + g
                slices.append(_block_slices(d, L, KW, nqb, i // nqb, i % nqb, S))
            fn(di, KW, slices)
            return carry
        lax.fori_loop(0, d * nqb // n, step, 0)


def _attn_fwd(proj, B, S):
    T = B * S
    geom = _attn_geometry(S)
    n_pairs = ATTN_WIDTH // LANES

    def body(q_ref, k_ref, v_ref, a_ref, lse_ref, bias_scr, q4, k4, v4, *per_dilation):
        o_scr, m_scr, l_scr = per_dilation[0:3], per_dilation[3:6], per_dilation[6:9]
        lo, hm = _head_masks()
        pair = pl.program_id(0)

        @pl.when(pl.program_id(1) == 0)
        def _():
            _init_bias(bias_scr, geom, pair)
        for src, dst in ((q_ref, q4), (k_ref, k4), (v_ref, v4)):
            _to_by4(src, dst, S)

        def group(di, KW, all_slices):
            run = 8
            for first in range(0, len(all_slices), run):
                some(di, KW, all_slices[first:first + run])

        def some(di, KW, slices):
            chains = [(g, j) for g in range(len(slices)) for j in (0, 1)]
            q_src, k_src, v_src = (q_ref, k_ref, v_ref) if di == 0 else (q4, k4, v4)
            q = [q_src[qsl, :] for qsl, _, _ in slices]
            kw = [k_src[ksl, :].astype(BF16) for _, ksl, _ in slices]
            vw = [v_src[ksl, :].astype(BF16) for _, ksl, _ in slices]
            s = {(g, j): _nt((q[g] * (hm[j] * 0.125)).astype(BF16), kw[g])
                 + bias_scr[di * 6 + slices[g][2] * 2 + j, :, pl.ds(0, KW)] for g, j in chains}
            m = {c: jnp.max(s[c], axis=1, keepdims=True) for c in chains}
            p = {c: jnp.exp(s[c] - m[c]) for c in chains}
            l = {c: jnp.sum(p[c], axis=1, keepdims=True) for c in chains}
            o = {(g, j): _nn(p[(g, j)].astype(BF16), vw[g]) for g, j in chains}
            for g, (qsl, _, _) in enumerate(slices):
                o_scr[di][qsl, :] = jnp.where(lo, o[(g, 0)], o[(g, 1)])
                m_scr[di][qsl, :] = jnp.where(lo, m[(g, 0)], m[(g, 1)])
                l_scr[di][qsl, :] = jnp.where(lo, l[(g, 0)], l[(g, 1)])

        _for_groups(geom, S, (16, 16, 16), group)

        for i in range(S // COPY_ROWS):
            natural, by4 = _by4_rows(S, i)
            rows = [natural, by4, by4]
            ms = [m_scr[di][rows[di], :] for di in range(3)]
            mx = jnp.maximum(jnp.maximum(ms[0], ms[1]), ms[2])
            num = 0.0
            den = 0.0
            for di in range(3):
                w = jnp.exp(ms[di] - mx)
                num = num + w * o_scr[di][rows[di], :]
                den = den + w * l_scr[di][rows[di], :]
            a_ref[natural, :] = num / den
            lse_ref[natural, :] = mx + jnp.log(den)

    blk = lambda off: pl.BlockSpec((S, LANES), lambda h, b, off=off: (b, off + h))
    out_blk = pl.BlockSpec((S, LANES), lambda h, b: (b, h))
    return pl.pallas_call(
        body, grid=(n_pairs, B),
        in_specs=[blk(0), blk(n_pairs), blk(2 * n_pairs)],
        out_specs=[out_blk, out_blk],
        out_shape=[jax.ShapeDtypeStruct((T, ATTN_WIDTH), F32)] * 2,
        scratch_shapes=[pltpu.VMEM((18, Q_BLOCK, 2 * Q_BLOCK), F32)] + [pltpu.VMEM((S, LANES), F32)] * 12,
        compiler_params=_params(("arbitrary", "arbitrary")), name="attn_fwd")(proj, proj, proj)


def _attn_bwd(proj, a, lse, da, B, S, after=()):
    T = B * S
    geom = _attn_geometry(S)
    n_pairs = ATTN_WIDTH // LANES

    def body(q_ref, k_ref, v_ref, a_ref, lse_ref, do_ref, *rest):
        dq_ref, dk_ref, dv_ref, bias_scr = rest[len(after):len(after) + 4]
        scr = rest[len(after) + 4:]
        acc = (scr[0:3], scr[3:6])
        natural_in = (q_ref, k_ref, v_ref, a_ref, lse_ref, do_ref)
        by4_in = scr[6:12]
        _, hm = _head_masks()
        pair = pl.program_id(0)

        @pl.when(pl.program_id(1) == 0)
        def _():
            _init_bias(bias_scr, geom, pair)
        for ref in scr[0:6]:
            ref[...] = jnp.zeros_like(ref)
        for src, dst in zip(natural_in, by4_in):
            _to_by4(src, dst, S)

        def group(di, KW, all_slices):
            run = (4, 4, 8)[di]
            for first in range(0, len(all_slices), run):
                some(di, KW, all_slices[first:first + run])

        def some(di, KW, slices):
            n = len(slices)
            chains = [(g, j) for g in range(n) for j in (0, 1)]
            q_src, k_src, v_src, a_src, lse_src, do_src = natural_in if di == 0 else by4_in
            dq_scr, dk_scr, dv_scr = acc[0 if di == 0 else 1]
            q = [q_src[qsl, :] for qsl, _, _ in slices]
            do = [do_src[qsl, :] for qsl, _, _ in slices]
            doa = [do[g] * a_src[slices[g][0], :] for g in range(n)]
            lse_q = [lse_src[qsl, :] for qsl, _, _ in slices]
            kw = [k_src[ksl, :].astype(BF16) for _, ksl, _ in slices]
            vw = [v_src[ksl, :].astype(BF16) for _, ksl, _ in slices]
            qj = {(g, j): (q[g] * (hm[j] * 0.125)).astype(BF16) for g, j in chains}
            doj = {(g, j): (do[g] * hm[j]).astype(BF16) for g, j in chains}
            s = {(g, j): _nt(qj[(g, j)], kw[g])
                 + bias_scr[di * 6 + slices[g][2] * 2 + j, :, pl.ds(0, KW)] for g, j in chains}
            dp = {(g, j): _nt(doj[(g, j)], vw[g]) for g, j in chains}
            dsum = {(g, j): jnp.sum(doa[g] * hm[j], axis=1, keepdims=True) for g, j in chains}
            p = {(g, j): jnp.exp(s[(g, j)] - lse_q[g][:, HEAD_DIM * j:HEAD_DIM * j + 1]) for g, j in chains}
            ds = {c: (p[c] * (dp[c] - dsum[c])).astype(BF16) for c in chains}
            pb = {c: p[c].astype(BF16) for c in chains}
            dq = [_nn(ds[(g, 0)], kw[g]) * (hm[0] * 0.125) + _nn(ds[(g, 1)], kw[g]) * (hm[1] * 0.125)
                  for g in range(n)]
            both = lambda t, g: jnp.concatenate([t[(g, 0)], t[(g, 1)]], axis=0)
            dkw = [_tn(both(ds, g), both(qj, g)) for g in range(n)]
            dvw = [_tn(both(pb, g), both(doj, g)) for g in range(n)]
            for g, (qsl, ksl, _) in enumerate(slices):
                dq_scr[qsl, :] = dq_scr[qsl, :] + dq[g]
                dk_scr[ksl, :] = dk_scr[ksl, :] + dkw[g]
                dv_scr[ksl, :] = dv_scr[ksl, :] + dvw[g]

        _for_groups(geom, S, (16, 16, 16), group)

        for i in range(S // COPY_ROWS):
            natural, by4 = _by4_rows(S, i)
            for nat, split in zip(*acc):
                nat[natural, :] = nat[natural, :] + split[by4, :]
        for out, nat in zip((dq_ref, dk_ref, dv_ref), acc[0]):
            out[...] = nat[...].astype(BF16)

    blk = lambda off: pl.BlockSpec((S, LANES), lambda h, b, off=off: (b, off + h))
    return pl.pallas_call(
        body, grid=(n_pairs, B),
        in_specs=[blk(0), blk(n_pairs), blk(2 * n_pairs), blk(0), blk(0), blk(0)] + _after(after),
        out_specs=[blk(0), blk(0), blk(0)],
        out_shape=[jax.ShapeDtypeStruct((T, ATTN_WIDTH), BF16)] * 3,
        scratch_shapes=[pltpu.VMEM((18, Q_BLOCK, 2 * Q_BLOCK), F32)] + [pltpu.VMEM((S, LANES), F32)] * 12,
        compiler_params=_params(("arbitrary", "arbitrary")), name="attn_bwd")(proj, proj, proj, a, lse, da, *after)


def _mid(x2d, t2d, a, proj, kv, w_s, w_sT, b_tab, g_v, w_out, g_final, B, S):
    T = B * S
    tm = 512
    nt = S // tm
    halves = 2
    hrows = tm // halves

    def body(x_ref, t_ref, a_ref, za_ref, ub_ref, vb_ref, zb_ref, qm_ref, zm_ref, kv_ref,
              ws_ref, wsT_ref, btab_ref, gv_ref, wout_ref, gf_ref,
              dx2_ref, da_ref, drest_ref, loss_ref, dwout_bf_ref, dws_ref, dbs_ref, dgv_ref, dgf_ref, dkv_ref,
              dbtab_scr, dwout_ref):
        b = pl.program_id(0)
        t = pl.program_id(1)
        first = jnp.logical_and(b == 0, t == 0)
        last = jnp.logical_and(b == B - 1, t == nt - 1)
        _, hm = _head_masks()
        lane_g = lax.broadcasted_iota(jnp.int32, (1, SGU_WIDTH), 1) // HEAD_DIM
        gm = [(lane_g == g).astype(F32) for g in range(N_SGU_GROUPS)]
        H = range(halves)
        rows = [pl.ds(h * hrows, hrows) for h in H]
        ld = lambda ref: [ref[r, :] for r in rows]
        cat = lambda parts, axis: jnp.concatenate(parts, axis=axis)
        chunks = [slice(ci * SGU_CHUNK, (ci + 1) * SGU_CHUNK) for ci in range(hrows // SGU_CHUNK)]
        pairs = [slice(pr * LANES, (pr + 1) * LANES) for pr in range(2)]
        heads = [(pr, j) for pr in range(2) for j in (0, 1)]

        @pl.when(first)
        def _():
            loss_ref[...] = jnp.zeros_like(loss_ref)
            dwout_ref[...] = jnp.zeros_like(dwout_ref)
            dws_ref[...] = jnp.zeros_like(dws_ref)
            dbs_ref[...] = jnp.zeros_like(dbs_ref)
            dgv_ref[...] = jnp.zeros_like(dgv_ref)
            dgf_ref[...] = jnp.zeros_like(dgf_ref)
            dbtab_scr[...] = jnp.zeros_like(dbtab_scr)

        @pl.when(t == 0)
        def _():
            dkv_ref[...] = jnp.zeros_like(dkv_ref)

        a_val = ld(a_ref)
        sil_a = [_silu_parts(z) for z in ld(za_ref)]
        gated_a = [s[0] * a for s, a in zip(sil_a, a_val)]
        u = [_gelu_parts(z) for z in ld(ub_ref)]
        vv = [_gelu_parts(z) for z in ld(vb_ref)]
        vnorm = [_rms(v[0]) for v in vv]
        gv = gv_ref[...]
        vn = [(n[1] * gv).astype(BF16) for n in vnorm]
        w_cat = cat([ws_ref[g].astype(BF16) for g in range(N_SGU_GROUPS)], 1)
        wT_cat = cat([wsT_ref[g].astype(BF16) for g in range(N_SGU_GROUPS)], 1)
        gmb = [m.astype(BF16) for m in gm]
        by_group = lambda chunk: cat([chunk * gmb[g] for g in range(N_SGU_GROUPS)], 0)
        btab = btab_ref[...]
        mixed = [cat([btab + _nn(w_cat, by_group(vn[h][c, :])) for c in chunks], 0) for h in H]
        sg = [u[h][0] * mixed[h] for h in H]
        sil_b = [_silu_parts(z) for z in ld(zb_ref)]
        gated_b = [sil_b[h][0] * sg[h] for h in H]

        kvv = kv_ref[...].astype(BF16)
        kp = [kvv[:, p] for p in pairs]
        vp = [kvv[:, MEM_WIDTH + pr * LANES:MEM_WIDTH + (pr + 1) * LANES] for pr in range(2)]
        qm = ld(qm_ref)
        qj = {(h, pr, j): (qm[h][:, pairs[pr]] * (hm[j] * 0.125)).astype(BF16) for h in H for pr, j in heads}
        sc = {k: _nt(qj[k], kp[k[1]]) for k in qj}
        ex = {k: jnp.exp(sc[k] - jnp.max(sc[k], axis=1, keepdims=True)) for k in qj}
        prob = {k: ex[k] * (1.0 / jnp.sum(ex[k], axis=1, keepdims=True)) for k in qj}
        probb = {k: prob[k].astype(BF16) for k in qj}
        mo = [cat([sum(_nn(probb[(h, pr, j)], vp[pr]) * hm[j] for j in (0, 1)) for pr in range(2)], 1) for h in H]
        sil_m = [_silu_parts(z) for z in ld(zm_ref)]
        gated_m = [sil_m[h][0] * mo[h] for h in H]

        gated = [cat([gated_a[h], gated_b[h], gated_m[h]], 1).astype(BF16) for h in H]
        wout = wout_ref[...]
        x_in = ld(x_ref)
        x2 = [x_in[h] + _nn(gated[h], wout) for h in H]
        fin = [_rms(z) for z in x2]
        gf = gf_ref[...]
        tgt = ld(t_ref)
        err = [fin[h][1] * gf - tgt[h] for h in H]
        loss_ref[...] += sum(jnp.sum(e * e) for e in err) * (0.5 / D_MODEL)

        dy = [e * (1.0 / D_MODEL) for e in err]
        dgf_ref[...] += sum(jnp.sum(dy[h] * fin[h][1], axis=0, keepdims=True) for h in H)
        gdy = [d * gf for d in dy]
        dx2 = [fin[h][0] * (gdy[h] - fin[h][1] * jnp.mean(gdy[h] * fin[h][1], axis=1, keepdims=True)) for h in H]
        for h in H:
            dx2_ref[rows[h], :] = dx2[h]
        dx2b = [d.astype(BF16) for d in dx2]
        dgated = [_nt(d, wout) for d in dx2b]
        dwout_ref[...] += _tn(cat(gated, 0), cat(dx2b, 0))
        dga = [d[:, 0:ATTN_WIDTH] for d in dgated]
        dgb = [d[:, ATTN_WIDTH:ATTN_WIDTH + SGU_WIDTH] for d in dgated]
        dgm = [d[:, ATTN_WIDTH + SGU_WIDTH:] for d in dgated]

        for h in H:
            da_ref[rows[h], :] = dga[h] * sil_a[h][0]
        dza = [dga[h] * a_val[h] * sil_a[h][1] for h in H]

        dsg = [dgb[h] * sil_b[h][0] for h in H]
        dzb = [dgb[h] * sg[h] * sil_b[h][1] for h in H]
        dub = [dsg[h] * mixed[h] * u[h][1] for h in H]
        dmixed = [dsg[h] * u[h][0] for h in H]
        dmixed_b = [d.astype(BF16) for d in dmixed]
        dvn = [cat([_nn(wT_cat, by_group(dmixed_b[h][c, :])) for c in chunks], 0) for h in H]
        for g in range(N_SGU_GROUPS):
            dws_ref[g] += sum(_nt((dmixed[h][c, :] * gm[g]).astype(BF16), vn[h][c, :]) for h in H for c in chunks)
        dbtab_scr[...] += sum(dmixed[h][c, :] for h in H for c in chunks)
        dgv_ref[...] += sum(jnp.sum(dvn[h] * vnorm[h][1], axis=0, keepdims=True) for h in H)
        tv = [d * gv for d in dvn]
        dvv = [vnorm[h][0] * (tv[h] - vnorm[h][1] * jnp.mean(tv[h] * vnorm[h][1], axis=1, keepdims=True)) for h in H]
        dvb = [dvv[h] * vv[h][1] for h in H]

        dmo = [dgm[h] * sil_m[h][0] for h in H]
        dzm = [dgm[h] * mo[h] * sil_m[h][1] for h in H]
        dmoj = {(h, pr, j): (dmo[h][:, pairs[pr]] * hm[j]).astype(BF16) for h in H for pr, j in heads}
        dp = {k: _nt(dmoj[k], vp[k[1]]) for k in qj}
        ds = {k: (prob[k] * (dp[k] - jnp.sum(dp[k] * prob[k], axis=1, keepdims=True))).astype(BF16) for k in qj}
        dqm = [cat([sum(_nn(ds[(h, pr, j)], kp[pr]) * (hm[j] * 0.125) for j in (0, 1)) for pr in range(2)], 1)
               for h in H]
        every = lambda tbl, pr: cat([tbl[(h, pr, j)] for h in H for j in (0, 1)], 0)
        dk = [_tn(every(ds, pr), every(qj, pr)) for pr in range(2)]
        dv = [_tn(every(probb, pr), every(dmoj, pr)) for pr in range(2)]
        dkv_ref[...] += cat(dk + dv, 1)

        for h in H:
            drest_ref[rows[h], :] = cat([dza[h], dub[h], dvb[h], dzb[h], dqm[h], dzm[h]], 1).astype(BF16)

        @pl.when(last)
        def _():
            lane = lax.broadcasted_iota(jnp.int32, (1, LANES), 1)
            dbt = dbtab_scr[...]
            out = jnp.zeros((SGU_CHUNK, LANES), F32)
            for g in range(N_SGU_GROUPS):
                out = out + jnp.where(lane == g, jnp.sum(dbt * gm[g], axis=1, keepdims=True), 0.0)
            dbs_ref[...] = out
            for r0 in range(0, D_MODEL, SGU_CHUNK):
                k, row = divmod(r0, D_MODEL // N_CHIPS)
                dwout_bf_ref[k, row:row + SGU_CHUNK, :] = dwout_ref[r0:r0 + SGU_CHUNK, :].astype(BF16)

    tile = lambda w, cb: pl.BlockSpec((tm, w), lambda b, t, cb=cb: (b * nt + t, cb))
    const = lambda shape: pl.BlockSpec(shape, lambda b, t, n=len(shape): (0,) * n)
    return pl.pallas_call(
        body, grid=(B, nt),
        in_specs=[tile(D_MODEL, 0), tile(D_MODEL, 0), tile(ATTN_WIDTH, 0),
                  tile(ATTN_WIDTH, 3),
                  tile(SGU_WIDTH, 8), tile(SGU_WIDTH, 9), tile(SGU_WIDTH, 10),
                  tile(MEM_WIDTH, 11), tile(MEM_WIDTH, 12),
                  pl.BlockSpec((N_MEM, 2 * MEM_WIDTH), lambda b, t: (b, 0)),
                  const((N_SGU_GROUPS, SGU_CHUNK, SGU_CHUNK)), const((N_SGU_GROUPS, SGU_CHUNK, SGU_CHUNK)),
                  const((SGU_CHUNK, SGU_WIDTH)), const((1, SGU_WIDTH)),
                  const((D_MODEL, D_MODEL)), const((1, D_MODEL))],
        out_specs=[tile(D_MODEL, 0), tile(ATTN_WIDTH, 0), tile(REST_COLS, 0),
                   const((8, LANES)), const((N_CHIPS, D_MODEL // N_CHIPS, D_MODEL)),
                   const((N_SGU_GROUPS, SGU_CHUNK, SGU_CHUNK)), const((SGU_CHUNK, LANES)),
                   const((1, SGU_WIDTH)), const((1, D_MODEL)),
                   pl.BlockSpec((N_MEM, 2 * MEM_WIDTH), lambda b, t: (b, 0))],
        out_shape=[jax.ShapeDtypeStruct((T, D_MODEL), F32), jax.ShapeDtypeStruct((T, ATTN_WIDTH), F32),
                   jax.ShapeDtypeStruct((T, REST_COLS), BF16),
                   jax.ShapeDtypeStruct((8, LANES), F32),
                   jax.ShapeDtypeStruct((N_CHIPS, D_MODEL // N_CHIPS, D_MODEL), BF16),
                   jax.ShapeDtypeStruct((N_SGU_GROUPS, SGU_CHUNK, SGU_CHUNK), F32),
                   jax.ShapeDtypeStruct((SGU_CHUNK, LANES), F32),
                   jax.ShapeDtypeStruct((1, SGU_WIDTH), F32), jax.ShapeDtypeStruct((1, D_MODEL), F32),
                   jax.ShapeDtypeStruct((B * N_MEM, 2 * MEM_WIDTH), F32)],
        scratch_shapes=[pltpu.VMEM((SGU_CHUNK, SGU_WIDTH), F32), pltpu.VMEM((D_MODEL, D_MODEL), F32)],
        compiler_params=_params(("arbitrary", "arbitrary"), vmem=VMEM_LIMIT + 2 * 1024 * 1024), name="mid")(
            x2d, t2d, a, proj, proj, proj, proj, proj, proj, kv, w_s, w_sT, b_tab, g_v, w_out, g_final)


def _inproj_bwd_dx(dq, dk, dv, drest, x2d, dx2, g_norm, w_in_t, after=()):
    T = x2d.shape[0]
    tm = 512
    W = ATTN_WIDTH

    def body(dq_ref, dk_ref, dv_ref, dr_ref, x_ref, dx2_ref, g_ref, w_ref, *rest):
        gx_ref, dg_ref = rest[-2:]

        @pl.when(pl.program_id(0) == 0)
        def _():
            dg_ref[...] = jnp.zeros_like(dg_ref)

        halves = [pl.ds(h * (tm // 2), tm // 2) for h in (0, 1)]
        dh = [(_nn(dq_ref[r, :], w_ref[0:W, :]) + _nn(dk_ref[r, :], w_ref[W:2 * W, :])
               + _nn(dv_ref[r, :], w_ref[2 * W:3 * W, :]) + _nn(dr_ref[r, :], w_ref[QKV_COLS:IN_COLS, :]))
              for r in halves]
        nrm = [_rms(x_ref[r, :]) for r in halves]
        dg_ref[...] += sum(jnp.sum(d * n[1], axis=0, keepdims=True) for d, n in zip(dh, nrm))
        g = g_ref[...]
        for r, d, (rstd, xh) in zip(halves, dh, nrm):
            th = d * g
            gx_ref[r, :] = rstd * (th - xh * jnp.mean(th * xh, axis=1, keepdims=True)) + dx2_ref[r, :]

    tile = lambda w: pl.BlockSpec((tm, w), lambda i: (i, 0))
    return pl.pallas_call(
        body, grid=(T // tm,),
        in_specs=[tile(W), tile(W), tile(W), tile(REST_COLS), tile(D_MODEL), tile(D_MODEL),
                  pl.BlockSpec((1, D_MODEL), lambda i: (0, 0)),
                  pl.BlockSpec((IN_COLS, D_MODEL), lambda i: (0, 0))] + _after(after),
        out_specs=[tile(D_MODEL), pl.BlockSpec((1, D_MODEL), lambda i: (0, 0))],
        out_shape=[jax.ShapeDtypeStruct((T, D_MODEL), F32), jax.ShapeDtypeStruct((1, D_MODEL), F32)],
        compiler_params=_params(("arbitrary",)), name="inproj_bwd_dx")(
            dq, dk, dv, drest, x2d, dx2, g_norm, w_in_t, *after)


def _inproj_bwd_dw(dq, dk, dv, drest, x2d, g_norm, reduce=False):
    T = x2d.shape[0]
    tm = 512
    nt = T // tm
    W = ATTN_WIDTH
    shard = IN_COLS // N_CHIPS
    half = shard // 2
    row_block = 32

    def body(dq_ref, dk_ref, dv_ref, dr_ref, x_ref, g_ref, *rest):
        if reduce:
            sends_out, own_out, acc, ras, narrow, sends, own, s_sem, r_sem, put_sem = rest
            x, y, c, chip, peers, peer_chip = _place()
            sib = (x, y, 1 - c)

            def part(k, cc, r0):
                return acc.at[pl.ds(pl.multiple_of(k * shard + cc * half + r0, 8), row_block), :]

            def swap_win(k):
                return _remote(narrow.at[k], ras.at[k], s_sem.at[k], r_sem.at[k], sib)

            def put_send(m):
                return pltpu.make_async_copy(sends.at[m], sends_out.at[m], put_sem.at[m])

            put_own = pltpu.make_async_copy(own, own_out, put_sem.at[N_CHIPS - 1])
        else:
            acc = rest[0]

        @pl.when(pl.program_id(0) == 0)
        def _():
            acc[...] = jnp.zeros_like(acc)

        _, xh = _rms(x_ref[...])
        h = (xh * g_ref[...]).astype(BF16)
        acc[0:W, :] += _tn(dq_ref[...], h)
        acc[W:2 * W, :] += _tn(dk_ref[...], h)
        acc[2 * W:3 * W, :] += _tn(dv_ref[...], h)
        acc[QKV_COLS:IN_COLS, :] += _tn(dr_ref[...], h)

        if reduce:
            @pl.when(pl.program_id(0) == nt - 1)
            def _():
                for k in range(N_CHIPS):
                    def to_bf16(i, carry, k=k):
                        r0 = pl.multiple_of(i * row_block, row_block)
                        narrow[k, pl.ds(r0, row_block), :] = part(k, 1 - c, r0)[...].astype(BF16)
                        return carry
                    lax.fori_loop(0, half // row_block, to_bf16, 0)
                    swap_win(k).start()

                def chip_sum(k, r0):
                    return part(k, c, r0)[...] + ras[k, pl.ds(r0, row_block), :].astype(F32)

                for k in range(N_CHIPS):
                    swap_win(k).wait_recv()

                    @pl.when(chip == k)
                    def _(k=k):
                        def mine(i, carry):
                            r0 = pl.multiple_of(i * row_block, row_block)
                            own[pl.ds(r0, row_block), :] = chip_sum(k, r0)
                            return carry
                        lax.fori_loop(0, half // row_block, mine, 0)
                        put_own.start()

                    @pl.when(chip != k)
                    def _(k=k):
                        def other(i, carry):
                            r0 = pl.multiple_of(i * row_block, row_block)
                            sends[(k ^ chip) - 1, pl.ds(r0, row_block), :] = chip_sum(k, r0).astype(BF16)
                            return carry
                        lax.fori_loop(0, half // row_block, other, 0)
                        put_send((k ^ chip) - 1).start()
                for k in range(N_CHIPS):
                    swap_win(k).wait_send()
                for m in range(N_CHIPS - 1):
                    put_send(m).wait()
                put_own.wait()

    tile = lambda w: pl.BlockSpec((tm, w), lambda i: (i, 0))
    vmem = pl.BlockSpec(memory_space=pltpu.VMEM)
    in_specs = [tile(W), tile(W), tile(W), tile(REST_COLS), tile(D_MODEL), pl.BlockSpec((1, D_MODEL), lambda i: (0, 0))]
    if not reduce:
        return pl.pallas_call(
            body, grid=(nt,), in_specs=in_specs,
            out_specs=pl.BlockSpec((IN_COLS, D_MODEL), lambda i: (0, 0)),
            out_shape=jax.ShapeDtypeStruct((IN_COLS, D_MODEL), F32),
            compiler_params=_params(("arbitrary",)), name="inproj_bwd_dw")(dq, dk, dv, drest, x2d, g_norm)
    quarters = pltpu.VMEM((N_CHIPS, half, D_MODEL), BF16)
    far = pl.BlockSpec(memory_space=pl.ANY)
    return pl.pallas_call(
        body, grid=(nt,), in_specs=in_specs, out_specs=[far] * 2,
        out_shape=[jax.ShapeDtypeStruct((N_CHIPS - 1, half, D_MODEL), BF16),
                   jax.ShapeDtypeStruct((half, D_MODEL), F32)],
        scratch_shapes=[pltpu.VMEM((IN_COLS, D_MODEL), F32), quarters, quarters,
                        pltpu.VMEM((N_CHIPS - 1, half, D_MODEL), BF16), pltpu.VMEM((half, D_MODEL), F32),
                        pltpu.SemaphoreType.DMA((N_CHIPS,)), pltpu.SemaphoreType.DMA((N_CHIPS,)),
                        pltpu.SemaphoreType.DMA((N_CHIPS,))],
        compiler_params=_params(("arbitrary",)), name="inproj_bwd_dw_reduce")(dq, dk, dv, drest, x2d, g_norm)


def _adamw_update(w, g, m, v):
    nm = ADAM_B1 * m + (1.0 - ADAM_B1) * g
    nv = ADAM_B2 * v + (1.0 - ADAM_B2) * (g * g)
    m_hat = nm / (1.0 - ADAM_B1 ** ADAM_STEP)
    v_hat = nv / (1.0 - ADAM_B2 ** ADAM_STEP)
    return -ADAM_LR * (m_hat / (jnp.sqrt(v_hat) + ADAM_EPS) + ADAM_WD * w), nm, nv


def _adamw_all(g_packed, ws, ms, vs, large):
    n, nl = len(ws), len(large)
    row_block = 8
    chunk_bytes = 128 * 1024

    def chunk_rows(w):
        R, C = w.shape
        return max(r for r in range(8, R + 1, 8) if R % r == 0 and r * C * 4 <= chunk_bytes)

    jobs = [(b, r0, chunk_rows(four[0])) for b, four in enumerate(large)
            for r0 in range(0, four[0].shape[0], chunk_rows(four[0]))]

    def body(*refs):
        g_ref = refs[0]
        w_refs, m_refs, v_refs = refs[1:1 + n], refs[1 + n:1 + 2 * n], refs[1 + 2 * n:1 + 3 * n]
        far_in = refs[1 + 3 * n:1 + 3 * n + 4 * nl]
        outs = refs[1 + 3 * n + 4 * nl:1 + 7 * n + 4 * nl]
        far_out = refs[1 + 7 * n + 4 * nl:1 + 7 * n + 8 * nl]
        loss_ref = refs[1 + 7 * n + 8 * nl]
        scr = refs[2 + 7 * n + 8 * nl:]
        in_scr, out_scr, in_sem, out_sem = scr[:4 * nl], scr[4 * nl:7 * nl], scr[7 * nl], scr[7 * nl + 1]

        def read(j, k):
            b, r0, rows = jobs[j]
            blk = pl.ds(r0, rows)
            return pltpu.make_async_copy(far_in[4 * b + k].at[blk, :], in_scr[4 * b + k].at[blk, :],
                                         in_sem.at[4 * j + k])

        def write(j, k):
            b, r0, rows = jobs[j]
            blk = pl.ds(r0, rows)
            src = in_scr[4 * b + 1] if k == 0 else out_scr[3 * b + k - 1]
            return pltpu.make_async_copy(src.at[blk, :], far_out[4 * b + k].at[blk, :], out_sem.at[4 * j + k])

        for j in range(len(jobs)):
            for k in range(4):
                read(j, k).start()
        off = 0
        for i, (_, used, padded) in enumerate(_SMALL_PARTS[:n]):
            g = g_ref[off:off + used, :]
            delta, nm, nv = _adamw_update(w_refs[i][...], g, m_refs[i][...], v_refs[i][...])
            outs[4 * i][...], outs[4 * i + 1][...], outs[4 * i + 2][...], outs[4 * i + 3][...] = g, delta, nm, nv
            off += padded
        loss_ref[...] = g_ref[_LOSS_ROW:_LOSS_ROW + 1, 0:1]
        for j, (b, r0, rows) in enumerate(jobs):
            for k in range(4):
                read(j, k).wait()

            def update(i, carry, b=b, r0=r0):
                blk = pl.ds(pl.multiple_of(r0 + i * row_block, row_block), row_block)
                w, g, m, v = [in_scr[4 * b + k][blk, :] for k in range(4)]
                for k, val in enumerate(_adamw_update(w, g, m, v)):
                    out_scr[3 * b + k][blk, :] = val
                return carry
            lax.fori_loop(0, rows // row_block, update, 0)
            for k in range(4):
                write(j, k).start()
        for j in range(len(jobs)):
            for k in range(4):
                write(j, k).wait()

    vmem = pl.BlockSpec(memory_space=pltpu.VMEM)
    far = pl.BlockSpec(memory_space=pl.ANY)
    flat = [a for four in large for a in four]
    outs = pl.pallas_call(
        body, in_specs=[vmem] * (1 + 3 * n) + [far] * (4 * nl),
        out_specs=[vmem] * (4 * n) + [far] * (4 * nl) + [vmem],
        out_shape=[jax.ShapeDtypeStruct(w.shape, F32) for w in ws for _ in range(4)]
        + [jax.ShapeDtypeStruct(four[0].shape, F32) for four in large for _ in range(4)]
        + [jax.ShapeDtypeStruct((1, 1), F32)],
        scratch_shapes=[pltpu.VMEM(a.shape, F32) for a in flat]
        + [pltpu.VMEM(four[0].shape, F32) for four in large for _ in range(3)]
        + [pltpu.SemaphoreType.DMA((4 * len(jobs),)), pltpu.SemaphoreType.DMA((4 * len(jobs),))],
        compiler_params=_params(), name="adamw_all")(g_packed, *ws, *ms, *vs, *flat)
    return ([outs[4 * i:4 * i + 4] for i in range(n)], [outs[4 * (n + i):4 * (n + i) + 4] for i in range(nl)],
            outs[4 * (n + nl)])


def _place():
    x, y, c = lax.axis_index("x"), lax.axis_index("y"), lax.axis_index("c")
    chip = 2 * x + y
    peers = [(x, 1 - y), (1 - x, y), (1 - x, 1 - y)]
    peer_chip = [2 * px + py for px, py in peers]
    return x, y, c, chip, peers, peer_chip


def _remote(src, dst, send_sem, recv_sem, dev):
    return pltpu.make_async_remote_copy(src_ref=src, dst_ref=dst, send_sem=send_sem, recv_sem=recv_sem,
                                        device_id=dev, device_id_type=MESH)


def _ag_weights(weights, late=()):
    nw, nl = len(weights), len(late)
    pieces = 2

    def body(*refs):
        srcs, late_srcs = refs[:nw], refs[nw:nw + nl]
        outs, late_bf, late_land = (refs[nw + nl:2 * nw + nl], refs[2 * nw + nl:2 * nw + 2 * nl],
                                    refs[2 * nw + 2 * nl:2 * nw + 3 * nl])
        scr = refs[2 * nw + 3 * nl:]
        wide, narrow = scr[:nw], scr[nw:2 * nw]
        late_scr, late_wide = scr[2 * nw:2 * nw + nl], scr[2 * nw + nl:2 * nw + 2 * nl]
        in_sem, put_sem, late_sem, get_sem, s_ici, r_ici, s_d2d, r_d2d = scr[2 * nw + 2 * nl:]
        x, y, c = lax.axis_index("x"), lax.axis_index("y"), lax.axis_index("c")
        chip = 2 * x + y
        sib = (x, y, 1 - c)
        first = ((x + 1 - c) % 2, (y + c) % 2)
        second = ((x + c) % 2, (y + 1 - c) % 2)
        first_chip, second_chip = 2 * first[0] + first[1], 2 * second[0] + second[1]
        diag_chip = 3 - chip

        parts = [(w, out, pc) for w, out in enumerate(outs) for pc in range(pieces)]

        def rows_of(out, cc, pc):
            rows = out.shape[1] // 2 // pieces
            return pl.ds(pl.multiple_of((cc * pieces + pc) * rows, 16), rows)

        def piece(out, k, cc, pc):
            return out.at[k, rows_of(out, cc, pc), :]

        def ici(w, slot, out, k, dev, pc, src=None):
            blk, sem = piece(out, k, c, pc), (nw * slot + w) * pieces + pc
            return _remote(blk if src is None else src, blk, s_ici.at[sem], r_ici.at[sem], (dev[0], dev[1], c))

        def d2d(w, slot, out, k, cc, pc):
            blk, sem = piece(out, k, cc, pc), (nw * slot + w) * pieces + pc
            return _remote(blk, blk, s_d2d.at[sem], r_d2d.at[sem], sib)

        def read(w, cc, pc):
            rows = rows_of(outs[w], cc, pc)
            return pltpu.make_async_copy(srcs[w].at[rows, :], wide[w].at[rows, :],
                                         in_sem.at[(w * 2 + cc) * pieces + pc])

        order = [(w, cc, pc) for cc in (0, 1) for w in range(nw) for pc in range(pieces)]
        for w, cc, pc in order:
            read(w, (c + cc) % 2, pc).start()
        late_gets = [pltpu.make_async_copy(src, dst, get_sem.at[i])
                     for i, (src, dst) in enumerate(zip(late_srcs, late_wide))]
        for cp in late_gets:
            cp.start()
        sent = []
        for w, cc, pc in order[:nw * pieces]:
            rows = rows_of(outs[w], c, pc)
            read(w, c, pc).wait()
            narrow[w][rows, :] = wide[w][rows, :].astype(BF16)
            for slot, dev in enumerate((first, second)):
                sent.append(ici(w, slot, outs[w], chip, dev, pc, src=narrow[w].at[rows, :]))
                sent[-1].start()
        late_puts = []
        for i, (src, scr_bf, bf, land) in enumerate(zip(late_wide, late_scr, late_bf, late_land)):
            late_gets[i].wait()
            scr_bf[...] = src[...].astype(BF16)
            for k, dst in enumerate([bf] + [land.at[s] for s in range(N_CHIPS)]):
                late_puts.append(pltpu.make_async_copy(scr_bf, dst, late_sem.at[i * (N_CHIPS + 1) + k]))
                late_puts[-1].start()
        for w, cc, pc in order[nw * pieces:]:
            rows = rows_of(outs[w], 1 - c, pc)
            read(w, 1 - c, pc).wait()
            narrow[w][rows, :] = wide[w][rows, :].astype(BF16)
        puts = [pltpu.make_async_copy(narrow[w], outs[w].at[chip], put_sem.at[w]) for w in range(nw)]
        for cp in puts:
            cp.start()
        for slot, k, dev in ((0, first_chip, first), (1, second_chip, second), (2, diag_chip, second)):
            for w, out, pc in parts:
                ici(w, slot, out, k, dev, pc).wait_recv()
                if slot == 0:
                    sent.append(ici(w, 2, out, k, second, pc))
                    sent[-1].start()
                sent.append(d2d(w, slot, out, k, c, pc))
                sent[-1].start()
        for slot, k in ((0, second_chip), (1, first_chip), (2, diag_chip)):
            for w, out, pc in parts:
                d2d(w, slot, out, k, 1 - c, pc).wait_recv()
        for cp in sent:
            cp.wait_send()
        for cp in puts + late_puts:
            cp.wait()

    vmem = pl.BlockSpec(memory_space=pltpu.VMEM)
    far = pl.BlockSpec(memory_space=pl.ANY)
    outs = pl.pallas_call(
        body,
        out_shape=[jax.ShapeDtypeStruct((N_CHIPS,) + w.shape, BF16) for w in weights]
        + [jax.ShapeDtypeStruct(w.shape, BF16) for w in late]
        + [jax.ShapeDtypeStruct((N_CHIPS,) + w.shape, BF16) for w in late],
        in_specs=[far] * (nw + nl), out_specs=[far] * (nw + 2 * nl),
        scratch_shapes=[pltpu.VMEM(w.shape, F32) for w in weights] + [pltpu.VMEM(w.shape, BF16) for w in weights]
        + [pltpu.VMEM(w.shape, BF16) for w in late] + [pltpu.VMEM(w.shape, F32) for w in late]
        + [pltpu.SemaphoreType.DMA((2 * nw * pieces,)), pltpu.SemaphoreType.DMA((nw,)),
           pltpu.SemaphoreType.DMA((nl * (N_CHIPS + 1),)), pltpu.SemaphoreType.DMA((nl,))]
        + [pltpu.SemaphoreType.DMA((3 * nw * pieces,))] * 4,
        compiler_params=pltpu.CompilerParams(vmem_limit_bytes=VMEM_LIMIT), name="ag_weights")(*weights, *late)
    return outs[:nw], outs[nw:nw + nl], outs[nw + nl:]


_HBM = pl.BlockSpec(memory_space=pltpu.HBM)
_SEM = pl.BlockSpec(memory_space=pltpu.SEMAPHORE)
_ANY = pl.BlockSpec(memory_space=pl.ANY)
_DATAFLOW = pltpu.SideEffectType.DATAFLOW_SIDE_EFFECTING


def _in_hbm(a):
    return pltpu.with_memory_space_constraint(a, pltpu.HBM)


_PEERS_OF = {"gather": 3, "scatter": 3, "direct": 7}


def _exchange_copies(mode, srcs, lands, send_sems, recv_sems):
    nw = len(srcs)
    x, y, c, chip, peers, peer_chip = _place()
    pairs = []
    if mode == "direct":
        targets = [((x, y), chip, 1)] + [(p, k, d) for p, k in zip(peers, peer_chip) for d in (0, 1)]
        for r, ((px, py), k, d) in enumerate(targets):
            for w in range(nw):
                sems = (send_sems.at[nw * r + w], recv_sems.at[nw * r + w], (px, py, (c + d) % 2))
                share = srcs[w].at[k if srcs[w].shape[0] > 1 else 0]
                pairs.append((_remote(share, lands[w].at[r], *sems),) * 2)
        return pairs
    gather = mode == "gather"
    for m, (px, py) in enumerate(peers):
        for w in range(nw):
            sems = (send_sems.at[nw * m + w], recv_sems.at[nw * m + w], (px, py, c))
            if gather:
                pairs.append((_remote(srcs[w], lands[w].at[chip], *sems),
                              _remote(srcs[w], lands[w].at[peer_chip[m]], *sems)))
            else:
                pairs.append((_remote(srcs[w].at[m], lands[w].at[m], *sems),) * 2)
    return pairs


def _exchange_start(mode, srcs, after, name, lands=None):
    nw = len(srcs)
    n_copies = _PEERS_OF[mode] * nw

    after = tuple(after)

    def body(*refs):
        send_sems, recv_sems = refs[2 * nw + len(after)], refs[2 * nw + len(after) + 1]
        for start, _ in _exchange_copies(mode, refs[:nw], refs[nw:2 * nw], send_sems, recv_sems):
            start.start()
        refs[-1][...] = jnp.zeros_like(refs[-1])

    if lands is None:
        shape = {"gather": lambda s: (N_CHIPS,) + s.shape, "scatter": lambda s: s.shape,
                 "direct": lambda s: (_PEERS_OF["direct"],) + s.shape[1:]}[mode]
        lands = [lax.empty(shape(s), s.dtype) for s in srcs]
    lands = [_in_hbm(l) for l in lands]
    return pl.pallas_call(
        body, name=name,
        out_shape=(pltpu.SemaphoreType.DMA((n_copies,)), pltpu.SemaphoreType.DMA((n_copies,)))
        + tuple(pltpu.HBM(s.shape, s.dtype) for s in srcs)
        + tuple(pltpu.HBM(l.shape, l.dtype) for l in lands)
        + (jax.ShapeDtypeStruct((8, LANES), F32),),
        in_specs=[_HBM] * (2 * nw) + [_ANY] * len(after),
        out_specs=(_SEM, _SEM) + (_HBM,) * (2 * nw) + (pl.BlockSpec(memory_space=pltpu.VMEM),),
        input_output_aliases={i: 2 + i for i in range(2 * nw)},
        compiler_params=pltpu.CompilerParams(has_side_effects=_DATAFLOW),
    )(*[_in_hbm(s) for s in srcs], *lands, *after)


def _exchange_wait(mode, started, after, name):
    nw = (len(started) - 3) // 2
    send_sems, recv_sems = started[0], started[1]
    thru = started[2:2 + 2 * nw]

    def body(*refs):
        for _, arrival in _exchange_copies(mode, refs[:nw], refs[nw:2 * nw], refs[2 * nw], refs[2 * nw + 1]):
            arrival.wait_send()
            arrival.wait_recv()

    outs = pl.pallas_call(
        body, name=name,
        out_shape=tuple(pltpu.HBM(t.shape, t.dtype) for t in thru),
        in_specs=[_HBM] * (2 * nw) + [_SEM, _SEM, _ANY], out_specs=(_HBM,) * (2 * nw),
        input_output_aliases={i: i for i in range(2 * nw)},
        compiler_params=pltpu.CompilerParams(has_side_effects=_DATAFLOW),
    )(*thru, send_sems, recv_sems, after)
    return outs[:nw], outs[nw:]


def _reduce_last(owns, landed, g_small, direct_srcs, direct_landed, spread_row0):
    ns, nd = len(owns), len(direct_srcs)
    halves = [o.shape[0] for o in owns]
    row_block = 16
    spread_rows = direct_srcs[-1].shape[1]
    rest0, rest1 = spread_row0, SMALL_ROWS - spread_row0 - spread_rows
    hs = (rest0 + rest1) // 2
    jobs = [(w, p * (halves[w] // 2), halves[w] // 2) for w in range(ns) for p in range(2)]
    n_swaps = len(jobs)
    jobs += [(ns + d, 0, direct_srcs[d].shape[1]) for d in range(nd)]

    def body(*refs):
        own_refs, land_refs, gsm_ref = refs[:ns], refs[ns:2 * ns], refs[2 * ns]
        dsrc_refs, dland_refs = refs[2 * ns + 1:2 * ns + 1 + nd], refs[2 * ns + 1 + nd:2 * ns + 1 + 2 * nd]
        n_in = 2 * ns + 1 + 2 * nd
        out_refs, osm_ref = refs[n_in:n_in + ns + nd - 1], refs[n_in + ns + nd - 1]
        scr = refs[n_in + ns + nd:]
        own_scr, land_scr, dsrc_scr, dland_scr = (scr[:ns], scr[ns:2 * ns], scr[2 * ns:2 * ns + nd],
                                                  scr[2 * ns + nd:2 * ns + 2 * nd])
        res = scr[2 * ns + 2 * nd:3 * ns + 3 * nd - 1]
        o_rest, ra_sm, p_sm, put_sem, in_sem, s_sem, r_sem, sm_s, sm_r = scr[3 * ns + 3 * nd - 1:]
        x, y, c, chip, peers, peer_chip = _place()
        sib = (x, y, 1 - c)
        half = lambda cc: pl.ds(pl.multiple_of(cc * hs, 8), hs)
        sm_a = _remote(gsm_ref.at[half(1 - c), :], ra_sm, sm_s.at[0], sm_r.at[0], sib)
        sm_a.start()
        swaps = [sm_a]

        def reads(j):
            w, r0, n = jobs[j]
            rows = pl.ds(r0, n)
            if w < ns:
                pairs = [(own_refs[w].at[rows, :], own_scr[w].at[rows, :]),
                         (land_refs[w].at[:, rows, :], land_scr[w].at[:, rows, :])]
            else:
                share = dsrc_refs[w - ns].at[chip if w < ns + nd - 1 else 0]
                pairs = [(share, dsrc_scr[w - ns]), (dland_refs[w - ns], dland_scr[w - ns])]
            return [pltpu.make_async_copy(s, d, in_sem.at[2 * j + i]) for i, (s, d) in enumerate(pairs)]

        for j in range(len(jobs)):
            for cp in reads(j):
                cp.start()
        sm_a.wait_recv()
        p_sm[chip] = gsm_ref[half(c), :] + ra_sm[...]
        for m, (px, py) in enumerate(peers):
            swaps.append(_remote(p_sm.at[chip], p_sm.at[chip], sm_s.at[1 + m], sm_r.at[1 + m], (px, py, c)))
            swaps[-1].start()

        def mine_of(j):
            w, r0, n = jobs[j]
            return out_refs[w].at[pl.ds(pl.multiple_of(c * halves[w] + r0, 8), n), :]

        def sum_of(j):
            w, r0, n = jobs[j]
            return res[w].at[pl.ds(r0, n), :]

        puts = []
        for j, (w, r0, n) in enumerate(jobs):
            for cp in reads(j):
                cp.wait()

            def total(i, carry, w=w, r0=r0):
                rr = pl.multiple_of(r0 + i * row_block, row_block)
                blk = pl.ds(rr, row_block)
                if w < ns:
                    acc = own_scr[w][blk, :]
                    for m in range(_PEERS_OF["scatter"]):
                        acc = acc + land_scr[w][m, blk, :].astype(F32)
                    res[w][blk, :] = acc
                else:
                    theirs = lambda r: dland_scr[w - ns][r, blk, :].astype(F32)
                    acc = ((dsrc_scr[w - ns][blk, :].astype(F32) + theirs(0)) + (theirs(1) + theirs(2))) + (
                        (theirs(3) + theirs(4)) + (theirs(5) + theirs(6)))
                    if w < ns + nd - 1:
                        res[w][blk, :] = acc
                    else:
                        osm_ref[pl.ds(pl.multiple_of(spread_row0 + rr, 8), row_block), :] = acc
                return carry
            lax.fori_loop(0, n // row_block, total, 0)
            if w < ns:
                puts.append(pltpu.make_async_copy(sum_of(j), mine_of(j), put_sem.at[j]))
                swaps.append(_remote(sum_of(j), mine_of(j), s_sem.at[j], r_sem.at[j], sib))
                swaps[-1].start()
            elif w < ns + nd - 1:
                puts.append(pltpu.make_async_copy(res[w], out_refs[w], put_sem.at[j]))
            if w < ns + nd - 1:
                puts[-1].start()
        for m, (px, py) in enumerate(peers):
            _remote(p_sm.at[chip], p_sm.at[peer_chip[m]], sm_s.at[1 + m], sm_r.at[1 + m], (px, py, c)).wait_recv()
        o_rest[half(c), :] = (p_sm[0] + p_sm[1]) + (p_sm[2] + p_sm[3])
        swaps.append(_remote(o_rest.at[half(c), :], o_rest.at[half(c), :], sm_s.at[4], sm_r.at[4], sib))
        swaps[-1].start()
        for j, (w, r0, n) in enumerate(jobs[:n_swaps]):
            theirs = out_refs[w].at[pl.ds(pl.multiple_of((1 - c) * halves[w] + r0, 8), n), :]
            _remote(sum_of(j), theirs, s_sem.at[j], r_sem.at[j], sib).wait_recv()
        _remote(o_rest.at[half(1 - c), :], o_rest.at[half(1 - c), :], sm_s.at[4], sm_r.at[4], sib).wait_recv()
        osm_ref[0:rest0, :] = o_rest[0:rest0, :]
        osm_ref[SMALL_ROWS - rest1:SMALL_ROWS, :] = o_rest[rest0:rest0 + rest1, :]
        for cp in swaps:
            cp.wait_send()
        for cp in puts:
            cp.wait()

    vmem = pl.BlockSpec(memory_space=pltpu.VMEM)
    far = [pl.BlockSpec(memory_space=pl.ANY)]
    return pl.pallas_call(
        body, out_shape=[jax.ShapeDtypeStruct((2 * o.shape[0], o.shape[1]), F32) for o in owns]
        + [jax.ShapeDtypeStruct(s.shape[1:], F32) for s in direct_srcs[:-1]]
        + [jax.ShapeDtypeStruct((SMALL_ROWS, LANES), F32)],
        in_specs=far * (2 * ns) + [vmem] + far * (2 * nd), out_specs=far * (ns + nd - 1) + [vmem],
        scratch_shapes=[pltpu.VMEM(o.shape, o.dtype) for o in owns] + [pltpu.VMEM(l.shape, l.dtype) for l in landed]
        + [pltpu.VMEM(s.shape[1:], s.dtype) for s in direct_srcs]
        + [pltpu.VMEM(l.shape, l.dtype) for l in direct_landed]
        + [pltpu.VMEM(o.shape, F32) for o in owns] + [pltpu.VMEM(s.shape[1:], F32) for s in direct_srcs[:-1]]
        + [pltpu.VMEM((2 * hs, LANES), F32), pltpu.VMEM((hs, LANES), F32), pltpu.VMEM((N_CHIPS, hs, LANES), F32),
           pltpu.SemaphoreType.DMA((len(jobs),)), pltpu.SemaphoreType.DMA((2 * len(jobs),)),
           pltpu.SemaphoreType.DMA((n_swaps,)), pltpu.SemaphoreType.DMA((n_swaps,)),
           pltpu.SemaphoreType.DMA((5,)), pltpu.SemaphoreType.DMA((5,))],
        compiler_params=pltpu.CompilerParams(vmem_limit_bytes=VMEM_LIMIT),
        name="reduce_last")(*owns, *landed, g_small, *direct_srcs, *direct_landed)


_SMALL_PARTS = (("g_norm", 8, 8), ("w_s", 512, 512), ("b_s", 4, 8), ("g_v", 2, 8), ("g_mem", 8, 8),
                ("g_final", 8, 8), ("loss", 8, 8))
_LOSS_ROW = SMALL_ROWS - 8
_W_S_ROW = 8
assert sum(p for _, _, p in _SMALL_PARTS) == SMALL_ROWS and _SMALL_PARTS[1][0] == "w_s"


def _pack_small(parts, loss_block):
    rows = []
    parts = dict(parts, loss=loss_block)
    for name, used, padded in _SMALL_PARTS:
        if name == "w_s":
            continue
        p = parts[name].reshape(used, LANES)
        if padded > used:
            p = jnp.pad(p, ((0, padded - used), (0, 0)))
        rows.append(p)
    return jnp.concatenate(rows, axis=0)


def _local_step(x, mem, target, g_norm, w_in, w_s, b_s, g_v, g_mem, late_weights, g_final,
                fwd_token=None, on_late=None, on_dw=None):
    B, S, _ = x.shape
    x2d = x.reshape(B * S, D_MODEL)
    t2d = target.reshape(B * S, D_MODEL)
    mem2d = mem.reshape(B * N_MEM, D_MODEL)

    proj = _inproj_fwd(x2d, g_norm, w_in, after=() if fwd_token is None else (fwd_token,))
    w_kv, w_out = late_weights(proj)
    kv = _kv_fwd(mem2d, g_mem, w_kv)
    a, lse = _attn_fwd(proj, B, S)
    w_sT = jnp.swapaxes(w_s, 1, 2)
    b_tab = jnp.repeat(b_s.T, HEAD_DIM, axis=1)
    (dx2, da, drest, loss, d_wout, d_ws, d_bs, d_gv, d_gf, dkv) = _mid(
        x2d, t2d, a, proj, kv, w_s, w_sT, b_tab, g_v, w_out, g_final, B, S)
    d_wkv, d_gmem = _kv_bwd(mem2d, g_mem, w_kv, dkv)
    dq, dk, dv = _attn_bwd(proj, a, lse, da, B, S, after=() if on_late is None else (on_late(d_wkv, d_wout, d_ws),))
    if on_dw is None:
        d_win = _inproj_bwd_dw(dq, dk, dv, drest, x2d, g_norm)
        after = ()
    else:
        d_win = None
        after = (on_dw(*_inproj_bwd_dw(dq, dk, dv, drest, x2d, g_norm, reduce=True)),)
    grad_x, d_gnorm = _inproj_bwd_dx(dq, dk, dv, drest, x2d, dx2, g_norm, w_in, after=after)
    d_bs = d_bs[:, :N_SGU_GROUPS].T
    return (loss, grad_x.reshape(B, S, D_MODEL),
            dict(g_norm=d_gnorm, w_in=d_win, w_s=d_ws, b_s=d_bs, g_v=d_gv, g_mem=d_gmem, w_kv=d_wkv,
                 w_out=d_wout, g_final=d_gf))


def kernel(x, mem, g_norm, w_in, w_sgu_spatial, b_sgu_spatial, g_sgu_v, g_mem, w_mem_kv, w_out, g_final, loss_target, m_g_norm, m_w_in, m_w_sgu_spatial, m_b_sgu_spatial, m_g_sgu_v, m_g_mem, m_w_mem_kv, m_w_out, m_g_final, v_g_norm, v_w_in, v_w_sgu_spatial, v_b_sgu_spatial, v_g_sgu_v, v_g_mem, v_w_mem_kv, v_w_out, v_g_final):
    t = lambda w: jnp.swapaxes(w[0], 0, 1)
    (win_all,), late_shards, late_lands = _ag_weights([t(w_in)], [w_mem_kv[0], w_out[0]])
    w_in_full = win_all.reshape(-1, win_all.shape[-1])
    late = _exchange_start("gather", list(late_shards), (win_all,), "gather_late_start", lands=late_lands)

    def late_weights(proj):
        return [z.reshape(-1, z.shape[-1]) for z in _exchange_wait("gather", late, proj, "gather_late_wait")[1]]

    scatter = {}

    def on_late(d_wkv, d_wout, d_ws):
        d_ws = d_ws.reshape(1, -1, LANES)
        scatter["late"] = _exchange_start("direct", [d_wkv, d_wout, d_ws], (), "scatter_late_start")
        return scatter["late"][-1]

    def on_dw(sends, own):
        scatter["own"] = own
        scatter["started"] = _exchange_start("scatter", [sends], (own,), "scatter_start")
        return scatter["started"][-1]

    loss, grad_x, g = _local_step(
        x, mem, loss_target, g_norm, w_in_full, w_sgu_spatial[0], b_sgu_spatial[0], g_sgu_v, g_mem,
        late_weights, g_final.reshape(1, D_MODEL), fwd_token=late[-1], on_late=on_late, on_dw=on_dw)

    small_names = ("g_norm", "w_s", "b_s", "g_v", "g_mem", "g_final")
    g_small = _pack_small({n: g[n] for n in small_names if n != "w_s"}, loss)
    late_srcs, late_landed = _exchange_wait("direct", scatter["late"], g_small, "scatter_late_wait")
    _, landed = _exchange_wait("scatter", scatter["started"], late_landed[0], "scatter_wait")
    gr_in, gr_kv, gr_out, gr_small = _reduce_last([scatter["own"]], landed, g_small, late_srcs, late_landed, _W_S_ROW)

    small_w = (g_norm, w_sgu_spatial, b_sgu_spatial, g_sgu_v, g_mem, g_final)
    small_m = (m_g_norm, m_w_sgu_spatial, m_b_sgu_spatial, m_g_sgu_v, m_g_mem, m_g_final)
    small_v = (v_g_norm, v_w_sgu_spatial, v_b_sgu_spatial, v_g_sgu_v, v_g_mem, v_g_final)
    rows = lambda ws: [w.reshape(-1, LANES) for w in ws]
    small_new, (of_in, (gr_kv, d_kv, nm_kv, nv_kv), (gr_out, d_out, nm_out, nv_out)), loss = _adamw_all(
        gr_small, rows(small_w), rows(small_m), rows(small_v),
        [(t(w_in), gr_in, t(m_w_in), t(v_w_in)), (w_mem_kv[0], gr_kv, m_w_mem_kv[0], v_w_mem_kv[0]),
         (w_out[0], gr_out, m_w_out[0], v_w_out[0])])
    loss = loss.reshape(())
    small = [[z.reshape(w.shape) for z in four] for w, four in zip(small_w, small_new)]
    gr_in, d_in, nm_in, nv_in = [jnp.swapaxes(z, 0, 1) for z in of_in]

    def leaves(kind, big_in, big_kv, big_out):
        s_norm, s_ws, s_bs, s_gv, s_gmem, s_gf = [four[kind] for four in small]
        return [s_norm, big_in[None], s_ws, s_bs, s_gv, s_gmem, big_kv[None], big_out[None], s_gf]

    return (loss, grad_x, *leaves(0, gr_in, gr_kv, gr_out), *leaves(1, d_in, d_kv, d_out),
            *leaves(2, nm_in, nm_kv, nm_out), *leaves(3, nv_in, nv_kv, nv_out))
```
